```python
import jax, jax.numpy as jnp
from jax import lax
import numpy as np

D_MODEL = 1024
BATCH = 16
SEQ = 2048
DEPTH = 1

HEAD_DIM = 64
A_Q_HEADS = 8
A_KV_HEADS = 2
A_GROUP = A_Q_HEADS // A_KV_HEADS
A_WINDOW = 128
B_HEADS = 8
B_BRANCHES = ((128, 1), (512, 4), (2048, 16))
ROT_DIM = HEAD_DIM // 4
ROPE_THETA = 500000.0
D_FF = 4 * D_MODEL
A_Q_W = A_Q_HEADS * HEAD_DIM
A_KV_W = A_KV_HEADS * HEAD_DIM
B_W = B_HEADS * HEAD_DIM
MIX_W = A_Q_W + B_W
IN_W = A_Q_W + 2 * A_KV_W + 3 * B_W
N_MOD = 6
BLOCK = 128
EPS = 1e-6
NEG_INF = -1e30

kernel_name = 'hybrid_swa_sink_dilated_sqrelu_block'


def rms_norm(x, g):
    xf = x.astype(jnp.float32)
    y = xf * lax.rsqrt(jnp.mean(xf * xf, axis=-1, keepdims=True) + EPS)
    return (y * g.astype(jnp.float32)).astype(x.dtype)


def partial_rope(x, cos, sin):
    half = ROT_DIM // 2
    x1 = x[..., :half].astype(jnp.float32)
    x2 = x[..., half:ROT_DIM].astype(jnp.float32)
    rot = jnp.concatenate([x1 * cos - x2 * sin, x2 * cos + x1 * sin], axis=-1).astype(x.dtype)
    return jnp.concatenate([rot, x[..., ROT_DIM:]], axis=-1)


def banded_attention(q, k, v, max_dist, sink=None, return_lse=False):
    n, seq_len, hk, g, dh = q.shape
    blk = min(BLOCK, seq_len)
    nb = -(-seq_len // blk)
    lp = nb * blk
    pad = lp - seq_len
    if pad:
        q = jnp.pad(q, ((0, 0), (0, pad), (0, 0), (0, 0), (0, 0)))
        k = jnp.pad(k, ((0, 0), (0, pad), (0, 0), (0, 0)))
        v = jnp.pad(v, ((0, 0), (0, pad), (0, 0), (0, 0)))
    qb = q.reshape(n, nb, blk, hk, g, dh)
    kb = k.reshape(n, nb, blk, hk, dh)
    vb = v.reshape(n, nb, blk, hk, dh)
    kcat = jnp.concatenate([jnp.pad(kb, ((0, 0), (1, 0), (0, 0), (0, 0), (0, 0)))[:, :nb], kb], axis=2)
    vcat = jnp.concatenate([jnp.pad(vb, ((0, 0), (1, 0), (0, 0), (0, 0), (0, 0)))[:, :nb], vb], axis=2)
    s = jnp.einsum('ncqhgd,nckhd->nhgcqk', qb, kcat,
                   preferred_element_type=jnp.float32) * (1.0 / float(np.sqrt(dh)))
    blocks = jnp.arange(nb)[:, None, None]
    qpos = blocks * blk + jnp.arange(blk)[None, :, None]
    kpos = (blocks - 1) * blk + jnp.arange(2 * blk)[None, None, :]
    dist = qpos - kpos
    valid = (dist >= 0) & (dist <= max_dist) & (kpos >= 0)
    s = jnp.where(valid, s, NEG_INF)
    m = jnp.max(s, axis=-1, keepdims=True)
    if sink is not None:
        sk = sink.astype(jnp.float32).reshape(hk, g)[None, :, :, None, None, None]
        m = jnp.maximum(m, sk)
        p = jnp.exp(s - m)
        denom = jnp.sum(p, axis=-1, keepdims=True) + jnp.exp(sk - m)
    else:
        p = jnp.exp(s - m)
        denom = jnp.sum(p, axis=-1, keepdims=True)
    o = jnp.einsum('nhgcqk,nckhd->ncqhgd', p / denom, vcat.astype(jnp.float32))
    o = o.reshape(n, lp, hk, g, dh)[:, :seq_len]
    if return_lse:
        lse = (m + jnp.log(denom))[..., 0]
        lse = lse.transpose(0, 3, 4, 1, 2).reshape(n, lp, hk, g)[:, :seq_len]
        return o, lse
    return o


def dilated_mixture_attention(q, k, v):
    b, s, h, dh = q.shape
    outs, lses = [], []
    for window, d in B_BRANCHES:
        sub = s // d

        def to_sub(t):
            return t.reshape(b, sub, d, h, dh).transpose(0, 2, 1, 3, 4).reshape(b * d, sub, h, dh)

        o, lse = banded_attention(to_sub(q)[:, :, :, None, :], to_sub(k), to_sub(v),
                                  window // d, return_lse=True)
        outs.append(o[:, :, :, 0].reshape(b, d, sub, h, dh).transpose(0, 2, 1, 3, 4).reshape(b, s, h, dh))
        lses.append(lse[..., 0].reshape(b, d, sub, h).transpose(0, 2, 1, 3).reshape(b, s, h))
    w = jax.nn.softmax(jnp.stack(lses, axis=0), axis=0)
    return jnp.sum(w[..., None] * jnp.stack(outs, axis=0), axis=0)


def _fwd_setup_inputs(seed: int = 0) -> dict:
    key = jax.random.key(seed)
    ks = jax.random.split(key, 17)
    f32 = jnp.float32

    def gain(k, n):
        return 1.0 + 0.05 * jax.random.normal(k, (DEPTH, n), f32)

    x = jax.random.normal(ks[0], (BATCH, SEQ, D_MODEL), f32)
    c = jax.random.normal(ks[1], (BATCH, D_MODEL), f32)
    offsets = jax.random.randint(ks[2], (BATCH, 1), 0, 1024, dtype=jnp.int32)
    positions = offsets + jnp.arange(SEQ, dtype=jnp.int32)[None, :]
    w_ada = jax.random.normal(ks[3], (DEPTH, D_MODEL, N_MOD * D_MODEL), f32) * D_MODEL ** -0.5
    b_ada = 0.02 * jax.random.normal(ks[4], (DEPTH, N_MOD * D_MODEL), f32)
    g_attn_pre = gain(ks[5], D_MODEL)
    g_attn_post = gain(ks[6], D_MODEL)
    w_in = jax.random.normal(ks[7], (DEPTH, D_MODEL, IN_W), f32) * D_MODEL ** -0.5
    sink_a = jax.random.normal(ks[8], (DEPTH, A_Q_HEADS), f32)
    g_mix_a = gain(ks[9], A_Q_W)
    g_mix_b = gain(ks[10], B_W)
    w_out = jax.random.normal(ks[11], (DEPTH, MIX_W, D_MODEL), f32) * MIX_W ** -0.5
    g_mlp_pre = gain(ks[12], D_MODEL)
    g_mlp_post = gain(ks[13], D_MODEL)
    w_up = jax.random.normal(ks[14], (DEPTH, D_MODEL, D_FF), f32) * D_MODEL ** -0.5
    w_down = jax.random.normal(ks[15], (DEPTH, D_FF, D_MODEL), f32) * D_FF ** -0.5
    return {'x': x, 'c': c, 'positions': positions, 'w_ada': w_ada, 'b_ada': b_ada,
            'g_attn_pre': g_attn_pre, 'g_attn_post': g_attn_post, 'w_in': w_in,
            'sink_a': sink_a, 'g_mix_a': g_mix_a, 'g_mix_b': g_mix_b, 'w_out': w_out,
            'g_mlp_pre': g_mlp_pre, 'g_mlp_post': g_mlp_post, 'w_up': w_up, 'w_down': w_down}


def _fwd_reference(x, c, positions, w_ada, b_ada, g_attn_pre, g_attn_post, w_in, sink_a,
              g_mix_a, g_mix_b, w_out, g_mlp_pre, g_mlp_post, w_up, w_down):
    b, s, _ = x.shape
    inv_freq = ROPE_THETA ** (-jnp.arange(0, ROT_DIM, 2, dtype=jnp.float32) / ROT_DIM)
    ang = positions.astype(jnp.float32)[..., None] * inv_freq
    cos = jnp.cos(ang)[:, :, None, :]
    sin = jnp.sin(ang)[:, :, None, :]
    cond = jax.nn.silu(c)
    o1 = A_Q_W
    o2 = o1 + A_KV_W
    o3 = o2 + A_KV_W
    o4 = o3 + B_W
    o5 = o4 + B_W
    for l in range(DEPTH):
        mod = (cond @ w_ada[l] + b_ada[l]).astype(x.dtype)
        sh_a, sc_a, gt_a, sh_m, sc_m, gt_m = [m[:, None, :] for m in jnp.split(mod, N_MOD, axis=-1)]

        h = rms_norm(x, g_attn_pre[l]) * (1 + sc_a) + sh_a
        proj = h @ w_in[l]
        qa = partial_rope(proj[..., :o1].reshape(b, s, A_Q_HEADS, HEAD_DIM), cos, sin)
        ka = partial_rope(proj[..., o1:o2].reshape(b, s, A_KV_HEADS, HEAD_DIM), cos, sin)
        va = proj[..., o2:o3].reshape(b, s, A_KV_HEADS, HEAD_DIM)
        qb = partial_rope(proj[..., o3:o4].reshape(b, s, B_HEADS, HEAD_DIM), cos, sin)
        kb = partial_rope(proj[..., o4:o5].reshape(b, s, B_HEADS, HEAD_DIM), cos, sin)
        vb = proj[..., o5:].reshape(b, s, B_HEADS, HEAD_DIM)

        oa = banded_attention(qa.reshape(b, s, A_KV_HEADS, A_GROUP, HEAD_DIM), ka, va,
                              A_WINDOW - 1, sink=sink_a[l])
        oa = oa.reshape(b, s, A_Q_W).astype(x.dtype)
        ob = dilated_mixture_attention(qb, kb, vb).reshape(b, s, B_W).astype(x.dtype)

        mixed = jnp.concatenate([rms_norm(oa, g_mix_a[l]), rms_norm(ob, g_mix_b[l])], axis=-1)
        y = mixed @ w_out[l]
        x = x + gt_a * rms_norm(y, g_attn_post[l])

        h = rms_norm(x, g_mlp_pre[l]) * (1 + sc_m) + sh_m
        y = jnp.square(jax.nn.relu(h @ w_up[l])) @ w_down[l]
        x = x + gt_m * rms_norm(y, g_mlp_post[l])
    return x


import jax as _jax
import jax.numpy as _jnp

TWIN_FORMAT = 'train_step'
FWD_PARAMS = ['x', 'c', 'positions', 'w_ada', 'b_ada', 'g_attn_pre', 'g_attn_post', 'w_in', 'sink_a', 'g_mix_a', 'g_mix_b', 'w_out', 'g_mlp_pre', 'g_mlp_post', 'w_up', 'w_down']
TWIN_WEIGHTS = ['w_ada', 'b_ada', 'g_attn_pre', 'g_attn_post', 'w_in', 'sink_a', 'g_mix_a', 'g_mix_b', 'w_out', 'g_mlp_pre', 'g_mlp_post', 'w_up', 'w_down']
TWIN_DIFF_INPUT = 'x'
TWIN_INPUTS = ['x', 'c', 'positions', 'w_ada', 'b_ada', 'g_attn_pre', 'g_attn_post', 'w_in', 'sink_a', 'g_mix_a', 'g_mix_b', 'w_out', 'g_mlp_pre', 'g_mlp_post', 'w_up', 'w_down', 'loss_target', 'm_w_ada', 'm_b_ada', 'm_g_attn_pre', 'm_g_attn_post', 'm_w_in', 'm_sink_a', 'm_g_mix_a', 'm_g_mix_b', 'm_w_out', 'm_g_mlp_pre', 'm_g_mlp_post', 'm_w_up', 'm_w_down', 'v_w_ada', 'v_b_ada', 'v_g_attn_pre', 'v_g_attn_post', 'v_w_in', 'v_sink_a', 'v_g_mix_a', 'v_g_mix_b', 'v_w_out', 'v_g_mlp_pre', 'v_g_mlp_post', 'v_w_up', 'v_w_down']
TWIN_OUTPUTS = ['loss', 'grad_x', 'grad_w_ada', 'grad_b_ada', 'grad_g_attn_pre', 'grad_g_attn_post', 'grad_w_in', 'grad_sink_a', 'grad_g_mix_a', 'grad_g_mix_b', 'grad_w_out', 'grad_g_mlp_pre', 'grad_g_mlp_post', 'grad_w_up', 'grad_w_down', 'delta_w_ada', 'delta_b_ada', 'delta_g_attn_pre', 'delta_g_attn_post', 'delta_w_in', 'delta_sink_a', 'delta_g_mix_a', 'delta_g_mix_b', 'delta_w_out', 'delta_g_mlp_pre', 'delta_g_mlp_post', 'delta_w_up', 'delta_w_down', 'new_m_w_ada', 'new_m_b_ada', 'new_m_g_attn_pre', 'new_m_g_attn_post', 'new_m_w_in', 'new_m_sink_a', 'new_m_g_mix_a', 'new_m_g_mix_b', 'new_m_w_out', 'new_m_g_mlp_pre', 'new_m_g_mlp_post', 'new_m_w_up', 'new_m_w_down', 'new_v_w_ada', 'new_v_b_ada', 'new_v_g_attn_pre', 'new_v_g_attn_post', 'new_v_w_in', 'new_v_sink_a', 'new_v_g_mix_a', 'new_v_g_mix_b', 'new_v_w_out', 'new_v_g_mlp_pre', 'new_v_g_mlp_post', 'new_v_w_up', 'new_v_w_down']
TWIN_LEAF_KINDS = {'loss': 'loss', 'grad_x': 'grad_x', 'grad_w_ada': 'grad_w', 'grad_b_ada': 'grad_w', 'grad_g_attn_pre': 'grad_w', 'grad_g_attn_post': 'grad_w', 'grad_w_in': 'grad_w', 'grad_sink_a': 'grad_w', 'grad_g_mix_a': 'grad_w', 'grad_g_mix_b': 'grad_w', 'grad_w_out': 'grad_w', 'grad_g_mlp_pre': 'grad_w', 'grad_g_mlp_post': 'grad_w', 'grad_w_up': 'grad_w', 'grad_w_down': 'grad_w', 'delta_w_ada': 'delta_w', 'delta_b_ada': 'delta_w', 'delta_g_attn_pre': 'delta_w', 'delta_g_attn_post': 'delta_w', 'delta_w_in': 'delta_w', 'delta_sink_a': 'delta_w', 'delta_g_mix_a': 'delta_w', 'delta_g_mix_b': 'delta_w', 'delta_w_out': 'delta_w', 'delta_g_mlp_pre': 'delta_w', 'delta_g_mlp_post': 'delta_w', 'delta_w_up': 'delta_w', 'delta_w_down': 'delta_w', 'new_m_w_ada': 'new_m', 'new_m_b_ada': 'new_m', 'new_m_g_attn_pre': 'new_m', 'new_m_g_attn_post': 'new_m', 'new_m_w_in': 'new_m', 'new_m_sink_a': 'new_m', 'new_m_g_mix_a': 'new_m', 'new_m_g_mix_b': 'new_m', 'new_m_w_out': 'new_m', 'new_m_g_mlp_pre': 'new_m', 'new_m_g_mlp_post': 'new_m', 'new_m_w_up': 'new_m', 'new_m_w_down': 'new_m', 'new_v_w_ada': 'new_v', 'new_v_b_ada': 'new_v', 'new_v_g_attn_pre': 'new_v', 'new_v_g_attn_post': 'new_v', 'new_v_w_in': 'new_v', 'new_v_sink_a': 'new_v', 'new_v_g_mix_a': 'new_v', 'new_v_g_mix_b': 'new_v', 'new_v_w_out': 'new_v', 'new_v_g_mlp_pre': 'new_v', 'new_v_g_mlp_post': 'new_v', 'new_v_w_up': 'new_v', 'new_v_w_down': 'new_v'}


def _forward(args):
    return _fwd_reference(*[args[k] for k in FWD_PARAMS])


def _output_shape():
    out = _jax.eval_shape(lambda: _forward(_fwd_setup_inputs(0)))
    return out.shape, out.dtype

N_MICROBATCH = 1
ADAM_LR = 0.001
ADAM_B1 = 0.9
ADAM_B2 = 0.999
ADAM_EPS = 1e-08
ADAM_WD = 0.01
ADAM_STEP = 10
PER_EXAMPLE_BATCH_AXIS = {'x': 0, 'c': 0, 'positions': 0, 'loss_target': 0}
SHARED_INPUTS = []
_WEIGHT_DTYPES = {'w_ada': _jnp.float32, 'b_ada': _jnp.float32, 'g_attn_pre': _jnp.float32, 'g_attn_post': _jnp.float32, 'w_in': _jnp.float32, 'sink_a': _jnp.float32, 'g_mix_a': _jnp.float32, 'g_mix_b': _jnp.float32, 'w_out': _jnp.float32, 'g_mlp_pre': _jnp.float32, 'g_mlp_post': _jnp.float32, 'w_up': _jnp.float32, 'w_down': _jnp.float32}
MOMENT_SCALE = {'w_ada': 6.079752e+00, 'b_ada': 1.035728e+01, 'g_attn_pre': 7.684967e-01, 'g_attn_post': 1.799187e+01, 'w_in': 5.129851e+00, 'sink_a': 1.489077e-01, 'g_mix_a': 7.353972e+00, 'g_mix_b': 7.206846e+00, 'w_out': 7.530771e+00, 'g_mlp_pre': 2.679030e+00, 'g_mlp_post': 1.652755e+01, 'w_up': 1.983489e+00, 'w_down': 4.599392e+00}


def _to_microbatches(a, axis):
    t = _jnp.moveaxis(a, axis, 0)
    t = t.reshape((N_MICROBATCH, t.shape[0] // N_MICROBATCH) + t.shape[1:])
    return _jnp.moveaxis(t, 1, axis + 1)


def setup_inputs(seed: int = 0) -> dict:
    inp = _fwd_setup_inputs(seed)
    key = _jax.random.fold_in(_jax.random.key(seed), 7919)
    shape, _ = _output_shape()
    out = dict(inp)
    out["loss_target"] = _jax.random.normal(_jax.random.fold_in(key, 0), shape, _jnp.float32)
    for i, name in enumerate(TWIN_WEIGHTS):
        w = inp[name].astype(_jnp.float32)
        if MOMENT_SCALE is None:
            s = _jnp.sqrt(_jnp.mean(_jnp.square(w)) + 1e-30)
        else:
            s = MOMENT_SCALE[name]
        km, kv = _jax.random.split(_jax.random.fold_in(key, i + 1))
        out[name] = w
        out["m_" + name] = s * _jax.random.normal(km, w.shape, _jnp.float32)
        out["v_" + name] = (s * s) * _jax.random.uniform(kv, w.shape, _jnp.float32, 0.5, 1.5)
    if N_MICROBATCH > 1:
        for name, axis in PER_EXAMPLE_BATCH_AXIS.items():
            out[name] = _to_microbatches(out[name], axis)
    return {'x': out['x'], 'c': out['c'], 'positions': out['positions'], 'w_ada': out['w_ada'], 'b_ada': out['b_ada'], 'g_attn_pre': out['g_attn_pre'], 'g_attn_post': out['g_attn_post'], 'w_in': out['w_in'], 'sink_a': out['sink_a'], 'g_mix_a': out['g_mix_a'], 'g_mix_b': out['g_mix_b'], 'w_out': out['w_out'], 'g_mlp_pre': out['g_mlp_pre'], 'g_mlp_post': out['g_mlp_post'], 'w_up': out['w_up'], 'w_down': out['w_down'], 'loss_target': out['loss_target'], 'm_w_ada': out['m_w_ada'], 'm_b_ada': out['m_b_ada'], 'm_g_attn_pre': out['m_g_attn_pre'], 'm_g_attn_post': out['m_g_attn_post'], 'm_w_in': out['m_w_in'], 'm_sink_a': out['m_sink_a'], 'm_g_mix_a': out['m_g_mix_a'], 'm_g_mix_b': out['m_g_mix_b'], 'm_w_out': out['m_w_out'], 'm_g_mlp_pre': out['m_g_mlp_pre'], 'm_g_mlp_post': out['m_g_mlp_post'], 'm_w_up': out['m_w_up'], 'm_w_down': out['m_w_down'], 'v_w_ada': out['v_w_ada'], 'v_b_ada': out['v_b_ada'], 'v_g_attn_pre': out['v_g_attn_pre'], 'v_g_attn_post': out['v_g_attn_post'], 'v_w_in': out['v_w_in'], 'v_sink_a': out['v_sink_a'], 'v_g_mix_a': out['v_g_mix_a'], 'v_g_mix_b': out['v_g_mix_b'], 'v_w_out': out['v_w_out'], 'v_g_mlp_pre': out['v_g_mlp_pre'], 'v_g_mlp_post': out['v_g_mlp_post'], 'v_w_up': out['v_w_up'], 'v_w_down': out['v_w_down']}


def _loss(weights, diff, rest, loss_target):
    with _jax.named_scope("forward"):
        args = {**rest, TWIN_DIFF_INPUT: diff, **{k: w.astype(_WEIGHT_DTYPES[k]) for k, w in weights.items()}}
        y = _forward(args)
    with _jax.named_scope("loss_head"):
        err = _jnp.square(y.astype(_jnp.float32) - loss_target)
        return 0.5 * _jnp.sum(_jnp.mean(err, axis=-1)) if err.ndim else 0.5 * err


def _adamw(w, g, m, v):
    m = ADAM_B1 * m + (1.0 - ADAM_B1) * g
    v = ADAM_B2 * v + (1.0 - ADAM_B2) * _jnp.square(g)
    m_hat = m / (1.0 - ADAM_B1 ** ADAM_STEP)
    v_hat = v / (1.0 - ADAM_B2 ** ADAM_STEP)
    delta = -ADAM_LR * (m_hat / (_jnp.sqrt(v_hat) + ADAM_EPS) + ADAM_WD * w)
    return delta, m, v


def reference(x, c, positions, w_ada, b_ada, g_attn_pre, g_attn_post, w_in, sink_a, g_mix_a, g_mix_b, w_out, g_mlp_pre, g_mlp_post, w_up, w_down, loss_target, m_w_ada, m_b_ada, m_g_attn_pre, m_g_attn_post, m_w_in, m_sink_a, m_g_mix_a, m_g_mix_b, m_w_out, m_g_mlp_pre, m_g_mlp_post, m_w_up, m_w_down, v_w_ada, v_b_ada, v_g_attn_pre, v_g_attn_post, v_w_in, v_sink_a, v_g_mix_a, v_g_mix_b, v_w_out, v_g_mlp_pre, v_g_mlp_post, v_w_up, v_w_down):
    given = dict(x=x, c=c, positions=positions, w_ada=w_ada, b_ada=b_ada, g_attn_pre=g_attn_pre, g_attn_post=g_attn_post, w_in=w_in, sink_a=sink_a, g_mix_a=g_mix_a, g_mix_b=g_mix_b, w_out=w_out, g_mlp_pre=g_mlp_pre, g_mlp_post=g_mlp_post, w_up=w_up, w_down=w_down, loss_target=loss_target, m_w_ada=m_w_ada, m_b_ada=m_b_ada, m_g_attn_pre=m_g_attn_pre, m_g_attn_post=m_g_attn_post, m_w_in=m_w_in, m_sink_a=m_sink_a, m_g_mix_a=m_g_mix_a, m_g_mix_b=m_g_mix_b, m_w_out=m_w_out, m_g_mlp_pre=m_g_mlp_pre, m_g_mlp_post=m_g_mlp_post, m_w_up=m_w_up, m_w_down=m_w_down, v_w_ada=v_w_ada, v_b_ada=v_b_ada, v_g_attn_pre=v_g_attn_pre, v_g_attn_post=v_g_attn_post, v_w_in=v_w_in, v_sink_a=v_sink_a, v_g_mix_a=v_g_mix_a, v_g_mix_b=v_g_mix_b, v_w_out=v_w_out, v_g_mlp_pre=v_g_mlp_pre, v_g_mlp_post=v_g_mlp_post, v_w_up=v_w_up, v_w_down=v_w_down)
    weights = {n: given[n] for n in TWIN_WEIGHTS}
    shared = {n: given[n] for n in SHARED_INPUTS}
    per_example = {n: given[n] for n in ['x', 'c', 'positions']}
    grad_fn = _jax.value_and_grad(_loss, argnums=(0, 1))

    def one_microbatch(ex, loss_target):
        ex = dict(ex)
        diff = ex.pop(TWIN_DIFF_INPUT)
        return grad_fn(weights, diff, {**shared, **ex}, loss_target)

    if N_MICROBATCH == 1:
        loss, (grad_w, grad_x) = one_microbatch(per_example, given["loss_target"])
    else:
        def body(carry, xs):
            loss_sum, grad_sum = carry
            l_k, (gw_k, gx_k) = one_microbatch(xs[0], xs[1])
            with _jax.named_scope("update"):
                return (loss_sum + l_k, _jax.tree.map(_jnp.add, grad_sum, gw_k)), gx_k

        init = (_jnp.zeros((), _jnp.float32), _jax.tree.map(_jnp.zeros_like, weights))
        (loss, grad_w), grad_x = _jax.lax.scan(body, init, (per_example, given["loss_target"]))
    with _jax.named_scope("update"):
        delta_w, new_m, new_v = {}, {}, {}
        for n in TWIN_WEIGHTS:
            delta_w[n], new_m[n], new_v[n] = _adamw(weights[n], grad_w[n], given["m_" + n], given["v_" + n])
    return (loss, grad_x, *[grad_w[n] for n in TWIN_WEIGHTS], *[delta_w[n] for n in TWIN_WEIGHTS],
            *[new_m[n] for n in TWIN_WEIGHTS], *[new_v[n] for n in TWIN_WEIGHTS])
```

```python
import functools

import numpy as np
import jax
import jax.numpy as jnp
from jax import lax
from jax.experimental import pallas as pl
from jax.experimental.pallas import tpu as pltpu

f32 = jnp.float32
bf16 = jnp.bfloat16
MESH = pl.DeviceIdType.MESH

D = 1024
SEQ = 2048
BL = 2
HD = 64
AQ = 512
AKV = 128
BW = 512
INW = 2304
DFF = 4096
NMOD = 6
ROT = 16
THETA = 500000.0
EPS = 1e-6
NEG = -1e30
BLK = 128
TM = 256
NJ = SEQ // TM
LANES = 128
NCHIP = 4
NDEV = 8
VMEM_LIMIT = 56 << 20

LR, B1, B2, AEPS, WD, STEP = 0.001, 0.9, 0.999, 1e-08, 0.01, 10

OFF_G_ATTN_PRE, OFF_G_ATTN_POST, OFF_G_MIX_A, OFF_G_MIX_B = 0, 1024, 2048, 2560
OFF_G_MLP_PRE, OFF_G_MLP_POST, OFF_SINK, OFF_LOSS = 3072, 4096, 5120, 5248
PAYW = NMOD * D


def _cp(sem=None):
    return pltpu.CompilerParams(dimension_semantics=sem, vmem_limit_bytes=VMEM_LIMIT)


def _dot(a, b):
    return jnp.dot(a, b, preferred_element_type=f32)


def _dot_nt(a, b):
    return lax.dot_general(a, b, (((1,), (1,)), ((), ())), preferred_element_type=f32)


def _dot_tn(a, b):
    return lax.dot_general(a, b, (((0,), (0,)), ((), ())), preferred_element_type=f32)


def _rms(x):
    r = lax.rsqrt(jnp.mean(x * x, axis=-1, keepdims=True) + EPS)
    return x * r, r


def _rms_bwd(dy, y, r):
    return r * (dy - y * jnp.mean(dy * y, axis=-1, keepdims=True))


def _colsum(v):
    return jnp.sum(v, axis=0, keepdims=True)


def _rope(p, c, s1, s2):
    outs = []
    for c0 in range(0, p.shape[1], LANES):
        pc = p[:, c0:c0 + LANES]
        outs.append(pc * c + pltpu.roll(pc, LANES - ROT // 2, 1) * s1 + pltpu.roll(pc, ROT // 2, 1) * s2)
    return outs[0] if len(outs) == 1 else jnp.concatenate(outs, axis=1)


def _rope_t(g, c, s1, s2):
    outs = []
    for c0 in range(0, g.shape[1], LANES):
        gc = g[:, c0:c0 + LANES]
        outs.append(gc * c + pltpu.roll(gc * s1, ROT // 2, 1) + pltpu.roll(gc * s2, LANES - ROT // 2, 1))
    return outs[0] if len(outs) == 1 else jnp.concatenate(outs, axis=1)


def _perm_store(val, scr, out_ref, d):
    nc = val.shape[1] // LANES
    for c in range(nc):
        scr[c] = val[:, LANES * c:LANES * (c + 1)]
    for c in range(nc):
        for r in range(d):
            out_ref[r, :, LANES * c:LANES * (c + 1)] = scr[c, pl.ds(r, TM // d, stride=d), :].astype(out_ref.dtype)


def _perm_load(in_ref, scr, d):
    nc = in_ref.shape[-1] // LANES
    for c in range(nc):
        for r in range(d):
            scr[c, pl.ds(r, TM // d, stride=d), :] = in_ref[r, :, LANES * c:LANES * (c + 1)].astype(f32)
    return jnp.concatenate([scr[c] for c in range(nc)], axis=1)


def _tok(w, dtype=None):
    return pl.BlockSpec((None, TM, w), lambda b, j: (b, j, 0))


def _perm_spec(d, w):
    return pl.BlockSpec((None, d, TM // d, w), lambda b, j: (b, 0, j, 0))


def _full(shape):
    n = len(shape)
    return pl.BlockSpec(shape, lambda b, j: (0,) * n)


MOD_SPEC = pl.BlockSpec((None, NMOD, D), lambda b, j: (b, 0, 0))
ACCB_SPEC = pl.BlockSpec((None, 8, D), lambda b, j: (b, 0, 0))
ACCG_SPEC = pl.BlockSpec((8, D), lambda b, j: (0, 0))
ACC_SHAPES = [jax.ShapeDtypeStruct((BL, 8, D), f32), jax.ShapeDtypeStruct((8, D), f32)]


def _acc_init(accb_ref, accg_ref):
    b, j = pl.program_id(0), pl.program_id(1)

    @pl.when(j == 0)
    def _():
        accb_ref[...] = jnp.zeros_like(accb_ref)

    @pl.when((b == 0) & (j == 0))
    def _():
        accg_ref[...] = jnp.zeros_like(accg_ref)


def _rope_tables(pos_col, inv_lane):
    def body(p_ref, inv_ref, c_ref, s1_ref, s2_ref):
        ang = p_ref[...].astype(f32) * inv_ref[...]
        j = lax.broadcasted_iota(jnp.int32, (TM, LANES), 1) % HD
        cs, sn = jnp.cos(ang), jnp.sin(ang)
        c_ref[...] = jnp.where(j < ROT, cs, 1.0)
        s1_ref[...] = jnp.where(j < ROT // 2, -sn, 0.0)
        s2_ref[...] = jnp.where((j >= ROT // 2) & (j < ROT), sn, 0.0)

    n = BL * SEQ // TM
    return pl.pallas_call(
        body, name="rope_tables", grid=(n,),
        in_specs=[pl.BlockSpec((TM, 1), lambda i: (i, 0)), pl.BlockSpec((1, LANES), lambda i: (0, 0))],
        out_specs=[pl.BlockSpec((TM, LANES), lambda i: (i, 0))] * 3,
        out_shape=[jax.ShapeDtypeStruct((BL * SEQ, LANES), f32)] * 3,
    )(pos_col, inv_lane)


def _attn_in(x, mod, g_pre, w_in, tc, ts1, ts2):
    def body(x_ref, mod_ref, g_ref, w_ref, c_ref, s1_ref, s2_ref,
             h_ref, qa_ref, ka_ref, va_ref, q1_ref, k1_ref, v1_ref, q4_ref, k4_ref, v4_ref, q16_ref, k16_ref, v16_ref,
             scr):
        xn, _ = _rms(x_ref[...])
        h = (xn * g_ref[...]) * (1.0 + mod_ref[1:2, :]) + mod_ref[0:1, :]
        hb = h.astype(bf16)
        h_ref[...] = hb
        proj = _dot(hb, w_ref[...])
        c, s1, s2 = c_ref[...], s1_ref[...], s2_ref[...]
        o1, o2, o3, o4, o5 = AQ, AQ + AKV, AQ + 2 * AKV, AQ + 2 * AKV + BW, AQ + 2 * AKV + 2 * BW
        qa_ref[...] = (_rope(proj[:, :o1], c, s1, s2) * 0.125).astype(bf16)
        ka_ref[...] = _rope(proj[:, o1:o2], c, s1, s2).astype(bf16)
        va_ref[...] = proj[:, o2:o3].astype(bf16)
        qb = _rope(proj[:, o3:o4], c, s1, s2) * 0.125
        kb = _rope(proj[:, o4:o5], c, s1, s2)
        vb = proj[:, o5:]
        for val, r1, r4, r16 in ((qb, q1_ref, q4_ref, q16_ref), (kb, k1_ref, k4_ref, k16_ref), (vb, v1_ref, v4_ref, v16_ref)):
            r1[...] = val.astype(bf16)
            _perm_store(val, scr, r4, 4)
            _perm_store(val, scr, r16, 16)

    nat = lambda w: jax.ShapeDtypeStruct((BL, SEQ, w), bf16)
    p4 = jax.ShapeDtypeStruct((BL, 4, SEQ // 4, BW), bf16)
    p16 = jax.ShapeDtypeStruct((BL, 16, SEQ // 16, BW), bf16)
    return pl.pallas_call(
        body, name="attn_in", grid=(BL, NJ),
        in_specs=[_tok(D), MOD_SPEC, _full((1, D)), _full((D, INW)), _tok(LANES), _tok(LANES), _tok(LANES)],
        out_specs=[_tok(D), _tok(AQ), _tok(AKV), _tok(AKV)] + [_tok(BW)] * 3 + [_perm_spec(4, BW)] * 3 + [_perm_spec(16, BW)] * 3,
        out_shape=[nat(D), nat(AQ), nat(AKV), nat(AKV)] + [nat(BW)] * 3 + [p4] * 3 + [p16] * 3,
        scratch_shapes=[pltpu.VMEM((BW // LANES, TM, LANES), f32)],
        compiler_params=_cp(("arbitrary", "arbitrary")),
    )(x, mod, g_pre, w_in, tc, ts1, ts2)


def _pair_kv(ref, p, gqa, lo):
    if not gqa:
        return ref[:, LANES * p:LANES * (p + 1)]
    k = ref[...]
    kr = pltpu.roll(k, HD, 1)
    return jnp.where(lo, k, kr) if p < 2 else jnp.where(lo, kr, k)


def _valid_mask(blk_idx, nb, max_dist):
    if nb == 1:
        qi = lax.broadcasted_iota(jnp.int32, (BLK, BLK), 0)
        kj = lax.broadcasted_iota(jnp.int32, (BLK, BLK), 1)
        return kj <= qi
    qi = lax.broadcasted_iota(jnp.int32, (BLK, 2 * BLK), 0)
    col = lax.broadcasted_iota(jnp.int32, (BLK, 2 * BLK), 1)
    prev_ok = jnp.logical_and(jnp.logical_and(col < BLK, col >= qi + (BLK - max_dist)), blk_idx > 0)
    return jnp.logical_or(jnp.logical_and(col >= BLK, (col - BLK) <= qi), prev_ok)


def _attn_fwd(q, k, v, sink, *, max_dist, name):
    n, l, w = q.shape
    wk = k.shape[-1]
    nb = l // BLK
    gqa = wk != w
    has_sink = sink is not None

    def body(*refs):
        if has_sink:
            sink_ref, refs = refs[0], refs[1:]
        if nb > 1:
            q_ref, kc_ref, kp_ref, vc_ref, vp_ref, o_ref, lse_ref = refs
        else:
            q_ref, kc_ref, vc_ref, o_ref, lse_ref = refs
        i = pl.program_id(1)
        lo = lax.broadcasted_iota(jnp.int32, (BLK, LANES), 1) < HD
        valid = _valid_mask(i, nb, max_dist)
        for p in range(w // LANES):
            qpair = q_ref[:, LANES * p:LANES * (p + 1)]
            kcat, vcat = _pair_kv(kc_ref, p, gqa, lo), _pair_kv(vc_ref, p, gqa, lo)
            if nb > 1:
                kcat = jnp.concatenate([_pair_kv(kp_ref, p, gqa, lo), kcat], axis=0)
                vcat = jnp.concatenate([_pair_kv(vp_ref, p, gqa, lo), vcat], axis=0)
            lov = lax.broadcasted_iota(jnp.int32, vcat.shape, 1) < HD
            o_pair = jnp.zeros((BLK, LANES), f32)
            lse_pair = jnp.zeros((BLK, LANES), f32)
            for hh in range(2):
                msk = lo if hh == 0 else jnp.logical_not(lo)
                mskv = lov if hh == 0 else jnp.logical_not(lov)
                s = _dot_nt(jnp.where(msk, qpair, jnp.zeros_like(qpair)), kcat)
                s = jnp.where(valid, s, NEG)
                m = jnp.max(s, axis=-1, keepdims=True)
                if has_sink:
                    sk = sink_ref[0, 2 * p + hh]
                    m = jnp.maximum(m, sk)
                e = jnp.exp(s - m)
                den = jnp.sum(e, axis=-1, keepdims=True)
                if has_sink:
                    den = den + jnp.exp(sk - m)
                pn = (e * (1.0 / den)).astype(bf16)
                o_pair = o_pair + _dot(pn, jnp.where(mskv, vcat, jnp.zeros_like(vcat)))
                lse_pair = jnp.where(msk, m + jnp.log(den), lse_pair)
            o_ref[:, LANES * p:LANES * (p + 1)] = o_pair
            lse_ref[:, LANES * p:LANES * (p + 1)] = lse_pair

    cur = lambda ww: pl.BlockSpec((None, BLK, ww), lambda a, i: (a, i, 0))
    prev = lambda ww: pl.BlockSpec((None, BLK, ww), lambda a, i: (a, jnp.maximum(i - 1, 0), 0))
    in_specs = [cur(w), cur(wk)] + ([prev(wk)] if nb > 1 else []) + [cur(wk)] + ([prev(wk)] if nb > 1 else [])
    args = [q, k] + ([k] if nb > 1 else []) + [v] + ([v] if nb > 1 else [])
    if has_sink:
        in_specs = [pl.BlockSpec(memory_space=pltpu.SMEM)] + in_specs
        args = [sink] + args
    return pl.pallas_call(
        body, name=name, grid=(n, nb), in_specs=in_specs,
        out_specs=[cur(w), cur(w)], out_shape=[jax.ShapeDtypeStruct((n, l, w), f32)] * 2,
        compiler_params=_cp(("arbitrary", "arbitrary")),
    )(*args)


def _attn_bwd(q, k, v, do, o, lse, sink, *, max_dist, name):
    n, l, w = q.shape
    wk = k.shape[-1]
    nb = l // BLK
    gqa = wk != w
    has_sink = sink is not None

    def body(*refs):
        if has_sink:
            sink_ref, refs = refs[0], refs[1:]
        if nb > 1:
            q_ref, kc_ref, kp_ref, vc_ref, vp_ref, do_ref, o_ref, lse_ref = refs[:8]
            rest = refs[8:]
        else:
            q_ref, kc_ref, vc_ref, do_ref, o_ref, lse_ref = refs[:6]
            rest = refs[6:]
        if has_sink:
            dq_ref, dk_ref, dv_ref, dsink_ref = rest[:4]
            rest = rest[4:]
        else:
            dq_ref, dk_ref, dv_ref = rest[:3]
            rest = rest[3:]
        step = pl.program_id(1)
        blk_idx = nb - 1 - step
        if nb > 1:
            ck, cv = rest

            @pl.when(step == 0)
            def _():
                ck[...] = jnp.zeros_like(ck)
                cv[...] = jnp.zeros_like(cv)

        if has_sink:
            @pl.when((pl.program_id(0) == 0) & (step == 0))
            def _():
                dsink_ref[...] = jnp.zeros_like(dsink_ref)

        lo = lax.broadcasted_iota(jnp.int32, (BLK, LANES), 1) < HD
        valid = _valid_mask(blk_idx, nb, max_dist)
        rows = 2 * BLK if nb > 1 else BLK
        gk = [jnp.zeros((rows, LANES), f32), jnp.zeros((rows, LANES), f32)]
        gv = [jnp.zeros((rows, LANES), f32), jnp.zeros((rows, LANES), f32)]
        for p in range(w // LANES):
            sl = slice(LANES * p, LANES * (p + 1))
            qpair = q_ref[:, sl]
            dopair = do_ref[:, sl]
            prod = dopair * o_ref[:, sl]
            lsepair = lse_ref[:, sl]
            kcat, vcat = _pair_kv(kc_ref, p, gqa, lo), _pair_kv(vc_ref, p, gqa, lo)
            if nb > 1:
                kcat = jnp.concatenate([_pair_kv(kp_ref, p, gqa, lo), kcat], axis=0)
                vcat = jnp.concatenate([_pair_kv(vp_ref, p, gqa, lo), vcat], axis=0)
            lov = lax.broadcasted_iota(jnp.int32, kcat.shape, 1) < HD
            dq_pair = jnp.zeros((BLK, LANES), f32)
            dk_pair = jnp.zeros((rows, LANES), f32)
            dv_pair = jnp.zeros((rows, LANES), f32)
            for hh in range(2):
                msk = lo if hh == 0 else jnp.logical_not(lo)
                mskv = lov if hh == 0 else jnp.logical_not(lov)
                qm = jnp.where(msk, qpair, jnp.zeros_like(qpair))
                dom = jnp.where(msk, dopair, 0.0).astype(bf16)
                delta = jnp.sum(jnp.where(msk, prod, 0.0), axis=-1, keepdims=True)
                lse_h = lsepair[:, HD * hh:HD * hh + 1]
                s = jnp.where(valid, _dot_nt(qm, kcat), NEG)
                pr = jnp.exp(s - lse_h)
                dp = _dot_nt(dom, vcat)
                ds = (pr * (dp - delta)).astype(bf16)
                prb = pr.astype(bf16)
                dq_pair = dq_pair + _dot(ds, jnp.where(mskv, kcat, jnp.zeros_like(kcat)))
                dk_pair = dk_pair + _dot_tn(ds, qm)
                dv_pair = dv_pair + _dot_tn(prb, dom)
                if has_sink:
                    h = 2 * p + hh
                    dsk = -jnp.sum(jnp.exp(sink_ref[0, h] - lse_h) * delta, keepdims=True)
                    dsink_ref[h:h + 1, :] += jnp.broadcast_to(dsk, (1, LANES))
            dq_ref[:, sl] = dq_pair
            if gqa:
                gk[p // 2] = gk[p // 2] + dk_pair
                gv[p // 2] = gv[p // 2] + dv_pair
            elif nb > 1:
                dk_ref[:, sl] = dk_pair[BLK:] + ck[:, sl]
                dv_ref[:, sl] = dv_pair[BLK:] + cv[:, sl]
                ck[:, sl] = dk_pair[:BLK]
                cv[:, sl] = dv_pair[:BLK]
            else:
                dk_ref[:, sl] = dk_pair
                dv_ref[:, sl] = dv_pair
        if gqa:
            lor = lax.broadcasted_iota(jnp.int32, (rows, LANES), 1) < HD
            fold = lambda g: jnp.where(lor, g[0] + pltpu.roll(g[0], HD, 1), g[1] + pltpu.roll(g[1], HD, 1))
            dk_full, dv_full = fold(gk), fold(gv)
            dk_ref[...] = dk_full[BLK:] + ck[...]
            dv_ref[...] = dv_full[BLK:] + cv[...]
            ck[...] = dk_full[:BLK]
            cv[...] = dv_full[:BLK]

    cur = lambda ww: pl.BlockSpec((None, BLK, ww), lambda a, i: (a, nb - 1 - i, 0))
    prev = lambda ww: pl.BlockSpec((None, BLK, ww), lambda a, i: (a, jnp.maximum(nb - 2 - i, 0), 0))
    in_specs = [cur(w), cur(wk)] + ([prev(wk)] if nb > 1 else []) + [cur(wk)] + ([prev(wk)] if nb > 1 else []) + [cur(w)] * 3
    args = [q, k] + ([k] if nb > 1 else []) + [v] + ([v] if nb > 1 else []) + [do, o, lse]
    out_specs = [cur(w), cur(wk), cur(wk)]
    out_shape = [jax.ShapeDtypeStruct((n, l, w), f32), jax.ShapeDtypeStruct((n, l, wk), f32), jax.ShapeDtypeStruct((n, l, wk), f32)]
    if has_sink:
        in_specs = [pl.BlockSpec(memory_space=pltpu.SMEM)] + in_specs
        args = [sink] + args
        out_specs.append(pl.BlockSpec((8, LANES), lambda a, i: (0, 0)))
        out_shape.append(jax.ShapeDtypeStruct((8, LANES), f32))
    scratch = [pltpu.VMEM((BLK, wk), f32), pltpu.VMEM((BLK, wk), f32)] if nb > 1 else []
    return pl.pallas_call(
        body, name=name, grid=(n, nb), in_specs=in_specs, out_specs=out_specs, out_shape=out_shape,
        scratch_shapes=scratch, compiler_params=_cp(("arbitrary", "arbitrary")),
    )(*args)


def _mix_out(oa, o1, l1, o4, l4, o16, l16, g_mix_a, g_mix_b, w_out, x, mod, g_post):
    def body(oa_ref, o1_ref, l1_ref, o4_ref, l4_ref, o16_ref, l16_ref, ga_ref, gb_ref, w_ref, x_ref, mod_ref, gp_ref,
             x1_ref, y_ref, mixed_ref, ob_ref, ob4_ref, ob16_ref, w1_ref, w4_ref, w16_ref, scr):
        o4v = _perm_load(o4_ref, scr, 4)
        l4v = _perm_load(l4_ref, scr, 4)
        o16v = _perm_load(o16_ref, scr, 16)
        l16v = _perm_load(l16_ref, scr, 16)
        l1v = l1_ref[...]
        m = jnp.maximum(jnp.maximum(l1v, l4v), l16v)
        e1, e4, e16 = jnp.exp(l1v - m), jnp.exp(l4v - m), jnp.exp(l16v - m)
        z = e1 + e4 + e16
        w1, w4, w16 = e1 / z, e4 / z, e16 / z
        ob = w1 * o1_ref[...] + w4 * o4v + w16 * o16v
        w1_ref[...] = w1
        w4_ref[...] = w4
        w16_ref[...] = w16
        ob_ref[...] = ob
        _perm_store(ob, scr, ob4_ref, 4)
        _perm_store(ob, scr, ob16_ref, 16)
        oan, _ = _rms(oa_ref[...])
        obn, _ = _rms(ob)
        mixed = jnp.concatenate([oan * ga_ref[...], obn * gb_ref[...]], axis=1).astype(bf16)
        mixed_ref[...] = mixed
        y = _dot(mixed, w_ref[...])
        y_ref[...] = y
        yn, _ = _rms(y)
        x1_ref[...] = x_ref[...] + mod_ref[2:3, :] * (yn * gp_ref[...])

    nat = lambda w, dt: jax.ShapeDtypeStruct((BL, SEQ, w), dt)
    return pl.pallas_call(
        body, name="mix_out", grid=(BL, NJ),
        in_specs=[_tok(AQ), _tok(BW), _tok(BW), _perm_spec(4, BW), _perm_spec(4, BW), _perm_spec(16, BW), _perm_spec(16, BW),
                  _full((1, AQ)), _full((1, BW)), _full((D, D)), _tok(D), MOD_SPEC, _full((1, D))],
        out_specs=[_tok(D), _tok(D), _tok(D), _tok(BW), _perm_spec(4, BW), _perm_spec(16, BW), _tok(BW), _tok(BW), _tok(BW)],
        out_shape=[nat(D, f32), nat(D, f32), nat(D, bf16), nat(BW, f32),
                   jax.ShapeDtypeStruct((BL, 4, SEQ // 4, BW), f32), jax.ShapeDtypeStruct((BL, 16, SEQ // 16, BW), f32),
                   nat(BW, f32), nat(BW, f32), nat(BW, f32)],
        scratch_shapes=[pltpu.VMEM((BW // LANES, TM, LANES), f32)],
        compiler_params=_cp(("arbitrary", "arbitrary")),
    )(oa, o1, l1, o4, l4, o16, l16, g_mix_a, g_mix_b, w_out, x, mod, g_post)


def _mlp_up(x1, mod, g_pre, w_up):
    def body(x_ref, mod_ref, g_ref, w_ref, h_ref, u_ref, a_ref):
        xn, _ = _rms(x_ref[...])
        h = (xn * g_ref[...]) * (1.0 + mod_ref[4:5, :]) + mod_ref[3:4, :]
        hb = h.astype(bf16)
        h_ref[...] = hb
        for s in range(NCHIP):
            u = _dot(hb, w_ref[s])
            u_ref[:, D * s:D * (s + 1)] = u.astype(bf16)
            a_ref[:, D * s:D * (s + 1)] = jnp.square(jnp.maximum(u, 0.0)).astype(bf16)

    nat = lambda w: jax.ShapeDtypeStruct((BL, SEQ, w), bf16)
    return pl.pallas_call(
        body, name="mlp_up", grid=(BL, NJ),
        in_specs=[_tok(D), MOD_SPEC, _full((1, D)), _full((NCHIP, D, D))],
        out_specs=[_tok(D), _tok(DFF), _tok(DFF)], out_shape=[nat(D), nat(DFF), nat(DFF)],
        compiler_params=_cp(("arbitrary", "arbitrary")),
    )(x1, mod, g_pre, w_up)


def _mlp_down(a, w_down, x1, target, mod, g_post):
    def body(a_ref, w_ref, x_ref, t_ref, mod_ref, g_ref, gx_ref, dy_ref, accb_ref, accg_ref):
        _acc_init(accb_ref, accg_ref)
        y2 = _dot(a_ref[...], w_ref[...])
        yn, r = _rms(y2)
        g = g_ref[...]
        gt = mod_ref[5:6, :]
        n2 = yn * g
        err = x_ref[...] + gt * n2 - t_ref[...]
        gout = err * (1.0 / D)
        gx_ref[...] = gout
        dn2 = gout * gt
        dy_ref[...] = _rms_bwd(dn2 * g, yn, r).astype(bf16)
        accb_ref[0:1, :] += _colsum(gout * n2)
        accg_ref[0:1, :] += _colsum(dn2 * yn)
        accg_ref[1:2, :] += jnp.broadcast_to(jnp.sum(err * err, keepdims=True), (1, D))

    return pl.pallas_call(
        body, name="mlp_down", grid=(BL, NJ),
        in_specs=[_tok(DFF), _full((DFF, D)), _tok(D), _tok(D), MOD_SPEC, _full((1, D))],
        out_specs=[_tok(D), _tok(D), ACCB_SPEC, ACCG_SPEC],
        out_shape=[jax.ShapeDtypeStruct((BL, SEQ, D), f32), jax.ShapeDtypeStruct((BL, SEQ, D), bf16)] + ACC_SHAPES,
        compiler_params=_cp(("arbitrary", "arbitrary")),
    )(a, w_down, x1, target, mod, g_post)


def _mlp_bwd(dy2, u, w_down, w_up, x1, gx, mod, g_pre):
    def body(dy_ref, u_ref, wd_hbm, wu_hbm, x_ref, gx_ref, mod_ref, g_ref, du_ref, gx1_ref, accb_ref, accg_ref, wd, wu, sem):
        _acc_init(accb_ref, accg_ref)

        @pl.when((pl.program_id(0) == 0) & (pl.program_id(1) == 0))
        def _():
            c1 = pltpu.make_async_copy(wd_hbm, wd, sem.at[0])
            c2 = pltpu.make_async_copy(wu_hbm, wu, sem.at[1])
            c1.start()
            c2.start()
            c1.wait()
            c2.wait()

        dy = dy_ref[...]
        dh = jnp.zeros((TM, D), f32)
        for s in range(NCHIP):
            sl = slice(D * s, D * (s + 1))
            da = _dot_nt(dy, wd[sl, :])
            du = (da * (2.0 * jnp.maximum(u_ref[:, sl].astype(f32), 0.0))).astype(bf16)
            du_ref[:, sl] = du
            dh = dh + _dot_nt(du, wu[s])
        xn, r = _rms(x_ref[...])
        g = g_ref[...]
        n = xn * g
        dn = dh * (1.0 + mod_ref[4:5, :])
        gx1_ref[...] = gx_ref[...] + _rms_bwd(dn * g, xn, r)
        accb_ref[0:1, :] += _colsum(dh * n)
        accb_ref[1:2, :] += _colsum(dh)
        accg_ref[0:1, :] += _colsum(dn * xn)

    anyspec = pl.BlockSpec(memory_space=pl.ANY)
    return pl.pallas_call(
        body, name="mlp_bwd", grid=(BL, NJ),
        in_specs=[_tok(D), _tok(DFF), anyspec, anyspec, _tok(D), _tok(D), MOD_SPEC, _full((1, D))],
        out_specs=[_tok(DFF), _tok(D), ACCB_SPEC, ACCG_SPEC],
        out_shape=[jax.ShapeDtypeStruct((BL, SEQ, DFF), bf16), jax.ShapeDtypeStruct((BL, SEQ, D), f32)] + ACC_SHAPES,
        scratch_shapes=[pltpu.VMEM((DFF, D), bf16), pltpu.VMEM((NCHIP, D, D), bf16), pltpu.SemaphoreType.DMA((2,))],
        compiler_params=_cp(("arbitrary", "arbitrary")),
    )(dy2, u, w_down, w_up, x1, gx, mod, g_pre)


def _matmul_tn(a, b, *, tn, col_blocked, name):
    t, m = a.shape
    n = b.shape[1]
    tmm = min(m, 1024)
    tk = 512
    nk = t // tk

    def body(a_ref, b_ref, o_ref):
        @pl.when(pl.program_id(2) == 0)
        def _():
            o_ref[...] = jnp.zeros_like(o_ref)

        o_ref[...] += _dot_tn(a_ref[...], b_ref[...])

    if col_blocked:
        out_spec = pl.BlockSpec((None, tmm, tn), lambda i, j, k: (j, i, 0))
        out_shape = jax.ShapeDtypeStruct((n // tn, m, tn), f32)
    else:
        out_spec = pl.BlockSpec((tmm, tn), lambda i, j, k: (i, j))
        out_shape = jax.ShapeDtypeStruct((m, n), f32)
    return pl.pallas_call(
        body, name=name, grid=(m // tmm, n // tn, nk),
        in_specs=[pl.BlockSpec((tk, tmm), lambda i, j, k: (k, i)), pl.BlockSpec((tk, tn), lambda i, j, k: (k, j))],
        out_specs=out_spec, out_shape=out_shape,
        compiler_params=_cp(("arbitrary", "arbitrary", "arbitrary")),
    )(a, b)


def _attn_out_bwd(gx1, y, mod, g_post, w_out, oa, ob, g_mix_a, g_mix_b, w1, w4, w16):
    def body(gx_ref, y_ref, mod_ref, gp_ref, w_ref, oa_ref, ob_ref, ga_ref, gb_ref, w1_ref, w4_ref, w16_ref,
             dy_ref, doa_ref, do1_ref, do4_ref, do16_ref, accb_ref, accg_ref, scr):
        _acc_init(accb_ref, accg_ref)
        gx1v = gx_ref[...]
        yn, ry = _rms(y_ref[...])
        gp = gp_ref[...]
        gt = mod_ref[2:3, :]
        dn1 = gx1v * gt
        dy = _rms_bwd(dn1 * gp, yn, ry).astype(bf16)
        dy_ref[...] = dy
        dmixed = _dot_nt(dy, w_ref[...])
        dma, dmb = dmixed[:, :AQ], dmixed[:, AQ:]
        oan, ra = _rms(oa_ref[...])
        obn, rb = _rms(ob_ref[...])
        doa_ref[...] = _rms_bwd(dma * ga_ref[...], oan, ra)
        dob = _rms_bwd(dmb * gb_ref[...], obn, rb)
        do1_ref[...] = w1_ref[...] * dob
        _perm_store(w4_ref[...] * dob, scr, do4_ref, 4)
        _perm_store(w16_ref[...] * dob, scr, do16_ref, 16)
        accb_ref[0:1, :] += _colsum(gx1v * (yn * gp))
        accg_ref[0:1, :] += _colsum(dn1 * yn)
        accg_ref[1:2, :] += jnp.concatenate([_colsum(dma * oan), _colsum(dmb * obn)], axis=1)

    nat = lambda w, dt: jax.ShapeDtypeStruct((BL, SEQ, w), dt)
    return pl.pallas_call(
        body, name="attn_out_bwd", grid=(BL, NJ),
        in_specs=[_tok(D), _tok(D), MOD_SPEC, _full((1, D)), _full((D, D)), _tok(AQ), _tok(BW), _full((1, AQ)), _full((1, BW)),
                  _tok(BW), _tok(BW), _tok(BW)],
        out_specs=[_tok(D), _tok(AQ), _tok(BW), _perm_spec(4, BW), _perm_spec(16, BW), ACCB_SPEC, ACCG_SPEC],
        out_shape=[nat(D, bf16), nat(AQ, f32), nat(BW, f32), jax.ShapeDtypeStruct((BL, 4, SEQ // 4, BW), f32),
                   jax.ShapeDtypeStruct((BL, 16, SEQ // 16, BW), f32)] + ACC_SHAPES,
        scratch_shapes=[pltpu.VMEM((BW // LANES, TM, LANES), f32)],
        compiler_params=_cp(("arbitrary", "arbitrary")),
    )(gx1, y, mod, g_post, w_out, oa, ob, g_mix_a, g_mix_b, w1, w4, w16)


def _attn_in_bwd(dqa, dka, dva, d1, d4, d16, tc, ts1, ts2, w_in, x, gx1, mod, g_pre):
    def body(dqa_ref, dka_ref, dva_ref, dq1_ref, dk1_ref, dv1_ref, dq4_ref, dk4_ref, dv4_ref, dq16_ref, dk16_ref, dv16_ref,
             c_ref, s1_ref, s2_ref, w_ref, x_ref, gx_ref, mod_ref, g_ref, dproj_ref, dx_ref, accb_ref, accg_ref, scr):
        _acc_init(accb_ref, accg_ref)
        c, s1, s2 = c_ref[...], s1_ref[...], s2_ref[...]
        tot = lambda r1, r4, r16: r1[...] + _perm_load(r4, scr, 4) + _perm_load(r16, scr, 16)
        dqb = tot(dq1_ref, dq4_ref, dq16_ref)
        dkb = tot(dk1_ref, dk4_ref, dk16_ref)
        dvb = tot(dv1_ref, dv4_ref, dv16_ref)
        dproj = jnp.concatenate([
            _rope_t(dqa_ref[...], c, s1, s2) * 0.125, _rope_t(dka_ref[...], c, s1, s2), dva_ref[...],
            _rope_t(dqb, c, s1, s2) * 0.125, _rope_t(dkb, c, s1, s2), dvb], axis=1).astype(bf16)
        dproj_ref[...] = dproj
        dh = _dot_nt(dproj, w_ref[...])
        xn, r = _rms(x_ref[...])
        g = g_ref[...]
        dn = dh * (1.0 + mod_ref[1:2, :])
        dx_ref[...] = gx_ref[...] + _rms_bwd(dn * g, xn, r)
        accb_ref[0:1, :] += _colsum(dh * (xn * g))
        accb_ref[1:2, :] += _colsum(dh)
        accg_ref[0:1, :] += _colsum(dn * xn)

    return pl.pallas_call(
        body, name="attn_in_bwd", grid=(BL, NJ),
        in_specs=[_tok(AQ), _tok(AKV), _tok(AKV)] + [_tok(BW)] * 3 + [_perm_spec(4, BW)] * 3 + [_perm_spec(16, BW)] * 3
                 + [_tok(LANES)] * 3 + [_full((D, INW)), _tok(D), _tok(D), MOD_SPEC, _full((1, D))],
        out_specs=[_tok(INW), _tok(D), ACCB_SPEC, ACCG_SPEC],
        out_shape=[jax.ShapeDtypeStruct((BL, SEQ, INW), bf16), jax.ShapeDtypeStruct((BL, SEQ, D), f32)] + ACC_SHAPES,
        scratch_shapes=[pltpu.VMEM((BW // LANES, TM, LANES), f32)],
        compiler_params=_cp(("arbitrary", "arbitrary")),
    )(dqa, dka, dva, *d1, *d4, *d16, tc, ts1, ts2, w_in, x, gx1, mod, g_pre)


def _local_step(x, positions, mod, target, w_in, w_out, w_up, w_down, g_attn_pre, g_attn_post, sink_a, g_mix_a, g_mix_b,
                g_mlp_pre, g_mlp_post):
    inv = np.float32(THETA) ** (-np.arange(0, ROT, 2, dtype=np.float32) / np.float32(ROT))
    lane = np.arange(LANES) % HD
    inv_lane = jnp.asarray(np.where(lane < ROT, inv[lane % (ROT // 2)], 0.0).astype(np.float32)[None, :])
    tabs = _rope_tables(positions.reshape(BL * SEQ, 1), inv_lane)
    tc, ts1, ts2 = [t.reshape(BL, SEQ, LANES) for t in tabs]

    (h, qa, ka, va, q1, k1, v1, q4, k4, v4, q16, k16, v16) = _attn_in(x, mod, g_attn_pre, w_in, tc, ts1, ts2)
    seqs = lambda t: t.reshape(t.shape[0] * t.shape[1], t.shape[2], t.shape[3])
    q4, k4, v4, q16, k16, v16 = [seqs(t) for t in (q4, k4, v4, q16, k16, v16)]
    oa, la = _attn_fwd(qa, ka, va, sink_a, max_dist=BLK - 1, name="attn_a_fwd")
    o1, l1 = _attn_fwd(q1, k1, v1, None, max_dist=BLK, name="attn_b1_fwd")
    o4, l4 = _attn_fwd(q4, k4, v4, None, max_dist=BLK, name="attn_b4_fwd")
    o16, l16 = _attn_fwd(q16, k16, v16, None, max_dist=BLK, name="attn_b16_fwd")
    b4 = lambda t: t.reshape(BL, 4, SEQ // 4, BW)
    b16 = lambda t: t.reshape(BL, 16, SEQ // 16, BW)
    x1, y, mixed, ob, ob4, ob16, w1, w4, w16 = _mix_out(oa, o1, l1, b4(o4), b4(l4), b16(o16), b16(l16), g_mix_a, g_mix_b,
                                                        w_out, x, mod, g_attn_post)
    h2, u, a = _mlp_up(x1, mod, g_mlp_pre, w_up)
    gx, dy2, accb_d, accg_d = _mlp_down(a, w_down, x1, target, mod, g_mlp_post)

    flat = lambda t: t.reshape(BL * SEQ, t.shape[-1])
    gw_down = _matmul_tn(flat(a), flat(dy2), tn=D, col_blocked=False, name="grad_w_down")
    du, gx1, accb_m, accg_m = _mlp_bwd(dy2, u, w_down, w_up, x1, gx, mod, g_mlp_pre)
    gw_up = _matmul_tn(flat(h2), flat(du), tn=D, col_blocked=True, name="grad_w_up")

    dy, doa, do1, do4, do16, accb_o, accg_o = _attn_out_bwd(gx1, y, mod, g_attn_post, w_out, oa, ob, g_mix_a, g_mix_b, w1, w4, w16)
    gw_out = _matmul_tn(flat(mixed), flat(dy), tn=D, col_blocked=False, name="grad_w_out")
    dqa, dka, dva, dsink = _attn_bwd(qa, ka, va, doa, oa, la, sink_a, max_dist=BLK - 1, name="attn_a_bwd")
    d1 = _attn_bwd(q1, k1, v1, do1, ob, l1, None, max_dist=BLK, name="attn_b1_bwd")
    d4 = _attn_bwd(q4, k4, v4, seqs(do4), seqs(ob4), l4, None, max_dist=BLK, name="attn_b4_bwd")
    d16 = _attn_bwd(q16, k16, v16, seqs(do16), seqs(ob16), l16, None, max_dist=BLK, name="attn_b16_bwd")
    dproj, grad_x, accb_i, accg_i = _attn_in_bwd(dqa, dka, dva, d1, [b4(t) for t in d4], [b16(t) for t in d16],
                                                 tc, ts1, ts2, w_in, x, gx1, mod, g_attn_pre)
    gw_in = _matmul_tn(flat(h), flat(dproj), tn=INW, col_blocked=False, name="grad_w_in")

    return grad_x, gw_in, gw_out, gw_up, gw_down, (accb_i, accb_o, accb_m, accb_d, accg_i, accg_o, accg_m, accg_d, dsink)


ADAW = NMOD * D // NCHIP


def _pos():
    return lax.axis_index("x"), lax.axis_index("y"), lax.axis_index("c")


def _flip(v, bit):
    return 1 - v if bit else v


def _all_peers(x, y, c):
    return [(_flip(x, k >> 2 & 1), _flip(y, k >> 1 & 1), _flip(c, k & 1)) for k in range(1, NDEV)]


def _other_chips(x, y):
    return [(1 - x, y), (x, 1 - y), (1 - x, 1 - y)]


def _rcopy(src, dst, send, recv, k, dev):
    return pltpu.make_async_remote_copy(src_ref=src, dst_ref=dst, send_sem=send.at[k], recv_sem=recv.at[k],
                                        device_id=dev, device_id_type=MESH)


def _gather_small(src, buf, send, recv):
    x, y, c = _pos()
    me = 4 * x + 2 * y + c
    peers = _all_peers(x, y, c)
    sends = [_rcopy(src, buf.at[me], send, recv, k, p) for k, p in enumerate(peers)]
    for cp in sends:
        cp.start()
    for k, (px, py, pc) in enumerate(peers):
        _rcopy(src, buf.at[4 * px + 2 * py + pc], send, recv, k, (px, py, pc)).wait_recv()
    for cp in sends:
        cp.wait_send()
    return me


def _ada_fwd(c_in, w_ada, b_cols):
    def body(c_ref, w_ref, b_ref, mod_ref, cond_ref, cbuf, mbuf, s1, r1, s2, r2):
        x, y, c = _pos()
        chip = 2 * x + y
        me = _gather_small(c_ref, cbuf, s1, r1)
        cbuf[me] = c_ref[...]
        for i in range(NDEV):
            cond_ref[BL * i:BL * (i + 1), :] = cbuf[i]
        call = cond_ref[...]
        cond = call / (1.0 + jnp.exp(-call))
        cond_ref[...] = cond
        mbuf[chip] = jnp.dot(cond, w_ref[...], preferred_element_type=f32, precision=lax.Precision.HIGHEST) + b_ref[...]
        chips = _other_chips(x, y)
        sends = [_rcopy(mbuf.at[chip], mbuf.at[chip], s2, r2, j, (px, py, c)) for j, (px, py) in enumerate(chips)]
        for cp in sends:
            cp.start()
        for j, (px, py) in enumerate(chips):
            _rcopy(mbuf.at[chip], mbuf.at[2 * px + py], s2, r2, j, (px, py, c)).wait_recv()
        for cp in sends:
            cp.wait_send()
        row = lax.broadcasted_iota(jnp.int32, (BL * NDEV, ADAW), 0)
        for s in range(NCHIP):
            slab = mbuf[s]
            for j in range(BL):
                mod_ref[j:j + 1, ADAW * s:ADAW * (s + 1)] = jnp.sum(jnp.where(row == BL * me + j, slab, 0.0), axis=0, keepdims=True)

    vm = pl.BlockSpec(memory_space=pltpu.VMEM)
    return pl.pallas_call(
        body, name="ada_fwd", in_specs=[vm, vm, vm], out_specs=[vm, vm],
        out_shape=[jax.ShapeDtypeStruct((BL, NMOD * D), f32), jax.ShapeDtypeStruct((BL * NDEV, D), f32)],
        scratch_shapes=[pltpu.VMEM((NDEV, BL, D), f32), pltpu.VMEM((NCHIP, BL * NDEV, ADAW), f32),
                        pltpu.SemaphoreType.DMA((NDEV - 1,)), pltpu.SemaphoreType.DMA((NDEV - 1,)),
                        pltpu.SemaphoreType.DMA((NCHIP - 1,)), pltpu.SemaphoreType.DMA((NCHIP - 1,))],
        compiler_params=pltpu.CompilerParams(vmem_limit_bytes=VMEM_LIMIT),
    )(c_in, w_ada, b_cols)


def _small_allreduce(accs, cond_all):
    def body(bi, bo, bm, bd, gi, go, gm, gd, dsink, cond_ref, gw_ref, gb_ref, small_ref, pay, pbuf, dall, s1, r1):
        x, y, c = _pos()
        chip = 2 * x + y
        pay[...] = jnp.zeros_like(pay)
        for b in range(BL):
            for k, (ref, r) in enumerate(((bi, 1), (bi, 0), (bo, 0), (bm, 1), (bm, 0), (bd, 0))):
                pay[b:b + 1, D * k:D * (k + 1)] = ref[b, r:r + 1, :]
        for off, ref, r in ((OFF_G_ATTN_PRE, gi, 0), (OFF_G_ATTN_POST, go, 0), (OFF_G_MIX_A, go, 1), (OFF_G_MLP_PRE, gm, 0),
                            (OFF_G_MLP_POST, gd, 0)):
            pay[BL:BL + 1, off:off + D] = ref[r:r + 1, :]
        eye = lax.broadcasted_iota(jnp.int32, (8, LANES), 0) == lax.broadcasted_iota(jnp.int32, (8, LANES), 1)
        pay[BL:BL + 1, OFF_SINK:OFF_SINK + LANES] = jnp.sum(jnp.where(eye, dsink[...], 0.0), axis=0, keepdims=True)
        pay[BL:BL + 1, OFF_LOSS:OFF_LOSS + LANES] = gd[1:2, 0:LANES]
        me = _gather_small(pay, pbuf, s1, r1)
        pbuf[me] = pay[...]
        small = pbuf[0, BL:BL + 1, :]
        for i in range(1, NDEV):
            small = small + pbuf[i, BL:BL + 1, :]
        small_ref[...] = small
        for i in range(NDEV):
            dall[BL * i:BL * (i + 1), :] = pbuf[i, 0:BL, :]
        gb_ref[...] = jnp.sum(dall[...], axis=0, keepdims=True)
        cols = jnp.zeros((BL * NDEV, ADAW), f32)
        for s in range(NCHIP):
            cols = cols + jnp.where(chip == s, dall[:, ADAW * s:ADAW * (s + 1)], 0.0)
        gw_ref[...] = lax.dot_general(cond_ref[...], cols, (((0,), (0,)), ((), ())), preferred_element_type=f32,
                                      precision=lax.Precision.HIGHEST)

    vm = pl.BlockSpec(memory_space=pltpu.VMEM)
    return pl.pallas_call(
        body, name="small_allreduce", in_specs=[vm] * 10, out_specs=[vm] * 3,
        out_shape=[jax.ShapeDtypeStruct((D, ADAW), f32), jax.ShapeDtypeStruct((1, PAYW), f32), jax.ShapeDtypeStruct((1, PAYW), f32)],
        scratch_shapes=[pltpu.VMEM((4, PAYW), f32), pltpu.VMEM((NDEV, 4, PAYW), f32), pltpu.VMEM((BL * NDEV, PAYW), f32),
                        pltpu.SemaphoreType.DMA((NDEV - 1,)), pltpu.SemaphoreType.DMA((NDEV - 1,))],
        compiler_params=pltpu.CompilerParams(vmem_limit_bytes=VMEM_LIMIT),
    )(*accs, cond_all)


def _half(ref, c):
    r2 = ref.shape[0] // 2
    return ref.at[pl.ds(pl.multiple_of(c * r2, 16), r2), :]


def _wgather(shards):
    nt = len(shards)

    def body(*refs):
        ins, outs = refs[:nt], refs[nt:2 * nt]
        send, recv, lsem = refs[2 * nt:]
        x, y, c = _pos()
        chip = 2 * x + y
        sib = (x, y, 1 - c)
        chips = _other_chips(x, y)
        mine = [pltpu.make_async_copy(ins[t], outs[t].at[chip], lsem.at[t]) for t in range(nt)]
        for cp in mine:
            cp.start()
        first = [_rcopy(_half(ins[t], c), _half(outs[t].at[chip], c), send, recv, 6 * t + j, (px, py, c))
                 for t in range(nt) for j, (px, py) in enumerate(chips)]
        for cp in first:
            cp.start()
        passed = []
        for t in range(nt):
            for j, (px, py) in enumerate(chips):
                landed = _half(outs[t].at[2 * px + py], c)
                _rcopy(_half(ins[t], c), landed, send, recv, 6 * t + j, (px, py, c)).wait_recv()
                cp = _rcopy(landed, landed, send, recv, 6 * t + 3 + j, sib)
                cp.start()
                passed.append(cp)
        for t in range(nt):
            for j, (px, py) in enumerate(chips):
                theirs = _half(outs[t].at[2 * px + py], 1 - c)
                _rcopy(theirs, theirs, send, recv, 6 * t + 3 + j, sib).wait_recv()
        for cp in first + passed:
            cp.wait_send()
        for cp in mine:
            cp.wait()

    hbm = pl.BlockSpec(memory_space=pl.ANY)
    return pl.pallas_call(
        body, name="weight_gather", in_specs=[hbm] * nt, out_specs=[hbm] * nt,
        out_shape=[jax.ShapeDtypeStruct((NCHIP,) + s.shape, s.dtype) for s in shards],
        scratch_shapes=[pltpu.SemaphoreType.DMA((6 * nt,)), pltpu.SemaphoreType.DMA((6 * nt,)), pltpu.SemaphoreType.DMA((nt,))],
    )(*shards)


def _rs_pair(grads):
    nt = len(grads)

    def body(*refs):
        ins, outs = refs[:nt], refs[nt:2 * nt]
        send, recv = refs[2 * nt:]
        x, y, c = _pos()
        sib = (x, y, 1 - c)
        cps = []
        for t in range(nt):
            r2 = ins[t].shape[1] // 2
            src = ins[t].at[:, pl.ds(pl.multiple_of((1 - c) * r2, 8), r2), :]
            cps.append(_rcopy(src, outs[t], send, recv, t, sib))
        for cp in cps:
            cp.start()
        for cp in cps:
            cp.wait()

    hbm = pl.BlockSpec(memory_space=pl.ANY)
    return pl.pallas_call(
        body, name="grad_pair_exchange", in_specs=[hbm] * nt, out_specs=[hbm] * nt,
        out_shape=[jax.ShapeDtypeStruct((NCHIP, g.shape[1] // 2, g.shape[2]), f32) for g in grads],
        scratch_shapes=[pltpu.SemaphoreType.DMA((nt,)), pltpu.SemaphoreType.DMA((nt,))],
    )(*grads)


RS_ROWS = 128


def _pair_add(g, landed, c_arr, name):
    _, r2, cw = landed.shape
    nr = r2 // RS_ROWS

    def body(c_ref, g_ref, p_ref, o_ref):
        o_ref[...] = (g_ref[...] + p_ref[...]).astype(bf16)

    gs = pltpu.PrefetchScalarGridSpec(
        num_scalar_prefetch=1, grid=(NCHIP, nr),
        in_specs=[pl.BlockSpec((None, RS_ROWS, cw), lambda s, j, c: (s, c[0] * nr + j, 0)),
                  pl.BlockSpec((None, RS_ROWS, cw), lambda s, j, c: (s, j, 0))],
        out_specs=pl.BlockSpec((None, RS_ROWS, cw), lambda s, j, c: (s, j, 0)))
    return pl.pallas_call(body, name=name, grid_spec=gs, out_shape=jax.ShapeDtypeStruct((NCHIP, r2, cw), bf16),
                          compiler_params=_cp(("arbitrary", "arbitrary")))(c_arr, g, landed)


def _rs_ici(halves):
    nt = len(halves)

    def body(*refs):
        ins, outs = refs[:nt], refs[nt:2 * nt]
        send, recv = refs[2 * nt:]
        x, y, c = _pos()
        chips = _other_chips(x, y)
        cps = [_rcopy(ins[t].at[2 * px + py], outs[t].at[j], send, recv, 3 * t + j, (px, py, c))
               for t in range(nt) for j, (px, py) in enumerate(chips)]
        for cp in cps:
            cp.start()
        for cp in cps:
            cp.wait()

    hbm = pl.BlockSpec(memory_space=pl.ANY)
    return pl.pallas_call(
        body, name="grad_chip_exchange", in_specs=[hbm] * nt, out_specs=[hbm] * nt,
        out_shape=[jax.ShapeDtypeStruct((NCHIP - 1,) + h.shape[1:], bf16) for h in halves],
        scratch_shapes=[pltpu.SemaphoreType.DMA((3 * nt,)), pltpu.SemaphoreType.DMA((3 * nt,))],
    )(*halves)


def _chip_add(half, landed, chip_arr, name):
    _, r2, cw = half.shape
    nr = r2 // RS_ROWS

    def body(s_ref, h_ref, q_ref, o_ref):
        acc = h_ref[...].astype(f32)
        for j in range(NCHIP - 1):
            acc = acc + q_ref[j].astype(f32)
        o_ref[...] = acc

    gs = pltpu.PrefetchScalarGridSpec(
        num_scalar_prefetch=1, grid=(nr,),
        in_specs=[pl.BlockSpec((None, RS_ROWS, cw), lambda j, s: (s[0], j, 0)),
                  pl.BlockSpec((NCHIP - 1, RS_ROWS, cw), lambda j, s: (0, j, 0))],
        out_specs=pl.BlockSpec((RS_ROWS, cw), lambda j, s: (j, 0)))
    return pl.pallas_call(body, name=name, grid_spec=gs, out_shape=jax.ShapeDtypeStruct((r2, cw), f32),
                          compiler_params=_cp(("arbitrary",)))(chip_arr, half, landed)


def _ag_pair(halves):
    nt = len(halves)

    def body(*refs):
        ins, outs = refs[:nt], refs[nt:2 * nt]
        send, recv, lsem = refs[2 * nt:]
        x, y, c = _pos()
        sib = (x, y, 1 - c)
        mine, cps = [], []
        for t in range(nt):
            r2 = ins[t].shape[0]
            dst = outs[t].at[pl.ds(pl.multiple_of(c * r2, 8), r2), :]
            mine.append(pltpu.make_async_copy(ins[t], dst, lsem.at[t]))
            cps.append(_rcopy(ins[t], dst, send, recv, t, sib))
        for cp in mine + cps:
            cp.start()
        for t in range(nt):
            r2 = ins[t].shape[0]
            theirs = outs[t].at[pl.ds(pl.multiple_of((1 - c) * r2, 8), r2), :]
            _rcopy(ins[t], theirs, send, recv, t, sib).wait_recv()
        for cp in cps:
            cp.wait_send()
        for cp in mine:
            cp.wait()

    hbm = pl.BlockSpec(memory_space=pl.ANY)
    return pl.pallas_call(
        body, name="grad_pair_gather", in_specs=[hbm] * nt, out_specs=[hbm] * nt,
        out_shape=[jax.ShapeDtypeStruct((2 * h.shape[0], h.shape[1]), f32) for h in halves],
        scratch_shapes=[pltpu.SemaphoreType.DMA((nt,)), pltpu.SemaphoreType.DMA((nt,)), pltpu.SemaphoreType.DMA((nt,))],
    )(*halves)


def _adamw_math(w, g, m, v):
    m = B1 * m + (1.0 - B1) * g
    v = B2 * v + (1.0 - B2) * jnp.square(g)
    m_hat = m / (1.0 - B1 ** STEP)
    v_hat = v / (1.0 - B2 ** STEP)
    return -LR * (m_hat / (jnp.sqrt(v_hat) + AEPS) + WD * w), m, v


ADAM_ROWS = 128


def _adamw(w, g, m, v, name):
    r, cw = w.shape

    def body(w_ref, g_ref, m_ref, v_ref, d_ref, mo_ref, vo_ref):
        d_ref[...], mo_ref[...], vo_ref[...] = _adamw_math(w_ref[...], g_ref[...], m_ref[...], v_ref[...])

    spec = pl.BlockSpec((ADAM_ROWS, cw), lambda i: (i, 0))
    return pl.pallas_call(body, name=name, grid=(r // ADAM_ROWS,), in_specs=[spec] * 4, out_specs=[spec] * 3,
                          out_shape=[jax.ShapeDtypeStruct((r, cw), f32)] * 3, compiler_params=_cp(("arbitrary",)))(w, g, m, v)


SMALL = (("b_ada", None, PAYW), ("g_attn_pre", OFF_G_ATTN_PRE, D), ("g_attn_post", OFF_G_ATTN_POST, D), ("sink_a", OFF_SINK, 8),
         ("g_mix_a", OFF_G_MIX_A, AQ), ("g_mix_b", OFF_G_MIX_B, BW), ("g_mlp_pre", OFF_G_MLP_PRE, D), ("g_mlp_post", OFF_G_MLP_POST, D))


def _adamw_small(small, gb, params):
    n = len(SMALL)

    def body(*refs):
        small_ref, gb_ref = refs[:2]
        wmv = refs[2:2 + 3 * n]
        loss_ref = refs[2 + 3 * n]
        outs = refs[3 + 3 * n:]
        loss_ref[...] = small_ref[:, OFF_LOSS:OFF_LOSS + 1] * (0.5 / D)
        for i, (_, off, width) in enumerate(SMALL):
            g = gb_ref[...] if off is None else small_ref[:, off:off + width]
            w_ref, m_ref, v_ref = wmv[3 * i:3 * i + 3]
            outs[4 * i][...] = g
            outs[4 * i + 1][...], outs[4 * i + 2][...], outs[4 * i + 3][...] = _adamw_math(w_ref[...], g, m_ref[...], v_ref[...])

    vm = pl.BlockSpec(memory_space=pltpu.VMEM)
    out_shape = [jax.ShapeDtypeStruct((1, 1), f32)]
    for _, _, width in SMALL:
        out_shape += [jax.ShapeDtypeStruct((1, width), f32)] * 4
    flat = [a for wmv in params for a in wmv]
    res = pl.pallas_call(body, name="adamw_small", in_specs=[vm] * (2 + 3 * n), out_specs=[vm] * len(out_shape),
                         out_shape=out_shape)(small, gb, *flat)
    return res[0], {name: res[1 + 4 * i:5 + 4 * i] for i, (name, _, _) in enumerate(SMALL)}


def kernel(x, c, positions, w_ada, b_ada, g_attn_pre, g_attn_post, w_in, sink_a, g_mix_a, g_mix_b, w_out, g_mlp_pre, g_mlp_post, w_up, w_down, loss_target, m_w_ada, m_b_ada, m_g_attn_pre, m_g_attn_post, m_w_in, m_sink_a, m_g_mix_a, m_g_mix_b, m_w_out, m_g_mlp_pre, m_g_mlp_post, m_w_up, m_w_down, v_w_ada, v_b_ada, v_g_attn_pre, v_g_attn_post, v_w_in, v_sink_a, v_g_mix_a, v_g_mix_b, v_w_out, v_g_mlp_pre, v_g_mlp_post, v_w_up, v_w_down):
    given = dict(w_ada=w_ada, b_ada=b_ada, g_attn_pre=g_attn_pre, g_attn_post=g_attn_post, w_in=w_in, sink_a=sink_a, g_mix_a=g_mix_a,
                 g_mix_b=g_mix_b, w_out=w_out, g_mlp_pre=g_mlp_pre, g_mlp_post=g_mlp_post, w_up=w_up, w_down=w_down)
    moms = dict(w_ada=(m_w_ada, v_w_ada), b_ada=(m_b_ada, v_b_ada), g_attn_pre=(m_g_attn_pre, v_g_attn_pre),
                g_attn_post=(m_g_attn_post, v_g_attn_post), w_in=(m_w_in, v_w_in), sink_a=(m_sink_a, v_sink_a),
                g_mix_a=(m_g_mix_a, v_g_mix_a), g_mix_b=(m_g_mix_b, v_g_mix_b), w_out=(m_w_out, v_w_out),
                g_mlp_pre=(m_g_mlp_pre, v_g_mlp_pre), g_mlp_post=(m_g_mlp_post, v_g_mlp_post), w_up=(m_w_up, v_w_up),
                w_down=(m_w_down, v_w_down))
    order = ["w_ada", "b_ada", "g_attn_pre", "g_attn_post", "w_in", "sink_a", "g_mix_a", "g_mix_b", "w_out", "g_mlp_pre",
             "g_mlp_post", "w_up", "w_down"]
    xi, yi, ci = _pos()
    chip = 2 * xi + yi

    b_cols = lax.dynamic_slice(b_ada, (0, chip * ADAW), (1, ADAW))
    mod, cond_all = _ada_fwd(c, w_ada[0], b_cols)
    mod = mod.reshape(BL, NMOD, D)

    big = ("w_in", "w_out", "w_up", "w_down")
    win_g, wout_g, wup_g, wdn_g = _wgather([given[n][0].astype(bf16) for n in big])
    w_in_full = win_g.transpose(1, 0, 2).reshape(D, INW)

    grad_x, gw_in, gw_out, gw_up, gw_down, accs = _local_step(
        x, positions, mod, loss_target, w_in_full, wout_g.reshape(D, D), wup_g, wdn_g.reshape(DFF, D),
        g_attn_pre, g_attn_post, sink_a, g_mix_a, g_mix_b, g_mlp_pre, g_mlp_post)

    slabs = [gw_in.reshape(D, NCHIP, INW // NCHIP).transpose(1, 0, 2), gw_out.reshape(NCHIP, D // NCHIP, D), gw_up,
             gw_down.reshape(NCHIP, DFF // NCHIP, D)]
    landed = _rs_pair(slabs)
    c_arr = jnp.reshape(ci, (1,)).astype(jnp.int32)
    chip_arr = jnp.reshape(chip, (1,)).astype(jnp.int32)
    halves = [_pair_add(g, p, c_arr, "grad_pair_sum_" + n) for g, p, n in zip(slabs, landed, big)]
    arrived = _rs_ici(halves)
    reduced = [_chip_add(h, q, chip_arr, "grad_chip_sum_" + n) for h, q, n in zip(halves, arrived, big)]
    grads = dict(zip(big, _ag_pair(reduced)))

    grads["w_ada"], gb, small = _small_allreduce(accs, cond_all)

    out = {}
    for n in ("w_ada",) + big:
        d, m2, v2 = _adamw(given[n][0], grads[n], moms[n][0][0], moms[n][1][0], "adamw_" + n)
        out[n] = (grads[n][None], d[None], m2[None], v2[None])
    loss, res = _adamw_small(small, gb, [(given[n], moms[n][0], moms[n][1]) for n, _, _ in SMALL])
    for n, _, _ in SMALL:
        out[n] = tuple(res[n])
    return (loss.reshape(()), grad_x, *[out[n][0] for n in order], *[out[n][1] for n in order],
            *[out[n][2] for n in order], *[out[n][3] for n in order])
```

```python
import functools

import numpy as np
import jax
import jax.numpy as jnp
from jax import lax
from jax.experimental import pallas as pl
from jax.experimental.pallas import tpu as pltpu

f32 = jnp.float32
bf16 = jnp.bfloat16
MESH = pl.DeviceIdType.MESH

D = 1024
SEQ = 2048
BL = 2
HD = 64
AQ = 512
AKV = 128
BW = 512
INW = 2304
DFF = 4096
NMOD = 6
ROT = 16
THETA = 500000.0
EPS = 1e-6
NEG = -1e30
BLK = 128
TM = 256
NJ = SEQ // TM
LANES = 128
NCHIP = 4
NDEV = 8
VMEM_LIMIT = 56 << 20

LR, B1, B2, AEPS, WD, STEP = 0.001, 0.9, 0.999, 1e-08, 0.01, 10

OFF_G_ATTN_PRE, OFF_G_ATTN_POST, OFF_G_MIX_A, OFF_G_MIX_B = 0, 1024, 2048, 2560
OFF_G_MLP_PRE, OFF_G_MLP_POST, OFF_SINK, OFF_LOSS = 3072, 4096, 5120, 5248
PAYW = NMOD * D


def _cp(sem=None):
    return pltpu.CompilerParams(dimension_semantics=sem, vmem_limit_bytes=VMEM_LIMIT)


def _dot(a, b):
    return jnp.dot(a, b, preferred_element_type=f32)


def _dot_nt(a, b):
    return lax.dot_general(a, b, (((1,), (1,)), ((), ())), preferred_element_type=f32)


def _dot_tn(a, b):
    return lax.dot_general(a, b, (((0,), (0,)), ((), ())), preferred_element_type=f32)


def _rms(x):
    r = lax.rsqrt(jnp.mean(x * x, axis=-1, keepdims=True) + EPS)
    return x * r, r


def _rms_bwd(dy, y, r):
    return r * (dy - y * jnp.mean(dy * y, axis=-1, keepdims=True))


def _colsum(v):
    return jnp.sum(v, axis=0, keepdims=True)


def _rope(p, c, s1, s2):
    outs = []
    for c0 in range(0, p.shape[1], LANES):
        pc = p[:, c0:c0 + LANES]
        outs.append(pc * c + pltpu.roll(pc, LANES - ROT // 2, 1) * s1 + pltpu.roll(pc, ROT // 2, 1) * s2)
    return outs[0] if len(outs) == 1 else jnp.concatenate(outs, axis=1)


def _rope_t(g, c, s1, s2):
    outs = []
    for c0 in range(0, g.shape[1], LANES):
        gc = g[:, c0:c0 + LANES]
        outs.append(gc * c + pltpu.roll(gc * s1, ROT // 2, 1) + pltpu.roll(gc * s2, LANES - ROT // 2, 1))
    return outs[0] if len(outs) == 1 else jnp.concatenate(outs, axis=1)


def _perm_store(val, scr, out_ref, d):
    nc = val.shape[1] // LANES
    for c in range(nc):
        scr[c] = val[:, LANES * c:LANES * (c + 1)]
    for c in range(nc):
        for r in range(d):
            out_ref[r, :, LANES * c:LANES * (c + 1)] = scr[c, pl.ds(r, TM // d, stride=d), :].astype(out_ref.dtype)


def _perm_load(in_ref, scr, d):
    nc = in_ref.shape[-1] // LANES
    for c in range(nc):
        for r in range(d):
            scr[c, pl.ds(r, TM // d, stride=d), :] = in_ref[r, :, LANES * c:LANES * (c + 1)].astype(f32)
    return jnp.concatenate([scr[c] for c in range(nc)], axis=1)


def _tok(w, dtype=None):
    return pl.BlockSpec((None, TM, w), lambda b, j: (b, j, 0))


def _perm_spec(d, w):
    return pl.BlockSpec((None, d, TM // d, w), lambda b, j: (b, 0, j, 0))


def _full(shape):
    n = len(shape)
    return pl.BlockSpec(shape, lambda b, j: (0,) * n)


MOD_SPEC = pl.BlockSpec((None, NMOD, D), lambda b, j: (b, 0, 0))
ACCB_SPEC = pl.BlockSpec((None, 8, D), lambda b, j: (b, 0, 0))
ACCG_SPEC = pl.BlockSpec((8, D), lambda b, j: (0, 0))
ACC_SHAPES = [jax.ShapeDtypeStruct((BL, 8, D), f32), jax.ShapeDtypeStruct((8, D), f32)]


def _acc_init(accb_ref, accg_ref):
    b, j = pl.program_id(0), pl.program_id(1)

    @pl.when(j == 0)
    def _():
        accb_ref[...] = jnp.zeros_like(accb_ref)

    @pl.when((b == 0) & (j == 0))
    def _():
        accg_ref[...] = jnp.zeros_like(accg_ref)


def _rope_tables(pos_col, inv_lane):
    def body(p_ref, inv_ref, c_ref, s1_ref, s2_ref):
        ang = p_ref[...].astype(f32) * inv_ref[...]
        j = lax.broadcasted_iota(jnp.int32, (TM, LANES), 1) % HD
        cs, sn = jnp.cos(ang), jnp.sin(ang)
        c_ref[...] = jnp.where(j < ROT, cs, 1.0)
        s1_ref[...] = jnp.where(j < ROT // 2, -sn, 0.0)
        s2_ref[...] = jnp.where((j >= ROT // 2) & (j < ROT), sn, 0.0)

    n = BL * SEQ // TM
    return pl.pallas_call(
        body, name="rope_tables", grid=(n,),
        in_specs=[pl.BlockSpec((TM, 1), lambda i: (i, 0)), pl.BlockSpec((1, LANES), lambda i: (0, 0))],
        out_specs=[pl.BlockSpec((TM, LANES), lambda i: (i, 0))] * 3,
        out_shape=[jax.ShapeDtypeStruct((BL * SEQ, LANES), f32)] * 3,
    )(pos_col, inv_lane)


def _attn_in(x, mod, g_pre, w_in, tc, ts1, ts2):
    def body(x_ref, mod_ref, g_ref, w_ref, c_ref, s1_ref, s2_ref,
             h_ref, qa_ref, ka_ref, va_ref, q1_ref, k1_ref, v1_ref, q4_ref, k4_ref, v4_ref, q16_ref, k16_ref, v16_ref,
             scr):
        xn, _ = _rms(x_ref[...])
        h = (xn * g_ref[...]) * (1.0 + mod_ref[1:2, :]) + mod_ref[0:1, :]
        hb = h.astype(bf16)
        h_ref[...] = hb
        proj = _dot(hb, w_ref[...])
        c, s1, s2 = c_ref[...], s1_ref[...], s2_ref[...]
        o1, o2, o3, o4, o5 = AQ, AQ + AKV, AQ + 2 * AKV, AQ + 2 * AKV + BW, AQ + 2 * AKV + 2 * BW
        qa_ref[...] = (_rope(proj[:, :o1], c, s1, s2) * 0.125).astype(bf16)
        ka_ref[...] = _rope(proj[:, o1:o2], c, s1, s2).astype(bf16)
        va_ref[...] = proj[:, o2:o3].astype(bf16)
        qb = _rope(proj[:, o3:o4], c, s1, s2) * 0.125
        kb = _rope(proj[:, o4:o5], c, s1, s2)
        vb = proj[:, o5:]
        for val, r1, r4, r16 in ((qb, q1_ref, q4_ref, q16_ref), (kb, k1_ref, k4_ref, k16_ref), (vb, v1_ref, v4_ref, v16_ref)):
            r1[...] = val.astype(bf16)
            _perm_store(val, scr, r4, 4)
            _perm_store(val, scr, r16, 16)

    nat = lambda w: jax.ShapeDtypeStruct((BL, SEQ, w), bf16)
    p4 = jax.ShapeDtypeStruct((BL, 4, SEQ // 4, BW), bf16)
    p16 = jax.ShapeDtypeStruct((BL, 16, SEQ // 16, BW), bf16)
    return pl.pallas_call(
        body, name="attn_in", grid=(BL, NJ),
        in_specs=[_tok(D), MOD_SPEC, _full((1, D)), _full((D, INW)), _tok(LANES), _tok(LANES), _tok(LANES)],
        out_specs=[_tok(D), _tok(AQ), _tok(AKV), _tok(AKV)] + [_tok(BW)] * 3 + [_perm_spec(4, BW)] * 3 + [_perm_spec(16, BW)] * 3,
        out_shape=[nat(D), nat(AQ), nat(AKV), nat(AKV)] + [nat(BW)] * 3 + [p4] * 3 + [p16] * 3,
        scratch_shapes=[pltpu.VMEM((BW // LANES, TM, LANES), f32)],
        compiler_params=_cp(("arbitrary", "arbitrary")),
    )(x, mod, g_pre, w_in, tc, ts1, ts2)


def _pair_kv(ref, p, gqa, lo):
    if not gqa:
        return ref[:, LANES * p:LANES * (p + 1)]
    k = ref[...]
    kr = pltpu.roll(k, HD, 1)
    return jnp.where(lo, k, kr) if p < 2 else jnp.where(lo, kr, k)


def _valid_mask(blk_idx, nb, max_dist):
    if nb == 1:
        qi = lax.broadcasted_iota(jnp.int32, (BLK, BLK), 0)
        kj = lax.broadcasted_iota(jnp.int32, (BLK, BLK), 1)
        return kj <= qi
    qi = lax.broadcasted_iota(jnp.int32, (BLK, 2 * BLK), 0)
    col = lax.broadcasted_iota(jnp.int32, (BLK, 2 * BLK), 1)
    prev_ok = jnp.logical_and(jnp.logical_and(col < BLK, col >= qi + (BLK - max_dist)), blk_idx > 0)
    return jnp.logical_or(jnp.logical_and(col >= BLK, (col - BLK) <= qi), prev_ok)


def _attn_fwd(q, k, v, sink, *, max_dist, name):
    n, l, w = q.shape
    wk = k.shape[-1]
    nb = l // BLK
    gqa = wk != w
    has_sink = sink is not None

    def body(*refs):
        if has_sink:
            sink_ref, refs = refs[0], refs[1:]
        if nb > 1:
            q_ref, kc_ref, kp_ref, vc_ref, vp_ref, o_ref, lse_ref = refs
        else:
            q_ref, kc_ref, vc_ref, o_ref, lse_ref = refs
        i = pl.program_id(1)
        lo = lax.broadcasted_iota(jnp.int32, (BLK, LANES), 1) < HD
        valid = _valid_mask(i, nb, max_dist)
        for p in range(w // LANES):
            qpair = q_ref[:, LANES * p:LANES * (p + 1)]
            kcat, vcat = _pair_kv(kc_ref, p, gqa, lo), _pair_kv(vc_ref, p, gqa, lo)
            if nb > 1:
                kcat = jnp.concatenate([_pair_kv(kp_ref, p, gqa, lo), kcat], axis=0)
                vcat = jnp.concatenate([_pair_kv(vp_ref, p, gqa, lo), vcat], axis=0)
            lov = lax.broadcasted_iota(jnp.int32, vcat.shape, 1) < HD
            o_pair = jnp.zeros((BLK, LANES), f32)
            lse_pair = jnp.zeros((BLK, LANES), f32)
            for hh in range(2):
                msk = lo if hh == 0 else jnp.logical_not(lo)
                mskv = lov if hh == 0 else jnp.logical_not(lov)
                s = _dot_nt(jnp.where(msk, qpair, jnp.zeros_like(qpair)), kcat)
                s = jnp.where(valid, s, NEG)
                m = jnp.max(s, axis=-1, keepdims=True)
                if has_sink:
                    sk = sink_ref[0, 2 * p + hh]
                    m = jnp.maximum(m, sk)
                e = jnp.exp(s - m)
                den = jnp.sum(e, axis=-1, keepdims=True)
                if has_sink:
                    den = den + jnp.exp(sk - m)
                pn = (e * (1.0 / den)).astype(bf16)
                o_pair = o_pair + _dot(pn, jnp.where(mskv, vcat, jnp.zeros_like(vcat)))
                lse_pair = jnp.where(msk, m + jnp.log(den), lse_pair)
            o_ref[:, LANES * p:LANES * (p + 1)] = o_pair
            lse_ref[:, LANES * p:LANES * (p + 1)] = lse_pair

    cur = lambda ww: pl.BlockSpec((None, BLK, ww), lambda a, i: (a, i, 0))
    prev = lambda ww: pl.BlockSpec((None, BLK, ww), lambda a, i: (a, jnp.maximum(i - 1, 0), 0))
    in_specs = [cur(w), cur(wk)] + ([prev(wk)] if nb > 1 else []) + [cur(wk)] + ([prev(wk)] if nb > 1 else [])
    args = [q, k] + ([k] if nb > 1 else []) + [v] + ([v] if nb > 1 else [])
    if has_sink:
        in_specs = [pl.BlockSpec(memory_space=pltpu.SMEM)] + in_specs
        args = [sink] + args
    return pl.pallas_call(
        body, name=name, grid=(n, nb), in_specs=in_specs,
        out_specs=[cur(w), cur(w)], out_shape=[jax.ShapeDtypeStruct((n, l, w), f32)] * 2,
        compiler_params=_cp(("arbitrary", "arbitrary")),
    )(*args)


def _attn_bwd(q, k, v, do, o, lse, sink, *, max_dist, name):
    n, l, w = q.shape
    wk = k.shape[-1]
    nb = l // BLK
    gqa = wk != w
    has_sink = sink is not None

    def body(*refs):
        if has_sink:
            sink_ref, refs = refs[0], refs[1:]
        if nb > 1:
            q_ref, kc_ref, kp_ref, vc_ref, vp_ref, do_ref, o_ref, lse_ref = refs[:8]
            rest = refs[8:]
        else:
            q_ref, kc_ref, vc_ref, do_ref, o_ref, lse_ref = refs[:6]
            rest = refs[6:]
        if has_sink:
            dq_ref, dk_ref, dv_ref, dsink_ref = rest[:4]
            rest = rest[4:]
        else:
            dq_ref, dk_ref, dv_ref = rest[:3]
            rest = rest[3:]
        step = pl.program_id(1)
        blk_idx = nb - 1 - step
        if nb > 1:
            ck, cv = rest

            @pl.when(step == 0)
            def _():
                ck[...] = jnp.zeros_like(ck)
                cv[...] = jnp.zeros_like(cv)

        if has_sink:
            @pl.when((pl.program_id(0) == 0) & (step == 0))
            def _():
                dsink_ref[...] = jnp.zeros_like(dsink_ref)

        lo = lax.broadcasted_iota(jnp.int32, (BLK, LANES), 1) < HD
        valid = _valid_mask(blk_idx, nb, max_dist)
        rows = 2 * BLK if nb > 1 else BLK
        gk = [jnp.zeros((rows, LANES), f32), jnp.zeros((rows, LANES), f32)]
        gv = [jnp.zeros((rows, LANES), f32), jnp.zeros((rows, LANES), f32)]
        for p in range(w // LANES):
            sl = slice(LANES * p, LANES * (p + 1))
            qpair = q_ref[:, sl]
            dopair = do_ref[:, sl]
            prod = dopair * o_ref[:, sl]
            lsepair = lse_ref[:, sl]
            kcat, vcat = _pair_kv(kc_ref, p, gqa, lo), _pair_kv(vc_ref, p, gqa, lo)
            if nb > 1:
                kcat = jnp.concatenate([_pair_kv(kp_ref, p, gqa, lo), kcat], axis=0)
                vcat = jnp.concatenate([_pair_kv(vp_ref, p, gqa, lo), vcat], axis=0)
            lov = lax.broadcasted_iota(jnp.int32, kcat.shape, 1) < HD
            dq_pair = jnp.zeros((BLK, LANES), f32)
            dk_pair = jnp.zeros((rows, LANES), f32)
            dv_pair = jnp.zeros((rows, LANES), f32)
            for hh in range(2):
                msk = lo if hh == 0 else jnp.logical_not(lo)
                mskv = lov if hh == 0 else jnp.logical_not(lov)
                qm = jnp.where(msk, qpair, jnp.zeros_like(qpair))
                dom = jnp.where(msk, dopair, 0.0).astype(bf16)
                delta = jnp.sum(jnp.where(msk, prod, 0.0), axis=-1, keepdims=True)
                lse_h = lsepair[:, HD * hh:HD * hh + 1]
                s = jnp.where(valid, _dot_nt(qm, kcat), NEG)
                pr = jnp.exp(s - lse_h)
                dp = _dot_nt(dom, vcat)
                ds = (pr * (dp - delta)).astype(bf16)
                prb = pr.astype(bf16)
                dq_pair = dq_pair + _dot(ds, jnp.where(mskv, kcat, jnp.zeros_like(kcat)))
                dk_pair = dk_pair + _dot_tn(ds, qm)
                dv_pair = dv_pair + _dot_tn(prb, dom)
                if has_sink:
                    h = 2 * p + hh
                    dsk = -jnp.sum(jnp.exp(sink_ref[0, h] - lse_h) * delta, keepdims=True)
                    dsink_ref[h:h + 1, :] += jnp.broadcast_to(dsk, (1, LANES))
            dq_ref[:, sl] = dq_pair
            if gqa:
                gk[p // 2] = gk[p // 2] + dk_pair
                gv[p // 2] = gv[p // 2] + dv_pair
            elif nb > 1:
                dk_ref[:, sl] = dk_pair[BLK:] + ck[:, sl]
                dv_ref[:, sl] = dv_pair[BLK:] + cv[:, sl]
                ck[:, sl] = dk_pair[:BLK]
                cv[:, sl] = dv_pair[:BLK]
            else:
                dk_ref[:, sl] = dk_pair
                dv_ref[:, sl] = dv_pair
        if gqa:
            lor = lax.broadcasted_iota(jnp.int32, (rows, LANES), 1) < HD
            fold = lambda g: jnp.where(lor, g[0] + pltpu.roll(g[0], HD, 1), g[1] + pltpu.roll(g[1], HD, 1))
            dk_full, dv_full = fold(gk), fold(gv)
            dk_ref[...] = dk_full[BLK:] + ck[...]
            dv_ref[...] = dv_full[BLK:] + cv[...]
            ck[...] = dk_full[:BLK]
            cv[...] = dv_full[:BLK]

    cur = lambda ww: pl.BlockSpec((None, BLK, ww), lambda a, i: (a, nb - 1 - i, 0))
    prev = lambda ww: pl.BlockSpec((None, BLK, ww), lambda a, i: (a, jnp.maximum(nb - 2 - i, 0), 0))
    in_specs = [cur(w), cur(wk)] + ([prev(wk)] if nb > 1 else []) + [cur(wk)] + ([prev(wk)] if nb > 1 else []) + [cur(w)] * 3
    args = [q, k] + ([k] if nb > 1 else []) + [v] + ([v] if nb > 1 else []) + [do, o, lse]
    out_specs = [cur(w), cur(wk), cur(wk)]
    out_shape = [jax.ShapeDtypeStruct((n, l, w), f32), jax.ShapeDtypeStruct((n, l, wk), f32), jax.ShapeDtypeStruct((n, l, wk), f32)]
    if has_sink:
        in_specs = [pl.BlockSpec(memory_space=pltpu.SMEM)] + in_specs
        args = [sink] + args
        out_specs.append(pl.BlockSpec((8, LANES), lambda a, i: (0, 0)))
        out_shape.append(jax.ShapeDtypeStruct((8, LANES), f32))
    scratch = [pltpu.VMEM((BLK, wk), f32), pltpu.VMEM((BLK, wk), f32)] if nb > 1 else []
    return pl.pallas_call(
        body, name=name, grid=(n, nb), in_specs=in_specs, out_specs=out_specs, out_shape=out_shape,
        scratch_shapes=scratch, compiler_params=_cp(("arbitrary", "arbitrary")),
    )(*args)


def _mix_out(oa, o1, l1, o4, l4, o16, l16, g_mix_a, g_mix_b, w_out, x, mod, g_post):
    def body(oa_ref, o1_ref, l1_ref, o4_ref, l4_ref, o16_ref, l16_ref, ga_ref, gb_ref, w_ref, x_ref, mod_ref, gp_ref,
             x1_ref, y_ref, mixed_ref, ob_ref, ob4_ref, ob16_ref, w1_ref, w4_ref, w16_ref, scr):
        o4v = _perm_load(o4_ref, scr, 4)
        l4v = _perm_load(l4_ref, scr, 4)
        o16v = _perm_load(o16_ref, scr, 16)
        l16v = _perm_load(l16_ref, scr, 16)
        l1v = l1_ref[...]
        m = jnp.maximum(jnp.maximum(l1v, l4v), l16v)
        e1, e4, e16 = jnp.exp(l1v - m), jnp.exp(l4v - m), jnp.exp(l16v - m)
        z = e1 + e4 + e16
        w1, w4, w16 = e1 / z, e4 / z, e16 / z
        ob = w1 * o1_ref[...] + w4 * o4v + w16 * o16v
        w1_ref[...] = w1
        w4_ref[...] = w4
        w16_ref[...] = w16
        ob_ref[...] = ob
        _perm_store(ob, scr, ob4_ref, 4)
        _perm_store(ob, scr, ob16_ref, 16)
        oan, _ = _rms(oa_ref[...])
        obn, _ = _rms(ob)
        mixed = jnp.concatenate([oan * ga_ref[...], obn * gb_ref[...]], axis=1).astype(bf16)
        mixed_ref[...] = mixed
        y = _dot(mixed, w_ref[...])
        y_ref[...] = y
        yn, _ = _rms(y)
        x1_ref[...] = x_ref[...] + mod_ref[2:3, :] * (yn * gp_ref[...])

    nat = lambda w, dt: jax.ShapeDtypeStruct((BL, SEQ, w), dt)
    return pl.pallas_call(
        body, name="mix_out", grid=(BL, NJ),
        in_specs=[_tok(AQ), _tok(BW), _tok(BW), _perm_spec(4, BW), _perm_spec(4, BW), _perm_spec(16, BW), _perm_spec(16, BW),
                  _full((1, AQ)), _full((1, BW)), _full((D, D)), _tok(D), MOD_SPEC, _full((1, D))],
        out_specs=[_tok(D), _tok(D), _tok(D), _tok(BW), _perm_spec(4, BW), _perm_spec(16, BW), _tok(BW), _tok(BW), _tok(BW)],
        out_shape=[nat(D, f32), nat(D, f32), nat(D, bf16), nat(BW, f32),
                   jax.ShapeDtypeStruct((BL, 4, SEQ // 4, BW), f32), jax.ShapeDtypeStruct((BL, 16, SEQ // 16, BW), f32),
                   nat(BW, f32), nat(BW, f32), nat(BW, f32)],
        scratch_shapes=[pltpu.VMEM((BW // LANES, TM, LANES), f32)],
        compiler_params=_cp(("arbitrary", "arbitrary")),
    )(oa, o1, l1, o4, l4, o16, l16, g_mix_a, g_mix_b, w_out, x, mod, g_post)


def _mlp_up(x1, mod, g_pre, w_up):
    def body(x_ref, mod_ref, g_ref, w_ref, h_ref, u_ref, a_ref):
        xn, _ = _rms(x_ref[...])
        h = (xn * g_ref[...]) * (1.0 + mod_ref[4:5, :]) + mod_ref[3:4, :]
        hb = h.astype(bf16)
        h_ref[...] = hb
        for s in range(NCHIP):
            u = _dot(hb, w_ref[s])
            u_ref[:, D * s:D * (s + 1)] = u.astype(bf16)
            a_ref[:, D * s:D * (s + 1)] = jnp.square(jnp.maximum(u, 0.0)).astype(bf16)

    nat = lambda w: jax.ShapeDtypeStruct((BL, SEQ, w), bf16)
    return pl.pallas_call(
        body, name="mlp_up", grid=(BL, NJ),
        in_specs=[_tok(D), MOD_SPEC, _full((1, D)), _full((NCHIP, D, D))],
        out_specs=[_tok(D), _tok(DFF), _tok(DFF)], out_shape=[nat(D), nat(DFF), nat(DFF)],
        compiler_params=_cp(("arbitrary", "arbitrary")),
    )(x1, mod, g_pre, w_up)


def _mlp_down(a, w_down, x1, target, mod, g_post):
    def body(a_ref, w_ref, x_ref, t_ref, mod_ref, g_ref, gx_ref, dy_ref, accb_ref, accg_ref):
        _acc_init(accb_ref, accg_ref)
        y2 = _dot(a_ref[...], w_ref[...])
        yn, r = _rms(y2)
        g = g_ref[...]
        gt = mod_ref[5:6, :]
        n2 = yn * g
        err = x_ref[...] + gt * n2 - t_ref[...]
        gout = err * (1.0 / D)
        gx_ref[...] = gout
        dn2 = gout * gt
        dy_ref[...] = _rms_bwd(dn2 * g, yn, r).astype(bf16)
        accb_ref[0:1, :] += _colsum(gout * n2)
        accg_ref[0:1, :] += _colsum(dn2 * yn)
        accg_ref[1:2, :] += jnp.broadcast_to(jnp.sum(err * err, keepdims=True), (1, D))

    return pl.pallas_call(
        body, name="mlp_down", grid=(BL, NJ),
        in_specs=[_tok(DFF), _full((DFF, D)), _tok(D), _tok(D), MOD_SPEC, _full((1, D))],
        out_specs=[_tok(D), _tok(D), ACCB_SPEC, ACCG_SPEC],
        out_shape=[jax.ShapeDtypeStruct((BL, SEQ, D), f32), jax.ShapeDtypeStruct((BL, SEQ, D), bf16)] + ACC_SHAPES,
        compiler_params=_cp(("arbitrary", "arbitrary")),
    )(a, w_down, x1, target, mod, g_post)


def _mlp_bwd(dy2, u, w_down, w_up, x1, gx, mod, g_pre):
    def body(dy_ref, u_ref, wd_hbm, wu_hbm, x_ref, gx_ref, mod_ref, g_ref, du_ref, gx1_ref, accb_ref, accg_ref, wd, wu, sem):
        _acc_init(accb_ref, accg_ref)

        @pl.when((pl.program_id(0) == 0) & (pl.program_id(1) == 0))
        def _():
            c1 = pltpu.make_async_copy(wd_hbm, wd, sem.at[0])
            c2 = pltpu.make_async_copy(wu_hbm, wu, sem.at[1])
            c1.start()
            c2.start()
            c1.wait()
            c2.wait()

        dy = dy_ref[...]
        dh = jnp.zeros((TM, D), f32)
        for s in range(NCHIP):
            sl = slice(D * s, D * (s + 1))
            da = _dot_nt(dy, wd[sl, :])
            du = (da * (2.0 * jnp.maximum(u_ref[:, sl].astype(f32), 0.0))).astype(bf16)
            du_ref[:, sl] = du
            dh = dh + _dot_nt(du, wu[s])
        xn, r = _rms(x_ref[...])
        g = g_ref[...]
        n = xn * g
        dn = dh * (1.0 + mod_ref[4:5, :])
        gx1_ref[...] = gx_ref[...] + _rms_bwd(dn * g, xn, r)
        accb_ref[0:1, :] += _colsum(dh * n)
        accb_ref[1:2, :] += _colsum(dh)
        accg_ref[0:1, :] += _colsum(dn * xn)

    anyspec = pl.BlockSpec(memory_space=pl.ANY)
    return pl.pallas_call(
        body, name="mlp_bwd", grid=(BL, NJ),
        in_specs=[_tok(D), _tok(DFF), anyspec, anyspec, _tok(D), _tok(D), MOD_SPEC, _full((1, D))],
        out_specs=[_tok(DFF), _tok(D), ACCB_SPEC, ACCG_SPEC],
        out_shape=[jax.ShapeDtypeStruct((BL, SEQ, DFF), bf16), jax.ShapeDtypeStruct((BL, SEQ, D), f32)] + ACC_SHAPES,
        scratch_shapes=[pltpu.VMEM((DFF, D), bf16), pltpu.VMEM((NCHIP, D, D), bf16), pltpu.SemaphoreType.DMA((2,))],
        compiler_params=_cp(("arbitrary", "arbitrary")),
    )(dy2, u, w_down, w_up, x1, gx, mod, g_pre)


def _matmul_tn(a, b, *, tn, col_blocked, name):
    t, m = a.shape
    n = b.shape[1]
    tmm = min(m, 1024)
    tk = 512
    nk = t // tk

    def body(a_ref, b_ref, o_ref):
        @pl.when(pl.program_id(2) == 0)
        def _():
            o_ref[...] = jnp.zeros_like(o_ref)

        o_ref[...] += _dot_tn(a_ref[...], b_ref[...])

    if col_blocked:
        out_spec = pl.BlockSpec((None, tmm, tn), lambda i, j, k: (j, i, 0))
        out_shape = jax.ShapeDtypeStruct((n // tn, m, tn), f32)
    else:
        out_spec = pl.BlockSpec((tmm, tn), lambda i, j, k: (i, j))
        out_shape = jax.ShapeDtypeStruct((m, n), f32)
    return pl.pallas_call(
        body, name=name, grid=(m // tmm, n // tn, nk),
        in_specs=[pl.BlockSpec((tk, tmm), lambda i, j, k: (k, i)), pl.BlockSpec((tk, tn), lambda i, j, k: (k, j))],
        out_specs=out_spec, out_shape=out_shape,
        compiler_params=_cp(("arbitrary", "arbitrary", "arbitrary")),
    )(a, b)


def _attn_out_bwd(gx1, y, mod, g_post, w_out, oa, ob, g_mix_a, g_mix_b, w1, w4, w16):
    def body(gx_ref, y_ref, mod_ref, gp_ref, w_ref, oa_ref, ob_ref, ga_ref, gb_ref, w1_ref, w4_ref, w16_ref,
             dy_ref, doa_ref, do1_ref, do4_ref, do16_ref, accb_ref, accg_ref, scr):
        _acc_init(accb_ref, accg_ref)
        gx1v = gx_ref[...]
        yn, ry = _rms(y_ref[...])
        gp = gp_ref[...]
        gt = mod_ref[2:3, :]
        dn1 = gx1v * gt
        dy = _rms_bwd(dn1 * gp, yn, ry).astype(bf16)
        dy_ref[...] = dy
        dmixed = _dot_nt(dy, w_ref[...])
        dma, dmb = dmixed[:, :AQ], dmixed[:, AQ:]
        oan, ra = _rms(oa_ref[...])
        obn, rb = _rms(ob_ref[...])
        doa_ref[...] = _rms_bwd(dma * ga_ref[...], oan, ra)
        dob = _rms_bwd(dmb * gb_ref[...], obn, rb)
        do1_ref[...] = w1_ref[...] * dob
        _perm_store(w4_ref[...] * dob, scr, do4_ref, 4)
        _perm_store(w16_ref[...] * dob, scr, do16_ref, 16)
        accb_ref[0:1, :] += _colsum(gx1v * (yn * gp))
        accg_ref[0:1, :] += _colsum(dn1 * yn)
        accg_ref[1:2, :] += jnp.concatenate([_colsum(dma * oan), _colsum(dmb * obn)], axis=1)

    nat = lambda w, dt: jax.ShapeDtypeStruct((BL, SEQ, w), dt)
    return pl.pallas_call(
        body, name="attn_out_bwd", grid=(BL, NJ),
        in_specs=[_tok(D), _tok(D), MOD_SPEC, _full((1, D)), _full((D, D)), _tok(AQ), _tok(BW), _full((1, AQ)), _full((1, BW)),
                  _tok(BW), _tok(BW), _tok(BW)],
        out_specs=[_tok(D), _tok(AQ), _tok(BW), _perm_spec(4, BW), _perm_spec(16, BW), ACCB_SPEC, ACCG_SPEC],
        out_shape=[nat(D, bf16), nat(AQ, f32), nat(BW, f32), jax.ShapeDtypeStruct((BL, 4, SEQ // 4, BW), f32),
                   jax.ShapeDtypeStruct((BL, 16, SEQ // 16, BW), f32)] + ACC_SHAPES,
        scratch_shapes=[pltpu.VMEM((BW // LANES, TM, LANES), f32)],
        compiler_params=_cp(("arbitrary", "arbitrary")),
    )(gx1, y, mod, g_post, w_out, oa, ob, g_mix_a, g_mix_b, w1, w4, w16)


def _attn_in_bwd(dqa, dka, dva, d1, d4, d16, tc, ts1, ts2, w_in, x, gx1, mod, g_pre):
    def body(dqa_ref, dka_ref, dva_ref, dq1_ref, dk1_ref, dv1_ref, dq4_ref, dk4_ref, dv4_ref, dq16_ref, dk16_ref, dv16_ref,
             c_ref, s1_ref, s2_ref, w_ref, x_ref, gx_ref, mod_ref, g_ref, dproj_ref, dx_ref, accb_ref, accg_ref, scr):
        _acc_init(accb_ref, accg_ref)
        c, s1, s2 = c_ref[...], s1_ref[...], s2_ref[...]
        tot = lambda r1, r4, r16: r1[...] + _perm_load(r4, scr, 4) + _perm_load(r16, scr, 16)
        dqb = tot(dq1_ref, dq4_ref, dq16_ref)
        dkb = tot(dk1_ref, dk4_ref, dk16_ref)
        dvb = tot(dv1_ref, dv4_ref, dv16_ref)
        dproj = jnp.concatenate([
            _rope_t(dqa_ref[...], c, s1, s2) * 0.125, _rope_t(dka_ref[...], c, s1, s2), dva_ref[...],
            _rope_t(dqb, c, s1, s2) * 0.125, _rope_t(dkb, c, s1, s2), dvb], axis=1).astype(bf16)
        dproj_ref[...] = dproj
        dh = _dot_nt(dproj, w_ref[...])
        xn, r = _rms(x_ref[...])
        g = g_ref[...]
        dn = dh * (1.0 + mod_ref[1:2, :])
        dx_ref[...] = gx_ref[...] + _rms_bwd(dn * g, xn, r)
        accb_ref[0:1, :] += _colsum(dh * (xn * g))
        accb_ref[1:2, :] += _colsum(dh)
        accg_ref[0:1, :] += _colsum(dn * xn)

    return pl.pallas_call(
        body, name="attn_in_bwd", grid=(BL, NJ),
        in_specs=[_tok(AQ), _tok(AKV), _tok(AKV)] + [_tok(BW)] * 3 + [_perm_spec(4, BW)] * 3 + [_perm_spec(16, BW)] * 3
                 + [_tok(LANES)] * 3 + [_full((D, INW)), _tok(D), _tok(D), MOD_SPEC, _full((1, D))],
        out_specs=[_tok(INW), _tok(D), ACCB_SPEC, ACCG_SPEC],
        out_shape=[jax.ShapeDtypeStruct((BL, SEQ, INW), bf16), jax.ShapeDtypeStruct((BL, SEQ, D), f32)] + ACC_SHAPES,
        scratch_shapes=[pltpu.VMEM((BW // LANES, TM, LANES), f32)],
        compiler_params=_cp(("arbitrary", "arbitrary")),
    )(dqa, dka, dva, *d1, *d4, *d16, tc, ts1, ts2, w_in, x, gx1, mod, g_pre)


def _local_step(x, positions, mod, target, w_in, later_weights, grad_ready, g_attn_pre, g_attn_post, sink_a, g_mix_a, g_mix_b,
                g_mlp_pre, g_mlp_post):
    inv = np.float32(THETA) ** (-np.arange(0, ROT, 2, dtype=np.float32) / np.float32(ROT))
    lane = np.arange(LANES) % HD
    inv_lane = jnp.asarray(np.where(lane < ROT, inv[lane % (ROT // 2)], 0.0).astype(np.float32)[None, :])
    tabs = _rope_tables(positions.reshape(BL * SEQ, 1), inv_lane)
    tc, ts1, ts2 = [t.reshape(BL, SEQ, LANES) for t in tabs]

    (h, qa, ka, va, q1, k1, v1, q4, k4, v4, q16, k16, v16) = _attn_in(x, mod, g_attn_pre, w_in, tc, ts1, ts2)
    seqs = lambda t: t.reshape(t.shape[0] * t.shape[1], t.shape[2], t.shape[3])
    q4, k4, v4, q16, k16, v16 = [seqs(t) for t in (q4, k4, v4, q16, k16, v16)]
    oa, la = _attn_fwd(qa, ka, va, sink_a, max_dist=BLK - 1, name="attn_a_fwd")
    o1, l1 = _attn_fwd(q1, k1, v1, None, max_dist=BLK, name="attn_b1_fwd")
    o4, l4 = _attn_fwd(q4, k4, v4, None, max_dist=BLK, name="attn_b4_fwd")
    o16, l16 = _attn_fwd(q16, k16, v16, None, max_dist=BLK, name="attn_b16_fwd")
    b4 = lambda t: t.reshape(BL, 4, SEQ // 4, BW)
    b16 = lambda t: t.reshape(BL, 16, SEQ // 16, BW)
    w_out, w_up, w_down = later_weights((oa, o1, o4, o16))
    x1, y, mixed, ob, ob4, ob16, w1, w4, w16 = _mix_out(oa, o1, l1, b4(o4), b4(l4), b16(o16), b16(l16), g_mix_a, g_mix_b,
                                                        w_out, x, mod, g_attn_post)
    h2, u, a = _mlp_up(x1, mod, g_mlp_pre, w_up)
    gx, dy2, accb_d, accg_d = _mlp_down(a, w_down, x1, target, mod, g_mlp_post)

    flat = lambda t: t.reshape(BL * SEQ, t.shape[-1])
    mod = mod + grad_ready("w_down", _matmul_tn(flat(a), flat(dy2), tn=D, col_blocked=False, name="grad_w_down"))
    du, gx1, accb_m, accg_m = _mlp_bwd(dy2, u, w_down, w_up, x1, gx, mod, g_mlp_pre)
    mod = mod + grad_ready("w_up", _matmul_tn(flat(h2), flat(du), tn=D, col_blocked=True, name="grad_w_up"))

    dy, doa, do1, do4, do16, accb_o, accg_o = _attn_out_bwd(gx1, y, mod, g_attn_post, w_out, oa, ob, g_mix_a, g_mix_b, w1, w4, w16)
    gw_out = _matmul_tn(flat(mixed), flat(dy), tn=D, col_blocked=False, name="grad_w_out")
    dqa, dka, dva, dsink = _attn_bwd(qa, ka, va, doa, oa, la, sink_a, max_dist=BLK - 1, name="attn_a_bwd")
    d1 = _attn_bwd(q1, k1, v1, do1, ob, l1, None, max_dist=BLK, name="attn_b1_bwd")
    d4 = _attn_bwd(q4, k4, v4, seqs(do4), seqs(ob4), l4, None, max_dist=BLK, name="attn_b4_bwd")
    d16 = _attn_bwd(q16, k16, v16, seqs(do16), seqs(ob16), l16, None, max_dist=BLK, name="attn_b16_bwd")
    dproj, grad_x, accb_i, accg_i = _attn_in_bwd(dqa, dka, dva, d1, [b4(t) for t in d4], [b16(t) for t in d16],
                                                 tc, ts1, ts2, w_in, x, gx1, mod, g_attn_pre)
    gw_in = _matmul_tn(flat(h), flat(dproj), tn=INW, col_blocked=False, name="grad_w_in")
    dsink = dsink + grad_ready("w_in_w_out", (gw_in, gw_out))

    return grad_x, (accb_i, accb_o, accb_m, accb_d, accg_i, accg_o, accg_m, accg_d, dsink)


ADAW = NMOD * D // NCHIP


def _pos():
    return lax.axis_index("x"), lax.axis_index("y"), lax.axis_index("c")


def _flip(v, bit):
    return 1 - v if bit else v


def _all_peers(x, y, c):
    return [(_flip(x, k >> 2 & 1), _flip(y, k >> 1 & 1), _flip(c, k & 1)) for k in range(1, NDEV)]


def _other_chips(x, y):
    return [(1 - x, y), (x, 1 - y), (1 - x, 1 - y)]


def _rcopy(src, dst, send, recv, k, dev):
    return pltpu.make_async_remote_copy(src_ref=src, dst_ref=dst, send_sem=send.at[k], recv_sem=recv.at[k],
                                        device_id=dev, device_id_type=MESH)


def _gather_small(src, buf, send, recv):
    x, y, c = _pos()
    me = 4 * x + 2 * y + c
    peers = _all_peers(x, y, c)
    sends = [_rcopy(src, buf.at[me], send, recv, k, p) for k, p in enumerate(peers)]
    for cp in sends:
        cp.start()
    for k, (px, py, pc) in enumerate(peers):
        _rcopy(src, buf.at[4 * px + 2 * py + pc], send, recv, k, (px, py, pc)).wait_recv()
    for cp in sends:
        cp.wait_send()
    return me


def _ada_fwd(c_in, w_ada, b_cols):
    def body(c_ref, w_ref, b_ref, mod_ref, cond_ref, cbuf, mbuf, s1, r1, s2, r2):
        x, y, c = _pos()
        chip = 2 * x + y
        me = _gather_small(c_ref, cbuf, s1, r1)
        cbuf[me] = c_ref[...]
        for i in range(NDEV):
            cond_ref[BL * i:BL * (i + 1), :] = cbuf[i]
        call = cond_ref[...]
        cond = call / (1.0 + jnp.exp(-call))
        cond_ref[...] = cond
        mbuf[chip] = jnp.dot(cond, w_ref[...], preferred_element_type=f32, precision=lax.Precision.HIGHEST) + b_ref[...]
        chips = _other_chips(x, y)
        sends = [_rcopy(mbuf.at[chip], mbuf.at[chip], s2, r2, j, (px, py, c)) for j, (px, py) in enumerate(chips)]
        for cp in sends:
            cp.start()
        for j, (px, py) in enumerate(chips):
            _rcopy(mbuf.at[chip], mbuf.at[2 * px + py], s2, r2, j, (px, py, c)).wait_recv()
        for cp in sends:
            cp.wait_send()
        row = lax.broadcasted_iota(jnp.int32, (BL * NDEV, ADAW), 0)
        for s in range(NCHIP):
            slab = mbuf[s]
            for j in range(BL):
                mod_ref[j:j + 1, ADAW * s:ADAW * (s + 1)] = jnp.sum(jnp.where(row == BL * me + j, slab, 0.0), axis=0, keepdims=True)

    vm = pl.BlockSpec(memory_space=pltpu.VMEM)
    return pl.pallas_call(
        body, name="ada_fwd", in_specs=[vm, vm, vm], out_specs=[vm, vm],
        out_shape=[jax.ShapeDtypeStruct((BL, NMOD * D), f32), jax.ShapeDtypeStruct((BL * NDEV, D), f32)],
        scratch_shapes=[pltpu.VMEM((NDEV, BL, D), f32), pltpu.VMEM((NCHIP, BL * NDEV, ADAW), f32),
                        pltpu.SemaphoreType.DMA((NDEV - 1,)), pltpu.SemaphoreType.DMA((NDEV - 1,)),
                        pltpu.SemaphoreType.DMA((NCHIP - 1,)), pltpu.SemaphoreType.DMA((NCHIP - 1,))],
        compiler_params=pltpu.CompilerParams(vmem_limit_bytes=VMEM_LIMIT),
    )(c_in, w_ada, b_cols)


def _small_allreduce(accs, cond_all):
    def body(bi, bo, bm, bd, gi, go, gm, gd, dsink, cond_ref, gw_ref, gb_ref, small_ref, pay, pbuf, dall, s1, r1):
        x, y, c = _pos()
        chip = 2 * x + y
        pay[...] = jnp.zeros_like(pay)
        for b in range(BL):
            for k, (ref, r) in enumerate(((bi, 1), (bi, 0), (bo, 0), (bm, 1), (bm, 0), (bd, 0))):
                pay[b:b + 1, D * k:D * (k + 1)] = ref[b, r:r + 1, :]
        for off, ref, r in ((OFF_G_ATTN_PRE, gi, 0), (OFF_G_ATTN_POST, go, 0), (OFF_G_MIX_A, go, 1), (OFF_G_MLP_PRE, gm, 0),
                            (OFF_G_MLP_POST, gd, 0)):
            pay[BL:BL + 1, off:off + D] = ref[r:r + 1, :]
        eye = lax.broadcasted_iota(jnp.int32, (8, LANES), 0) == lax.broadcasted_iota(jnp.int32, (8, LANES), 1)
        pay[BL:BL + 1, OFF_SINK:OFF_SINK + LANES] = jnp.sum(jnp.where(eye, dsink[...], 0.0), axis=0, keepdims=True)
        pay[BL:BL + 1, OFF_LOSS:OFF_LOSS + LANES] = gd[1:2, 0:LANES]
        me = _gather_small(pay, pbuf, s1, r1)
        pbuf[me] = pay[...]
        small = pbuf[0, BL:BL + 1, :]
        for i in range(1, NDEV):
            small = small + pbuf[i, BL:BL + 1, :]
        small_ref[...] = small
        for i in range(NDEV):
            dall[BL * i:BL * (i + 1), :] = pbuf[i, 0:BL, :]
        gb_ref[...] = jnp.sum(dall[...], axis=0, keepdims=True)
        cols = jnp.zeros((BL * NDEV, ADAW), f32)
        for s in range(NCHIP):
            cols = cols + jnp.where(chip == s, dall[:, ADAW * s:ADAW * (s + 1)], 0.0)
        gw_ref[...] = lax.dot_general(cond_ref[...], cols, (((0,), (0,)), ((), ())), preferred_element_type=f32,
                                      precision=lax.Precision.HIGHEST)

    vm = pl.BlockSpec(memory_space=pltpu.VMEM)
    return pl.pallas_call(
        body, name="small_allreduce", in_specs=[vm] * 10, out_specs=[vm] * 3,
        out_shape=[jax.ShapeDtypeStruct((D, ADAW), f32), jax.ShapeDtypeStruct((1, PAYW), f32), jax.ShapeDtypeStruct((1, PAYW), f32)],
        scratch_shapes=[pltpu.VMEM((4, PAYW), f32), pltpu.VMEM((NDEV, 4, PAYW), f32), pltpu.VMEM((BL * NDEV, PAYW), f32),
                        pltpu.SemaphoreType.DMA((NDEV - 1,)), pltpu.SemaphoreType.DMA((NDEV - 1,))],
        compiler_params=pltpu.CompilerParams(vmem_limit_bytes=VMEM_LIMIT),
    )(*accs, cond_all)


def _half(ref, c):
    r2 = ref.shape[0] // 2
    return ref.at[pl.ds(pl.multiple_of(c * r2, 16), r2), :]


HBM_SPEC = pl.BlockSpec(memory_space=pltpu.HBM)
SEM_SPEC = pl.BlockSpec(memory_space=pltpu.SEMAPHORE)
EFFECT = pltpu.SideEffectType.DATAFLOW_SIDE_EFFECTING
NLINK = NCHIP - 1


def _in_hbm(a):
    return pltpu.with_memory_space_constraint(a, pltpu.HBM)


def _split_start(name, srcs, land_shapes, builds):
    n = len(srcs)

    def body(*refs):
        src, land, send, recv, token = refs[:n], refs[n:2 * n], refs[2 * n:3 * n], refs[3 * n:4 * n], refs[-1]
        for t in range(n):
            for out_cp, _ in builds[t](src[t], land[t], send[t], recv[t]):
                out_cp.start()
        token[...] = jnp.zeros_like(token)

    lands = [_in_hbm(lax.empty(s.shape, s.dtype)) for s in land_shapes]
    sems = [pltpu.SemaphoreType.DMA((NLINK,))] * (2 * n)
    thru = [pltpu.HBM(a.shape, a.dtype) for a in list(srcs) + lands]
    res = pl.pallas_call(
        body, name=name, out_shape=sems + thru + [jax.ShapeDtypeStruct((8, LANES), f32)],
        in_specs=[HBM_SPEC] * (2 * n), out_specs=[SEM_SPEC] * (2 * n) + [HBM_SPEC] * (2 * n) + [pl.BlockSpec(memory_space=pltpu.VMEM)],
        input_output_aliases={i: 2 * n + i for i in range(2 * n)},
        compiler_params=pltpu.CompilerParams(has_side_effects=EFFECT),
    )(*[_in_hbm(a) for a in srcs], *lands)
    flight = [(res[2 * n + t], res[3 * n + t], res[t], res[n + t]) for t in range(n)]
    return flight, res[-1][0, 0]


def _split_wait(name, flight, builds, after):
    m = len(flight)
    na = len(after)

    def body(*refs):
        src, land, send, recv = refs[:m], refs[m:2 * m], refs[2 * m:3 * m], refs[3 * m:4 * m]
        for t in range(m):
            for out_cp, in_cp in builds[t](src[t], land[t], send[t], recv[t]):
                out_cp.wait_send()
                in_cp.wait_recv()

    ops = [f[0] for f in flight] + [f[1] for f in flight] + [f[2] for f in flight] + [f[3] for f in flight]
    res = pl.pallas_call(
        body, name=name, out_shape=[pltpu.HBM(a.shape, a.dtype) for a in ops[:2 * m]],
        in_specs=[HBM_SPEC] * (2 * m) + [SEM_SPEC] * (2 * m) + [pl.BlockSpec(memory_space=pl.ANY)] * na,
        out_specs=[HBM_SPEC] * (2 * m), input_output_aliases={i: i for i in range(2 * m)},
        compiler_params=pltpu.CompilerParams(has_side_effects=EFFECT),
    )(*ops, *after)
    return res[:m], res[m:2 * m]


def _weight_copies(src, land, send, recv):
    x, y, c = _pos()
    chip = 2 * x + y
    return [(_rcopy(_half(src, c), _half(land.at[chip], c), send, recv, j, (px, py, c)),
             _rcopy(_half(src, c), _half(land.at[2 * px + py], c), send, recv, j, (px, py, c)))
            for j, (px, py) in enumerate(_other_chips(x, y))]


def _grad_copies(src, land, send, recv):
    x, y, c = _pos()
    return [(_rcopy(src.at[2 * px + py], land.at[j], send, recv, j, (px, py, c)),
             _rcopy(src.at[2 * px + py], land.at[j], send, recv, j, (px, py, c)))
            for j, (px, py) in enumerate(_other_chips(x, y))]


def _pair_forward(shards, gathered, name):
    nt = len(shards)

    def body(*refs):
        sh, gin, gout = refs[:nt], refs[nt:2 * nt], refs[2 * nt:3 * nt]
        send, recv = refs[3 * nt:]
        x, y, c = _pos()
        chip = 2 * x + y
        sib = (x, y, 1 - c)
        chips = _other_chips(x, y)
        cps = []
        for t in range(nt):
            for j, (px, py) in enumerate(chips):
                cps.append(_rcopy(_half(gin[t].at[2 * px + py], c), _half(gout[t].at[2 * px + py], c), send, recv, 4 * t + j, sib))
            cps.append(_rcopy(sh[t], gout[t].at[chip], send, recv, 4 * t + 3, sib))
        for cp in cps:
            cp.start()
        for t in range(nt):
            for j, (px, py) in enumerate(chips):
                theirs = _half(gout[t].at[2 * px + py], 1 - c)
                _rcopy(theirs, theirs, send, recv, 4 * t + j, sib).wait_recv()
            _rcopy(sh[t], gout[t].at[chip], send, recv, 4 * t + 3, sib).wait_recv()
        for cp in cps:
            cp.wait_send()

    hbm = pl.BlockSpec(memory_space=pl.ANY)
    return pl.pallas_call(
        body, name=name, in_specs=[hbm] * (2 * nt), out_specs=[hbm] * nt,
        out_shape=[jax.ShapeDtypeStruct(g.shape, g.dtype) for g in gathered],
        input_output_aliases={nt + t: t for t in range(nt)},
        scratch_shapes=[pltpu.SemaphoreType.DMA((4 * nt,)), pltpu.SemaphoreType.DMA((4 * nt,))],
    )(*shards, *gathered)


def _rs_pair(grads, name):
    nt = len(grads)

    def body(*refs):
        ins, outs = refs[:nt], refs[nt:2 * nt]
        send, recv = refs[2 * nt:]
        x, y, c = _pos()
        sib = (x, y, 1 - c)
        cps = []
        for t in range(nt):
            r2 = ins[t].shape[1] // 2
            src = ins[t].at[:, pl.ds(pl.multiple_of((1 - c) * r2, 8), r2), :]
            cps.append(_rcopy(src, outs[t], send, recv, t, sib))
        for cp in cps:
            cp.start()
        for cp in cps:
            cp.wait()

    hbm = pl.BlockSpec(memory_space=pl.ANY)
    return pl.pallas_call(
        body, name=name, in_specs=[hbm] * nt, out_specs=[hbm] * nt,
        out_shape=[jax.ShapeDtypeStruct((NCHIP, g.shape[1] // 2, g.shape[2]), f32) for g in grads],
        scratch_shapes=[pltpu.SemaphoreType.DMA((nt,)), pltpu.SemaphoreType.DMA((nt,))],
    )(*grads)


RS_ROWS = 128


def _pair_add(g, landed, c_arr, name):
    _, r2, cw = landed.shape
    nr = r2 // RS_ROWS

    def body(c_ref, g_ref, p_ref, o_ref):
        o_ref[...] = (g_ref[...] + p_ref[...]).astype(bf16)

    gs = pltpu.PrefetchScalarGridSpec(
        num_scalar_prefetch=1, grid=(NCHIP, nr),
        in_specs=[pl.BlockSpec((None, RS_ROWS, cw), lambda s, j, c: (s, c[0] * nr + j, 0)),
                  pl.BlockSpec((None, RS_ROWS, cw), lambda s, j, c: (s, j, 0))],
        out_specs=pl.BlockSpec((None, RS_ROWS, cw), lambda s, j, c: (s, j, 0)))
    return pl.pallas_call(body, name=name, grid_spec=gs, out_shape=jax.ShapeDtypeStruct((NCHIP, r2, cw), bf16),
                          compiler_params=_cp(("arbitrary", "arbitrary")))(c_arr, g, landed)


def _chip_add(half, landed, pos_arr, name):
    _, r2, cw = half.shape
    nr = r2 // RS_ROWS

    def body(s_ref, h_ref, q_ref, o_ref):
        acc = h_ref[...].astype(f32)
        for j in range(NCHIP - 1):
            acc = acc + q_ref[j].astype(f32)
        o_ref[...] = acc

    gs = pltpu.PrefetchScalarGridSpec(
        num_scalar_prefetch=1, grid=(nr,),
        in_specs=[pl.BlockSpec((None, RS_ROWS, cw), lambda j, s: (s[0], j, 0)),
                  pl.BlockSpec((NCHIP - 1, RS_ROWS, cw), lambda j, s: (0, j, 0))],
        out_specs=pl.BlockSpec((RS_ROWS, cw), lambda j, s: (s[1] * nr + j, 0)))
    return pl.pallas_call(body, name=name, grid_spec=gs, out_shape=jax.ShapeDtypeStruct((2 * r2, cw), f32),
                          compiler_params=_cp(("arbitrary",)))(pos_arr, half, landed)


def _ag_pair(fulls, name):
    nt = len(fulls)

    def body(*refs):
        ins, outs = refs[:nt], refs[nt:2 * nt]
        send, recv = refs[2 * nt:]
        x, y, c = _pos()
        sib = (x, y, 1 - c)
        cps = [_rcopy(_half(ins[t], c), _half(outs[t], c), send, recv, t, sib) for t in range(nt)]
        for cp in cps:
            cp.start()
        for t in range(nt):
            _rcopy(_half(ins[t], c), _half(outs[t], 1 - c), send, recv, t, sib).wait_recv()
        for cp in cps:
            cp.wait_send()

    hbm = pl.BlockSpec(memory_space=pl.ANY)
    return pl.pallas_call(
        body, name=name, in_specs=[hbm] * nt, out_specs=[hbm] * nt,
        out_shape=[jax.ShapeDtypeStruct(a.shape, f32) for a in fulls],
        input_output_aliases={t: t for t in range(nt)},
        scratch_shapes=[pltpu.SemaphoreType.DMA((nt,)), pltpu.SemaphoreType.DMA((nt,))],
    )(*fulls)


def _adamw_math(w, g, m, v):
    m = B1 * m + (1.0 - B1) * g
    v = B2 * v + (1.0 - B2) * jnp.square(g)
    m_hat = m / (1.0 - B1 ** STEP)
    v_hat = v / (1.0 - B2 ** STEP)
    return -LR * (m_hat / (jnp.sqrt(v_hat) + AEPS) + WD * w), m, v


ADAM_ROWS = 128


def _adamw(w, g, m, v, name):
    r, cw = w.shape

    def body(w_ref, g_ref, m_ref, v_ref, d_ref, mo_ref, vo_ref):
        d_ref[...], mo_ref[...], vo_ref[...] = _adamw_math(w_ref[...], g_ref[...], m_ref[...], v_ref[...])

    spec = pl.BlockSpec((ADAM_ROWS, cw), lambda i: (i, 0))
    return pl.pallas_call(body, name=name, grid=(r // ADAM_ROWS,), in_specs=[spec] * 4, out_specs=[spec] * 3,
                          out_shape=[jax.ShapeDtypeStruct((r, cw), f32)] * 3, compiler_params=_cp(("arbitrary",)))(w, g, m, v)


SMALL = (("b_ada", None, PAYW), ("g_attn_pre", OFF_G_ATTN_PRE, D), ("g_attn_post", OFF_G_ATTN_POST, D), ("sink_a", OFF_SINK, 8),
         ("g_mix_a", OFF_G_MIX_A, AQ), ("g_mix_b", OFF_G_MIX_B, BW), ("g_mlp_pre", OFF_G_MLP_PRE, D), ("g_mlp_post", OFF_G_MLP_POST, D))


def _adamw_small(small, gb, params):
    n = len(SMALL)

    def body(*refs):
        small_ref, gb_ref = refs[:2]
        wmv = refs[2:2 + 3 * n]
        loss_ref = refs[2 + 3 * n]
        outs = refs[3 + 3 * n:]
        loss_ref[...] = small_ref[:, OFF_LOSS:OFF_LOSS + 1] * (0.5 / D)
        for i, (_, off, width) in enumerate(SMALL):
            g = gb_ref[...] if off is None else small_ref[:, off:off + width]
            w_ref, m_ref, v_ref = wmv[3 * i:3 * i + 3]
            outs[4 * i][...] = g
            outs[4 * i + 1][...], outs[4 * i + 2][...], outs[4 * i + 3][...] = _adamw_math(w_ref[...], g, m_ref[...], v_ref[...])

    vm = pl.BlockSpec(memory_space=pltpu.VMEM)
    out_shape = [jax.ShapeDtypeStruct((1, 1), f32)]
    for _, _, width in SMALL:
        out_shape += [jax.ShapeDtypeStruct((1, width), f32)] * 4
    flat = [a for wmv in params for a in wmv]
    res = pl.pallas_call(body, name="adamw_small", in_specs=[vm] * (2 + 3 * n), out_specs=[vm] * len(out_shape),
                         out_shape=out_shape)(small, gb, *flat)
    return res[0], {name: res[1 + 4 * i:5 + 4 * i] for i, (name, _, _) in enumerate(SMALL)}


def kernel(x, c, positions, w_ada, b_ada, g_attn_pre, g_attn_post, w_in, sink_a, g_mix_a, g_mix_b, w_out, g_mlp_pre, g_mlp_post, w_up, w_down, loss_target, m_w_ada, m_b_ada, m_g_attn_pre, m_g_attn_post, m_w_in, m_sink_a, m_g_mix_a, m_g_mix_b, m_w_out, m_g_mlp_pre, m_g_mlp_post, m_w_up, m_w_down, v_w_ada, v_b_ada, v_g_attn_pre, v_g_attn_post, v_w_in, v_sink_a, v_g_mix_a, v_g_mix_b, v_w_out, v_g_mlp_pre, v_g_mlp_post, v_w_up, v_w_down):
    given = dict(w_ada=w_ada, b_ada=b_ada, g_attn_pre=g_attn_pre, g_attn_post=g_attn_post, w_in=w_in, sink_a=sink_a, g_mix_a=g_mix_a,
                 g_mix_b=g_mix_b, w_out=w_out, g_mlp_pre=g_mlp_pre, g_mlp_post=g_mlp_post, w_up=w_up, w_down=w_down)
    moms = dict(w_ada=(m_w_ada, v_w_ada), b_ada=(m_b_ada, v_b_ada), g_attn_pre=(m_g_attn_pre, v_g_attn_pre),
                g_attn_post=(m_g_attn_post, v_g_attn_post), w_in=(m_w_in, v_w_in), sink_a=(m_sink_a, v_sink_a),
                g_mix_a=(m_g_mix_a, v_g_mix_a), g_mix_b=(m_g_mix_b, v_g_mix_b), w_out=(m_w_out, v_w_out),
                g_mlp_pre=(m_g_mlp_pre, v_g_mlp_pre), g_mlp_post=(m_g_mlp_post, v_g_mlp_post), w_up=(m_w_up, v_w_up),
                w_down=(m_w_down, v_w_down))
    order = ["w_ada", "b_ada", "g_attn_pre", "g_attn_post", "w_in", "sink_a", "g_mix_a", "g_mix_b", "w_out", "g_mlp_pre",
             "g_mlp_post", "w_up", "w_down"]
    xi, yi, ci = _pos()
    chip = 2 * xi + yi

    c_arr = jnp.reshape(ci, (1,)).astype(jnp.int32)
    pos_arr = jnp.stack([chip, ci]).astype(jnp.int32)
    big = ("w_in", "w_out", "w_up", "w_down")

    shards = [given[n][0].astype(bf16) for n in big]
    flight, tok = _split_start("weights_start", shards, [jax.ShapeDtypeStruct((NCHIP,) + s.shape, bf16) for s in shards],
                               [_weight_copies] * len(big))
    b_cols = lax.dynamic_slice(b_ada, (0, chip * ADAW), (1, ADAW))
    mod, cond_all = _ada_fwd(c + tok, w_ada[0], b_cols)
    mod = mod.reshape(BL, NMOD, D)
    srcs, lands = _split_wait("weights_wait_first", flight[:1], [_weight_copies], (mod,))
    (win_g,) = _pair_forward(srcs, lands, "weights_pair_first")
    w_in_full = win_g.transpose(1, 0, 2).reshape(D, INW)

    def later_weights(after):
        srcs, lands = _split_wait("weights_wait_rest", flight[1:], [_weight_copies] * 3, after)
        wout_g, wup_g, wdn_g = _pair_forward(srcs, lands, "weights_pair_rest")
        return wout_g.reshape(D, D), wup_g, wdn_g.reshape(DFF, D)

    pending = {}

    def grad_ready(group, g):
        if group == "w_down":
            names, slabs = ("w_down",), [g.reshape(NCHIP, DFF // NCHIP, D)]
        elif group == "w_up":
            names, slabs = ("w_up",), [g]
        else:
            names = ("w_in", "w_out")
            slabs = [g[0].reshape(D, NCHIP, INW // NCHIP).transpose(1, 0, 2), g[1].reshape(NCHIP, D // NCHIP, D)]
        landed = _rs_pair(slabs, "grad_pair_exchange_" + group)
        halves = [_pair_add(s, p, c_arr, "grad_pair_sum_" + n) for s, p, n in zip(slabs, landed, names)]
        fl, tk = _split_start("grad_start_" + group, halves,
                              [jax.ShapeDtypeStruct((NLINK,) + h.shape[1:], bf16) for h in halves], [_grad_copies] * len(names))
        pending[group] = (names, fl)
        return tk

    grad_x, accs = _local_step(x, positions, mod, loss_target, w_in_full, later_weights, grad_ready,
                               g_attn_pre, g_attn_post, sink_a, g_mix_a, g_mix_b, g_mlp_pre, g_mlp_post)

    def finish(groups, after):
        names = sum((pending[g][0] for g in groups), ())
        fl = sum((pending[g][1] for g in groups), [])
        halves, landed = _split_wait("grad_wait_" + groups[0], fl, [_grad_copies] * len(names), after)
        fulls = [_chip_add(h, q, pos_arr, "grad_chip_sum_" + n) for h, q, n in zip(halves, landed, names)]
        return dict(zip(names, _ag_pair(fulls, "grad_pair_gather_" + groups[0])))

    grads, out = {}, {}

    def update(n):
        d, m2, v2 = _adamw(given[n][0], grads[n], moms[n][0][0], moms[n][1][0], "adamw_" + n)
        out[n] = (grads[n][None], d[None], m2[None], v2[None])
        return v2

    grads["w_ada"], gb, small = _small_allreduce(accs, cond_all)
    grads.update(finish(("w_down", "w_up"), (small,)))
    last = [update(n) for n in ("w_down", "w_up", "w_ada")]
    grads.update(finish(("w_in_w_out",), tuple(last)))
    update("w_in")
    update("w_out")
    loss, res = _adamw_small(small, gb, [(given[n], moms[n][0], moms[n][1]) for n, _, _ in SMALL])
    for n, _, _ in SMALL:
        out[n] = tuple(res[n])
    return (loss.reshape(()), grad_x, *[out[n][0] for n in order], *[out[n][1] for n in order],
            *[out[n][2] for n in order], *[out[n][3] for n in order])
```

```python
import functools

import numpy as np
import jax
import jax.numpy as jnp
from jax import lax
from jax.experimental import pallas as pl
from jax.experimental.pallas import tpu as pltpu

f32 = jnp.float32
bf16 = jnp.bfloat16
MESH = pl.DeviceIdType.MESH

D = 1024
SEQ = 2048
BL = 2
HD = 64
AQ = 512
AKV = 128
BW = 512
INW = 2304
DFF = 4096
NMOD = 6
ROT = 16
THETA = 500000.0
EPS = 1e-6
NEG = -1e30
BLK = 128
TM = 256
NJ = SEQ // TM
LANES = 128
NCHIP = 4
NDEV = 8
VMEM_LIMIT = 56 << 20

LR, B1, B2, AEPS, WD, STEP = 0.001, 0.9, 0.999, 1e-08, 0.01, 10

OFF_G_ATTN_PRE, OFF_G_ATTN_POST, OFF_G_MIX_A, OFF_G_MIX_B = 0, 1024, 2048, 2560
OFF_G_MLP_PRE, OFF_G_MLP_POST, OFF_SINK, OFF_LOSS = 3072, 4096, 5120, 5248
PAYW = NMOD * D


def _cp(sem=None):
    return pltpu.CompilerParams(dimension_semantics=sem, vmem_limit_bytes=VMEM_LIMIT)


def _dot(a, b):
    return jnp.dot(a, b, preferred_element_type=f32)


def _dot_nt(a, b):
    return lax.dot_general(a, b, (((1,), (1,)), ((), ())), preferred_element_type=f32)


def _dot_tn(a, b):
    return lax.dot_general(a, b, (((0,), (0,)), ((), ())), preferred_element_type=f32)


def _rms(x):
    r = lax.rsqrt(jnp.mean(x * x, axis=-1, keepdims=True) + EPS)
    return x * r, r


def _rms_bwd(dy, y, r):
    return r * (dy - y * jnp.mean(dy * y, axis=-1, keepdims=True))


def _colsum(v):
    return jnp.sum(v, axis=0, keepdims=True)


def _rope(p, c, s1, s2):
    outs = []
    for c0 in range(0, p.shape[1], LANES):
        pc = p[:, c0:c0 + LANES]
        outs.append(pc * c + pltpu.roll(pc, LANES - ROT // 2, 1) * s1 + pltpu.roll(pc, ROT // 2, 1) * s2)
    return outs[0] if len(outs) == 1 else jnp.concatenate(outs, axis=1)


def _rope_t(g, c, s1, s2):
    outs = []
    for c0 in range(0, g.shape[1], LANES):
        gc = g[:, c0:c0 + LANES]
        outs.append(gc * c + pltpu.roll(gc * s1, ROT // 2, 1) + pltpu.roll(gc * s2, LANES - ROT // 2, 1))
    return outs[0] if len(outs) == 1 else jnp.concatenate(outs, axis=1)


def _perm_store(val, scr, out_ref, d):
    nc = val.shape[1] // LANES
    for c in range(nc):
        scr[c] = val[:, LANES * c:LANES * (c + 1)]
    for c in range(nc):
        for r in range(d):
            out_ref[r, :, LANES * c:LANES * (c + 1)] = scr[c, pl.ds(r, TM // d, stride=d), :].astype(out_ref.dtype)


def _perm_load(in_ref, scr, d):
    nc = in_ref.shape[-1] // LANES
    for c in range(nc):
        for r in range(d):
            scr[c, pl.ds(r, TM // d, stride=d), :] = in_ref[r, :, LANES * c:LANES * (c + 1)].astype(f32)
    return jnp.concatenate([scr[c] for c in range(nc)], axis=1)


def _tok(w, dtype=None):
    return pl.BlockSpec((None, TM, w), lambda b, j: (b, j, 0))


def _perm_spec(d, w):
    return pl.BlockSpec((None, d, TM // d, w), lambda b, j: (b, 0, j, 0))


def _full(shape):
    n = len(shape)
    return pl.BlockSpec(shape, lambda b, j: (0,) * n)


MOD_SPEC = pl.BlockSpec((None, NMOD, D), lambda b, j: (b, 0, 0))
ACCB_SPEC = pl.BlockSpec((None, 8, D), lambda b, j: (b, 0, 0))
ACCG_SPEC = pl.BlockSpec((8, D), lambda b, j: (0, 0))
ACC_SHAPES = [jax.ShapeDtypeStruct((BL, 8, D), f32), jax.ShapeDtypeStruct((8, D), f32)]


def _acc_init(accb_ref, accg_ref):
    b, j = pl.program_id(0), pl.program_id(1)

    @pl.when(j == 0)
    def _():
        accb_ref[...] = jnp.zeros_like(accb_ref)

    @pl.when((b == 0) & (j == 0))
    def _():
        accg_ref[...] = jnp.zeros_like(accg_ref)


def _rope_tables(pos_col, inv_lane):
    def body(p_ref, inv_ref, c_ref, s1_ref, s2_ref):
        ang = p_ref[...].astype(f32) * inv_ref[...]
        j = lax.broadcasted_iota(jnp.int32, (TM, LANES), 1) % HD
        cs, sn = jnp.cos(ang), jnp.sin(ang)
        c_ref[...] = jnp.where(j < ROT, cs, 1.0)
        s1_ref[...] = jnp.where(j < ROT // 2, -sn, 0.0)
        s2_ref[...] = jnp.where((j >= ROT // 2) & (j < ROT), sn, 0.0)

    n = BL * SEQ // TM
    return pl.pallas_call(
        body, name="rope_tables", grid=(n,),
        in_specs=[pl.BlockSpec((TM, 1), lambda i: (i, 0)), pl.BlockSpec((1, LANES), lambda i: (0, 0))],
        out_specs=[pl.BlockSpec((TM, LANES), lambda i: (i, 0))] * 3,
        out_shape=[jax.ShapeDtypeStruct((BL * SEQ, LANES), f32)] * 3,
    )(pos_col, inv_lane)


def _attn_in(x, mod, g_pre, w_in, tc, ts1, ts2):
    def body(x_ref, mod_ref, g_ref, w_ref, c_ref, s1_ref, s2_ref,
             h_ref, qa_ref, ka_ref, va_ref, q1_ref, k1_ref, v1_ref, q4_ref, k4_ref, v4_ref, q16_ref, k16_ref, v16_ref,
             scr):
        xn, _ = _rms(x_ref[...])
        h = (xn * g_ref[...]) * (1.0 + mod_ref[1:2, :]) + mod_ref[0:1, :]
        hb = h.astype(bf16)
        h_ref[...] = hb
        proj = _dot(hb, w_ref[...])
        c, s1, s2 = c_ref[...], s1_ref[...], s2_ref[...]
        o1, o2, o3, o4, o5 = AQ, AQ + AKV, AQ + 2 * AKV, AQ + 2 * AKV + BW, AQ + 2 * AKV + 2 * BW
        qa_ref[...] = (_rope(proj[:, :o1], c, s1, s2) * 0.125).astype(bf16)
        ka_ref[...] = _rope(proj[:, o1:o2], c, s1, s2).astype(bf16)
        va_ref[...] = proj[:, o2:o3].astype(bf16)
        qb = _rope(proj[:, o3:o4], c, s1, s2) * 0.125
        kb = _rope(proj[:, o4:o5], c, s1, s2)
        vb = proj[:, o5:]
        for val, r1, r4, r16 in ((qb, q1_ref, q4_ref, q16_ref), (kb, k1_ref, k4_ref, k16_ref), (vb, v1_ref, v4_ref, v16_ref)):
            r1[...] = val.astype(bf16)
            _perm_store(val, scr, r4, 4)
            _perm_store(val, scr, r16, 16)

    nat = lambda w: jax.ShapeDtypeStruct((BL, SEQ, w), bf16)
    p4 = jax.ShapeDtypeStruct((BL, 4, SEQ // 4, BW), bf16)
    p16 = jax.ShapeDtypeStruct((BL, 16, SEQ // 16, BW), bf16)
    return pl.pallas_call(
        body, name="attn_in", grid=(BL, NJ),
        in_specs=[_tok(D), MOD_SPEC, _full((1, D)), _full((D, INW)), _tok(LANES), _tok(LANES), _tok(LANES)],
        out_specs=[_tok(D), _tok(AQ), _tok(AKV), _tok(AKV)] + [_tok(BW)] * 3 + [_perm_spec(4, BW)] * 3 + [_perm_spec(16, BW)] * 3,
        out_shape=[nat(D), nat(AQ), nat(AKV), nat(AKV)] + [nat(BW)] * 3 + [p4] * 3 + [p16] * 3,
        scratch_shapes=[pltpu.VMEM((BW // LANES, TM, LANES), f32)],
        compiler_params=_cp(("arbitrary", "arbitrary")),
    )(x, mod, g_pre, w_in, tc, ts1, ts2)


def _pair_kv(ref, p, gqa, lo):
    if not gqa:
        return ref[:, LANES * p:LANES * (p + 1)]
    k = ref[...]
    kr = pltpu.roll(k, HD, 1)
    return jnp.where(lo, k, kr) if p < 2 else jnp.where(lo, kr, k)


def _valid_mask(blk_idx, nb, max_dist):
    if nb == 1:
        qi = lax.broadcasted_iota(jnp.int32, (BLK, BLK), 0)
        kj = lax.broadcasted_iota(jnp.int32, (BLK, BLK), 1)
        return kj <= qi
    qi = lax.broadcasted_iota(jnp.int32, (BLK, 2 * BLK), 0)
    col = lax.broadcasted_iota(jnp.int32, (BLK, 2 * BLK), 1)
    prev_ok = jnp.logical_and(jnp.logical_and(col < BLK, col >= qi + (BLK - max_dist)), blk_idx > 0)
    return jnp.logical_or(jnp.logical_and(col >= BLK, (col - BLK) <= qi), prev_ok)


def _attn_fwd(q, k, v, sink, *, max_dist, name):
    n, l, w = q.shape
    wk = k.shape[-1]
    nb = l // BLK
    gqa = wk != w
    has_sink = sink is not None

    def body(*refs):
        if has_sink:
            sink_ref, refs = refs[0], refs[1:]
        if nb > 1:
            q_ref, kc_ref, kp_ref, vc_ref, vp_ref, o_ref, lse_ref = refs
        else:
            q_ref, kc_ref, vc_ref, o_ref, lse_ref = refs
        i = pl.program_id(1)
        lo = lax.broadcasted_iota(jnp.int32, (BLK, LANES), 1) < HD
        valid = _valid_mask(i, nb, max_dist)
        for p in range(w // LANES):
            qpair = q_ref[:, LANES * p:LANES * (p + 1)]
            kcat, vcat = _pair_kv(kc_ref, p, gqa, lo), _pair_kv(vc_ref, p, gqa, lo)
            if nb > 1:
                kcat = jnp.concatenate([_pair_kv(kp_ref, p, gqa, lo), kcat], axis=0)
                vcat = jnp.concatenate([_pair_kv(vp_ref, p, gqa, lo), vcat], axis=0)
            lov = lax.broadcasted_iota(jnp.int32, vcat.shape, 1) < HD
            o_pair = jnp.zeros((BLK, LANES), f32)
            lse_pair = jnp.zeros((BLK, LANES), f32)
            for hh in range(2):
                msk = lo if hh == 0 else jnp.logical_not(lo)
                mskv = lov if hh == 0 else jnp.logical_not(lov)
                s = _dot_nt(jnp.where(msk, qpair, jnp.zeros_like(qpair)), kcat)
                s = jnp.where(valid, s, NEG)
                m = jnp.max(s, axis=-1, keepdims=True)
                if has_sink:
                    sk = sink_ref[0, 2 * p + hh]
                    m = jnp.maximum(m, sk)
                e = jnp.exp(s - m)
                den = jnp.sum(e, axis=-1, keepdims=True)
                if has_sink:
                    den = den + jnp.exp(sk - m)
                pn = (e * (1.0 / den)).astype(bf16)
                o_pair = o_pair + _dot(pn, jnp.where(mskv, vcat, jnp.zeros_like(vcat)))
                lse_pair = jnp.where(msk, m + jnp.log(den), lse_pair)
            o_ref[:, LANES * p:LANES * (p + 1)] = o_pair
            lse_ref[:, LANES * p:LANES * (p + 1)] = lse_pair

    cur = lambda ww: pl.BlockSpec((None, BLK, ww), lambda a, i: (a, i, 0))
    prev = lambda ww: pl.BlockSpec((None, BLK, ww), lambda a, i: (a, jnp.maximum(i - 1, 0), 0))
    in_specs = [cur(w), cur(wk)] + ([prev(wk)] if nb > 1 else []) + [cur(wk)] + ([prev(wk)] if nb > 1 else [])
    args = [q, k] + ([k] if nb > 1 else []) + [v] + ([v] if nb > 1 else [])
    if has_sink:
        in_specs = [pl.BlockSpec(memory_space=pltpu.SMEM)] + in_specs
        args = [sink] + args
    return pl.pallas_call(
        body, name=name, grid=(n, nb), in_specs=in_specs,
        out_specs=[cur(w), cur(w)], out_shape=[jax.ShapeDtypeStruct((n, l, w), f32)] * 2,
        compiler_params=_cp(("arbitrary", "arbitrary")),
    )(*args)


def _attn_bwd(q, k, v, do, o, lse, sink, *, max_dist, name):
    n, l, w = q.shape
    wk = k.shape[-1]
    nb = l // BLK
    gqa = wk != w
    has_sink = sink is not None

    def body(*refs):
        if has_sink:
            sink_ref, refs = refs[0], refs[1:]
        if nb > 1:
            q_ref, kc_ref, kp_ref, vc_ref, vp_ref, do_ref, o_ref, lse_ref = refs[:8]
            rest = refs[8:]
        else:
            q_ref, kc_ref, vc_ref, do_ref, o_ref, lse_ref = refs[:6]
            rest = refs[6:]
        if has_sink:
            dq_ref, dk_ref, dv_ref, dsink_ref = rest[:4]
            rest = rest[4:]
        else:
            dq_ref, dk_ref, dv_ref = rest[:3]
            rest = rest[3:]
        step = pl.program_id(1)
        blk_idx = nb - 1 - step
        if nb > 1:
            ck, cv = rest

            @pl.when(step == 0)
            def _():
                ck[...] = jnp.zeros_like(ck)
                cv[...] = jnp.zeros_like(cv)

        if has_sink:
            @pl.when((pl.program_id(0) == 0) & (step == 0))
            def _():
                dsink_ref[...] = jnp.zeros_like(dsink_ref)

        lo = lax.broadcasted_iota(jnp.int32, (BLK, LANES), 1) < HD
        valid = _valid_mask(blk_idx, nb, max_dist)
        rows = 2 * BLK if nb > 1 else BLK
        gk = [jnp.zeros((rows, LANES), f32), jnp.zeros((rows, LANES), f32)]
        gv = [jnp.zeros((rows, LANES), f32), jnp.zeros((rows, LANES), f32)]
        for p in range(w // LANES):
            sl = slice(LANES * p, LANES * (p + 1))
            qpair = q_ref[:, sl]
            dopair = do_ref[:, sl]
            prod = dopair * o_ref[:, sl]
            lsepair = lse_ref[:, sl]
            kcat, vcat = _pair_kv(kc_ref, p, gqa, lo), _pair_kv(vc_ref, p, gqa, lo)
            if nb > 1:
                kcat = jnp.concatenate([_pair_kv(kp_ref, p, gqa, lo), kcat], axis=0)
                vcat = jnp.concatenate([_pair_kv(vp_ref, p, gqa, lo), vcat], axis=0)
            lov = lax.broadcasted_iota(jnp.int32, kcat.shape, 1) < HD
            dq_pair = jnp.zeros((BLK, LANES), f32)
            dk_pair = jnp.zeros((rows, LANES), f32)
            dv_pair = jnp.zeros((rows, LANES), f32)
            for hh in range(2):
                msk = lo if hh == 0 else jnp.logical_not(lo)
                mskv = lov if hh == 0 else jnp.logical_not(lov)
                qm = jnp.where(msk, qpair, jnp.zeros_like(qpair))
                dom = jnp.where(msk, dopair, 0.0).astype(bf16)
                delta = jnp.sum(jnp.where(msk, prod, 0.0), axis=-1, keepdims=True)
                lse_h = lsepair[:, HD * hh:HD * hh + 1]
                s = jnp.where(valid, _dot_nt(qm, kcat), NEG)
                pr = jnp.exp(s - lse_h)
                dp = _dot_nt(dom, vcat)
                ds = (pr * (dp - delta)).astype(bf16)
                prb = pr.astype(bf16)
                dq_pair = dq_pair + _dot(ds, jnp.where(mskv, kcat, jnp.zeros_like(kcat)))
                dk_pair = dk_pair + _dot_tn(ds, qm)
                dv_pair = dv_pair + _dot_tn(prb, dom)
                if has_sink:
                    h = 2 * p + hh
                    dsk = -jnp.sum(jnp.exp(sink_ref[0, h] - lse_h) * delta, keepdims=True)
                    dsink_ref[h:h + 1, :] += jnp.broadcast_to(dsk, (1, LANES))
            dq_ref[:, sl] = dq_pair
            if gqa:
                gk[p // 2] = gk[p // 2] + dk_pair
                gv[p // 2] = gv[p // 2] + dv_pair
            elif nb > 1:
                dk_ref[:, sl] = dk_pair[BLK:] + ck[:, sl]
                dv_ref[:, sl] = dv_pair[BLK:] + cv[:, sl]
                ck[:, sl] = dk_pair[:BLK]
                cv[:, sl] = dv_pair[:BLK]
            else:
                dk_ref[:, sl] = dk_pair
                dv_ref[:, sl] = dv_pair
        if gqa:
            lor = lax.broadcasted_iota(jnp.int32, (rows, LANES), 1) < HD
            fold = lambda g: jnp.where(lor, g[0] + pltpu.roll(g[0], HD, 1), g[1] + pltpu.roll(g[1], HD, 1))
            dk_full, dv_full = fold(gk), fold(gv)
            dk_ref[...] = dk_full[BLK:] + ck[...]
            dv_ref[...] = dv_full[BLK:] + cv[...]
            ck[...] = dk_full[:BLK]
            cv[...] = dv_full[:BLK]

    cur = lambda ww: pl.BlockSpec((None, BLK, ww), lambda a, i: (a, nb - 1 - i, 0))
    prev = lambda ww: pl.BlockSpec((None, BLK, ww), lambda a, i: (a, jnp.maximum(nb - 2 - i, 0), 0))
    in_specs = [cur(w), cur(wk)] + ([prev(wk)] if nb > 1 else []) + [cur(wk)] + ([prev(wk)] if nb > 1 else []) + [cur(w)] * 3
    args = [q, k] + ([k] if nb > 1 else []) + [v] + ([v] if nb > 1 else []) + [do, o, lse]
    out_specs = [cur(w), cur(wk), cur(wk)]
    out_shape = [jax.ShapeDtypeStruct((n, l, w), f32), jax.ShapeDtypeStruct((n, l, wk), f32), jax.ShapeDtypeStruct((n, l, wk), f32)]
    if has_sink:
        in_specs = [pl.BlockSpec(memory_space=pltpu.SMEM)] + in_specs
        args = [sink] + args
        out_specs.append(pl.BlockSpec((8, LANES), lambda a, i: (0, 0)))
        out_shape.append(jax.ShapeDtypeStruct((8, LANES), f32))
    scratch = [pltpu.VMEM((BLK, wk), f32), pltpu.VMEM((BLK, wk), f32)] if nb > 1 else []
    return pl.pallas_call(
        body, name=name, grid=(n, nb), in_specs=in_specs, out_specs=out_specs, out_shape=out_shape,
        scratch_shapes=scratch, compiler_params=_cp(("arbitrary", "arbitrary")),
    )(*args)


def _mix_out(oa, o1, l1, o4, l4, o16, l16, g_mix_a, g_mix_b, w_out, x, mod, g_post):
    def body(oa_ref, o1_ref, l1_ref, o4_ref, l4_ref, o16_ref, l16_ref, ga_ref, gb_ref, w_ref, x_ref, mod_ref, gp_ref,
             x1_ref, y_ref, mixed_ref, ob_ref, ob4_ref, ob16_ref, w1_ref, w4_ref, w16_ref, scr):
        o4v = _perm_load(o4_ref, scr, 4)
        l4v = _perm_load(l4_ref, scr, 4)
        o16v = _perm_load(o16_ref, scr, 16)
        l16v = _perm_load(l16_ref, scr, 16)
        l1v = l1_ref[...]
        m = jnp.maximum(jnp.maximum(l1v, l4v), l16v)
        e1, e4, e16 = jnp.exp(l1v - m), jnp.exp(l4v - m), jnp.exp(l16v - m)
        z = e1 + e4 + e16
        w1, w4, w16 = e1 / z, e4 / z, e16 / z
        ob = w1 * o1_ref[...] + w4 * o4v + w16 * o16v
        w1_ref[...] = w1
        w4_ref[...] = w4
        w16_ref[...] = w16
        ob_ref[...] = ob
        _perm_store(ob, scr, ob4_ref, 4)
        _perm_store(ob, scr, ob16_ref, 16)
        oan, _ = _rms(oa_ref[...])
        obn, _ = _rms(ob)
        mixed = jnp.concatenate([oan * ga_ref[...], obn * gb_ref[...]], axis=1).astype(bf16)
        mixed_ref[...] = mixed
        y = _dot(mixed, w_ref[...])
        y_ref[...] = y
        yn, _ = _rms(y)
        x1_ref[...] = x_ref[...] + mod_ref[2:3, :] * (yn * gp_ref[...])

    nat = lambda w, dt: jax.ShapeDtypeStruct((BL, SEQ, w), dt)
    return pl.pallas_call(
        body, name="mix_out", grid=(BL, NJ),
        in_specs=[_tok(AQ), _tok(BW), _tok(BW), _perm_spec(4, BW), _perm_spec(4, BW), _perm_spec(16, BW), _perm_spec(16, BW),
                  _full((1, AQ)), _full((1, BW)), _full((D, D)), _tok(D), MOD_SPEC, _full((1, D))],
        out_specs=[_tok(D), _tok(D), _tok(D), _tok(BW), _perm_spec(4, BW), _perm_spec(16, BW), _tok(BW), _tok(BW), _tok(BW)],
        out_shape=[nat(D, f32), nat(D, f32), nat(D, bf16), nat(BW, f32),
                   jax.ShapeDtypeStruct((BL, 4, SEQ // 4, BW), f32), jax.ShapeDtypeStruct((BL, 16, SEQ // 16, BW), f32),
                   nat(BW, f32), nat(BW, f32), nat(BW, f32)],
        scratch_shapes=[pltpu.VMEM((BW // LANES, TM, LANES), f32)],
        compiler_params=_cp(("arbitrary", "arbitrary")),
    )(oa, o1, l1, o4, l4, o16, l16, g_mix_a, g_mix_b, w_out, x, mod, g_post)


def _mlp_up(x1, mod, g_pre, w_up):
    def body(x_ref, mod_ref, g_ref, w_ref, h_ref, u_ref, a_ref):
        xn, _ = _rms(x_ref[...])
        h = (xn * g_ref[...]) * (1.0 + mod_ref[4:5, :]) + mod_ref[3:4, :]
        hb = h.astype(bf16)
        h_ref[...] = hb
        for s in range(NCHIP):
            u = _dot(hb, w_ref[s])
            u_ref[:, D * s:D * (s + 1)] = u.astype(bf16)
            a_ref[:, D * s:D * (s + 1)] = jnp.square(jnp.maximum(u, 0.0)).astype(bf16)

    nat = lambda w: jax.ShapeDtypeStruct((BL, SEQ, w), bf16)
    return pl.pallas_call(
        body, name="mlp_up", grid=(BL, NJ),
        in_specs=[_tok(D), MOD_SPEC, _full((1, D)), _full((NCHIP, D, D))],
        out_specs=[_tok(D), _tok(DFF), _tok(DFF)], out_shape=[nat(D), nat(DFF), nat(DFF)],
        compiler_params=_cp(("arbitrary", "arbitrary")),
    )(x1, mod, g_pre, w_up)


def _mlp_down(a, w_down, x1, target, mod, g_post):
    def body(a_ref, w_ref, x_ref, t_ref, mod_ref, g_ref, gx_ref, dy_ref, accb_ref, accg_ref):
        _acc_init(accb_ref, accg_ref)
        y2 = _dot(a_ref[...], w_ref[...])
        yn, r = _rms(y2)
        g = g_ref[...]
        gt = mod_ref[5:6, :]
        n2 = yn * g
        err = x_ref[...] + gt * n2 - t_ref[...]
        gout = err * (1.0 / D)
        gx_ref[...] = gout
        dn2 = gout * gt
        dy_ref[...] = _rms_bwd(dn2 * g, yn, r).astype(bf16)
        accb_ref[0:1, :] += _colsum(gout * n2)
        accg_ref[0:1, :] += _colsum(dn2 * yn)
        accg_ref[1:2, :] += jnp.broadcast_to(jnp.sum(err * err, keepdims=True), (1, D))

    return pl.pallas_call(
        body, name="mlp_down", grid=(BL, NJ),
        in_specs=[_tok(DFF), _full((DFF, D)), _tok(D), _tok(D), MOD_SPEC, _full((1, D))],
        out_specs=[_tok(D), _tok(D), ACCB_SPEC, ACCG_SPEC],
        out_shape=[jax.ShapeDtypeStruct((BL, SEQ, D), f32), jax.ShapeDtypeStruct((BL, SEQ, D), bf16)] + ACC_SHAPES,
        compiler_params=_cp(("arbitrary", "arbitrary")),
    )(a, w_down, x1, target, mod, g_post)


def _mlp_bwd(dy2, u, w_down, w_up, x1, gx, mod, g_pre):
    def body(dy_ref, u_ref, wd_hbm, wu_hbm, x_ref, gx_ref, mod_ref, g_ref, du_ref, gx1_ref, accb_ref, accg_ref, wd, wu, sem):
        _acc_init(accb_ref, accg_ref)

        @pl.when((pl.program_id(0) == 0) & (pl.program_id(1) == 0))
        def _():
            c1 = pltpu.make_async_copy(wd_hbm, wd, sem.at[0])
            c2 = pltpu.make_async_copy(wu_hbm, wu, sem.at[1])
            c1.start()
            c2.start()
            c1.wait()
            c2.wait()

        dy = dy_ref[...]
        dh = jnp.zeros((TM, D), f32)
        for s in range(NCHIP):
            sl = slice(D * s, D * (s + 1))
            da = _dot_nt(dy, wd[sl, :])
            du = (da * (2.0 * jnp.maximum(u_ref[:, sl].astype(f32), 0.0))).astype(bf16)
            du_ref[:, sl] = du
            dh = dh + _dot_nt(du, wu[s])
        xn, r = _rms(x_ref[...])
        g = g_ref[...]
        n = xn * g
        dn = dh * (1.0 + mod_ref[4:5, :])
        gx1_ref[...] = gx_ref[...] + _rms_bwd(dn * g, xn, r)
        accb_ref[0:1, :] += _colsum(dh * n)
        accb_ref[1:2, :] += _colsum(dh)
        accg_ref[0:1, :] += _colsum(dn * xn)

    anyspec = pl.BlockSpec(memory_space=pl.ANY)
    return pl.pallas_call(
        body, name="mlp_bwd", grid=(BL, NJ),
        in_specs=[_tok(D), _tok(DFF), anyspec, anyspec, _tok(D), _tok(D), MOD_SPEC, _full((1, D))],
        out_specs=[_tok(DFF), _tok(D), ACCB_SPEC, ACCG_SPEC],
        out_shape=[jax.ShapeDtypeStruct((BL, SEQ, DFF), bf16), jax.ShapeDtypeStruct((BL, SEQ, D), f32)] + ACC_SHAPES,
        scratch_shapes=[pltpu.VMEM((DFF, D), bf16), pltpu.VMEM((NCHIP, D, D), bf16), pltpu.SemaphoreType.DMA((2,))],
        compiler_params=_cp(("arbitrary", "arbitrary")),
    )(dy2, u, w_down, w_up, x1, gx, mod, g_pre)


def _matmul_tn(a, b, *, tn, col_blocked, name):
    t, m = a.shape
    n = b.shape[1]
    tmm = min(m, 1024)
    tk = 512
    nk = t // tk

    def body(a_ref, b_ref, o_ref):
        @pl.when(pl.program_id(2) == 0)
        def _():
            o_ref[...] = jnp.zeros_like(o_ref)

        o_ref[...] += _dot_tn(a_ref[...], b_ref[...])

    if col_blocked:
        out_spec = pl.BlockSpec((None, tmm, tn), lambda i, j, k: (j, i, 0))
        out_shape = jax.ShapeDtypeStruct((n // tn, m, tn), f32)
    else:
        out_spec = pl.BlockSpec((tmm, tn), lambda i, j, k: (i, j))
        out_shape = jax.ShapeDtypeStruct((m, n), f32)
    return pl.pallas_call(
        body, name=name, grid=(m // tmm, n // tn, nk),
        in_specs=[pl.BlockSpec((tk, tmm), lambda i, j, k: (k, i)), pl.BlockSpec((tk, tn), lambda i, j, k: (k, j))],
        out_specs=out_spec, out_shape=out_shape,
        compiler_params=_cp(("arbitrary", "arbitrary", "arbitrary")),
    )(a, b)


def _attn_out_bwd(gx1, y, mod, g_post, w_out, oa, ob, g_mix_a, g_mix_b, w1, w4, w16):
    def body(gx_ref, y_ref, mod_ref, gp_ref, w_ref, oa_ref, ob_ref, ga_ref, gb_ref, w1_ref, w4_ref, w16_ref,
             dy_ref, doa_ref, do1_ref, do4_ref, do16_ref, accb_ref, accg_ref, scr):
        _acc_init(accb_ref, accg_ref)
        gx1v = gx_ref[...]
        yn, ry = _rms(y_ref[...])
        gp = gp_ref[...]
        gt = mod_ref[2:3, :]
        dn1 = gx1v * gt
        dy = _rms_bwd(dn1 * gp, yn, ry).astype(bf16)
        dy_ref[...] = dy
        dmixed = _dot_nt(dy, w_ref[...])
        dma, dmb = dmixed[:, :AQ], dmixed[:, AQ:]
        oan, ra = _rms(oa_ref[...])
        obn, rb = _rms(ob_ref[...])
        doa_ref[...] = _rms_bwd(dma * ga_ref[...], oan, ra)
        dob = _rms_bwd(dmb * gb_ref[...], obn, rb)
        do1_ref[...] = w1_ref[...] * dob
        _perm_store(w4_ref[...] * dob, scr, do4_ref, 4)
        _perm_store(w16_ref[...] * dob, scr, do16_ref, 16)
        accb_ref[0:1, :] += _colsum(gx1v * (yn * gp))
        accg_ref[0:1, :] += _colsum(dn1 * yn)
        accg_ref[1:2, :] += jnp.concatenate([_colsum(dma * oan), _colsum(dmb * obn)], axis=1)

    nat = lambda w, dt: jax.ShapeDtypeStruct((BL, SEQ, w), dt)
    return pl.pallas_call(
        body, name="attn_out_bwd", grid=(BL, NJ),
        in_specs=[_tok(D), _tok(D), MOD_SPEC, _full((1, D)), _full((D, D)), _tok(AQ), _tok(BW), _full((1, AQ)), _full((1, BW)),
                  _tok(BW), _tok(BW), _tok(BW)],
        out_specs=[_tok(D), _tok(AQ), _tok(BW), _perm_spec(4, BW), _perm_spec(16, BW), ACCB_SPEC, ACCG_SPEC],
        out_shape=[nat(D, bf16), nat(AQ, f32), nat(BW, f32), jax.ShapeDtypeStruct((BL, 4, SEQ // 4, BW), f32),
                   jax.ShapeDtypeStruct((BL, 16, SEQ // 16, BW), f32)] + ACC_SHAPES,
        scratch_shapes=[pltpu.VMEM((BW // LANES, TM, LANES), f32)],
        compiler_params=_cp(("arbitrary", "arbitrary")),
    )(gx1, y, mod, g_post, w_out, oa, ob, g_mix_a, g_mix_b, w1, w4, w16)


def _attn_in_bwd(dqa, dka, dva, d1, d4, d16, tc, ts1, ts2, w_in, x, gx1, mod, g_pre):
    def body(dqa_ref, dka_ref, dva_ref, dq1_ref, dk1_ref, dv1_ref, dq4_ref, dk4_ref, dv4_ref, dq16_ref, dk16_ref, dv16_ref,
             c_ref, s1_ref, s2_ref, w_ref, x_ref, gx_ref, mod_ref, g_ref, dproj_ref, dx_ref, accb_ref, accg_ref, scr):
        _acc_init(accb_ref, accg_ref)
        c, s1, s2 = c_ref[...], s1_ref[...], s2_ref[...]
        tot = lambda r1, r4, r16: r1[...] + _perm_load(r4, scr, 4) + _perm_load(r16, scr, 16)
        dqb = tot(dq1_ref, dq4_ref, dq16_ref)
        dkb = tot(dk1_ref, dk4_ref, dk16_ref)
        dvb = tot(dv1_ref, dv4_ref, dv16_ref)
        dproj = jnp.concatenate([
            _rope_t(dqa_ref[...], c, s1, s2) * 0.125, _rope_t(dka_ref[...], c, s1, s2), dva_ref[...],
            _rope_t(dqb, c, s1, s2) * 0.125, _rope_t(dkb, c, s1, s2), dvb], axis=1).astype(bf16)
        dproj_ref[...] = dproj
        dh = _dot_nt(dproj, w_ref[...])
        xn, r = _rms(x_ref[...])
        g = g_ref[...]
        dn = dh * (1.0 + mod_ref[1:2, :])
        dx_ref[...] = gx_ref[...] + _rms_bwd(dn * g, xn, r)
        accb_ref[0:1, :] += _colsum(dh * (xn * g))
        accb_ref[1:2, :] += _colsum(dh)
        accg_ref[0:1, :] += _colsum(dn * xn)

    return pl.pallas_call(
        body, name="attn_in_bwd", grid=(BL, NJ),
        in_specs=[_tok(AQ), _tok(AKV), _tok(AKV)] + [_tok(BW)] * 3 + [_perm_spec(4, BW)] * 3 + [_perm_spec(16, BW)] * 3
                 + [_tok(LANES)] * 3 + [_full((D, INW)), _tok(D), _tok(D), MOD_SPEC, _full((1, D))],
        out_specs=[_tok(INW), _tok(D), ACCB_SPEC, ACCG_SPEC],
        out_shape=[jax.ShapeDtypeStruct((BL, SEQ, INW), bf16), jax.ShapeDtypeStruct((BL, SEQ, D), f32)] + ACC_SHAPES,
        scratch_shapes=[pltpu.VMEM((BW // LANES, TM, LANES), f32)],
        compiler_params=_cp(("arbitrary", "arbitrary")),
    )(dqa, dka, dva, *d1, *d4, *d16, tc, ts1, ts2, w_in, x, gx1, mod, g_pre)


def _local_step(x, positions, mod, target, w_in, later_weights, grad_ready, g_attn_pre, g_attn_post, sink_a, g_mix_a, g_mix_b,
                g_mlp_pre, g_mlp_post):
    inv = np.float32(THETA) ** (-np.arange(0, ROT, 2, dtype=np.float32) / np.float32(ROT))
    lane = np.arange(LANES) % HD
    inv_lane = jnp.asarray(np.where(lane < ROT, inv[lane % (ROT // 2)], 0.0).astype(np.float32)[None, :])
    tabs = _rope_tables(positions.reshape(BL * SEQ, 1), inv_lane)
    tc, ts1, ts2 = [t.reshape(BL, SEQ, LANES) for t in tabs]

    (h, qa, ka, va, q1, k1, v1, q4, k4, v4, q16, k16, v16) = _attn_in(x, mod, g_attn_pre, w_in, tc, ts1, ts2)
    seqs = lambda t: t.reshape(t.shape[0] * t.shape[1], t.shape[2], t.shape[3])
    q4, k4, v4, q16, k16, v16 = [seqs(t) for t in (q4, k4, v4, q16, k16, v16)]
    oa, la = _attn_fwd(qa, ka, va, sink_a, max_dist=BLK - 1, name="attn_a_fwd")
    o1, l1 = _attn_fwd(q1, k1, v1, None, max_dist=BLK, name="attn_b1_fwd")
    o4, l4 = _attn_fwd(q4, k4, v4, None, max_dist=BLK, name="attn_b4_fwd")
    o16, l16 = _attn_fwd(q16, k16, v16, None, max_dist=BLK, name="attn_b16_fwd")
    b4 = lambda t: t.reshape(BL, 4, SEQ // 4, BW)
    b16 = lambda t: t.reshape(BL, 16, SEQ // 16, BW)
    w_out, w_up, w_down = later_weights((oa, o1, o4, o16))
    x1, y, mixed, ob, ob4, ob16, w1, w4, w16 = _mix_out(oa, o1, l1, b4(o4), b4(l4), b16(o16), b16(l16), g_mix_a, g_mix_b,
                                                        w_out, x, mod, g_attn_post)
    h2, u, a = _mlp_up(x1, mod, g_mlp_pre, w_up)
    gx, dy2, accb_d, accg_d = _mlp_down(a, w_down, x1, target, mod, g_mlp_post)

    flat = lambda t: t.reshape(BL * SEQ, t.shape[-1])
    mod = mod + grad_ready("w_down", _matmul_tn(flat(a), flat(dy2), tn=D, col_blocked=False, name="grad_w_down"))
    du, gx1, accb_m, accg_m = _mlp_bwd(dy2, u, w_down, w_up, x1, gx, mod, g_mlp_pre)
    mod = mod + grad_ready("w_up", _matmul_tn(flat(h2), flat(du), tn=D, col_blocked=True, name="grad_w_up"))

    dy, doa, do1, do4, do16, accb_o, accg_o = _attn_out_bwd(gx1, y, mod, g_attn_post, w_out, oa, ob, g_mix_a, g_mix_b, w1, w4, w16)
    gw_out = _matmul_tn(flat(mixed), flat(dy), tn=D, col_blocked=False, name="grad_w_out")
    dqa, dka, dva, dsink = _attn_bwd(qa, ka, va, doa, oa, la, sink_a, max_dist=BLK - 1, name="attn_a_bwd")
    d1 = _attn_bwd(q1, k1, v1, do1, ob, l1, None, max_dist=BLK, name="attn_b1_bwd")
    d4 = _attn_bwd(q4, k4, v4, seqs(do4), seqs(ob4), l4, None, max_dist=BLK, name="attn_b4_bwd")
    d16 = _attn_bwd(q16, k16, v16, seqs(do16), seqs(ob16), l16, None, max_dist=BLK, name="attn_b16_bwd")
    dproj, grad_x, accb_i, accg_i = _attn_in_bwd(dqa, dka, dva, d1, [b4(t) for t in d4], [b16(t) for t in d16],
                                                 tc, ts1, ts2, w_in, x, gx1, mod, g_attn_pre)
    gw_in = _matmul_tn(flat(h), flat(dproj), tn=INW, col_blocked=False, name="grad_w_in")
    dsink = dsink + grad_ready("w_in_w_out", (gw_in, gw_out))

    return grad_x, (accb_i, accb_o, accb_m, accb_d, accg_i, accg_o, accg_m, accg_d, dsink)


ADAW = NMOD * D // NCHIP


def _pos():
    return lax.axis_index("x"), lax.axis_index("y"), lax.axis_index("c")


def _flip(v, bit):
    return 1 - v if bit else v


def _all_peers(x, y, c):
    return [(_flip(x, k >> 2 & 1), _flip(y, k >> 1 & 1), _flip(c, k & 1)) for k in range(1, NDEV)]


def _other_chips(x, y):
    return [(1 - x, y), (x, 1 - y), (1 - x, 1 - y)]


def _rcopy(src, dst, send, recv, k, dev):
    return pltpu.make_async_remote_copy(src_ref=src, dst_ref=dst, send_sem=send.at[k], recv_sem=recv.at[k],
                                        device_id=dev, device_id_type=MESH)


def _gather_small(src, buf, send, recv):
    x, y, c = _pos()
    me = 4 * x + 2 * y + c
    peers = _all_peers(x, y, c)
    sends = [_rcopy(src, buf.at[me], send, recv, k, p) for k, p in enumerate(peers)]
    for cp in sends:
        cp.start()
    for k, (px, py, pc) in enumerate(peers):
        _rcopy(src, buf.at[4 * px + 2 * py + pc], send, recv, k, (px, py, pc)).wait_recv()
    for cp in sends:
        cp.wait_send()
    return me


def _ada_fwd(c_in, w_ada, b_cols):
    def body(c_ref, w_ref, b_ref, mod_ref, cond_ref, cbuf, mbuf, s1, r1, s2, r2):
        x, y, c = _pos()
        chip = 2 * x + y
        me = _gather_small(c_ref, cbuf, s1, r1)
        cbuf[me] = c_ref[...]
        for i in range(NDEV):
            cond_ref[BL * i:BL * (i + 1), :] = cbuf[i]
        call = cond_ref[...]
        cond = call / (1.0 + jnp.exp(-call))
        cond_ref[...] = cond
        mbuf[chip] = jnp.dot(cond, w_ref[...], preferred_element_type=f32, precision=lax.Precision.HIGHEST) + b_ref[...]
        chips = _other_chips(x, y)
        sends = [_rcopy(mbuf.at[chip], mbuf.at[chip], s2, r2, j, (px, py, c)) for j, (px, py) in enumerate(chips)]
        for cp in sends:
            cp.start()
        for j, (px, py) in enumerate(chips):
            _rcopy(mbuf.at[chip], mbuf.at[2 * px + py], s2, r2, j, (px, py, c)).wait_recv()
        for cp in sends:
            cp.wait_send()
        row = lax.broadcasted_iota(jnp.int32, (BL * NDEV, ADAW), 0)
        for s in range(NCHIP):
            slab = mbuf[s]
            for j in range(BL):
                mod_ref[j:j + 1, ADAW * s:ADAW * (s + 1)] = jnp.sum(jnp.where(row == BL * me + j, slab, 0.0), axis=0, keepdims=True)

    vm = pl.BlockSpec(memory_space=pltpu.VMEM)
    return pl.pallas_call(
        body, name="ada_fwd", in_specs=[vm, vm, vm], out_specs=[vm, vm],
        out_shape=[jax.ShapeDtypeStruct((BL, NMOD * D), f32), jax.ShapeDtypeStruct((BL * NDEV, D), f32)],
        scratch_shapes=[pltpu.VMEM((NDEV, BL, D), f32), pltpu.VMEM((NCHIP, BL * NDEV, ADAW), f32),
                        pltpu.SemaphoreType.DMA((NDEV - 1,)), pltpu.SemaphoreType.DMA((NDEV - 1,)),
                        pltpu.SemaphoreType.DMA((NCHIP - 1,)), pltpu.SemaphoreType.DMA((NCHIP - 1,))],
        compiler_params=pltpu.CompilerParams(vmem_limit_bytes=VMEM_LIMIT),
    )(c_in, w_ada, b_cols)


def _small_allreduce(accs, cond_all, after=()):
    na = len(after)

    def body(bi, bo, bm, bd, gi, go, gm, gd, dsink, cond_ref, *rest):
        gw_ref, gb_ref, small_ref, pay, pbuf, dall, s1, r1 = rest[na:]
        x, y, c = _pos()
        chip = 2 * x + y
        pay[...] = jnp.zeros_like(pay)
        for b in range(BL):
            for k, (ref, r) in enumerate(((bi, 1), (bi, 0), (bo, 0), (bm, 1), (bm, 0), (bd, 0))):
                pay[b:b + 1, D * k:D * (k + 1)] = ref[b, r:r + 1, :]
        for off, ref, r in ((OFF_G_ATTN_PRE, gi, 0), (OFF_G_ATTN_POST, go, 0), (OFF_G_MIX_A, go, 1), (OFF_G_MLP_PRE, gm, 0),
                            (OFF_G_MLP_POST, gd, 0)):
            pay[BL:BL + 1, off:off + D] = ref[r:r + 1, :]
        eye = lax.broadcasted_iota(jnp.int32, (8, LANES), 0) == lax.broadcasted_iota(jnp.int32, (8, LANES), 1)
        pay[BL:BL + 1, OFF_SINK:OFF_SINK + LANES] = jnp.sum(jnp.where(eye, dsink[...], 0.0), axis=0, keepdims=True)
        pay[BL:BL + 1, OFF_LOSS:OFF_LOSS + LANES] = gd[1:2, 0:LANES]
        me = _gather_small(pay, pbuf, s1, r1)
        pbuf[me] = pay[...]
        small = pbuf[0, BL:BL + 1, :]
        for i in range(1, NDEV):
            small = small + pbuf[i, BL:BL + 1, :]
        small_ref[...] = small
        for i in range(NDEV):
            dall[BL * i:BL * (i + 1), :] = pbuf[i, 0:BL, :]
        gb_ref[...] = jnp.sum(dall[...], axis=0, keepdims=True)
        cols = jnp.zeros((BL * NDEV, ADAW), f32)
        for s in range(NCHIP):
            cols = cols + jnp.where(chip == s, dall[:, ADAW * s:ADAW * (s + 1)], 0.0)
        gw_ref[...] = lax.dot_general(cond_ref[...], cols, (((0,), (0,)), ((), ())), preferred_element_type=f32,
                                      precision=lax.Precision.HIGHEST)

    vm = pl.BlockSpec(memory_space=pltpu.VMEM)
    return pl.pallas_call(
        body, name="small_allreduce", in_specs=[vm] * 10 + [pl.BlockSpec(memory_space=pl.ANY)] * na, out_specs=[vm] * 3,
        out_shape=[jax.ShapeDtypeStruct((D, ADAW), f32), jax.ShapeDtypeStruct((1, PAYW), f32), jax.ShapeDtypeStruct((1, PAYW), f32)],
        scratch_shapes=[pltpu.VMEM((4, PAYW), f32), pltpu.VMEM((NDEV, 4, PAYW), f32), pltpu.VMEM((BL * NDEV, PAYW), f32),
                        pltpu.SemaphoreType.DMA((NDEV - 1,)), pltpu.SemaphoreType.DMA((NDEV - 1,))],
        compiler_params=pltpu.CompilerParams(vmem_limit_bytes=VMEM_LIMIT),
    )(*accs, cond_all, *after)


def _half(ref, c):
    r2 = ref.shape[0] // 2
    return ref.at[pl.ds(pl.multiple_of(c * r2, 16), r2), :]


HBM_SPEC = pl.BlockSpec(memory_space=pltpu.HBM)
SEM_SPEC = pl.BlockSpec(memory_space=pltpu.SEMAPHORE)
EFFECT = pltpu.SideEffectType.DATAFLOW_SIDE_EFFECTING
NLINK = NCHIP - 1


def _in_hbm(a):
    return pltpu.with_memory_space_constraint(a, pltpu.HBM)


def _split_start(name, srcs, land_shapes, builds, after=()):
    n = len(srcs)

    na = len(after)

    def body(*refs):
        src, land, token = refs[:n], refs[n:2 * n], refs[-1]
        send, recv = refs[2 * n + na:3 * n + na], refs[3 * n + na:4 * n + na]
        for t in range(n):
            for out_cp, _ in builds[t](src[t], land[t], send[t], recv[t]):
                out_cp.start()
        token[...] = jnp.zeros_like(token)

    lands = [_in_hbm(lax.empty(s.shape, s.dtype)) for s in land_shapes]
    sems = [pltpu.SemaphoreType.DMA((NLINK,))] * (2 * n)
    thru = [pltpu.HBM(a.shape, a.dtype) for a in list(srcs) + lands]
    res = pl.pallas_call(
        body, name=name, out_shape=sems + thru + [jax.ShapeDtypeStruct((8, LANES), f32)],
        in_specs=[HBM_SPEC] * (2 * n) + [pl.BlockSpec(memory_space=pl.ANY)] * na,
        out_specs=[SEM_SPEC] * (2 * n) + [HBM_SPEC] * (2 * n) + [pl.BlockSpec(memory_space=pltpu.VMEM)],
        input_output_aliases={i: 2 * n + i for i in range(2 * n)},
        compiler_params=pltpu.CompilerParams(has_side_effects=EFFECT),
    )(*[_in_hbm(a) for a in srcs], *lands, *after)
    flight = [(res[2 * n + t], res[3 * n + t], res[t], res[n + t]) for t in range(n)]
    return flight, res[-1][0, 0]


def _split_wait(name, flight, builds, after):
    m = len(flight)
    na = len(after)

    def body(*refs):
        src, land, send, recv = refs[:m], refs[m:2 * m], refs[2 * m:3 * m], refs[3 * m:4 * m]
        for t in range(m):
            for out_cp, in_cp in builds[t](src[t], land[t], send[t], recv[t]):
                out_cp.wait_send()
                in_cp.wait_recv()

    ops = [f[0] for f in flight] + [f[1] for f in flight] + [f[2] for f in flight] + [f[3] for f in flight]
    res = pl.pallas_call(
        body, name=name, out_shape=[pltpu.HBM(a.shape, a.dtype) for a in ops[:2 * m]],
        in_specs=[HBM_SPEC] * (2 * m) + [SEM_SPEC] * (2 * m) + [pl.BlockSpec(memory_space=pl.ANY)] * na,
        out_specs=[HBM_SPEC] * (2 * m), input_output_aliases={i: i for i in range(2 * m)},
        compiler_params=pltpu.CompilerParams(has_side_effects=EFFECT),
    )(*ops, *after)
    return res[:m], res[m:2 * m]


def _weight_copies(src, land, send, recv):
    x, y, c = _pos()
    chip = 2 * x + y
    return [(_rcopy(_half(src, c), _half(land.at[chip], c), send, recv, j, (px, py, c)),
             _rcopy(_half(src, c), _half(land.at[2 * px + py], c), send, recv, j, (px, py, c)))
            for j, (px, py) in enumerate(_other_chips(x, y))]


def _grad_copies(src, land, send, recv):
    x, y, c = _pos()
    return [(_rcopy(src.at[2 * px + py], land.at[j], send, recv, j, (px, py, c)),
             _rcopy(src.at[2 * px + py], land.at[j], send, recv, j, (px, py, c)))
            for j, (px, py) in enumerate(_other_chips(x, y))]


def _pair_forward(shards, gathered, name):
    nt = len(shards)

    def body(*refs):
        sh, gin, gout = refs[:nt], refs[nt:2 * nt], refs[2 * nt:3 * nt]
        send, recv = refs[3 * nt:]
        x, y, c = _pos()
        chip = 2 * x + y
        sib = (x, y, 1 - c)
        chips = _other_chips(x, y)
        cps = []
        for t in range(nt):
            for j, (px, py) in enumerate(chips):
                cps.append(_rcopy(_half(gin[t].at[2 * px + py], c), _half(gout[t].at[2 * px + py], c), send, recv, 4 * t + j, sib))
            cps.append(_rcopy(sh[t], gout[t].at[chip], send, recv, 4 * t + 3, sib))
        for cp in cps:
            cp.start()
        for t in range(nt):
            for j, (px, py) in enumerate(chips):
                theirs = _half(gout[t].at[2 * px + py], 1 - c)
                _rcopy(theirs, theirs, send, recv, 4 * t + j, sib).wait_recv()
            _rcopy(sh[t], gout[t].at[chip], send, recv, 4 * t + 3, sib).wait_recv()
        for cp in cps:
            cp.wait_send()

    hbm = pl.BlockSpec(memory_space=pl.ANY)
    return pl.pallas_call(
        body, name=name, in_specs=[hbm] * (2 * nt), out_specs=[hbm] * nt,
        out_shape=[jax.ShapeDtypeStruct(g.shape, g.dtype) for g in gathered],
        input_output_aliases={nt + t: t for t in range(nt)},
        scratch_shapes=[pltpu.SemaphoreType.DMA((4 * nt,)), pltpu.SemaphoreType.DMA((4 * nt,))],
    )(*shards, *gathered)


def _rs_pair(grads, name):
    nt = len(grads)

    def body(*refs):
        ins, outs = refs[:nt], refs[nt:2 * nt]
        send, recv = refs[2 * nt:]
        x, y, c = _pos()
        sib = (x, y, 1 - c)
        cps = []
        for t in range(nt):
            r2 = ins[t].shape[1] // 2
            src = ins[t].at[:, pl.ds(pl.multiple_of((1 - c) * r2, 8), r2), :]
            cps.append(_rcopy(src, outs[t], send, recv, t, sib))
        for cp in cps:
            cp.start()
        for cp in cps:
            cp.wait()

    hbm = pl.BlockSpec(memory_space=pl.ANY)
    return pl.pallas_call(
        body, name=name, in_specs=[hbm] * nt, out_specs=[hbm] * nt,
        out_shape=[jax.ShapeDtypeStruct((NCHIP, g.shape[1] // 2, g.shape[2]), f32) for g in grads],
        scratch_shapes=[pltpu.SemaphoreType.DMA((nt,)), pltpu.SemaphoreType.DMA((nt,))],
    )(*grads)


RS_ROWS = 128


def _pair_add(g, landed, c_arr, name):
    _, r2, cw = landed.shape
    nr = r2 // RS_ROWS

    def body(c_ref, g_ref, p_ref, o_ref):
        o_ref[...] = (g_ref[...] + p_ref[...]).astype(bf16)

    gs = pltpu.PrefetchScalarGridSpec(
        num_scalar_prefetch=1, grid=(NCHIP, nr),
        in_specs=[pl.BlockSpec((None, RS_ROWS, cw), lambda s, j, c: (s, c[0] * nr + j, 0)),
                  pl.BlockSpec((None, RS_ROWS, cw), lambda s, j, c: (s, j, 0))],
        out_specs=pl.BlockSpec((None, RS_ROWS, cw), lambda s, j, c: (s, j, 0)))
    return pl.pallas_call(body, name=name, grid_spec=gs, out_shape=jax.ShapeDtypeStruct((NCHIP, r2, cw), bf16),
                          compiler_params=_cp(("arbitrary", "arbitrary")))(c_arr, g, landed)


def _chip_add(half, landed, pos_arr, name):
    _, r2, cw = half.shape
    nr = r2 // RS_ROWS

    def body(s_ref, h_ref, q_ref, o_ref):
        acc = h_ref[...].astype(f32)
        for j in range(NCHIP - 1):
            acc = acc + q_ref[j].astype(f32)
        o_ref[...] = acc

    gs = pltpu.PrefetchScalarGridSpec(
        num_scalar_prefetch=1, grid=(nr,),
        in_specs=[pl.BlockSpec((None, RS_ROWS, cw), lambda j, s: (s[0], j, 0)),
                  pl.BlockSpec((NCHIP - 1, RS_ROWS, cw), lambda j, s: (0, j, 0))],
        out_specs=pl.BlockSpec((RS_ROWS, cw), lambda j, s: (s[1] * nr + j, 0)))
    return pl.pallas_call(body, name=name, grid_spec=gs, out_shape=jax.ShapeDtypeStruct((2 * r2, cw), f32),
                          compiler_params=_cp(("arbitrary",)))(pos_arr, half, landed)


def _ag_pair(fulls, name):
    nt = len(fulls)

    def body(*refs):
        ins, outs = refs[:nt], refs[nt:2 * nt]
        send, recv = refs[2 * nt:]
        x, y, c = _pos()
        sib = (x, y, 1 - c)
        cps = [_rcopy(_half(ins[t], c), _half(outs[t], c), send, recv, t, sib) for t in range(nt)]
        for cp in cps:
            cp.start()
        for t in range(nt):
            _rcopy(_half(ins[t], c), _half(outs[t], 1 - c), send, recv, t, sib).wait_recv()
        for cp in cps:
            cp.wait_send()

    hbm = pl.BlockSpec(memory_space=pl.ANY)
    return pl.pallas_call(
        body, name=name, in_specs=[hbm] * nt, out_specs=[hbm] * nt,
        out_shape=[jax.ShapeDtypeStruct(a.shape, f32) for a in fulls],
        input_output_aliases={t: t for t in range(nt)},
        scratch_shapes=[pltpu.SemaphoreType.DMA((nt,)), pltpu.SemaphoreType.DMA((nt,))],
    )(*fulls)


def _adamw_math(w, g, m, v):
    m = B1 * m + (1.0 - B1) * g
    v = B2 * v + (1.0 - B2) * jnp.square(g)
    m_hat = m / (1.0 - B1 ** STEP)
    v_hat = v / (1.0 - B2 ** STEP)
    return -LR * (m_hat / (jnp.sqrt(v_hat) + AEPS) + WD * w), m, v


ADAM_ROWS = 128


def _adamw(w, g, m, v, name):
    r, cw = w.shape

    def body(w_ref, g_ref, m_ref, v_ref, d_ref, mo_ref, vo_ref):
        d_ref[...], mo_ref[...], vo_ref[...] = _adamw_math(w_ref[...], g_ref[...], m_ref[...], v_ref[...])

    spec = pl.BlockSpec((ADAM_ROWS, cw), lambda i: (i, 0))
    return pl.pallas_call(body, name=name, grid=(r // ADAM_ROWS,), in_specs=[spec] * 4, out_specs=[spec] * 3,
                          out_shape=[jax.ShapeDtypeStruct((r, cw), f32)] * 3, compiler_params=_cp(("arbitrary",)))(w, g, m, v)


SMALL = (("b_ada", None, PAYW), ("g_attn_pre", OFF_G_ATTN_PRE, D), ("g_attn_post", OFF_G_ATTN_POST, D), ("sink_a", OFF_SINK, 8),
         ("g_mix_a", OFF_G_MIX_A, AQ), ("g_mix_b", OFF_G_MIX_B, BW), ("g_mlp_pre", OFF_G_MLP_PRE, D), ("g_mlp_post", OFF_G_MLP_POST, D))


def _adamw_small(small, gb, params):
    n = len(SMALL)

    def body(*refs):
        small_ref, gb_ref = refs[:2]
        wmv = refs[2:2 + 3 * n]
        loss_ref = refs[2 + 3 * n]
        outs = refs[3 + 3 * n:]
        loss_ref[...] = small_ref[:, OFF_LOSS:OFF_LOSS + 1] * (0.5 / D)
        for i, (_, off, width) in enumerate(SMALL):
            g = gb_ref[...] if off is None else small_ref[:, off:off + width]
            w_ref, m_ref, v_ref = wmv[3 * i:3 * i + 3]
            outs[4 * i][...] = g
            outs[4 * i + 1][...], outs[4 * i + 2][...], outs[4 * i + 3][...] = _adamw_math(w_ref[...], g, m_ref[...], v_ref[...])

    vm = pl.BlockSpec(memory_space=pltpu.VMEM)
    out_shape = [jax.ShapeDtypeStruct((1, 1), f32)]
    for _, _, width in SMALL:
        out_shape += [jax.ShapeDtypeStruct((1, width), f32)] * 4
    flat = [a for wmv in params for a in wmv]
    res = pl.pallas_call(body, name="adamw_small", in_specs=[vm] * (2 + 3 * n), out_specs=[vm] * len(out_shape),
                         out_shape=out_shape)(small, gb, *flat)
    return res[0], {name: res[1 + 4 * i:5 + 4 * i] for i, (name, _, _) in enumerate(SMALL)}


def kernel(x, c, positions, w_ada, b_ada, g_attn_pre, g_attn_post, w_in, sink_a, g_mix_a, g_mix_b, w_out, g_mlp_pre, g_mlp_post, w_up, w_down, loss_target, m_w_ada, m_b_ada, m_g_attn_pre, m_g_attn_post, m_w_in, m_sink_a, m_g_mix_a, m_g_mix_b, m_w_out, m_g_mlp_pre, m_g_mlp_post, m_w_up, m_w_down, v_w_ada, v_b_ada, v_g_attn_pre, v_g_attn_post, v_w_in, v_sink_a, v_g_mix_a, v_g_mix_b, v_w_out, v_g_mlp_pre, v_g_mlp_post, v_w_up, v_w_down):
    given = dict(w_ada=w_ada, b_ada=b_ada, g_attn_pre=g_attn_pre, g_attn_post=g_attn_post, w_in=w_in, sink_a=sink_a, g_mix_a=g_mix_a,
                 g_mix_b=g_mix_b, w_out=w_out, g_mlp_pre=g_mlp_pre, g_mlp_post=g_mlp_post, w_up=w_up, w_down=w_down)
    moms = dict(w_ada=(m_w_ada, v_w_ada), b_ada=(m_b_ada, v_b_ada), g_attn_pre=(m_g_attn_pre, v_g_attn_pre),
                g_attn_post=(m_g_attn_post, v_g_attn_post), w_in=(m_w_in, v_w_in), sink_a=(m_sink_a, v_sink_a),
                g_mix_a=(m_g_mix_a, v_g_mix_a), g_mix_b=(m_g_mix_b, v_g_mix_b), w_out=(m_w_out, v_w_out),
                g_mlp_pre=(m_g_mlp_pre, v_g_mlp_pre), g_mlp_post=(m_g_mlp_post, v_g_mlp_post), w_up=(m_w_up, v_w_up),
                w_down=(m_w_down, v_w_down))
    order = ["w_ada", "b_ada", "g_attn_pre", "g_attn_post", "w_in", "sink_a", "g_mix_a", "g_mix_b", "w_out", "g_mlp_pre",
             "g_mlp_post", "w_up", "w_down"]
    xi, yi, ci = _pos()
    chip = 2 * xi + yi

    c_arr = jnp.reshape(ci, (1,)).astype(jnp.int32)
    pos_arr = jnp.stack([chip, ci]).astype(jnp.int32)
    big = ("w_in", "w_out", "w_up", "w_down")

    shards = [given[n][0].astype(bf16) for n in big]
    gathered = [jax.ShapeDtypeStruct((NCHIP,) + s.shape, bf16) for s in shards]
    flight_in, tok = _split_start("weights_start_first", shards[:1], gathered[:1], [_weight_copies])
    b_cols = lax.dynamic_slice(b_ada, (0, chip * ADAW), (1, ADAW))
    mod, cond_all = _ada_fwd(c + tok, w_ada[0], b_cols)
    flight_rest, tok = _split_start("weights_start_rest", shards[1:], gathered[1:], [_weight_copies] * 3, after=(mod,))
    mod = mod.reshape(BL, NMOD, D) + tok
    srcs, lands = _split_wait("weights_wait_first", flight_in, [_weight_copies], (mod,))
    (win_g,) = _pair_forward(srcs, lands, "weights_pair_first")
    w_in_full = win_g.transpose(1, 0, 2).reshape(D, INW)

    def later_weights(after):
        srcs, lands = _split_wait("weights_wait_rest", flight_rest, [_weight_copies] * 3, after)
        wout_g, wup_g, wdn_g = _pair_forward(srcs, lands, "weights_pair_rest")
        return wout_g.reshape(D, D), wup_g, wdn_g.reshape(DFF, D)

    pending = {}

    def grad_ready(group, g):
        if group == "w_down":
            names, slabs = ("w_down",), [g.reshape(NCHIP, DFF // NCHIP, D)]
        elif group == "w_up":
            names, slabs = ("w_up",), [g]
        else:
            names = ("w_in", "w_out")
            slabs = [g[0].reshape(D, NCHIP, INW // NCHIP).transpose(1, 0, 2), g[1].reshape(NCHIP, D // NCHIP, D)]
        landed = _rs_pair(slabs, "grad_pair_exchange_" + group)
        halves = [_pair_add(s, p, c_arr, "grad_pair_sum_" + n) for s, p, n in zip(slabs, landed, names)]
        fl, tk = _split_start("grad_start_" + group, halves,
                              [jax.ShapeDtypeStruct((NLINK,) + h.shape[1:], bf16) for h in halves], [_grad_copies] * len(names))
        pending[group] = (names, fl)
        return tk

    grad_x, accs = _local_step(x, positions, mod, loss_target, w_in_full, later_weights, grad_ready,
                               g_attn_pre, g_attn_post, sink_a, g_mix_a, g_mix_b, g_mlp_pre, g_mlp_post)

    def finish(groups, after):
        names = sum((pending[g][0] for g in groups), ())
        fl = sum((pending[g][1] for g in groups), [])
        halves, landed = _split_wait("grad_wait_" + groups[0], fl, [_grad_copies] * len(names), after)
        fulls = [_chip_add(h, q, pos_arr, "grad_chip_sum_" + n) for h, q, n in zip(halves, landed, names)]
        return dict(zip(names, _ag_pair(fulls, "grad_pair_gather_" + groups[0])))

    grads, out = {}, {}

    def update(n):
        d, m2, v2 = _adamw(given[n][0], grads[n], moms[n][0][0], moms[n][1][0], "adamw_" + n)
        out[n] = (grads[n][None], d[None], m2[None], v2[None])
        return v2

    grads.update(finish(("w_down", "w_up"), (accs[-1],)))
    last = [update(n) for n in ("w_down", "w_up")]
    grads["w_ada"], gb, small = _small_allreduce(accs, cond_all, after=tuple(last))
    update("w_ada")
    grads.update(finish(("w_in_w_out",), (small,)))
    update("w_in")
    update("w_out")
    loss, res = _adamw_small(small, gb, [(given[n], moms[n][0], moms[n][1]) for n, _, _ in SMALL])
    for n, _, _ in SMALL:
        out[n] = tuple(res[n])
    return (loss.reshape(()), grad_x, *[out[n][0] for n in order], *[out[n][1] for n in order],
            *[out[n][2] for n in order], *[out[n][3] for n in order])
```

```python
import functools

import numpy as np
import jax
import jax.numpy as jnp
from jax import lax
from jax.experimental import pallas as pl
from jax.experimental.pallas import tpu as pltpu

f32 = jnp.float32
bf16 = jnp.bfloat16
MESH = pl.DeviceIdType.MESH

D = 1024
SEQ = 2048
BL = 2
HD = 64
AQ = 512
AKV = 128
BW = 512
INW = 2304
DFF = 4096
NMOD = 6
ROT = 16
THETA = 500000.0
EPS = 1e-6
NEG = -1e30
BLK = 128
TM = 256
NJ = SEQ // TM
LANES = 128
NCHIP = 4
NDEV = 8
VMEM_LIMIT = 56 << 20

LR, B1, B2, AEPS, WD, STEP = 0.001, 0.9, 0.999, 1e-08, 0.01, 10

OFF_G_ATTN_PRE, OFF_G_ATTN_POST, OFF_G_MIX_A, OFF_G_MIX_B = 0, 1024, 2048, 2560
OFF_G_MLP_PRE, OFF_G_MLP_POST, OFF_SINK, OFF_LOSS = 3072, 4096, 5120, 5248
PAYW = NMOD * D


def _cp(sem=None):
    return pltpu.CompilerParams(dimension_semantics=sem, vmem_limit_bytes=VMEM_LIMIT)


def _dot(a, b):
    return jnp.dot(a, b, preferred_element_type=f32)


def _dot_nt(a, b):
    return lax.dot_general(a, b, (((1,), (1,)), ((), ())), preferred_element_type=f32)


def _dot_tn(a, b):
    return lax.dot_general(a, b, (((0,), (0,)), ((), ())), preferred_element_type=f32)


def _rms(x):
    r = lax.rsqrt(jnp.mean(x * x, axis=-1, keepdims=True) + EPS)
    return x * r, r


def _rms_bwd(dy, y, r):
    return r * (dy - y * jnp.mean(dy * y, axis=-1, keepdims=True))


def _colsum(v):
    return jnp.sum(v, axis=0, keepdims=True)


def _rope(p, c, s1, s2):
    outs = []
    for c0 in range(0, p.shape[1], LANES):
        pc = p[:, c0:c0 + LANES]
        outs.append(pc * c + pltpu.roll(pc, LANES - ROT // 2, 1) * s1 + pltpu.roll(pc, ROT // 2, 1) * s2)
    return outs[0] if len(outs) == 1 else jnp.concatenate(outs, axis=1)


def _rope_t(g, c, s1, s2):
    outs = []
    for c0 in range(0, g.shape[1], LANES):
        gc = g[:, c0:c0 + LANES]
        outs.append(gc * c + pltpu.roll(gc * s1, ROT // 2, 1) + pltpu.roll(gc * s2, LANES - ROT // 2, 1))
    return outs[0] if len(outs) == 1 else jnp.concatenate(outs, axis=1)


def _perm_store(val, scr, out_ref, d):
    nc = val.shape[1] // LANES
    for c in range(nc):
        scr[c] = val[:, LANES * c:LANES * (c + 1)]
    for c in range(nc):
        for r in range(d):
            out_ref[r, :, LANES * c:LANES * (c + 1)] = scr[c, pl.ds(r, TM // d, stride=d), :].astype(out_ref.dtype)


def _perm_load(in_ref, scr, d):
    nc = in_ref.shape[-1] // LANES
    for c in range(nc):
        for r in range(d):
            scr[c, pl.ds(r, TM // d, stride=d), :] = in_ref[r, :, LANES * c:LANES * (c + 1)].astype(f32)
    return jnp.concatenate([scr[c] for c in range(nc)], axis=1)


def _tok(w, dtype=None):
    return pl.BlockSpec((None, TM, w), lambda b, j: (b, j, 0))


def _perm_spec(d, w):
    return pl.BlockSpec((None, d, TM // d, w), lambda b, j: (b, 0, j, 0))


def _full(shape):
    n = len(shape)
    return pl.BlockSpec(shape, lambda b, j: (0,) * n)


MOD_SPEC = pl.BlockSpec((None, NMOD, D), lambda b, j: (b, 0, 0))
ACCB_SPEC = pl.BlockSpec((None, 8, D), lambda b, j: (b, 0, 0))
ACCG_SPEC = pl.BlockSpec((8, D), lambda b, j: (0, 0))
ACC_SHAPES = [jax.ShapeDtypeStruct((BL, 8, D), f32), jax.ShapeDtypeStruct((8, D), f32)]


def _acc_init(accb_ref, accg_ref):
    b, j = pl.program_id(0), pl.program_id(1)

    @pl.when(j == 0)
    def _():
        accb_ref[...] = jnp.zeros_like(accb_ref)

    @pl.when((b == 0) & (j == 0))
    def _():
        accg_ref[...] = jnp.zeros_like(accg_ref)


def _rope_tables(pos_col, inv_lane):
    def body(p_ref, inv_ref, c_ref, s1_ref, s2_ref):
        ang = p_ref[...].astype(f32) * inv_ref[...]
        j = lax.broadcasted_iota(jnp.int32, (TM, LANES), 1) % HD
        cs, sn = jnp.cos(ang), jnp.sin(ang)
        c_ref[...] = jnp.where(j < ROT, cs, 1.0)
        s1_ref[...] = jnp.where(j < ROT // 2, -sn, 0.0)
        s2_ref[...] = jnp.where((j >= ROT // 2) & (j < ROT), sn, 0.0)

    n = BL * SEQ // TM
    return pl.pallas_call(
        body, name="rope_tables", grid=(n,),
        in_specs=[pl.BlockSpec((TM, 1), lambda i: (i, 0)), pl.BlockSpec((1, LANES), lambda i: (0, 0))],
        out_specs=[pl.BlockSpec((TM, LANES), lambda i: (i, 0))] * 3,
        out_shape=[jax.ShapeDtypeStruct((BL * SEQ, LANES), f32)] * 3,
    )(pos_col, inv_lane)


def _attn_in(x, mod, g_pre, w_in, tc, ts1, ts2):
    def body(x_ref, mod_ref, g_ref, w_ref, c_ref, s1_ref, s2_ref,
             h_ref, qa_ref, ka_ref, va_ref, q1_ref, k1_ref, v1_ref, q4_ref, k4_ref, v4_ref, q16_ref, k16_ref, v16_ref,
             scr):
        xn, _ = _rms(x_ref[...])
        h = (xn * g_ref[...]) * (1.0 + mod_ref[1:2, :]) + mod_ref[0:1, :]
        hb = h.astype(bf16)
        h_ref[...] = hb
        proj = _dot(hb, w_ref[...])
        c, s1, s2 = c_ref[...], s1_ref[...], s2_ref[...]
        o1, o2, o3, o4, o5 = AQ, AQ + AKV, AQ + 2 * AKV, AQ + 2 * AKV + BW, AQ + 2 * AKV + 2 * BW
        qa_ref[...] = (_rope(proj[:, :o1], c, s1, s2) * 0.125).astype(bf16)
        ka_ref[...] = _rope(proj[:, o1:o2], c, s1, s2).astype(bf16)
        va_ref[...] = proj[:, o2:o3].astype(bf16)
        qb = _rope(proj[:, o3:o4], c, s1, s2) * 0.125
        kb = _rope(proj[:, o4:o5], c, s1, s2)
        vb = proj[:, o5:]
        for val, r1, r4, r16 in ((qb, q1_ref, q4_ref, q16_ref), (kb, k1_ref, k4_ref, k16_ref), (vb, v1_ref, v4_ref, v16_ref)):
            r1[...] = val.astype(bf16)
            _perm_store(val, scr, r4, 4)
            _perm_store(val, scr, r16, 16)

    nat = lambda w: jax.ShapeDtypeStruct((BL, SEQ, w), bf16)
    p4 = jax.ShapeDtypeStruct((BL, 4, SEQ // 4, BW), bf16)
    p16 = jax.ShapeDtypeStruct((BL, 16, SEQ // 16, BW), bf16)
    return pl.pallas_call(
        body, name="attn_in", grid=(BL, NJ),
        in_specs=[_tok(D), MOD_SPEC, _full((1, D)), _full((D, INW)), _tok(LANES), _tok(LANES), _tok(LANES)],
        out_specs=[_tok(D), _tok(AQ), _tok(AKV), _tok(AKV)] + [_tok(BW)] * 3 + [_perm_spec(4, BW)] * 3 + [_perm_spec(16, BW)] * 3,
        out_shape=[nat(D), nat(AQ), nat(AKV), nat(AKV)] + [nat(BW)] * 3 + [p4] * 3 + [p16] * 3,
        scratch_shapes=[pltpu.VMEM((BW // LANES, TM, LANES), f32)],
        compiler_params=_cp(("arbitrary", "arbitrary")),
    )(x, mod, g_pre, w_in, tc, ts1, ts2)


def _kv_cat(cur_ref, prev_ref, p, gqa, cache):
    def one(ref):
        if not gqa:
            return ref[:, LANES * p:LANES * (p + 1)]
        k = ref[...]
        kr = pltpu.roll(k, HD, 1)
        lo = lax.broadcasted_iota(jnp.int32, k.shape, 1) < HD
        return jnp.where(lo, k, kr) if p < 2 else jnp.where(lo, kr, k)

    key = (id(cur_ref), p // 2 if gqa else p)
    if key not in cache:
        cache[key] = one(cur_ref) if prev_ref is None else jnp.concatenate([one(prev_ref), one(cur_ref)], axis=0)
    return cache[key]


def _lane_half(a, hh):
    lo = lax.broadcasted_iota(jnp.int32, a.shape, 1) < HD
    return jnp.where(lo, a, jnp.zeros_like(a)) if hh == 0 else jnp.where(lo, jnp.zeros_like(a), a)


def _attn_fwd(q, k, v, sink, *, max_dist, name):
    n, l, w = q.shape
    wk = k.shape[-1]
    nb = l // BLK
    gqa = wk != w
    has_sink = sink is not None

    def body(*refs):
        if has_sink:
            sink_ref, refs = refs[0], refs[1:]
        if nb > 1:
            q_ref, kc_ref, kp_ref, vc_ref, vp_ref, o_ref, lse_ref, sscr, pscr, dscr = refs
        else:
            q_ref, kc_ref, vc_ref, o_ref, lse_ref, sscr, pscr, dscr = refs
        i = pl.program_id(1)
        lo = lax.broadcasted_iota(jnp.int32, (BLK, LANES), 1) < HD
        qi = lax.broadcasted_iota(jnp.int32, (BLK, BLK), 0)
        kj = lax.broadcasted_iota(jnp.int32, (BLK, BLK), 1)
        tri = kj <= qi
        eye = kj == qi
        cache = {}
        for p in range(w // LANES):
            qpair = q_ref[:, LANES * p:LANES * (p + 1)]
            kcat = _kv_cat(kc_ref, kp_ref if nb > 1 else None, p, gqa, cache)
            for hh in range(2):
                s = _dot_nt(_lane_half(qpair, hh), kcat)
                if nb > 1:
                    sp = jnp.where(i > 0, s[:, :BLK], NEG)
                    sscr[2 * p + hh] = jnp.where(tri, s[:, BLK:], sp)
                    if diag:
                        dscr[2 * p + hh] = jnp.where(eye, sp, NEG)
                else:
                    sscr[2 * p + hh] = jnp.where(tri, s, NEG)
        for p in range(w // LANES):
            lse_pair = None
            for hh in range(2):
                h = 2 * p + hh
                comb = sscr[h]
                if diag:
                    dtile = dscr[h]
                    m = jnp.max(jnp.maximum(comb, dtile), axis=-1, keepdims=True)
                else:
                    m = jnp.max(comb, axis=-1, keepdims=True)
                if has_sink:
                    sk = sink_ref[0, h]
                    m = jnp.maximum(m, sk)
                e = jnp.exp(comb - m)
                if diag:
                    ed = jnp.exp(dtile - m)
                    den = jnp.sum(e + ed, axis=-1, keepdims=True)
                else:
                    den = jnp.sum(e, axis=-1, keepdims=True)
                if has_sink:
                    den = den + jnp.exp(sk - m)
                inv = 1.0 / den
                if nb > 1:
                    pscr[h, :, :BLK] = (jnp.where(tri, ed if diag else 0.0, e) * inv).astype(bf16)
                    pscr[h, :, BLK:] = (jnp.where(tri, e, 0.0) * inv).astype(bf16)
                else:
                    pscr[h] = (e * inv).astype(bf16)
                lse_h = jnp.broadcast_to(m + jnp.log(den), (BLK, LANES))
                lse_pair = lse_h if hh == 0 else jnp.where(lo, lse_pair, lse_h)
            lse_ref[:, LANES * p:LANES * (p + 1)] = lse_pair
        for p in range(w // LANES):
            vcat = _kv_cat(vc_ref, vp_ref if nb > 1 else None, p, gqa, cache)
            key = ("halves", id(vc_ref), p // 2 if gqa else p)
            if key not in cache:
                cache[key] = (_lane_half(vcat, 0), _lane_half(vcat, 1))
            o_ref[:, LANES * p:LANES * (p + 1)] = _dot(pscr[2 * p], cache[key][0]) + _dot(pscr[2 * p + 1], cache[key][1])

    assert max_dist in (BLK - 1, BLK)
    diag = nb > 1 and max_dist == BLK
    cur = lambda ww: pl.BlockSpec((None, BLK, ww), lambda a, i: (a, i, 0))
    prev = lambda ww: pl.BlockSpec((None, BLK, ww), lambda a, i: (a, jnp.maximum(i - 1, 0), 0))
    in_specs = [cur(w), cur(wk)] + ([prev(wk)] if nb > 1 else []) + [cur(wk)] + ([prev(wk)] if nb > 1 else [])
    args = [q, k] + ([k] if nb > 1 else []) + [v] + ([v] if nb > 1 else [])
    if has_sink:
        in_specs = [pl.BlockSpec(memory_space=pltpu.SMEM)] + in_specs
        args = [sink] + args
    return pl.pallas_call(
        body, name=name, grid=(n, nb), in_specs=in_specs,
        out_specs=[cur(w), cur(w)], out_shape=[jax.ShapeDtypeStruct((n, l, w), f32)] * 2,
        scratch_shapes=[pltpu.VMEM((w // HD, BLK, BLK), f32), pltpu.VMEM((w // HD, BLK, 2 * BLK if nb > 1 else BLK), bf16),
                        pltpu.VMEM((w // HD if diag else 1, BLK, BLK), f32)],
        compiler_params=_cp(("arbitrary", "arbitrary")),
    )(*args)


def _attn_bwd(q, k, v, do, o, lse, sink, *, max_dist, name):
    n, l, w = q.shape
    wk = k.shape[-1]
    nb = l // BLK
    gqa = wk != w
    has_sink = sink is not None

    def body(*refs):
        if has_sink:
            sink_ref, refs = refs[0], refs[1:]
        if nb > 1:
            q_ref, kc_ref, kp_ref, vc_ref, vp_ref, do_ref, o_ref, lse_ref = refs[:8]
            rest = refs[8:]
        else:
            q_ref, kc_ref, vc_ref, do_ref, o_ref, lse_ref = refs[:6]
            rest = refs[6:]
        if has_sink:
            dq_ref, dk_ref, dv_ref, dsink_ref = rest[:4]
            rest = rest[4:]
        else:
            dq_ref, dk_ref, dv_ref = rest[:3]
            rest = rest[3:]
        step = pl.program_id(1)
        blk_idx = nb - 1 - step
        if nb > 1:
            ck, cv = rest[:2]
            rest = rest[2:]

            @pl.when(step == 0)
            def _():
                ck[...] = jnp.zeros_like(ck)
                cv[...] = jnp.zeros_like(cv)

        sscr, dpscr, pscr, dsscr = rest[:4]
        if diag:
            dscr, ddscr = rest[4:]
        if has_sink:
            @pl.when((pl.program_id(0) == 0) & (step == 0))
            def _():
                dsink_ref[...] = jnp.zeros_like(dsink_ref)

        lane = lax.broadcasted_iota(jnp.int32, (BLK, LANES), 1)
        lo = lane < HD
        qi = lax.broadcasted_iota(jnp.int32, (BLK, BLK), 0)
        kj = lax.broadcasted_iota(jnp.int32, (BLK, BLK), 1)
        tri = kj <= qi
        eye = kj == qi
        cache = {}
        kp, vp = (kp_ref, vp_ref) if nb > 1 else (None, None)
        rows = 2 * BLK if nb > 1 else BLK
        for p in range(w // LANES):
            sl = slice(LANES * p, LANES * (p + 1))
            qpair, dopair = q_ref[:, sl], do_ref[:, sl]
            kcat, vcat = _kv_cat(kc_ref, kp, p, gqa, cache), _kv_cat(vc_ref, vp, p, gqa, cache)
            for hh in range(2):
                h = 2 * p + hh
                s = _dot_nt(_lane_half(qpair, hh), kcat)
                dp = _dot_nt(_lane_half(dopair, hh).astype(bf16), vcat)
                if nb > 1:
                    sp = jnp.where(blk_idx > 0, s[:, :BLK], NEG)
                    sscr[h] = jnp.where(tri, s[:, BLK:], sp)
                    dpscr[h] = jnp.where(tri, dp[:, BLK:], dp[:, :BLK])
                    if diag:
                        dscr[h] = jnp.where(eye, sp, NEG)
                        ddscr[h] = dp[:, :BLK]
                else:
                    sscr[h] = jnp.where(tri, s, NEG)
                    dpscr[h] = dp
        for p in range(w // LANES):
            sl = slice(LANES * p, LANES * (p + 1))
            prod = do_ref[:, sl] * o_ref[:, sl]
            lsepair = lse_ref[:, sl]
            lser = pltpu.roll(lsepair, HD, 1)
            for hh in range(2):
                h = 2 * p + hh
                lse_b = jnp.where(lo, lsepair, lser) if hh == 0 else jnp.where(lo, lser, lsepair)
                delta = jnp.sum(_lane_half(prod, hh), axis=-1, keepdims=True)
                pr = jnp.exp(sscr[h] - lse_b)
                ds = pr * (dpscr[h] - delta)
                if nb > 1:
                    if diag:
                        prd = jnp.exp(dscr[h] - lse_b)
                        dsd = prd * (ddscr[h] - delta)
                    else:
                        prd = dsd = 0.0
                    pscr[h, :, :BLK] = jnp.where(tri, prd, pr).astype(bf16)
                    pscr[h, :, BLK:] = jnp.where(tri, pr, 0.0).astype(bf16)
                    dsscr[h, :, :BLK] = jnp.where(tri, dsd, ds).astype(bf16)
                    dsscr[h, :, BLK:] = jnp.where(tri, ds, 0.0).astype(bf16)
                else:
                    pscr[h] = pr.astype(bf16)
                    dsscr[h] = ds.astype(bf16)
                if has_sink:
                    dsk = -jnp.sum(jnp.where(lane == 0, jnp.exp(sink_ref[0, h] - lse_b) * delta, 0.0), keepdims=True)
                    dsink_ref[h:h + 1, :] += jnp.broadcast_to(dsk, (1, LANES))
        gk = [jnp.zeros((rows, LANES), f32), jnp.zeros((rows, LANES), f32)]
        gv = [jnp.zeros((rows, LANES), f32), jnp.zeros((rows, LANES), f32)]
        for p in range(w // LANES):
            sl = slice(LANES * p, LANES * (p + 1))
            qpair, dopair = q_ref[:, sl], do_ref[:, sl]
            kcat = _kv_cat(kc_ref, kp, p, gqa, cache)
            key = ("halves", p // 2 if gqa else p)
            if key not in cache:
                cache[key] = (_lane_half(kcat, 0), _lane_half(kcat, 1))
            dq_ref[:, sl] = _dot(dsscr[2 * p], cache[key][0]) + _dot(dsscr[2 * p + 1], cache[key][1])
            dk_pair = _dot_tn(dsscr[2 * p], _lane_half(qpair, 0)) + _dot_tn(dsscr[2 * p + 1], _lane_half(qpair, 1))
            dv_pair = (_dot_tn(pscr[2 * p], _lane_half(dopair, 0).astype(bf16))
                       + _dot_tn(pscr[2 * p + 1], _lane_half(dopair, 1).astype(bf16)))
            if gqa:
                gk[p // 2] = gk[p // 2] + dk_pair
                gv[p // 2] = gv[p // 2] + dv_pair
            elif nb > 1:
                dk_ref[:, sl] = dk_pair[BLK:] + ck[:, sl]
                dv_ref[:, sl] = dv_pair[BLK:] + cv[:, sl]
                ck[:, sl] = dk_pair[:BLK]
                cv[:, sl] = dv_pair[:BLK]
            else:
                dk_ref[:, sl] = dk_pair
                dv_ref[:, sl] = dv_pair
        if gqa:
            lor = lax.broadcasted_iota(jnp.int32, (rows, LANES), 1) < HD
            fold = lambda g: jnp.where(lor, g[0] + pltpu.roll(g[0], HD, 1), g[1] + pltpu.roll(g[1], HD, 1))
            dk_full, dv_full = fold(gk), fold(gv)
            dk_ref[...] = dk_full[BLK:] + ck[...]
            dv_ref[...] = dv_full[BLK:] + cv[...]
            ck[...] = dk_full[:BLK]
            cv[...] = dv_full[:BLK]

    assert max_dist in (BLK - 1, BLK)
    diag = nb > 1 and max_dist == BLK
    cur = lambda ww: pl.BlockSpec((None, BLK, ww), lambda a, i: (a, nb - 1 - i, 0))
    prev = lambda ww: pl.BlockSpec((None, BLK, ww), lambda a, i: (a, jnp.maximum(nb - 2 - i, 0), 0))
    in_specs = [cur(w), cur(wk)] + ([prev(wk)] if nb > 1 else []) + [cur(wk)] + ([prev(wk)] if nb > 1 else []) + [cur(w)] * 3
    args = [q, k] + ([k] if nb > 1 else []) + [v] + ([v] if nb > 1 else []) + [do, o, lse]
    out_specs = [cur(w), cur(wk), cur(wk)]
    out_shape = [jax.ShapeDtypeStruct((n, l, w), f32), jax.ShapeDtypeStruct((n, l, wk), f32), jax.ShapeDtypeStruct((n, l, wk), f32)]
    if has_sink:
        in_specs = [pl.BlockSpec(memory_space=pltpu.SMEM)] + in_specs
        args = [sink] + args
        out_specs.append(pl.BlockSpec((8, LANES), lambda a, i: (0, 0)))
        out_shape.append(jax.ShapeDtypeStruct((8, LANES), f32))
    nh = w // HD
    scratch = [pltpu.VMEM((BLK, wk), f32), pltpu.VMEM((BLK, wk), f32)] if nb > 1 else []
    scratch += [pltpu.VMEM((nh, BLK, BLK), f32)] * 2 + [pltpu.VMEM((nh, BLK, 2 * BLK if nb > 1 else BLK), bf16)] * 2
    if diag:
        scratch += [pltpu.VMEM((nh, BLK, BLK), f32)] * 2
    return pl.pallas_call(
        body, name=name, grid=(n, nb), in_specs=in_specs, out_specs=out_specs, out_shape=out_shape,
        scratch_shapes=scratch, compiler_params=_cp(("arbitrary", "arbitrary")),
    )(*args)


def _mix_out(oa, o1, l1, o4, l4, o16, l16, g_mix_a, g_mix_b, w_out, x, mod, g_post):
    def body(oa_ref, o1_ref, l1_ref, o4_ref, l4_ref, o16_ref, l16_ref, ga_ref, gb_ref, w_ref, x_ref, mod_ref, gp_ref,
             x1_ref, y_ref, mixed_ref, ob_ref, ob4_ref, ob16_ref, w1_ref, w4_ref, w16_ref, scr):
        o4v = _perm_load(o4_ref, scr, 4)
        l4v = _perm_load(l4_ref, scr, 4)
        o16v = _perm_load(o16_ref, scr, 16)
        l16v = _perm_load(l16_ref, scr, 16)
        l1v = l1_ref[...]
        m = jnp.maximum(jnp.maximum(l1v, l4v), l16v)
        e1, e4, e16 = jnp.exp(l1v - m), jnp.exp(l4v - m), jnp.exp(l16v - m)
        z = e1 + e4 + e16
        w1, w4, w16 = e1 / z, e4 / z, e16 / z
        ob = w1 * o1_ref[...] + w4 * o4v + w16 * o16v
        w1_ref[...] = w1
        w4_ref[...] = w4
        w16_ref[...] = w16
        ob_ref[...] = ob
        _perm_store(ob, scr, ob4_ref, 4)
        _perm_store(ob, scr, ob16_ref, 16)
        oan, _ = _rms(oa_ref[...])
        obn, _ = _rms(ob)
        mixed = jnp.concatenate([oan * ga_ref[...], obn * gb_ref[...]], axis=1).astype(bf16)
        mixed_ref[...] = mixed
        y = _dot(mixed, w_ref[...])
        y_ref[...] = y
        yn, _ = _rms(y)
        x1_ref[...] = x_ref[...] + mod_ref[2:3, :] * (yn * gp_ref[...])

    nat = lambda w, dt: jax.ShapeDtypeStruct((BL, SEQ, w), dt)
    return pl.pallas_call(
        body, name="mix_out", grid=(BL, NJ),
        in_specs=[_tok(AQ), _tok(BW), _tok(BW), _perm_spec(4, BW), _perm_spec(4, BW), _perm_spec(16, BW), _perm_spec(16, BW),
                  _full((1, AQ)), _full((1, BW)), _full((D, D)), _tok(D), MOD_SPEC, _full((1, D))],
        out_specs=[_tok(D), _tok(D), _tok(D), _tok(BW), _perm_spec(4, BW), _perm_spec(16, BW), _tok(BW), _tok(BW), _tok(BW)],
        out_shape=[nat(D, f32), nat(D, f32), nat(D, bf16), nat(BW, f32),
                   jax.ShapeDtypeStruct((BL, 4, SEQ // 4, BW), f32), jax.ShapeDtypeStruct((BL, 16, SEQ // 16, BW), f32),
                   nat(BW, f32), nat(BW, f32), nat(BW, f32)],
        scratch_shapes=[pltpu.VMEM((BW // LANES, TM, LANES), f32)],
        compiler_params=_cp(("arbitrary", "arbitrary")),
    )(oa, o1, l1, o4, l4, o16, l16, g_mix_a, g_mix_b, w_out, x, mod, g_post)


def _mlp_up(x1, mod, g_pre, w_up):
    def body(x_ref, mod_ref, g_ref, w_ref, h_ref, u_ref, a_ref):
        xn, _ = _rms(x_ref[...])
        h = (xn * g_ref[...]) * (1.0 + mod_ref[4:5, :]) + mod_ref[3:4, :]
        hb = h.astype(bf16)
        h_ref[...] = hb
        for s in range(NCHIP):
            u = _dot(hb, w_ref[s])
            u_ref[:, D * s:D * (s + 1)] = u.astype(bf16)
            a_ref[:, D * s:D * (s + 1)] = jnp.square(jnp.maximum(u, 0.0)).astype(bf16)

    nat = lambda w: jax.ShapeDtypeStruct((BL, SEQ, w), bf16)
    return pl.pallas_call(
        body, name="mlp_up", grid=(BL, NJ),
        in_specs=[_tok(D), MOD_SPEC, _full((1, D)), _full((NCHIP, D, D))],
        out_specs=[_tok(D), _tok(DFF), _tok(DFF)], out_shape=[nat(D), nat(DFF), nat(DFF)],
        compiler_params=_cp(("arbitrary", "arbitrary")),
    )(x1, mod, g_pre, w_up)


def _mlp_down(a, w_down, x1, target, mod, g_post):
    def body(a_ref, w_ref, x_ref, t_ref, mod_ref, g_ref, gx_ref, dy_ref, accb_ref, accg_ref):
        _acc_init(accb_ref, accg_ref)
        y2 = _dot(a_ref[...], w_ref[...])
        yn, r = _rms(y2)
        g = g_ref[...]
        gt = mod_ref[5:6, :]
        n2 = yn * g
        err = x_ref[...] + gt * n2 - t_ref[...]
        gout = err * (1.0 / D)
        gx_ref[...] = gout
        dn2 = gout * gt
        dy_ref[...] = _rms_bwd(dn2 * g, yn, r).astype(bf16)
        accb_ref[0:1, :] += _colsum(gout * n2)
        accg_ref[0:1, :] += _colsum(dn2 * yn)
        accg_ref[1:2, :] += jnp.broadcast_to(jnp.sum(err * err, keepdims=True), (1, D))

    return pl.pallas_call(
        body, name="mlp_down", grid=(BL, NJ),
        in_specs=[_tok(DFF), _full((DFF, D)), _tok(D), _tok(D), MOD_SPEC, _full((1, D))],
        out_specs=[_tok(D), _tok(D), ACCB_SPEC, ACCG_SPEC],
        out_shape=[jax.ShapeDtypeStruct((BL, SEQ, D), f32), jax.ShapeDtypeStruct((BL, SEQ, D), bf16)] + ACC_SHAPES,
        compiler_params=_cp(("arbitrary", "arbitrary")),
    )(a, w_down, x1, target, mod, g_post)


def _mlp_bwd(dy2, u, w_down, w_up, x1, gx, mod, g_pre):
    def body(dy_ref, u_ref, wd_hbm, wu_hbm, x_ref, gx_ref, mod_ref, g_ref, du_ref, gx1_ref, accb_ref, accg_ref, wd, wu, sem):
        _acc_init(accb_ref, accg_ref)

        @pl.when((pl.program_id(0) == 0) & (pl.program_id(1) == 0))
        def _():
            c1 = pltpu.make_async_copy(wd_hbm, wd, sem.at[0])
            c2 = pltpu.make_async_copy(wu_hbm, wu, sem.at[1])
            c1.start()
            c2.start()
            c1.wait()
            c2.wait()

        dy = dy_ref[...]
        dh = jnp.zeros((TM, D), f32)
        for s in range(NCHIP):
            sl = slice(D * s, D * (s + 1))
            da = _dot_nt(dy, wd[sl, :])
            du = (da * (2.0 * jnp.maximum(u_ref[:, sl].astype(f32), 0.0))).astype(bf16)
            du_ref[:, sl] = du
            dh = dh + _dot_nt(du, wu[s])
        xn, r = _rms(x_ref[...])
        g = g_ref[...]
        n = xn * g
        dn = dh * (1.0 + mod_ref[4:5, :])
        gx1_ref[...] = gx_ref[...] + _rms_bwd(dn * g, xn, r)
        accb_ref[0:1, :] += _colsum(dh * n)
        accb_ref[1:2, :] += _colsum(dh)
        accg_ref[0:1, :] += _colsum(dn * xn)

    anyspec = pl.BlockSpec(memory_space=pl.ANY)
    return pl.pallas_call(
        body, name="mlp_bwd", grid=(BL, NJ),
        in_specs=[_tok(D), _tok(DFF), anyspec, anyspec, _tok(D), _tok(D), MOD_SPEC, _full((1, D))],
        out_specs=[_tok(DFF), _tok(D), ACCB_SPEC, ACCG_SPEC],
        out_shape=[jax.ShapeDtypeStruct((BL, SEQ, DFF), bf16), jax.ShapeDtypeStruct((BL, SEQ, D), f32)] + ACC_SHAPES,
        scratch_shapes=[pltpu.VMEM((DFF, D), bf16), pltpu.VMEM((NCHIP, D, D), bf16), pltpu.SemaphoreType.DMA((2,))],
        compiler_params=_cp(("arbitrary", "arbitrary")),
    )(dy2, u, w_down, w_up, x1, gx, mod, g_pre)


def _matmul_tn(a, b, *, tn, col_blocked, name):
    t, m = a.shape
    n = b.shape[1]
    tmm = min(m, 1024)
    tk = 512
    nk = t // tk

    def body(a_ref, b_ref, o_ref):
        @pl.when(pl.program_id(2) == 0)
        def _():
            o_ref[...] = jnp.zeros_like(o_ref)

        o_ref[...] += _dot_tn(a_ref[...], b_ref[...])

    if col_blocked:
        out_spec = pl.BlockSpec((None, tmm, tn), lambda i, j, k: (j, i, 0))
        out_shape = jax.ShapeDtypeStruct((n // tn, m, tn), f32)
    else:
        out_spec = pl.BlockSpec((tmm, tn), lambda i, j, k: (i, j))
        out_shape = jax.ShapeDtypeStruct((m, n), f32)
    return pl.pallas_call(
        body, name=name, grid=(m // tmm, n // tn, nk),
        in_specs=[pl.BlockSpec((tk, tmm), lambda i, j, k: (k, i)), pl.BlockSpec((tk, tn), lambda i, j, k: (k, j))],
        out_specs=out_spec, out_shape=out_shape,
        compiler_params=_cp(("arbitrary", "arbitrary", "arbitrary")),
    )(a, b)


def _attn_out_bwd(gx1, y, mod, g_post, w_out, oa, ob, g_mix_a, g_mix_b, w1, w4, w16):
    def body(gx_ref, y_ref, mod_ref, gp_ref, w_ref, oa_ref, ob_ref, ga_ref, gb_ref, w1_ref, w4_ref, w16_ref,
             dy_ref, doa_ref, do1_ref, do4_ref, do16_ref, accb_ref, accg_ref, scr):
        _acc_init(accb_ref, accg_ref)
        gx1v = gx_ref[...]
        yn, ry = _rms(y_ref[...])
        gp = gp_ref[...]
        gt = mod_ref[2:3, :]
        dn1 = gx1v * gt
        dy = _rms_bwd(dn1 * gp, yn, ry).astype(bf16)
        dy_ref[...] = dy
        dmixed = _dot_nt(dy, w_ref[...])
        dma, dmb = dmixed[:, :AQ], dmixed[:, AQ:]
        oan, ra = _rms(oa_ref[...])
        obn, rb = _rms(ob_ref[...])
        doa_ref[...] = _rms_bwd(dma * ga_ref[...], oan, ra)
        dob = _rms_bwd(dmb * gb_ref[...], obn, rb)
        do1_ref[...] = w1_ref[...] * dob
        _perm_store(w4_ref[...] * dob, scr, do4_ref, 4)
        _perm_store(w16_ref[...] * dob, scr, do16_ref, 16)
        accb_ref[0:1, :] += _colsum(gx1v * (yn * gp))
        accg_ref[0:1, :] += _colsum(dn1 * yn)
        accg_ref[1:2, :] += jnp.concatenate([_colsum(dma * oan), _colsum(dmb * obn)], axis=1)

    nat = lambda w, dt: jax.ShapeDtypeStruct((BL, SEQ, w), dt)
    return pl.pallas_call(
        body, name="attn_out_bwd", grid=(BL, NJ),
        in_specs=[_tok(D), _tok(D), MOD_SPEC, _full((1, D)), _full((D, D)), _tok(AQ), _tok(BW), _full((1, AQ)), _full((1, BW)),
                  _tok(BW), _tok(BW), _tok(BW)],
        out_specs=[_tok(D), _tok(AQ), _tok(BW), _perm_spec(4, BW), _perm_spec(16, BW), ACCB_SPEC, ACCG_SPEC],
        out_shape=[nat(D, bf16), nat(AQ, f32), nat(BW, f32), jax.ShapeDtypeStruct((BL, 4, SEQ // 4, BW), f32),
                   jax.ShapeDtypeStruct((BL, 16, SEQ // 16, BW), f32)] + ACC_SHAPES,
        scratch_shapes=[pltpu.VMEM((BW // LANES, TM, LANES), f32)],
        compiler_params=_cp(("arbitrary", "arbitrary")),
    )(gx1, y, mod, g_post, w_out, oa, ob, g_mix_a, g_mix_b, w1, w4, w16)


def _attn_in_bwd(dqa, dka, dva, d1, d4, d16, tc, ts1, ts2, w_in, x, gx1, mod, g_pre):
    def body(dqa_ref, dka_ref, dva_ref, dq1_ref, dk1_ref, dv1_ref, dq4_ref, dk4_ref, dv4_ref, dq16_ref, dk16_ref, dv16_ref,
             c_ref, s1_ref, s2_ref, w_ref, x_ref, gx_ref, mod_ref, g_ref, dproj_ref, dx_ref, accb_ref, accg_ref, scr):
        _acc_init(accb_ref, accg_ref)
        c, s1, s2 = c_ref[...], s1_ref[...], s2_ref[...]
        tot = lambda r1, r4, r16: r1[...] + _perm_load(r4, scr, 4) + _perm_load(r16, scr, 16)
        dqb = tot(dq1_ref, dq4_ref, dq16_ref)
        dkb = tot(dk1_ref, dk4_ref, dk16_ref)
        dvb = tot(dv1_ref, dv4_ref, dv16_ref)
        dproj = jnp.concatenate([
            _rope_t(dqa_ref[...], c, s1, s2) * 0.125, _rope_t(dka_ref[...], c, s1, s2), dva_ref[...],
            _rope_t(dqb, c, s1, s2) * 0.125, _rope_t(dkb, c, s1, s2), dvb], axis=1).astype(bf16)
        dproj_ref[...] = dproj
        dh = _dot_nt(dproj, w_ref[...])
        xn, r = _rms(x_ref[...])
        g = g_ref[...]
        dn = dh * (1.0 + mod_ref[1:2, :])
        dx_ref[...] = gx_ref[...] + _rms_bwd(dn * g, xn, r)
        accb_ref[0:1, :] += _colsum(dh * (xn * g))
        accb_ref[1:2, :] += _colsum(dh)
        accg_ref[0:1, :] += _colsum(dn * xn)

    return pl.pallas_call(
        body, name="attn_in_bwd", grid=(BL, NJ),
        in_specs=[_tok(AQ), _tok(AKV), _tok(AKV)] + [_tok(BW)] * 3 + [_perm_spec(4, BW)] * 3 + [_perm_spec(16, BW)] * 3
                 + [_tok(LANES)] * 3 + [_full((D, INW)), _tok(D), _tok(D), MOD_SPEC, _full((1, D))],
        out_specs=[_tok(INW), _tok(D), ACCB_SPEC, ACCG_SPEC],
        out_shape=[jax.ShapeDtypeStruct((BL, SEQ, INW), bf16), jax.ShapeDtypeStruct((BL, SEQ, D), f32)] + ACC_SHAPES,
        scratch_shapes=[pltpu.VMEM((BW // LANES, TM, LANES), f32)],
        compiler_params=_cp(("arbitrary", "arbitrary")),
    )(dqa, dka, dva, *d1, *d4, *d16, tc, ts1, ts2, w_in, x, gx1, mod, g_pre)


def _local_step(x, positions, mod, target, w_in, later_weights, grad_ready, g_attn_pre, g_attn_post, sink_a, g_mix_a, g_mix_b,
                g_mlp_pre, g_mlp_post):
    inv = np.float32(THETA) ** (-np.arange(0, ROT, 2, dtype=np.float32) / np.float32(ROT))
    lane = np.arange(LANES) % HD
    inv_lane = jnp.asarray(np.where(lane < ROT, inv[lane % (ROT // 2)], 0.0).astype(np.float32)[None, :])
    tabs = _rope_tables(positions.reshape(BL * SEQ, 1), inv_lane)
    tc, ts1, ts2 = [t.reshape(BL, SEQ, LANES) for t in tabs]

    (h, qa, ka, va, q1, k1, v1, q4, k4, v4, q16, k16, v16) = _attn_in(x, mod, g_attn_pre, w_in, tc, ts1, ts2)
    seqs = lambda t: t.reshape(t.shape[0] * t.shape[1], t.shape[2], t.shape[3])
    q4, k4, v4, q16, k16, v16 = [seqs(t) for t in (q4, k4, v4, q16, k16, v16)]
    oa, la = _attn_fwd(qa, ka, va, sink_a, max_dist=BLK - 1, name="attn_a_fwd")
    o1, l1 = _attn_fwd(q1, k1, v1, None, max_dist=BLK, name="attn_b1_fwd")
    o4, l4 = _attn_fwd(q4, k4, v4, None, max_dist=BLK, name="attn_b4_fwd")
    o16, l16 = _attn_fwd(q16, k16, v16, None, max_dist=BLK, name="attn_b16_fwd")
    b4 = lambda t: t.reshape(BL, 4, SEQ // 4, BW)
    b16 = lambda t: t.reshape(BL, 16, SEQ // 16, BW)
    w_out, w_up, w_down = later_weights((oa, o1, o4, o16))
    x1, y, mixed, ob, ob4, ob16, w1, w4, w16 = _mix_out(oa, o1, l1, b4(o4), b4(l4), b16(o16), b16(l16), g_mix_a, g_mix_b,
                                                        w_out, x, mod, g_attn_post)
    h2, u, a = _mlp_up(x1, mod, g_mlp_pre, w_up)
    gx, dy2, accb_d, accg_d = _mlp_down(a, w_down, x1, target, mod, g_mlp_post)

    flat = lambda t: t.reshape(BL * SEQ, t.shape[-1])
    mod = mod + grad_ready("w_down", _matmul_tn(flat(a), flat(dy2), tn=D, col_blocked=False, name="grad_w_down"))
    du, gx1, accb_m, accg_m = _mlp_bwd(dy2, u, w_down, w_up, x1, gx, mod, g_mlp_pre)
    mod = mod + grad_ready("w_up", _matmul_tn(flat(h2), flat(du), tn=D, col_blocked=True, name="grad_w_up"))

    dy, doa, do1, do4, do16, accb_o, accg_o = _attn_out_bwd(gx1, y, mod, g_attn_post, w_out, oa, ob, g_mix_a, g_mix_b, w1, w4, w16)
    gw_out = _matmul_tn(flat(mixed), flat(dy), tn=D, col_blocked=False, name="grad_w_out")
    dqa, dka, dva, dsink = _attn_bwd(qa, ka, va, doa, oa, la, sink_a, max_dist=BLK - 1, name="attn_a_bwd")
    d1 = _attn_bwd(q1, k1, v1, do1, ob, l1, None, max_dist=BLK, name="attn_b1_bwd")
    d4 = _attn_bwd(q4, k4, v4, seqs(do4), seqs(ob4), l4, None, max_dist=BLK, name="attn_b4_bwd")
    d16 = _attn_bwd(q16, k16, v16, seqs(do16), seqs(ob16), l16, None, max_dist=BLK, name="attn_b16_bwd")
    dproj, grad_x, accb_i, accg_i = _attn_in_bwd(dqa, dka, dva, d1, [b4(t) for t in d4], [b16(t) for t in d16],
                                                 tc, ts1, ts2, w_in, x, gx1, mod, g_attn_pre)
    gw_in = _matmul_tn(flat(h), flat(dproj), tn=INW, col_blocked=False, name="grad_w_in")
    dsink = dsink + grad_ready("w_in_w_out", (gw_in, gw_out))

    return grad_x, (accb_i, accb_o, accb_m, accb_d, accg_i, accg_o, accg_m, accg_d, dsink)


ADAW = NMOD * D // NCHIP


def _pos():
    return lax.axis_index("x"), lax.axis_index("y"), lax.axis_index("c")


def _flip(v, bit):
    return 1 - v if bit else v


def _all_peers(x, y, c):
    return [(_flip(x, k >> 2 & 1), _flip(y, k >> 1 & 1), _flip(c, k & 1)) for k in range(1, NDEV)]


def _other_chips(x, y):
    return [(1 - x, y), (x, 1 - y), (1 - x, 1 - y)]


def _rcopy(src, dst, send, recv, k, dev):
    return pltpu.make_async_remote_copy(src_ref=src, dst_ref=dst, send_sem=send.at[k], recv_sem=recv.at[k],
                                        device_id=dev, device_id_type=MESH)


def _gather_small(src, buf, send, recv):
    x, y, c = _pos()
    me = 4 * x + 2 * y + c
    peers = _all_peers(x, y, c)
    sends = [_rcopy(src, buf.at[me], send, recv, k, p) for k, p in enumerate(peers)]
    for cp in sends:
        cp.start()
    for k, (px, py, pc) in enumerate(peers):
        _rcopy(src, buf.at[4 * px + 2 * py + pc], send, recv, k, (px, py, pc)).wait_recv()
    for cp in sends:
        cp.wait_send()
    return me


def _ada_fwd(c_in, w_ada, b_cols):
    def body(c_ref, w_ref, b_ref, mod_ref, cond_ref, cbuf, mbuf, s1, r1, s2, r2):
        x, y, c = _pos()
        chip = 2 * x + y
        me = _gather_small(c_ref, cbuf, s1, r1)
        cbuf[me] = c_ref[...]
        for i in range(NDEV):
            cond_ref[BL * i:BL * (i + 1), :] = cbuf[i]
        call = cond_ref[...]
        cond = call / (1.0 + jnp.exp(-call))
        cond_ref[...] = cond
        mbuf[chip] = jnp.dot(cond, w_ref[...], preferred_element_type=f32, precision=lax.Precision.HIGHEST) + b_ref[...]
        chips = _other_chips(x, y)
        sends = [_rcopy(mbuf.at[chip], mbuf.at[chip], s2, r2, j, (px, py, c)) for j, (px, py) in enumerate(chips)]
        for cp in sends:
            cp.start()
        for j, (px, py) in enumerate(chips):
            _rcopy(mbuf.at[chip], mbuf.at[2 * px + py], s2, r2, j, (px, py, c)).wait_recv()
        for cp in sends:
            cp.wait_send()
        row = lax.broadcasted_iota(jnp.int32, (BL * NDEV, ADAW), 0)
        for s in range(NCHIP):
            slab = mbuf[s]
            for j in range(BL):
                mod_ref[j:j + 1, ADAW * s:ADAW * (s + 1)] = jnp.sum(jnp.where(row == BL * me + j, slab, 0.0), axis=0, keepdims=True)

    vm = pl.BlockSpec(memory_space=pltpu.VMEM)
    return pl.pallas_call(
        body, name="ada_fwd", in_specs=[vm, vm, vm], out_specs=[vm, vm],
        out_shape=[jax.ShapeDtypeStruct((BL, NMOD * D), f32), jax.ShapeDtypeStruct((BL * NDEV, D), f32)],
        scratch_shapes=[pltpu.VMEM((NDEV, BL, D), f32), pltpu.VMEM((NCHIP, BL * NDEV, ADAW), f32),
                        pltpu.SemaphoreType.DMA((NDEV - 1,)), pltpu.SemaphoreType.DMA((NDEV - 1,)),
                        pltpu.SemaphoreType.DMA((NCHIP - 1,)), pltpu.SemaphoreType.DMA((NCHIP - 1,))],
        compiler_params=pltpu.CompilerParams(vmem_limit_bytes=VMEM_LIMIT),
    )(c_in, w_ada, b_cols)


def _small_allreduce(accs, cond_all, after=()):
    na = len(after)

    def body(bi, bo, bm, bd, gi, go, gm, gd, dsink, cond_ref, *rest):
        gw_ref, gb_ref, small_ref, pay, pbuf, dall, s1, r1 = rest[na:]
        x, y, c = _pos()
        chip = 2 * x + y
        pay[...] = jnp.zeros_like(pay)
        for b in range(BL):
            for k, (ref, r) in enumerate(((bi, 1), (bi, 0), (bo, 0), (bm, 1), (bm, 0), (bd, 0))):
                pay[b:b + 1, D * k:D * (k + 1)] = ref[b, r:r + 1, :]
        for off, ref, r in ((OFF_G_ATTN_PRE, gi, 0), (OFF_G_ATTN_POST, go, 0), (OFF_G_MIX_A, go, 1), (OFF_G_MLP_PRE, gm, 0),
                            (OFF_G_MLP_POST, gd, 0)):
            pay[BL:BL + 1, off:off + D] = ref[r:r + 1, :]
        eye = lax.broadcasted_iota(jnp.int32, (8, LANES), 0) == lax.broadcasted_iota(jnp.int32, (8, LANES), 1)
        pay[BL:BL + 1, OFF_SINK:OFF_SINK + LANES] = jnp.sum(jnp.where(eye, dsink[...], 0.0), axis=0, keepdims=True)
        pay[BL:BL + 1, OFF_LOSS:OFF_LOSS + LANES] = gd[1:2, 0:LANES]
        me = _gather_small(pay, pbuf, s1, r1)
        pbuf[me] = pay[...]
        small = pbuf[0, BL:BL + 1, :]
        for i in range(1, NDEV):
            small = small + pbuf[i, BL:BL + 1, :]
        small_ref[...] = small
        for i in range(NDEV):
            dall[BL * i:BL * (i + 1), :] = pbuf[i, 0:BL, :]
        gb_ref[...] = jnp.sum(dall[...], axis=0, keepdims=True)
        cols = jnp.zeros((BL * NDEV, ADAW), f32)
        for s in range(NCHIP):
            cols = cols + jnp.where(chip == s, dall[:, ADAW * s:ADAW * (s + 1)], 0.0)
        gw_ref[...] = lax.dot_general(cond_ref[...], cols, (((0,), (0,)), ((), ())), preferred_element_type=f32,
                                      precision=lax.Precision.HIGHEST)

    vm = pl.BlockSpec(memory_space=pltpu.VMEM)
    return pl.pallas_call(
        body, name="small_allreduce", in_specs=[vm] * 10 + [pl.BlockSpec(memory_space=pl.ANY)] * na, out_specs=[vm] * 3,
        out_shape=[jax.ShapeDtypeStruct((D, ADAW), f32), jax.ShapeDtypeStruct((1, PAYW), f32), jax.ShapeDtypeStruct((1, PAYW), f32)],
        scratch_shapes=[pltpu.VMEM((4, PAYW), f32), pltpu.VMEM((NDEV, 4, PAYW), f32), pltpu.VMEM((BL * NDEV, PAYW), f32),
                        pltpu.SemaphoreType.DMA((NDEV - 1,)), pltpu.SemaphoreType.DMA((NDEV - 1,))],
        compiler_params=pltpu.CompilerParams(vmem_limit_bytes=VMEM_LIMIT),
    )(*accs, cond_all, *after)


def _half(ref, c):
    r2 = ref.shape[0] // 2
    return ref.at[pl.ds(pl.multiple_of(c * r2, 16), r2), :]


HBM_SPEC = pl.BlockSpec(memory_space=pltpu.HBM)
SEM_SPEC = pl.BlockSpec(memory_space=pltpu.SEMAPHORE)
EFFECT = pltpu.SideEffectType.DATAFLOW_SIDE_EFFECTING
NLINK = NCHIP - 1


def _in_hbm(a):
    return pltpu.with_memory_space_constraint(a, pltpu.HBM)


def _split_start(name, srcs, land_shapes, builds, after=()):
    n = len(srcs)

    na = len(after)

    def body(*refs):
        src, land, token = refs[:n], refs[n:2 * n], refs[-1]
        send, recv = refs[2 * n + na:3 * n + na], refs[3 * n + na:4 * n + na]
        for t in range(n):
            for out_cp, _ in builds[t](src[t], land[t], send[t], recv[t]):
                out_cp.start()
        token[...] = jnp.zeros_like(token)

    lands = [_in_hbm(lax.empty(s.shape, s.dtype)) for s in land_shapes]
    sems = [pltpu.SemaphoreType.DMA((NLINK,))] * (2 * n)
    thru = [pltpu.HBM(a.shape, a.dtype) for a in list(srcs) + lands]
    res = pl.pallas_call(
        body, name=name, out_shape=sems + thru + [jax.ShapeDtypeStruct((8, LANES), f32)],
        in_specs=[HBM_SPEC] * (2 * n) + [pl.BlockSpec(memory_space=pl.ANY)] * na,
        out_specs=[SEM_SPEC] * (2 * n) + [HBM_SPEC] * (2 * n) + [pl.BlockSpec(memory_space=pltpu.VMEM)],
        input_output_aliases={i: 2 * n + i for i in range(2 * n)},
        compiler_params=pltpu.CompilerParams(has_side_effects=EFFECT),
    )(*[_in_hbm(a) for a in srcs], *lands, *after)
    flight = [(res[2 * n + t], res[3 * n + t], res[t], res[n + t]) for t in range(n)]
    return flight, res[-1][0, 0]


def _split_wait(name, flight, builds, after):
    m = len(flight)
    na = len(after)

    def body(*refs):
        src, land, send, recv = refs[:m], refs[m:2 * m], refs[2 * m:3 * m], refs[3 * m:4 * m]
        for t in range(m):
            for out_cp, in_cp in builds[t](src[t], land[t], send[t], recv[t]):
                out_cp.wait_send()
                in_cp.wait_recv()

    ops = [f[0] for f in flight] + [f[1] for f in flight] + [f[2] for f in flight] + [f[3] for f in flight]
    res = pl.pallas_call(
        body, name=name, out_shape=[pltpu.HBM(a.shape, a.dtype) for a in ops[:2 * m]],
        in_specs=[HBM_SPEC] * (2 * m) + [SEM_SPEC] * (2 * m) + [pl.BlockSpec(memory_space=pl.ANY)] * na,
        out_specs=[HBM_SPEC] * (2 * m), input_output_aliases={i: i for i in range(2 * m)},
        compiler_params=pltpu.CompilerParams(has_side_effects=EFFECT),
    )(*ops, *after)
    return res[:m], res[m:2 * m]


def _weight_copies(src, land, send, recv):
    x, y, c = _pos()
    chip = 2 * x + y
    return [(_rcopy(_half(src, c), _half(land.at[chip], c), send, recv, j, (px, py, c)),
             _rcopy(_half(src, c), _half(land.at[2 * px + py], c), send, recv, j, (px, py, c)))
            for j, (px, py) in enumerate(_other_chips(x, y))]


def _grad_copies(src, land, send, recv):
    x, y, c = _pos()
    return [(_rcopy(src.at[2 * px + py], land.at[j], send, recv, j, (px, py, c)),
             _rcopy(src.at[2 * px + py], land.at[j], send, recv, j, (px, py, c)))
            for j, (px, py) in enumerate(_other_chips(x, y))]


def _pair_forward(shards, gathered, name):
    nt = len(shards)

    def body(*refs):
        sh, gin, gout = refs[:nt], refs[nt:2 * nt], refs[2 * nt:3 * nt]
        send, recv = refs[3 * nt:]
        x, y, c = _pos()
        chip = 2 * x + y
        sib = (x, y, 1 - c)
        chips = _other_chips(x, y)
        cps = []
        for t in range(nt):
            for j, (px, py) in enumerate(chips):
                cps.append(_rcopy(_half(gin[t].at[2 * px + py], c), _half(gout[t].at[2 * px + py], c), send, recv, 4 * t + j, sib))
            cps.append(_rcopy(sh[t], gout[t].at[chip], send, recv, 4 * t + 3, sib))
        for cp in cps:
            cp.start()
        for t in range(nt):
            for j, (px, py) in enumerate(chips):
                theirs = _half(gout[t].at[2 * px + py], 1 - c)
                _rcopy(theirs, theirs, send, recv, 4 * t + j, sib).wait_recv()
            _rcopy(sh[t], gout[t].at[chip], send, recv, 4 * t + 3, sib).wait_recv()
        for cp in cps:
            cp.wait_send()

    hbm = pl.BlockSpec(memory_space=pl.ANY)
    return pl.pallas_call(
        body, name=name, in_specs=[hbm] * (2 * nt), out_specs=[hbm] * nt,
        out_shape=[jax.ShapeDtypeStruct(g.shape, g.dtype) for g in gathered],
        input_output_aliases={nt + t: t for t in range(nt)},
        scratch_shapes=[pltpu.SemaphoreType.DMA((4 * nt,)), pltpu.SemaphoreType.DMA((4 * nt,))],
    )(*shards, *gathered)


def _rs_pair(grads, name):
    nt = len(grads)

    def body(*refs):
        ins, outs = refs[:nt], refs[nt:2 * nt]
        send, recv = refs[2 * nt:]
        x, y, c = _pos()
        sib = (x, y, 1 - c)
        cps = []
        for t in range(nt):
            r2 = ins[t].shape[1] // 2
            src = ins[t].at[:, pl.ds(pl.multiple_of((1 - c) * r2, 8), r2), :]
            cps.append(_rcopy(src, outs[t], send, recv, t, sib))
        for cp in cps:
            cp.start()
        for cp in cps:
            cp.wait()

    hbm = pl.BlockSpec(memory_space=pl.ANY)
    return pl.pallas_call(
        body, name=name, in_specs=[hbm] * nt, out_specs=[hbm] * nt,
        out_shape=[jax.ShapeDtypeStruct((NCHIP, g.shape[1] // 2, g.shape[2]), f32) for g in grads],
        scratch_shapes=[pltpu.SemaphoreType.DMA((nt,)), pltpu.SemaphoreType.DMA((nt,))],
    )(*grads)


RS_ROWS = 128


def _pair_add(g, landed, c_arr, name):
    _, r2, cw = landed.shape
    nr = r2 // RS_ROWS

    def body(c_ref, g_ref, p_ref, o_ref):
        o_ref[...] = (g_ref[...] + p_ref[...]).astype(bf16)

    gs = pltpu.PrefetchScalarGridSpec(
        num_scalar_prefetch=1, grid=(NCHIP, nr),
        in_specs=[pl.BlockSpec((None, RS_ROWS, cw), lambda s, j, c: (s, c[0] * nr + j, 0)),
                  pl.BlockSpec((None, RS_ROWS, cw), lambda s, j, c: (s, j, 0))],
        out_specs=pl.BlockSpec((None, RS_ROWS, cw), lambda s, j, c: (s, j, 0)))
    return pl.pallas_call(body, name=name, grid_spec=gs, out_shape=jax.ShapeDtypeStruct((NCHIP, r2, cw), bf16),
                          compiler_params=_cp(("arbitrary", "arbitrary")))(c_arr, g, landed)


def _chip_add(half, landed, pos_arr, name):
    _, r2, cw = half.shape
    nr = r2 // RS_ROWS

    def body(s_ref, h_ref, q_ref, o_ref):
        acc = h_ref[...].astype(f32)
        for j in range(NCHIP - 1):
            acc = acc + q_ref[j].astype(f32)
        o_ref[...] = acc

    gs = pltpu.PrefetchScalarGridSpec(
        num_scalar_prefetch=1, grid=(nr,),
        in_specs=[pl.BlockSpec((None, RS_ROWS, cw), lambda j, s: (s[0], j, 0)),
                  pl.BlockSpec((NCHIP - 1, RS_ROWS, cw), lambda j, s: (0, j, 0))],
        out_specs=pl.BlockSpec((RS_ROWS, cw), lambda j, s: (s[1] * nr + j, 0)))
    return pl.pallas_call(body, name=name, grid_spec=gs, out_shape=jax.ShapeDtypeStruct((2 * r2, cw), f32),
                          compiler_params=_cp(("arbitrary",)))(pos_arr, half, landed)


def _ag_pair(fulls, name):
    nt = len(fulls)

    def body(*refs):
        ins, outs = refs[:nt], refs[nt:2 * nt]
        send, recv = refs[2 * nt:]
        x, y, c = _pos()
        sib = (x, y, 1 - c)
        cps = [_rcopy(_half(ins[t], c), _half(outs[t], c), send, recv, t, sib) for t in range(nt)]
        for cp in cps:
            cp.start()
        for t in range(nt):
            _rcopy(_half(ins[t], c), _half(outs[t], 1 - c), send, recv, t, sib).wait_recv()
        for cp in cps:
            cp.wait_send()

    hbm = pl.BlockSpec(memory_space=pl.ANY)
    return pl.pallas_call(
        body, name=name, in_specs=[hbm] * nt, out_specs=[hbm] * nt,
        out_shape=[jax.ShapeDtypeStruct(a.shape, f32) for a in fulls],
        input_output_aliases={t: t for t in range(nt)},
        scratch_shapes=[pltpu.SemaphoreType.DMA((nt,)), pltpu.SemaphoreType.DMA((nt,))],
    )(*fulls)


def _adamw_math(w, g, m, v):
    m = B1 * m + (1.0 - B1) * g
    v = B2 * v + (1.0 - B2) * jnp.square(g)
    m_hat = m / (1.0 - B1 ** STEP)
    v_hat = v / (1.0 - B2 ** STEP)
    return -LR * (m_hat / (jnp.sqrt(v_hat) + AEPS) + WD * w), m, v


ADAM_ROWS = 128


def _adamw(w, g, m, v, name):
    r, cw = w.shape

    def body(w_ref, g_ref, m_ref, v_ref, d_ref, mo_ref, vo_ref):
        d_ref[...], mo_ref[...], vo_ref[...] = _adamw_math(w_ref[...], g_ref[...], m_ref[...], v_ref[...])

    spec = pl.BlockSpec((ADAM_ROWS, cw), lambda i: (i, 0))
    return pl.pallas_call(body, name=name, grid=(r // ADAM_ROWS,), in_specs=[spec] * 4, out_specs=[spec] * 3,
                          out_shape=[jax.ShapeDtypeStruct((r, cw), f32)] * 3, compiler_params=_cp(("arbitrary",)))(w, g, m, v)


SMALL = (("b_ada", None, PAYW), ("g_attn_pre", OFF_G_ATTN_PRE, D), ("g_attn_post", OFF_G_ATTN_POST, D), ("sink_a", OFF_SINK, 8),
         ("g_mix_a", OFF_G_MIX_A, AQ), ("g_mix_b", OFF_G_MIX_B, BW), ("g_mlp_pre", OFF_G_MLP_PRE, D), ("g_mlp_post", OFF_G_MLP_POST, D))


def _adamw_small(small, gb, params):
    n = len(SMALL)

    def body(*refs):
        small_ref, gb_ref = refs[:2]
        wmv = refs[2:2 + 3 * n]
        loss_ref = refs[2 + 3 * n]
        outs = refs[3 + 3 * n:]
        loss_ref[...] = small_ref[:, OFF_LOSS:OFF_LOSS + 1] * (0.5 / D)
        for i, (_, off, width) in enumerate(SMALL):
            g = gb_ref[...] if off is None else small_ref[:, off:off + width]
            w_ref, m_ref, v_ref = wmv[3 * i:3 * i + 3]
            outs[4 * i][...] = g
            outs[4 * i + 1][...], outs[4 * i + 2][...], outs[4 * i + 3][...] = _adamw_math(w_ref[...], g, m_ref[...], v_ref[...])

    vm = pl.BlockSpec(memory_space=pltpu.VMEM)
    out_shape = [jax.ShapeDtypeStruct((1, 1), f32)]
    for _, _, width in SMALL:
        out_shape += [jax.ShapeDtypeStruct((1, width), f32)] * 4
    flat = [a for wmv in params for a in wmv]
    res = pl.pallas_call(body, name="adamw_small", in_specs=[vm] * (2 + 3 * n), out_specs=[vm] * len(out_shape),
                         out_shape=out_shape)(small, gb, *flat)
    return res[0], {name: res[1 + 4 * i:5 + 4 * i] for i, (name, _, _) in enumerate(SMALL)}


def kernel(x, c, positions, w_ada, b_ada, g_attn_pre, g_attn_post, w_in, sink_a, g_mix_a, g_mix_b, w_out, g_mlp_pre, g_mlp_post, w_up, w_down, loss_target, m_w_ada, m_b_ada, m_g_attn_pre, m_g_attn_post, m_w_in, m_sink_a, m_g_mix_a, m_g_mix_b, m_w_out, m_g_mlp_pre, m_g_mlp_post, m_w_up, m_w_down, v_w_ada, v_b_ada, v_g_attn_pre, v_g_attn_post, v_w_in, v_sink_a, v_g_mix_a, v_g_mix_b, v_w_out, v_g_mlp_pre, v_g_mlp_post, v_w_up, v_w_down):
    given = dict(w_ada=w_ada, b_ada=b_ada, g_attn_pre=g_attn_pre, g_attn_post=g_attn_post, w_in=w_in, sink_a=sink_a, g_mix_a=g_mix_a,
                 g_mix_b=g_mix_b, w_out=w_out, g_mlp_pre=g_mlp_pre, g_mlp_post=g_mlp_post, w_up=w_up, w_down=w_down)
    moms = dict(w_ada=(m_w_ada, v_w_ada), b_ada=(m_b_ada, v_b_ada), g_attn_pre=(m_g_attn_pre, v_g_attn_pre),
                g_attn_post=(m_g_attn_post, v_g_attn_post), w_in=(m_w_in, v_w_in), sink_a=(m_sink_a, v_sink_a),
                g_mix_a=(m_g_mix_a, v_g_mix_a), g_mix_b=(m_g_mix_b, v_g_mix_b), w_out=(m_w_out, v_w_out),
                g_mlp_pre=(m_g_mlp_pre, v_g_mlp_pre), g_mlp_post=(m_g_mlp_post, v_g_mlp_post), w_up=(m_w_up, v_w_up),
                w_down=(m_w_down, v_w_down))
    order = ["w_ada", "b_ada", "g_attn_pre", "g_attn_post", "w_in", "sink_a", "g_mix_a", "g_mix_b", "w_out", "g_mlp_pre",
             "g_mlp_post", "w_up", "w_down"]
    xi, yi, ci = _pos()
    chip = 2 * xi + yi

    c_arr = jnp.reshape(ci, (1,)).astype(jnp.int32)
    pos_arr = jnp.stack([chip, ci]).astype(jnp.int32)
    big = ("w_in", "w_out", "w_up", "w_down")

    shards = [given[n][0].astype(bf16) for n in big]
    gathered = [jax.ShapeDtypeStruct((NCHIP,) + s.shape, bf16) for s in shards]
    flight_in, tok = _split_start("weights_start_first", shards[:1], gathered[:1], [_weight_copies])
    b_cols = lax.dynamic_slice(b_ada, (0, chip * ADAW), (1, ADAW))
    mod, cond_all = _ada_fwd(c + tok, w_ada[0], b_cols)
    flight_rest, tok = _split_start("weights_start_rest", shards[1:], gathered[1:], [_weight_copies] * 3, after=(mod,))
    mod = mod.reshape(BL, NMOD, D) + tok
    srcs, lands = _split_wait("weights_wait_first", flight_in, [_weight_copies], (mod,))
    (win_g,) = _pair_forward(srcs, lands, "weights_pair_first")
    w_in_full = win_g.transpose(1, 0, 2).reshape(D, INW)

    def later_weights(after):
        srcs, lands = _split_wait("weights_wait_rest", flight_rest, [_weight_copies] * 3, after)
        wout_g, wup_g, wdn_g = _pair_forward(srcs, lands, "weights_pair_rest")
        return wout_g.reshape(D, D), wup_g, wdn_g.reshape(DFF, D)

    pending = {}

    def grad_ready(group, g):
        if group == "w_down":
            names, slabs = ("w_down",), [g.reshape(NCHIP, DFF // NCHIP, D)]
        elif group == "w_up":
            names, slabs = ("w_up",), [g]
        else:
            names = ("w_in", "w_out")
            slabs = [g[0].reshape(D, NCHIP, INW // NCHIP).transpose(1, 0, 2), g[1].reshape(NCHIP, D // NCHIP, D)]
        landed = _rs_pair(slabs, "grad_pair_exchange_" + group)
        halves = [_pair_add(s, p, c_arr, "grad_pair_sum_" + n) for s, p, n in zip(slabs, landed, names)]
        fl, tk = _split_start("grad_start_" + group, halves,
                              [jax.ShapeDtypeStruct((NLINK,) + h.shape[1:], bf16) for h in halves], [_grad_copies] * len(names))
        pending[group] = (names, fl)
        return tk

    grad_x, accs = _local_step(x, positions, mod, loss_target, w_in_full, later_weights, grad_ready,
                               g_attn_pre, g_attn_post, sink_a, g_mix_a, g_mix_b, g_mlp_pre, g_mlp_post)

    def finish(groups, after):
        names = sum((pending[g][0] for g in groups), ())
        fl = sum((pending[g][1] for g in groups), [])
        halves, landed = _split_wait("grad_wait_" + groups[0], fl, [_grad_copies] * len(names), after)
        fulls = [_chip_add(h, q, pos_arr, "grad_chip_sum_" + n) for h, q, n in zip(halves, landed, names)]
        return dict(zip(names, _ag_pair(fulls, "grad_pair_gather_" + groups[0])))

    grads, out = {}, {}

    def update(n):
        d, m2, v2 = _adamw(given[n][0], grads[n], moms[n][0][0], moms[n][1][0], "adamw_" + n)
        out[n] = (grads[n][None], d[None], m2[None], v2[None])
        return v2

    grads.update(finish(("w_down", "w_up"), (accs[-1],)))
    last = [update(n) for n in ("w_down", "w_up")]
    grads["w_ada"], gb, small = _small_allreduce(accs, cond_all, after=tuple(last))
    update("w_ada")
    grads.update(finish(("w_in_w_out",), (small,)))
    update("w_in")
    update("w_out")
    loss, res = _adamw_small(small, gb, [(given[n], moms[n][0], moms[n][1]) for n, _, _ in SMALL])
    for n, _, _ in SMALL:
        out[n] = tuple(res[n])
    return (loss.reshape(()), grad_x, *[out[n][0] for n in order], *[out[n][1] for n in order],
            *[out[n][2] for n in order], *[out[n][3] for n in order])
```

```python
import functools

import numpy as np
import jax
import jax.numpy as jnp
from jax import lax
from jax.experimental import pallas as pl
from jax.experimental.pallas import tpu as pltpu

f32 = jnp.float32
bf16 = jnp.bfloat16
MESH = pl.DeviceIdType.MESH

D = 1024
SEQ = 2048
BL = 2
HD = 64
AQ = 512
AKV = 128
BW = 512
INW = 2304
DFF = 4096
NMOD = 6
ROT = 16
THETA = 500000.0
EPS = 1e-6
NEG = -1e30
BLK = 128
TM = 256
NJ = SEQ // TM
LANES = 128
NCHIP = 4
NDEV = 8
VMEM_LIMIT = 56 << 20

LR, B1, B2, AEPS, WD, STEP = 0.001, 0.9, 0.999, 1e-08, 0.01, 10

OFF_G_ATTN_PRE, OFF_G_ATTN_POST, OFF_G_MIX_A, OFF_G_MIX_B = 0, 1024, 2048, 2560
OFF_G_MLP_PRE, OFF_G_MLP_POST, OFF_SINK, OFF_LOSS = 3072, 4096, 5120, 5248
PAYW = NMOD * D


def _cp(sem=None):
    return pltpu.CompilerParams(dimension_semantics=sem, vmem_limit_bytes=VMEM_LIMIT)


def _dot(a, b):
    return jnp.dot(a, b, preferred_element_type=f32)


def _dot_nt(a, b):
    return lax.dot_general(a, b, (((1,), (1,)), ((), ())), preferred_element_type=f32)


def _dot_tn(a, b):
    return lax.dot_general(a, b, (((0,), (0,)), ((), ())), preferred_element_type=f32)


def _rms(x):
    r = lax.rsqrt(jnp.mean(x * x, axis=-1, keepdims=True) + EPS)
    return x * r, r


def _rms_bwd(dy, y, r):
    return r * (dy - y * jnp.mean(dy * y, axis=-1, keepdims=True))


def _colsum(v):
    return jnp.sum(v, axis=0, keepdims=True)


def _rope(p, c, s1, s2):
    outs = []
    for c0 in range(0, p.shape[1], LANES):
        pc = p[:, c0:c0 + LANES]
        outs.append(pc * c + pltpu.roll(pc, LANES - ROT // 2, 1) * s1 + pltpu.roll(pc, ROT // 2, 1) * s2)
    return outs[0] if len(outs) == 1 else jnp.concatenate(outs, axis=1)


def _rope_t(g, c, s1, s2):
    outs = []
    for c0 in range(0, g.shape[1], LANES):
        gc = g[:, c0:c0 + LANES]
        outs.append(gc * c + pltpu.roll(gc * s1, ROT // 2, 1) + pltpu.roll(gc * s2, LANES - ROT // 2, 1))
    return outs[0] if len(outs) == 1 else jnp.concatenate(outs, axis=1)


def _perm_store(val, scr, out_ref, d):
    nc = val.shape[1] // LANES
    for c in range(nc):
        scr[c] = val[:, LANES * c:LANES * (c + 1)]
    for c in range(nc):
        for r in range(d):
            out_ref[r, :, LANES * c:LANES * (c + 1)] = scr[c, pl.ds(r, TM // d, stride=d), :].astype(out_ref.dtype)


def _perm_load(in_ref, scr, d):
    nc = in_ref.shape[-1] // LANES
    for c in range(nc):
        for r in range(d):
            scr[c, pl.ds(r, TM // d, stride=d), :] = in_ref[r, :, LANES * c:LANES * (c + 1)].astype(f32)
    return jnp.concatenate([scr[c] for c in range(nc)], axis=1)


def _tok(w, dtype=None):
    return pl.BlockSpec((None, TM, w), lambda b, j: (b, j, 0))


def _perm_spec(d, w):
    return pl.BlockSpec((None, d, TM // d, w), lambda b, j: (b, 0, j, 0))


def _full(shape):
    n = len(shape)
    return pl.BlockSpec(shape, lambda b, j: (0,) * n)


MOD_SPEC = pl.BlockSpec((None, NMOD, D), lambda b, j: (b, 0, 0))
ACCB_SPEC = pl.BlockSpec((None, 8, D), lambda b, j: (b, 0, 0))
ACCG_SPEC = pl.BlockSpec((8, D), lambda b, j: (0, 0))
ACC_SHAPES = [jax.ShapeDtypeStruct((BL, 8, D), f32), jax.ShapeDtypeStruct((8, D), f32)]


def _acc_init(accb_ref, accg_ref):
    b, j = pl.program_id(0), pl.program_id(1)

    @pl.when(j == 0)
    def _():
        accb_ref[...] = jnp.zeros_like(accb_ref)

    @pl.when((b == 0) & (j == 0))
    def _():
        accg_ref[...] = jnp.zeros_like(accg_ref)


def _rope_tables(pos_col, inv_lane):
    def body(p_ref, inv_ref, c_ref, s1_ref, s2_ref):
        ang = p_ref[...].astype(f32) * inv_ref[...]
        j = lax.broadcasted_iota(jnp.int32, (TM, LANES), 1) % HD
        cs, sn = jnp.cos(ang), jnp.sin(ang)
        c_ref[...] = jnp.where(j < ROT, cs, 1.0)
        s1_ref[...] = jnp.where(j < ROT // 2, -sn, 0.0)
        s2_ref[...] = jnp.where((j >= ROT // 2) & (j < ROT), sn, 0.0)

    n = BL * SEQ // TM
    return pl.pallas_call(
        body, name="rope_tables", grid=(n,),
        in_specs=[pl.BlockSpec((TM, 1), lambda i: (i, 0)), pl.BlockSpec((1, LANES), lambda i: (0, 0))],
        out_specs=[pl.BlockSpec((TM, LANES), lambda i: (i, 0))] * 3,
        out_shape=[jax.ShapeDtypeStruct((BL * SEQ, LANES), f32)] * 3,
    )(pos_col, inv_lane)


def _attn_in(x, mod, g_pre, w_in, tc, ts1, ts2):
    def body(x_ref, mod_ref, g_ref, w_ref, c_ref, s1_ref, s2_ref,
             h_ref, qa_ref, ka_ref, va_ref, q1_ref, k1_ref, v1_ref, q4_ref, k4_ref, v4_ref, q16_ref, k16_ref, v16_ref,
             scr):
        xn, _ = _rms(x_ref[...])
        h = (xn * g_ref[...]) * (1.0 + mod_ref[1:2, :]) + mod_ref[0:1, :]
        hb = h.astype(bf16)
        h_ref[...] = hb
        proj = _dot(hb, w_ref[...])
        c, s1, s2 = c_ref[...], s1_ref[...], s2_ref[...]
        o1, o2, o3, o4, o5 = AQ, AQ + AKV, AQ + 2 * AKV, AQ + 2 * AKV + BW, AQ + 2 * AKV + 2 * BW
        qa_ref[...] = (_rope(proj[:, :o1], c, s1, s2) * 0.125).astype(bf16)
        ka_ref[...] = _rope(proj[:, o1:o2], c, s1, s2).astype(bf16)
        va_ref[...] = proj[:, o2:o3].astype(bf16)
        qb = _rope(proj[:, o3:o4], c, s1, s2) * 0.125
        kb = _rope(proj[:, o4:o5], c, s1, s2)
        vb = proj[:, o5:]
        for val, r1, r4, r16 in ((qb, q1_ref, q4_ref, q16_ref), (kb, k1_ref, k4_ref, k16_ref), (vb, v1_ref, v4_ref, v16_ref)):
            r1[...] = val.astype(bf16)
            _perm_store(val, scr, r4, 4)
            _perm_store(val, scr, r16, 16)

    nat = lambda w: jax.ShapeDtypeStruct((BL, SEQ, w), bf16)
    p4 = jax.ShapeDtypeStruct((BL, 4, SEQ // 4, BW), bf16)
    p16 = jax.ShapeDtypeStruct((BL, 16, SEQ // 16, BW), bf16)
    return pl.pallas_call(
        body, name="attn_in", grid=(BL, NJ),
        in_specs=[_tok(D), MOD_SPEC, _full((1, D)), _full((D, INW)), _tok(LANES), _tok(LANES), _tok(LANES)],
        out_specs=[_tok(D), _tok(AQ), _tok(AKV), _tok(AKV)] + [_tok(BW)] * 3 + [_perm_spec(4, BW)] * 3 + [_perm_spec(16, BW)] * 3,
        out_shape=[nat(D), nat(AQ), nat(AKV), nat(AKV)] + [nat(BW)] * 3 + [p4] * 3 + [p16] * 3,
        scratch_shapes=[pltpu.VMEM((BW // LANES, TM, LANES), f32)],
        compiler_params=_cp(("arbitrary", "arbitrary")),
    )(x, mod, g_pre, w_in, tc, ts1, ts2)


def _kv_cat(cur_ref, prev_ref, p, gqa, cache):
    def one(ref):
        if not gqa:
            return ref[:, LANES * p:LANES * (p + 1)]
        k = ref[...]
        kr = pltpu.roll(k, HD, 1)
        lo = lax.broadcasted_iota(jnp.int32, k.shape, 1) < HD
        return jnp.where(lo, k, kr) if p < 2 else jnp.where(lo, kr, k)

    key = (id(cur_ref), p // 2 if gqa else p)
    if key not in cache:
        cache[key] = one(cur_ref) if prev_ref is None else jnp.concatenate([one(prev_ref), one(cur_ref)], axis=0)
    return cache[key]


def _lane_half(a, hh):
    lo = lax.broadcasted_iota(jnp.int32, a.shape, 1) < HD
    return jnp.where(lo, a, jnp.zeros_like(a)) if hh == 0 else jnp.where(lo, jnp.zeros_like(a), a)


def _attn_fwd(q, k, v, sink, *, max_dist, o_dtype, name):
    n, l, w = q.shape
    wk = k.shape[-1]
    nb = l // BLK
    gqa = wk != w
    has_sink = sink is not None

    def body(*refs):
        if has_sink:
            sink_ref, refs = refs[0], refs[1:]
        if nb > 1:
            q_ref, kc_ref, kp_ref, vc_ref, vp_ref, o_ref, lse_ref, sscr, pscr, dscr = refs
        else:
            q_ref, kc_ref, vc_ref, o_ref, lse_ref, sscr, pscr, dscr = refs
        i = pl.program_id(1)
        qi = lax.broadcasted_iota(jnp.int32, (BLK, BLK), 0)
        kj = lax.broadcasted_iota(jnp.int32, (BLK, BLK), 1)
        tri = kj <= qi
        eye = kj == qi
        cache = {}
        for p in range(w // LANES):
            qpair = q_ref[:, LANES * p:LANES * (p + 1)]
            kcat = _kv_cat(kc_ref, kp_ref if nb > 1 else None, p, gqa, cache)
            for hh in range(2):
                s = _dot_nt(_lane_half(qpair, hh), kcat)
                if nb > 1:
                    sp = jnp.where(i > 0, s[:, :BLK], NEG)
                    sscr[2 * p + hh] = jnp.where(tri, s[:, BLK:], sp)
                    if diag:
                        dscr[2 * p + hh] = jnp.where(eye, sp, NEG)
                else:
                    sscr[2 * p + hh] = jnp.where(tri, s, NEG)
        lane = lax.broadcasted_iota(jnp.int32, (BLK, LANES), 1)
        lse_all = jnp.zeros((BLK, LANES), f32)
        for p in range(w // LANES):
            for hh in range(2):
                h = 2 * p + hh
                comb = sscr[h]
                if diag:
                    dtile = dscr[h]
                    m = jnp.max(jnp.maximum(comb, dtile), axis=-1, keepdims=True)
                else:
                    m = jnp.max(comb, axis=-1, keepdims=True)
                if has_sink:
                    sk = sink_ref[0, h]
                    m = jnp.maximum(m, sk)
                e = jnp.exp(comb - m)
                if diag:
                    ed = jnp.exp(dtile - m)
                    den = jnp.sum(e + ed, axis=-1, keepdims=True)
                else:
                    den = jnp.sum(e, axis=-1, keepdims=True)
                if has_sink:
                    den = den + jnp.exp(sk - m)
                inv = 1.0 / den
                if nb > 1:
                    pscr[h, :, :BLK] = (jnp.where(tri, ed if diag else 0.0, e) * inv).astype(bf16)
                    pscr[h, :, BLK:] = (jnp.where(tri, e, 0.0) * inv).astype(bf16)
                else:
                    pscr[h] = (e * inv).astype(bf16)
                lse_all = jnp.where(lane == h, jnp.broadcast_to(m + jnp.log(den), (BLK, LANES)), lse_all)
        lse_ref[...] = lse_all
        for p in range(w // LANES):
            vcat = _kv_cat(vc_ref, vp_ref if nb > 1 else None, p, gqa, cache)
            key = ("halves", id(vc_ref), p // 2 if gqa else p)
            if key not in cache:
                cache[key] = (_lane_half(vcat, 0), _lane_half(vcat, 1))
            o_ref[:, LANES * p:LANES * (p + 1)] = (_dot(pscr[2 * p], cache[key][0])
                                                   + _dot(pscr[2 * p + 1], cache[key][1])).astype(o_ref.dtype)

    assert max_dist in (BLK - 1, BLK)
    diag = nb > 1 and max_dist == BLK
    cur = lambda ww: pl.BlockSpec((None, BLK, ww), lambda a, i: (a, i, 0))
    prev = lambda ww: pl.BlockSpec((None, BLK, ww), lambda a, i: (a, jnp.maximum(i - 1, 0), 0))
    in_specs = [cur(w), cur(wk)] + ([prev(wk)] if nb > 1 else []) + [cur(wk)] + ([prev(wk)] if nb > 1 else [])
    args = [q, k] + ([k] if nb > 1 else []) + [v] + ([v] if nb > 1 else [])
    if has_sink:
        in_specs = [pl.BlockSpec(memory_space=pltpu.SMEM)] + in_specs
        args = [sink] + args
    return pl.pallas_call(
        body, name=name, grid=(n, nb), in_specs=in_specs,
        out_specs=[cur(w), cur(LANES)],
        out_shape=[jax.ShapeDtypeStruct((n, l, w), o_dtype), jax.ShapeDtypeStruct((n, l, LANES), f32)],
        scratch_shapes=[pltpu.VMEM((w // HD, BLK, BLK), f32), pltpu.VMEM((w // HD, BLK, 2 * BLK if nb > 1 else BLK), bf16),
                        pltpu.VMEM((w // HD if diag else 1, BLK, BLK), f32)],
        compiler_params=_cp(("arbitrary", "arbitrary")),
    )(*args)


def _attn_bwd(q, k, v, do, delta, lse, sink, *, max_dist, name):
    n, l, w = q.shape
    wk = k.shape[-1]
    nb = l // BLK
    gqa = wk != w
    has_sink = sink is not None

    def body(*refs):
        if has_sink:
            sink_ref, refs = refs[0], refs[1:]
        if nb > 1:
            q_ref, kc_ref, kp_ref, vc_ref, vp_ref, do_ref, delta_ref, lse_ref = refs[:8]
            rest = refs[8:]
        else:
            q_ref, kc_ref, vc_ref, do_ref, delta_ref, lse_ref = refs[:6]
            rest = refs[6:]
        if has_sink:
            dq_ref, dk_ref, dv_ref, dsink_ref = rest[:4]
            rest = rest[4:]
        else:
            dq_ref, dk_ref, dv_ref = rest[:3]
            rest = rest[3:]
        step = pl.program_id(1)
        blk_idx = nb - 1 - step
        if nb > 1:
            ck, cv = rest[:2]
            rest = rest[2:]

            @pl.when(step == 0)
            def _():
                ck[...] = jnp.zeros_like(ck)
                cv[...] = jnp.zeros_like(cv)

        sscr, dpscr, pscr, dsscr = rest[:4]
        if diag:
            dscr, ddscr = rest[4:]
        if has_sink:
            @pl.when((pl.program_id(0) == 0) & (step == 0))
            def _():
                dsink_ref[...] = jnp.zeros_like(dsink_ref)

        lane = lax.broadcasted_iota(jnp.int32, (BLK, LANES), 1)
        lo = lane < HD
        qi = lax.broadcasted_iota(jnp.int32, (BLK, BLK), 0)
        kj = lax.broadcasted_iota(jnp.int32, (BLK, BLK), 1)
        tri = kj <= qi
        eye = kj == qi
        cache = {}
        kp, vp = (kp_ref, vp_ref) if nb > 1 else (None, None)
        rows = 2 * BLK if nb > 1 else BLK
        for p in range(w // LANES):
            sl = slice(LANES * p, LANES * (p + 1))
            qpair, dopair = q_ref[:, sl], do_ref[:, sl]
            kcat, vcat = _kv_cat(kc_ref, kp, p, gqa, cache), _kv_cat(vc_ref, vp, p, gqa, cache)
            for hh in range(2):
                h = 2 * p + hh
                s = _dot_nt(_lane_half(qpair, hh), kcat)
                dp = _dot_nt(_lane_half(dopair, hh), vcat)
                if nb > 1:
                    sp = jnp.where(blk_idx > 0, s[:, :BLK], NEG)
                    sscr[h] = jnp.where(tri, s[:, BLK:], sp)
                    dpscr[h] = jnp.where(tri, dp[:, BLK:], dp[:, :BLK])
                    if diag:
                        dscr[h] = jnp.where(eye, sp, NEG)
                        ddscr[h] = dp[:, :BLK]
                else:
                    sscr[h] = jnp.where(tri, s, NEG)
                    dpscr[h] = dp
        for p in range(w // LANES):
            for hh in range(2):
                h = 2 * p + hh
                lse_b = jnp.broadcast_to(lse_ref[:, h:h + 1], (BLK, BLK))
                delta = jnp.broadcast_to(delta_ref[:, h:h + 1], (BLK, BLK))
                pr = jnp.exp(sscr[h] - lse_b)
                ds = pr * (dpscr[h] - delta)
                if nb > 1:
                    if diag:
                        prd = jnp.exp(dscr[h] - lse_b)
                        dsd = prd * (ddscr[h] - delta)
                    else:
                        prd = dsd = 0.0
                    pscr[h, :, :BLK] = jnp.where(tri, prd, pr).astype(bf16)
                    pscr[h, :, BLK:] = jnp.where(tri, pr, 0.0).astype(bf16)
                    dsscr[h, :, :BLK] = jnp.where(tri, dsd, ds).astype(bf16)
                    dsscr[h, :, BLK:] = jnp.where(tri, ds, 0.0).astype(bf16)
                else:
                    pscr[h] = pr.astype(bf16)
                    dsscr[h] = ds.astype(bf16)
                if has_sink:
                    dsk = -jnp.sum(jnp.where(lane == 0, jnp.exp(sink_ref[0, h] - lse_b) * delta, 0.0), keepdims=True)
                    dsink_ref[h:h + 1, :] += jnp.broadcast_to(dsk, (1, LANES))
        gk = [jnp.zeros((rows, LANES), f32), jnp.zeros((rows, LANES), f32)]
        gv = [jnp.zeros((rows, LANES), f32), jnp.zeros((rows, LANES), f32)]
        for p in range(w // LANES):
            sl = slice(LANES * p, LANES * (p + 1))
            qpair, dopair = q_ref[:, sl], do_ref[:, sl]
            kcat = _kv_cat(kc_ref, kp, p, gqa, cache)
            key = ("halves", p // 2 if gqa else p)
            if key not in cache:
                cache[key] = (_lane_half(kcat, 0), _lane_half(kcat, 1))
            dq_ref[:, sl] = _dot(dsscr[2 * p], cache[key][0]) + _dot(dsscr[2 * p + 1], cache[key][1])
            dk_pair = _dot_tn(dsscr[2 * p], _lane_half(qpair, 0)) + _dot_tn(dsscr[2 * p + 1], _lane_half(qpair, 1))
            dv_pair = _dot_tn(pscr[2 * p], _lane_half(dopair, 0)) + _dot_tn(pscr[2 * p + 1], _lane_half(dopair, 1))
            if gqa:
                gk[p // 2] = gk[p // 2] + dk_pair
                gv[p // 2] = gv[p // 2] + dv_pair
            elif nb > 1:
                dk_ref[:, sl] = dk_pair[BLK:] + ck[:, sl]
                dv_ref[:, sl] = dv_pair[BLK:] + cv[:, sl]
                ck[:, sl] = dk_pair[:BLK]
                cv[:, sl] = dv_pair[:BLK]
            else:
                dk_ref[:, sl] = dk_pair
                dv_ref[:, sl] = dv_pair
        if gqa:
            lor = lax.broadcasted_iota(jnp.int32, (rows, LANES), 1) < HD
            fold = lambda g: jnp.where(lor, g[0] + pltpu.roll(g[0], HD, 1), g[1] + pltpu.roll(g[1], HD, 1))
            dk_full, dv_full = fold(gk), fold(gv)
            dk_ref[...] = dk_full[BLK:] + ck[...]
            dv_ref[...] = dv_full[BLK:] + cv[...]
            ck[...] = dk_full[:BLK]
            cv[...] = dv_full[:BLK]

    assert max_dist in (BLK - 1, BLK)
    diag = nb > 1 and max_dist == BLK
    cur = lambda ww: pl.BlockSpec((None, BLK, ww), lambda a, i: (a, nb - 1 - i, 0))
    prev = lambda ww: pl.BlockSpec((None, BLK, ww), lambda a, i: (a, jnp.maximum(nb - 2 - i, 0), 0))
    in_specs = ([cur(w), cur(wk)] + ([prev(wk)] if nb > 1 else []) + [cur(wk)] + ([prev(wk)] if nb > 1 else [])
                + [cur(w), cur(LANES), cur(LANES)])
    args = [q, k] + ([k] if nb > 1 else []) + [v] + ([v] if nb > 1 else []) + [do, delta, lse]
    out_specs = [cur(w), cur(wk), cur(wk)]
    out_shape = [jax.ShapeDtypeStruct((n, l, w), f32), jax.ShapeDtypeStruct((n, l, wk), f32), jax.ShapeDtypeStruct((n, l, wk), f32)]
    if has_sink:
        in_specs = [pl.BlockSpec(memory_space=pltpu.SMEM)] + in_specs
        args = [sink] + args
        out_specs.append(pl.BlockSpec((8, LANES), lambda a, i: (0, 0)))
        out_shape.append(jax.ShapeDtypeStruct((8, LANES), f32))
    nh = w // HD
    scratch = [pltpu.VMEM((BLK, wk), f32), pltpu.VMEM((BLK, wk), f32)] if nb > 1 else []
    scratch += [pltpu.VMEM((nh, BLK, BLK), f32)] * 2 + [pltpu.VMEM((nh, BLK, 2 * BLK if nb > 1 else BLK), bf16)] * 2
    if diag:
        scratch += [pltpu.VMEM((nh, BLK, BLK), f32)] * 2
    return pl.pallas_call(
        body, name=name, grid=(n, nb), in_specs=in_specs, out_specs=out_specs, out_shape=out_shape,
        scratch_shapes=scratch, compiler_params=_cp(("arbitrary", "arbitrary")),
    )(*args)


def _split3(x):
    hi = x.astype(bf16)
    r = x - hi.astype(f32)
    mid = r.astype(bf16)
    return hi, mid, (r - mid.astype(f32)).astype(bf16)


def _heads_to_lanes(xc, e):
    return sum(_dot(t, e) for t in _split3(xc))


def _lanes_to_heads(x, g):
    return sum(_dot(t, g) for t in _split3(x))


HEAD_EXPAND = (np.arange(LANES)[:, None] == np.arange(BW)[None, :] // HD).astype(np.float32)
HEAD_SUM = HEAD_EXPAND.T.copy()


def _branch_weights(l1_ref, l4_ref, l16_ref, scr):
    l4v = _perm_load(l4_ref, scr, 4)
    l16v = _perm_load(l16_ref, scr, 16)
    l1v = l1_ref[...]
    m = jnp.maximum(jnp.maximum(l1v, l4v), l16v)
    e1, e4, e16 = jnp.exp(l1v - m), jnp.exp(l4v - m), jnp.exp(l16v - m)
    z = e1 + e4 + e16
    return e1 / z, e4 / z, e16 / z


def _mix_out(oa, o1, l1, o4, l4, o16, l16, g_mix_a, g_mix_b, w_out, x, mod, g_post):
    def body(oa_ref, o1_ref, l1_ref, o4_ref, l4_ref, o16_ref, l16_ref, ga_ref, gb_ref, w_ref, x_ref, mod_ref, gp_ref, e_ref,
             x1_ref, y_ref, mixed_ref, ob_ref, scr):
        w1, w4, w16 = _branch_weights(l1_ref, l4_ref, l16_ref, scr)
        e = e_ref[...]
        ob = (_heads_to_lanes(w1, e) * o1_ref[...].astype(f32) + _heads_to_lanes(w4, e) * _perm_load(o4_ref, scr, 4)
              + _heads_to_lanes(w16, e) * _perm_load(o16_ref, scr, 16))
        ob_ref[...] = ob
        oan, _ = _rms(oa_ref[...])
        obn, _ = _rms(ob)
        mixed = jnp.concatenate([oan * ga_ref[...], obn * gb_ref[...]], axis=1).astype(bf16)
        mixed_ref[...] = mixed
        y = _dot(mixed, w_ref[...])
        y_ref[...] = y
        yn, _ = _rms(y)
        x1_ref[...] = x_ref[...] + mod_ref[2:3, :] * (yn * gp_ref[...])

    nat = lambda w, dt: jax.ShapeDtypeStruct((BL, SEQ, w), dt)
    return pl.pallas_call(
        body, name="mix_out", grid=(BL, NJ),
        in_specs=[_tok(AQ), _tok(BW), _tok(LANES), _perm_spec(4, BW), _perm_spec(4, LANES), _perm_spec(16, BW),
                  _perm_spec(16, LANES), _full((1, AQ)), _full((1, BW)), _full((D, D)), _tok(D), MOD_SPEC, _full((1, D)),
                  _full((LANES, BW))],
        out_specs=[_tok(D), _tok(D), _tok(D), _tok(BW)],
        out_shape=[nat(D, f32), nat(D, f32), nat(D, bf16), nat(BW, f32)],
        scratch_shapes=[pltpu.VMEM((BW // LANES, TM, LANES), f32)],
        compiler_params=_cp(("arbitrary", "arbitrary")),
    )(oa, o1, l1, o4, l4, o16, l16, g_mix_a, g_mix_b, w_out, x, mod, g_post, jnp.asarray(HEAD_EXPAND, bf16))


def _mlp_up(x1, mod, g_pre, w_up):
    def body(x_ref, mod_ref, g_ref, w_ref, h_ref, u_ref, a_ref):
        xn, _ = _rms(x_ref[...])
        h = (xn * g_ref[...]) * (1.0 + mod_ref[4:5, :]) + mod_ref[3:4, :]
        hb = h.astype(bf16)
        h_ref[...] = hb
        for s in range(NCHIP):
            u = _dot(hb, w_ref[s])
            u_ref[:, D * s:D * (s + 1)] = u.astype(bf16)
            a_ref[:, D * s:D * (s + 1)] = jnp.square(jnp.maximum(u, 0.0)).astype(bf16)

    nat = lambda w: jax.ShapeDtypeStruct((BL, SEQ, w), bf16)
    return pl.pallas_call(
        body, name="mlp_up", grid=(BL, NJ),
        in_specs=[_tok(D), MOD_SPEC, _full((1, D)), _full((NCHIP, D, D))],
        out_specs=[_tok(D), _tok(DFF), _tok(DFF)], out_shape=[nat(D), nat(DFF), nat(DFF)],
        compiler_params=_cp(("arbitrary", "arbitrary")),
    )(x1, mod, g_pre, w_up)


def _mlp_down(a, w_down, x1, target, mod, g_post):
    def body(a_ref, w_ref, x_ref, t_ref, mod_ref, g_ref, gx_ref, dy_ref, accb_ref, accg_ref):
        _acc_init(accb_ref, accg_ref)
        y2 = _dot(a_ref[...], w_ref[...])
        yn, r = _rms(y2)
        g = g_ref[...]
        gt = mod_ref[5:6, :]
        n2 = yn * g
        err = x_ref[...] + gt * n2 - t_ref[...]
        gout = err * (1.0 / D)
        gx_ref[...] = gout
        dn2 = gout * gt
        dy_ref[...] = _rms_bwd(dn2 * g, yn, r).astype(bf16)
        accb_ref[0:1, :] += _colsum(gout * n2)
        accg_ref[0:1, :] += _colsum(dn2 * yn)
        accg_ref[1:2, :] += jnp.broadcast_to(jnp.sum(err * err, keepdims=True), (1, D))

    return pl.pallas_call(
        body, name="mlp_down", grid=(BL, NJ),
        in_specs=[_tok(DFF), _full((DFF, D)), _tok(D), _tok(D), MOD_SPEC, _full((1, D))],
        out_specs=[_tok(D), _tok(D), ACCB_SPEC, ACCG_SPEC],
        out_shape=[jax.ShapeDtypeStruct((BL, SEQ, D), f32), jax.ShapeDtypeStruct((BL, SEQ, D), bf16)] + ACC_SHAPES,
        compiler_params=_cp(("arbitrary", "arbitrary")),
    )(a, w_down, x1, target, mod, g_post)


def _mlp_bwd(dy2, u, w_down, w_up, x1, gx, mod, g_pre):
    def body(dy_ref, u_ref, wd_hbm, wu_hbm, x_ref, gx_ref, mod_ref, g_ref, du_ref, gx1_ref, accb_ref, accg_ref, wd, wu, sem):
        _acc_init(accb_ref, accg_ref)

        @pl.when((pl.program_id(0) == 0) & (pl.program_id(1) == 0))
        def _():
            c1 = pltpu.make_async_copy(wd_hbm, wd, sem.at[0])
            c2 = pltpu.make_async_copy(wu_hbm, wu, sem.at[1])
            c1.start()
            c2.start()
            c1.wait()
            c2.wait()

        dy = dy_ref[...]
        dh = jnp.zeros((TM, D), f32)
        for s in range(NCHIP):
            sl = slice(D * s, D * (s + 1))
            da = _dot_nt(dy, wd[sl, :])
            du = (da * (2.0 * jnp.maximum(u_ref[:, sl].astype(f32), 0.0))).astype(bf16)
            du_ref[:, sl] = du
            dh = dh + _dot_nt(du, wu[s])
        xn, r = _rms(x_ref[...])
        g = g_ref[...]
        n = xn * g
        dn = dh * (1.0 + mod_ref[4:5, :])
        gx1_ref[...] = gx_ref[...] + _rms_bwd(dn * g, xn, r)
        accb_ref[0:1, :] += _colsum(dh * n)
        accb_ref[1:2, :] += _colsum(dh)
        accg_ref[0:1, :] += _colsum(dn * xn)

    anyspec = pl.BlockSpec(memory_space=pl.ANY)
    return pl.pallas_call(
        body, name="mlp_bwd", grid=(BL, NJ),
        in_specs=[_tok(D), _tok(DFF), anyspec, anyspec, _tok(D), _tok(D), MOD_SPEC, _full((1, D))],
        out_specs=[_tok(DFF), _tok(D), ACCB_SPEC, ACCG_SPEC],
        out_shape=[jax.ShapeDtypeStruct((BL, SEQ, DFF), bf16), jax.ShapeDtypeStruct((BL, SEQ, D), f32)] + ACC_SHAPES,
        scratch_shapes=[pltpu.VMEM((DFF, D), bf16), pltpu.VMEM((NCHIP, D, D), bf16), pltpu.SemaphoreType.DMA((2,))],
        compiler_params=_cp(("arbitrary", "arbitrary")),
    )(dy2, u, w_down, w_up, x1, gx, mod, g_pre)


def _matmul_tn(a, b, *, tn, col_blocked, name):
    t, m = a.shape
    n = b.shape[1]
    tmm = min(m, 1024)
    tk = 512
    nk = t // tk

    def body(a_ref, b_ref, o_ref):
        @pl.when(pl.program_id(2) == 0)
        def _():
            o_ref[...] = jnp.zeros_like(o_ref)

        o_ref[...] += _dot_tn(a_ref[...], b_ref[...])

    if col_blocked:
        out_spec = pl.BlockSpec((None, tmm, tn), lambda i, j, k: (j, i, 0))
        out_shape = jax.ShapeDtypeStruct((n // tn, m, tn), f32)
    else:
        out_spec = pl.BlockSpec((tmm, tn), lambda i, j, k: (i, j))
        out_shape = jax.ShapeDtypeStruct((m, n), f32)
    return pl.pallas_call(
        body, name=name, grid=(m // tmm, n // tn, nk),
        in_specs=[pl.BlockSpec((tk, tmm), lambda i, j, k: (k, i)), pl.BlockSpec((tk, tn), lambda i, j, k: (k, j))],
        out_specs=out_spec, out_shape=out_shape,
        compiler_params=_cp(("arbitrary", "arbitrary", "arbitrary")),
    )(a, b)


def _attn_out_bwd(gx1, y, mod, g_post, w_out, oa, ob, g_mix_a, g_mix_b, l1, l4, l16):
    def body(gx_ref, y_ref, mod_ref, gp_ref, w_ref, oa_ref, ob_ref, ga_ref, gb_ref, l1_ref, l4_ref, l16_ref, e_ref, g_ref,
             dy_ref, doa_ref, do1_ref, do4_ref, do16_ref, da_ref, d1_ref, d4_ref, d16_ref, accb_ref, accg_ref, scr):
        _acc_init(accb_ref, accg_ref)
        w1, w4, w16 = _branch_weights(l1_ref, l4_ref, l16_ref, scr)
        e, hs = e_ref[...], g_ref[...]
        gx1v = gx_ref[...]
        yn, ry = _rms(y_ref[...])
        gp = gp_ref[...]
        gt = mod_ref[2:3, :]
        dn1 = gx1v * gt
        dy = _rms_bwd(dn1 * gp, yn, ry).astype(bf16)
        dy_ref[...] = dy
        dmixed = _dot_nt(dy, w_ref[...])
        dma, dmb = dmixed[:, :AQ], dmixed[:, AQ:]
        oa, ob = oa_ref[...], ob_ref[...]
        oan, ra = _rms(oa)
        obn, rb = _rms(ob)
        doa = _rms_bwd(dma * ga_ref[...], oan, ra)
        doa_ref[...] = doa.astype(bf16)
        da_ref[...] = _lanes_to_heads(doa * oa, hs)
        dob = _rms_bwd(dmb * gb_ref[...], obn, rb)
        dd = _lanes_to_heads(dob * ob, hs)
        do1_ref[...] = (_heads_to_lanes(w1, e) * dob).astype(bf16)
        d1_ref[...] = w1 * dd
        _perm_store(_heads_to_lanes(w4, e) * dob, scr, do4_ref, 4)
        _perm_store(w4 * dd, scr, d4_ref, 4)
        _perm_store(_heads_to_lanes(w16, e) * dob, scr, do16_ref, 16)
        _perm_store(w16 * dd, scr, d16_ref, 16)
        accb_ref[0:1, :] += _colsum(gx1v * (yn * gp))
        accg_ref[0:1, :] += _colsum(dn1 * yn)
        accg_ref[1:2, :] += jnp.concatenate([_colsum(dma * oan), _colsum(dmb * obn)], axis=1)

    nat = lambda w, dt: jax.ShapeDtypeStruct((BL, SEQ, w), dt)
    return pl.pallas_call(
        body, name="attn_out_bwd", grid=(BL, NJ),
        in_specs=[_tok(D), _tok(D), MOD_SPEC, _full((1, D)), _full((D, D)), _tok(AQ), _tok(BW), _full((1, AQ)), _full((1, BW)),
                  _tok(LANES), _perm_spec(4, LANES), _perm_spec(16, LANES), _full((LANES, BW)), _full((BW, LANES))],
        out_specs=[_tok(D), _tok(AQ), _tok(BW), _perm_spec(4, BW), _perm_spec(16, BW),
                   _tok(LANES), _tok(LANES), _perm_spec(4, LANES), _perm_spec(16, LANES), ACCB_SPEC, ACCG_SPEC],
        out_shape=[nat(D, bf16), nat(AQ, bf16), nat(BW, bf16), jax.ShapeDtypeStruct((BL, 4, SEQ // 4, BW), bf16),
                   jax.ShapeDtypeStruct((BL, 16, SEQ // 16, BW), bf16), nat(LANES, f32), nat(LANES, f32),
                   jax.ShapeDtypeStruct((BL, 4, SEQ // 4, LANES), f32), jax.ShapeDtypeStruct((BL, 16, SEQ // 16, LANES), f32)]
                  + ACC_SHAPES,
        scratch_shapes=[pltpu.VMEM((BW // LANES, TM, LANES), f32)],
        compiler_params=_cp(("arbitrary", "arbitrary")),
    )(gx1, y, mod, g_post, w_out, oa, ob, g_mix_a, g_mix_b, l1, l4, l16, jnp.asarray(HEAD_EXPAND, bf16),
      jnp.asarray(HEAD_SUM, bf16))


def _attn_in_bwd(dqa, dka, dva, d1, d4, d16, tc, ts1, ts2, w_in, x, gx1, mod, g_pre):
    def body(dqa_ref, dka_ref, dva_ref, dq1_ref, dk1_ref, dv1_ref, dq4_ref, dk4_ref, dv4_ref, dq16_ref, dk16_ref, dv16_ref,
             c_ref, s1_ref, s2_ref, w_ref, x_ref, gx_ref, mod_ref, g_ref, dproj_ref, dx_ref, accb_ref, accg_ref, scr):
        _acc_init(accb_ref, accg_ref)
        c, s1, s2 = c_ref[...], s1_ref[...], s2_ref[...]
        tot = lambda r1, r4, r16: r1[...] + _perm_load(r4, scr, 4) + _perm_load(r16, scr, 16)
        dqb = tot(dq1_ref, dq4_ref, dq16_ref)
        dkb = tot(dk1_ref, dk4_ref, dk16_ref)
        dvb = tot(dv1_ref, dv4_ref, dv16_ref)
        dproj = jnp.concatenate([
            _rope_t(dqa_ref[...], c, s1, s2) * 0.125, _rope_t(dka_ref[...], c, s1, s2), dva_ref[...],
            _rope_t(dqb, c, s1, s2) * 0.125, _rope_t(dkb, c, s1, s2), dvb], axis=1).astype(bf16)
        dproj_ref[...] = dproj
        dh = _dot_nt(dproj, w_ref[...])
        xn, r = _rms(x_ref[...])
        g = g_ref[...]
        dn = dh * (1.0 + mod_ref[1:2, :])
        dx_ref[...] = gx_ref[...] + _rms_bwd(dn * g, xn, r)
        accb_ref[0:1, :] += _colsum(dh * (xn * g))
        accb_ref[1:2, :] += _colsum(dh)
        accg_ref[0:1, :] += _colsum(dn * xn)

    return pl.pallas_call(
        body, name="attn_in_bwd", grid=(BL, NJ),
        in_specs=[_tok(AQ), _tok(AKV), _tok(AKV)] + [_tok(BW)] * 3 + [_perm_spec(4, BW)] * 3 + [_perm_spec(16, BW)] * 3
                 + [_tok(LANES)] * 3 + [_full((D, INW)), _tok(D), _tok(D), MOD_SPEC, _full((1, D))],
        out_specs=[_tok(INW), _tok(D), ACCB_SPEC, ACCG_SPEC],
        out_shape=[jax.ShapeDtypeStruct((BL, SEQ, INW), bf16), jax.ShapeDtypeStruct((BL, SEQ, D), f32)] + ACC_SHAPES,
        scratch_shapes=[pltpu.VMEM((BW // LANES, TM, LANES), f32)],
        compiler_params=_cp(("arbitrary", "arbitrary")),
    )(dqa, dka, dva, *d1, *d4, *d16, tc, ts1, ts2, w_in, x, gx1, mod, g_pre)


def _local_step(x, positions, mod, target, w_in, later_weights, grad_ready, g_attn_pre, g_attn_post, sink_a, g_mix_a, g_mix_b,
                g_mlp_pre, g_mlp_post):
    inv = np.float32(THETA) ** (-np.arange(0, ROT, 2, dtype=np.float32) / np.float32(ROT))
    lane = np.arange(LANES) % HD
    inv_lane = jnp.asarray(np.where(lane < ROT, inv[lane % (ROT // 2)], 0.0).astype(np.float32)[None, :])
    tabs = _rope_tables(positions.reshape(BL * SEQ, 1), inv_lane)
    tc, ts1, ts2 = [t.reshape(BL, SEQ, LANES) for t in tabs]

    (h, qa, ka, va, q1, k1, v1, q4, k4, v4, q16, k16, v16) = _attn_in(x, mod, g_attn_pre, w_in, tc, ts1, ts2)
    seqs = lambda t: t.reshape(t.shape[0] * t.shape[1], t.shape[2], t.shape[3])
    q4, k4, v4, q16, k16, v16 = [seqs(t) for t in (q4, k4, v4, q16, k16, v16)]
    oa, la = _attn_fwd(qa, ka, va, sink_a, max_dist=BLK - 1, o_dtype=f32, name="attn_a_fwd")
    o1, l1 = _attn_fwd(q1, k1, v1, None, max_dist=BLK, o_dtype=bf16, name="attn_b1_fwd")
    o4, l4 = _attn_fwd(q4, k4, v4, None, max_dist=BLK, o_dtype=bf16, name="attn_b4_fwd")
    o16, l16 = _attn_fwd(q16, k16, v16, None, max_dist=BLK, o_dtype=bf16, name="attn_b16_fwd")
    b4 = lambda t: t.reshape(BL, 4, SEQ // 4, t.shape[-1])
    b16 = lambda t: t.reshape(BL, 16, SEQ // 16, t.shape[-1])
    w_out, w_up, w_down = later_weights((oa, o1, o4, o16))
    x1, y, mixed, ob = _mix_out(oa, o1, l1, b4(o4), b4(l4), b16(o16), b16(l16), g_mix_a, g_mix_b, w_out, x, mod, g_attn_post)
    h2, u, a = _mlp_up(x1, mod, g_mlp_pre, w_up)
    gx, dy2, accb_d, accg_d = _mlp_down(a, w_down, x1, target, mod, g_mlp_post)

    flat = lambda t: t.reshape(BL * SEQ, t.shape[-1])
    mod = mod + grad_ready("w_down", _matmul_tn(flat(a), flat(dy2), tn=D, col_blocked=False, name="grad_w_down"))
    du, gx1, accb_m, accg_m = _mlp_bwd(dy2, u, w_down, w_up, x1, gx, mod, g_mlp_pre)
    mod = mod + grad_ready("w_up", _matmul_tn(flat(h2), flat(du), tn=D, col_blocked=True, name="grad_w_up"))

    dy, doa, do1, do4, do16, da, dl1, dl4, dl16, accb_o, accg_o = _attn_out_bwd(
        gx1, y, mod, g_attn_post, w_out, oa, ob, g_mix_a, g_mix_b, l1, b4(l4), b16(l16))
    gw_out = _matmul_tn(flat(mixed), flat(dy), tn=D, col_blocked=False, name="grad_w_out")
    dqa, dka, dva, dsink = _attn_bwd(qa, ka, va, doa, da, la, sink_a, max_dist=BLK - 1, name="attn_a_bwd")
    d1 = _attn_bwd(q1, k1, v1, do1, dl1, l1, None, max_dist=BLK, name="attn_b1_bwd")
    d4 = _attn_bwd(q4, k4, v4, seqs(do4), seqs(dl4), l4, None, max_dist=BLK, name="attn_b4_bwd")
    d16 = _attn_bwd(q16, k16, v16, seqs(do16), seqs(dl16), l16, None, max_dist=BLK, name="attn_b16_bwd")
    dproj, grad_x, accb_i, accg_i = _attn_in_bwd(dqa, dka, dva, d1, [b4(t) for t in d4], [b16(t) for t in d16],
                                                 tc, ts1, ts2, w_in, x, gx1, mod, g_attn_pre)
    gw_in = _matmul_tn(flat(h), flat(dproj), tn=INW, col_blocked=False, name="grad_w_in")
    dsink = dsink + grad_ready("w_in_w_out", (gw_in, gw_out))

    return grad_x, (accb_i, accb_o, accb_m, accb_d, accg_i, accg_o, accg_m, accg_d, dsink)


ADAW = NMOD * D // NCHIP


def _pos():
    return lax.axis_index("x"), lax.axis_index("y"), lax.axis_index("c")


def _flip(v, bit):
    return 1 - v if bit else v


def _all_peers(x, y, c):
    return [(_flip(x, k >> 2 & 1), _flip(y, k >> 1 & 1), _flip(c, k & 1)) for k in range(1, NDEV)]


def _other_chips(x, y):
    return [(1 - x, y), (x, 1 - y), (1 - x, 1 - y)]


def _rcopy(src, dst, send, recv, k, dev):
    return pltpu.make_async_remote_copy(src_ref=src, dst_ref=dst, send_sem=send.at[k], recv_sem=recv.at[k],
                                        device_id=dev, device_id_type=MESH)


def _gather_small(src, buf, send, recv):
    x, y, c = _pos()
    me = 4 * x + 2 * y + c
    peers = _all_peers(x, y, c)
    sends = [_rcopy(src, buf.at[me], send, recv, k, p) for k, p in enumerate(peers)]
    for cp in sends:
        cp.start()
    for k, (px, py, pc) in enumerate(peers):
        _rcopy(src, buf.at[4 * px + 2 * py + pc], send, recv, k, (px, py, pc)).wait_recv()
    for cp in sends:
        cp.wait_send()
    return me


def _ada_fwd(c_in, w_ada, b_cols):
    def body(c_ref, w_ref, b_ref, mod_ref, cond_ref, cbuf, mbuf, s1, r1, s2, r2):
        x, y, c = _pos()
        chip = 2 * x + y
        me = _gather_small(c_ref, cbuf, s1, r1)
        cbuf[me] = c_ref[...]
        for i in range(NDEV):
            cond_ref[BL * i:BL * (i + 1), :] = cbuf[i]
        call = cond_ref[...]
        cond = call / (1.0 + jnp.exp(-call))
        cond_ref[...] = cond
        mbuf[chip] = jnp.dot(cond, w_ref[...], preferred_element_type=f32, precision=lax.Precision.HIGHEST) + b_ref[...]
        chips = _other_chips(x, y)
        sends = [_rcopy(mbuf.at[chip], mbuf.at[chip], s2, r2, j, (px, py, c)) for j, (px, py) in enumerate(chips)]
        for cp in sends:
            cp.start()
        for j, (px, py) in enumerate(chips):
            _rcopy(mbuf.at[chip], mbuf.at[2 * px + py], s2, r2, j, (px, py, c)).wait_recv()
        for cp in sends:
            cp.wait_send()
        row = lax.broadcasted_iota(jnp.int32, (BL * NDEV, ADAW), 0)
        for s in range(NCHIP):
            slab = mbuf[s]
            for j in range(BL):
                mod_ref[j:j + 1, ADAW * s:ADAW * (s + 1)] = jnp.sum(jnp.where(row == BL * me + j, slab, 0.0), axis=0, keepdims=True)

    vm = pl.BlockSpec(memory_space=pltpu.VMEM)
    return pl.pallas_call(
        body, name="ada_fwd", in_specs=[vm, vm, vm], out_specs=[vm, vm],
        out_shape=[jax.ShapeDtypeStruct((BL, NMOD * D), f32), jax.ShapeDtypeStruct((BL * NDEV, D), f32)],
        scratch_shapes=[pltpu.VMEM((NDEV, BL, D), f32), pltpu.VMEM((NCHIP, BL * NDEV, ADAW), f32),
                        pltpu.SemaphoreType.DMA((NDEV - 1,)), pltpu.SemaphoreType.DMA((NDEV - 1,)),
                        pltpu.SemaphoreType.DMA((NCHIP - 1,)), pltpu.SemaphoreType.DMA((NCHIP - 1,))],
        compiler_params=pltpu.CompilerParams(vmem_limit_bytes=VMEM_LIMIT),
    )(c_in, w_ada, b_cols)


def _small_allreduce(accs, cond_all, after=()):
    na = len(after)

    def body(bi, bo, bm, bd, gi, go, gm, gd, dsink, cond_ref, *rest):
        gw_ref, gb_ref, small_ref, pay, pbuf, dall, s1, r1 = rest[na:]
        x, y, c = _pos()
        chip = 2 * x + y
        pay[...] = jnp.zeros_like(pay)
        for b in range(BL):
            for k, (ref, r) in enumerate(((bi, 1), (bi, 0), (bo, 0), (bm, 1), (bm, 0), (bd, 0))):
                pay[b:b + 1, D * k:D * (k + 1)] = ref[b, r:r + 1, :]
        for off, ref, r in ((OFF_G_ATTN_PRE, gi, 0), (OFF_G_ATTN_POST, go, 0), (OFF_G_MIX_A, go, 1), (OFF_G_MLP_PRE, gm, 0),
                            (OFF_G_MLP_POST, gd, 0)):
            pay[BL:BL + 1, off:off + D] = ref[r:r + 1, :]
        eye = lax.broadcasted_iota(jnp.int32, (8, LANES), 0) == lax.broadcasted_iota(jnp.int32, (8, LANES), 1)
        pay[BL:BL + 1, OFF_SINK:OFF_SINK + LANES] = jnp.sum(jnp.where(eye, dsink[...], 0.0), axis=0, keepdims=True)
        pay[BL:BL + 1, OFF_LOSS:OFF_LOSS + LANES] = gd[1:2, 0:LANES]
        me = _gather_small(pay, pbuf, s1, r1)
        pbuf[me] = pay[...]
        small = pbuf[0, BL:BL + 1, :]
        for i in range(1, NDEV):
            small = small + pbuf[i, BL:BL + 1, :]
        small_ref[...] = small
        for i in range(NDEV):
            dall[BL * i:BL * (i + 1), :] = pbuf[i, 0:BL, :]
        gb_ref[...] = jnp.sum(dall[...], axis=0, keepdims=True)
        cols = jnp.zeros((BL * NDEV, ADAW), f32)
        for s in range(NCHIP):
            cols = cols + jnp.where(chip == s, dall[:, ADAW * s:ADAW * (s + 1)], 0.0)
        gw_ref[...] = lax.dot_general(cond_ref[...], cols, (((0,), (0,)), ((), ())), preferred_element_type=f32,
                                      precision=lax.Precision.HIGHEST)

    vm = pl.BlockSpec(memory_space=pltpu.VMEM)
    return pl.pallas_call(
        body, name="small_allreduce", in_specs=[vm] * 10 + [pl.BlockSpec(memory_space=pl.ANY)] * na, out_specs=[vm] * 3,
        out_shape=[jax.ShapeDtypeStruct((D, ADAW), f32), jax.ShapeDtypeStruct((1, PAYW), f32), jax.ShapeDtypeStruct((1, PAYW), f32)],
        scratch_shapes=[pltpu.VMEM((4, PAYW), f32), pltpu.VMEM((NDEV, 4, PAYW), f32), pltpu.VMEM((BL * NDEV, PAYW), f32),
                        pltpu.SemaphoreType.DMA((NDEV - 1,)), pltpu.SemaphoreType.DMA((NDEV - 1,))],
        compiler_params=pltpu.CompilerParams(vmem_limit_bytes=VMEM_LIMIT),
    )(*accs, cond_all, *after)


def _half(ref, c):
    r2 = ref.shape[0] // 2
    return ref.at[pl.ds(pl.multiple_of(c * r2, 16), r2), :]


HBM_SPEC = pl.BlockSpec(memory_space=pltpu.HBM)
SEM_SPEC = pl.BlockSpec(memory_space=pltpu.SEMAPHORE)
EFFECT = pltpu.SideEffectType.DATAFLOW_SIDE_EFFECTING
NLINK = NCHIP - 1


def _in_hbm(a):
    return pltpu.with_memory_space_constraint(a, pltpu.HBM)


def _split_start(name, srcs, land_shapes, builds, after=()):
    n = len(srcs)

    na = len(after)

    def body(*refs):
        src, land, token = refs[:n], refs[n:2 * n], refs[-1]
        send, recv = refs[2 * n + na:3 * n + na], refs[3 * n + na:4 * n + na]
        for t in range(n):
            for out_cp, _ in builds[t](src[t], land[t], send[t], recv[t]):
                out_cp.start()
        token[...] = jnp.zeros_like(token)

    lands = [_in_hbm(lax.empty(s.shape, s.dtype)) for s in land_shapes]
    sems = [pltpu.SemaphoreType.DMA((NLINK,))] * (2 * n)
    thru = [pltpu.HBM(a.shape, a.dtype) for a in list(srcs) + lands]
    res = pl.pallas_call(
        body, name=name, out_shape=sems + thru + [jax.ShapeDtypeStruct((8, LANES), f32)],
        in_specs=[HBM_SPEC] * (2 * n) + [pl.BlockSpec(memory_space=pl.ANY)] * na,
        out_specs=[SEM_SPEC] * (2 * n) + [HBM_SPEC] * (2 * n) + [pl.BlockSpec(memory_space=pltpu.VMEM)],
        input_output_aliases={i: 2 * n + i for i in range(2 * n)},
        compiler_params=pltpu.CompilerParams(has_side_effects=EFFECT),
    )(*[_in_hbm(a) for a in srcs], *lands, *after)
    flight = [(res[2 * n + t], res[3 * n + t], res[t], res[n + t]) for t in range(n)]
    return flight, res[-1][0, 0]


def _split_wait(name, flight, builds, after):
    m = len(flight)
    na = len(after)

    def body(*refs):
        src, land, send, recv = refs[:m], refs[m:2 * m], refs[2 * m:3 * m], refs[3 * m:4 * m]
        for t in range(m):
            for out_cp, in_cp in builds[t](src[t], land[t], send[t], recv[t]):
                out_cp.wait_send()
                in_cp.wait_recv()

    ops = [f[0] for f in flight] + [f[1] for f in flight] + [f[2] for f in flight] + [f[3] for f in flight]
    res = pl.pallas_call(
        body, name=name, out_shape=[pltpu.HBM(a.shape, a.dtype) for a in ops[:2 * m]],
        in_specs=[HBM_SPEC] * (2 * m) + [SEM_SPEC] * (2 * m) + [pl.BlockSpec(memory_space=pl.ANY)] * na,
        out_specs=[HBM_SPEC] * (2 * m), input_output_aliases={i: i for i in range(2 * m)},
        compiler_params=pltpu.CompilerParams(has_side_effects=EFFECT),
    )(*ops, *after)
    return res[:m], res[m:2 * m]


def _weight_copies(src, land, send, recv):
    x, y, c = _pos()
    chip = 2 * x + y
    return [(_rcopy(_half(src, c), _half(land.at[chip], c), send, recv, j, (px, py, c)),
             _rcopy(_half(src, c), _half(land.at[2 * px + py], c), send, recv, j, (px, py, c)))
            for j, (px, py) in enumerate(_other_chips(x, y))]


def _grad_copies(src, land, send, recv):
    x, y, c = _pos()
    return [(_rcopy(src.at[2 * px + py], land.at[j], send, recv, j, (px, py, c)),
             _rcopy(src.at[2 * px + py], land.at[j], send, recv, j, (px, py, c)))
            for j, (px, py) in enumerate(_other_chips(x, y))]


def _pair_forward(shards, gathered, name):
    nt = len(shards)

    def body(*refs):
        sh, gin, gout = refs[:nt], refs[nt:2 * nt], refs[2 * nt:3 * nt]
        send, recv = refs[3 * nt:]
        x, y, c = _pos()
        chip = 2 * x + y
        sib = (x, y, 1 - c)
        chips = _other_chips(x, y)
        cps = []
        for t in range(nt):
            for j, (px, py) in enumerate(chips):
                cps.append(_rcopy(_half(gin[t].at[2 * px + py], c), _half(gout[t].at[2 * px + py], c), send, recv, 4 * t + j, sib))
            cps.append(_rcopy(sh[t], gout[t].at[chip], send, recv, 4 * t + 3, sib))
        for cp in cps:
            cp.start()
        for t in range(nt):
            for j, (px, py) in enumerate(chips):
                theirs = _half(gout[t].at[2 * px + py], 1 - c)
                _rcopy(theirs, theirs, send, recv, 4 * t + j, sib).wait_recv()
            _rcopy(sh[t], gout[t].at[chip], send, recv, 4 * t + 3, sib).wait_recv()
        for cp in cps:
            cp.wait_send()

    hbm = pl.BlockSpec(memory_space=pl.ANY)
    return pl.pallas_call(
        body, name=name, in_specs=[hbm] * (2 * nt), out_specs=[hbm] * nt,
        out_shape=[jax.ShapeDtypeStruct(g.shape, g.dtype) for g in gathered],
        input_output_aliases={nt + t: t for t in range(nt)},
        scratch_shapes=[pltpu.SemaphoreType.DMA((4 * nt,)), pltpu.SemaphoreType.DMA((4 * nt,))],
    )(*shards, *gathered)


def _rs_pair(grads, name):
    nt = len(grads)

    def body(*refs):
        ins, outs = refs[:nt], refs[nt:2 * nt]
        send, recv = refs[2 * nt:]
        x, y, c = _pos()
        sib = (x, y, 1 - c)
        cps = []
        for t in range(nt):
            r2 = ins[t].shape[1] // 2
            src = ins[t].at[:, pl.ds(pl.multiple_of((1 - c) * r2, 8), r2), :]
            cps.append(_rcopy(src, outs[t], send, recv, t, sib))
        for cp in cps:
            cp.start()
        for cp in cps:
            cp.wait()

    hbm = pl.BlockSpec(memory_space=pl.ANY)
    return pl.pallas_call(
        body, name=name, in_specs=[hbm] * nt, out_specs=[hbm] * nt,
        out_shape=[jax.ShapeDtypeStruct((NCHIP, g.shape[1] // 2, g.shape[2]), f32) for g in grads],
        scratch_shapes=[pltpu.SemaphoreType.DMA((nt,)), pltpu.SemaphoreType.DMA((nt,))],
    )(*grads)


RS_ROWS = 128


def _pair_add(g, landed, c_arr, name):
    _, r2, cw = landed.shape
    nr = r2 // RS_ROWS

    def body(c_ref, g_ref, p_ref, o_ref):
        o_ref[...] = (g_ref[...] + p_ref[...]).astype(bf16)

    gs = pltpu.PrefetchScalarGridSpec(
        num_scalar_prefetch=1, grid=(NCHIP, nr),
        in_specs=[pl.BlockSpec((None, RS_ROWS, cw), lambda s, j, c: (s, c[0] * nr + j, 0)),
                  pl.BlockSpec((None, RS_ROWS, cw), lambda s, j, c: (s, j, 0))],
        out_specs=pl.BlockSpec((None, RS_ROWS, cw), lambda s, j, c: (s, j, 0)))
    return pl.pallas_call(body, name=name, grid_spec=gs, out_shape=jax.ShapeDtypeStruct((NCHIP, r2, cw), bf16),
                          compiler_params=_cp(("arbitrary", "arbitrary")))(c_arr, g, landed)


def _chip_add(half, landed, pos_arr, name):
    _, r2, cw = half.shape
    nr = r2 // RS_ROWS

    def body(s_ref, h_ref, q_ref, o_ref):
        acc = h_ref[...].astype(f32)
        for j in range(NCHIP - 1):
            acc = acc + q_ref[j].astype(f32)
        o_ref[...] = acc

    gs = pltpu.PrefetchScalarGridSpec(
        num_scalar_prefetch=1, grid=(nr,),
        in_specs=[pl.BlockSpec((None, RS_ROWS, cw), lambda j, s: (s[0], j, 0)),
                  pl.BlockSpec((NCHIP - 1, RS_ROWS, cw), lambda j, s: (0, j, 0))],
        out_specs=pl.BlockSpec((RS_ROWS, cw), lambda j, s: (s[1] * nr + j, 0)))
    return pl.pallas_call(body, name=name, grid_spec=gs, out_shape=jax.ShapeDtypeStruct((2 * r2, cw), f32),
                          compiler_params=_cp(("arbitrary",)))(pos_arr, half, landed)


def _ag_pair(fulls, name):
    nt = len(fulls)

    def body(*refs):
        ins, outs = refs[:nt], refs[nt:2 * nt]
        send, recv = refs[2 * nt:]
        x, y, c = _pos()
        sib = (x, y, 1 - c)
        cps = [_rcopy(_half(ins[t], c), _half(outs[t], c), send, recv, t, sib) for t in range(nt)]
        for cp in cps:
            cp.start()
        for t in range(nt):
            _rcopy(_half(ins[t], c), _half(outs[t], 1 - c), send, recv, t, sib).wait_recv()
        for cp in cps:
            cp.wait_send()

    hbm = pl.BlockSpec(memory_space=pl.ANY)
    return pl.pallas_call(
        body, name=name, in_specs=[hbm] * nt, out_specs=[hbm] * nt,
        out_shape=[jax.ShapeDtypeStruct(a.shape, f32) for a in fulls],
        input_output_aliases={t: t for t in range(nt)},
        scratch_shapes=[pltpu.SemaphoreType.DMA((nt,)), pltpu.SemaphoreType.DMA((nt,))],
    )(*fulls)


def _adamw_math(w, g, m, v):
    m = B1 * m + (1.0 - B1) * g
    v = B2 * v + (1.0 - B2) * jnp.square(g)
    m_hat = m / (1.0 - B1 ** STEP)
    v_hat = v / (1.0 - B2 ** STEP)
    return -LR * (m_hat / (jnp.sqrt(v_hat) + AEPS) + WD * w), m, v


ADAM_ROWS = 128


def _adamw(w, g, m, v, name):
    r, cw = w.shape

    def body(w_ref, g_ref, m_ref, v_ref, d_ref, mo_ref, vo_ref):
        d_ref[...], mo_ref[...], vo_ref[...] = _adamw_math(w_ref[...], g_ref[...], m_ref[...], v_ref[...])

    spec = pl.BlockSpec((ADAM_ROWS, cw), lambda i: (i, 0))
    return pl.pallas_call(body, name=name, grid=(r // ADAM_ROWS,), in_specs=[spec] * 4, out_specs=[spec] * 3,
                          out_shape=[jax.ShapeDtypeStruct((r, cw), f32)] * 3, compiler_params=_cp(("arbitrary",)))(w, g, m, v)


SMALL = (("b_ada", None, PAYW), ("g_attn_pre", OFF_G_ATTN_PRE, D), ("g_attn_post", OFF_G_ATTN_POST, D), ("sink_a", OFF_SINK, 8),
         ("g_mix_a", OFF_G_MIX_A, AQ), ("g_mix_b", OFF_G_MIX_B, BW), ("g_mlp_pre", OFF_G_MLP_PRE, D), ("g_mlp_post", OFF_G_MLP_POST, D))


def _adamw_small(small, gb, params):
    n = len(SMALL)

    def body(*refs):
        small_ref, gb_ref = refs[:2]
        wmv = refs[2:2 + 3 * n]
        loss_ref = refs[2 + 3 * n]
        outs = refs[3 + 3 * n:]
        loss_ref[...] = small_ref[:, OFF_LOSS:OFF_LOSS + 1] * (0.5 / D)
        for i, (_, off, width) in enumerate(SMALL):
            g = gb_ref[...] if off is None else small_ref[:, off:off + width]
            w_ref, m_ref, v_ref = wmv[3 * i:3 * i + 3]
            outs[4 * i][...] = g
            outs[4 * i + 1][...], outs[4 * i + 2][...], outs[4 * i + 3][...] = _adamw_math(w_ref[...], g, m_ref[...], v_ref[...])

    vm = pl.BlockSpec(memory_space=pltpu.VMEM)
    out_shape = [jax.ShapeDtypeStruct((1, 1), f32)]
    for _, _, width in SMALL:
        out_shape += [jax.ShapeDtypeStruct((1, width), f32)] * 4
    flat = [a for wmv in params for a in wmv]
    res = pl.pallas_call(body, name="adamw_small", in_specs=[vm] * (2 + 3 * n), out_specs=[vm] * len(out_shape),
                         out_shape=out_shape)(small, gb, *flat)
    return res[0], {name: res[1 + 4 * i:5 + 4 * i] for i, (name, _, _) in enumerate(SMALL)}


def kernel(x, c, positions, w_ada, b_ada, g_attn_pre, g_attn_post, w_in, sink_a, g_mix_a, g_mix_b, w_out, g_mlp_pre, g_mlp_post, w_up, w_down, loss_target, m_w_ada, m_b_ada, m_g_attn_pre, m_g_attn_post, m_w_in, m_sink_a, m_g_mix_a, m_g_mix_b, m_w_out, m_g_mlp_pre, m_g_mlp_post, m_w_up, m_w_down, v_w_ada, v_b_ada, v_g_attn_pre, v_g_attn_post, v_w_in, v_sink_a, v_g_mix_a, v_g_mix_b, v_w_out, v_g_mlp_pre, v_g_mlp_post, v_w_up, v_w_down):
    given = dict(w_ada=w_ada, b_ada=b_ada, g_attn_pre=g_attn_pre, g_attn_post=g_attn_post, w_in=w_in, sink_a=sink_a, g_mix_a=g_mix_a,
                 g_mix_b=g_mix_b, w_out=w_out, g_mlp_pre=g_mlp_pre, g_mlp_post=g_mlp_post, w_up=w_up, w_down=w_down)
    moms = dict(w_ada=(m_w_ada, v_w_ada), b_ada=(m_b_ada, v_b_ada), g_attn_pre=(m_g_attn_pre, v_g_attn_pre),
                g_attn_post=(m_g_attn_post, v_g_attn_post), w_in=(m_w_in, v_w_in), sink_a=(m_sink_a, v_sink_a),
                g_mix_a=(m_g_mix_a, v_g_mix_a), g_mix_b=(m_g_mix_b, v_g_mix_b), w_out=(m_w_out, v_w_out),
                g_mlp_pre=(m_g_mlp_pre, v_g_mlp_pre), g_mlp_post=(m_g_mlp_post, v_g_mlp_post), w_up=(m_w_up, v_w_up),
                w_down=(m_w_down, v_w_down))
    order = ["w_ada", "b_ada", "g_attn_pre", "g_attn_post", "w_in", "sink_a", "g_mix_a", "g_mix_b", "w_out", "g_mlp_pre",
             "g_mlp_post", "w_up", "w_down"]
    xi, yi, ci = _pos()
    chip = 2 * xi + yi

    c_arr = jnp.reshape(ci, (1,)).astype(jnp.int32)
    pos_arr = jnp.stack([chip, ci]).astype(jnp.int32)
    big = ("w_in", "w_out", "w_up", "w_down")

    shards = [given[n][0].astype(bf16) for n in big]
    gathered = [jax.ShapeDtypeStruct((NCHIP,) + s.shape, bf16) for s in shards]
    flight_in, tok = _split_start("weights_start_first", shards[:1], gathered[:1], [_weight_copies])
    b_cols = lax.dynamic_slice(b_ada, (0, chip * ADAW), (1, ADAW))
    mod, cond_all = _ada_fwd(c + tok, w_ada[0], b_cols)
    flight_rest, tok = _split_start("weights_start_rest", shards[1:], gathered[1:], [_weight_copies] * 3, after=(mod,))
    mod = mod.reshape(BL, NMOD, D) + tok
    srcs, lands = _split_wait("weights_wait_first", flight_in, [_weight_copies], (mod,))
    (win_g,) = _pair_forward(srcs, lands, "weights_pair_first")
    w_in_full = win_g.transpose(1, 0, 2).reshape(D, INW)

    def later_weights(after):
        srcs, lands = _split_wait("weights_wait_rest", flight_rest, [_weight_copies] * 3, after)
        wout_g, wup_g, wdn_g = _pair_forward(srcs, lands, "weights_pair_rest")
        return wout_g.reshape(D, D), wup_g, wdn_g.reshape(DFF, D)

    pending = {}

    def grad_ready(group, g):
        if group == "w_down":
            names, slabs = ("w_down",), [g.reshape(NCHIP, DFF // NCHIP, D)]
        elif group == "w_up":
            names, slabs = ("w_up",), [g]
        else:
            names = ("w_in", "w_out")
            slabs = [g[0].reshape(D, NCHIP, INW // NCHIP).transpose(1, 0, 2), g[1].reshape(NCHIP, D // NCHIP, D)]
        landed = _rs_pair(slabs, "grad_pair_exchange_" + group)
        halves = [_pair_add(s, p, c_arr, "grad_pair_sum_" + n) for s, p, n in zip(slabs, landed, names)]
        fl, tk = _split_start("grad_start_" + group, halves,
                              [jax.ShapeDtypeStruct((NLINK,) + h.shape[1:], bf16) for h in halves], [_grad_copies] * len(names))
        pending[group] = (names, fl)
        return tk

    grad_x, accs = _local_step(x, positions, mod, loss_target, w_in_full, later_weights, grad_ready,
                               g_attn_pre, g_attn_post, sink_a, g_mix_a, g_mix_b, g_mlp_pre, g_mlp_post)

    def finish(groups, after):
        names = sum((pending[g][0] for g in groups), ())
        fl = sum((pending[g][1] for g in groups), [])
        halves, landed = _split_wait("grad_wait_" + groups[0], fl, [_grad_copies] * len(names), after)
        fulls = [_chip_add(h, q, pos_arr, "grad_chip_sum_" + n) for h, q, n in zip(halves, landed, names)]
        return dict(zip(names, _ag_pair(fulls, "grad_pair_gather_" + groups[0])))

    grads, out = {}, {}

    def update(n):
        d, m2, v2 = _adamw(given[n][0], grads[n], moms[n][0][0], moms[n][1][0], "adamw_" + n)
        out[n] = (grads[n][None], d[None], m2[None], v2[None])
        return v2

    grads.update(finish(("w_down", "w_up"), (accs[-1],)))
    last = [update(n) for n in ("w_down", "w_up")]
    grads["w_ada"], gb, small = _small_allreduce(accs, cond_all, after=tuple(last))
    update("w_ada")
    grads.update(finish(("w_in_w_out",), (small,)))
    update("w_in")
    update("w_out")
    loss, res = _adamw_small(small, gb, [(given[n], moms[n][0], moms[n][1]) for n, _, _ in SMALL])
    for n, _, _ in SMALL:
        out[n] = tuple(res[n])
    return (loss.reshape(()), grad_x, *[out[n][0] for n in order], *[out[n][1] for n in order],
            *[out[n][2] for n in order], *[out[n][3] for n in order])
```

```python
import functools

import numpy as np
import jax
import jax.numpy as jnp
from jax import lax
from jax.experimental import pallas as pl
from jax.experimental.pallas import tpu as pltpu

f32 = jnp.float32
bf16 = jnp.bfloat16
MESH = pl.DeviceIdType.MESH

D = 1024
SEQ = 2048
BL = 2
HD = 64
AQ = 512
AKV = 128
BW = 512
INW = 2304
DFF = 4096
NMOD = 6
ROT = 16
THETA = 500000.0
EPS = 1e-6
NEG = -1e30
BLK = 128
TM = 256
NJ = SEQ // TM
LANES = 128
NCHIP = 4
NDEV = 8
VMEM_LIMIT = 56 << 20

LR, B1, B2, AEPS, WD, STEP = 0.001, 0.9, 0.999, 1e-08, 0.01, 10

OFF_G_ATTN_PRE, OFF_G_ATTN_POST, OFF_G_MIX_A, OFF_G_MIX_B = 0, 1024, 2048, 2560
OFF_G_MLP_PRE, OFF_G_MLP_POST, OFF_SINK, OFF_LOSS = 3072, 4096, 5120, 5248
PAYW = NMOD * D


def _cp(sem=None):
    return pltpu.CompilerParams(dimension_semantics=sem, vmem_limit_bytes=VMEM_LIMIT)


def _dot(a, b):
    return jnp.dot(a, b, preferred_element_type=f32)


def _dot_nt(a, b):
    return lax.dot_general(a, b, (((1,), (1,)), ((), ())), preferred_element_type=f32)


def _dot_tn(a, b):
    return lax.dot_general(a, b, (((0,), (0,)), ((), ())), preferred_element_type=f32)


def _rms(x):
    r = lax.rsqrt(jnp.mean(x * x, axis=-1, keepdims=True) + EPS)
    return x * r, r


def _rms_bwd(dy, y, r):
    return r * (dy - y * jnp.mean(dy * y, axis=-1, keepdims=True))


def _colsum(v):
    return jnp.sum(v, axis=0, keepdims=True)


def _rope(p, c, s1, s2):
    outs = []
    for c0 in range(0, p.shape[1], LANES):
        pc = p[:, c0:c0 + LANES]
        outs.append(pc * c + pltpu.roll(pc, LANES - ROT // 2, 1) * s1 + pltpu.roll(pc, ROT // 2, 1) * s2)
    return outs[0] if len(outs) == 1 else jnp.concatenate(outs, axis=1)


def _rope_t(g, c, s1, s2):
    outs = []
    for c0 in range(0, g.shape[1], LANES):
        gc = g[:, c0:c0 + LANES]
        outs.append(gc * c + pltpu.roll(gc * s1, ROT // 2, 1) + pltpu.roll(gc * s2, LANES - ROT // 2, 1))
    return outs[0] if len(outs) == 1 else jnp.concatenate(outs, axis=1)


def _perm_store(val, scr, out_ref, d):
    nc = val.shape[1] // LANES
    for c in range(nc):
        scr[c] = val[:, LANES * c:LANES * (c + 1)]
    for c in range(nc):
        for r in range(d):
            out_ref[r, :, LANES * c:LANES * (c + 1)] = scr[c, pl.ds(r, TM // d, stride=d), :].astype(out_ref.dtype)


def _perm_load(in_ref, scr, d):
    nc = in_ref.shape[-1] // LANES
    for c in range(nc):
        for r in range(d):
            scr[c, pl.ds(r, TM // d, stride=d), :] = in_ref[r, :, LANES * c:LANES * (c + 1)].astype(f32)
    return jnp.concatenate([scr[c] for c in range(nc)], axis=1)


def _tok(w, dtype=None):
    return pl.BlockSpec((None, TM, w), lambda b, j: (b, j, 0))


def _perm_spec(d, w):
    return pl.BlockSpec((None, d, TM // d, w), lambda b, j: (b, 0, j, 0))


def _full(shape):
    n = len(shape)
    return pl.BlockSpec(shape, lambda b, j: (0,) * n)


MOD_SPEC = pl.BlockSpec((None, NMOD, D), lambda b, j: (b, 0, 0))
ACCB_SPEC = pl.BlockSpec((None, 8, D), lambda b, j: (b, 0, 0))
ACCG_SPEC = pl.BlockSpec((8, D), lambda b, j: (0, 0))
ACC_SHAPES = [jax.ShapeDtypeStruct((BL, 8, D), f32), jax.ShapeDtypeStruct((8, D), f32)]


def _acc_init(accb_ref, accg_ref):
    b, j = pl.program_id(0), pl.program_id(1)

    @pl.when(j == 0)
    def _():
        accb_ref[...] = jnp.zeros_like(accb_ref)

    @pl.when((b == 0) & (j == 0))
    def _():
        accg_ref[...] = jnp.zeros_like(accg_ref)


def _rope_tables(pos_col, inv_lane):
    def body(p_ref, inv_ref, c_ref, s1_ref, s2_ref):
        ang = p_ref[...].astype(f32) * inv_ref[...]
        j = lax.broadcasted_iota(jnp.int32, (TM, LANES), 1) % HD
        cs, sn = jnp.cos(ang), jnp.sin(ang)
        c_ref[...] = jnp.where(j < ROT, cs, 1.0)
        s1_ref[...] = jnp.where(j < ROT // 2, -sn, 0.0)
        s2_ref[...] = jnp.where((j >= ROT // 2) & (j < ROT), sn, 0.0)

    n = BL * SEQ // TM
    return pl.pallas_call(
        body, name="rope_tables", grid=(n,),
        in_specs=[pl.BlockSpec((TM, 1), lambda i: (i, 0)), pl.BlockSpec((1, LANES), lambda i: (0, 0))],
        out_specs=[pl.BlockSpec((TM, LANES), lambda i: (i, 0))] * 3,
        out_shape=[jax.ShapeDtypeStruct((BL * SEQ, LANES), f32)] * 3,
    )(pos_col, inv_lane)


def _attn_in(x, mod, g_pre, w_in, tc, ts1, ts2):
    def body(x_ref, mod_ref, g_ref, w_ref, c_ref, s1_ref, s2_ref,
             h_ref, qa_ref, ka_ref, va_ref, q1_ref, k1_ref, v1_ref, q4_ref, k4_ref, v4_ref, q16_ref, k16_ref, v16_ref,
             scr):
        xn, _ = _rms(x_ref[...])
        h = (xn * g_ref[...]) * (1.0 + mod_ref[1:2, :]) + mod_ref[0:1, :]
        hb = h.astype(bf16)
        h_ref[...] = hb
        proj = _dot(hb, w_ref[...])
        c, s1, s2 = c_ref[...], s1_ref[...], s2_ref[...]
        o1, o2, o3, o4, o5 = AQ, AQ + AKV, AQ + 2 * AKV, AQ + 2 * AKV + BW, AQ + 2 * AKV + 2 * BW
        qa_ref[...] = (_rope(proj[:, :o1], c, s1, s2) * 0.125).astype(bf16)
        ka_ref[...] = _rope(proj[:, o1:o2], c, s1, s2).astype(bf16)
        va_ref[...] = proj[:, o2:o3].astype(bf16)
        qb = _rope(proj[:, o3:o4], c, s1, s2) * 0.125
        kb = _rope(proj[:, o4:o5], c, s1, s2)
        vb = proj[:, o5:]
        for val, r1, r4, r16 in ((qb, q1_ref, q4_ref, q16_ref), (kb, k1_ref, k4_ref, k16_ref), (vb, v1_ref, v4_ref, v16_ref)):
            r1[...] = val.astype(bf16)
            _perm_store(val, scr, r4, 4)
            _perm_store(val, scr, r16, 16)

    nat = lambda w: jax.ShapeDtypeStruct((BL, SEQ, w), bf16)
    p4 = jax.ShapeDtypeStruct((BL, 4, SEQ // 4, BW), bf16)
    p16 = jax.ShapeDtypeStruct((BL, 16, SEQ // 16, BW), bf16)
    return pl.pallas_call(
        body, name="attn_in", grid=(BL, NJ),
        in_specs=[_tok(D), MOD_SPEC, _full((1, D)), _full((D, INW)), _tok(LANES), _tok(LANES), _tok(LANES)],
        out_specs=[_tok(D), _tok(AQ), _tok(AKV), _tok(AKV)] + [_tok(BW)] * 3 + [_perm_spec(4, BW)] * 3 + [_perm_spec(16, BW)] * 3,
        out_shape=[nat(D), nat(AQ), nat(AKV), nat(AKV)] + [nat(BW)] * 3 + [p4] * 3 + [p16] * 3,
        scratch_shapes=[pltpu.VMEM((BW // LANES, TM, LANES), f32)],
        compiler_params=_cp(("arbitrary", "arbitrary")),
    )(x, mod, g_pre, w_in, tc, ts1, ts2)


def _kv_cat(cur_ref, prev_ref, p, gqa, cache):
    def one(ref):
        if not gqa:
            return ref[:, LANES * p:LANES * (p + 1)]
        k = ref[...]
        kr = pltpu.roll(k, HD, 1)
        lo = lax.broadcasted_iota(jnp.int32, k.shape, 1) < HD
        return jnp.where(lo, k, kr) if p < 2 else jnp.where(lo, kr, k)

    key = (id(cur_ref), p // 2 if gqa else p)
    if key not in cache:
        cache[key] = one(cur_ref) if prev_ref is None else jnp.concatenate([one(prev_ref), one(cur_ref)], axis=0)
    return cache[key]


def _lane_half(a, hh):
    lo = lax.broadcasted_iota(jnp.int32, a.shape, 1) < HD
    return jnp.where(lo, a, jnp.zeros_like(a)) if hh == 0 else jnp.where(lo, jnp.zeros_like(a), a)


def _attn_fwd(q, k, v, sink, *, max_dist, o_dtype, name):
    n, l, w = q.shape
    wk = k.shape[-1]
    nb = l // BLK
    gqa = wk != w
    has_sink = sink is not None

    def body(*refs):
        if has_sink:
            sink_ref, refs = refs[0], refs[1:]
        if nb > 1:
            q_ref, kc_ref, kp_ref, vc_ref, vp_ref, o_ref, lse_ref, sscr, pscr, dscr = refs
        else:
            q_ref, kc_ref, vc_ref, o_ref, lse_ref, sscr, pscr, dscr = refs
        i = pl.program_id(1)
        qi = lax.broadcasted_iota(jnp.int32, (BLK, BLK), 0)
        kj = lax.broadcasted_iota(jnp.int32, (BLK, BLK), 1)
        tri = kj <= qi
        eye = kj == qi
        cache = {}
        for p in range(w // LANES):
            qpair = q_ref[:, LANES * p:LANES * (p + 1)]
            kcat = _kv_cat(kc_ref, kp_ref if nb > 1 else None, p, gqa, cache)
            for hh in range(2):
                s = _dot_nt(_lane_half(qpair, hh), kcat)
                if nb > 1:
                    sp = jnp.where(i > 0, s[:, :BLK], NEG)
                    sscr[2 * p + hh] = jnp.where(tri, s[:, BLK:], sp)
                    if diag:
                        dscr[2 * p + hh] = jnp.where(eye, sp, NEG)
                else:
                    sscr[2 * p + hh] = jnp.where(tri, s, NEG)
        lane = lax.broadcasted_iota(jnp.int32, (BLK, LANES), 1)
        lse_all = jnp.zeros((BLK, LANES), f32)
        for p in range(w // LANES):
            for hh in range(2):
                h = 2 * p + hh
                comb = sscr[h]
                if diag:
                    dtile = dscr[h]
                    m = jnp.max(jnp.maximum(comb, dtile), axis=-1, keepdims=True)
                else:
                    m = jnp.max(comb, axis=-1, keepdims=True)
                if has_sink:
                    sk = sink_ref[0, h]
                    m = jnp.maximum(m, sk)
                e = jnp.exp(comb - m)
                if diag:
                    ed = jnp.exp(dtile - m)
                    den = jnp.sum(e + ed, axis=-1, keepdims=True)
                else:
                    den = jnp.sum(e, axis=-1, keepdims=True)
                if has_sink:
                    den = den + jnp.exp(sk - m)
                inv = 1.0 / den
                if nb > 1:
                    pscr[h, :, :BLK] = (jnp.where(tri, ed if diag else 0.0, e) * inv).astype(bf16)
                    pscr[h, :, BLK:] = (jnp.where(tri, e, 0.0) * inv).astype(bf16)
                else:
                    pscr[h] = (e * inv).astype(bf16)
                lse_all = jnp.where(lane == h, jnp.broadcast_to(m + jnp.log(den), (BLK, LANES)), lse_all)
        lse_ref[...] = lse_all
        for p in range(w // LANES):
            vcat = _kv_cat(vc_ref, vp_ref if nb > 1 else None, p, gqa, cache)
            key = ("halves", id(vc_ref), p // 2 if gqa else p)
            if key not in cache:
                cache[key] = (_lane_half(vcat, 0), _lane_half(vcat, 1))
            o_ref[:, LANES * p:LANES * (p + 1)] = (_dot(pscr[2 * p], cache[key][0])
                                                   + _dot(pscr[2 * p + 1], cache[key][1])).astype(o_ref.dtype)

    assert max_dist in (BLK - 1, BLK)
    diag = nb > 1 and max_dist == BLK
    cur = lambda ww: pl.BlockSpec((None, BLK, ww), lambda a, i: (a, i, 0))
    prev = lambda ww: pl.BlockSpec((None, BLK, ww), lambda a, i: (a, jnp.maximum(i - 1, 0), 0))
    in_specs = [cur(w), cur(wk)] + ([prev(wk)] if nb > 1 else []) + [cur(wk)] + ([prev(wk)] if nb > 1 else [])
    args = [q, k] + ([k] if nb > 1 else []) + [v] + ([v] if nb > 1 else [])
    if has_sink:
        in_specs = [pl.BlockSpec(memory_space=pltpu.SMEM)] + in_specs
        args = [sink] + args
    return pl.pallas_call(
        body, name=name, grid=(n, nb), in_specs=in_specs,
        out_specs=[cur(w), cur(LANES)],
        out_shape=[jax.ShapeDtypeStruct((n, l, w), o_dtype), jax.ShapeDtypeStruct((n, l, LANES), f32)],
        scratch_shapes=[pltpu.VMEM((w // HD, BLK, BLK), f32), pltpu.VMEM((w // HD, BLK, 2 * BLK if nb > 1 else BLK), bf16),
                        pltpu.VMEM((w // HD if diag else 1, BLK, BLK), f32)],
        compiler_params=_cp(("arbitrary", "arbitrary")),
    )(*args)


def _attn_bwd(q, k, v, do, delta, lse, sink, *, max_dist, name):
    n, l, w = q.shape
    wk = k.shape[-1]
    nb = l // BLK
    gqa = wk != w
    has_sink = sink is not None

    def body(*refs):
        if has_sink:
            sink_ref, refs = refs[0], refs[1:]
        if nb > 1:
            q_ref, kc_ref, kp_ref, vc_ref, vp_ref, do_ref, delta_ref, lse_ref = refs[:8]
            rest = refs[8:]
        else:
            q_ref, kc_ref, vc_ref, do_ref, delta_ref, lse_ref = refs[:6]
            rest = refs[6:]
        if has_sink:
            dq_ref, dk_ref, dv_ref, dsink_ref = rest[:4]
            rest = rest[4:]
        else:
            dq_ref, dk_ref, dv_ref = rest[:3]
            rest = rest[3:]
        step = pl.program_id(1)
        blk_idx = nb - 1 - step
        if nb > 1:
            ck, cv = rest[:2]
            rest = rest[2:]

            @pl.when(step == 0)
            def _():
                ck[...] = jnp.zeros_like(ck)
                cv[...] = jnp.zeros_like(cv)

        sscr, dpscr, pscr, dsscr = rest[:4]
        if diag:
            dscr, ddscr = rest[4:]
        if has_sink:
            @pl.when((pl.program_id(0) == 0) & (step == 0))
            def _():
                dsink_ref[...] = jnp.zeros_like(dsink_ref)

        lane = lax.broadcasted_iota(jnp.int32, (BLK, LANES), 1)
        lo = lane < HD
        qi = lax.broadcasted_iota(jnp.int32, (BLK, BLK), 0)
        kj = lax.broadcasted_iota(jnp.int32, (BLK, BLK), 1)
        tri = kj <= qi
        eye = kj == qi
        cache = {}
        kp, vp = (kp_ref, vp_ref) if nb > 1 else (None, None)
        rows = 2 * BLK if nb > 1 else BLK
        for p in range(w // LANES):
            sl = slice(LANES * p, LANES * (p + 1))
            qpair, dopair = q_ref[:, sl], do_ref[:, sl]
            kcat, vcat = _kv_cat(kc_ref, kp, p, gqa, cache), _kv_cat(vc_ref, vp, p, gqa, cache)
            for hh in range(2):
                h = 2 * p + hh
                s = _dot_nt(_lane_half(qpair, hh), kcat)
                dp = _dot_nt(_lane_half(dopair, hh), vcat)
                if nb > 1:
                    sp = jnp.where(blk_idx > 0, s[:, :BLK], NEG)
                    sscr[h] = jnp.where(tri, s[:, BLK:], sp)
                    dpscr[h] = jnp.where(tri, dp[:, BLK:], dp[:, :BLK])
                    if diag:
                        dscr[h] = jnp.where(eye, sp, NEG)
                        ddscr[h] = dp[:, :BLK]
                else:
                    sscr[h] = jnp.where(tri, s, NEG)
                    dpscr[h] = dp
        for p in range(w // LANES):
            for hh in range(2):
                h = 2 * p + hh
                lse_b = jnp.broadcast_to(lse_ref[:, h:h + 1], (BLK, BLK))
                delta = jnp.broadcast_to(delta_ref[:, h:h + 1], (BLK, BLK))
                pr = jnp.exp(sscr[h] - lse_b)
                ds = pr * (dpscr[h] - delta)
                if nb > 1:
                    if diag:
                        prd = jnp.exp(dscr[h] - lse_b)
                        dsd = prd * (ddscr[h] - delta)
                    else:
                        prd = dsd = 0.0
                    pscr[h, :, :BLK] = jnp.where(tri, prd, pr).astype(bf16)
                    pscr[h, :, BLK:] = jnp.where(tri, pr, 0.0).astype(bf16)
                    dsscr[h, :, :BLK] = jnp.where(tri, dsd, ds).astype(bf16)
                    dsscr[h, :, BLK:] = jnp.where(tri, ds, 0.0).astype(bf16)
                else:
                    pscr[h] = pr.astype(bf16)
                    dsscr[h] = ds.astype(bf16)
                if has_sink:
                    dsk = -jnp.sum(jnp.where(lane == 0, jnp.exp(sink_ref[0, h] - lse_b) * delta, 0.0), keepdims=True)
                    dsink_ref[h:h + 1, :] += jnp.broadcast_to(dsk, (1, LANES))
        gk = [jnp.zeros((rows, LANES), f32), jnp.zeros((rows, LANES), f32)]
        gv = [jnp.zeros((rows, LANES), f32), jnp.zeros((rows, LANES), f32)]
        for p in range(w // LANES):
            sl = slice(LANES * p, LANES * (p + 1))
            qpair, dopair = q_ref[:, sl], do_ref[:, sl]
            kcat = _kv_cat(kc_ref, kp, p, gqa, cache)
            key = ("halves", p // 2 if gqa else p)
            if key not in cache:
                cache[key] = (_lane_half(kcat, 0), _lane_half(kcat, 1))
            dq_ref[:, sl] = _dot(dsscr[2 * p], cache[key][0]) + _dot(dsscr[2 * p + 1], cache[key][1])
            dk_pair = _dot_tn(dsscr[2 * p], _lane_half(qpair, 0)) + _dot_tn(dsscr[2 * p + 1], _lane_half(qpair, 1))
            dv_pair = _dot_tn(pscr[2 * p], _lane_half(dopair, 0)) + _dot_tn(pscr[2 * p + 1], _lane_half(dopair, 1))
            if gqa:
                gk[p // 2] = gk[p // 2] + dk_pair
                gv[p // 2] = gv[p // 2] + dv_pair
            elif nb > 1:
                dk_ref[:, sl] = dk_pair[BLK:] + ck[:, sl]
                dv_ref[:, sl] = dv_pair[BLK:] + cv[:, sl]
                ck[:, sl] = dk_pair[:BLK]
                cv[:, sl] = dv_pair[:BLK]
            else:
                dk_ref[:, sl] = dk_pair
                dv_ref[:, sl] = dv_pair
        if gqa:
            lor = lax.broadcasted_iota(jnp.int32, (rows, LANES), 1) < HD
            fold = lambda g: jnp.where(lor, g[0] + pltpu.roll(g[0], HD, 1), g[1] + pltpu.roll(g[1], HD, 1))
            dk_full, dv_full = fold(gk), fold(gv)
            dk_ref[...] = dk_full[BLK:] + ck[...]
            dv_ref[...] = dv_full[BLK:] + cv[...]
            ck[...] = dk_full[:BLK]
            cv[...] = dv_full[:BLK]

    assert max_dist in (BLK - 1, BLK)
    diag = nb > 1 and max_dist == BLK
    cur = lambda ww: pl.BlockSpec((None, BLK, ww), lambda a, i: (a, nb - 1 - i, 0))
    prev = lambda ww: pl.BlockSpec((None, BLK, ww), lambda a, i: (a, jnp.maximum(nb - 2 - i, 0), 0))
    in_specs = ([cur(w), cur(wk)] + ([prev(wk)] if nb > 1 else []) + [cur(wk)] + ([prev(wk)] if nb > 1 else [])
                + [cur(w), cur(LANES), cur(LANES)])
    args = [q, k] + ([k] if nb > 1 else []) + [v] + ([v] if nb > 1 else []) + [do, delta, lse]
    out_specs = [cur(w), cur(wk), cur(wk)]
    out_shape = [jax.ShapeDtypeStruct((n, l, w), f32), jax.ShapeDtypeStruct((n, l, wk), f32), jax.ShapeDtypeStruct((n, l, wk), f32)]
    if has_sink:
        in_specs = [pl.BlockSpec(memory_space=pltpu.SMEM)] + in_specs
        args = [sink] + args
        out_specs.append(pl.BlockSpec((8, LANES), lambda a, i: (0, 0)))
        out_shape.append(jax.ShapeDtypeStruct((8, LANES), f32))
    nh = w // HD
    scratch = [pltpu.VMEM((BLK, wk), f32), pltpu.VMEM((BLK, wk), f32)] if nb > 1 else []
    scratch += [pltpu.VMEM((nh, BLK, BLK), f32)] * 2 + [pltpu.VMEM((nh, BLK, 2 * BLK if nb > 1 else BLK), bf16)] * 2
    if diag:
        scratch += [pltpu.VMEM((nh, BLK, BLK), f32)] * 2
    return pl.pallas_call(
        body, name=name, grid=(n, nb), in_specs=in_specs, out_specs=out_specs, out_shape=out_shape,
        scratch_shapes=scratch, compiler_params=_cp(("arbitrary", "arbitrary")),
    )(*args)


def _split3(x):
    hi = x.astype(bf16)
    r = x - hi.astype(f32)
    mid = r.astype(bf16)
    return hi, mid, (r - mid.astype(f32)).astype(bf16)


def _heads_to_lanes(xc, e):
    return sum(_dot(t, e) for t in _split3(xc))


def _lanes_to_heads(x, g):
    return sum(_dot(t, g) for t in _split3(x))


HEAD_EXPAND = (np.arange(LANES)[:, None] == np.arange(BW)[None, :] // HD).astype(np.float32)
HEAD_SUM = HEAD_EXPAND.T.copy()


def _branch_weights(l1_ref, l4_ref, l16_ref, scr):
    l4v = _perm_load(l4_ref, scr, 4)
    l16v = _perm_load(l16_ref, scr, 16)
    l1v = l1_ref[...]
    m = jnp.maximum(jnp.maximum(l1v, l4v), l16v)
    e1, e4, e16 = jnp.exp(l1v - m), jnp.exp(l4v - m), jnp.exp(l16v - m)
    z = e1 + e4 + e16
    return e1 / z, e4 / z, e16 / z


def _mix_out(oa, o1, l1, o4, l4, o16, l16, g_mix_a, g_mix_b, w_out, x, mod, g_post):
    def body(oa_ref, o1_ref, l1_ref, o4_ref, l4_ref, o16_ref, l16_ref, ga_ref, gb_ref, w_ref, x_ref, mod_ref, gp_ref, e_ref,
             x1_ref, y_ref, mixed_ref, ob_ref, scr):
        w1, w4, w16 = _branch_weights(l1_ref, l4_ref, l16_ref, scr)
        e = e_ref[...]
        ob = (_heads_to_lanes(w1, e) * o1_ref[...].astype(f32) + _heads_to_lanes(w4, e) * _perm_load(o4_ref, scr, 4)
              + _heads_to_lanes(w16, e) * _perm_load(o16_ref, scr, 16))
        ob_ref[...] = ob
        oan, _ = _rms(oa_ref[...])
        obn, _ = _rms(ob)
        mixed = jnp.concatenate([oan * ga_ref[...], obn * gb_ref[...]], axis=1).astype(bf16)
        mixed_ref[...] = mixed
        y = _dot(mixed, w_ref[...])
        y_ref[...] = y
        yn, _ = _rms(y)
        x1_ref[...] = x_ref[...] + mod_ref[2:3, :] * (yn * gp_ref[...])

    nat = lambda w, dt: jax.ShapeDtypeStruct((BL, SEQ, w), dt)
    return pl.pallas_call(
        body, name="mix_out", grid=(BL, NJ),
        in_specs=[_tok(AQ), _tok(BW), _tok(LANES), _perm_spec(4, BW), _perm_spec(4, LANES), _perm_spec(16, BW),
                  _perm_spec(16, LANES), _full((1, AQ)), _full((1, BW)), _full((D, D)), _tok(D), MOD_SPEC, _full((1, D)),
                  _full((LANES, BW))],
        out_specs=[_tok(D), _tok(D), _tok(D), _tok(BW)],
        out_shape=[nat(D, f32), nat(D, f32), nat(D, bf16), nat(BW, f32)],
        scratch_shapes=[pltpu.VMEM((BW // LANES, TM, LANES), f32)],
        compiler_params=_cp(("arbitrary", "arbitrary")),
    )(oa, o1, l1, o4, l4, o16, l16, g_mix_a, g_mix_b, w_out, x, mod, g_post, jnp.asarray(HEAD_EXPAND, bf16))


def _mlp_up(x1, mod, g_pre, w_up):
    def body(x_ref, mod_ref, g_ref, w_ref, h_ref, u_ref, a_ref):
        xn, _ = _rms(x_ref[...])
        h = (xn * g_ref[...]) * (1.0 + mod_ref[4:5, :]) + mod_ref[3:4, :]
        hb = h.astype(bf16)
        h_ref[...] = hb
        for s in range(NCHIP):
            u = _dot(hb, w_ref[s])
            u_ref[:, D * s:D * (s + 1)] = u.astype(bf16)
            a_ref[:, D * s:D * (s + 1)] = jnp.square(jnp.maximum(u, 0.0)).astype(bf16)

    nat = lambda w: jax.ShapeDtypeStruct((BL, SEQ, w), bf16)
    return pl.pallas_call(
        body, name="mlp_up", grid=(BL, NJ),
        in_specs=[_tok(D), MOD_SPEC, _full((1, D)), _full((NCHIP, D, D))],
        out_specs=[_tok(D), _tok(DFF), _tok(DFF)], out_shape=[nat(D), nat(DFF), nat(DFF)],
        compiler_params=_cp(("arbitrary", "arbitrary")),
    )(x1, mod, g_pre, w_up)


def _mlp_down(a, w_down, x1, target, mod, g_post):
    def body(a_ref, w_ref, x_ref, t_ref, mod_ref, g_ref, gx_ref, dy_ref, accb_ref, accg_ref):
        _acc_init(accb_ref, accg_ref)
        y2 = _dot(a_ref[...], w_ref[...])
        yn, r = _rms(y2)
        g = g_ref[...]
        gt = mod_ref[5:6, :]
        n2 = yn * g
        err = x_ref[...] + gt * n2 - t_ref[...]
        gout = err * (1.0 / D)
        gx_ref[...] = gout
        dn2 = gout * gt
        dy_ref[...] = _rms_bwd(dn2 * g, yn, r).astype(bf16)
        accb_ref[0:1, :] += _colsum(gout * n2)
        accg_ref[0:1, :] += _colsum(dn2 * yn)
        accg_ref[1:2, :] += jnp.broadcast_to(jnp.sum(err * err, keepdims=True), (1, D))

    return pl.pallas_call(
        body, name="mlp_down", grid=(BL, NJ),
        in_specs=[_tok(DFF), _full((DFF, D)), _tok(D), _tok(D), MOD_SPEC, _full((1, D))],
        out_specs=[_tok(D), _tok(D), ACCB_SPEC, ACCG_SPEC],
        out_shape=[jax.ShapeDtypeStruct((BL, SEQ, D), f32), jax.ShapeDtypeStruct((BL, SEQ, D), bf16)] + ACC_SHAPES,
        compiler_params=_cp(("arbitrary", "arbitrary")),
    )(a, w_down, x1, target, mod, g_post)


def _mlp_bwd(dy2, u, w_down, w_up, x1, gx, mod, g_pre):
    def body(dy_ref, u_ref, wd_hbm, wu_hbm, x_ref, gx_ref, mod_ref, g_ref, du_ref, gx1_ref, accb_ref, accg_ref, wd, wu, sem):
        _acc_init(accb_ref, accg_ref)

        @pl.when((pl.program_id(0) == 0) & (pl.program_id(1) == 0))
        def _():
            c1 = pltpu.make_async_copy(wd_hbm, wd, sem.at[0])
            c2 = pltpu.make_async_copy(wu_hbm, wu, sem.at[1])
            c1.start()
            c2.start()
            c1.wait()
            c2.wait()

        dy = dy_ref[...]
        dh = jnp.zeros((TM, D), f32)
        for s in range(NCHIP):
            sl = slice(D * s, D * (s + 1))
            da = _dot_nt(dy, wd[sl, :])
            du = (da * (2.0 * jnp.maximum(u_ref[:, sl].astype(f32), 0.0))).astype(bf16)
            du_ref[:, sl] = du
            dh = dh + _dot_nt(du, wu[s])
        xn, r = _rms(x_ref[...])
        g = g_ref[...]
        n = xn * g
        dn = dh * (1.0 + mod_ref[4:5, :])
        gx1_ref[...] = gx_ref[...] + _rms_bwd(dn * g, xn, r)
        accb_ref[0:1, :] += _colsum(dh * n)
        accb_ref[1:2, :] += _colsum(dh)
        accg_ref[0:1, :] += _colsum(dn * xn)

    anyspec = pl.BlockSpec(memory_space=pl.ANY)
    return pl.pallas_call(
        body, name="mlp_bwd", grid=(BL, NJ),
        in_specs=[_tok(D), _tok(DFF), anyspec, anyspec, _tok(D), _tok(D), MOD_SPEC, _full((1, D))],
        out_specs=[_tok(DFF), _tok(D), ACCB_SPEC, ACCG_SPEC],
        out_shape=[jax.ShapeDtypeStruct((BL, SEQ, DFF), bf16), jax.ShapeDtypeStruct((BL, SEQ, D), f32)] + ACC_SHAPES,
        scratch_shapes=[pltpu.VMEM((DFF, D), bf16), pltpu.VMEM((NCHIP, D, D), bf16), pltpu.SemaphoreType.DMA((2,))],
        compiler_params=_cp(("arbitrary", "arbitrary")),
    )(dy2, u, w_down, w_up, x1, gx, mod, g_pre)


def _matmul_tn(a, b, *, tn, col_blocked, name):
    t, m = a.shape
    n = b.shape[1]
    tmm = min(m, 1024)
    tk = 2048 if tn <= 1024 else 1024
    nk = t // tk

    def body(a_ref, b_ref, o_ref):
        @pl.when(pl.program_id(2) == 0)
        def _():
            o_ref[...] = jnp.zeros_like(o_ref)

        o_ref[...] += _dot_tn(a_ref[...], b_ref[...])

    if col_blocked:
        out_spec = pl.BlockSpec((None, tmm, tn), lambda i, j, k: (j, i, 0))
        out_shape = jax.ShapeDtypeStruct((n // tn, m, tn), f32)
    else:
        out_spec = pl.BlockSpec((tmm, tn), lambda i, j, k: (i, j))
        out_shape = jax.ShapeDtypeStruct((m, n), f32)
    return pl.pallas_call(
        body, name=name, grid=(m // tmm, n // tn, nk),
        in_specs=[pl.BlockSpec((tk, tmm), lambda i, j, k: (k, i)), pl.BlockSpec((tk, tn), lambda i, j, k: (k, j))],
        out_specs=out_spec, out_shape=out_shape,
        compiler_params=_cp(("arbitrary", "arbitrary", "arbitrary")),
    )(a, b)


def _attn_out_bwd(gx1, y, mod, g_post, w_out, oa, ob, g_mix_a, g_mix_b, l1, l4, l16):
    def body(gx_ref, y_ref, mod_ref, gp_ref, w_ref, oa_ref, ob_ref, ga_ref, gb_ref, l1_ref, l4_ref, l16_ref, e_ref, g_ref,
             dy_ref, doa_ref, do1_ref, do4_ref, do16_ref, da_ref, d1_ref, d4_ref, d16_ref, accb_ref, accg_ref, scr):
        _acc_init(accb_ref, accg_ref)
        w1, w4, w16 = _branch_weights(l1_ref, l4_ref, l16_ref, scr)
        e, hs = e_ref[...], g_ref[...]
        gx1v = gx_ref[...]
        yn, ry = _rms(y_ref[...])
        gp = gp_ref[...]
        gt = mod_ref[2:3, :]
        dn1 = gx1v * gt
        dy = _rms_bwd(dn1 * gp, yn, ry).astype(bf16)
        dy_ref[...] = dy
        dmixed = _dot_nt(dy, w_ref[...])
        dma, dmb = dmixed[:, :AQ], dmixed[:, AQ:]
        oa, ob = oa_ref[...], ob_ref[...]
        oan, ra = _rms(oa)
        obn, rb = _rms(ob)
        doa = _rms_bwd(dma * ga_ref[...], oan, ra)
        doa_ref[...] = doa.astype(bf16)
        da_ref[...] = _lanes_to_heads(doa * oa, hs)
        dob = _rms_bwd(dmb * gb_ref[...], obn, rb)
        dd = _lanes_to_heads(dob * ob, hs)
        do1_ref[...] = (_heads_to_lanes(w1, e) * dob).astype(bf16)
        d1_ref[...] = w1 * dd
        _perm_store(_heads_to_lanes(w4, e) * dob, scr, do4_ref, 4)
        _perm_store(w4 * dd, scr, d4_ref, 4)
        _perm_store(_heads_to_lanes(w16, e) * dob, scr, do16_ref, 16)
        _perm_store(w16 * dd, scr, d16_ref, 16)
        accb_ref[0:1, :] += _colsum(gx1v * (yn * gp))
        accg_ref[0:1, :] += _colsum(dn1 * yn)
        accg_ref[1:2, :] += jnp.concatenate([_colsum(dma * oan), _colsum(dmb * obn)], axis=1)

    nat = lambda w, dt: jax.ShapeDtypeStruct((BL, SEQ, w), dt)
    return pl.pallas_call(
        body, name="attn_out_bwd", grid=(BL, NJ),
        in_specs=[_tok(D), _tok(D), MOD_SPEC, _full((1, D)), _full((D, D)), _tok(AQ), _tok(BW), _full((1, AQ)), _full((1, BW)),
                  _tok(LANES), _perm_spec(4, LANES), _perm_spec(16, LANES), _full((LANES, BW)), _full((BW, LANES))],
        out_specs=[_tok(D), _tok(AQ), _tok(BW), _perm_spec(4, BW), _perm_spec(16, BW),
                   _tok(LANES), _tok(LANES), _perm_spec(4, LANES), _perm_spec(16, LANES), ACCB_SPEC, ACCG_SPEC],
        out_shape=[nat(D, bf16), nat(AQ, bf16), nat(BW, bf16), jax.ShapeDtypeStruct((BL, 4, SEQ // 4, BW), bf16),
                   jax.ShapeDtypeStruct((BL, 16, SEQ // 16, BW), bf16), nat(LANES, f32), nat(LANES, f32),
                   jax.ShapeDtypeStruct((BL, 4, SEQ // 4, LANES), f32), jax.ShapeDtypeStruct((BL, 16, SEQ // 16, LANES), f32)]
                  + ACC_SHAPES,
        scratch_shapes=[pltpu.VMEM((BW // LANES, TM, LANES), f32)],
        compiler_params=_cp(("arbitrary", "arbitrary")),
    )(gx1, y, mod, g_post, w_out, oa, ob, g_mix_a, g_mix_b, l1, l4, l16, jnp.asarray(HEAD_EXPAND, bf16),
      jnp.asarray(HEAD_SUM, bf16))


def _attn_in_bwd(dqa, dka, dva, d1, d4, d16, tc, ts1, ts2, w_in, x, gx1, mod, g_pre):
    def body(dqa_ref, dka_ref, dva_ref, dq1_ref, dk1_ref, dv1_ref, dq4_ref, dk4_ref, dv4_ref, dq16_ref, dk16_ref, dv16_ref,
             c_ref, s1_ref, s2_ref, w_ref, x_ref, gx_ref, mod_ref, g_ref, dproj_ref, dx_ref, accb_ref, accg_ref, scr):
        _acc_init(accb_ref, accg_ref)
        c, s1, s2 = c_ref[...], s1_ref[...], s2_ref[...]
        tot = lambda r1, r4, r16: r1[...] + _perm_load(r4, scr, 4) + _perm_load(r16, scr, 16)
        dqb = tot(dq1_ref, dq4_ref, dq16_ref)
        dkb = tot(dk1_ref, dk4_ref, dk16_ref)
        dvb = tot(dv1_ref, dv4_ref, dv16_ref)
        dproj = jnp.concatenate([
            _rope_t(dqa_ref[...], c, s1, s2) * 0.125, _rope_t(dka_ref[...], c, s1, s2), dva_ref[...],
            _rope_t(dqb, c, s1, s2) * 0.125, _rope_t(dkb, c, s1, s2), dvb], axis=1).astype(bf16)
        dproj_ref[...] = dproj
        dh = _dot_nt(dproj, w_ref[...])
        xn, r = _rms(x_ref[...])
        g = g_ref[...]
        dn = dh * (1.0 + mod_ref[1:2, :])
        dx_ref[...] = gx_ref[...] + _rms_bwd(dn * g, xn, r)
        accb_ref[0:1, :] += _colsum(dh * (xn * g))
        accb_ref[1:2, :] += _colsum(dh)
        accg_ref[0:1, :] += _colsum(dn * xn)

    return pl.pallas_call(
        body, name="attn_in_bwd", grid=(BL, NJ),
        in_specs=[_tok(AQ), _tok(AKV), _tok(AKV)] + [_tok(BW)] * 3 + [_perm_spec(4, BW)] * 3 + [_perm_spec(16, BW)] * 3
                 + [_tok(LANES)] * 3 + [_full((D, INW)), _tok(D), _tok(D), MOD_SPEC, _full((1, D))],
        out_specs=[_tok(INW), _tok(D), ACCB_SPEC, ACCG_SPEC],
        out_shape=[jax.ShapeDtypeStruct((BL, SEQ, INW), bf16), jax.ShapeDtypeStruct((BL, SEQ, D), f32)] + ACC_SHAPES,
        scratch_shapes=[pltpu.VMEM((BW // LANES, TM, LANES), f32)],
        compiler_params=_cp(("arbitrary", "arbitrary")),
    )(dqa, dka, dva, *d1, *d4, *d16, tc, ts1, ts2, w_in, x, gx1, mod, g_pre)


def _local_step(x, positions, mod, target, w_in, later_weights, grad_ready, grad_reduce, g_attn_pre, g_attn_post, sink_a,
                g_mix_a, g_mix_b, g_mlp_pre, g_mlp_post):
    inv = np.float32(THETA) ** (-np.arange(0, ROT, 2, dtype=np.float32) / np.float32(ROT))
    lane = np.arange(LANES) % HD
    inv_lane = jnp.asarray(np.where(lane < ROT, inv[lane % (ROT // 2)], 0.0).astype(np.float32)[None, :])
    tabs = _rope_tables(positions.reshape(BL * SEQ, 1), inv_lane)
    tc, ts1, ts2 = [t.reshape(BL, SEQ, LANES) for t in tabs]

    (h, qa, ka, va, q1, k1, v1, q4, k4, v4, q16, k16, v16) = _attn_in(x, mod, g_attn_pre, w_in, tc, ts1, ts2)
    seqs = lambda t: t.reshape(t.shape[0] * t.shape[1], t.shape[2], t.shape[3])
    q4, k4, v4, q16, k16, v16 = [seqs(t) for t in (q4, k4, v4, q16, k16, v16)]
    oa, la = _attn_fwd(qa, ka, va, sink_a, max_dist=BLK - 1, o_dtype=f32, name="attn_a_fwd")
    o1, l1 = _attn_fwd(q1, k1, v1, None, max_dist=BLK, o_dtype=bf16, name="attn_b1_fwd")
    o4, l4 = _attn_fwd(q4, k4, v4, None, max_dist=BLK, o_dtype=bf16, name="attn_b4_fwd")
    o16, l16 = _attn_fwd(q16, k16, v16, None, max_dist=BLK, o_dtype=bf16, name="attn_b16_fwd")
    b4 = lambda t: t.reshape(BL, 4, SEQ // 4, t.shape[-1])
    b16 = lambda t: t.reshape(BL, 16, SEQ // 16, t.shape[-1])
    w_out, mlp_weights, tok = later_weights((oa, o1, o4, o16))
    x1, y, mixed, ob = _mix_out(oa, o1, l1, b4(o4), b4(l4), b16(o16), b16(l16), g_mix_a, g_mix_b, w_out, x, mod + tok, g_attn_post)
    w_up, w_down = mlp_weights((x1,))
    h2, u, a = _mlp_up(x1, mod, g_mlp_pre, w_up)
    gx, dy2, accb_d, accg_d = _mlp_down(a, w_down, x1, target, mod, g_mlp_post)

    flat = lambda t: t.reshape(BL * SEQ, t.shape[-1])
    mod = mod + grad_ready("w_down", _matmul_tn(flat(a), flat(dy2), tn=D, col_blocked=False, name="grad_w_down"))
    du, gx1, accb_m, accg_m = _mlp_bwd(dy2, u, w_down, w_up, x1, gx, mod, g_mlp_pre)
    mod = mod + grad_reduce("w_down", (gx1,))
    mod = mod + grad_ready("w_up", _matmul_tn(flat(h2), flat(du), tn=D, col_blocked=True, name="grad_w_up"))

    dy, doa, do1, do4, do16, da, dl1, dl4, dl16, accb_o, accg_o = _attn_out_bwd(
        gx1, y, mod, g_attn_post, w_out, oa, ob, g_mix_a, g_mix_b, l1, b4(l4), b16(l16))
    tok = grad_reduce("w_up", (dy,))
    gw_out = _matmul_tn(flat(mixed), flat(dy), tn=D, col_blocked=False, name="grad_w_out")
    dqa, dka, dva, dsink = _attn_bwd(qa, ka, va, doa, da + tok, la, sink_a, max_dist=BLK - 1, name="attn_a_bwd")
    d1 = _attn_bwd(q1, k1, v1, do1, dl1, l1, None, max_dist=BLK, name="attn_b1_bwd")
    d4 = _attn_bwd(q4, k4, v4, seqs(do4), seqs(dl4), l4, None, max_dist=BLK, name="attn_b4_bwd")
    d16 = _attn_bwd(q16, k16, v16, seqs(do16), seqs(dl16), l16, None, max_dist=BLK, name="attn_b16_bwd")
    dproj, grad_x, accb_i, accg_i = _attn_in_bwd(dqa, dka, dva, d1, [b4(t) for t in d4], [b16(t) for t in d16],
                                                 tc, ts1, ts2, w_in, x, gx1, mod, g_attn_pre)
    gw_in = _matmul_tn(flat(h), flat(dproj), tn=INW, col_blocked=False, name="grad_w_in")
    dsink = dsink + grad_ready("w_in_w_out", (gw_in, gw_out))

    return grad_x, (accb_i, accb_o, accb_m, accb_d, accg_i, accg_o, accg_m, accg_d, dsink)


ADAW = NMOD * D // NCHIP


def _pos():
    return lax.axis_index("x"), lax.axis_index("y"), lax.axis_index("c")


def _flip(v, bit):
    return 1 - v if bit else v


def _all_peers(x, y, c):
    return [(_flip(x, k >> 2 & 1), _flip(y, k >> 1 & 1), _flip(c, k & 1)) for k in range(1, NDEV)]


def _other_chips(x, y):
    return [(1 - x, y), (x, 1 - y), (1 - x, 1 - y)]


def _rcopy(src, dst, send, recv, k, dev):
    return pltpu.make_async_remote_copy(src_ref=src, dst_ref=dst, send_sem=send.at[k], recv_sem=recv.at[k],
                                        device_id=dev, device_id_type=MESH)


def _gather_small(src, buf, send, recv):
    x, y, c = _pos()
    me = 4 * x + 2 * y + c
    peers = _all_peers(x, y, c)
    sends = [_rcopy(src, buf.at[me], send, recv, k, p) for k, p in enumerate(peers)]
    for cp in sends:
        cp.start()
    for k, (px, py, pc) in enumerate(peers):
        _rcopy(src, buf.at[4 * px + 2 * py + pc], send, recv, k, (px, py, pc)).wait_recv()
    for cp in sends:
        cp.wait_send()
    return me


def _ada_fwd(c_in, w_ada, b_cols):
    def body(c_ref, w_ref, b_ref, mod_ref, cond_ref, cbuf, mbuf, s1, r1, s2, r2):
        x, y, c = _pos()
        chip = 2 * x + y
        me = _gather_small(c_ref, cbuf, s1, r1)
        cbuf[me] = c_ref[...]
        for i in range(NDEV):
            cond_ref[BL * i:BL * (i + 1), :] = cbuf[i]
        call = cond_ref[...]
        cond = call / (1.0 + jnp.exp(-call))
        cond_ref[...] = cond
        mbuf[chip] = _dot(cond.astype(bf16), w_ref[...].astype(bf16)) + b_ref[...]
        chips = _other_chips(x, y)
        sends = [_rcopy(mbuf.at[chip], mbuf.at[chip], s2, r2, j, (px, py, c)) for j, (px, py) in enumerate(chips)]
        for cp in sends:
            cp.start()
        for j, (px, py) in enumerate(chips):
            _rcopy(mbuf.at[chip], mbuf.at[2 * px + py], s2, r2, j, (px, py, c)).wait_recv()
        for cp in sends:
            cp.wait_send()
        row = lax.broadcasted_iota(jnp.int32, (BL * NDEV, ADAW), 0)
        for s in range(NCHIP):
            slab = mbuf[s]
            for j in range(BL):
                mod_ref[j:j + 1, ADAW * s:ADAW * (s + 1)] = jnp.sum(jnp.where(row == BL * me + j, slab, 0.0), axis=0, keepdims=True)

    vm = pl.BlockSpec(memory_space=pltpu.VMEM)
    return pl.pallas_call(
        body, name="ada_fwd", in_specs=[vm, vm, vm], out_specs=[vm, vm],
        out_shape=[jax.ShapeDtypeStruct((BL, NMOD * D), f32), jax.ShapeDtypeStruct((BL * NDEV, D), f32)],
        scratch_shapes=[pltpu.VMEM((NDEV, BL, D), f32), pltpu.VMEM((NCHIP, BL * NDEV, ADAW), f32),
                        pltpu.SemaphoreType.DMA((NDEV - 1,)), pltpu.SemaphoreType.DMA((NDEV - 1,)),
                        pltpu.SemaphoreType.DMA((NCHIP - 1,)), pltpu.SemaphoreType.DMA((NCHIP - 1,))],
        compiler_params=pltpu.CompilerParams(vmem_limit_bytes=VMEM_LIMIT),
    )(c_in, w_ada, b_cols)


def _small_allreduce(accs, cond_all, after=()):
    na = len(after)

    def body(bi, bo, bm, bd, gi, go, gm, gd, dsink, cond_ref, *rest):
        gw_ref, gb_ref, small_ref, pay, pbuf, dall, s1, r1 = rest[na:]
        x, y, c = _pos()
        chip = 2 * x + y
        pay[...] = jnp.zeros_like(pay)
        for b in range(BL):
            for k, (ref, r) in enumerate(((bi, 1), (bi, 0), (bo, 0), (bm, 1), (bm, 0), (bd, 0))):
                pay[b:b + 1, D * k:D * (k + 1)] = ref[b, r:r + 1, :]
        for off, ref, r in ((OFF_G_ATTN_PRE, gi, 0), (OFF_G_ATTN_POST, go, 0), (OFF_G_MIX_A, go, 1), (OFF_G_MLP_PRE, gm, 0),
                            (OFF_G_MLP_POST, gd, 0)):
            pay[BL:BL + 1, off:off + D] = ref[r:r + 1, :]
        eye = lax.broadcasted_iota(jnp.int32, (8, LANES), 0) == lax.broadcasted_iota(jnp.int32, (8, LANES), 1)
        pay[BL:BL + 1, OFF_SINK:OFF_SINK + LANES] = jnp.sum(jnp.where(eye, dsink[...], 0.0), axis=0, keepdims=True)
        pay[BL:BL + 1, OFF_LOSS:OFF_LOSS + LANES] = gd[1:2, 0:LANES]
        me = _gather_small(pay, pbuf, s1, r1)
        pbuf[me] = pay[...]
        small = pbuf[0, BL:BL + 1, :]
        for i in range(1, NDEV):
            small = small + pbuf[i, BL:BL + 1, :]
        small_ref[...] = small
        for i in range(NDEV):
            dall[BL * i:BL * (i + 1), :] = pbuf[i, 0:BL, :]
        gb_ref[...] = jnp.sum(dall[...], axis=0, keepdims=True)
        cols = jnp.zeros((BL * NDEV, ADAW), f32)
        for s in range(NCHIP):
            cols = cols + jnp.where(chip == s, dall[:, ADAW * s:ADAW * (s + 1)], 0.0)
        gw_ref[...] = lax.dot_general(cond_ref[...], cols, (((0,), (0,)), ((), ())), preferred_element_type=f32,
                                      precision=lax.Precision.HIGHEST)

    vm = pl.BlockSpec(memory_space=pltpu.VMEM)
    return pl.pallas_call(
        body, name="small_allreduce", in_specs=[vm] * 10 + [pl.BlockSpec(memory_space=pl.ANY)] * na, out_specs=[vm] * 3,
        out_shape=[jax.ShapeDtypeStruct((D, ADAW), f32), jax.ShapeDtypeStruct((1, PAYW), f32), jax.ShapeDtypeStruct((1, PAYW), f32)],
        scratch_shapes=[pltpu.VMEM((4, PAYW), f32), pltpu.VMEM((NDEV, 4, PAYW), f32), pltpu.VMEM((BL * NDEV, PAYW), f32),
                        pltpu.SemaphoreType.DMA((NDEV - 1,)), pltpu.SemaphoreType.DMA((NDEV - 1,))],
        compiler_params=pltpu.CompilerParams(vmem_limit_bytes=VMEM_LIMIT),
    )(*accs, cond_all, *after)


def _half(ref, c):
    r2 = ref.shape[0] // 2
    return ref.at[pl.ds(pl.multiple_of(c * r2, 16), r2), :]


HBM_SPEC = pl.BlockSpec(memory_space=pltpu.HBM)
SEM_SPEC = pl.BlockSpec(memory_space=pltpu.SEMAPHORE)
EFFECT = pltpu.SideEffectType.DATAFLOW_SIDE_EFFECTING
NLINK = NCHIP - 1


def _in_hbm(a):
    return pltpu.with_memory_space_constraint(a, pltpu.HBM)


NSEM = 4


def _split_start(name, srcs, land_shapes, builds, after=(), lands=None):
    n = len(srcs)

    na = len(after)

    def body(*refs):
        src, land, token = refs[:n], refs[n:2 * n], refs[-1]
        send, recv = refs[2 * n + na:3 * n + na], refs[3 * n + na:4 * n + na]
        for t in range(n):
            for out_cp, _ in builds[t](src[t], land[t], send[t], recv[t]):
                out_cp.start()
        token[...] = jnp.zeros_like(token)

    if lands is None:
        lands = [lax.empty(s.shape, s.dtype) for s in land_shapes]
    lands = [_in_hbm(a) for a in lands]
    sems = [pltpu.SemaphoreType.DMA((NSEM,))] * (2 * n)
    thru = [pltpu.HBM(a.shape, a.dtype) for a in list(srcs) + lands]
    res = pl.pallas_call(
        body, name=name, out_shape=sems + thru + [jax.ShapeDtypeStruct((8, LANES), f32)],
        in_specs=[HBM_SPEC] * (2 * n) + [pl.BlockSpec(memory_space=pl.ANY)] * na,
        out_specs=[SEM_SPEC] * (2 * n) + [HBM_SPEC] * (2 * n) + [pl.BlockSpec(memory_space=pltpu.VMEM)],
        input_output_aliases={i: 2 * n + i for i in range(2 * n)},
        compiler_params=pltpu.CompilerParams(has_side_effects=EFFECT),
    )(*[_in_hbm(a) for a in srcs], *lands, *after)
    flight = [(res[2 * n + t], res[3 * n + t], res[t], res[n + t]) for t in range(n)]
    return flight, res[-1][0, 0]


def _split_wait(name, flight, builds, after):
    m = len(flight)
    na = len(after)

    def body(*refs):
        src, land, send, recv = refs[:m], refs[m:2 * m], refs[2 * m:3 * m], refs[3 * m:4 * m]
        for t in range(m):
            for out_cp, in_cp in builds[t](src[t], land[t], send[t], recv[t]):
                out_cp.wait_send()
                in_cp.wait_recv()

    ops = [f[0] for f in flight] + [f[1] for f in flight] + [f[2] for f in flight] + [f[3] for f in flight]
    res = pl.pallas_call(
        body, name=name, out_shape=[pltpu.HBM(a.shape, a.dtype) for a in ops[:2 * m]],
        in_specs=[HBM_SPEC] * (2 * m) + [SEM_SPEC] * (2 * m) + [pl.BlockSpec(memory_space=pl.ANY)] * na,
        out_specs=[HBM_SPEC] * (2 * m), input_output_aliases={i: i for i in range(2 * m)},
        compiler_params=pltpu.CompilerParams(has_side_effects=EFFECT),
    )(*ops, *after)
    return res[:m], res[m:2 * m]


def _weight_copies(src, land, send, recv):
    x, y, c = _pos()
    chip = 2 * x + y
    return [(_rcopy(_half(src, c), _half(land.at[chip], c), send, recv, j, (px, py, c)),
             _rcopy(_half(src, c), _half(land.at[2 * px + py], c), send, recv, j, (px, py, c)))
            for j, (px, py) in enumerate(_other_chips(x, y))]


def _grad_copies(src, land, send, recv):
    x, y, c = _pos()
    return [(_rcopy(src.at[2 * px + py], land.at[j], send, recv, j, (px, py, c)),
             _rcopy(src.at[2 * px + py], land.at[j], send, recv, j, (px, py, c)))
            for j, (px, py) in enumerate(_other_chips(x, y))]


def _pair_grad_copies(src, land, send, recv):
    x, y, c = _pos()
    r2 = src.shape[1] // 2
    cp = _rcopy(src.at[:, pl.ds(pl.multiple_of((1 - c) * r2, 8), r2), :], land, send, recv, 0, (x, y, 1 - c))
    return [(cp, cp)]


def _pair_weight_copies(src, land, send, recv):
    x, y, c = _pos()
    sib = (x, y, 1 - c)
    cps = []
    for j, (px, py) in enumerate(_other_chips(x, y)):
        mine, theirs = _half(land.at[2 * px + py], c), _half(land.at[2 * px + py], 1 - c)
        cps.append((_rcopy(mine, mine, send, recv, j, sib), _rcopy(theirs, theirs, send, recv, j, sib)))
    own = _rcopy(src, land.at[2 * x + y], send, recv, NLINK, sib)
    return cps + [(own, own)]


def _pair_forward(shards, gathered, name):
    nt = len(shards)

    def body(*refs):
        sh, gin, gout = refs[:nt], refs[nt:2 * nt], refs[2 * nt:3 * nt]
        send, recv = refs[3 * nt:]
        x, y, c = _pos()
        chip = 2 * x + y
        sib = (x, y, 1 - c)
        chips = _other_chips(x, y)
        cps = []
        for t in range(nt):
            for j, (px, py) in enumerate(chips):
                cps.append(_rcopy(_half(gin[t].at[2 * px + py], c), _half(gout[t].at[2 * px + py], c), send, recv, 4 * t + j, sib))
            cps.append(_rcopy(sh[t], gout[t].at[chip], send, recv, 4 * t + 3, sib))
        for cp in cps:
            cp.start()
        for t in range(nt):
            for j, (px, py) in enumerate(chips):
                theirs = _half(gout[t].at[2 * px + py], 1 - c)
                _rcopy(theirs, theirs, send, recv, 4 * t + j, sib).wait_recv()
            _rcopy(sh[t], gout[t].at[chip], send, recv, 4 * t + 3, sib).wait_recv()
        for cp in cps:
            cp.wait_send()

    hbm = pl.BlockSpec(memory_space=pl.ANY)
    return pl.pallas_call(
        body, name=name, in_specs=[hbm] * (2 * nt), out_specs=[hbm] * nt,
        out_shape=[jax.ShapeDtypeStruct(g.shape, g.dtype) for g in gathered],
        input_output_aliases={nt + t: t for t in range(nt)},
        scratch_shapes=[pltpu.SemaphoreType.DMA((4 * nt,)), pltpu.SemaphoreType.DMA((4 * nt,))],
    )(*shards, *gathered)


RS_ROWS = 128


def _pair_add(g, landed, c_arr, name):
    _, r2, cw = landed.shape
    nr = r2 // RS_ROWS

    def body(c_ref, g_ref, p_ref, o_ref):
        o_ref[...] = (g_ref[...] + p_ref[...]).astype(bf16)

    gs = pltpu.PrefetchScalarGridSpec(
        num_scalar_prefetch=1, grid=(NCHIP, nr),
        in_specs=[pl.BlockSpec((None, RS_ROWS, cw), lambda s, j, c: (s, c[0] * nr + j, 0)),
                  pl.BlockSpec((None, RS_ROWS, cw), lambda s, j, c: (s, j, 0))],
        out_specs=pl.BlockSpec((None, RS_ROWS, cw), lambda s, j, c: (s, j, 0)))
    return pl.pallas_call(body, name=name, grid_spec=gs, out_shape=jax.ShapeDtypeStruct((NCHIP, r2, cw), bf16),
                          compiler_params=_cp(("arbitrary", "arbitrary")))(c_arr, g, landed)


def _chip_add(half, landed, pos_arr, name):
    _, r2, cw = half.shape
    nr = r2 // RS_ROWS

    def body(s_ref, h_ref, q_ref, o_ref):
        acc = h_ref[...].astype(f32)
        for j in range(NCHIP - 1):
            acc = acc + q_ref[j].astype(f32)
        o_ref[...] = acc

    gs = pltpu.PrefetchScalarGridSpec(
        num_scalar_prefetch=1, grid=(nr,),
        in_specs=[pl.BlockSpec((None, RS_ROWS, cw), lambda j, s: (s[0], j, 0)),
                  pl.BlockSpec((NCHIP - 1, RS_ROWS, cw), lambda j, s: (0, j, 0))],
        out_specs=pl.BlockSpec((RS_ROWS, cw), lambda j, s: (s[1] * nr + j, 0)))
    return pl.pallas_call(body, name=name, grid_spec=gs, out_shape=jax.ShapeDtypeStruct((2 * r2, cw), f32),
                          compiler_params=_cp(("arbitrary",)))(pos_arr, half, landed)


def _ag_pair(fulls, name):
    nt = len(fulls)

    def body(*refs):
        ins, outs = refs[:nt], refs[nt:2 * nt]
        send, recv = refs[2 * nt:]
        x, y, c = _pos()
        sib = (x, y, 1 - c)
        cps = [_rcopy(_half(ins[t], c), _half(outs[t], c), send, recv, t, sib) for t in range(nt)]
        for cp in cps:
            cp.start()
        for t in range(nt):
            _rcopy(_half(ins[t], c), _half(outs[t], 1 - c), send, recv, t, sib).wait_recv()
        for cp in cps:
            cp.wait_send()

    hbm = pl.BlockSpec(memory_space=pl.ANY)
    return pl.pallas_call(
        body, name=name, in_specs=[hbm] * nt, out_specs=[hbm] * nt,
        out_shape=[jax.ShapeDtypeStruct(a.shape, f32) for a in fulls],
        input_output_aliases={t: t for t in range(nt)},
        scratch_shapes=[pltpu.SemaphoreType.DMA((nt,)), pltpu.SemaphoreType.DMA((nt,))],
    )(*fulls)


def _adamw_math(w, g, m, v):
    m = B1 * m + (1.0 - B1) * g
    v = B2 * v + (1.0 - B2) * jnp.square(g)
    m_hat = m / (1.0 - B1 ** STEP)
    v_hat = v / (1.0 - B2 ** STEP)
    return -LR * (m_hat / (jnp.sqrt(v_hat) + AEPS) + WD * w), m, v


ADAM_ROWS = 256


def _adamw(w, g, m, v, name):
    r, cw = w.shape

    def body(w_ref, g_ref, m_ref, v_ref, d_ref, mo_ref, vo_ref):
        d_ref[...], mo_ref[...], vo_ref[...] = _adamw_math(w_ref[...], g_ref[...], m_ref[...], v_ref[...])

    spec = pl.BlockSpec((ADAM_ROWS, cw), lambda i: (i, 0))
    return pl.pallas_call(body, name=name, grid=(r // ADAM_ROWS,), in_specs=[spec] * 4, out_specs=[spec] * 3,
                          out_shape=[jax.ShapeDtypeStruct((r, cw), f32)] * 3, compiler_params=_cp(("arbitrary",)))(w, g, m, v)


SMALL = (("b_ada", None, PAYW), ("g_attn_pre", OFF_G_ATTN_PRE, D), ("g_attn_post", OFF_G_ATTN_POST, D), ("sink_a", OFF_SINK, 8),
         ("g_mix_a", OFF_G_MIX_A, AQ), ("g_mix_b", OFF_G_MIX_B, BW), ("g_mlp_pre", OFF_G_MLP_PRE, D), ("g_mlp_post", OFF_G_MLP_POST, D))


def _adamw_small(small, gb, params):
    n = len(SMALL)

    def body(*refs):
        small_ref, gb_ref = refs[:2]
        wmv = refs[2:2 + 3 * n]
        loss_ref = refs[2 + 3 * n]
        outs = refs[3 + 3 * n:]
        loss_ref[...] = small_ref[:, OFF_LOSS:OFF_LOSS + 1] * (0.5 / D)
        for i, (_, off, width) in enumerate(SMALL):
            g = gb_ref[...] if off is None else small_ref[:, off:off + width]
            w_ref, m_ref, v_ref = wmv[3 * i:3 * i + 3]
            outs[4 * i][...] = g
            outs[4 * i + 1][...], outs[4 * i + 2][...], outs[4 * i + 3][...] = _adamw_math(w_ref[...], g, m_ref[...], v_ref[...])

    vm = pl.BlockSpec(memory_space=pltpu.VMEM)
    out_shape = [jax.ShapeDtypeStruct((1, 1), f32)]
    for _, _, width in SMALL:
        out_shape += [jax.ShapeDtypeStruct((1, width), f32)] * 4
    flat = [a for wmv in params for a in wmv]
    res = pl.pallas_call(body, name="adamw_small", in_specs=[vm] * (2 + 3 * n), out_specs=[vm] * len(out_shape),
                         out_shape=out_shape)(small, gb, *flat)
    return res[0], {name: res[1 + 4 * i:5 + 4 * i] for i, (name, _, _) in enumerate(SMALL)}


def kernel(x, c, positions, w_ada, b_ada, g_attn_pre, g_attn_post, w_in, sink_a, g_mix_a, g_mix_b, w_out, g_mlp_pre, g_mlp_post, w_up, w_down, loss_target, m_w_ada, m_b_ada, m_g_attn_pre, m_g_attn_post, m_w_in, m_sink_a, m_g_mix_a, m_g_mix_b, m_w_out, m_g_mlp_pre, m_g_mlp_post, m_w_up, m_w_down, v_w_ada, v_b_ada, v_g_attn_pre, v_g_attn_post, v_w_in, v_sink_a, v_g_mix_a, v_g_mix_b, v_w_out, v_g_mlp_pre, v_g_mlp_post, v_w_up, v_w_down):
    given = dict(w_ada=w_ada, b_ada=b_ada, g_attn_pre=g_attn_pre, g_attn_post=g_attn_post, w_in=w_in, sink_a=sink_a, g_mix_a=g_mix_a,
                 g_mix_b=g_mix_b, w_out=w_out, g_mlp_pre=g_mlp_pre, g_mlp_post=g_mlp_post, w_up=w_up, w_down=w_down)
    moms = dict(w_ada=(m_w_ada, v_w_ada), b_ada=(m_b_ada, v_b_ada), g_attn_pre=(m_g_attn_pre, v_g_attn_pre),
                g_attn_post=(m_g_attn_post, v_g_attn_post), w_in=(m_w_in, v_w_in), sink_a=(m_sink_a, v_sink_a),
                g_mix_a=(m_g_mix_a, v_g_mix_a), g_mix_b=(m_g_mix_b, v_g_mix_b), w_out=(m_w_out, v_w_out),
                g_mlp_pre=(m_g_mlp_pre, v_g_mlp_pre), g_mlp_post=(m_g_mlp_post, v_g_mlp_post), w_up=(m_w_up, v_w_up),
                w_down=(m_w_down, v_w_down))
    order = ["w_ada", "b_ada", "g_attn_pre", "g_attn_post", "w_in", "sink_a", "g_mix_a", "g_mix_b", "w_out", "g_mlp_pre",
             "g_mlp_post", "w_up", "w_down"]
    xi, yi, ci = _pos()
    chip = 2 * xi + yi

    c_arr = jnp.reshape(ci, (1,)).astype(jnp.int32)
    pos_arr = jnp.stack([chip, ci]).astype(jnp.int32)
    big = ("w_in", "w_out", "w_up", "w_down")

    shards = [given[n][0].astype(bf16) for n in big]
    gathered = [jax.ShapeDtypeStruct((NCHIP,) + s.shape, bf16) for s in shards]
    flight_in, tok = _split_start("weights_start_first", shards[:1], gathered[:1], [_weight_copies])
    b_cols = lax.dynamic_slice(b_ada, (0, chip * ADAW), (1, ADAW))
    mod, cond_all = _ada_fwd(c + tok, w_ada[0], b_cols)
    flight_rest, tok = _split_start("weights_start_rest", shards[1:], gathered[1:], [_weight_copies] * 3, after=(mod,))
    mod = mod.reshape(BL, NMOD, D) + tok
    srcs, lands = _split_wait("weights_wait_first", flight_in, [_weight_copies], (mod,))
    (win_g,) = _pair_forward(srcs, lands, "weights_pair_first")
    w_in_full = win_g.transpose(1, 0, 2).reshape(D, INW)

    def later_weights(after):
        srcs, lands = _split_wait("weights_wait_rest", flight_rest, [_weight_copies] * 3, after)
        (wout_g,) = _pair_forward(srcs[:1], lands[:1], "weights_pair_out")
        fl, tk = _split_start("weights_pair_start", srcs[1:], None, [_pair_weight_copies] * 2, lands=lands[1:])

        def mlp_weights(after):
            _, (wup_g, wdn_g) = _split_wait("weights_pair_wait", fl, [_pair_weight_copies] * 2, after)
            return wup_g, wdn_g.reshape(DFF, D)

        return wout_g.reshape(D, D), mlp_weights, tk

    crossing, pending = {}, {}

    def grad_ready(group, g):
        if group == "w_down":
            names, slabs = ("w_down",), [g.reshape(NCHIP, DFF // NCHIP, D)]
        elif group == "w_up":
            names, slabs = ("w_up",), [g]
        else:
            names = ("w_in", "w_out")
            slabs = [g[0].reshape(D, NCHIP, INW // NCHIP).transpose(1, 0, 2), g[1].reshape(NCHIP, D // NCHIP, D)]
        fl, tk = _split_start("grad_pair_start_" + group, slabs,
                              [jax.ShapeDtypeStruct((NCHIP, s.shape[1] // 2, s.shape[2]), f32) for s in slabs],
                              [_pair_grad_copies] * len(names))
        crossing[group] = (names, fl)
        return tk

    def grad_reduce(group, after):
        names, fl = crossing[group]
        slabs, landed = _split_wait("grad_pair_wait_" + group, fl, [_pair_grad_copies] * len(names), after)
        halves = [_pair_add(s, p, c_arr, "grad_pair_sum_" + n) for s, p, n in zip(slabs, landed, names)]
        fl, tk = _split_start("grad_start_" + group, halves,
                              [jax.ShapeDtypeStruct((NLINK,) + h.shape[1:], bf16) for h in halves], [_grad_copies] * len(names))
        pending[group] = (names, fl)
        return tk

    grad_x, accs = _local_step(x, positions, mod, loss_target, w_in_full, later_weights, grad_ready, grad_reduce,
                               g_attn_pre, g_attn_post, sink_a, g_mix_a, g_mix_b, g_mlp_pre, g_mlp_post)

    def finish(groups, after):
        names = sum((pending[g][0] for g in groups), ())
        fl = sum((pending[g][1] for g in groups), [])
        halves, landed = _split_wait("grad_wait_" + groups[0], fl, [_grad_copies] * len(names), after)
        fulls = [_chip_add(h, q, pos_arr, "grad_chip_sum_" + n) for h, q, n in zip(halves, landed, names)]
        return dict(zip(names, _ag_pair(fulls, "grad_pair_gather_" + groups[0])))

    grads, out = {}, {}

    def update(n):
        d, m2, v2 = _adamw(given[n][0], grads[n], moms[n][0][0], moms[n][1][0], "adamw_" + n)
        out[n] = (grads[n][None], d[None], m2[None], v2[None])
        return v2

    grads["w_ada"], gb, small = _small_allreduce(accs, cond_all)
    small = small + grad_reduce("w_in_w_out", (small,))
    grads.update(finish(("w_down", "w_up"), (small,)))
    last = [update(n) for n in ("w_down", "w_up", "w_ada")]
    grads.update(finish(("w_in_w_out",), tuple(last)))
    update("w_in")
    update("w_out")
    loss, res = _adamw_small(small, gb, [(given[n], moms[n][0], moms[n][1]) for n, _, _ in SMALL])
    for n, _, _ in SMALL:
        out[n] = tuple(res[n])
    return (loss.reshape(()), grad_x, *[out[n][0] for n in order], *[out[n][1] for n in order],
            *[out[n][2] for n in order], *[out[n][3] for n in order])
```

```python
import functools

import numpy as np
import jax
import jax.numpy as jnp
from jax import lax
from jax.experimental import pallas as pl
from jax.experimental.pallas import tpu as pltpu

f32 = jnp.float32
bf16 = jnp.bfloat16
MESH = pl.DeviceIdType.MESH

D = 1024
SEQ = 2048
BL = 2
HD = 64
AQ = 512
AKV = 128
BW = 512
INW = 2304
DFF = 4096
NMOD = 6
ROT = 16
THETA = 500000.0
EPS = 1e-6
NEG = -1e30
BLK = 128
TM = 256
NJ = SEQ // TM
LANES = 128
NCHIP = 4
NDEV = 8
VMEM_LIMIT = 56 << 20

LR, B1, B2, AEPS, WD, STEP = 0.001, 0.9, 0.999, 1e-08, 0.01, 10

OFF_G_ATTN_PRE, OFF_G_ATTN_POST, OFF_G_MIX_A, OFF_G_MIX_B = 0, 1024, 2048, 2560
OFF_G_MLP_PRE, OFF_G_MLP_POST, OFF_SINK, OFF_LOSS = 3072, 4096, 5120, 5248
PAYW = NMOD * D


def _cp(sem=None):
    return pltpu.CompilerParams(dimension_semantics=sem, vmem_limit_bytes=VMEM_LIMIT)


def _dot(a, b):
    return jnp.dot(a, b, preferred_element_type=f32)


def _dot_nt(a, b):
    return lax.dot_general(a, b, (((1,), (1,)), ((), ())), preferred_element_type=f32)


def _dot_tn(a, b):
    return lax.dot_general(a, b, (((0,), (0,)), ((), ())), preferred_element_type=f32)


def _rms(x):
    r = lax.rsqrt(jnp.mean(x * x, axis=-1, keepdims=True) + EPS)
    return x * r, r


def _rms_bwd(dy, y, r):
    return r * (dy - y * jnp.mean(dy * y, axis=-1, keepdims=True))


def _colsum(v):
    return jnp.sum(v, axis=0, keepdims=True)


def _rope(p, c, s1, s2):
    outs = []
    for c0 in range(0, p.shape[1], LANES):
        pc = p[:, c0:c0 + LANES]
        outs.append(pc * c + pltpu.roll(pc, LANES - ROT // 2, 1) * s1 + pltpu.roll(pc, ROT // 2, 1) * s2)
    return outs[0] if len(outs) == 1 else jnp.concatenate(outs, axis=1)


def _rope_t(g, c, s1, s2):
    outs = []
    for c0 in range(0, g.shape[1], LANES):
        gc = g[:, c0:c0 + LANES]
        outs.append(gc * c + pltpu.roll(gc * s1, ROT // 2, 1) + pltpu.roll(gc * s2, LANES - ROT // 2, 1))
    return outs[0] if len(outs) == 1 else jnp.concatenate(outs, axis=1)


def _perm_store(val, scr, out_ref, d):
    nc = val.shape[1] // LANES
    for c in range(nc):
        scr[c] = val[:, LANES * c:LANES * (c + 1)]
    for c in range(nc):
        for r in range(d):
            out_ref[r, :, LANES * c:LANES * (c + 1)] = scr[c, pl.ds(r, TM // d, stride=d), :].astype(out_ref.dtype)


def _perm_load(in_ref, scr, d):
    nc = in_ref.shape[-1] // LANES
    for c in range(nc):
        for r in range(d):
            scr[c, pl.ds(r, TM // d, stride=d), :] = in_ref[r, :, LANES * c:LANES * (c + 1)].astype(f32)
    return jnp.concatenate([scr[c] for c in range(nc)], axis=1)


def _tok(w, dtype=None):
    return pl.BlockSpec((None, TM, w), lambda b, j: (b, j, 0))


def _perm_spec(d, w):
    return pl.BlockSpec((None, d, TM // d, w), lambda b, j: (b, 0, j, 0))


def _full(shape):
    n = len(shape)
    return pl.BlockSpec(shape, lambda b, j: (0,) * n)


MOD_SPEC = pl.BlockSpec((None, NMOD, D), lambda b, j: (b, 0, 0))
ACCB_SPEC = pl.BlockSpec((None, 8, D), lambda b, j: (b, 0, 0))
ACCG_SPEC = pl.BlockSpec((8, D), lambda b, j: (0, 0))
ACC_SHAPES = [jax.ShapeDtypeStruct((BL, 8, D), f32), jax.ShapeDtypeStruct((8, D), f32)]


def _acc_init(accb_ref, accg_ref):
    b, j = pl.program_id(0), pl.program_id(1)

    @pl.when(j == 0)
    def _():
        accb_ref[...] = jnp.zeros_like(accb_ref)

    @pl.when((b == 0) & (j == 0))
    def _():
        accg_ref[...] = jnp.zeros_like(accg_ref)


def _rope_tables(pos_col, inv_lane):
    def body(p_ref, inv_ref, c_ref, s1_ref, s2_ref):
        ang = p_ref[...].astype(f32) * inv_ref[...]
        j = lax.broadcasted_iota(jnp.int32, (TM, LANES), 1) % HD
        cs, sn = jnp.cos(ang), jnp.sin(ang)
        c_ref[...] = jnp.where(j < ROT, cs, 1.0)
        s1_ref[...] = jnp.where(j < ROT // 2, -sn, 0.0)
        s2_ref[...] = jnp.where((j >= ROT // 2) & (j < ROT), sn, 0.0)

    n = BL * SEQ // TM
    return pl.pallas_call(
        body, name="rope_tables", grid=(n,),
        in_specs=[pl.BlockSpec((TM, 1), lambda i: (i, 0)), pl.BlockSpec((1, LANES), lambda i: (0, 0))],
        out_specs=[pl.BlockSpec((TM, LANES), lambda i: (i, 0))] * 3,
        out_shape=[jax.ShapeDtypeStruct((BL * SEQ, LANES), f32)] * 3,
    )(pos_col, inv_lane)


def _attn_in(x, mod, g_pre, w_in, tc, ts1, ts2):
    def body(x_ref, mod_ref, g_ref, w_ref, c_ref, s1_ref, s2_ref,
             h_ref, qa_ref, ka_ref, va_ref, q1_ref, k1_ref, v1_ref, q4_ref, k4_ref, v4_ref, q16_ref, k16_ref, v16_ref,
             scr):
        xn, _ = _rms(x_ref[...])
        h = (xn * g_ref[...]) * (1.0 + mod_ref[1:2, :]) + mod_ref[0:1, :]
        hb = h.astype(bf16)
        h_ref[...] = hb
        proj = _dot(hb, w_ref[...])
        c, s1, s2 = c_ref[...], s1_ref[...], s2_ref[...]
        o1, o2, o3, o4, o5 = AQ, AQ + AKV, AQ + 2 * AKV, AQ + 2 * AKV + BW, AQ + 2 * AKV + 2 * BW
        qa_ref[...] = (_rope(proj[:, :o1], c, s1, s2) * 0.125).astype(bf16)
        ka_ref[...] = _rope(proj[:, o1:o2], c, s1, s2).astype(bf16)
        va_ref[...] = proj[:, o2:o3].astype(bf16)
        qb = _rope(proj[:, o3:o4], c, s1, s2) * 0.125
        kb = _rope(proj[:, o4:o5], c, s1, s2)
        vb = proj[:, o5:]
        for val, r1, r4, r16 in ((qb, q1_ref, q4_ref, q16_ref), (kb, k1_ref, k4_ref, k16_ref), (vb, v1_ref, v4_ref, v16_ref)):
            r1[...] = val.astype(bf16)
            _perm_store(val, scr, r4, 4)
            _perm_store(val, scr, r16, 16)

    nat = lambda w: jax.ShapeDtypeStruct((BL, SEQ, w), bf16)
    p4 = jax.ShapeDtypeStruct((BL, 4, SEQ // 4, BW), bf16)
    p16 = jax.ShapeDtypeStruct((BL, 16, SEQ // 16, BW), bf16)
    return pl.pallas_call(
        body, name="attn_in", grid=(BL, NJ),
        in_specs=[_tok(D), MOD_SPEC, _full((1, D)), _full((D, INW)), _tok(LANES), _tok(LANES), _tok(LANES)],
        out_specs=[_tok(D), _tok(AQ), _tok(AKV), _tok(AKV)] + [_tok(BW)] * 3 + [_perm_spec(4, BW)] * 3 + [_perm_spec(16, BW)] * 3,
        out_shape=[nat(D), nat(AQ), nat(AKV), nat(AKV)] + [nat(BW)] * 3 + [p4] * 3 + [p16] * 3,
        scratch_shapes=[pltpu.VMEM((BW // LANES, TM, LANES), f32)],
        compiler_params=_cp(("arbitrary", "arbitrary")),
    )(x, mod, g_pre, w_in, tc, ts1, ts2)


def _kv_cat(cur_ref, prev_ref, p, gqa, cache):
    def one(ref):
        if not gqa:
            return ref[:, LANES * p:LANES * (p + 1)]
        k = ref[...]
        kr = pltpu.roll(k, HD, 1)
        lo = lax.broadcasted_iota(jnp.int32, k.shape, 1) < HD
        return jnp.where(lo, k, kr) if p < 2 else jnp.where(lo, kr, k)

    key = (id(cur_ref), p // 2 if gqa else p)
    if key not in cache:
        cache[key] = one(cur_ref) if prev_ref is None else jnp.concatenate([one(prev_ref), one(cur_ref)], axis=0)
    return cache[key]


def _lane_half(a, hh):
    lo = lax.broadcasted_iota(jnp.int32, a.shape, 1) < HD
    return jnp.where(lo, a, jnp.zeros_like(a)) if hh == 0 else jnp.where(lo, jnp.zeros_like(a), a)


def _attn_fwd(q, k, v, sink, *, max_dist, o_dtype, name):
    n, l, w = q.shape
    wk = k.shape[-1]
    nb = l // BLK
    gqa = wk != w
    has_sink = sink is not None

    def body(*refs):
        if has_sink:
            sink_ref, refs = refs[0], refs[1:]
        if nb > 1:
            q_ref, kc_ref, kp_ref, vc_ref, vp_ref, o_ref, lse_ref, sscr, pscr, dscr = refs
        else:
            q_ref, kc_ref, vc_ref, o_ref, lse_ref, sscr, pscr, dscr = refs
        i = pl.program_id(1)
        qi = lax.broadcasted_iota(jnp.int32, (BLK, BLK), 0)
        kj = lax.broadcasted_iota(jnp.int32, (BLK, BLK), 1)
        tri = kj <= qi
        eye = kj == qi
        cache = {}
        for p in range(w // LANES):
            qpair = q_ref[:, LANES * p:LANES * (p + 1)]
            kcat = _kv_cat(kc_ref, kp_ref if nb > 1 else None, p, gqa, cache)
            for hh in range(2):
                s = _dot_nt(_lane_half(qpair, hh), kcat)
                if nb > 1:
                    sp = jnp.where(i > 0, s[:, :BLK], NEG)
                    sscr[2 * p + hh] = jnp.where(tri, s[:, BLK:], sp)
                    if diag:
                        dscr[2 * p + hh] = jnp.where(eye, sp, NEG)
                else:
                    sscr[2 * p + hh] = jnp.where(tri, s, NEG)
        lane = lax.broadcasted_iota(jnp.int32, (BLK, LANES), 1)
        lse_all = jnp.zeros((BLK, LANES), f32)
        for p in range(w // LANES):
            for hh in range(2):
                h = 2 * p + hh
                comb = sscr[h]
                if diag:
                    dtile = dscr[h]
                    m = jnp.max(jnp.maximum(comb, dtile), axis=-1, keepdims=True)
                else:
                    m = jnp.max(comb, axis=-1, keepdims=True)
                if has_sink:
                    sk = sink_ref[0, h]
                    m = jnp.maximum(m, sk)
                e = jnp.exp(comb - m)
                if diag:
                    ed = jnp.exp(dtile - m)
                    den = jnp.sum(e + ed, axis=-1, keepdims=True)
                else:
                    den = jnp.sum(e, axis=-1, keepdims=True)
                if has_sink:
                    den = den + jnp.exp(sk - m)
                inv = 1.0 / den
                if nb > 1:
                    pscr[h, :, :BLK] = (jnp.where(tri, ed if diag else 0.0, e) * inv).astype(bf16)
                    pscr[h, :, BLK:] = (jnp.where(tri, e, 0.0) * inv).astype(bf16)
                else:
                    pscr[h] = (e * inv).astype(bf16)
                lse_all = jnp.where(lane == h, jnp.broadcast_to(m + jnp.log(den), (BLK, LANES)), lse_all)
        lse_ref[...] = lse_all
        for p in range(w // LANES):
            vcat = _kv_cat(vc_ref, vp_ref if nb > 1 else None, p, gqa, cache)
            key = ("halves", id(vc_ref), p // 2 if gqa else p)
            if key not in cache:
                cache[key] = (_lane_half(vcat, 0), _lane_half(vcat, 1))
            o_ref[:, LANES * p:LANES * (p + 1)] = (_dot(pscr[2 * p], cache[key][0])
                                                   + _dot(pscr[2 * p + 1], cache[key][1])).astype(o_ref.dtype)

    assert max_dist in (BLK - 1, BLK)
    diag = nb > 1 and max_dist == BLK
    cur = lambda ww: pl.BlockSpec((None, BLK, ww), lambda a, i: (a, i, 0))
    prev = lambda ww: pl.BlockSpec((None, BLK, ww), lambda a, i: (a, jnp.maximum(i - 1, 0), 0))
    in_specs = [cur(w), cur(wk)] + ([prev(wk)] if nb > 1 else []) + [cur(wk)] + ([prev(wk)] if nb > 1 else [])
    args = [q, k] + ([k] if nb > 1 else []) + [v] + ([v] if nb > 1 else [])
    if has_sink:
        in_specs = [pl.BlockSpec(memory_space=pltpu.SMEM)] + in_specs
        args = [sink] + args
    return pl.pallas_call(
        body, name=name, grid=(n, nb), in_specs=in_specs,
        out_specs=[cur(w), cur(LANES)],
        out_shape=[jax.ShapeDtypeStruct((n, l, w), o_dtype), jax.ShapeDtypeStruct((n, l, LANES), f32)],
        scratch_shapes=[pltpu.VMEM((w // HD, BLK, BLK), f32), pltpu.VMEM((w // HD, BLK, 2 * BLK if nb > 1 else BLK), bf16),
                        pltpu.VMEM((w // HD if diag else 1, BLK, BLK), f32)],
        compiler_params=_cp(("arbitrary", "arbitrary")),
    )(*args)


def _attn_bwd(q, k, v, do, delta, lse, sink, *, max_dist, name):
    n, l, w = q.shape
    wk = k.shape[-1]
    nb = l // BLK
    gqa = wk != w
    has_sink = sink is not None

    def body(*refs):
        if has_sink:
            sink_ref, refs = refs[0], refs[1:]
        if nb > 1:
            q_ref, kc_ref, kp_ref, vc_ref, vp_ref, do_ref, delta_ref, lse_ref = refs[:8]
            rest = refs[8:]
        else:
            q_ref, kc_ref, vc_ref, do_ref, delta_ref, lse_ref = refs[:6]
            rest = refs[6:]
        if has_sink:
            dq_ref, dk_ref, dv_ref, dsink_ref = rest[:4]
            rest = rest[4:]
        else:
            dq_ref, dk_ref, dv_ref = rest[:3]
            rest = rest[3:]
        step = pl.program_id(1)
        blk_idx = nb - 1 - step
        if nb > 1:
            ck, cv = rest[:2]
            rest = rest[2:]

            @pl.when(step == 0)
            def _():
                ck[...] = jnp.zeros_like(ck)
                cv[...] = jnp.zeros_like(cv)

        sscr, dpscr, pscr, dsscr = rest[:4]
        if diag:
            dscr, ddscr = rest[4:]
        if has_sink:
            @pl.when((pl.program_id(0) == 0) & (step == 0))
            def _():
                dsink_ref[...] = jnp.zeros_like(dsink_ref)

        lane = lax.broadcasted_iota(jnp.int32, (BLK, LANES), 1)
        lo = lane < HD
        qi = lax.broadcasted_iota(jnp.int32, (BLK, BLK), 0)
        kj = lax.broadcasted_iota(jnp.int32, (BLK, BLK), 1)
        tri = kj <= qi
        eye = kj == qi
        cache = {}
        kp, vp = (kp_ref, vp_ref) if nb > 1 else (None, None)
        rows = 2 * BLK if nb > 1 else BLK
        for p in range(w // LANES):
            sl = slice(LANES * p, LANES * (p + 1))
            qpair, dopair = q_ref[:, sl], do_ref[:, sl]
            kcat, vcat = _kv_cat(kc_ref, kp, p, gqa, cache), _kv_cat(vc_ref, vp, p, gqa, cache)
            for hh in range(2):
                h = 2 * p + hh
                s = _dot_nt(_lane_half(qpair, hh), kcat)
                dp = _dot_nt(_lane_half(dopair, hh), vcat)
                if nb > 1:
                    sp = jnp.where(blk_idx > 0, s[:, :BLK], NEG)
                    sscr[h] = jnp.where(tri, s[:, BLK:], sp)
                    dpscr[h] = jnp.where(tri, dp[:, BLK:], dp[:, :BLK])
                    if diag:
                        dscr[h] = jnp.where(eye, sp, NEG)
                        ddscr[h] = dp[:, :BLK]
                else:
                    sscr[h] = jnp.where(tri, s, NEG)
                    dpscr[h] = dp
        for p in range(w // LANES):
            for hh in range(2):
                h = 2 * p + hh
                lse_b = jnp.broadcast_to(lse_ref[:, h:h + 1], (BLK, BLK))
                delta = jnp.broadcast_to(delta_ref[:, h:h + 1], (BLK, BLK))
                pr = jnp.exp(sscr[h] - lse_b)
                ds = pr * (dpscr[h] - delta)
                if nb > 1:
                    if diag:
                        prd = jnp.exp(dscr[h] - lse_b)
                        dsd = prd * (ddscr[h] - delta)
                    else:
                        prd = dsd = 0.0
                    pscr[h, :, :BLK] = jnp.where(tri, prd, pr).astype(bf16)
                    pscr[h, :, BLK:] = jnp.where(tri, pr, 0.0).astype(bf16)
                    dsscr[h, :, :BLK] = jnp.where(tri, dsd, ds).astype(bf16)
                    dsscr[h, :, BLK:] = jnp.where(tri, ds, 0.0).astype(bf16)
                else:
                    pscr[h] = pr.astype(bf16)
                    dsscr[h] = ds.astype(bf16)
                if has_sink:
                    dsk = -jnp.sum(jnp.where(lane == 0, jnp.exp(sink_ref[0, h] - lse_b) * delta, 0.0), keepdims=True)
                    dsink_ref[h:h + 1, :] += jnp.broadcast_to(dsk, (1, LANES))
        gk = [jnp.zeros((rows, LANES), f32), jnp.zeros((rows, LANES), f32)]
        gv = [jnp.zeros((rows, LANES), f32), jnp.zeros((rows, LANES), f32)]
        for p in range(w // LANES):
            sl = slice(LANES * p, LANES * (p + 1))
            qpair, dopair = q_ref[:, sl], do_ref[:, sl]
            kcat = _kv_cat(kc_ref, kp, p, gqa, cache)
            key = ("halves", p // 2 if gqa else p)
            if key not in cache:
                cache[key] = (_lane_half(kcat, 0), _lane_half(kcat, 1))
            dq_ref[:, sl] = _dot(dsscr[2 * p], cache[key][0]) + _dot(dsscr[2 * p + 1], cache[key][1])
            dk_pair = _dot_tn(dsscr[2 * p], _lane_half(qpair, 0)) + _dot_tn(dsscr[2 * p + 1], _lane_half(qpair, 1))
            dv_pair = _dot_tn(pscr[2 * p], _lane_half(dopair, 0)) + _dot_tn(pscr[2 * p + 1], _lane_half(dopair, 1))
            if gqa:
                gk[p // 2] = gk[p // 2] + dk_pair
                gv[p // 2] = gv[p // 2] + dv_pair
            elif nb > 1:
                dk_ref[:, sl] = dk_pair[BLK:] + ck[:, sl]
                dv_ref[:, sl] = dv_pair[BLK:] + cv[:, sl]
                ck[:, sl] = dk_pair[:BLK]
                cv[:, sl] = dv_pair[:BLK]
            else:
                dk_ref[:, sl] = dk_pair
                dv_ref[:, sl] = dv_pair
        if gqa:
            lor = lax.broadcasted_iota(jnp.int32, (rows, LANES), 1) < HD
            fold = lambda g: jnp.where(lor, g[0] + pltpu.roll(g[0], HD, 1), g[1] + pltpu.roll(g[1], HD, 1))
            dk_full, dv_full = fold(gk), fold(gv)
            dk_ref[...] = dk_full[BLK:] + ck[...]
            dv_ref[...] = dv_full[BLK:] + cv[...]
            ck[...] = dk_full[:BLK]
            cv[...] = dv_full[:BLK]

    assert max_dist in (BLK - 1, BLK)
    diag = nb > 1 and max_dist == BLK
    cur = lambda ww: pl.BlockSpec((None, BLK, ww), lambda a, i: (a, nb - 1 - i, 0))
    prev = lambda ww: pl.BlockSpec((None, BLK, ww), lambda a, i: (a, jnp.maximum(nb - 2 - i, 0), 0))
    in_specs = ([cur(w), cur(wk)] + ([prev(wk)] if nb > 1 else []) + [cur(wk)] + ([prev(wk)] if nb > 1 else [])
                + [cur(w), cur(LANES), cur(LANES)])
    args = [q, k] + ([k] if nb > 1 else []) + [v] + ([v] if nb > 1 else []) + [do, delta, lse]
    out_specs = [cur(w), cur(wk), cur(wk)]
    out_shape = [jax.ShapeDtypeStruct((n, l, w), f32), jax.ShapeDtypeStruct((n, l, wk), f32), jax.ShapeDtypeStruct((n, l, wk), f32)]
    if has_sink:
        in_specs = [pl.BlockSpec(memory_space=pltpu.SMEM)] + in_specs
        args = [sink] + args
        out_specs.append(pl.BlockSpec((8, LANES), lambda a, i: (0, 0)))
        out_shape.append(jax.ShapeDtypeStruct((8, LANES), f32))
    nh = w // HD
    scratch = [pltpu.VMEM((BLK, wk), f32), pltpu.VMEM((BLK, wk), f32)] if nb > 1 else []
    scratch += [pltpu.VMEM((nh, BLK, BLK), f32)] * 2 + [pltpu.VMEM((nh, BLK, 2 * BLK if nb > 1 else BLK), bf16)] * 2
    if diag:
        scratch += [pltpu.VMEM((nh, BLK, BLK), f32)] * 2
    return pl.pallas_call(
        body, name=name, grid=(n, nb), in_specs=in_specs, out_specs=out_specs, out_shape=out_shape,
        scratch_shapes=scratch, compiler_params=_cp(("arbitrary", "arbitrary")),
    )(*args)


def _split3(x):
    hi = x.astype(bf16)
    r = x - hi.astype(f32)
    mid = r.astype(bf16)
    return hi, mid, (r - mid.astype(f32)).astype(bf16)


def _heads_to_lanes(xc, e):
    return sum(_dot(t, e) for t in _split3(xc))


def _lanes_to_heads(x, g):
    return sum(_dot(t, g) for t in _split3(x))


HEAD_EXPAND = (np.arange(LANES)[:, None] == np.arange(BW)[None, :] // HD).astype(np.float32)
HEAD_SUM = HEAD_EXPAND.T.copy()


def _branch_weights(l1_ref, l4_ref, l16_ref, scr):
    l4v = _perm_load(l4_ref, scr, 4)
    l16v = _perm_load(l16_ref, scr, 16)
    l1v = l1_ref[...]
    m = jnp.maximum(jnp.maximum(l1v, l4v), l16v)
    e1, e4, e16 = jnp.exp(l1v - m), jnp.exp(l4v - m), jnp.exp(l16v - m)
    z = e1 + e4 + e16
    return e1 / z, e4 / z, e16 / z


def _mix_out(oa, o1, l1, o4, l4, o16, l16, g_mix_a, g_mix_b, w_out, x, mod, g_post):
    def body(oa_ref, o1_ref, l1_ref, o4_ref, l4_ref, o16_ref, l16_ref, ga_ref, gb_ref, w_ref, x_ref, mod_ref, gp_ref, e_ref,
             x1_ref, y_ref, mixed_ref, ob_ref, scr):
        w1, w4, w16 = _branch_weights(l1_ref, l4_ref, l16_ref, scr)
        e = e_ref[...]
        ob = (_heads_to_lanes(w1, e) * o1_ref[...].astype(f32) + _heads_to_lanes(w4, e) * _perm_load(o4_ref, scr, 4)
              + _heads_to_lanes(w16, e) * _perm_load(o16_ref, scr, 16))
        ob_ref[...] = ob
        oan, _ = _rms(oa_ref[...])
        obn, _ = _rms(ob)
        mixed = jnp.concatenate([oan * ga_ref[...], obn * gb_ref[...]], axis=1).astype(bf16)
        mixed_ref[...] = mixed
        y = _dot(mixed, w_ref[...])
        y_ref[...] = y
        yn, _ = _rms(y)
        x1_ref[...] = x_ref[...] + mod_ref[2:3, :] * (yn * gp_ref[...])

    nat = lambda w, dt: jax.ShapeDtypeStruct((BL, SEQ, w), dt)
    return pl.pallas_call(
        body, name="mix_out", grid=(BL, NJ),
        in_specs=[_tok(AQ), _tok(BW), _tok(LANES), _perm_spec(4, BW), _perm_spec(4, LANES), _perm_spec(16, BW),
                  _perm_spec(16, LANES), _full((1, AQ)), _full((1, BW)), _full((D, D)), _tok(D), MOD_SPEC, _full((1, D)),
                  _full((LANES, BW))],
        out_specs=[_tok(D), _tok(D), _tok(D), _tok(BW)],
        out_shape=[nat(D, f32), nat(D, f32), nat(D, bf16), nat(BW, f32)],
        scratch_shapes=[pltpu.VMEM((BW // LANES, TM, LANES), f32)],
        compiler_params=_cp(("arbitrary", "arbitrary")),
    )(oa, o1, l1, o4, l4, o16, l16, g_mix_a, g_mix_b, w_out, x, mod, g_post, jnp.asarray(HEAD_EXPAND, bf16))


def _mlp_up(x1, mod, g_pre, w_up):
    def body(x_ref, mod_ref, g_ref, w_ref, h_ref, u_ref, a_ref):
        xn, _ = _rms(x_ref[...])
        h = (xn * g_ref[...]) * (1.0 + mod_ref[4:5, :]) + mod_ref[3:4, :]
        hb = h.astype(bf16)
        h_ref[...] = hb
        for s in range(NCHIP):
            u = _dot(hb, w_ref[s])
            u_ref[:, D * s:D * (s + 1)] = u.astype(bf16)
            a_ref[:, D * s:D * (s + 1)] = jnp.square(jnp.maximum(u, 0.0)).astype(bf16)

    nat = lambda w: jax.ShapeDtypeStruct((BL, SEQ, w), bf16)
    return pl.pallas_call(
        body, name="mlp_up", grid=(BL, NJ),
        in_specs=[_tok(D), MOD_SPEC, _full((1, D)), _full((NCHIP, D, D))],
        out_specs=[_tok(D), _tok(DFF), _tok(DFF)], out_shape=[nat(D), nat(DFF), nat(DFF)],
        compiler_params=_cp(("arbitrary", "arbitrary")),
    )(x1, mod, g_pre, w_up)


def _mlp_down(a, w_down, x1, target, mod, g_post):
    def body(a_ref, w_ref, x_ref, t_ref, mod_ref, g_ref, gx_ref, dy_ref, accb_ref, accg_ref):
        _acc_init(accb_ref, accg_ref)
        y2 = _dot(a_ref[...], w_ref[...])
        yn, r = _rms(y2)
        g = g_ref[...]
        gt = mod_ref[5:6, :]
        n2 = yn * g
        err = x_ref[...] + gt * n2 - t_ref[...]
        gout = err * (1.0 / D)
        gx_ref[...] = gout
        dn2 = gout * gt
        dy_ref[...] = _rms_bwd(dn2 * g, yn, r).astype(bf16)
        accb_ref[0:1, :] += _colsum(gout * n2)
        accg_ref[0:1, :] += _colsum(dn2 * yn)
        accg_ref[1:2, :] += jnp.broadcast_to(jnp.sum(err * err, keepdims=True), (1, D))

    return pl.pallas_call(
        body, name="mlp_down", grid=(BL, NJ),
        in_specs=[_tok(DFF), _full((DFF, D)), _tok(D), _tok(D), MOD_SPEC, _full((1, D))],
        out_specs=[_tok(D), _tok(D), ACCB_SPEC, ACCG_SPEC],
        out_shape=[jax.ShapeDtypeStruct((BL, SEQ, D), f32), jax.ShapeDtypeStruct((BL, SEQ, D), bf16)] + ACC_SHAPES,
        compiler_params=_cp(("arbitrary", "arbitrary")),
    )(a, w_down, x1, target, mod, g_post)


def _mlp_bwd(dy2, u, w_down, w_up, x1, gx, mod, g_pre):
    def body(dy_ref, u_ref, wd_hbm, wu_hbm, x_ref, gx_ref, mod_ref, g_ref, du_ref, gx1_ref, accb_ref, accg_ref, wd, wu, sem):
        _acc_init(accb_ref, accg_ref)

        @pl.when((pl.program_id(0) == 0) & (pl.program_id(1) == 0))
        def _():
            c1 = pltpu.make_async_copy(wd_hbm, wd, sem.at[0])
            c2 = pltpu.make_async_copy(wu_hbm, wu, sem.at[1])
            c1.start()
            c2.start()
            c1.wait()
            c2.wait()

        dy = dy_ref[...]
        dh = jnp.zeros((TM, D), f32)
        for s in range(NCHIP):
            sl = slice(D * s, D * (s + 1))
            da = _dot_nt(dy, wd[sl, :])
            du = (da * (2.0 * jnp.maximum(u_ref[:, sl].astype(f32), 0.0))).astype(bf16)
            du_ref[:, sl] = du
            dh = dh + _dot_nt(du, wu[s])
        xn, r = _rms(x_ref[...])
        g = g_ref[...]
        n = xn * g
        dn = dh * (1.0 + mod_ref[4:5, :])
        gx1_ref[...] = gx_ref[...] + _rms_bwd(dn * g, xn, r)
        accb_ref[0:1, :] += _colsum(dh * n)
        accb_ref[1:2, :] += _colsum(dh)
        accg_ref[0:1, :] += _colsum(dn * xn)

    anyspec = pl.BlockSpec(memory_space=pl.ANY)
    return pl.pallas_call(
        body, name="mlp_bwd", grid=(BL, NJ),
        in_specs=[_tok(D), _tok(DFF), anyspec, anyspec, _tok(D), _tok(D), MOD_SPEC, _full((1, D))],
        out_specs=[_tok(DFF), _tok(D), ACCB_SPEC, ACCG_SPEC],
        out_shape=[jax.ShapeDtypeStruct((BL, SEQ, DFF), bf16), jax.ShapeDtypeStruct((BL, SEQ, D), f32)] + ACC_SHAPES,
        scratch_shapes=[pltpu.VMEM((DFF, D), bf16), pltpu.VMEM((NCHIP, D, D), bf16), pltpu.SemaphoreType.DMA((2,))],
        compiler_params=_cp(("arbitrary", "arbitrary")),
    )(dy2, u, w_down, w_up, x1, gx, mod, g_pre)


def _matmul_tn(a, b, *, tn, col_blocked, name):
    t, m = a.shape
    n = b.shape[1]
    tmm = min(m, 1024)
    tk = 2048 if tn <= 1024 else 1024
    nk = t // tk

    def body(a_ref, b_ref, o_ref):
        @pl.when(pl.program_id(2) == 0)
        def _():
            o_ref[...] = jnp.zeros_like(o_ref)

        o_ref[...] += _dot_tn(a_ref[...], b_ref[...])

    if col_blocked:
        out_spec = pl.BlockSpec((None, tmm, tn), lambda i, j, k: (j, i, 0))
        out_shape = jax.ShapeDtypeStruct((n // tn, m, tn), f32)
    else:
        out_spec = pl.BlockSpec((tmm, tn), lambda i, j, k: (i, j))
        out_shape = jax.ShapeDtypeStruct((m, n), f32)
    return pl.pallas_call(
        body, name=name, grid=(m // tmm, n // tn, nk),
        in_specs=[pl.BlockSpec((tk, tmm), lambda i, j, k: (k, i)), pl.BlockSpec((tk, tn), lambda i, j, k: (k, j))],
        out_specs=out_spec, out_shape=out_shape,
        compiler_params=_cp(("arbitrary", "arbitrary", "arbitrary")),
    )(a, b)


def _grad_w_in(h, dproj):
    t = h.shape[0]
    tk = 1024
    nk = t // tk
    sw = INW // NCHIP

    def body(a_ref, b_ref, o_ref, acc):
        k = pl.program_id(0)

        @pl.when(k == 0)
        def _():
            acc[...] = jnp.zeros_like(acc)

        acc[...] += _dot_tn(a_ref[...], b_ref[...])

        @pl.when(k == nk - 1)
        def _():
            for s in range(NCHIP):
                o_ref[s] = acc[:, sw * s:sw * (s + 1)]

    return pl.pallas_call(
        body, name="grad_w_in", grid=(nk,),
        in_specs=[pl.BlockSpec((tk, D), lambda k: (k, 0)), pl.BlockSpec((tk, INW), lambda k: (k, 0))],
        out_specs=pl.BlockSpec((NCHIP, D, sw), lambda k: (0, 0, 0)), out_shape=jax.ShapeDtypeStruct((NCHIP, D, sw), f32),
        scratch_shapes=[pltpu.VMEM((D, INW), f32)], compiler_params=_cp(("arbitrary",)),
    )(h, dproj)


def _attn_out_bwd(gx1, y, mod, g_post, w_out, oa, ob, g_mix_a, g_mix_b, l1, l4, l16):
    def body(gx_ref, y_ref, mod_ref, gp_ref, w_ref, oa_ref, ob_ref, ga_ref, gb_ref, l1_ref, l4_ref, l16_ref, e_ref, g_ref,
             dy_ref, doa_ref, do1_ref, do4_ref, do16_ref, da_ref, d1_ref, d4_ref, d16_ref, accb_ref, accg_ref, scr):
        _acc_init(accb_ref, accg_ref)
        w1, w4, w16 = _branch_weights(l1_ref, l4_ref, l16_ref, scr)
        e, hs = e_ref[...], g_ref[...]
        gx1v = gx_ref[...]
        yn, ry = _rms(y_ref[...])
        gp = gp_ref[...]
        gt = mod_ref[2:3, :]
        dn1 = gx1v * gt
        dy = _rms_bwd(dn1 * gp, yn, ry).astype(bf16)
        dy_ref[...] = dy
        dmixed = _dot_nt(dy, w_ref[...])
        dma, dmb = dmixed[:, :AQ], dmixed[:, AQ:]
        oa, ob = oa_ref[...], ob_ref[...]
        oan, ra = _rms(oa)
        obn, rb = _rms(ob)
        doa = _rms_bwd(dma * ga_ref[...], oan, ra)
        doa_ref[...] = doa.astype(bf16)
        da_ref[...] = _lanes_to_heads(doa * oa, hs)
        dob = _rms_bwd(dmb * gb_ref[...], obn, rb)
        dd = _lanes_to_heads(dob * ob, hs)
        do1_ref[...] = (_heads_to_lanes(w1, e) * dob).astype(bf16)
        d1_ref[...] = w1 * dd
        _perm_store(_heads_to_lanes(w4, e) * dob, scr, do4_ref, 4)
        _perm_store(w4 * dd, scr, d4_ref, 4)
        _perm_store(_heads_to_lanes(w16, e) * dob, scr, do16_ref, 16)
        _perm_store(w16 * dd, scr, d16_ref, 16)
        accb_ref[0:1, :] += _colsum(gx1v * (yn * gp))
        accg_ref[0:1, :] += _colsum(dn1 * yn)
        accg_ref[1:2, :] += jnp.concatenate([_colsum(dma * oan), _colsum(dmb * obn)], axis=1)

    nat = lambda w, dt: jax.ShapeDtypeStruct((BL, SEQ, w), dt)
    return pl.pallas_call(
        body, name="attn_out_bwd", grid=(BL, NJ),
        in_specs=[_tok(D), _tok(D), MOD_SPEC, _full((1, D)), _full((D, D)), _tok(AQ), _tok(BW), _full((1, AQ)), _full((1, BW)),
                  _tok(LANES), _perm_spec(4, LANES), _perm_spec(16, LANES), _full((LANES, BW)), _full((BW, LANES))],
        out_specs=[_tok(D), _tok(AQ), _tok(BW), _perm_spec(4, BW), _perm_spec(16, BW),
                   _tok(LANES), _tok(LANES), _perm_spec(4, LANES), _perm_spec(16, LANES), ACCB_SPEC, ACCG_SPEC],
        out_shape=[nat(D, bf16), nat(AQ, bf16), nat(BW, bf16), jax.ShapeDtypeStruct((BL, 4, SEQ // 4, BW), bf16),
                   jax.ShapeDtypeStruct((BL, 16, SEQ // 16, BW), bf16), nat(LANES, f32), nat(LANES, f32),
                   jax.ShapeDtypeStruct((BL, 4, SEQ // 4, LANES), f32), jax.ShapeDtypeStruct((BL, 16, SEQ // 16, LANES), f32)]
                  + ACC_SHAPES,
        scratch_shapes=[pltpu.VMEM((BW // LANES, TM, LANES), f32)],
        compiler_params=_cp(("arbitrary", "arbitrary")),
    )(gx1, y, mod, g_post, w_out, oa, ob, g_mix_a, g_mix_b, l1, l4, l16, jnp.asarray(HEAD_EXPAND, bf16),
      jnp.asarray(HEAD_SUM, bf16))


def _attn_in_bwd(dqa, dka, dva, d1, d4, d16, tc, ts1, ts2, w_in, x, gx1, mod, g_pre):
    def body(dqa_ref, dka_ref, dva_ref, dq1_ref, dk1_ref, dv1_ref, dq4_ref, dk4_ref, dv4_ref, dq16_ref, dk16_ref, dv16_ref,
             c_ref, s1_ref, s2_ref, w_ref, x_ref, gx_ref, mod_ref, g_ref, dproj_ref, dx_ref, accb_ref, accg_ref, scr):
        _acc_init(accb_ref, accg_ref)
        c, s1, s2 = c_ref[...], s1_ref[...], s2_ref[...]
        tot = lambda r1, r4, r16: r1[...] + _perm_load(r4, scr, 4) + _perm_load(r16, scr, 16)
        dqb = tot(dq1_ref, dq4_ref, dq16_ref)
        dkb = tot(dk1_ref, dk4_ref, dk16_ref)
        dvb = tot(dv1_ref, dv4_ref, dv16_ref)
        dproj = jnp.concatenate([
            _rope_t(dqa_ref[...], c, s1, s2) * 0.125, _rope_t(dka_ref[...], c, s1, s2), dva_ref[...],
            _rope_t(dqb, c, s1, s2) * 0.125, _rope_t(dkb, c, s1, s2), dvb], axis=1).astype(bf16)
        dproj_ref[...] = dproj
        dh = _dot_nt(dproj, w_ref[...])
        xn, r = _rms(x_ref[...])
        g = g_ref[...]
        dn = dh * (1.0 + mod_ref[1:2, :])
        dx_ref[...] = gx_ref[...] + _rms_bwd(dn * g, xn, r)
        accb_ref[0:1, :] += _colsum(dh * (xn * g))
        accb_ref[1:2, :] += _colsum(dh)
        accg_ref[0:1, :] += _colsum(dn * xn)

    return pl.pallas_call(
        body, name="attn_in_bwd", grid=(BL, NJ),
        in_specs=[_tok(AQ), _tok(AKV), _tok(AKV)] + [_tok(BW)] * 3 + [_perm_spec(4, BW)] * 3 + [_perm_spec(16, BW)] * 3
                 + [_tok(LANES)] * 3 + [_full((D, INW)), _tok(D), _tok(D), MOD_SPEC, _full((1, D))],
        out_specs=[_tok(INW), _tok(D), ACCB_SPEC, ACCG_SPEC],
        out_shape=[jax.ShapeDtypeStruct((BL, SEQ, INW), bf16), jax.ShapeDtypeStruct((BL, SEQ, D), f32)] + ACC_SHAPES,
        scratch_shapes=[pltpu.VMEM((BW // LANES, TM, LANES), f32)],
        compiler_params=_cp(("arbitrary", "arbitrary")),
    )(dqa, dka, dva, *d1, *d4, *d16, tc, ts1, ts2, w_in, x, gx1, mod, g_pre)


def _local_step(x, positions, mod, target, tok0, first_weight, later_weights, grad_ready, grad_reduce, g_attn_pre, g_attn_post,
                sink_a, g_mix_a, g_mix_b, g_mlp_pre, g_mlp_post):
    inv = np.float32(THETA) ** (-np.arange(0, ROT, 2, dtype=np.float32) / np.float32(ROT))
    lane = np.arange(LANES) % HD
    inv_lane = jnp.asarray(np.where(lane < ROT, inv[lane % (ROT // 2)], 0.0).astype(np.float32)[None, :])
    tabs = _rope_tables(positions.reshape(BL * SEQ, 1), inv_lane + tok0)
    w_in = first_weight(tuple(tabs))
    tc, ts1, ts2 = [t.reshape(BL, SEQ, LANES) for t in tabs]

    (h, qa, ka, va, q1, k1, v1, q4, k4, v4, q16, k16, v16) = _attn_in(x, mod, g_attn_pre, w_in, tc, ts1, ts2)
    seqs = lambda t: t.reshape(t.shape[0] * t.shape[1], t.shape[2], t.shape[3])
    q4, k4, v4, q16, k16, v16 = [seqs(t) for t in (q4, k4, v4, q16, k16, v16)]
    oa, la = _attn_fwd(qa, ka, va, sink_a, max_dist=BLK - 1, o_dtype=f32, name="attn_a_fwd")
    o1, l1 = _attn_fwd(q1, k1, v1, None, max_dist=BLK, o_dtype=bf16, name="attn_b1_fwd")
    o4, l4 = _attn_fwd(q4, k4, v4, None, max_dist=BLK, o_dtype=bf16, name="attn_b4_fwd")
    o16, l16 = _attn_fwd(q16, k16, v16, None, max_dist=BLK, o_dtype=bf16, name="attn_b16_fwd")
    b4 = lambda t: t.reshape(BL, 4, SEQ // 4, t.shape[-1])
    b16 = lambda t: t.reshape(BL, 16, SEQ // 16, t.shape[-1])
    w_out, mlp_weights, tok = later_weights((oa, o1, o4, o16))
    x1, y, mixed, ob = _mix_out(oa, o1, l1, b4(o4), b4(l4), b16(o16), b16(l16), g_mix_a, g_mix_b, w_out, x, mod + tok, g_attn_post)
    w_up, w_down = mlp_weights((x1,))
    h2, u, a = _mlp_up(x1, mod, g_mlp_pre, w_up)
    gx, dy2, accb_d, accg_d = _mlp_down(a, w_down, x1, target, mod, g_mlp_post)

    flat = lambda t: t.reshape(BL * SEQ, t.shape[-1])
    mod = mod + grad_ready("w_down", _matmul_tn(flat(a), flat(dy2), tn=D, col_blocked=False, name="grad_w_down"))
    du, gx1, accb_m, accg_m = _mlp_bwd(dy2, u, w_down, w_up, x1, gx, mod, g_mlp_pre)
    mod = mod + grad_reduce("w_down", (gx1,))
    mod = mod + grad_ready("w_up", _matmul_tn(flat(h2), flat(du), tn=D, col_blocked=True, name="grad_w_up"))

    dy, doa, do1, do4, do16, da, dl1, dl4, dl16, accb_o, accg_o = _attn_out_bwd(
        gx1, y, mod, g_attn_post, w_out, oa, ob, g_mix_a, g_mix_b, l1, b4(l4), b16(l16))
    tok = grad_reduce("w_up", (dy,))
    gw_out = _matmul_tn(flat(mixed), flat(dy), tn=D, col_blocked=False, name="grad_w_out")
    dqa, dka, dva, dsink = _attn_bwd(qa, ka, va, doa, da + tok, la, sink_a, max_dist=BLK - 1, name="attn_a_bwd")
    d1 = _attn_bwd(q1, k1, v1, do1, dl1, l1, None, max_dist=BLK, name="attn_b1_bwd")
    d4 = _attn_bwd(q4, k4, v4, seqs(do4), seqs(dl4), l4, None, max_dist=BLK, name="attn_b4_bwd")
    d16 = _attn_bwd(q16, k16, v16, seqs(do16), seqs(dl16), l16, None, max_dist=BLK, name="attn_b16_bwd")
    dproj, grad_x, accb_i, accg_i = _attn_in_bwd(dqa, dka, dva, d1, [b4(t) for t in d4], [b16(t) for t in d16],
                                                 tc, ts1, ts2, w_in, x, gx1, mod, g_attn_pre)
    gw_in = _grad_w_in(flat(h), flat(dproj))
    dsink = dsink + grad_ready("w_in_w_out", (gw_in, gw_out))

    return grad_x, (accb_i, accb_o, accb_m, accb_d, accg_i, accg_o, accg_m, accg_d, dsink)


ADAW = NMOD * D // NCHIP


def _pos():
    return lax.axis_index("x"), lax.axis_index("y"), lax.axis_index("c")


def _flip(v, bit):
    return 1 - v if bit else v


def _all_peers(x, y, c):
    return [(_flip(x, k >> 2 & 1), _flip(y, k >> 1 & 1), _flip(c, k & 1)) for k in range(1, NDEV)]


def _other_chips(x, y):
    return [(1 - x, y), (x, 1 - y), (1 - x, 1 - y)]


def _rcopy(src, dst, send, recv, k, dev):
    return pltpu.make_async_remote_copy(src_ref=src, dst_ref=dst, send_sem=send.at[k], recv_sem=recv.at[k],
                                        device_id=dev, device_id_type=MESH)


def _gather_small(src, buf, send, recv):
    x, y, c = _pos()
    me = 4 * x + 2 * y + c
    peers = _all_peers(x, y, c)
    sends = [_rcopy(src, buf.at[me], send, recv, k, p) for k, p in enumerate(peers)]
    for cp in sends:
        cp.start()
    for k, (px, py, pc) in enumerate(peers):
        _rcopy(src, buf.at[4 * px + 2 * py + pc], send, recv, k, (px, py, pc)).wait_recv()
    for cp in sends:
        cp.wait_send()
    return me


def _ada_fwd(c_in, w_ada, b_cols):
    def body(c_ref, w_ref, b_ref, mod_ref, cond_ref, cbuf, mbuf, s1, r1, s2, r2):
        x, y, c = _pos()
        chip = 2 * x + y
        me = _gather_small(c_ref, cbuf, s1, r1)
        cbuf[me] = c_ref[...]
        for i in range(NDEV):
            cond_ref[BL * i:BL * (i + 1), :] = cbuf[i]
        call = cond_ref[...]
        cond = call / (1.0 + jnp.exp(-call))
        cond_ref[...] = cond
        mbuf[chip] = _dot(cond.astype(bf16), w_ref[...].astype(bf16)) + b_ref[...]
        chips = _other_chips(x, y)
        sends = [_rcopy(mbuf.at[chip], mbuf.at[chip], s2, r2, j, (px, py, c)) for j, (px, py) in enumerate(chips)]
        for cp in sends:
            cp.start()
        for j, (px, py) in enumerate(chips):
            _rcopy(mbuf.at[chip], mbuf.at[2 * px + py], s2, r2, j, (px, py, c)).wait_recv()
        for cp in sends:
            cp.wait_send()
        row = lax.broadcasted_iota(jnp.int32, (BL * NDEV, ADAW), 0)
        for s in range(NCHIP):
            slab = mbuf[s]
            for j in range(BL):
                mod_ref[j:j + 1, ADAW * s:ADAW * (s + 1)] = jnp.sum(jnp.where(row == BL * me + j, slab, 0.0), axis=0, keepdims=True)

    vm = pl.BlockSpec(memory_space=pltpu.VMEM)
    return pl.pallas_call(
        body, name="ada_fwd", in_specs=[vm, vm, vm], out_specs=[vm, vm],
        out_shape=[jax.ShapeDtypeStruct((BL, NMOD * D), f32), jax.ShapeDtypeStruct((BL * NDEV, D), f32)],
        scratch_shapes=[pltpu.VMEM((NDEV, BL, D), f32), pltpu.VMEM((NCHIP, BL * NDEV, ADAW), f32),
                        pltpu.SemaphoreType.DMA((NDEV - 1,)), pltpu.SemaphoreType.DMA((NDEV - 1,)),
                        pltpu.SemaphoreType.DMA((NCHIP - 1,)), pltpu.SemaphoreType.DMA((NCHIP - 1,))],
        compiler_params=pltpu.CompilerParams(vmem_limit_bytes=VMEM_LIMIT),
    )(c_in, w_ada, b_cols)


def _small_allreduce(accs, cond_all):
    def body(bi, bo, bm, bd, gi, go, gm, gd, dsink, cond_ref, gw_ref, gb_ref, small_ref, pay, pbuf, dall, s1, r1):
        x, y, c = _pos()
        chip = 2 * x + y
        pay[...] = jnp.zeros_like(pay)
        for b in range(BL):
            for k, (ref, r) in enumerate(((bi, 1), (bi, 0), (bo, 0), (bm, 1), (bm, 0), (bd, 0))):
                pay[b:b + 1, D * k:D * (k + 1)] = ref[b, r:r + 1, :]
        for off, ref, r in ((OFF_G_ATTN_PRE, gi, 0), (OFF_G_ATTN_POST, go, 0), (OFF_G_MIX_A, go, 1), (OFF_G_MLP_PRE, gm, 0),
                            (OFF_G_MLP_POST, gd, 0)):
            pay[BL:BL + 1, off:off + D] = ref[r:r + 1, :]
        eye = lax.broadcasted_iota(jnp.int32, (8, LANES), 0) == lax.broadcasted_iota(jnp.int32, (8, LANES), 1)
        pay[BL:BL + 1, OFF_SINK:OFF_SINK + LANES] = jnp.sum(jnp.where(eye, dsink[...], 0.0), axis=0, keepdims=True)
        pay[BL:BL + 1, OFF_LOSS:OFF_LOSS + LANES] = gd[1:2, 0:LANES]
        me = _gather_small(pay, pbuf, s1, r1)
        pbuf[me] = pay[...]
        small = pbuf[0, BL:BL + 1, :]
        for i in range(1, NDEV):
            small = small + pbuf[i, BL:BL + 1, :]
        small_ref[...] = small
        for i in range(NDEV):
            dall[BL * i:BL * (i + 1), :] = pbuf[i, 0:BL, :]
        gb_ref[...] = jnp.sum(dall[...], axis=0, keepdims=True)
        cols = jnp.zeros((BL * NDEV, ADAW), f32)
        for s in range(NCHIP):
            cols = cols + jnp.where(chip == s, dall[:, ADAW * s:ADAW * (s + 1)], 0.0)
        gw_ref[...] = lax.dot_general(cond_ref[...], cols, (((0,), (0,)), ((), ())), preferred_element_type=f32,
                                      precision=lax.Precision.HIGHEST)

    vm = pl.BlockSpec(memory_space=pltpu.VMEM)
    return pl.pallas_call(
        body, name="small_allreduce", in_specs=[vm] * 10, out_specs=[vm] * 3,
        out_shape=[jax.ShapeDtypeStruct((D, ADAW), f32), jax.ShapeDtypeStruct((1, PAYW), f32), jax.ShapeDtypeStruct((1, PAYW), f32)],
        scratch_shapes=[pltpu.VMEM((4, PAYW), f32), pltpu.VMEM((NDEV, 4, PAYW), f32), pltpu.VMEM((BL * NDEV, PAYW), f32),
                        pltpu.SemaphoreType.DMA((NDEV - 1,)), pltpu.SemaphoreType.DMA((NDEV - 1,))],
        compiler_params=pltpu.CompilerParams(vmem_limit_bytes=VMEM_LIMIT),
    )(*accs, cond_all)


def _half(ref, c):
    r2 = ref.shape[0] // 2
    return ref.at[pl.ds(pl.multiple_of(c * r2, 16), r2), :]


HBM_SPEC = pl.BlockSpec(memory_space=pltpu.HBM)
SEM_SPEC = pl.BlockSpec(memory_space=pltpu.SEMAPHORE)
EFFECT = pltpu.SideEffectType.DATAFLOW_SIDE_EFFECTING
NLINK = NCHIP - 1


def _in_hbm(a):
    return pltpu.with_memory_space_constraint(a, pltpu.HBM)


NSEM = 4


def _split_start(name, srcs, land_shapes, builds, after=(), lands=None):
    n = len(srcs)
    after = [a.reshape(1, 1) if a.ndim == 0 else a for a in after]
    na = len(after)

    def body(*refs):
        src, land, token = refs[:n], refs[n:2 * n], refs[-1]
        send, recv = refs[2 * n + na:3 * n + na], refs[3 * n + na:4 * n + na]
        for t in range(n):
            for out_cp, _ in builds[t](src[t], land[t], send[t], recv[t]):
                out_cp.start()
        token[...] = jnp.zeros_like(token)

    if lands is None:
        lands = [lax.empty(s.shape, s.dtype) for s in land_shapes]
    lands = [_in_hbm(a) for a in lands]
    sems = [pltpu.SemaphoreType.DMA((NSEM,))] * (2 * n)
    thru = [pltpu.HBM(a.shape, a.dtype) for a in list(srcs) + lands]
    res = pl.pallas_call(
        body, name=name, out_shape=sems + thru + [jax.ShapeDtypeStruct((8, LANES), f32)],
        in_specs=[HBM_SPEC] * (2 * n) + [pl.BlockSpec(memory_space=pl.ANY)] * na,
        out_specs=[SEM_SPEC] * (2 * n) + [HBM_SPEC] * (2 * n) + [pl.BlockSpec(memory_space=pltpu.VMEM)],
        input_output_aliases={i: 2 * n + i for i in range(2 * n)},
        compiler_params=pltpu.CompilerParams(has_side_effects=EFFECT),
    )(*[_in_hbm(a) for a in srcs], *lands, *after)
    flight = [(res[2 * n + t], res[3 * n + t], res[t], res[n + t]) for t in range(n)]
    return flight, res[-1][0, 0]


def _split_wait(name, flight, builds, after):
    m = len(flight)
    after = [a.reshape(1, 1) if a.ndim == 0 else a for a in after]
    na = len(after)

    def body(*refs):
        src, land, send, recv = refs[:m], refs[m:2 * m], refs[2 * m:3 * m], refs[3 * m:4 * m]
        for t in range(m):
            for out_cp, in_cp in builds[t](src[t], land[t], send[t], recv[t]):
                out_cp.wait_send()
                in_cp.wait_recv()

    ops = [f[0] for f in flight] + [f[1] for f in flight] + [f[2] for f in flight] + [f[3] for f in flight]
    res = pl.pallas_call(
        body, name=name, out_shape=[pltpu.HBM(a.shape, a.dtype) for a in ops[:2 * m]],
        in_specs=[HBM_SPEC] * (2 * m) + [SEM_SPEC] * (2 * m) + [pl.BlockSpec(memory_space=pl.ANY)] * na,
        out_specs=[HBM_SPEC] * (2 * m), input_output_aliases={i: i for i in range(2 * m)},
        compiler_params=pltpu.CompilerParams(has_side_effects=EFFECT),
    )(*ops, *after)
    return res[:m], res[m:2 * m]


def _weight_copies(src, land, send, recv):
    x, y, c = _pos()
    chip = 2 * x + y
    return [(_rcopy(_half(src, c), _half(land.at[chip], c), send, recv, j, (px, py, c)),
             _rcopy(_half(src, c), _half(land.at[2 * px + py], c), send, recv, j, (px, py, c)))
            for j, (px, py) in enumerate(_other_chips(x, y))]


def _grad_copies(src, land, send, recv):
    x, y, c = _pos()
    return [(_rcopy(src.at[2 * px + py], land.at[j], send, recv, j, (px, py, c)),
             _rcopy(src.at[2 * px + py], land.at[j], send, recv, j, (px, py, c)))
            for j, (px, py) in enumerate(_other_chips(x, y))]


def _pair_grad_copies(src, land, send, recv):
    x, y, c = _pos()
    r2 = src.shape[1] // 2
    cp = _rcopy(src.at[:, pl.ds(pl.multiple_of((1 - c) * r2, 8), r2), :], land, send, recv, 0, (x, y, 1 - c))
    return [(cp, cp)]


def _pair_weight_copies(src, land, send, recv):
    x, y, c = _pos()
    sib = (x, y, 1 - c)
    cps = []
    for j, (px, py) in enumerate(_other_chips(x, y)):
        mine, theirs = _half(land.at[2 * px + py], c), _half(land.at[2 * px + py], 1 - c)
        cps.append((_rcopy(mine, mine, send, recv, j, sib), _rcopy(theirs, theirs, send, recv, j, sib)))
    own = _rcopy(src, land.at[2 * x + y], send, recv, NLINK, sib)
    return cps + [(own, own)]


RS_ROWS = 128


def _pair_add(g, landed, c_arr, name):
    _, r2, cw = landed.shape
    nr = r2 // RS_ROWS

    def body(c_ref, g_ref, p_ref, o_ref):
        o_ref[...] = (g_ref[...] + p_ref[...]).astype(bf16)

    gs = pltpu.PrefetchScalarGridSpec(
        num_scalar_prefetch=1, grid=(NCHIP, nr),
        in_specs=[pl.BlockSpec((None, RS_ROWS, cw), lambda s, j, c: (s, c[0] * nr + j, 0)),
                  pl.BlockSpec((None, RS_ROWS, cw), lambda s, j, c: (s, j, 0))],
        out_specs=pl.BlockSpec((None, RS_ROWS, cw), lambda s, j, c: (s, j, 0)))
    return pl.pallas_call(body, name=name, grid_spec=gs, out_shape=jax.ShapeDtypeStruct((NCHIP, r2, cw), bf16),
                          compiler_params=_cp(("arbitrary", "arbitrary")))(c_arr, g, landed)


def _chip_add(half, landed, pos_arr, name):
    _, r2, cw = half.shape
    nr = r2 // RS_ROWS

    def body(s_ref, h_ref, q_ref, o_ref):
        acc = h_ref[...].astype(f32)
        for j in range(NCHIP - 1):
            acc = acc + q_ref[j].astype(f32)
        o_ref[...] = acc

    gs = pltpu.PrefetchScalarGridSpec(
        num_scalar_prefetch=1, grid=(nr,),
        in_specs=[pl.BlockSpec((None, RS_ROWS, cw), lambda j, s: (s[0], j, 0)),
                  pl.BlockSpec((NCHIP - 1, RS_ROWS, cw), lambda j, s: (0, j, 0))],
        out_specs=pl.BlockSpec((RS_ROWS, cw), lambda j, s: (s[1] * nr + j, 0)))
    return pl.pallas_call(body, name=name, grid_spec=gs, out_shape=jax.ShapeDtypeStruct((2 * r2, cw), f32),
                          compiler_params=_cp(("arbitrary",)))(pos_arr, half, landed)


def _pair_gather_copies(src, land, send, recv):
    x, y, c = _pos()
    sib = (x, y, 1 - c)
    return [(_rcopy(_half(land, c), _half(land, c), send, recv, 0, sib),
             _rcopy(_half(land, 1 - c), _half(land, 1 - c), send, recv, 0, sib))]


def _adamw_math(w, g, m, v):
    m = B1 * m + (1.0 - B1) * g
    v = B2 * v + (1.0 - B2) * jnp.square(g)
    m_hat = m / (1.0 - B1 ** STEP)
    v_hat = v / (1.0 - B2 ** STEP)
    return -LR * (m_hat / (jnp.sqrt(v_hat) + AEPS) + WD * w), m, v


ADAM_ROWS = 256


def _adamw(w, g, m, v, name):
    r, cw = w.shape

    def body(w_ref, g_ref, m_ref, v_ref, d_ref, mo_ref, vo_ref):
        d_ref[...], mo_ref[...], vo_ref[...] = _adamw_math(w_ref[...], g_ref[...], m_ref[...], v_ref[...])

    spec = pl.BlockSpec((ADAM_ROWS, cw), lambda i: (i, 0))
    return pl.pallas_call(body, name=name, grid=(r // ADAM_ROWS,), in_specs=[spec] * 4, out_specs=[spec] * 3,
                          out_shape=[jax.ShapeDtypeStruct((r, cw), f32)] * 3, compiler_params=_cp(("arbitrary",)))(w, g, m, v)


SMALL = (("b_ada", None, PAYW), ("g_attn_pre", OFF_G_ATTN_PRE, D), ("g_attn_post", OFF_G_ATTN_POST, D), ("sink_a", OFF_SINK, 8),
         ("g_mix_a", OFF_G_MIX_A, AQ), ("g_mix_b", OFF_G_MIX_B, BW), ("g_mlp_pre", OFF_G_MLP_PRE, D), ("g_mlp_post", OFF_G_MLP_POST, D))


def _adamw_small(small, gb, params):
    n = len(SMALL)

    def body(*refs):
        small_ref, gb_ref = refs[:2]
        wmv = refs[2:2 + 3 * n]
        loss_ref = refs[2 + 3 * n]
        outs = refs[3 + 3 * n:]
        loss_ref[...] = small_ref[:, OFF_LOSS:OFF_LOSS + 1] * (0.5 / D)
        for i, (_, off, width) in enumerate(SMALL):
            g = gb_ref[...] if off is None else small_ref[:, off:off + width]
            w_ref, m_ref, v_ref = wmv[3 * i:3 * i + 3]
            outs[4 * i][...] = g
            outs[4 * i + 1][...], outs[4 * i + 2][...], outs[4 * i + 3][...] = _adamw_math(w_ref[...], g, m_ref[...], v_ref[...])

    vm = pl.BlockSpec(memory_space=pltpu.VMEM)
    out_shape = [jax.ShapeDtypeStruct((1, 1), f32)]
    for _, _, width in SMALL:
        out_shape += [jax.ShapeDtypeStruct((1, width), f32)] * 4
    flat = [a for wmv in params for a in wmv]
    res = pl.pallas_call(body, name="adamw_small", in_specs=[vm] * (2 + 3 * n), out_specs=[vm] * len(out_shape),
                         out_shape=out_shape)(small, gb, *flat)
    return res[0], {name: res[1 + 4 * i:5 + 4 * i] for i, (name, _, _) in enumerate(SMALL)}


def kernel(x, c, positions, w_ada, b_ada, g_attn_pre, g_attn_post, w_in, sink_a, g_mix_a, g_mix_b, w_out, g_mlp_pre, g_mlp_post, w_up, w_down, loss_target, m_w_ada, m_b_ada, m_g_attn_pre, m_g_attn_post, m_w_in, m_sink_a, m_g_mix_a, m_g_mix_b, m_w_out, m_g_mlp_pre, m_g_mlp_post, m_w_up, m_w_down, v_w_ada, v_b_ada, v_g_attn_pre, v_g_attn_post, v_w_in, v_sink_a, v_g_mix_a, v_g_mix_b, v_w_out, v_g_mlp_pre, v_g_mlp_post, v_w_up, v_w_down):
    given = dict(w_ada=w_ada, b_ada=b_ada, g_attn_pre=g_attn_pre, g_attn_post=g_attn_post, w_in=w_in, sink_a=sink_a, g_mix_a=g_mix_a,
                 g_mix_b=g_mix_b, w_out=w_out, g_mlp_pre=g_mlp_pre, g_mlp_post=g_mlp_post, w_up=w_up, w_down=w_down)
    moms = dict(w_ada=(m_w_ada, v_w_ada), b_ada=(m_b_ada, v_b_ada), g_attn_pre=(m_g_attn_pre, v_g_attn_pre),
                g_attn_post=(m_g_attn_post, v_g_attn_post), w_in=(m_w_in, v_w_in), sink_a=(m_sink_a, v_sink_a),
                g_mix_a=(m_g_mix_a, v_g_mix_a), g_mix_b=(m_g_mix_b, v_g_mix_b), w_out=(m_w_out, v_w_out),
                g_mlp_pre=(m_g_mlp_pre, v_g_mlp_pre), g_mlp_post=(m_g_mlp_post, v_g_mlp_post), w_up=(m_w_up, v_w_up),
                w_down=(m_w_down, v_w_down))
    order = ["w_ada", "b_ada", "g_attn_pre", "g_attn_post", "w_in", "sink_a", "g_mix_a", "g_mix_b", "w_out", "g_mlp_pre",
             "g_mlp_post", "w_up", "w_down"]
    xi, yi, ci = _pos()
    chip = 2 * xi + yi

    c_arr = jnp.reshape(ci, (1,)).astype(jnp.int32)
    pos_arr = jnp.stack([chip, ci]).astype(jnp.int32)
    big = ("w_in", "w_out", "w_up", "w_down")

    shards = [given[n][0].astype(bf16) for n in big]
    gathered = [jax.ShapeDtypeStruct((NCHIP,) + s.shape, bf16) for s in shards]
    flight_in, tok = _split_start("weights_start_first", shards[:1], gathered[:1], [_weight_copies])
    b_cols = lax.dynamic_slice(b_ada, (0, chip * ADAW), (1, ADAW))
    mod, cond_all = _ada_fwd(c + tok, w_ada[0], b_cols)
    flight_rest, tok = _split_start("weights_start_rest", shards[1:], gathered[1:], [_weight_copies] * 3, after=(mod,))
    mod = mod.reshape(BL, NMOD, D) + tok
    srcs, lands = _split_wait("weights_wait_first", flight_in, [_weight_copies], (mod,))
    cross_in, tok0 = _split_start("weights_pair_start_first", srcs, None, [_pair_weight_copies], lands=lands)

    def first_weight(after):
        _, (win_g,) = _split_wait("weights_pair_wait_first", cross_in, [_pair_weight_copies], after)
        return win_g.transpose(1, 0, 2).reshape(D, INW)

    def later_weights(after):
        srcs, lands = _split_wait("weights_wait_rest", flight_rest, [_weight_copies] * 3, after)
        fl, tk = _split_start("weights_pair_start_rest", srcs, None, [_pair_weight_copies] * 3, lands=lands)
        _, (wout_g,) = _split_wait("weights_pair_wait_out", fl[:1], [_pair_weight_copies], ())

        def mlp_weights(after):
            _, (wup_g, wdn_g) = _split_wait("weights_pair_wait_mlp", fl[1:], [_pair_weight_copies] * 2, after)
            return wup_g, wdn_g.reshape(DFF, D)

        return wout_g.reshape(D, D), mlp_weights, tk

    crossing, pending = {}, {}

    def grad_ready(group, g):
        if group == "w_down":
            names, slabs = ("w_down",), [g.reshape(NCHIP, DFF // NCHIP, D)]
        elif group == "w_up":
            names, slabs = ("w_up",), [g]
        else:
            names = ("w_in", "w_out")
            slabs = [g[0], g[1].reshape(NCHIP, D // NCHIP, D)]
        fl, tk = _split_start("grad_pair_start_" + group, slabs,
                              [jax.ShapeDtypeStruct((NCHIP, s.shape[1] // 2, s.shape[2]), f32) for s in slabs],
                              [_pair_grad_copies] * len(names))
        crossing[group] = (names, fl)
        return tk

    def grad_reduce(group, after):
        names, fl = crossing[group]
        slabs, landed = _split_wait("grad_pair_wait_" + group, fl, [_pair_grad_copies] * len(names), after)
        halves = [_pair_add(s, p, c_arr, "grad_pair_sum_" + n) for s, p, n in zip(slabs, landed, names)]
        fl, tk = _split_start("grad_start_" + group, halves,
                              [jax.ShapeDtypeStruct((NLINK,) + h.shape[1:], bf16) for h in halves], [_grad_copies] * len(names))
        pending[group] = (names, fl)
        return tk

    grad_x, accs = _local_step(x, positions, mod, loss_target, tok0, first_weight, later_weights, grad_ready, grad_reduce,
                               g_attn_pre, g_attn_post, sink_a, g_mix_a, g_mix_b, g_mlp_pre, g_mlp_post)

    grads, out = {}, {}

    def update(n):
        d, m2, v2 = _adamw(given[n][0], grads[n], moms[n][0][0], moms[n][1][0], "adamw_" + n)
        out[n] = (grads[n][None], d[None], m2[None], v2[None])
        return v2

    def finish(groups, after):
        names = sum((pending[g][0] for g in groups), ())
        fl = sum((pending[g][1] for g in groups), [])
        halves, landed = _split_wait("grad_wait_" + groups[0], fl, [_grad_copies] * len(names), after)
        flights = []
        for h, q, n in zip(halves, landed, names):
            full = _chip_add(h, q, pos_arr, "grad_chip_sum_" + n)
            flights.append(_split_start("grad_gather_start_" + n, [jnp.zeros((8, LANES), f32)], None, [_pair_gather_copies],
                                        lands=[full])[0])
        last = None
        for n, fl1 in zip(names, flights):
            after = (flights[-1][0][0],) if last is None and fl1 is not flights[-1] else () if last is None else (last,)
            _, (grads[n],) = _split_wait("grad_gather_wait_" + n, fl1, [_pair_gather_copies], after)
            last = update(n)
        return last

    grads["w_ada"], gb, small = _small_allreduce(accs, cond_all)
    small = small + grad_reduce("w_in_w_out", (small,))
    last = finish(("w_down", "w_up"), (small,))
    finish(("w_in_w_out",), (last, update("w_ada")))
    loss, res = _adamw_small(small, gb, [(given[n], moms[n][0], moms[n][1]) for n, _, _ in SMALL])
    for n, _, _ in SMALL:
        out[n] = tuple(res[n])
    return (loss.reshape(()), grad_x, *[out[n][0] for n in order], *[out[n][1] for n in order],
            *[out[n][2] for n in order], *[out[n][3] for n in order])
```

```python
import functools

import numpy as np
import jax
import jax.numpy as jnp
from jax import lax
from jax.experimental import pallas as pl
from jax.experimental.pallas import tpu as pltpu

f32 = jnp.float32
bf16 = jnp.bfloat16
MESH = pl.DeviceIdType.MESH

D = 1024
SEQ = 2048
BL = 2
HD = 64
AQ = 512
AKV = 128
BW = 512
INW = 2304
DFF = 4096
NMOD = 6
ROT = 16
THETA = 500000.0
EPS = 1e-6
NEG = -1e30
BLK = 128
TM = 512
NJ = SEQ // TM
LANES = 128
NCHIP = 4
NDEV = 8
VMEM_LIMIT = 56 << 20

LR, B1, B2, AEPS, WD, STEP = 0.001, 0.9, 0.999, 1e-08, 0.01, 10

OFF_G_ATTN_PRE, OFF_G_ATTN_POST, OFF_G_MIX_A, OFF_G_MIX_B = 0, 1024, 2048, 2560
OFF_G_MLP_PRE, OFF_G_MLP_POST, OFF_SINK, OFF_LOSS = 3072, 4096, 5120, 5248
PAYW = NMOD * D


def _cp(sem=None):
    return pltpu.CompilerParams(dimension_semantics=sem, vmem_limit_bytes=VMEM_LIMIT)


def _dot(a, b):
    return jnp.dot(a, b, preferred_element_type=f32)


def _dot_nt(a, b):
    return lax.dot_general(a, b, (((1,), (1,)), ((), ())), preferred_element_type=f32)


def _dot_tn(a, b):
    return lax.dot_general(a, b, (((0,), (0,)), ((), ())), preferred_element_type=f32)


def _rms(x):
    r = lax.rsqrt(jnp.mean(x * x, axis=-1, keepdims=True) + EPS)
    return x * r, r


def _rms_bwd(dy, y, r):
    return r * (dy - y * jnp.mean(dy * y, axis=-1, keepdims=True))


def _colsum(v):
    return jnp.sum(v, axis=0, keepdims=True)


def _rope(p, c, s1, s2):
    outs = []
    for c0 in range(0, p.shape[1], LANES):
        pc = p[:, c0:c0 + LANES]
        outs.append(pc * c + pltpu.roll(pc, LANES - ROT // 2, 1) * s1 + pltpu.roll(pc, ROT // 2, 1) * s2)
    return outs[0] if len(outs) == 1 else jnp.concatenate(outs, axis=1)


def _rope_t(g, c, s1, s2):
    outs = []
    for c0 in range(0, g.shape[1], LANES):
        gc = g[:, c0:c0 + LANES]
        outs.append(gc * c + pltpu.roll(gc * s1, ROT // 2, 1) + pltpu.roll(gc * s2, LANES - ROT // 2, 1))
    return outs[0] if len(outs) == 1 else jnp.concatenate(outs, axis=1)


def _perm_store(val, scr, out_ref, d):
    nc = val.shape[1] // LANES
    for c in range(nc):
        scr[c] = val[:, LANES * c:LANES * (c + 1)]
    for c in range(nc):
        for r in range(d):
            out_ref[r, :, LANES * c:LANES * (c + 1)] = scr[c, pl.ds(r, TM // d, stride=d), :].astype(out_ref.dtype)


def _perm_load(in_ref, scr, d):
    nc = in_ref.shape[-1] // LANES
    for c in range(nc):
        for r in range(d):
            scr[c, pl.ds(r, TM // d, stride=d), :] = in_ref[r, :, LANES * c:LANES * (c + 1)].astype(f32)
    return jnp.concatenate([scr[c] for c in range(nc)], axis=1)


def _tok(w, dtype=None):
    return pl.BlockSpec((None, TM, w), lambda b, j: (b, j, 0))


def _perm_spec(d, w):
    return pl.BlockSpec((None, d, TM // d, w), lambda b, j: (b, 0, j, 0))


def _full(shape):
    n = len(shape)
    return pl.BlockSpec(shape, lambda b, j: (0,) * n)


MOD_SPEC = pl.BlockSpec((None, NMOD, D), lambda b, j: (b, 0, 0))
ACCB_SPEC = pl.BlockSpec((None, 8, D), lambda b, j: (b, 0, 0))
ACCG_SPEC = pl.BlockSpec((8, D), lambda b, j: (0, 0))
ACC_SHAPES = [jax.ShapeDtypeStruct((BL, 8, D), f32), jax.ShapeDtypeStruct((8, D), f32)]


def _acc_init(accb_ref, accg_ref):
    b, j = pl.program_id(0), pl.program_id(1)

    @pl.when(j == 0)
    def _():
        accb_ref[...] = jnp.zeros_like(accb_ref)

    @pl.when((b == 0) & (j == 0))
    def _():
        accg_ref[...] = jnp.zeros_like(accg_ref)


def _rope_tables(pos_col, inv_lane):
    def body(p_ref, inv_ref, c_ref, s1_ref, s2_ref):
        ang = p_ref[...].astype(f32) * inv_ref[...]
        j = lax.broadcasted_iota(jnp.int32, (TM, LANES), 1) % HD
        cs, sn = jnp.cos(ang), jnp.sin(ang)
        c_ref[...] = jnp.where(j < ROT, cs, 1.0)
        s1_ref[...] = jnp.where(j < ROT // 2, -sn, 0.0)
        s2_ref[...] = jnp.where((j >= ROT // 2) & (j < ROT), sn, 0.0)

    n = BL * SEQ // TM
    return pl.pallas_call(
        body, name="rope_tables", grid=(n,),
        in_specs=[pl.BlockSpec((TM, 1), lambda i: (i, 0)), pl.BlockSpec((1, LANES), lambda i: (0, 0))],
        out_specs=[pl.BlockSpec((TM, LANES), lambda i: (i, 0))] * 3,
        out_shape=[jax.ShapeDtypeStruct((BL * SEQ, LANES), f32)] * 3,
    )(pos_col, inv_lane)


def _attn_in(x, mod, g_pre, w_in, tc, ts1, ts2):
    def body(x_ref, mod_ref, g_ref, w_ref, c_ref, s1_ref, s2_ref,
             h_ref, qa_ref, ka_ref, va_ref, q1_ref, k1_ref, v1_ref, q4_ref, k4_ref, v4_ref, q16_ref, k16_ref, v16_ref,
             scr):
        xn, _ = _rms(x_ref[...])
        h = (xn * g_ref[...]) * (1.0 + mod_ref[1:2, :]) + mod_ref[0:1, :]
        hb = h.astype(bf16)
        h_ref[...] = hb
        proj = _dot(hb, w_ref[...])
        c, s1, s2 = c_ref[...], s1_ref[...], s2_ref[...]
        o1, o2, o3, o4, o5 = AQ, AQ + AKV, AQ + 2 * AKV, AQ + 2 * AKV + BW, AQ + 2 * AKV + 2 * BW
        qa_ref[...] = (_rope(proj[:, :o1], c, s1, s2) * 0.125).astype(bf16)
        ka_ref[...] = _rope(proj[:, o1:o2], c, s1, s2).astype(bf16)
        va_ref[...] = proj[:, o2:o3].astype(bf16)
        qb = _rope(proj[:, o3:o4], c, s1, s2) * 0.125
        kb = _rope(proj[:, o4:o5], c, s1, s2)
        vb = proj[:, o5:]
        for val, r1, r4, r16 in ((qb, q1_ref, q4_ref, q16_ref), (kb, k1_ref, k4_ref, k16_ref), (vb, v1_ref, v4_ref, v16_ref)):
            r1[...] = val.astype(bf16)
            _perm_store(val, scr, r4, 4)
            _perm_store(val, scr, r16, 16)

    nat = lambda w: jax.ShapeDtypeStruct((BL, SEQ, w), bf16)
    p4 = jax.ShapeDtypeStruct((BL, 4, SEQ // 4, BW), bf16)
    p16 = jax.ShapeDtypeStruct((BL, 16, SEQ // 16, BW), bf16)
    return pl.pallas_call(
        body, name="attn_in", grid=(BL, NJ),
        in_specs=[_tok(D), MOD_SPEC, _full((1, D)), _full((D, INW)), _tok(LANES), _tok(LANES), _tok(LANES)],
        out_specs=[_tok(D), _tok(AQ), _tok(AKV), _tok(AKV)] + [_tok(BW)] * 3 + [_perm_spec(4, BW)] * 3 + [_perm_spec(16, BW)] * 3,
        out_shape=[nat(D), nat(AQ), nat(AKV), nat(AKV)] + [nat(BW)] * 3 + [p4] * 3 + [p16] * 3,
        scratch_shapes=[pltpu.VMEM((BW // LANES, TM, LANES), f32)],
        compiler_params=_cp(("arbitrary", "arbitrary")),
    )(x, mod, g_pre, w_in, tc, ts1, ts2)


def _kv_cat(cur_ref, prev_ref, p, gqa, cache):
    def one(ref):
        if not gqa:
            return ref[:, LANES * p:LANES * (p + 1)]
        k = ref[...]
        kr = pltpu.roll(k, HD, 1)
        lo = lax.broadcasted_iota(jnp.int32, k.shape, 1) < HD
        return jnp.where(lo, k, kr) if p < 2 else jnp.where(lo, kr, k)

    key = (id(cur_ref), p // 2 if gqa else p)
    if key not in cache:
        cache[key] = one(cur_ref) if prev_ref is None else jnp.concatenate([one(prev_ref), one(cur_ref)], axis=0)
    return cache[key]


def _lane_half(a, hh):
    lo = lax.broadcasted_iota(jnp.int32, a.shape, 1) < HD
    return jnp.where(lo, a, jnp.zeros_like(a)) if hh == 0 else jnp.where(lo, jnp.zeros_like(a), a)


def _attn_fwd(q, k, v, sink, *, max_dist, o_dtype, name):
    n, l, w = q.shape
    wk = k.shape[-1]
    nb = l // BLK
    gqa = wk != w
    has_sink = sink is not None

    def body(*refs):
        if has_sink:
            sink_ref, refs = refs[0], refs[1:]
        if nb > 1:
            q_ref, kc_ref, kp_ref, vc_ref, vp_ref, o_ref, lse_ref, sscr, pscr, dscr = refs
        else:
            q_ref, kc_ref, vc_ref, o_ref, lse_ref, sscr, pscr, dscr = refs
        i = pl.program_id(1)
        qi = lax.broadcasted_iota(jnp.int32, (BLK, BLK), 0)
        kj = lax.broadcasted_iota(jnp.int32, (BLK, BLK), 1)
        tri = kj <= qi
        eye = kj == qi
        cache = {}
        for p in range(w // LANES):
            qpair = q_ref[:, LANES * p:LANES * (p + 1)]
            kcat = _kv_cat(kc_ref, kp_ref if nb > 1 else None, p, gqa, cache)
            for hh in range(2):
                s = _dot_nt(_lane_half(qpair, hh), kcat)
                if nb > 1:
                    sp = jnp.where(i > 0, s[:, :BLK], NEG)
                    sscr[2 * p + hh] = jnp.where(tri, s[:, BLK:], sp)
                    if diag:
                        dscr[2 * p + hh] = jnp.where(eye, sp, NEG)
                else:
                    sscr[2 * p + hh] = jnp.where(tri, s, NEG)
        lane = lax.broadcasted_iota(jnp.int32, (BLK, LANES), 1)
        lse_all = jnp.zeros((BLK, LANES), f32)
        for p in range(w // LANES):
            for hh in range(2):
                h = 2 * p + hh
                comb = sscr[h]
                if diag:
                    dtile = dscr[h]
                    m = jnp.max(jnp.maximum(comb, dtile), axis=-1, keepdims=True)
                else:
                    m = jnp.max(comb, axis=-1, keepdims=True)
                if has_sink:
                    sk = sink_ref[0, h]
                    m = jnp.maximum(m, sk)
                e = jnp.exp(comb - m)
                if diag:
                    ed = jnp.exp(dtile - m)
                    den = jnp.sum(e + ed, axis=-1, keepdims=True)
                else:
                    den = jnp.sum(e, axis=-1, keepdims=True)
                if has_sink:
                    den = den + jnp.exp(sk - m)
                inv = 1.0 / den
                if nb > 1:
                    pscr[h, :, :BLK] = (jnp.where(tri, ed if diag else 0.0, e) * inv).astype(bf16)
                    pscr[h, :, BLK:] = (jnp.where(tri, e, 0.0) * inv).astype(bf16)
                else:
                    pscr[h] = (e * inv).astype(bf16)
                lse_all = jnp.where(lane == h, jnp.broadcast_to(m + jnp.log(den), (BLK, LANES)), lse_all)
        lse_ref[...] = lse_all
        for p in range(w // LANES):
            vcat = _kv_cat(vc_ref, vp_ref if nb > 1 else None, p, gqa, cache)
            key = ("halves", id(vc_ref), p // 2 if gqa else p)
            if key not in cache:
                cache[key] = (_lane_half(vcat, 0), _lane_half(vcat, 1))
            o_ref[:, LANES * p:LANES * (p + 1)] = (_dot(pscr[2 * p], cache[key][0])
                                                   + _dot(pscr[2 * p + 1], cache[key][1])).astype(o_ref.dtype)

    assert max_dist in (BLK - 1, BLK)
    diag = nb > 1 and max_dist == BLK
    cur = lambda ww: pl.BlockSpec((None, BLK, ww), lambda a, i: (a, i, 0))
    prev = lambda ww: pl.BlockSpec((None, BLK, ww), lambda a, i: (a, jnp.maximum(i - 1, 0), 0))
    in_specs = [cur(w), cur(wk)] + ([prev(wk)] if nb > 1 else []) + [cur(wk)] + ([prev(wk)] if nb > 1 else [])
    args = [q, k] + ([k] if nb > 1 else []) + [v] + ([v] if nb > 1 else [])
    if has_sink:
        in_specs = [pl.BlockSpec(memory_space=pltpu.SMEM)] + in_specs
        args = [sink] + args
    return pl.pallas_call(
        body, name=name, grid=(n, nb), in_specs=in_specs,
        out_specs=[cur(w), cur(LANES)],
        out_shape=[jax.ShapeDtypeStruct((n, l, w), o_dtype), jax.ShapeDtypeStruct((n, l, LANES), f32)],
        scratch_shapes=[pltpu.VMEM((w // HD, BLK, BLK), f32), pltpu.VMEM((w // HD, BLK, 2 * BLK if nb > 1 else BLK), bf16),
                        pltpu.VMEM((w // HD if diag else 1, BLK, BLK), f32)],
        compiler_params=_cp(("arbitrary", "arbitrary")),
    )(*args)


def _attn_bwd(q, k, v, do, delta, lse, sink, *, max_dist, name):
    n, l, w = q.shape
    wk = k.shape[-1]
    nb = l // BLK
    gqa = wk != w
    has_sink = sink is not None

    def body(*refs):
        if has_sink:
            sink_ref, refs = refs[0], refs[1:]
        if nb > 1:
            q_ref, kc_ref, kp_ref, vc_ref, vp_ref, do_ref, delta_ref, lse_ref = refs[:8]
            rest = refs[8:]
        else:
            q_ref, kc_ref, vc_ref, do_ref, delta_ref, lse_ref = refs[:6]
            rest = refs[6:]
        if has_sink:
            dq_ref, dk_ref, dv_ref, dsink_ref = rest[:4]
            rest = rest[4:]
        else:
            dq_ref, dk_ref, dv_ref = rest[:3]
            rest = rest[3:]
        step = pl.program_id(1)
        blk_idx = nb - 1 - step
        if nb > 1:
            ck, cv = rest[:2]
            rest = rest[2:]

            @pl.when(step == 0)
            def _():
                ck[...] = jnp.zeros_like(ck)
                cv[...] = jnp.zeros_like(cv)

        sscr, dpscr, pscr, dsscr = rest[:4]
        if diag:
            dscr, ddscr = rest[4:]
        if has_sink:
            @pl.when((pl.program_id(0) == 0) & (step == 0))
            def _():
                dsink_ref[...] = jnp.zeros_like(dsink_ref)

        lane = lax.broadcasted_iota(jnp.int32, (BLK, LANES), 1)
        lo = lane < HD
        qi = lax.broadcasted_iota(jnp.int32, (BLK, BLK), 0)
        kj = lax.broadcasted_iota(jnp.int32, (BLK, BLK), 1)
        tri = kj <= qi
        eye = kj == qi
        cache = {}
        kp, vp = (kp_ref, vp_ref) if nb > 1 else (None, None)
        rows = 2 * BLK if nb > 1 else BLK
        for p in range(w // LANES):
            sl = slice(LANES * p, LANES * (p + 1))
            qpair, dopair = q_ref[:, sl], do_ref[:, sl]
            kcat, vcat = _kv_cat(kc_ref, kp, p, gqa, cache), _kv_cat(vc_ref, vp, p, gqa, cache)
            for hh in range(2):
                h = 2 * p + hh
                s = _dot_nt(_lane_half(qpair, hh), kcat)
                dp = _dot_nt(_lane_half(dopair, hh), vcat)
                if nb > 1:
                    sp = jnp.where(blk_idx > 0, s[:, :BLK], NEG)
                    sscr[h] = jnp.where(tri, s[:, BLK:], sp)
                    dpscr[h] = jnp.where(tri, dp[:, BLK:], dp[:, :BLK])
                    if diag:
                        dscr[h] = jnp.where(eye, sp, NEG)
                        ddscr[h] = dp[:, :BLK]
                else:
                    sscr[h] = jnp.where(tri, s, NEG)
                    dpscr[h] = dp
        for p in range(w // LANES):
            for hh in range(2):
                h = 2 * p + hh
                lse_b = jnp.broadcast_to(lse_ref[:, h:h + 1], (BLK, BLK))
                delta = jnp.broadcast_to(delta_ref[:, h:h + 1], (BLK, BLK))
                pr = jnp.exp(sscr[h] - lse_b)
                ds = pr * (dpscr[h] - delta)
                if nb > 1:
                    if diag:
                        prd = jnp.exp(dscr[h] - lse_b)
                        dsd = prd * (ddscr[h] - delta)
                    else:
                        prd = dsd = 0.0
                    pscr[h, :, :BLK] = jnp.where(tri, prd, pr).astype(bf16)
                    pscr[h, :, BLK:] = jnp.where(tri, pr, 0.0).astype(bf16)
                    dsscr[h, :, :BLK] = jnp.where(tri, dsd, ds).astype(bf16)
                    dsscr[h, :, BLK:] = jnp.where(tri, ds, 0.0).astype(bf16)
                else:
                    pscr[h] = pr.astype(bf16)
                    dsscr[h] = ds.astype(bf16)
                if has_sink:
                    dsk = -jnp.sum(jnp.where(lane == 0, jnp.exp(sink_ref[0, h] - lse_b) * delta, 0.0), keepdims=True)
                    dsink_ref[h:h + 1, :] += jnp.broadcast_to(dsk, (1, LANES))
        gk = [jnp.zeros((rows, LANES), f32), jnp.zeros((rows, LANES), f32)]
        gv = [jnp.zeros((rows, LANES), f32), jnp.zeros((rows, LANES), f32)]
        for p in range(w // LANES):
            sl = slice(LANES * p, LANES * (p + 1))
            qpair, dopair = q_ref[:, sl], do_ref[:, sl]
            kcat = _kv_cat(kc_ref, kp, p, gqa, cache)
            key = ("halves", p // 2 if gqa else p)
            if key not in cache:
                cache[key] = (_lane_half(kcat, 0), _lane_half(kcat, 1))
            dq_ref[:, sl] = _dot(dsscr[2 * p], cache[key][0]) + _dot(dsscr[2 * p + 1], cache[key][1])
            dk_pair = _dot_tn(dsscr[2 * p], _lane_half(qpair, 0)) + _dot_tn(dsscr[2 * p + 1], _lane_half(qpair, 1))
            dv_pair = _dot_tn(pscr[2 * p], _lane_half(dopair, 0)) + _dot_tn(pscr[2 * p + 1], _lane_half(dopair, 1))
            if gqa:
                gk[p // 2] = gk[p // 2] + dk_pair
                gv[p // 2] = gv[p // 2] + dv_pair
            elif nb > 1:
                dk_ref[:, sl] = dk_pair[BLK:] + ck[:, sl]
                dv_ref[:, sl] = dv_pair[BLK:] + cv[:, sl]
                ck[:, sl] = dk_pair[:BLK]
                cv[:, sl] = dv_pair[:BLK]
            else:
                dk_ref[:, sl] = dk_pair
                dv_ref[:, sl] = dv_pair
        if gqa:
            lor = lax.broadcasted_iota(jnp.int32, (rows, LANES), 1) < HD
            fold = lambda g: jnp.where(lor, g[0] + pltpu.roll(g[0], HD, 1), g[1] + pltpu.roll(g[1], HD, 1))
            dk_full, dv_full = fold(gk), fold(gv)
            dk_ref[...] = dk_full[BLK:] + ck[...]
            dv_ref[...] = dv_full[BLK:] + cv[...]
            ck[...] = dk_full[:BLK]
            cv[...] = dv_full[:BLK]

    assert max_dist in (BLK - 1, BLK)
    diag = nb > 1 and max_dist == BLK
    cur = lambda ww: pl.BlockSpec((None, BLK, ww), lambda a, i: (a, nb - 1 - i, 0))
    prev = lambda ww: pl.BlockSpec((None, BLK, ww), lambda a, i: (a, jnp.maximum(nb - 2 - i, 0), 0))
    in_specs = ([cur(w), cur(wk)] + ([prev(wk)] if nb > 1 else []) + [cur(wk)] + ([prev(wk)] if nb > 1 else [])
                + [cur(w), cur(LANES), cur(LANES)])
    args = [q, k] + ([k] if nb > 1 else []) + [v] + ([v] if nb > 1 else []) + [do, delta, lse]
    out_specs = [cur(w), cur(wk), cur(wk)]
    out_shape = [jax.ShapeDtypeStruct((n, l, w), f32), jax.ShapeDtypeStruct((n, l, wk), f32), jax.ShapeDtypeStruct((n, l, wk), f32)]
    if has_sink:
        in_specs = [pl.BlockSpec(memory_space=pltpu.SMEM)] + in_specs
        args = [sink] + args
        out_specs.append(pl.BlockSpec((8, LANES), lambda a, i: (0, 0)))
        out_shape.append(jax.ShapeDtypeStruct((8, LANES), f32))
    nh = w // HD
    scratch = [pltpu.VMEM((BLK, wk), f32), pltpu.VMEM((BLK, wk), f32)] if nb > 1 else []
    scratch += [pltpu.VMEM((nh, BLK, BLK), f32)] * 2 + [pltpu.VMEM((nh, BLK, 2 * BLK if nb > 1 else BLK), bf16)] * 2
    if diag:
        scratch += [pltpu.VMEM((nh, BLK, BLK), f32)] * 2
    return pl.pallas_call(
        body, name=name, grid=(n, nb), in_specs=in_specs, out_specs=out_specs, out_shape=out_shape,
        scratch_shapes=scratch, compiler_params=_cp(("arbitrary", "arbitrary")),
    )(*args)


def _split2(x):
    hi = x.astype(bf16)
    return hi, (x - hi.astype(f32)).astype(bf16)


def _heads_to_lanes(xc, e):
    return sum(_dot(t, e) for t in _split2(xc))


def _lanes_to_heads(x, g):
    return sum(_dot(t, g) for t in _split2(x))


HEAD_EXPAND = (np.arange(LANES)[:, None] == np.arange(BW)[None, :] // HD).astype(np.float32)
HEAD_SUM = HEAD_EXPAND.T.copy()


def _branch_weights(l1_ref, l4_ref, l16_ref, scr):
    l4v = _perm_load(l4_ref, scr, 4)
    l16v = _perm_load(l16_ref, scr, 16)
    l1v = l1_ref[...]
    m = jnp.maximum(jnp.maximum(l1v, l4v), l16v)
    e1, e4, e16 = jnp.exp(l1v - m), jnp.exp(l4v - m), jnp.exp(l16v - m)
    z = e1 + e4 + e16
    return e1 / z, e4 / z, e16 / z


def _mix_out(oa, o1, l1, o4, l4, o16, l16, g_mix_a, g_mix_b, w_out, x, mod, g_post):
    def body(oa_ref, o1_ref, l1_ref, o4_ref, l4_ref, o16_ref, l16_ref, ga_ref, gb_ref, w_ref, x_ref, mod_ref, gp_ref, e_ref,
             x1_ref, y_ref, mixed_ref, ob_ref, scr):
        w1, w4, w16 = _branch_weights(l1_ref, l4_ref, l16_ref, scr)
        e = e_ref[...]
        x1w, x4w = _heads_to_lanes(w1, e), _heads_to_lanes(w4, e)
        ob = (x1w * o1_ref[...].astype(f32) + x4w * _perm_load(o4_ref, scr, 4)
              + (1.0 - x1w - x4w) * _perm_load(o16_ref, scr, 16))
        ob_ref[...] = ob
        oan, _ = _rms(oa_ref[...])
        obn, _ = _rms(ob)
        mixed = jnp.concatenate([oan * ga_ref[...], obn * gb_ref[...]], axis=1).astype(bf16)
        mixed_ref[...] = mixed
        y = _dot(mixed, w_ref[...])
        y_ref[...] = y
        yn, _ = _rms(y)
        x1_ref[...] = x_ref[...] + mod_ref[2:3, :] * (yn * gp_ref[...])

    nat = lambda w, dt: jax.ShapeDtypeStruct((BL, SEQ, w), dt)
    return pl.pallas_call(
        body, name="mix_out", grid=(BL, NJ),
        in_specs=[_tok(AQ), _tok(BW), _tok(LANES), _perm_spec(4, BW), _perm_spec(4, LANES), _perm_spec(16, BW),
                  _perm_spec(16, LANES), _full((1, AQ)), _full((1, BW)), _full((D, D)), _tok(D), MOD_SPEC, _full((1, D)),
                  _full((LANES, BW))],
        out_specs=[_tok(D), _tok(D), _tok(D), _tok(BW)],
        out_shape=[nat(D, f32), nat(D, f32), nat(D, bf16), nat(BW, f32)],
        scratch_shapes=[pltpu.VMEM((BW // LANES, TM, LANES), f32)],
        compiler_params=_cp(("arbitrary", "arbitrary")),
    )(oa, o1, l1, o4, l4, o16, l16, g_mix_a, g_mix_b, w_out, x, mod, g_post, jnp.asarray(HEAD_EXPAND, bf16))


def _mlp_up(x1, mod, g_pre, w_up):
    def body(x_ref, mod_ref, g_ref, w_ref, h_ref, u_ref, a_ref):
        xn, _ = _rms(x_ref[...])
        h = (xn * g_ref[...]) * (1.0 + mod_ref[4:5, :]) + mod_ref[3:4, :]
        hb = h.astype(bf16)
        h_ref[...] = hb
        for s in range(NCHIP):
            u = _dot(hb, w_ref[s])
            u_ref[:, D * s:D * (s + 1)] = u.astype(bf16)
            a_ref[:, D * s:D * (s + 1)] = jnp.square(jnp.maximum(u, 0.0)).astype(bf16)

    nat = lambda w: jax.ShapeDtypeStruct((BL, SEQ, w), bf16)
    return pl.pallas_call(
        body, name="mlp_up", grid=(BL, NJ),
        in_specs=[_tok(D), MOD_SPEC, _full((1, D)), _full((NCHIP, D, D))],
        out_specs=[_tok(D), _tok(DFF), _tok(DFF)], out_shape=[nat(D), nat(DFF), nat(DFF)],
        compiler_params=_cp(("arbitrary", "arbitrary")),
    )(x1, mod, g_pre, w_up)


def _mlp_down(a, w_down, x1, target, mod, g_post):
    def body(a_ref, w_ref, x_ref, t_ref, mod_ref, g_ref, gx_ref, dy_ref, accb_ref, accg_ref):
        _acc_init(accb_ref, accg_ref)
        y2 = _dot(a_ref[...], w_ref[...])
        yn, r = _rms(y2)
        g = g_ref[...]
        gt = mod_ref[5:6, :]
        n2 = yn * g
        err = x_ref[...] + gt * n2 - t_ref[...]
        gout = err * (1.0 / D)
        gx_ref[...] = gout
        dn2 = gout * gt
        dy_ref[...] = _rms_bwd(dn2 * g, yn, r).astype(bf16)
        accb_ref[0:1, :] += _colsum(gout * n2)
        accg_ref[0:1, :] += _colsum(dn2 * yn)
        accg_ref[1:2, :] += jnp.broadcast_to(jnp.sum(err * err, keepdims=True), (1, D))

    return pl.pallas_call(
        body, name="mlp_down", grid=(BL, NJ),
        in_specs=[_tok(DFF), _full((DFF, D)), _tok(D), _tok(D), MOD_SPEC, _full((1, D))],
        out_specs=[_tok(D), _tok(D), ACCB_SPEC, ACCG_SPEC],
        out_shape=[jax.ShapeDtypeStruct((BL, SEQ, D), f32), jax.ShapeDtypeStruct((BL, SEQ, D), bf16)] + ACC_SHAPES,
        compiler_params=_cp(("arbitrary", "arbitrary")),
    )(a, w_down, x1, target, mod, g_post)


def _mlp_bwd(dy2, u, w_down, w_up, x1, gx, mod, g_pre):
    def body(dy_ref, u_ref, wd_hbm, wu_hbm, x_ref, gx_ref, mod_ref, g_ref, du_ref, gx1_ref, accb_ref, accg_ref, wd, wu, sem):
        _acc_init(accb_ref, accg_ref)

        @pl.when((pl.program_id(0) == 0) & (pl.program_id(1) == 0))
        def _():
            c1 = pltpu.make_async_copy(wd_hbm, wd, sem.at[0])
            c2 = pltpu.make_async_copy(wu_hbm, wu, sem.at[1])
            c1.start()
            c2.start()
            c1.wait()
            c2.wait()

        dy = dy_ref[...]
        dh = jnp.zeros((TM, D), f32)
        for s in range(NCHIP):
            sl = slice(D * s, D * (s + 1))
            da = _dot_nt(dy, wd[sl, :])
            du = (da * (2.0 * jnp.maximum(u_ref[:, sl].astype(f32), 0.0))).astype(bf16)
            du_ref[:, sl] = du
            dh = dh + _dot_nt(du, wu[s])
        xn, r = _rms(x_ref[...])
        g = g_ref[...]
        n = xn * g
        dn = dh * (1.0 + mod_ref[4:5, :])
        gx1_ref[...] = gx_ref[...] + _rms_bwd(dn * g, xn, r)
        accb_ref[0:1, :] += _colsum(dh * n)
        accb_ref[1:2, :] += _colsum(dh)
        accg_ref[0:1, :] += _colsum(dn * xn)

    anyspec = pl.BlockSpec(memory_space=pl.ANY)
    return pl.pallas_call(
        body, name="mlp_bwd", grid=(BL, NJ),
        in_specs=[_tok(D), _tok(DFF), anyspec, anyspec, _tok(D), _tok(D), MOD_SPEC, _full((1, D))],
        out_specs=[_tok(DFF), _tok(D), ACCB_SPEC, ACCG_SPEC],
        out_shape=[jax.ShapeDtypeStruct((BL, SEQ, DFF), bf16), jax.ShapeDtypeStruct((BL, SEQ, D), f32)] + ACC_SHAPES,
        scratch_shapes=[pltpu.VMEM((DFF, D), bf16), pltpu.VMEM((NCHIP, D, D), bf16), pltpu.SemaphoreType.DMA((2,))],
        compiler_params=_cp(("arbitrary", "arbitrary")),
    )(dy2, u, w_down, w_up, x1, gx, mod, g_pre)


def _matmul_tn(a, b, *, tn, col_blocked, name):
    t, m = a.shape
    n = b.shape[1]
    tmm = min(m, 1024)
    tk = 2048 if tn <= 1024 else 1024
    nk = t // tk

    def body(a_ref, b_ref, o_ref):
        @pl.when(pl.program_id(2) == 0)
        def _():
            o_ref[...] = jnp.zeros_like(o_ref)

        o_ref[...] += _dot_tn(a_ref[...], b_ref[...])

    if col_blocked:
        out_spec = pl.BlockSpec((None, tmm, tn), lambda i, j, k: (j, i, 0))
        out_shape = jax.ShapeDtypeStruct((n // tn, m, tn), f32)
    else:
        out_spec = pl.BlockSpec((tmm, tn), lambda i, j, k: (i, j))
        out_shape = jax.ShapeDtypeStruct((m, n), f32)
    return pl.pallas_call(
        body, name=name, grid=(m // tmm, n // tn, nk),
        in_specs=[pl.BlockSpec((tk, tmm), lambda i, j, k: (k, i)), pl.BlockSpec((tk, tn), lambda i, j, k: (k, j))],
        out_specs=out_spec, out_shape=out_shape,
        compiler_params=_cp(("arbitrary", "arbitrary", "arbitrary")),
    )(a, b)


def _grad_w_in(h, dproj):
    t = h.shape[0]
    tk = 1024
    nk = t // tk
    sw = INW // NCHIP

    def body(a_ref, b_ref, o_ref, acc):
        k = pl.program_id(0)

        @pl.when(k == 0)
        def _():
            acc[...] = jnp.zeros_like(acc)

        acc[...] += _dot_tn(a_ref[...], b_ref[...])

        @pl.when(k == nk - 1)
        def _():
            for s in range(NCHIP):
                o_ref[s] = acc[:, sw * s:sw * (s + 1)]

    return pl.pallas_call(
        body, name="grad_w_in", grid=(nk,),
        in_specs=[pl.BlockSpec((tk, D), lambda k: (k, 0)), pl.BlockSpec((tk, INW), lambda k: (k, 0))],
        out_specs=pl.BlockSpec((NCHIP, D, sw), lambda k: (0, 0, 0)), out_shape=jax.ShapeDtypeStruct((NCHIP, D, sw), f32),
        scratch_shapes=[pltpu.VMEM((D, INW), f32)], compiler_params=_cp(("arbitrary",)),
    )(h, dproj)


def _attn_out_bwd(gx1, y, mod, g_post, w_out, oa, ob, g_mix_a, g_mix_b, l1, l4, l16):
    def body(gx_ref, y_ref, mod_ref, gp_ref, w_ref, oa_ref, ob_ref, ga_ref, gb_ref, l1_ref, l4_ref, l16_ref, e_ref, g_ref,
             dy_ref, doa_ref, do1_ref, do4_ref, do16_ref, da_ref, d1_ref, d4_ref, d16_ref, accb_ref, accg_ref, scr):
        _acc_init(accb_ref, accg_ref)
        w1, w4, w16 = _branch_weights(l1_ref, l4_ref, l16_ref, scr)
        e, hs = e_ref[...], g_ref[...]
        gx1v = gx_ref[...]
        yn, ry = _rms(y_ref[...])
        gp = gp_ref[...]
        gt = mod_ref[2:3, :]
        dn1 = gx1v * gt
        dy = _rms_bwd(dn1 * gp, yn, ry).astype(bf16)
        dy_ref[...] = dy
        dmixed = _dot_nt(dy, w_ref[...])
        dma, dmb = dmixed[:, :AQ], dmixed[:, AQ:]
        oa, ob = oa_ref[...], ob_ref[...]
        oan, ra = _rms(oa)
        obn, rb = _rms(ob)
        doa = _rms_bwd(dma * ga_ref[...], oan, ra)
        doa_ref[...] = doa.astype(bf16)
        da_ref[...] = _lanes_to_heads(doa * oa, hs)
        dob = _rms_bwd(dmb * gb_ref[...], obn, rb)
        dd = _lanes_to_heads(dob * ob, hs)
        x1w, x4w = _heads_to_lanes(w1, e), _heads_to_lanes(w4, e)
        do1_ref[...] = (x1w * dob).astype(bf16)
        d1_ref[...] = w1 * dd
        _perm_store(x4w * dob, scr, do4_ref, 4)
        _perm_store(w4 * dd, scr, d4_ref, 4)
        _perm_store((1.0 - x1w - x4w) * dob, scr, do16_ref, 16)
        _perm_store(w16 * dd, scr, d16_ref, 16)
        accb_ref[0:1, :] += _colsum(gx1v * (yn * gp))
        accg_ref[0:1, :] += _colsum(dn1 * yn)
        accg_ref[1:2, :] += jnp.concatenate([_colsum(dma * oan), _colsum(dmb * obn)], axis=1)

    nat = lambda w, dt: jax.ShapeDtypeStruct((BL, SEQ, w), dt)
    return pl.pallas_call(
        body, name="attn_out_bwd", grid=(BL, NJ),
        in_specs=[_tok(D), _tok(D), MOD_SPEC, _full((1, D)), _full((D, D)), _tok(AQ), _tok(BW), _full((1, AQ)), _full((1, BW)),
                  _tok(LANES), _perm_spec(4, LANES), _perm_spec(16, LANES), _full((LANES, BW)), _full((BW, LANES))],
        out_specs=[_tok(D), _tok(AQ), _tok(BW), _perm_spec(4, BW), _perm_spec(16, BW),
                   _tok(LANES), _tok(LANES), _perm_spec(4, LANES), _perm_spec(16, LANES), ACCB_SPEC, ACCG_SPEC],
        out_shape=[nat(D, bf16), nat(AQ, bf16), nat(BW, bf16), jax.ShapeDtypeStruct((BL, 4, SEQ // 4, BW), bf16),
                   jax.ShapeDtypeStruct((BL, 16, SEQ // 16, BW), bf16), nat(LANES, f32), nat(LANES, f32),
                   jax.ShapeDtypeStruct((BL, 4, SEQ // 4, LANES), f32), jax.ShapeDtypeStruct((BL, 16, SEQ // 16, LANES), f32)]
                  + ACC_SHAPES,
        scratch_shapes=[pltpu.VMEM((BW // LANES, TM, LANES), f32)],
        compiler_params=_cp(("arbitrary", "arbitrary")),
    )(gx1, y, mod, g_post, w_out, oa, ob, g_mix_a, g_mix_b, l1, l4, l16, jnp.asarray(HEAD_EXPAND, bf16),
      jnp.asarray(HEAD_SUM, bf16))


def _attn_in_bwd(dqa, dka, dva, d1, d4, d16, tc, ts1, ts2, w_in, x, gx1, mod, g_pre):
    def body(dqa_ref, dka_ref, dva_ref, dq1_ref, dk1_ref, dv1_ref, dq4_ref, dk4_ref, dv4_ref, dq16_ref, dk16_ref, dv16_ref,
             c_ref, s1_ref, s2_ref, w_ref, x_ref, gx_ref, mod_ref, g_ref, dproj_ref, dx_ref, accb_ref, accg_ref, scr):
        _acc_init(accb_ref, accg_ref)
        c, s1, s2 = c_ref[...], s1_ref[...], s2_ref[...]
        tot = lambda r1, r4, r16: r1[...] + _perm_load(r4, scr, 4) + _perm_load(r16, scr, 16)
        dqb = tot(dq1_ref, dq4_ref, dq16_ref)
        dkb = tot(dk1_ref, dk4_ref, dk16_ref)
        dvb = tot(dv1_ref, dv4_ref, dv16_ref)
        dproj = jnp.concatenate([
            _rope_t(dqa_ref[...], c, s1, s2) * 0.125, _rope_t(dka_ref[...], c, s1, s2), dva_ref[...],
            _rope_t(dqb, c, s1, s2) * 0.125, _rope_t(dkb, c, s1, s2), dvb], axis=1).astype(bf16)
        dproj_ref[...] = dproj
        dh = _dot_nt(dproj, w_ref[...])
        xn, r = _rms(x_ref[...])
        g = g_ref[...]
        dn = dh * (1.0 + mod_ref[1:2, :])
        dx_ref[...] = gx_ref[...] + _rms_bwd(dn * g, xn, r)
        accb_ref[0:1, :] += _colsum(dh * (xn * g))
        accb_ref[1:2, :] += _colsum(dh)
        accg_ref[0:1, :] += _colsum(dn * xn)

    return pl.pallas_call(
        body, name="attn_in_bwd", grid=(BL, NJ),
        in_specs=[_tok(AQ), _tok(AKV), _tok(AKV)] + [_tok(BW)] * 3 + [_perm_spec(4, BW)] * 3 + [_perm_spec(16, BW)] * 3
                 + [_tok(LANES)] * 3 + [_full((D, INW)), _tok(D), _tok(D), MOD_SPEC, _full((1, D))],
        out_specs=[_tok(INW), _tok(D), ACCB_SPEC, ACCG_SPEC],
        out_shape=[jax.ShapeDtypeStruct((BL, SEQ, INW), bf16), jax.ShapeDtypeStruct((BL, SEQ, D), f32)] + ACC_SHAPES,
        scratch_shapes=[pltpu.VMEM((BW // LANES, TM, LANES), f32)],
        compiler_params=_cp(("arbitrary", "arbitrary")),
    )(dqa, dka, dva, *d1, *d4, *d16, tc, ts1, ts2, w_in, x, gx1, mod, g_pre)


def _local_step(x, positions, mod, target, tok0, first_weight, later_weights, grad_ready, grad_reduce, g_attn_pre, g_attn_post,
                sink_a, g_mix_a, g_mix_b, g_mlp_pre, g_mlp_post):
    inv = np.float32(THETA) ** (-np.arange(0, ROT, 2, dtype=np.float32) / np.float32(ROT))
    lane = np.arange(LANES) % HD
    inv_lane = jnp.asarray(np.where(lane < ROT, inv[lane % (ROT // 2)], 0.0).astype(np.float32)[None, :])
    tabs = _rope_tables(positions.reshape(BL * SEQ, 1), inv_lane + tok0)
    w_in = first_weight(tuple(tabs))
    tc, ts1, ts2 = [t.reshape(BL, SEQ, LANES) for t in tabs]

    (h, qa, ka, va, q1, k1, v1, q4, k4, v4, q16, k16, v16) = _attn_in(x, mod, g_attn_pre, w_in, tc, ts1, ts2)
    seqs = lambda t: t.reshape(t.shape[0] * t.shape[1], t.shape[2], t.shape[3])
    q4, k4, v4, q16, k16, v16 = [seqs(t) for t in (q4, k4, v4, q16, k16, v16)]
    oa, la = _attn_fwd(qa, ka, va, sink_a, max_dist=BLK - 1, o_dtype=f32, name="attn_a_fwd")
    o1, l1 = _attn_fwd(q1, k1, v1, None, max_dist=BLK, o_dtype=bf16, name="attn_b1_fwd")
    o4, l4 = _attn_fwd(q4, k4, v4, None, max_dist=BLK, o_dtype=bf16, name="attn_b4_fwd")
    o16, l16 = _attn_fwd(q16, k16, v16, None, max_dist=BLK, o_dtype=bf16, name="attn_b16_fwd")
    b4 = lambda t: t.reshape(BL, 4, SEQ // 4, t.shape[-1])
    b16 = lambda t: t.reshape(BL, 16, SEQ // 16, t.shape[-1])
    w_out, mlp_weights, tok = later_weights((oa, o1, o4, o16))
    x1, y, mixed, ob = _mix_out(oa, o1, l1, b4(o4), b4(l4), b16(o16), b16(l16), g_mix_a, g_mix_b, w_out, x, mod + tok, g_attn_post)
    w_up, w_down = mlp_weights((x1,))
    h2, u, a = _mlp_up(x1, mod, g_mlp_pre, w_up)
    gx, dy2, accb_d, accg_d = _mlp_down(a, w_down, x1, target, mod, g_mlp_post)

    flat = lambda t: t.reshape(BL * SEQ, t.shape[-1])
    mod = mod + grad_ready("w_down", _matmul_tn(flat(a), flat(dy2), tn=D, col_blocked=False, name="grad_w_down"))
    du, gx1, accb_m, accg_m = _mlp_bwd(dy2, u, w_down, w_up, x1, gx, mod, g_mlp_pre)
    mod = mod + grad_reduce("w_down", (gx1,))
    mod = mod + grad_ready("w_up", _matmul_tn(flat(h2), flat(du), tn=D, col_blocked=True, name="grad_w_up"))

    dy, doa, do1, do4, do16, da, dl1, dl4, dl16, accb_o, accg_o = _attn_out_bwd(
        gx1, y, mod, g_attn_post, w_out, oa, ob, g_mix_a, g_mix_b, l1, b4(l4), b16(l16))
    tok = grad_reduce("w_up", (dy,))
    gw_out = _matmul_tn(flat(mixed), flat(dy), tn=D, col_blocked=False, name="grad_w_out")
    dqa, dka, dva, dsink = _attn_bwd(qa, ka, va, doa, da + tok, la, sink_a, max_dist=BLK - 1, name="attn_a_bwd")
    d1 = _attn_bwd(q1, k1, v1, do1, dl1, l1, None, max_dist=BLK, name="attn_b1_bwd")
    d4 = _attn_bwd(q4, k4, v4, seqs(do4), seqs(dl4), l4, None, max_dist=BLK, name="attn_b4_bwd")
    d16 = _attn_bwd(q16, k16, v16, seqs(do16), seqs(dl16), l16, None, max_dist=BLK, name="attn_b16_bwd")
    dproj, grad_x, accb_i, accg_i = _attn_in_bwd(dqa, dka, dva, d1, [b4(t) for t in d4], [b16(t) for t in d16],
                                                 tc, ts1, ts2, w_in, x, gx1, mod, g_attn_pre)
    gw_in = _grad_w_in(flat(h), flat(dproj))
    dsink = dsink + grad_ready("w_in_w_out", (gw_in, gw_out))

    return grad_x, (accb_i, accb_o, accb_m, accb_d, accg_i, accg_o, accg_m, accg_d, dsink)


ADAW = NMOD * D // NCHIP


def _pos():
    return lax.axis_index("x"), lax.axis_index("y"), lax.axis_index("c")


def _flip(v, bit):
    return 1 - v if bit else v


def _all_peers(x, y, c):
    return [(_flip(x, k >> 2 & 1), _flip(y, k >> 1 & 1), _flip(c, k & 1)) for k in range(1, NDEV)]


def _other_chips(x, y):
    return [(1 - x, y), (x, 1 - y), (1 - x, 1 - y)]


def _rcopy(src, dst, send, recv, k, dev):
    return pltpu.make_async_remote_copy(src_ref=src, dst_ref=dst, send_sem=send.at[k], recv_sem=recv.at[k],
                                        device_id=dev, device_id_type=MESH)


def _gather_small(src, buf, send, recv):
    x, y, c = _pos()
    me = 4 * x + 2 * y + c
    peers = _all_peers(x, y, c)
    sends = [_rcopy(src, buf.at[me], send, recv, k, p) for k, p in enumerate(peers)]
    for cp in sends:
        cp.start()
    for k, (px, py, pc) in enumerate(peers):
        _rcopy(src, buf.at[4 * px + 2 * py + pc], send, recv, k, (px, py, pc)).wait_recv()
    for cp in sends:
        cp.wait_send()
    return me


def _ada_fwd(c_in, w_ada, b_cols):
    def body(c_ref, w_ref, b_ref, mod_ref, cond_ref, cbuf, mbuf, s1, r1, s2, r2):
        x, y, c = _pos()
        chip = 2 * x + y
        me = _gather_small(c_ref, cbuf, s1, r1)
        cbuf[me] = c_ref[...]
        for i in range(NDEV):
            cond_ref[BL * i:BL * (i + 1), :] = cbuf[i]
        call = cond_ref[...]
        cond = call / (1.0 + jnp.exp(-call))
        cond_ref[...] = cond
        mbuf[chip] = _dot(cond.astype(bf16), w_ref[...].astype(bf16)) + b_ref[...]
        chips = _other_chips(x, y)
        sends = [_rcopy(mbuf.at[chip], mbuf.at[chip], s2, r2, j, (px, py, c)) for j, (px, py) in enumerate(chips)]
        for cp in sends:
            cp.start()
        for j, (px, py) in enumerate(chips):
            _rcopy(mbuf.at[chip], mbuf.at[2 * px + py], s2, r2, j, (px, py, c)).wait_recv()
        for cp in sends:
            cp.wait_send()
        row = lax.broadcasted_iota(jnp.int32, (BL * NDEV, ADAW), 0)
        for s in range(NCHIP):
            slab = mbuf[s]
            for j in range(BL):
                mod_ref[j:j + 1, ADAW * s:ADAW * (s + 1)] = jnp.sum(jnp.where(row == BL * me + j, slab, 0.0), axis=0, keepdims=True)

    vm = pl.BlockSpec(memory_space=pltpu.VMEM)
    return pl.pallas_call(
        body, name="ada_fwd", in_specs=[vm, vm, vm], out_specs=[vm, vm],
        out_shape=[jax.ShapeDtypeStruct((BL, NMOD * D), f32), jax.ShapeDtypeStruct((BL * NDEV, D), f32)],
        scratch_shapes=[pltpu.VMEM((NDEV, BL, D), f32), pltpu.VMEM((NCHIP, BL * NDEV, ADAW), f32),
                        pltpu.SemaphoreType.DMA((NDEV - 1,)), pltpu.SemaphoreType.DMA((NDEV - 1,)),
                        pltpu.SemaphoreType.DMA((NCHIP - 1,)), pltpu.SemaphoreType.DMA((NCHIP - 1,))],
        compiler_params=pltpu.CompilerParams(vmem_limit_bytes=VMEM_LIMIT),
    )(c_in, w_ada, b_cols)


def _small_allreduce(accs, cond_all):
    def body(bi, bo, bm, bd, gi, go, gm, gd, dsink, cond_ref, gw_ref, gb_ref, small_ref, pay, pbuf, dall, s1, r1):
        x, y, c = _pos()
        chip = 2 * x + y
        pay[...] = jnp.zeros_like(pay)
        for b in range(BL):
            for k, (ref, r) in enumerate(((bi, 1), (bi, 0), (bo, 0), (bm, 1), (bm, 0), (bd, 0))):
                pay[b:b + 1, D * k:D * (k + 1)] = ref[b, r:r + 1, :]
        for off, ref, r in ((OFF_G_ATTN_PRE, gi, 0), (OFF_G_ATTN_POST, go, 0), (OFF_G_MIX_A, go, 1), (OFF_G_MLP_PRE, gm, 0),
                            (OFF_G_MLP_POST, gd, 0)):
            pay[BL:BL + 1, off:off + D] = ref[r:r + 1, :]
        eye = lax.broadcasted_iota(jnp.int32, (8, LANES), 0) == lax.broadcasted_iota(jnp.int32, (8, LANES), 1)
        pay[BL:BL + 1, OFF_SINK:OFF_SINK + LANES] = jnp.sum(jnp.where(eye, dsink[...], 0.0), axis=0, keepdims=True)
        pay[BL:BL + 1, OFF_LOSS:OFF_LOSS + LANES] = gd[1:2, 0:LANES]
        me = _gather_small(pay, pbuf, s1, r1)
        pbuf[me] = pay[...]
        small = pbuf[0, BL:BL + 1, :]
        for i in range(1, NDEV):
            small = small + pbuf[i, BL:BL + 1, :]
        small_ref[...] = small
        for i in range(NDEV):
            dall[BL * i:BL * (i + 1), :] = pbuf[i, 0:BL, :]
        gb_ref[...] = jnp.sum(dall[...], axis=0, keepdims=True)
        cols = jnp.zeros((BL * NDEV, ADAW), f32)
        for s in range(NCHIP):
            cols = cols + jnp.where(chip == s, dall[:, ADAW * s:ADAW * (s + 1)], 0.0)
        gw_ref[...] = lax.dot_general(cond_ref[...], cols, (((0,), (0,)), ((), ())), preferred_element_type=f32,
                                      precision=lax.Precision.HIGHEST)

    vm = pl.BlockSpec(memory_space=pltpu.VMEM)
    return pl.pallas_call(
        body, name="small_allreduce", in_specs=[vm] * 10, out_specs=[vm] * 3,
        out_shape=[jax.ShapeDtypeStruct((D, ADAW), f32), jax.ShapeDtypeStruct((1, PAYW), f32), jax.ShapeDtypeStruct((1, PAYW), f32)],
        scratch_shapes=[pltpu.VMEM((4, PAYW), f32), pltpu.VMEM((NDEV, 4, PAYW), f32), pltpu.VMEM((BL * NDEV, PAYW), f32),
                        pltpu.SemaphoreType.DMA((NDEV - 1,)), pltpu.SemaphoreType.DMA((NDEV - 1,))],
        compiler_params=pltpu.CompilerParams(vmem_limit_bytes=VMEM_LIMIT),
    )(*accs, cond_all)


def _half(ref, c):
    r2 = ref.shape[0] // 2
    return ref.at[pl.ds(pl.multiple_of(c * r2, 16), r2), :]


HBM_SPEC = pl.BlockSpec(memory_space=pltpu.HBM)
SEM_SPEC = pl.BlockSpec(memory_space=pltpu.SEMAPHORE)
EFFECT = pltpu.SideEffectType.DATAFLOW_SIDE_EFFECTING
NLINK = NCHIP - 1


def _in_hbm(a):
    return pltpu.with_memory_space_constraint(a, pltpu.HBM)


NSEM = 4


def _split_start(name, srcs, land_shapes, builds, after=(), lands=None):
    n = len(srcs)
    after = [a.reshape(1, 1) if a.ndim == 0 else a for a in after]
    na = len(after)

    def body(*refs):
        src, land, token = refs[:n], refs[n:2 * n], refs[-1]
        send, recv = refs[2 * n + na:3 * n + na], refs[3 * n + na:4 * n + na]
        for t in range(n):
            for out_cp, _ in builds[t](src[t], land[t], send[t], recv[t]):
                out_cp.start()
        token[...] = jnp.zeros_like(token)

    if lands is None:
        lands = [lax.empty(s.shape, s.dtype) for s in land_shapes]
    lands = [_in_hbm(a) for a in lands]
    sems = [pltpu.SemaphoreType.DMA((NSEM,))] * (2 * n)
    thru = [pltpu.HBM(a.shape, a.dtype) for a in list(srcs) + lands]
    res = pl.pallas_call(
        body, name=name, out_shape=sems + thru + [jax.ShapeDtypeStruct((8, LANES), f32)],
        in_specs=[HBM_SPEC] * (2 * n) + [pl.BlockSpec(memory_space=pl.ANY)] * na,
        out_specs=[SEM_SPEC] * (2 * n) + [HBM_SPEC] * (2 * n) + [pl.BlockSpec(memory_space=pltpu.VMEM)],
        input_output_aliases={i: 2 * n + i for i in range(2 * n)},
        compiler_params=pltpu.CompilerParams(has_side_effects=EFFECT),
    )(*[_in_hbm(a) for a in srcs], *lands, *after)
    flight = [(res[2 * n + t], res[3 * n + t], res[t], res[n + t]) for t in range(n)]
    return flight, res[-1][0, 0]


def _split_wait(name, flight, builds, after):
    m = len(flight)
    after = [a.reshape(1, 1) if a.ndim == 0 else a for a in after]
    na = len(after)

    def body(*refs):
        src, land, send, recv = refs[:m], refs[m:2 * m], refs[2 * m:3 * m], refs[3 * m:4 * m]
        for t in range(m):
            for out_cp, in_cp in builds[t](src[t], land[t], send[t], recv[t]):
                out_cp.wait_send()
                in_cp.wait_recv()

    ops = [f[0] for f in flight] + [f[1] for f in flight] + [f[2] for f in flight] + [f[3] for f in flight]
    res = pl.pallas_call(
        body, name=name, out_shape=[pltpu.HBM(a.shape, a.dtype) for a in ops[:2 * m]],
        in_specs=[HBM_SPEC] * (2 * m) + [SEM_SPEC] * (2 * m) + [pl.BlockSpec(memory_space=pl.ANY)] * na,
        out_specs=[HBM_SPEC] * (2 * m), input_output_aliases={i: i for i in range(2 * m)},
        compiler_params=pltpu.CompilerParams(has_side_effects=EFFECT),
    )(*ops, *after)
    return res[:m], res[m:2 * m]


def _weight_copies(src, land, send, recv):
    x, y, c = _pos()
    chip = 2 * x + y
    return [(_rcopy(_half(src, c), _half(land.at[chip], c), send, recv, j, (px, py, c)),
             _rcopy(_half(src, c), _half(land.at[2 * px + py], c), send, recv, j, (px, py, c)))
            for j, (px, py) in enumerate(_other_chips(x, y))]


def _grad_copies(src, land, send, recv):
    x, y, c = _pos()
    return [(_rcopy(src.at[2 * px + py], land.at[j], send, recv, j, (px, py, c)),
             _rcopy(src.at[2 * px + py], land.at[j], send, recv, j, (px, py, c)))
            for j, (px, py) in enumerate(_other_chips(x, y))]


def _pair_grad_copies(src, land, send, recv):
    x, y, c = _pos()
    r2 = src.shape[1] // 2
    cp = _rcopy(src.at[:, pl.ds(pl.multiple_of((1 - c) * r2, 8), r2), :], land, send, recv, 0, (x, y, 1 - c))
    return [(cp, cp)]


def _pair_weight_copies(src, land, send, recv):
    x, y, c = _pos()
    sib = (x, y, 1 - c)
    cps = []
    for j, (px, py) in enumerate(_other_chips(x, y)):
        mine, theirs = _half(land.at[2 * px + py], c), _half(land.at[2 * px + py], 1 - c)
        cps.append((_rcopy(mine, mine, send, recv, j, sib), _rcopy(theirs, theirs, send, recv, j, sib)))
    own = _rcopy(src, land.at[2 * x + y], send, recv, NLINK, sib)
    return cps + [(own, own)]


RS_ROWS = 128


def _pair_add(g, landed, c_arr, name):
    _, r2, cw = landed.shape
    nr = r2 // RS_ROWS

    def body(c_ref, g_ref, p_ref, o_ref):
        o_ref[...] = (g_ref[...] + p_ref[...]).astype(bf16)

    gs = pltpu.PrefetchScalarGridSpec(
        num_scalar_prefetch=1, grid=(NCHIP, nr),
        in_specs=[pl.BlockSpec((None, RS_ROWS, cw), lambda s, j, c: (s, c[0] * nr + j, 0)),
                  pl.BlockSpec((None, RS_ROWS, cw), lambda s, j, c: (s, j, 0))],
        out_specs=pl.BlockSpec((None, RS_ROWS, cw), lambda s, j, c: (s, j, 0)))
    return pl.pallas_call(body, name=name, grid_spec=gs, out_shape=jax.ShapeDtypeStruct((NCHIP, r2, cw), bf16),
                          compiler_params=_cp(("arbitrary", "arbitrary")))(c_arr, g, landed)


def _chip_add(half, landed, pos_arr, name):
    _, r2, cw = half.shape
    nr = r2 // RS_ROWS

    def body(s_ref, h_ref, q_ref, o_ref):
        acc = h_ref[...].astype(f32)
        for j in range(NCHIP - 1):
            acc = acc + q_ref[j].astype(f32)
        o_ref[...] = acc

    gs = pltpu.PrefetchScalarGridSpec(
        num_scalar_prefetch=1, grid=(nr,),
        in_specs=[pl.BlockSpec((None, RS_ROWS, cw), lambda j, s: (s[0], j, 0)),
                  pl.BlockSpec((NCHIP - 1, RS_ROWS, cw), lambda j, s: (0, j, 0))],
        out_specs=pl.BlockSpec((RS_ROWS, cw), lambda j, s: (s[1] * nr + j, 0)))
    return pl.pallas_call(body, name=name, grid_spec=gs, out_shape=jax.ShapeDtypeStruct((2 * r2, cw), f32),
                          compiler_params=_cp(("arbitrary",)))(pos_arr, half, landed)


def _pair_gather_copies(src, land, send, recv):
    x, y, c = _pos()
    sib = (x, y, 1 - c)
    return [(_rcopy(_half(land, c), _half(land, c), send, recv, 0, sib),
             _rcopy(_half(land, 1 - c), _half(land, 1 - c), send, recv, 0, sib))]


def _adamw_math(w, g, m, v):
    m = B1 * m + (1.0 - B1) * g
    v = B2 * v + (1.0 - B2) * jnp.square(g)
    m_hat = m / (1.0 - B1 ** STEP)
    v_hat = v / (1.0 - B2 ** STEP)
    return -LR * (m_hat / (jnp.sqrt(v_hat) + AEPS) + WD * w), m, v


ADAM_ROWS = 256


def _adamw(w, g, m, v, name):
    r, cw = w.shape

    def body(w_ref, g_ref, m_ref, v_ref, d_ref, mo_ref, vo_ref):
        d_ref[...], mo_ref[...], vo_ref[...] = _adamw_math(w_ref[...], g_ref[...], m_ref[...], v_ref[...])

    spec = pl.BlockSpec((ADAM_ROWS, cw), lambda i: (i, 0))
    return pl.pallas_call(body, name=name, grid=(r // ADAM_ROWS,), in_specs=[spec] * 4, out_specs=[spec] * 3,
                          out_shape=[jax.ShapeDtypeStruct((r, cw), f32)] * 3, compiler_params=_cp(("arbitrary",)))(w, g, m, v)


SMALL = (("b_ada", None, PAYW), ("g_attn_pre", OFF_G_ATTN_PRE, D), ("g_attn_post", OFF_G_ATTN_POST, D), ("sink_a", OFF_SINK, 8),
         ("g_mix_a", OFF_G_MIX_A, AQ), ("g_mix_b", OFF_G_MIX_B, BW), ("g_mlp_pre", OFF_G_MLP_PRE, D), ("g_mlp_post", OFF_G_MLP_POST, D))


def _adamw_small(small, gb, params):
    n = len(SMALL)

    def body(*refs):
        small_ref, gb_ref = refs[:2]
        wmv = refs[2:2 + 3 * n]
        loss_ref = refs[2 + 3 * n]
        outs = refs[3 + 3 * n:]
        loss_ref[...] = small_ref[:, OFF_LOSS:OFF_LOSS + 1] * (0.5 / D)
        for i, (_, off, width) in enumerate(SMALL):
            g = gb_ref[...] if off is None else small_ref[:, off:off + width]
            w_ref, m_ref, v_ref = wmv[3 * i:3 * i + 3]
            outs[4 * i][...] = g
            outs[4 * i + 1][...], outs[4 * i + 2][...], outs[4 * i + 3][...] = _adamw_math(w_ref[...], g, m_ref[...], v_ref[...])

    vm = pl.BlockSpec(memory_space=pltpu.VMEM)
    out_shape = [jax.ShapeDtypeStruct((1, 1), f32)]
    for _, _, width in SMALL:
        out_shape += [jax.ShapeDtypeStruct((1, width), f32)] * 4
    flat = [a for wmv in params for a in wmv]
    res = pl.pallas_call(body, name="adamw_small", in_specs=[vm] * (2 + 3 * n), out_specs=[vm] * len(out_shape),
                         out_shape=out_shape)(small, gb, *flat)
    return res[0], {name: res[1 + 4 * i:5 + 4 * i] for i, (name, _, _) in enumerate(SMALL)}


def kernel(x, c, positions, w_ada, b_ada, g_attn_pre, g_attn_post, w_in, sink_a, g_mix_a, g_mix_b, w_out, g_mlp_pre, g_mlp_post, w_up, w_down, loss_target, m_w_ada, m_b_ada, m_g_attn_pre, m_g_attn_post, m_w_in, m_sink_a, m_g_mix_a, m_g_mix_b, m_w_out, m_g_mlp_pre, m_g_mlp_post, m_w_up, m_w_down, v_w_ada, v_b_ada, v_g_attn_pre, v_g_attn_post, v_w_in, v_sink_a, v_g_mix_a, v_g_mix_b, v_w_out, v_g_mlp_pre, v_g_mlp_post, v_w_up, v_w_down):
    given = dict(w_ada=w_ada, b_ada=b_ada, g_attn_pre=g_attn_pre, g_attn_post=g_attn_post, w_in=w_in, sink_a=sink_a, g_mix_a=g_mix_a,
                 g_mix_b=g_mix_b, w_out=w_out, g_mlp_pre=g_mlp_pre, g_mlp_post=g_mlp_post, w_up=w_up, w_down=w_down)
    moms = dict(w_ada=(m_w_ada, v_w_ada), b_ada=(m_b_ada, v_b_ada), g_attn_pre=(m_g_attn_pre, v_g_attn_pre),
                g_attn_post=(m_g_attn_post, v_g_attn_post), w_in=(m_w_in, v_w_in), sink_a=(m_sink_a, v_sink_a),
                g_mix_a=(m_g_mix_a, v_g_mix_a), g_mix_b=(m_g_mix_b, v_g_mix_b), w_out=(m_w_out, v_w_out),
                g_mlp_pre=(m_g_mlp_pre, v_g_mlp_pre), g_mlp_post=(m_g_mlp_post, v_g_mlp_post), w_up=(m_w_up, v_w_up),
                w_down=(m_w_down, v_w_down))
    order = ["w_ada", "b_ada", "g_attn_pre", "g_attn_post", "w_in", "sink_a", "g_mix_a", "g_mix_b", "w_out", "g_mlp_pre",
             "g_mlp_post", "w_up", "w_down"]
    xi, yi, ci = _pos()
    chip = 2 * xi + yi

    c_arr = jnp.reshape(ci, (1,)).astype(jnp.int32)
    pos_arr = jnp.stack([chip, ci]).astype(jnp.int32)
    big = ("w_in", "w_out", "w_up", "w_down")

    shards = [given[n][0].astype(bf16) for n in big]
    gathered = [jax.ShapeDtypeStruct((NCHIP,) + s.shape, bf16) for s in shards]
    flight_in, tok = _split_start("weights_start_first", shards[:1], gathered[:1], [_weight_copies])
    b_cols = lax.dynamic_slice(b_ada, (0, chip * ADAW), (1, ADAW))
    mod, cond_all = _ada_fwd(c + tok, w_ada[0], b_cols)
    flight_rest, tok = _split_start("weights_start_rest", shards[1:], gathered[1:], [_weight_copies] * 3, after=(mod,))
    mod = mod.reshape(BL, NMOD, D) + tok
    srcs, lands = _split_wait("weights_wait_first", flight_in, [_weight_copies], (mod,))
    cross_in, tok0 = _split_start("weights_pair_start_first", srcs, None, [_pair_weight_copies], lands=lands)

    def first_weight(after):
        _, (win_g,) = _split_wait("weights_pair_wait_first", cross_in, [_pair_weight_copies], after)
        return win_g.transpose(1, 0, 2).reshape(D, INW)

    def later_weights(after):
        srcs, lands = _split_wait("weights_wait_rest", flight_rest, [_weight_copies] * 3, after)
        fl, tk = _split_start("weights_pair_start_rest", srcs, None, [_pair_weight_copies] * 3, lands=lands)
        _, (wout_g,) = _split_wait("weights_pair_wait_out", fl[:1], [_pair_weight_copies], ())

        def mlp_weights(after):
            _, (wup_g, wdn_g) = _split_wait("weights_pair_wait_mlp", fl[1:], [_pair_weight_copies] * 2, after)
            return wup_g, wdn_g.reshape(DFF, D)

        return wout_g.reshape(D, D), mlp_weights, tk

    crossing, pending = {}, {}

    def grad_ready(group, g):
        if group == "w_down":
            names, slabs = ("w_down",), [g.reshape(NCHIP, DFF // NCHIP, D)]
        elif group == "w_up":
            names, slabs = ("w_up",), [g]
        else:
            names = ("w_in", "w_out")
            slabs = [g[0], g[1].reshape(NCHIP, D // NCHIP, D)]
        fl, tk = _split_start("grad_pair_start_" + group, slabs,
                              [jax.ShapeDtypeStruct((NCHIP, s.shape[1] // 2, s.shape[2]), f32) for s in slabs],
                              [_pair_grad_copies] * len(names))
        crossing[group] = (names, fl)
        return tk

    def grad_reduce(group, after):
        names, fl = crossing[group]
        slabs, landed = _split_wait("grad_pair_wait_" + group, fl, [_pair_grad_copies] * len(names), after)
        halves = [_pair_add(s, p, c_arr, "grad_pair_sum_" + n) for s, p, n in zip(slabs, landed, names)]
        fl, tk = _split_start("grad_start_" + group, halves,
                              [jax.ShapeDtypeStruct((NLINK,) + h.shape[1:], bf16) for h in halves], [_grad_copies] * len(names))
        pending[group] = (names, fl)
        return tk

    grad_x, accs = _local_step(x, positions, mod, loss_target, tok0, first_weight, later_weights, grad_ready, grad_reduce,
                               g_attn_pre, g_attn_post, sink_a, g_mix_a, g_mix_b, g_mlp_pre, g_mlp_post)

    grads, out = {}, {}

    def update(n):
        d, m2, v2 = _adamw(given[n][0], grads[n], moms[n][0][0], moms[n][1][0], "adamw_" + n)
        out[n] = (grads[n][None], d[None], m2[None], v2[None])
        return v2

    def finish(groups, after):
        names = sum((pending[g][0] for g in groups), ())
        fl = sum((pending[g][1] for g in groups), [])
        halves, landed = _split_wait("grad_wait_" + groups[0], fl, [_grad_copies] * len(names), after)
        flights = []
        for h, q, n in zip(halves, landed, names):
            full = _chip_add(h, q, pos_arr, "grad_chip_sum_" + n)
            flights.append(_split_start("grad_gather_start_" + n, [jnp.zeros((8, LANES), f32)], None, [_pair_gather_copies],
                                        lands=[full])[0])
        last = None
        for n, fl1 in zip(names, flights):
            after = (flights[-1][0][0],) if last is None and fl1 is not flights[-1] else () if last is None else (last,)
            _, (grads[n],) = _split_wait("grad_gather_wait_" + n, fl1, [_pair_gather_copies], after)
            last = update(n)
        return last

    grads["w_ada"], gb, small = _small_allreduce(accs, cond_all)
    small = small + grad_reduce("w_in_w_out", (small,))
    last = finish(("w_down", "w_up"), (small,))
    finish(("w_in_w_out",), (last, update("w_ada")))
    loss, res = _adamw_small(small, gb, [(given[n], moms[n][0], moms[n][1]) for n, _, _ in SMALL])
    for n, _, _ in SMALL:
        out[n] = tuple(res[n])
    return (loss.reshape(()), grad_x, *[out[n][0] for n in order], *[out[n][1] for n in order],
            *[out[n][2] for n in order], *[out[n][3] for n in order])
```

```python
import functools

import numpy as np
import jax
import jax.numpy as jnp
from jax import lax
from jax.experimental import pallas as pl
from jax.experimental.pallas import tpu as pltpu

f32 = jnp.float32
bf16 = jnp.bfloat16
MESH = pl.DeviceIdType.MESH

D = 1024
SEQ = 2048
BL = 2
HD = 64
AQ = 512
AKV = 128
BW = 512
INW = 2304
DFF = 4096
NMOD = 6
ROT = 16
THETA = 500000.0
EPS = 1e-6
NEG = -1e30
BLK = 128
TM = 512
NJ = SEQ // TM
LANES = 128
NCHIP = 4
NDEV = 8
VMEM_LIMIT = 56 << 20

LR, B1, B2, AEPS, WD, STEP = 0.001, 0.9, 0.999, 1e-08, 0.01, 10

OFF_G_ATTN_PRE, OFF_G_ATTN_POST, OFF_G_MIX_A, OFF_G_MIX_B = 0, 1024, 2048, 2560
OFF_G_MLP_PRE, OFF_G_MLP_POST, OFF_SINK, OFF_LOSS = 3072, 4096, 5120, 5248
PAYW = NMOD * D


def _cp(sem=None):
    return pltpu.CompilerParams(dimension_semantics=sem, vmem_limit_bytes=VMEM_LIMIT)


def _dot(a, b):
    return jnp.dot(a, b, preferred_element_type=f32)


def _dot_nt(a, b):
    return lax.dot_general(a, b, (((1,), (1,)), ((), ())), preferred_element_type=f32)


def _dot_tn(a, b):
    return lax.dot_general(a, b, (((0,), (0,)), ((), ())), preferred_element_type=f32)


def _rms(x):
    r = lax.rsqrt(jnp.mean(x * x, axis=-1, keepdims=True) + EPS)
    return x * r, r


def _rms_bwd(dy, y, r):
    return r * (dy - y * jnp.mean(dy * y, axis=-1, keepdims=True))


def _colsum(v):
    return jnp.sum(v, axis=0, keepdims=True)


def _rope(p, c, s1, s2):
    outs = []
    for c0 in range(0, p.shape[1], LANES):
        pc = p[:, c0:c0 + LANES]
        outs.append(pc * c + pltpu.roll(pc, LANES - ROT // 2, 1) * s1 + pltpu.roll(pc, ROT // 2, 1) * s2)
    return outs[0] if len(outs) == 1 else jnp.concatenate(outs, axis=1)


def _rope_t(g, c, s1, s2):
    outs = []
    for c0 in range(0, g.shape[1], LANES):
        gc = g[:, c0:c0 + LANES]
        outs.append(gc * c + pltpu.roll(gc * s1, ROT // 2, 1) + pltpu.roll(gc * s2, LANES - ROT // 2, 1))
    return outs[0] if len(outs) == 1 else jnp.concatenate(outs, axis=1)


def _perm_store(val, scr, out_ref, d):
    nc = val.shape[1] // LANES
    for c in range(nc):
        scr[c] = val[:, LANES * c:LANES * (c + 1)]
    for c in range(nc):
        for r in range(d):
            out_ref[r, :, LANES * c:LANES * (c + 1)] = scr[c, pl.ds(r, TM // d, stride=d), :].astype(out_ref.dtype)


def _perm_load(in_ref, scr, d):
    nc = in_ref.shape[-1] // LANES
    for c in range(nc):
        for r in range(d):
            scr[c, pl.ds(r, TM // d, stride=d), :] = in_ref[r, :, LANES * c:LANES * (c + 1)].astype(f32)
    return jnp.concatenate([scr[c] for c in range(nc)], axis=1)


def _tok(w, dtype=None):
    return pl.BlockSpec((None, TM, w), lambda b, j: (b, j, 0))


def _perm_spec(d, w):
    return pl.BlockSpec((None, d, TM // d, w), lambda b, j: (b, 0, j, 0))


def _full(shape):
    n = len(shape)
    return pl.BlockSpec(shape, lambda b, j: (0,) * n)


MOD_SPEC = pl.BlockSpec((None, NMOD, D), lambda b, j: (b, 0, 0))
ACCB_SPEC = pl.BlockSpec((None, 8, D), lambda b, j: (b, 0, 0))
ACCG_SPEC = pl.BlockSpec((8, D), lambda b, j: (0, 0))
ACC_SHAPES = [jax.ShapeDtypeStruct((BL, 8, D), f32), jax.ShapeDtypeStruct((8, D), f32)]


def _acc_init(accb_ref, accg_ref):
    b, j = pl.program_id(0), pl.program_id(1)

    @pl.when(j == 0)
    def _():
        accb_ref[...] = jnp.zeros_like(accb_ref)

    @pl.when((b == 0) & (j == 0))
    def _():
        accg_ref[...] = jnp.zeros_like(accg_ref)


def _rope_tables(pos_col, inv_lane):
    def body(p_ref, inv_ref, c_ref, s1_ref, s2_ref):
        ang = p_ref[...].astype(f32) * inv_ref[...]
        j = lax.broadcasted_iota(jnp.int32, (TM, LANES), 1) % HD
        cs, sn = jnp.cos(ang), jnp.sin(ang)
        c_ref[...] = jnp.where(j < ROT, cs, 1.0)
        s1_ref[...] = jnp.where(j < ROT // 2, -sn, 0.0)
        s2_ref[...] = jnp.where((j >= ROT // 2) & (j < ROT), sn, 0.0)

    n = BL * SEQ // TM
    return pl.pallas_call(
        body, name="rope_tables", grid=(n,),
        in_specs=[pl.BlockSpec((TM, 1), lambda i: (i, 0)), pl.BlockSpec((1, LANES), lambda i: (0, 0))],
        out_specs=[pl.BlockSpec((TM, LANES), lambda i: (i, 0))] * 3,
        out_shape=[jax.ShapeDtypeStruct((BL * SEQ, LANES), f32)] * 3,
    )(pos_col, inv_lane)


def _attn_in(x, mod, g_pre, w_in, tc, ts1, ts2):
    def body(x_ref, mod_ref, g_ref, w_ref, c_ref, s1_ref, s2_ref,
             h_ref, qa_ref, ka_ref, va_ref, q1_ref, k1_ref, v1_ref, q4_ref, k4_ref, v4_ref, q16_ref, k16_ref, v16_ref,
             scr):
        xn, _ = _rms(x_ref[...])
        h = (xn * g_ref[...]) * (1.0 + mod_ref[1:2, :]) + mod_ref[0:1, :]
        hb = h.astype(bf16)
        h_ref[...] = hb
        proj = _dot(hb, w_ref[...])
        c, s1, s2 = c_ref[...], s1_ref[...], s2_ref[...]
        o1, o2, o3, o4, o5 = AQ, AQ + AKV, AQ + 2 * AKV, AQ + 2 * AKV + BW, AQ + 2 * AKV + 2 * BW
        qa_ref[...] = (_rope(proj[:, :o1], c, s1, s2) * 0.125).astype(bf16)
        ka_ref[...] = _rope(proj[:, o1:o2], c, s1, s2).astype(bf16)
        va_ref[...] = proj[:, o2:o3].astype(bf16)
        qb = _rope(proj[:, o3:o4], c, s1, s2) * 0.125
        kb = _rope(proj[:, o4:o5], c, s1, s2)
        vb = proj[:, o5:]
        for val, r1, r4, r16 in ((qb, q1_ref, q4_ref, q16_ref), (kb, k1_ref, k4_ref, k16_ref), (vb, v1_ref, v4_ref, v16_ref)):
            r1[...] = val.astype(bf16)
            _perm_store(val, scr, r4, 4)
            _perm_store(val, scr, r16, 16)

    nat = lambda w: jax.ShapeDtypeStruct((BL, SEQ, w), bf16)
    p4 = jax.ShapeDtypeStruct((BL, 4, SEQ // 4, BW), bf16)
    p16 = jax.ShapeDtypeStruct((BL, 16, SEQ // 16, BW), bf16)
    return pl.pallas_call(
        body, name="attn_in", grid=(BL, NJ),
        in_specs=[_tok(D), MOD_SPEC, _full((1, D)), _full((D, INW)), _tok(LANES), _tok(LANES), _tok(LANES)],
        out_specs=[_tok(D), _tok(AQ), _tok(AKV), _tok(AKV)] + [_tok(BW)] * 3 + [_perm_spec(4, BW)] * 3 + [_perm_spec(16, BW)] * 3,
        out_shape=[nat(D), nat(AQ), nat(AKV), nat(AKV)] + [nat(BW)] * 3 + [p4] * 3 + [p16] * 3,
        scratch_shapes=[pltpu.VMEM((BW // LANES, TM, LANES), f32)],
        compiler_params=_cp(("arbitrary", "arbitrary")),
    )(x, mod, g_pre, w_in, tc, ts1, ts2)


def _kv_cat(cur_ref, prev_ref, p, gqa, cache):
    def one(ref):
        if not gqa:
            return ref[:, LANES * p:LANES * (p + 1)]
        k = ref[...]
        kr = pltpu.roll(k, HD, 1)
        lo = lax.broadcasted_iota(jnp.int32, k.shape, 1) < HD
        return jnp.where(lo, k, kr) if p < 2 else jnp.where(lo, kr, k)

    key = (id(cur_ref), p // 2 if gqa else p)
    if key not in cache:
        cache[key] = one(cur_ref) if prev_ref is None else jnp.concatenate([one(prev_ref), one(cur_ref)], axis=0)
    return cache[key]


def _lane_half(a, hh):
    lo = lax.broadcasted_iota(jnp.int32, a.shape, 1) < HD
    return jnp.where(lo, a, jnp.zeros_like(a)) if hh == 0 else jnp.where(lo, jnp.zeros_like(a), a)


def _attn_fwd(q, k, v, sink, *, max_dist, o_dtype, name):
    n, l, w = q.shape
    wk = k.shape[-1]
    nb = l // BLK
    gqa = wk != w
    has_sink = sink is not None

    def body(*refs):
        if has_sink:
            sink_ref, refs = refs[0], refs[1:]
        if nb > 1:
            q_ref, kc_ref, kp_ref, vc_ref, vp_ref, o_ref, lse_ref, sscr, pscr, dscr = refs
        else:
            q_ref, kc_ref, vc_ref, o_ref, lse_ref, sscr, pscr, dscr = refs
        i = pl.program_id(1)
        qi = lax.broadcasted_iota(jnp.int32, (BLK, BLK), 0)
        kj = lax.broadcasted_iota(jnp.int32, (BLK, BLK), 1)
        tri = kj <= qi
        eye = kj == qi
        cache = {}
        for p in range(w // LANES):
            qpair = q_ref[:, LANES * p:LANES * (p + 1)]
            kcat = _kv_cat(kc_ref, kp_ref if nb > 1 else None, p, gqa, cache)
            for hh in range(2):
                s = _dot_nt(_lane_half(qpair, hh), kcat)
                if nb > 1:
                    sp = jnp.where(i > 0, s[:, :BLK], NEG)
                    sscr[2 * p + hh] = jnp.where(tri, s[:, BLK:], sp)
                    if diag:
                        dscr[2 * p + hh] = jnp.where(eye, sp, NEG)
                else:
                    sscr[2 * p + hh] = jnp.where(tri, s, NEG)
        lane = lax.broadcasted_iota(jnp.int32, (BLK, LANES), 1)
        lse_all = jnp.zeros((BLK, LANES), f32)
        for p in range(w // LANES):
            for hh in range(2):
                h = 2 * p + hh
                comb = sscr[h]
                if diag:
                    dtile = dscr[h]
                    m = jnp.max(jnp.maximum(comb, dtile), axis=-1, keepdims=True)
                else:
                    m = jnp.max(comb, axis=-1, keepdims=True)
                if has_sink:
                    sk = sink_ref[0, h]
                    m = jnp.maximum(m, sk)
                e = jnp.exp(comb - m)
                if diag:
                    ed = jnp.exp(dtile - m)
                    den = jnp.sum(e + ed, axis=-1, keepdims=True)
                else:
                    den = jnp.sum(e, axis=-1, keepdims=True)
                if has_sink:
                    den = den + jnp.exp(sk - m)
                inv = 1.0 / den
                if nb > 1:
                    pscr[h, :, :BLK] = (jnp.where(tri, ed if diag else 0.0, e) * inv).astype(bf16)
                    pscr[h, :, BLK:] = (jnp.where(tri, e, 0.0) * inv).astype(bf16)
                else:
                    pscr[h] = (e * inv).astype(bf16)
                lse_all = jnp.where(lane == h, jnp.broadcast_to(m + jnp.log(den), (BLK, LANES)), lse_all)
        lse_ref[...] = lse_all
        for p in range(w // LANES):
            vcat = _kv_cat(vc_ref, vp_ref if nb > 1 else None, p, gqa, cache)
            key = ("halves", id(vc_ref), p // 2 if gqa else p)
            if key not in cache:
                cache[key] = (_lane_half(vcat, 0), _lane_half(vcat, 1))
            o_ref[:, LANES * p:LANES * (p + 1)] = (_dot(pscr[2 * p], cache[key][0])
                                                   + _dot(pscr[2 * p + 1], cache[key][1])).astype(o_ref.dtype)

    assert max_dist in (BLK - 1, BLK)
    diag = nb > 1 and max_dist == BLK
    cur = lambda ww: pl.BlockSpec((None, BLK, ww), lambda a, i: (a, i, 0))
    prev = lambda ww: pl.BlockSpec((None, BLK, ww), lambda a, i: (a, jnp.maximum(i - 1, 0), 0))
    in_specs = [cur(w), cur(wk)] + ([prev(wk)] if nb > 1 else []) + [cur(wk)] + ([prev(wk)] if nb > 1 else [])
    args = [q, k] + ([k] if nb > 1 else []) + [v] + ([v] if nb > 1 else [])
    if has_sink:
        in_specs = [pl.BlockSpec(memory_space=pltpu.SMEM)] + in_specs
        args = [sink] + args
    return pl.pallas_call(
        body, name=name, grid=(n, nb), in_specs=in_specs,
        out_specs=[cur(w), cur(LANES)],
        out_shape=[jax.ShapeDtypeStruct((n, l, w), o_dtype), jax.ShapeDtypeStruct((n, l, LANES), f32)],
        scratch_shapes=[pltpu.VMEM((w // HD, BLK, BLK), f32), pltpu.VMEM((w // HD, BLK, 2 * BLK if nb > 1 else BLK), bf16),
                        pltpu.VMEM((w // HD if diag else 1, BLK, BLK), f32)],
        compiler_params=_cp(("arbitrary", "arbitrary")),
    )(*args)


def _attn_bwd(q, k, v, do, delta, lse, sink, *, max_dist, name):
    n, l, w = q.shape
    wk = k.shape[-1]
    nb = l // BLK
    gqa = wk != w
    has_sink = sink is not None

    def body(*refs):
        if has_sink:
            sink_ref, refs = refs[0], refs[1:]
        if nb > 1:
            q_ref, kc_ref, kp_ref, vc_ref, vp_ref, do_ref, delta_ref, lse_ref = refs[:8]
            rest = refs[8:]
        else:
            q_ref, kc_ref, vc_ref, do_ref, delta_ref, lse_ref = refs[:6]
            rest = refs[6:]
        if has_sink:
            dq_ref, dk_ref, dv_ref, dsink_ref = rest[:4]
            rest = rest[4:]
        else:
            dq_ref, dk_ref, dv_ref = rest[:3]
            rest = rest[3:]
        step = pl.program_id(1)
        blk_idx = nb - 1 - step
        if nb > 1:
            ck, cv = rest[:2]
            rest = rest[2:]

            @pl.when(step == 0)
            def _():
                ck[...] = jnp.zeros_like(ck)
                cv[...] = jnp.zeros_like(cv)

        sscr, dpscr, pscr, dsscr = rest[:4]
        if diag:
            dscr, ddscr = rest[4:]
        if has_sink:
            @pl.when((pl.program_id(0) == 0) & (step == 0))
            def _():
                dsink_ref[...] = jnp.zeros_like(dsink_ref)

        lane = lax.broadcasted_iota(jnp.int32, (BLK, LANES), 1)
        lo = lane < HD
        qi = lax.broadcasted_iota(jnp.int32, (BLK, BLK), 0)
        kj = lax.broadcasted_iota(jnp.int32, (BLK, BLK), 1)
        tri = kj <= qi
        eye = kj == qi
        cache = {}
        kp, vp = (kp_ref, vp_ref) if nb > 1 else (None, None)
        rows = 2 * BLK if nb > 1 else BLK
        for p in range(w // LANES):
            sl = slice(LANES * p, LANES * (p + 1))
            qpair, dopair = q_ref[:, sl], do_ref[:, sl]
            kcat, vcat = _kv_cat(kc_ref, kp, p, gqa, cache), _kv_cat(vc_ref, vp, p, gqa, cache)
            for hh in range(2):
                h = 2 * p + hh
                s = _dot_nt(_lane_half(qpair, hh), kcat)
                dp = _dot_nt(_lane_half(dopair, hh), vcat)
                if nb > 1:
                    sp = jnp.where(blk_idx > 0, s[:, :BLK], NEG)
                    sscr[h] = jnp.where(tri, s[:, BLK:], sp)
                    dpscr[h] = jnp.where(tri, dp[:, BLK:], dp[:, :BLK])
                    if diag:
                        dscr[h] = jnp.where(eye, sp, NEG)
                        ddscr[h] = dp[:, :BLK]
                else:
                    sscr[h] = jnp.where(tri, s, NEG)
                    dpscr[h] = dp
        for p in range(w // LANES):
            for hh in range(2):
                h = 2 * p + hh
                lse_b = jnp.broadcast_to(lse_ref[:, h:h + 1], (BLK, BLK))
                delta = jnp.broadcast_to(delta_ref[:, h:h + 1], (BLK, BLK))
                pr = jnp.exp(sscr[h] - lse_b)
                ds = pr * (dpscr[h] - delta)
                if nb > 1:
                    if diag:
                        prd = jnp.exp(dscr[h] - lse_b)
                        dsd = prd * (ddscr[h] - delta)
                    else:
                        prd = dsd = 0.0
                    pscr[h, :, :BLK] = jnp.where(tri, prd, pr).astype(bf16)
                    pscr[h, :, BLK:] = jnp.where(tri, pr, 0.0).astype(bf16)
                    dsscr[h, :, :BLK] = jnp.where(tri, dsd, ds).astype(bf16)
                    dsscr[h, :, BLK:] = jnp.where(tri, ds, 0.0).astype(bf16)
                else:
                    pscr[h] = pr.astype(bf16)
                    dsscr[h] = ds.astype(bf16)
                if has_sink:
                    dsk = -jnp.sum(jnp.where(lane == 0, jnp.exp(sink_ref[0, h] - lse_b) * delta, 0.0), keepdims=True)
                    dsink_ref[h:h + 1, :] += jnp.broadcast_to(dsk, (1, LANES))
        gk = [jnp.zeros((rows, LANES), f32), jnp.zeros((rows, LANES), f32)]
        gv = [jnp.zeros((rows, LANES), f32), jnp.zeros((rows, LANES), f32)]
        for p in range(w // LANES):
            sl = slice(LANES * p, LANES * (p + 1))
            qpair, dopair = q_ref[:, sl], do_ref[:, sl]
            kcat = _kv_cat(kc_ref, kp, p, gqa, cache)
            key = ("halves", p // 2 if gqa else p)
            if key not in cache:
                cache[key] = (_lane_half(kcat, 0), _lane_half(kcat, 1))
            dq_ref[:, sl] = _dot(dsscr[2 * p], cache[key][0]) + _dot(dsscr[2 * p + 1], cache[key][1])
            dk_pair = _dot_tn(dsscr[2 * p], _lane_half(qpair, 0)) + _dot_tn(dsscr[2 * p + 1], _lane_half(qpair, 1))
            dv_pair = _dot_tn(pscr[2 * p], _lane_half(dopair, 0)) + _dot_tn(pscr[2 * p + 1], _lane_half(dopair, 1))
            if gqa:
                gk[p // 2] = gk[p // 2] + dk_pair
                gv[p // 2] = gv[p // 2] + dv_pair
            elif nb > 1:
                dk_ref[:, sl] = dk_pair[BLK:] + ck[:, sl]
                dv_ref[:, sl] = dv_pair[BLK:] + cv[:, sl]
                ck[:, sl] = dk_pair[:BLK]
                cv[:, sl] = dv_pair[:BLK]
            else:
                dk_ref[:, sl] = dk_pair
                dv_ref[:, sl] = dv_pair
        if gqa:
            lor = lax.broadcasted_iota(jnp.int32, (rows, LANES), 1) < HD
            fold = lambda g: jnp.where(lor, g[0] + pltpu.roll(g[0], HD, 1), g[1] + pltpu.roll(g[1], HD, 1))
            dk_full, dv_full = fold(gk), fold(gv)
            dk_ref[...] = dk_full[BLK:] + ck[...]
            dv_ref[...] = dv_full[BLK:] + cv[...]
            ck[...] = dk_full[:BLK]
            cv[...] = dv_full[:BLK]

    assert max_dist in (BLK - 1, BLK)
    diag = nb > 1 and max_dist == BLK
    cur = lambda ww: pl.BlockSpec((None, BLK, ww), lambda a, i: (a, nb - 1 - i, 0))
    prev = lambda ww: pl.BlockSpec((None, BLK, ww), lambda a, i: (a, jnp.maximum(nb - 2 - i, 0), 0))
    in_specs = ([cur(w), cur(wk)] + ([prev(wk)] if nb > 1 else []) + [cur(wk)] + ([prev(wk)] if nb > 1 else [])
                + [cur(w), cur(LANES), cur(LANES)])
    args = [q, k] + ([k] if nb > 1 else []) + [v] + ([v] if nb > 1 else []) + [do, delta, lse]
    out_specs = [cur(w), cur(wk), cur(wk)]
    out_shape = [jax.ShapeDtypeStruct((n, l, w), f32), jax.ShapeDtypeStruct((n, l, wk), f32), jax.ShapeDtypeStruct((n, l, wk), f32)]
    if has_sink:
        in_specs = [pl.BlockSpec(memory_space=pltpu.SMEM)] + in_specs
        args = [sink] + args
        out_specs.append(pl.BlockSpec((8, LANES), lambda a, i: (0, 0)))
        out_shape.append(jax.ShapeDtypeStruct((8, LANES), f32))
    nh = w // HD
    scratch = [pltpu.VMEM((BLK, wk), f32), pltpu.VMEM((BLK, wk), f32)] if nb > 1 else []
    scratch += [pltpu.VMEM((nh, BLK, BLK), f32)] * 2 + [pltpu.VMEM((nh, BLK, 2 * BLK if nb > 1 else BLK), bf16)] * 2
    if diag:
        scratch += [pltpu.VMEM((nh, BLK, BLK), f32)] * 2
    return pl.pallas_call(
        body, name=name, grid=(n, nb), in_specs=in_specs, out_specs=out_specs, out_shape=out_shape,
        scratch_shapes=scratch, compiler_params=_cp(("arbitrary", "arbitrary")),
    )(*args)


def _split2(x):
    hi = x.astype(bf16)
    return hi, (x - hi.astype(f32)).astype(bf16)


def _heads_to_lanes(xc, e):
    return sum(_dot(t, e) for t in _split2(xc))


def _lanes_to_heads(x, g):
    return sum(_dot(t, g) for t in _split2(x))


HEAD_EXPAND = (np.arange(LANES)[:, None] == np.arange(BW)[None, :] // HD).astype(np.float32)
HEAD_SUM = HEAD_EXPAND.T.copy()


def _branch_weights(l1_ref, l4_ref, l16_ref, scr):
    l4v = _perm_load(l4_ref, scr, 4)
    l16v = _perm_load(l16_ref, scr, 16)
    l1v = l1_ref[...]
    m = jnp.maximum(jnp.maximum(l1v, l4v), l16v)
    e1, e4, e16 = jnp.exp(l1v - m), jnp.exp(l4v - m), jnp.exp(l16v - m)
    z = e1 + e4 + e16
    return e1 / z, e4 / z, e16 / z


def _mix_out(oa, o1, l1, o4, l4, o16, l16, g_mix_a, g_mix_b, w_out, x, mod, g_post):
    def body(oa_ref, o1_ref, l1_ref, o4_ref, l4_ref, o16_ref, l16_ref, ga_ref, gb_ref, w_ref, x_ref, mod_ref, gp_ref, e_ref,
             x1_ref, y_ref, mixed_ref, ob_ref, scr):
        w1, w4, w16 = _branch_weights(l1_ref, l4_ref, l16_ref, scr)
        e = e_ref[...]
        x1w, x4w = _heads_to_lanes(w1, e), _heads_to_lanes(w4, e)
        ob = (x1w * o1_ref[...].astype(f32) + x4w * _perm_load(o4_ref, scr, 4)
              + (1.0 - x1w - x4w) * _perm_load(o16_ref, scr, 16))
        ob_ref[...] = ob
        oan, _ = _rms(oa_ref[...])
        obn, _ = _rms(ob)
        mixed = jnp.concatenate([oan * ga_ref[...], obn * gb_ref[...]], axis=1).astype(bf16)
        mixed_ref[...] = mixed
        y = _dot(mixed, w_ref[...])
        y_ref[...] = y
        yn, _ = _rms(y)
        x1_ref[...] = x_ref[...] + mod_ref[2:3, :] * (yn * gp_ref[...])

    nat = lambda w, dt: jax.ShapeDtypeStruct((BL, SEQ, w), dt)
    return pl.pallas_call(
        body, name="mix_out", grid=(BL, NJ),
        in_specs=[_tok(AQ), _tok(BW), _tok(LANES), _perm_spec(4, BW), _perm_spec(4, LANES), _perm_spec(16, BW),
                  _perm_spec(16, LANES), _full((1, AQ)), _full((1, BW)), _full((D, D)), _tok(D), MOD_SPEC, _full((1, D)),
                  _full((LANES, BW))],
        out_specs=[_tok(D), _tok(D), _tok(D), _tok(BW)],
        out_shape=[nat(D, f32), nat(D, f32), nat(D, bf16), nat(BW, f32)],
        scratch_shapes=[pltpu.VMEM((BW // LANES, TM, LANES), f32)],
        compiler_params=_cp(("arbitrary", "arbitrary")),
    )(oa, o1, l1, o4, l4, o16, l16, g_mix_a, g_mix_b, w_out, x, mod, g_post, jnp.asarray(HEAD_EXPAND, bf16))


def _mlp_up(x1, mod, g_pre, w_up):
    def body(x_ref, mod_ref, g_ref, w_ref, h_ref, u_ref, a_ref):
        xn, _ = _rms(x_ref[...])
        h = (xn * g_ref[...]) * (1.0 + mod_ref[4:5, :]) + mod_ref[3:4, :]
        hb = h.astype(bf16)
        h_ref[...] = hb
        for s in range(NCHIP):
            u = _dot(hb, w_ref[s])
            u_ref[:, D * s:D * (s + 1)] = u.astype(bf16)
            a_ref[:, D * s:D * (s + 1)] = jnp.square(jnp.maximum(u, 0.0)).astype(bf16)

    nat = lambda w: jax.ShapeDtypeStruct((BL, SEQ, w), bf16)
    return pl.pallas_call(
        body, name="mlp_up", grid=(BL, NJ),
        in_specs=[_tok(D), MOD_SPEC, _full((1, D)), _full((NCHIP, D, D))],
        out_specs=[_tok(D), _tok(DFF), _tok(DFF)], out_shape=[nat(D), nat(DFF), nat(DFF)],
        compiler_params=_cp(("arbitrary", "arbitrary")),
    )(x1, mod, g_pre, w_up)


def _mlp_down(a, w_down, x1, target, mod, g_post):
    def body(a_ref, w_ref, x_ref, t_ref, mod_ref, g_ref, gx_ref, dy_ref, accb_ref, accg_ref):
        _acc_init(accb_ref, accg_ref)
        y2 = _dot(a_ref[...], w_ref[...])
        yn, r = _rms(y2)
        g = g_ref[...]
        gt = mod_ref[5:6, :]
        n2 = yn * g
        err = x_ref[...] + gt * n2 - t_ref[...]
        gout = err * (1.0 / D)
        gx_ref[...] = gout
        dn2 = gout * gt
        dy_ref[...] = _rms_bwd(dn2 * g, yn, r).astype(bf16)
        accb_ref[0:1, :] += _colsum(gout * n2)
        accg_ref[0:1, :] += _colsum(dn2 * yn)
        accg_ref[1:2, :] += jnp.broadcast_to(jnp.sum(err * err, keepdims=True), (1, D))

    return pl.pallas_call(
        body, name="mlp_down", grid=(BL, NJ),
        in_specs=[_tok(DFF), _full((DFF, D)), _tok(D), _tok(D), MOD_SPEC, _full((1, D))],
        out_specs=[_tok(D), _tok(D), ACCB_SPEC, ACCG_SPEC],
        out_shape=[jax.ShapeDtypeStruct((BL, SEQ, D), f32), jax.ShapeDtypeStruct((BL, SEQ, D), bf16)] + ACC_SHAPES,
        compiler_params=_cp(("arbitrary", "arbitrary")),
    )(a, w_down, x1, target, mod, g_post)


def _mlp_bwd(dy2, u, w_down, w_up, x1, gx, mod, g_pre):
    def body(dy_ref, u_ref, wd_hbm, wu_hbm, x_ref, gx_ref, mod_ref, g_ref, du_ref, gx1_ref, accb_ref, accg_ref, wd, wu, sem):
        _acc_init(accb_ref, accg_ref)

        @pl.when((pl.program_id(0) == 0) & (pl.program_id(1) == 0))
        def _():
            c1 = pltpu.make_async_copy(wd_hbm, wd, sem.at[0])
            c2 = pltpu.make_async_copy(wu_hbm, wu, sem.at[1])
            c1.start()
            c2.start()
            c1.wait()
            c2.wait()

        dy = dy_ref[...]
        dh = jnp.zeros((TM, D), f32)
        for s in range(NCHIP):
            sl = slice(D * s, D * (s + 1))
            da = _dot_nt(dy, wd[sl, :])
            du = (da * (2.0 * jnp.maximum(u_ref[:, sl].astype(f32), 0.0))).astype(bf16)
            du_ref[:, sl] = du
            dh = dh + _dot_nt(du, wu[s])
        xn, r = _rms(x_ref[...])
        g = g_ref[...]
        n = xn * g
        dn = dh * (1.0 + mod_ref[4:5, :])
        gx1_ref[...] = gx_ref[...] + _rms_bwd(dn * g, xn, r)
        accb_ref[0:1, :] += _colsum(dh * n)
        accb_ref[1:2, :] += _colsum(dh)
        accg_ref[0:1, :] += _colsum(dn * xn)

    anyspec = pl.BlockSpec(memory_space=pl.ANY)
    return pl.pallas_call(
        body, name="mlp_bwd", grid=(BL, NJ),
        in_specs=[_tok(D), _tok(DFF), anyspec, anyspec, _tok(D), _tok(D), MOD_SPEC, _full((1, D))],
        out_specs=[_tok(DFF), _tok(D), ACCB_SPEC, ACCG_SPEC],
        out_shape=[jax.ShapeDtypeStruct((BL, SEQ, DFF), bf16), jax.ShapeDtypeStruct((BL, SEQ, D), f32)] + ACC_SHAPES,
        scratch_shapes=[pltpu.VMEM((DFF, D), bf16), pltpu.VMEM((NCHIP, D, D), bf16), pltpu.SemaphoreType.DMA((2,))],
        compiler_params=_cp(("arbitrary", "arbitrary")),
    )(dy2, u, w_down, w_up, x1, gx, mod, g_pre)


def _matmul_tn(a, b, *, tn, col_blocked, name):
    t, m = a.shape
    n = b.shape[1]
    tmm = min(m, 1024)
    tk = 2048 if tn <= 1024 else 1024
    nk = t // tk

    def body(a_ref, b_ref, o_ref):
        @pl.when(pl.program_id(2) == 0)
        def _():
            o_ref[...] = jnp.zeros_like(o_ref)

        o_ref[...] += _dot_tn(a_ref[...], b_ref[...])

    if col_blocked:
        out_spec = pl.BlockSpec((None, tmm, tn), lambda i, j, k: (j, i, 0))
        out_shape = jax.ShapeDtypeStruct((n // tn, m, tn), f32)
    else:
        out_spec = pl.BlockSpec((tmm, tn), lambda i, j, k: (i, j))
        out_shape = jax.ShapeDtypeStruct((m, n), f32)
    return pl.pallas_call(
        body, name=name, grid=(m // tmm, n // tn, nk),
        in_specs=[pl.BlockSpec((tk, tmm), lambda i, j, k: (k, i)), pl.BlockSpec((tk, tn), lambda i, j, k: (k, j))],
        out_specs=out_spec, out_shape=out_shape,
        compiler_params=_cp(("arbitrary", "arbitrary", "arbitrary")),
    )(a, b)


def _grad_w_in(h, dproj):
    t = h.shape[0]
    tk = 1024
    nk = t // tk
    sw = INW // NCHIP

    def body(a_ref, b_ref, o_ref, acc):
        k = pl.program_id(0)

        @pl.when(k == 0)
        def _():
            acc[...] = jnp.zeros_like(acc)

        acc[...] += _dot_tn(a_ref[...], b_ref[...])

        @pl.when(k == nk - 1)
        def _():
            for s in range(NCHIP):
                o_ref[s] = acc[:, sw * s:sw * (s + 1)]

    return pl.pallas_call(
        body, name="grad_w_in", grid=(nk,),
        in_specs=[pl.BlockSpec((tk, D), lambda k: (k, 0)), pl.BlockSpec((tk, INW), lambda k: (k, 0))],
        out_specs=pl.BlockSpec((NCHIP, D, sw), lambda k: (0, 0, 0)), out_shape=jax.ShapeDtypeStruct((NCHIP, D, sw), f32),
        scratch_shapes=[pltpu.VMEM((D, INW), f32)], compiler_params=_cp(("arbitrary",)),
    )(h, dproj)


def _attn_out_bwd(gx1, y, mod, g_post, w_out, oa, ob, g_mix_a, g_mix_b, l1, l4, l16):
    def body(gx_ref, y_ref, mod_ref, gp_ref, w_ref, oa_ref, ob_ref, ga_ref, gb_ref, l1_ref, l4_ref, l16_ref, e_ref, g_ref,
             dy_ref, doa_ref, do1_ref, do4_ref, do16_ref, da_ref, d1_ref, d4_ref, d16_ref, accb_ref, accg_ref, scr):
        _acc_init(accb_ref, accg_ref)
        w1, w4, w16 = _branch_weights(l1_ref, l4_ref, l16_ref, scr)
        e, hs = e_ref[...], g_ref[...]
        gx1v = gx_ref[...]
        yn, ry = _rms(y_ref[...])
        gp = gp_ref[...]
        gt = mod_ref[2:3, :]
        dn1 = gx1v * gt
        dy = _rms_bwd(dn1 * gp, yn, ry).astype(bf16)
        dy_ref[...] = dy
        dmixed = _dot_nt(dy, w_ref[...])
        dma, dmb = dmixed[:, :AQ], dmixed[:, AQ:]
        oa, ob = oa_ref[...], ob_ref[...]
        oan, ra = _rms(oa)
        obn, rb = _rms(ob)
        doa = _rms_bwd(dma * ga_ref[...], oan, ra)
        doa_ref[...] = doa.astype(bf16)
        da_ref[...] = _lanes_to_heads(doa * oa, hs)
        dob = _rms_bwd(dmb * gb_ref[...], obn, rb)
        dd = _lanes_to_heads(dob * ob, hs)
        x1w, x4w = _heads_to_lanes(w1, e), _heads_to_lanes(w4, e)
        do1_ref[...] = (x1w * dob).astype(bf16)
        d1_ref[...] = w1 * dd
        _perm_store(x4w * dob, scr, do4_ref, 4)
        _perm_store(w4 * dd, scr, d4_ref, 4)
        _perm_store((1.0 - x1w - x4w) * dob, scr, do16_ref, 16)
        _perm_store(w16 * dd, scr, d16_ref, 16)
        accb_ref[0:1, :] += _colsum(gx1v * (yn * gp))
        accg_ref[0:1, :] += _colsum(dn1 * yn)
        accg_ref[1:2, :] += jnp.concatenate([_colsum(dma * oan), _colsum(dmb * obn)], axis=1)

    nat = lambda w, dt: jax.ShapeDtypeStruct((BL, SEQ, w), dt)
    return pl.pallas_call(
        body, name="attn_out_bwd", grid=(BL, NJ),
        in_specs=[_tok(D), _tok(D), MOD_SPEC, _full((1, D)), _full((D, D)), _tok(AQ), _tok(BW), _full((1, AQ)), _full((1, BW)),
                  _tok(LANES), _perm_spec(4, LANES), _perm_spec(16, LANES), _full((LANES, BW)), _full((BW, LANES))],
        out_specs=[_tok(D), _tok(AQ), _tok(BW), _perm_spec(4, BW), _perm_spec(16, BW),
                   _tok(LANES), _tok(LANES), _perm_spec(4, LANES), _perm_spec(16, LANES), ACCB_SPEC, ACCG_SPEC],
        out_shape=[nat(D, bf16), nat(AQ, bf16), nat(BW, bf16), jax.ShapeDtypeStruct((BL, 4, SEQ // 4, BW), bf16),
                   jax.ShapeDtypeStruct((BL, 16, SEQ // 16, BW), bf16), nat(LANES, f32), nat(LANES, f32),
                   jax.ShapeDtypeStruct((BL, 4, SEQ // 4, LANES), f32), jax.ShapeDtypeStruct((BL, 16, SEQ // 16, LANES), f32)]
                  + ACC_SHAPES,
        scratch_shapes=[pltpu.VMEM((BW // LANES, TM, LANES), f32)],
        compiler_params=_cp(("arbitrary", "arbitrary")),
    )(gx1, y, mod, g_post, w_out, oa, ob, g_mix_a, g_mix_b, l1, l4, l16, jnp.asarray(HEAD_EXPAND, bf16),
      jnp.asarray(HEAD_SUM, bf16))


def _attn_in_bwd(dqa, dka, dva, d1, d4, d16, tc, ts1, ts2, w_in, x, gx1, mod, g_pre):
    def body(dqa_ref, dka_ref, dva_ref, dq1_ref, dk1_ref, dv1_ref, dq4_ref, dk4_ref, dv4_ref, dq16_ref, dk16_ref, dv16_ref,
             c_ref, s1_ref, s2_ref, w_ref, x_ref, gx_ref, mod_ref, g_ref, dproj_ref, dx_ref, accb_ref, accg_ref, scr):
        _acc_init(accb_ref, accg_ref)
        c, s1, s2 = c_ref[...], s1_ref[...], s2_ref[...]
        tot = lambda r1, r4, r16: r1[...] + _perm_load(r4, scr, 4) + _perm_load(r16, scr, 16)
        dqb = tot(dq1_ref, dq4_ref, dq16_ref)
        dkb = tot(dk1_ref, dk4_ref, dk16_ref)
        dvb = tot(dv1_ref, dv4_ref, dv16_ref)
        dproj = jnp.concatenate([
            _rope_t(dqa_ref[...], c, s1, s2) * 0.125, _rope_t(dka_ref[...], c, s1, s2), dva_ref[...],
            _rope_t(dqb, c, s1, s2) * 0.125, _rope_t(dkb, c, s1, s2), dvb], axis=1).astype(bf16)
        dproj_ref[...] = dproj
        dh = _dot_nt(dproj, w_ref[...])
        xn, r = _rms(x_ref[...])
        g = g_ref[...]
        dn = dh * (1.0 + mod_ref[1:2, :])
        dx_ref[...] = gx_ref[...] + _rms_bwd(dn * g, xn, r)
        accb_ref[0:1, :] += _colsum(dh * (xn * g))
        accb_ref[1:2, :] += _colsum(dh)
        accg_ref[0:1, :] += _colsum(dn * xn)

    return pl.pallas_call(
        body, name="attn_in_bwd", grid=(BL, NJ),
        in_specs=[_tok(AQ), _tok(AKV), _tok(AKV)] + [_tok(BW)] * 3 + [_perm_spec(4, BW)] * 3 + [_perm_spec(16, BW)] * 3
                 + [_tok(LANES)] * 3 + [_full((D, INW)), _tok(D), _tok(D), MOD_SPEC, _full((1, D))],
        out_specs=[_tok(INW), _tok(D), ACCB_SPEC, ACCG_SPEC],
        out_shape=[jax.ShapeDtypeStruct((BL, SEQ, INW), bf16), jax.ShapeDtypeStruct((BL, SEQ, D), f32)] + ACC_SHAPES,
        scratch_shapes=[pltpu.VMEM((BW // LANES, TM, LANES), f32)],
        compiler_params=_cp(("arbitrary", "arbitrary")),
    )(dqa, dka, dva, *d1, *d4, *d16, tc, ts1, ts2, w_in, x, gx1, mod, g_pre)


def _inv_lane():
    inv = np.float32(THETA) ** (-np.arange(0, ROT, 2, dtype=np.float32) / np.float32(ROT))
    lane = np.arange(LANES) % HD
    return jnp.asarray(np.where(lane < ROT, inv[lane % (ROT // 2)], 0.0).astype(np.float32)[None, :])


def _local_step(x, positions, mod, target, inv_lane, first_weight, later_weights, grad_ready, grad_reduce, g_attn_pre,
                g_attn_post, sink_a, g_mix_a, g_mix_b, g_mlp_pre, g_mlp_post):
    tabs = _rope_tables(positions.reshape(BL * SEQ, 1), inv_lane)
    w_in = first_weight(tuple(tabs))
    tc, ts1, ts2 = [t.reshape(BL, SEQ, LANES) for t in tabs]

    (h, qa, ka, va, q1, k1, v1, q4, k4, v4, q16, k16, v16) = _attn_in(x, mod, g_attn_pre, w_in, tc, ts1, ts2)
    seqs = lambda t: t.reshape(t.shape[0] * t.shape[1], t.shape[2], t.shape[3])
    q4, k4, v4, q16, k16, v16 = [seqs(t) for t in (q4, k4, v4, q16, k16, v16)]
    oa, la = _attn_fwd(qa, ka, va, sink_a, max_dist=BLK - 1, o_dtype=f32, name="attn_a_fwd")
    o1, l1 = _attn_fwd(q1, k1, v1, None, max_dist=BLK, o_dtype=bf16, name="attn_b1_fwd")
    o4, l4 = _attn_fwd(q4, k4, v4, None, max_dist=BLK, o_dtype=bf16, name="attn_b4_fwd")
    o16, l16 = _attn_fwd(q16, k16, v16, None, max_dist=BLK, o_dtype=bf16, name="attn_b16_fwd")
    b4 = lambda t: t.reshape(BL, 4, SEQ // 4, t.shape[-1])
    b16 = lambda t: t.reshape(BL, 16, SEQ // 16, t.shape[-1])
    w_out, mlp_weights, mod = later_weights((oa, o1, o4, o16), mod)
    x1, y, mixed, ob = _mix_out(oa, o1, l1, b4(o4), b4(l4), b16(o16), b16(l16), g_mix_a, g_mix_b, w_out, x, mod, g_attn_post)
    w_up, w_down = mlp_weights((x1,))
    h2, u, a = _mlp_up(x1, mod, g_mlp_pre, w_up)
    gx, dy2, accb_d, accg_d = _mlp_down(a, w_down, x1, target, mod, g_mlp_post)

    flat = lambda t: t.reshape(BL * SEQ, t.shape[-1])
    mod = grad_ready("w_down", _matmul_tn(flat(a), flat(dy2), tn=D, col_blocked=False, name="grad_w_down"), mod)
    du, gx1, accb_m, accg_m = _mlp_bwd(dy2, u, w_down, w_up, x1, gx, mod, g_mlp_pre)
    mod = grad_reduce("w_down", (gx1,), mod)
    mod = grad_ready("w_up", _matmul_tn(flat(h2), flat(du), tn=D, col_blocked=True, name="grad_w_up"), mod)

    dy, doa, do1, do4, do16, da, dl1, dl4, dl16, accb_o, accg_o = _attn_out_bwd(
        gx1, y, mod, g_attn_post, w_out, oa, ob, g_mix_a, g_mix_b, l1, b4(l4), b16(l16))
    sink_behind = grad_reduce("w_up", (dy,), sink_a)
    gw_out = _matmul_tn(flat(mixed), flat(dy), tn=D, col_blocked=False, name="grad_w_out")
    dqa, dka, dva, dsink = _attn_bwd(qa, ka, va, doa, da, la, sink_behind, max_dist=BLK - 1, name="attn_a_bwd")
    d1 = _attn_bwd(q1, k1, v1, do1, dl1, l1, None, max_dist=BLK, name="attn_b1_bwd")
    d4 = _attn_bwd(q4, k4, v4, seqs(do4), seqs(dl4), l4, None, max_dist=BLK, name="attn_b4_bwd")
    d16 = _attn_bwd(q16, k16, v16, seqs(do16), seqs(dl16), l16, None, max_dist=BLK, name="attn_b16_bwd")
    dproj, grad_x, accb_i, accg_i = _attn_in_bwd(dqa, dka, dva, d1, [b4(t) for t in d4], [b16(t) for t in d16],
                                                 tc, ts1, ts2, w_in, x, gx1, mod, g_attn_pre)
    gw_in = _grad_w_in(flat(h), flat(dproj))
    dsink = grad_ready("w_in_w_out", (gw_in, gw_out), dsink)

    return grad_x, (accb_i, accb_o, accb_m, accb_d, accg_i, accg_o, accg_m, accg_d, dsink)


ADAW = NMOD * D // NCHIP


def _pos():
    return lax.axis_index("x"), lax.axis_index("y"), lax.axis_index("c")


def _flip(v, bit):
    return 1 - v if bit else v


def _all_peers(x, y, c):
    return [(_flip(x, k >> 2 & 1), _flip(y, k >> 1 & 1), _flip(c, k & 1)) for k in range(1, NDEV)]


def _other_chips(x, y):
    return [(1 - x, y), (x, 1 - y), (1 - x, 1 - y)]


def _rcopy(src, dst, send, recv, k, dev):
    return pltpu.make_async_remote_copy(src_ref=src, dst_ref=dst, send_sem=send.at[k], recv_sem=recv.at[k],
                                        device_id=dev, device_id_type=MESH)


def _gather_small(src, buf, send, recv):
    x, y, c = _pos()
    me = 4 * x + 2 * y + c
    peers = _all_peers(x, y, c)
    sends = [_rcopy(src, buf.at[me], send, recv, k, p) for k, p in enumerate(peers)]
    for cp in sends:
        cp.start()
    for k, (px, py, pc) in enumerate(peers):
        _rcopy(src, buf.at[4 * px + 2 * py + pc], send, recv, k, (px, py, pc)).wait_recv()
    for cp in sends:
        cp.wait_send()
    return me


def _ada_fwd(c_in, w_ada, b_cols):
    def body(c_ref, w_ref, b_ref, mod_ref, cond_ref, cbuf, mbuf, s1, r1, s2, r2):
        x, y, c = _pos()
        chip = 2 * x + y
        me = _gather_small(c_ref, cbuf, s1, r1)
        cbuf[me] = c_ref[...]
        for i in range(NDEV):
            cond_ref[BL * i:BL * (i + 1), :] = cbuf[i]
        call = cond_ref[...]
        cond = call / (1.0 + jnp.exp(-call))
        cond_ref[...] = cond
        mbuf[chip] = _dot(cond.astype(bf16), w_ref[...].astype(bf16)) + b_ref[...]
        chips = _other_chips(x, y)
        sends = [_rcopy(mbuf.at[chip], mbuf.at[chip], s2, r2, j, (px, py, c)) for j, (px, py) in enumerate(chips)]
        for cp in sends:
            cp.start()
        for j, (px, py) in enumerate(chips):
            _rcopy(mbuf.at[chip], mbuf.at[2 * px + py], s2, r2, j, (px, py, c)).wait_recv()
        for cp in sends:
            cp.wait_send()
        row = lax.broadcasted_iota(jnp.int32, (BL * NDEV, ADAW), 0)
        for s in range(NCHIP):
            slab = mbuf[s]
            for j in range(BL):
                mod_ref[j:j + 1, ADAW * s:ADAW * (s + 1)] = jnp.sum(jnp.where(row == BL * me + j, slab, 0.0), axis=0, keepdims=True)

    vm = pl.BlockSpec(memory_space=pltpu.VMEM)
    return pl.pallas_call(
        body, name="ada_fwd", in_specs=[vm, vm, vm], out_specs=[vm, vm],
        out_shape=[jax.ShapeDtypeStruct((BL, NMOD * D), f32), jax.ShapeDtypeStruct((BL * NDEV, D), f32)],
        scratch_shapes=[pltpu.VMEM((NDEV, BL, D), f32), pltpu.VMEM((NCHIP, BL * NDEV, ADAW), f32),
                        pltpu.SemaphoreType.DMA((NDEV - 1,)), pltpu.SemaphoreType.DMA((NDEV - 1,)),
                        pltpu.SemaphoreType.DMA((NCHIP - 1,)), pltpu.SemaphoreType.DMA((NCHIP - 1,))],
        compiler_params=pltpu.CompilerParams(vmem_limit_bytes=VMEM_LIMIT),
    )(c_in, w_ada, b_cols)


def _small_allreduce(accs, cond_all):
    def body(bi, bo, bm, bd, gi, go, gm, gd, dsink, cond_ref, gw_ref, gb_ref, small_ref, pay, pbuf, dall, s1, r1):
        x, y, c = _pos()
        chip = 2 * x + y
        pay[...] = jnp.zeros_like(pay)
        for b in range(BL):
            for k, (ref, r) in enumerate(((bi, 1), (bi, 0), (bo, 0), (bm, 1), (bm, 0), (bd, 0))):
                pay[b:b + 1, D * k:D * (k + 1)] = ref[b, r:r + 1, :]
        for off, ref, r in ((OFF_G_ATTN_PRE, gi, 0), (OFF_G_ATTN_POST, go, 0), (OFF_G_MIX_A, go, 1), (OFF_G_MLP_PRE, gm, 0),
                            (OFF_G_MLP_POST, gd, 0)):
            pay[BL:BL + 1, off:off + D] = ref[r:r + 1, :]
        eye = lax.broadcasted_iota(jnp.int32, (8, LANES), 0) == lax.broadcasted_iota(jnp.int32, (8, LANES), 1)
        pay[BL:BL + 1, OFF_SINK:OFF_SINK + LANES] = jnp.sum(jnp.where(eye, dsink[...], 0.0), axis=0, keepdims=True)
        pay[BL:BL + 1, OFF_LOSS:OFF_LOSS + LANES] = gd[1:2, 0:LANES]
        me = _gather_small(pay, pbuf, s1, r1)
        pbuf[me] = pay[...]
        small = pbuf[0, BL:BL + 1, :]
        for i in range(1, NDEV):
            small = small + pbuf[i, BL:BL + 1, :]
        small_ref[...] = small
        for i in range(NDEV):
            dall[BL * i:BL * (i + 1), :] = pbuf[i, 0:BL, :]
        gb_ref[...] = jnp.sum(dall[...], axis=0, keepdims=True)
        cols = jnp.zeros((BL * NDEV, ADAW), f32)
        for s in range(NCHIP):
            cols = cols + jnp.where(chip == s, dall[:, ADAW * s:ADAW * (s + 1)], 0.0)
        gw_ref[...] = lax.dot_general(cond_ref[...], cols, (((0,), (0,)), ((), ())), preferred_element_type=f32,
                                      precision=lax.Precision.HIGHEST)

    vm = pl.BlockSpec(memory_space=pltpu.VMEM)
    return pl.pallas_call(
        body, name="small_allreduce", in_specs=[vm] * 10, out_specs=[vm] * 3,
        out_shape=[jax.ShapeDtypeStruct((D, ADAW), f32), jax.ShapeDtypeStruct((1, PAYW), f32), jax.ShapeDtypeStruct((1, PAYW), f32)],
        scratch_shapes=[pltpu.VMEM((4, PAYW), f32), pltpu.VMEM((NDEV, 4, PAYW), f32), pltpu.VMEM((BL * NDEV, PAYW), f32),
                        pltpu.SemaphoreType.DMA((NDEV - 1,)), pltpu.SemaphoreType.DMA((NDEV - 1,))],
        compiler_params=pltpu.CompilerParams(vmem_limit_bytes=VMEM_LIMIT),
    )(*accs, cond_all)


def _half(ref, c):
    r2 = ref.shape[0] // 2
    return ref.at[pl.ds(pl.multiple_of(c * r2, 16), r2), :]


HBM_SPEC = pl.BlockSpec(memory_space=pltpu.HBM)
SEM_SPEC = pl.BlockSpec(memory_space=pltpu.SEMAPHORE)
EFFECT = pltpu.SideEffectType.DATAFLOW_SIDE_EFFECTING
NLINK = NCHIP - 1


def _in_hbm(a):
    return pltpu.with_memory_space_constraint(a, pltpu.HBM)


NSEM = 4


def _split_start(name, srcs, land_shapes, builds, carry, after=(), lands=None):
    n = len(srcs)
    na, nc = len(after), len(carry)

    def body(*refs):
        src, land = refs[:n], refs[n:2 * n]
        kept = refs[2 * n + na:2 * n + na + nc]
        outs = refs[2 * n + na + nc:]
        send, recv, passed = outs[:n], outs[n:2 * n], outs[4 * n:]
        for t in range(n):
            for out_cp, _ in builds[t](src[t], land[t], send[t], recv[t]):
                out_cp.start()
        for a, b in zip(kept, passed):
            b[...] = a[...]

    if lands is None:
        lands = [lax.empty(s.shape, s.dtype) for s in land_shapes]
    lands = [_in_hbm(a) for a in lands]
    sems = [pltpu.SemaphoreType.DMA((NSEM,))] * (2 * n)
    thru = [pltpu.HBM(a.shape, a.dtype) for a in list(srcs) + lands]
    vm = pl.BlockSpec(memory_space=pltpu.VMEM)
    res = pl.pallas_call(
        body, name=name, out_shape=sems + thru + [jax.ShapeDtypeStruct(a.shape, a.dtype) for a in carry],
        in_specs=[HBM_SPEC] * (2 * n) + [pl.BlockSpec(memory_space=pl.ANY)] * na + [vm] * nc,
        out_specs=[SEM_SPEC] * (2 * n) + [HBM_SPEC] * (2 * n) + [vm] * nc,
        input_output_aliases={i: 2 * n + i for i in range(2 * n)},
        compiler_params=pltpu.CompilerParams(has_side_effects=EFFECT),
    )(*[_in_hbm(a) for a in srcs], *lands, *after, *carry)
    flight = [(res[2 * n + t], res[3 * n + t], res[t], res[n + t]) for t in range(n)]
    return flight, list(res[4 * n:])


def _split_wait(name, flight, builds, after):
    m = len(flight)
    na = len(after)

    def body(*refs):
        src, land, send, recv = refs[:m], refs[m:2 * m], refs[2 * m:3 * m], refs[3 * m:4 * m]
        for t in range(m):
            for out_cp, in_cp in builds[t](src[t], land[t], send[t], recv[t]):
                out_cp.wait_send()
                in_cp.wait_recv()

    ops = [f[0] for f in flight] + [f[1] for f in flight] + [f[2] for f in flight] + [f[3] for f in flight]
    res = pl.pallas_call(
        body, name=name, out_shape=[pltpu.HBM(a.shape, a.dtype) for a in ops[:2 * m]],
        in_specs=[HBM_SPEC] * (2 * m) + [SEM_SPEC] * (2 * m) + [pl.BlockSpec(memory_space=pl.ANY)] * na,
        out_specs=[HBM_SPEC] * (2 * m), input_output_aliases={i: i for i in range(2 * m)},
        compiler_params=pltpu.CompilerParams(has_side_effects=EFFECT),
    )(*ops, *after)
    return res[:m], res[m:2 * m]


def _weight_copies(src, land, send, recv):
    x, y, c = _pos()
    chip = 2 * x + y
    return [(_rcopy(_half(src, c), _half(land.at[chip], c), send, recv, j, (px, py, c)),
             _rcopy(_half(src, c), _half(land.at[2 * px + py], c), send, recv, j, (px, py, c)))
            for j, (px, py) in enumerate(_other_chips(x, y))]


def _grad_copies(src, land, send, recv):
    x, y, c = _pos()
    return [(_rcopy(src.at[2 * px + py], land.at[j], send, recv, j, (px, py, c)),
             _rcopy(src.at[2 * px + py], land.at[j], send, recv, j, (px, py, c)))
            for j, (px, py) in enumerate(_other_chips(x, y))]


def _pair_grad_copies(src, land, send, recv):
    x, y, c = _pos()
    r2 = src.shape[1] // 2
    cp = _rcopy(src.at[:, pl.ds(pl.multiple_of((1 - c) * r2, 8), r2), :], land, send, recv, 0, (x, y, 1 - c))
    return [(cp, cp)]


def _pair_weight_copies(src, land, send, recv):
    x, y, c = _pos()
    sib = (x, y, 1 - c)
    cps = []
    for j, (px, py) in enumerate(_other_chips(x, y)):
        mine, theirs = _half(land.at[2 * px + py], c), _half(land.at[2 * px + py], 1 - c)
        cps.append((_rcopy(mine, mine, send, recv, j, sib), _rcopy(theirs, theirs, send, recv, j, sib)))
    own = _rcopy(src, land.at[2 * x + y], send, recv, NLINK, sib)
    return cps + [(own, own)]


RS_ROWS = 128


def _pair_add(g, landed, c_arr, name):
    _, r2, cw = landed.shape
    nr = r2 // RS_ROWS

    def body(c_ref, g_ref, p_ref, o_ref):
        o_ref[...] = (g_ref[...] + p_ref[...]).astype(bf16)

    gs = pltpu.PrefetchScalarGridSpec(
        num_scalar_prefetch=1, grid=(NCHIP, nr),
        in_specs=[pl.BlockSpec((None, RS_ROWS, cw), lambda s, j, c: (s, c[0] * nr + j, 0)),
                  pl.BlockSpec((None, RS_ROWS, cw), lambda s, j, c: (s, j, 0))],
        out_specs=pl.BlockSpec((None, RS_ROWS, cw), lambda s, j, c: (s, j, 0)))
    return pl.pallas_call(body, name=name, grid_spec=gs, out_shape=jax.ShapeDtypeStruct((NCHIP, r2, cw), bf16),
                          compiler_params=_cp(("arbitrary", "arbitrary")))(c_arr, g, landed)


def _chip_add(half, landed, pos_arr, name):
    _, r2, cw = half.shape
    nr = r2 // RS_ROWS

    def body(s_ref, h_ref, q_ref, o_ref):
        acc = h_ref[...].astype(f32)
        for j in range(NCHIP - 1):
            acc = acc + q_ref[j].astype(f32)
        o_ref[...] = acc

    gs = pltpu.PrefetchScalarGridSpec(
        num_scalar_prefetch=1, grid=(nr,),
        in_specs=[pl.BlockSpec((None, RS_ROWS, cw), lambda j, s: (s[0], j, 0)),
                  pl.BlockSpec((NCHIP - 1, RS_ROWS, cw), lambda j, s: (0, j, 0))],
        out_specs=pl.BlockSpec((RS_ROWS, cw), lambda j, s: (s[1] * nr + j, 0)))
    return pl.pallas_call(body, name=name, grid_spec=gs, out_shape=jax.ShapeDtypeStruct((2 * r2, cw), f32),
                          compiler_params=_cp(("arbitrary",)))(pos_arr, half, landed)


def _pair_gather_copies(src, land, send, recv):
    x, y, c = _pos()
    sib = (x, y, 1 - c)
    return [(_rcopy(_half(land, c), _half(land, c), send, recv, 0, sib),
             _rcopy(_half(land, 1 - c), _half(land, 1 - c), send, recv, 0, sib))]


def _adamw_math(w, g, m, v):
    m = B1 * m + (1.0 - B1) * g
    v = B2 * v + (1.0 - B2) * jnp.square(g)
    m_hat = m / (1.0 - B1 ** STEP)
    v_hat = v / (1.0 - B2 ** STEP)
    return -LR * (m_hat / (jnp.sqrt(v_hat) + AEPS) + WD * w), m, v


ADAM_ROWS = 256


def _adamw(w, g, m, v, name):
    r, cw = w.shape

    def body(w_ref, g_ref, m_ref, v_ref, go_ref, d_ref, mo_ref, vo_ref):
        g = g_ref[...]
        go_ref[...] = g
        d_ref[...], mo_ref[...], vo_ref[...] = _adamw_math(w_ref[...], g, m_ref[...], v_ref[...])

    spec = pl.BlockSpec((ADAM_ROWS, cw), lambda i: (i, 0))
    return pl.pallas_call(body, name=name, grid=(r // ADAM_ROWS,), in_specs=[spec] * 4, out_specs=[spec] * 4,
                          out_shape=[jax.ShapeDtypeStruct((r, cw), f32)] * 4, compiler_params=_cp(("arbitrary",)))(w, g, m, v)


SMALL = (("b_ada", None, PAYW), ("g_attn_pre", OFF_G_ATTN_PRE, D), ("g_attn_post", OFF_G_ATTN_POST, D), ("sink_a", OFF_SINK, 8),
         ("g_mix_a", OFF_G_MIX_A, AQ), ("g_mix_b", OFF_G_MIX_B, BW), ("g_mlp_pre", OFF_G_MLP_PRE, D), ("g_mlp_post", OFF_G_MLP_POST, D))


def _adamw_small(small, gb, params):
    n = len(SMALL)

    def body(*refs):
        small_ref, gb_ref = refs[:2]
        wmv = refs[2:2 + 3 * n]
        loss_ref = refs[2 + 3 * n]
        outs = refs[3 + 3 * n:]
        loss_ref[...] = small_ref[:, OFF_LOSS:OFF_LOSS + 1] * (0.5 / D)
        for i, (_, off, width) in enumerate(SMALL):
            g = gb_ref[...] if off is None else small_ref[:, off:off + width]
            w_ref, m_ref, v_ref = wmv[3 * i:3 * i + 3]
            outs[4 * i][...] = g
            outs[4 * i + 1][...], outs[4 * i + 2][...], outs[4 * i + 3][...] = _adamw_math(w_ref[...], g, m_ref[...], v_ref[...])

    vm = pl.BlockSpec(memory_space=pltpu.VMEM)
    out_shape = [jax.ShapeDtypeStruct((1, 1), f32)]
    for _, _, width in SMALL:
        out_shape += [jax.ShapeDtypeStruct((1, width), f32)] * 4
    flat = [a for wmv in params for a in wmv]
    res = pl.pallas_call(body, name="adamw_small", in_specs=[vm] * (2 + 3 * n), out_specs=[vm] * len(out_shape),
                         out_shape=out_shape)(small, gb, *flat)
    return res[0], {name: res[1 + 4 * i:5 + 4 * i] for i, (name, _, _) in enumerate(SMALL)}


def kernel(x, c, positions, w_ada, b_ada, g_attn_pre, g_attn_post, w_in, sink_a, g_mix_a, g_mix_b, w_out, g_mlp_pre, g_mlp_post, w_up, w_down, loss_target, m_w_ada, m_b_ada, m_g_attn_pre, m_g_attn_post, m_w_in, m_sink_a, m_g_mix_a, m_g_mix_b, m_w_out, m_g_mlp_pre, m_g_mlp_post, m_w_up, m_w_down, v_w_ada, v_b_ada, v_g_attn_pre, v_g_attn_post, v_w_in, v_sink_a, v_g_mix_a, v_g_mix_b, v_w_out, v_g_mlp_pre, v_g_mlp_post, v_w_up, v_w_down):
    given = dict(w_ada=w_ada, b_ada=b_ada, g_attn_pre=g_attn_pre, g_attn_post=g_attn_post, w_in=w_in, sink_a=sink_a, g_mix_a=g_mix_a,
                 g_mix_b=g_mix_b, w_out=w_out, g_mlp_pre=g_mlp_pre, g_mlp_post=g_mlp_post, w_up=w_up, w_down=w_down)
    moms = dict(w_ada=(m_w_ada, v_w_ada), b_ada=(m_b_ada, v_b_ada), g_attn_pre=(m_g_attn_pre, v_g_attn_pre),
                g_attn_post=(m_g_attn_post, v_g_attn_post), w_in=(m_w_in, v_w_in), sink_a=(m_sink_a, v_sink_a),
                g_mix_a=(m_g_mix_a, v_g_mix_a), g_mix_b=(m_g_mix_b, v_g_mix_b), w_out=(m_w_out, v_w_out),
                g_mlp_pre=(m_g_mlp_pre, v_g_mlp_pre), g_mlp_post=(m_g_mlp_post, v_g_mlp_post), w_up=(m_w_up, v_w_up),
                w_down=(m_w_down, v_w_down))
    order = ["w_ada", "b_ada", "g_attn_pre", "g_attn_post", "w_in", "sink_a", "g_mix_a", "g_mix_b", "w_out", "g_mlp_pre",
             "g_mlp_post", "w_up", "w_down"]
    xi, yi, ci = _pos()
    chip = 2 * xi + yi

    c_arr = jnp.reshape(ci, (1,)).astype(jnp.int32)
    pos_arr = jnp.stack([chip, ci]).astype(jnp.int32)
    big = ("w_in", "w_out", "w_up", "w_down")

    b_cols = lax.dynamic_slice(b_ada, (0, chip * ADAW), (1, ADAW))
    mod, cond_all = _ada_fwd(c, w_ada[0], b_cols)
    shards = [given[n][0].astype(bf16) for n in big]
    gathered = [jax.ShapeDtypeStruct((NCHIP,) + s.shape, bf16) for s in shards]
    flight_in, (mod,) = _split_start("weights_start_first", shards[:1], gathered[:1], [_weight_copies], [mod])
    flight_rest, (mod, inv_lane) = _split_start("weights_start_rest", shards[1:], gathered[1:], [_weight_copies] * 3,
                                                [mod, _inv_lane()])
    mod = mod.reshape(BL, NMOD, D)

    def first_weight(after):
        srcs, lands = _split_wait("weights_wait_first", flight_in, [_weight_copies], after)
        cross, _ = _split_start("weights_pair_start_first", srcs, None, [_pair_weight_copies], [], lands=lands)
        _, (win_g,) = _split_wait("weights_pair_wait_first", cross, [_pair_weight_copies], ())
        return win_g.transpose(1, 0, 2).reshape(D, INW)

    def later_weights(after, carry):
        srcs, lands = _split_wait("weights_wait_rest", flight_rest, [_weight_copies] * 3, after)
        fl, (carry,) = _split_start("weights_pair_start_rest", srcs, None, [_pair_weight_copies] * 3, [carry], lands=lands)
        _, (wout_g,) = _split_wait("weights_pair_wait_out", fl[:1], [_pair_weight_copies], ())

        def mlp_weights(after):
            _, (wup_g, wdn_g) = _split_wait("weights_pair_wait_mlp", fl[1:], [_pair_weight_copies] * 2, after)
            return wup_g, wdn_g.reshape(DFF, D)

        return wout_g.reshape(D, D), mlp_weights, carry

    crossing, pending = {}, {}

    def grad_ready(group, g, carry):
        if group == "w_down":
            names, slabs = ("w_down",), [g.reshape(NCHIP, DFF // NCHIP, D)]
        elif group == "w_up":
            names, slabs = ("w_up",), [g]
        else:
            names = ("w_in", "w_out")
            slabs = [g[0], g[1].reshape(NCHIP, D // NCHIP, D)]
        fl, (carry,) = _split_start("grad_pair_start_" + group, slabs,
                                    [jax.ShapeDtypeStruct((NCHIP, s.shape[1] // 2, s.shape[2]), f32) for s in slabs],
                                    [_pair_grad_copies] * len(names), [carry])
        crossing[group] = (names, fl)
        return carry

    def grad_reduce(group, after, carry):
        names, fl = crossing[group]
        slabs, landed = _split_wait("grad_pair_wait_" + group, fl, [_pair_grad_copies] * len(names), after)
        halves = [_pair_add(s, p, c_arr, "grad_pair_sum_" + n) for s, p, n in zip(slabs, landed, names)]
        fl, (carry,) = _split_start("grad_start_" + group, halves,
                                    [jax.ShapeDtypeStruct((NLINK,) + h.shape[1:], bf16) for h in halves],
                                    [_grad_copies] * len(names), [carry])
        pending[group] = (names, fl)
        return carry

    grad_x, accs = _local_step(x, positions, mod, loss_target, inv_lane, first_weight, later_weights, grad_ready, grad_reduce,
                               g_attn_pre, g_attn_post, sink_a, g_mix_a, g_mix_b, g_mlp_pre, g_mlp_post)

    grads, out = {}, {}

    def update(n):
        g, d, m2, v2 = _adamw(given[n][0], grads[n], moms[n][0][0], moms[n][1][0], "adamw_" + n)
        out[n] = (g[None], d[None], m2[None], v2[None])
        return v2

    def finish(groups, after):
        names = sum((pending[g][0] for g in groups), ())
        fl = sum((pending[g][1] for g in groups), [])
        halves, landed = _split_wait("grad_wait_" + groups[0], fl, [_grad_copies] * len(names), after)
        flights = []
        for h, q, n in zip(halves, landed, names):
            full = _chip_add(h, q, pos_arr, "grad_chip_sum_" + n)
            flights.append(_split_start("grad_gather_start_" + n, [jnp.zeros((8, LANES), f32)], None, [_pair_gather_copies],
                                        [], lands=[full])[0])
        last = None
        for n, fl1 in zip(names, flights):
            after = (flights[-1][0][0],) if last is None and fl1 is not flights[-1] else () if last is None else (last,)
            _, (grads[n],) = _split_wait("grad_gather_wait_" + n, fl1, [_pair_gather_copies], after)
            last = update(n)
        return last

    grads["w_ada"], gb, small = _small_allreduce(accs, cond_all)
    small = grad_reduce("w_in_w_out", (small,), small)
    last = finish(("w_down", "w_up"), (small,))
    finish(("w_in_w_out",), (last, update("w_ada")))
    loss, res = _adamw_small(small, gb, [(given[n], moms[n][0], moms[n][1]) for n, _, _ in SMALL])
    for n, _, _ in SMALL:
        out[n] = tuple(res[n])
    return (loss.reshape(()), grad_x, *[out[n][0] for n in order], *[out[n][1] for n in order],
            *[out[n][2] for n in order], *[out[n][3] for n in order])
```

```python
import functools

import numpy as np
import jax
import jax.numpy as jnp
from jax import lax
from jax.experimental import pallas as pl
from jax.experimental.pallas import tpu as pltpu

f32 = jnp.float32
bf16 = jnp.bfloat16
MESH = pl.DeviceIdType.MESH

D = 1024
SEQ = 2048
BL = 2
HD = 64
AQ = 512
AKV = 128
BW = 512
INW = 2304
DFF = 4096
NMOD = 6
ROT = 16
THETA = 500000.0
EPS = 1e-6
NEG = -1e30
BLK = 128
TM = 512
NJ = SEQ // TM
LANES = 128
NCHIP = 4
NDEV = 8
VMEM_LIMIT = 56 << 20

LR, B1, B2, AEPS, WD, STEP = 0.001, 0.9, 0.999, 1e-08, 0.01, 10

OFF_G_ATTN_PRE, OFF_G_ATTN_POST, OFF_G_MIX_A, OFF_G_MIX_B = 0, 1024, 2048, 2560
OFF_G_MLP_PRE, OFF_G_MLP_POST, OFF_SINK, OFF_LOSS = 3072, 4096, 5120, 5248
PAYW = NMOD * D


def _cp(sem=None):
    return pltpu.CompilerParams(dimension_semantics=sem, vmem_limit_bytes=VMEM_LIMIT)


def _dot(a, b):
    return jnp.dot(a, b, preferred_element_type=f32)


def _dot_nt(a, b):
    return lax.dot_general(a, b, (((1,), (1,)), ((), ())), preferred_element_type=f32)


def _dot_tn(a, b):
    return lax.dot_general(a, b, (((0,), (0,)), ((), ())), preferred_element_type=f32)


def _rms(x):
    r = lax.rsqrt(jnp.mean(x * x, axis=-1, keepdims=True) + EPS)
    return x * r, r


def _rms_bwd(dy, y, r):
    return r * (dy - y * jnp.mean(dy * y, axis=-1, keepdims=True))


def _colsum(v):
    return jnp.sum(v, axis=0, keepdims=True)


def _rope(p, c, s1, s2):
    outs = []
    for c0 in range(0, p.shape[1], LANES):
        pc = p[:, c0:c0 + LANES]
        outs.append(pc * c + pltpu.roll(pc, LANES - ROT // 2, 1) * s1 + pltpu.roll(pc, ROT // 2, 1) * s2)
    return outs[0] if len(outs) == 1 else jnp.concatenate(outs, axis=1)


def _rope_t(g, c, s1, s2):
    outs = []
    for c0 in range(0, g.shape[1], LANES):
        gc = g[:, c0:c0 + LANES]
        outs.append(gc * c + pltpu.roll(gc * s1, ROT // 2, 1) + pltpu.roll(gc * s2, LANES - ROT // 2, 1))
    return outs[0] if len(outs) == 1 else jnp.concatenate(outs, axis=1)


def _perm_store(val, scr, out_ref, d):
    nc = val.shape[1] // LANES
    for c in range(nc):
        scr[c] = val[:, LANES * c:LANES * (c + 1)]
    for c in range(nc):
        for r in range(d):
            out_ref[r, :, LANES * c:LANES * (c + 1)] = scr[c, pl.ds(r, TM // d, stride=d), :].astype(out_ref.dtype)


def _perm_load(in_ref, scr, d):
    nc = in_ref.shape[-1] // LANES
    for c in range(nc):
        for r in range(d):
            scr[c, pl.ds(r, TM // d, stride=d), :] = in_ref[r, :, LANES * c:LANES * (c + 1)].astype(f32)
    return jnp.concatenate([scr[c] for c in range(nc)], axis=1)


def _tok(w, dtype=None):
    return pl.BlockSpec((None, TM, w), lambda b, j: (b, j, 0))


def _perm_spec(d, w):
    return pl.BlockSpec((None, d, TM // d, w), lambda b, j: (b, 0, j, 0))


def _full(shape):
    n = len(shape)
    return pl.BlockSpec(shape, lambda b, j: (0,) * n)


MOD_SPEC = pl.BlockSpec((None, NMOD, D), lambda b, j: (b, 0, 0))
ACCB_SPEC = pl.BlockSpec((None, 8, D), lambda b, j: (b, 0, 0))
ACCG_SPEC = pl.BlockSpec((8, D), lambda b, j: (0, 0))
ACC_SHAPES = [jax.ShapeDtypeStruct((BL, 8, D), f32), jax.ShapeDtypeStruct((8, D), f32)]


def _acc_init(accb_ref, accg_ref):
    b, j = pl.program_id(0), pl.program_id(1)

    @pl.when(j == 0)
    def _():
        accb_ref[...] = jnp.zeros_like(accb_ref)

    @pl.when((b == 0) & (j == 0))
    def _():
        accg_ref[...] = jnp.zeros_like(accg_ref)


def _rope_tables(pos_col, inv_lane):
    def body(p_ref, inv_ref, c_ref, s1_ref, s2_ref):
        ang = p_ref[...].astype(f32) * inv_ref[...]
        j = lax.broadcasted_iota(jnp.int32, (TM, LANES), 1) % HD
        cs, sn = jnp.cos(ang), jnp.sin(ang)
        c_ref[...] = jnp.where(j < ROT, cs, 1.0)
        s1_ref[...] = jnp.where(j < ROT // 2, -sn, 0.0)
        s2_ref[...] = jnp.where((j >= ROT // 2) & (j < ROT), sn, 0.0)

    n = BL * SEQ // TM
    return pl.pallas_call(
        body, name="rope_tables", grid=(n,),
        in_specs=[pl.BlockSpec((TM, 1), lambda i: (i, 0)), pl.BlockSpec((1, LANES), lambda i: (0, 0))],
        out_specs=[pl.BlockSpec((TM, LANES), lambda i: (i, 0))] * 3,
        out_shape=[jax.ShapeDtypeStruct((BL * SEQ, LANES), f32)] * 3,
    )(pos_col, inv_lane)


def _attn_in(x, mod, g_pre, w_in, tc, ts1, ts2):
    def body(x_ref, mod_ref, g_ref, w_ref, c_ref, s1_ref, s2_ref,
             h_ref, qa_ref, ka_ref, va_ref, q1_ref, k1_ref, v1_ref, q4_ref, k4_ref, v4_ref, q16_ref, k16_ref, v16_ref,
             scr):
        xn, _ = _rms(x_ref[...])
        h = (xn * g_ref[...]) * (1.0 + mod_ref[1:2, :]) + mod_ref[0:1, :]
        hb = h.astype(bf16)
        h_ref[...] = hb
        proj = _dot(hb, w_ref[...])
        c, s1, s2 = c_ref[...], s1_ref[...], s2_ref[...]
        o1, o2, o3, o4, o5 = AQ, AQ + AKV, AQ + 2 * AKV, AQ + 2 * AKV + BW, AQ + 2 * AKV + 2 * BW
        qa_ref[...] = (_rope(proj[:, :o1], c, s1, s2) * 0.125).astype(bf16)
        ka_ref[...] = _rope(proj[:, o1:o2], c, s1, s2).astype(bf16)
        va_ref[...] = proj[:, o2:o3].astype(bf16)
        qb = _rope(proj[:, o3:o4], c, s1, s2) * 0.125
        kb = _rope(proj[:, o4:o5], c, s1, s2)
        vb = proj[:, o5:]
        for val, r1, r4, r16 in ((qb, q1_ref, q4_ref, q16_ref), (kb, k1_ref, k4_ref, k16_ref), (vb, v1_ref, v4_ref, v16_ref)):
            r1[...] = val.astype(bf16)
            _perm_store(val, scr, r4, 4)
            _perm_store(val, scr, r16, 16)

    nat = lambda w: jax.ShapeDtypeStruct((BL, SEQ, w), bf16)
    p4 = jax.ShapeDtypeStruct((BL, 4, SEQ // 4, BW), bf16)
    p16 = jax.ShapeDtypeStruct((BL, 16, SEQ // 16, BW), bf16)
    return pl.pallas_call(
        body, name="attn_in", grid=(BL, NJ),
        in_specs=[_tok(D), MOD_SPEC, _full((1, D)), _full((D, INW)), _tok(LANES), _tok(LANES), _tok(LANES)],
        out_specs=[_tok(D), _tok(AQ), _tok(AKV), _tok(AKV)] + [_tok(BW)] * 3 + [_perm_spec(4, BW)] * 3 + [_perm_spec(16, BW)] * 3,
        out_shape=[nat(D), nat(AQ), nat(AKV), nat(AKV)] + [nat(BW)] * 3 + [p4] * 3 + [p16] * 3,
        scratch_shapes=[pltpu.VMEM((BW // LANES, TM, LANES), f32)],
        compiler_params=_cp(("arbitrary", "arbitrary")),
    )(x, mod, g_pre, w_in, tc, ts1, ts2)


def _kv_cat(cur_ref, prev_ref, p, gqa, cache):
    def one(ref):
        if not gqa:
            return ref[:, LANES * p:LANES * (p + 1)]
        k = ref[...]
        kr = pltpu.roll(k, HD, 1)
        lo = lax.broadcasted_iota(jnp.int32, k.shape, 1) < HD
        return jnp.where(lo, k, kr) if p < 2 else jnp.where(lo, kr, k)

    key = (id(cur_ref), p // 2 if gqa else p)
    if key not in cache:
        cache[key] = one(cur_ref) if prev_ref is None else jnp.concatenate([one(prev_ref), one(cur_ref)], axis=0)
    return cache[key]


def _lane_half(a, hh):
    lo = lax.broadcasted_iota(jnp.int32, a.shape, 1) < HD
    return jnp.where(lo, a, jnp.zeros_like(a)) if hh == 0 else jnp.where(lo, jnp.zeros_like(a), a)


def _attn_fwd(q, k, v, sink, *, max_dist, o_dtype, name):
    n, l, w = q.shape
    wk = k.shape[-1]
    nb = l // BLK
    gqa = wk != w
    has_sink = sink is not None

    def body(*refs):
        if has_sink:
            sink_ref, refs = refs[0], refs[1:]
        if nb > 1:
            q_ref, kc_ref, kp_ref, vc_ref, vp_ref, o_ref, lse_ref, sscr, pscr, dscr = refs
        else:
            q_ref, kc_ref, vc_ref, o_ref, lse_ref, sscr, pscr, dscr = refs
        i = pl.program_id(1)
        qi = lax.broadcasted_iota(jnp.int32, (BLK, BLK), 0)
        kj = lax.broadcasted_iota(jnp.int32, (BLK, BLK), 1)
        tri = kj <= qi
        eye = kj == qi
        cache = {}
        for p in range(w // LANES):
            qpair = q_ref[:, LANES * p:LANES * (p + 1)]
            kcat = _kv_cat(kc_ref, kp_ref if nb > 1 else None, p, gqa, cache)
            for hh in range(2):
                s = _dot_nt(_lane_half(qpair, hh), kcat)
                if nb > 1:
                    sp = jnp.where(i > 0, s[:, :BLK], NEG)
                    sscr[2 * p + hh] = jnp.where(tri, s[:, BLK:], sp)
                    if diag:
                        dscr[2 * p + hh] = jnp.where(eye, sp, NEG)
                else:
                    sscr[2 * p + hh] = jnp.where(tri, s, NEG)
        lane = lax.broadcasted_iota(jnp.int32, (BLK, LANES), 1)
        lse_all = jnp.zeros((BLK, LANES), f32)
        for p in range(w // LANES):
            for hh in range(2):
                h = 2 * p + hh
                comb = sscr[h]
                if diag:
                    dtile = dscr[h]
                    m = jnp.max(jnp.maximum(comb, dtile), axis=-1, keepdims=True)
                else:
                    m = jnp.max(comb, axis=-1, keepdims=True)
                if has_sink:
                    sk = sink_ref[0, h]
                    m = jnp.maximum(m, sk)
                e = jnp.exp(comb - m)
                if diag:
                    ed = jnp.exp(dtile - m)
                    den = jnp.sum(e + ed, axis=-1, keepdims=True)
                else:
                    den = jnp.sum(e, axis=-1, keepdims=True)
                if has_sink:
                    den = den + jnp.exp(sk - m)
                inv = 1.0 / den
                if nb > 1:
                    pscr[h, :, :BLK] = (jnp.where(tri, ed if diag else 0.0, e) * inv).astype(bf16)
                    pscr[h, :, BLK:] = (jnp.where(tri, e, 0.0) * inv).astype(bf16)
                else:
                    pscr[h] = (e * inv).astype(bf16)
                lse_all = jnp.where(lane == h, jnp.broadcast_to(m + jnp.log(den), (BLK, LANES)), lse_all)
        lse_ref[...] = lse_all
        for p in range(w // LANES):
            vcat = _kv_cat(vc_ref, vp_ref if nb > 1 else None, p, gqa, cache)
            key = ("halves", id(vc_ref), p // 2 if gqa else p)
            if key not in cache:
                cache[key] = (_lane_half(vcat, 0), _lane_half(vcat, 1))
            o_ref[:, LANES * p:LANES * (p + 1)] = (_dot(pscr[2 * p], cache[key][0])
                                                   + _dot(pscr[2 * p + 1], cache[key][1])).astype(o_ref.dtype)

    assert max_dist in (BLK - 1, BLK)
    diag = nb > 1 and max_dist == BLK
    cur = lambda ww: pl.BlockSpec((None, BLK, ww), lambda a, i: (a, i, 0))
    prev = lambda ww: pl.BlockSpec((None, BLK, ww), lambda a, i: (a, jnp.maximum(i - 1, 0), 0))
    in_specs = [cur(w), cur(wk)] + ([prev(wk)] if nb > 1 else []) + [cur(wk)] + ([prev(wk)] if nb > 1 else [])
    args = [q, k] + ([k] if nb > 1 else []) + [v] + ([v] if nb > 1 else [])
    if has_sink:
        in_specs = [pl.BlockSpec(memory_space=pltpu.SMEM)] + in_specs
        args = [sink] + args
    return pl.pallas_call(
        body, name=name, grid=(n, nb), in_specs=in_specs,
        out_specs=[cur(w), cur(LANES)],
        out_shape=[jax.ShapeDtypeStruct((n, l, w), o_dtype), jax.ShapeDtypeStruct((n, l, LANES), f32)],
        scratch_shapes=[pltpu.VMEM((w // HD, BLK, BLK), f32), pltpu.VMEM((w // HD, BLK, 2 * BLK if nb > 1 else BLK), bf16),
                        pltpu.VMEM((w // HD if diag else 1, BLK, BLK), f32)],
        compiler_params=_cp(("arbitrary", "arbitrary")),
    )(*args)


def _attn_bwd(q, k, v, do, delta, lse, sink, *, max_dist, name):
    n, l, w = q.shape
    wk = k.shape[-1]
    nb = l // BLK
    gqa = wk != w
    has_sink = sink is not None

    def body(*refs):
        if has_sink:
            sink_ref, refs = refs[0], refs[1:]
        if nb > 1:
            q_ref, kc_ref, kp_ref, vc_ref, vp_ref, do_ref, delta_ref, lse_ref = refs[:8]
            rest = refs[8:]
        else:
            q_ref, kc_ref, vc_ref, do_ref, delta_ref, lse_ref = refs[:6]
            rest = refs[6:]
        if has_sink:
            dq_ref, dk_ref, dv_ref, dsink_ref = rest[:4]
            rest = rest[4:]
        else:
            dq_ref, dk_ref, dv_ref = rest[:3]
            rest = rest[3:]
        step = pl.program_id(1)
        blk_idx = nb - 1 - step
        if nb > 1:
            ck, cv = rest[:2]
            rest = rest[2:]

            @pl.when(step == 0)
            def _():
                ck[...] = jnp.zeros_like(ck)
                cv[...] = jnp.zeros_like(cv)

        sscr, dpscr, pscr, dsscr = rest[:4]
        if diag:
            dscr, ddscr = rest[4:]
        if has_sink:
            @pl.when((pl.program_id(0) == 0) & (step == 0))
            def _():
                dsink_ref[...] = jnp.zeros_like(dsink_ref)

        lane = lax.broadcasted_iota(jnp.int32, (BLK, LANES), 1)
        lo = lane < HD
        qi = lax.broadcasted_iota(jnp.int32, (BLK, BLK), 0)
        kj = lax.broadcasted_iota(jnp.int32, (BLK, BLK), 1)
        tri = kj <= qi
        eye = kj == qi
        cache = {}
        kp, vp = (kp_ref, vp_ref) if nb > 1 else (None, None)
        rows = 2 * BLK if nb > 1 else BLK
        for p in range(w // LANES):
            sl = slice(LANES * p, LANES * (p + 1))
            qpair, dopair = q_ref[:, sl], do_ref[:, sl]
            kcat, vcat = _kv_cat(kc_ref, kp, p, gqa, cache), _kv_cat(vc_ref, vp, p, gqa, cache)
            for hh in range(2):
                h = 2 * p + hh
                s = _dot_nt(_lane_half(qpair, hh), kcat)
                dp = _dot_nt(_lane_half(dopair, hh), vcat)
                if nb > 1:
                    sp = jnp.where(blk_idx > 0, s[:, :BLK], NEG)
                    sscr[h] = jnp.where(tri, s[:, BLK:], sp)
                    dpscr[h] = jnp.where(tri, dp[:, BLK:], dp[:, :BLK])
                    if diag:
                        dscr[h] = jnp.where(eye, sp, NEG)
                        ddscr[h] = dp[:, :BLK]
                else:
                    sscr[h] = jnp.where(tri, s, NEG)
                    dpscr[h] = dp
        for p in range(w // LANES):
            for hh in range(2):
                h = 2 * p + hh
                lse_b = jnp.broadcast_to(lse_ref[:, h:h + 1], (BLK, BLK))
                delta = jnp.broadcast_to(delta_ref[:, h:h + 1], (BLK, BLK))
                pr = jnp.exp(sscr[h] - lse_b)
                ds = pr * (dpscr[h] - delta)
                if nb > 1:
                    if diag:
                        prd = jnp.exp(dscr[h] - lse_b)
                        dsd = prd * (ddscr[h] - delta)
                    else:
                        prd = dsd = 0.0
                    pscr[h, :, :BLK] = jnp.where(tri, prd, pr).astype(bf16)
                    pscr[h, :, BLK:] = jnp.where(tri, pr, 0.0).astype(bf16)
                    dsscr[h, :, :BLK] = jnp.where(tri, dsd, ds).astype(bf16)
                    dsscr[h, :, BLK:] = jnp.where(tri, ds, 0.0).astype(bf16)
                else:
                    pscr[h] = pr.astype(bf16)
                    dsscr[h] = ds.astype(bf16)
                if has_sink:
                    dsk = -jnp.sum(jnp.where(lane == 0, jnp.exp(sink_ref[0, h] - lse_b) * delta, 0.0), keepdims=True)
                    dsink_ref[h:h + 1, :] += jnp.broadcast_to(dsk, (1, LANES))
        gk = [jnp.zeros((rows, LANES), f32), jnp.zeros((rows, LANES), f32)]
        gv = [jnp.zeros((rows, LANES), f32), jnp.zeros((rows, LANES), f32)]
        for p in range(w // LANES):
            sl = slice(LANES * p, LANES * (p + 1))
            qpair, dopair = q_ref[:, sl], do_ref[:, sl]
            kcat = _kv_cat(kc_ref, kp, p, gqa, cache)
            key = ("halves", p // 2 if gqa else p)
            if key not in cache:
                cache[key] = (_lane_half(kcat, 0), _lane_half(kcat, 1))
            dq_ref[:, sl] = _dot(dsscr[2 * p], cache[key][0]) + _dot(dsscr[2 * p + 1], cache[key][1])
            dk_pair = _dot_tn(dsscr[2 * p], _lane_half(qpair, 0)) + _dot_tn(dsscr[2 * p + 1], _lane_half(qpair, 1))
            dv_pair = _dot_tn(pscr[2 * p], _lane_half(dopair, 0)) + _dot_tn(pscr[2 * p + 1], _lane_half(dopair, 1))
            if gqa:
                gk[p // 2] = gk[p // 2] + dk_pair
                gv[p // 2] = gv[p // 2] + dv_pair
            elif nb > 1:
                dk_ref[:, sl] = dk_pair[BLK:] + ck[:, sl]
                dv_ref[:, sl] = dv_pair[BLK:] + cv[:, sl]
                ck[:, sl] = dk_pair[:BLK]
                cv[:, sl] = dv_pair[:BLK]
            else:
                dk_ref[:, sl] = dk_pair
                dv_ref[:, sl] = dv_pair
        if gqa:
            lor = lax.broadcasted_iota(jnp.int32, (rows, LANES), 1) < HD
            fold = lambda g: jnp.where(lor, g[0] + pltpu.roll(g[0], HD, 1), g[1] + pltpu.roll(g[1], HD, 1))
            dk_full, dv_full = fold(gk), fold(gv)
            dk_ref[...] = dk_full[BLK:] + ck[...]
            dv_ref[...] = dv_full[BLK:] + cv[...]
            ck[...] = dk_full[:BLK]
            cv[...] = dv_full[:BLK]

    assert max_dist in (BLK - 1, BLK)
    diag = nb > 1 and max_dist == BLK
    cur = lambda ww: pl.BlockSpec((None, BLK, ww), lambda a, i: (a, nb - 1 - i, 0))
    prev = lambda ww: pl.BlockSpec((None, BLK, ww), lambda a, i: (a, jnp.maximum(nb - 2 - i, 0), 0))
    in_specs = ([cur(w), cur(wk)] + ([prev(wk)] if nb > 1 else []) + [cur(wk)] + ([prev(wk)] if nb > 1 else [])
                + [cur(w), cur(LANES), cur(LANES)])
    args = [q, k] + ([k] if nb > 1 else []) + [v] + ([v] if nb > 1 else []) + [do, delta, lse]
    out_specs = [cur(w), cur(wk), cur(wk)]
    out_shape = [jax.ShapeDtypeStruct((n, l, w), f32), jax.ShapeDtypeStruct((n, l, wk), f32), jax.ShapeDtypeStruct((n, l, wk), f32)]
    if has_sink:
        in_specs = [pl.BlockSpec(memory_space=pltpu.SMEM)] + in_specs
        args = [sink] + args
        out_specs.append(pl.BlockSpec((8, LANES), lambda a, i: (0, 0)))
        out_shape.append(jax.ShapeDtypeStruct((8, LANES), f32))
    nh = w // HD
    scratch = [pltpu.VMEM((BLK, wk), f32), pltpu.VMEM((BLK, wk), f32)] if nb > 1 else []
    scratch += [pltpu.VMEM((nh, BLK, BLK), f32)] * 2 + [pltpu.VMEM((nh, BLK, 2 * BLK if nb > 1 else BLK), bf16)] * 2
    if diag:
        scratch += [pltpu.VMEM((nh, BLK, BLK), f32)] * 2
    return pl.pallas_call(
        body, name=name, grid=(n, nb), in_specs=in_specs, out_specs=out_specs, out_shape=out_shape,
        scratch_shapes=scratch, compiler_params=_cp(("arbitrary", "arbitrary")),
    )(*args)


def _split2(x):
    hi = x.astype(bf16)
    return hi, (x - hi.astype(f32)).astype(bf16)


def _heads_to_lanes(xc, e):
    return sum(_dot(t, e) for t in _split2(xc))


def _lanes_to_heads(x, g):
    return sum(_dot(t, g) for t in _split2(x))


HEAD_EXPAND = (np.arange(LANES)[:, None] == np.arange(BW)[None, :] // HD).astype(np.float32)
HEAD_SUM = HEAD_EXPAND.T.copy()


def _branch_weights(l1_ref, l4_ref, l16_ref, scr):
    l4v = _perm_load(l4_ref, scr, 4)
    l16v = _perm_load(l16_ref, scr, 16)
    l1v = l1_ref[...]
    m = jnp.maximum(jnp.maximum(l1v, l4v), l16v)
    e1, e4, e16 = jnp.exp(l1v - m), jnp.exp(l4v - m), jnp.exp(l16v - m)
    z = e1 + e4 + e16
    return e1 / z, e4 / z, e16 / z


def _mix_out(oa, o1, l1, o4, l4, o16, l16, g_mix_a, g_mix_b, w_out, x, mod, g_post):
    def body(oa_ref, o1_ref, l1_ref, o4_ref, l4_ref, o16_ref, l16_ref, ga_ref, gb_ref, w_ref, x_ref, mod_ref, gp_ref, e_ref,
             x1_ref, y_ref, mixed_ref, ob_ref, scr):
        w1, w4, w16 = _branch_weights(l1_ref, l4_ref, l16_ref, scr)
        e = e_ref[...]
        x1w, x4w = _heads_to_lanes(w1, e), _heads_to_lanes(w4, e)
        ob = (x1w * o1_ref[...].astype(f32) + x4w * _perm_load(o4_ref, scr, 4)
              + (1.0 - x1w - x4w) * _perm_load(o16_ref, scr, 16))
        ob_ref[...] = ob
        oan, _ = _rms(oa_ref[...])
        obn, _ = _rms(ob)
        mixed = jnp.concatenate([oan * ga_ref[...], obn * gb_ref[...]], axis=1).astype(bf16)
        mixed_ref[...] = mixed
        y = _dot(mixed, w_ref[...])
        y_ref[...] = y
        yn, _ = _rms(y)
        x1_ref[...] = x_ref[...] + mod_ref[2:3, :] * (yn * gp_ref[...])

    nat = lambda w, dt: jax.ShapeDtypeStruct((BL, SEQ, w), dt)
    return pl.pallas_call(
        body, name="mix_out", grid=(BL, NJ),
        in_specs=[_tok(AQ), _tok(BW), _tok(LANES), _perm_spec(4, BW), _perm_spec(4, LANES), _perm_spec(16, BW),
                  _perm_spec(16, LANES), _full((1, AQ)), _full((1, BW)), _full((D, D)), _tok(D), MOD_SPEC, _full((1, D)),
                  _full((LANES, BW))],
        out_specs=[_tok(D), _tok(D), _tok(D), _tok(BW)],
        out_shape=[nat(D, f32), nat(D, f32), nat(D, bf16), nat(BW, f32)],
        scratch_shapes=[pltpu.VMEM((BW // LANES, TM, LANES), f32)],
        compiler_params=_cp(("arbitrary", "arbitrary")),
    )(oa, o1, l1, o4, l4, o16, l16, g_mix_a, g_mix_b, w_out, x, mod, g_post, jnp.asarray(HEAD_EXPAND, bf16))


def _mlp_up(x1, mod, g_pre, w_up):
    def body(x_ref, mod_ref, g_ref, w_ref, h_ref, u_ref, a_ref):
        xn, _ = _rms(x_ref[...])
        h = (xn * g_ref[...]) * (1.0 + mod_ref[4:5, :]) + mod_ref[3:4, :]
        hb = h.astype(bf16)
        h_ref[...] = hb
        for s in range(NCHIP):
            u = _dot(hb, w_ref[s])
            u_ref[:, D * s:D * (s + 1)] = u.astype(bf16)
            a_ref[:, D * s:D * (s + 1)] = jnp.square(jnp.maximum(u, 0.0)).astype(bf16)

    nat = lambda w: jax.ShapeDtypeStruct((BL, SEQ, w), bf16)
    return pl.pallas_call(
        body, name="mlp_up", grid=(BL, NJ),
        in_specs=[_tok(D), MOD_SPEC, _full((1, D)), _full((NCHIP, D, D))],
        out_specs=[_tok(D), _tok(DFF), _tok(DFF)], out_shape=[nat(D), nat(DFF), nat(DFF)],
        compiler_params=_cp(("arbitrary", "arbitrary")),
    )(x1, mod, g_pre, w_up)


def _mlp_down(a, w_down, x1, target, mod, g_post):
    def body(a_ref, w_ref, x_ref, t_ref, mod_ref, g_ref, gx_ref, dy_ref, accb_ref, accg_ref):
        _acc_init(accb_ref, accg_ref)
        y2 = _dot(a_ref[...], w_ref[...])
        yn, r = _rms(y2)
        g = g_ref[...]
        gt = mod_ref[5:6, :]
        n2 = yn * g
        err = x_ref[...] + gt * n2 - t_ref[...]
        gout = err * (1.0 / D)
        gx_ref[...] = gout
        dn2 = gout * gt
        dy_ref[...] = _rms_bwd(dn2 * g, yn, r).astype(bf16)
        accb_ref[0:1, :] += _colsum(gout * n2)
        accg_ref[0:1, :] += _colsum(dn2 * yn)
        accg_ref[1:2, :] += jnp.broadcast_to(jnp.sum(err * err, keepdims=True), (1, D))

    return pl.pallas_call(
        body, name="mlp_down", grid=(BL, NJ),
        in_specs=[_tok(DFF), _full((DFF, D)), _tok(D), _tok(D), MOD_SPEC, _full((1, D))],
        out_specs=[_tok(D), _tok(D), ACCB_SPEC, ACCG_SPEC],
        out_shape=[jax.ShapeDtypeStruct((BL, SEQ, D), f32), jax.ShapeDtypeStruct((BL, SEQ, D), bf16)] + ACC_SHAPES,
        compiler_params=_cp(("arbitrary", "arbitrary")),
    )(a, w_down, x1, target, mod, g_post)


def _mlp_bwd(dy2, u, w_down, w_up, x1, gx, mod, g_pre):
    def body(dy_ref, u_ref, wd_hbm, wu_hbm, x_ref, gx_ref, mod_ref, g_ref, du_ref, gx1_ref, accb_ref, accg_ref, wd, wu, sem):
        _acc_init(accb_ref, accg_ref)

        @pl.when((pl.program_id(0) == 0) & (pl.program_id(1) == 0))
        def _():
            c1 = pltpu.make_async_copy(wd_hbm, wd, sem.at[0])
            c2 = pltpu.make_async_copy(wu_hbm, wu, sem.at[1])
            c1.start()
            c2.start()
            c1.wait()
            c2.wait()

        dy = dy_ref[...]
        dh = jnp.zeros((TM, D), f32)
        for s in range(NCHIP):
            sl = slice(D * s, D * (s + 1))
            da = _dot_nt(dy, wd[sl, :])
            du = (da * (2.0 * jnp.maximum(u_ref[:, sl].astype(f32), 0.0))).astype(bf16)
            du_ref[:, sl] = du
            dh = dh + _dot_nt(du, wu[s])
        xn, r = _rms(x_ref[...])
        g = g_ref[...]
        n = xn * g
        dn = dh * (1.0 + mod_ref[4:5, :])
        gx1_ref[...] = gx_ref[...] + _rms_bwd(dn * g, xn, r)
        accb_ref[0:1, :] += _colsum(dh * n)
        accb_ref[1:2, :] += _colsum(dh)
        accg_ref[0:1, :] += _colsum(dn * xn)

    anyspec = pl.BlockSpec(memory_space=pl.ANY)
    return pl.pallas_call(
        body, name="mlp_bwd", grid=(BL, NJ),
        in_specs=[_tok(D), _tok(DFF), anyspec, anyspec, _tok(D), _tok(D), MOD_SPEC, _full((1, D))],
        out_specs=[_tok(DFF), _tok(D), ACCB_SPEC, ACCG_SPEC],
        out_shape=[jax.ShapeDtypeStruct((BL, SEQ, DFF), bf16), jax.ShapeDtypeStruct((BL, SEQ, D), f32)] + ACC_SHAPES,
        scratch_shapes=[pltpu.VMEM((DFF, D), bf16), pltpu.VMEM((NCHIP, D, D), bf16), pltpu.SemaphoreType.DMA((2,))],
        compiler_params=_cp(("arbitrary", "arbitrary")),
    )(dy2, u, w_down, w_up, x1, gx, mod, g_pre)


def _matmul_tn(a, b, *, tn, col_blocked, name, out_dtype=f32):
    t, m = a.shape
    n = b.shape[1]
    tmm = min(m, 1024)
    tk = 2048 if tn <= 1024 else 1024
    nk = t // tk

    def body(a_ref, b_ref, o_ref, acc):
        k = pl.program_id(2)

        @pl.when(k == 0)
        def _():
            acc[...] = jnp.zeros_like(acc)

        acc[...] += _dot_tn(a_ref[...], b_ref[...])

        @pl.when(k == nk - 1)
        def _():
            o_ref[...] = acc[...].astype(out_dtype)

    if col_blocked:
        out_spec = pl.BlockSpec((None, tmm, tn), lambda i, j, k: (j, i, 0))
        out_shape = jax.ShapeDtypeStruct((n // tn, m, tn), out_dtype)
    else:
        out_spec = pl.BlockSpec((tmm, tn), lambda i, j, k: (i, j))
        out_shape = jax.ShapeDtypeStruct((m, n), out_dtype)
    return pl.pallas_call(
        body, name=name, grid=(m // tmm, n // tn, nk),
        in_specs=[pl.BlockSpec((tk, tmm), lambda i, j, k: (k, i)), pl.BlockSpec((tk, tn), lambda i, j, k: (k, j))],
        out_specs=out_spec, out_shape=out_shape, scratch_shapes=[pltpu.VMEM((tmm, tn), f32)],
        compiler_params=_cp(("arbitrary", "arbitrary", "arbitrary")),
    )(a, b)


def _grad_w_in(h, dproj):
    t = h.shape[0]
    tk = 1024
    nk = t // tk
    sw = INW // NCHIP

    def body(a_ref, b_ref, o_ref, acc):
        k = pl.program_id(0)

        @pl.when(k == 0)
        def _():
            acc[...] = jnp.zeros_like(acc)

        acc[...] += _dot_tn(a_ref[...], b_ref[...])

        @pl.when(k == nk - 1)
        def _():
            for s in range(NCHIP):
                o_ref[s] = acc[:, sw * s:sw * (s + 1)]

    return pl.pallas_call(
        body, name="grad_w_in", grid=(nk,),
        in_specs=[pl.BlockSpec((tk, D), lambda k: (k, 0)), pl.BlockSpec((tk, INW), lambda k: (k, 0))],
        out_specs=pl.BlockSpec((NCHIP, D, sw), lambda k: (0, 0, 0)), out_shape=jax.ShapeDtypeStruct((NCHIP, D, sw), f32),
        scratch_shapes=[pltpu.VMEM((D, INW), f32)], compiler_params=_cp(("arbitrary",)),
    )(h, dproj)


def _attn_out_bwd(gx1, y, mod, g_post, w_out, oa, ob, g_mix_a, g_mix_b, l1, l4, l16):
    def body(gx_ref, y_ref, mod_ref, gp_ref, w_ref, oa_ref, ob_ref, ga_ref, gb_ref, l1_ref, l4_ref, l16_ref, e_ref, g_ref,
             dy_ref, doa_ref, do1_ref, do4_ref, do16_ref, da_ref, d1_ref, d4_ref, d16_ref, accb_ref, accg_ref, scr):
        _acc_init(accb_ref, accg_ref)
        w1, w4, w16 = _branch_weights(l1_ref, l4_ref, l16_ref, scr)
        e, hs = e_ref[...], g_ref[...]
        gx1v = gx_ref[...]
        yn, ry = _rms(y_ref[...])
        gp = gp_ref[...]
        gt = mod_ref[2:3, :]
        dn1 = gx1v * gt
        dy = _rms_bwd(dn1 * gp, yn, ry).astype(bf16)
        dy_ref[...] = dy
        dmixed = _dot_nt(dy, w_ref[...])
        dma, dmb = dmixed[:, :AQ], dmixed[:, AQ:]
        oa, ob = oa_ref[...], ob_ref[...]
        oan, ra = _rms(oa)
        obn, rb = _rms(ob)
        doa = _rms_bwd(dma * ga_ref[...], oan, ra)
        doa_ref[...] = doa.astype(bf16)
        da_ref[...] = _lanes_to_heads(doa * oa, hs)
        dob = _rms_bwd(dmb * gb_ref[...], obn, rb)
        dd = _lanes_to_heads(dob * ob, hs)
        x1w, x4w = _heads_to_lanes(w1, e), _heads_to_lanes(w4, e)
        do1_ref[...] = (x1w * dob).astype(bf16)
        d1_ref[...] = w1 * dd
        _perm_store(x4w * dob, scr, do4_ref, 4)
        _perm_store(w4 * dd, scr, d4_ref, 4)
        _perm_store((1.0 - x1w - x4w) * dob, scr, do16_ref, 16)
        _perm_store(w16 * dd, scr, d16_ref, 16)
        accb_ref[0:1, :] += _colsum(gx1v * (yn * gp))
        accg_ref[0:1, :] += _colsum(dn1 * yn)
        accg_ref[1:2, :] += jnp.concatenate([_colsum(dma * oan), _colsum(dmb * obn)], axis=1)

    nat = lambda w, dt: jax.ShapeDtypeStruct((BL, SEQ, w), dt)
    return pl.pallas_call(
        body, name="attn_out_bwd", grid=(BL, NJ),
        in_specs=[_tok(D), _tok(D), MOD_SPEC, _full((1, D)), _full((D, D)), _tok(AQ), _tok(BW), _full((1, AQ)), _full((1, BW)),
                  _tok(LANES), _perm_spec(4, LANES), _perm_spec(16, LANES), _full((LANES, BW)), _full((BW, LANES))],
        out_specs=[_tok(D), _tok(AQ), _tok(BW), _perm_spec(4, BW), _perm_spec(16, BW),
                   _tok(LANES), _tok(LANES), _perm_spec(4, LANES), _perm_spec(16, LANES), ACCB_SPEC, ACCG_SPEC],
        out_shape=[nat(D, bf16), nat(AQ, bf16), nat(BW, bf16), jax.ShapeDtypeStruct((BL, 4, SEQ // 4, BW), bf16),
                   jax.ShapeDtypeStruct((BL, 16, SEQ // 16, BW), bf16), nat(LANES, f32), nat(LANES, f32),
                   jax.ShapeDtypeStruct((BL, 4, SEQ // 4, LANES), f32), jax.ShapeDtypeStruct((BL, 16, SEQ // 16, LANES), f32)]
                  + ACC_SHAPES,
        scratch_shapes=[pltpu.VMEM((BW // LANES, TM, LANES), f32)],
        compiler_params=_cp(("arbitrary", "arbitrary")),
    )(gx1, y, mod, g_post, w_out, oa, ob, g_mix_a, g_mix_b, l1, l4, l16, jnp.asarray(HEAD_EXPAND, bf16),
      jnp.asarray(HEAD_SUM, bf16))


def _attn_in_bwd(dqa, dka, dva, d1, d4, d16, tc, ts1, ts2, w_in, x, gx1, mod, g_pre):
    def body(dqa_ref, dka_ref, dva_ref, dq1_ref, dk1_ref, dv1_ref, dq4_ref, dk4_ref, dv4_ref, dq16_ref, dk16_ref, dv16_ref,
             c_ref, s1_ref, s2_ref, w_ref, x_ref, gx_ref, mod_ref, g_ref, dproj_ref, dx_ref, accb_ref, accg_ref, scr):
        _acc_init(accb_ref, accg_ref)
        c, s1, s2 = c_ref[...], s1_ref[...], s2_ref[...]
        tot = lambda r1, r4, r16: r1[...] + _perm_load(r4, scr, 4) + _perm_load(r16, scr, 16)
        dqb = tot(dq1_ref, dq4_ref, dq16_ref)
        dkb = tot(dk1_ref, dk4_ref, dk16_ref)
        dvb = tot(dv1_ref, dv4_ref, dv16_ref)
        dproj = jnp.concatenate([
            _rope_t(dqa_ref[...], c, s1, s2) * 0.125, _rope_t(dka_ref[...], c, s1, s2), dva_ref[...],
            _rope_t(dqb, c, s1, s2) * 0.125, _rope_t(dkb, c, s1, s2), dvb], axis=1).astype(bf16)
        dproj_ref[...] = dproj
        dh = _dot_nt(dproj, w_ref[...])
        xn, r = _rms(x_ref[...])
        g = g_ref[...]
        dn = dh * (1.0 + mod_ref[1:2, :])
        dx_ref[...] = gx_ref[...] + _rms_bwd(dn * g, xn, r)
        accb_ref[0:1, :] += _colsum(dh * (xn * g))
        accb_ref[1:2, :] += _colsum(dh)
        accg_ref[0:1, :] += _colsum(dn * xn)

    return pl.pallas_call(
        body, name="attn_in_bwd", grid=(BL, NJ),
        in_specs=[_tok(AQ), _tok(AKV), _tok(AKV)] + [_tok(BW)] * 3 + [_perm_spec(4, BW)] * 3 + [_perm_spec(16, BW)] * 3
                 + [_tok(LANES)] * 3 + [_full((D, INW)), _tok(D), _tok(D), MOD_SPEC, _full((1, D))],
        out_specs=[_tok(INW), _tok(D), ACCB_SPEC, ACCG_SPEC],
        out_shape=[jax.ShapeDtypeStruct((BL, SEQ, INW), bf16), jax.ShapeDtypeStruct((BL, SEQ, D), f32)] + ACC_SHAPES,
        scratch_shapes=[pltpu.VMEM((BW // LANES, TM, LANES), f32)],
        compiler_params=_cp(("arbitrary", "arbitrary")),
    )(dqa, dka, dva, *d1, *d4, *d16, tc, ts1, ts2, w_in, x, gx1, mod, g_pre)


def _inv_lane():
    inv = np.float32(THETA) ** (-np.arange(0, ROT, 2, dtype=np.float32) / np.float32(ROT))
    lane = np.arange(LANES) % HD
    return jnp.asarray(np.where(lane < ROT, inv[lane % (ROT // 2)], 0.0).astype(np.float32)[None, :])


def _local_step(x, positions, mod, target, inv_lane, first_weight, later_weights, grad_ready, g_attn_pre,
                g_attn_post, sink_a, g_mix_a, g_mix_b, g_mlp_pre, g_mlp_post):
    tabs = _rope_tables(positions.reshape(BL * SEQ, 1), inv_lane)
    w_in = first_weight(tuple(tabs))
    tc, ts1, ts2 = [t.reshape(BL, SEQ, LANES) for t in tabs]

    (h, qa, ka, va, q1, k1, v1, q4, k4, v4, q16, k16, v16) = _attn_in(x, mod, g_attn_pre, w_in, tc, ts1, ts2)
    seqs = lambda t: t.reshape(t.shape[0] * t.shape[1], t.shape[2], t.shape[3])
    q4, k4, v4, q16, k16, v16 = [seqs(t) for t in (q4, k4, v4, q16, k16, v16)]
    oa, la = _attn_fwd(qa, ka, va, sink_a, max_dist=BLK - 1, o_dtype=f32, name="attn_a_fwd")
    o1, l1 = _attn_fwd(q1, k1, v1, None, max_dist=BLK, o_dtype=bf16, name="attn_b1_fwd")
    o4, l4 = _attn_fwd(q4, k4, v4, None, max_dist=BLK, o_dtype=bf16, name="attn_b4_fwd")
    o16, l16 = _attn_fwd(q16, k16, v16, None, max_dist=BLK, o_dtype=bf16, name="attn_b16_fwd")
    b4 = lambda t: t.reshape(BL, 4, SEQ // 4, t.shape[-1])
    b16 = lambda t: t.reshape(BL, 16, SEQ // 16, t.shape[-1])
    w_out, mlp_weights, mod = later_weights((oa, o1, o4, o16), mod)
    x1, y, mixed, ob = _mix_out(oa, o1, l1, b4(o4), b4(l4), b16(o16), b16(l16), g_mix_a, g_mix_b, w_out, x, mod, g_attn_post)
    w_up, w_down = mlp_weights((x1,))
    h2, u, a = _mlp_up(x1, mod, g_mlp_pre, w_up)
    gx, dy2, accb_d, accg_d = _mlp_down(a, w_down, x1, target, mod, g_mlp_post)

    flat = lambda t: t.reshape(BL * SEQ, t.shape[-1])
    mod = grad_ready("w_down", _matmul_tn(flat(a), flat(dy2), tn=D, col_blocked=False, name="grad_w_down", out_dtype=bf16), mod)
    du, gx1, accb_m, accg_m = _mlp_bwd(dy2, u, w_down, w_up, x1, gx, mod, g_mlp_pre)
    mod = grad_ready("w_up", _matmul_tn(flat(h2), flat(du), tn=D, col_blocked=True, name="grad_w_up", out_dtype=bf16), mod)

    dy, doa, do1, do4, do16, da, dl1, dl4, dl16, accb_o, accg_o = _attn_out_bwd(
        gx1, y, mod, g_attn_post, w_out, oa, ob, g_mix_a, g_mix_b, l1, b4(l4), b16(l16))
    gw_out = _matmul_tn(flat(mixed), flat(dy), tn=D, col_blocked=False, name="grad_w_out")
    dqa, dka, dva, dsink = _attn_bwd(qa, ka, va, doa, da, la, sink_a, max_dist=BLK - 1, name="attn_a_bwd")
    d1 = _attn_bwd(q1, k1, v1, do1, dl1, l1, None, max_dist=BLK, name="attn_b1_bwd")
    d4 = _attn_bwd(q4, k4, v4, seqs(do4), seqs(dl4), l4, None, max_dist=BLK, name="attn_b4_bwd")
    d16 = _attn_bwd(q16, k16, v16, seqs(do16), seqs(dl16), l16, None, max_dist=BLK, name="attn_b16_bwd")
    dproj, grad_x, accb_i, accg_i = _attn_in_bwd(dqa, dka, dva, d1, [b4(t) for t in d4], [b16(t) for t in d16],
                                                 tc, ts1, ts2, w_in, x, gx1, mod, g_attn_pre)
    gw_in = _grad_w_in(flat(h), flat(dproj))
    dsink = grad_ready("w_in_w_out", (gw_in, gw_out), dsink)

    return grad_x, (accb_i, accb_o, accb_m, accb_d, accg_i, accg_o, accg_m, accg_d, dsink)


ADAW = NMOD * D // NCHIP


def _pos():
    return lax.axis_index("x"), lax.axis_index("y"), lax.axis_index("c")


def _flip(v, bit):
    return 1 - v if bit else v


def _all_peers(x, y, c):
    return [(_flip(x, k >> 2 & 1), _flip(y, k >> 1 & 1), _flip(c, k & 1)) for k in range(1, NDEV)]


def _other_chips(x, y):
    return [(1 - x, y), (x, 1 - y), (1 - x, 1 - y)]


def _rcopy(src, dst, send, recv, k, dev, k_recv=None):
    return pltpu.make_async_remote_copy(src_ref=src, dst_ref=dst, send_sem=send.at[k],
                                        recv_sem=recv.at[k if k_recv is None else k_recv],
                                        device_id=dev, device_id_type=MESH)


def _gather_small(src, buf, send, recv):
    x, y, c = _pos()
    me = 4 * x + 2 * y + c
    peers = _all_peers(x, y, c)
    sends = [_rcopy(src, buf.at[me], send, recv, k, p) for k, p in enumerate(peers)]
    for cp in sends:
        cp.start()
    for k, (px, py, pc) in enumerate(peers):
        _rcopy(src, buf.at[4 * px + 2 * py + pc], send, recv, k, (px, py, pc)).wait_recv()
    for cp in sends:
        cp.wait_send()
    return me


def _ada_fwd(c_in, w_ada, b_cols):
    def body(c_ref, w_ref, b_ref, mod_ref, cond_ref, cbuf, mbuf, s1, r1, s2, r2):
        x, y, c = _pos()
        chip = 2 * x + y
        me = _gather_small(c_ref, cbuf, s1, r1)
        cbuf[me] = c_ref[...]
        for i in range(NDEV):
            cond_ref[BL * i:BL * (i + 1), :] = cbuf[i]
        call = cond_ref[...]
        cond = call / (1.0 + jnp.exp(-call))
        cond_ref[...] = cond
        mbuf[chip] = _dot(cond.astype(bf16), w_ref[...].astype(bf16)) + b_ref[...]
        chips = _other_chips(x, y)
        sends = [_rcopy(mbuf.at[chip], mbuf.at[chip], s2, r2, j, (px, py, c)) for j, (px, py) in enumerate(chips)]
        for cp in sends:
            cp.start()
        for j, (px, py) in enumerate(chips):
            _rcopy(mbuf.at[chip], mbuf.at[2 * px + py], s2, r2, j, (px, py, c)).wait_recv()
        for cp in sends:
            cp.wait_send()
        row = lax.broadcasted_iota(jnp.int32, (BL * NDEV, ADAW), 0)
        for s in range(NCHIP):
            slab = mbuf[s]
            for j in range(BL):
                mod_ref[j:j + 1, ADAW * s:ADAW * (s + 1)] = jnp.sum(jnp.where(row == BL * me + j, slab, 0.0), axis=0, keepdims=True)

    vm = pl.BlockSpec(memory_space=pltpu.VMEM)
    return pl.pallas_call(
        body, name="ada_fwd", in_specs=[vm, vm, vm], out_specs=[vm, vm],
        out_shape=[jax.ShapeDtypeStruct((BL, NMOD * D), f32), jax.ShapeDtypeStruct((BL * NDEV, D), f32)],
        scratch_shapes=[pltpu.VMEM((NDEV, BL, D), f32), pltpu.VMEM((NCHIP, BL * NDEV, ADAW), f32),
                        pltpu.SemaphoreType.DMA((NDEV - 1,)), pltpu.SemaphoreType.DMA((NDEV - 1,)),
                        pltpu.SemaphoreType.DMA((NCHIP - 1,)), pltpu.SemaphoreType.DMA((NCHIP - 1,))],
        compiler_params=pltpu.CompilerParams(vmem_limit_bytes=VMEM_LIMIT),
    )(c_in, w_ada, b_cols)


def _small_allreduce(accs, cond_all):
    def body(bi, bo, bm, bd, gi, go, gm, gd, dsink, cond_ref, gw_ref, gb_ref, small_ref, pay, pbuf, dall, s1, r1):
        x, y, c = _pos()
        chip = 2 * x + y
        pay[...] = jnp.zeros_like(pay)
        for b in range(BL):
            for k, (ref, r) in enumerate(((bi, 1), (bi, 0), (bo, 0), (bm, 1), (bm, 0), (bd, 0))):
                pay[b:b + 1, D * k:D * (k + 1)] = ref[b, r:r + 1, :]
        for off, ref, r in ((OFF_G_ATTN_PRE, gi, 0), (OFF_G_ATTN_POST, go, 0), (OFF_G_MIX_A, go, 1), (OFF_G_MLP_PRE, gm, 0),
                            (OFF_G_MLP_POST, gd, 0)):
            pay[BL:BL + 1, off:off + D] = ref[r:r + 1, :]
        eye = lax.broadcasted_iota(jnp.int32, (8, LANES), 0) == lax.broadcasted_iota(jnp.int32, (8, LANES), 1)
        pay[BL:BL + 1, OFF_SINK:OFF_SINK + LANES] = jnp.sum(jnp.where(eye, dsink[...], 0.0), axis=0, keepdims=True)
        pay[BL:BL + 1, OFF_LOSS:OFF_LOSS + LANES] = gd[1:2, 0:LANES]
        me = _gather_small(pay, pbuf, s1, r1)
        pbuf[me] = pay[...]
        small = pbuf[0, BL:BL + 1, :]
        for i in range(1, NDEV):
            small = small + pbuf[i, BL:BL + 1, :]
        small_ref[...] = small
        for i in range(NDEV):
            dall[BL * i:BL * (i + 1), :] = pbuf[i, 0:BL, :]
        gb_ref[...] = jnp.sum(dall[...], axis=0, keepdims=True)
        cols = jnp.zeros((BL * NDEV, ADAW), f32)
        for s in range(NCHIP):
            cols = cols + jnp.where(chip == s, dall[:, ADAW * s:ADAW * (s + 1)], 0.0)
        gw_ref[...] = lax.dot_general(cond_ref[...], cols, (((0,), (0,)), ((), ())), preferred_element_type=f32,
                                      precision=lax.Precision.HIGHEST)

    vm = pl.BlockSpec(memory_space=pltpu.VMEM)
    return pl.pallas_call(
        body, name="small_allreduce", in_specs=[vm] * 10, out_specs=[vm] * 3,
        out_shape=[jax.ShapeDtypeStruct((D, ADAW), f32), jax.ShapeDtypeStruct((1, PAYW), f32), jax.ShapeDtypeStruct((1, PAYW), f32)],
        scratch_shapes=[pltpu.VMEM((4, PAYW), f32), pltpu.VMEM((NDEV, 4, PAYW), f32), pltpu.VMEM((BL * NDEV, PAYW), f32),
                        pltpu.SemaphoreType.DMA((NDEV - 1,)), pltpu.SemaphoreType.DMA((NDEV - 1,))],
        compiler_params=pltpu.CompilerParams(vmem_limit_bytes=VMEM_LIMIT),
    )(*accs, cond_all)


def _half(ref, c):
    r2 = ref.shape[0] // 2
    return ref.at[pl.ds(c * r2 if isinstance(c, int) else pl.multiple_of(c * r2, 16), r2), :]


HBM_SPEC = pl.BlockSpec(memory_space=pltpu.HBM)
SEM_SPEC = pl.BlockSpec(memory_space=pltpu.SEMAPHORE)
EFFECT = pltpu.SideEffectType.DATAFLOW_SIDE_EFFECTING
NLINK = NCHIP - 1


def _in_hbm(a):
    return pltpu.with_memory_space_constraint(a, pltpu.HBM)


NSEM = 8


def _split_start(name, srcs, land_shapes, builds, carry, after=(), lands=None):
    n = len(srcs)
    na, nc = len(after), len(carry)

    def body(*refs):
        src, land = refs[:n], refs[n:2 * n]
        kept = refs[2 * n + na:2 * n + na + nc]
        outs = refs[2 * n + na + nc:]
        send, recv, passed = outs[:n], outs[n:2 * n], outs[4 * n:]
        for t in range(n):
            for out_cp, _ in builds[t](src[t], land[t], send[t], recv[t]):
                out_cp.start()
        for a, b in zip(kept, passed):
            b[...] = a[...]

    if lands is None:
        lands = [lax.empty(s.shape, s.dtype) for s in land_shapes]
    lands = [_in_hbm(a) for a in lands]
    sems = [pltpu.SemaphoreType.DMA((NSEM,))] * (2 * n)
    thru = [pltpu.HBM(a.shape, a.dtype) for a in list(srcs) + lands]
    vm = pl.BlockSpec(memory_space=pltpu.VMEM)
    res = pl.pallas_call(
        body, name=name, out_shape=sems + thru + [jax.ShapeDtypeStruct(a.shape, a.dtype) for a in carry],
        in_specs=[HBM_SPEC] * (2 * n) + [pl.BlockSpec(memory_space=pl.ANY)] * na + [vm] * nc,
        out_specs=[SEM_SPEC] * (2 * n) + [HBM_SPEC] * (2 * n) + [vm] * nc,
        input_output_aliases={i: 2 * n + i for i in range(2 * n)},
        compiler_params=pltpu.CompilerParams(has_side_effects=EFFECT),
    )(*[_in_hbm(a) for a in srcs], *lands, *after, *carry)
    flight = [(res[2 * n + t], res[3 * n + t], res[t], res[n + t]) for t in range(n)]
    return flight, list(res[4 * n:])


def _split_wait(name, flight, builds, after):
    m = len(flight)
    na = len(after)

    def body(*refs):
        src, land, send, recv = refs[:m], refs[m:2 * m], refs[2 * m:3 * m], refs[3 * m:4 * m]
        for t in range(m):
            for out_cp, in_cp in builds[t](src[t], land[t], send[t], recv[t]):
                out_cp.wait_send()
                in_cp.wait_recv()

    ops = [f[0] for f in flight] + [f[1] for f in flight] + [f[2] for f in flight] + [f[3] for f in flight]
    res = pl.pallas_call(
        body, name=name, out_shape=[pltpu.HBM(a.shape, a.dtype) for a in ops[:2 * m]],
        in_specs=[HBM_SPEC] * (2 * m) + [SEM_SPEC] * (2 * m) + [pl.BlockSpec(memory_space=pl.ANY)] * na,
        out_specs=[HBM_SPEC] * (2 * m), input_output_aliases={i: i for i in range(2 * m)},
        compiler_params=pltpu.CompilerParams(has_side_effects=EFFECT),
    )(*ops, *after)
    return res[:m], res[m:2 * m]


def _weight_copies(src, land, send, recv):
    x, y, c = _pos()
    chip = 2 * x + y
    return [(_rcopy(_half(src, c), _half(land.at[chip], c), send, recv, j, (px, py, c)),
             _rcopy(_half(src, c), _half(land.at[2 * px + py], c), send, recv, j, (px, py, c)))
            for j, (px, py) in enumerate(_other_chips(x, y))]


def _grad_copies(src, land, send, recv):
    x, y, c = _pos()
    return [(_rcopy(src.at[2 * px + py], land.at[j], send, recv, j, (px, py, c)),
             _rcopy(src.at[2 * px + py], land.at[j], send, recv, j, (px, py, c)))
            for j, (px, py) in enumerate(_other_chips(x, y))]


NDIRECT = NDEV - 1


def _direct_grad_copies(src, land, send, recv):
    x, y, c = _pos()
    out, arrive = [], []
    for j, (px, py) in enumerate(_other_chips(x, y)):
        for hc in range(2):
            out.append(_rcopy(_half(src.at[2 * px + py], hc), land.at[2 * j + c], send, recv, 2 * j + hc, (px, py, hc),
                              k_recv=2 * j + c))
            arrive.append(_rcopy(_half(src.at[2 * px + py], hc), land.at[2 * j + hc], send, recv, 2 * j + hc, (px, py, hc)))
    own = _rcopy(_half(src.at[2 * x + y], 1 - c), land.at[NDIRECT - 1], send, recv, NDIRECT - 1, (x, y, 1 - c))
    return list(zip(out, arrive)) + [(own, own)]


def _pair_grad_copies(src, land, send, recv):
    x, y, c = _pos()
    r2 = src.shape[1] // 2
    cp = _rcopy(src.at[:, pl.ds(pl.multiple_of((1 - c) * r2, 8), r2), :], land, send, recv, 0, (x, y, 1 - c))
    return [(cp, cp)]


def _pair_weight_copies(src, land, send, recv):
    x, y, c = _pos()
    sib = (x, y, 1 - c)
    cps = []
    for j, (px, py) in enumerate(_other_chips(x, y)):
        mine, theirs = _half(land.at[2 * px + py], c), _half(land.at[2 * px + py], 1 - c)
        cps.append((_rcopy(mine, mine, send, recv, j, sib), _rcopy(theirs, theirs, send, recv, j, sib)))
    own = _rcopy(src, land.at[2 * x + y], send, recv, NLINK, sib)
    return cps + [(own, own)]


RS_ROWS = 128


def _pair_add(g, landed, c_arr, name):
    _, r2, cw = landed.shape
    nr = r2 // RS_ROWS

    def body(c_ref, g_ref, p_ref, o_ref):
        o_ref[...] = (g_ref[...] + p_ref[...]).astype(bf16)

    gs = pltpu.PrefetchScalarGridSpec(
        num_scalar_prefetch=1, grid=(NCHIP, nr),
        in_specs=[pl.BlockSpec((None, RS_ROWS, cw), lambda s, j, c: (s, c[0] * nr + j, 0)),
                  pl.BlockSpec((None, RS_ROWS, cw), lambda s, j, c: (s, j, 0))],
        out_specs=pl.BlockSpec((None, RS_ROWS, cw), lambda s, j, c: (s, j, 0)))
    return pl.pallas_call(body, name=name, grid_spec=gs, out_shape=jax.ShapeDtypeStruct((NCHIP, r2, cw), bf16),
                          compiler_params=_cp(("arbitrary", "arbitrary")))(c_arr, g, landed)


def _chip_add(own, landed, pos_arr, name):
    nl, r2, cw = landed.shape
    nr = r2 // RS_ROWS
    whole = own.shape[1] == 2 * r2

    def body(s_ref, h_ref, q_ref, o_ref):
        acc = h_ref[...].astype(f32)
        for j in range(nl):
            acc = acc + q_ref[j].astype(f32)
        o_ref[...] = acc

    gs = pltpu.PrefetchScalarGridSpec(
        num_scalar_prefetch=1, grid=(nr,),
        in_specs=[pl.BlockSpec((None, RS_ROWS, cw), lambda j, s: (s[0], (s[1] * nr if whole else 0) + j, 0)),
                  pl.BlockSpec((nl, RS_ROWS, cw), lambda j, s: (0, j, 0))],
        out_specs=pl.BlockSpec((RS_ROWS, cw), lambda j, s: (s[1] * nr + j, 0)))
    return pl.pallas_call(body, name=name, grid_spec=gs, out_shape=jax.ShapeDtypeStruct((2 * r2, cw), f32),
                          compiler_params=_cp(("arbitrary",)))(pos_arr, own, landed)


def _pair_gather_copies(src, land, send, recv):
    x, y, c = _pos()
    sib = (x, y, 1 - c)
    return [(_rcopy(_half(land, c), _half(land, c), send, recv, 0, sib),
             _rcopy(_half(land, 1 - c), _half(land, 1 - c), send, recv, 0, sib))]


def _adamw_math(w, g, m, v):
    m = B1 * m + (1.0 - B1) * g
    v = B2 * v + (1.0 - B2) * jnp.square(g)
    m_hat = m / (1.0 - B1 ** STEP)
    v_hat = v / (1.0 - B2 ** STEP)
    return -LR * (m_hat / (jnp.sqrt(v_hat) + AEPS) + WD * w), m, v


ADAM_ROWS = 256


def _adamw(w, g, m, v, name):
    r, cw = w.shape

    def body(w_ref, g_ref, m_ref, v_ref, go_ref, d_ref, mo_ref, vo_ref):
        g = g_ref[...]
        go_ref[...] = g
        d_ref[...], mo_ref[...], vo_ref[...] = _adamw_math(w_ref[...], g, m_ref[...], v_ref[...])

    spec = pl.BlockSpec((ADAM_ROWS, cw), lambda i: (i, 0))
    return pl.pallas_call(body, name=name, grid=(r // ADAM_ROWS,), in_specs=[spec] * 4, out_specs=[spec] * 4,
                          out_shape=[jax.ShapeDtypeStruct((r, cw), f32)] * 4, compiler_params=_cp(("arbitrary",)))(w, g, m, v)


SMALL = (("b_ada", None, PAYW), ("g_attn_pre", OFF_G_ATTN_PRE, D), ("g_attn_post", OFF_G_ATTN_POST, D), ("sink_a", OFF_SINK, 8),
         ("g_mix_a", OFF_G_MIX_A, AQ), ("g_mix_b", OFF_G_MIX_B, BW), ("g_mlp_pre", OFF_G_MLP_PRE, D), ("g_mlp_post", OFF_G_MLP_POST, D))


def _adamw_small(small, gb, params):
    n = len(SMALL)

    def body(*refs):
        small_ref, gb_ref = refs[:2]
        wmv = refs[2:2 + 3 * n]
        loss_ref = refs[2 + 3 * n]
        outs = refs[3 + 3 * n:]
        loss_ref[...] = small_ref[:, OFF_LOSS:OFF_LOSS + 1] * (0.5 / D)
        for i, (_, off, width) in enumerate(SMALL):
            g = gb_ref[...] if off is None else small_ref[:, off:off + width]
            w_ref, m_ref, v_ref = wmv[3 * i:3 * i + 3]
            outs[4 * i][...] = g
            outs[4 * i + 1][...], outs[4 * i + 2][...], outs[4 * i + 3][...] = _adamw_math(w_ref[...], g, m_ref[...], v_ref[...])

    vm = pl.BlockSpec(memory_space=pltpu.VMEM)
    out_shape = [jax.ShapeDtypeStruct((1, 1), f32)]
    for _, _, width in SMALL:
        out_shape += [jax.ShapeDtypeStruct((1, width), f32)] * 4
    flat = [a for wmv in params for a in wmv]
    res = pl.pallas_call(body, name="adamw_small", in_specs=[vm] * (2 + 3 * n), out_specs=[vm] * len(out_shape),
                         out_shape=out_shape)(small, gb, *flat)
    return res[0], {name: res[1 + 4 * i:5 + 4 * i] for i, (name, _, _) in enumerate(SMALL)}


def kernel(x, c, positions, w_ada, b_ada, g_attn_pre, g_attn_post, w_in, sink_a, g_mix_a, g_mix_b, w_out, g_mlp_pre, g_mlp_post, w_up, w_down, loss_target, m_w_ada, m_b_ada, m_g_attn_pre, m_g_attn_post, m_w_in, m_sink_a, m_g_mix_a, m_g_mix_b, m_w_out, m_g_mlp_pre, m_g_mlp_post, m_w_up, m_w_down, v_w_ada, v_b_ada, v_g_attn_pre, v_g_attn_post, v_w_in, v_sink_a, v_g_mix_a, v_g_mix_b, v_w_out, v_g_mlp_pre, v_g_mlp_post, v_w_up, v_w_down):
    given = dict(w_ada=w_ada, b_ada=b_ada, g_attn_pre=g_attn_pre, g_attn_post=g_attn_post, w_in=w_in, sink_a=sink_a, g_mix_a=g_mix_a,
                 g_mix_b=g_mix_b, w_out=w_out, g_mlp_pre=g_mlp_pre, g_mlp_post=g_mlp_post, w_up=w_up, w_down=w_down)
    moms = dict(w_ada=(m_w_ada, v_w_ada), b_ada=(m_b_ada, v_b_ada), g_attn_pre=(m_g_attn_pre, v_g_attn_pre),
                g_attn_post=(m_g_attn_post, v_g_attn_post), w_in=(m_w_in, v_w_in), sink_a=(m_sink_a, v_sink_a),
                g_mix_a=(m_g_mix_a, v_g_mix_a), g_mix_b=(m_g_mix_b, v_g_mix_b), w_out=(m_w_out, v_w_out),
                g_mlp_pre=(m_g_mlp_pre, v_g_mlp_pre), g_mlp_post=(m_g_mlp_post, v_g_mlp_post), w_up=(m_w_up, v_w_up),
                w_down=(m_w_down, v_w_down))
    order = ["w_ada", "b_ada", "g_attn_pre", "g_attn_post", "w_in", "sink_a", "g_mix_a", "g_mix_b", "w_out", "g_mlp_pre",
             "g_mlp_post", "w_up", "w_down"]
    xi, yi, ci = _pos()
    chip = 2 * xi + yi

    c_arr = jnp.reshape(ci, (1,)).astype(jnp.int32)
    pos_arr = jnp.stack([chip, ci]).astype(jnp.int32)
    big = ("w_in", "w_out", "w_up", "w_down")

    b_cols = lax.dynamic_slice(b_ada, (0, chip * ADAW), (1, ADAW))
    mod, cond_all = _ada_fwd(c, w_ada[0], b_cols)
    shards = [given[n][0].astype(bf16) for n in big]
    gathered = [jax.ShapeDtypeStruct((NCHIP,) + s.shape, bf16) for s in shards]
    flight_in, (mod,) = _split_start("weights_start_first", shards[:1], gathered[:1], [_weight_copies], [mod])
    flight_rest, (mod, inv_lane) = _split_start("weights_start_rest", shards[1:], gathered[1:], [_weight_copies] * 3,
                                                [mod, _inv_lane()])
    mod = mod.reshape(BL, NMOD, D)

    def first_weight(after):
        srcs, lands = _split_wait("weights_wait_first", flight_in, [_weight_copies], after)
        cross, _ = _split_start("weights_pair_start_first", srcs, None, [_pair_weight_copies], [], lands=lands)
        _, (win_g,) = _split_wait("weights_pair_wait_first", cross, [_pair_weight_copies], ())
        return win_g.transpose(1, 0, 2).reshape(D, INW)

    def later_weights(after, carry):
        srcs, lands = _split_wait("weights_wait_rest", flight_rest, [_weight_copies] * 3, after)
        fl, (carry,) = _split_start("weights_pair_start_rest", srcs, None, [_pair_weight_copies] * 3, [carry], lands=lands)
        _, (wout_g,) = _split_wait("weights_pair_wait_out", fl[:1], [_pair_weight_copies], ())

        def mlp_weights(after):
            _, (wup_g, wdn_g) = _split_wait("weights_pair_wait_mlp", fl[1:], [_pair_weight_copies] * 2, after)
            return wup_g, wdn_g.reshape(DFF, D)

        return wout_g.reshape(D, D), mlp_weights, carry

    crossing, pending = {}, {}

    def grad_ready(group, g, carry):
        if group != "w_in_w_out":
            slab = g.reshape(NCHIP, DFF // NCHIP, D) if group == "w_down" else g
            land = jax.ShapeDtypeStruct((NDIRECT, slab.shape[1] // 2, slab.shape[2]), bf16)
            fl, (carry,) = _split_start("grad_start_" + group, [slab], [land], [_direct_grad_copies], [carry])
            pending[group] = ((group,), fl, [_direct_grad_copies])
            return carry
        names = ("w_in", "w_out")
        slabs = [g[0], g[1].reshape(NCHIP, D // NCHIP, D)]
        fl, (carry,) = _split_start("grad_pair_start_" + group, slabs,
                                    [jax.ShapeDtypeStruct((NCHIP, s.shape[1] // 2, s.shape[2]), f32) for s in slabs],
                                    [_pair_grad_copies] * len(names), [carry])
        crossing[group] = (names, fl)
        return carry

    def grad_reduce(group, after, carry):
        names, fl = crossing[group]
        slabs, landed = _split_wait("grad_pair_wait_" + group, fl, [_pair_grad_copies] * len(names), after)
        halves = [_pair_add(s, p, c_arr, "grad_pair_sum_" + n) for s, p, n in zip(slabs, landed, names)]
        fl, (carry,) = _split_start("grad_start_" + group, halves,
                                    [jax.ShapeDtypeStruct((NLINK,) + h.shape[1:], bf16) for h in halves],
                                    [_grad_copies] * len(names), [carry])
        pending[group] = (names, fl, [_grad_copies] * len(names))
        return carry

    grad_x, accs = _local_step(x, positions, mod, loss_target, inv_lane, first_weight, later_weights, grad_ready,
                               g_attn_pre, g_attn_post, sink_a, g_mix_a, g_mix_b, g_mlp_pre, g_mlp_post)

    grads, out = {}, {}

    def update(n):
        g, d, m2, v2 = _adamw(given[n][0], grads[n], moms[n][0][0], moms[n][1][0], "adamw_" + n)
        out[n] = (g[None], d[None], m2[None], v2[None])
        return v2

    def finish(groups, after):
        names = sum((pending[g][0] for g in groups), ())
        fl = sum((pending[g][1] for g in groups), [])
        halves, landed = _split_wait("grad_wait_" + groups[0], fl, sum((pending[g][2] for g in groups), []), after)
        flights = []
        for h, q, n in zip(halves, landed, names):
            full = _chip_add(h, q, pos_arr, "grad_chip_sum_" + n)
            flights.append(_split_start("grad_gather_start_" + n, [jnp.zeros((8, LANES), f32)], None, [_pair_gather_copies],
                                        [], lands=[full])[0])
        last = None
        for n, fl1 in zip(names, flights):
            after = (flights[-1][0][0],) if last is None and fl1 is not flights[-1] else () if last is None else (last,)
            _, (grads[n],) = _split_wait("grad_gather_wait_" + n, fl1, [_pair_gather_copies], after)
            last = update(n)
        return last

    grads["w_ada"], gb, small = _small_allreduce(accs, cond_all)
    small = grad_reduce("w_in_w_out", (small,), small)
    last = finish(("w_down", "w_up"), (small,))
    finish(("w_in_w_out",), (last, update("w_ada")))
    loss, res = _adamw_small(small, gb, [(given[n], moms[n][0], moms[n][1]) for n, _, _ in SMALL])
    for n, _, _ in SMALL:
        out[n] = tuple(res[n])
    return (loss.reshape(()), grad_x, *[out[n][0] for n in order], *[out[n][1] for n in order],
            *[out[n][2] for n in order], *[out[n][3] for n in order])
```

```python
import functools

import numpy as np
import jax
import jax.numpy as jnp
from jax import lax
from jax.experimental import pallas as pl
from jax.experimental.pallas import tpu as pltpu

f32 = jnp.float32
bf16 = jnp.bfloat16
MESH = pl.DeviceIdType.MESH

D = 1024
SEQ = 2048
BL = 2
HD = 64
AQ = 512
AKV = 128
BW = 512
INW = 2304
DFF = 4096
NMOD = 6
ROT = 16
THETA = 500000.0
EPS = 1e-6
NEG = -1e30
BLK = 128
TM = 512
NJ = SEQ // TM
LANES = 128
NCHIP = 4
NDEV = 8
VMEM_LIMIT = 56 << 20

LR, B1, B2, AEPS, WD, STEP = 0.001, 0.9, 0.999, 1e-08, 0.01, 10

OFF_G_ATTN_PRE, OFF_G_ATTN_POST, OFF_G_MIX_A, OFF_G_MIX_B = 0, 1024, 2048, 2560
OFF_G_MLP_PRE, OFF_G_MLP_POST, OFF_SINK, OFF_LOSS = 3072, 4096, 5120, 5248
PAYW = NMOD * D


def _cp(sem=None):
    return pltpu.CompilerParams(dimension_semantics=sem, vmem_limit_bytes=VMEM_LIMIT)


def _dot(a, b):
    return jnp.dot(a, b, preferred_element_type=f32)


def _dot_nt(a, b):
    return lax.dot_general(a, b, (((1,), (1,)), ((), ())), preferred_element_type=f32)


def _dot_tn(a, b):
    return lax.dot_general(a, b, (((0,), (0,)), ((), ())), preferred_element_type=f32)


def _rms(x):
    r = lax.rsqrt(jnp.mean(x * x, axis=-1, keepdims=True) + EPS)
    return x * r, r


def _rms_bwd(dy, y, r):
    return r * (dy - y * jnp.mean(dy * y, axis=-1, keepdims=True))


def _colsum(v):
    return jnp.sum(v, axis=0, keepdims=True)


def _rope(p, c, s1, s2):
    outs = []
    for c0 in range(0, p.shape[1], LANES):
        pc = p[:, c0:c0 + LANES]
        outs.append(pc * c + pltpu.roll(pc, LANES - ROT // 2, 1) * s1 + pltpu.roll(pc, ROT // 2, 1) * s2)
    return outs[0] if len(outs) == 1 else jnp.concatenate(outs, axis=1)


def _rope_t(g, c, s1, s2):
    outs = []
    for c0 in range(0, g.shape[1], LANES):
        gc = g[:, c0:c0 + LANES]
        outs.append(gc * c + pltpu.roll(gc * s1, ROT // 2, 1) + pltpu.roll(gc * s2, LANES - ROT // 2, 1))
    return outs[0] if len(outs) == 1 else jnp.concatenate(outs, axis=1)


def _perm_store(val, scr, out_ref, d):
    nc = val.shape[1] // LANES
    for c in range(nc):
        scr[c] = val[:, LANES * c:LANES * (c + 1)]
    for c in range(nc):
        for r in range(d):
            out_ref[r, :, LANES * c:LANES * (c + 1)] = scr[c, pl.ds(r, TM // d, stride=d), :].astype(out_ref.dtype)


def _perm_load(in_ref, scr, d):
    nc = in_ref.shape[-1] // LANES
    for c in range(nc):
        for r in range(d):
            scr[c, pl.ds(r, TM // d, stride=d), :] = in_ref[r, :, LANES * c:LANES * (c + 1)].astype(f32)
    return jnp.concatenate([scr[c] for c in range(nc)], axis=1)


def _tok(w, dtype=None):
    return pl.BlockSpec((None, TM, w), lambda b, j: (b, j, 0))


def _perm_spec(d, w):
    return pl.BlockSpec((None, d, TM // d, w), lambda b, j: (b, 0, j, 0))


def _full(shape):
    n = len(shape)
    return pl.BlockSpec(shape, lambda b, j: (0,) * n)


MOD_SPEC = pl.BlockSpec((None, NMOD, D), lambda b, j: (b, 0, 0))
ACCB_SPEC = pl.BlockSpec((None, 8, D), lambda b, j: (b, 0, 0))
ACCG_SPEC = pl.BlockSpec((8, D), lambda b, j: (0, 0))
ACC_SHAPES = [jax.ShapeDtypeStruct((BL, 8, D), f32), jax.ShapeDtypeStruct((8, D), f32)]


def _acc_init(accb_ref, accg_ref):
    b, j = pl.program_id(0), pl.program_id(1)

    @pl.when(j == 0)
    def _():
        accb_ref[...] = jnp.zeros_like(accb_ref)

    @pl.when((b == 0) & (j == 0))
    def _():
        accg_ref[...] = jnp.zeros_like(accg_ref)


def _rope_tables(pos_col, inv_lane):
    def body(p_ref, inv_ref, c_ref, s1_ref, s2_ref):
        ang = p_ref[...].astype(f32) * inv_ref[...]
        j = lax.broadcasted_iota(jnp.int32, (TM, LANES), 1) % HD
        cs, sn = jnp.cos(ang), jnp.sin(ang)
        c_ref[...] = jnp.where(j < ROT, cs, 1.0)
        s1_ref[...] = jnp.where(j < ROT // 2, -sn, 0.0)
        s2_ref[...] = jnp.where((j >= ROT // 2) & (j < ROT), sn, 0.0)

    n = BL * SEQ // TM
    return pl.pallas_call(
        body, name="rope_tables", grid=(n,),
        in_specs=[pl.BlockSpec((TM, 1), lambda i: (i, 0)), pl.BlockSpec((1, LANES), lambda i: (0, 0))],
        out_specs=[pl.BlockSpec((TM, LANES), lambda i: (i, 0))] * 3,
        out_shape=[jax.ShapeDtypeStruct((BL * SEQ, LANES), f32)] * 3,
    )(pos_col, inv_lane)


def _attn_in(x, mod, g_pre, w_in, tc, ts1, ts2):
    def body(x_ref, mod_ref, g_ref, wg_ref, c_ref, s1_ref, s2_ref,
             h_ref, qa_ref, ka_ref, va_ref, q1_ref, k1_ref, v1_ref, q4_ref, k4_ref, v4_ref, q16_ref, k16_ref, v16_ref,
             w_ref, scr):
        @pl.when((pl.program_id(0) == 0) & (pl.program_id(1) == 0))
        def _():
            w_ref[...] = jnp.concatenate([wg_ref[s] for s in range(NCHIP)], axis=1)

        xn, _ = _rms(x_ref[...])
        h = (xn * g_ref[...]) * (1.0 + mod_ref[1:2, :]) + mod_ref[0:1, :]
        hb = h.astype(bf16)
        h_ref[...] = hb
        proj = _dot(hb, w_ref[...])
        c, s1, s2 = c_ref[...], s1_ref[...], s2_ref[...]
        o1, o2, o3, o4, o5 = AQ, AQ + AKV, AQ + 2 * AKV, AQ + 2 * AKV + BW, AQ + 2 * AKV + 2 * BW
        qa_ref[...] = (_rope(proj[:, :o1], c, s1, s2) * 0.125).astype(bf16)
        ka_ref[...] = _rope(proj[:, o1:o2], c, s1, s2).astype(bf16)
        va_ref[...] = proj[:, o2:o3].astype(bf16)
        qb = _rope(proj[:, o3:o4], c, s1, s2) * 0.125
        kb = _rope(proj[:, o4:o5], c, s1, s2)
        vb = proj[:, o5:]
        for val, r1, r4, r16 in ((qb, q1_ref, q4_ref, q16_ref), (kb, k1_ref, k4_ref, k16_ref), (vb, v1_ref, v4_ref, v16_ref)):
            r1[...] = val.astype(bf16)
            _perm_store(val, scr, r4, 4)
            _perm_store(val, scr, r16, 16)

    nat = lambda w: jax.ShapeDtypeStruct((BL, SEQ, w), bf16)
    p4 = jax.ShapeDtypeStruct((BL, 4, SEQ // 4, BW), bf16)
    p16 = jax.ShapeDtypeStruct((BL, 16, SEQ // 16, BW), bf16)
    return pl.pallas_call(
        body, name="attn_in", grid=(BL, NJ),
        in_specs=[_tok(D), MOD_SPEC, _full((1, D)), _full((NCHIP, D, INW // NCHIP)), _tok(LANES), _tok(LANES), _tok(LANES)],
        out_specs=([_tok(D), _tok(AQ), _tok(AKV), _tok(AKV)] + [_tok(BW)] * 3 + [_perm_spec(4, BW)] * 3 + [_perm_spec(16, BW)] * 3
                   + [_full((D, INW))]),
        out_shape=[nat(D), nat(AQ), nat(AKV), nat(AKV)] + [nat(BW)] * 3 + [p4] * 3 + [p16] * 3
                  + [jax.ShapeDtypeStruct((D, INW), bf16)],
        scratch_shapes=[pltpu.VMEM((BW // LANES, TM, LANES), f32)],
        compiler_params=_cp(("arbitrary", "arbitrary")),
    )(x, mod, g_pre, w_in, tc, ts1, ts2)


def _kv_cat(cur_ref, prev_ref, p, gqa, cache):
    def one(ref):
        if not gqa:
            return ref[:, LANES * p:LANES * (p + 1)]
        k = ref[...]
        kr = pltpu.roll(k, HD, 1)
        lo = lax.broadcasted_iota(jnp.int32, k.shape, 1) < HD
        return jnp.where(lo, k, kr) if p < 2 else jnp.where(lo, kr, k)

    key = (id(cur_ref), p // 2 if gqa else p)
    if key not in cache:
        cache[key] = one(cur_ref) if prev_ref is None else jnp.concatenate([one(prev_ref), one(cur_ref)], axis=0)
    return cache[key]


def _lane_half(a, hh):
    lo = lax.broadcasted_iota(jnp.int32, a.shape, 1) < HD
    return jnp.where(lo, a, jnp.zeros_like(a)) if hh == 0 else jnp.where(lo, jnp.zeros_like(a), a)


def _attn_fwd(q, k, v, sink, *, max_dist, o_dtype, name):
    n, l, w = q.shape
    wk = k.shape[-1]
    nb = l // BLK
    gqa = wk != w
    has_sink = sink is not None

    def body(*refs):
        if has_sink:
            sink_ref, refs = refs[0], refs[1:]
        if nb > 1:
            q_ref, kc_ref, kp_ref, vc_ref, vp_ref, o_ref, lse_ref, sscr, pscr, dscr = refs
        else:
            q_ref, kc_ref, vc_ref, o_ref, lse_ref, sscr, pscr, dscr = refs
        i = pl.program_id(1)
        qi = lax.broadcasted_iota(jnp.int32, (BLK, BLK), 0)
        kj = lax.broadcasted_iota(jnp.int32, (BLK, BLK), 1)
        tri = kj <= qi
        eye = kj == qi
        cache = {}
        for p in range(w // LANES):
            qpair = q_ref[:, LANES * p:LANES * (p + 1)]
            kcat = _kv_cat(kc_ref, kp_ref if nb > 1 else None, p, gqa, cache)
            for hh in range(2):
                s = _dot_nt(_lane_half(qpair, hh), kcat)
                if nb > 1:
                    sp = jnp.where(i > 0, s[:, :BLK], NEG)
                    sscr[2 * p + hh] = jnp.where(tri, s[:, BLK:], sp)
                    if diag:
                        dscr[2 * p + hh] = jnp.where(eye, sp, NEG)
                else:
                    sscr[2 * p + hh] = jnp.where(tri, s, NEG)
        lane = lax.broadcasted_iota(jnp.int32, (BLK, LANES), 1)
        lse_all = jnp.zeros((BLK, LANES), f32)
        for p in range(w // LANES):
            for hh in range(2):
                h = 2 * p + hh
                comb = sscr[h]
                if diag:
                    dtile = dscr[h]
                    m = jnp.max(jnp.maximum(comb, dtile), axis=-1, keepdims=True)
                else:
                    m = jnp.max(comb, axis=-1, keepdims=True)
                if has_sink:
                    sk = sink_ref[0, h]
                    m = jnp.maximum(m, sk)
                e = jnp.exp(comb - m)
                if diag:
                    ed = jnp.exp(dtile - m)
                    den = jnp.sum(e + ed, axis=-1, keepdims=True)
                else:
                    den = jnp.sum(e, axis=-1, keepdims=True)
                if has_sink:
                    den = den + jnp.exp(sk - m)
                inv = 1.0 / den
                if nb > 1:
                    pscr[h, :, :BLK] = (jnp.where(tri, ed if diag else 0.0, e) * inv).astype(bf16)
                    pscr[h, :, BLK:] = (jnp.where(tri, e, 0.0) * inv).astype(bf16)
                else:
                    pscr[h] = (e * inv).astype(bf16)
                lse_all = jnp.where(lane == h, jnp.broadcast_to(m + jnp.log(den), (BLK, LANES)), lse_all)
        lse_ref[...] = lse_all
        for p in range(w // LANES):
            vcat = _kv_cat(vc_ref, vp_ref if nb > 1 else None, p, gqa, cache)
            key = ("halves", id(vc_ref), p // 2 if gqa else p)
            if key not in cache:
                cache[key] = (_lane_half(vcat, 0), _lane_half(vcat, 1))
            o_ref[:, LANES * p:LANES * (p + 1)] = (_dot(pscr[2 * p], cache[key][0])
                                                   + _dot(pscr[2 * p + 1], cache[key][1])).astype(o_ref.dtype)

    assert max_dist in (BLK - 1, BLK)
    diag = nb > 1 and max_dist == BLK
    cur = lambda ww: pl.BlockSpec((None, BLK, ww), lambda a, i: (a, i, 0))
    prev = lambda ww: pl.BlockSpec((None, BLK, ww), lambda a, i: (a, jnp.maximum(i - 1, 0), 0))
    in_specs = [cur(w), cur(wk)] + ([prev(wk)] if nb > 1 else []) + [cur(wk)] + ([prev(wk)] if nb > 1 else [])
    args = [q, k] + ([k] if nb > 1 else []) + [v] + ([v] if nb > 1 else [])
    if has_sink:
        in_specs = [pl.BlockSpec(memory_space=pltpu.SMEM)] + in_specs
        args = [sink] + args
    return pl.pallas_call(
        body, name=name, grid=(n, nb), in_specs=in_specs,
        out_specs=[cur(w), cur(LANES)],
        out_shape=[jax.ShapeDtypeStruct((n, l, w), o_dtype), jax.ShapeDtypeStruct((n, l, LANES), f32)],
        scratch_shapes=[pltpu.VMEM((w // HD, BLK, BLK), f32), pltpu.VMEM((w // HD, BLK, 2 * BLK if nb > 1 else BLK), bf16),
                        pltpu.VMEM((w // HD if diag else 1, BLK, BLK), f32)],
        compiler_params=_cp(("arbitrary", "arbitrary")),
    )(*args)


def _attn_bwd(q, k, v, do, delta, lse, sink, *, max_dist, name):
    n, l, w = q.shape
    wk = k.shape[-1]
    nb = l // BLK
    gqa = wk != w
    has_sink = sink is not None

    def body(*refs):
        if has_sink:
            sink_ref, refs = refs[0], refs[1:]
        if nb > 1:
            q_ref, kc_ref, kp_ref, vc_ref, vp_ref, do_ref, delta_ref, lse_ref = refs[:8]
            rest = refs[8:]
        else:
            q_ref, kc_ref, vc_ref, do_ref, delta_ref, lse_ref = refs[:6]
            rest = refs[6:]
        if has_sink:
            dq_ref, dk_ref, dv_ref, dsink_ref = rest[:4]
            rest = rest[4:]
        else:
            dq_ref, dk_ref, dv_ref = rest[:3]
            rest = rest[3:]
        step = pl.program_id(1)
        blk_idx = nb - 1 - step
        if nb > 1:
            ck, cv = rest[:2]
            rest = rest[2:]

            @pl.when(step == 0)
            def _():
                ck[...] = jnp.zeros_like(ck)
                cv[...] = jnp.zeros_like(cv)

        sscr, dpscr, pscr, dsscr = rest[:4]
        if diag:
            dscr, ddscr = rest[4:]
        if has_sink:
            @pl.when((pl.program_id(0) == 0) & (step == 0))
            def _():
                dsink_ref[...] = jnp.zeros_like(dsink_ref)

        lane = lax.broadcasted_iota(jnp.int32, (BLK, LANES), 1)
        lo = lane < HD
        qi = lax.broadcasted_iota(jnp.int32, (BLK, BLK), 0)
        kj = lax.broadcasted_iota(jnp.int32, (BLK, BLK), 1)
        tri = kj <= qi
        eye = kj == qi
        cache = {}
        kp, vp = (kp_ref, vp_ref) if nb > 1 else (None, None)
        rows = 2 * BLK if nb > 1 else BLK
        for p in range(w // LANES):
            sl = slice(LANES * p, LANES * (p + 1))
            qpair, dopair = q_ref[:, sl], do_ref[:, sl]
            kcat, vcat = _kv_cat(kc_ref, kp, p, gqa, cache), _kv_cat(vc_ref, vp, p, gqa, cache)
            for hh in range(2):
                h = 2 * p + hh
                s = _dot_nt(_lane_half(qpair, hh), kcat)
                dp = _dot_nt(_lane_half(dopair, hh), vcat)
                if nb > 1:
                    sp = jnp.where(blk_idx > 0, s[:, :BLK], NEG)
                    sscr[h] = jnp.where(tri, s[:, BLK:], sp)
                    dpscr[h] = jnp.where(tri, dp[:, BLK:], dp[:, :BLK])
                    if diag:
                        dscr[h] = jnp.where(eye, sp, NEG)
                        ddscr[h] = dp[:, :BLK]
                else:
                    sscr[h] = jnp.where(tri, s, NEG)
                    dpscr[h] = dp
        for p in range(w // LANES):
            for hh in range(2):
                h = 2 * p + hh
                lse_b = jnp.broadcast_to(lse_ref[:, h:h + 1], (BLK, BLK))
                delta = jnp.broadcast_to(delta_ref[:, h:h + 1], (BLK, BLK))
                pr = jnp.exp(sscr[h] - lse_b)
                ds = pr * (dpscr[h] - delta)
                if nb > 1:
                    if diag:
                        prd = jnp.exp(dscr[h] - lse_b)
                        dsd = prd * (ddscr[h] - delta)
                    else:
                        prd = dsd = 0.0
                    pscr[h, :, :BLK] = jnp.where(tri, prd, pr).astype(bf16)
                    pscr[h, :, BLK:] = jnp.where(tri, pr, 0.0).astype(bf16)
                    dsscr[h, :, :BLK] = jnp.where(tri, dsd, ds).astype(bf16)
                    dsscr[h, :, BLK:] = jnp.where(tri, ds, 0.0).astype(bf16)
                else:
                    pscr[h] = pr.astype(bf16)
                    dsscr[h] = ds.astype(bf16)
                if has_sink:
                    dsk = -jnp.sum(jnp.where(lane == 0, jnp.exp(sink_ref[0, h] - lse_b) * delta, 0.0), keepdims=True)
                    dsink_ref[h:h + 1, :] += jnp.broadcast_to(dsk, (1, LANES))
        gk = [jnp.zeros((rows, LANES), f32), jnp.zeros((rows, LANES), f32)]
        gv = [jnp.zeros((rows, LANES), f32), jnp.zeros((rows, LANES), f32)]
        for p in range(w // LANES):
            sl = slice(LANES * p, LANES * (p + 1))
            qpair, dopair = q_ref[:, sl], do_ref[:, sl]
            kcat = _kv_cat(kc_ref, kp, p, gqa, cache)
            key = ("halves", p // 2 if gqa else p)
            if key not in cache:
                cache[key] = (_lane_half(kcat, 0), _lane_half(kcat, 1))
            dq_ref[:, sl] = _dot(dsscr[2 * p], cache[key][0]) + _dot(dsscr[2 * p + 1], cache[key][1])
            dk_pair = _dot_tn(dsscr[2 * p], _lane_half(qpair, 0)) + _dot_tn(dsscr[2 * p + 1], _lane_half(qpair, 1))
            dv_pair = _dot_tn(pscr[2 * p], _lane_half(dopair, 0)) + _dot_tn(pscr[2 * p + 1], _lane_half(dopair, 1))
            if gqa:
                gk[p // 2] = gk[p // 2] + dk_pair
                gv[p // 2] = gv[p // 2] + dv_pair
            elif nb > 1:
                dk_ref[:, sl] = dk_pair[BLK:] + ck[:, sl]
                dv_ref[:, sl] = dv_pair[BLK:] + cv[:, sl]
                ck[:, sl] = dk_pair[:BLK]
                cv[:, sl] = dv_pair[:BLK]
            else:
                dk_ref[:, sl] = dk_pair
                dv_ref[:, sl] = dv_pair
        if gqa:
            lor = lax.broadcasted_iota(jnp.int32, (rows, LANES), 1) < HD
            fold = lambda g: jnp.where(lor, g[0] + pltpu.roll(g[0], HD, 1), g[1] + pltpu.roll(g[1], HD, 1))
            dk_full, dv_full = fold(gk), fold(gv)
            dk_ref[...] = dk_full[BLK:] + ck[...]
            dv_ref[...] = dv_full[BLK:] + cv[...]
            ck[...] = dk_full[:BLK]
            cv[...] = dv_full[:BLK]

    assert max_dist in (BLK - 1, BLK)
    diag = nb > 1 and max_dist == BLK
    cur = lambda ww: pl.BlockSpec((None, BLK, ww), lambda a, i: (a, nb - 1 - i, 0))
    prev = lambda ww: pl.BlockSpec((None, BLK, ww), lambda a, i: (a, jnp.maximum(nb - 2 - i, 0), 0))
    in_specs = ([cur(w), cur(wk)] + ([prev(wk)] if nb > 1 else []) + [cur(wk)] + ([prev(wk)] if nb > 1 else [])
                + [cur(w), cur(LANES), cur(LANES)])
    args = [q, k] + ([k] if nb > 1 else []) + [v] + ([v] if nb > 1 else []) + [do, delta, lse]
    out_specs = [cur(w), cur(wk), cur(wk)]
    out_shape = [jax.ShapeDtypeStruct((n, l, w), f32), jax.ShapeDtypeStruct((n, l, wk), f32), jax.ShapeDtypeStruct((n, l, wk), f32)]
    if has_sink:
        in_specs = [pl.BlockSpec(memory_space=pltpu.SMEM)] + in_specs
        args = [sink] + args
        out_specs.append(pl.BlockSpec((8, LANES), lambda a, i: (0, 0)))
        out_shape.append(jax.ShapeDtypeStruct((8, LANES), f32))
    nh = w // HD
    scratch = [pltpu.VMEM((BLK, wk), f32), pltpu.VMEM((BLK, wk), f32)] if nb > 1 else []
    scratch += [pltpu.VMEM((nh, BLK, BLK), f32)] * 2 + [pltpu.VMEM((nh, BLK, 2 * BLK if nb > 1 else BLK), bf16)] * 2
    if diag:
        scratch += [pltpu.VMEM((nh, BLK, BLK), f32)] * 2
    return pl.pallas_call(
        body, name=name, grid=(n, nb), in_specs=in_specs, out_specs=out_specs, out_shape=out_shape,
        scratch_shapes=scratch, compiler_params=_cp(("arbitrary", "arbitrary")),
    )(*args)


def _split2(x):
    hi = x.astype(bf16)
    return hi, (x - hi.astype(f32)).astype(bf16)


def _heads_to_lanes(xc, e):
    return sum(_dot(t, e) for t in _split2(xc))


def _lanes_to_heads(x, g):
    return sum(_dot(t, g) for t in _split2(x))


HEAD_EXPAND = (np.arange(LANES)[:, None] == np.arange(BW)[None, :] // HD).astype(np.float32)
HEAD_SUM = HEAD_EXPAND.T.copy()


def _branch_weights(l1_ref, l4_ref, l16_ref, scr):
    l4v = _perm_load(l4_ref, scr, 4)
    l16v = _perm_load(l16_ref, scr, 16)
    l1v = l1_ref[...]
    m = jnp.maximum(jnp.maximum(l1v, l4v), l16v)
    e1, e4, e16 = jnp.exp(l1v - m), jnp.exp(l4v - m), jnp.exp(l16v - m)
    z = e1 + e4 + e16
    return e1 / z, e4 / z, e16 / z


def _mix_out(oa, o1, l1, o4, l4, o16, l16, g_mix_a, g_mix_b, w_out, x, mod, g_post):
    def body(oa_ref, o1_ref, l1_ref, o4_ref, l4_ref, o16_ref, l16_ref, ga_ref, gb_ref, w_ref, x_ref, mod_ref, gp_ref, e_ref,
             x1_ref, y_ref, mixed_ref, ob_ref, scr):
        w1, w4, w16 = _branch_weights(l1_ref, l4_ref, l16_ref, scr)
        e = e_ref[...]
        x1w, x4w = _heads_to_lanes(w1, e), _heads_to_lanes(w4, e)
        ob = (x1w * o1_ref[...].astype(f32) + x4w * _perm_load(o4_ref, scr, 4)
              + (1.0 - x1w - x4w) * _perm_load(o16_ref, scr, 16))
        ob_ref[...] = ob
        oan, _ = _rms(oa_ref[...])
        obn, _ = _rms(ob)
        mixed = jnp.concatenate([oan * ga_ref[...], obn * gb_ref[...]], axis=1).astype(bf16)
        mixed_ref[...] = mixed
        y = _dot(mixed, w_ref[...])
        y_ref[...] = y
        yn, _ = _rms(y)
        x1_ref[...] = x_ref[...] + mod_ref[2:3, :] * (yn * gp_ref[...])

    nat = lambda w, dt: jax.ShapeDtypeStruct((BL, SEQ, w), dt)
    return pl.pallas_call(
        body, name="mix_out", grid=(BL, NJ),
        in_specs=[_tok(AQ), _tok(BW), _tok(LANES), _perm_spec(4, BW), _perm_spec(4, LANES), _perm_spec(16, BW),
                  _perm_spec(16, LANES), _full((1, AQ)), _full((1, BW)), _full((D, D)), _tok(D), MOD_SPEC, _full((1, D)),
                  _full((LANES, BW))],
        out_specs=[_tok(D), _tok(D), _tok(D), _tok(BW)],
        out_shape=[nat(D, f32), nat(D, f32), nat(D, bf16), nat(BW, f32)],
        scratch_shapes=[pltpu.VMEM((BW // LANES, TM, LANES), f32)],
        compiler_params=_cp(("arbitrary", "arbitrary")),
    )(oa, o1, l1, o4, l4, o16, l16, g_mix_a, g_mix_b, w_out, x, mod, g_post, jnp.asarray(HEAD_EXPAND, bf16))


def _mlp_up(x1, mod, g_pre, w_up):
    def body(x_ref, mod_ref, g_ref, w_ref, h_ref, u_ref, a_ref):
        xn, _ = _rms(x_ref[...])
        h = (xn * g_ref[...]) * (1.0 + mod_ref[4:5, :]) + mod_ref[3:4, :]
        hb = h.astype(bf16)
        h_ref[...] = hb
        for s in range(NCHIP):
            u = _dot(hb, w_ref[s])
            u_ref[:, D * s:D * (s + 1)] = u.astype(bf16)
            a_ref[:, D * s:D * (s + 1)] = jnp.square(jnp.maximum(u, 0.0)).astype(bf16)

    nat = lambda w: jax.ShapeDtypeStruct((BL, SEQ, w), bf16)
    return pl.pallas_call(
        body, name="mlp_up", grid=(BL, NJ),
        in_specs=[_tok(D), MOD_SPEC, _full((1, D)), _full((NCHIP, D, D))],
        out_specs=[_tok(D), _tok(DFF), _tok(DFF)], out_shape=[nat(D), nat(DFF), nat(DFF)],
        compiler_params=_cp(("arbitrary", "arbitrary")),
    )(x1, mod, g_pre, w_up)


def _mlp_down(a, w_down, x1, target, mod, g_post):
    def body(a_ref, w_ref, x_ref, t_ref, mod_ref, g_ref, gx_ref, dy_ref, accb_ref, accg_ref):
        _acc_init(accb_ref, accg_ref)
        y2 = _dot(a_ref[...], w_ref[...])
        yn, r = _rms(y2)
        g = g_ref[...]
        gt = mod_ref[5:6, :]
        n2 = yn * g
        err = x_ref[...] + gt * n2 - t_ref[...]
        gout = err * (1.0 / D)
        gx_ref[...] = gout
        dn2 = gout * gt
        dy_ref[...] = _rms_bwd(dn2 * g, yn, r).astype(bf16)
        accb_ref[0:1, :] += _colsum(gout * n2)
        accg_ref[0:1, :] += _colsum(dn2 * yn)
        accg_ref[1:2, :] += jnp.broadcast_to(jnp.sum(err * err, keepdims=True), (1, D))

    return pl.pallas_call(
        body, name="mlp_down", grid=(BL, NJ),
        in_specs=[_tok(DFF), _full((DFF, D)), _tok(D), _tok(D), MOD_SPEC, _full((1, D))],
        out_specs=[_tok(D), _tok(D), ACCB_SPEC, ACCG_SPEC],
        out_shape=[jax.ShapeDtypeStruct((BL, SEQ, D), f32), jax.ShapeDtypeStruct((BL, SEQ, D), bf16)] + ACC_SHAPES,
        compiler_params=_cp(("arbitrary", "arbitrary")),
    )(a, w_down, x1, target, mod, g_post)


def _mlp_bwd(dy2, u, w_down, w_up, x1, gx, mod, g_pre):
    def body(dy_ref, u_ref, wd_hbm, wu_hbm, x_ref, gx_ref, mod_ref, g_ref, du_ref, gx1_ref, accb_ref, accg_ref, wd, wu, sem):
        _acc_init(accb_ref, accg_ref)

        @pl.when((pl.program_id(0) == 0) & (pl.program_id(1) == 0))
        def _():
            c1 = pltpu.make_async_copy(wd_hbm, wd, sem.at[0])
            c2 = pltpu.make_async_copy(wu_hbm, wu, sem.at[1])
            c1.start()
            c2.start()
            c1.wait()
            c2.wait()

        dy = dy_ref[...]
        dh = jnp.zeros((TM, D), f32)
        for s in range(NCHIP):
            sl = slice(D * s, D * (s + 1))
            da = _dot_nt(dy, wd[sl, :])
            du = (da * (2.0 * jnp.maximum(u_ref[:, sl].astype(f32), 0.0))).astype(bf16)
            du_ref[:, sl] = du
            dh = dh + _dot_nt(du, wu[s])
        xn, r = _rms(x_ref[...])
        g = g_ref[...]
        n = xn * g
        dn = dh * (1.0 + mod_ref[4:5, :])
        gx1_ref[...] = gx_ref[...] + _rms_bwd(dn * g, xn, r)
        accb_ref[0:1, :] += _colsum(dh * n)
        accb_ref[1:2, :] += _colsum(dh)
        accg_ref[0:1, :] += _colsum(dn * xn)

    anyspec = pl.BlockSpec(memory_space=pl.ANY)
    return pl.pallas_call(
        body, name="mlp_bwd", grid=(BL, NJ),
        in_specs=[_tok(D), _tok(DFF), anyspec, anyspec, _tok(D), _tok(D), MOD_SPEC, _full((1, D))],
        out_specs=[_tok(DFF), _tok(D), ACCB_SPEC, ACCG_SPEC],
        out_shape=[jax.ShapeDtypeStruct((BL, SEQ, DFF), bf16), jax.ShapeDtypeStruct((BL, SEQ, D), f32)] + ACC_SHAPES,
        scratch_shapes=[pltpu.VMEM((DFF, D), bf16), pltpu.VMEM((NCHIP, D, D), bf16), pltpu.SemaphoreType.DMA((2,))],
        compiler_params=_cp(("arbitrary", "arbitrary")),
    )(dy2, u, w_down, w_up, x1, gx, mod, g_pre)


def _matmul_tn(a, b, *, tn, col_blocked, name, out_dtype=f32):
    t, m = a.shape
    n = b.shape[1]
    tmm = min(m, 1024)
    tk = 2048 if tn <= 1024 else 1024
    nk = t // tk

    def body(a_ref, b_ref, o_ref, acc):
        k = pl.program_id(2)

        @pl.when(k == 0)
        def _():
            acc[...] = jnp.zeros_like(acc)

        acc[...] += _dot_tn(a_ref[...], b_ref[...])

        @pl.when(k == nk - 1)
        def _():
            o_ref[...] = acc[...].astype(out_dtype)

    if col_blocked:
        out_spec = pl.BlockSpec((None, tmm, tn), lambda i, j, k: (j, i, 0))
        out_shape = jax.ShapeDtypeStruct((n // tn, m, tn), out_dtype)
    else:
        out_spec = pl.BlockSpec((tmm, tn), lambda i, j, k: (i, j))
        out_shape = jax.ShapeDtypeStruct((m, n), out_dtype)
    return pl.pallas_call(
        body, name=name, grid=(m // tmm, n // tn, nk),
        in_specs=[pl.BlockSpec((tk, tmm), lambda i, j, k: (k, i)), pl.BlockSpec((tk, tn), lambda i, j, k: (k, j))],
        out_specs=out_spec, out_shape=out_shape, scratch_shapes=[pltpu.VMEM((tmm, tn), f32)],
        compiler_params=_cp(("arbitrary", "arbitrary", "arbitrary")),
    )(a, b)


def _grad_w_in(h, dproj):
    t = h.shape[0]
    tk = 1024
    nk = t // tk
    sw = INW // NCHIP

    def body(a_ref, b_ref, o_ref, acc):
        k = pl.program_id(0)

        @pl.when(k == 0)
        def _():
            acc[...] = jnp.zeros_like(acc)

        acc[...] += _dot_tn(a_ref[...], b_ref[...])

        @pl.when(k == nk - 1)
        def _():
            for s in range(NCHIP):
                o_ref[s] = acc[:, sw * s:sw * (s + 1)]

    return pl.pallas_call(
        body, name="grad_w_in", grid=(nk,),
        in_specs=[pl.BlockSpec((tk, D), lambda k: (k, 0)), pl.BlockSpec((tk, INW), lambda k: (k, 0))],
        out_specs=pl.BlockSpec((NCHIP, D, sw), lambda k: (0, 0, 0)), out_shape=jax.ShapeDtypeStruct((NCHIP, D, sw), f32),
        scratch_shapes=[pltpu.VMEM((D, INW), f32)], compiler_params=_cp(("arbitrary",)),
    )(h, dproj)


def _attn_out_bwd(gx1, y, mod, g_post, w_out, oa, ob, g_mix_a, g_mix_b, l1, l4, l16):
    def body(gx_ref, y_ref, mod_ref, gp_ref, w_ref, oa_ref, ob_ref, ga_ref, gb_ref, l1_ref, l4_ref, l16_ref, e_ref, g_ref,
             dy_ref, doa_ref, do1_ref, do4_ref, do16_ref, da_ref, d1_ref, d4_ref, d16_ref, accb_ref, accg_ref, scr):
        _acc_init(accb_ref, accg_ref)
        w1, w4, w16 = _branch_weights(l1_ref, l4_ref, l16_ref, scr)
        e, hs = e_ref[...], g_ref[...]
        gx1v = gx_ref[...]
        yn, ry = _rms(y_ref[...])
        gp = gp_ref[...]
        gt = mod_ref[2:3, :]
        dn1 = gx1v * gt
        dy = _rms_bwd(dn1 * gp, yn, ry).astype(bf16)
        dy_ref[...] = dy
        dmixed = _dot_nt(dy, w_ref[...])
        dma, dmb = dmixed[:, :AQ], dmixed[:, AQ:]
        oa, ob = oa_ref[...], ob_ref[...]
        oan, ra = _rms(oa)
        obn, rb = _rms(ob)
        doa = _rms_bwd(dma * ga_ref[...], oan, ra)
        doa_ref[...] = doa.astype(bf16)
        da_ref[...] = _lanes_to_heads(doa * oa, hs)
        dob = _rms_bwd(dmb * gb_ref[...], obn, rb)
        dd = _lanes_to_heads(dob * ob, hs)
        x1w, x4w = _heads_to_lanes(w1, e), _heads_to_lanes(w4, e)
        do1_ref[...] = (x1w * dob).astype(bf16)
        d1_ref[...] = w1 * dd
        _perm_store(x4w * dob, scr, do4_ref, 4)
        _perm_store(w4 * dd, scr, d4_ref, 4)
        _perm_store((1.0 - x1w - x4w) * dob, scr, do16_ref, 16)
        _perm_store(w16 * dd, scr, d16_ref, 16)
        accb_ref[0:1, :] += _colsum(gx1v * (yn * gp))
        accg_ref[0:1, :] += _colsum(dn1 * yn)
        accg_ref[1:2, :] += jnp.concatenate([_colsum(dma * oan), _colsum(dmb * obn)], axis=1)

    nat = lambda w, dt: jax.ShapeDtypeStruct((BL, SEQ, w), dt)
    return pl.pallas_call(
        body, name="attn_out_bwd", grid=(BL, NJ),
        in_specs=[_tok(D), _tok(D), MOD_SPEC, _full((1, D)), _full((D, D)), _tok(AQ), _tok(BW), _full((1, AQ)), _full((1, BW)),
                  _tok(LANES), _perm_spec(4, LANES), _perm_spec(16, LANES), _full((LANES, BW)), _full((BW, LANES))],
        out_specs=[_tok(D), _tok(AQ), _tok(BW), _perm_spec(4, BW), _perm_spec(16, BW),
                   _tok(LANES), _tok(LANES), _perm_spec(4, LANES), _perm_spec(16, LANES), ACCB_SPEC, ACCG_SPEC],
        out_shape=[nat(D, bf16), nat(AQ, bf16), nat(BW, bf16), jax.ShapeDtypeStruct((BL, 4, SEQ // 4, BW), bf16),
                   jax.ShapeDtypeStruct((BL, 16, SEQ // 16, BW), bf16), nat(LANES, f32), nat(LANES, f32),
                   jax.ShapeDtypeStruct((BL, 4, SEQ // 4, LANES), f32), jax.ShapeDtypeStruct((BL, 16, SEQ // 16, LANES), f32)]
                  + ACC_SHAPES,
        scratch_shapes=[pltpu.VMEM((BW // LANES, TM, LANES), f32)],
        compiler_params=_cp(("arbitrary", "arbitrary")),
    )(gx1, y, mod, g_post, w_out, oa, ob, g_mix_a, g_mix_b, l1, l4, l16, jnp.asarray(HEAD_EXPAND, bf16),
      jnp.asarray(HEAD_SUM, bf16))


def _attn_in_bwd(dqa, dka, dva, d1, d4, d16, tc, ts1, ts2, w_in, x, gx1, mod, g_pre):
    def body(dqa_ref, dka_ref, dva_ref, dq1_ref, dk1_ref, dv1_ref, dq4_ref, dk4_ref, dv4_ref, dq16_ref, dk16_ref, dv16_ref,
             c_ref, s1_ref, s2_ref, w_ref, x_ref, gx_ref, mod_ref, g_ref, dproj_ref, dx_ref, accb_ref, accg_ref, scr):
        _acc_init(accb_ref, accg_ref)
        c, s1, s2 = c_ref[...], s1_ref[...], s2_ref[...]
        tot = lambda r1, r4, r16: r1[...] + _perm_load(r4, scr, 4) + _perm_load(r16, scr, 16)
        dqb = tot(dq1_ref, dq4_ref, dq16_ref)
        dkb = tot(dk1_ref, dk4_ref, dk16_ref)
        dvb = tot(dv1_ref, dv4_ref, dv16_ref)
        dproj = jnp.concatenate([
            _rope_t(dqa_ref[...], c, s1, s2) * 0.125, _rope_t(dka_ref[...], c, s1, s2), dva_ref[...],
            _rope_t(dqb, c, s1, s2) * 0.125, _rope_t(dkb, c, s1, s2), dvb], axis=1).astype(bf16)
        dproj_ref[...] = dproj
        dh = _dot_nt(dproj, w_ref[...])
        xn, r = _rms(x_ref[...])
        g = g_ref[...]
        dn = dh * (1.0 + mod_ref[1:2, :])
        dx_ref[...] = gx_ref[...] + _rms_bwd(dn * g, xn, r)
        accb_ref[0:1, :] += _colsum(dh * (xn * g))
        accb_ref[1:2, :] += _colsum(dh)
        accg_ref[0:1, :] += _colsum(dn * xn)

    return pl.pallas_call(
        body, name="attn_in_bwd", grid=(BL, NJ),
        in_specs=[_tok(AQ), _tok(AKV), _tok(AKV)] + [_tok(BW)] * 3 + [_perm_spec(4, BW)] * 3 + [_perm_spec(16, BW)] * 3
                 + [_tok(LANES)] * 3 + [_full((D, INW)), _tok(D), _tok(D), MOD_SPEC, _full((1, D))],
        out_specs=[_tok(INW), _tok(D), ACCB_SPEC, ACCG_SPEC],
        out_shape=[jax.ShapeDtypeStruct((BL, SEQ, INW), bf16), jax.ShapeDtypeStruct((BL, SEQ, D), f32)] + ACC_SHAPES,
        scratch_shapes=[pltpu.VMEM((BW // LANES, TM, LANES), f32)],
        compiler_params=_cp(("arbitrary", "arbitrary")),
    )(dqa, dka, dva, *d1, *d4, *d16, tc, ts1, ts2, w_in, x, gx1, mod, g_pre)


def _inv_lane():
    inv = np.float32(THETA) ** (-np.arange(0, ROT, 2, dtype=np.float32) / np.float32(ROT))
    lane = np.arange(LANES) % HD
    return jnp.asarray(np.where(lane < ROT, inv[lane % (ROT // 2)], 0.0).astype(np.float32)[None, :])


def _local_step(x, positions, mod, target, inv_lane, first_weight, later_weights, grad_ready, g_attn_pre,
                g_attn_post, sink_a, g_mix_a, g_mix_b, g_mlp_pre, g_mlp_post):
    tabs = _rope_tables(positions.reshape(BL * SEQ, 1), inv_lane)
    w_in = first_weight(tuple(tabs))
    tc, ts1, ts2 = [t.reshape(BL, SEQ, LANES) for t in tabs]

    (h, qa, ka, va, q1, k1, v1, q4, k4, v4, q16, k16, v16, w_in) = _attn_in(x, mod, g_attn_pre, w_in, tc, ts1, ts2)
    seqs = lambda t: t.reshape(t.shape[0] * t.shape[1], t.shape[2], t.shape[3])
    q4, k4, v4, q16, k16, v16 = [seqs(t) for t in (q4, k4, v4, q16, k16, v16)]
    oa, la = _attn_fwd(qa, ka, va, sink_a, max_dist=BLK - 1, o_dtype=f32, name="attn_a_fwd")
    o1, l1 = _attn_fwd(q1, k1, v1, None, max_dist=BLK, o_dtype=bf16, name="attn_b1_fwd")
    o4, l4 = _attn_fwd(q4, k4, v4, None, max_dist=BLK, o_dtype=bf16, name="attn_b4_fwd")
    o16, l16 = _attn_fwd(q16, k16, v16, None, max_dist=BLK, o_dtype=bf16, name="attn_b16_fwd")
    b4 = lambda t: t.reshape(BL, 4, SEQ // 4, t.shape[-1])
    b16 = lambda t: t.reshape(BL, 16, SEQ // 16, t.shape[-1])
    w_out, mlp_weights, mod = later_weights((oa, o1, o4, o16), mod)
    x1, y, mixed, ob = _mix_out(oa, o1, l1, b4(o4), b4(l4), b16(o16), b16(l16), g_mix_a, g_mix_b, w_out, x, mod, g_attn_post)
    w_up, w_down = mlp_weights((x1,))
    h2, u, a = _mlp_up(x1, mod, g_mlp_pre, w_up)
    gx, dy2, accb_d, accg_d = _mlp_down(a, w_down, x1, target, mod, g_mlp_post)

    flat = lambda t: t.reshape(BL * SEQ, t.shape[-1])
    mod = grad_ready("w_down", _matmul_tn(flat(a), flat(dy2), tn=D, col_blocked=False, name="grad_w_down", out_dtype=bf16), mod)
    du, gx1, accb_m, accg_m = _mlp_bwd(dy2, u, w_down, w_up, x1, gx, mod, g_mlp_pre)
    mod = grad_ready("w_up", _matmul_tn(flat(h2), flat(du), tn=D, col_blocked=True, name="grad_w_up", out_dtype=bf16), mod)

    dy, doa, do1, do4, do16, da, dl1, dl4, dl16, accb_o, accg_o = _attn_out_bwd(
        gx1, y, mod, g_attn_post, w_out, oa, ob, g_mix_a, g_mix_b, l1, b4(l4), b16(l16))
    gw_out = _matmul_tn(flat(mixed), flat(dy), tn=D, col_blocked=False, name="grad_w_out")
    dqa, dka, dva, dsink = _attn_bwd(qa, ka, va, doa, da, la, sink_a, max_dist=BLK - 1, name="attn_a_bwd")
    d1 = _attn_bwd(q1, k1, v1, do1, dl1, l1, None, max_dist=BLK, name="attn_b1_bwd")
    d4 = _attn_bwd(q4, k4, v4, seqs(do4), seqs(dl4), l4, None, max_dist=BLK, name="attn_b4_bwd")
    d16 = _attn_bwd(q16, k16, v16, seqs(do16), seqs(dl16), l16, None, max_dist=BLK, name="attn_b16_bwd")
    dproj, grad_x, accb_i, accg_i = _attn_in_bwd(dqa, dka, dva, d1, [b4(t) for t in d4], [b16(t) for t in d16],
                                                 tc, ts1, ts2, w_in, x, gx1, mod, g_attn_pre)
    gw_in = _grad_w_in(flat(h), flat(dproj))
    dsink = grad_ready("w_in_w_out", (gw_in, gw_out), dsink)

    return grad_x, (accb_i, accb_o, accb_m, accb_d, accg_i, accg_o, accg_m, accg_d, dsink)


ADAW = NMOD * D // NCHIP


def _pos():
    return lax.axis_index("x"), lax.axis_index("y"), lax.axis_index("c")


def _flip(v, bit):
    return 1 - v if bit else v


def _all_peers(x, y, c):
    return [(_flip(x, k >> 2 & 1), _flip(y, k >> 1 & 1), _flip(c, k & 1)) for k in range(1, NDEV)]


def _other_chips(x, y):
    return [(1 - x, y), (x, 1 - y), (1 - x, 1 - y)]


def _rcopy(src, dst, send, recv, k, dev, k_recv=None):
    return pltpu.make_async_remote_copy(src_ref=src, dst_ref=dst, send_sem=send.at[k],
                                        recv_sem=recv.at[k if k_recv is None else k_recv],
                                        device_id=dev, device_id_type=MESH)


def _gather_small(src, buf, send, recv):
    x, y, c = _pos()
    me = 4 * x + 2 * y + c
    peers = _all_peers(x, y, c)
    sends = [_rcopy(src, buf.at[me], send, recv, k, p) for k, p in enumerate(peers)]
    for cp in sends:
        cp.start()
    for k, (px, py, pc) in enumerate(peers):
        _rcopy(src, buf.at[4 * px + 2 * py + pc], send, recv, k, (px, py, pc)).wait_recv()
    for cp in sends:
        cp.wait_send()
    return me


def _ada_fwd(c_in, w_ada, b_cols):
    def body(c_ref, w_ref, b_ref, mod_ref, cond_ref, cbuf, mbuf, s1, r1, s2, r2):
        x, y, c = _pos()
        chip = 2 * x + y
        me = _gather_small(c_ref, cbuf, s1, r1)
        cbuf[me] = c_ref[...]
        for i in range(NDEV):
            cond_ref[BL * i:BL * (i + 1), :] = cbuf[i]
        call = cond_ref[...]
        cond = call / (1.0 + jnp.exp(-call))
        cond_ref[...] = cond
        mbuf[chip] = _dot(cond.astype(bf16), w_ref[...].astype(bf16)) + b_ref[...]
        chips = _other_chips(x, y)
        sends = [_rcopy(mbuf.at[chip], mbuf.at[chip], s2, r2, j, (px, py, c)) for j, (px, py) in enumerate(chips)]
        for cp in sends:
            cp.start()
        for j, (px, py) in enumerate(chips):
            _rcopy(mbuf.at[chip], mbuf.at[2 * px + py], s2, r2, j, (px, py, c)).wait_recv()
        for cp in sends:
            cp.wait_send()
        row = lax.broadcasted_iota(jnp.int32, (BL * NDEV, ADAW), 0)
        for s in range(NCHIP):
            slab = mbuf[s]
            for j in range(BL):
                mod_ref[j:j + 1, ADAW * s:ADAW * (s + 1)] = jnp.sum(jnp.where(row == BL * me + j, slab, 0.0), axis=0, keepdims=True)

    vm = pl.BlockSpec(memory_space=pltpu.VMEM)
    return pl.pallas_call(
        body, name="ada_fwd", in_specs=[vm, vm, vm], out_specs=[vm, vm],
        out_shape=[jax.ShapeDtypeStruct((BL, NMOD * D), f32), jax.ShapeDtypeStruct((BL * NDEV, D), f32)],
        scratch_shapes=[pltpu.VMEM((NDEV, BL, D), f32), pltpu.VMEM((NCHIP, BL * NDEV, ADAW), f32),
                        pltpu.SemaphoreType.DMA((NDEV - 1,)), pltpu.SemaphoreType.DMA((NDEV - 1,)),
                        pltpu.SemaphoreType.DMA((NCHIP - 1,)), pltpu.SemaphoreType.DMA((NCHIP - 1,))],
        compiler_params=pltpu.CompilerParams(vmem_limit_bytes=VMEM_LIMIT),
    )(c_in, w_ada, b_cols)


def _small_allreduce(accs, cond_all):
    def body(bi, bo, bm, bd, gi, go, gm, gd, dsink, cond_ref, gw_ref, gb_ref, small_ref, pay, pbuf, dall, s1, r1):
        x, y, c = _pos()
        chip = 2 * x + y
        pay[...] = jnp.zeros_like(pay)
        for b in range(BL):
            for k, (ref, r) in enumerate(((bi, 1), (bi, 0), (bo, 0), (bm, 1), (bm, 0), (bd, 0))):
                pay[b:b + 1, D * k:D * (k + 1)] = ref[b, r:r + 1, :]
        for off, ref, r in ((OFF_G_ATTN_PRE, gi, 0), (OFF_G_ATTN_POST, go, 0), (OFF_G_MIX_A, go, 1), (OFF_G_MLP_PRE, gm, 0),
                            (OFF_G_MLP_POST, gd, 0)):
            pay[BL:BL + 1, off:off + D] = ref[r:r + 1, :]
        eye = lax.broadcasted_iota(jnp.int32, (8, LANES), 0) == lax.broadcasted_iota(jnp.int32, (8, LANES), 1)
        pay[BL:BL + 1, OFF_SINK:OFF_SINK + LANES] = jnp.sum(jnp.where(eye, dsink[...], 0.0), axis=0, keepdims=True)
        pay[BL:BL + 1, OFF_LOSS:OFF_LOSS + LANES] = gd[1:2, 0:LANES]
        me = _gather_small(pay, pbuf, s1, r1)
        pbuf[me] = pay[...]
        small = pbuf[0, BL:BL + 1, :]
        for i in range(1, NDEV):
            small = small + pbuf[i, BL:BL + 1, :]
        small_ref[...] = small
        for i in range(NDEV):
            dall[BL * i:BL * (i + 1), :] = pbuf[i, 0:BL, :]
        gb_ref[...] = jnp.sum(dall[...], axis=0, keepdims=True)
        cols = jnp.zeros((BL * NDEV, ADAW), f32)
        for s in range(NCHIP):
            cols = cols + jnp.where(chip == s, dall[:, ADAW * s:ADAW * (s + 1)], 0.0)
        gw_ref[...] = lax.dot_general(cond_ref[...], cols, (((0,), (0,)), ((), ())), preferred_element_type=f32,
                                      precision=lax.Precision.HIGHEST)

    vm = pl.BlockSpec(memory_space=pltpu.VMEM)
    return pl.pallas_call(
        body, name="small_allreduce", in_specs=[vm] * 10, out_specs=[vm] * 3,
        out_shape=[jax.ShapeDtypeStruct((D, ADAW), f32), jax.ShapeDtypeStruct((1, PAYW), f32), jax.ShapeDtypeStruct((1, PAYW), f32)],
        scratch_shapes=[pltpu.VMEM((4, PAYW), f32), pltpu.VMEM((NDEV, 4, PAYW), f32), pltpu.VMEM((BL * NDEV, PAYW), f32),
                        pltpu.SemaphoreType.DMA((NDEV - 1,)), pltpu.SemaphoreType.DMA((NDEV - 1,))],
        compiler_params=pltpu.CompilerParams(vmem_limit_bytes=VMEM_LIMIT),
    )(*accs, cond_all)


def _half(ref, c):
    r2 = ref.shape[0] // 2
    return ref.at[pl.ds(c * r2 if isinstance(c, int) else pl.multiple_of(c * r2, 16), r2), :]


HBM_SPEC = pl.BlockSpec(memory_space=pltpu.HBM)
SEM_SPEC = pl.BlockSpec(memory_space=pltpu.SEMAPHORE)
EFFECT = pltpu.SideEffectType.DATAFLOW_SIDE_EFFECTING
NLINK = NCHIP - 1


def _in_hbm(a):
    return pltpu.with_memory_space_constraint(a, pltpu.HBM)


NSEM = 8


def _split_start(name, srcs, land_shapes, builds, carry, after=(), lands=None):
    n = len(srcs)
    na, nc = len(after), len(carry)

    def body(*refs):
        src, land = refs[:n], refs[n:2 * n]
        kept = refs[2 * n + na:2 * n + na + nc]
        outs = refs[2 * n + na + nc:]
        send, recv, passed = outs[:n], outs[n:2 * n], outs[4 * n:]
        for t in range(n):
            for out_cp, _ in builds[t](src[t], land[t], send[t], recv[t]):
                out_cp.start()
        for a, b in zip(kept, passed):
            b[...] = a[...]

    if lands is None:
        lands = [lax.empty(s.shape, s.dtype) for s in land_shapes]
    lands = [_in_hbm(a) for a in lands]
    sems = [pltpu.SemaphoreType.DMA((NSEM,))] * (2 * n)
    thru = [pltpu.HBM(a.shape, a.dtype) for a in list(srcs) + lands]
    vm = pl.BlockSpec(memory_space=pltpu.VMEM)
    res = pl.pallas_call(
        body, name=name, out_shape=sems + thru + [jax.ShapeDtypeStruct(a.shape, a.dtype) for a in carry],
        in_specs=[HBM_SPEC] * (2 * n) + [pl.BlockSpec(memory_space=pl.ANY)] * na + [vm] * nc,
        out_specs=[SEM_SPEC] * (2 * n) + [HBM_SPEC] * (2 * n) + [vm] * nc,
        input_output_aliases={i: 2 * n + i for i in range(2 * n)},
        compiler_params=pltpu.CompilerParams(has_side_effects=EFFECT),
    )(*[_in_hbm(a) for a in srcs], *lands, *after, *carry)
    flight = [(res[2 * n + t], res[3 * n + t], res[t], res[n + t]) for t in range(n)]
    return flight, list(res[4 * n:])


def _split_wait(name, flight, builds, after):
    m = len(flight)
    na = len(after)

    def body(*refs):
        src, land, send, recv = refs[:m], refs[m:2 * m], refs[2 * m:3 * m], refs[3 * m:4 * m]
        for t in range(m):
            for out_cp, in_cp in builds[t](src[t], land[t], send[t], recv[t]):
                out_cp.wait_send()
                in_cp.wait_recv()

    ops = [f[0] for f in flight] + [f[1] for f in flight] + [f[2] for f in flight] + [f[3] for f in flight]
    res = pl.pallas_call(
        body, name=name, out_shape=[pltpu.HBM(a.shape, a.dtype) for a in ops[:2 * m]],
        in_specs=[HBM_SPEC] * (2 * m) + [SEM_SPEC] * (2 * m) + [pl.BlockSpec(memory_space=pl.ANY)] * na,
        out_specs=[HBM_SPEC] * (2 * m), input_output_aliases={i: i for i in range(2 * m)},
        compiler_params=pltpu.CompilerParams(has_side_effects=EFFECT),
    )(*ops, *after)
    return res[:m], res[m:2 * m]


def _weight_copies(src, land, send, recv):
    x, y, c = _pos()
    chip = 2 * x + y
    return [(_rcopy(_half(src, c), _half(land.at[chip], c), send, recv, j, (px, py, c)),
             _rcopy(_half(src, c), _half(land.at[2 * px + py], c), send, recv, j, (px, py, c)))
            for j, (px, py) in enumerate(_other_chips(x, y))]


def _grad_copies(src, land, send, recv):
    x, y, c = _pos()
    return [(_rcopy(src.at[2 * px + py], land.at[j], send, recv, j, (px, py, c)),
             _rcopy(src.at[2 * px + py], land.at[j], send, recv, j, (px, py, c)))
            for j, (px, py) in enumerate(_other_chips(x, y))]


NDIRECT = NDEV - 1


def _direct_grad_copies(src, land, send, recv):
    x, y, c = _pos()
    out, arrive = [], []
    for j, (px, py) in enumerate(_other_chips(x, y)):
        for hc in range(2):
            out.append(_rcopy(_half(src.at[2 * px + py], hc), land.at[2 * j + c], send, recv, 2 * j + hc, (px, py, hc),
                              k_recv=2 * j + c))
            arrive.append(_rcopy(_half(src.at[2 * px + py], hc), land.at[2 * j + hc], send, recv, 2 * j + hc, (px, py, hc)))
    own = _rcopy(_half(src.at[2 * x + y], 1 - c), land.at[NDIRECT - 1], send, recv, NDIRECT - 1, (x, y, 1 - c))
    return list(zip(out, arrive)) + [(own, own)]


def _pair_grad_copies(src, land, send, recv):
    x, y, c = _pos()
    r2 = src.shape[1] // 2
    cp = _rcopy(src.at[:, pl.ds(pl.multiple_of((1 - c) * r2, 8), r2), :], land, send, recv, 0, (x, y, 1 - c))
    return [(cp, cp)]


def _pair_weight_copies(src, land, send, recv):
    x, y, c = _pos()
    sib = (x, y, 1 - c)
    cps = []
    for j, (px, py) in enumerate(_other_chips(x, y)):
        mine, theirs = _half(land.at[2 * px + py], c), _half(land.at[2 * px + py], 1 - c)
        cps.append((_rcopy(mine, mine, send, recv, j, sib), _rcopy(theirs, theirs, send, recv, j, sib)))
    own = _rcopy(src, land.at[2 * x + y], send, recv, NLINK, sib)
    return cps + [(own, own)]


RS_ROWS = 128


def _pair_add(g, landed, c_arr, name):
    _, r2, cw = landed.shape
    nr = r2 // RS_ROWS

    def body(c_ref, g_ref, p_ref, o_ref):
        o_ref[...] = (g_ref[...] + p_ref[...]).astype(bf16)

    gs = pltpu.PrefetchScalarGridSpec(
        num_scalar_prefetch=1, grid=(NCHIP, nr),
        in_specs=[pl.BlockSpec((None, RS_ROWS, cw), lambda s, j, c: (s, c[0] * nr + j, 0)),
                  pl.BlockSpec((None, RS_ROWS, cw), lambda s, j, c: (s, j, 0))],
        out_specs=pl.BlockSpec((None, RS_ROWS, cw), lambda s, j, c: (s, j, 0)))
    return pl.pallas_call(body, name=name, grid_spec=gs, out_shape=jax.ShapeDtypeStruct((NCHIP, r2, cw), bf16),
                          compiler_params=_cp(("arbitrary", "arbitrary")))(c_arr, g, landed)


def _chip_add(own, landed, pos_arr, name):
    nl, r2, cw = landed.shape
    nr = r2 // RS_ROWS
    whole = own.shape[1] == 2 * r2

    def body(s_ref, h_ref, q_ref, o_ref):
        acc = h_ref[...].astype(f32)
        for j in range(nl):
            acc = acc + q_ref[j].astype(f32)
        o_ref[...] = acc

    gs = pltpu.PrefetchScalarGridSpec(
        num_scalar_prefetch=1, grid=(nr,),
        in_specs=[pl.BlockSpec((None, RS_ROWS, cw), lambda j, s: (s[0], (s[1] * nr if whole else 0) + j, 0)),
                  pl.BlockSpec((nl, RS_ROWS, cw), lambda j, s: (0, j, 0))],
        out_specs=pl.BlockSpec((RS_ROWS, cw), lambda j, s: (s[1] * nr + j, 0)))
    return pl.pallas_call(body, name=name, grid_spec=gs, out_shape=jax.ShapeDtypeStruct((2 * r2, cw), f32),
                          compiler_params=_cp(("arbitrary",)))(pos_arr, own, landed)


def _pair_gather_copies(src, land, send, recv):
    x, y, c = _pos()
    sib = (x, y, 1 - c)
    return [(_rcopy(_half(land, c), _half(land, c), send, recv, 0, sib),
             _rcopy(_half(land, 1 - c), _half(land, 1 - c), send, recv, 0, sib))]


def _adamw_math(w, g, m, v):
    m = B1 * m + (1.0 - B1) * g
    v = B2 * v + (1.0 - B2) * jnp.square(g)
    m_hat = m / (1.0 - B1 ** STEP)
    v_hat = v / (1.0 - B2 ** STEP)
    return -LR * (m_hat / (jnp.sqrt(v_hat) + AEPS) + WD * w), m, v


ADAM_ROWS = 256


def _adamw(w, g, m, v, name):
    r, cw = w.shape

    def body(w_ref, g_ref, m_ref, v_ref, go_ref, d_ref, mo_ref, vo_ref):
        g = g_ref[...]
        go_ref[...] = g
        d_ref[...], mo_ref[...], vo_ref[...] = _adamw_math(w_ref[...], g, m_ref[...], v_ref[...])

    rows = max(k for k in range(8, ADAM_ROWS + 1, 8) if r % k == 0)
    spec = pl.BlockSpec((rows, cw), lambda i: (i, 0))
    return pl.pallas_call(body, name=name, grid=(r // rows,), in_specs=[spec] * 4, out_specs=[spec] * 4,
                          out_shape=[jax.ShapeDtypeStruct((r, cw), f32)] * 4, compiler_params=_cp(("arbitrary",)))(w, g, m, v)


SMALL = (("b_ada", None, PAYW), ("g_attn_pre", OFF_G_ATTN_PRE, D), ("g_attn_post", OFF_G_ATTN_POST, D), ("sink_a", OFF_SINK, 8),
         ("g_mix_a", OFF_G_MIX_A, AQ), ("g_mix_b", OFF_G_MIX_B, BW), ("g_mlp_pre", OFF_G_MLP_PRE, D), ("g_mlp_post", OFF_G_MLP_POST, D))


def _adamw_small(small, gb, params):
    n = len(SMALL)

    def body(*refs):
        small_ref, gb_ref = refs[:2]
        wmv = refs[2:2 + 3 * n]
        loss_ref = refs[2 + 3 * n]
        outs = refs[3 + 3 * n:]
        loss_ref[...] = small_ref[:, OFF_LOSS:OFF_LOSS + 1] * (0.5 / D)
        for i, (_, off, width) in enumerate(SMALL):
            g = gb_ref[...] if off is None else small_ref[:, off:off + width]
            w_ref, m_ref, v_ref = wmv[3 * i:3 * i + 3]
            outs[4 * i][...] = g
            outs[4 * i + 1][...], outs[4 * i + 2][...], outs[4 * i + 3][...] = _adamw_math(w_ref[...], g, m_ref[...], v_ref[...])

    vm = pl.BlockSpec(memory_space=pltpu.VMEM)
    out_shape = [jax.ShapeDtypeStruct((1, 1), f32)]
    for _, _, width in SMALL:
        out_shape += [jax.ShapeDtypeStruct((1, width), f32)] * 4
    flat = [a for wmv in params for a in wmv]
    res = pl.pallas_call(body, name="adamw_small", in_specs=[vm] * (2 + 3 * n), out_specs=[vm] * len(out_shape),
                         out_shape=out_shape)(small, gb, *flat)
    return res[0], {name: res[1 + 4 * i:5 + 4 * i] for i, (name, _, _) in enumerate(SMALL)}


def kernel(x, c, positions, w_ada, b_ada, g_attn_pre, g_attn_post, w_in, sink_a, g_mix_a, g_mix_b, w_out, g_mlp_pre, g_mlp_post, w_up, w_down, loss_target, m_w_ada, m_b_ada, m_g_attn_pre, m_g_attn_post, m_w_in, m_sink_a, m_g_mix_a, m_g_mix_b, m_w_out, m_g_mlp_pre, m_g_mlp_post, m_w_up, m_w_down, v_w_ada, v_b_ada, v_g_attn_pre, v_g_attn_post, v_w_in, v_sink_a, v_g_mix_a, v_g_mix_b, v_w_out, v_g_mlp_pre, v_g_mlp_post, v_w_up, v_w_down):
    given = dict(w_ada=w_ada, b_ada=b_ada, g_attn_pre=g_attn_pre, g_attn_post=g_attn_post, w_in=w_in, sink_a=sink_a, g_mix_a=g_mix_a,
                 g_mix_b=g_mix_b, w_out=w_out, g_mlp_pre=g_mlp_pre, g_mlp_post=g_mlp_post, w_up=w_up, w_down=w_down)
    moms = dict(w_ada=(m_w_ada, v_w_ada), b_ada=(m_b_ada, v_b_ada), g_attn_pre=(m_g_attn_pre, v_g_attn_pre),
                g_attn_post=(m_g_attn_post, v_g_attn_post), w_in=(m_w_in, v_w_in), sink_a=(m_sink_a, v_sink_a),
                g_mix_a=(m_g_mix_a, v_g_mix_a), g_mix_b=(m_g_mix_b, v_g_mix_b), w_out=(m_w_out, v_w_out),
                g_mlp_pre=(m_g_mlp_pre, v_g_mlp_pre), g_mlp_post=(m_g_mlp_post, v_g_mlp_post), w_up=(m_w_up, v_w_up),
                w_down=(m_w_down, v_w_down))
    order = ["w_ada", "b_ada", "g_attn_pre", "g_attn_post", "w_in", "sink_a", "g_mix_a", "g_mix_b", "w_out", "g_mlp_pre",
             "g_mlp_post", "w_up", "w_down"]
    xi, yi, ci = _pos()
    chip = 2 * xi + yi

    c_arr = jnp.reshape(ci, (1,)).astype(jnp.int32)
    pos_arr = jnp.stack([chip, ci]).astype(jnp.int32)
    big = ("w_in", "w_out", "w_up", "w_down")

    b_cols = lax.dynamic_slice(b_ada, (0, chip * ADAW), (1, ADAW))
    mod, cond_all = _ada_fwd(c, w_ada[0], b_cols)
    gathered = [jax.ShapeDtypeStruct((NCHIP,) + given[n].shape[1:], bf16) for n in big]
    flight_in, (mod,) = _split_start("weights_start_first", [w_in[0].astype(bf16)], gathered[:1], [_weight_copies], [mod])
    mod, rest = lax.optimization_barrier((mod, [given[n][0] for n in big[1:]]))
    flight_rest, (mod, inv_lane) = _split_start("weights_start_rest", [w.astype(bf16) for w in rest], gathered[1:],
                                                [_weight_copies] * 3, [mod, _inv_lane()])
    mod = mod.reshape(BL, NMOD, D)

    def first_weight(after):
        srcs, lands = _split_wait("weights_wait_first", flight_in, [_weight_copies], after)
        cross, _ = _split_start("weights_pair_start_first", srcs, None, [_pair_weight_copies], [], lands=lands)
        _, (win_g,) = _split_wait("weights_pair_wait_first", cross, [_pair_weight_copies], ())
        return win_g

    def later_weights(after, carry):
        srcs, lands = _split_wait("weights_wait_rest", flight_rest, [_weight_copies] * 3, after)
        fl, (carry,) = _split_start("weights_pair_start_rest", srcs, None, [_pair_weight_copies] * 3, [carry], lands=lands)
        _, (wout_g,) = _split_wait("weights_pair_wait_out", fl[:1], [_pair_weight_copies], ())

        def mlp_weights(after):
            _, (wup_g, wdn_g) = _split_wait("weights_pair_wait_mlp", fl[1:], [_pair_weight_copies] * 2, after)
            return wup_g, wdn_g.reshape(DFF, D)

        return wout_g.reshape(D, D), mlp_weights, carry

    crossing, pending = {}, {}

    def grad_ready(group, g, carry):
        if group != "w_in_w_out":
            slab = g.reshape(NCHIP, DFF // NCHIP, D) if group == "w_down" else g
            land = jax.ShapeDtypeStruct((NDIRECT, slab.shape[1] // 2, slab.shape[2]), bf16)
            fl, (carry,) = _split_start("grad_start_" + group, [slab], [land], [_direct_grad_copies], [carry])
            pending[group] = ((group,), fl, [_direct_grad_copies])
            return carry
        names = ("w_in", "w_out")
        slabs = [g[0], g[1].reshape(NCHIP, D // NCHIP, D)]
        fl, (carry,) = _split_start("grad_pair_start_" + group, slabs,
                                    [jax.ShapeDtypeStruct((NCHIP, s.shape[1] // 2, s.shape[2]), f32) for s in slabs],
                                    [_pair_grad_copies] * len(names), [carry])
        crossing[group] = (names, fl)
        return carry

    def grad_reduce(group, after, carry):
        names, fl = crossing[group]
        slabs, landed = _split_wait("grad_pair_wait_" + group, fl, [_pair_grad_copies] * len(names), after)
        halves = [_pair_add(s, p, c_arr, "grad_pair_sum_" + n) for s, p, n in zip(slabs, landed, names)]
        fl, (carry,) = _split_start("grad_start_" + group, halves,
                                    [jax.ShapeDtypeStruct((NLINK,) + h.shape[1:], bf16) for h in halves],
                                    [_grad_copies] * len(names), [carry])
        pending[group] = (names, fl, [_grad_copies] * len(names))
        return carry

    grad_x, accs = _local_step(x, positions, mod, loss_target, inv_lane, first_weight, later_weights, grad_ready,
                               g_attn_pre, g_attn_post, sink_a, g_mix_a, g_mix_b, g_mlp_pre, g_mlp_post)

    grads, out = {}, {}

    def update(n):
        tr = (lambda a: a.T) if n == "w_in" else (lambda a: a)
        res = _adamw(tr(given[n][0]), tr(grads[n]), tr(moms[n][0][0]), tr(moms[n][1][0]), "adamw_" + n)
        out[n] = tuple(tr(a)[None] for a in res)
        return res[3]

    def finish(groups, after):
        names = sum((pending[g][0] for g in groups), ())
        fl = sum((pending[g][1] for g in groups), [])
        halves, landed = _split_wait("grad_wait_" + groups[0], fl, sum((pending[g][2] for g in groups), []), after)
        flights = []
        for h, q, n in zip(halves, landed, names):
            full = _chip_add(h, q, pos_arr, "grad_chip_sum_" + n)
            flights.append(_split_start("grad_gather_start_" + n, [jnp.zeros((8, LANES), f32)], None, [_pair_gather_copies],
                                        [], lands=[full])[0])
        last = None
        for n, fl1 in zip(names, flights):
            after = (flights[-1][0][0],) if last is None and fl1 is not flights[-1] else () if last is None else (last,)
            _, (grads[n],) = _split_wait("grad_gather_wait_" + n, fl1, [_pair_gather_copies], after)
            last = update(n)
        return last

    grads["w_ada"], gb, small = _small_allreduce(accs, cond_all)
    small = grad_reduce("w_in_w_out", (small,), small)
    last = finish(("w_down", "w_up"), (small,))
    finish(("w_in_w_out",), (last, update("w_ada")))
    loss, res = _adamw_small(small, gb, [(given[n], moms[n][0], moms[n][1]) for n, _, _ in SMALL])
    for n, _, _ in SMALL:
        out[n] = tuple(res[n])
    return (loss.reshape(()), grad_x, *[out[n][0] for n in order], *[out[n][1] for n in order],
            *[out[n][2] for n in order], *[out[n][3] for n in order])
```

```python
import functools

import numpy as np
import jax
import jax.numpy as jnp
from jax import lax
from jax.experimental import pallas as pl
from jax.experimental.pallas import tpu as pltpu

f32 = jnp.float32
bf16 = jnp.bfloat16
MESH = pl.DeviceIdType.MESH

D = 1024
SEQ = 2048
BL = 2
HD = 64
AQ = 512
AKV = 128
BW = 512
INW = 2304
DFF = 4096
NMOD = 6
ROT = 16
THETA = 500000.0
EPS = 1e-6
NEG = -1e30
BLK = 128
TM = 512
NJ = SEQ // TM
LANES = 128
NCHIP = 4
NDEV = 8
VMEM_LIMIT = 56 << 20

LR, B1, B2, AEPS, WD, STEP = 0.001, 0.9, 0.999, 1e-08, 0.01, 10

OFF_G_ATTN_PRE, OFF_G_ATTN_POST, OFF_G_MIX_A, OFF_G_MIX_B = 0, 1024, 2048, 2560
OFF_G_MLP_PRE, OFF_G_MLP_POST, OFF_SINK, OFF_LOSS = 3072, 4096, 5120, 5248
PAYW = NMOD * D


def _cp(sem=None):
    return pltpu.CompilerParams(dimension_semantics=sem, vmem_limit_bytes=VMEM_LIMIT)


def _dot(a, b):
    return jnp.dot(a, b, preferred_element_type=f32)


def _dot_nt(a, b):
    return lax.dot_general(a, b, (((1,), (1,)), ((), ())), preferred_element_type=f32)


def _dot_tn(a, b):
    return lax.dot_general(a, b, (((0,), (0,)), ((), ())), preferred_element_type=f32)


def _rms(x):
    r = lax.rsqrt(jnp.mean(x * x, axis=-1, keepdims=True) + EPS)
    return x * r, r


def _rms_bwd(dy, y, r):
    return r * (dy - y * jnp.mean(dy * y, axis=-1, keepdims=True))


def _colsum(v):
    return jnp.sum(v, axis=0, keepdims=True)


def _rope(p, c, s1, s2):
    outs = []
    for c0 in range(0, p.shape[1], LANES):
        pc = p[:, c0:c0 + LANES]
        outs.append(pc * c + pltpu.roll(pc, LANES - ROT // 2, 1) * s1 + pltpu.roll(pc, ROT // 2, 1) * s2)
    return outs[0] if len(outs) == 1 else jnp.concatenate(outs, axis=1)


def _rope_t(g, c, s1, s2):
    outs = []
    for c0 in range(0, g.shape[1], LANES):
        gc = g[:, c0:c0 + LANES]
        outs.append(gc * c + pltpu.roll(gc * s1, ROT // 2, 1) + pltpu.roll(gc * s2, LANES - ROT // 2, 1))
    return outs[0] if len(outs) == 1 else jnp.concatenate(outs, axis=1)


def _perm_store(val, scr, out_ref, d):
    nc = val.shape[1] // LANES
    for c in range(nc):
        scr[c] = val[:, LANES * c:LANES * (c + 1)]
    for c in range(nc):
        for r in range(d):
            out_ref[r, :, LANES * c:LANES * (c + 1)] = scr[c, pl.ds(r, TM // d, stride=d), :].astype(out_ref.dtype)


def _perm_load(in_ref, scr, d):
    nc = in_ref.shape[-1] // LANES
    for c in range(nc):
        for r in range(d):
            scr[c, pl.ds(r, TM // d, stride=d), :] = in_ref[r, :, LANES * c:LANES * (c + 1)].astype(f32)
    return jnp.concatenate([scr[c] for c in range(nc)], axis=1)


def _tok(w, dtype=None):
    return pl.BlockSpec((None, TM, w), lambda b, j: (b, j, 0))


def _perm_spec(d, w):
    return pl.BlockSpec((None, d, TM // d, w), lambda b, j: (b, 0, j, 0))


def _full(shape):
    n = len(shape)
    return pl.BlockSpec(shape, lambda b, j: (0,) * n)


MOD_SPEC = pl.BlockSpec((None, NMOD, D), lambda b, j: (b, 0, 0))
ACCB_SPEC = pl.BlockSpec((None, 8, D), lambda b, j: (b, 0, 0))
ACCG_SPEC = pl.BlockSpec((8, D), lambda b, j: (0, 0))
ACC_SHAPES = [jax.ShapeDtypeStruct((BL, 8, D), f32), jax.ShapeDtypeStruct((8, D), f32)]


def _acc_init(accb_ref, accg_ref):
    b, j = pl.program_id(0), pl.program_id(1)

    @pl.when(j == 0)
    def _():
        accb_ref[...] = jnp.zeros_like(accb_ref)

    @pl.when((b == 0) & (j == 0))
    def _():
        accg_ref[...] = jnp.zeros_like(accg_ref)


def _rope_tables(pos_col, inv_lane):
    def body(p_ref, inv_ref, c_ref, s1_ref, s2_ref):
        ang = p_ref[...].astype(f32) * inv_ref[...]
        j = lax.broadcasted_iota(jnp.int32, (TM, LANES), 1) % HD
        cs, sn = jnp.cos(ang), jnp.sin(ang)
        c_ref[...] = jnp.where(j < ROT, cs, 1.0)
        s1_ref[...] = jnp.where(j < ROT // 2, -sn, 0.0)
        s2_ref[...] = jnp.where((j >= ROT // 2) & (j < ROT), sn, 0.0)

    n = BL * SEQ // TM
    return pl.pallas_call(
        body, name="rope_tables", grid=(n,),
        in_specs=[pl.BlockSpec((TM, 1), lambda i: (i, 0)), pl.BlockSpec((1, LANES), lambda i: (0, 0))],
        out_specs=[pl.BlockSpec((TM, LANES), lambda i: (i, 0))] * 3,
        out_shape=[jax.ShapeDtypeStruct((BL * SEQ, LANES), f32)] * 3,
    )(pos_col, inv_lane)


def _attn_in(x, mod, g_pre, w_in, tc, ts1, ts2):
    def body(x_ref, mod_ref, g_ref, wg_ref, c_ref, s1_ref, s2_ref,
             h_ref, qa_ref, ka_ref, va_ref, q1_ref, k1_ref, v1_ref, q4_ref, k4_ref, v4_ref, q16_ref, k16_ref, v16_ref,
             w_ref, scr):
        @pl.when((pl.program_id(0) == 0) & (pl.program_id(1) == 0))
        def _():
            w_ref[...] = jnp.concatenate([wg_ref[s] for s in range(NCHIP)], axis=1)

        xn, _ = _rms(x_ref[...])
        h = (xn * g_ref[...]) * (1.0 + mod_ref[1:2, :]) + mod_ref[0:1, :]
        hb = h.astype(bf16)
        h_ref[...] = hb
        proj = _dot(hb, w_ref[...])
        c, s1, s2 = c_ref[...], s1_ref[...], s2_ref[...]
        o1, o2, o3, o4, o5 = AQ, AQ + AKV, AQ + 2 * AKV, AQ + 2 * AKV + BW, AQ + 2 * AKV + 2 * BW
        qa_ref[...] = (_rope(proj[:, :o1], c, s1, s2) * 0.125).astype(bf16)
        ka_ref[...] = _rope(proj[:, o1:o2], c, s1, s2).astype(bf16)
        va_ref[...] = proj[:, o2:o3].astype(bf16)
        qb = _rope(proj[:, o3:o4], c, s1, s2) * 0.125
        kb = _rope(proj[:, o4:o5], c, s1, s2)
        vb = proj[:, o5:]
        for val, r1, r4, r16 in ((qb, q1_ref, q4_ref, q16_ref), (kb, k1_ref, k4_ref, k16_ref), (vb, v1_ref, v4_ref, v16_ref)):
            r1[...] = val.astype(bf16)
            _perm_store(val, scr, r4, 4)
            _perm_store(val, scr, r16, 16)

    nat = lambda w: jax.ShapeDtypeStruct((BL, SEQ, w), bf16)
    p4 = jax.ShapeDtypeStruct((BL, 4, SEQ // 4, BW), bf16)
    p16 = jax.ShapeDtypeStruct((BL, 16, SEQ // 16, BW), bf16)
    return pl.pallas_call(
        body, name="attn_in", grid=(BL, NJ),
        in_specs=[_tok(D), MOD_SPEC, _full((1, D)), _full((NCHIP, D, INW // NCHIP)), _tok(LANES), _tok(LANES), _tok(LANES)],
        out_specs=([_tok(D), _tok(AQ), _tok(AKV), _tok(AKV)] + [_tok(BW)] * 3 + [_perm_spec(4, BW)] * 3 + [_perm_spec(16, BW)] * 3
                   + [_full((D, INW))]),
        out_shape=[nat(D), nat(AQ), nat(AKV), nat(AKV)] + [nat(BW)] * 3 + [p4] * 3 + [p16] * 3
                  + [jax.ShapeDtypeStruct((D, INW), bf16)],
        scratch_shapes=[pltpu.VMEM((BW // LANES, TM, LANES), f32)],
        compiler_params=_cp(("arbitrary", "arbitrary")),
    )(x, mod, g_pre, w_in, tc, ts1, ts2)


def _kv_cat(cur_ref, prev_ref, p, gqa, cache):
    def one(ref):
        if not gqa:
            return ref[:, LANES * p:LANES * (p + 1)]
        k = ref[...]
        kr = pltpu.roll(k, HD, 1)
        lo = lax.broadcasted_iota(jnp.int32, k.shape, 1) < HD
        return jnp.where(lo, k, kr) if p < 2 else jnp.where(lo, kr, k)

    key = (id(cur_ref), p // 2 if gqa else p)
    if key not in cache:
        cache[key] = one(cur_ref) if prev_ref is None else jnp.concatenate([one(prev_ref), one(cur_ref)], axis=0)
    return cache[key]


def _lane_half(a, hh):
    lo = lax.broadcasted_iota(jnp.int32, a.shape, 1) < HD
    return jnp.where(lo, a, jnp.zeros_like(a)) if hh == 0 else jnp.where(lo, jnp.zeros_like(a), a)


ATT_UNITS = 2


def _attn_specs(n, nb, descending):
    u = ATT_UNITS
    if nb == 1:
        return (lambda ww: pl.BlockSpec((u, BLK, ww), lambda a, i: (a, 0, 0))), None, (n // u, 1)
    steps = nb // u
    at = (lambda i: steps - 1 - i) if descending else (lambda i: i)
    cur = lambda ww: pl.BlockSpec((None, u * BLK, ww), lambda a, i: (a, at(i), 0))
    prev = lambda ww: pl.BlockSpec((None, BLK, ww), lambda a, i: (a, jnp.maximum(u * at(i) - 1, 0), 0))
    return cur, prev, (n, steps)


def _attn_fwd(q, k, v, sink, *, max_dist, o_dtype, name):
    n, l, w = q.shape
    wk = k.shape[-1]
    nb = l // BLK
    gqa = wk != w
    has_sink = sink is not None

    def body(*refs):
        sink_ref = None
        if has_sink:
            sink_ref, refs = refs[0], refs[1:]
        if nb > 1:
            q_ref, kc_ref, kp_ref, vc_ref, vp_ref, o_ref, lse_ref = refs[:7]
            first = pl.program_id(1) == 0
            for u in range(ATT_UNITS):
                rows, before = pl.ds(BLK * u, BLK), pl.ds(BLK * (u - 1), BLK)
                unit(q_ref.at[rows, :], kc_ref.at[rows, :], kp_ref if u == 0 else kc_ref.at[before, :],
                     vc_ref.at[rows, :], vp_ref if u == 0 else vc_ref.at[before, :], o_ref.at[rows, :], lse_ref.at[rows, :],
                     jnp.logical_not(first) if u == 0 else True, sink_ref, *refs[7:])
        else:
            q_ref, kc_ref, vc_ref, o_ref, lse_ref = refs[:5]
            for u in range(ATT_UNITS):
                unit(q_ref.at[u], kc_ref.at[u], None, vc_ref.at[u], None, o_ref.at[u], lse_ref.at[u], None, sink_ref, *refs[5:])

    def unit(q_ref, kc_ref, kp_ref, vc_ref, vp_ref, o_ref, lse_ref, has_prev, sink_ref, sscr, pscr, dscr):
        qi = lax.broadcasted_iota(jnp.int32, (BLK, BLK), 0)
        kj = lax.broadcasted_iota(jnp.int32, (BLK, BLK), 1)
        tri = kj <= qi
        eye = kj == qi
        cache = {}
        for p in range(w // LANES):
            qpair = q_ref[:, LANES * p:LANES * (p + 1)]
            kcat = _kv_cat(kc_ref, kp_ref, p, gqa, cache)
            for hh in range(2):
                s = _dot_nt(_lane_half(qpair, hh), kcat)
                if nb > 1:
                    sp = s[:, :BLK] if has_prev is True else jnp.where(has_prev, s[:, :BLK], NEG)
                    sscr[2 * p + hh] = jnp.where(tri, s[:, BLK:], sp)
                    if diag:
                        dscr[2 * p + hh] = jnp.where(eye, sp, NEG)
                else:
                    sscr[2 * p + hh] = jnp.where(tri, s, NEG)
        lane = lax.broadcasted_iota(jnp.int32, (BLK, LANES), 1)
        lse_all = jnp.zeros((BLK, LANES), f32)
        for p in range(w // LANES):
            for hh in range(2):
                h = 2 * p + hh
                comb = sscr[h]
                if diag:
                    dtile = dscr[h]
                    m = jnp.max(jnp.maximum(comb, dtile), axis=-1, keepdims=True)
                else:
                    m = jnp.max(comb, axis=-1, keepdims=True)
                if has_sink:
                    sk = sink_ref[0, h]
                    m = jnp.maximum(m, sk)
                e = jnp.exp(comb - m)
                if diag:
                    ed = jnp.exp(dtile - m)
                    den = jnp.sum(e + ed, axis=-1, keepdims=True)
                else:
                    den = jnp.sum(e, axis=-1, keepdims=True)
                if has_sink:
                    den = den + jnp.exp(sk - m)
                inv = 1.0 / den
                if nb > 1:
                    pscr[h, :, :BLK] = (jnp.where(tri, ed if diag else 0.0, e) * inv).astype(bf16)
                    pscr[h, :, BLK:] = (jnp.where(tri, e, 0.0) * inv).astype(bf16)
                else:
                    pscr[h] = (e * inv).astype(bf16)
                lse_all = jnp.where(lane == h, jnp.broadcast_to(m + jnp.log(den), (BLK, LANES)), lse_all)
        lse_ref[...] = lse_all
        for p in range(w // LANES):
            vcat = _kv_cat(vc_ref, vp_ref, p, gqa, cache)
            key = ("halves", id(vc_ref), p // 2 if gqa else p)
            if key not in cache:
                cache[key] = (_lane_half(vcat, 0), _lane_half(vcat, 1))
            o_ref[:, LANES * p:LANES * (p + 1)] = (_dot(pscr[2 * p], cache[key][0])
                                                   + _dot(pscr[2 * p + 1], cache[key][1])).astype(o_ref.dtype)

    assert max_dist in (BLK - 1, BLK)
    diag = nb > 1 and max_dist == BLK
    cur, prev, grid = _attn_specs(n, nb, False)
    in_specs = [cur(w), cur(wk)] + ([prev(wk)] if nb > 1 else []) + [cur(wk)] + ([prev(wk)] if nb > 1 else [])
    args = [q, k] + ([k] if nb > 1 else []) + [v] + ([v] if nb > 1 else [])
    if has_sink:
        in_specs = [pl.BlockSpec(memory_space=pltpu.SMEM)] + in_specs
        args = [sink] + args
    return pl.pallas_call(
        body, name=name, grid=grid, in_specs=in_specs,
        out_specs=[cur(w), cur(LANES)],
        out_shape=[jax.ShapeDtypeStruct((n, l, w), o_dtype), jax.ShapeDtypeStruct((n, l, LANES), f32)],
        scratch_shapes=[pltpu.VMEM((w // HD, BLK, BLK), f32), pltpu.VMEM((w // HD, BLK, 2 * BLK if nb > 1 else BLK), bf16),
                        pltpu.VMEM((w // HD if diag else 1, BLK, BLK), f32)],
        compiler_params=_cp(("arbitrary", "arbitrary")),
    )(*args)


def _attn_bwd(q, k, v, do, delta, lse, sink, *, max_dist, name):
    n, l, w = q.shape
    wk = k.shape[-1]
    nb = l // BLK
    gqa = wk != w
    has_sink = sink is not None

    def body(*refs):
        sink_ref = dsink_ref = ck = cv = None
        if has_sink:
            sink_ref, refs = refs[0], refs[1:]
        nin = 8 if nb > 1 else 6
        ins, rest = refs[:nin], refs[nin:]
        if has_sink:
            dq_ref, dk_ref, dv_ref, dsink_ref = rest[:4]
            rest = rest[4:]
        else:
            dq_ref, dk_ref, dv_ref = rest[:3]
            rest = rest[3:]
        step = pl.program_id(1)
        if has_sink:
            @pl.when((pl.program_id(0) == 0) & (step == 0))
            def _():
                dsink_ref[...] = jnp.zeros_like(dsink_ref)

        if nb > 1:
            q_ref, kc_ref, kp_ref, vc_ref, vp_ref, do_ref, delta_ref, lse_ref = ins
            ck, cv = rest[:2]

            @pl.when(step == 0)
            def _():
                ck[...] = jnp.zeros_like(ck)
                cv[...] = jnp.zeros_like(cv)

            last = step == nb // ATT_UNITS - 1
            for u in reversed(range(ATT_UNITS)):
                rows, before = pl.ds(BLK * u, BLK), pl.ds(BLK * (u - 1), BLK)
                unit(q_ref.at[rows, :], kc_ref.at[rows, :], kp_ref if u == 0 else kc_ref.at[before, :],
                     vc_ref.at[rows, :], vp_ref if u == 0 else vc_ref.at[before, :], do_ref.at[rows, :],
                     delta_ref.at[rows, :], lse_ref.at[rows, :], dq_ref.at[rows, :], dk_ref.at[rows, :], dv_ref.at[rows, :],
                     jnp.logical_not(last) if u == 0 else True, sink_ref, dsink_ref, ck, cv, *rest[2:])
        else:
            q_ref, kc_ref, vc_ref, do_ref, delta_ref, lse_ref = ins
            for u in range(ATT_UNITS):
                unit(q_ref.at[u], kc_ref.at[u], None, vc_ref.at[u], None, do_ref.at[u], delta_ref.at[u], lse_ref.at[u],
                     dq_ref.at[u], dk_ref.at[u], dv_ref.at[u], None, sink_ref, dsink_ref, None, None, *rest)

    def unit(q_ref, kc_ref, kp_ref, vc_ref, vp_ref, do_ref, delta_ref, lse_ref, dq_ref, dk_ref, dv_ref, has_prev,
             sink_ref, dsink_ref, ck, cv, sscr, dpscr, pscr, dsscr, dscr=None, ddscr=None):
        lane = lax.broadcasted_iota(jnp.int32, (BLK, LANES), 1)
        qi = lax.broadcasted_iota(jnp.int32, (BLK, BLK), 0)
        kj = lax.broadcasted_iota(jnp.int32, (BLK, BLK), 1)
        tri = kj <= qi
        eye = kj == qi
        cache = {}
        kp, vp = kp_ref, vp_ref
        rows = 2 * BLK if nb > 1 else BLK
        for p in range(w // LANES):
            sl = slice(LANES * p, LANES * (p + 1))
            qpair, dopair = q_ref[:, sl], do_ref[:, sl]
            kcat, vcat = _kv_cat(kc_ref, kp, p, gqa, cache), _kv_cat(vc_ref, vp, p, gqa, cache)
            for hh in range(2):
                h = 2 * p + hh
                s = _dot_nt(_lane_half(qpair, hh), kcat)
                dp = _dot_nt(_lane_half(dopair, hh), vcat)
                if nb > 1:
                    sp = s[:, :BLK] if has_prev is True else jnp.where(has_prev, s[:, :BLK], NEG)
                    sscr[h] = jnp.where(tri, s[:, BLK:], sp)
                    dpscr[h] = jnp.where(tri, dp[:, BLK:], dp[:, :BLK])
                    if diag:
                        dscr[h] = jnp.where(eye, sp, NEG)
                        ddscr[h] = dp[:, :BLK]
                else:
                    sscr[h] = jnp.where(tri, s, NEG)
                    dpscr[h] = dp
        for p in range(w // LANES):
            for hh in range(2):
                h = 2 * p + hh
                lse_b = jnp.broadcast_to(lse_ref[:, h:h + 1], (BLK, BLK))
                delta = jnp.broadcast_to(delta_ref[:, h:h + 1], (BLK, BLK))
                pr = jnp.exp(sscr[h] - lse_b)
                ds = pr * (dpscr[h] - delta)
                if nb > 1:
                    if diag:
                        prd = jnp.exp(dscr[h] - lse_b)
                        dsd = prd * (ddscr[h] - delta)
                    else:
                        prd = dsd = 0.0
                    pscr[h, :, :BLK] = jnp.where(tri, prd, pr).astype(bf16)
                    pscr[h, :, BLK:] = jnp.where(tri, pr, 0.0).astype(bf16)
                    dsscr[h, :, :BLK] = jnp.where(tri, dsd, ds).astype(bf16)
                    dsscr[h, :, BLK:] = jnp.where(tri, ds, 0.0).astype(bf16)
                else:
                    pscr[h] = pr.astype(bf16)
                    dsscr[h] = ds.astype(bf16)
                if has_sink:
                    dsk = -jnp.sum(jnp.where(lane == 0, jnp.exp(sink_ref[0, h] - lse_b) * delta, 0.0), keepdims=True)
                    dsink_ref[h:h + 1, :] += jnp.broadcast_to(dsk, (1, LANES))
        gk = [jnp.zeros((rows, LANES), f32), jnp.zeros((rows, LANES), f32)]
        gv = [jnp.zeros((rows, LANES), f32), jnp.zeros((rows, LANES), f32)]
        for p in range(w // LANES):
            sl = slice(LANES * p, LANES * (p + 1))
            qpair, dopair = q_ref[:, sl], do_ref[:, sl]
            kcat = _kv_cat(kc_ref, kp, p, gqa, cache)
            key = ("halves", p // 2 if gqa else p)
            if key not in cache:
                cache[key] = (_lane_half(kcat, 0), _lane_half(kcat, 1))
            dq_ref[:, sl] = _dot(dsscr[2 * p], cache[key][0]) + _dot(dsscr[2 * p + 1], cache[key][1])
            dk_pair = _dot_tn(dsscr[2 * p], _lane_half(qpair, 0)) + _dot_tn(dsscr[2 * p + 1], _lane_half(qpair, 1))
            dv_pair = _dot_tn(pscr[2 * p], _lane_half(dopair, 0)) + _dot_tn(pscr[2 * p + 1], _lane_half(dopair, 1))
            if gqa:
                gk[p // 2] = gk[p // 2] + dk_pair
                gv[p // 2] = gv[p // 2] + dv_pair
            elif nb > 1:
                dk_ref[:, sl] = dk_pair[BLK:] + ck[:, sl]
                dv_ref[:, sl] = dv_pair[BLK:] + cv[:, sl]
                ck[:, sl] = dk_pair[:BLK]
                cv[:, sl] = dv_pair[:BLK]
            else:
                dk_ref[:, sl] = dk_pair
                dv_ref[:, sl] = dv_pair
        if gqa:
            lor = lax.broadcasted_iota(jnp.int32, (rows, LANES), 1) < HD
            fold = lambda g: jnp.where(lor, g[0] + pltpu.roll(g[0], HD, 1), g[1] + pltpu.roll(g[1], HD, 1))
            dk_full, dv_full = fold(gk), fold(gv)
            dk_ref[...] = dk_full[BLK:] + ck[...]
            dv_ref[...] = dv_full[BLK:] + cv[...]
            ck[...] = dk_full[:BLK]
            cv[...] = dv_full[:BLK]

    assert max_dist in (BLK - 1, BLK)
    diag = nb > 1 and max_dist == BLK
    cur, prev, grid = _attn_specs(n, nb, True)
    in_specs = ([cur(w), cur(wk)] + ([prev(wk)] if nb > 1 else []) + [cur(wk)] + ([prev(wk)] if nb > 1 else [])
                + [cur(w), cur(LANES), cur(LANES)])
    args = [q, k] + ([k] if nb > 1 else []) + [v] + ([v] if nb > 1 else []) + [do, delta, lse]
    out_specs = [cur(w), cur(wk), cur(wk)]
    out_shape = [jax.ShapeDtypeStruct((n, l, w), f32), jax.ShapeDtypeStruct((n, l, wk), f32), jax.ShapeDtypeStruct((n, l, wk), f32)]
    if has_sink:
        in_specs = [pl.BlockSpec(memory_space=pltpu.SMEM)] + in_specs
        args = [sink] + args
        out_specs.append(pl.BlockSpec((8, LANES), lambda a, i: (0, 0)))
        out_shape.append(jax.ShapeDtypeStruct((8, LANES), f32))
    nh = w // HD
    scratch = [pltpu.VMEM((BLK, wk), f32), pltpu.VMEM((BLK, wk), f32)] if nb > 1 else []
    scratch += [pltpu.VMEM((nh, BLK, BLK), f32)] * 2 + [pltpu.VMEM((nh, BLK, 2 * BLK if nb > 1 else BLK), bf16)] * 2
    if diag:
        scratch += [pltpu.VMEM((nh, BLK, BLK), f32)] * 2
    return pl.pallas_call(
        body, name=name, grid=grid, in_specs=in_specs, out_specs=out_specs, out_shape=out_shape,
        scratch_shapes=scratch, compiler_params=_cp(("arbitrary", "arbitrary")),
    )(*args)


def _split2(x):
    hi = x.astype(bf16)
    return hi, (x - hi.astype(f32)).astype(bf16)


def _heads_to_lanes(xc, e):
    return sum(_dot(t, e) for t in _split2(xc))


def _lanes_to_heads(x, g):
    return sum(_dot(t, g) for t in _split2(x))


HEAD_EXPAND = (np.arange(LANES)[:, None] == np.arange(BW)[None, :] // HD).astype(np.float32)
HEAD_SUM = HEAD_EXPAND.T.copy()


def _branch_weights(l1_ref, l4_ref, l16_ref, scr):
    l4v = _perm_load(l4_ref, scr, 4)
    l16v = _perm_load(l16_ref, scr, 16)
    l1v = l1_ref[...]
    m = jnp.maximum(jnp.maximum(l1v, l4v), l16v)
    e1, e4, e16 = jnp.exp(l1v - m), jnp.exp(l4v - m), jnp.exp(l16v - m)
    z = e1 + e4 + e16
    return e1 / z, e4 / z, e16 / z


def _mix_out(oa, o1, l1, o4, l4, o16, l16, g_mix_a, g_mix_b, w_out, x, mod, g_post):
    def body(oa_ref, o1_ref, l1_ref, o4_ref, l4_ref, o16_ref, l16_ref, ga_ref, gb_ref, w_ref, x_ref, mod_ref, gp_ref, e_ref,
             x1_ref, y_ref, mixed_ref, ob_ref, scr):
        w1, w4, w16 = _branch_weights(l1_ref, l4_ref, l16_ref, scr)
        e = e_ref[...]
        x1w, x4w = _heads_to_lanes(w1, e), _heads_to_lanes(w4, e)
        ob = (x1w * o1_ref[...].astype(f32) + x4w * _perm_load(o4_ref, scr, 4)
              + (1.0 - x1w - x4w) * _perm_load(o16_ref, scr, 16))
        ob_ref[...] = ob
        oan, _ = _rms(oa_ref[...])
        obn, _ = _rms(ob)
        mixed = jnp.concatenate([oan * ga_ref[...], obn * gb_ref[...]], axis=1).astype(bf16)
        mixed_ref[...] = mixed
        y = _dot(mixed, w_ref[...])
        y_ref[...] = y
        yn, _ = _rms(y)
        x1_ref[...] = x_ref[...] + mod_ref[2:3, :] * (yn * gp_ref[...])

    nat = lambda w, dt: jax.ShapeDtypeStruct((BL, SEQ, w), dt)
    return pl.pallas_call(
        body, name="mix_out", grid=(BL, NJ),
        in_specs=[_tok(AQ), _tok(BW), _tok(LANES), _perm_spec(4, BW), _perm_spec(4, LANES), _perm_spec(16, BW),
                  _perm_spec(16, LANES), _full((1, AQ)), _full((1, BW)), _full((D, D)), _tok(D), MOD_SPEC, _full((1, D)),
                  _full((LANES, BW))],
        out_specs=[_tok(D), _tok(D), _tok(D), _tok(BW)],
        out_shape=[nat(D, f32), nat(D, f32), nat(D, bf16), nat(BW, f32)],
        scratch_shapes=[pltpu.VMEM((BW // LANES, TM, LANES), f32)],
        compiler_params=_cp(("arbitrary", "arbitrary")),
    )(oa, o1, l1, o4, l4, o16, l16, g_mix_a, g_mix_b, w_out, x, mod, g_post, jnp.asarray(HEAD_EXPAND, bf16))


def _mlp_up(x1, mod, g_pre, w_up):
    def body(x_ref, mod_ref, g_ref, w_ref, h_ref, u_ref, a_ref):
        xn, _ = _rms(x_ref[...])
        h = (xn * g_ref[...]) * (1.0 + mod_ref[4:5, :]) + mod_ref[3:4, :]
        hb = h.astype(bf16)
        h_ref[...] = hb
        for s in range(NCHIP):
            u = _dot(hb, w_ref[s])
            u_ref[:, D * s:D * (s + 1)] = u.astype(bf16)
            a_ref[:, D * s:D * (s + 1)] = jnp.square(jnp.maximum(u, 0.0)).astype(bf16)

    nat = lambda w: jax.ShapeDtypeStruct((BL, SEQ, w), bf16)
    return pl.pallas_call(
        body, name="mlp_up", grid=(BL, NJ),
        in_specs=[_tok(D), MOD_SPEC, _full((1, D)), _full((NCHIP, D, D))],
        out_specs=[_tok(D), _tok(DFF), _tok(DFF)], out_shape=[nat(D), nat(DFF), nat(DFF)],
        compiler_params=_cp(("arbitrary", "arbitrary")),
    )(x1, mod, g_pre, w_up)


def _mlp_down(a, w_down, x1, target, mod, g_post):
    def body(a_ref, w_ref, x_ref, t_ref, mod_ref, g_ref, gx_ref, dy_ref, accb_ref, accg_ref):
        _acc_init(accb_ref, accg_ref)
        y2 = _dot(a_ref[...], w_ref[...])
        yn, r = _rms(y2)
        g = g_ref[...]
        gt = mod_ref[5:6, :]
        n2 = yn * g
        err = x_ref[...] + gt * n2 - t_ref[...]
        gout = err * (1.0 / D)
        gx_ref[...] = gout
        dn2 = gout * gt
        dy_ref[...] = _rms_bwd(dn2 * g, yn, r).astype(bf16)
        accb_ref[0:1, :] += _colsum(gout * n2)
        accg_ref[0:1, :] += _colsum(dn2 * yn)
        accg_ref[1:2, :] += jnp.broadcast_to(jnp.sum(err * err, keepdims=True), (1, D))

    return pl.pallas_call(
        body, name="mlp_down", grid=(BL, NJ),
        in_specs=[_tok(DFF), _full((DFF, D)), _tok(D), _tok(D), MOD_SPEC, _full((1, D))],
        out_specs=[_tok(D), _tok(D), ACCB_SPEC, ACCG_SPEC],
        out_shape=[jax.ShapeDtypeStruct((BL, SEQ, D), f32), jax.ShapeDtypeStruct((BL, SEQ, D), bf16)] + ACC_SHAPES,
        compiler_params=_cp(("arbitrary", "arbitrary")),
    )(a, w_down, x1, target, mod, g_post)


def _mlp_bwd(dy2, u, w_down, w_up, x1, gx, mod, g_pre):
    def body(dy_ref, u_ref, wd_hbm, wu_hbm, x_ref, gx_ref, mod_ref, g_ref, du_ref, gx1_ref, accb_ref, accg_ref, wd, wu, sem):
        _acc_init(accb_ref, accg_ref)

        @pl.when((pl.program_id(0) == 0) & (pl.program_id(1) == 0))
        def _():
            c1 = pltpu.make_async_copy(wd_hbm, wd, sem.at[0])
            c2 = pltpu.make_async_copy(wu_hbm, wu, sem.at[1])
            c1.start()
            c2.start()
            c1.wait()
            c2.wait()

        dy = dy_ref[...]
        dh = jnp.zeros((TM, D), f32)
        for s in range(NCHIP):
            sl = slice(D * s, D * (s + 1))
            da = _dot_nt(dy, wd[sl, :])
            du = (da * (2.0 * jnp.maximum(u_ref[:, sl].astype(f32), 0.0))).astype(bf16)
            du_ref[:, sl] = du
            dh = dh + _dot_nt(du, wu[s])
        xn, r = _rms(x_ref[...])
        g = g_ref[...]
        n = xn * g
        dn = dh * (1.0 + mod_ref[4:5, :])
        gx1_ref[...] = gx_ref[...] + _rms_bwd(dn * g, xn, r)
        accb_ref[0:1, :] += _colsum(dh * n)
        accb_ref[1:2, :] += _colsum(dh)
        accg_ref[0:1, :] += _colsum(dn * xn)

    anyspec = pl.BlockSpec(memory_space=pl.ANY)
    return pl.pallas_call(
        body, name="mlp_bwd", grid=(BL, NJ),
        in_specs=[_tok(D), _tok(DFF), anyspec, anyspec, _tok(D), _tok(D), MOD_SPEC, _full((1, D))],
        out_specs=[_tok(DFF), _tok(D), ACCB_SPEC, ACCG_SPEC],
        out_shape=[jax.ShapeDtypeStruct((BL, SEQ, DFF), bf16), jax.ShapeDtypeStruct((BL, SEQ, D), f32)] + ACC_SHAPES,
        scratch_shapes=[pltpu.VMEM((DFF, D), bf16), pltpu.VMEM((NCHIP, D, D), bf16), pltpu.SemaphoreType.DMA((2,))],
        compiler_params=_cp(("arbitrary", "arbitrary")),
    )(dy2, u, w_down, w_up, x1, gx, mod, g_pre)


def _matmul_tn(a, b, *, tn, col_blocked, name, out_dtype=f32):
    t, m = a.shape
    n = b.shape[1]
    tmm = min(m, 1024)
    tk = 2048 if tn <= 1024 else 1024
    nk = t // tk

    def body(a_ref, b_ref, o_ref, acc):
        k = pl.program_id(2)

        @pl.when(k == 0)
        def _():
            acc[...] = jnp.zeros_like(acc)

        acc[...] += _dot_tn(a_ref[...], b_ref[...])

        @pl.when(k == nk - 1)
        def _():
            o_ref[...] = acc[...].astype(out_dtype)

    if col_blocked:
        out_spec = pl.BlockSpec((None, tmm, tn), lambda i, j, k: (j, i, 0))
        out_shape = jax.ShapeDtypeStruct((n // tn, m, tn), out_dtype)
    else:
        out_spec = pl.BlockSpec((tmm, tn), lambda i, j, k: (i, j))
        out_shape = jax.ShapeDtypeStruct((m, n), out_dtype)
    return pl.pallas_call(
        body, name=name, grid=(m // tmm, n // tn, nk),
        in_specs=[pl.BlockSpec((tk, tmm), lambda i, j, k: (k, i)), pl.BlockSpec((tk, tn), lambda i, j, k: (k, j))],
        out_specs=out_spec, out_shape=out_shape, scratch_shapes=[pltpu.VMEM((tmm, tn), f32)],
        compiler_params=_cp(("arbitrary", "arbitrary", "arbitrary")),
    )(a, b)


def _grad_w_in(h, dproj):
    t = h.shape[0]
    tk = 1024
    nk = t // tk
    sw = INW // NCHIP

    def body(a_ref, b_ref, o_ref, acc):
        k = pl.program_id(0)

        @pl.when(k == 0)
        def _():
            acc[...] = jnp.zeros_like(acc)

        acc[...] += _dot_tn(a_ref[...], b_ref[...])

        @pl.when(k == nk - 1)
        def _():
            for s in range(NCHIP):
                o_ref[s] = acc[:, sw * s:sw * (s + 1)]

    return pl.pallas_call(
        body, name="grad_w_in", grid=(nk,),
        in_specs=[pl.BlockSpec((tk, D), lambda k: (k, 0)), pl.BlockSpec((tk, INW), lambda k: (k, 0))],
        out_specs=pl.BlockSpec((NCHIP, D, sw), lambda k: (0, 0, 0)), out_shape=jax.ShapeDtypeStruct((NCHIP, D, sw), f32),
        scratch_shapes=[pltpu.VMEM((D, INW), f32)], compiler_params=_cp(("arbitrary",)),
    )(h, dproj)


def _attn_out_bwd(gx1, y, mod, g_post, w_out, oa, ob, g_mix_a, g_mix_b, l1, l4, l16):
    def body(gx_ref, y_ref, mod_ref, gp_ref, w_ref, oa_ref, ob_ref, ga_ref, gb_ref, l1_ref, l4_ref, l16_ref, e_ref, g_ref,
             dy_ref, doa_ref, do1_ref, do4_ref, do16_ref, da_ref, d1_ref, d4_ref, d16_ref, accb_ref, accg_ref, scr):
        _acc_init(accb_ref, accg_ref)
        w1, w4, w16 = _branch_weights(l1_ref, l4_ref, l16_ref, scr)
        e, hs = e_ref[...], g_ref[...]
        gx1v = gx_ref[...]
        yn, ry = _rms(y_ref[...])
        gp = gp_ref[...]
        gt = mod_ref[2:3, :]
        dn1 = gx1v * gt
        dy = _rms_bwd(dn1 * gp, yn, ry).astype(bf16)
        dy_ref[...] = dy
        dmixed = _dot_nt(dy, w_ref[...])
        dma, dmb = dmixed[:, :AQ], dmixed[:, AQ:]
        oa, ob = oa_ref[...], ob_ref[...]
        oan, ra = _rms(oa)
        obn, rb = _rms(ob)
        doa = _rms_bwd(dma * ga_ref[...], oan, ra)
        doa_ref[...] = doa.astype(bf16)
        da_ref[...] = _lanes_to_heads(doa * oa, hs)
        dob = _rms_bwd(dmb * gb_ref[...], obn, rb)
        dd = _lanes_to_heads(dob * ob, hs)
        x1w, x4w = _heads_to_lanes(w1, e), _heads_to_lanes(w4, e)
        do1_ref[...] = (x1w * dob).astype(bf16)
        d1_ref[...] = w1 * dd
        _perm_store(x4w * dob, scr, do4_ref, 4)
        _perm_store(w4 * dd, scr, d4_ref, 4)
        _perm_store((1.0 - x1w - x4w) * dob, scr, do16_ref, 16)
        _perm_store(w16 * dd, scr, d16_ref, 16)
        accb_ref[0:1, :] += _colsum(gx1v * (yn * gp))
        accg_ref[0:1, :] += _colsum(dn1 * yn)
        accg_ref[1:2, :] += jnp.concatenate([_colsum(dma * oan), _colsum(dmb * obn)], axis=1)

    nat = lambda w, dt: jax.ShapeDtypeStruct((BL, SEQ, w), dt)
    return pl.pallas_call(
        body, name="attn_out_bwd", grid=(BL, NJ),
        in_specs=[_tok(D), _tok(D), MOD_SPEC, _full((1, D)), _full((D, D)), _tok(AQ), _tok(BW), _full((1, AQ)), _full((1, BW)),
                  _tok(LANES), _perm_spec(4, LANES), _perm_spec(16, LANES), _full((LANES, BW)), _full((BW, LANES))],
        out_specs=[_tok(D), _tok(AQ), _tok(BW), _perm_spec(4, BW), _perm_spec(16, BW),
                   _tok(LANES), _tok(LANES), _perm_spec(4, LANES), _perm_spec(16, LANES), ACCB_SPEC, ACCG_SPEC],
        out_shape=[nat(D, bf16), nat(AQ, bf16), nat(BW, bf16), jax.ShapeDtypeStruct((BL, 4, SEQ // 4, BW), bf16),
                   jax.ShapeDtypeStruct((BL, 16, SEQ // 16, BW), bf16), nat(LANES, f32), nat(LANES, f32),
                   jax.ShapeDtypeStruct((BL, 4, SEQ // 4, LANES), f32), jax.ShapeDtypeStruct((BL, 16, SEQ // 16, LANES), f32)]
                  + ACC_SHAPES,
        scratch_shapes=[pltpu.VMEM((BW // LANES, TM, LANES), f32)],
        compiler_params=_cp(("arbitrary", "arbitrary")),
    )(gx1, y, mod, g_post, w_out, oa, ob, g_mix_a, g_mix_b, l1, l4, l16, jnp.asarray(HEAD_EXPAND, bf16),
      jnp.asarray(HEAD_SUM, bf16))


def _attn_in_bwd(dqa, dka, dva, d1, d4, d16, tc, ts1, ts2, w_in, x, gx1, mod, g_pre):
    def body(dqa_ref, dka_ref, dva_ref, dq1_ref, dk1_ref, dv1_ref, dq4_ref, dk4_ref, dv4_ref, dq16_ref, dk16_ref, dv16_ref,
             c_ref, s1_ref, s2_ref, w_ref, x_ref, gx_ref, mod_ref, g_ref, dproj_ref, dx_ref, accb_ref, accg_ref, scr):
        _acc_init(accb_ref, accg_ref)
        c, s1, s2 = c_ref[...], s1_ref[...], s2_ref[...]
        tot = lambda r1, r4, r16: r1[...] + _perm_load(r4, scr, 4) + _perm_load(r16, scr, 16)
        dqb = tot(dq1_ref, dq4_ref, dq16_ref)
        dkb = tot(dk1_ref, dk4_ref, dk16_ref)
        dvb = tot(dv1_ref, dv4_ref, dv16_ref)
        dproj = jnp.concatenate([
            _rope_t(dqa_ref[...], c, s1, s2) * 0.125, _rope_t(dka_ref[...], c, s1, s2), dva_ref[...],
            _rope_t(dqb, c, s1, s2) * 0.125, _rope_t(dkb, c, s1, s2), dvb], axis=1).astype(bf16)
        dproj_ref[...] = dproj
        dh = _dot_nt(dproj, w_ref[...])
        xn, r = _rms(x_ref[...])
        g = g_ref[...]
        dn = dh * (1.0 + mod_ref[1:2, :])
        dx_ref[...] = gx_ref[...] + _rms_bwd(dn * g, xn, r)
        accb_ref[0:1, :] += _colsum(dh * (xn * g))
        accb_ref[1:2, :] += _colsum(dh)
        accg_ref[0:1, :] += _colsum(dn * xn)

    return pl.pallas_call(
        body, name="attn_in_bwd", grid=(BL, NJ),
        in_specs=[_tok(AQ), _tok(AKV), _tok(AKV)] + [_tok(BW)] * 3 + [_perm_spec(4, BW)] * 3 + [_perm_spec(16, BW)] * 3
                 + [_tok(LANES)] * 3 + [_full((D, INW)), _tok(D), _tok(D), MOD_SPEC, _full((1, D))],
        out_specs=[_tok(INW), _tok(D), ACCB_SPEC, ACCG_SPEC],
        out_shape=[jax.ShapeDtypeStruct((BL, SEQ, INW), bf16), jax.ShapeDtypeStruct((BL, SEQ, D), f32)] + ACC_SHAPES,
        scratch_shapes=[pltpu.VMEM((BW // LANES, TM, LANES), f32)],
        compiler_params=_cp(("arbitrary", "arbitrary")),
    )(dqa, dka, dva, *d1, *d4, *d16, tc, ts1, ts2, w_in, x, gx1, mod, g_pre)


def _inv_lane():
    inv = np.float32(THETA) ** (-np.arange(0, ROT, 2, dtype=np.float32) / np.float32(ROT))
    lane = np.arange(LANES) % HD
    return jnp.asarray(np.where(lane < ROT, inv[lane % (ROT // 2)], 0.0).astype(np.float32)[None, :])


def _local_step(x, positions, mod, target, inv_lane, first_weight, later_weights, grad_ready, g_attn_pre,
                g_attn_post, sink_a, g_mix_a, g_mix_b, g_mlp_pre, g_mlp_post):
    tabs = _rope_tables(positions.reshape(BL * SEQ, 1), inv_lane)
    w_in = first_weight(tuple(tabs))
    tc, ts1, ts2 = [t.reshape(BL, SEQ, LANES) for t in tabs]

    (h, qa, ka, va, q1, k1, v1, q4, k4, v4, q16, k16, v16, w_in) = _attn_in(x, mod, g_attn_pre, w_in, tc, ts1, ts2)
    seqs = lambda t: t.reshape(t.shape[0] * t.shape[1], t.shape[2], t.shape[3])
    q4, k4, v4, q16, k16, v16 = [seqs(t) for t in (q4, k4, v4, q16, k16, v16)]
    oa, la = _attn_fwd(qa, ka, va, sink_a, max_dist=BLK - 1, o_dtype=f32, name="attn_a_fwd")
    o1, l1 = _attn_fwd(q1, k1, v1, None, max_dist=BLK, o_dtype=bf16, name="attn_b1_fwd")
    o4, l4 = _attn_fwd(q4, k4, v4, None, max_dist=BLK, o_dtype=bf16, name="attn_b4_fwd")
    o16, l16 = _attn_fwd(q16, k16, v16, None, max_dist=BLK, o_dtype=bf16, name="attn_b16_fwd")
    b4 = lambda t: t.reshape(BL, 4, SEQ // 4, t.shape[-1])
    b16 = lambda t: t.reshape(BL, 16, SEQ // 16, t.shape[-1])
    w_out, mlp_weights, mod = later_weights((oa, o1, o4, o16), mod)
    x1, y, mixed, ob = _mix_out(oa, o1, l1, b4(o4), b4(l4), b16(o16), b16(l16), g_mix_a, g_mix_b, w_out, x, mod, g_attn_post)
    w_up, w_down = mlp_weights((x1,))
    h2, u, a = _mlp_up(x1, mod, g_mlp_pre, w_up)
    gx, dy2, accb_d, accg_d = _mlp_down(a, w_down, x1, target, mod, g_mlp_post)

    flat = lambda t: t.reshape(BL * SEQ, t.shape[-1])
    mod = grad_ready("w_down", _matmul_tn(flat(a), flat(dy2), tn=D, col_blocked=False, name="grad_w_down", out_dtype=bf16), mod)
    du, gx1, accb_m, accg_m = _mlp_bwd(dy2, u, w_down, w_up, x1, gx, mod, g_mlp_pre)
    mod = grad_ready("w_up", _matmul_tn(flat(h2), flat(du), tn=D, col_blocked=True, name="grad_w_up", out_dtype=bf16), mod)

    dy, doa, do1, do4, do16, da, dl1, dl4, dl16, accb_o, accg_o = _attn_out_bwd(
        gx1, y, mod, g_attn_post, w_out, oa, ob, g_mix_a, g_mix_b, l1, b4(l4), b16(l16))
    gw_out = _matmul_tn(flat(mixed), flat(dy), tn=D, col_blocked=False, name="grad_w_out")
    dqa, dka, dva, dsink = _attn_bwd(qa, ka, va, doa, da, la, sink_a, max_dist=BLK - 1, name="attn_a_bwd")
    d1 = _attn_bwd(q1, k1, v1, do1, dl1, l1, None, max_dist=BLK, name="attn_b1_bwd")
    d4 = _attn_bwd(q4, k4, v4, seqs(do4), seqs(dl4), l4, None, max_dist=BLK, name="attn_b4_bwd")
    d16 = _attn_bwd(q16, k16, v16, seqs(do16), seqs(dl16), l16, None, max_dist=BLK, name="attn_b16_bwd")
    dproj, grad_x, accb_i, accg_i = _attn_in_bwd(dqa, dka, dva, d1, [b4(t) for t in d4], [b16(t) for t in d16],
                                                 tc, ts1, ts2, w_in, x, gx1, mod, g_attn_pre)
    gw_in = _grad_w_in(flat(h), flat(dproj))
    dsink = grad_ready("w_in_w_out", (gw_in, gw_out), dsink)

    return grad_x, (accb_i, accb_o, accb_m, accb_d, accg_i, accg_o, accg_m, accg_d, dsink)


ADAW = NMOD * D // NCHIP


def _pos():
    return lax.axis_index("x"), lax.axis_index("y"), lax.axis_index("c")


def _flip(v, bit):
    return 1 - v if bit else v


def _all_peers(x, y, c):
    return [(_flip(x, k >> 2 & 1), _flip(y, k >> 1 & 1), _flip(c, k & 1)) for k in range(1, NDEV)]


def _other_chips(x, y):
    return [(1 - x, y), (x, 1 - y), (1 - x, 1 - y)]


def _rcopy(src, dst, send, recv, k, dev, k_recv=None):
    return pltpu.make_async_remote_copy(src_ref=src, dst_ref=dst, send_sem=send.at[k],
                                        recv_sem=recv.at[k if k_recv is None else k_recv],
                                        device_id=dev, device_id_type=MESH)


def _gather_small(src, buf, send, recv):
    x, y, c = _pos()
    me = 4 * x + 2 * y + c
    peers = _all_peers(x, y, c)
    sends = [_rcopy(src, buf.at[me], send, recv, k, p) for k, p in enumerate(peers)]
    for cp in sends:
        cp.start()
    for k, (px, py, pc) in enumerate(peers):
        _rcopy(src, buf.at[4 * px + 2 * py + pc], send, recv, k, (px, py, pc)).wait_recv()
    for cp in sends:
        cp.wait_send()
    return me


def _ada_fwd(c_in, w_ada, b_cols):
    def body(c_ref, w_ref, b_ref, mod_ref, cond_ref, cbuf, mbuf, s1, r1, s2, r2):
        x, y, c = _pos()
        chip = 2 * x + y
        me = _gather_small(c_ref, cbuf, s1, r1)
        cbuf[me] = c_ref[...]
        for i in range(NDEV):
            cond_ref[BL * i:BL * (i + 1), :] = cbuf[i]
        call = cond_ref[...]
        cond = call / (1.0 + jnp.exp(-call))
        cond_ref[...] = cond
        mbuf[chip] = _dot(cond.astype(bf16), w_ref[...].astype(bf16)) + b_ref[...]
        chips = _other_chips(x, y)
        sends = [_rcopy(mbuf.at[chip], mbuf.at[chip], s2, r2, j, (px, py, c)) for j, (px, py) in enumerate(chips)]
        for cp in sends:
            cp.start()
        for j, (px, py) in enumerate(chips):
            _rcopy(mbuf.at[chip], mbuf.at[2 * px + py], s2, r2, j, (px, py, c)).wait_recv()
        for cp in sends:
            cp.wait_send()
        row = lax.broadcasted_iota(jnp.int32, (BL * NDEV, ADAW), 0)
        for s in range(NCHIP):
            slab = mbuf[s]
            for j in range(BL):
                mod_ref[j:j + 1, ADAW * s:ADAW * (s + 1)] = jnp.sum(jnp.where(row == BL * me + j, slab, 0.0), axis=0, keepdims=True)

    vm = pl.BlockSpec(memory_space=pltpu.VMEM)
    return pl.pallas_call(
        body, name="ada_fwd", in_specs=[vm, vm, vm], out_specs=[vm, vm],
        out_shape=[jax.ShapeDtypeStruct((BL, NMOD * D), f32), jax.ShapeDtypeStruct((BL * NDEV, D), f32)],
        scratch_shapes=[pltpu.VMEM((NDEV, BL, D), f32), pltpu.VMEM((NCHIP, BL * NDEV, ADAW), f32),
                        pltpu.SemaphoreType.DMA((NDEV - 1,)), pltpu.SemaphoreType.DMA((NDEV - 1,)),
                        pltpu.SemaphoreType.DMA((NCHIP - 1,)), pltpu.SemaphoreType.DMA((NCHIP - 1,))],
        compiler_params=pltpu.CompilerParams(vmem_limit_bytes=VMEM_LIMIT),
    )(c_in, w_ada, b_cols)


def _small_allreduce(accs, cond_all):
    def body(bi, bo, bm, bd, gi, go, gm, gd, dsink, cond_ref, gw_ref, gb_ref, small_ref, pay, pbuf, dall, s1, r1):
        x, y, c = _pos()
        chip = 2 * x + y
        pay[...] = jnp.zeros_like(pay)
        for b in range(BL):
            for k, (ref, r) in enumerate(((bi, 1), (bi, 0), (bo, 0), (bm, 1), (bm, 0), (bd, 0))):
                pay[b:b + 1, D * k:D * (k + 1)] = ref[b, r:r + 1, :]
        for off, ref, r in ((OFF_G_ATTN_PRE, gi, 0), (OFF_G_ATTN_POST, go, 0), (OFF_G_MIX_A, go, 1), (OFF_G_MLP_PRE, gm, 0),
                            (OFF_G_MLP_POST, gd, 0)):
            pay[BL:BL + 1, off:off + D] = ref[r:r + 1, :]
        eye = lax.broadcasted_iota(jnp.int32, (8, LANES), 0) == lax.broadcasted_iota(jnp.int32, (8, LANES), 1)
        pay[BL:BL + 1, OFF_SINK:OFF_SINK + LANES] = jnp.sum(jnp.where(eye, dsink[...], 0.0), axis=0, keepdims=True)
        pay[BL:BL + 1, OFF_LOSS:OFF_LOSS + LANES] = gd[1:2, 0:LANES]
        me = _gather_small(pay, pbuf, s1, r1)
        pbuf[me] = pay[...]
        small = pbuf[0, BL:BL + 1, :]
        for i in range(1, NDEV):
            small = small + pbuf[i, BL:BL + 1, :]
        small_ref[...] = small
        for i in range(NDEV):
            dall[BL * i:BL * (i + 1), :] = pbuf[i, 0:BL, :]
        gb_ref[...] = jnp.sum(dall[...], axis=0, keepdims=True)
        cols = jnp.zeros((BL * NDEV, ADAW), f32)
        for s in range(NCHIP):
            cols = cols + jnp.where(chip == s, dall[:, ADAW * s:ADAW * (s + 1)], 0.0)
        gw_ref[...] = lax.dot_general(cond_ref[...], cols, (((0,), (0,)), ((), ())), preferred_element_type=f32,
                                      precision=lax.Precision.HIGHEST)

    vm = pl.BlockSpec(memory_space=pltpu.VMEM)
    return pl.pallas_call(
        body, name="small_allreduce", in_specs=[vm] * 10, out_specs=[vm] * 3,
        out_shape=[jax.ShapeDtypeStruct((D, ADAW), f32), jax.ShapeDtypeStruct((1, PAYW), f32), jax.ShapeDtypeStruct((1, PAYW), f32)],
        scratch_shapes=[pltpu.VMEM((4, PAYW), f32), pltpu.VMEM((NDEV, 4, PAYW), f32), pltpu.VMEM((BL * NDEV, PAYW), f32),
                        pltpu.SemaphoreType.DMA((NDEV - 1,)), pltpu.SemaphoreType.DMA((NDEV - 1,))],
        compiler_params=pltpu.CompilerParams(vmem_limit_bytes=VMEM_LIMIT),
    )(*accs, cond_all)


def _half(ref, c):
    r2 = ref.shape[0] // 2
    return ref.at[pl.ds(c * r2 if isinstance(c, int) else pl.multiple_of(c * r2, 16), r2), :]


HBM_SPEC = pl.BlockSpec(memory_space=pltpu.HBM)
SEM_SPEC = pl.BlockSpec(memory_space=pltpu.SEMAPHORE)
EFFECT = pltpu.SideEffectType.DATAFLOW_SIDE_EFFECTING
NLINK = NCHIP - 1


def _in_hbm(a):
    return pltpu.with_memory_space_constraint(a, pltpu.HBM)


NSEM = 8


def _split_start(name, srcs, land_shapes, builds, carry, after=(), lands=None):
    n = len(srcs)
    na, nc = len(after), len(carry)

    def body(*refs):
        src, land = refs[:n], refs[n:2 * n]
        kept = refs[2 * n + na:2 * n + na + nc]
        outs = refs[2 * n + na + nc:]
        send, recv, passed = outs[:n], outs[n:2 * n], outs[4 * n:]
        for t in range(n):
            for out_cp, _ in builds[t](src[t], land[t], send[t], recv[t]):
                out_cp.start()
        for a, b in zip(kept, passed):
            b[...] = a[...]

    if lands is None:
        lands = [lax.empty(s.shape, s.dtype) for s in land_shapes]
    lands = [_in_hbm(a) for a in lands]
    sems = [pltpu.SemaphoreType.DMA((NSEM,))] * (2 * n)
    thru = [pltpu.HBM(a.shape, a.dtype) for a in list(srcs) + lands]
    vm = pl.BlockSpec(memory_space=pltpu.VMEM)
    res = pl.pallas_call(
        body, name=name, out_shape=sems + thru + [jax.ShapeDtypeStruct(a.shape, a.dtype) for a in carry],
        in_specs=[HBM_SPEC] * (2 * n) + [pl.BlockSpec(memory_space=pl.ANY)] * na + [vm] * nc,
        out_specs=[SEM_SPEC] * (2 * n) + [HBM_SPEC] * (2 * n) + [vm] * nc,
        input_output_aliases={i: 2 * n + i for i in range(2 * n)},
        compiler_params=pltpu.CompilerParams(has_side_effects=EFFECT),
    )(*[_in_hbm(a) for a in srcs], *lands, *after, *carry)
    flight = [(res[2 * n + t], res[3 * n + t], res[t], res[n + t]) for t in range(n)]
    return flight, list(res[4 * n:])


def _split_wait(name, flight, builds, after):
    m = len(flight)
    na = len(after)

    def body(*refs):
        src, land, send, recv = refs[:m], refs[m:2 * m], refs[2 * m:3 * m], refs[3 * m:4 * m]
        for t in range(m):
            for out_cp, in_cp in builds[t](src[t], land[t], send[t], recv[t]):
                out_cp.wait_send()
                in_cp.wait_recv()

    ops = [f[0] for f in flight] + [f[1] for f in flight] + [f[2] for f in flight] + [f[3] for f in flight]
    res = pl.pallas_call(
        body, name=name, out_shape=[pltpu.HBM(a.shape, a.dtype) for a in ops[:2 * m]],
        in_specs=[HBM_SPEC] * (2 * m) + [SEM_SPEC] * (2 * m) + [pl.BlockSpec(memory_space=pl.ANY)] * na,
        out_specs=[HBM_SPEC] * (2 * m), input_output_aliases={i: i for i in range(2 * m)},
        compiler_params=pltpu.CompilerParams(has_side_effects=EFFECT),
    )(*ops, *after)
    return res[:m], res[m:2 * m]


def _weight_copies(src, land, send, recv):
    x, y, c = _pos()
    chip = 2 * x + y
    return [(_rcopy(_half(src, c), _half(land.at[chip], c), send, recv, j, (px, py, c)),
             _rcopy(_half(src, c), _half(land.at[2 * px + py], c), send, recv, j, (px, py, c)))
            for j, (px, py) in enumerate(_other_chips(x, y))]


def _grad_copies(src, land, send, recv):
    x, y, c = _pos()
    return [(_rcopy(src.at[2 * px + py], land.at[j], send, recv, j, (px, py, c)),
             _rcopy(src.at[2 * px + py], land.at[j], send, recv, j, (px, py, c)))
            for j, (px, py) in enumerate(_other_chips(x, y))]


NDIRECT = NDEV - 1


def _direct_grad_copies(src, land, send, recv):
    x, y, c = _pos()
    out, arrive = [], []
    for j, (px, py) in enumerate(_other_chips(x, y)):
        for hc in range(2):
            out.append(_rcopy(_half(src.at[2 * px + py], hc), land.at[2 * j + c], send, recv, 2 * j + hc, (px, py, hc),
                              k_recv=2 * j + c))
            arrive.append(_rcopy(_half(src.at[2 * px + py], hc), land.at[2 * j + hc], send, recv, 2 * j + hc, (px, py, hc)))
    own = _rcopy(_half(src.at[2 * x + y], 1 - c), land.at[NDIRECT - 1], send, recv, NDIRECT - 1, (x, y, 1 - c))
    return list(zip(out, arrive)) + [(own, own)]


def _pair_grad_copies(src, land, send, recv):
    x, y, c = _pos()
    r2 = src.shape[1] // 2
    cp = _rcopy(src.at[:, pl.ds(pl.multiple_of((1 - c) * r2, 8), r2), :], land, send, recv, 0, (x, y, 1 - c))
    return [(cp, cp)]


def _pair_weight_copies(src, land, send, recv):
    x, y, c = _pos()
    sib = (x, y, 1 - c)
    cps = []
    for j, (px, py) in enumerate(_other_chips(x, y)):
        mine, theirs = _half(land.at[2 * px + py], c), _half(land.at[2 * px + py], 1 - c)
        cps.append((_rcopy(mine, mine, send, recv, j, sib), _rcopy(theirs, theirs, send, recv, j, sib)))
    own = _rcopy(src, land.at[2 * x + y], send, recv, NLINK, sib)
    return cps + [(own, own)]


RS_ROWS = 128


def _pair_add(g, landed, c_arr, name):
    _, r2, cw = landed.shape
    nr = r2 // RS_ROWS

    def body(c_ref, g_ref, p_ref, o_ref):
        o_ref[...] = (g_ref[...] + p_ref[...]).astype(bf16)

    gs = pltpu.PrefetchScalarGridSpec(
        num_scalar_prefetch=1, grid=(NCHIP, nr),
        in_specs=[pl.BlockSpec((None, RS_ROWS, cw), lambda s, j, c: (s, c[0] * nr + j, 0)),
                  pl.BlockSpec((None, RS_ROWS, cw), lambda s, j, c: (s, j, 0))],
        out_specs=pl.BlockSpec((None, RS_ROWS, cw), lambda s, j, c: (s, j, 0)))
    return pl.pallas_call(body, name=name, grid_spec=gs, out_shape=jax.ShapeDtypeStruct((NCHIP, r2, cw), bf16),
                          compiler_params=_cp(("arbitrary", "arbitrary")))(c_arr, g, landed)


def _chip_add(own, landed, pos_arr, name):
    nl, r2, cw = landed.shape
    nr = r2 // RS_ROWS
    whole = own.shape[1] == 2 * r2

    def body(s_ref, h_ref, q_ref, o_ref):
        acc = h_ref[...].astype(f32)
        for j in range(nl):
            acc = acc + q_ref[j].astype(f32)
        o_ref[...] = acc

    gs = pltpu.PrefetchScalarGridSpec(
        num_scalar_prefetch=1, grid=(nr,),
        in_specs=[pl.BlockSpec((None, RS_ROWS, cw), lambda j, s: (s[0], (s[1] * nr if whole else 0) + j, 0)),
                  pl.BlockSpec((nl, RS_ROWS, cw), lambda j, s: (0, j, 0))],
        out_specs=pl.BlockSpec((RS_ROWS, cw), lambda j, s: (s[1] * nr + j, 0)))
    return pl.pallas_call(body, name=name, grid_spec=gs, out_shape=jax.ShapeDtypeStruct((2 * r2, cw), f32),
                          compiler_params=_cp(("arbitrary",)))(pos_arr, own, landed)


def _pair_gather_copies(src, land, send, recv):
    x, y, c = _pos()
    sib = (x, y, 1 - c)
    return [(_rcopy(_half(land, c), _half(land, c), send, recv, 0, sib),
             _rcopy(_half(land, 1 - c), _half(land, 1 - c), send, recv, 0, sib))]


def _adamw_math(w, g, m, v):
    m = B1 * m + (1.0 - B1) * g
    v = B2 * v + (1.0 - B2) * jnp.square(g)
    m_hat = m / (1.0 - B1 ** STEP)
    v_hat = v / (1.0 - B2 ** STEP)
    return -LR * (m_hat / (jnp.sqrt(v_hat) + AEPS) + WD * w), m, v


ADAM_ROWS = 256


def _adamw(w, g, m, v, name):
    r, cw = w.shape

    def body(w_ref, g_ref, m_ref, v_ref, go_ref, d_ref, mo_ref, vo_ref):
        g = g_ref[...]
        go_ref[...] = g
        d_ref[...], mo_ref[...], vo_ref[...] = _adamw_math(w_ref[...], g, m_ref[...], v_ref[...])

    rows = max(k for k in range(8, ADAM_ROWS + 1, 8) if r % k == 0)
    spec = pl.BlockSpec((rows, cw), lambda i: (i, 0))
    return pl.pallas_call(body, name=name, grid=(r // rows,), in_specs=[spec] * 4, out_specs=[spec] * 4,
                          out_shape=[jax.ShapeDtypeStruct((r, cw), f32)] * 4, compiler_params=_cp(("arbitrary",)))(w, g, m, v)


SMALL = (("b_ada", None, PAYW), ("g_attn_pre", OFF_G_ATTN_PRE, D), ("g_attn_post", OFF_G_ATTN_POST, D), ("sink_a", OFF_SINK, 8),
         ("g_mix_a", OFF_G_MIX_A, AQ), ("g_mix_b", OFF_G_MIX_B, BW), ("g_mlp_pre", OFF_G_MLP_PRE, D), ("g_mlp_post", OFF_G_MLP_POST, D))


def _adamw_small(small, gb, params):
    n = len(SMALL)

    def body(*refs):
        small_ref, gb_ref = refs[:2]
        wmv = refs[2:2 + 3 * n]
        loss_ref = refs[2 + 3 * n]
        outs = refs[3 + 3 * n:]
        loss_ref[...] = small_ref[:, OFF_LOSS:OFF_LOSS + 1] * (0.5 / D)
        for i, (_, off, width) in enumerate(SMALL):
            g = gb_ref[...] if off is None else small_ref[:, off:off + width]
            w_ref, m_ref, v_ref = wmv[3 * i:3 * i + 3]
            outs[4 * i][...] = g
            outs[4 * i + 1][...], outs[4 * i + 2][...], outs[4 * i + 3][...] = _adamw_math(w_ref[...], g, m_ref[...], v_ref[...])

    vm = pl.BlockSpec(memory_space=pltpu.VMEM)
    out_shape = [jax.ShapeDtypeStruct((1, 1), f32)]
    for _, _, width in SMALL:
        out_shape += [jax.ShapeDtypeStruct((1, width), f32)] * 4
    flat = [a for wmv in params for a in wmv]
    res = pl.pallas_call(body, name="adamw_small", in_specs=[vm] * (2 + 3 * n), out_specs=[vm] * len(out_shape),
                         out_shape=out_shape)(small, gb, *flat)
    return res[0], {name: res[1 + 4 * i:5 + 4 * i] for i, (name, _, _) in enumerate(SMALL)}


def kernel(x, c, positions, w_ada, b_ada, g_attn_pre, g_attn_post, w_in, sink_a, g_mix_a, g_mix_b, w_out, g_mlp_pre, g_mlp_post, w_up, w_down, loss_target, m_w_ada, m_b_ada, m_g_attn_pre, m_g_attn_post, m_w_in, m_sink_a, m_g_mix_a, m_g_mix_b, m_w_out, m_g_mlp_pre, m_g_mlp_post, m_w_up, m_w_down, v_w_ada, v_b_ada, v_g_attn_pre, v_g_attn_post, v_w_in, v_sink_a, v_g_mix_a, v_g_mix_b, v_w_out, v_g_mlp_pre, v_g_mlp_post, v_w_up, v_w_down):
    given = dict(w_ada=w_ada, b_ada=b_ada, g_attn_pre=g_attn_pre, g_attn_post=g_attn_post, w_in=w_in, sink_a=sink_a, g_mix_a=g_mix_a,
                 g_mix_b=g_mix_b, w_out=w_out, g_mlp_pre=g_mlp_pre, g_mlp_post=g_mlp_post, w_up=w_up, w_down=w_down)
    moms = dict(w_ada=(m_w_ada, v_w_ada), b_ada=(m_b_ada, v_b_ada), g_attn_pre=(m_g_attn_pre, v_g_attn_pre),
                g_attn_post=(m_g_attn_post, v_g_attn_post), w_in=(m_w_in, v_w_in), sink_a=(m_sink_a, v_sink_a),
                g_mix_a=(m_g_mix_a, v_g_mix_a), g_mix_b=(m_g_mix_b, v_g_mix_b), w_out=(m_w_out, v_w_out),
                g_mlp_pre=(m_g_mlp_pre, v_g_mlp_pre), g_mlp_post=(m_g_mlp_post, v_g_mlp_post), w_up=(m_w_up, v_w_up),
                w_down=(m_w_down, v_w_down))
    order = ["w_ada", "b_ada", "g_attn_pre", "g_attn_post", "w_in", "sink_a", "g_mix_a", "g_mix_b", "w_out", "g_mlp_pre",
             "g_mlp_post", "w_up", "w_down"]
    xi, yi, ci = _pos()
    chip = 2 * xi + yi

    c_arr = jnp.reshape(ci, (1,)).astype(jnp.int32)
    pos_arr = jnp.stack([chip, ci]).astype(jnp.int32)
    big = ("w_in", "w_out", "w_up", "w_down")

    b_cols = lax.dynamic_slice(b_ada, (0, chip * ADAW), (1, ADAW))
    mod, cond_all = _ada_fwd(c, w_ada[0], b_cols)
    gathered = [jax.ShapeDtypeStruct((NCHIP,) + given[n].shape[1:], bf16) for n in big]
    flight_in, (mod,) = _split_start("weights_start_first", [w_in[0].astype(bf16)], gathered[:1], [_weight_copies], [mod])
    mod, rest = lax.optimization_barrier((mod, [given[n][0] for n in big[1:]]))
    flight_rest, (mod, inv_lane) = _split_start("weights_start_rest", [w.astype(bf16) for w in rest], gathered[1:],
                                                [_weight_copies] * 3, [mod, _inv_lane()])
    mod = mod.reshape(BL, NMOD, D)

    def first_weight(after):
        srcs, lands = _split_wait("weights_wait_first", flight_in, [_weight_copies], after)
        cross, _ = _split_start("weights_pair_start_first", srcs, None, [_pair_weight_copies], [], lands=lands)
        _, (win_g,) = _split_wait("weights_pair_wait_first", cross, [_pair_weight_copies], ())
        return win_g

    def later_weights(after, carry):
        srcs, lands = _split_wait("weights_wait_rest", flight_rest, [_weight_copies] * 3, after)
        fl, (carry,) = _split_start("weights_pair_start_rest", srcs, None, [_pair_weight_copies] * 3, [carry], lands=lands)
        _, (wout_g,) = _split_wait("weights_pair_wait_out", fl[:1], [_pair_weight_copies], ())

        def mlp_weights(after):
            _, (wup_g, wdn_g) = _split_wait("weights_pair_wait_mlp", fl[1:], [_pair_weight_copies] * 2, after)
            return wup_g, wdn_g.reshape(DFF, D)

        return wout_g.reshape(D, D), mlp_weights, carry

    crossing, pending = {}, {}

    def grad_ready(group, g, carry):
        if group != "w_in_w_out":
            slab = g.reshape(NCHIP, DFF // NCHIP, D) if group == "w_down" else g
            land = jax.ShapeDtypeStruct((NDIRECT, slab.shape[1] // 2, slab.shape[2]), bf16)
            fl, (carry,) = _split_start("grad_start_" + group, [slab], [land], [_direct_grad_copies], [carry])
            pending[group] = ((group,), fl, [_direct_grad_copies])
            return carry
        names = ("w_in", "w_out")
        slabs = [g[0], g[1].reshape(NCHIP, D // NCHIP, D)]
        fl, (carry,) = _split_start("grad_pair_start_" + group, slabs,
                                    [jax.ShapeDtypeStruct((NCHIP, s.shape[1] // 2, s.shape[2]), f32) for s in slabs],
                                    [_pair_grad_copies] * len(names), [carry])
        crossing[group] = (names, fl)
        return carry

    def grad_reduce(group, after, carry):
        names, fl = crossing[group]
        slabs, landed = _split_wait("grad_pair_wait_" + group, fl, [_pair_grad_copies] * len(names), after)
        halves = [_pair_add(s, p, c_arr, "grad_pair_sum_" + n) for s, p, n in zip(slabs, landed, names)]
        fl, (carry,) = _split_start("grad_start_" + group, halves,
                                    [jax.ShapeDtypeStruct((NLINK,) + h.shape[1:], bf16) for h in halves],
                                    [_grad_copies] * len(names), [carry])
        pending[group] = (names, fl, [_grad_copies] * len(names))
        return carry

    grad_x, accs = _local_step(x, positions, mod, loss_target, inv_lane, first_weight, later_weights, grad_ready,
                               g_attn_pre, g_attn_post, sink_a, g_mix_a, g_mix_b, g_mlp_pre, g_mlp_post)

    grads, out = {}, {}

    def update(n):
        tr = (lambda a: a.T) if n == "w_in" else (lambda a: a)
        res = _adamw(tr(given[n][0]), tr(grads[n]), tr(moms[n][0][0]), tr(moms[n][1][0]), "adamw_" + n)
        out[n] = tuple(tr(a)[None] for a in res)
        return res[3]

    def finish(groups, after):
        names = sum((pending[g][0] for g in groups), ())
        fl = sum((pending[g][1] for g in groups), [])
        halves, landed = _split_wait("grad_wait_" + groups[0], fl, sum((pending[g][2] for g in groups), []), after)
        flights = []
        for h, q, n in zip(halves, landed, names):
            full = _chip_add(h, q, pos_arr, "grad_chip_sum_" + n)
            flights.append(_split_start("grad_gather_start_" + n, [jnp.zeros((8, LANES), f32)], None, [_pair_gather_copies],
                                        [], lands=[full])[0])
        last = None
        for n, fl1 in zip(names, flights):
            after = (flights[-1][0][0],) if last is None and fl1 is not flights[-1] else () if last is None else (last,)
            _, (grads[n],) = _split_wait("grad_gather_wait_" + n, fl1, [_pair_gather_copies], after)
            last = update(n)
        return last

    grads["w_ada"], gb, small = _small_allreduce(accs, cond_all)
    small = grad_reduce("w_in_w_out", (small,), small)
    last = finish(("w_down", "w_up"), (small,))
    finish(("w_in_w_out",), (last, update("w_ada")))
    loss, res = _adamw_small(small, gb, [(given[n], moms[n][0], moms[n][1]) for n, _, _ in SMALL])
    for n, _, _ in SMALL:
        out[n] = tuple(res[n])
    return (loss.reshape(()), grad_x, *[out[n][0] for n in order], *[out[n][1] for n in order],
            *[out[n][2] for n in order], *[out[n][3] for n in order])
```

```python
import functools

import numpy as np
import jax
import jax.numpy as jnp
from jax import lax
from jax.experimental import pallas as pl
from jax.experimental.pallas import tpu as pltpu

f32 = jnp.float32
bf16 = jnp.bfloat16
MESH = pl.DeviceIdType.MESH

D = 1024
SEQ = 2048
BL = 2
HD = 64
AQ = 512
AKV = 128
BW = 512
INW = 2304
DFF = 4096
NMOD = 6
ROT = 16
THETA = 500000.0
EPS = 1e-6
NEG = -1e30
BLK = 128
TM = 512
NJ = SEQ // TM
LANES = 128
NCHIP = 4
NDEV = 8
VMEM_LIMIT = 56 << 20

LR, B1, B2, AEPS, WD, STEP = 0.001, 0.9, 0.999, 1e-08, 0.01, 10

OFF_G_ATTN_PRE, OFF_G_ATTN_POST, OFF_G_MIX_A, OFF_G_MIX_B = 0, 1024, 2048, 2560
OFF_G_MLP_PRE, OFF_G_MLP_POST, OFF_SINK, OFF_LOSS = 3072, 4096, 5120, 5248
PAYW = NMOD * D


def _cp(sem=None):
    return pltpu.CompilerParams(dimension_semantics=sem, vmem_limit_bytes=VMEM_LIMIT)


def _dot(a, b):
    return jnp.dot(a, b, preferred_element_type=f32)


def _dot_nt(a, b):
    return lax.dot_general(a, b, (((1,), (1,)), ((), ())), preferred_element_type=f32)


def _dot_tn(a, b):
    return lax.dot_general(a, b, (((0,), (0,)), ((), ())), preferred_element_type=f32)


def _rms(x):
    r = lax.rsqrt(jnp.mean(x * x, axis=-1, keepdims=True) + EPS)
    return x * r, r


def _rms_bwd(dy, y, r):
    return r * (dy - y * jnp.mean(dy * y, axis=-1, keepdims=True))


def _colsum(v):
    return jnp.sum(v, axis=0, keepdims=True)


def _rope(p, c, s1, s2):
    outs = []
    for c0 in range(0, p.shape[1], LANES):
        pc = p[:, c0:c0 + LANES]
        outs.append(pc * c + pltpu.roll(pc, LANES - ROT // 2, 1) * s1 + pltpu.roll(pc, ROT // 2, 1) * s2)
    return outs[0] if len(outs) == 1 else jnp.concatenate(outs, axis=1)


def _rope_t(g, c, s1, s2):
    outs = []
    for c0 in range(0, g.shape[1], LANES):
        gc = g[:, c0:c0 + LANES]
        outs.append(gc * c + pltpu.roll(gc * s1, ROT // 2, 1) + pltpu.roll(gc * s2, LANES - ROT // 2, 1))
    return outs[0] if len(outs) == 1 else jnp.concatenate(outs, axis=1)


def _perm_store(val, scr, out_ref, d):
    nc = val.shape[1] // LANES
    for c in range(nc):
        scr[c] = val[:, LANES * c:LANES * (c + 1)]
    for c in range(nc):
        for r in range(d):
            out_ref[r, :, LANES * c:LANES * (c + 1)] = scr[c, pl.ds(r, TM // d, stride=d), :].astype(out_ref.dtype)


def _perm_load(in_ref, scr, d):
    nc = in_ref.shape[-1] // LANES
    for c in range(nc):
        for r in range(d):
            scr[c, pl.ds(r, TM // d, stride=d), :] = in_ref[r, :, LANES * c:LANES * (c + 1)].astype(f32)
    return jnp.concatenate([scr[c] for c in range(nc)], axis=1)


def _tok(w, dtype=None):
    return pl.BlockSpec((None, TM, w), lambda b, j: (b, j, 0))


def _perm_spec(d, w):
    return pl.BlockSpec((None, d, TM // d, w), lambda b, j: (b, 0, j, 0))


def _full(shape):
    n = len(shape)
    return pl.BlockSpec(shape, lambda b, j: (0,) * n)


MOD_SPEC = pl.BlockSpec((None, NMOD, D), lambda b, j: (b, 0, 0))
ACCB_SPEC = pl.BlockSpec((None, 8, D), lambda b, j: (b, 0, 0))
ACCG_SPEC = pl.BlockSpec((8, D), lambda b, j: (0, 0))
ACC_SHAPES = [jax.ShapeDtypeStruct((BL, 8, D), f32), jax.ShapeDtypeStruct((8, D), f32)]


def _acc_init(accb_ref, accg_ref):
    b, j = pl.program_id(0), pl.program_id(1)

    @pl.when(j == 0)
    def _():
        accb_ref[...] = jnp.zeros_like(accb_ref)

    @pl.when((b == 0) & (j == 0))
    def _():
        accg_ref[...] = jnp.zeros_like(accg_ref)


def _rope_tables(pos_col, inv_lane):
    def body(p_ref, inv_ref, c_ref, s1_ref, s2_ref):
        ang = p_ref[...].astype(f32) * inv_ref[...]
        j = lax.broadcasted_iota(jnp.int32, (TM, LANES), 1) % HD
        cs, sn = jnp.cos(ang), jnp.sin(ang)
        c_ref[...] = jnp.where(j < ROT, cs, 1.0)
        s1_ref[...] = jnp.where(j < ROT // 2, -sn, 0.0)
        s2_ref[...] = jnp.where((j >= ROT // 2) & (j < ROT), sn, 0.0)

    n = BL * SEQ // TM
    return pl.pallas_call(
        body, name="rope_tables", grid=(n,),
        in_specs=[pl.BlockSpec((TM, 1), lambda i: (i, 0)), pl.BlockSpec((1, LANES), lambda i: (0, 0))],
        out_specs=[pl.BlockSpec((TM, LANES), lambda i: (i, 0))] * 3,
        out_shape=[jax.ShapeDtypeStruct((BL * SEQ, LANES), f32)] * 3,
    )(pos_col, inv_lane)


def _attn_in(x, mod, g_pre, w_in, tc, ts1, ts2):
    def body(x_ref, mod_ref, g_ref, wg_ref, c_ref, s1_ref, s2_ref,
             h_ref, qa_ref, ka_ref, va_ref, q1_ref, k1_ref, v1_ref, q4_ref, k4_ref, v4_ref, q16_ref, k16_ref, v16_ref,
             w_ref, scr):
        @pl.when((pl.program_id(0) == 0) & (pl.program_id(1) == 0))
        def _():
            w_ref[...] = jnp.concatenate([wg_ref[s] for s in range(NCHIP)], axis=1)

        xn, _ = _rms(x_ref[...])
        h = (xn * g_ref[...]) * (1.0 + mod_ref[1:2, :]) + mod_ref[0:1, :]
        hb = h.astype(bf16)
        h_ref[...] = hb
        proj = _dot(hb, w_ref[...])
        c, s1, s2 = c_ref[...], s1_ref[...], s2_ref[...]
        o1, o2, o3, o4, o5 = AQ, AQ + AKV, AQ + 2 * AKV, AQ + 2 * AKV + BW, AQ + 2 * AKV + 2 * BW
        qa_ref[...] = (_rope(proj[:, :o1], c, s1, s2) * 0.125).astype(bf16)
        ka_ref[...] = _rope(proj[:, o1:o2], c, s1, s2).astype(bf16)
        va_ref[...] = proj[:, o2:o3].astype(bf16)
        qb = _rope(proj[:, o3:o4], c, s1, s2) * 0.125
        kb = _rope(proj[:, o4:o5], c, s1, s2)
        vb = proj[:, o5:]
        for val, r1, r4, r16 in ((qb, q1_ref, q4_ref, q16_ref), (kb, k1_ref, k4_ref, k16_ref), (vb, v1_ref, v4_ref, v16_ref)):
            r1[...] = val.astype(bf16)
            _perm_store(val, scr, r4, 4)
            _perm_store(val, scr, r16, 16)

    nat = lambda w: jax.ShapeDtypeStruct((BL, SEQ, w), bf16)
    p4 = jax.ShapeDtypeStruct((BL, 4, SEQ // 4, BW), bf16)
    p16 = jax.ShapeDtypeStruct((BL, 16, SEQ // 16, BW), bf16)
    return pl.pallas_call(
        body, name="attn_in", grid=(BL, NJ),
        in_specs=[_tok(D), MOD_SPEC, _full((1, D)), _full((NCHIP, D, INW // NCHIP)), _tok(LANES), _tok(LANES), _tok(LANES)],
        out_specs=([_tok(D), _tok(AQ), _tok(AKV), _tok(AKV)] + [_tok(BW)] * 3 + [_perm_spec(4, BW)] * 3 + [_perm_spec(16, BW)] * 3
                   + [_full((D, INW))]),
        out_shape=[nat(D), nat(AQ), nat(AKV), nat(AKV)] + [nat(BW)] * 3 + [p4] * 3 + [p16] * 3
                  + [jax.ShapeDtypeStruct((D, INW), bf16)],
        scratch_shapes=[pltpu.VMEM((BW // LANES, TM, LANES), f32)],
        compiler_params=_cp(("arbitrary", "arbitrary")),
    )(x, mod, g_pre, w_in, tc, ts1, ts2)


def _kv_cat(cur_ref, prev_ref, p, gqa, cache):
    def one(ref):
        if not gqa:
            return ref[:, LANES * p:LANES * (p + 1)]
        k = ref[...]
        kr = pltpu.roll(k, HD, 1)
        lo = lax.broadcasted_iota(jnp.int32, k.shape, 1) < HD
        return jnp.where(lo, k, kr) if p < 2 else jnp.where(lo, kr, k)

    key = (id(cur_ref), p // 2 if gqa else p)
    if key not in cache:
        cache[key] = one(cur_ref) if prev_ref is None else jnp.concatenate([one(prev_ref), one(cur_ref)], axis=0)
    return cache[key]


def _lane_half(a, hh):
    lo = lax.broadcasted_iota(jnp.int32, a.shape, 1) < HD
    return jnp.where(lo, a, jnp.zeros_like(a)) if hh == 0 else jnp.where(lo, jnp.zeros_like(a), a)


ATT_UNITS = 4


def _attn_specs(n, nb, descending):
    u = ATT_UNITS
    if nb == 1:
        return (lambda ww: pl.BlockSpec((u, BLK, ww), lambda a, i: (a, 0, 0))), None, (n // u, 1)
    steps = nb // u
    at = (lambda i: steps - 1 - i) if descending else (lambda i: i)
    cur = lambda ww: pl.BlockSpec((None, u * BLK, ww), lambda a, i: (a, at(i), 0))
    prev = lambda ww: pl.BlockSpec((None, BLK, ww), lambda a, i: (a, jnp.maximum(u * at(i) - 1, 0), 0))
    return cur, prev, (n, steps)


def _attn_fwd(q, k, v, sink, *, max_dist, o_dtype, name):
    n, l, w = q.shape
    wk = k.shape[-1]
    nb = l // BLK
    gqa = wk != w
    has_sink = sink is not None

    def body(*refs):
        sink_ref = None
        if has_sink:
            sink_ref, refs = refs[0], refs[1:]
        if nb > 1:
            q_ref, kc_ref, kp_ref, vc_ref, vp_ref, o_ref, lse_ref = refs[:7]
            first = pl.program_id(1) == 0
            for u in range(ATT_UNITS):
                rows, before = pl.ds(BLK * u, BLK), pl.ds(BLK * (u - 1), BLK)
                unit(q_ref.at[rows, :], kc_ref.at[rows, :], kp_ref if u == 0 else kc_ref.at[before, :],
                     vc_ref.at[rows, :], vp_ref if u == 0 else vc_ref.at[before, :], o_ref.at[rows, :], lse_ref.at[rows, :],
                     jnp.logical_not(first) if u == 0 else True, sink_ref, *refs[7:])
        else:
            q_ref, kc_ref, vc_ref, o_ref, lse_ref = refs[:5]
            for u in range(ATT_UNITS):
                unit(q_ref.at[u], kc_ref.at[u], None, vc_ref.at[u], None, o_ref.at[u], lse_ref.at[u], None, sink_ref, *refs[5:])

    def unit(q_ref, kc_ref, kp_ref, vc_ref, vp_ref, o_ref, lse_ref, has_prev, sink_ref, sscr, pscr, dscr):
        qi = lax.broadcasted_iota(jnp.int32, (BLK, BLK), 0)
        kj = lax.broadcasted_iota(jnp.int32, (BLK, BLK), 1)
        tri = kj <= qi
        eye = kj == qi
        cache = {}
        for p in range(w // LANES):
            qpair = q_ref[:, LANES * p:LANES * (p + 1)]
            kcat = _kv_cat(kc_ref, kp_ref, p, gqa, cache)
            for hh in range(2):
                s = _dot_nt(_lane_half(qpair, hh), kcat)
                if nb > 1:
                    sp = s[:, :BLK] if has_prev is True else jnp.where(has_prev, s[:, :BLK], NEG)
                    sscr[2 * p + hh] = jnp.where(tri, s[:, BLK:], sp)
                    if diag:
                        dscr[2 * p + hh] = jnp.where(eye, sp, NEG)
                else:
                    sscr[2 * p + hh] = jnp.where(tri, s, NEG)
        lane = lax.broadcasted_iota(jnp.int32, (BLK, LANES), 1)
        lse_all = jnp.zeros((BLK, LANES), f32)
        for p in range(w // LANES):
            for hh in range(2):
                h = 2 * p + hh
                comb = sscr[h]
                if diag:
                    dtile = dscr[h]
                    m = jnp.max(jnp.maximum(comb, dtile), axis=-1, keepdims=True)
                else:
                    m = jnp.max(comb, axis=-1, keepdims=True)
                if has_sink:
                    sk = sink_ref[0, h]
                    m = jnp.maximum(m, sk)
                e = jnp.exp(comb - m)
                if diag:
                    ed = jnp.exp(dtile - m)
                    den = jnp.sum(e + ed, axis=-1, keepdims=True)
                else:
                    den = jnp.sum(e, axis=-1, keepdims=True)
                if has_sink:
                    den = den + jnp.exp(sk - m)
                inv = 1.0 / den
                if nb > 1:
                    pscr[h, :, :BLK] = (jnp.where(tri, ed if diag else 0.0, e) * inv).astype(bf16)
                    pscr[h, :, BLK:] = (jnp.where(tri, e, 0.0) * inv).astype(bf16)
                else:
                    pscr[h] = (e * inv).astype(bf16)
                lse_all = jnp.where(lane == h, jnp.broadcast_to(m + jnp.log(den), (BLK, LANES)), lse_all)
        lse_ref[...] = lse_all
        for p in range(w // LANES):
            vcat = _kv_cat(vc_ref, vp_ref, p, gqa, cache)
            key = ("halves", id(vc_ref), p // 2 if gqa else p)
            if key not in cache:
                cache[key] = (_lane_half(vcat, 0), _lane_half(vcat, 1))
            o_ref[:, LANES * p:LANES * (p + 1)] = (_dot(pscr[2 * p], cache[key][0])
                                                   + _dot(pscr[2 * p + 1], cache[key][1])).astype(o_ref.dtype)

    assert max_dist in (BLK - 1, BLK)
    diag = nb > 1 and max_dist == BLK
    cur, prev, grid = _attn_specs(n, nb, False)
    in_specs = [cur(w), cur(wk)] + ([prev(wk)] if nb > 1 else []) + [cur(wk)] + ([prev(wk)] if nb > 1 else [])
    args = [q, k] + ([k] if nb > 1 else []) + [v] + ([v] if nb > 1 else [])
    if has_sink:
        in_specs = [pl.BlockSpec(memory_space=pltpu.SMEM)] + in_specs
        args = [sink] + args
    return pl.pallas_call(
        body, name=name, grid=grid, in_specs=in_specs,
        out_specs=[cur(w), cur(LANES)],
        out_shape=[jax.ShapeDtypeStruct((n, l, w), o_dtype), jax.ShapeDtypeStruct((n, l, LANES), f32)],
        scratch_shapes=[pltpu.VMEM((w // HD, BLK, BLK), f32), pltpu.VMEM((w // HD, BLK, 2 * BLK if nb > 1 else BLK), bf16),
                        pltpu.VMEM((w // HD if diag else 1, BLK, BLK), f32)],
        compiler_params=_cp(("arbitrary", "arbitrary")),
    )(*args)


def _attn_bwd(q, k, v, do, delta, lse, sink, *, max_dist, name):
    n, l, w = q.shape
    wk = k.shape[-1]
    nb = l // BLK
    gqa = wk != w
    has_sink = sink is not None

    def body(*refs):
        sink_ref = dsink_ref = ck = cv = None
        if has_sink:
            sink_ref, refs = refs[0], refs[1:]
        nin = 8 if nb > 1 else 6
        ins, rest = refs[:nin], refs[nin:]
        if has_sink:
            dq_ref, dk_ref, dv_ref, dsink_ref = rest[:4]
            rest = rest[4:]
        else:
            dq_ref, dk_ref, dv_ref = rest[:3]
            rest = rest[3:]
        step = pl.program_id(1)
        if has_sink:
            @pl.when((pl.program_id(0) == 0) & (step == 0))
            def _():
                dsink_ref[...] = jnp.zeros_like(dsink_ref)

        if nb > 1:
            q_ref, kc_ref, kp_ref, vc_ref, vp_ref, do_ref, delta_ref, lse_ref = ins
            ck, cv = rest[:2]

            @pl.when(step == 0)
            def _():
                ck[...] = jnp.zeros_like(ck)
                cv[...] = jnp.zeros_like(cv)

            last = step == nb // ATT_UNITS - 1
            for u in reversed(range(ATT_UNITS)):
                rows, before = pl.ds(BLK * u, BLK), pl.ds(BLK * (u - 1), BLK)
                unit(q_ref.at[rows, :], kc_ref.at[rows, :], kp_ref if u == 0 else kc_ref.at[before, :],
                     vc_ref.at[rows, :], vp_ref if u == 0 else vc_ref.at[before, :], do_ref.at[rows, :],
                     delta_ref.at[rows, :], lse_ref.at[rows, :], dq_ref.at[rows, :], dk_ref.at[rows, :], dv_ref.at[rows, :],
                     jnp.logical_not(last) if u == 0 else True, sink_ref, dsink_ref, ck, cv, *rest[2:])
        else:
            q_ref, kc_ref, vc_ref, do_ref, delta_ref, lse_ref = ins
            for u in range(ATT_UNITS):
                unit(q_ref.at[u], kc_ref.at[u], None, vc_ref.at[u], None, do_ref.at[u], delta_ref.at[u], lse_ref.at[u],
                     dq_ref.at[u], dk_ref.at[u], dv_ref.at[u], None, sink_ref, dsink_ref, None, None, *rest)

    def unit(q_ref, kc_ref, kp_ref, vc_ref, vp_ref, do_ref, delta_ref, lse_ref, dq_ref, dk_ref, dv_ref, has_prev,
             sink_ref, dsink_ref, ck, cv, sscr, dpscr, pscr, dsscr, dscr=None, ddscr=None):
        lane = lax.broadcasted_iota(jnp.int32, (BLK, LANES), 1)
        qi = lax.broadcasted_iota(jnp.int32, (BLK, BLK), 0)
        kj = lax.broadcasted_iota(jnp.int32, (BLK, BLK), 1)
        tri = kj <= qi
        eye = kj == qi
        cache = {}
        kp, vp = kp_ref, vp_ref
        rows = 2 * BLK if nb > 1 else BLK
        for p in range(w // LANES):
            sl = slice(LANES * p, LANES * (p + 1))
            qpair, dopair = q_ref[:, sl], do_ref[:, sl]
            kcat, vcat = _kv_cat(kc_ref, kp, p, gqa, cache), _kv_cat(vc_ref, vp, p, gqa, cache)
            for hh in range(2):
                h = 2 * p + hh
                s = _dot_nt(_lane_half(qpair, hh), kcat)
                dp = _dot_nt(_lane_half(dopair, hh), vcat)
                if nb > 1:
                    sp = s[:, :BLK] if has_prev is True else jnp.where(has_prev, s[:, :BLK], NEG)
                    sscr[h] = jnp.where(tri, s[:, BLK:], sp)
                    dpscr[h] = jnp.where(tri, dp[:, BLK:], dp[:, :BLK])
                    if diag:
                        dscr[h] = jnp.where(eye, sp, NEG)
                        ddscr[h] = dp[:, :BLK]
                else:
                    sscr[h] = jnp.where(tri, s, NEG)
                    dpscr[h] = dp
        for p in range(w // LANES):
            for hh in range(2):
                h = 2 * p + hh
                lse_b = jnp.broadcast_to(lse_ref[:, h:h + 1], (BLK, BLK))
                delta = jnp.broadcast_to(delta_ref[:, h:h + 1], (BLK, BLK))
                pr = jnp.exp(sscr[h] - lse_b)
                ds = pr * (dpscr[h] - delta)
                if nb > 1:
                    if diag:
                        prd = jnp.exp(dscr[h] - lse_b)
                        dsd = prd * (ddscr[h] - delta)
                    else:
                        prd = dsd = 0.0
                    pscr[h, :, :BLK] = jnp.where(tri, prd, pr).astype(bf16)
                    pscr[h, :, BLK:] = jnp.where(tri, pr, 0.0).astype(bf16)
                    dsscr[h, :, :BLK] = jnp.where(tri, dsd, ds).astype(bf16)
                    dsscr[h, :, BLK:] = jnp.where(tri, ds, 0.0).astype(bf16)
                else:
                    pscr[h] = pr.astype(bf16)
                    dsscr[h] = ds.astype(bf16)
                if has_sink:
                    dsk = -jnp.sum(jnp.where(lane == 0, jnp.exp(sink_ref[0, h] - lse_b) * delta, 0.0), keepdims=True)
                    dsink_ref[h:h + 1, :] += jnp.broadcast_to(dsk, (1, LANES))
        gk = [jnp.zeros((rows, LANES), f32), jnp.zeros((rows, LANES), f32)]
        gv = [jnp.zeros((rows, LANES), f32), jnp.zeros((rows, LANES), f32)]
        for p in range(w // LANES):
            sl = slice(LANES * p, LANES * (p + 1))
            qpair, dopair = q_ref[:, sl], do_ref[:, sl]
            kcat = _kv_cat(kc_ref, kp, p, gqa, cache)
            key = ("halves", p // 2 if gqa else p)
            if key not in cache:
                cache[key] = (_lane_half(kcat, 0), _lane_half(kcat, 1))
            dq_ref[:, sl] = _dot(dsscr[2 * p], cache[key][0]) + _dot(dsscr[2 * p + 1], cache[key][1])
            dk_pair = _dot_tn(dsscr[2 * p], _lane_half(qpair, 0)) + _dot_tn(dsscr[2 * p + 1], _lane_half(qpair, 1))
            dv_pair = _dot_tn(pscr[2 * p], _lane_half(dopair, 0)) + _dot_tn(pscr[2 * p + 1], _lane_half(dopair, 1))
            if gqa:
                gk[p // 2] = gk[p // 2] + dk_pair
                gv[p // 2] = gv[p // 2] + dv_pair
            elif nb > 1:
                dk_ref[:, sl] = dk_pair[BLK:] + ck[:, sl]
                dv_ref[:, sl] = dv_pair[BLK:] + cv[:, sl]
                ck[:, sl] = dk_pair[:BLK]
                cv[:, sl] = dv_pair[:BLK]
            else:
                dk_ref[:, sl] = dk_pair
                dv_ref[:, sl] = dv_pair
        if gqa:
            lor = lax.broadcasted_iota(jnp.int32, (rows, LANES), 1) < HD
            fold = lambda g: jnp.where(lor, g[0] + pltpu.roll(g[0], HD, 1), g[1] + pltpu.roll(g[1], HD, 1))
            dk_full, dv_full = fold(gk), fold(gv)
            dk_ref[...] = dk_full[BLK:] + ck[...]
            dv_ref[...] = dv_full[BLK:] + cv[...]
            ck[...] = dk_full[:BLK]
            cv[...] = dv_full[:BLK]

    assert max_dist in (BLK - 1, BLK)
    diag = nb > 1 and max_dist == BLK
    cur, prev, grid = _attn_specs(n, nb, True)
    in_specs = ([cur(w), cur(wk)] + ([prev(wk)] if nb > 1 else []) + [cur(wk)] + ([prev(wk)] if nb > 1 else [])
                + [cur(w), cur(LANES), cur(LANES)])
    args = [q, k] + ([k] if nb > 1 else []) + [v] + ([v] if nb > 1 else []) + [do, delta, lse]
    out_specs = [cur(w), cur(wk), cur(wk)]
    out_shape = [jax.ShapeDtypeStruct((n, l, w), f32), jax.ShapeDtypeStruct((n, l, wk), f32), jax.ShapeDtypeStruct((n, l, wk), f32)]
    if has_sink:
        in_specs = [pl.BlockSpec(memory_space=pltpu.SMEM)] + in_specs
        args = [sink] + args
        out_specs.append(pl.BlockSpec((8, LANES), lambda a, i: (0, 0)))
        out_shape.append(jax.ShapeDtypeStruct((8, LANES), f32))
    nh = w // HD
    scratch = [pltpu.VMEM((BLK, wk), f32), pltpu.VMEM((BLK, wk), f32)] if nb > 1 else []
    scratch += [pltpu.VMEM((nh, BLK, BLK), f32)] * 2 + [pltpu.VMEM((nh, BLK, 2 * BLK if nb > 1 else BLK), bf16)] * 2
    if diag:
        scratch += [pltpu.VMEM((nh, BLK, BLK), f32)] * 2
    return pl.pallas_call(
        body, name=name, grid=grid, in_specs=in_specs, out_specs=out_specs, out_shape=out_shape,
        scratch_shapes=scratch, compiler_params=_cp(("arbitrary", "arbitrary")),
    )(*args)


def _split2(x):
    hi = x.astype(bf16)
    return hi, (x - hi.astype(f32)).astype(bf16)


def _heads_to_lanes(xc, e):
    return sum(_dot(t, e) for t in _split2(xc))


def _lanes_to_heads(x, g):
    return sum(_dot(t, g) for t in _split2(x))


HEAD_EXPAND = (np.arange(LANES)[:, None] == np.arange(BW)[None, :] // HD).astype(np.float32)
HEAD_SUM = HEAD_EXPAND.T.copy()


def _branch_weights(l1_ref, l4_ref, l16_ref, scr):
    l4v = _perm_load(l4_ref, scr, 4)
    l16v = _perm_load(l16_ref, scr, 16)
    l1v = l1_ref[...]
    m = jnp.maximum(jnp.maximum(l1v, l4v), l16v)
    e1, e4, e16 = jnp.exp(l1v - m), jnp.exp(l4v - m), jnp.exp(l16v - m)
    z = e1 + e4 + e16
    return e1 / z, e4 / z, e16 / z


def _mix_out(oa, o1, l1, o4, l4, o16, l16, g_mix_a, g_mix_b, w_out, x, mod, g_post):
    def body(oa_ref, o1_ref, l1_ref, o4_ref, l4_ref, o16_ref, l16_ref, ga_ref, gb_ref, w_ref, x_ref, mod_ref, gp_ref, e_ref,
             x1_ref, y_ref, mixed_ref, ob_ref, scr):
        w1, w4, w16 = _branch_weights(l1_ref, l4_ref, l16_ref, scr)
        e = e_ref[...]
        x1w, x4w = _heads_to_lanes(w1, e), _heads_to_lanes(w4, e)
        ob = (x1w * o1_ref[...].astype(f32) + x4w * _perm_load(o4_ref, scr, 4)
              + (1.0 - x1w - x4w) * _perm_load(o16_ref, scr, 16))
        ob_ref[...] = ob
        oan, _ = _rms(oa_ref[...])
        obn, _ = _rms(ob)
        mixed = jnp.concatenate([oan * ga_ref[...], obn * gb_ref[...]], axis=1).astype(bf16)
        mixed_ref[...] = mixed
        y = _dot(mixed, w_ref[...])
        y_ref[...] = y
        yn, _ = _rms(y)
        x1_ref[...] = x_ref[...] + mod_ref[2:3, :] * (yn * gp_ref[...])

    nat = lambda w, dt: jax.ShapeDtypeStruct((BL, SEQ, w), dt)
    return pl.pallas_call(
        body, name="mix_out", grid=(BL, NJ),
        in_specs=[_tok(AQ), _tok(BW), _tok(LANES), _perm_spec(4, BW), _perm_spec(4, LANES), _perm_spec(16, BW),
                  _perm_spec(16, LANES), _full((1, AQ)), _full((1, BW)), _full((D, D)), _tok(D), MOD_SPEC, _full((1, D)),
                  _full((LANES, BW))],
        out_specs=[_tok(D), _tok(D), _tok(D), _tok(BW)],
        out_shape=[nat(D, f32), nat(D, f32), nat(D, bf16), nat(BW, f32)],
        scratch_shapes=[pltpu.VMEM((BW // LANES, TM, LANES), f32)],
        compiler_params=_cp(("arbitrary", "arbitrary")),
    )(oa, o1, l1, o4, l4, o16, l16, g_mix_a, g_mix_b, w_out, x, mod, g_post, jnp.asarray(HEAD_EXPAND, bf16))


def _mlp_up(x1, mod, g_pre, w_up):
    def body(x_ref, mod_ref, g_ref, w_ref, h_ref, u_ref, a_ref):
        xn, _ = _rms(x_ref[...])
        h = (xn * g_ref[...]) * (1.0 + mod_ref[4:5, :]) + mod_ref[3:4, :]
        hb = h.astype(bf16)
        h_ref[...] = hb
        for s in range(NCHIP):
            u = _dot(hb, w_ref[s])
            u_ref[:, D * s:D * (s + 1)] = u.astype(bf16)
            a_ref[:, D * s:D * (s + 1)] = jnp.square(jnp.maximum(u, 0.0)).astype(bf16)

    nat = lambda w: jax.ShapeDtypeStruct((BL, SEQ, w), bf16)
    return pl.pallas_call(
        body, name="mlp_up", grid=(BL, NJ),
        in_specs=[_tok(D), MOD_SPEC, _full((1, D)), _full((NCHIP, D, D))],
        out_specs=[_tok(D), _tok(DFF), _tok(DFF)], out_shape=[nat(D), nat(DFF), nat(DFF)],
        compiler_params=_cp(("arbitrary", "arbitrary")),
    )(x1, mod, g_pre, w_up)


def _mlp_down(a, w_down, x1, target, mod, g_post):
    def body(a_ref, w_ref, x_ref, t_ref, mod_ref, g_ref, gx_ref, dy_ref, accb_ref, accg_ref):
        _acc_init(accb_ref, accg_ref)
        y2 = _dot(a_ref[...], w_ref[...])
        yn, r = _rms(y2)
        g = g_ref[...]
        gt = mod_ref[5:6, :]
        n2 = yn * g
        err = x_ref[...] + gt * n2 - t_ref[...]
        gout = err * (1.0 / D)
        gx_ref[...] = gout
        dn2 = gout * gt
        dy_ref[...] = _rms_bwd(dn2 * g, yn, r).astype(bf16)
        accb_ref[0:1, :] += _colsum(gout * n2)
        accg_ref[0:1, :] += _colsum(dn2 * yn)
        accg_ref[1:2, :] += jnp.broadcast_to(jnp.sum(err * err, keepdims=True), (1, D))

    return pl.pallas_call(
        body, name="mlp_down", grid=(BL, NJ),
        in_specs=[_tok(DFF), _full((DFF, D)), _tok(D), _tok(D), MOD_SPEC, _full((1, D))],
        out_specs=[_tok(D), _tok(D), ACCB_SPEC, ACCG_SPEC],
        out_shape=[jax.ShapeDtypeStruct((BL, SEQ, D), f32), jax.ShapeDtypeStruct((BL, SEQ, D), bf16)] + ACC_SHAPES,
        compiler_params=_cp(("arbitrary", "arbitrary")),
    )(a, w_down, x1, target, mod, g_post)


def _mlp_bwd(dy2, u, w_down, w_up, x1, gx, mod, g_pre):
    def body(dy_ref, u_ref, wd_hbm, wu_hbm, x_ref, gx_ref, mod_ref, g_ref, du_ref, gx1_ref, accb_ref, accg_ref, wd, wu, sem):
        _acc_init(accb_ref, accg_ref)

        @pl.when((pl.program_id(0) == 0) & (pl.program_id(1) == 0))
        def _():
            c1 = pltpu.make_async_copy(wd_hbm, wd, sem.at[0])
            c2 = pltpu.make_async_copy(wu_hbm, wu, sem.at[1])
            c1.start()
            c2.start()
            c1.wait()
            c2.wait()

        dy = dy_ref[...]
        dh = jnp.zeros((TM, D), f32)
        for s in range(NCHIP):
            sl = slice(D * s, D * (s + 1))
            da = _dot_nt(dy, wd[sl, :])
            du = (da * (2.0 * jnp.maximum(u_ref[:, sl].astype(f32), 0.0))).astype(bf16)
            du_ref[:, sl] = du
            dh = dh + _dot_nt(du, wu[s])
        xn, r = _rms(x_ref[...])
        g = g_ref[...]
        n = xn * g
        dn = dh * (1.0 + mod_ref[4:5, :])
        gx1_ref[...] = gx_ref[...] + _rms_bwd(dn * g, xn, r)
        accb_ref[0:1, :] += _colsum(dh * n)
        accb_ref[1:2, :] += _colsum(dh)
        accg_ref[0:1, :] += _colsum(dn * xn)

    anyspec = pl.BlockSpec(memory_space=pl.ANY)
    return pl.pallas_call(
        body, name="mlp_bwd", grid=(BL, NJ),
        in_specs=[_tok(D), _tok(DFF), anyspec, anyspec, _tok(D), _tok(D), MOD_SPEC, _full((1, D))],
        out_specs=[_tok(DFF), _tok(D), ACCB_SPEC, ACCG_SPEC],
        out_shape=[jax.ShapeDtypeStruct((BL, SEQ, DFF), bf16), jax.ShapeDtypeStruct((BL, SEQ, D), f32)] + ACC_SHAPES,
        scratch_shapes=[pltpu.VMEM((DFF, D), bf16), pltpu.VMEM((NCHIP, D, D), bf16), pltpu.SemaphoreType.DMA((2,))],
        compiler_params=_cp(("arbitrary", "arbitrary")),
    )(dy2, u, w_down, w_up, x1, gx, mod, g_pre)


def _matmul_tn(a, b, *, tn, col_blocked, name, out_dtype=f32):
    t, m = a.shape
    n = b.shape[1]
    tmm = min(m, 1024)
    tk = 2048 if tn <= 1024 else 1024
    nk = t // tk

    def body(a_ref, b_ref, o_ref, acc):
        k = pl.program_id(2)

        @pl.when(k == 0)
        def _():
            acc[...] = jnp.zeros_like(acc)

        acc[...] += _dot_tn(a_ref[...], b_ref[...])

        @pl.when(k == nk - 1)
        def _():
            o_ref[...] = acc[...].astype(out_dtype)

    if col_blocked:
        out_spec = pl.BlockSpec((None, tmm, tn), lambda i, j, k: (j, i, 0))
        out_shape = jax.ShapeDtypeStruct((n // tn, m, tn), out_dtype)
    else:
        out_spec = pl.BlockSpec((tmm, tn), lambda i, j, k: (i, j))
        out_shape = jax.ShapeDtypeStruct((m, n), out_dtype)
    return pl.pallas_call(
        body, name=name, grid=(m // tmm, n // tn, nk),
        in_specs=[pl.BlockSpec((tk, tmm), lambda i, j, k: (k, i)), pl.BlockSpec((tk, tn), lambda i, j, k: (k, j))],
        out_specs=out_spec, out_shape=out_shape, scratch_shapes=[pltpu.VMEM((tmm, tn), f32)],
        compiler_params=_cp(("arbitrary", "arbitrary", "arbitrary")),
    )(a, b)


def _grad_w_in(h, dproj):
    t = h.shape[0]
    tk = 1024
    nk = t // tk
    sw = INW // NCHIP

    def body(a_ref, b_ref, o_ref, acc):
        k = pl.program_id(0)

        @pl.when(k == 0)
        def _():
            acc[...] = jnp.zeros_like(acc)

        acc[...] += _dot_tn(a_ref[...], b_ref[...])

        @pl.when(k == nk - 1)
        def _():
            for s in range(NCHIP):
                o_ref[s] = acc[:, sw * s:sw * (s + 1)]

    return pl.pallas_call(
        body, name="grad_w_in", grid=(nk,),
        in_specs=[pl.BlockSpec((tk, D), lambda k: (k, 0)), pl.BlockSpec((tk, INW), lambda k: (k, 0))],
        out_specs=pl.BlockSpec((NCHIP, D, sw), lambda k: (0, 0, 0)), out_shape=jax.ShapeDtypeStruct((NCHIP, D, sw), f32),
        scratch_shapes=[pltpu.VMEM((D, INW), f32)], compiler_params=_cp(("arbitrary",)),
    )(h, dproj)


def _attn_out_bwd(gx1, y, mod, g_post, w_out, oa, ob, g_mix_a, g_mix_b, l1, l4, l16):
    def body(gx_ref, y_ref, mod_ref, gp_ref, w_ref, oa_ref, ob_ref, ga_ref, gb_ref, l1_ref, l4_ref, l16_ref, e_ref, g_ref,
             dy_ref, doa_ref, do1_ref, do4_ref, do16_ref, da_ref, d1_ref, d4_ref, d16_ref, accb_ref, accg_ref, scr):
        _acc_init(accb_ref, accg_ref)
        w1, w4, w16 = _branch_weights(l1_ref, l4_ref, l16_ref, scr)
        e, hs = e_ref[...], g_ref[...]
        gx1v = gx_ref[...]
        yn, ry = _rms(y_ref[...])
        gp = gp_ref[...]
        gt = mod_ref[2:3, :]
        dn1 = gx1v * gt
        dy = _rms_bwd(dn1 * gp, yn, ry).astype(bf16)
        dy_ref[...] = dy
        dmixed = _dot_nt(dy, w_ref[...])
        dma, dmb = dmixed[:, :AQ], dmixed[:, AQ:]
        oa, ob = oa_ref[...], ob_ref[...]
        oan, ra = _rms(oa)
        obn, rb = _rms(ob)
        doa = _rms_bwd(dma * ga_ref[...], oan, ra)
        doa_ref[...] = doa.astype(bf16)
        da_ref[...] = _lanes_to_heads(doa * oa, hs)
        dob = _rms_bwd(dmb * gb_ref[...], obn, rb)
        dd = _lanes_to_heads(dob * ob, hs)
        x1w, x4w = _heads_to_lanes(w1, e), _heads_to_lanes(w4, e)
        do1_ref[...] = (x1w * dob).astype(bf16)
        d1_ref[...] = w1 * dd
        _perm_store(x4w * dob, scr, do4_ref, 4)
        _perm_store(w4 * dd, scr, d4_ref, 4)
        _perm_store((1.0 - x1w - x4w) * dob, scr, do16_ref, 16)
        _perm_store(w16 * dd, scr, d16_ref, 16)
        accb_ref[0:1, :] += _colsum(gx1v * (yn * gp))
        accg_ref[0:1, :] += _colsum(dn1 * yn)
        accg_ref[1:2, :] += jnp.concatenate([_colsum(dma * oan), _colsum(dmb * obn)], axis=1)

    nat = lambda w, dt: jax.ShapeDtypeStruct((BL, SEQ, w), dt)
    return pl.pallas_call(
        body, name="attn_out_bwd", grid=(BL, NJ),
        in_specs=[_tok(D), _tok(D), MOD_SPEC, _full((1, D)), _full((D, D)), _tok(AQ), _tok(BW), _full((1, AQ)), _full((1, BW)),
                  _tok(LANES), _perm_spec(4, LANES), _perm_spec(16, LANES), _full((LANES, BW)), _full((BW, LANES))],
        out_specs=[_tok(D), _tok(AQ), _tok(BW), _perm_spec(4, BW), _perm_spec(16, BW),
                   _tok(LANES), _tok(LANES), _perm_spec(4, LANES), _perm_spec(16, LANES), ACCB_SPEC, ACCG_SPEC],
        out_shape=[nat(D, bf16), nat(AQ, bf16), nat(BW, bf16), jax.ShapeDtypeStruct((BL, 4, SEQ // 4, BW), bf16),
                   jax.ShapeDtypeStruct((BL, 16, SEQ // 16, BW), bf16), nat(LANES, f32), nat(LANES, f32),
                   jax.ShapeDtypeStruct((BL, 4, SEQ // 4, LANES), f32), jax.ShapeDtypeStruct((BL, 16, SEQ // 16, LANES), f32)]
                  + ACC_SHAPES,
        scratch_shapes=[pltpu.VMEM((BW // LANES, TM, LANES), f32)],
        compiler_params=_cp(("arbitrary", "arbitrary")),
    )(gx1, y, mod, g_post, w_out, oa, ob, g_mix_a, g_mix_b, l1, l4, l16, jnp.asarray(HEAD_EXPAND, bf16),
      jnp.asarray(HEAD_SUM, bf16))


def _attn_in_bwd(dqa, dka, dva, d1, d4, d16, tc, ts1, ts2, w_in, x, gx1, mod, g_pre):
    def body(dqa_ref, dka_ref, dva_ref, dq1_ref, dk1_ref, dv1_ref, dq4_ref, dk4_ref, dv4_ref, dq16_ref, dk16_ref, dv16_ref,
             c_ref, s1_ref, s2_ref, w_ref, x_ref, gx_ref, mod_ref, g_ref, dproj_ref, dx_ref, accb_ref, accg_ref, scr):
        _acc_init(accb_ref, accg_ref)
        c, s1, s2 = c_ref[...], s1_ref[...], s2_ref[...]
        tot = lambda r1, r4, r16: r1[...] + _perm_load(r4, scr, 4) + _perm_load(r16, scr, 16)
        dqb = tot(dq1_ref, dq4_ref, dq16_ref)
        dkb = tot(dk1_ref, dk4_ref, dk16_ref)
        dvb = tot(dv1_ref, dv4_ref, dv16_ref)
        dproj = jnp.concatenate([
            _rope_t(dqa_ref[...], c, s1, s2) * 0.125, _rope_t(dka_ref[...], c, s1, s2), dva_ref[...],
            _rope_t(dqb, c, s1, s2) * 0.125, _rope_t(dkb, c, s1, s2), dvb], axis=1).astype(bf16)
        dproj_ref[...] = dproj
        dh = _dot_nt(dproj, w_ref[...])
        xn, r = _rms(x_ref[...])
        g = g_ref[...]
        dn = dh * (1.0 + mod_ref[1:2, :])
        dx_ref[...] = gx_ref[...] + _rms_bwd(dn * g, xn, r)
        accb_ref[0:1, :] += _colsum(dh * (xn * g))
        accb_ref[1:2, :] += _colsum(dh)
        accg_ref[0:1, :] += _colsum(dn * xn)

    return pl.pallas_call(
        body, name="attn_in_bwd", grid=(BL, NJ),
        in_specs=[_tok(AQ), _tok(AKV), _tok(AKV)] + [_tok(BW)] * 3 + [_perm_spec(4, BW)] * 3 + [_perm_spec(16, BW)] * 3
                 + [_tok(LANES)] * 3 + [_full((D, INW)), _tok(D), _tok(D), MOD_SPEC, _full((1, D))],
        out_specs=[_tok(INW), _tok(D), ACCB_SPEC, ACCG_SPEC],
        out_shape=[jax.ShapeDtypeStruct((BL, SEQ, INW), bf16), jax.ShapeDtypeStruct((BL, SEQ, D), f32)] + ACC_SHAPES,
        scratch_shapes=[pltpu.VMEM((BW // LANES, TM, LANES), f32)],
        compiler_params=_cp(("arbitrary", "arbitrary")),
    )(dqa, dka, dva, *d1, *d4, *d16, tc, ts1, ts2, w_in, x, gx1, mod, g_pre)


def _inv_lane():
    inv = np.float32(THETA) ** (-np.arange(0, ROT, 2, dtype=np.float32) / np.float32(ROT))
    lane = np.arange(LANES) % HD
    return jnp.asarray(np.where(lane < ROT, inv[lane % (ROT // 2)], 0.0).astype(np.float32)[None, :])


def _local_step(x, positions, mod, target, inv_lane, first_weight, later_weights, grad_ready, g_attn_pre,
                g_attn_post, sink_a, g_mix_a, g_mix_b, g_mlp_pre, g_mlp_post):
    tabs = _rope_tables(positions.reshape(BL * SEQ, 1), inv_lane)
    w_in = first_weight(tuple(tabs))
    tc, ts1, ts2 = [t.reshape(BL, SEQ, LANES) for t in tabs]

    (h, qa, ka, va, q1, k1, v1, q4, k4, v4, q16, k16, v16, w_in) = _attn_in(x, mod, g_attn_pre, w_in, tc, ts1, ts2)
    seqs = lambda t: t.reshape(t.shape[0] * t.shape[1], t.shape[2], t.shape[3])
    q4, k4, v4, q16, k16, v16 = [seqs(t) for t in (q4, k4, v4, q16, k16, v16)]
    oa, la = _attn_fwd(qa, ka, va, sink_a, max_dist=BLK - 1, o_dtype=f32, name="attn_a_fwd")
    o1, l1 = _attn_fwd(q1, k1, v1, None, max_dist=BLK, o_dtype=bf16, name="attn_b1_fwd")
    o4, l4 = _attn_fwd(q4, k4, v4, None, max_dist=BLK, o_dtype=bf16, name="attn_b4_fwd")
    o16, l16 = _attn_fwd(q16, k16, v16, None, max_dist=BLK, o_dtype=bf16, name="attn_b16_fwd")
    b4 = lambda t: t.reshape(BL, 4, SEQ // 4, t.shape[-1])
    b16 = lambda t: t.reshape(BL, 16, SEQ // 16, t.shape[-1])
    w_out, mlp_weights, mod = later_weights((oa, o1, o4, o16), mod)
    x1, y, mixed, ob = _mix_out(oa, o1, l1, b4(o4), b4(l4), b16(o16), b16(l16), g_mix_a, g_mix_b, w_out, x, mod, g_attn_post)
    w_up, w_down = mlp_weights((x1,))
    h2, u, a = _mlp_up(x1, mod, g_mlp_pre, w_up)
    gx, dy2, accb_d, accg_d = _mlp_down(a, w_down, x1, target, mod, g_mlp_post)

    flat = lambda t: t.reshape(BL * SEQ, t.shape[-1])
    mod = grad_ready("w_down", _matmul_tn(flat(a), flat(dy2), tn=D, col_blocked=False, name="grad_w_down", out_dtype=bf16), mod)
    du, gx1, accb_m, accg_m = _mlp_bwd(dy2, u, w_down, w_up, x1, gx, mod, g_mlp_pre)
    mod = grad_ready("w_up", _matmul_tn(flat(h2), flat(du), tn=D, col_blocked=True, name="grad_w_up", out_dtype=bf16), mod)

    dy, doa, do1, do4, do16, da, dl1, dl4, dl16, accb_o, accg_o = _attn_out_bwd(
        gx1, y, mod, g_attn_post, w_out, oa, ob, g_mix_a, g_mix_b, l1, b4(l4), b16(l16))
    gw_out = _matmul_tn(flat(mixed), flat(dy), tn=D, col_blocked=False, name="grad_w_out")
    dqa, dka, dva, dsink = _attn_bwd(qa, ka, va, doa, da, la, sink_a, max_dist=BLK - 1, name="attn_a_bwd")
    d1 = _attn_bwd(q1, k1, v1, do1, dl1, l1, None, max_dist=BLK, name="attn_b1_bwd")
    d4 = _attn_bwd(q4, k4, v4, seqs(do4), seqs(dl4), l4, None, max_dist=BLK, name="attn_b4_bwd")
    d16 = _attn_bwd(q16, k16, v16, seqs(do16), seqs(dl16), l16, None, max_dist=BLK, name="attn_b16_bwd")
    dproj, grad_x, accb_i, accg_i = _attn_in_bwd(dqa, dka, dva, d1, [b4(t) for t in d4], [b16(t) for t in d16],
                                                 tc, ts1, ts2, w_in, x, gx1, mod, g_attn_pre)
    gw_in = _grad_w_in(flat(h), flat(dproj))
    dsink = grad_ready("w_in_w_out", (gw_in, gw_out), dsink)

    return grad_x, (accb_i, accb_o, accb_m, accb_d, accg_i, accg_o, accg_m, accg_d, dsink)


ADAW = NMOD * D // NCHIP


def _pos():
    return lax.axis_index("x"), lax.axis_index("y"), lax.axis_index("c")


def _flip(v, bit):
    return 1 - v if bit else v


def _all_peers(x, y, c):
    return [(_flip(x, k >> 2 & 1), _flip(y, k >> 1 & 1), _flip(c, k & 1)) for k in range(1, NDEV)]


def _other_chips(x, y):
    return [(1 - x, y), (x, 1 - y), (1 - x, 1 - y)]


def _rcopy(src, dst, send, recv, k, dev, k_recv=None):
    return pltpu.make_async_remote_copy(src_ref=src, dst_ref=dst, send_sem=send.at[k],
                                        recv_sem=recv.at[k if k_recv is None else k_recv],
                                        device_id=dev, device_id_type=MESH)


def _gather_small(src, buf, send, recv):
    x, y, c = _pos()
    me = 4 * x + 2 * y + c
    peers = _all_peers(x, y, c)
    sends = [_rcopy(src, buf.at[me], send, recv, k, p) for k, p in enumerate(peers)]
    for cp in sends:
        cp.start()
    for k, (px, py, pc) in enumerate(peers):
        _rcopy(src, buf.at[4 * px + 2 * py + pc], send, recv, k, (px, py, pc)).wait_recv()
    for cp in sends:
        cp.wait_send()
    return me


def _ada_fwd(c_in, w_ada, b_cols):
    def body(c_ref, w_ref, b_ref, mod_ref, cond_ref, cbuf, mbuf, s1, r1, s2, r2):
        x, y, c = _pos()
        chip = 2 * x + y
        me = _gather_small(c_ref, cbuf, s1, r1)
        cbuf[me] = c_ref[...]
        for i in range(NDEV):
            cond_ref[BL * i:BL * (i + 1), :] = cbuf[i]
        call = cond_ref[...]
        cond = call / (1.0 + jnp.exp(-call))
        cond_ref[...] = cond
        mbuf[chip] = _dot(cond.astype(bf16), w_ref[...].astype(bf16)) + b_ref[...]
        chips = _other_chips(x, y)
        sends = [_rcopy(mbuf.at[chip], mbuf.at[chip], s2, r2, j, (px, py, c)) for j, (px, py) in enumerate(chips)]
        for cp in sends:
            cp.start()
        for j, (px, py) in enumerate(chips):
            _rcopy(mbuf.at[chip], mbuf.at[2 * px + py], s2, r2, j, (px, py, c)).wait_recv()
        for cp in sends:
            cp.wait_send()
        row = lax.broadcasted_iota(jnp.int32, (BL * NDEV, ADAW), 0)
        for s in range(NCHIP):
            slab = mbuf[s]
            for j in range(BL):
                mod_ref[j:j + 1, ADAW * s:ADAW * (s + 1)] = jnp.sum(jnp.where(row == BL * me + j, slab, 0.0), axis=0, keepdims=True)

    vm = pl.BlockSpec(memory_space=pltpu.VMEM)
    return pl.pallas_call(
        body, name="ada_fwd", in_specs=[vm, vm, vm], out_specs=[vm, vm],
        out_shape=[jax.ShapeDtypeStruct((BL, NMOD * D), f32), jax.ShapeDtypeStruct((BL * NDEV, D), f32)],
        scratch_shapes=[pltpu.VMEM((NDEV, BL, D), f32), pltpu.VMEM((NCHIP, BL * NDEV, ADAW), f32),
                        pltpu.SemaphoreType.DMA((NDEV - 1,)), pltpu.SemaphoreType.DMA((NDEV - 1,)),
                        pltpu.SemaphoreType.DMA((NCHIP - 1,)), pltpu.SemaphoreType.DMA((NCHIP - 1,))],
        compiler_params=pltpu.CompilerParams(vmem_limit_bytes=VMEM_LIMIT),
    )(c_in, w_ada, b_cols)


def _small_allreduce(accs, cond_all):
    def body(bi, bo, bm, bd, gi, go, gm, gd, dsink, cond_ref, gw_ref, gb_ref, small_ref, pay, pbuf, dall, s1, r1):
        x, y, c = _pos()
        chip = 2 * x + y
        pay[...] = jnp.zeros_like(pay)
        for b in range(BL):
            for k, (ref, r) in enumerate(((bi, 1), (bi, 0), (bo, 0), (bm, 1), (bm, 0), (bd, 0))):
                pay[b:b + 1, D * k:D * (k + 1)] = ref[b, r:r + 1, :]
        for off, ref, r in ((OFF_G_ATTN_PRE, gi, 0), (OFF_G_ATTN_POST, go, 0), (OFF_G_MIX_A, go, 1), (OFF_G_MLP_PRE, gm, 0),
                            (OFF_G_MLP_POST, gd, 0)):
            pay[BL:BL + 1, off:off + D] = ref[r:r + 1, :]
        eye = lax.broadcasted_iota(jnp.int32, (8, LANES), 0) == lax.broadcasted_iota(jnp.int32, (8, LANES), 1)
        pay[BL:BL + 1, OFF_SINK:OFF_SINK + LANES] = jnp.sum(jnp.where(eye, dsink[...], 0.0), axis=0, keepdims=True)
        pay[BL:BL + 1, OFF_LOSS:OFF_LOSS + LANES] = gd[1:2, 0:LANES]
        me = _gather_small(pay, pbuf, s1, r1)
        pbuf[me] = pay[...]
        small = pbuf[0, BL:BL + 1, :]
        for i in range(1, NDEV):
            small = small + pbuf[i, BL:BL + 1, :]
        small_ref[...] = small
        for i in range(NDEV):
            dall[BL * i:BL * (i + 1), :] = pbuf[i, 0:BL, :]
        gb_ref[...] = jnp.sum(dall[...], axis=0, keepdims=True)
        cols = jnp.zeros((BL * NDEV, ADAW), f32)
        for s in range(NCHIP):
            cols = cols + jnp.where(chip == s, dall[:, ADAW * s:ADAW * (s + 1)], 0.0)
        gw_ref[...] = lax.dot_general(cond_ref[...], cols, (((0,), (0,)), ((), ())), preferred_element_type=f32,
                                      precision=lax.Precision.HIGHEST)

    vm = pl.BlockSpec(memory_space=pltpu.VMEM)
    return pl.pallas_call(
        body, name="small_allreduce", in_specs=[vm] * 10, out_specs=[vm] * 3,
        out_shape=[jax.ShapeDtypeStruct((D, ADAW), f32), jax.ShapeDtypeStruct((1, PAYW), f32), jax.ShapeDtypeStruct((1, PAYW), f32)],
        scratch_shapes=[pltpu.VMEM((4, PAYW), f32), pltpu.VMEM((NDEV, 4, PAYW), f32), pltpu.VMEM((BL * NDEV, PAYW), f32),
                        pltpu.SemaphoreType.DMA((NDEV - 1,)), pltpu.SemaphoreType.DMA((NDEV - 1,))],
        compiler_params=pltpu.CompilerParams(vmem_limit_bytes=VMEM_LIMIT),
    )(*accs, cond_all)


def _half(ref, c):
    r2 = ref.shape[0] // 2
    return ref.at[pl.ds(c * r2 if isinstance(c, int) else pl.multiple_of(c * r2, 16), r2), :]


HBM_SPEC = pl.BlockSpec(memory_space=pltpu.HBM)
SEM_SPEC = pl.BlockSpec(memory_space=pltpu.SEMAPHORE)
EFFECT = pltpu.SideEffectType.DATAFLOW_SIDE_EFFECTING
NLINK = NCHIP - 1


def _in_hbm(a):
    return pltpu.with_memory_space_constraint(a, pltpu.HBM)


NSEM = 8


def _split_start(name, srcs, land_shapes, builds, carry, after=(), lands=None):
    n = len(srcs)
    na, nc = len(after), len(carry)

    def body(*refs):
        src, land = refs[:n], refs[n:2 * n]
        kept = refs[2 * n + na:2 * n + na + nc]
        outs = refs[2 * n + na + nc:]
        send, recv, passed = outs[:n], outs[n:2 * n], outs[4 * n:]
        for t in range(n):
            for out_cp, _ in builds[t](src[t], land[t], send[t], recv[t]):
                out_cp.start()
        for a, b in zip(kept, passed):
            b[...] = a[...]

    if lands is None:
        lands = [lax.empty(s.shape, s.dtype) for s in land_shapes]
    lands = [_in_hbm(a) for a in lands]
    sems = [pltpu.SemaphoreType.DMA((NSEM,))] * (2 * n)
    thru = [pltpu.HBM(a.shape, a.dtype) for a in list(srcs) + lands]
    vm = pl.BlockSpec(memory_space=pltpu.VMEM)
    res = pl.pallas_call(
        body, name=name, out_shape=sems + thru + [jax.ShapeDtypeStruct(a.shape, a.dtype) for a in carry],
        in_specs=[HBM_SPEC] * (2 * n) + [pl.BlockSpec(memory_space=pl.ANY)] * na + [vm] * nc,
        out_specs=[SEM_SPEC] * (2 * n) + [HBM_SPEC] * (2 * n) + [vm] * nc,
        input_output_aliases={i: 2 * n + i for i in range(2 * n)},
        compiler_params=pltpu.CompilerParams(has_side_effects=EFFECT),
    )(*[_in_hbm(a) for a in srcs], *lands, *after, *carry)
    flight = [(res[2 * n + t], res[3 * n + t], res[t], res[n + t]) for t in range(n)]
    return flight, list(res[4 * n:])


def _split_wait(name, flight, builds, after):
    m = len(flight)
    na = len(after)

    def body(*refs):
        src, land, send, recv = refs[:m], refs[m:2 * m], refs[2 * m:3 * m], refs[3 * m:4 * m]
        for t in range(m):
            for out_cp, in_cp in builds[t](src[t], land[t], send[t], recv[t]):
                out_cp.wait_send()
                in_cp.wait_recv()

    ops = [f[0] for f in flight] + [f[1] for f in flight] + [f[2] for f in flight] + [f[3] for f in flight]
    res = pl.pallas_call(
        body, name=name, out_shape=[pltpu.HBM(a.shape, a.dtype) for a in ops[:2 * m]],
        in_specs=[HBM_SPEC] * (2 * m) + [SEM_SPEC] * (2 * m) + [pl.BlockSpec(memory_space=pl.ANY)] * na,
        out_specs=[HBM_SPEC] * (2 * m), input_output_aliases={i: i for i in range(2 * m)},
        compiler_params=pltpu.CompilerParams(has_side_effects=EFFECT),
    )(*ops, *after)
    return res[:m], res[m:2 * m]


def _weight_copies(src, land, send, recv):
    x, y, c = _pos()
    chip = 2 * x + y
    return [(_rcopy(_half(src, c), _half(land.at[chip], c), send, recv, j, (px, py, c)),
             _rcopy(_half(src, c), _half(land.at[2 * px + py], c), send, recv, j, (px, py, c)))
            for j, (px, py) in enumerate(_other_chips(x, y))]


def _grad_copies(src, land, send, recv):
    x, y, c = _pos()
    return [(_rcopy(src.at[2 * px + py], land.at[j], send, recv, j, (px, py, c)),
             _rcopy(src.at[2 * px + py], land.at[j], send, recv, j, (px, py, c)))
            for j, (px, py) in enumerate(_other_chips(x, y))]


NDIRECT = NDEV - 1


def _direct_grad_copies(src, land, send, recv):
    x, y, c = _pos()
    out, arrive = [], []
    for j, (px, py) in enumerate(_other_chips(x, y)):
        for hc in range(2):
            out.append(_rcopy(_half(src.at[2 * px + py], hc), land.at[2 * j + c], send, recv, 2 * j + hc, (px, py, hc),
                              k_recv=2 * j + c))
            arrive.append(_rcopy(_half(src.at[2 * px + py], hc), land.at[2 * j + hc], send, recv, 2 * j + hc, (px, py, hc)))
    own = _rcopy(_half(src.at[2 * x + y], 1 - c), land.at[NDIRECT - 1], send, recv, NDIRECT - 1, (x, y, 1 - c))
    return list(zip(out, arrive)) + [(own, own)]


def _pair_grad_copies(src, land, send, recv):
    x, y, c = _pos()
    r2 = src.shape[1] // 2
    cp = _rcopy(src.at[:, pl.ds(pl.multiple_of((1 - c) * r2, 8), r2), :], land, send, recv, 0, (x, y, 1 - c))
    return [(cp, cp)]


def _pair_weight_copies(src, land, send, recv):
    x, y, c = _pos()
    sib = (x, y, 1 - c)
    cps = []
    for j, (px, py) in enumerate(_other_chips(x, y)):
        mine, theirs = _half(land.at[2 * px + py], c), _half(land.at[2 * px + py], 1 - c)
        cps.append((_rcopy(mine, mine, send, recv, j, sib), _rcopy(theirs, theirs, send, recv, j, sib)))
    own = _rcopy(src, land.at[2 * x + y], send, recv, NLINK, sib)
    return cps + [(own, own)]


RS_ROWS = 128


def _pair_add(g, landed, c_arr, name):
    _, r2, cw = landed.shape
    nr = r2 // RS_ROWS

    def body(c_ref, g_ref, p_ref, o_ref):
        o_ref[...] = (g_ref[...] + p_ref[...]).astype(bf16)

    gs = pltpu.PrefetchScalarGridSpec(
        num_scalar_prefetch=1, grid=(NCHIP, nr),
        in_specs=[pl.BlockSpec((None, RS_ROWS, cw), lambda s, j, c: (s, c[0] * nr + j, 0)),
                  pl.BlockSpec((None, RS_ROWS, cw), lambda s, j, c: (s, j, 0))],
        out_specs=pl.BlockSpec((None, RS_ROWS, cw), lambda s, j, c: (s, j, 0)))
    return pl.pallas_call(body, name=name, grid_spec=gs, out_shape=jax.ShapeDtypeStruct((NCHIP, r2, cw), bf16),
                          compiler_params=_cp(("arbitrary", "arbitrary")))(c_arr, g, landed)


def _chip_add(own, landed, pos_arr, name):
    nl, r2, cw = landed.shape
    nr = r2 // RS_ROWS
    whole = own.shape[1] == 2 * r2

    def body(s_ref, h_ref, q_ref, o_ref):
        acc = h_ref[...].astype(f32)
        for j in range(nl):
            acc = acc + q_ref[j].astype(f32)
        o_ref[...] = acc

    gs = pltpu.PrefetchScalarGridSpec(
        num_scalar_prefetch=1, grid=(nr,),
        in_specs=[pl.BlockSpec((None, RS_ROWS, cw), lambda j, s: (s[0], (s[1] * nr if whole else 0) + j, 0)),
                  pl.BlockSpec((nl, RS_ROWS, cw), lambda j, s: (0, j, 0))],
        out_specs=pl.BlockSpec((RS_ROWS, cw), lambda j, s: (s[1] * nr + j, 0)))
    return pl.pallas_call(body, name=name, grid_spec=gs, out_shape=jax.ShapeDtypeStruct((2 * r2, cw), f32),
                          compiler_params=_cp(("arbitrary",)))(pos_arr, own, landed)


def _pair_gather_copies(src, land, send, recv):
    x, y, c = _pos()
    sib = (x, y, 1 - c)
    return [(_rcopy(_half(land, c), _half(land, c), send, recv, 0, sib),
             _rcopy(_half(land, 1 - c), _half(land, 1 - c), send, recv, 0, sib))]


def _adamw_math(w, g, m, v):
    m = B1 * m + (1.0 - B1) * g
    v = B2 * v + (1.0 - B2) * jnp.square(g)
    m_hat = m / (1.0 - B1 ** STEP)
    v_hat = v / (1.0 - B2 ** STEP)
    return -LR * (m_hat / (jnp.sqrt(v_hat) + AEPS) + WD * w), m, v


ADAM_ROWS = 256


def _adamw(w, g, m, v, name):
    r, cw = w.shape

    def body(w_ref, g_ref, m_ref, v_ref, go_ref, d_ref, mo_ref, vo_ref):
        g = g_ref[...]
        go_ref[...] = g
        d_ref[...], mo_ref[...], vo_ref[...] = _adamw_math(w_ref[...], g, m_ref[...], v_ref[...])

    rows = max(k for k in range(8, ADAM_ROWS + 1, 8) if r % k == 0)
    spec = pl.BlockSpec((rows, cw), lambda i: (i, 0))
    return pl.pallas_call(body, name=name, grid=(r // rows,), in_specs=[spec] * 4, out_specs=[spec] * 4,
                          out_shape=[jax.ShapeDtypeStruct((r, cw), f32)] * 4, compiler_params=_cp(("arbitrary",)))(w, g, m, v)


SMALL = (("b_ada", None, PAYW), ("g_attn_pre", OFF_G_ATTN_PRE, D), ("g_attn_post", OFF_G_ATTN_POST, D), ("sink_a", OFF_SINK, 8),
         ("g_mix_a", OFF_G_MIX_A, AQ), ("g_mix_b", OFF_G_MIX_B, BW), ("g_mlp_pre", OFF_G_MLP_PRE, D), ("g_mlp_post", OFF_G_MLP_POST, D))


def _adamw_small(small, gb, params):
    n = len(SMALL)

    def body(*refs):
        small_ref, gb_ref = refs[:2]
        wmv = refs[2:2 + 3 * n]
        loss_ref = refs[2 + 3 * n]
        outs = refs[3 + 3 * n:]
        loss_ref[...] = small_ref[:, OFF_LOSS:OFF_LOSS + 1] * (0.5 / D)
        for i, (_, off, width) in enumerate(SMALL):
            g = gb_ref[...] if off is None else small_ref[:, off:off + width]
            w_ref, m_ref, v_ref = wmv[3 * i:3 * i + 3]
            outs[4 * i][...] = g
            outs[4 * i + 1][...], outs[4 * i + 2][...], outs[4 * i + 3][...] = _adamw_math(w_ref[...], g, m_ref[...], v_ref[...])

    vm = pl.BlockSpec(memory_space=pltpu.VMEM)
    out_shape = [jax.ShapeDtypeStruct((1, 1), f32)]
    for _, _, width in SMALL:
        out_shape += [jax.ShapeDtypeStruct((1, width), f32)] * 4
    flat = [a for wmv in params for a in wmv]
    res = pl.pallas_call(body, name="adamw_small", in_specs=[vm] * (2 + 3 * n), out_specs=[vm] * len(out_shape),
                         out_shape=out_shape)(small, gb, *flat)
    return res[0], {name: res[1 + 4 * i:5 + 4 * i] for i, (name, _, _) in enumerate(SMALL)}


def kernel(x, c, positions, w_ada, b_ada, g_attn_pre, g_attn_post, w_in, sink_a, g_mix_a, g_mix_b, w_out, g_mlp_pre, g_mlp_post, w_up, w_down, loss_target, m_w_ada, m_b_ada, m_g_attn_pre, m_g_attn_post, m_w_in, m_sink_a, m_g_mix_a, m_g_mix_b, m_w_out, m_g_mlp_pre, m_g_mlp_post, m_w_up, m_w_down, v_w_ada, v_b_ada, v_g_attn_pre, v_g_attn_post, v_w_in, v_sink_a, v_g_mix_a, v_g_mix_b, v_w_out, v_g_mlp_pre, v_g_mlp_post, v_w_up, v_w_down):
    given = dict(w_ada=w_ada, b_ada=b_ada, g_attn_pre=g_attn_pre, g_attn_post=g_attn_post, w_in=w_in, sink_a=sink_a, g_mix_a=g_mix_a,
                 g_mix_b=g_mix_b, w_out=w_out, g_mlp_pre=g_mlp_pre, g_mlp_post=g_mlp_post, w_up=w_up, w_down=w_down)
    moms = dict(w_ada=(m_w_ada, v_w_ada), b_ada=(m_b_ada, v_b_ada), g_attn_pre=(m_g_attn_pre, v_g_attn_pre),
                g_attn_post=(m_g_attn_post, v_g_attn_post), w_in=(m_w_in, v_w_in), sink_a=(m_sink_a, v_sink_a),
                g_mix_a=(m_g_mix_a, v_g_mix_a), g_mix_b=(m_g_mix_b, v_g_mix_b), w_out=(m_w_out, v_w_out),
                g_mlp_pre=(m_g_mlp_pre, v_g_mlp_pre), g_mlp_post=(m_g_mlp_post, v_g_mlp_post), w_up=(m_w_up, v_w_up),
                w_down=(m_w_down, v_w_down))
    order = ["w_ada", "b_ada", "g_attn_pre", "g_attn_post", "w_in", "sink_a", "g_mix_a", "g_mix_b", "w_out", "g_mlp_pre",
             "g_mlp_post", "w_up", "w_down"]
    xi, yi, ci = _pos()
    chip = 2 * xi + yi

    c_arr = jnp.reshape(ci, (1,)).astype(jnp.int32)
    pos_arr = jnp.stack([chip, ci]).astype(jnp.int32)
    big = ("w_in", "w_out", "w_up", "w_down")

    b_cols = lax.dynamic_slice(b_ada, (0, chip * ADAW), (1, ADAW))
    mod, cond_all = _ada_fwd(c, w_ada[0], b_cols)
    gathered = [jax.ShapeDtypeStruct((NCHIP,) + given[n].shape[1:], bf16) for n in big]
    flight_in, (mod,) = _split_start("weights_start_first", [w_in[0].astype(bf16)], gathered[:1], [_weight_copies], [mod])
    mod, rest = lax.optimization_barrier((mod, [given[n][0] for n in big[1:]]))
    flight_rest, (mod, inv_lane) = _split_start("weights_start_rest", [w.astype(bf16) for w in rest], gathered[1:],
                                                [_weight_copies] * 3, [mod, _inv_lane()])
    mod = mod.reshape(BL, NMOD, D)

    def first_weight(after):
        srcs, lands = _split_wait("weights_wait_first", flight_in, [_weight_copies], after)
        cross, _ = _split_start("weights_pair_start_first", srcs, None, [_pair_weight_copies], [], lands=lands)
        _, (win_g,) = _split_wait("weights_pair_wait_first", cross, [_pair_weight_copies], ())
        return win_g

    def later_weights(after, carry):
        srcs, lands = _split_wait("weights_wait_rest", flight_rest, [_weight_copies] * 3, after)
        fl, (carry,) = _split_start("weights_pair_start_rest", srcs, None, [_pair_weight_copies] * 3, [carry], lands=lands)
        _, (wout_g,) = _split_wait("weights_pair_wait_out", fl[:1], [_pair_weight_copies], ())

        def mlp_weights(after):
            _, (wup_g, wdn_g) = _split_wait("weights_pair_wait_mlp", fl[1:], [_pair_weight_copies] * 2, after)
            return wup_g, wdn_g.reshape(DFF, D)

        return wout_g.reshape(D, D), mlp_weights, carry

    crossing, pending = {}, {}

    def grad_ready(group, g, carry):
        if group != "w_in_w_out":
            slab = g.reshape(NCHIP, DFF // NCHIP, D) if group == "w_down" else g
            land = jax.ShapeDtypeStruct((NDIRECT, slab.shape[1] // 2, slab.shape[2]), bf16)
            fl, (carry,) = _split_start("grad_start_" + group, [slab], [land], [_direct_grad_copies], [carry])
            pending[group] = ((group,), fl, [_direct_grad_copies])
            return carry
        names = ("w_in", "w_out")
        slabs = [g[0], g[1].reshape(NCHIP, D // NCHIP, D)]
        fl, (carry,) = _split_start("grad_pair_start_" + group, slabs,
                                    [jax.ShapeDtypeStruct((NCHIP, s.shape[1] // 2, s.shape[2]), f32) for s in slabs],
                                    [_pair_grad_copies] * len(names), [carry])
        crossing[group] = (names, fl)
        return carry

    def grad_reduce(group, after, carry):
        names, fl = crossing[group]
        slabs, landed = _split_wait("grad_pair_wait_" + group, fl, [_pair_grad_copies] * len(names), after)
        halves = [_pair_add(s, p, c_arr, "grad_pair_sum_" + n) for s, p, n in zip(slabs, landed, names)]
        fl, (carry,) = _split_start("grad_start_" + group, halves,
                                    [jax.ShapeDtypeStruct((NLINK,) + h.shape[1:], bf16) for h in halves],
                                    [_grad_copies] * len(names), [carry])
        pending[group] = (names, fl, [_grad_copies] * len(names))
        return carry

    grad_x, accs = _local_step(x, positions, mod, loss_target, inv_lane, first_weight, later_weights, grad_ready,
                               g_attn_pre, g_attn_post, sink_a, g_mix_a, g_mix_b, g_mlp_pre, g_mlp_post)

    grads, out = {}, {}

    def update(n):
        tr = (lambda a: a.T) if n == "w_in" else (lambda a: a)
        res = _adamw(tr(given[n][0]), tr(grads[n]), tr(moms[n][0][0]), tr(moms[n][1][0]), "adamw_" + n)
        out[n] = tuple(tr(a)[None] for a in res)
        return res[3]

    def finish(groups, after):
        names = sum((pending[g][0] for g in groups), ())
        fl = sum((pending[g][1] for g in groups), [])
        halves, landed = _split_wait("grad_wait_" + groups[0], fl, sum((pending[g][2] for g in groups), []), after)
        flights = []
        for h, q, n in zip(halves, landed, names):
            full = _chip_add(h, q, pos_arr, "grad_chip_sum_" + n)
            flights.append(_split_start("grad_gather_start_" + n, [jnp.zeros((8, LANES), f32)], None, [_pair_gather_copies],
                                        [], lands=[full])[0])
        last = None
        for n, fl1 in zip(names, flights):
            after = (flights[-1][0][0],) if last is None and fl1 is not flights[-1] else () if last is None else (last,)
            _, (grads[n],) = _split_wait("grad_gather_wait_" + n, fl1, [_pair_gather_copies], after)
            last = update(n)
        return last

    grads["w_ada"], gb, small = _small_allreduce(accs, cond_all)
    small = grad_reduce("w_in_w_out", (small,), small)
    last = finish(("w_down", "w_up"), (small,))
    finish(("w_in_w_out",), (last, update("w_ada")))
    loss, res = _adamw_small(small, gb, [(given[n], moms[n][0], moms[n][1]) for n, _, _ in SMALL])
    for n, _, _ in SMALL:
        out[n] = tuple(res[n])
    return (loss.reshape(()), grad_x, *[out[n][0] for n in order], *[out[n][1] for n in order],
            *[out[n][2] for n in order], *[out[n][3] for n in order])
```

```python
import functools

import numpy as np
import jax
import jax.numpy as jnp
from jax import lax
from jax.experimental import pallas as pl
from jax.experimental.pallas import tpu as pltpu

f32 = jnp.float32
bf16 = jnp.bfloat16
MESH = pl.DeviceIdType.MESH

D = 1024
SEQ = 2048
BL = 2
HD = 64
AQ = 512
AKV = 128
BW = 512
INW = 2304
DFF = 4096
NMOD = 6
ROT = 16
THETA = 500000.0
EPS = 1e-6
NEG = -1e30
BLK = 128
TM = 512
NJ = SEQ // TM
LANES = 128
NCHIP = 4
NDEV = 8
VMEM_LIMIT = 56 << 20

LR, B1, B2, AEPS, WD, STEP = 0.001, 0.9, 0.999, 1e-08, 0.01, 10

OFF_G_ATTN_PRE, OFF_G_ATTN_POST, OFF_G_MIX_A, OFF_G_MIX_B = 0, 1024, 2048, 2560
OFF_G_MLP_PRE, OFF_G_MLP_POST, OFF_SINK, OFF_LOSS = 3072, 4096, 5120, 5248
PAYW = NMOD * D


def _cp(sem=None):
    return pltpu.CompilerParams(dimension_semantics=sem, vmem_limit_bytes=VMEM_LIMIT)


def _dot(a, b):
    return jnp.dot(a, b, preferred_element_type=f32)


def _dot_nt(a, b):
    return lax.dot_general(a, b, (((1,), (1,)), ((), ())), preferred_element_type=f32)


def _dot_tn(a, b):
    return lax.dot_general(a, b, (((0,), (0,)), ((), ())), preferred_element_type=f32)


def _rms(x):
    r = lax.rsqrt(jnp.mean(x * x, axis=-1, keepdims=True) + EPS)
    return x * r, r


def _rms_bwd(dy, y, r):
    return r * (dy - y * jnp.mean(dy * y, axis=-1, keepdims=True))


def _colsum(v):
    return jnp.sum(v, axis=0, keepdims=True)


def _rope(p, c, s1, s2):
    outs = []
    for c0 in range(0, p.shape[1], LANES):
        pc = p[:, c0:c0 + LANES]
        outs.append(pc * c + pltpu.roll(pc, LANES - ROT // 2, 1) * s1 + pltpu.roll(pc, ROT // 2, 1) * s2)
    return outs[0] if len(outs) == 1 else jnp.concatenate(outs, axis=1)


def _rope_t(g, c, s1, s2):
    outs = []
    for c0 in range(0, g.shape[1], LANES):
        gc = g[:, c0:c0 + LANES]
        outs.append(gc * c + pltpu.roll(gc * s1, ROT // 2, 1) + pltpu.roll(gc * s2, LANES - ROT // 2, 1))
    return outs[0] if len(outs) == 1 else jnp.concatenate(outs, axis=1)


def _perm_store(val, scr, out_ref, d):
    nc = val.shape[1] // LANES
    for c in range(nc):
        scr[c] = val[:, LANES * c:LANES * (c + 1)]
    for c in range(nc):
        for r in range(d):
            out_ref[r, :, LANES * c:LANES * (c + 1)] = scr[c, pl.ds(r, TM // d, stride=d), :].astype(out_ref.dtype)


def _perm_load(in_ref, scr, d):
    nc = in_ref.shape[-1] // LANES
    for c in range(nc):
        for r in range(d):
            scr[c, pl.ds(r, TM // d, stride=d), :] = in_ref[r, :, LANES * c:LANES * (c + 1)].astype(f32)
    return jnp.concatenate([scr[c] for c in range(nc)], axis=1)


def _per_query_head(kv):
    r = pltpu.roll(kv, HD, 1)
    lo = lax.broadcasted_iota(jnp.int32, kv.shape, 1) < HD
    first, second = jnp.where(lo, kv, r), jnp.where(lo, r, kv)
    return jnp.concatenate([first, first, second, second], axis=1)


def _per_kv_head(g):
    g0, g1 = g[:, :LANES] + g[:, LANES:2 * LANES], g[:, 2 * LANES:3 * LANES] + g[:, 3 * LANES:]
    lo = lax.broadcasted_iota(jnp.int32, g0.shape, 1) < HD
    return jnp.where(lo, g0 + pltpu.roll(g0, HD, 1), g1 + pltpu.roll(g1, HD, 1))


def _tok(w, dtype=None):
    return pl.BlockSpec((None, TM, w), lambda b, j: (b, j, 0))


def _perm_spec(d, w):
    return pl.BlockSpec((None, d, TM // d, w), lambda b, j: (b, 0, j, 0))


def _full(shape):
    n = len(shape)
    return pl.BlockSpec(shape, lambda b, j: (0,) * n)


MOD_SPEC = pl.BlockSpec((None, NMOD, D), lambda b, j: (b, 0, 0))
ACCB_SPEC = pl.BlockSpec((None, 8, D), lambda b, j: (b, 0, 0))
ACCG_SPEC = pl.BlockSpec((8, D), lambda b, j: (0, 0))
ACC_SHAPES = [jax.ShapeDtypeStruct((BL, 8, D), f32), jax.ShapeDtypeStruct((8, D), f32)]


def _acc_init(accb_ref, accg_ref):
    b, j = pl.program_id(0), pl.program_id(1)

    @pl.when(j == 0)
    def _():
        accb_ref[...] = jnp.zeros_like(accb_ref)

    @pl.when((b == 0) & (j == 0))
    def _():
        accg_ref[...] = jnp.zeros_like(accg_ref)


def _rope_tables(pos_col, inv_lane):
    def body(p_ref, inv_ref, c_ref, s1_ref, s2_ref):
        ang = p_ref[...].astype(f32) * inv_ref[...]
        j = lax.broadcasted_iota(jnp.int32, (TM, LANES), 1) % HD
        cs, sn = jnp.cos(ang), jnp.sin(ang)
        c_ref[...] = jnp.where(j < ROT, cs, 1.0)
        s1_ref[...] = jnp.where(j < ROT // 2, -sn, 0.0)
        s2_ref[...] = jnp.where((j >= ROT // 2) & (j < ROT), sn, 0.0)

    n = BL * SEQ // TM
    return pl.pallas_call(
        body, name="rope_tables", grid=(n,),
        in_specs=[pl.BlockSpec((TM, 1), lambda i: (i, 0)), pl.BlockSpec((1, LANES), lambda i: (0, 0))],
        out_specs=[pl.BlockSpec((TM, LANES), lambda i: (i, 0))] * 3,
        out_shape=[jax.ShapeDtypeStruct((BL * SEQ, LANES), f32)] * 3,
    )(pos_col, inv_lane)


def _attn_in(x, mod, g_pre, w_in, tc, ts1, ts2):
    def body(x_ref, mod_ref, g_ref, wg_ref, c_ref, s1_ref, s2_ref,
             h_ref, qa_ref, ka_ref, va_ref, q1_ref, k1_ref, v1_ref, q4_ref, k4_ref, v4_ref, q16_ref, k16_ref, v16_ref,
             w_ref, scr):
        @pl.when((pl.program_id(0) == 0) & (pl.program_id(1) == 0))
        def _():
            w_ref[...] = jnp.concatenate([wg_ref[s] for s in range(NCHIP)], axis=1)

        xn, _ = _rms(x_ref[...])
        h = (xn * g_ref[...]) * (1.0 + mod_ref[1:2, :]) + mod_ref[0:1, :]
        hb = h.astype(bf16)
        h_ref[...] = hb
        proj = _dot(hb, w_ref[...])
        c, s1, s2 = c_ref[...], s1_ref[...], s2_ref[...]
        o1, o2, o3, o4, o5 = AQ, AQ + AKV, AQ + 2 * AKV, AQ + 2 * AKV + BW, AQ + 2 * AKV + 2 * BW
        qa_ref[...] = (_rope(proj[:, :o1], c, s1, s2) * 0.125).astype(bf16)
        ka_ref[...] = _per_query_head(_rope(proj[:, o1:o2], c, s1, s2)).astype(bf16)
        va_ref[...] = _per_query_head(proj[:, o2:o3]).astype(bf16)
        qb = _rope(proj[:, o3:o4], c, s1, s2) * 0.125
        kb = _rope(proj[:, o4:o5], c, s1, s2)
        vb = proj[:, o5:]
        for val, r1, r4, r16 in ((qb, q1_ref, q4_ref, q16_ref), (kb, k1_ref, k4_ref, k16_ref), (vb, v1_ref, v4_ref, v16_ref)):
            r1[...] = val.astype(bf16)
            _perm_store(val, scr, r4, 4)
            _perm_store(val, scr, r16, 16)

    nat = lambda w: jax.ShapeDtypeStruct((BL, SEQ, w), bf16)
    p4 = jax.ShapeDtypeStruct((BL, 4, SEQ // 4, BW), bf16)
    p16 = jax.ShapeDtypeStruct((BL, 16, SEQ // 16, BW), bf16)
    return pl.pallas_call(
        body, name="attn_in", grid=(BL, NJ),
        in_specs=[_tok(D), MOD_SPEC, _full((1, D)), _full((NCHIP, D, INW // NCHIP)), _tok(LANES), _tok(LANES), _tok(LANES)],
        out_specs=([_tok(D), _tok(AQ), _tok(AQ), _tok(AQ)] + [_tok(BW)] * 3 + [_perm_spec(4, BW)] * 3 + [_perm_spec(16, BW)] * 3
                   + [_full((D, INW))]),
        out_shape=[nat(D), nat(AQ), nat(AQ), nat(AQ)] + [nat(BW)] * 3 + [p4] * 3 + [p16] * 3
                  + [jax.ShapeDtypeStruct((D, INW), bf16)],
        scratch_shapes=[pltpu.VMEM((BW // LANES, TM, LANES), f32)],
        compiler_params=_cp(("arbitrary", "arbitrary")),
    )(x, mod, g_pre, w_in, tc, ts1, ts2)


def _kv_cat(cur_ref, prev_ref, p, cache):
    key = (id(cur_ref), p)
    if key not in cache:
        sl = slice(LANES * p, LANES * (p + 1))
        cache[key] = cur_ref[:, sl] if prev_ref is None else jnp.concatenate([prev_ref[:, sl], cur_ref[:, sl]], axis=0)
    return cache[key]


def _lane_half(a, hh):
    lo = lax.broadcasted_iota(jnp.int32, a.shape, 1) < HD
    return jnp.where(lo, a, jnp.zeros_like(a)) if hh == 0 else jnp.where(lo, jnp.zeros_like(a), a)


ATT_UNITS = 4


def _attn_specs(n, nb, descending):
    u = ATT_UNITS
    if nb == 1:
        return (lambda ww: pl.BlockSpec((u, BLK, ww), lambda a, i: (a, 0, 0))), None, (n // u, 1)
    steps = nb // u
    at = (lambda i: steps - 1 - i) if descending else (lambda i: i)
    cur = lambda ww: pl.BlockSpec((None, u * BLK, ww), lambda a, i: (a, at(i), 0))
    prev = lambda ww: pl.BlockSpec((None, BLK, ww), lambda a, i: (a, jnp.maximum(u * at(i) - 1, 0), 0))
    return cur, prev, (n, steps)


def _attn_fwd(q, k, v, sink, *, max_dist, o_dtype, name):
    n, l, w = q.shape
    wk = k.shape[-1]
    nb = l // BLK
    has_sink = sink is not None

    def body(*refs):
        sink_ref = None
        if has_sink:
            sink_ref, refs = refs[0], refs[1:]
        if nb > 1:
            q_ref, kc_ref, kp_ref, vc_ref, vp_ref, o_ref, lse_ref = refs[:7]
            first = pl.program_id(1) == 0
            for u in range(ATT_UNITS):
                rows, before = pl.ds(BLK * u, BLK), pl.ds(BLK * (u - 1), BLK)
                unit(q_ref.at[rows, :], kc_ref.at[rows, :], kp_ref if u == 0 else kc_ref.at[before, :],
                     vc_ref.at[rows, :], vp_ref if u == 0 else vc_ref.at[before, :], o_ref.at[rows, :], lse_ref.at[rows, :],
                     jnp.logical_not(first) if u == 0 else True, sink_ref, *refs[7:])
        else:
            q_ref, kc_ref, vc_ref, o_ref, lse_ref = refs[:5]
            for u in range(ATT_UNITS):
                unit(q_ref.at[u], kc_ref.at[u], None, vc_ref.at[u], None, o_ref.at[u], lse_ref.at[u], None, sink_ref, *refs[5:])

    def unit(q_ref, kc_ref, kp_ref, vc_ref, vp_ref, o_ref, lse_ref, has_prev, sink_ref, sscr, pscr, dscr):
        qi = lax.broadcasted_iota(jnp.int32, (BLK, BLK), 0)
        kj = lax.broadcasted_iota(jnp.int32, (BLK, BLK), 1)
        tri = kj <= qi
        eye = kj == qi
        cache = {}
        for p in range(w // LANES):
            qpair = q_ref[:, LANES * p:LANES * (p + 1)]
            kcat = _kv_cat(kc_ref, kp_ref, p, cache)
            for hh in range(2):
                s = _dot_nt(_lane_half(qpair, hh), kcat)
                if nb > 1:
                    sp = s[:, :BLK] if has_prev is True else jnp.where(has_prev, s[:, :BLK], NEG)
                    sscr[2 * p + hh] = jnp.where(tri, s[:, BLK:], sp)
                    if diag:
                        dscr[2 * p + hh] = jnp.where(eye, sp, NEG)
                else:
                    sscr[2 * p + hh] = jnp.where(tri, s, NEG)
        lane = lax.broadcasted_iota(jnp.int32, (BLK, LANES), 1)
        lse_all = jnp.zeros((BLK, LANES), f32)
        for p in range(w // LANES):
            for hh in range(2):
                h = 2 * p + hh
                comb = sscr[h]
                if diag:
                    dtile = dscr[h]
                    m = jnp.max(jnp.maximum(comb, dtile), axis=-1, keepdims=True)
                else:
                    m = jnp.max(comb, axis=-1, keepdims=True)
                if has_sink:
                    sk = sink_ref[0, h]
                    m = jnp.maximum(m, sk)
                e = jnp.exp(comb - m)
                if diag:
                    ed = jnp.exp(dtile - m)
                    den = jnp.sum(e + ed, axis=-1, keepdims=True)
                else:
                    den = jnp.sum(e, axis=-1, keepdims=True)
                if has_sink:
                    den = den + jnp.exp(sk - m)
                inv = 1.0 / den
                if nb > 1:
                    pscr[h, :, :BLK] = (jnp.where(tri, ed if diag else 0.0, e) * inv).astype(bf16)
                    pscr[h, :, BLK:] = (jnp.where(tri, e, 0.0) * inv).astype(bf16)
                else:
                    pscr[h] = (e * inv).astype(bf16)
                lse_all = jnp.where(lane == h, jnp.broadcast_to(m + jnp.log(den), (BLK, LANES)), lse_all)
        lse_ref[...] = lse_all
        for p in range(w // LANES):
            vcat = _kv_cat(vc_ref, vp_ref, p, cache)
            o_ref[:, LANES * p:LANES * (p + 1)] = (_dot(pscr[2 * p], _lane_half(vcat, 0))
                                                   + _dot(pscr[2 * p + 1], _lane_half(vcat, 1))).astype(o_ref.dtype)

    assert max_dist in (BLK - 1, BLK) and k.shape == q.shape
    diag = nb > 1 and max_dist == BLK
    cur, prev, grid = _attn_specs(n, nb, False)
    in_specs = [cur(w), cur(wk)] + ([prev(wk)] if nb > 1 else []) + [cur(wk)] + ([prev(wk)] if nb > 1 else [])
    args = [q, k] + ([k] if nb > 1 else []) + [v] + ([v] if nb > 1 else [])
    if has_sink:
        in_specs = [pl.BlockSpec(memory_space=pltpu.SMEM)] + in_specs
        args = [sink] + args
    return pl.pallas_call(
        body, name=name, grid=grid, in_specs=in_specs,
        out_specs=[cur(w), cur(LANES)],
        out_shape=[jax.ShapeDtypeStruct((n, l, w), o_dtype), jax.ShapeDtypeStruct((n, l, LANES), f32)],
        scratch_shapes=[pltpu.VMEM((w // HD, BLK, BLK), f32), pltpu.VMEM((w // HD, BLK, 2 * BLK if nb > 1 else BLK), bf16),
                        pltpu.VMEM((w // HD if diag else 1, BLK, BLK), f32)],
        compiler_params=_cp(("arbitrary", "arbitrary")),
    )(*args)


def _attn_bwd(q, k, v, do, delta, lse, sink, *, max_dist, name):
    n, l, w = q.shape
    wk = k.shape[-1]
    nb = l // BLK
    has_sink = sink is not None

    def body(*refs):
        sink_ref = dsink_ref = ck = cv = None
        if has_sink:
            sink_ref, refs = refs[0], refs[1:]
        nin = 8 if nb > 1 else 6
        ins, rest = refs[:nin], refs[nin:]
        if has_sink:
            dq_ref, dk_ref, dv_ref, dsink_ref = rest[:4]
            rest = rest[4:]
        else:
            dq_ref, dk_ref, dv_ref = rest[:3]
            rest = rest[3:]
        step = pl.program_id(1)
        if has_sink:
            @pl.when((pl.program_id(0) == 0) & (step == 0))
            def _():
                dsink_ref[...] = jnp.zeros_like(dsink_ref)

        if nb > 1:
            q_ref, kc_ref, kp_ref, vc_ref, vp_ref, do_ref, delta_ref, lse_ref = ins
            ck, cv = rest[:2]

            @pl.when(step == 0)
            def _():
                ck[...] = jnp.zeros_like(ck)
                cv[...] = jnp.zeros_like(cv)

            last = step == nb // ATT_UNITS - 1
            for u in reversed(range(ATT_UNITS)):
                rows, before = pl.ds(BLK * u, BLK), pl.ds(BLK * (u - 1), BLK)
                unit(q_ref.at[rows, :], kc_ref.at[rows, :], kp_ref if u == 0 else kc_ref.at[before, :],
                     vc_ref.at[rows, :], vp_ref if u == 0 else vc_ref.at[before, :], do_ref.at[rows, :],
                     delta_ref.at[rows, :], lse_ref.at[rows, :], dq_ref.at[rows, :], dk_ref.at[rows, :], dv_ref.at[rows, :],
                     jnp.logical_not(last) if u == 0 else True, sink_ref, dsink_ref, ck, cv, *rest[2:])
        else:
            q_ref, kc_ref, vc_ref, do_ref, delta_ref, lse_ref = ins
            for u in range(ATT_UNITS):
                unit(q_ref.at[u], kc_ref.at[u], None, vc_ref.at[u], None, do_ref.at[u], delta_ref.at[u], lse_ref.at[u],
                     dq_ref.at[u], dk_ref.at[u], dv_ref.at[u], None, sink_ref, dsink_ref, None, None, *rest)

    def unit(q_ref, kc_ref, kp_ref, vc_ref, vp_ref, do_ref, delta_ref, lse_ref, dq_ref, dk_ref, dv_ref, has_prev,
             sink_ref, dsink_ref, ck, cv, sscr, dpscr, pscr, dsscr, dscr=None, ddscr=None):
        lane = lax.broadcasted_iota(jnp.int32, (BLK, LANES), 1)
        qi = lax.broadcasted_iota(jnp.int32, (BLK, BLK), 0)
        kj = lax.broadcasted_iota(jnp.int32, (BLK, BLK), 1)
        tri = kj <= qi
        eye = kj == qi
        cache = {}
        kp, vp = kp_ref, vp_ref
        for p in range(w // LANES):
            sl = slice(LANES * p, LANES * (p + 1))
            qpair, dopair = q_ref[:, sl], do_ref[:, sl]
            kcat, vcat = _kv_cat(kc_ref, kp, p, cache), _kv_cat(vc_ref, vp, p, cache)
            for hh in range(2):
                h = 2 * p + hh
                s = _dot_nt(_lane_half(qpair, hh), kcat)
                dp = _dot_nt(_lane_half(dopair, hh), vcat)
                if nb > 1:
                    sp = s[:, :BLK] if has_prev is True else jnp.where(has_prev, s[:, :BLK], NEG)
                    sscr[h] = jnp.where(tri, s[:, BLK:], sp)
                    dpscr[h] = jnp.where(tri, dp[:, BLK:], dp[:, :BLK])
                    if diag:
                        dscr[h] = jnp.where(eye, sp, NEG)
                        ddscr[h] = dp[:, :BLK]
                else:
                    sscr[h] = jnp.where(tri, s, NEG)
                    dpscr[h] = dp
        for p in range(w // LANES):
            for hh in range(2):
                h = 2 * p + hh
                lse_b = jnp.broadcast_to(lse_ref[:, h:h + 1], (BLK, BLK))
                delta = jnp.broadcast_to(delta_ref[:, h:h + 1], (BLK, BLK))
                pr = jnp.exp(sscr[h] - lse_b)
                ds = pr * (dpscr[h] - delta)
                if nb > 1:
                    if diag:
                        prd = jnp.exp(dscr[h] - lse_b)
                        dsd = prd * (ddscr[h] - delta)
                    else:
                        prd = dsd = 0.0
                    pscr[h, :, :BLK] = jnp.where(tri, prd, pr).astype(bf16)
                    pscr[h, :, BLK:] = jnp.where(tri, pr, 0.0).astype(bf16)
                    dsscr[h, :, :BLK] = jnp.where(tri, dsd, ds).astype(bf16)
                    dsscr[h, :, BLK:] = jnp.where(tri, ds, 0.0).astype(bf16)
                else:
                    pscr[h] = pr.astype(bf16)
                    dsscr[h] = ds.astype(bf16)
                if has_sink:
                    dsk = -jnp.sum(jnp.where(lane == 0, jnp.exp(sink_ref[0, h] - lse_b) * delta, 0.0), keepdims=True)
                    dsink_ref[h:h + 1, :] += jnp.broadcast_to(dsk, (1, LANES))
        for p in range(w // LANES):
            sl = slice(LANES * p, LANES * (p + 1))
            qpair, dopair = q_ref[:, sl], do_ref[:, sl]
            kcat = _kv_cat(kc_ref, kp, p, cache)
            dq_ref[:, sl] = _dot(dsscr[2 * p], _lane_half(kcat, 0)) + _dot(dsscr[2 * p + 1], _lane_half(kcat, 1))
            dk_pair = _dot_tn(dsscr[2 * p], _lane_half(qpair, 0)) + _dot_tn(dsscr[2 * p + 1], _lane_half(qpair, 1))
            dv_pair = _dot_tn(pscr[2 * p], _lane_half(dopair, 0)) + _dot_tn(pscr[2 * p + 1], _lane_half(dopair, 1))
            if nb > 1:
                dk_ref[:, sl] = dk_pair[BLK:] + ck[:, sl]
                dv_ref[:, sl] = dv_pair[BLK:] + cv[:, sl]
                ck[:, sl] = dk_pair[:BLK]
                cv[:, sl] = dv_pair[:BLK]
            else:
                dk_ref[:, sl] = dk_pair
                dv_ref[:, sl] = dv_pair

    assert max_dist in (BLK - 1, BLK) and k.shape == q.shape
    diag = nb > 1 and max_dist == BLK
    cur, prev, grid = _attn_specs(n, nb, True)
    in_specs = ([cur(w), cur(wk)] + ([prev(wk)] if nb > 1 else []) + [cur(wk)] + ([prev(wk)] if nb > 1 else [])
                + [cur(w), cur(LANES), cur(LANES)])
    args = [q, k] + ([k] if nb > 1 else []) + [v] + ([v] if nb > 1 else []) + [do, delta, lse]
    out_specs = [cur(w), cur(wk), cur(wk)]
    out_shape = [jax.ShapeDtypeStruct((n, l, w), f32), jax.ShapeDtypeStruct((n, l, wk), f32), jax.ShapeDtypeStruct((n, l, wk), f32)]
    if has_sink:
        in_specs = [pl.BlockSpec(memory_space=pltpu.SMEM)] + in_specs
        args = [sink] + args
        out_specs.append(pl.BlockSpec((8, LANES), lambda a, i: (0, 0)))
        out_shape.append(jax.ShapeDtypeStruct((8, LANES), f32))
    nh = w // HD
    scratch = [pltpu.VMEM((BLK, wk), f32), pltpu.VMEM((BLK, wk), f32)] if nb > 1 else []
    scratch += [pltpu.VMEM((nh, BLK, BLK), f32)] * 2 + [pltpu.VMEM((nh, BLK, 2 * BLK if nb > 1 else BLK), bf16)] * 2
    if diag:
        scratch += [pltpu.VMEM((nh, BLK, BLK), f32)] * 2
    return pl.pallas_call(
        body, name=name, grid=grid, in_specs=in_specs, out_specs=out_specs, out_shape=out_shape,
        scratch_shapes=scratch, compiler_params=_cp(("arbitrary", "arbitrary")),
    )(*args)


def _split2(x):
    hi = x.astype(bf16)
    return hi, (x - hi.astype(f32)).astype(bf16)


def _heads_to_lanes(xc, e):
    return sum(_dot(t, e) for t in _split2(xc))


def _lanes_to_heads(x, g):
    return sum(_dot(t, g) for t in _split2(x))


HEAD_EXPAND = (np.arange(LANES)[:, None] == np.arange(BW)[None, :] // HD).astype(np.float32)
HEAD_SUM = HEAD_EXPAND.T.copy()


def _branch_weights(l1_ref, l4_ref, l16_ref, scr):
    l4v = _perm_load(l4_ref, scr, 4)
    l16v = _perm_load(l16_ref, scr, 16)
    l1v = l1_ref[...]
    m = jnp.maximum(jnp.maximum(l1v, l4v), l16v)
    e1, e4, e16 = jnp.exp(l1v - m), jnp.exp(l4v - m), jnp.exp(l16v - m)
    z = e1 + e4 + e16
    return e1 / z, e4 / z, e16 / z


def _mix_out(oa, o1, l1, o4, l4, o16, l16, g_mix_a, g_mix_b, w_out, x, mod, g_post):
    def body(oa_ref, o1_ref, l1_ref, o4_ref, l4_ref, o16_ref, l16_ref, ga_ref, gb_ref, w_ref, x_ref, mod_ref, gp_ref, e_ref,
             x1_ref, y_ref, mixed_ref, ob_ref, scr):
        w1, w4, w16 = _branch_weights(l1_ref, l4_ref, l16_ref, scr)
        e = e_ref[...]
        x1w, x4w = _heads_to_lanes(w1, e), _heads_to_lanes(w4, e)
        ob = (x1w * o1_ref[...].astype(f32) + x4w * _perm_load(o4_ref, scr, 4)
              + (1.0 - x1w - x4w) * _perm_load(o16_ref, scr, 16))
        ob_ref[...] = ob
        oan, _ = _rms(oa_ref[...])
        obn, _ = _rms(ob)
        mixed = jnp.concatenate([oan * ga_ref[...], obn * gb_ref[...]], axis=1).astype(bf16)
        mixed_ref[...] = mixed
        y = _dot(mixed, w_ref[...])
        y_ref[...] = y
        yn, _ = _rms(y)
        x1_ref[...] = x_ref[...] + mod_ref[2:3, :] * (yn * gp_ref[...])

    nat = lambda w, dt: jax.ShapeDtypeStruct((BL, SEQ, w), dt)
    return pl.pallas_call(
        body, name="mix_out", grid=(BL, NJ),
        in_specs=[_tok(AQ), _tok(BW), _tok(LANES), _perm_spec(4, BW), _perm_spec(4, LANES), _perm_spec(16, BW),
                  _perm_spec(16, LANES), _full((1, AQ)), _full((1, BW)), _full((D, D)), _tok(D), MOD_SPEC, _full((1, D)),
                  _full((LANES, BW))],
        out_specs=[_tok(D), _tok(D), _tok(D), _tok(BW)],
        out_shape=[nat(D, f32), nat(D, f32), nat(D, bf16), nat(BW, f32)],
        scratch_shapes=[pltpu.VMEM((BW // LANES, TM, LANES), f32)],
        compiler_params=_cp(("arbitrary", "arbitrary")),
    )(oa, o1, l1, o4, l4, o16, l16, g_mix_a, g_mix_b, w_out, x, mod, g_post, jnp.asarray(HEAD_EXPAND, bf16))


def _mlp_up(x1, mod, g_pre, w_up):
    def body(x_ref, mod_ref, g_ref, w_ref, h_ref, u_ref, a_ref):
        xn, _ = _rms(x_ref[...])
        h = (xn * g_ref[...]) * (1.0 + mod_ref[4:5, :]) + mod_ref[3:4, :]
        hb = h.astype(bf16)
        h_ref[...] = hb
        for s in range(NCHIP):
            u = _dot(hb, w_ref[s])
            u_ref[:, D * s:D * (s + 1)] = u.astype(bf16)
            a_ref[:, D * s:D * (s + 1)] = jnp.square(jnp.maximum(u, 0.0)).astype(bf16)

    nat = lambda w: jax.ShapeDtypeStruct((BL, SEQ, w), bf16)
    return pl.pallas_call(
        body, name="mlp_up", grid=(BL, NJ),
        in_specs=[_tok(D), MOD_SPEC, _full((1, D)), _full((NCHIP, D, D))],
        out_specs=[_tok(D), _tok(DFF), _tok(DFF)], out_shape=[nat(D), nat(DFF), nat(DFF)],
        compiler_params=_cp(("arbitrary", "arbitrary")),
    )(x1, mod, g_pre, w_up)


def _mlp_down(a, w_down, x1, target, mod, g_post):
    def body(a_ref, w_ref, x_ref, t_ref, mod_ref, g_ref, gx_ref, dy_ref, accb_ref, accg_ref):
        _acc_init(accb_ref, accg_ref)
        y2 = _dot(a_ref[...], w_ref[...])
        yn, r = _rms(y2)
        g = g_ref[...]
        gt = mod_ref[5:6, :]
        n2 = yn * g
        err = x_ref[...] + gt * n2 - t_ref[...]
        gout = err * (1.0 / D)
        gx_ref[...] = gout
        dn2 = gout * gt
        dy_ref[...] = _rms_bwd(dn2 * g, yn, r).astype(bf16)
        accb_ref[0:1, :] += _colsum(gout * n2)
        accg_ref[0:1, :] += _colsum(dn2 * yn)
        accg_ref[1:2, :] += jnp.broadcast_to(jnp.sum(err * err, keepdims=True), (1, D))

    return pl.pallas_call(
        body, name="mlp_down", grid=(BL, NJ),
        in_specs=[_tok(DFF), _full((DFF, D)), _tok(D), _tok(D), MOD_SPEC, _full((1, D))],
        out_specs=[_tok(D), _tok(D), ACCB_SPEC, ACCG_SPEC],
        out_shape=[jax.ShapeDtypeStruct((BL, SEQ, D), f32), jax.ShapeDtypeStruct((BL, SEQ, D), bf16)] + ACC_SHAPES,
        compiler_params=_cp(("arbitrary", "arbitrary")),
    )(a, w_down, x1, target, mod, g_post)


def _mlp_bwd(dy2, u, w_down, w_up, x1, gx, mod, g_pre):
    def body(dy_ref, u_ref, wd_hbm, wu_hbm, x_ref, gx_ref, mod_ref, g_ref, du_ref, gx1_ref, accb_ref, accg_ref, wd, wu, sem):
        _acc_init(accb_ref, accg_ref)

        @pl.when((pl.program_id(0) == 0) & (pl.program_id(1) == 0))
        def _():
            c1 = pltpu.make_async_copy(wd_hbm, wd, sem.at[0])
            c2 = pltpu.make_async_copy(wu_hbm, wu, sem.at[1])
            c1.start()
            c2.start()
            c1.wait()
            c2.wait()

        dy = dy_ref[...]
        dh = jnp.zeros((TM, D), f32)
        for s in range(NCHIP):
            sl = slice(D * s, D * (s + 1))
            da = _dot_nt(dy, wd[sl, :])
            du = (da * (2.0 * jnp.maximum(u_ref[:, sl].astype(f32), 0.0))).astype(bf16)
            du_ref[:, sl] = du
            dh = dh + _dot_nt(du, wu[s])
        xn, r = _rms(x_ref[...])
        g = g_ref[...]
        n = xn * g
        dn = dh * (1.0 + mod_ref[4:5, :])
        gx1_ref[...] = gx_ref[...] + _rms_bwd(dn * g, xn, r)
        accb_ref[0:1, :] += _colsum(dh * n)
        accb_ref[1:2, :] += _colsum(dh)
        accg_ref[0:1, :] += _colsum(dn * xn)

    anyspec = pl.BlockSpec(memory_space=pl.ANY)
    return pl.pallas_call(
        body, name="mlp_bwd", grid=(BL, NJ),
        in_specs=[_tok(D), _tok(DFF), anyspec, anyspec, _tok(D), _tok(D), MOD_SPEC, _full((1, D))],
        out_specs=[_tok(DFF), _tok(D), ACCB_SPEC, ACCG_SPEC],
        out_shape=[jax.ShapeDtypeStruct((BL, SEQ, DFF), bf16), jax.ShapeDtypeStruct((BL, SEQ, D), f32)] + ACC_SHAPES,
        scratch_shapes=[pltpu.VMEM((DFF, D), bf16), pltpu.VMEM((NCHIP, D, D), bf16), pltpu.SemaphoreType.DMA((2,))],
        compiler_params=_cp(("arbitrary", "arbitrary")),
    )(dy2, u, w_down, w_up, x1, gx, mod, g_pre)


def _matmul_tn(a, b, *, tn, col_blocked, name, out_dtype=f32):
    t, m = a.shape
    n = b.shape[1]
    tmm = min(m, 1024)
    tk = 2048 if tn <= 1024 else 1024
    nk = t // tk

    def body(a_ref, b_ref, o_ref, acc):
        k = pl.program_id(2)

        @pl.when(k == 0)
        def _():
            acc[...] = jnp.zeros_like(acc)

        acc[...] += _dot_tn(a_ref[...], b_ref[...])

        @pl.when(k == nk - 1)
        def _():
            o_ref[...] = acc[...].astype(out_dtype)

    if col_blocked:
        out_spec = pl.BlockSpec((None, tmm, tn), lambda i, j, k: (j, i, 0))
        out_shape = jax.ShapeDtypeStruct((n // tn, m, tn), out_dtype)
    else:
        out_spec = pl.BlockSpec((tmm, tn), lambda i, j, k: (i, j))
        out_shape = jax.ShapeDtypeStruct((m, n), out_dtype)
    return pl.pallas_call(
        body, name=name, grid=(m // tmm, n // tn, nk),
        in_specs=[pl.BlockSpec((tk, tmm), lambda i, j, k: (k, i)), pl.BlockSpec((tk, tn), lambda i, j, k: (k, j))],
        out_specs=out_spec, out_shape=out_shape, scratch_shapes=[pltpu.VMEM((tmm, tn), f32)],
        compiler_params=_cp(("arbitrary", "arbitrary", "arbitrary")),
    )(a, b)


def _grad_w_in(h, dproj):
    t = h.shape[0]
    tk = 1024
    nk = t // tk
    sw = INW // NCHIP

    def body(a_ref, b_ref, o_ref, acc):
        k = pl.program_id(0)

        @pl.when(k == 0)
        def _():
            acc[...] = jnp.zeros_like(acc)

        acc[...] += _dot_tn(a_ref[...], b_ref[...])

        @pl.when(k == nk - 1)
        def _():
            for s in range(NCHIP):
                o_ref[s] = acc[:, sw * s:sw * (s + 1)]

    return pl.pallas_call(
        body, name="grad_w_in", grid=(nk,),
        in_specs=[pl.BlockSpec((tk, D), lambda k: (k, 0)), pl.BlockSpec((tk, INW), lambda k: (k, 0))],
        out_specs=pl.BlockSpec((NCHIP, D, sw), lambda k: (0, 0, 0)), out_shape=jax.ShapeDtypeStruct((NCHIP, D, sw), f32),
        scratch_shapes=[pltpu.VMEM((D, INW), f32)], compiler_params=_cp(("arbitrary",)),
    )(h, dproj)


def _attn_out_bwd(gx1, y, mod, g_post, w_out, oa, ob, g_mix_a, g_mix_b, l1, l4, l16):
    def body(gx_ref, y_ref, mod_ref, gp_ref, w_ref, oa_ref, ob_ref, ga_ref, gb_ref, l1_ref, l4_ref, l16_ref, e_ref, g_ref,
             dy_ref, doa_ref, do1_ref, do4_ref, do16_ref, da_ref, d1_ref, d4_ref, d16_ref, accb_ref, accg_ref, scr):
        _acc_init(accb_ref, accg_ref)
        w1, w4, w16 = _branch_weights(l1_ref, l4_ref, l16_ref, scr)
        e, hs = e_ref[...], g_ref[...]
        gx1v = gx_ref[...]
        yn, ry = _rms(y_ref[...])
        gp = gp_ref[...]
        gt = mod_ref[2:3, :]
        dn1 = gx1v * gt
        dy = _rms_bwd(dn1 * gp, yn, ry).astype(bf16)
        dy_ref[...] = dy
        dmixed = _dot_nt(dy, w_ref[...])
        dma, dmb = dmixed[:, :AQ], dmixed[:, AQ:]
        oa, ob = oa_ref[...], ob_ref[...]
        oan, ra = _rms(oa)
        obn, rb = _rms(ob)
        doa = _rms_bwd(dma * ga_ref[...], oan, ra)
        doa_ref[...] = doa.astype(bf16)
        da_ref[...] = _lanes_to_heads(doa * oa, hs)
        dob = _rms_bwd(dmb * gb_ref[...], obn, rb)
        dd = _lanes_to_heads(dob * ob, hs)
        x1w, x4w = _heads_to_lanes(w1, e), _heads_to_lanes(w4, e)
        do1_ref[...] = (x1w * dob).astype(bf16)
        d1_ref[...] = w1 * dd
        _perm_store(x4w * dob, scr, do4_ref, 4)
        _perm_store(w4 * dd, scr, d4_ref, 4)
        _perm_store((1.0 - x1w - x4w) * dob, scr, do16_ref, 16)
        _perm_store(w16 * dd, scr, d16_ref, 16)
        accb_ref[0:1, :] += _colsum(gx1v * (yn * gp))
        accg_ref[0:1, :] += _colsum(dn1 * yn)
        accg_ref[1:2, :] += jnp.concatenate([_colsum(dma * oan), _colsum(dmb * obn)], axis=1)

    nat = lambda w, dt: jax.ShapeDtypeStruct((BL, SEQ, w), dt)
    return pl.pallas_call(
        body, name="attn_out_bwd", grid=(BL, NJ),
        in_specs=[_tok(D), _tok(D), MOD_SPEC, _full((1, D)), _full((D, D)), _tok(AQ), _tok(BW), _full((1, AQ)), _full((1, BW)),
                  _tok(LANES), _perm_spec(4, LANES), _perm_spec(16, LANES), _full((LANES, BW)), _full((BW, LANES))],
        out_specs=[_tok(D), _tok(AQ), _tok(BW), _perm_spec(4, BW), _perm_spec(16, BW),
                   _tok(LANES), _tok(LANES), _perm_spec(4, LANES), _perm_spec(16, LANES), ACCB_SPEC, ACCG_SPEC],
        out_shape=[nat(D, bf16), nat(AQ, bf16), nat(BW, bf16), jax.ShapeDtypeStruct((BL, 4, SEQ // 4, BW), bf16),
                   jax.ShapeDtypeStruct((BL, 16, SEQ // 16, BW), bf16), nat(LANES, f32), nat(LANES, f32),
                   jax.ShapeDtypeStruct((BL, 4, SEQ // 4, LANES), f32), jax.ShapeDtypeStruct((BL, 16, SEQ // 16, LANES), f32)]
                  + ACC_SHAPES,
        scratch_shapes=[pltpu.VMEM((BW // LANES, TM, LANES), f32)],
        compiler_params=_cp(("arbitrary", "arbitrary")),
    )(gx1, y, mod, g_post, w_out, oa, ob, g_mix_a, g_mix_b, l1, l4, l16, jnp.asarray(HEAD_EXPAND, bf16),
      jnp.asarray(HEAD_SUM, bf16))


def _attn_in_bwd(dqa, dka, dva, d1, d4, d16, tc, ts1, ts2, w_in, x, gx1, mod, g_pre):
    def body(dqa_ref, dka_ref, dva_ref, dq1_ref, dk1_ref, dv1_ref, dq4_ref, dk4_ref, dv4_ref, dq16_ref, dk16_ref, dv16_ref,
             c_ref, s1_ref, s2_ref, w_ref, x_ref, gx_ref, mod_ref, g_ref, dproj_ref, dx_ref, accb_ref, accg_ref, scr):
        _acc_init(accb_ref, accg_ref)
        c, s1, s2 = c_ref[...], s1_ref[...], s2_ref[...]
        tot = lambda r1, r4, r16: r1[...] + _perm_load(r4, scr, 4) + _perm_load(r16, scr, 16)
        dqb = tot(dq1_ref, dq4_ref, dq16_ref)
        dkb = tot(dk1_ref, dk4_ref, dk16_ref)
        dvb = tot(dv1_ref, dv4_ref, dv16_ref)
        dproj = jnp.concatenate([
            _rope_t(dqa_ref[...], c, s1, s2) * 0.125, _rope_t(_per_kv_head(dka_ref[...]), c, s1, s2),
            _per_kv_head(dva_ref[...]),
            _rope_t(dqb, c, s1, s2) * 0.125, _rope_t(dkb, c, s1, s2), dvb], axis=1).astype(bf16)
        dproj_ref[...] = dproj
        dh = _dot_nt(dproj, w_ref[...])
        xn, r = _rms(x_ref[...])
        g = g_ref[...]
        dn = dh * (1.0 + mod_ref[1:2, :])
        dx_ref[...] = gx_ref[...] + _rms_bwd(dn * g, xn, r)
        accb_ref[0:1, :] += _colsum(dh * (xn * g))
        accb_ref[1:2, :] += _colsum(dh)
        accg_ref[0:1, :] += _colsum(dn * xn)

    return pl.pallas_call(
        body, name="attn_in_bwd", grid=(BL, NJ),
        in_specs=[_tok(AQ), _tok(AQ), _tok(AQ)] + [_tok(BW)] * 3 + [_perm_spec(4, BW)] * 3 + [_perm_spec(16, BW)] * 3
                 + [_tok(LANES)] * 3 + [_full((D, INW)), _tok(D), _tok(D), MOD_SPEC, _full((1, D))],
        out_specs=[_tok(INW), _tok(D), ACCB_SPEC, ACCG_SPEC],
        out_shape=[jax.ShapeDtypeStruct((BL, SEQ, INW), bf16), jax.ShapeDtypeStruct((BL, SEQ, D), f32)] + ACC_SHAPES,
        scratch_shapes=[pltpu.VMEM((BW // LANES, TM, LANES), f32)],
        compiler_params=_cp(("arbitrary", "arbitrary")),
    )(dqa, dka, dva, *d1, *d4, *d16, tc, ts1, ts2, w_in, x, gx1, mod, g_pre)


def _inv_lane():
    inv = np.float32(THETA) ** (-np.arange(0, ROT, 2, dtype=np.float32) / np.float32(ROT))
    lane = np.arange(LANES) % HD
    return jnp.asarray(np.where(lane < ROT, inv[lane % (ROT // 2)], 0.0).astype(np.float32)[None, :])


def _local_step(x, positions, mod, target, inv_lane, first_weight, later_weights, grad_ready, g_attn_pre,
                g_attn_post, sink_a, g_mix_a, g_mix_b, g_mlp_pre, g_mlp_post):
    tabs = _rope_tables(positions.reshape(BL * SEQ, 1), inv_lane)
    w_in = first_weight(tuple(tabs))
    tc, ts1, ts2 = [t.reshape(BL, SEQ, LANES) for t in tabs]

    (h, qa, ka, va, q1, k1, v1, q4, k4, v4, q16, k16, v16, w_in) = _attn_in(x, mod, g_attn_pre, w_in, tc, ts1, ts2)
    seqs = lambda t: t.reshape(t.shape[0] * t.shape[1], t.shape[2], t.shape[3])
    q4, k4, v4, q16, k16, v16 = [seqs(t) for t in (q4, k4, v4, q16, k16, v16)]
    oa, la = _attn_fwd(qa, ka, va, sink_a, max_dist=BLK - 1, o_dtype=f32, name="attn_a_fwd")
    o1, l1 = _attn_fwd(q1, k1, v1, None, max_dist=BLK, o_dtype=bf16, name="attn_b1_fwd")
    o4, l4 = _attn_fwd(q4, k4, v4, None, max_dist=BLK, o_dtype=bf16, name="attn_b4_fwd")
    o16, l16 = _attn_fwd(q16, k16, v16, None, max_dist=BLK, o_dtype=bf16, name="attn_b16_fwd")
    b4 = lambda t: t.reshape(BL, 4, SEQ // 4, t.shape[-1])
    b16 = lambda t: t.reshape(BL, 16, SEQ // 16, t.shape[-1])
    w_out, mlp_weights, mod = later_weights((oa, o1, o4, o16), mod)
    x1, y, mixed, ob = _mix_out(oa, o1, l1, b4(o4), b4(l4), b16(o16), b16(l16), g_mix_a, g_mix_b, w_out, x, mod, g_attn_post)
    w_up, w_down = mlp_weights((x1,))
    h2, u, a = _mlp_up(x1, mod, g_mlp_pre, w_up)
    gx, dy2, accb_d, accg_d = _mlp_down(a, w_down, x1, target, mod, g_mlp_post)

    flat = lambda t: t.reshape(BL * SEQ, t.shape[-1])
    mod = grad_ready("w_down", _matmul_tn(flat(a), flat(dy2), tn=D, col_blocked=False, name="grad_w_down", out_dtype=bf16), mod)
    du, gx1, accb_m, accg_m = _mlp_bwd(dy2, u, w_down, w_up, x1, gx, mod, g_mlp_pre)
    mod = grad_ready("w_up", _matmul_tn(flat(h2), flat(du), tn=D, col_blocked=True, name="grad_w_up", out_dtype=bf16), mod)

    dy, doa, do1, do4, do16, da, dl1, dl4, dl16, accb_o, accg_o = _attn_out_bwd(
        gx1, y, mod, g_attn_post, w_out, oa, ob, g_mix_a, g_mix_b, l1, b4(l4), b16(l16))
    gw_out = _matmul_tn(flat(mixed), flat(dy), tn=D, col_blocked=False, name="grad_w_out")
    dqa, dka, dva, dsink = _attn_bwd(qa, ka, va, doa, da, la, sink_a, max_dist=BLK - 1, name="attn_a_bwd")
    d1 = _attn_bwd(q1, k1, v1, do1, dl1, l1, None, max_dist=BLK, name="attn_b1_bwd")
    d4 = _attn_bwd(q4, k4, v4, seqs(do4), seqs(dl4), l4, None, max_dist=BLK, name="attn_b4_bwd")
    d16 = _attn_bwd(q16, k16, v16, seqs(do16), seqs(dl16), l16, None, max_dist=BLK, name="attn_b16_bwd")
    dproj, grad_x, accb_i, accg_i = _attn_in_bwd(dqa, dka, dva, d1, [b4(t) for t in d4], [b16(t) for t in d16],
                                                 tc, ts1, ts2, w_in, x, gx1, mod, g_attn_pre)
    gw_in = _grad_w_in(flat(h), flat(dproj))
    dsink = grad_ready("w_in_w_out", (gw_in, gw_out), dsink)

    return grad_x, (accb_i, accb_o, accb_m, accb_d, accg_i, accg_o, accg_m, accg_d, dsink)


ADAW = NMOD * D // NCHIP


def _pos():
    return lax.axis_index("x"), lax.axis_index("y"), lax.axis_index("c")


def _flip(v, bit):
    return 1 - v if bit else v


def _all_peers(x, y, c):
    return [(_flip(x, k >> 2 & 1), _flip(y, k >> 1 & 1), _flip(c, k & 1)) for k in range(1, NDEV)]


def _other_chips(x, y):
    return [(1 - x, y), (x, 1 - y), (1 - x, 1 - y)]


def _rcopy(src, dst, send, recv, k, dev, k_recv=None):
    return pltpu.make_async_remote_copy(src_ref=src, dst_ref=dst, send_sem=send.at[k],
                                        recv_sem=recv.at[k if k_recv is None else k_recv],
                                        device_id=dev, device_id_type=MESH)


def _gather_small(src, buf, send, recv):
    x, y, c = _pos()
    me = 4 * x + 2 * y + c
    peers = _all_peers(x, y, c)
    sends = [_rcopy(src, buf.at[me], send, recv, k, p) for k, p in enumerate(peers)]
    for cp in sends:
        cp.start()
    for k, (px, py, pc) in enumerate(peers):
        _rcopy(src, buf.at[4 * px + 2 * py + pc], send, recv, k, (px, py, pc)).wait_recv()
    for cp in sends:
        cp.wait_send()
    return me


def _ada_fwd(c_in, w_ada, b_cols):
    def body(c_ref, w_ref, b_ref, mod_ref, cond_ref, cbuf, mbuf, s1, r1, s2, r2):
        x, y, c = _pos()
        chip = 2 * x + y
        me = _gather_small(c_ref, cbuf, s1, r1)
        cbuf[me] = c_ref[...]
        for i in range(NDEV):
            cond_ref[BL * i:BL * (i + 1), :] = cbuf[i]
        call = cond_ref[...]
        cond = call / (1.0 + jnp.exp(-call))
        cond_ref[...] = cond
        mbuf[chip] = _dot(cond.astype(bf16), w_ref[...].astype(bf16)) + b_ref[...]
        chips = _other_chips(x, y)
        sends = [_rcopy(mbuf.at[chip], mbuf.at[chip], s2, r2, j, (px, py, c)) for j, (px, py) in enumerate(chips)]
        for cp in sends:
            cp.start()
        for j, (px, py) in enumerate(chips):
            _rcopy(mbuf.at[chip], mbuf.at[2 * px + py], s2, r2, j, (px, py, c)).wait_recv()
        for cp in sends:
            cp.wait_send()
        row = lax.broadcasted_iota(jnp.int32, (BL * NDEV, ADAW), 0)
        for s in range(NCHIP):
            slab = mbuf[s]
            for j in range(BL):
                mod_ref[j:j + 1, ADAW * s:ADAW * (s + 1)] = jnp.sum(jnp.where(row == BL * me + j, slab, 0.0), axis=0, keepdims=True)

    vm = pl.BlockSpec(memory_space=pltpu.VMEM)
    return pl.pallas_call(
        body, name="ada_fwd", in_specs=[vm, vm, vm], out_specs=[vm, vm],
        out_shape=[jax.ShapeDtypeStruct((BL, NMOD * D), f32), jax.ShapeDtypeStruct((BL * NDEV, D), f32)],
        scratch_shapes=[pltpu.VMEM((NDEV, BL, D), f32), pltpu.VMEM((NCHIP, BL * NDEV, ADAW), f32),
                        pltpu.SemaphoreType.DMA((NDEV - 1,)), pltpu.SemaphoreType.DMA((NDEV - 1,)),
                        pltpu.SemaphoreType.DMA((NCHIP - 1,)), pltpu.SemaphoreType.DMA((NCHIP - 1,))],
        compiler_params=pltpu.CompilerParams(vmem_limit_bytes=VMEM_LIMIT),
    )(c_in, w_ada, b_cols)


def _small_allreduce(accs, cond_all):
    def body(bi, bo, bm, bd, gi, go, gm, gd, dsink, cond_ref, gw_ref, gb_ref, small_ref, pay, pbuf, dall, s1, r1):
        x, y, c = _pos()
        chip = 2 * x + y
        pay[...] = jnp.zeros_like(pay)
        for b in range(BL):
            for k, (ref, r) in enumerate(((bi, 1), (bi, 0), (bo, 0), (bm, 1), (bm, 0), (bd, 0))):
                pay[b:b + 1, D * k:D * (k + 1)] = ref[b, r:r + 1, :]
        for off, ref, r in ((OFF_G_ATTN_PRE, gi, 0), (OFF_G_ATTN_POST, go, 0), (OFF_G_MIX_A, go, 1), (OFF_G_MLP_PRE, gm, 0),
                            (OFF_G_MLP_POST, gd, 0)):
            pay[BL:BL + 1, off:off + D] = ref[r:r + 1, :]
        eye = lax.broadcasted_iota(jnp.int32, (8, LANES), 0) == lax.broadcasted_iota(jnp.int32, (8, LANES), 1)
        pay[BL:BL + 1, OFF_SINK:OFF_SINK + LANES] = jnp.sum(jnp.where(eye, dsink[...], 0.0), axis=0, keepdims=True)
        pay[BL:BL + 1, OFF_LOSS:OFF_LOSS + LANES] = gd[1:2, 0:LANES]
        me = _gather_small(pay, pbuf, s1, r1)
        pbuf[me] = pay[...]
        small = pbuf[0, BL:BL + 1, :]
        for i in range(1, NDEV):
            small = small + pbuf[i, BL:BL + 1, :]
        small_ref[...] = small
        for i in range(NDEV):
            dall[BL * i:BL * (i + 1), :] = pbuf[i, 0:BL, :]
        gb_ref[...] = jnp.sum(dall[...], axis=0, keepdims=True)
        cols = jnp.zeros((BL * NDEV, ADAW), f32)
        for s in range(NCHIP):
            cols = cols + jnp.where(chip == s, dall[:, ADAW * s:ADAW * (s + 1)], 0.0)
        gw_ref[...] = lax.dot_general(cond_ref[...], cols, (((0,), (0,)), ((), ())), preferred_element_type=f32,
                                      precision=lax.Precision.HIGHEST)

    vm = pl.BlockSpec(memory_space=pltpu.VMEM)
    return pl.pallas_call(
        body, name="small_allreduce", in_specs=[vm] * 10, out_specs=[vm] * 3,
        out_shape=[jax.ShapeDtypeStruct((D, ADAW), f32), jax.ShapeDtypeStruct((1, PAYW), f32), jax.ShapeDtypeStruct((1, PAYW), f32)],
        scratch_shapes=[pltpu.VMEM((4, PAYW), f32), pltpu.VMEM((NDEV, 4, PAYW), f32), pltpu.VMEM((BL * NDEV, PAYW), f32),
                        pltpu.SemaphoreType.DMA((NDEV - 1,)), pltpu.SemaphoreType.DMA((NDEV - 1,))],
        compiler_params=pltpu.CompilerParams(vmem_limit_bytes=VMEM_LIMIT),
    )(*accs, cond_all)


def _half(ref, c):
    r2 = ref.shape[0] // 2
    return ref.at[pl.ds(c * r2 if isinstance(c, int) else pl.multiple_of(c * r2, 16), r2), :]


HBM_SPEC = pl.BlockSpec(memory_space=pltpu.HBM)
SEM_SPEC = pl.BlockSpec(memory_space=pltpu.SEMAPHORE)
EFFECT = pltpu.SideEffectType.DATAFLOW_SIDE_EFFECTING
NLINK = NCHIP - 1


def _in_hbm(a):
    return pltpu.with_memory_space_constraint(a, pltpu.HBM)


NSEM = 8


def _split_start(name, srcs, land_shapes, builds, carry, after=(), lands=None):
    n = len(srcs)
    na, nc = len(after), len(carry)

    def body(*refs):
        src, land = refs[:n], refs[n:2 * n]
        kept = refs[2 * n + na:2 * n + na + nc]
        outs = refs[2 * n + na + nc:]
        send, recv, passed = outs[:n], outs[n:2 * n], outs[4 * n:]
        for t in range(n):
            for out_cp, _ in builds[t](src[t], land[t], send[t], recv[t]):
                out_cp.start()
        for a, b in zip(kept, passed):
            b[...] = a[...]

    if lands is None:
        lands = [lax.empty(s.shape, s.dtype) for s in land_shapes]
    lands = [_in_hbm(a) for a in lands]
    sems = [pltpu.SemaphoreType.DMA((NSEM,))] * (2 * n)
    thru = [pltpu.HBM(a.shape, a.dtype) for a in list(srcs) + lands]
    vm = pl.BlockSpec(memory_space=pltpu.VMEM)
    res = pl.pallas_call(
        body, name=name, out_shape=sems + thru + [jax.ShapeDtypeStruct(a.shape, a.dtype) for a in carry],
        in_specs=[HBM_SPEC] * (2 * n) + [pl.BlockSpec(memory_space=pl.ANY)] * na + [vm] * nc,
        out_specs=[SEM_SPEC] * (2 * n) + [HBM_SPEC] * (2 * n) + [vm] * nc,
        input_output_aliases={i: 2 * n + i for i in range(2 * n)},
        compiler_params=pltpu.CompilerParams(has_side_effects=EFFECT),
    )(*[_in_hbm(a) for a in srcs], *lands, *after, *carry)
    flight = [(res[2 * n + t], res[3 * n + t], res[t], res[n + t]) for t in range(n)]
    return flight, list(res[4 * n:])


def _split_wait(name, flight, builds, after):
    m = len(flight)
    na = len(after)

    def body(*refs):
        src, land, send, recv = refs[:m], refs[m:2 * m], refs[2 * m:3 * m], refs[3 * m:4 * m]
        for t in range(m):
            for out_cp, in_cp in builds[t](src[t], land[t], send[t], recv[t]):
                out_cp.wait_send()
                in_cp.wait_recv()

    ops = [f[0] for f in flight] + [f[1] for f in flight] + [f[2] for f in flight] + [f[3] for f in flight]
    res = pl.pallas_call(
        body, name=name, out_shape=[pltpu.HBM(a.shape, a.dtype) for a in ops[:2 * m]],
        in_specs=[HBM_SPEC] * (2 * m) + [SEM_SPEC] * (2 * m) + [pl.BlockSpec(memory_space=pl.ANY)] * na,
        out_specs=[HBM_SPEC] * (2 * m), input_output_aliases={i: i for i in range(2 * m)},
        compiler_params=pltpu.CompilerParams(has_side_effects=EFFECT),
    )(*ops, *after)
    return res[:m], res[m:2 * m]


def _weight_copies(src, land, send, recv):
    x, y, c = _pos()
    chip = 2 * x + y
    return [(_rcopy(_half(src, c), _half(land.at[chip], c), send, recv, j, (px, py, c)),
             _rcopy(_half(src, c), _half(land.at[2 * px + py], c), send, recv, j, (px, py, c)))
            for j, (px, py) in enumerate(_other_chips(x, y))]


def _grad_copies(src, land, send, recv):
    x, y, c = _pos()
    return [(_rcopy(src.at[2 * px + py], land.at[j], send, recv, j, (px, py, c)),
             _rcopy(src.at[2 * px + py], land.at[j], send, recv, j, (px, py, c)))
            for j, (px, py) in enumerate(_other_chips(x, y))]


NDIRECT = NDEV - 1


def _direct_grad_copies(src, land, send, recv):
    x, y, c = _pos()
    out, arrive = [], []
    for j, (px, py) in enumerate(_other_chips(x, y)):
        for hc in range(2):
            out.append(_rcopy(_half(src.at[2 * px + py], hc), land.at[2 * j + c], send, recv, 2 * j + hc, (px, py, hc),
                              k_recv=2 * j + c))
            arrive.append(_rcopy(_half(src.at[2 * px + py], hc), land.at[2 * j + hc], send, recv, 2 * j + hc, (px, py, hc)))
    own = _rcopy(_half(src.at[2 * x + y], 1 - c), land.at[NDIRECT - 1], send, recv, NDIRECT - 1, (x, y, 1 - c))
    return list(zip(out, arrive)) + [(own, own)]


def _pair_grad_copies(src, land, send, recv):
    x, y, c = _pos()
    r2 = src.shape[1] // 2
    cp = _rcopy(src.at[:, pl.ds(pl.multiple_of((1 - c) * r2, 8), r2), :], land, send, recv, 0, (x, y, 1 - c))
    return [(cp, cp)]


def _pair_weight_copies(src, land, send, recv):
    x, y, c = _pos()
    sib = (x, y, 1 - c)
    cps = []
    for j, (px, py) in enumerate(_other_chips(x, y)):
        mine, theirs = _half(land.at[2 * px + py], c), _half(land.at[2 * px + py], 1 - c)
        cps.append((_rcopy(mine, mine, send, recv, j, sib), _rcopy(theirs, theirs, send, recv, j, sib)))
    own = _rcopy(src, land.at[2 * x + y], send, recv, NLINK, sib)
    return cps + [(own, own)]


RS_ROWS = 128


def _pair_add(g, landed, c_arr, name):
    _, r2, cw = landed.shape
    nr = r2 // RS_ROWS

    def body(c_ref, g_ref, p_ref, o_ref):
        o_ref[...] = (g_ref[...] + p_ref[...]).astype(bf16)

    gs = pltpu.PrefetchScalarGridSpec(
        num_scalar_prefetch=1, grid=(NCHIP, nr),
        in_specs=[pl.BlockSpec((None, RS_ROWS, cw), lambda s, j, c: (s, c[0] * nr + j, 0)),
                  pl.BlockSpec((None, RS_ROWS, cw), lambda s, j, c: (s, j, 0))],
        out_specs=pl.BlockSpec((None, RS_ROWS, cw), lambda s, j, c: (s, j, 0)))
    return pl.pallas_call(body, name=name, grid_spec=gs, out_shape=jax.ShapeDtypeStruct((NCHIP, r2, cw), bf16),
                          compiler_params=_cp(("arbitrary", "arbitrary")))(c_arr, g, landed)


def _chip_add(own, landed, pos_arr, name):
    nl, r2, cw = landed.shape
    nr = r2 // RS_ROWS
    whole = own.shape[1] == 2 * r2

    def body(s_ref, h_ref, q_ref, o_ref):
        acc = h_ref[...].astype(f32)
        for j in range(nl):
            acc = acc + q_ref[j].astype(f32)
        o_ref[...] = acc

    gs = pltpu.PrefetchScalarGridSpec(
        num_scalar_prefetch=1, grid=(nr,),
        in_specs=[pl.BlockSpec((None, RS_ROWS, cw), lambda j, s: (s[0], (s[1] * nr if whole else 0) + j, 0)),
                  pl.BlockSpec((nl, RS_ROWS, cw), lambda j, s: (0, j, 0))],
        out_specs=pl.BlockSpec((RS_ROWS, cw), lambda j, s: (s[1] * nr + j, 0)))
    return pl.pallas_call(body, name=name, grid_spec=gs, out_shape=jax.ShapeDtypeStruct((2 * r2, cw), f32),
                          compiler_params=_cp(("arbitrary",)))(pos_arr, own, landed)


def _pair_gather_copies(src, land, send, recv):
    x, y, c = _pos()
    sib = (x, y, 1 - c)
    return [(_rcopy(_half(land, c), _half(land, c), send, recv, 0, sib),
             _rcopy(_half(land, 1 - c), _half(land, 1 - c), send, recv, 0, sib))]


def _adamw_math(w, g, m, v):
    m = B1 * m + (1.0 - B1) * g
    v = B2 * v + (1.0 - B2) * jnp.square(g)
    m_hat = m / (1.0 - B1 ** STEP)
    v_hat = v / (1.0 - B2 ** STEP)
    return -LR * (m_hat / (jnp.sqrt(v_hat) + AEPS) + WD * w), m, v


ADAM_ROWS = 256


def _adamw(w, g, m, v, name):
    r, cw = w.shape

    def body(w_ref, g_ref, m_ref, v_ref, go_ref, d_ref, mo_ref, vo_ref):
        g = g_ref[...]
        go_ref[...] = g
        d_ref[...], mo_ref[...], vo_ref[...] = _adamw_math(w_ref[...], g, m_ref[...], v_ref[...])

    rows = max(k for k in range(8, ADAM_ROWS + 1, 8) if r % k == 0)
    spec = pl.BlockSpec((rows, cw), lambda i: (i, 0))
    return pl.pallas_call(body, name=name, grid=(r // rows,), in_specs=[spec] * 4, out_specs=[spec] * 4,
                          out_shape=[jax.ShapeDtypeStruct((r, cw), f32)] * 4, compiler_params=_cp(("arbitrary",)))(w, g, m, v)


SMALL = (("b_ada", None, PAYW), ("g_attn_pre", OFF_G_ATTN_PRE, D), ("g_attn_post", OFF_G_ATTN_POST, D), ("sink_a", OFF_SINK, 8),
         ("g_mix_a", OFF_G_MIX_A, AQ), ("g_mix_b", OFF_G_MIX_B, BW), ("g_mlp_pre", OFF_G_MLP_PRE, D), ("g_mlp_post", OFF_G_MLP_POST, D))


def _adamw_small(small, gb, params):
    n = len(SMALL)

    def body(*refs):
        small_ref, gb_ref = refs[:2]
        wmv = refs[2:2 + 3 * n]
        loss_ref = refs[2 + 3 * n]
        outs = refs[3 + 3 * n:]
        loss_ref[...] = small_ref[:, OFF_LOSS:OFF_LOSS + 1] * (0.5 / D)
        for i, (_, off, width) in enumerate(SMALL):
            g = gb_ref[...] if off is None else small_ref[:, off:off + width]
            w_ref, m_ref, v_ref = wmv[3 * i:3 * i + 3]
            outs[4 * i][...] = g
            outs[4 * i + 1][...], outs[4 * i + 2][...], outs[4 * i + 3][...] = _adamw_math(w_ref[...], g, m_ref[...], v_ref[...])

    vm = pl.BlockSpec(memory_space=pltpu.VMEM)
    out_shape = [jax.ShapeDtypeStruct((1, 1), f32)]
    for _, _, width in SMALL:
        out_shape += [jax.ShapeDtypeStruct((1, width), f32)] * 4
    flat = [a for wmv in params for a in wmv]
    res = pl.pallas_call(body, name="adamw_small", in_specs=[vm] * (2 + 3 * n), out_specs=[vm] * len(out_shape),
                         out_shape=out_shape)(small, gb, *flat)
    return res[0], {name: res[1 + 4 * i:5 + 4 * i] for i, (name, _, _) in enumerate(SMALL)}


def kernel(x, c, positions, w_ada, b_ada, g_attn_pre, g_attn_post, w_in, sink_a, g_mix_a, g_mix_b, w_out, g_mlp_pre, g_mlp_post, w_up, w_down, loss_target, m_w_ada, m_b_ada, m_g_attn_pre, m_g_attn_post, m_w_in, m_sink_a, m_g_mix_a, m_g_mix_b, m_w_out, m_g_mlp_pre, m_g_mlp_post, m_w_up, m_w_down, v_w_ada, v_b_ada, v_g_attn_pre, v_g_attn_post, v_w_in, v_sink_a, v_g_mix_a, v_g_mix_b, v_w_out, v_g_mlp_pre, v_g_mlp_post, v_w_up, v_w_down):
    given = dict(w_ada=w_ada, b_ada=b_ada, g_attn_pre=g_attn_pre, g_attn_post=g_attn_post, w_in=w_in, sink_a=sink_a, g_mix_a=g_mix_a,
                 g_mix_b=g_mix_b, w_out=w_out, g_mlp_pre=g_mlp_pre, g_mlp_post=g_mlp_post, w_up=w_up, w_down=w_down)
    moms = dict(w_ada=(m_w_ada, v_w_ada), b_ada=(m_b_ada, v_b_ada), g_attn_pre=(m_g_attn_pre, v_g_attn_pre),
                g_attn_post=(m_g_attn_post, v_g_attn_post), w_in=(m_w_in, v_w_in), sink_a=(m_sink_a, v_sink_a),
                g_mix_a=(m_g_mix_a, v_g_mix_a), g_mix_b=(m_g_mix_b, v_g_mix_b), w_out=(m_w_out, v_w_out),
                g_mlp_pre=(m_g_mlp_pre, v_g_mlp_pre), g_mlp_post=(m_g_mlp_post, v_g_mlp_post), w_up=(m_w_up, v_w_up),
                w_down=(m_w_down, v_w_down))
    order = ["w_ada", "b_ada", "g_attn_pre", "g_attn_post", "w_in", "sink_a", "g_mix_a", "g_mix_b", "w_out", "g_mlp_pre",
             "g_mlp_post", "w_up", "w_down"]
    xi, yi, ci = _pos()
    chip = 2 * xi + yi

    c_arr = jnp.reshape(ci, (1,)).astype(jnp.int32)
    pos_arr = jnp.stack([chip, ci]).astype(jnp.int32)
    big = ("w_in", "w_out", "w_up", "w_down")

    b_cols = lax.dynamic_slice(b_ada, (0, chip * ADAW), (1, ADAW))
    mod, cond_all = _ada_fwd(c, w_ada[0], b_cols)
    gathered = [jax.ShapeDtypeStruct((NCHIP,) + given[n].shape[1:], bf16) for n in big]
    flight_in, (mod,) = _split_start("weights_start_first", [w_in[0].astype(bf16)], gathered[:1], [_weight_copies], [mod])
    mod, rest = lax.optimization_barrier((mod, [given[n][0] for n in big[1:]]))
    flight_rest, (mod, inv_lane) = _split_start("weights_start_rest", [w.astype(bf16) for w in rest], gathered[1:],
                                                [_weight_copies] * 3, [mod, _inv_lane()])
    mod = mod.reshape(BL, NMOD, D)

    def first_weight(after):
        srcs, lands = _split_wait("weights_wait_first", flight_in, [_weight_copies], after)
        cross, _ = _split_start("weights_pair_start_first", srcs, None, [_pair_weight_copies], [], lands=lands)
        _, (win_g,) = _split_wait("weights_pair_wait_first", cross, [_pair_weight_copies], ())
        return win_g

    def later_weights(after, carry):
        srcs, lands = _split_wait("weights_wait_rest", flight_rest, [_weight_copies] * 3, after)
        fl, (carry,) = _split_start("weights_pair_start_rest", srcs, None, [_pair_weight_copies] * 3, [carry], lands=lands)
        _, (wout_g,) = _split_wait("weights_pair_wait_out", fl[:1], [_pair_weight_copies], ())

        def mlp_weights(after):
            _, (wup_g, wdn_g) = _split_wait("weights_pair_wait_mlp", fl[1:], [_pair_weight_copies] * 2, after)
            return wup_g, wdn_g.reshape(DFF, D)

        return wout_g.reshape(D, D), mlp_weights, carry

    crossing, pending = {}, {}

    def grad_ready(group, g, carry):
        if group != "w_in_w_out":
            slab = g.reshape(NCHIP, DFF // NCHIP, D) if group == "w_down" else g
            land = jax.ShapeDtypeStruct((NDIRECT, slab.shape[1] // 2, slab.shape[2]), bf16)
            fl, (carry,) = _split_start("grad_start_" + group, [slab], [land], [_direct_grad_copies], [carry])
            pending[group] = ((group,), fl, [_direct_grad_copies])
            return carry
        names = ("w_in", "w_out")
        slabs = [g[0], g[1].reshape(NCHIP, D // NCHIP, D)]
        fl, (carry,) = _split_start("grad_pair_start_" + group, slabs,
                                    [jax.ShapeDtypeStruct((NCHIP, s.shape[1] // 2, s.shape[2]), f32) for s in slabs],
                                    [_pair_grad_copies] * len(names), [carry])
        crossing[group] = (names, fl)
        return carry

    def grad_reduce(group, after, carry):
        names, fl = crossing[group]
        slabs, landed = _split_wait("grad_pair_wait_" + group, fl, [_pair_grad_copies] * len(names), after)
        halves = [_pair_add(s, p, c_arr, "grad_pair_sum_" + n) for s, p, n in zip(slabs, landed, names)]
        fl, (carry,) = _split_start("grad_start_" + group, halves,
                                    [jax.ShapeDtypeStruct((NLINK,) + h.shape[1:], bf16) for h in halves],
                                    [_grad_copies] * len(names), [carry])
        pending[group] = (names, fl, [_grad_copies] * len(names))
        return carry

    grad_x, accs = _local_step(x, positions, mod, loss_target, inv_lane, first_weight, later_weights, grad_ready,
                               g_attn_pre, g_attn_post, sink_a, g_mix_a, g_mix_b, g_mlp_pre, g_mlp_post)

    grads, out = {}, {}

    def update(n):
        tr = (lambda a: a.T) if n == "w_in" else (lambda a: a)
        res = _adamw(tr(given[n][0]), tr(grads[n]), tr(moms[n][0][0]), tr(moms[n][1][0]), "adamw_" + n)
        out[n] = tuple(tr(a)[None] for a in res)
        return res[3]

    def finish(groups, after):
        names = sum((pending[g][0] for g in groups), ())
        fl = sum((pending[g][1] for g in groups), [])
        halves, landed = _split_wait("grad_wait_" + groups[0], fl, sum((pending[g][2] for g in groups), []), after)
        flights = []
        for h, q, n in zip(halves, landed, names):
            full = _chip_add(h, q, pos_arr, "grad_chip_sum_" + n)
            flights.append(_split_start("grad_gather_start_" + n, [jnp.zeros((8, LANES), f32)], None, [_pair_gather_copies],
                                        [], lands=[full])[0])
        last = None
        for n, fl1 in zip(names, flights):
            after = (flights[-1][0][0],) if last is None and fl1 is not flights[-1] else () if last is None else (last,)
            _, (grads[n],) = _split_wait("grad_gather_wait_" + n, fl1, [_pair_gather_copies], after)
            last = update(n)
        return last

    grads["w_ada"], gb, small = _small_allreduce(accs, cond_all)
    small = grad_reduce("w_in_w_out", (small,), small)
    last = finish(("w_down", "w_up"), (small,))
    finish(("w_in_w_out",), (last, update("w_ada")))
    loss, res = _adamw_small(small, gb, [(given[n], moms[n][0], moms[n][1]) for n, _, _ in SMALL])
    for n, _, _ in SMALL:
        out[n] = tuple(res[n])
    return (loss.reshape(()), grad_x, *[out[n][0] for n in order], *[out[n][1] for n in order],
            *[out[n][2] for n in order], *[out[n][3] for n in order])
```

```python
import functools

import numpy as np
import jax
import jax.numpy as jnp
from jax import lax
from jax.experimental import pallas as pl
from jax.experimental.pallas import tpu as pltpu

f32 = jnp.float32
bf16 = jnp.bfloat16
MESH = pl.DeviceIdType.MESH

D = 1024
SEQ = 2048
BL = 2
HD = 64
AQ = 512
AKV = 128
BW = 512
INW = 2304
DFF = 4096
NMOD = 6
ROT = 16
THETA = 500000.0
EPS = 1e-6
NEG = -1e30
BLK = 128
TM = 512
NJ = SEQ // TM
LANES = 128
NCHIP = 4
NDEV = 8
VMEM_LIMIT = 56 << 20

LR, B1, B2, AEPS, WD, STEP = 0.001, 0.9, 0.999, 1e-08, 0.01, 10

OFF_G_ATTN_PRE, OFF_G_ATTN_POST, OFF_G_MIX_A, OFF_G_MIX_B = 0, 1024, 2048, 2560
OFF_G_MLP_PRE, OFF_G_MLP_POST, OFF_SINK, OFF_LOSS = 3072, 4096, 5120, 5248
PAYW = NMOD * D


def _cp(sem=None):
    return pltpu.CompilerParams(dimension_semantics=sem, vmem_limit_bytes=VMEM_LIMIT)


def _dot(a, b):
    return jnp.dot(a, b, preferred_element_type=f32)


def _dot_nt(a, b):
    return lax.dot_general(a, b, (((1,), (1,)), ((), ())), preferred_element_type=f32)


def _dot_tn(a, b):
    return lax.dot_general(a, b, (((0,), (0,)), ((), ())), preferred_element_type=f32)


def _rms(x):
    r = lax.rsqrt(jnp.mean(x * x, axis=-1, keepdims=True) + EPS)
    return x * r, r


def _rms_bwd(dy, y, r):
    return r * (dy - y * jnp.mean(dy * y, axis=-1, keepdims=True))


def _colsum(v):
    return jnp.sum(v, axis=0, keepdims=True)


def _rope(p, c, s1, s2):
    outs = []
    for c0 in range(0, p.shape[1], LANES):
        pc = p[:, c0:c0 + LANES]
        outs.append(pc * c + pltpu.roll(pc, LANES - ROT // 2, 1) * s1 + pltpu.roll(pc, ROT // 2, 1) * s2)
    return outs[0] if len(outs) == 1 else jnp.concatenate(outs, axis=1)


def _rope_t(g, c, s1, s2):
    outs = []
    for c0 in range(0, g.shape[1], LANES):
        gc = g[:, c0:c0 + LANES]
        outs.append(gc * c + pltpu.roll(gc * s1, ROT // 2, 1) + pltpu.roll(gc * s2, LANES - ROT // 2, 1))
    return outs[0] if len(outs) == 1 else jnp.concatenate(outs, axis=1)


def _perm_store(val, scr, out_ref, d):
    nc = val.shape[1] // LANES
    for c in range(nc):
        scr[c] = val[:, LANES * c:LANES * (c + 1)]
    for c in range(nc):
        for r in range(d):
            out_ref[r, :, LANES * c:LANES * (c + 1)] = scr[c, pl.ds(r, TM // d, stride=d), :].astype(out_ref.dtype)


def _perm_load(in_ref, scr, d):
    nc = in_ref.shape[-1] // LANES
    for c in range(nc):
        for r in range(d):
            scr[c, pl.ds(r, TM // d, stride=d), :] = in_ref[r, :, LANES * c:LANES * (c + 1)].astype(f32)
    return jnp.concatenate([scr[c] for c in range(nc)], axis=1)


def _per_query_head(kv):
    r = pltpu.roll(kv, HD, 1)
    lo = lax.broadcasted_iota(jnp.int32, kv.shape, 1) < HD
    first, second = jnp.where(lo, kv, r), jnp.where(lo, r, kv)
    return jnp.concatenate([first, first, second, second], axis=1)


def _per_kv_head(g):
    g0, g1 = g[:, :LANES] + g[:, LANES:2 * LANES], g[:, 2 * LANES:3 * LANES] + g[:, 3 * LANES:]
    lo = lax.broadcasted_iota(jnp.int32, g0.shape, 1) < HD
    return jnp.where(lo, g0 + pltpu.roll(g0, HD, 1), g1 + pltpu.roll(g1, HD, 1))


def _tok(w, dtype=None):
    return pl.BlockSpec((None, TM, w), lambda b, j: (b, j, 0))


def _perm_spec(d, w):
    return pl.BlockSpec((None, d, TM // d, w), lambda b, j: (b, 0, j, 0))


def _full(shape):
    n = len(shape)
    return pl.BlockSpec(shape, lambda b, j: (0,) * n)


MOD_SPEC = pl.BlockSpec((None, NMOD, D), lambda b, j: (b, 0, 0))
ACCB_SPEC = pl.BlockSpec((None, 8, D), lambda b, j: (b, 0, 0))
ACCG_SPEC = pl.BlockSpec((8, D), lambda b, j: (0, 0))
ACC_SHAPES = [jax.ShapeDtypeStruct((BL, 8, D), f32), jax.ShapeDtypeStruct((8, D), f32)]


def _acc_init(accb_ref, accg_ref):
    b, j = pl.program_id(0), pl.program_id(1)

    @pl.when(j == 0)
    def _():
        accb_ref[...] = jnp.zeros_like(accb_ref)

    @pl.when((b == 0) & (j == 0))
    def _():
        accg_ref[...] = jnp.zeros_like(accg_ref)


def _rope_tables(pos_col, inv_lane):
    def body(p_ref, inv_ref, c_ref, s1_ref, s2_ref):
        ang = p_ref[...].astype(f32) * inv_ref[...]
        j = lax.broadcasted_iota(jnp.int32, (TM, LANES), 1) % HD
        cs, sn = jnp.cos(ang), jnp.sin(ang)
        c_ref[...] = jnp.where(j < ROT, cs, 1.0)
        s1_ref[...] = jnp.where(j < ROT // 2, -sn, 0.0)
        s2_ref[...] = jnp.where((j >= ROT // 2) & (j < ROT), sn, 0.0)

    n = BL * SEQ // TM
    return pl.pallas_call(
        body, name="rope_tables", grid=(n,),
        in_specs=[pl.BlockSpec((TM, 1), lambda i: (i, 0)), pl.BlockSpec((1, LANES), lambda i: (0, 0))],
        out_specs=[pl.BlockSpec((TM, LANES), lambda i: (i, 0))] * 3,
        out_shape=[jax.ShapeDtypeStruct((BL * SEQ, LANES), f32)] * 3,
    )(pos_col, inv_lane)


def _attn_in(x, mod, g_pre, w_in, tc, ts1, ts2):
    def body(x_ref, mod_ref, g_ref, wg_ref, c_ref, s1_ref, s2_ref,
             h_ref, qa_ref, ka_ref, va_ref, q1_ref, k1_ref, v1_ref, q4_ref, k4_ref, v4_ref, q16_ref, k16_ref, v16_ref,
             w_ref, scr):
        @pl.when((pl.program_id(0) == 0) & (pl.program_id(1) == 0))
        def _():
            w_ref[...] = jnp.concatenate([wg_ref[s] for s in range(NCHIP)], axis=1)

        xn, _ = _rms(x_ref[...])
        h = (xn * g_ref[...]) * (1.0 + mod_ref[1:2, :]) + mod_ref[0:1, :]
        hb = h.astype(bf16)
        h_ref[...] = hb
        proj = _dot(hb, w_ref[...])
        c, s1, s2 = c_ref[...], s1_ref[...], s2_ref[...]
        o1, o2, o3, o4, o5 = AQ, AQ + AKV, AQ + 2 * AKV, AQ + 2 * AKV + BW, AQ + 2 * AKV + 2 * BW
        qa_ref[...] = (_rope(proj[:, :o1], c, s1, s2) * 0.125).astype(bf16)
        ka_ref[...] = _per_query_head(_rope(proj[:, o1:o2], c, s1, s2)).astype(bf16)
        va_ref[...] = _per_query_head(proj[:, o2:o3]).astype(bf16)
        qb = _rope(proj[:, o3:o4], c, s1, s2) * 0.125
        kb = _rope(proj[:, o4:o5], c, s1, s2)
        vb = proj[:, o5:]
        for val, r1, r4, r16 in ((qb, q1_ref, q4_ref, q16_ref), (kb, k1_ref, k4_ref, k16_ref), (vb, v1_ref, v4_ref, v16_ref)):
            r1[...] = val.astype(bf16)
            _perm_store(val, scr, r4, 4)
            _perm_store(val, scr, r16, 16)

    nat = lambda w: jax.ShapeDtypeStruct((BL, SEQ, w), bf16)
    p4 = jax.ShapeDtypeStruct((BL, 4, SEQ // 4, BW), bf16)
    p16 = jax.ShapeDtypeStruct((BL, 16, SEQ // 16, BW), bf16)
    return pl.pallas_call(
        body, name="attn_in", grid=(BL, NJ),
        in_specs=[_tok(D), MOD_SPEC, _full((1, D)), _full((NCHIP, D, INW // NCHIP)), _tok(LANES), _tok(LANES), _tok(LANES)],
        out_specs=([_tok(D), _tok(AQ), _tok(AQ), _tok(AQ)] + [_tok(BW)] * 3 + [_perm_spec(4, BW)] * 3 + [_perm_spec(16, BW)] * 3
                   + [_full((D, INW))]),
        out_shape=[nat(D), nat(AQ), nat(AQ), nat(AQ)] + [nat(BW)] * 3 + [p4] * 3 + [p16] * 3
                  + [jax.ShapeDtypeStruct((D, INW), bf16)],
        scratch_shapes=[pltpu.VMEM((BW // LANES, TM, LANES), f32)],
        compiler_params=_cp(("arbitrary", "arbitrary")),
    )(x, mod, g_pre, w_in, tc, ts1, ts2)


def _kv_cat(cur_ref, prev_ref, p, cache):
    key = (id(cur_ref), p)
    if key not in cache:
        sl = slice(LANES * p, LANES * (p + 1))
        cache[key] = cur_ref[:, sl] if prev_ref is None else jnp.concatenate([prev_ref[:, sl], cur_ref[:, sl]], axis=0)
    return cache[key]


def _lane_half(a, hh):
    lo = lax.broadcasted_iota(jnp.int32, a.shape, 1) < HD
    return jnp.where(lo, a, jnp.zeros_like(a)) if hh == 0 else jnp.where(lo, jnp.zeros_like(a), a)


ATT_UNITS = 4


def _attn_specs(n, nb, descending):
    u = ATT_UNITS
    if nb == 1:
        return (lambda ww: pl.BlockSpec((u, BLK, ww), lambda a, i: (a, 0, 0))), None, (n // u, 1)
    steps = nb // u
    at = (lambda i: steps - 1 - i) if descending else (lambda i: i)
    cur = lambda ww: pl.BlockSpec((None, u * BLK, ww), lambda a, i: (a, at(i), 0))
    prev = lambda ww: pl.BlockSpec((None, BLK, ww), lambda a, i: (a, jnp.maximum(u * at(i) - 1, 0), 0))
    return cur, prev, (n, steps)


def _attn_fwd(q, k, v, sink, *, max_dist, o_dtype, name):
    n, l, w = q.shape
    wk = k.shape[-1]
    nb = l // BLK
    has_sink = sink is not None

    def body(*refs):
        sink_ref = None
        if has_sink:
            sink_ref, refs = refs[0], refs[1:]
        if nb > 1:
            q_ref, kc_ref, kp_ref, vc_ref, vp_ref, o_ref, lse_ref = refs[:7]
            first = pl.program_id(1) == 0
            for u in range(ATT_UNITS):
                rows, before = pl.ds(BLK * u, BLK), pl.ds(BLK * (u - 1), BLK)
                unit(q_ref.at[rows, :], kc_ref.at[rows, :], kp_ref if u == 0 else kc_ref.at[before, :],
                     vc_ref.at[rows, :], vp_ref if u == 0 else vc_ref.at[before, :], o_ref.at[rows, :], lse_ref.at[rows, :],
                     jnp.logical_not(first) if u == 0 else True, sink_ref, *refs[7:])
        else:
            q_ref, kc_ref, vc_ref, o_ref, lse_ref = refs[:5]
            for u in range(ATT_UNITS):
                unit(q_ref.at[u], kc_ref.at[u], None, vc_ref.at[u], None, o_ref.at[u], lse_ref.at[u], None, sink_ref, *refs[5:])

    def unit(q_ref, kc_ref, kp_ref, vc_ref, vp_ref, o_ref, lse_ref, has_prev, sink_ref, sscr, pscr, dscr):
        qi = lax.broadcasted_iota(jnp.int32, (BLK, BLK), 0)
        kj = lax.broadcasted_iota(jnp.int32, (BLK, BLK), 1)
        tri = kj <= qi
        eye = kj == qi
        cache = {}
        for p in range(w // LANES):
            qpair = q_ref[:, LANES * p:LANES * (p + 1)]
            kcat = _kv_cat(kc_ref, kp_ref, p, cache)
            for hh in range(2):
                s = _dot_nt(_lane_half(qpair, hh), kcat)
                if nb > 1:
                    sp = s[:, :BLK] if has_prev is True else jnp.where(has_prev, s[:, :BLK], NEG)
                    sscr[2 * p + hh] = jnp.where(tri, s[:, BLK:], sp)
                    if diag:
                        dscr[2 * p + hh] = jnp.where(eye, sp, NEG)
                else:
                    sscr[2 * p + hh] = jnp.where(tri, s, NEG)
        lane = lax.broadcasted_iota(jnp.int32, (BLK, LANES), 1)
        lse_all = jnp.zeros((BLK, LANES), f32)
        for p in range(w // LANES):
            for hh in range(2):
                h = 2 * p + hh
                comb = sscr[h]
                if diag:
                    dtile = dscr[h]
                    m = jnp.max(jnp.maximum(comb, dtile), axis=-1, keepdims=True)
                else:
                    m = jnp.max(comb, axis=-1, keepdims=True)
                if has_sink:
                    sk = sink_ref[0, h]
                    m = jnp.maximum(m, sk)
                e = jnp.exp(comb - m)
                if diag:
                    ed = jnp.exp(dtile - m)
                    den = jnp.sum(e + ed, axis=-1, keepdims=True)
                else:
                    den = jnp.sum(e, axis=-1, keepdims=True)
                if has_sink:
                    den = den + jnp.exp(sk - m)
                inv = 1.0 / den
                if nb > 1:
                    pscr[h, :, :BLK] = (jnp.where(tri, ed if diag else 0.0, e) * inv).astype(bf16)
                    pscr[h, :, BLK:] = (jnp.where(tri, e, 0.0) * inv).astype(bf16)
                else:
                    pscr[h] = (e * inv).astype(bf16)
                lse_all = jnp.where(lane == h, jnp.broadcast_to(m + jnp.log(den), (BLK, LANES)), lse_all)
        lse_ref[...] = lse_all
        for p in range(w // LANES):
            vcat = _kv_cat(vc_ref, vp_ref, p, cache)
            o_ref[:, LANES * p:LANES * (p + 1)] = (_dot(pscr[2 * p], _lane_half(vcat, 0))
                                                   + _dot(pscr[2 * p + 1], _lane_half(vcat, 1))).astype(o_ref.dtype)

    assert max_dist in (BLK - 1, BLK) and k.shape == q.shape
    diag = nb > 1 and max_dist == BLK
    cur, prev, grid = _attn_specs(n, nb, False)
    in_specs = [cur(w), cur(wk)] + ([prev(wk)] if nb > 1 else []) + [cur(wk)] + ([prev(wk)] if nb > 1 else [])
    args = [q, k] + ([k] if nb > 1 else []) + [v] + ([v] if nb > 1 else [])
    if has_sink:
        in_specs = [pl.BlockSpec(memory_space=pltpu.SMEM)] + in_specs
        args = [sink] + args
    return pl.pallas_call(
        body, name=name, grid=grid, in_specs=in_specs,
        out_specs=[cur(w), cur(LANES)],
        out_shape=[jax.ShapeDtypeStruct((n, l, w), o_dtype), jax.ShapeDtypeStruct((n, l, LANES), f32)],
        scratch_shapes=[pltpu.VMEM((w // HD, BLK, BLK), f32), pltpu.VMEM((w // HD, BLK, 2 * BLK if nb > 1 else BLK), bf16),
                        pltpu.VMEM((w // HD if diag else 1, BLK, BLK), f32)],
        compiler_params=_cp(("arbitrary", "arbitrary")),
    )(*args)


def _attn_bwd(q, k, v, do, delta, lse, sink, *, max_dist, name):
    n, l, w = q.shape
    wk = k.shape[-1]
    nb = l // BLK
    has_sink = sink is not None

    def body(*refs):
        sink_ref = dsink_ref = ck = cv = None
        if has_sink:
            sink_ref, refs = refs[0], refs[1:]
        nin = 8 if nb > 1 else 6
        ins, rest = refs[:nin], refs[nin:]
        if has_sink:
            dq_ref, dk_ref, dv_ref, dsink_ref = rest[:4]
            rest = rest[4:]
        else:
            dq_ref, dk_ref, dv_ref = rest[:3]
            rest = rest[3:]
        step = pl.program_id(1)
        if has_sink:
            @pl.when((pl.program_id(0) == 0) & (step == 0))
            def _():
                dsink_ref[...] = jnp.zeros_like(dsink_ref)

        if nb > 1:
            q_ref, kc_ref, kp_ref, vc_ref, vp_ref, do_ref, delta_ref, lse_ref = ins
            ck, cv = rest[:2]

            @pl.when(step == 0)
            def _():
                ck[...] = jnp.zeros_like(ck)
                cv[...] = jnp.zeros_like(cv)

            last = step == nb // ATT_UNITS - 1
            for u in reversed(range(ATT_UNITS)):
                rows, before = pl.ds(BLK * u, BLK), pl.ds(BLK * (u - 1), BLK)
                unit(q_ref.at[rows, :], kc_ref.at[rows, :], kp_ref if u == 0 else kc_ref.at[before, :],
                     vc_ref.at[rows, :], vp_ref if u == 0 else vc_ref.at[before, :], do_ref.at[rows, :],
                     delta_ref.at[rows, :], lse_ref.at[rows, :], dq_ref.at[rows, :], dk_ref.at[rows, :], dv_ref.at[rows, :],
                     jnp.logical_not(last) if u == 0 else True, sink_ref, dsink_ref, ck, cv, *rest[2:])
        else:
            q_ref, kc_ref, vc_ref, do_ref, delta_ref, lse_ref = ins
            for u in range(ATT_UNITS):
                unit(q_ref.at[u], kc_ref.at[u], None, vc_ref.at[u], None, do_ref.at[u], delta_ref.at[u], lse_ref.at[u],
                     dq_ref.at[u], dk_ref.at[u], dv_ref.at[u], None, sink_ref, dsink_ref, None, None, *rest)

    def unit(q_ref, kc_ref, kp_ref, vc_ref, vp_ref, do_ref, delta_ref, lse_ref, dq_ref, dk_ref, dv_ref, has_prev,
             sink_ref, dsink_ref, ck, cv, sscr, dpscr, pscr, dsscr, dscr=None, ddscr=None):
        lane = lax.broadcasted_iota(jnp.int32, (BLK, LANES), 1)
        qi = lax.broadcasted_iota(jnp.int32, (BLK, BLK), 0)
        kj = lax.broadcasted_iota(jnp.int32, (BLK, BLK), 1)
        tri = kj <= qi
        eye = kj == qi
        cache = {}
        kp, vp = kp_ref, vp_ref
        for p in range(w // LANES):
            sl = slice(LANES * p, LANES * (p + 1))
            qpair, dopair = q_ref[:, sl], do_ref[:, sl]
            kcat, vcat = _kv_cat(kc_ref, kp, p, cache), _kv_cat(vc_ref, vp, p, cache)
            for hh in range(2):
                h = 2 * p + hh
                s = _dot_nt(_lane_half(qpair, hh), kcat)
                dp = _dot_nt(_lane_half(dopair, hh), vcat)
                if nb > 1:
                    sp = s[:, :BLK] if has_prev is True else jnp.where(has_prev, s[:, :BLK], NEG)
                    sscr[h] = jnp.where(tri, s[:, BLK:], sp)
                    dpscr[h] = jnp.where(tri, dp[:, BLK:], dp[:, :BLK])
                    if diag:
                        dscr[h] = jnp.where(eye, sp, NEG)
                        ddscr[h] = dp[:, :BLK]
                else:
                    sscr[h] = jnp.where(tri, s, NEG)
                    dpscr[h] = dp
        for p in range(w // LANES):
            for hh in range(2):
                h = 2 * p + hh
                lse_b = jnp.broadcast_to(lse_ref[:, h:h + 1], (BLK, BLK))
                delta = jnp.broadcast_to(delta_ref[:, h:h + 1], (BLK, BLK))
                pr = jnp.exp(sscr[h] - lse_b)
                ds = pr * (dpscr[h] - delta)
                if nb > 1:
                    if diag:
                        prd = jnp.exp(dscr[h] - lse_b)
                        dsd = prd * (ddscr[h] - delta)
                    else:
                        prd = dsd = 0.0
                    pscr[h, :, :BLK] = jnp.where(tri, prd, pr).astype(bf16)
                    pscr[h, :, BLK:] = jnp.where(tri, pr, 0.0).astype(bf16)
                    dsscr[h, :, :BLK] = jnp.where(tri, dsd, ds).astype(bf16)
                    dsscr[h, :, BLK:] = jnp.where(tri, ds, 0.0).astype(bf16)
                else:
                    pscr[h] = pr.astype(bf16)
                    dsscr[h] = ds.astype(bf16)
                if has_sink:
                    dsk = -jnp.sum(jnp.where(lane == 0, jnp.exp(sink_ref[0, h] - lse_b) * delta, 0.0), keepdims=True)
                    dsink_ref[h:h + 1, :] += jnp.broadcast_to(dsk, (1, LANES))
        for p in range(w // LANES):
            sl = slice(LANES * p, LANES * (p + 1))
            qpair, dopair = q_ref[:, sl], do_ref[:, sl]
            kcat = _kv_cat(kc_ref, kp, p, cache)
            dq_ref[:, sl] = _dot(dsscr[2 * p], _lane_half(kcat, 0)) + _dot(dsscr[2 * p + 1], _lane_half(kcat, 1))
            dk_pair = _dot_tn(dsscr[2 * p], _lane_half(qpair, 0)) + _dot_tn(dsscr[2 * p + 1], _lane_half(qpair, 1))
            dv_pair = _dot_tn(pscr[2 * p], _lane_half(dopair, 0)) + _dot_tn(pscr[2 * p + 1], _lane_half(dopair, 1))
            if nb > 1:
                dk_ref[:, sl] = dk_pair[BLK:] + ck[:, sl]
                dv_ref[:, sl] = dv_pair[BLK:] + cv[:, sl]
                ck[:, sl] = dk_pair[:BLK]
                cv[:, sl] = dv_pair[:BLK]
            else:
                dk_ref[:, sl] = dk_pair
                dv_ref[:, sl] = dv_pair

    assert max_dist in (BLK - 1, BLK) and k.shape == q.shape
    diag = nb > 1 and max_dist == BLK
    cur, prev, grid = _attn_specs(n, nb, True)
    in_specs = ([cur(w), cur(wk)] + ([prev(wk)] if nb > 1 else []) + [cur(wk)] + ([prev(wk)] if nb > 1 else [])
                + [cur(w), cur(LANES), cur(LANES)])
    args = [q, k] + ([k] if nb > 1 else []) + [v] + ([v] if nb > 1 else []) + [do, delta, lse]
    out_specs = [cur(w), cur(wk), cur(wk)]
    out_shape = [jax.ShapeDtypeStruct((n, l, w), f32), jax.ShapeDtypeStruct((n, l, wk), f32), jax.ShapeDtypeStruct((n, l, wk), f32)]
    if has_sink:
        in_specs = [pl.BlockSpec(memory_space=pltpu.SMEM)] + in_specs
        args = [sink] + args
        out_specs.append(pl.BlockSpec((8, LANES), lambda a, i: (0, 0)))
        out_shape.append(jax.ShapeDtypeStruct((8, LANES), f32))
    nh = w // HD
    scratch = [pltpu.VMEM((BLK, wk), f32), pltpu.VMEM((BLK, wk), f32)] if nb > 1 else []
    scratch += [pltpu.VMEM((nh, BLK, BLK), f32)] * 2 + [pltpu.VMEM((nh, BLK, 2 * BLK if nb > 1 else BLK), bf16)] * 2
    if diag:
        scratch += [pltpu.VMEM((nh, BLK, BLK), f32)] * 2
    return pl.pallas_call(
        body, name=name, grid=grid, in_specs=in_specs, out_specs=out_specs, out_shape=out_shape,
        scratch_shapes=scratch, compiler_params=_cp(("arbitrary", "arbitrary")),
    )(*args)


def _split2(x):
    hi = x.astype(bf16)
    return hi, (x - hi.astype(f32)).astype(bf16)


def _heads_to_lanes(xc, e):
    return sum(_dot(t, e) for t in _split2(xc))


def _lanes_to_heads(x, g):
    return sum(_dot(t, g) for t in _split2(x))


HEAD_EXPAND = (np.arange(LANES)[:, None] == np.arange(BW)[None, :] // HD).astype(np.float32)
HEAD_SUM = HEAD_EXPAND.T.copy()


def _branch_weights(l1_ref, l4_ref, l16_ref, scr):
    l4v = _perm_load(l4_ref, scr, 4)
    l16v = _perm_load(l16_ref, scr, 16)
    l1v = l1_ref[...]
    m = jnp.maximum(jnp.maximum(l1v, l4v), l16v)
    e1, e4, e16 = jnp.exp(l1v - m), jnp.exp(l4v - m), jnp.exp(l16v - m)
    z = e1 + e4 + e16
    return e1 / z, e4 / z, e16 / z


def _mix_out(oa, o1, l1, o4, l4, o16, l16, g_mix_a, g_mix_b, w_out, x, mod, g_post):
    def body(oa_ref, o1_ref, l1_ref, o4_ref, l4_ref, o16_ref, l16_ref, ga_ref, gb_ref, w_ref, x_ref, mod_ref, gp_ref, e_ref,
             x1_ref, y_ref, mixed_ref, ob_ref, scr):
        w1, w4, w16 = _branch_weights(l1_ref, l4_ref, l16_ref, scr)
        e = e_ref[...]
        x1w, x4w = _heads_to_lanes(w1, e), _heads_to_lanes(w4, e)
        ob = (x1w * o1_ref[...].astype(f32) + x4w * _perm_load(o4_ref, scr, 4)
              + (1.0 - x1w - x4w) * _perm_load(o16_ref, scr, 16))
        ob_ref[...] = ob
        oan, _ = _rms(oa_ref[...])
        obn, _ = _rms(ob)
        mixed = jnp.concatenate([oan * ga_ref[...], obn * gb_ref[...]], axis=1).astype(bf16)
        mixed_ref[...] = mixed
        y = _dot(mixed, w_ref[...])
        y_ref[...] = y
        yn, _ = _rms(y)
        x1_ref[...] = x_ref[...] + mod_ref[2:3, :] * (yn * gp_ref[...])

    nat = lambda w, dt: jax.ShapeDtypeStruct((BL, SEQ, w), dt)
    return pl.pallas_call(
        body, name="mix_out", grid=(BL, NJ),
        in_specs=[_tok(AQ), _tok(BW), _tok(LANES), _perm_spec(4, BW), _perm_spec(4, LANES), _perm_spec(16, BW),
                  _perm_spec(16, LANES), _full((1, AQ)), _full((1, BW)), _full((D, D)), _tok(D), MOD_SPEC, _full((1, D)),
                  _full((LANES, BW))],
        out_specs=[_tok(D), _tok(D), _tok(D), _tok(BW)],
        out_shape=[nat(D, f32), nat(D, f32), nat(D, bf16), nat(BW, f32)],
        scratch_shapes=[pltpu.VMEM((BW // LANES, TM, LANES), f32)],
        compiler_params=_cp(("arbitrary", "arbitrary")),
    )(oa, o1, l1, o4, l4, o16, l16, g_mix_a, g_mix_b, w_out, x, mod, g_post, jnp.asarray(HEAD_EXPAND, bf16))


def _mlp_up(x1, mod, g_pre, w_up):
    def body(x_ref, mod_ref, g_ref, w_ref, h_ref, u_ref, a_ref):
        xn, _ = _rms(x_ref[...])
        h = (xn * g_ref[...]) * (1.0 + mod_ref[4:5, :]) + mod_ref[3:4, :]
        hb = h.astype(bf16)
        h_ref[...] = hb
        for s in range(NCHIP):
            u = _dot(hb, w_ref[s])
            u_ref[:, D * s:D * (s + 1)] = u.astype(bf16)
            a_ref[:, D * s:D * (s + 1)] = jnp.square(jnp.maximum(u, 0.0)).astype(bf16)

    nat = lambda w: jax.ShapeDtypeStruct((BL, SEQ, w), bf16)
    return pl.pallas_call(
        body, name="mlp_up", grid=(BL, NJ),
        in_specs=[_tok(D), MOD_SPEC, _full((1, D)), _full((NCHIP, D, D))],
        out_specs=[_tok(D), _tok(DFF), _tok(DFF)], out_shape=[nat(D), nat(DFF), nat(DFF)],
        compiler_params=_cp(("arbitrary", "arbitrary")),
    )(x1, mod, g_pre, w_up)


def _mlp_down(a, w_down, x1, target, mod, g_post):
    def body(a_ref, w_ref, x_ref, t_ref, mod_ref, g_ref, gx_ref, dy_ref, accb_ref, accg_ref):
        _acc_init(accb_ref, accg_ref)
        y2 = _dot(a_ref[...], w_ref[...])
        yn, r = _rms(y2)
        g = g_ref[...]
        gt = mod_ref[5:6, :]
        n2 = yn * g
        err = x_ref[...] + gt * n2 - t_ref[...]
        gout = err * (1.0 / D)
        gx_ref[...] = gout
        dn2 = gout * gt
        dy_ref[...] = _rms_bwd(dn2 * g, yn, r).astype(bf16)
        accb_ref[0:1, :] += _colsum(gout * n2)
        accg_ref[0:1, :] += _colsum(dn2 * yn)
        accg_ref[1:2, :] += jnp.broadcast_to(jnp.sum(err * err, keepdims=True), (1, D))

    return pl.pallas_call(
        body, name="mlp_down", grid=(BL, NJ),
        in_specs=[_tok(DFF), _full((DFF, D)), _tok(D), _tok(D), MOD_SPEC, _full((1, D))],
        out_specs=[_tok(D), _tok(D), ACCB_SPEC, ACCG_SPEC],
        out_shape=[jax.ShapeDtypeStruct((BL, SEQ, D), f32), jax.ShapeDtypeStruct((BL, SEQ, D), bf16)] + ACC_SHAPES,
        compiler_params=_cp(("arbitrary", "arbitrary")),
    )(a, w_down, x1, target, mod, g_post)


def _mlp_bwd(dy2, u, w_down, w_up, x1, gx, mod, g_pre):
    def body(dy_ref, u_ref, wd_hbm, wu_hbm, x_ref, gx_ref, mod_ref, g_ref, du_ref, gx1_ref, accb_ref, accg_ref, wd, wu, sem):
        _acc_init(accb_ref, accg_ref)
        first = (pl.program_id(0) == 0) & (pl.program_id(1) == 0)
        c1 = pltpu.make_async_copy(wd_hbm, wd, sem.at[0])
        c2 = pltpu.make_async_copy(wu_hbm, wu, sem.at[1])

        @pl.when(first)
        def _():
            c1.start()
            c2.start()
            c1.wait()

        dy = dy_ref[...]
        for s in range(NCHIP):
            sl = slice(D * s, D * (s + 1))
            da = _dot_nt(dy, wd[sl, :])
            du_ref[:, sl] = (da * (2.0 * jnp.maximum(u_ref[:, sl].astype(f32), 0.0))).astype(bf16)

        @pl.when(first)
        def _():
            c2.wait()

        dh = jnp.zeros((TM, D), f32)
        for s in range(NCHIP):
            dh = dh + _dot_nt(du_ref[:, D * s:D * (s + 1)], wu[s])
        xn, r = _rms(x_ref[...])
        g = g_ref[...]
        n = xn * g
        dn = dh * (1.0 + mod_ref[4:5, :])
        gx1_ref[...] = gx_ref[...] + _rms_bwd(dn * g, xn, r)
        accb_ref[0:1, :] += _colsum(dh * n)
        accb_ref[1:2, :] += _colsum(dh)
        accg_ref[0:1, :] += _colsum(dn * xn)

    anyspec = pl.BlockSpec(memory_space=pl.ANY)
    return pl.pallas_call(
        body, name="mlp_bwd", grid=(BL, NJ),
        in_specs=[_tok(D), _tok(DFF), anyspec, anyspec, _tok(D), _tok(D), MOD_SPEC, _full((1, D))],
        out_specs=[_tok(DFF), _tok(D), ACCB_SPEC, ACCG_SPEC],
        out_shape=[jax.ShapeDtypeStruct((BL, SEQ, DFF), bf16), jax.ShapeDtypeStruct((BL, SEQ, D), f32)] + ACC_SHAPES,
        scratch_shapes=[pltpu.VMEM((DFF, D), bf16), pltpu.VMEM((NCHIP, D, D), bf16), pltpu.SemaphoreType.DMA((2,))],
        compiler_params=_cp(("arbitrary", "arbitrary")),
    )(dy2, u, w_down, w_up, x1, gx, mod, g_pre)


def _matmul_tn(a, b, *, tn, col_blocked, name, out_dtype=f32):
    t, m = a.shape
    n = b.shape[1]
    tmm = min(m, 1024)
    tk = 2048 if tn <= 1024 else 1024
    nk = t // tk

    def body(a_ref, b_ref, o_ref, acc):
        k = pl.program_id(2)

        @pl.when(k == 0)
        def _():
            acc[...] = jnp.zeros_like(acc)

        acc[...] += _dot_tn(a_ref[...], b_ref[...])

        @pl.when(k == nk - 1)
        def _():
            o_ref[...] = acc[...].astype(out_dtype)

    if col_blocked:
        out_spec = pl.BlockSpec((None, tmm, tn), lambda i, j, k: (j, i, 0))
        out_shape = jax.ShapeDtypeStruct((n // tn, m, tn), out_dtype)
    else:
        out_spec = pl.BlockSpec((tmm, tn), lambda i, j, k: (i, j))
        out_shape = jax.ShapeDtypeStruct((m, n), out_dtype)
    return pl.pallas_call(
        body, name=name, grid=(m // tmm, n // tn, nk),
        in_specs=[pl.BlockSpec((tk, tmm), lambda i, j, k: (k, i)), pl.BlockSpec((tk, tn), lambda i, j, k: (k, j))],
        out_specs=out_spec, out_shape=out_shape, scratch_shapes=[pltpu.VMEM((tmm, tn), f32)],
        compiler_params=_cp(("arbitrary", "arbitrary", "arbitrary")),
    )(a, b)


def _grad_w_in(h, dproj):
    t = h.shape[0]
    tk = 1024
    nk = t // tk
    sw = INW // NCHIP

    def body(a_ref, b_ref, o_ref, acc):
        k = pl.program_id(0)

        @pl.when(k == 0)
        def _():
            acc[...] = jnp.zeros_like(acc)

        acc[...] += _dot_tn(a_ref[...], b_ref[...])

        @pl.when(k == nk - 1)
        def _():
            for s in range(NCHIP):
                o_ref[s] = acc[:, sw * s:sw * (s + 1)]

    return pl.pallas_call(
        body, name="grad_w_in", grid=(nk,),
        in_specs=[pl.BlockSpec((tk, D), lambda k: (k, 0)), pl.BlockSpec((tk, INW), lambda k: (k, 0))],
        out_specs=pl.BlockSpec((NCHIP, D, sw), lambda k: (0, 0, 0)), out_shape=jax.ShapeDtypeStruct((NCHIP, D, sw), f32),
        scratch_shapes=[pltpu.VMEM((D, INW), f32)], compiler_params=_cp(("arbitrary",)),
    )(h, dproj)


def _attn_out_bwd(gx1, y, mod, g_post, w_out, oa, ob, g_mix_a, g_mix_b, l1, l4, l16):
    def body(gx_ref, y_ref, mod_ref, gp_ref, w_ref, oa_ref, ob_ref, ga_ref, gb_ref, l1_ref, l4_ref, l16_ref, e_ref, g_ref,
             dy_ref, doa_ref, do1_ref, do4_ref, do16_ref, da_ref, d1_ref, d4_ref, d16_ref, accb_ref, accg_ref, scr):
        _acc_init(accb_ref, accg_ref)
        w1, w4, w16 = _branch_weights(l1_ref, l4_ref, l16_ref, scr)
        e, hs = e_ref[...], g_ref[...]
        gx1v = gx_ref[...]
        yn, ry = _rms(y_ref[...])
        gp = gp_ref[...]
        gt = mod_ref[2:3, :]
        dn1 = gx1v * gt
        dy = _rms_bwd(dn1 * gp, yn, ry).astype(bf16)
        dy_ref[...] = dy
        dmixed = _dot_nt(dy, w_ref[...])
        dma, dmb = dmixed[:, :AQ], dmixed[:, AQ:]
        oa, ob = oa_ref[...], ob_ref[...]
        oan, ra = _rms(oa)
        obn, rb = _rms(ob)
        doa = _rms_bwd(dma * ga_ref[...], oan, ra)
        doa_ref[...] = doa.astype(bf16)
        da_ref[...] = _lanes_to_heads(doa * oa, hs)
        dob = _rms_bwd(dmb * gb_ref[...], obn, rb)
        dd = _lanes_to_heads(dob * ob, hs)
        x1w, x4w = _heads_to_lanes(w1, e), _heads_to_lanes(w4, e)
        do1_ref[...] = (x1w * dob).astype(bf16)
        d1_ref[...] = w1 * dd
        _perm_store(x4w * dob, scr, do4_ref, 4)
        _perm_store(w4 * dd, scr, d4_ref, 4)
        _perm_store((1.0 - x1w - x4w) * dob, scr, do16_ref, 16)
        _perm_store(w16 * dd, scr, d16_ref, 16)
        accb_ref[0:1, :] += _colsum(gx1v * (yn * gp))
        accg_ref[0:1, :] += _colsum(dn1 * yn)
        accg_ref[1:2, :] += jnp.concatenate([_colsum(dma * oan), _colsum(dmb * obn)], axis=1)

    nat = lambda w, dt: jax.ShapeDtypeStruct((BL, SEQ, w), dt)
    return pl.pallas_call(
        body, name="attn_out_bwd", grid=(BL, NJ),
        in_specs=[_tok(D), _tok(D), MOD_SPEC, _full((1, D)), _full((D, D)), _tok(AQ), _tok(BW), _full((1, AQ)), _full((1, BW)),
                  _tok(LANES), _perm_spec(4, LANES), _perm_spec(16, LANES), _full((LANES, BW)), _full((BW, LANES))],
        out_specs=[_tok(D), _tok(AQ), _tok(BW), _perm_spec(4, BW), _perm_spec(16, BW),
                   _tok(LANES), _tok(LANES), _perm_spec(4, LANES), _perm_spec(16, LANES), ACCB_SPEC, ACCG_SPEC],
        out_shape=[nat(D, bf16), nat(AQ, bf16), nat(BW, bf16), jax.ShapeDtypeStruct((BL, 4, SEQ // 4, BW), bf16),
                   jax.ShapeDtypeStruct((BL, 16, SEQ // 16, BW), bf16), nat(LANES, f32), nat(LANES, f32),
                   jax.ShapeDtypeStruct((BL, 4, SEQ // 4, LANES), f32), jax.ShapeDtypeStruct((BL, 16, SEQ // 16, LANES), f32)]
                  + ACC_SHAPES,
        scratch_shapes=[pltpu.VMEM((BW // LANES, TM, LANES), f32)],
        compiler_params=_cp(("arbitrary", "arbitrary")),
    )(gx1, y, mod, g_post, w_out, oa, ob, g_mix_a, g_mix_b, l1, l4, l16, jnp.asarray(HEAD_EXPAND, bf16),
      jnp.asarray(HEAD_SUM, bf16))


def _attn_in_bwd(dqa, dka, dva, d1, d4, d16, tc, ts1, ts2, w_in, x, gx1, mod, g_pre):
    def body(dqa_ref, dka_ref, dva_ref, dq1_ref, dk1_ref, dv1_ref, dq4_ref, dk4_ref, dv4_ref, dq16_ref, dk16_ref, dv16_ref,
             c_ref, s1_ref, s2_ref, w_ref, x_ref, gx_ref, mod_ref, g_ref, dproj_ref, dx_ref, accb_ref, accg_ref, scr):
        _acc_init(accb_ref, accg_ref)
        c, s1, s2 = c_ref[...], s1_ref[...], s2_ref[...]
        tot = lambda r1, r4, r16: r1[...] + _perm_load(r4, scr, 4) + _perm_load(r16, scr, 16)
        dqb = tot(dq1_ref, dq4_ref, dq16_ref)
        dkb = tot(dk1_ref, dk4_ref, dk16_ref)
        dvb = tot(dv1_ref, dv4_ref, dv16_ref)
        dproj = jnp.concatenate([
            _rope_t(dqa_ref[...], c, s1, s2) * 0.125, _rope_t(_per_kv_head(dka_ref[...]), c, s1, s2),
            _per_kv_head(dva_ref[...]),
            _rope_t(dqb, c, s1, s2) * 0.125, _rope_t(dkb, c, s1, s2), dvb], axis=1).astype(bf16)
        dproj_ref[...] = dproj
        dh = _dot_nt(dproj, w_ref[...])
        xn, r = _rms(x_ref[...])
        g = g_ref[...]
        dn = dh * (1.0 + mod_ref[1:2, :])
        dx_ref[...] = gx_ref[...] + _rms_bwd(dn * g, xn, r)
        accb_ref[0:1, :] += _colsum(dh * (xn * g))
        accb_ref[1:2, :] += _colsum(dh)
        accg_ref[0:1, :] += _colsum(dn * xn)

    return pl.pallas_call(
        body, name="attn_in_bwd", grid=(BL, NJ),
        in_specs=[_tok(AQ), _tok(AQ), _tok(AQ)] + [_tok(BW)] * 3 + [_perm_spec(4, BW)] * 3 + [_perm_spec(16, BW)] * 3
                 + [_tok(LANES)] * 3 + [_full((D, INW)), _tok(D), _tok(D), MOD_SPEC, _full((1, D))],
        out_specs=[_tok(INW), _tok(D), ACCB_SPEC, ACCG_SPEC],
        out_shape=[jax.ShapeDtypeStruct((BL, SEQ, INW), bf16), jax.ShapeDtypeStruct((BL, SEQ, D), f32)] + ACC_SHAPES,
        scratch_shapes=[pltpu.VMEM((BW // LANES, TM, LANES), f32)],
        compiler_params=_cp(("arbitrary", "arbitrary")),
    )(dqa, dka, dva, *d1, *d4, *d16, tc, ts1, ts2, w_in, x, gx1, mod, g_pre)


def _inv_lane():
    inv = np.float32(THETA) ** (-np.arange(0, ROT, 2, dtype=np.float32) / np.float32(ROT))
    lane = np.arange(LANES) % HD
    return jnp.asarray(np.where(lane < ROT, inv[lane % (ROT // 2)], 0.0).astype(np.float32)[None, :])


def _local_step(x, positions, mod, target, inv_lane, first_weight, later_weights, grad_ready, g_attn_pre,
                g_attn_post, sink_a, g_mix_a, g_mix_b, g_mlp_pre, g_mlp_post):
    tabs = _rope_tables(positions.reshape(BL * SEQ, 1), inv_lane)
    w_in = first_weight(tuple(tabs))
    tc, ts1, ts2 = [t.reshape(BL, SEQ, LANES) for t in tabs]

    (h, qa, ka, va, q1, k1, v1, q4, k4, v4, q16, k16, v16, w_in) = _attn_in(x, mod, g_attn_pre, w_in, tc, ts1, ts2)
    seqs = lambda t: t.reshape(t.shape[0] * t.shape[1], t.shape[2], t.shape[3])
    q4, k4, v4, q16, k16, v16 = [seqs(t) for t in (q4, k4, v4, q16, k16, v16)]
    oa, la = _attn_fwd(qa, ka, va, sink_a, max_dist=BLK - 1, o_dtype=f32, name="attn_a_fwd")
    o1, l1 = _attn_fwd(q1, k1, v1, None, max_dist=BLK, o_dtype=bf16, name="attn_b1_fwd")
    o4, l4 = _attn_fwd(q4, k4, v4, None, max_dist=BLK, o_dtype=bf16, name="attn_b4_fwd")
    o16, l16 = _attn_fwd(q16, k16, v16, None, max_dist=BLK, o_dtype=bf16, name="attn_b16_fwd")
    b4 = lambda t: t.reshape(BL, 4, SEQ // 4, t.shape[-1])
    b16 = lambda t: t.reshape(BL, 16, SEQ // 16, t.shape[-1])
    w_out, mlp_weights, mod = later_weights((oa, o1, o4, o16), mod)
    x1, y, mixed, ob = _mix_out(oa, o1, l1, b4(o4), b4(l4), b16(o16), b16(l16), g_mix_a, g_mix_b, w_out, x, mod, g_attn_post)
    w_up, w_down = mlp_weights((x1,))
    h2, u, a = _mlp_up(x1, mod, g_mlp_pre, w_up)
    gx, dy2, accb_d, accg_d = _mlp_down(a, w_down, x1, target, mod, g_mlp_post)

    flat = lambda t: t.reshape(BL * SEQ, t.shape[-1])
    mod = grad_ready("w_down", _matmul_tn(flat(a), flat(dy2), tn=D, col_blocked=False, name="grad_w_down", out_dtype=bf16), mod)
    du, gx1, accb_m, accg_m = _mlp_bwd(dy2, u, w_down, w_up, x1, gx, mod, g_mlp_pre)
    mod = grad_ready("w_up", _matmul_tn(flat(h2), flat(du), tn=D, col_blocked=True, name="grad_w_up", out_dtype=bf16), mod)

    dy, doa, do1, do4, do16, da, dl1, dl4, dl16, accb_o, accg_o = _attn_out_bwd(
        gx1, y, mod, g_attn_post, w_out, oa, ob, g_mix_a, g_mix_b, l1, b4(l4), b16(l16))
    gw_out = _matmul_tn(flat(mixed), flat(dy), tn=D, col_blocked=False, name="grad_w_out")
    dqa, dka, dva, dsink = _attn_bwd(qa, ka, va, doa, da, la, sink_a, max_dist=BLK - 1, name="attn_a_bwd")
    d1 = _attn_bwd(q1, k1, v1, do1, dl1, l1, None, max_dist=BLK, name="attn_b1_bwd")
    d4 = _attn_bwd(q4, k4, v4, seqs(do4), seqs(dl4), l4, None, max_dist=BLK, name="attn_b4_bwd")
    d16 = _attn_bwd(q16, k16, v16, seqs(do16), seqs(dl16), l16, None, max_dist=BLK, name="attn_b16_bwd")
    dproj, grad_x, accb_i, accg_i = _attn_in_bwd(dqa, dka, dva, d1, [b4(t) for t in d4], [b16(t) for t in d16],
                                                 tc, ts1, ts2, w_in, x, gx1, mod, g_attn_pre)
    gw_in = _grad_w_in(flat(h), flat(dproj))
    dsink = grad_ready("w_in_w_out", (gw_in, gw_out), dsink)

    return grad_x, (accb_i, accb_o, accb_m, accb_d, accg_i, accg_o, accg_m, accg_d, dsink)


ADAW = NMOD * D // NCHIP


def _pos():
    return lax.axis_index("x"), lax.axis_index("y"), lax.axis_index("c")


def _flip(v, bit):
    return 1 - v if bit else v


def _all_peers(x, y, c):
    return [(_flip(x, k >> 2 & 1), _flip(y, k >> 1 & 1), _flip(c, k & 1)) for k in range(1, NDEV)]


def _other_chips(x, y):
    return [(1 - x, y), (x, 1 - y), (1 - x, 1 - y)]


def _rcopy(src, dst, send, recv, k, dev, k_recv=None):
    return pltpu.make_async_remote_copy(src_ref=src, dst_ref=dst, send_sem=send.at[k],
                                        recv_sem=recv.at[k if k_recv is None else k_recv],
                                        device_id=dev, device_id_type=MESH)


def _gather_small(src, buf, send, recv):
    x, y, c = _pos()
    me = 4 * x + 2 * y + c
    peers = _all_peers(x, y, c)
    sends = [_rcopy(src, buf.at[me], send, recv, k, p) for k, p in enumerate(peers)]
    for cp in sends:
        cp.start()
    for k, (px, py, pc) in enumerate(peers):
        _rcopy(src, buf.at[4 * px + 2 * py + pc], send, recv, k, (px, py, pc)).wait_recv()
    for cp in sends:
        cp.wait_send()
    return me


def _ada_fwd(c_in, w_ada, b_cols):
    def body(c_ref, w_hbm, b_ref, mod_ref, cond_ref, cbuf, mbuf, w_ref, s1, r1, s2, r2, wsem):
        x, y, c = _pos()
        chip = 2 * x + y
        wcopy = pltpu.make_async_copy(w_hbm, w_ref, wsem)
        wcopy.start()
        me = _gather_small(c_ref, cbuf, s1, r1)
        cbuf[me] = c_ref[...]
        for i in range(NDEV):
            cond_ref[BL * i:BL * (i + 1), :] = cbuf[i]
        call = cond_ref[...]
        cond = call / (1.0 + jnp.exp(-call))
        cond_ref[...] = cond
        wcopy.wait()
        mbuf[chip] = _dot(cond.astype(bf16), w_ref[...].astype(bf16)) + b_ref[...]
        chips = _other_chips(x, y)
        sends = [_rcopy(mbuf.at[chip], mbuf.at[chip], s2, r2, j, (px, py, c)) for j, (px, py) in enumerate(chips)]
        for cp in sends:
            cp.start()
        for j, (px, py) in enumerate(chips):
            _rcopy(mbuf.at[chip], mbuf.at[2 * px + py], s2, r2, j, (px, py, c)).wait_recv()
        for cp in sends:
            cp.wait_send()
        row = lax.broadcasted_iota(jnp.int32, (BL * NDEV, ADAW), 0)
        for s in range(NCHIP):
            slab = mbuf[s]
            for j in range(BL):
                mod_ref[j:j + 1, ADAW * s:ADAW * (s + 1)] = jnp.sum(jnp.where(row == BL * me + j, slab, 0.0), axis=0, keepdims=True)

    vm = pl.BlockSpec(memory_space=pltpu.VMEM)
    return pl.pallas_call(
        body, name="ada_fwd", in_specs=[vm, pl.BlockSpec(memory_space=pl.ANY), vm], out_specs=[vm, vm],
        out_shape=[jax.ShapeDtypeStruct((BL, NMOD * D), f32), jax.ShapeDtypeStruct((BL * NDEV, D), f32)],
        scratch_shapes=[pltpu.VMEM((NDEV, BL, D), f32), pltpu.VMEM((NCHIP, BL * NDEV, ADAW), f32),
                        pltpu.VMEM((D, ADAW), f32),
                        pltpu.SemaphoreType.DMA((NDEV - 1,)), pltpu.SemaphoreType.DMA((NDEV - 1,)),
                        pltpu.SemaphoreType.DMA((NCHIP - 1,)), pltpu.SemaphoreType.DMA((NCHIP - 1,)),
                        pltpu.SemaphoreType.DMA],
        compiler_params=pltpu.CompilerParams(vmem_limit_bytes=VMEM_LIMIT),
    )(c_in, w_ada, b_cols)


def _small_allreduce(accs, cond_all):
    def body(bi, bo, bm, bd, gi, go, gm, gd, dsink, cond_ref, gw_ref, gb_ref, small_ref, pay, pbuf, dall, s1, r1):
        x, y, c = _pos()
        chip = 2 * x + y
        pay[...] = jnp.zeros_like(pay)
        for b in range(BL):
            for k, (ref, r) in enumerate(((bi, 1), (bi, 0), (bo, 0), (bm, 1), (bm, 0), (bd, 0))):
                pay[b:b + 1, D * k:D * (k + 1)] = ref[b, r:r + 1, :]
        for off, ref, r in ((OFF_G_ATTN_PRE, gi, 0), (OFF_G_ATTN_POST, go, 0), (OFF_G_MIX_A, go, 1), (OFF_G_MLP_PRE, gm, 0),
                            (OFF_G_MLP_POST, gd, 0)):
            pay[BL:BL + 1, off:off + D] = ref[r:r + 1, :]
        eye = lax.broadcasted_iota(jnp.int32, (8, LANES), 0) == lax.broadcasted_iota(jnp.int32, (8, LANES), 1)
        pay[BL:BL + 1, OFF_SINK:OFF_SINK + LANES] = jnp.sum(jnp.where(eye, dsink[...], 0.0), axis=0, keepdims=True)
        pay[BL:BL + 1, OFF_LOSS:OFF_LOSS + LANES] = gd[1:2, 0:LANES]
        me = _gather_small(pay, pbuf, s1, r1)
        pbuf[me] = pay[...]
        small = pbuf[0, BL:BL + 1, :]
        for i in range(1, NDEV):
            small = small + pbuf[i, BL:BL + 1, :]
        small_ref[...] = small
        for i in range(NDEV):
            dall[BL * i:BL * (i + 1), :] = pbuf[i, 0:BL, :]
        gb_ref[...] = jnp.sum(dall[...], axis=0, keepdims=True)
        cols = jnp.zeros((BL * NDEV, ADAW), f32)
        for s in range(NCHIP):
            cols = cols + jnp.where(chip == s, dall[:, ADAW * s:ADAW * (s + 1)], 0.0)
        gw_ref[...] = _dot_tn(cond_ref[...].astype(bf16), cols.astype(bf16))

    vm = pl.BlockSpec(memory_space=pltpu.VMEM)
    return pl.pallas_call(
        body, name="small_allreduce", in_specs=[vm] * 10, out_specs=[vm] * 3,
        out_shape=[jax.ShapeDtypeStruct((D, ADAW), f32), jax.ShapeDtypeStruct((1, PAYW), f32), jax.ShapeDtypeStruct((1, PAYW), f32)],
        scratch_shapes=[pltpu.VMEM((4, PAYW), f32), pltpu.VMEM((NDEV, 4, PAYW), f32), pltpu.VMEM((BL * NDEV, PAYW), f32),
                        pltpu.SemaphoreType.DMA((NDEV - 1,)), pltpu.SemaphoreType.DMA((NDEV - 1,))],
        compiler_params=pltpu.CompilerParams(vmem_limit_bytes=VMEM_LIMIT),
    )(*accs, cond_all)


def _half(ref, c):
    r2 = ref.shape[0] // 2
    return ref.at[pl.ds(c * r2 if isinstance(c, int) else pl.multiple_of(c * r2, 16), r2), :]


HBM_SPEC = pl.BlockSpec(memory_space=pltpu.HBM)
SEM_SPEC = pl.BlockSpec(memory_space=pltpu.SEMAPHORE)
EFFECT = pltpu.SideEffectType.DATAFLOW_SIDE_EFFECTING
NLINK = NCHIP - 1


def _in_hbm(a):
    return pltpu.with_memory_space_constraint(a, pltpu.HBM)


NSEM = 8


def _split_start(name, srcs, land_shapes, builds, carry, after=(), lands=None):
    n = len(srcs)
    na, nc = len(after), len(carry)

    def body(*refs):
        src, land = refs[:n], refs[n:2 * n]
        kept = refs[2 * n + na:2 * n + na + nc]
        outs = refs[2 * n + na + nc:]
        send, recv, passed = outs[:n], outs[n:2 * n], outs[4 * n:]
        for t in range(n):
            for out_cp, _ in builds[t](src[t], land[t], send[t], recv[t]):
                out_cp.start()
        for a, b in zip(kept, passed):
            b[...] = a[...]

    if lands is None:
        lands = [lax.empty(s.shape, s.dtype) for s in land_shapes]
    lands = [_in_hbm(a) for a in lands]
    sems = [pltpu.SemaphoreType.DMA((NSEM,))] * (2 * n)
    thru = [pltpu.HBM(a.shape, a.dtype) for a in list(srcs) + lands]
    vm = pl.BlockSpec(memory_space=pltpu.VMEM)
    res = pl.pallas_call(
        body, name=name, out_shape=sems + thru + [jax.ShapeDtypeStruct(a.shape, a.dtype) for a in carry],
        in_specs=[HBM_SPEC] * (2 * n) + [pl.BlockSpec(memory_space=pl.ANY)] * na + [vm] * nc,
        out_specs=[SEM_SPEC] * (2 * n) + [HBM_SPEC] * (2 * n) + [vm] * nc,
        input_output_aliases={i: 2 * n + i for i in range(2 * n)},
        compiler_params=pltpu.CompilerParams(has_side_effects=EFFECT),
    )(*[_in_hbm(a) for a in srcs], *lands, *after, *carry)
    flight = [(res[2 * n + t], res[3 * n + t], res[t], res[n + t]) for t in range(n)]
    return flight, list(res[4 * n:])


def _split_wait(name, flight, builds, after):
    m = len(flight)
    na = len(after)

    def body(*refs):
        src, land, send, recv = refs[:m], refs[m:2 * m], refs[2 * m:3 * m], refs[3 * m:4 * m]
        for t in range(m):
            for out_cp, in_cp in builds[t](src[t], land[t], send[t], recv[t]):
                out_cp.wait_send()
                in_cp.wait_recv()

    ops = [f[0] for f in flight] + [f[1] for f in flight] + [f[2] for f in flight] + [f[3] for f in flight]
    res = pl.pallas_call(
        body, name=name, out_shape=[pltpu.HBM(a.shape, a.dtype) for a in ops[:2 * m]],
        in_specs=[HBM_SPEC] * (2 * m) + [SEM_SPEC] * (2 * m) + [pl.BlockSpec(memory_space=pl.ANY)] * na,
        out_specs=[HBM_SPEC] * (2 * m), input_output_aliases={i: i for i in range(2 * m)},
        compiler_params=pltpu.CompilerParams(has_side_effects=EFFECT),
    )(*ops, *after)
    return res[:m], res[m:2 * m]


def _weight_copies(src, land, send, recv):
    x, y, c = _pos()
    chip = 2 * x + y
    return [(_rcopy(_half(src, c), _half(land.at[chip], c), send, recv, j, (px, py, c)),
             _rcopy(_half(src, c), _half(land.at[2 * px + py], c), send, recv, j, (px, py, c)))
            for j, (px, py) in enumerate(_other_chips(x, y))]


def _grad_copies(src, land, send, recv):
    x, y, c = _pos()
    return [(_rcopy(src.at[2 * px + py], land.at[j], send, recv, j, (px, py, c)),
             _rcopy(src.at[2 * px + py], land.at[j], send, recv, j, (px, py, c)))
            for j, (px, py) in enumerate(_other_chips(x, y))]


NDIRECT = NDEV - 1


def _direct_grad_copies(src, land, send, recv):
    x, y, c = _pos()
    out, arrive = [], []
    for j, (px, py) in enumerate(_other_chips(x, y)):
        for hc in range(2):
            out.append(_rcopy(_half(src.at[2 * px + py], hc), land.at[2 * j + c], send, recv, 2 * j + hc, (px, py, hc),
                              k_recv=2 * j + c))
            arrive.append(_rcopy(_half(src.at[2 * px + py], hc), land.at[2 * j + hc], send, recv, 2 * j + hc, (px, py, hc)))
    own = _rcopy(_half(src.at[2 * x + y], 1 - c), land.at[NDIRECT - 1], send, recv, NDIRECT - 1, (x, y, 1 - c))
    return list(zip(out, arrive)) + [(own, own)]


def _pair_grad_copies(src, land, send, recv):
    x, y, c = _pos()
    r2 = src.shape[1] // 2
    cp = _rcopy(src.at[:, pl.ds(pl.multiple_of((1 - c) * r2, 8), r2), :], land, send, recv, 0, (x, y, 1 - c))
    return [(cp, cp)]


def _pair_weight_copies(src, land, send, recv):
    x, y, c = _pos()
    sib = (x, y, 1 - c)
    cps = []
    for j, (px, py) in enumerate(_other_chips(x, y)):
        mine, theirs = _half(land.at[2 * px + py], c), _half(land.at[2 * px + py], 1 - c)
        cps.append((_rcopy(mine, mine, send, recv, j, sib), _rcopy(theirs, theirs, send, recv, j, sib)))
    own = _rcopy(src, land.at[2 * x + y], send, recv, NLINK, sib)
    return cps + [(own, own)]


RS_ROWS = 128


def _pair_add(g, landed, c_arr, name):
    _, r2, cw = landed.shape
    nr = r2 // RS_ROWS

    def body(c_ref, g_ref, p_ref, o_ref):
        o_ref[...] = (g_ref[...] + p_ref[...]).astype(bf16)

    gs = pltpu.PrefetchScalarGridSpec(
        num_scalar_prefetch=1, grid=(NCHIP, nr),
        in_specs=[pl.BlockSpec((None, RS_ROWS, cw), lambda s, j, c: (s, c[0] * nr + j, 0)),
                  pl.BlockSpec((None, RS_ROWS, cw), lambda s, j, c: (s, j, 0))],
        out_specs=pl.BlockSpec((None, RS_ROWS, cw), lambda s, j, c: (s, j, 0)))
    return pl.pallas_call(body, name=name, grid_spec=gs, out_shape=jax.ShapeDtypeStruct((NCHIP, r2, cw), bf16),
                          compiler_params=_cp(("arbitrary", "arbitrary")))(c_arr, g, landed)


def _chip_add(own, landed, pos_arr, name):
    nl, r2, cw = landed.shape
    nr = r2 // RS_ROWS
    whole = own.shape[1] == 2 * r2

    def body(s_ref, h_ref, q_ref, o_ref):
        acc = h_ref[...].astype(f32)
        for j in range(nl):
            acc = acc + q_ref[j].astype(f32)
        o_ref[...] = acc

    gs = pltpu.PrefetchScalarGridSpec(
        num_scalar_prefetch=1, grid=(nr,),
        in_specs=[pl.BlockSpec((None, RS_ROWS, cw), lambda j, s: (s[0], (s[1] * nr if whole else 0) + j, 0)),
                  pl.BlockSpec((nl, RS_ROWS, cw), lambda j, s: (0, j, 0))],
        out_specs=pl.BlockSpec((RS_ROWS, cw), lambda j, s: (s[1] * nr + j, 0)))
    return pl.pallas_call(body, name=name, grid_spec=gs, out_shape=jax.ShapeDtypeStruct((2 * r2, cw), f32),
                          compiler_params=_cp(("arbitrary",)))(pos_arr, own, landed)


def _pair_gather_copies(src, land, send, recv):
    x, y, c = _pos()
    sib = (x, y, 1 - c)
    return [(_rcopy(_half(land, c), _half(land, c), send, recv, 0, sib),
             _rcopy(_half(land, 1 - c), _half(land, 1 - c), send, recv, 0, sib))]


def _adamw_math(w, g, m, v):
    m = B1 * m + (1.0 - B1) * g
    v = B2 * v + (1.0 - B2) * jnp.square(g)
    m_hat = m / (1.0 - B1 ** STEP)
    v_hat = v / (1.0 - B2 ** STEP)
    return -LR * (m_hat / (jnp.sqrt(v_hat) + AEPS) + WD * w), m, v


ADAM_ROWS = 256


def _adamw(w, g, m, v, name):
    r, cw = w.shape

    def body(w_ref, g_ref, m_ref, v_ref, go_ref, d_ref, mo_ref, vo_ref):
        g = g_ref[...]
        go_ref[...] = g
        d_ref[...], mo_ref[...], vo_ref[...] = _adamw_math(w_ref[...], g, m_ref[...], v_ref[...])

    rows = max(k for k in range(8, ADAM_ROWS + 1, 8) if r % k == 0)
    spec = pl.BlockSpec((rows, cw), lambda i: (i, 0))
    return pl.pallas_call(body, name=name, grid=(r // rows,), in_specs=[spec] * 4, out_specs=[spec] * 4,
                          out_shape=[jax.ShapeDtypeStruct((r, cw), f32)] * 4, compiler_params=_cp(("arbitrary",)))(w, g, m, v)


SMALL = (("b_ada", None, PAYW), ("g_attn_pre", OFF_G_ATTN_PRE, D), ("g_attn_post", OFF_G_ATTN_POST, D), ("sink_a", OFF_SINK, 8),
         ("g_mix_a", OFF_G_MIX_A, AQ), ("g_mix_b", OFF_G_MIX_B, BW), ("g_mlp_pre", OFF_G_MLP_PRE, D), ("g_mlp_post", OFF_G_MLP_POST, D))


def _adamw_small(small, gb, params):
    n = len(SMALL)

    def body(*refs):
        small_ref, gb_ref = refs[:2]
        wmv = refs[2:2 + 3 * n]
        loss_ref = refs[2 + 3 * n]
        outs = refs[3 + 3 * n:]
        loss_ref[...] = small_ref[:, OFF_LOSS:OFF_LOSS + 1] * (0.5 / D)
        for i, (_, off, width) in enumerate(SMALL):
            g = gb_ref[...] if off is None else small_ref[:, off:off + width]
            w_ref, m_ref, v_ref = wmv[3 * i:3 * i + 3]
            outs[4 * i][...] = g
            outs[4 * i + 1][...], outs[4 * i + 2][...], outs[4 * i + 3][...] = _adamw_math(w_ref[...], g, m_ref[...], v_ref[...])

    vm = pl.BlockSpec(memory_space=pltpu.VMEM)
    out_shape = [jax.ShapeDtypeStruct((1, 1), f32)]
    for _, _, width in SMALL:
        out_shape += [jax.ShapeDtypeStruct((1, width), f32)] * 4
    flat = [a for wmv in params for a in wmv]
    res = pl.pallas_call(body, name="adamw_small", in_specs=[vm] * (2 + 3 * n), out_specs=[vm] * len(out_shape),
                         out_shape=out_shape)(small, gb, *flat)
    return res[0], {name: res[1 + 4 * i:5 + 4 * i] for i, (name, _, _) in enumerate(SMALL)}


def kernel(x, c, positions, w_ada, b_ada, g_attn_pre, g_attn_post, w_in, sink_a, g_mix_a, g_mix_b, w_out, g_mlp_pre, g_mlp_post, w_up, w_down, loss_target, m_w_ada, m_b_ada, m_g_attn_pre, m_g_attn_post, m_w_in, m_sink_a, m_g_mix_a, m_g_mix_b, m_w_out, m_g_mlp_pre, m_g_mlp_post, m_w_up, m_w_down, v_w_ada, v_b_ada, v_g_attn_pre, v_g_attn_post, v_w_in, v_sink_a, v_g_mix_a, v_g_mix_b, v_w_out, v_g_mlp_pre, v_g_mlp_post, v_w_up, v_w_down):
    given = dict(w_ada=w_ada, b_ada=b_ada, g_attn_pre=g_attn_pre, g_attn_post=g_attn_post, w_in=w_in, sink_a=sink_a, g_mix_a=g_mix_a,
                 g_mix_b=g_mix_b, w_out=w_out, g_mlp_pre=g_mlp_pre, g_mlp_post=g_mlp_post, w_up=w_up, w_down=w_down)
    moms = dict(w_ada=(m_w_ada, v_w_ada), b_ada=(m_b_ada, v_b_ada), g_attn_pre=(m_g_attn_pre, v_g_attn_pre),
                g_attn_post=(m_g_attn_post, v_g_attn_post), w_in=(m_w_in, v_w_in), sink_a=(m_sink_a, v_sink_a),
                g_mix_a=(m_g_mix_a, v_g_mix_a), g_mix_b=(m_g_mix_b, v_g_mix_b), w_out=(m_w_out, v_w_out),
                g_mlp_pre=(m_g_mlp_pre, v_g_mlp_pre), g_mlp_post=(m_g_mlp_post, v_g_mlp_post), w_up=(m_w_up, v_w_up),
                w_down=(m_w_down, v_w_down))
    order = ["w_ada", "b_ada", "g_attn_pre", "g_attn_post", "w_in", "sink_a", "g_mix_a", "g_mix_b", "w_out", "g_mlp_pre",
             "g_mlp_post", "w_up", "w_down"]
    xi, yi, ci = _pos()
    chip = 2 * xi + yi

    c_arr = jnp.reshape(ci, (1,)).astype(jnp.int32)
    pos_arr = jnp.stack([chip, ci]).astype(jnp.int32)
    big = ("w_in", "w_out", "w_up", "w_down")

    b_cols = lax.dynamic_slice(b_ada, (0, chip * ADAW), (1, ADAW))
    mod, cond_all = _ada_fwd(c, w_ada[0], b_cols)
    gathered = [jax.ShapeDtypeStruct((NCHIP,) + given[n].shape[1:], bf16) for n in big]
    flight_in, (mod,) = _split_start("weights_start_first", [w_in[0].astype(bf16)], gathered[:1], [_weight_copies], [mod])
    mod, rest = lax.optimization_barrier((mod, [given[n][0] for n in big[1:]]))
    flight_rest, (mod, inv_lane) = _split_start("weights_start_rest", [w.astype(bf16) for w in rest], gathered[1:],
                                                [_weight_copies] * 3, [mod, _inv_lane()])
    mod = mod.reshape(BL, NMOD, D)

    def first_weight(after):
        srcs, lands = _split_wait("weights_wait_first", flight_in, [_weight_copies], after)
        cross, _ = _split_start("weights_pair_start_first", srcs, None, [_pair_weight_copies], [], lands=lands)
        _, (win_g,) = _split_wait("weights_pair_wait_first", cross, [_pair_weight_copies], ())
        return win_g

    def later_weights(after, carry):
        srcs, lands = _split_wait("weights_wait_rest", flight_rest, [_weight_copies] * 3, after)
        fl, (carry,) = _split_start("weights_pair_start_rest", srcs, None, [_pair_weight_copies] * 3, [carry], lands=lands)
        _, (wout_g,) = _split_wait("weights_pair_wait_out", fl[:1], [_pair_weight_copies], ())

        def mlp_weights(after):
            _, (wup_g, wdn_g) = _split_wait("weights_pair_wait_mlp", fl[1:], [_pair_weight_copies] * 2, after)
            return wup_g, wdn_g.reshape(DFF, D)

        return wout_g.reshape(D, D), mlp_weights, carry

    crossing, pending = {}, {}

    def grad_ready(group, g, carry):
        if group != "w_in_w_out":
            slab = g.reshape(NCHIP, DFF // NCHIP, D) if group == "w_down" else g
            land = jax.ShapeDtypeStruct((NDIRECT, slab.shape[1] // 2, slab.shape[2]), bf16)
            fl, (carry,) = _split_start("grad_start_" + group, [slab], [land], [_direct_grad_copies], [carry])
            pending[group] = ((group,), fl, [_direct_grad_copies])
            return carry
        names = ("w_in", "w_out")
        slabs = [g[0], g[1].reshape(NCHIP, D // NCHIP, D)]
        fl, (carry,) = _split_start("grad_pair_start_" + group, slabs,
                                    [jax.ShapeDtypeStruct((NCHIP, s.shape[1] // 2, s.shape[2]), f32) for s in slabs],
                                    [_pair_grad_copies] * len(names), [carry])
        crossing[group] = (names, fl)
        return carry

    def grad_reduce(group, after, carry):
        names, fl = crossing[group]
        slabs, landed = _split_wait("grad_pair_wait_" + group, fl, [_pair_grad_copies] * len(names), after)
        halves = [_pair_add(s, p, c_arr, "grad_pair_sum_" + n) for s, p, n in zip(slabs, landed, names)]
        fl, (carry,) = _split_start("grad_start_" + group, halves,
                                    [jax.ShapeDtypeStruct((NLINK,) + h.shape[1:], bf16) for h in halves],
                                    [_grad_copies] * len(names), [carry])
        pending[group] = (names, fl, [_grad_copies] * len(names))
        return carry

    grad_x, accs = _local_step(x, positions, mod, loss_target, inv_lane, first_weight, later_weights, grad_ready,
                               g_attn_pre, g_attn_post, sink_a, g_mix_a, g_mix_b, g_mlp_pre, g_mlp_post)

    grads, out = {}, {}

    def update(n):
        tr = (lambda a: a.T) if n == "w_in" else (lambda a: a)
        res = _adamw(tr(given[n][0]), tr(grads[n]), tr(moms[n][0][0]), tr(moms[n][1][0]), "adamw_" + n)
        out[n] = tuple(tr(a)[None] for a in res)
        return res[3]

    def finish(groups, after):
        names = sum((pending[g][0] for g in groups), ())
        fl = sum((pending[g][1] for g in groups), [])
        halves, landed = _split_wait("grad_wait_" + groups[0], fl, sum((pending[g][2] for g in groups), []), after)
        flights = []
        for h, q, n in zip(halves, landed, names):
            full = _chip_add(h, q, pos_arr, "grad_chip_sum_" + n)
            flights.append(_split_start("grad_gather_start_" + n, [jnp.zeros((8, LANES), f32)], None, [_pair_gather_copies],
                                        [], lands=[full])[0])
        last = None
        for n, fl1 in zip(names, flights):
            after = (flights[-1][0][0],) if last is None and fl1 is not flights[-1] else () if last is None else (last,)
            _, (grads[n],) = _split_wait("grad_gather_wait_" + n, fl1, [_pair_gather_copies], after)
            last = update(n)
        return last

    grads["w_ada"], gb, small = _small_allreduce(accs, cond_all)
    small = grad_reduce("w_in_w_out", (small,), small)
    last = finish(("w_down", "w_up"), (small,))
    finish(("w_in_w_out",), (last, update("w_ada")))
    loss, res = _adamw_small(small, gb, [(given[n], moms[n][0], moms[n][1]) for n, _, _ in SMALL])
    for n, _, _ in SMALL:
        out[n] = tuple(res[n])
    return (loss.reshape(()), grad_x, *[out[n][0] for n in order], *[out[n][1] for n in order],
            *[out[n][2] for n in order], *[out[n][3] for n in order])
```

```python
import numpy as np
import jax
import jax.numpy as jnp
from jax import lax
from jax.experimental import pallas as pl
from jax.experimental.pallas import tpu as pltpu

f32 = jnp.float32
bf16 = jnp.bfloat16
MESH = pl.DeviceIdType.MESH

D = 1024
SEQ = 2048
BL = 2
HD = 64
AQ = 512
AKV = 128
BW = 512
INW = 2304
DFF = 4096
NMOD = 6
ROT = 16
THETA = 500000.0
EPS = 1e-6
NEG = -1e30
BLK = 128
TM = 512
NJ = SEQ // TM
LANES = 128
SUBLANES = 8
NHEAD = AQ // HD
QSCALE = HD ** -0.5
NCHIP = 4
NDEV = 8
VMEM_LIMIT = 56 << 20

LR, B1, B2, AEPS, WD, STEP = 0.001, 0.9, 0.999, 1e-08, 0.01, 10

OFF_G_ATTN_PRE, OFF_G_ATTN_POST, OFF_G_MIX_A, OFF_G_MIX_B = 0, 1024, 2048, 2560
OFF_G_MLP_PRE, OFF_G_MLP_POST, OFF_SINK, OFF_LOSS = 3072, 4096, 5120, 5248
PAYW = NMOD * D


def _cp(sem=None):
    return pltpu.CompilerParams(dimension_semantics=sem, vmem_limit_bytes=VMEM_LIMIT)


def _dot(a, b):
    return jnp.dot(a, b, preferred_element_type=f32)


def _dot_nt(a, b):
    return lax.dot_general(a, b, (((1,), (1,)), ((), ())), preferred_element_type=f32)


def _dot_tn(a, b):
    return lax.dot_general(a, b, (((0,), (0,)), ((), ())), preferred_element_type=f32)


def _rms(x):
    r = lax.rsqrt(jnp.mean(x * x, axis=-1, keepdims=True) + EPS)
    return x * r, r


def _rms_bwd(dy, y, r):
    return r * (dy - y * jnp.mean(dy * y, axis=-1, keepdims=True))


def _colsum(v):
    return jnp.sum(v, axis=0, keepdims=True)


def _rope(p, c, s1, s2):
    outs = []
    for c0 in range(0, p.shape[1], LANES):
        pc = p[:, c0:c0 + LANES]
        outs.append(pc * c + pltpu.roll(pc, LANES - ROT // 2, 1) * s1 + pltpu.roll(pc, ROT // 2, 1) * s2)
    return outs[0] if len(outs) == 1 else jnp.concatenate(outs, axis=1)


def _rope_t(g, c, s1, s2):
    outs = []
    for c0 in range(0, g.shape[1], LANES):
        gc = g[:, c0:c0 + LANES]
        outs.append(gc * c + pltpu.roll(gc * s1, ROT // 2, 1) + pltpu.roll(gc * s2, LANES - ROT // 2, 1))
    return outs[0] if len(outs) == 1 else jnp.concatenate(outs, axis=1)


def _perm_store(val, scr, out_ref, d):
    nc = val.shape[1] // LANES
    for c in range(nc):
        scr[c] = val[:, LANES * c:LANES * (c + 1)]
    for c in range(nc):
        for r in range(d):
            out_ref[r, :, LANES * c:LANES * (c + 1)] = scr[c, pl.ds(r, TM // d, stride=d), :].astype(out_ref.dtype)


def _perm_load(in_ref, scr, d):
    nc = in_ref.shape[-1] // LANES
    for c in range(nc):
        for r in range(d):
            scr[c, pl.ds(r, TM // d, stride=d), :] = in_ref[r, :, LANES * c:LANES * (c + 1)].astype(f32)
    return jnp.concatenate([scr[c] for c in range(nc)], axis=1)


def _per_query_head(kv):
    r = pltpu.roll(kv, HD, 1)
    lo = lax.broadcasted_iota(jnp.int32, kv.shape, 1) < HD
    return jnp.concatenate([jnp.where(lo, kv, r), jnp.where(lo, r, kv)], axis=1)


def _per_kv_head(g):
    g0, g1 = g[:, :LANES] + g[:, LANES:2 * LANES], g[:, 2 * LANES:3 * LANES] + g[:, 3 * LANES:]
    lo = lax.broadcasted_iota(jnp.int32, g0.shape, 1) < HD
    return jnp.where(lo, g0 + pltpu.roll(g0, HD, 1), g1 + pltpu.roll(g1, HD, 1))


def _tok(w):
    return pl.BlockSpec((None, TM, w), lambda b, j: (b, j, 0))


def _perm_spec(d, w):
    return pl.BlockSpec((None, d, TM // d, w), lambda b, j: (b, 0, j, 0))


def _full(shape):
    n = len(shape)
    return pl.BlockSpec(shape, lambda b, j: (0,) * n)


MOD_SPEC = pl.BlockSpec((None, NMOD, D), lambda b, j: (b, 0, 0))
ACCB_SPEC = pl.BlockSpec((None, SUBLANES, D), lambda b, j: (b, 0, 0))
ACCG_SPEC = pl.BlockSpec((SUBLANES, D), lambda b, j: (0, 0))
ACC_SHAPES = [jax.ShapeDtypeStruct((BL, SUBLANES, D), f32), jax.ShapeDtypeStruct((SUBLANES, D), f32)]


def _acc_init(accb_ref, accg_ref):
    b, j = pl.program_id(0), pl.program_id(1)

    @pl.when(j == 0)
    def _():
        accb_ref[...] = jnp.zeros_like(accb_ref)

    @pl.when((b == 0) & (j == 0))
    def _():
        accg_ref[...] = jnp.zeros_like(accg_ref)


def _rope_tables(pos_col, inv_lane):
    def body(p_ref, inv_ref, c_ref, s1_ref, s2_ref):
        ang = p_ref[...].astype(f32) * inv_ref[...]
        j = lax.broadcasted_iota(jnp.int32, (TM, LANES), 1) % HD
        cs, sn = jnp.cos(ang), jnp.sin(ang)
        c_ref[...] = jnp.where(j < ROT, cs, 1.0)
        s1_ref[...] = jnp.where(j < ROT // 2, -sn, 0.0)
        s2_ref[...] = jnp.where((j >= ROT // 2) & (j < ROT), sn, 0.0)

    n = BL * SEQ // TM
    return pl.pallas_call(
        body, name="rope_tables", grid=(n,),
        in_specs=[pl.BlockSpec((TM, 1), lambda i: (i, 0)), pl.BlockSpec((1, LANES), lambda i: (0, 0))],
        out_specs=[pl.BlockSpec((TM, LANES), lambda i: (i, 0))] * 3,
        out_shape=[jax.ShapeDtypeStruct((BL * SEQ, LANES), f32)] * 3,
    )(pos_col, inv_lane)


def _attn_in(x, mod, g_pre, w_in, tc, ts1, ts2):
    def body(x_ref, mod_ref, g_ref, wg_ref, c_ref, s1_ref, s2_ref,
             h_ref, qa_ref, ka_ref, va_ref, q1_ref, k1_ref, v1_ref, q4_ref, k4_ref, v4_ref, q16_ref, k16_ref, v16_ref,
             w_ref, scr):
        @pl.when((pl.program_id(0) == 0) & (pl.program_id(1) == 0))
        def _():
            w_ref[...] = jnp.concatenate([wg_ref[s] for s in range(NCHIP)], axis=1)

        xn, _ = _rms(x_ref[...])
        h = (xn * g_ref[...]) * (1.0 + mod_ref[1:2, :]) + mod_ref[0:1, :]
        hb = h.astype(bf16)
        h_ref[...] = hb
        proj = _dot(hb, w_ref[...])
        c, s1, s2 = c_ref[...], s1_ref[...], s2_ref[...]
        o1, o2, o3, o4, o5 = AQ, AQ + AKV, AQ + 2 * AKV, AQ + 2 * AKV + BW, AQ + 2 * AKV + 2 * BW
        qa_ref[...] = (_rope(proj[:, :o1], c, s1, s2) * QSCALE).astype(bf16)
        ka_ref[...] = _per_query_head(_rope(proj[:, o1:o2], c, s1, s2)).astype(bf16)
        va_ref[...] = _per_query_head(proj[:, o2:o3]).astype(bf16)
        qb = _rope(proj[:, o3:o4], c, s1, s2) * QSCALE
        kb = _rope(proj[:, o4:o5], c, s1, s2)
        vb = proj[:, o5:]
        for val, r1, r4, r16 in ((qb, q1_ref, q4_ref, q16_ref), (kb, k1_ref, k4_ref, k16_ref), (vb, v1_ref, v4_ref, v16_ref)):
            r1[...] = val.astype(bf16)
            _perm_store(val, scr, r4, 4)
            _perm_store(val, scr, r16, 16)

    nat = lambda w: jax.ShapeDtypeStruct((BL, SEQ, w), bf16)
    p4 = jax.ShapeDtypeStruct((BL, 4, SEQ // 4, BW), bf16)
    p16 = jax.ShapeDtypeStruct((BL, 16, SEQ // 16, BW), bf16)
    return pl.pallas_call(
        body, name="attn_in", grid=(BL, NJ),
        in_specs=[_tok(D), MOD_SPEC, _full((1, D)), _full((NCHIP, D, INW // NCHIP)), _tok(LANES), _tok(LANES), _tok(LANES)],
        out_specs=([_tok(D), _tok(AQ), _tok(2 * AKV), _tok(2 * AKV)] + [_tok(BW)] * 3 + [_perm_spec(4, BW)] * 3 + [_perm_spec(16, BW)] * 3
                   + [_full((D, INW))]),
        out_shape=[nat(D), nat(AQ), nat(2 * AKV), nat(2 * AKV)] + [nat(BW)] * 3 + [p4] * 3 + [p16] * 3
                  + [jax.ShapeDtypeStruct((D, INW), bf16)],
        scratch_shapes=[pltpu.VMEM((BW // LANES, TM, LANES), f32)],
        compiler_params=_cp(("arbitrary", "arbitrary")),
    )(x, mod, g_pre, w_in, tc, ts1, ts2)


def _kv_cat(cur_ref, prev_ref, p, cache):
    key = (id(cur_ref), p)
    if key not in cache:
        sl = slice(LANES * p, LANES * (p + 1))
        cache[key] = cur_ref[:, sl] if prev_ref is None else jnp.concatenate([prev_ref[:, sl], cur_ref[:, sl]], axis=0)
    return cache[key]


def _lane_half(a, hh):
    lo = lax.broadcasted_iota(jnp.int32, a.shape, 1) < HD
    return jnp.where(lo, a, jnp.zeros_like(a)) if hh == 0 else jnp.where(lo, jnp.zeros_like(a), a)


ATT_UNITS = 4


def _attn_specs(n, nb, descending):
    u = ATT_UNITS
    if nb == 1:
        return (lambda ww: pl.BlockSpec((u, BLK, ww), lambda a, i: (a, 0, 0))), None, (n // u, 1)
    steps = nb // u
    at = (lambda i: steps - 1 - i) if descending else (lambda i: i)
    cur = lambda ww: pl.BlockSpec((None, u * BLK, ww), lambda a, i: (a, at(i), 0))
    prev = lambda ww: pl.BlockSpec((None, BLK, ww), lambda a, i: (a, jnp.maximum(u * at(i) - 1, 0), 0))
    return cur, prev, (n, steps)


def _attn_fwd(q, k, v, sink, *, max_dist, o_dtype, name):
    n, l, w = q.shape
    wk = k.shape[-1]
    nb = l // BLK
    has_sink = sink is not None

    def body(*refs):
        sink_ref = None
        if has_sink:
            sink_ref, refs = refs[0], refs[1:]
        if nb > 1:
            q_ref, kc_ref, kp_ref, vc_ref, vp_ref, o_ref, lse_ref = refs[:7]
            first = pl.program_id(1) == 0
            for u in range(ATT_UNITS):
                rows, before = pl.ds(BLK * u, BLK), pl.ds(BLK * (u - 1), BLK)
                unit(q_ref.at[rows, :], kc_ref.at[rows, :], kp_ref if u == 0 else kc_ref.at[before, :],
                     vc_ref.at[rows, :], vp_ref if u == 0 else vc_ref.at[before, :], o_ref.at[rows, :], lse_ref.at[rows, :],
                     jnp.logical_not(first) if u == 0 else True, sink_ref, *refs[7:])
        else:
            q_ref, kc_ref, vc_ref, o_ref, lse_ref = refs[:5]
            for u in range(ATT_UNITS):
                unit(q_ref.at[u], kc_ref.at[u], None, vc_ref.at[u], None, o_ref.at[u], lse_ref.at[u], None, sink_ref, *refs[5:])

    def unit(q_ref, kc_ref, kp_ref, vc_ref, vp_ref, o_ref, lse_ref, has_prev, sink_ref, sscr, pscr, dscr):
        qi = lax.broadcasted_iota(jnp.int32, (BLK, BLK), 0)
        kj = lax.broadcasted_iota(jnp.int32, (BLK, BLK), 1)
        tri = kj <= qi
        eye = kj == qi
        cache = {}
        for p in range(w // LANES):
            qpair = q_ref[:, LANES * p:LANES * (p + 1)]
            kcat = _kv_cat(kc_ref, kp_ref, p // share, cache)
            for hh in range(2):
                s = _dot_nt(_lane_half(qpair, hh), kcat)
                if nb > 1:
                    sp = s[:, :BLK] if has_prev is True else jnp.where(has_prev, s[:, :BLK], NEG)
                    sscr[2 * p + hh] = jnp.where(tri, s[:, BLK:], sp)
                    if diag:
                        dscr[2 * p + hh] = jnp.where(eye, sp, NEG)
                else:
                    sscr[2 * p + hh] = jnp.where(tri, s, NEG)
        lane = lax.broadcasted_iota(jnp.int32, (BLK, LANES), 1)
        lse_all = jnp.zeros((BLK, LANES), f32)
        for p in range(w // LANES):
            for hh in range(2):
                h = 2 * p + hh
                comb = sscr[h]
                if diag:
                    dtile = dscr[h]
                    m = jnp.max(jnp.maximum(comb, dtile), axis=-1, keepdims=True)
                else:
                    m = jnp.max(comb, axis=-1, keepdims=True)
                if has_sink:
                    sk = sink_ref[0, h]
                    m = jnp.maximum(m, sk)
                e = jnp.exp(comb - m)
                if diag:
                    ed = jnp.exp(dtile - m)
                    den = jnp.sum(e + ed, axis=-1, keepdims=True)
                else:
                    den = jnp.sum(e, axis=-1, keepdims=True)
                if has_sink:
                    den = den + jnp.exp(sk - m)
                inv = 1.0 / den
                if nb > 1:
                    pscr[h, :, :BLK] = (jnp.where(tri, ed if diag else 0.0, e) * inv).astype(bf16)
                    pscr[h, :, BLK:] = (jnp.where(tri, e, 0.0) * inv).astype(bf16)
                else:
                    pscr[h] = (e * inv).astype(bf16)
                lse_all = jnp.where(lane == h, jnp.broadcast_to(m + jnp.log(den), (BLK, LANES)), lse_all)
        lse_ref[...] = lse_all
        for p in range(w // LANES):
            vcat = _kv_cat(vc_ref, vp_ref, p // share, cache)
            o_ref[:, LANES * p:LANES * (p + 1)] = (_dot(pscr[2 * p], _lane_half(vcat, 0))
                                                   + _dot(pscr[2 * p + 1], _lane_half(vcat, 1))).astype(o_ref.dtype)

    assert max_dist in (BLK - 1, BLK) and w % wk == 0
    share = w // wk
    diag = nb > 1 and max_dist == BLK
    cur, prev, grid = _attn_specs(n, nb, False)
    in_specs = [cur(w), cur(wk)] + ([prev(wk)] if nb > 1 else []) + [cur(wk)] + ([prev(wk)] if nb > 1 else [])
    args = [q, k] + ([k] if nb > 1 else []) + [v] + ([v] if nb > 1 else [])
    if has_sink:
        in_specs = [pl.BlockSpec(memory_space=pltpu.SMEM)] + in_specs
        args = [sink] + args
    return pl.pallas_call(
        body, name=name, grid=grid, in_specs=in_specs,
        out_specs=[cur(w), cur(LANES)],
        out_shape=[jax.ShapeDtypeStruct((n, l, w), o_dtype), jax.ShapeDtypeStruct((n, l, LANES), f32)],
        scratch_shapes=[pltpu.VMEM((w // HD, BLK, BLK), f32), pltpu.VMEM((w // HD, BLK, 2 * BLK if nb > 1 else BLK), bf16),
                        pltpu.VMEM((w // HD if diag else 1, BLK, BLK), f32)],
        compiler_params=_cp(("arbitrary", "arbitrary")),
    )(*args)


def _attn_bwd(q, k, v, do, delta, lse, sink, *, max_dist, name):
    n, l, w = q.shape
    wk = k.shape[-1]
    nb = l // BLK
    has_sink = sink is not None

    def body(*refs):
        sink_ref = dsink_ref = ck = cv = None
        if has_sink:
            sink_ref, refs = refs[0], refs[1:]
        nin = 8 if nb > 1 else 6
        ins, rest = refs[:nin], refs[nin:]
        if has_sink:
            dq_ref, dk_ref, dv_ref, dsink_ref = rest[:4]
            rest = rest[4:]
        else:
            dq_ref, dk_ref, dv_ref = rest[:3]
            rest = rest[3:]
        step = pl.program_id(1)
        if has_sink:
            @pl.when((pl.program_id(0) == 0) & (step == 0))
            def _():
                dsink_ref[...] = jnp.zeros_like(dsink_ref)

        if nb > 1:
            q_ref, kc_ref, kp_ref, vc_ref, vp_ref, do_ref, delta_ref, lse_ref = ins
            ck, cv = rest[:2]

            @pl.when(step == 0)
            def _():
                ck[...] = jnp.zeros_like(ck)
                cv[...] = jnp.zeros_like(cv)

            last = step == nb // ATT_UNITS - 1
            for u in reversed(range(ATT_UNITS)):
                rows, before = pl.ds(BLK * u, BLK), pl.ds(BLK * (u - 1), BLK)
                unit(q_ref.at[rows, :], kc_ref.at[rows, :], kp_ref if u == 0 else kc_ref.at[before, :],
                     vc_ref.at[rows, :], vp_ref if u == 0 else vc_ref.at[before, :], do_ref.at[rows, :],
                     delta_ref.at[rows, :], lse_ref.at[rows, :], dq_ref.at[rows, :], dk_ref.at[rows, :], dv_ref.at[rows, :],
                     jnp.logical_not(last) if u == 0 else True, sink_ref, dsink_ref, ck, cv, *rest[2:])
        else:
            q_ref, kc_ref, vc_ref, do_ref, delta_ref, lse_ref = ins
            for u in range(ATT_UNITS):
                unit(q_ref.at[u], kc_ref.at[u], None, vc_ref.at[u], None, do_ref.at[u], delta_ref.at[u], lse_ref.at[u],
                     dq_ref.at[u], dk_ref.at[u], dv_ref.at[u], None, sink_ref, dsink_ref, None, None, *rest)

    def unit(q_ref, kc_ref, kp_ref, vc_ref, vp_ref, do_ref, delta_ref, lse_ref, dq_ref, dk_ref, dv_ref, has_prev,
             sink_ref, dsink_ref, ck, cv, sscr, dpscr, pscr, dsscr, dscr=None, ddscr=None):
        lane = lax.broadcasted_iota(jnp.int32, (BLK, LANES), 1)
        qi = lax.broadcasted_iota(jnp.int32, (BLK, BLK), 0)
        kj = lax.broadcasted_iota(jnp.int32, (BLK, BLK), 1)
        tri = kj <= qi
        eye = kj == qi
        cache = {}
        kp, vp = kp_ref, vp_ref
        for p in range(w // LANES):
            sl = slice(LANES * p, LANES * (p + 1))
            qpair, dopair = q_ref[:, sl], do_ref[:, sl]
            kcat, vcat = _kv_cat(kc_ref, kp, p // share, cache), _kv_cat(vc_ref, vp, p // share, cache)
            for hh in range(2):
                h = 2 * p + hh
                s = _dot_nt(_lane_half(qpair, hh), kcat)
                dp = _dot_nt(_lane_half(dopair, hh), vcat)
                if nb > 1:
                    sp = s[:, :BLK] if has_prev is True else jnp.where(has_prev, s[:, :BLK], NEG)
                    sscr[h] = jnp.where(tri, s[:, BLK:], sp)
                    dpscr[h] = jnp.where(tri, dp[:, BLK:], dp[:, :BLK])
                    if diag:
                        dscr[h] = jnp.where(eye, sp, NEG)
                        ddscr[h] = dp[:, :BLK]
                else:
                    sscr[h] = jnp.where(tri, s, NEG)
                    dpscr[h] = dp
        for p in range(w // LANES):
            for hh in range(2):
                h = 2 * p + hh
                lse_b = jnp.broadcast_to(lse_ref[:, h:h + 1], (BLK, BLK))
                delta = jnp.broadcast_to(delta_ref[:, h:h + 1], (BLK, BLK))
                pr = jnp.exp(sscr[h] - lse_b)
                ds = pr * (dpscr[h] - delta)
                if nb > 1:
                    if diag:
                        prd = jnp.exp(dscr[h] - lse_b)
                        dsd = prd * (ddscr[h] - delta)
                    else:
                        prd = dsd = 0.0
                    pscr[h, :, :BLK] = jnp.where(tri, prd, pr).astype(bf16)
                    pscr[h, :, BLK:] = jnp.where(tri, pr, 0.0).astype(bf16)
                    dsscr[h, :, :BLK] = jnp.where(tri, dsd, ds).astype(bf16)
                    dsscr[h, :, BLK:] = jnp.where(tri, ds, 0.0).astype(bf16)
                else:
                    pscr[h] = pr.astype(bf16)
                    dsscr[h] = ds.astype(bf16)
                if has_sink:
                    dsk = -jnp.sum(jnp.where(lane == 0, jnp.exp(sink_ref[0, h] - lse_b) * delta, 0.0), keepdims=True)
                    dsink_ref[h:h + 1, :] += jnp.broadcast_to(dsk, (1, LANES))
        for p in range(w // LANES):
            sl = slice(LANES * p, LANES * (p + 1))
            qpair, dopair = q_ref[:, sl], do_ref[:, sl]
            kcat = _kv_cat(kc_ref, kp, p // share, cache)
            dq_ref[:, sl] = _dot(dsscr[2 * p], _lane_half(kcat, 0)) + _dot(dsscr[2 * p + 1], _lane_half(kcat, 1))
            dk_pair = _dot_tn(dsscr[2 * p], _lane_half(qpair, 0)) + _dot_tn(dsscr[2 * p + 1], _lane_half(qpair, 1))
            dv_pair = _dot_tn(pscr[2 * p], _lane_half(dopair, 0)) + _dot_tn(pscr[2 * p + 1], _lane_half(dopair, 1))
            if nb > 1:
                dk_ref[:, sl] = dk_pair[BLK:] + ck[:, sl]
                dv_ref[:, sl] = dv_pair[BLK:] + cv[:, sl]
                ck[:, sl] = dk_pair[:BLK]
                cv[:, sl] = dv_pair[:BLK]
            else:
                dk_ref[:, sl] = dk_pair
                dv_ref[:, sl] = dv_pair

    assert max_dist in (BLK - 1, BLK) and w % wk == 0
    share = w // wk
    diag = nb > 1 and max_dist == BLK
    cur, prev, grid = _attn_specs(n, nb, True)
    in_specs = ([cur(w), cur(wk)] + ([prev(wk)] if nb > 1 else []) + [cur(wk)] + ([prev(wk)] if nb > 1 else [])
                + [cur(w), cur(LANES), cur(LANES)])
    args = [q, k] + ([k] if nb > 1 else []) + [v] + ([v] if nb > 1 else []) + [do, delta, lse]
    out_specs = [cur(w)] * 3
    out_shape = [jax.ShapeDtypeStruct((n, l, w), f32)] * 3
    if has_sink:
        in_specs = [pl.BlockSpec(memory_space=pltpu.SMEM)] + in_specs
        args = [sink] + args
        out_specs.append(pl.BlockSpec((NHEAD, LANES), lambda a, i: (0, 0)))
        out_shape.append(jax.ShapeDtypeStruct((NHEAD, LANES), f32))
    nh = w // HD
    scratch = [pltpu.VMEM((BLK, w), f32), pltpu.VMEM((BLK, w), f32)] if nb > 1 else []
    scratch += [pltpu.VMEM((nh, BLK, BLK), f32)] * 2 + [pltpu.VMEM((nh, BLK, 2 * BLK if nb > 1 else BLK), bf16)] * 2
    if diag:
        scratch += [pltpu.VMEM((nh, BLK, BLK), f32)] * 2
    return pl.pallas_call(
        body, name=name, grid=grid, in_specs=in_specs, out_specs=out_specs, out_shape=out_shape,
        scratch_shapes=scratch, compiler_params=_cp(("arbitrary", "arbitrary")),
    )(*args)


def _split2(x):
    hi = x.astype(bf16)
    return hi, (x - hi.astype(f32)).astype(bf16)


def _heads_to_lanes(xc, e):
    return sum(_dot(t, e) for t in _split2(xc))


def _lanes_to_heads(x, g):
    return sum(_dot(t, g) for t in _split2(x))


HEAD_EXPAND = (np.arange(LANES)[:, None] == np.arange(BW)[None, :] // HD).astype(np.float32)
HEAD_SUM = HEAD_EXPAND.T.copy()


def _branch_weights(l1_ref, l4_ref, l16_ref, scr):
    l4v = _perm_load(l4_ref, scr, 4)
    l16v = _perm_load(l16_ref, scr, 16)
    l1v = l1_ref[...]
    m = jnp.maximum(jnp.maximum(l1v, l4v), l16v)
    e1, e4, e16 = jnp.exp(l1v - m), jnp.exp(l4v - m), jnp.exp(l16v - m)
    z = e1 + e4 + e16
    return e1 / z, e4 / z, e16 / z


def _mix_out(oa, o1, l1, o4, l4, o16, l16, g_mix_a, g_mix_b, w_out, x, mod, g_post):
    def body(oa_ref, o1_ref, l1_ref, o4_ref, l4_ref, o16_ref, l16_ref, ga_ref, gb_ref, w_ref, x_ref, mod_ref, gp_ref, e_ref,
             x1_ref, y_ref, mixed_ref, ob_ref, scr):
        w1, w4, w16 = _branch_weights(l1_ref, l4_ref, l16_ref, scr)
        e = e_ref[...]
        x1w, x4w = _heads_to_lanes(w1, e), _heads_to_lanes(w4, e)
        ob = (x1w * o1_ref[...].astype(f32) + x4w * _perm_load(o4_ref, scr, 4)
              + (1.0 - x1w - x4w) * _perm_load(o16_ref, scr, 16))
        ob_ref[...] = ob
        oan, _ = _rms(oa_ref[...])
        obn, _ = _rms(ob)
        mixed = jnp.concatenate([oan * ga_ref[...], obn * gb_ref[...]], axis=1).astype(bf16)
        mixed_ref[...] = mixed
        y = _dot(mixed, w_ref[...])
        y_ref[...] = y
        yn, _ = _rms(y)
        x1_ref[...] = x_ref[...] + mod_ref[2:3, :] * (yn * gp_ref[...])

    nat = lambda w, dt: jax.ShapeDtypeStruct((BL, SEQ, w), dt)
    return pl.pallas_call(
        body, name="mix_out", grid=(BL, NJ),
        in_specs=[_tok(AQ), _tok(BW), _tok(LANES), _perm_spec(4, BW), _perm_spec(4, LANES), _perm_spec(16, BW),
                  _perm_spec(16, LANES), _full((1, AQ)), _full((1, BW)), _full((D, D)), _tok(D), MOD_SPEC, _full((1, D)),
                  _full((LANES, BW))],
        out_specs=[_tok(D), _tok(D), _tok(D), _tok(BW)],
        out_shape=[nat(D, f32), nat(D, f32), nat(D, bf16), nat(BW, f32)],
        scratch_shapes=[pltpu.VMEM((BW // LANES, TM, LANES), f32)],
        compiler_params=_cp(("arbitrary", "arbitrary")),
    )(oa, o1, l1, o4, l4, o16, l16, g_mix_a, g_mix_b, w_out, x, mod, g_post, jnp.asarray(HEAD_EXPAND, bf16))


def _mlp_up(x1, mod, g_pre, w_up):
    def body(x_ref, mod_ref, g_ref, w_ref, h_ref, u_ref, a_ref):
        xn, _ = _rms(x_ref[...])
        h = (xn * g_ref[...]) * (1.0 + mod_ref[4:5, :]) + mod_ref[3:4, :]
        hb = h.astype(bf16)
        h_ref[...] = hb
        for s in range(NCHIP):
            u = _dot(hb, w_ref[s])
            u_ref[:, D * s:D * (s + 1)] = u.astype(bf16)
            a_ref[:, D * s:D * (s + 1)] = jnp.square(jnp.maximum(u, 0.0)).astype(bf16)

    nat = lambda w: jax.ShapeDtypeStruct((BL, SEQ, w), bf16)
    return pl.pallas_call(
        body, name="mlp_up", grid=(BL, NJ),
        in_specs=[_tok(D), MOD_SPEC, _full((1, D)), _full((NCHIP, D, D))],
        out_specs=[_tok(D), _tok(DFF), _tok(DFF)], out_shape=[nat(D), nat(DFF), nat(DFF)],
        compiler_params=_cp(("arbitrary", "arbitrary")),
    )(x1, mod, g_pre, w_up)


def _mlp_down(a, w_down, x1, target, mod, g_post):
    def body(a_ref, w_ref, x_ref, t_ref, mod_ref, g_ref, gx_ref, dy_ref, accb_ref, accg_ref):
        _acc_init(accb_ref, accg_ref)
        y2 = _dot(a_ref[...], w_ref[...])
        yn, r = _rms(y2)
        g = g_ref[...]
        gt = mod_ref[5:6, :]
        n2 = yn * g
        err = x_ref[...] + gt * n2 - t_ref[...]
        gout = err * (1.0 / D)
        gx_ref[...] = gout
        dn2 = gout * gt
        dy_ref[...] = _rms_bwd(dn2 * g, yn, r).astype(bf16)
        accb_ref[0:1, :] += _colsum(gout * n2)
        accg_ref[0:1, :] += _colsum(dn2 * yn)
        accg_ref[1:2, :] += jnp.broadcast_to(jnp.sum(err * err, keepdims=True), (1, D))

    return pl.pallas_call(
        body, name="mlp_down", grid=(BL, NJ),
        in_specs=[_tok(DFF), _full((DFF, D)), _tok(D), _tok(D), MOD_SPEC, _full((1, D))],
        out_specs=[_tok(D), _tok(D), ACCB_SPEC, ACCG_SPEC],
        out_shape=[jax.ShapeDtypeStruct((BL, SEQ, D), f32), jax.ShapeDtypeStruct((BL, SEQ, D), bf16)] + ACC_SHAPES,
        compiler_params=_cp(("arbitrary", "arbitrary")),
    )(a, w_down, x1, target, mod, g_post)


def _mlp_bwd(dy2, u, w_down, w_up, x1, gx, mod, g_pre):
    def body(dy_ref, u_ref, wd_hbm, wu_hbm, x_ref, gx_ref, mod_ref, g_ref, du_ref, gx1_ref, accb_ref, accg_ref, wd, wu, sem):
        _acc_init(accb_ref, accg_ref)
        first = (pl.program_id(0) == 0) & (pl.program_id(1) == 0)
        c1 = pltpu.make_async_copy(wd_hbm, wd, sem.at[0])
        c2 = pltpu.make_async_copy(wu_hbm, wu, sem.at[1])

        @pl.when(first)
        def _():
            c1.start()
            c2.start()
            c1.wait()

        dy = dy_ref[...]
        for s in range(NCHIP):
            sl = slice(D * s, D * (s + 1))
            da = _dot_nt(dy, wd[sl, :])
            du_ref[:, sl] = (da * (2.0 * jnp.maximum(u_ref[:, sl].astype(f32), 0.0))).astype(bf16)

        @pl.when(first)
        def _():
            c2.wait()

        dh = jnp.zeros((TM, D), f32)
        for s in range(NCHIP):
            dh = dh + _dot_nt(du_ref[:, D * s:D * (s + 1)], wu[s])
        xn, r = _rms(x_ref[...])
        g = g_ref[...]
        n = xn * g
        dn = dh * (1.0 + mod_ref[4:5, :])
        gx1_ref[...] = gx_ref[...] + _rms_bwd(dn * g, xn, r)
        accb_ref[0:1, :] += _colsum(dh * n)
        accb_ref[1:2, :] += _colsum(dh)
        accg_ref[0:1, :] += _colsum(dn * xn)

    anyspec = pl.BlockSpec(memory_space=pl.ANY)
    return pl.pallas_call(
        body, name="mlp_bwd", grid=(BL, NJ),
        in_specs=[_tok(D), _tok(DFF), anyspec, anyspec, _tok(D), _tok(D), MOD_SPEC, _full((1, D))],
        out_specs=[_tok(DFF), _tok(D), ACCB_SPEC, ACCG_SPEC],
        out_shape=[jax.ShapeDtypeStruct((BL, SEQ, DFF), bf16), jax.ShapeDtypeStruct((BL, SEQ, D), f32)] + ACC_SHAPES,
        scratch_shapes=[pltpu.VMEM((DFF, D), bf16), pltpu.VMEM((NCHIP, D, D), bf16), pltpu.SemaphoreType.DMA((2,))],
        compiler_params=_cp(("arbitrary", "arbitrary")),
    )(dy2, u, w_down, w_up, x1, gx, mod, g_pre)


def _matmul_tn(a, b, *, tn, col_blocked, name, out_dtype=f32):
    t, m = a.shape
    n = b.shape[1]
    tmm = min(m, 1024)
    tk = 2048 if tn <= 1024 else 1024
    nk = t // tk

    def body(a_ref, b_ref, o_ref, acc):
        k = pl.program_id(2)

        @pl.when(k == 0)
        def _():
            acc[...] = jnp.zeros_like(acc)

        acc[...] += _dot_tn(a_ref[...], b_ref[...])

        @pl.when(k == nk - 1)
        def _():
            o_ref[...] = acc[...].astype(out_dtype)

    if col_blocked:
        out_spec = pl.BlockSpec((None, tmm, tn), lambda i, j, k: (j, i, 0))
        out_shape = jax.ShapeDtypeStruct((n // tn, m, tn), out_dtype)
    else:
        out_spec = pl.BlockSpec((tmm, tn), lambda i, j, k: (i, j))
        out_shape = jax.ShapeDtypeStruct((m, n), out_dtype)
    return pl.pallas_call(
        body, name=name, grid=(m // tmm, n // tn, nk),
        in_specs=[pl.BlockSpec((tk, tmm), lambda i, j, k: (k, i)), pl.BlockSpec((tk, tn), lambda i, j, k: (k, j))],
        out_specs=out_spec, out_shape=out_shape, scratch_shapes=[pltpu.VMEM((tmm, tn), f32)],
        compiler_params=_cp(("arbitrary", "arbitrary", "arbitrary")),
    )(a, b)


def _grad_w_in(h, dproj):
    t = h.shape[0]
    tk = 1024
    nk = t // tk
    sw = INW // NCHIP

    def body(a_ref, b_ref, o_ref, acc):
        k = pl.program_id(0)

        @pl.when(k == 0)
        def _():
            acc[...] = jnp.zeros_like(acc)

        acc[...] += _dot_tn(a_ref[...], b_ref[...])

        @pl.when(k == nk - 1)
        def _():
            for s in range(NCHIP):
                o_ref[s] = acc[:, sw * s:sw * (s + 1)]

    return pl.pallas_call(
        body, name="grad_w_in", grid=(nk,),
        in_specs=[pl.BlockSpec((tk, D), lambda k: (k, 0)), pl.BlockSpec((tk, INW), lambda k: (k, 0))],
        out_specs=pl.BlockSpec((NCHIP, D, sw), lambda k: (0, 0, 0)), out_shape=jax.ShapeDtypeStruct((NCHIP, D, sw), f32),
        scratch_shapes=[pltpu.VMEM((D, INW), f32)], compiler_params=_cp(("arbitrary",)),
    )(h, dproj)


def _attn_out_bwd(gx1, y, mod, g_post, w_out, oa, ob, g_mix_a, g_mix_b, l1, l4, l16):
    def body(gx_ref, y_ref, mod_ref, gp_ref, w_ref, oa_ref, ob_ref, ga_ref, gb_ref, l1_ref, l4_ref, l16_ref, e_ref, g_ref,
             dy_ref, doa_ref, do1_ref, do4_ref, do16_ref, da_ref, d1_ref, d4_ref, d16_ref, accb_ref, accg_ref, scr):
        _acc_init(accb_ref, accg_ref)
        w1, w4, w16 = _branch_weights(l1_ref, l4_ref, l16_ref, scr)
        e, hs = e_ref[...], g_ref[...]
        gx1v = gx_ref[...]
        yn, ry = _rms(y_ref[...])
        gp = gp_ref[...]
        gt = mod_ref[2:3, :]
        dn1 = gx1v * gt
        dy = _rms_bwd(dn1 * gp, yn, ry).astype(bf16)
        dy_ref[...] = dy
        dmixed = _dot_nt(dy, w_ref[...])
        dma, dmb = dmixed[:, :AQ], dmixed[:, AQ:]
        oa, ob = oa_ref[...], ob_ref[...]
        oan, ra = _rms(oa)
        obn, rb = _rms(ob)
        doa = _rms_bwd(dma * ga_ref[...], oan, ra)
        doa_ref[...] = doa.astype(bf16)
        da_ref[...] = _lanes_to_heads(doa * oa, hs)
        dob = _rms_bwd(dmb * gb_ref[...], obn, rb)
        dd = _lanes_to_heads(dob * ob, hs)
        x1w, x4w = _heads_to_lanes(w1, e), _heads_to_lanes(w4, e)
        do1_ref[...] = (x1w * dob).astype(bf16)
        d1_ref[...] = w1 * dd
        _perm_store(x4w * dob, scr, do4_ref, 4)
        _perm_store(w4 * dd, scr, d4_ref, 4)
        _perm_store((1.0 - x1w - x4w) * dob, scr, do16_ref, 16)
        _perm_store(w16 * dd, scr, d16_ref, 16)
        accb_ref[0:1, :] += _colsum(gx1v * (yn * gp))
        accg_ref[0:1, :] += _colsum(dn1 * yn)
        accg_ref[1:2, :] += jnp.concatenate([_colsum(dma * oan), _colsum(dmb * obn)], axis=1)

    nat = lambda w, dt: jax.ShapeDtypeStruct((BL, SEQ, w), dt)
    return pl.pallas_call(
        body, name="attn_out_bwd", grid=(BL, NJ),
        in_specs=[_tok(D), _tok(D), MOD_SPEC, _full((1, D)), _full((D, D)), _tok(AQ), _tok(BW), _full((1, AQ)), _full((1, BW)),
                  _tok(LANES), _perm_spec(4, LANES), _perm_spec(16, LANES), _full((LANES, BW)), _full((BW, LANES))],
        out_specs=[_tok(D), _tok(AQ), _tok(BW), _perm_spec(4, BW), _perm_spec(16, BW),
                   _tok(LANES), _tok(LANES), _perm_spec(4, LANES), _perm_spec(16, LANES), ACCB_SPEC, ACCG_SPEC],
        out_shape=[nat(D, bf16), nat(AQ, bf16), nat(BW, bf16), jax.ShapeDtypeStruct((BL, 4, SEQ // 4, BW), bf16),
                   jax.ShapeDtypeStruct((BL, 16, SEQ // 16, BW), bf16), nat(LANES, f32), nat(LANES, f32),
                   jax.ShapeDtypeStruct((BL, 4, SEQ // 4, LANES), f32), jax.ShapeDtypeStruct((BL, 16, SEQ // 16, LANES), f32)]
                  + ACC_SHAPES,
        scratch_shapes=[pltpu.VMEM((BW // LANES, TM, LANES), f32)],
        compiler_params=_cp(("arbitrary", "arbitrary")),
    )(gx1, y, mod, g_post, w_out, oa, ob, g_mix_a, g_mix_b, l1, l4, l16, jnp.asarray(HEAD_EXPAND, bf16),
      jnp.asarray(HEAD_SUM, bf16))


def _attn_in_bwd(dqa, dka, dva, d1, d4, d16, tc, ts1, ts2, w_in, x, gx1, mod, g_pre):
    def body(dqa_ref, dka_ref, dva_ref, dq1_ref, dk1_ref, dv1_ref, dq4_ref, dk4_ref, dv4_ref, dq16_ref, dk16_ref, dv16_ref,
             c_ref, s1_ref, s2_ref, w_ref, x_ref, gx_ref, mod_ref, g_ref, dproj_ref, dx_ref, accb_ref, accg_ref, scr):
        _acc_init(accb_ref, accg_ref)
        c, s1, s2 = c_ref[...], s1_ref[...], s2_ref[...]
        tot = lambda r1, r4, r16: r1[...] + _perm_load(r4, scr, 4) + _perm_load(r16, scr, 16)
        dqb = tot(dq1_ref, dq4_ref, dq16_ref)
        dkb = tot(dk1_ref, dk4_ref, dk16_ref)
        dvb = tot(dv1_ref, dv4_ref, dv16_ref)
        dproj = jnp.concatenate([
            _rope_t(dqa_ref[...], c, s1, s2) * QSCALE, _rope_t(_per_kv_head(dka_ref[...]), c, s1, s2),
            _per_kv_head(dva_ref[...]),
            _rope_t(dqb, c, s1, s2) * QSCALE, _rope_t(dkb, c, s1, s2), dvb], axis=1).astype(bf16)
        dproj_ref[...] = dproj
        dh = _dot_nt(dproj, w_ref[...])
        xn, r = _rms(x_ref[...])
        g = g_ref[...]
        dn = dh * (1.0 + mod_ref[1:2, :])
        dx_ref[...] = gx_ref[...] + _rms_bwd(dn * g, xn, r)
        accb_ref[0:1, :] += _colsum(dh * (xn * g))
        accb_ref[1:2, :] += _colsum(dh)
        accg_ref[0:1, :] += _colsum(dn * xn)

    return pl.pallas_call(
        body, name="attn_in_bwd", grid=(BL, NJ),
        in_specs=[_tok(AQ), _tok(AQ), _tok(AQ)] + [_tok(BW)] * 3 + [_perm_spec(4, BW)] * 3 + [_perm_spec(16, BW)] * 3
                 + [_tok(LANES)] * 3 + [_full((D, INW)), _tok(D), _tok(D), MOD_SPEC, _full((1, D))],
        out_specs=[_tok(INW), _tok(D), ACCB_SPEC, ACCG_SPEC],
        out_shape=[jax.ShapeDtypeStruct((BL, SEQ, INW), bf16), jax.ShapeDtypeStruct((BL, SEQ, D), f32)] + ACC_SHAPES,
        scratch_shapes=[pltpu.VMEM((BW // LANES, TM, LANES), f32)],
        compiler_params=_cp(("arbitrary", "arbitrary")),
    )(dqa, dka, dva, *d1, *d4, *d16, tc, ts1, ts2, w_in, x, gx1, mod, g_pre)


def _inv_lane():
    inv = np.float32(THETA) ** (-np.arange(0, ROT, 2, dtype=np.float32) / np.float32(ROT))
    lane = np.arange(LANES) % HD
    return jnp.asarray(np.where(lane < ROT, inv[lane % (ROT // 2)], 0.0).astype(np.float32)[None, :])


def _local_step(x, positions, mod, target, inv_lane, first_weight, later_weights, grad_ready, g_attn_pre,
                g_attn_post, sink_a, g_mix_a, g_mix_b, g_mlp_pre, g_mlp_post):
    tabs = _rope_tables(positions.reshape(BL * SEQ, 1), inv_lane)
    w_in = first_weight(tuple(tabs))
    tc, ts1, ts2 = [t.reshape(BL, SEQ, LANES) for t in tabs]

    (h, qa, ka, va, q1, k1, v1, q4, k4, v4, q16, k16, v16, w_in) = _attn_in(x, mod, g_attn_pre, w_in, tc, ts1, ts2)
    seqs = lambda t: t.reshape(t.shape[0] * t.shape[1], t.shape[2], t.shape[3])
    q4, k4, v4, q16, k16, v16 = [seqs(t) for t in (q4, k4, v4, q16, k16, v16)]
    oa, la = _attn_fwd(qa, ka, va, sink_a, max_dist=BLK - 1, o_dtype=f32, name="attn_a_fwd")
    o1, l1 = _attn_fwd(q1, k1, v1, None, max_dist=BLK, o_dtype=bf16, name="attn_b1_fwd")
    o4, l4 = _attn_fwd(q4, k4, v4, None, max_dist=BLK, o_dtype=bf16, name="attn_b4_fwd")
    o16, l16 = _attn_fwd(q16, k16, v16, None, max_dist=BLK, o_dtype=bf16, name="attn_b16_fwd")
    b4 = lambda t: t.reshape(BL, 4, SEQ // 4, t.shape[-1])
    b16 = lambda t: t.reshape(BL, 16, SEQ // 16, t.shape[-1])
    w_out, mlp_weights, mod = later_weights((oa, o1, o4, o16), mod)
    x1, y, mixed, ob = _mix_out(oa, o1, l1, b4(o4), b4(l4), b16(o16), b16(l16), g_mix_a, g_mix_b, w_out, x, mod, g_attn_post)
    w_up, w_down = mlp_weights((x1,))
    h2, u, a = _mlp_up(x1, mod, g_mlp_pre, w_up)
    gx, dy2, accb_d, accg_d = _mlp_down(a, w_down, x1, target, mod, g_mlp_post)

    flat = lambda t: t.reshape(BL * SEQ, t.shape[-1])
    mod = grad_ready("w_down", _matmul_tn(flat(a), flat(dy2), tn=D, col_blocked=False, name="grad_w_down", out_dtype=bf16), mod)
    du, gx1, accb_m, accg_m = _mlp_bwd(dy2, u, w_down, w_up, x1, gx, mod, g_mlp_pre)
    mod = grad_ready("w_up", _matmul_tn(flat(h2), flat(du), tn=D, col_blocked=True, name="grad_w_up", out_dtype=bf16), mod)

    dy, doa, do1, do4, do16, da, dl1, dl4, dl16, accb_o, accg_o = _attn_out_bwd(
        gx1, y, mod, g_attn_post, w_out, oa, ob, g_mix_a, g_mix_b, l1, b4(l4), b16(l16))
    gw_out = _matmul_tn(flat(mixed), flat(dy), tn=D, col_blocked=False, name="grad_w_out")
    dqa, dka, dva, dsink = _attn_bwd(qa, ka, va, doa, da, la, sink_a, max_dist=BLK - 1, name="attn_a_bwd")
    d1 = _attn_bwd(q1, k1, v1, do1, dl1, l1, None, max_dist=BLK, name="attn_b1_bwd")
    d4 = _attn_bwd(q4, k4, v4, seqs(do4), seqs(dl4), l4, None, max_dist=BLK, name="attn_b4_bwd")
    d16 = _attn_bwd(q16, k16, v16, seqs(do16), seqs(dl16), l16, None, max_dist=BLK, name="attn_b16_bwd")
    dproj, grad_x, accb_i, accg_i = _attn_in_bwd(dqa, dka, dva, d1, [b4(t) for t in d4], [b16(t) for t in d16],
                                                 tc, ts1, ts2, w_in, x, gx1, mod, g_attn_pre)
    gw_in = _grad_w_in(flat(h), flat(dproj))
    dsink = grad_ready("w_in_w_out", (gw_in, gw_out), dsink)

    return grad_x, (accb_i, accb_o, accb_m, accb_d, accg_i, accg_o, accg_m, accg_d, dsink)


ADAW = NMOD * D // NCHIP


def _pos():
    return lax.axis_index("x"), lax.axis_index("y"), lax.axis_index("c")


def _flip(v, bit):
    return 1 - v if bit else v


def _all_peers(x, y, c):
    return [(_flip(x, k >> 2 & 1), _flip(y, k >> 1 & 1), _flip(c, k & 1)) for k in range(1, NDEV)]


def _other_chips(x, y):
    return [(1 - x, y), (x, 1 - y), (1 - x, 1 - y)]


def _rcopy(src, dst, send, recv, k, dev, k_recv=None):
    return pltpu.make_async_remote_copy(src_ref=src, dst_ref=dst, send_sem=send.at[k],
                                        recv_sem=recv.at[k if k_recv is None else k_recv],
                                        device_id=dev, device_id_type=MESH)


def _gather_small(src, buf, send, recv):
    x, y, c = _pos()
    me = 4 * x + 2 * y + c
    peers = _all_peers(x, y, c)
    sends = [_rcopy(src, buf.at[me], send, recv, k, p) for k, p in enumerate(peers)]
    for cp in sends:
        cp.start()
    for k, (px, py, pc) in enumerate(peers):
        _rcopy(src, buf.at[4 * px + 2 * py + pc], send, recv, k, (px, py, pc)).wait_recv()
    for cp in sends:
        cp.wait_send()
    return me


def _ada_fwd(c_in, w_ada, b_cols):
    def body(c_ref, w_hbm, b_ref, mod_ref, cond_ref, cbuf, mbuf, w_ref, s1, r1, s2, r2, wsem):
        x, y, c = _pos()
        chip = 2 * x + y
        wcopy = pltpu.make_async_copy(w_hbm, w_ref, wsem)
        wcopy.start()
        me = _gather_small(c_ref, cbuf, s1, r1)
        cbuf[me] = c_ref[...]
        for i in range(NDEV):
            cond_ref[BL * i:BL * (i + 1), :] = cbuf[i]
        call = cond_ref[...]
        cond = call / (1.0 + jnp.exp(-call))
        cond_ref[...] = cond
        wcopy.wait()
        mbuf[chip] = _dot(cond.astype(bf16), w_ref[...].astype(bf16)) + b_ref[...]
        chips = _other_chips(x, y)
        sends = [_rcopy(mbuf.at[chip], mbuf.at[chip], s2, r2, j, (px, py, c)) for j, (px, py) in enumerate(chips)]
        for cp in sends:
            cp.start()
        for j, (px, py) in enumerate(chips):
            _rcopy(mbuf.at[chip], mbuf.at[2 * px + py], s2, r2, j, (px, py, c)).wait_recv()
        for cp in sends:
            cp.wait_send()
        row = lax.broadcasted_iota(jnp.int32, (BL * NDEV, ADAW), 0)
        for s in range(NCHIP):
            slab = mbuf[s]
            for j in range(BL):
                mod_ref[j:j + 1, ADAW * s:ADAW * (s + 1)] = jnp.sum(jnp.where(row == BL * me + j, slab, 0.0), axis=0, keepdims=True)

    vm = pl.BlockSpec(memory_space=pltpu.VMEM)
    return pl.pallas_call(
        body, name="ada_fwd", in_specs=[vm, pl.BlockSpec(memory_space=pl.ANY), vm], out_specs=[vm, vm],
        out_shape=[jax.ShapeDtypeStruct((BL, NMOD * D), f32), jax.ShapeDtypeStruct((BL * NDEV, D), f32)],
        scratch_shapes=[pltpu.VMEM((NDEV, BL, D), f32), pltpu.VMEM((NCHIP, BL * NDEV, ADAW), f32),
                        pltpu.VMEM((D, ADAW), f32),
                        pltpu.SemaphoreType.DMA((NDEV - 1,)), pltpu.SemaphoreType.DMA((NDEV - 1,)),
                        pltpu.SemaphoreType.DMA((NCHIP - 1,)), pltpu.SemaphoreType.DMA((NCHIP - 1,)),
                        pltpu.SemaphoreType.DMA],
        compiler_params=pltpu.CompilerParams(vmem_limit_bytes=VMEM_LIMIT),
    )(c_in, w_ada, b_cols)


def _small_allreduce(accs, cond_all):
    def body(bi, bo, bm, bd, gi, go, gm, gd, dsink, cond_ref, gw_ref, gb_ref, small_ref, pay, pbuf, dall, s1, r1):
        x, y, c = _pos()
        chip = 2 * x + y
        pay[...] = jnp.zeros_like(pay)
        for b in range(BL):
            for k, (ref, r) in enumerate(((bi, 1), (bi, 0), (bo, 0), (bm, 1), (bm, 0), (bd, 0))):
                pay[b:b + 1, D * k:D * (k + 1)] = ref[b, r:r + 1, :]
        for off, ref, r in ((OFF_G_ATTN_PRE, gi, 0), (OFF_G_ATTN_POST, go, 0), (OFF_G_MIX_A, go, 1), (OFF_G_MLP_PRE, gm, 0),
                            (OFF_G_MLP_POST, gd, 0)):
            pay[BL:BL + 1, off:off + D] = ref[r:r + 1, :]
        eye = lax.broadcasted_iota(jnp.int32, (NHEAD, LANES), 0) == lax.broadcasted_iota(jnp.int32, (NHEAD, LANES), 1)
        pay[BL:BL + 1, OFF_SINK:OFF_SINK + LANES] = jnp.sum(jnp.where(eye, dsink[...], 0.0), axis=0, keepdims=True)
        pay[BL:BL + 1, OFF_LOSS:OFF_LOSS + LANES] = gd[1:2, 0:LANES]
        me = _gather_small(pay, pbuf, s1, r1)
        pbuf[me] = pay[...]
        small = pbuf[0, BL:BL + 1, :]
        for i in range(1, NDEV):
            small = small + pbuf[i, BL:BL + 1, :]
        small_ref[...] = small
        for i in range(NDEV):
            dall[BL * i:BL * (i + 1), :] = pbuf[i, 0:BL, :]
        gb_ref[...] = jnp.sum(dall[...], axis=0, keepdims=True)
        cols = jnp.zeros((BL * NDEV, ADAW), f32)
        for s in range(NCHIP):
            cols = cols + jnp.where(chip == s, dall[:, ADAW * s:ADAW * (s + 1)], 0.0)
        gw_ref[...] = _dot_tn(cond_ref[...].astype(bf16), cols.astype(bf16))

    vm = pl.BlockSpec(memory_space=pltpu.VMEM)
    return pl.pallas_call(
        body, name="small_allreduce", in_specs=[vm] * 10, out_specs=[vm] * 3,
        out_shape=[jax.ShapeDtypeStruct((D, ADAW), f32), jax.ShapeDtypeStruct((1, PAYW), f32), jax.ShapeDtypeStruct((1, PAYW), f32)],
        scratch_shapes=[pltpu.VMEM((4, PAYW), f32), pltpu.VMEM((NDEV, 4, PAYW), f32), pltpu.VMEM((BL * NDEV, PAYW), f32),
                        pltpu.SemaphoreType.DMA((NDEV - 1,)), pltpu.SemaphoreType.DMA((NDEV - 1,))],
        compiler_params=pltpu.CompilerParams(vmem_limit_bytes=VMEM_LIMIT),
    )(*accs, cond_all)


def _half(ref, c):
    r2 = ref.shape[0] // 2
    return ref.at[pl.ds(c * r2 if isinstance(c, int) else pl.multiple_of(c * r2, 16), r2), :]


HBM_SPEC = pl.BlockSpec(memory_space=pltpu.HBM)
SEM_SPEC = pl.BlockSpec(memory_space=pltpu.SEMAPHORE)
EFFECT = pltpu.SideEffectType.DATAFLOW_SIDE_EFFECTING
NLINK = NCHIP - 1


def _in_hbm(a):
    return pltpu.with_memory_space_constraint(a, pltpu.HBM)


NSEM = 8


def _split_start(name, srcs, land_shapes, builds, carry, after=(), lands=None):
    n = len(srcs)
    na, nc = len(after), len(carry)

    def body(*refs):
        src, land = refs[:n], refs[n:2 * n]
        kept = refs[2 * n + na:2 * n + na + nc]
        outs = refs[2 * n + na + nc:]
        send, recv, passed = outs[:n], outs[n:2 * n], outs[4 * n:]
        for t in range(n):
            for out_cp, _ in builds[t](src[t], land[t], send[t], recv[t]):
                out_cp.start()
        for a, b in zip(kept, passed):
            b[...] = a[...]

    if lands is None:
        lands = [lax.empty(s.shape, s.dtype) for s in land_shapes]
    lands = [_in_hbm(a) for a in lands]
    sems = [pltpu.SemaphoreType.DMA((NSEM,))] * (2 * n)
    thru = [pltpu.HBM(a.shape, a.dtype) for a in list(srcs) + lands]
    vm = pl.BlockSpec(memory_space=pltpu.VMEM)
    res = pl.pallas_call(
        body, name=name, out_shape=sems + thru + [jax.ShapeDtypeStruct(a.shape, a.dtype) for a in carry],
        in_specs=[HBM_SPEC] * (2 * n) + [pl.BlockSpec(memory_space=pl.ANY)] * na + [vm] * nc,
        out_specs=[SEM_SPEC] * (2 * n) + [HBM_SPEC] * (2 * n) + [vm] * nc,
        input_output_aliases={i: 2 * n + i for i in range(2 * n)},
        compiler_params=pltpu.CompilerParams(has_side_effects=EFFECT),
    )(*[_in_hbm(a) for a in srcs], *lands, *after, *carry)
    flight = [(res[2 * n + t], res[3 * n + t], res[t], res[n + t]) for t in range(n)]
    return flight, list(res[4 * n:])


def _split_wait(name, flight, builds, after):
    m = len(flight)
    na = len(after)

    def body(*refs):
        src, land, send, recv = refs[:m], refs[m:2 * m], refs[2 * m:3 * m], refs[3 * m:4 * m]
        for t in range(m):
            for out_cp, in_cp in builds[t](src[t], land[t], send[t], recv[t]):
                out_cp.wait_send()
                in_cp.wait_recv()

    ops = [f[0] for f in flight] + [f[1] for f in flight] + [f[2] for f in flight] + [f[3] for f in flight]
    res = pl.pallas_call(
        body, name=name, out_shape=[pltpu.HBM(a.shape, a.dtype) for a in ops[:2 * m]],
        in_specs=[HBM_SPEC] * (2 * m) + [SEM_SPEC] * (2 * m) + [pl.BlockSpec(memory_space=pl.ANY)] * na,
        out_specs=[HBM_SPEC] * (2 * m), input_output_aliases={i: i for i in range(2 * m)},
        compiler_params=pltpu.CompilerParams(has_side_effects=EFFECT),
    )(*ops, *after)
    return res[:m], res[m:2 * m]


def _weight_copies(src, land, send, recv):
    x, y, c = _pos()
    chip = 2 * x + y
    return [(_rcopy(_half(src, c), _half(land.at[chip], c), send, recv, j, (px, py, c)),
             _rcopy(_half(src, c), _half(land.at[2 * px + py], c), send, recv, j, (px, py, c)))
            for j, (px, py) in enumerate(_other_chips(x, y))]


def _grad_copies(src, land, send, recv):
    x, y, c = _pos()
    return [(_rcopy(src.at[2 * px + py], land.at[j], send, recv, j, (px, py, c)),
             _rcopy(src.at[2 * px + py], land.at[j], send, recv, j, (px, py, c)))
            for j, (px, py) in enumerate(_other_chips(x, y))]


NDIRECT = NDEV - 1


def _direct_grad_copies(src, land, send, recv):
    x, y, c = _pos()
    out, arrive = [], []
    for j, (px, py) in enumerate(_other_chips(x, y)):
        for hc in range(2):
            out.append(_rcopy(_half(src.at[2 * px + py], hc), land.at[2 * j + c], send, recv, 2 * j + hc, (px, py, hc),
                              k_recv=2 * j + c))
            arrive.append(_rcopy(_half(src.at[2 * px + py], hc), land.at[2 * j + hc], send, recv, 2 * j + hc, (px, py, hc)))
    own = _rcopy(_half(src.at[2 * x + y], 1 - c), land.at[NDIRECT - 1], send, recv, NDIRECT - 1, (x, y, 1 - c))
    return list(zip(out, arrive)) + [(own, own)]


def _pair_grad_copies(src, land, send, recv):
    x, y, c = _pos()
    r2 = src.shape[1] // 2
    cp = _rcopy(src.at[:, pl.ds(pl.multiple_of((1 - c) * r2, 8), r2), :], land, send, recv, 0, (x, y, 1 - c))
    return [(cp, cp)]


def _pair_weight_copies(src, land, send, recv):
    x, y, c = _pos()
    sib = (x, y, 1 - c)
    cps = []
    for j, (px, py) in enumerate(_other_chips(x, y)):
        mine, theirs = _half(land.at[2 * px + py], c), _half(land.at[2 * px + py], 1 - c)
        cps.append((_rcopy(mine, mine, send, recv, j, sib), _rcopy(theirs, theirs, send, recv, j, sib)))
    own = _rcopy(src, land.at[2 * x + y], send, recv, NLINK, sib)
    return cps + [(own, own)]


RS_ROWS = 128


def _pair_add(g, landed, c_arr, name):
    _, r2, cw = landed.shape
    nr = r2 // RS_ROWS

    def body(c_ref, g_ref, p_ref, o_ref):
        o_ref[...] = (g_ref[...] + p_ref[...]).astype(bf16)

    gs = pltpu.PrefetchScalarGridSpec(
        num_scalar_prefetch=1, grid=(NCHIP, nr),
        in_specs=[pl.BlockSpec((None, RS_ROWS, cw), lambda s, j, c: (s, c[0] * nr + j, 0)),
                  pl.BlockSpec((None, RS_ROWS, cw), lambda s, j, c: (s, j, 0))],
        out_specs=pl.BlockSpec((None, RS_ROWS, cw), lambda s, j, c: (s, j, 0)))
    return pl.pallas_call(body, name=name, grid_spec=gs, out_shape=jax.ShapeDtypeStruct((NCHIP, r2, cw), bf16),
                          compiler_params=_cp(("arbitrary", "arbitrary")))(c_arr, g, landed)


def _chip_add(own, landed, pos_arr, name):
    nl, r2, cw = landed.shape
    nr = r2 // RS_ROWS
    whole = own.shape[1] == 2 * r2

    def body(s_ref, h_ref, q_ref, o_ref):
        acc = h_ref[...].astype(f32)
        for j in range(nl):
            acc = acc + q_ref[j].astype(f32)
        o_ref[...] = acc

    gs = pltpu.PrefetchScalarGridSpec(
        num_scalar_prefetch=1, grid=(nr,),
        in_specs=[pl.BlockSpec((None, RS_ROWS, cw), lambda j, s: (s[0], (s[1] * nr if whole else 0) + j, 0)),
                  pl.BlockSpec((nl, RS_ROWS, cw), lambda j, s: (0, j, 0))],
        out_specs=pl.BlockSpec((RS_ROWS, cw), lambda j, s: (s[1] * nr + j, 0)))
    return pl.pallas_call(body, name=name, grid_spec=gs, out_shape=jax.ShapeDtypeStruct((2 * r2, cw), f32),
                          compiler_params=_cp(("arbitrary",)))(pos_arr, own, landed)


def _pair_gather_copies(src, land, send, recv):
    x, y, c = _pos()
    sib = (x, y, 1 - c)
    return [(_rcopy(_half(land, c), _half(land, c), send, recv, 0, sib),
             _rcopy(_half(land, 1 - c), _half(land, 1 - c), send, recv, 0, sib))]


def _adamw_math(w, g, m, v):
    m = B1 * m + (1.0 - B1) * g
    v = B2 * v + (1.0 - B2) * jnp.square(g)
    m_hat = m / (1.0 - B1 ** STEP)
    v_hat = v / (1.0 - B2 ** STEP)
    return -LR * (m_hat / (jnp.sqrt(v_hat) + AEPS) + WD * w), m, v


ADAM_ROWS = 256


def _adamw(w, g, m, v, name):
    r, cw = w.shape

    def body(w_ref, g_ref, m_ref, v_ref, go_ref, d_ref, mo_ref, vo_ref):
        g = g_ref[...]
        go_ref[...] = g
        d_ref[...], mo_ref[...], vo_ref[...] = _adamw_math(w_ref[...], g, m_ref[...], v_ref[...])

    rows = max(k for k in range(8, ADAM_ROWS + 1, 8) if r % k == 0)
    spec = pl.BlockSpec((rows, cw), lambda i: (i, 0))
    return pl.pallas_call(body, name=name, grid=(r // rows,), in_specs=[spec] * 4, out_specs=[spec] * 4,
                          out_shape=[jax.ShapeDtypeStruct((r, cw), f32)] * 4, compiler_params=_cp(("arbitrary",)))(w, g, m, v)


SMALL = (("b_ada", None, PAYW), ("g_attn_pre", OFF_G_ATTN_PRE, D), ("g_attn_post", OFF_G_ATTN_POST, D), ("sink_a", OFF_SINK, 8),
         ("g_mix_a", OFF_G_MIX_A, AQ), ("g_mix_b", OFF_G_MIX_B, BW), ("g_mlp_pre", OFF_G_MLP_PRE, D), ("g_mlp_post", OFF_G_MLP_POST, D))


def _adamw_small(small, gb, params):
    n = len(SMALL)

    def body(*refs):
        small_ref, gb_ref = refs[:2]
        wmv = refs[2:2 + 3 * n]
        loss_ref = refs[2 + 3 * n]
        outs = refs[3 + 3 * n:]
        loss_ref[...] = small_ref[:, OFF_LOSS:OFF_LOSS + 1] * (0.5 / D)
        for i, (_, off, width) in enumerate(SMALL):
            g = gb_ref[...] if off is None else small_ref[:, off:off + width]
            w_ref, m_ref, v_ref = wmv[3 * i:3 * i + 3]
            outs[4 * i][...] = g
            outs[4 * i + 1][...], outs[4 * i + 2][...], outs[4 * i + 3][...] = _adamw_math(w_ref[...], g, m_ref[...], v_ref[...])

    vm = pl.BlockSpec(memory_space=pltpu.VMEM)
    out_shape = [jax.ShapeDtypeStruct((1, 1), f32)]
    for _, _, width in SMALL:
        out_shape += [jax.ShapeDtypeStruct((1, width), f32)] * 4
    flat = [a for wmv in params for a in wmv]
    res = pl.pallas_call(body, name="adamw_small", in_specs=[vm] * (2 + 3 * n), out_specs=[vm] * len(out_shape),
                         out_shape=out_shape)(small, gb, *flat)
    return res[0], {name: res[1 + 4 * i:5 + 4 * i] for i, (name, _, _) in enumerate(SMALL)}


def kernel(x, c, positions, w_ada, b_ada, g_attn_pre, g_attn_post, w_in, sink_a, g_mix_a, g_mix_b, w_out, g_mlp_pre, g_mlp_post, w_up, w_down, loss_target, m_w_ada, m_b_ada, m_g_attn_pre, m_g_attn_post, m_w_in, m_sink_a, m_g_mix_a, m_g_mix_b, m_w_out, m_g_mlp_pre, m_g_mlp_post, m_w_up, m_w_down, v_w_ada, v_b_ada, v_g_attn_pre, v_g_attn_post, v_w_in, v_sink_a, v_g_mix_a, v_g_mix_b, v_w_out, v_g_mlp_pre, v_g_mlp_post, v_w_up, v_w_down):
    given = dict(w_ada=w_ada, b_ada=b_ada, g_attn_pre=g_attn_pre, g_attn_post=g_attn_post, w_in=w_in, sink_a=sink_a, g_mix_a=g_mix_a,
                 g_mix_b=g_mix_b, w_out=w_out, g_mlp_pre=g_mlp_pre, g_mlp_post=g_mlp_post, w_up=w_up, w_down=w_down)
    moms = dict(w_ada=(m_w_ada, v_w_ada), b_ada=(m_b_ada, v_b_ada), g_attn_pre=(m_g_attn_pre, v_g_attn_pre),
                g_attn_post=(m_g_attn_post, v_g_attn_post), w_in=(m_w_in, v_w_in), sink_a=(m_sink_a, v_sink_a),
                g_mix_a=(m_g_mix_a, v_g_mix_a), g_mix_b=(m_g_mix_b, v_g_mix_b), w_out=(m_w_out, v_w_out),
                g_mlp_pre=(m_g_mlp_pre, v_g_mlp_pre), g_mlp_post=(m_g_mlp_post, v_g_mlp_post), w_up=(m_w_up, v_w_up),
                w_down=(m_w_down, v_w_down))
    order = ["w_ada", "b_ada", "g_attn_pre", "g_attn_post", "w_in", "sink_a", "g_mix_a", "g_mix_b", "w_out", "g_mlp_pre",
             "g_mlp_post", "w_up", "w_down"]
    xi, yi, ci = _pos()
    chip = 2 * xi + yi

    c_arr = jnp.reshape(ci, (1,)).astype(jnp.int32)
    pos_arr = jnp.stack([chip, ci]).astype(jnp.int32)
    big = ("w_in", "w_out", "w_up", "w_down")

    b_cols = lax.dynamic_slice(b_ada, (0, chip * ADAW), (1, ADAW))
    mod, cond_all = _ada_fwd(c, w_ada[0], b_cols)
    gathered = [jax.ShapeDtypeStruct((NCHIP,) + given[n].shape[1:], bf16) for n in big]
    flight_in, (mod,) = _split_start("weights_start_first", [w_in[0].astype(bf16)], gathered[:1], [_weight_copies], [mod])
    mod, rest = lax.optimization_barrier((mod, [given[n][0] for n in big[1:]]))
    flight_rest, (mod, inv_lane) = _split_start("weights_start_rest", [w.astype(bf16) for w in rest], gathered[1:],
                                                [_weight_copies] * 3, [mod, _inv_lane()])
    mod = mod.reshape(BL, NMOD, D)

    def first_weight(after):
        srcs, lands = _split_wait("weights_wait_first", flight_in, [_weight_copies], after)
        cross, _ = _split_start("weights_pair_start_first", srcs, None, [_pair_weight_copies], [], lands=lands)
        _, (win_g,) = _split_wait("weights_pair_wait_first", cross, [_pair_weight_copies], ())
        return win_g

    def later_weights(after, carry):
        srcs, lands = _split_wait("weights_wait_rest", flight_rest, [_weight_copies] * 3, after)
        fl, (carry,) = _split_start("weights_pair_start_rest", srcs, None, [_pair_weight_copies] * 3, [carry], lands=lands)
        _, (wout_g,) = _split_wait("weights_pair_wait_out", fl[:1], [_pair_weight_copies], ())

        def mlp_weights(after):
            _, (wup_g, wdn_g) = _split_wait("weights_pair_wait_mlp", fl[1:], [_pair_weight_copies] * 2, after)
            return wup_g, wdn_g.reshape(DFF, D)

        return wout_g.reshape(D, D), mlp_weights, carry

    crossing, pending = {}, {}

    def grad_ready(group, g, carry):
        if group != "w_in_w_out":
            slab = g.reshape(NCHIP, DFF // NCHIP, D) if group == "w_down" else g
            land = jax.ShapeDtypeStruct((NDIRECT, slab.shape[1] // 2, slab.shape[2]), bf16)
            fl, (carry,) = _split_start("grad_start_" + group, [slab], [land], [_direct_grad_copies], [carry])
            pending[group] = ((group,), fl, [_direct_grad_copies])
            return carry
        names = ("w_in", "w_out")
        slabs = [g[0], g[1].reshape(NCHIP, D // NCHIP, D)]
        fl, (carry,) = _split_start("grad_pair_start_" + group, slabs,
                                    [jax.ShapeDtypeStruct((NCHIP, s.shape[1] // 2, s.shape[2]), f32) for s in slabs],
                                    [_pair_grad_copies] * len(names), [carry])
        crossing[group] = (names, fl)
        return carry

    def grad_reduce(group, after, carry):
        names, fl = crossing[group]
        slabs, landed = _split_wait("grad_pair_wait_" + group, fl, [_pair_grad_copies] * len(names), after)
        halves = [_pair_add(s, p, c_arr, "grad_pair_sum_" + n) for s, p, n in zip(slabs, landed, names)]
        fl, (carry,) = _split_start("grad_start_" + group, halves,
                                    [jax.ShapeDtypeStruct((NLINK,) + h.shape[1:], bf16) for h in halves],
                                    [_grad_copies] * len(names), [carry])
        pending[group] = (names, fl, [_grad_copies] * len(names))
        return carry

    grad_x, accs = _local_step(x, positions, mod, loss_target, inv_lane, first_weight, later_weights, grad_ready,
                               g_attn_pre, g_attn_post, sink_a, g_mix_a, g_mix_b, g_mlp_pre, g_mlp_post)

    grads, out = {}, {}

    def update(n):
        tr = (lambda a: a.T) if n == "w_in" else (lambda a: a)
        res = _adamw(tr(given[n][0]), tr(grads[n]), tr(moms[n][0][0]), tr(moms[n][1][0]), "adamw_" + n)
        out[n] = tuple(tr(a)[None] for a in res)
        return res[3]

    def finish(groups, after):
        names = sum((pending[g][0] for g in groups), ())
        fl = sum((pending[g][1] for g in groups), [])
        halves, landed = _split_wait("grad_wait_" + groups[0], fl, sum((pending[g][2] for g in groups), []), after)
        flights = []
        for h, q, n in zip(halves, landed, names):
            full = _chip_add(h, q, pos_arr, "grad_chip_sum_" + n)
            flights.append(_split_start("grad_gather_start_" + n, [jnp.zeros((8, LANES), f32)], None, [_pair_gather_copies],
                                        [], lands=[full])[0])
        last = None
        for n, fl1 in zip(names, flights):
            after = (flights[-1][0][0],) if last is None and fl1 is not flights[-1] else () if last is None else (last,)
            _, (grads[n],) = _split_wait("grad_gather_wait_" + n, fl1, [_pair_gather_copies], after)
            last = update(n)
        return last

    grads["w_ada"], gb, small = _small_allreduce(accs, cond_all)
    small = grad_reduce("w_in_w_out", (small,), small)
    last = finish(("w_down", "w_up"), (small,))
    finish(("w_in_w_out",), (last, update("w_ada")))
    loss, res = _adamw_small(small, gb, [(given[n], moms[n][0], moms[n][1]) for n, _, _ in SMALL])
    for n, _, _ in SMALL:
        out[n] = tuple(res[n])
    return (loss.reshape(()), grad_x, *[out[n][0] for n in order], *[out[n][1] for n in order],
            *[out[n][2] for n in order], *[out[n][3] for n in order])
```

```python
import numpy as np
import jax
import jax.numpy as jnp
from jax import lax
from jax.experimental import pallas as pl
from jax.experimental.pallas import tpu as pltpu

f32 = jnp.float32
bf16 = jnp.bfloat16
MESH = pl.DeviceIdType.MESH

D = 1024
SEQ = 2048
BL = 2
HD = 64
AQ = 512
AKV = 128
BW = 512
INW = 2304
DFF = 4096
NMOD = 6
ROT = 16
THETA = 500000.0
EPS = 1e-6
NEG = -1e30
BLK = 128
TM = 512
NJ = SEQ // TM
LANES = 128
SUBLANES = 8
NHEAD = AQ // HD
QSCALE = HD ** -0.5
NCHIP = 4
NDEV = 8
VMEM_LIMIT = 56 << 20

LR, B1, B2, AEPS, WD, STEP = 0.001, 0.9, 0.999, 1e-08, 0.01, 10

OFF_G_ATTN_PRE, OFF_G_ATTN_POST, OFF_G_MIX_A, OFF_G_MIX_B = 0, 1024, 2048, 2560
OFF_G_MLP_PRE, OFF_G_MLP_POST, OFF_SINK, OFF_LOSS = 3072, 4096, 5120, 5248
PAYW = NMOD * D


def _cp(sem=None):
    return pltpu.CompilerParams(dimension_semantics=sem, vmem_limit_bytes=VMEM_LIMIT)


def _dot(a, b):
    return jnp.dot(a, b, preferred_element_type=f32)


def _dot_nt(a, b):
    return lax.dot_general(a, b, (((1,), (1,)), ((), ())), preferred_element_type=f32)


def _dot_tn(a, b):
    return lax.dot_general(a, b, (((0,), (0,)), ((), ())), preferred_element_type=f32)


def _rms(x):
    r = lax.rsqrt(jnp.mean(x * x, axis=-1, keepdims=True) + EPS)
    return x * r, r


def _rms_bwd(dy, y, r):
    return r * (dy - y * jnp.mean(dy * y, axis=-1, keepdims=True))


def _colsum(v):
    return jnp.sum(v, axis=0, keepdims=True)


def _rope(p, c, s1, s2):
    outs = []
    for c0 in range(0, p.shape[1], LANES):
        pc = p[:, c0:c0 + LANES]
        outs.append(pc * c + pltpu.roll(pc, LANES - ROT // 2, 1) * s1 + pltpu.roll(pc, ROT // 2, 1) * s2)
    return outs[0] if len(outs) == 1 else jnp.concatenate(outs, axis=1)


def _rope_t(g, c, s1, s2):
    outs = []
    for c0 in range(0, g.shape[1], LANES):
        gc = g[:, c0:c0 + LANES]
        outs.append(gc * c + pltpu.roll(gc * s1, ROT // 2, 1) + pltpu.roll(gc * s2, LANES - ROT // 2, 1))
    return outs[0] if len(outs) == 1 else jnp.concatenate(outs, axis=1)


def _perm_store(val, scr, out_ref, d):
    nc = val.shape[1] // LANES
    for c in range(nc):
        scr[c] = val[:, LANES * c:LANES * (c + 1)]
    for c in range(nc):
        for r in range(d):
            out_ref[r, :, LANES * c:LANES * (c + 1)] = scr[c, pl.ds(r, TM // d, stride=d), :].astype(out_ref.dtype)


def _perm_load(in_ref, scr, d):
    nc = in_ref.shape[-1] // LANES
    for c in range(nc):
        for r in range(d):
            scr[c, pl.ds(r, TM // d, stride=d), :] = in_ref[r, :, LANES * c:LANES * (c + 1)].astype(f32)
    return jnp.concatenate([scr[c] for c in range(nc)], axis=1)


def _per_query_head(kv):
    r = pltpu.roll(kv, HD, 1)
    lo = lax.broadcasted_iota(jnp.int32, kv.shape, 1) < HD
    return jnp.concatenate([jnp.where(lo, kv, r), jnp.where(lo, r, kv)], axis=1)


def _per_kv_head(g):
    g0, g1 = g[:, :LANES] + g[:, LANES:2 * LANES], g[:, 2 * LANES:3 * LANES] + g[:, 3 * LANES:]
    lo = lax.broadcasted_iota(jnp.int32, g0.shape, 1) < HD
    return jnp.where(lo, g0 + pltpu.roll(g0, HD, 1), g1 + pltpu.roll(g1, HD, 1))


def _tok(w):
    return pl.BlockSpec((None, TM, w), lambda b, j: (b, j, 0))


def _perm_spec(d, w):
    return pl.BlockSpec((None, d, TM // d, w), lambda b, j: (b, 0, j, 0))


def _full(shape):
    n = len(shape)
    return pl.BlockSpec(shape, lambda b, j: (0,) * n)


MOD_SPEC = pl.BlockSpec((None, NMOD, D), lambda b, j: (b, 0, 0))
ACCB_SPEC = pl.BlockSpec((None, SUBLANES, D), lambda b, j: (b, 0, 0))
ACCG_SPEC = pl.BlockSpec((SUBLANES, D), lambda b, j: (0, 0))
ACC_SHAPES = [jax.ShapeDtypeStruct((BL, SUBLANES, D), f32), jax.ShapeDtypeStruct((SUBLANES, D), f32)]


def _acc_init(accb_ref, accg_ref):
    b, j = pl.program_id(0), pl.program_id(1)

    @pl.when(j == 0)
    def _():
        accb_ref[...] = jnp.zeros_like(accb_ref)

    @pl.when((b == 0) & (j == 0))
    def _():
        accg_ref[...] = jnp.zeros_like(accg_ref)


def _rope_tables(pos_col, inv_lane):
    def body(p_ref, inv_ref, c_ref, s1_ref, s2_ref):
        ang = p_ref[...].astype(f32) * inv_ref[...]
        j = lax.broadcasted_iota(jnp.int32, (TM, LANES), 1) % HD
        cs, sn = jnp.cos(ang), jnp.sin(ang)
        c_ref[...] = jnp.where(j < ROT, cs, 1.0)
        s1_ref[...] = jnp.where(j < ROT // 2, -sn, 0.0)
        s2_ref[...] = jnp.where((j >= ROT // 2) & (j < ROT), sn, 0.0)

    n = BL * SEQ // TM
    return pl.pallas_call(
        body, name="rope_tables", grid=(n,),
        in_specs=[pl.BlockSpec((TM, 1), lambda i: (i, 0)), pl.BlockSpec((1, LANES), lambda i: (0, 0))],
        out_specs=[pl.BlockSpec((TM, LANES), lambda i: (i, 0))] * 3,
        out_shape=[jax.ShapeDtypeStruct((BL * SEQ, LANES), f32)] * 3,
    )(pos_col, inv_lane)


def _attn_in(x, mod, g_pre, w_in, tc, ts1, ts2):
    def body(x_ref, mod_ref, g_ref, wg_ref, c_ref, s1_ref, s2_ref,
             h_ref, qa_ref, ka_ref, va_ref, q1_ref, k1_ref, v1_ref, q4_ref, k4_ref, v4_ref, q16_ref, k16_ref, v16_ref,
             w_ref, scr):
        @pl.when((pl.program_id(0) == 0) & (pl.program_id(1) == 0))
        def _():
            w_ref[...] = jnp.concatenate([wg_ref[s] for s in range(NCHIP)], axis=1)

        xn, _ = _rms(x_ref[...])
        h = (xn * g_ref[...]) * (1.0 + mod_ref[1:2, :]) + mod_ref[0:1, :]
        hb = h.astype(bf16)
        h_ref[...] = hb
        proj = _dot(hb, w_ref[...])
        c, s1, s2 = c_ref[...], s1_ref[...], s2_ref[...]
        o1, o2, o3, o4, o5 = AQ, AQ + AKV, AQ + 2 * AKV, AQ + 2 * AKV + BW, AQ + 2 * AKV + 2 * BW
        qa_ref[...] = (_rope(proj[:, :o1], c, s1, s2) * QSCALE).astype(bf16)
        ka_ref[...] = _per_query_head(_rope(proj[:, o1:o2], c, s1, s2)).astype(bf16)
        va_ref[...] = _per_query_head(proj[:, o2:o3]).astype(bf16)
        qb = _rope(proj[:, o3:o4], c, s1, s2) * QSCALE
        kb = _rope(proj[:, o4:o5], c, s1, s2)
        vb = proj[:, o5:]
        for val, r1, r4, r16 in ((qb, q1_ref, q4_ref, q16_ref), (kb, k1_ref, k4_ref, k16_ref), (vb, v1_ref, v4_ref, v16_ref)):
            r1[...] = val.astype(bf16)
            _perm_store(val, scr, r4, 4)
            _perm_store(val, scr, r16, 16)

    nat = lambda w: jax.ShapeDtypeStruct((BL, SEQ, w), bf16)
    p4 = jax.ShapeDtypeStruct((BL, 4, SEQ // 4, BW), bf16)
    p16 = jax.ShapeDtypeStruct((BL, 16, SEQ // 16, BW), bf16)
    return pl.pallas_call(
        body, name="attn_in", grid=(BL, NJ),
        in_specs=[_tok(D), MOD_SPEC, _full((1, D)), _full((NCHIP, D, INW // NCHIP)), _tok(LANES), _tok(LANES), _tok(LANES)],
        out_specs=([_tok(D), _tok(AQ), _tok(2 * AKV), _tok(2 * AKV)] + [_tok(BW)] * 3 + [_perm_spec(4, BW)] * 3 + [_perm_spec(16, BW)] * 3
                   + [_full((D, INW))]),
        out_shape=[nat(D), nat(AQ), nat(2 * AKV), nat(2 * AKV)] + [nat(BW)] * 3 + [p4] * 3 + [p16] * 3
                  + [jax.ShapeDtypeStruct((D, INW), bf16)],
        scratch_shapes=[pltpu.VMEM((BW // LANES, TM, LANES), f32)],
        compiler_params=_cp(("arbitrary", "arbitrary")),
    )(x, mod, g_pre, w_in, tc, ts1, ts2)


def _kv_cat(cur_ref, prev_ref, p, cache):
    key = (id(cur_ref), p)
    if key not in cache:
        sl = slice(LANES * p, LANES * (p + 1))
        cache[key] = cur_ref[:, sl] if prev_ref is None else jnp.concatenate([prev_ref[:, sl], cur_ref[:, sl]], axis=0)
    return cache[key]


def _lane_half(a, hh):
    lo = lax.broadcasted_iota(jnp.int32, a.shape, 1) < HD
    return jnp.where(lo, a, jnp.zeros_like(a)) if hh == 0 else jnp.where(lo, jnp.zeros_like(a), a)


ATT_UNITS = 8


def _att_units(nb):
    return ATT_UNITS if nb == 1 else min(ATT_UNITS, nb)


def _attn_specs(n, nb, descending):
    u = _att_units(nb)
    if nb == 1:
        return (lambda ww: pl.BlockSpec((u, BLK, ww), lambda a, i: (a, 0, 0))), None, (n // u, 1)
    steps = nb // u
    at = (lambda i: steps - 1 - i) if descending else (lambda i: i)
    cur = lambda ww: pl.BlockSpec((None, u * BLK, ww), lambda a, i: (a, at(i), 0))
    prev = lambda ww: pl.BlockSpec((None, BLK, ww), lambda a, i: (a, jnp.maximum(u * at(i) - 1, 0), 0))
    return cur, prev, (n, steps)


def _attn_fwd(q, k, v, sink, *, max_dist, o_dtype, name):
    n, l, w = q.shape
    wk = k.shape[-1]
    nb = l // BLK
    has_sink = sink is not None

    def body(*refs):
        sink_ref = None
        if has_sink:
            sink_ref, refs = refs[0], refs[1:]
        if nb > 1:
            q_ref, kc_ref, kp_ref, vc_ref, vp_ref, o_ref, lse_ref = refs[:7]
            first = pl.program_id(1) == 0
            for u in range(_att_units(nb)):
                rows, before = pl.ds(BLK * u, BLK), pl.ds(BLK * (u - 1), BLK)
                unit(q_ref.at[rows, :], kc_ref.at[rows, :], kp_ref if u == 0 else kc_ref.at[before, :],
                     vc_ref.at[rows, :], vp_ref if u == 0 else vc_ref.at[before, :], o_ref.at[rows, :], lse_ref.at[rows, :],
                     jnp.logical_not(first) if u == 0 else True, sink_ref, *refs[7:])
        else:
            q_ref, kc_ref, vc_ref, o_ref, lse_ref = refs[:5]
            for u in range(_att_units(nb)):
                unit(q_ref.at[u], kc_ref.at[u], None, vc_ref.at[u], None, o_ref.at[u], lse_ref.at[u], None, sink_ref, *refs[5:])

    def unit(q_ref, kc_ref, kp_ref, vc_ref, vp_ref, o_ref, lse_ref, has_prev, sink_ref, sscr, pscr, dscr):
        qi = lax.broadcasted_iota(jnp.int32, (BLK, BLK), 0)
        kj = lax.broadcasted_iota(jnp.int32, (BLK, BLK), 1)
        tri = kj <= qi
        eye = kj == qi
        cache = {}
        for p in range(w // LANES):
            qpair = q_ref[:, LANES * p:LANES * (p + 1)]
            kcat = _kv_cat(kc_ref, kp_ref, p // share, cache)
            for hh in range(2):
                s = _dot_nt(_lane_half(qpair, hh), kcat)
                if nb > 1:
                    sp = s[:, :BLK] if has_prev is True else jnp.where(has_prev, s[:, :BLK], NEG)
                    sscr[2 * p + hh] = jnp.where(tri, s[:, BLK:], sp)
                    if diag:
                        dscr[2 * p + hh] = jnp.where(eye, sp, NEG)
                else:
                    sscr[2 * p + hh] = jnp.where(tri, s, NEG)
        lane = lax.broadcasted_iota(jnp.int32, (BLK, LANES), 1)
        lse_all = jnp.zeros((BLK, LANES), f32)
        for p in range(w // LANES):
            for hh in range(2):
                h = 2 * p + hh
                comb = sscr[h]
                if diag:
                    dtile = dscr[h]
                    m = jnp.max(jnp.maximum(comb, dtile), axis=-1, keepdims=True)
                else:
                    m = jnp.max(comb, axis=-1, keepdims=True)
                if has_sink:
                    sk = sink_ref[0, h]
                    m = jnp.maximum(m, sk)
                e = jnp.exp(comb - m)
                if diag:
                    ed = jnp.exp(dtile - m)
                    den = jnp.sum(e + ed, axis=-1, keepdims=True)
                else:
                    den = jnp.sum(e, axis=-1, keepdims=True)
                if has_sink:
                    den = den + jnp.exp(sk - m)
                inv = 1.0 / den
                if nb > 1:
                    pscr[h, :, :BLK] = (jnp.where(tri, ed if diag else 0.0, e) * inv).astype(bf16)
                    pscr[h, :, BLK:] = (jnp.where(tri, e, 0.0) * inv).astype(bf16)
                else:
                    pscr[h] = (e * inv).astype(bf16)
                lse_all = jnp.where(lane == h, jnp.broadcast_to(m + jnp.log(den), (BLK, LANES)), lse_all)
        lse_ref[...] = lse_all
        for p in range(w // LANES):
            vcat = _kv_cat(vc_ref, vp_ref, p // share, cache)
            o_ref[:, LANES * p:LANES * (p + 1)] = (_dot(pscr[2 * p], _lane_half(vcat, 0))
                                                   + _dot(pscr[2 * p + 1], _lane_half(vcat, 1))).astype(o_ref.dtype)

    assert max_dist in (BLK - 1, BLK) and w % wk == 0
    share = w // wk
    diag = nb > 1 and max_dist == BLK
    cur, prev, grid = _attn_specs(n, nb, False)
    in_specs = [cur(w), cur(wk)] + ([prev(wk)] if nb > 1 else []) + [cur(wk)] + ([prev(wk)] if nb > 1 else [])
    args = [q, k] + ([k] if nb > 1 else []) + [v] + ([v] if nb > 1 else [])
    if has_sink:
        in_specs = [pl.BlockSpec(memory_space=pltpu.SMEM)] + in_specs
        args = [sink] + args
    return pl.pallas_call(
        body, name=name, grid=grid, in_specs=in_specs,
        out_specs=[cur(w), cur(LANES)],
        out_shape=[jax.ShapeDtypeStruct((n, l, w), o_dtype), jax.ShapeDtypeStruct((n, l, LANES), f32)],
        scratch_shapes=[pltpu.VMEM((w // HD, BLK, BLK), f32), pltpu.VMEM((w // HD, BLK, 2 * BLK if nb > 1 else BLK), bf16),
                        pltpu.VMEM((w // HD if diag else 1, BLK, BLK), f32)],
        compiler_params=_cp(("arbitrary", "arbitrary")),
    )(*args)


def _attn_bwd(q, k, v, do, delta, lse, sink, *, max_dist, name):
    n, l, w = q.shape
    wk = k.shape[-1]
    nb = l // BLK
    has_sink = sink is not None

    def body(*refs):
        sink_ref = dsink_ref = ck = cv = None
        if has_sink:
            sink_ref, refs = refs[0], refs[1:]
        nin = 8 if nb > 1 else 6
        ins, rest = refs[:nin], refs[nin:]
        if has_sink:
            dq_ref, dk_ref, dv_ref, dsink_ref = rest[:4]
            rest = rest[4:]
        else:
            dq_ref, dk_ref, dv_ref = rest[:3]
            rest = rest[3:]
        step = pl.program_id(1)
        if has_sink:
            @pl.when((pl.program_id(0) == 0) & (step == 0))
            def _():
                dsink_ref[...] = jnp.zeros_like(dsink_ref)

        if nb > 1:
            q_ref, kc_ref, kp_ref, vc_ref, vp_ref, do_ref, delta_ref, lse_ref = ins
            ck, cv = rest[:2]

            @pl.when(step == 0)
            def _():
                ck[...] = jnp.zeros_like(ck)
                cv[...] = jnp.zeros_like(cv)

            last = step == nb // _att_units(nb) - 1
            for u in reversed(range(_att_units(nb))):
                rows, before = pl.ds(BLK * u, BLK), pl.ds(BLK * (u - 1), BLK)
                unit(q_ref.at[rows, :], kc_ref.at[rows, :], kp_ref if u == 0 else kc_ref.at[before, :],
                     vc_ref.at[rows, :], vp_ref if u == 0 else vc_ref.at[before, :], do_ref.at[rows, :],
                     delta_ref.at[rows, :], lse_ref.at[rows, :], dq_ref.at[rows, :], dk_ref.at[rows, :], dv_ref.at[rows, :],
                     jnp.logical_not(last) if u == 0 else True, sink_ref, dsink_ref, ck, cv, *rest[2:])
        else:
            q_ref, kc_ref, vc_ref, do_ref, delta_ref, lse_ref = ins
            for u in range(_att_units(nb)):
                unit(q_ref.at[u], kc_ref.at[u], None, vc_ref.at[u], None, do_ref.at[u], delta_ref.at[u], lse_ref.at[u],
                     dq_ref.at[u], dk_ref.at[u], dv_ref.at[u], None, sink_ref, dsink_ref, None, None, *rest)

    def unit(q_ref, kc_ref, kp_ref, vc_ref, vp_ref, do_ref, delta_ref, lse_ref, dq_ref, dk_ref, dv_ref, has_prev,
             sink_ref, dsink_ref, ck, cv, sscr, dpscr, pscr, dsscr, dscr=None, ddscr=None):
        lane = lax.broadcasted_iota(jnp.int32, (BLK, LANES), 1)
        qi = lax.broadcasted_iota(jnp.int32, (BLK, BLK), 0)
        kj = lax.broadcasted_iota(jnp.int32, (BLK, BLK), 1)
        tri = kj <= qi
        eye = kj == qi
        cache = {}
        kp, vp = kp_ref, vp_ref
        for p in range(w // LANES):
            sl = slice(LANES * p, LANES * (p + 1))
            qpair, dopair = q_ref[:, sl], do_ref[:, sl]
            kcat, vcat = _kv_cat(kc_ref, kp, p // share, cache), _kv_cat(vc_ref, vp, p // share, cache)
            for hh in range(2):
                h = 2 * p + hh
                s = _dot_nt(_lane_half(qpair, hh), kcat)
                dp = _dot_nt(_lane_half(dopair, hh), vcat)
                if nb > 1:
                    sp = s[:, :BLK] if has_prev is True else jnp.where(has_prev, s[:, :BLK], NEG)
                    sscr[h] = jnp.where(tri, s[:, BLK:], sp)
                    dpscr[h] = jnp.where(tri, dp[:, BLK:], dp[:, :BLK])
                    if diag:
                        dscr[h] = jnp.where(eye, sp, NEG)
                        ddscr[h] = dp[:, :BLK]
                else:
                    sscr[h] = jnp.where(tri, s, NEG)
                    dpscr[h] = dp
        for p in range(w // LANES):
            for hh in range(2):
                h = 2 * p + hh
                lse_b = jnp.broadcast_to(lse_ref[:, h:h + 1], (BLK, BLK))
                delta = jnp.broadcast_to(delta_ref[:, h:h + 1], (BLK, BLK))
                pr = jnp.exp(sscr[h] - lse_b)
                ds = pr * (dpscr[h] - delta)
                if nb > 1:
                    if diag:
                        prd = jnp.exp(dscr[h] - lse_b)
                        dsd = prd * (ddscr[h] - delta)
                    else:
                        prd = dsd = 0.0
                    pscr[h, :, :BLK] = jnp.where(tri, prd, pr).astype(bf16)
                    pscr[h, :, BLK:] = jnp.where(tri, pr, 0.0).astype(bf16)
                    dsscr[h, :, :BLK] = jnp.where(tri, dsd, ds).astype(bf16)
                    dsscr[h, :, BLK:] = jnp.where(tri, ds, 0.0).astype(bf16)
                else:
                    pscr[h] = pr.astype(bf16)
                    dsscr[h] = ds.astype(bf16)
                if has_sink:
                    dsk = -jnp.sum(jnp.where(lane == 0, jnp.exp(sink_ref[0, h] - lse_b) * delta, 0.0), keepdims=True)
                    dsink_ref[h:h + 1, :] += jnp.broadcast_to(dsk, (1, LANES))
        for p in range(w // LANES):
            sl = slice(LANES * p, LANES * (p + 1))
            qpair, dopair = q_ref[:, sl], do_ref[:, sl]
            kcat = _kv_cat(kc_ref, kp, p // share, cache)
            dq_ref[:, sl] = _dot(dsscr[2 * p], _lane_half(kcat, 0)) + _dot(dsscr[2 * p + 1], _lane_half(kcat, 1))
            dk_pair = _dot_tn(dsscr[2 * p], _lane_half(qpair, 0)) + _dot_tn(dsscr[2 * p + 1], _lane_half(qpair, 1))
            dv_pair = _dot_tn(pscr[2 * p], _lane_half(dopair, 0)) + _dot_tn(pscr[2 * p + 1], _lane_half(dopair, 1))
            if nb > 1:
                dk_ref[:, sl] = dk_pair[BLK:] + ck[:, sl]
                dv_ref[:, sl] = dv_pair[BLK:] + cv[:, sl]
                ck[:, sl] = dk_pair[:BLK]
                cv[:, sl] = dv_pair[:BLK]
            else:
                dk_ref[:, sl] = dk_pair
                dv_ref[:, sl] = dv_pair

    assert max_dist in (BLK - 1, BLK) and w % wk == 0
    share = w // wk
    diag = nb > 1 and max_dist == BLK
    cur, prev, grid = _attn_specs(n, nb, True)
    in_specs = ([cur(w), cur(wk)] + ([prev(wk)] if nb > 1 else []) + [cur(wk)] + ([prev(wk)] if nb > 1 else [])
                + [cur(w), cur(LANES), cur(LANES)])
    args = [q, k] + ([k] if nb > 1 else []) + [v] + ([v] if nb > 1 else []) + [do, delta, lse]
    out_specs = [cur(w)] * 3
    out_shape = [jax.ShapeDtypeStruct((n, l, w), f32)] * 3
    if has_sink:
        in_specs = [pl.BlockSpec(memory_space=pltpu.SMEM)] + in_specs
        args = [sink] + args
        out_specs.append(pl.BlockSpec((NHEAD, LANES), lambda a, i: (0, 0)))
        out_shape.append(jax.ShapeDtypeStruct((NHEAD, LANES), f32))
    nh = w // HD
    scratch = [pltpu.VMEM((BLK, w), f32), pltpu.VMEM((BLK, w), f32)] if nb > 1 else []
    scratch += [pltpu.VMEM((nh, BLK, BLK), f32)] * 2 + [pltpu.VMEM((nh, BLK, 2 * BLK if nb > 1 else BLK), bf16)] * 2
    if diag:
        scratch += [pltpu.VMEM((nh, BLK, BLK), f32)] * 2
    return pl.pallas_call(
        body, name=name, grid=grid, in_specs=in_specs, out_specs=out_specs, out_shape=out_shape,
        scratch_shapes=scratch, compiler_params=_cp(("arbitrary", "arbitrary")),
    )(*args)


def _split2(x):
    hi = x.astype(bf16)
    return hi, (x - hi.astype(f32)).astype(bf16)


def _heads_to_lanes(xc, e):
    return sum(_dot(t, e) for t in _split2(xc))


def _lanes_to_heads(x, g):
    return sum(_dot(t, g) for t in _split2(x))


HEAD_EXPAND = (np.arange(LANES)[:, None] == np.arange(BW)[None, :] // HD).astype(np.float32)
HEAD_SUM = HEAD_EXPAND.T.copy()


def _branch_weights(l1_ref, l4_ref, l16_ref, scr):
    l4v = _perm_load(l4_ref, scr, 4)
    l16v = _perm_load(l16_ref, scr, 16)
    l1v = l1_ref[...]
    m = jnp.maximum(jnp.maximum(l1v, l4v), l16v)
    e1, e4, e16 = jnp.exp(l1v - m), jnp.exp(l4v - m), jnp.exp(l16v - m)
    z = e1 + e4 + e16
    return e1 / z, e4 / z, e16 / z


def _mix_out(oa, o1, l1, o4, l4, o16, l16, g_mix_a, g_mix_b, w_out, x, mod, g_post):
    def body(oa_ref, o1_ref, l1_ref, o4_ref, l4_ref, o16_ref, l16_ref, ga_ref, gb_ref, w_ref, x_ref, mod_ref, gp_ref, e_ref,
             x1_ref, y_ref, mixed_ref, ob_ref, scr):
        w1, w4, w16 = _branch_weights(l1_ref, l4_ref, l16_ref, scr)
        e = e_ref[...]
        x1w, x4w = _heads_to_lanes(w1, e), _heads_to_lanes(w4, e)
        ob = (x1w * o1_ref[...].astype(f32) + x4w * _perm_load(o4_ref, scr, 4)
              + (1.0 - x1w - x4w) * _perm_load(o16_ref, scr, 16))
        ob_ref[...] = ob
        oan, _ = _rms(oa_ref[...])
        obn, _ = _rms(ob)
        mixed = jnp.concatenate([oan * ga_ref[...], obn * gb_ref[...]], axis=1).astype(bf16)
        mixed_ref[...] = mixed
        y = _dot(mixed, w_ref[...])
        y_ref[...] = y
        yn, _ = _rms(y)
        x1_ref[...] = x_ref[...] + mod_ref[2:3, :] * (yn * gp_ref[...])

    nat = lambda w, dt: jax.ShapeDtypeStruct((BL, SEQ, w), dt)
    return pl.pallas_call(
        body, name="mix_out", grid=(BL, NJ),
        in_specs=[_tok(AQ), _tok(BW), _tok(LANES), _perm_spec(4, BW), _perm_spec(4, LANES), _perm_spec(16, BW),
                  _perm_spec(16, LANES), _full((1, AQ)), _full((1, BW)), _full((D, D)), _tok(D), MOD_SPEC, _full((1, D)),
                  _full((LANES, BW))],
        out_specs=[_tok(D), _tok(D), _tok(D), _tok(BW)],
        out_shape=[nat(D, f32), nat(D, f32), nat(D, bf16), nat(BW, f32)],
        scratch_shapes=[pltpu.VMEM((BW // LANES, TM, LANES), f32)],
        compiler_params=_cp(("arbitrary", "arbitrary")),
    )(oa, o1, l1, o4, l4, o16, l16, g_mix_a, g_mix_b, w_out, x, mod, g_post, jnp.asarray(HEAD_EXPAND, bf16))


def _mlp_up(x1, mod, g_pre, w_up):
    def body(x_ref, mod_ref, g_ref, w_ref, h_ref, u_ref, a_ref):
        xn, _ = _rms(x_ref[...])
        h = (xn * g_ref[...]) * (1.0 + mod_ref[4:5, :]) + mod_ref[3:4, :]
        hb = h.astype(bf16)
        h_ref[...] = hb
        for s in range(NCHIP):
            u = _dot(hb, w_ref[s])
            u_ref[:, D * s:D * (s + 1)] = u.astype(bf16)
            a_ref[:, D * s:D * (s + 1)] = jnp.square(jnp.maximum(u, 0.0)).astype(bf16)

    nat = lambda w: jax.ShapeDtypeStruct((BL, SEQ, w), bf16)
    return pl.pallas_call(
        body, name="mlp_up", grid=(BL, NJ),
        in_specs=[_tok(D), MOD_SPEC, _full((1, D)), _full((NCHIP, D, D))],
        out_specs=[_tok(D), _tok(DFF), _tok(DFF)], out_shape=[nat(D), nat(DFF), nat(DFF)],
        compiler_params=_cp(("arbitrary", "arbitrary")),
    )(x1, mod, g_pre, w_up)


def _mlp_down(a, w_down, x1, target, mod, g_post):
    def body(a_ref, w_ref, x_ref, t_ref, mod_ref, g_ref, gx_ref, dy_ref, accb_ref, accg_ref):
        _acc_init(accb_ref, accg_ref)
        y2 = _dot(a_ref[...], w_ref[...])
        yn, r = _rms(y2)
        g = g_ref[...]
        gt = mod_ref[5:6, :]
        n2 = yn * g
        err = x_ref[...] + gt * n2 - t_ref[...]
        gout = err * (1.0 / D)
        gx_ref[...] = gout
        dn2 = gout * gt
        dy_ref[...] = _rms_bwd(dn2 * g, yn, r).astype(bf16)
        accb_ref[0:1, :] += _colsum(gout * n2)
        accg_ref[0:1, :] += _colsum(dn2 * yn)
        accg_ref[1:2, :] += jnp.broadcast_to(jnp.sum(err * err, keepdims=True), (1, D))

    return pl.pallas_call(
        body, name="mlp_down", grid=(BL, NJ),
        in_specs=[_tok(DFF), _full((DFF, D)), _tok(D), _tok(D), MOD_SPEC, _full((1, D))],
        out_specs=[_tok(D), _tok(D), ACCB_SPEC, ACCG_SPEC],
        out_shape=[jax.ShapeDtypeStruct((BL, SEQ, D), f32), jax.ShapeDtypeStruct((BL, SEQ, D), bf16)] + ACC_SHAPES,
        compiler_params=_cp(("arbitrary", "arbitrary")),
    )(a, w_down, x1, target, mod, g_post)


def _mlp_bwd(dy2, u, w_down, w_up, x1, gx, mod, g_pre):
    def body(dy_ref, u_ref, wd_hbm, wu_hbm, x_ref, gx_ref, mod_ref, g_ref, du_ref, gx1_ref, accb_ref, accg_ref, wd, wu, sem):
        _acc_init(accb_ref, accg_ref)
        first = (pl.program_id(0) == 0) & (pl.program_id(1) == 0)
        c1 = pltpu.make_async_copy(wd_hbm, wd, sem.at[0])
        c2 = pltpu.make_async_copy(wu_hbm, wu, sem.at[1])

        @pl.when(first)
        def _():
            c1.start()
            c2.start()
            c1.wait()

        dy = dy_ref[...]
        for s in range(NCHIP):
            sl = slice(D * s, D * (s + 1))
            da = _dot_nt(dy, wd[sl, :])
            du_ref[:, sl] = (da * (2.0 * jnp.maximum(u_ref[:, sl].astype(f32), 0.0))).astype(bf16)

        @pl.when(first)
        def _():
            c2.wait()

        dh = jnp.zeros((TM, D), f32)
        for s in range(NCHIP):
            dh = dh + _dot_nt(du_ref[:, D * s:D * (s + 1)], wu[s])
        xn, r = _rms(x_ref[...])
        g = g_ref[...]
        n = xn * g
        dn = dh * (1.0 + mod_ref[4:5, :])
        gx1_ref[...] = gx_ref[...] + _rms_bwd(dn * g, xn, r)
        accb_ref[0:1, :] += _colsum(dh * n)
        accb_ref[1:2, :] += _colsum(dh)
        accg_ref[0:1, :] += _colsum(dn * xn)

    anyspec = pl.BlockSpec(memory_space=pl.ANY)
    return pl.pallas_call(
        body, name="mlp_bwd", grid=(BL, NJ),
        in_specs=[_tok(D), _tok(DFF), anyspec, anyspec, _tok(D), _tok(D), MOD_SPEC, _full((1, D))],
        out_specs=[_tok(DFF), _tok(D), ACCB_SPEC, ACCG_SPEC],
        out_shape=[jax.ShapeDtypeStruct((BL, SEQ, DFF), bf16), jax.ShapeDtypeStruct((BL, SEQ, D), f32)] + ACC_SHAPES,
        scratch_shapes=[pltpu.VMEM((DFF, D), bf16), pltpu.VMEM((NCHIP, D, D), bf16), pltpu.SemaphoreType.DMA((2,))],
        compiler_params=_cp(("arbitrary", "arbitrary")),
    )(dy2, u, w_down, w_up, x1, gx, mod, g_pre)


def _matmul_tn(a, b, *, tn, col_blocked, name, out_dtype=f32):
    t, m = a.shape
    n = b.shape[1]
    tmm = min(m, 1024)
    tk = 2048 if tn <= 1024 else 1024
    nk = t // tk

    def body(a_ref, b_ref, o_ref, acc):
        k = pl.program_id(2)

        @pl.when(k == 0)
        def _():
            acc[...] = jnp.zeros_like(acc)

        acc[...] += _dot_tn(a_ref[...], b_ref[...])

        @pl.when(k == nk - 1)
        def _():
            o_ref[...] = acc[...].astype(out_dtype)

    if col_blocked:
        out_spec = pl.BlockSpec((None, tmm, tn), lambda i, j, k: (j, i, 0))
        out_shape = jax.ShapeDtypeStruct((n // tn, m, tn), out_dtype)
    else:
        out_spec = pl.BlockSpec((tmm, tn), lambda i, j, k: (i, j))
        out_shape = jax.ShapeDtypeStruct((m, n), out_dtype)
    return pl.pallas_call(
        body, name=name, grid=(m // tmm, n // tn, nk),
        in_specs=[pl.BlockSpec((tk, tmm), lambda i, j, k: (k, i)), pl.BlockSpec((tk, tn), lambda i, j, k: (k, j))],
        out_specs=out_spec, out_shape=out_shape, scratch_shapes=[pltpu.VMEM((tmm, tn), f32)],
        compiler_params=_cp(("arbitrary", "arbitrary", "arbitrary")),
    )(a, b)


def _grad_w_in(h, dproj):
    t = h.shape[0]
    tk = 1024
    nk = t // tk
    sw = INW // NCHIP

    def body(a_ref, b_ref, o_ref, acc):
        k = pl.program_id(0)

        @pl.when(k == 0)
        def _():
            acc[...] = jnp.zeros_like(acc)

        acc[...] += _dot_tn(a_ref[...], b_ref[...])

        @pl.when(k == nk - 1)
        def _():
            for s in range(NCHIP):
                o_ref[s] = acc[:, sw * s:sw * (s + 1)]

    return pl.pallas_call(
        body, name="grad_w_in", grid=(nk,),
        in_specs=[pl.BlockSpec((tk, D), lambda k: (k, 0)), pl.BlockSpec((tk, INW), lambda k: (k, 0))],
        out_specs=pl.BlockSpec((NCHIP, D, sw), lambda k: (0, 0, 0)), out_shape=jax.ShapeDtypeStruct((NCHIP, D, sw), f32),
        scratch_shapes=[pltpu.VMEM((D, INW), f32)], compiler_params=_cp(("arbitrary",)),
    )(h, dproj)


def _attn_out_bwd(gx1, y, mod, g_post, w_out, oa, ob, g_mix_a, g_mix_b, l1, l4, l16):
    def body(gx_ref, y_ref, mod_ref, gp_ref, w_ref, oa_ref, ob_ref, ga_ref, gb_ref, l1_ref, l4_ref, l16_ref, e_ref, g_ref,
             dy_ref, doa_ref, do1_ref, do4_ref, do16_ref, da_ref, d1_ref, d4_ref, d16_ref, accb_ref, accg_ref, scr):
        _acc_init(accb_ref, accg_ref)
        w1, w4, w16 = _branch_weights(l1_ref, l4_ref, l16_ref, scr)
        e, hs = e_ref[...], g_ref[...]
        gx1v = gx_ref[...]
        yn, ry = _rms(y_ref[...])
        gp = gp_ref[...]
        gt = mod_ref[2:3, :]
        dn1 = gx1v * gt
        dy = _rms_bwd(dn1 * gp, yn, ry).astype(bf16)
        dy_ref[...] = dy
        dmixed = _dot_nt(dy, w_ref[...])
        dma, dmb = dmixed[:, :AQ], dmixed[:, AQ:]
        oa, ob = oa_ref[...], ob_ref[...]
        oan, ra = _rms(oa)
        obn, rb = _rms(ob)
        doa = _rms_bwd(dma * ga_ref[...], oan, ra)
        doa_ref[...] = doa.astype(bf16)
        da_ref[...] = _lanes_to_heads(doa * oa, hs)
        dob = _rms_bwd(dmb * gb_ref[...], obn, rb)
        dd = _lanes_to_heads(dob * ob, hs)
        x1w, x4w = _heads_to_lanes(w1, e), _heads_to_lanes(w4, e)
        do1_ref[...] = (x1w * dob).astype(bf16)
        d1_ref[...] = w1 * dd
        _perm_store(x4w * dob, scr, do4_ref, 4)
        _perm_store(w4 * dd, scr, d4_ref, 4)
        _perm_store((1.0 - x1w - x4w) * dob, scr, do16_ref, 16)
        _perm_store(w16 * dd, scr, d16_ref, 16)
        accb_ref[0:1, :] += _colsum(gx1v * (yn * gp))
        accg_ref[0:1, :] += _colsum(dn1 * yn)
        accg_ref[1:2, :] += jnp.concatenate([_colsum(dma * oan), _colsum(dmb * obn)], axis=1)

    nat = lambda w, dt: jax.ShapeDtypeStruct((BL, SEQ, w), dt)
    return pl.pallas_call(
        body, name="attn_out_bwd", grid=(BL, NJ),
        in_specs=[_tok(D), _tok(D), MOD_SPEC, _full((1, D)), _full((D, D)), _tok(AQ), _tok(BW), _full((1, AQ)), _full((1, BW)),
                  _tok(LANES), _perm_spec(4, LANES), _perm_spec(16, LANES), _full((LANES, BW)), _full((BW, LANES))],
        out_specs=[_tok(D), _tok(AQ), _tok(BW), _perm_spec(4, BW), _perm_spec(16, BW),
                   _tok(LANES), _tok(LANES), _perm_spec(4, LANES), _perm_spec(16, LANES), ACCB_SPEC, ACCG_SPEC],
        out_shape=[nat(D, bf16), nat(AQ, bf16), nat(BW, bf16), jax.ShapeDtypeStruct((BL, 4, SEQ // 4, BW), bf16),
                   jax.ShapeDtypeStruct((BL, 16, SEQ // 16, BW), bf16), nat(LANES, f32), nat(LANES, f32),
                   jax.ShapeDtypeStruct((BL, 4, SEQ // 4, LANES), f32), jax.ShapeDtypeStruct((BL, 16, SEQ // 16, LANES), f32)]
                  + ACC_SHAPES,
        scratch_shapes=[pltpu.VMEM((BW // LANES, TM, LANES), f32)],
        compiler_params=_cp(("arbitrary", "arbitrary")),
    )(gx1, y, mod, g_post, w_out, oa, ob, g_mix_a, g_mix_b, l1, l4, l16, jnp.asarray(HEAD_EXPAND, bf16),
      jnp.asarray(HEAD_SUM, bf16))


def _attn_in_bwd(dqa, dka, dva, d1, d4, d16, tc, ts1, ts2, w_in, x, gx1, mod, g_pre):
    def body(dqa_ref, dka_ref, dva_ref, dq1_ref, dk1_ref, dv1_ref, dq4_ref, dk4_ref, dv4_ref, dq16_ref, dk16_ref, dv16_ref,
             c_ref, s1_ref, s2_ref, w_ref, x_ref, gx_ref, mod_ref, g_ref, dproj_ref, dx_ref, accb_ref, accg_ref, scr):
        _acc_init(accb_ref, accg_ref)
        c, s1, s2 = c_ref[...], s1_ref[...], s2_ref[...]
        tot = lambda r1, r4, r16: r1[...] + _perm_load(r4, scr, 4) + _perm_load(r16, scr, 16)
        dqb = tot(dq1_ref, dq4_ref, dq16_ref)
        dkb = tot(dk1_ref, dk4_ref, dk16_ref)
        dvb = tot(dv1_ref, dv4_ref, dv16_ref)
        dproj = jnp.concatenate([
            _rope_t(dqa_ref[...], c, s1, s2) * QSCALE, _rope_t(_per_kv_head(dka_ref[...]), c, s1, s2),
            _per_kv_head(dva_ref[...]),
            _rope_t(dqb, c, s1, s2) * QSCALE, _rope_t(dkb, c, s1, s2), dvb], axis=1).astype(bf16)
        dproj_ref[...] = dproj
        dh = _dot_nt(dproj, w_ref[...])
        xn, r = _rms(x_ref[...])
        g = g_ref[...]
        dn = dh * (1.0 + mod_ref[1:2, :])
        dx_ref[...] = gx_ref[...] + _rms_bwd(dn * g, xn, r)
        accb_ref[0:1, :] += _colsum(dh * (xn * g))
        accb_ref[1:2, :] += _colsum(dh)
        accg_ref[0:1, :] += _colsum(dn * xn)

    return pl.pallas_call(
        body, name="attn_in_bwd", grid=(BL, NJ),
        in_specs=[_tok(AQ), _tok(AQ), _tok(AQ)] + [_tok(BW)] * 3 + [_perm_spec(4, BW)] * 3 + [_perm_spec(16, BW)] * 3
                 + [_tok(LANES)] * 3 + [_full((D, INW)), _tok(D), _tok(D), MOD_SPEC, _full((1, D))],
        out_specs=[_tok(INW), _tok(D), ACCB_SPEC, ACCG_SPEC],
        out_shape=[jax.ShapeDtypeStruct((BL, SEQ, INW), bf16), jax.ShapeDtypeStruct((BL, SEQ, D), f32)] + ACC_SHAPES,
        scratch_shapes=[pltpu.VMEM((BW // LANES, TM, LANES), f32)],
        compiler_params=_cp(("arbitrary", "arbitrary")),
    )(dqa, dka, dva, *d1, *d4, *d16, tc, ts1, ts2, w_in, x, gx1, mod, g_pre)


def _inv_lane():
    inv = np.float32(THETA) ** (-np.arange(0, ROT, 2, dtype=np.float32) / np.float32(ROT))
    lane = np.arange(LANES) % HD
    return jnp.asarray(np.where(lane < ROT, inv[lane % (ROT // 2)], 0.0).astype(np.float32)[None, :])


def _local_step(x, positions, mod, target, inv_lane, first_weight, later_weights, grad_ready, g_attn_pre,
                g_attn_post, sink_a, g_mix_a, g_mix_b, g_mlp_pre, g_mlp_post):
    tabs = _rope_tables(positions.reshape(BL * SEQ, 1), inv_lane)
    w_in = first_weight(tuple(tabs))
    tc, ts1, ts2 = [t.reshape(BL, SEQ, LANES) for t in tabs]

    (h, qa, ka, va, q1, k1, v1, q4, k4, v4, q16, k16, v16, w_in) = _attn_in(x, mod, g_attn_pre, w_in, tc, ts1, ts2)
    seqs = lambda t: t.reshape(t.shape[0] * t.shape[1], t.shape[2], t.shape[3])
    q4, k4, v4, q16, k16, v16 = [seqs(t) for t in (q4, k4, v4, q16, k16, v16)]
    oa, la = _attn_fwd(qa, ka, va, sink_a, max_dist=BLK - 1, o_dtype=f32, name="attn_a_fwd")
    o1, l1 = _attn_fwd(q1, k1, v1, None, max_dist=BLK, o_dtype=bf16, name="attn_b1_fwd")
    o4, l4 = _attn_fwd(q4, k4, v4, None, max_dist=BLK, o_dtype=bf16, name="attn_b4_fwd")
    o16, l16 = _attn_fwd(q16, k16, v16, None, max_dist=BLK, o_dtype=bf16, name="attn_b16_fwd")
    b4 = lambda t: t.reshape(BL, 4, SEQ // 4, t.shape[-1])
    b16 = lambda t: t.reshape(BL, 16, SEQ // 16, t.shape[-1])
    w_out, mlp_weights, mod = later_weights((oa, o1, o4, o16), mod)
    x1, y, mixed, ob = _mix_out(oa, o1, l1, b4(o4), b4(l4), b16(o16), b16(l16), g_mix_a, g_mix_b, w_out, x, mod, g_attn_post)
    w_up, w_down = mlp_weights((x1,))
    h2, u, a = _mlp_up(x1, mod, g_mlp_pre, w_up)
    gx, dy2, accb_d, accg_d = _mlp_down(a, w_down, x1, target, mod, g_mlp_post)

    flat = lambda t: t.reshape(BL * SEQ, t.shape[-1])
    mod = grad_ready("w_down", _matmul_tn(flat(a), flat(dy2), tn=D, col_blocked=False, name="grad_w_down", out_dtype=bf16), mod)
    du, gx1, accb_m, accg_m = _mlp_bwd(dy2, u, w_down, w_up, x1, gx, mod, g_mlp_pre)
    mod = grad_ready("w_up", _matmul_tn(flat(h2), flat(du), tn=D, col_blocked=True, name="grad_w_up", out_dtype=bf16), mod)

    dy, doa, do1, do4, do16, da, dl1, dl4, dl16, accb_o, accg_o = _attn_out_bwd(
        gx1, y, mod, g_attn_post, w_out, oa, ob, g_mix_a, g_mix_b, l1, b4(l4), b16(l16))
    gw_out = _matmul_tn(flat(mixed), flat(dy), tn=D, col_blocked=False, name="grad_w_out")
    dqa, dka, dva, dsink = _attn_bwd(qa, ka, va, doa, da, la, sink_a, max_dist=BLK - 1, name="attn_a_bwd")
    d1 = _attn_bwd(q1, k1, v1, do1, dl1, l1, None, max_dist=BLK, name="attn_b1_bwd")
    d4 = _attn_bwd(q4, k4, v4, seqs(do4), seqs(dl4), l4, None, max_dist=BLK, name="attn_b4_bwd")
    d16 = _attn_bwd(q16, k16, v16, seqs(do16), seqs(dl16), l16, None, max_dist=BLK, name="attn_b16_bwd")
    dproj, grad_x, accb_i, accg_i = _attn_in_bwd(dqa, dka, dva, d1, [b4(t) for t in d4], [b16(t) for t in d16],
                                                 tc, ts1, ts2, w_in, x, gx1, mod, g_attn_pre)
    gw_in = _grad_w_in(flat(h), flat(dproj))
    dsink = grad_ready("w_in_w_out", (gw_in, gw_out), dsink)

    return grad_x, (accb_i, accb_o, accb_m, accb_d, accg_i, accg_o, accg_m, accg_d, dsink)


ADAW = NMOD * D // NCHIP


def _pos():
    return lax.axis_index("x"), lax.axis_index("y"), lax.axis_index("c")


def _flip(v, bit):
    return 1 - v if bit else v


def _all_peers(x, y, c):
    return [(_flip(x, k >> 2 & 1), _flip(y, k >> 1 & 1), _flip(c, k & 1)) for k in range(1, NDEV)]


def _other_chips(x, y):
    return [(1 - x, y), (x, 1 - y), (1 - x, 1 - y)]


def _rcopy(src, dst, send, recv, k, dev, k_recv=None):
    return pltpu.make_async_remote_copy(src_ref=src, dst_ref=dst, send_sem=send.at[k],
                                        recv_sem=recv.at[k if k_recv is None else k_recv],
                                        device_id=dev, device_id_type=MESH)


def _gather_small(src, buf, send, recv):
    x, y, c = _pos()
    me = 4 * x + 2 * y + c
    peers = _all_peers(x, y, c)
    sends = [_rcopy(src, buf.at[me], send, recv, k, p) for k, p in enumerate(peers)]
    for cp in sends:
        cp.start()
    for k, (px, py, pc) in enumerate(peers):
        _rcopy(src, buf.at[4 * px + 2 * py + pc], send, recv, k, (px, py, pc)).wait_recv()
    for cp in sends:
        cp.wait_send()
    return me


def _ada_fwd(c_in, w_ada, b_cols):
    def body(c_ref, w_hbm, b_ref, mod_ref, cond_ref, cbuf, mbuf, w_ref, s1, r1, s2, r2, wsem):
        x, y, c = _pos()
        chip = 2 * x + y
        wcopy = pltpu.make_async_copy(w_hbm, w_ref, wsem)
        wcopy.start()
        me = _gather_small(c_ref, cbuf, s1, r1)
        cbuf[me] = c_ref[...]
        for i in range(NDEV):
            cond_ref[BL * i:BL * (i + 1), :] = cbuf[i]
        call = cond_ref[...]
        cond = call / (1.0 + jnp.exp(-call))
        cond_ref[...] = cond
        wcopy.wait()
        mbuf[chip] = _dot(cond.astype(bf16), w_ref[...].astype(bf16)) + b_ref[...]
        chips = _other_chips(x, y)
        sends = [_rcopy(mbuf.at[chip], mbuf.at[chip], s2, r2, j, (px, py, c)) for j, (px, py) in enumerate(chips)]
        for cp in sends:
            cp.start()
        for j, (px, py) in enumerate(chips):
            _rcopy(mbuf.at[chip], mbuf.at[2 * px + py], s2, r2, j, (px, py, c)).wait_recv()
        for cp in sends:
            cp.wait_send()
        row = lax.broadcasted_iota(jnp.int32, (BL * NDEV, ADAW), 0)
        for s in range(NCHIP):
            slab = mbuf[s]
            for j in range(BL):
                mod_ref[j:j + 1, ADAW * s:ADAW * (s + 1)] = jnp.sum(jnp.where(row == BL * me + j, slab, 0.0), axis=0, keepdims=True)

    vm = pl.BlockSpec(memory_space=pltpu.VMEM)
    return pl.pallas_call(
        body, name="ada_fwd", in_specs=[vm, pl.BlockSpec(memory_space=pl.ANY), vm], out_specs=[vm, vm],
        out_shape=[jax.ShapeDtypeStruct((BL, NMOD * D), f32), jax.ShapeDtypeStruct((BL * NDEV, D), f32)],
        scratch_shapes=[pltpu.VMEM((NDEV, BL, D), f32), pltpu.VMEM((NCHIP, BL * NDEV, ADAW), f32),
                        pltpu.VMEM((D, ADAW), f32),
                        pltpu.SemaphoreType.DMA((NDEV - 1,)), pltpu.SemaphoreType.DMA((NDEV - 1,)),
                        pltpu.SemaphoreType.DMA((NCHIP - 1,)), pltpu.SemaphoreType.DMA((NCHIP - 1,)),
                        pltpu.SemaphoreType.DMA],
        compiler_params=pltpu.CompilerParams(vmem_limit_bytes=VMEM_LIMIT),
    )(c_in, w_ada, b_cols)


def _small_allreduce(accs, cond_all):
    def body(bi, bo, bm, bd, gi, go, gm, gd, dsink, cond_ref, gw_ref, gb_ref, small_ref, pay, pbuf, dall, s1, r1):
        x, y, c = _pos()
        chip = 2 * x + y
        pay[...] = jnp.zeros_like(pay)
        for b in range(BL):
            for k, (ref, r) in enumerate(((bi, 1), (bi, 0), (bo, 0), (bm, 1), (bm, 0), (bd, 0))):
                pay[b:b + 1, D * k:D * (k + 1)] = ref[b, r:r + 1, :]
        for off, ref, r in ((OFF_G_ATTN_PRE, gi, 0), (OFF_G_ATTN_POST, go, 0), (OFF_G_MIX_A, go, 1), (OFF_G_MLP_PRE, gm, 0),
                            (OFF_G_MLP_POST, gd, 0)):
            pay[BL:BL + 1, off:off + D] = ref[r:r + 1, :]
        eye = lax.broadcasted_iota(jnp.int32, (NHEAD, LANES), 0) == lax.broadcasted_iota(jnp.int32, (NHEAD, LANES), 1)
        pay[BL:BL + 1, OFF_SINK:OFF_SINK + LANES] = jnp.sum(jnp.where(eye, dsink[...], 0.0), axis=0, keepdims=True)
        pay[BL:BL + 1, OFF_LOSS:OFF_LOSS + LANES] = gd[1:2, 0:LANES]
        me = _gather_small(pay, pbuf, s1, r1)
        pbuf[me] = pay[...]
        small = pbuf[0, BL:BL + 1, :]
        for i in range(1, NDEV):
            small = small + pbuf[i, BL:BL + 1, :]
        small_ref[...] = small
        for i in range(NDEV):
            dall[BL * i:BL * (i + 1), :] = pbuf[i, 0:BL, :]
        gb_ref[...] = jnp.sum(dall[...], axis=0, keepdims=True)
        cols = jnp.zeros((BL * NDEV, ADAW), f32)
        for s in range(NCHIP):
            cols = cols + jnp.where(chip == s, dall[:, ADAW * s:ADAW * (s + 1)], 0.0)
        gw_ref[...] = _dot_tn(cond_ref[...].astype(bf16), cols.astype(bf16))

    vm = pl.BlockSpec(memory_space=pltpu.VMEM)
    return pl.pallas_call(
        body, name="small_allreduce", in_specs=[vm] * 10, out_specs=[vm] * 3,
        out_shape=[jax.ShapeDtypeStruct((D, ADAW), f32), jax.ShapeDtypeStruct((1, PAYW), f32), jax.ShapeDtypeStruct((1, PAYW), f32)],
        scratch_shapes=[pltpu.VMEM((4, PAYW), f32), pltpu.VMEM((NDEV, 4, PAYW), f32), pltpu.VMEM((BL * NDEV, PAYW), f32),
                        pltpu.SemaphoreType.DMA((NDEV - 1,)), pltpu.SemaphoreType.DMA((NDEV - 1,))],
        compiler_params=pltpu.CompilerParams(vmem_limit_bytes=VMEM_LIMIT),
    )(*accs, cond_all)


def _half(ref, c):
    r2 = ref.shape[0] // 2
    return ref.at[pl.ds(c * r2 if isinstance(c, int) else pl.multiple_of(c * r2, 16), r2), :]


HBM_SPEC = pl.BlockSpec(memory_space=pltpu.HBM)
SEM_SPEC = pl.BlockSpec(memory_space=pltpu.SEMAPHORE)
EFFECT = pltpu.SideEffectType.DATAFLOW_SIDE_EFFECTING
NLINK = NCHIP - 1


def _in_hbm(a):
    return pltpu.with_memory_space_constraint(a, pltpu.HBM)


NSEM = 8


def _split_start(name, srcs, land_shapes, builds, carry, after=(), lands=None):
    n = len(srcs)
    na, nc = len(after), len(carry)

    def body(*refs):
        src, land = refs[:n], refs[n:2 * n]
        kept = refs[2 * n + na:2 * n + na + nc]
        outs = refs[2 * n + na + nc:]
        send, recv, passed = outs[:n], outs[n:2 * n], outs[4 * n:]
        for t in range(n):
            for out_cp, _ in builds[t](src[t], land[t], send[t], recv[t]):
                out_cp.start()
        for a, b in zip(kept, passed):
            b[...] = a[...]

    if lands is None:
        lands = [lax.empty(s.shape, s.dtype) for s in land_shapes]
    lands = [_in_hbm(a) for a in lands]
    sems = [pltpu.SemaphoreType.DMA((NSEM,))] * (2 * n)
    thru = [pltpu.HBM(a.shape, a.dtype) for a in list(srcs) + lands]
    vm = pl.BlockSpec(memory_space=pltpu.VMEM)
    res = pl.pallas_call(
        body, name=name, out_shape=sems + thru + [jax.ShapeDtypeStruct(a.shape, a.dtype) for a in carry],
        in_specs=[HBM_SPEC] * (2 * n) + [pl.BlockSpec(memory_space=pl.ANY)] * na + [vm] * nc,
        out_specs=[SEM_SPEC] * (2 * n) + [HBM_SPEC] * (2 * n) + [vm] * nc,
        input_output_aliases={i: 2 * n + i for i in range(2 * n)},
        compiler_params=pltpu.CompilerParams(has_side_effects=EFFECT),
    )(*[_in_hbm(a) for a in srcs], *lands, *after, *carry)
    flight = [(res[2 * n + t], res[3 * n + t], res[t], res[n + t]) for t in range(n)]
    return flight, list(res[4 * n:])


def _split_wait(name, flight, builds, after):
    m = len(flight)
    na = len(after)

    def body(*refs):
        src, land, send, recv = refs[:m], refs[m:2 * m], refs[2 * m:3 * m], refs[3 * m:4 * m]
        for t in range(m):
            for out_cp, in_cp in builds[t](src[t], land[t], send[t], recv[t]):
                out_cp.wait_send()
                in_cp.wait_recv()

    ops = [f[0] for f in flight] + [f[1] for f in flight] + [f[2] for f in flight] + [f[3] for f in flight]
    res = pl.pallas_call(
        body, name=name, out_shape=[pltpu.HBM(a.shape, a.dtype) for a in ops[:2 * m]],
        in_specs=[HBM_SPEC] * (2 * m) + [SEM_SPEC] * (2 * m) + [pl.BlockSpec(memory_space=pl.ANY)] * na,
        out_specs=[HBM_SPEC] * (2 * m), input_output_aliases={i: i for i in range(2 * m)},
        compiler_params=pltpu.CompilerParams(has_side_effects=EFFECT),
    )(*ops, *after)
    return res[:m], res[m:2 * m]


def _weight_copies(src, land, send, recv):
    x, y, c = _pos()
    chip = 2 * x + y
    return [(_rcopy(_half(src, c), _half(land.at[chip], c), send, recv, j, (px, py, c)),
             _rcopy(_half(src, c), _half(land.at[2 * px + py], c), send, recv, j, (px, py, c)))
            for j, (px, py) in enumerate(_other_chips(x, y))]


def _grad_copies(src, land, send, recv):
    x, y, c = _pos()
    return [(_rcopy(src.at[2 * px + py], land.at[j], send, recv, j, (px, py, c)),
             _rcopy(src.at[2 * px + py], land.at[j], send, recv, j, (px, py, c)))
            for j, (px, py) in enumerate(_other_chips(x, y))]


NDIRECT = NDEV - 1


def _direct_grad_copies(src, land, send, recv):
    x, y, c = _pos()
    out, arrive = [], []
    for j, (px, py) in enumerate(_other_chips(x, y)):
        for hc in range(2):
            out.append(_rcopy(_half(src.at[2 * px + py], hc), land.at[2 * j + c], send, recv, 2 * j + hc, (px, py, hc),
                              k_recv=2 * j + c))
            arrive.append(_rcopy(_half(src.at[2 * px + py], hc), land.at[2 * j + hc], send, recv, 2 * j + hc, (px, py, hc)))
    own = _rcopy(_half(src.at[2 * x + y], 1 - c), land.at[NDIRECT - 1], send, recv, NDIRECT - 1, (x, y, 1 - c))
    return list(zip(out, arrive)) + [(own, own)]


def _pair_grad_copies(src, land, send, recv):
    x, y, c = _pos()
    r2 = src.shape[1] // 2
    cp = _rcopy(src.at[:, pl.ds(pl.multiple_of((1 - c) * r2, 8), r2), :], land, send, recv, 0, (x, y, 1 - c))
    return [(cp, cp)]


def _pair_weight_copies(src, land, send, recv):
    x, y, c = _pos()
    sib = (x, y, 1 - c)
    cps = []
    for j, (px, py) in enumerate(_other_chips(x, y)):
        mine, theirs = _half(land.at[2 * px + py], c), _half(land.at[2 * px + py], 1 - c)
        cps.append((_rcopy(mine, mine, send, recv, j, sib), _rcopy(theirs, theirs, send, recv, j, sib)))
    own = _rcopy(src, land.at[2 * x + y], send, recv, NLINK, sib)
    return cps + [(own, own)]


RS_ROWS = 128


def _pair_add(g, landed, c_arr, name):
    _, r2, cw = landed.shape
    nr = r2 // RS_ROWS

    def body(c_ref, g_ref, p_ref, o_ref):
        o_ref[...] = (g_ref[...] + p_ref[...]).astype(bf16)

    gs = pltpu.PrefetchScalarGridSpec(
        num_scalar_prefetch=1, grid=(NCHIP, nr),
        in_specs=[pl.BlockSpec((None, RS_ROWS, cw), lambda s, j, c: (s, c[0] * nr + j, 0)),
                  pl.BlockSpec((None, RS_ROWS, cw), lambda s, j, c: (s, j, 0))],
        out_specs=pl.BlockSpec((None, RS_ROWS, cw), lambda s, j, c: (s, j, 0)))
    return pl.pallas_call(body, name=name, grid_spec=gs, out_shape=jax.ShapeDtypeStruct((NCHIP, r2, cw), bf16),
                          compiler_params=_cp(("arbitrary", "arbitrary")))(c_arr, g, landed)


def _chip_add(own, landed, pos_arr, name):
    nl, r2, cw = landed.shape
    nr = r2 // RS_ROWS
    whole = own.shape[1] == 2 * r2

    def body(s_ref, h_ref, q_ref, o_ref):
        acc = h_ref[...].astype(f32)
        for j in range(nl):
            acc = acc + q_ref[j].astype(f32)
        o_ref[...] = acc

    gs = pltpu.PrefetchScalarGridSpec(
        num_scalar_prefetch=1, grid=(nr,),
        in_specs=[pl.BlockSpec((None, RS_ROWS, cw), lambda j, s: (s[0], (s[1] * nr if whole else 0) + j, 0)),
                  pl.BlockSpec((nl, RS_ROWS, cw), lambda j, s: (0, j, 0))],
        out_specs=pl.BlockSpec((RS_ROWS, cw), lambda j, s: (s[1] * nr + j, 0)))
    return pl.pallas_call(body, name=name, grid_spec=gs, out_shape=jax.ShapeDtypeStruct((2 * r2, cw), f32),
                          compiler_params=_cp(("arbitrary",)))(pos_arr, own, landed)


def _pair_gather_copies(src, land, send, recv):
    x, y, c = _pos()
    sib = (x, y, 1 - c)
    return [(_rcopy(_half(land, c), _half(land, c), send, recv, 0, sib),
             _rcopy(_half(land, 1 - c), _half(land, 1 - c), send, recv, 0, sib))]


def _adamw_math(w, g, m, v):
    m = B1 * m + (1.0 - B1) * g
    v = B2 * v + (1.0 - B2) * jnp.square(g)
    m_hat = m / (1.0 - B1 ** STEP)
    v_hat = v / (1.0 - B2 ** STEP)
    return -LR * (m_hat / (jnp.sqrt(v_hat) + AEPS) + WD * w), m, v


ADAM_ROWS = 256


def _adamw(w, g, m, v, name):
    r, cw = w.shape

    def body(w_ref, g_ref, m_ref, v_ref, go_ref, d_ref, mo_ref, vo_ref):
        g = g_ref[...]
        go_ref[...] = g
        d_ref[...], mo_ref[...], vo_ref[...] = _adamw_math(w_ref[...], g, m_ref[...], v_ref[...])

    rows = max(k for k in range(8, ADAM_ROWS + 1, 8) if r % k == 0)
    spec = pl.BlockSpec((rows, cw), lambda i: (i, 0))
    return pl.pallas_call(body, name=name, grid=(r // rows,), in_specs=[spec] * 4, out_specs=[spec] * 4,
                          out_shape=[jax.ShapeDtypeStruct((r, cw), f32)] * 4, compiler_params=_cp(("arbitrary",)))(w, g, m, v)


SMALL = (("b_ada", None, PAYW), ("g_attn_pre", OFF_G_ATTN_PRE, D), ("g_attn_post", OFF_G_ATTN_POST, D), ("sink_a", OFF_SINK, 8),
         ("g_mix_a", OFF_G_MIX_A, AQ), ("g_mix_b", OFF_G_MIX_B, BW), ("g_mlp_pre", OFF_G_MLP_PRE, D), ("g_mlp_post", OFF_G_MLP_POST, D))


def _adamw_small(small, gb, params):
    n = len(SMALL)

    def body(*refs):
        small_ref, gb_ref = refs[:2]
        wmv = refs[2:2 + 3 * n]
        loss_ref = refs[2 + 3 * n]
        outs = refs[3 + 3 * n:]
        loss_ref[...] = small_ref[:, OFF_LOSS:OFF_LOSS + 1] * (0.5 / D)
        for i, (_, off, width) in enumerate(SMALL):
            g = gb_ref[...] if off is None else small_ref[:, off:off + width]
            w_ref, m_ref, v_ref = wmv[3 * i:3 * i + 3]
            outs[4 * i][...] = g
            outs[4 * i + 1][...], outs[4 * i + 2][...], outs[4 * i + 3][...] = _adamw_math(w_ref[...], g, m_ref[...], v_ref[...])

    vm = pl.BlockSpec(memory_space=pltpu.VMEM)
    out_shape = [jax.ShapeDtypeStruct((1, 1), f32)]
    for _, _, width in SMALL:
        out_shape += [jax.ShapeDtypeStruct((1, width), f32)] * 4
    flat = [a for wmv in params for a in wmv]
    res = pl.pallas_call(body, name="adamw_small", in_specs=[vm] * (2 + 3 * n), out_specs=[vm] * len(out_shape),
                         out_shape=out_shape)(small, gb, *flat)
    return res[0], {name: res[1 + 4 * i:5 + 4 * i] for i, (name, _, _) in enumerate(SMALL)}


def kernel(x, c, positions, w_ada, b_ada, g_attn_pre, g_attn_post, w_in, sink_a, g_mix_a, g_mix_b, w_out, g_mlp_pre, g_mlp_post, w_up, w_down, loss_target, m_w_ada, m_b_ada, m_g_attn_pre, m_g_attn_post, m_w_in, m_sink_a, m_g_mix_a, m_g_mix_b, m_w_out, m_g_mlp_pre, m_g_mlp_post, m_w_up, m_w_down, v_w_ada, v_b_ada, v_g_attn_pre, v_g_attn_post, v_w_in, v_sink_a, v_g_mix_a, v_g_mix_b, v_w_out, v_g_mlp_pre, v_g_mlp_post, v_w_up, v_w_down):
    given = dict(w_ada=w_ada, b_ada=b_ada, g_attn_pre=g_attn_pre, g_attn_post=g_attn_post, w_in=w_in, sink_a=sink_a, g_mix_a=g_mix_a,
                 g_mix_b=g_mix_b, w_out=w_out, g_mlp_pre=g_mlp_pre, g_mlp_post=g_mlp_post, w_up=w_up, w_down=w_down)
    moms = dict(w_ada=(m_w_ada, v_w_ada), b_ada=(m_b_ada, v_b_ada), g_attn_pre=(m_g_attn_pre, v_g_attn_pre),
                g_attn_post=(m_g_attn_post, v_g_attn_post), w_in=(m_w_in, v_w_in), sink_a=(m_sink_a, v_sink_a),
                g_mix_a=(m_g_mix_a, v_g_mix_a), g_mix_b=(m_g_mix_b, v_g_mix_b), w_out=(m_w_out, v_w_out),
                g_mlp_pre=(m_g_mlp_pre, v_g_mlp_pre), g_mlp_post=(m_g_mlp_post, v_g_mlp_post), w_up=(m_w_up, v_w_up),
                w_down=(m_w_down, v_w_down))
    order = ["w_ada", "b_ada", "g_attn_pre", "g_attn_post", "w_in", "sink_a", "g_mix_a", "g_mix_b", "w_out", "g_mlp_pre",
             "g_mlp_post", "w_up", "w_down"]
    xi, yi, ci = _pos()
    chip = 2 * xi + yi

    c_arr = jnp.reshape(ci, (1,)).astype(jnp.int32)
    pos_arr = jnp.stack([chip, ci]).astype(jnp.int32)
    big = ("w_in", "w_out", "w_up", "w_down")

    b_cols = lax.dynamic_slice(b_ada, (0, chip * ADAW), (1, ADAW))
    mod, cond_all = _ada_fwd(c, w_ada[0], b_cols)
    gathered = [jax.ShapeDtypeStruct((NCHIP,) + given[n].shape[1:], bf16) for n in big]
    flight_in, (mod,) = _split_start("weights_start_first", [w_in[0].astype(bf16)], gathered[:1], [_weight_copies], [mod])
    mod, rest = lax.optimization_barrier((mod, [given[n][0] for n in big[1:]]))
    flight_rest, (mod, inv_lane) = _split_start("weights_start_rest", [w.astype(bf16) for w in rest], gathered[1:],
                                                [_weight_copies] * 3, [mod, _inv_lane()])
    mod = mod.reshape(BL, NMOD, D)

    def first_weight(after):
        srcs, lands = _split_wait("weights_wait_first", flight_in, [_weight_copies], after)
        cross, _ = _split_start("weights_pair_start_first", srcs, None, [_pair_weight_copies], [], lands=lands)
        _, (win_g,) = _split_wait("weights_pair_wait_first", cross, [_pair_weight_copies], ())
        return win_g

    def later_weights(after, carry):
        srcs, lands = _split_wait("weights_wait_rest", flight_rest, [_weight_copies] * 3, after)
        fl, (carry,) = _split_start("weights_pair_start_rest", srcs, None, [_pair_weight_copies] * 3, [carry], lands=lands)
        _, (wout_g,) = _split_wait("weights_pair_wait_out", fl[:1], [_pair_weight_copies], ())

        def mlp_weights(after):
            _, (wup_g, wdn_g) = _split_wait("weights_pair_wait_mlp", fl[1:], [_pair_weight_copies] * 2, after)
            return wup_g, wdn_g.reshape(DFF, D)

        return wout_g.reshape(D, D), mlp_weights, carry

    crossing, pending = {}, {}

    def grad_ready(group, g, carry):
        if group != "w_in_w_out":
            slab = g.reshape(NCHIP, DFF // NCHIP, D) if group == "w_down" else g
            land = jax.ShapeDtypeStruct((NDIRECT, slab.shape[1] // 2, slab.shape[2]), bf16)
            fl, (carry,) = _split_start("grad_start_" + group, [slab], [land], [_direct_grad_copies], [carry])
            pending[group] = ((group,), fl, [_direct_grad_copies])
            return carry
        names = ("w_in", "w_out")
        slabs = [g[0], g[1].reshape(NCHIP, D // NCHIP, D)]
        fl, (carry,) = _split_start("grad_pair_start_" + group, slabs,
                                    [jax.ShapeDtypeStruct((NCHIP, s.shape[1] // 2, s.shape[2]), f32) for s in slabs],
                                    [_pair_grad_copies] * len(names), [carry])
        crossing[group] = (names, fl)
        return carry

    def grad_reduce(group, after, carry):
        names, fl = crossing[group]
        slabs, landed = _split_wait("grad_pair_wait_" + group, fl, [_pair_grad_copies] * len(names), after)
        halves = [_pair_add(s, p, c_arr, "grad_pair_sum_" + n) for s, p, n in zip(slabs, landed, names)]
        fl, (carry,) = _split_start("grad_start_" + group, halves,
                                    [jax.ShapeDtypeStruct((NLINK,) + h.shape[1:], bf16) for h in halves],
                                    [_grad_copies] * len(names), [carry])
        pending[group] = (names, fl, [_grad_copies] * len(names))
        return carry

    grad_x, accs = _local_step(x, positions, mod, loss_target, inv_lane, first_weight, later_weights, grad_ready,
                               g_attn_pre, g_attn_post, sink_a, g_mix_a, g_mix_b, g_mlp_pre, g_mlp_post)

    grads, out = {}, {}

    def update(n):
        tr = (lambda a: a.T) if n == "w_in" else (lambda a: a)
        res = _adamw(tr(given[n][0]), tr(grads[n]), tr(moms[n][0][0]), tr(moms[n][1][0]), "adamw_" + n)
        out[n] = tuple(tr(a)[None] for a in res)
        return res[3]

    def finish(groups, after):
        names = sum((pending[g][0] for g in groups), ())
        fl = sum((pending[g][1] for g in groups), [])
        halves, landed = _split_wait("grad_wait_" + groups[0], fl, sum((pending[g][2] for g in groups), []), after)
        flights = []
        for h, q, n in zip(halves, landed, names):
            full = _chip_add(h, q, pos_arr, "grad_chip_sum_" + n)
            flights.append(_split_start("grad_gather_start_" + n, [jnp.zeros((8, LANES), f32)], None, [_pair_gather_copies],
                                        [], lands=[full])[0])
        last = None
        for n, fl1 in zip(names, flights):
            after = (flights[-1][0][0],) if last is None and fl1 is not flights[-1] else () if last is None else (last,)
            _, (grads[n],) = _split_wait("grad_gather_wait_" + n, fl1, [_pair_gather_copies], after)
            last = update(n)
        return last

    grads["w_ada"], gb, small = _small_allreduce(accs, cond_all)
    small = grad_reduce("w_in_w_out", (small,), small)
    last = finish(("w_down", "w_up"), (small,))
    finish(("w_in_w_out",), (last, update("w_ada")))
    loss, res = _adamw_small(small, gb, [(given[n], moms[n][0], moms[n][1]) for n, _, _ in SMALL])
    for n, _, _ in SMALL:
        out[n] = tuple(res[n])
    return (loss.reshape(()), grad_x, *[out[n][0] for n in order], *[out[n][1] for n in order],
            *[out[n][2] for n in order], *[out[n][3] for n in order])
```

```python
import numpy as np
import jax
import jax.numpy as jnp
from jax import lax
from jax.experimental import pallas as pl
from jax.experimental.pallas import tpu as pltpu

f32 = jnp.float32
bf16 = jnp.bfloat16
MESH = pl.DeviceIdType.MESH

D = 1024
SEQ = 2048
BL = 2
HD = 64
AQ = 512
AKV = 128
BW = 512
INW = 2304
DFF = 4096
NMOD = 6
ROT = 16
THETA = 500000.0
EPS = 1e-6
NEG = -1e30
BLK = 128
TM = 512
NJ = SEQ // TM
LANES = 128
SUBLANES = 8
NHEAD = AQ // HD
QSCALE = HD ** -0.5
NCHIP = 4
NDEV = 8
VMEM_LIMIT = 56 << 20

LR, B1, B2, AEPS, WD, STEP = 0.001, 0.9, 0.999, 1e-08, 0.01, 10

OFF_G_ATTN_PRE, OFF_G_ATTN_POST, OFF_G_MIX_A, OFF_G_MIX_B = 0, 1024, 2048, 2560
OFF_G_MLP_PRE, OFF_G_MLP_POST, OFF_SINK, OFF_LOSS = 3072, 4096, 5120, 5248
PAYW = NMOD * D


def _cp(sem=None):
    return pltpu.CompilerParams(dimension_semantics=sem, vmem_limit_bytes=VMEM_LIMIT)


def _dot(a, b):
    return jnp.dot(a, b, preferred_element_type=f32)


def _dot_nt(a, b):
    return lax.dot_general(a, b, (((1,), (1,)), ((), ())), preferred_element_type=f32)


def _dot_tn(a, b):
    return lax.dot_general(a, b, (((0,), (0,)), ((), ())), preferred_element_type=f32)


def _rms(x):
    r = lax.rsqrt(jnp.mean(x * x, axis=-1, keepdims=True) + EPS)
    return x * r, r


def _rms_bwd(dy, y, r):
    return r * (dy - y * jnp.mean(dy * y, axis=-1, keepdims=True))


def _colsum(v):
    return jnp.sum(v, axis=0, keepdims=True)


def _rope(p, c, s1, s2):
    outs = []
    for c0 in range(0, p.shape[1], LANES):
        pc = p[:, c0:c0 + LANES]
        outs.append(pc * c + pltpu.roll(pc, LANES - ROT // 2, 1) * s1 + pltpu.roll(pc, ROT // 2, 1) * s2)
    return outs[0] if len(outs) == 1 else jnp.concatenate(outs, axis=1)


def _rope_t(g, c, s1, s2):
    outs = []
    for c0 in range(0, g.shape[1], LANES):
        gc = g[:, c0:c0 + LANES]
        outs.append(gc * c + pltpu.roll(gc * s1, ROT // 2, 1) + pltpu.roll(gc * s2, LANES - ROT // 2, 1))
    return outs[0] if len(outs) == 1 else jnp.concatenate(outs, axis=1)


def _perm_store(val, scr, out_ref, d):
    nc = val.shape[1] // LANES
    for c in range(nc):
        scr[c] = val[:, LANES * c:LANES * (c + 1)]
    for c in range(nc):
        for r in range(d):
            out_ref[r, :, LANES * c:LANES * (c + 1)] = scr[c, pl.ds(r, TM // d, stride=d), :].astype(out_ref.dtype)


def _perm_load(in_ref, scr, d):
    nc = in_ref.shape[-1] // LANES
    for c in range(nc):
        for r in range(d):
            scr[c, pl.ds(r, TM // d, stride=d), :] = in_ref[r, :, LANES * c:LANES * (c + 1)].astype(f32)
    return jnp.concatenate([scr[c] for c in range(nc)], axis=1)


def _per_query_head(kv):
    r = pltpu.roll(kv, HD, 1)
    lo = lax.broadcasted_iota(jnp.int32, kv.shape, 1) < HD
    return jnp.concatenate([jnp.where(lo, kv, r), jnp.where(lo, r, kv)], axis=1)


def _per_kv_head(g):
    g0, g1 = g[:, :LANES] + g[:, LANES:2 * LANES], g[:, 2 * LANES:3 * LANES] + g[:, 3 * LANES:]
    lo = lax.broadcasted_iota(jnp.int32, g0.shape, 1) < HD
    return jnp.where(lo, g0 + pltpu.roll(g0, HD, 1), g1 + pltpu.roll(g1, HD, 1))


def _tok(w):
    return pl.BlockSpec((None, TM, w), lambda b, j: (b, j, 0))


def _perm_spec(d, w):
    return pl.BlockSpec((None, d, TM // d, w), lambda b, j: (b, 0, j, 0))


def _full(shape):
    n = len(shape)
    return pl.BlockSpec(shape, lambda b, j: (0,) * n)


MOD_SPEC = pl.BlockSpec((None, NMOD, D), lambda b, j: (b, 0, 0))
ACCB_SPEC = pl.BlockSpec((None, SUBLANES, D), lambda b, j: (b, 0, 0))
ACCG_SPEC = pl.BlockSpec((SUBLANES, D), lambda b, j: (0, 0))
ACC_SHAPES = [jax.ShapeDtypeStruct((BL, SUBLANES, D), f32), jax.ShapeDtypeStruct((SUBLANES, D), f32)]


def _acc_init(accb_ref, accg_ref):
    b, j = pl.program_id(0), pl.program_id(1)

    @pl.when(j == 0)
    def _():
        accb_ref[...] = jnp.zeros_like(accb_ref)

    @pl.when((b == 0) & (j == 0))
    def _():
        accg_ref[...] = jnp.zeros_like(accg_ref)


def _rope_tables(pos_col, inv_lane):
    def body(p_ref, inv_ref, c_ref, s1_ref, s2_ref):
        ang = p_ref[...].astype(f32) * inv_ref[...]
        j = lax.broadcasted_iota(jnp.int32, (TM, LANES), 1) % HD
        cs, sn = jnp.cos(ang), jnp.sin(ang)
        c_ref[...] = jnp.where(j < ROT, cs, 1.0)
        s1_ref[...] = jnp.where(j < ROT // 2, -sn, 0.0)
        s2_ref[...] = jnp.where((j >= ROT // 2) & (j < ROT), sn, 0.0)

    n = BL * SEQ // TM
    return pl.pallas_call(
        body, name="rope_tables", grid=(n,),
        in_specs=[pl.BlockSpec((TM, 1), lambda i: (i, 0)), pl.BlockSpec((1, LANES), lambda i: (0, 0))],
        out_specs=[pl.BlockSpec((TM, LANES), lambda i: (i, 0))] * 3,
        out_shape=[jax.ShapeDtypeStruct((BL * SEQ, LANES), f32)] * 3,
    )(pos_col, inv_lane)


def _attn_in(x, mod, g_pre, w_in, tc, ts1, ts2):
    def body(x_ref, mod_ref, g_ref, wg_ref, c_ref, s1_ref, s2_ref,
             h_ref, qa_ref, ka_ref, va_ref, q1_ref, k1_ref, v1_ref, q4_ref, k4_ref, v4_ref, q16_ref, k16_ref, v16_ref,
             w_ref, scr):
        @pl.when((pl.program_id(0) == 0) & (pl.program_id(1) == 0))
        def _():
            w_ref[...] = jnp.concatenate([wg_ref[s] for s in range(NCHIP)], axis=1)

        xn, _ = _rms(x_ref[...])
        h = (xn * g_ref[...]) * (1.0 + mod_ref[1:2, :]) + mod_ref[0:1, :]
        hb = h.astype(bf16)
        h_ref[...] = hb
        proj = _dot(hb, w_ref[...])
        c, s1, s2 = c_ref[...], s1_ref[...], s2_ref[...]
        o1, o2, o3, o4, o5 = AQ, AQ + AKV, AQ + 2 * AKV, AQ + 2 * AKV + BW, AQ + 2 * AKV + 2 * BW
        qa_ref[...] = (_rope(proj[:, :o1], c, s1, s2) * QSCALE).astype(bf16)
        ka_ref[...] = _per_query_head(_rope(proj[:, o1:o2], c, s1, s2)).astype(bf16)
        va_ref[...] = _per_query_head(proj[:, o2:o3]).astype(bf16)
        qb = _rope(proj[:, o3:o4], c, s1, s2) * QSCALE
        kb = _rope(proj[:, o4:o5], c, s1, s2)
        vb = proj[:, o5:]
        for val, r1, r4, r16 in ((qb, q1_ref, q4_ref, q16_ref), (kb, k1_ref, k4_ref, k16_ref), (vb, v1_ref, v4_ref, v16_ref)):
            r1[...] = val.astype(bf16)
            _perm_store(val, scr, r4, 4)
            _perm_store(val, scr, r16, 16)

    nat = lambda w: jax.ShapeDtypeStruct((BL, SEQ, w), bf16)
    p4 = jax.ShapeDtypeStruct((BL, 4, SEQ // 4, BW), bf16)
    p16 = jax.ShapeDtypeStruct((BL, 16, SEQ // 16, BW), bf16)
    return pl.pallas_call(
        body, name="attn_in", grid=(BL, NJ),
        in_specs=[_tok(D), MOD_SPEC, _full((1, D)), _full((NCHIP, D, INW // NCHIP)), _tok(LANES), _tok(LANES), _tok(LANES)],
        out_specs=([_tok(D), _tok(AQ), _tok(2 * AKV), _tok(2 * AKV)] + [_tok(BW)] * 3 + [_perm_spec(4, BW)] * 3 + [_perm_spec(16, BW)] * 3
                   + [_full((D, INW))]),
        out_shape=[nat(D), nat(AQ), nat(2 * AKV), nat(2 * AKV)] + [nat(BW)] * 3 + [p4] * 3 + [p16] * 3
                  + [jax.ShapeDtypeStruct((D, INW), bf16)],
        scratch_shapes=[pltpu.VMEM((BW // LANES, TM, LANES), f32)],
        compiler_params=_cp(("arbitrary", "arbitrary")),
    )(x, mod, g_pre, w_in, tc, ts1, ts2)


def _kv_cat(cur_ref, prev_ref, p, cache):
    key = (id(cur_ref), p)
    if key not in cache:
        sl = slice(LANES * p, LANES * (p + 1))
        cache[key] = cur_ref[:, sl] if prev_ref is None else jnp.concatenate([prev_ref[:, sl], cur_ref[:, sl]], axis=0)
    return cache[key]


def _lane_half(a, hh):
    lo = lax.broadcasted_iota(jnp.int32, a.shape, 1) < HD
    return jnp.where(lo, a, jnp.zeros_like(a)) if hh == 0 else jnp.where(lo, jnp.zeros_like(a), a)


ATT_UNITS = 4


def _att_units(nb):
    return ATT_UNITS if nb == 1 else min(ATT_UNITS, nb)


def _attn_specs(n, nb, descending):
    u = _att_units(nb)
    if nb == 1:
        return (lambda ww: pl.BlockSpec((u, BLK, ww), lambda a, i: (a, 0, 0))), None, (n // u, 1)
    steps = nb // u
    at = (lambda i: steps - 1 - i) if descending else (lambda i: i)
    cur = lambda ww: pl.BlockSpec((None, u * BLK, ww), lambda a, i: (a, at(i), 0))
    prev = lambda ww: pl.BlockSpec((None, BLK, ww), lambda a, i: (a, jnp.maximum(u * at(i) - 1, 0), 0))
    return cur, prev, (n, steps)


def _attn_fwd(q, k, v, sink, *, max_dist, o_dtype, name):
    n, l, w = q.shape
    wk = k.shape[-1]
    nb = l // BLK
    has_sink = sink is not None

    def body(*refs):
        sink_ref = None
        if has_sink:
            sink_ref, refs = refs[0], refs[1:]
        if nb > 1:
            q_ref, kc_ref, kp_ref, vc_ref, vp_ref, o_ref, lse_ref = refs[:7]
            first = pl.program_id(1) == 0
            for u in range(_att_units(nb)):
                rows, before = pl.ds(BLK * u, BLK), pl.ds(BLK * (u - 1), BLK)
                unit(q_ref.at[rows, :], kc_ref.at[rows, :], kp_ref if u == 0 else kc_ref.at[before, :],
                     vc_ref.at[rows, :], vp_ref if u == 0 else vc_ref.at[before, :], o_ref.at[rows, :], lse_ref.at[rows, :],
                     jnp.logical_not(first) if u == 0 else True, sink_ref, *refs[7:])
        else:
            q_ref, kc_ref, vc_ref, o_ref, lse_ref = refs[:5]
            for u in range(_att_units(nb)):
                unit(q_ref.at[u], kc_ref.at[u], None, vc_ref.at[u], None, o_ref.at[u], lse_ref.at[u], None, sink_ref, *refs[5:])

    def unit(q_ref, kc_ref, kp_ref, vc_ref, vp_ref, o_ref, lse_ref, has_prev, sink_ref, sscr, pscr, dscr):
        qi = lax.broadcasted_iota(jnp.int32, (BLK, BLK), 0)
        kj = lax.broadcasted_iota(jnp.int32, (BLK, BLK), 1)
        tri = kj <= qi
        eye = kj == qi
        cache = {}
        for p in range(w // LANES):
            qpair = q_ref[:, LANES * p:LANES * (p + 1)]
            kcat = _kv_cat(kc_ref, kp_ref, p // share, cache)
            for hh in range(2):
                s = _dot_nt(_lane_half(qpair, hh), kcat)
                if nb > 1:
                    sp = s[:, :BLK] if has_prev is True else jnp.where(has_prev, s[:, :BLK], NEG)
                    sscr[2 * p + hh] = jnp.where(tri, s[:, BLK:], sp)
                    if diag:
                        dscr[2 * p + hh] = jnp.where(eye, sp, NEG)
                else:
                    sscr[2 * p + hh] = jnp.where(tri, s, NEG)
        lane = lax.broadcasted_iota(jnp.int32, (BLK, LANES), 1)
        lse_all = jnp.zeros((BLK, LANES), f32)
        for p in range(w // LANES):
            for hh in range(2):
                h = 2 * p + hh
                comb = sscr[h]
                if diag:
                    dtile = dscr[h]
                    m = jnp.max(jnp.maximum(comb, dtile), axis=-1, keepdims=True)
                else:
                    m = jnp.max(comb, axis=-1, keepdims=True)
                if has_sink:
                    sk = sink_ref[0, h]
                    m = jnp.maximum(m, sk)
                e = jnp.exp(comb - m)
                if diag:
                    ed = jnp.exp(dtile - m)
                    den = jnp.sum(e + ed, axis=-1, keepdims=True)
                else:
                    den = jnp.sum(e, axis=-1, keepdims=True)
                if has_sink:
                    den = den + jnp.exp(sk - m)
                inv = 1.0 / den
                if nb > 1:
                    pscr[h, :, :BLK] = (jnp.where(tri, ed if diag else 0.0, e) * inv).astype(bf16)
                    pscr[h, :, BLK:] = (jnp.where(tri, e, 0.0) * inv).astype(bf16)
                else:
                    pscr[h] = (e * inv).astype(bf16)
                lse_all = jnp.where(lane == h, jnp.broadcast_to(m + jnp.log(den), (BLK, LANES)), lse_all)
        lse_ref[...] = lse_all
        for p in range(w // LANES):
            vcat = _kv_cat(vc_ref, vp_ref, p // share, cache)
            o_ref[:, LANES * p:LANES * (p + 1)] = (_dot(pscr[2 * p], _lane_half(vcat, 0))
                                                   + _dot(pscr[2 * p + 1], _lane_half(vcat, 1))).astype(o_ref.dtype)

    assert max_dist in (BLK - 1, BLK) and w % wk == 0
    share = w // wk
    diag = nb > 1 and max_dist == BLK
    cur, prev, grid = _attn_specs(n, nb, False)
    in_specs = [cur(w), cur(wk)] + ([prev(wk)] if nb > 1 else []) + [cur(wk)] + ([prev(wk)] if nb > 1 else [])
    args = [q, k] + ([k] if nb > 1 else []) + [v] + ([v] if nb > 1 else [])
    if has_sink:
        in_specs = [pl.BlockSpec(memory_space=pltpu.SMEM)] + in_specs
        args = [sink] + args
    return pl.pallas_call(
        body, name=name, grid=grid, in_specs=in_specs,
        out_specs=[cur(w), cur(LANES)],
        out_shape=[jax.ShapeDtypeStruct((n, l, w), o_dtype), jax.ShapeDtypeStruct((n, l, LANES), f32)],
        scratch_shapes=[pltpu.VMEM((w // HD, BLK, BLK), f32), pltpu.VMEM((w // HD, BLK, 2 * BLK if nb > 1 else BLK), bf16),
                        pltpu.VMEM((w // HD if diag else 1, BLK, BLK), f32)],
        compiler_params=_cp(("arbitrary", "arbitrary")),
    )(*args)


def _attn_bwd(q, k, v, do, delta, lse, sink, *, max_dist, name):
    n, l, w = q.shape
    wk = k.shape[-1]
    nb = l // BLK
    has_sink = sink is not None

    def body(*refs):
        sink_ref = dsink_ref = ck = cv = None
        if has_sink:
            sink_ref, refs = refs[0], refs[1:]
        nin = 8 if nb > 1 else 6
        ins, rest = refs[:nin], refs[nin:]
        if has_sink:
            dq_ref, dk_ref, dv_ref, dsink_ref = rest[:4]
            rest = rest[4:]
        else:
            dq_ref, dk_ref, dv_ref = rest[:3]
            rest = rest[3:]
        step = pl.program_id(1)
        if has_sink:
            @pl.when((pl.program_id(0) == 0) & (step == 0))
            def _():
                dsink_ref[...] = jnp.zeros_like(dsink_ref)

        if nb > 1:
            q_ref, kc_ref, kp_ref, vc_ref, vp_ref, do_ref, delta_ref, lse_ref = ins
            ck, cv = rest[:2]

            @pl.when(step == 0)
            def _():
                ck[...] = jnp.zeros_like(ck)
                cv[...] = jnp.zeros_like(cv)

            last = step == nb // _att_units(nb) - 1
            for u in reversed(range(_att_units(nb))):
                rows, before = pl.ds(BLK * u, BLK), pl.ds(BLK * (u - 1), BLK)
                unit(q_ref.at[rows, :], kc_ref.at[rows, :], kp_ref if u == 0 else kc_ref.at[before, :],
                     vc_ref.at[rows, :], vp_ref if u == 0 else vc_ref.at[before, :], do_ref.at[rows, :],
                     delta_ref.at[rows, :], lse_ref.at[rows, :], dq_ref.at[rows, :], dk_ref.at[rows, :], dv_ref.at[rows, :],
                     jnp.logical_not(last) if u == 0 else True, sink_ref, dsink_ref, ck, cv, *rest[2:])
        else:
            q_ref, kc_ref, vc_ref, do_ref, delta_ref, lse_ref = ins
            for u in range(_att_units(nb)):
                unit(q_ref.at[u], kc_ref.at[u], None, vc_ref.at[u], None, do_ref.at[u], delta_ref.at[u], lse_ref.at[u],
                     dq_ref.at[u], dk_ref.at[u], dv_ref.at[u], None, sink_ref, dsink_ref, None, None, *rest)

    def unit(q_ref, kc_ref, kp_ref, vc_ref, vp_ref, do_ref, delta_ref, lse_ref, dq_ref, dk_ref, dv_ref, has_prev,
             sink_ref, dsink_ref, ck, cv, sscr, dpscr, pscr, dsscr, dscr=None, ddscr=None):
        lane = lax.broadcasted_iota(jnp.int32, (BLK, LANES), 1)
        qi = lax.broadcasted_iota(jnp.int32, (BLK, BLK), 0)
        kj = lax.broadcasted_iota(jnp.int32, (BLK, BLK), 1)
        tri = kj <= qi
        eye = kj == qi
        cache = {}
        kp, vp = kp_ref, vp_ref
        for p in range(w // LANES):
            sl = slice(LANES * p, LANES * (p + 1))
            qpair, dopair = q_ref[:, sl], do_ref[:, sl]
            kcat, vcat = _kv_cat(kc_ref, kp, p // share, cache), _kv_cat(vc_ref, vp, p // share, cache)
            for hh in range(2):
                h = 2 * p + hh
                s = _dot_nt(_lane_half(qpair, hh), kcat)
                dp = _dot_nt(_lane_half(dopair, hh), vcat)
                if nb > 1:
                    sp = s[:, :BLK] if has_prev is True else jnp.where(has_prev, s[:, :BLK], NEG)
                    sscr[h] = jnp.where(tri, s[:, BLK:], sp)
                    dpscr[h] = jnp.where(tri, dp[:, BLK:], dp[:, :BLK])
                    if diag:
                        dscr[h] = jnp.where(eye, sp, NEG)
                        ddscr[h] = dp[:, :BLK]
                else:
                    sscr[h] = jnp.where(tri, s, NEG)
                    dpscr[h] = dp
        for p in range(w // LANES):
            for hh in range(2):
                h = 2 * p + hh
                lse_b = jnp.broadcast_to(lse_ref[:, h:h + 1], (BLK, BLK))
                delta = jnp.broadcast_to(delta_ref[:, h:h + 1], (BLK, BLK))
                pr = jnp.exp(sscr[h] - lse_b)
                ds = pr * (dpscr[h] - delta)
                if nb > 1:
                    if diag:
                        prd = jnp.exp(dscr[h] - lse_b)
                        dsd = prd * (ddscr[h] - delta)
                    else:
                        prd = dsd = 0.0
                    pscr[h, :, :BLK] = jnp.where(tri, prd, pr).astype(bf16)
                    pscr[h, :, BLK:] = jnp.where(tri, pr, 0.0).astype(bf16)
                    dsscr[h, :, :BLK] = jnp.where(tri, dsd, ds).astype(bf16)
                    dsscr[h, :, BLK:] = jnp.where(tri, ds, 0.0).astype(bf16)
                else:
                    pscr[h] = pr.astype(bf16)
                    dsscr[h] = ds.astype(bf16)
                if has_sink:
                    dsk = -jnp.sum(jnp.where(lane == 0, jnp.exp(sink_ref[0, h] - lse_b) * delta, 0.0), keepdims=True)
                    dsink_ref[h:h + 1, :] += jnp.broadcast_to(dsk, (1, LANES))
        for p in range(w // LANES):
            sl = slice(LANES * p, LANES * (p + 1))
            qpair, dopair = q_ref[:, sl], do_ref[:, sl]
            kcat = _kv_cat(kc_ref, kp, p // share, cache)
            dq_ref[:, sl] = _dot(dsscr[2 * p], _lane_half(kcat, 0)) + _dot(dsscr[2 * p + 1], _lane_half(kcat, 1))
            dk_pair = _dot_tn(dsscr[2 * p], _lane_half(qpair, 0)) + _dot_tn(dsscr[2 * p + 1], _lane_half(qpair, 1))
            dv_pair = _dot_tn(pscr[2 * p], _lane_half(dopair, 0)) + _dot_tn(pscr[2 * p + 1], _lane_half(dopair, 1))
            if nb > 1:
                dk_ref[:, sl] = dk_pair[BLK:] + ck[:, sl]
                dv_ref[:, sl] = dv_pair[BLK:] + cv[:, sl]
                ck[:, sl] = dk_pair[:BLK]
                cv[:, sl] = dv_pair[:BLK]
            else:
                dk_ref[:, sl] = dk_pair
                dv_ref[:, sl] = dv_pair

    assert max_dist in (BLK - 1, BLK) and w % wk == 0
    share = w // wk
    diag = nb > 1 and max_dist == BLK
    cur, prev, grid = _attn_specs(n, nb, True)
    in_specs = ([cur(w), cur(wk)] + ([prev(wk)] if nb > 1 else []) + [cur(wk)] + ([prev(wk)] if nb > 1 else [])
                + [cur(w), cur(LANES), cur(LANES)])
    args = [q, k] + ([k] if nb > 1 else []) + [v] + ([v] if nb > 1 else []) + [do, delta, lse]
    out_specs = [cur(w)] * 3
    out_shape = [jax.ShapeDtypeStruct((n, l, w), f32)] * 3
    if has_sink:
        in_specs = [pl.BlockSpec(memory_space=pltpu.SMEM)] + in_specs
        args = [sink] + args
        out_specs.append(pl.BlockSpec((NHEAD, LANES), lambda a, i: (0, 0)))
        out_shape.append(jax.ShapeDtypeStruct((NHEAD, LANES), f32))
    nh = w // HD
    scratch = [pltpu.VMEM((BLK, w), f32), pltpu.VMEM((BLK, w), f32)] if nb > 1 else []
    scratch += [pltpu.VMEM((nh, BLK, BLK), f32)] * 2 + [pltpu.VMEM((nh, BLK, 2 * BLK if nb > 1 else BLK), bf16)] * 2
    if diag:
        scratch += [pltpu.VMEM((nh, BLK, BLK), f32)] * 2
    return pl.pallas_call(
        body, name=name, grid=grid, in_specs=in_specs, out_specs=out_specs, out_shape=out_shape,
        scratch_shapes=scratch, compiler_params=_cp(("arbitrary", "arbitrary")),
    )(*args)


def _split2(x):
    hi = x.astype(bf16)
    return hi, (x - hi.astype(f32)).astype(bf16)


def _heads_to_lanes(xc, e):
    return sum(_dot(t, e) for t in _split2(xc))


def _lanes_to_heads(x, g):
    return sum(_dot(t, g) for t in _split2(x))


HEAD_EXPAND = (np.arange(LANES)[:, None] == np.arange(BW)[None, :] // HD).astype(np.float32)
HEAD_SUM = HEAD_EXPAND.T.copy()


def _branch_weights(l1_ref, l4_ref, l16_ref, scr):
    l4v = _perm_load(l4_ref, scr, 4)
    l16v = _perm_load(l16_ref, scr, 16)
    l1v = l1_ref[...]
    m = jnp.maximum(jnp.maximum(l1v, l4v), l16v)
    e1, e4, e16 = jnp.exp(l1v - m), jnp.exp(l4v - m), jnp.exp(l16v - m)
    z = e1 + e4 + e16
    return e1 / z, e4 / z, e16 / z


def _mix_out(oa, o1, l1, o4, l4, o16, l16, g_mix_a, g_mix_b, w_out, x, mod, g_post):
    def body(oa_ref, o1_ref, l1_ref, o4_ref, l4_ref, o16_ref, l16_ref, ga_ref, gb_ref, w_ref, x_ref, mod_ref, gp_ref, e_ref,
             x1_ref, y_ref, mixed_ref, ob_ref, scr):
        w1, w4, w16 = _branch_weights(l1_ref, l4_ref, l16_ref, scr)
        e = e_ref[...]
        x1w, x4w = _heads_to_lanes(w1, e), _heads_to_lanes(w4, e)
        ob = (x1w * o1_ref[...].astype(f32) + x4w * _perm_load(o4_ref, scr, 4)
              + (1.0 - x1w - x4w) * _perm_load(o16_ref, scr, 16))
        ob_ref[...] = ob
        oan, _ = _rms(oa_ref[...])
        obn, _ = _rms(ob)
        mixed = jnp.concatenate([oan * ga_ref[...], obn * gb_ref[...]], axis=1).astype(bf16)
        mixed_ref[...] = mixed
        y = _dot(mixed, w_ref[...])
        y_ref[...] = y
        yn, _ = _rms(y)
        x1_ref[...] = x_ref[...] + mod_ref[2:3, :] * (yn * gp_ref[...])

    nat = lambda w, dt: jax.ShapeDtypeStruct((BL, SEQ, w), dt)
    return pl.pallas_call(
        body, name="mix_out", grid=(BL, NJ),
        in_specs=[_tok(AQ), _tok(BW), _tok(LANES), _perm_spec(4, BW), _perm_spec(4, LANES), _perm_spec(16, BW),
                  _perm_spec(16, LANES), _full((1, AQ)), _full((1, BW)), _full((D, D)), _tok(D), MOD_SPEC, _full((1, D)),
                  _full((LANES, BW))],
        out_specs=[_tok(D), _tok(D), _tok(D), _tok(BW)],
        out_shape=[nat(D, f32), nat(D, f32), nat(D, bf16), nat(BW, f32)],
        scratch_shapes=[pltpu.VMEM((BW // LANES, TM, LANES), f32)],
        compiler_params=_cp(("arbitrary", "arbitrary")),
    )(oa, o1, l1, o4, l4, o16, l16, g_mix_a, g_mix_b, w_out, x, mod, g_post, jnp.asarray(HEAD_EXPAND, bf16))


def _mlp_up(x1, mod, g_pre, w_up):
    def body(x_ref, mod_ref, g_ref, w_ref, h_ref, u_ref, a_ref):
        xn, _ = _rms(x_ref[...])
        h = (xn * g_ref[...]) * (1.0 + mod_ref[4:5, :]) + mod_ref[3:4, :]
        hb = h.astype(bf16)
        h_ref[...] = hb
        for s in range(NCHIP):
            u = _dot(hb, w_ref[s])
            u_ref[:, D * s:D * (s + 1)] = u.astype(bf16)
            a_ref[:, D * s:D * (s + 1)] = jnp.square(jnp.maximum(u, 0.0)).astype(bf16)

    nat = lambda w: jax.ShapeDtypeStruct((BL, SEQ, w), bf16)
    return pl.pallas_call(
        body, name="mlp_up", grid=(BL, NJ),
        in_specs=[_tok(D), MOD_SPEC, _full((1, D)), _full((NCHIP, D, D))],
        out_specs=[_tok(D), _tok(DFF), _tok(DFF)], out_shape=[nat(D), nat(DFF), nat(DFF)],
        compiler_params=_cp(("arbitrary", "arbitrary")),
    )(x1, mod, g_pre, w_up)


def _mlp_down(a, w_down, x1, target, mod, g_post):
    def body(a_ref, w_ref, x_ref, t_ref, mod_ref, g_ref, gx_ref, dy_ref, accb_ref, accg_ref):
        _acc_init(accb_ref, accg_ref)
        y2 = _dot(a_ref[...], w_ref[...])
        yn, r = _rms(y2)
        g = g_ref[...]
        gt = mod_ref[5:6, :]
        n2 = yn * g
        err = x_ref[...] + gt * n2 - t_ref[...]
        gout = err * (1.0 / D)
        gx_ref[...] = gout
        dn2 = gout * gt
        dy_ref[...] = _rms_bwd(dn2 * g, yn, r).astype(bf16)
        accb_ref[0:1, :] += _colsum(gout * n2)
        accg_ref[0:1, :] += _colsum(dn2 * yn)
        accg_ref[1:2, :] += jnp.broadcast_to(jnp.sum(err * err, keepdims=True), (1, D))

    return pl.pallas_call(
        body, name="mlp_down", grid=(BL, NJ),
        in_specs=[_tok(DFF), _full((DFF, D)), _tok(D), _tok(D), MOD_SPEC, _full((1, D))],
        out_specs=[_tok(D), _tok(D), ACCB_SPEC, ACCG_SPEC],
        out_shape=[jax.ShapeDtypeStruct((BL, SEQ, D), f32), jax.ShapeDtypeStruct((BL, SEQ, D), bf16)] + ACC_SHAPES,
        compiler_params=_cp(("arbitrary", "arbitrary")),
    )(a, w_down, x1, target, mod, g_post)


def _mlp_bwd(dy2, u, w_down, w_up, x1, gx, mod, g_pre):
    def body(dy_ref, u_ref, wd_hbm, wu_hbm, x_ref, gx_ref, mod_ref, g_ref, du_ref, gx1_ref, accb_ref, accg_ref, wd, wu, sem):
        _acc_init(accb_ref, accg_ref)
        first = (pl.program_id(0) == 0) & (pl.program_id(1) == 0)
        c1 = pltpu.make_async_copy(wd_hbm, wd, sem.at[0])
        c2 = pltpu.make_async_copy(wu_hbm, wu, sem.at[1])

        @pl.when(first)
        def _():
            c1.start()
            c2.start()
            c1.wait()

        dy = dy_ref[...]
        for s in range(NCHIP):
            sl = slice(D * s, D * (s + 1))
            da = _dot_nt(dy, wd[sl, :])
            du_ref[:, sl] = (da * (2.0 * jnp.maximum(u_ref[:, sl].astype(f32), 0.0))).astype(bf16)

        @pl.when(first)
        def _():
            c2.wait()

        dh = jnp.zeros((TM, D), f32)
        for s in range(NCHIP):
            dh = dh + _dot_nt(du_ref[:, D * s:D * (s + 1)], wu[s])
        xn, r = _rms(x_ref[...])
        g = g_ref[...]
        n = xn * g
        dn = dh * (1.0 + mod_ref[4:5, :])
        gx1_ref[...] = gx_ref[...] + _rms_bwd(dn * g, xn, r)
        accb_ref[0:1, :] += _colsum(dh * n)
        accb_ref[1:2, :] += _colsum(dh)
        accg_ref[0:1, :] += _colsum(dn * xn)

    anyspec = pl.BlockSpec(memory_space=pl.ANY)
    return pl.pallas_call(
        body, name="mlp_bwd", grid=(BL, NJ),
        in_specs=[_tok(D), _tok(DFF), anyspec, anyspec, _tok(D), _tok(D), MOD_SPEC, _full((1, D))],
        out_specs=[_tok(DFF), _tok(D), ACCB_SPEC, ACCG_SPEC],
        out_shape=[jax.ShapeDtypeStruct((BL, SEQ, DFF), bf16), jax.ShapeDtypeStruct((BL, SEQ, D), f32)] + ACC_SHAPES,
        scratch_shapes=[pltpu.VMEM((DFF, D), bf16), pltpu.VMEM((NCHIP, D, D), bf16), pltpu.SemaphoreType.DMA((2,))],
        compiler_params=_cp(("arbitrary", "arbitrary")),
    )(dy2, u, w_down, w_up, x1, gx, mod, g_pre)


def _matmul_tn(a, b, *, tn, col_blocked, name, out_dtype=f32):
    t, m = a.shape
    n = b.shape[1]
    tmm = min(m, 1024)
    tk = 2048 if tn <= 1024 else 1024
    nk = t // tk

    def body(a_ref, b_ref, o_ref, acc):
        k = pl.program_id(2)

        @pl.when(k == 0)
        def _():
            acc[...] = jnp.zeros_like(acc)

        acc[...] += _dot_tn(a_ref[...], b_ref[...])

        @pl.when(k == nk - 1)
        def _():
            o_ref[...] = acc[...].astype(out_dtype)

    if col_blocked:
        out_spec = pl.BlockSpec((None, tmm, tn), lambda i, j, k: (j, i, 0))
        out_shape = jax.ShapeDtypeStruct((n // tn, m, tn), out_dtype)
    else:
        out_spec = pl.BlockSpec((tmm, tn), lambda i, j, k: (i, j))
        out_shape = jax.ShapeDtypeStruct((m, n), out_dtype)
    return pl.pallas_call(
        body, name=name, grid=(m // tmm, n // tn, nk),
        in_specs=[pl.BlockSpec((tk, tmm), lambda i, j, k: (k, i)), pl.BlockSpec((tk, tn), lambda i, j, k: (k, j))],
        out_specs=out_spec, out_shape=out_shape, scratch_shapes=[pltpu.VMEM((tmm, tn), f32)],
        compiler_params=_cp(("arbitrary", "arbitrary", "arbitrary")),
    )(a, b)


def _grad_w_in(h, dproj):
    t = h.shape[0]
    tk = 1024
    nk = t // tk
    sw = INW // NCHIP

    def body(a_ref, b_ref, o_ref, acc):
        k = pl.program_id(0)

        @pl.when(k == 0)
        def _():
            acc[...] = jnp.zeros_like(acc)

        acc[...] += _dot_tn(a_ref[...], b_ref[...])

        @pl.when(k == nk - 1)
        def _():
            for s in range(NCHIP):
                o_ref[s] = acc[:, sw * s:sw * (s + 1)].astype(bf16)

    return pl.pallas_call(
        body, name="grad_w_in", grid=(nk,),
        in_specs=[pl.BlockSpec((tk, D), lambda k: (k, 0)), pl.BlockSpec((tk, INW), lambda k: (k, 0))],
        out_specs=pl.BlockSpec((NCHIP, D, sw), lambda k: (0, 0, 0)), out_shape=jax.ShapeDtypeStruct((NCHIP, D, sw), bf16),
        scratch_shapes=[pltpu.VMEM((D, INW), f32)], compiler_params=_cp(("arbitrary",)),
    )(h, dproj)


def _attn_out_bwd(gx1, y, mod, g_post, w_out, oa, ob, g_mix_a, g_mix_b, l1, l4, l16):
    def body(gx_ref, y_ref, mod_ref, gp_ref, w_ref, oa_ref, ob_ref, ga_ref, gb_ref, l1_ref, l4_ref, l16_ref, e_ref, g_ref,
             dy_ref, doa_ref, do1_ref, do4_ref, do16_ref, da_ref, d1_ref, d4_ref, d16_ref, accb_ref, accg_ref, scr):
        _acc_init(accb_ref, accg_ref)
        w1, w4, w16 = _branch_weights(l1_ref, l4_ref, l16_ref, scr)
        e, hs = e_ref[...], g_ref[...]
        gx1v = gx_ref[...]
        yn, ry = _rms(y_ref[...])
        gp = gp_ref[...]
        gt = mod_ref[2:3, :]
        dn1 = gx1v * gt
        dy = _rms_bwd(dn1 * gp, yn, ry).astype(bf16)
        dy_ref[...] = dy
        dmixed = _dot_nt(dy, w_ref[...])
        dma, dmb = dmixed[:, :AQ], dmixed[:, AQ:]
        oa, ob = oa_ref[...], ob_ref[...]
        oan, ra = _rms(oa)
        obn, rb = _rms(ob)
        doa = _rms_bwd(dma * ga_ref[...], oan, ra)
        doa_ref[...] = doa.astype(bf16)
        da_ref[...] = _lanes_to_heads(doa * oa, hs)
        dob = _rms_bwd(dmb * gb_ref[...], obn, rb)
        dd = _lanes_to_heads(dob * ob, hs)
        x1w, x4w = _heads_to_lanes(w1, e), _heads_to_lanes(w4, e)
        do1_ref[...] = (x1w * dob).astype(bf16)
        d1_ref[...] = w1 * dd
        _perm_store(x4w * dob, scr, do4_ref, 4)
        _perm_store(w4 * dd, scr, d4_ref, 4)
        _perm_store((1.0 - x1w - x4w) * dob, scr, do16_ref, 16)
        _perm_store(w16 * dd, scr, d16_ref, 16)
        accb_ref[0:1, :] += _colsum(gx1v * (yn * gp))
        accg_ref[0:1, :] += _colsum(dn1 * yn)
        accg_ref[1:2, :] += jnp.concatenate([_colsum(dma * oan), _colsum(dmb * obn)], axis=1)

    nat = lambda w, dt: jax.ShapeDtypeStruct((BL, SEQ, w), dt)
    return pl.pallas_call(
        body, name="attn_out_bwd", grid=(BL, NJ),
        in_specs=[_tok(D), _tok(D), MOD_SPEC, _full((1, D)), _full((D, D)), _tok(AQ), _tok(BW), _full((1, AQ)), _full((1, BW)),
                  _tok(LANES), _perm_spec(4, LANES), _perm_spec(16, LANES), _full((LANES, BW)), _full((BW, LANES))],
        out_specs=[_tok(D), _tok(AQ), _tok(BW), _perm_spec(4, BW), _perm_spec(16, BW),
                   _tok(LANES), _tok(LANES), _perm_spec(4, LANES), _perm_spec(16, LANES), ACCB_SPEC, ACCG_SPEC],
        out_shape=[nat(D, bf16), nat(AQ, bf16), nat(BW, bf16), jax.ShapeDtypeStruct((BL, 4, SEQ // 4, BW), bf16),
                   jax.ShapeDtypeStruct((BL, 16, SEQ // 16, BW), bf16), nat(LANES, f32), nat(LANES, f32),
                   jax.ShapeDtypeStruct((BL, 4, SEQ // 4, LANES), f32), jax.ShapeDtypeStruct((BL, 16, SEQ // 16, LANES), f32)]
                  + ACC_SHAPES,
        scratch_shapes=[pltpu.VMEM((BW // LANES, TM, LANES), f32)],
        compiler_params=_cp(("arbitrary", "arbitrary")),
    )(gx1, y, mod, g_post, w_out, oa, ob, g_mix_a, g_mix_b, l1, l4, l16, jnp.asarray(HEAD_EXPAND, bf16),
      jnp.asarray(HEAD_SUM, bf16))


def _attn_in_bwd(dqa, dka, dva, d1, d4, d16, tc, ts1, ts2, w_in, x, gx1, mod, g_pre):
    def body(dqa_ref, dka_ref, dva_ref, dq1_ref, dk1_ref, dv1_ref, dq4_ref, dk4_ref, dv4_ref, dq16_ref, dk16_ref, dv16_ref,
             c_ref, s1_ref, s2_ref, w_ref, x_ref, gx_ref, mod_ref, g_ref, dproj_ref, dx_ref, accb_ref, accg_ref, scr):
        _acc_init(accb_ref, accg_ref)
        c, s1, s2 = c_ref[...], s1_ref[...], s2_ref[...]
        tot = lambda r1, r4, r16: r1[...] + _perm_load(r4, scr, 4) + _perm_load(r16, scr, 16)
        dqb = tot(dq1_ref, dq4_ref, dq16_ref)
        dkb = tot(dk1_ref, dk4_ref, dk16_ref)
        dvb = tot(dv1_ref, dv4_ref, dv16_ref)
        dproj = jnp.concatenate([
            _rope_t(dqa_ref[...], c, s1, s2) * QSCALE, _rope_t(_per_kv_head(dka_ref[...]), c, s1, s2),
            _per_kv_head(dva_ref[...]),
            _rope_t(dqb, c, s1, s2) * QSCALE, _rope_t(dkb, c, s1, s2), dvb], axis=1).astype(bf16)
        dproj_ref[...] = dproj
        dh = _dot_nt(dproj, w_ref[...])
        xn, r = _rms(x_ref[...])
        g = g_ref[...]
        dn = dh * (1.0 + mod_ref[1:2, :])
        dx_ref[...] = gx_ref[...] + _rms_bwd(dn * g, xn, r)
        accb_ref[0:1, :] += _colsum(dh * (xn * g))
        accb_ref[1:2, :] += _colsum(dh)
        accg_ref[0:1, :] += _colsum(dn * xn)

    return pl.pallas_call(
        body, name="attn_in_bwd", grid=(BL, NJ),
        in_specs=[_tok(AQ), _tok(AQ), _tok(AQ)] + [_tok(BW)] * 3 + [_perm_spec(4, BW)] * 3 + [_perm_spec(16, BW)] * 3
                 + [_tok(LANES)] * 3 + [_full((D, INW)), _tok(D), _tok(D), MOD_SPEC, _full((1, D))],
        out_specs=[_tok(INW), _tok(D), ACCB_SPEC, ACCG_SPEC],
        out_shape=[jax.ShapeDtypeStruct((BL, SEQ, INW), bf16), jax.ShapeDtypeStruct((BL, SEQ, D), f32)] + ACC_SHAPES,
        scratch_shapes=[pltpu.VMEM((BW // LANES, TM, LANES), f32)],
        compiler_params=_cp(("arbitrary", "arbitrary")),
    )(dqa, dka, dva, *d1, *d4, *d16, tc, ts1, ts2, w_in, x, gx1, mod, g_pre)


def _inv_lane():
    inv = np.float32(THETA) ** (-np.arange(0, ROT, 2, dtype=np.float32) / np.float32(ROT))
    lane = np.arange(LANES) % HD
    return jnp.asarray(np.where(lane < ROT, inv[lane % (ROT // 2)], 0.0).astype(np.float32)[None, :])


def _local_step(x, positions, mod, target, inv_lane, first_weight, later_weights, grad_ready, g_attn_pre,
                g_attn_post, sink_a, g_mix_a, g_mix_b, g_mlp_pre, g_mlp_post):
    tabs = _rope_tables(positions.reshape(BL * SEQ, 1), inv_lane)
    w_in = first_weight(tuple(tabs))
    tc, ts1, ts2 = [t.reshape(BL, SEQ, LANES) for t in tabs]

    (h, qa, ka, va, q1, k1, v1, q4, k4, v4, q16, k16, v16, w_in) = _attn_in(x, mod, g_attn_pre, w_in, tc, ts1, ts2)
    seqs = lambda t: t.reshape(t.shape[0] * t.shape[1], t.shape[2], t.shape[3])
    q4, k4, v4, q16, k16, v16 = [seqs(t) for t in (q4, k4, v4, q16, k16, v16)]
    oa, la = _attn_fwd(qa, ka, va, sink_a, max_dist=BLK - 1, o_dtype=f32, name="attn_a_fwd")
    o1, l1 = _attn_fwd(q1, k1, v1, None, max_dist=BLK, o_dtype=bf16, name="attn_b1_fwd")
    o4, l4 = _attn_fwd(q4, k4, v4, None, max_dist=BLK, o_dtype=bf16, name="attn_b4_fwd")
    o16, l16 = _attn_fwd(q16, k16, v16, None, max_dist=BLK, o_dtype=bf16, name="attn_b16_fwd")
    b4 = lambda t: t.reshape(BL, 4, SEQ // 4, t.shape[-1])
    b16 = lambda t: t.reshape(BL, 16, SEQ // 16, t.shape[-1])
    w_out, mlp_weights, mod = later_weights((oa, o1, o4, o16), mod)
    x1, y, mixed, ob = _mix_out(oa, o1, l1, b4(o4), b4(l4), b16(o16), b16(l16), g_mix_a, g_mix_b, w_out, x, mod, g_attn_post)
    w_up, w_down = mlp_weights((x1,))
    h2, u, a = _mlp_up(x1, mod, g_mlp_pre, w_up)
    gx, dy2, accb_d, accg_d = _mlp_down(a, w_down, x1, target, mod, g_mlp_post)

    flat = lambda t: t.reshape(BL * SEQ, t.shape[-1])
    mod = grad_ready("w_down", _matmul_tn(flat(a), flat(dy2), tn=D, col_blocked=False, name="grad_w_down", out_dtype=bf16), mod)
    du, gx1, accb_m, accg_m = _mlp_bwd(dy2, u, w_down, w_up, x1, gx, mod, g_mlp_pre)
    mod = grad_ready("w_up", _matmul_tn(flat(h2), flat(du), tn=D, col_blocked=True, name="grad_w_up", out_dtype=bf16), mod)

    dy, doa, do1, do4, do16, da, dl1, dl4, dl16, accb_o, accg_o = _attn_out_bwd(
        gx1, y, mod, g_attn_post, w_out, oa, ob, g_mix_a, g_mix_b, l1, b4(l4), b16(l16))
    gw_out = _matmul_tn(flat(mixed), flat(dy), tn=D, col_blocked=False, name="grad_w_out", out_dtype=bf16)
    dqa, dka, dva, dsink = _attn_bwd(qa, ka, va, doa, da, la, sink_a, max_dist=BLK - 1, name="attn_a_bwd")
    d1 = _attn_bwd(q1, k1, v1, do1, dl1, l1, None, max_dist=BLK, name="attn_b1_bwd")
    d4 = _attn_bwd(q4, k4, v4, seqs(do4), seqs(dl4), l4, None, max_dist=BLK, name="attn_b4_bwd")
    d16 = _attn_bwd(q16, k16, v16, seqs(do16), seqs(dl16), l16, None, max_dist=BLK, name="attn_b16_bwd")
    dproj, grad_x, accb_i, accg_i = _attn_in_bwd(dqa, dka, dva, d1, [b4(t) for t in d4], [b16(t) for t in d16],
                                                 tc, ts1, ts2, w_in, x, gx1, mod, g_attn_pre)
    gw_in = _grad_w_in(flat(h), flat(dproj))
    dsink = grad_ready("w_in_w_out", (gw_in, gw_out), dsink)

    return grad_x, (accb_i, accb_o, accb_m, accb_d, accg_i, accg_o, accg_m, accg_d, dsink)


ADAW = NMOD * D // NCHIP


def _pos():
    return lax.axis_index("x"), lax.axis_index("y"), lax.axis_index("c")


def _flip(v, bit):
    return 1 - v if bit else v


def _all_peers(x, y, c):
    return [(_flip(x, k >> 2 & 1), _flip(y, k >> 1 & 1), _flip(c, k & 1)) for k in range(1, NDEV)]


def _other_chips(x, y):
    return [(1 - x, y), (x, 1 - y), (1 - x, 1 - y)]


def _rcopy(src, dst, send, recv, k, dev, k_recv=None):
    return pltpu.make_async_remote_copy(src_ref=src, dst_ref=dst, send_sem=send.at[k],
                                        recv_sem=recv.at[k if k_recv is None else k_recv],
                                        device_id=dev, device_id_type=MESH)


def _gather_small(src, buf, send, recv):
    x, y, c = _pos()
    me = 4 * x + 2 * y + c
    peers = _all_peers(x, y, c)
    sends = [_rcopy(src, buf.at[me], send, recv, k, p) for k, p in enumerate(peers)]
    for cp in sends:
        cp.start()
    for k, (px, py, pc) in enumerate(peers):
        _rcopy(src, buf.at[4 * px + 2 * py + pc], send, recv, k, (px, py, pc)).wait_recv()
    for cp in sends:
        cp.wait_send()
    return me


def _ada_fwd(c_in, w_ada, b_cols):
    def body(c_ref, w_hbm, b_ref, mod_ref, cond_ref, cbuf, mbuf, w_ref, s1, r1, s2, r2, wsem):
        x, y, c = _pos()
        chip = 2 * x + y
        wcopy = pltpu.make_async_copy(w_hbm, w_ref, wsem)
        wcopy.start()
        me = _gather_small(c_ref, cbuf, s1, r1)
        cbuf[me] = c_ref[...]
        for i in range(NDEV):
            cond_ref[BL * i:BL * (i + 1), :] = cbuf[i]
        call = cond_ref[...]
        cond = call / (1.0 + jnp.exp(-call))
        cond_ref[...] = cond
        wcopy.wait()
        mbuf[chip] = _dot(cond.astype(bf16), w_ref[...].astype(bf16)) + b_ref[...]
        chips = _other_chips(x, y)
        sends = [_rcopy(mbuf.at[chip], mbuf.at[chip], s2, r2, j, (px, py, c)) for j, (px, py) in enumerate(chips)]
        for cp in sends:
            cp.start()
        for j, (px, py) in enumerate(chips):
            _rcopy(mbuf.at[chip], mbuf.at[2 * px + py], s2, r2, j, (px, py, c)).wait_recv()
        for cp in sends:
            cp.wait_send()
        row = lax.broadcasted_iota(jnp.int32, (BL * NDEV, ADAW), 0)
        for s in range(NCHIP):
            slab = mbuf[s]
            for j in range(BL):
                mod_ref[j:j + 1, ADAW * s:ADAW * (s + 1)] = jnp.sum(jnp.where(row == BL * me + j, slab, 0.0), axis=0, keepdims=True)

    vm = pl.BlockSpec(memory_space=pltpu.VMEM)
    return pl.pallas_call(
        body, name="ada_fwd", in_specs=[vm, pl.BlockSpec(memory_space=pl.ANY), vm], out_specs=[vm, vm],
        out_shape=[jax.ShapeDtypeStruct((BL, NMOD * D), f32), jax.ShapeDtypeStruct((BL * NDEV, D), f32)],
        scratch_shapes=[pltpu.VMEM((NDEV, BL, D), f32), pltpu.VMEM((NCHIP, BL * NDEV, ADAW), f32),
                        pltpu.VMEM((D, ADAW), f32),
                        pltpu.SemaphoreType.DMA((NDEV - 1,)), pltpu.SemaphoreType.DMA((NDEV - 1,)),
                        pltpu.SemaphoreType.DMA((NCHIP - 1,)), pltpu.SemaphoreType.DMA((NCHIP - 1,)),
                        pltpu.SemaphoreType.DMA],
        compiler_params=pltpu.CompilerParams(vmem_limit_bytes=VMEM_LIMIT),
    )(c_in, w_ada, b_cols)


def _small_allreduce(accs, cond_all):
    def body(bi, bo, bm, bd, gi, go, gm, gd, dsink, cond_ref, gw_ref, gb_ref, small_ref, pay, pbuf, dall, s1, r1):
        x, y, c = _pos()
        chip = 2 * x + y
        pay[...] = jnp.zeros_like(pay)
        for b in range(BL):
            for k, (ref, r) in enumerate(((bi, 1), (bi, 0), (bo, 0), (bm, 1), (bm, 0), (bd, 0))):
                pay[b:b + 1, D * k:D * (k + 1)] = ref[b, r:r + 1, :]
        for off, ref, r in ((OFF_G_ATTN_PRE, gi, 0), (OFF_G_ATTN_POST, go, 0), (OFF_G_MIX_A, go, 1), (OFF_G_MLP_PRE, gm, 0),
                            (OFF_G_MLP_POST, gd, 0)):
            pay[BL:BL + 1, off:off + D] = ref[r:r + 1, :]
        eye = lax.broadcasted_iota(jnp.int32, (NHEAD, LANES), 0) == lax.broadcasted_iota(jnp.int32, (NHEAD, LANES), 1)
        pay[BL:BL + 1, OFF_SINK:OFF_SINK + LANES] = jnp.sum(jnp.where(eye, dsink[...], 0.0), axis=0, keepdims=True)
        pay[BL:BL + 1, OFF_LOSS:OFF_LOSS + LANES] = gd[1:2, 0:LANES]
        me = _gather_small(pay, pbuf, s1, r1)
        pbuf[me] = pay[...]
        small = pbuf[0, BL:BL + 1, :]
        for i in range(1, NDEV):
            small = small + pbuf[i, BL:BL + 1, :]
        small_ref[...] = small
        for i in range(NDEV):
            dall[BL * i:BL * (i + 1), :] = pbuf[i, 0:BL, :]
        gb_ref[...] = jnp.sum(dall[...], axis=0, keepdims=True)
        cols = jnp.zeros((BL * NDEV, ADAW), f32)
        for s in range(NCHIP):
            cols = cols + jnp.where(chip == s, dall[:, ADAW * s:ADAW * (s + 1)], 0.0)
        gw_ref[...] = _dot_tn(cond_ref[...].astype(bf16), cols.astype(bf16))

    vm = pl.BlockSpec(memory_space=pltpu.VMEM)
    return pl.pallas_call(
        body, name="small_allreduce", in_specs=[vm] * 10, out_specs=[vm] * 3,
        out_shape=[jax.ShapeDtypeStruct((D, ADAW), f32), jax.ShapeDtypeStruct((1, PAYW), f32), jax.ShapeDtypeStruct((1, PAYW), f32)],
        scratch_shapes=[pltpu.VMEM((4, PAYW), f32), pltpu.VMEM((NDEV, 4, PAYW), f32), pltpu.VMEM((BL * NDEV, PAYW), f32),
                        pltpu.SemaphoreType.DMA((NDEV - 1,)), pltpu.SemaphoreType.DMA((NDEV - 1,))],
        compiler_params=pltpu.CompilerParams(vmem_limit_bytes=VMEM_LIMIT),
    )(*accs, cond_all)


def _half(ref, c):
    r2 = ref.shape[0] // 2
    return ref.at[pl.ds(c * r2 if isinstance(c, int) else pl.multiple_of(c * r2, 16), r2), :]


HBM_SPEC = pl.BlockSpec(memory_space=pltpu.HBM)
SEM_SPEC = pl.BlockSpec(memory_space=pltpu.SEMAPHORE)
EFFECT = pltpu.SideEffectType.DATAFLOW_SIDE_EFFECTING
NLINK = NCHIP - 1


def _in_hbm(a):
    return pltpu.with_memory_space_constraint(a, pltpu.HBM)


NSEM = 8


def _split_start(name, srcs, land_shapes, builds, carry, after=(), lands=None):
    n = len(srcs)
    na, nc = len(after), len(carry)

    def body(*refs):
        src, land = refs[:n], refs[n:2 * n]
        kept = refs[2 * n + na:2 * n + na + nc]
        outs = refs[2 * n + na + nc:]
        send, recv, passed = outs[:n], outs[n:2 * n], outs[4 * n:]
        for t in range(n):
            for out_cp, _ in builds[t](src[t], land[t], send[t], recv[t]):
                out_cp.start()
        for a, b in zip(kept, passed):
            b[...] = a[...]

    if lands is None:
        lands = [lax.empty(s.shape, s.dtype) for s in land_shapes]
    lands = [_in_hbm(a) for a in lands]
    sems = [pltpu.SemaphoreType.DMA((NSEM,))] * (2 * n)
    thru = [pltpu.HBM(a.shape, a.dtype) for a in list(srcs) + lands]
    vm = pl.BlockSpec(memory_space=pltpu.VMEM)
    res = pl.pallas_call(
        body, name=name, out_shape=sems + thru + [jax.ShapeDtypeStruct(a.shape, a.dtype) for a in carry],
        in_specs=[HBM_SPEC] * (2 * n) + [pl.BlockSpec(memory_space=pl.ANY)] * na + [vm] * nc,
        out_specs=[SEM_SPEC] * (2 * n) + [HBM_SPEC] * (2 * n) + [vm] * nc,
        input_output_aliases={i: 2 * n + i for i in range(2 * n)},
        compiler_params=pltpu.CompilerParams(has_side_effects=EFFECT),
    )(*[_in_hbm(a) for a in srcs], *lands, *after, *carry)
    flight = [(res[2 * n + t], res[3 * n + t], res[t], res[n + t]) for t in range(n)]
    return flight, list(res[4 * n:])


def _split_wait(name, flight, builds, after):
    m = len(flight)
    na = len(after)

    def body(*refs):
        src, land, send, recv = refs[:m], refs[m:2 * m], refs[2 * m:3 * m], refs[3 * m:4 * m]
        for t in range(m):
            for out_cp, in_cp in builds[t](src[t], land[t], send[t], recv[t]):
                out_cp.wait_send()
                in_cp.wait_recv()

    ops = [f[0] for f in flight] + [f[1] for f in flight] + [f[2] for f in flight] + [f[3] for f in flight]
    res = pl.pallas_call(
        body, name=name, out_shape=[pltpu.HBM(a.shape, a.dtype) for a in ops[:2 * m]],
        in_specs=[HBM_SPEC] * (2 * m) + [SEM_SPEC] * (2 * m) + [pl.BlockSpec(memory_space=pl.ANY)] * na,
        out_specs=[HBM_SPEC] * (2 * m), input_output_aliases={i: i for i in range(2 * m)},
        compiler_params=pltpu.CompilerParams(has_side_effects=EFFECT),
    )(*ops, *after)
    return res[:m], res[m:2 * m]


def _weight_copies(src, land, send, recv):
    x, y, c = _pos()
    chip = 2 * x + y
    return [(_rcopy(_half(src, c), _half(land.at[chip], c), send, recv, j, (px, py, c)),
             _rcopy(_half(src, c), _half(land.at[2 * px + py], c), send, recv, j, (px, py, c)))
            for j, (px, py) in enumerate(_other_chips(x, y))]


NDIRECT = NDEV - 1


def _direct_grad_copies(src, land, send, recv):
    x, y, c = _pos()
    out, arrive = [], []
    for j, (px, py) in enumerate(_other_chips(x, y)):
        for hc in range(2):
            out.append(_rcopy(_half(src.at[2 * px + py], hc), land.at[2 * j + c], send, recv, 2 * j + hc, (px, py, hc),
                              k_recv=2 * j + c))
            arrive.append(_rcopy(_half(src.at[2 * px + py], hc), land.at[2 * j + hc], send, recv, 2 * j + hc, (px, py, hc)))
    own = _rcopy(_half(src.at[2 * x + y], 1 - c), land.at[NDIRECT - 1], send, recv, NDIRECT - 1, (x, y, 1 - c))
    return list(zip(out, arrive)) + [(own, own)]


def _pair_weight_copies(src, land, send, recv):
    x, y, c = _pos()
    sib = (x, y, 1 - c)
    cps = []
    for j, (px, py) in enumerate(_other_chips(x, y)):
        mine, theirs = _half(land.at[2 * px + py], c), _half(land.at[2 * px + py], 1 - c)
        cps.append((_rcopy(mine, mine, send, recv, j, sib), _rcopy(theirs, theirs, send, recv, j, sib)))
    own = _rcopy(src, land.at[2 * x + y], send, recv, NLINK, sib)
    return cps + [(own, own)]


RS_ROWS = 128


def _chip_add(own, landed, pos_arr, name):
    nl, r2, cw = landed.shape
    nr = r2 // RS_ROWS

    def body(s_ref, h_ref, q_ref, o_ref):
        acc = h_ref[...].astype(f32)
        for j in range(nl):
            acc = acc + q_ref[j].astype(f32)
        o_ref[...] = acc

    gs = pltpu.PrefetchScalarGridSpec(
        num_scalar_prefetch=1, grid=(nr,),
        in_specs=[pl.BlockSpec((None, RS_ROWS, cw), lambda j, s: (s[0], s[1] * nr + j, 0)),
                  pl.BlockSpec((nl, RS_ROWS, cw), lambda j, s: (0, j, 0))],
        out_specs=pl.BlockSpec((RS_ROWS, cw), lambda j, s: (s[1] * nr + j, 0)))
    return pl.pallas_call(body, name=name, grid_spec=gs, out_shape=jax.ShapeDtypeStruct((2 * r2, cw), f32),
                          compiler_params=_cp(("arbitrary",)))(pos_arr, own, landed)


def _pair_gather_copies(src, land, send, recv):
    x, y, c = _pos()
    sib = (x, y, 1 - c)
    return [(_rcopy(_half(land, c), _half(land, c), send, recv, 0, sib),
             _rcopy(_half(land, 1 - c), _half(land, 1 - c), send, recv, 0, sib))]


def _adamw_math(w, g, m, v):
    m = B1 * m + (1.0 - B1) * g
    v = B2 * v + (1.0 - B2) * jnp.square(g)
    m_hat = m / (1.0 - B1 ** STEP)
    v_hat = v / (1.0 - B2 ** STEP)
    return -LR * (m_hat / (jnp.sqrt(v_hat) + AEPS) + WD * w), m, v


ADAM_ROWS = 256


def _adamw(w, g, m, v, name):
    r, cw = w.shape

    def body(w_ref, g_ref, m_ref, v_ref, go_ref, d_ref, mo_ref, vo_ref):
        g = g_ref[...]
        go_ref[...] = g
        d_ref[...], mo_ref[...], vo_ref[...] = _adamw_math(w_ref[...], g, m_ref[...], v_ref[...])

    rows = max(k for k in range(8, ADAM_ROWS + 1, 8) if r % k == 0)
    spec = pl.BlockSpec((rows, cw), lambda i: (i, 0))
    return pl.pallas_call(body, name=name, grid=(r // rows,), in_specs=[spec] * 4, out_specs=[spec] * 4,
                          out_shape=[jax.ShapeDtypeStruct((r, cw), f32)] * 4, compiler_params=_cp(("arbitrary",)))(w, g, m, v)


SMALL = (("b_ada", None, PAYW), ("g_attn_pre", OFF_G_ATTN_PRE, D), ("g_attn_post", OFF_G_ATTN_POST, D), ("sink_a", OFF_SINK, 8),
         ("g_mix_a", OFF_G_MIX_A, AQ), ("g_mix_b", OFF_G_MIX_B, BW), ("g_mlp_pre", OFF_G_MLP_PRE, D), ("g_mlp_post", OFF_G_MLP_POST, D))


def _adamw_small(small, gb, params):
    n = len(SMALL)

    def body(*refs):
        small_ref, gb_ref = refs[:2]
        wmv = refs[2:2 + 3 * n]
        loss_ref = refs[2 + 3 * n]
        outs = refs[3 + 3 * n:]
        loss_ref[...] = small_ref[:, OFF_LOSS:OFF_LOSS + 1] * (0.5 / D)
        for i, (_, off, width) in enumerate(SMALL):
            g = gb_ref[...] if off is None else small_ref[:, off:off + width]
            w_ref, m_ref, v_ref = wmv[3 * i:3 * i + 3]
            outs[4 * i][...] = g
            outs[4 * i + 1][...], outs[4 * i + 2][...], outs[4 * i + 3][...] = _adamw_math(w_ref[...], g, m_ref[...], v_ref[...])

    vm = pl.BlockSpec(memory_space=pltpu.VMEM)
    out_shape = [jax.ShapeDtypeStruct((1, 1), f32)]
    for _, _, width in SMALL:
        out_shape += [jax.ShapeDtypeStruct((1, width), f32)] * 4
    flat = [a for wmv in params for a in wmv]
    res = pl.pallas_call(body, name="adamw_small", in_specs=[vm] * (2 + 3 * n), out_specs=[vm] * len(out_shape),
                         out_shape=out_shape)(small, gb, *flat)
    return res[0], {name: res[1 + 4 * i:5 + 4 * i] for i, (name, _, _) in enumerate(SMALL)}


def kernel(x, c, positions, w_ada, b_ada, g_attn_pre, g_attn_post, w_in, sink_a, g_mix_a, g_mix_b, w_out, g_mlp_pre, g_mlp_post, w_up, w_down, loss_target, m_w_ada, m_b_ada, m_g_attn_pre, m_g_attn_post, m_w_in, m_sink_a, m_g_mix_a, m_g_mix_b, m_w_out, m_g_mlp_pre, m_g_mlp_post, m_w_up, m_w_down, v_w_ada, v_b_ada, v_g_attn_pre, v_g_attn_post, v_w_in, v_sink_a, v_g_mix_a, v_g_mix_b, v_w_out, v_g_mlp_pre, v_g_mlp_post, v_w_up, v_w_down):
    given = dict(w_ada=w_ada, b_ada=b_ada, g_attn_pre=g_attn_pre, g_attn_post=g_attn_post, w_in=w_in, sink_a=sink_a, g_mix_a=g_mix_a,
                 g_mix_b=g_mix_b, w_out=w_out, g_mlp_pre=g_mlp_pre, g_mlp_post=g_mlp_post, w_up=w_up, w_down=w_down)
    moms = dict(w_ada=(m_w_ada, v_w_ada), b_ada=(m_b_ada, v_b_ada), g_attn_pre=(m_g_attn_pre, v_g_attn_pre),
                g_attn_post=(m_g_attn_post, v_g_attn_post), w_in=(m_w_in, v_w_in), sink_a=(m_sink_a, v_sink_a),
                g_mix_a=(m_g_mix_a, v_g_mix_a), g_mix_b=(m_g_mix_b, v_g_mix_b), w_out=(m_w_out, v_w_out),
                g_mlp_pre=(m_g_mlp_pre, v_g_mlp_pre), g_mlp_post=(m_g_mlp_post, v_g_mlp_post), w_up=(m_w_up, v_w_up),
                w_down=(m_w_down, v_w_down))
    order = ["w_ada", "b_ada", "g_attn_pre", "g_attn_post", "w_in", "sink_a", "g_mix_a", "g_mix_b", "w_out", "g_mlp_pre",
             "g_mlp_post", "w_up", "w_down"]
    xi, yi, ci = _pos()
    chip = 2 * xi + yi

    pos_arr = jnp.stack([chip, ci]).astype(jnp.int32)
    big = ("w_in", "w_out", "w_up", "w_down")

    b_cols = lax.dynamic_slice(b_ada, (0, chip * ADAW), (1, ADAW))
    mod, cond_all = _ada_fwd(c, w_ada[0], b_cols)
    gathered = [jax.ShapeDtypeStruct((NCHIP,) + given[n].shape[1:], bf16) for n in big]
    flight_in, (mod,) = _split_start("weights_start_first", [w_in[0].astype(bf16)], gathered[:1], [_weight_copies], [mod])
    mod, rest = lax.optimization_barrier((mod, [given[n][0] for n in big[1:]]))
    flight_rest, (mod, inv_lane) = _split_start("weights_start_rest", [w.astype(bf16) for w in rest], gathered[1:],
                                                [_weight_copies] * 3, [mod, _inv_lane()])
    mod = mod.reshape(BL, NMOD, D)

    def first_weight(after):
        srcs, lands = _split_wait("weights_wait_first", flight_in, [_weight_copies], after)
        cross, _ = _split_start("weights_pair_start_first", srcs, None, [_pair_weight_copies], [], lands=lands)
        _, (win_g,) = _split_wait("weights_pair_wait_first", cross, [_pair_weight_copies], ())
        return win_g

    def later_weights(after, carry):
        srcs, lands = _split_wait("weights_wait_rest", flight_rest, [_weight_copies] * 3, after)
        fl, (carry,) = _split_start("weights_pair_start_rest", srcs, None, [_pair_weight_copies] * 3, [carry], lands=lands)
        _, (wout_g,) = _split_wait("weights_pair_wait_out", fl[:1], [_pair_weight_copies], ())

        def mlp_weights(after):
            _, (wup_g, wdn_g) = _split_wait("weights_pair_wait_mlp", fl[1:], [_pair_weight_copies] * 2, after)
            return wup_g, wdn_g.reshape(DFF, D)

        return wout_g.reshape(D, D), mlp_weights, carry

    waiting, pending = {}, {}

    def send_grads(group, names, slabs, carry):
        lands = [jax.ShapeDtypeStruct((NDIRECT, s.shape[1] // 2, s.shape[2]), bf16) for s in slabs]
        fl, (carry,) = _split_start("grad_start_" + group, slabs, lands, [_direct_grad_copies] * len(names), [carry])
        pending[group] = (names, fl)
        return carry

    def grad_ready(group, g, carry):
        if group == "w_in_w_out":
            waiting[group] = (("w_in", "w_out"), [g[0], g[1].reshape(NCHIP, D // NCHIP, D)])
            return carry
        return send_grads(group, (group,), [g.reshape(NCHIP, DFF // NCHIP, D) if group == "w_down" else g], carry)

    grad_x, accs = _local_step(x, positions, mod, loss_target, inv_lane, first_weight, later_weights, grad_ready,
                               g_attn_pre, g_attn_post, sink_a, g_mix_a, g_mix_b, g_mlp_pre, g_mlp_post)

    grads, out = {}, {}

    def update(n):
        tr = (lambda a: a.T) if n == "w_in" else (lambda a: a)
        res = _adamw(tr(given[n][0]), tr(grads[n]), tr(moms[n][0][0]), tr(moms[n][1][0]), "adamw_" + n)
        out[n] = tuple(tr(a)[None] for a in res)
        return res[3]

    def finish(groups, after):
        names = sum((pending[g][0] for g in groups), ())
        fl = sum((pending[g][1] for g in groups), [])
        halves, landed = _split_wait("grad_wait_" + groups[0], fl, [_direct_grad_copies] * len(names), after)
        flights = []
        for h, q, n in zip(halves, landed, names):
            full = _chip_add(h, q, pos_arr, "grad_chip_sum_" + n)
            flights.append(_split_start("grad_gather_start_" + n, [jnp.zeros((8, LANES), f32)], None, [_pair_gather_copies],
                                        [], lands=[full])[0])
        last = None
        for n, fl1 in zip(names, flights):
            after = (flights[-1][0][0],) if last is None and fl1 is not flights[-1] else () if last is None else (last,)
            _, (grads[n],) = _split_wait("grad_gather_wait_" + n, fl1, [_pair_gather_copies], after)
            last = update(n)
        return last

    grads["w_ada"], gb, small = _small_allreduce(accs, cond_all)
    small = send_grads("w_in_w_out", *waiting["w_in_w_out"], small)
    last = finish(("w_down", "w_up"), (small,))
    finish(("w_in_w_out",), (last, update("w_ada")))
    loss, res = _adamw_small(small, gb, [(given[n], moms[n][0], moms[n][1]) for n, _, _ in SMALL])
    for n, _, _ in SMALL:
        out[n] = tuple(res[n])
    return (loss.reshape(()), grad_x, *[out[n][0] for n in order], *[out[n][1] for n in order],
            *[out[n][2] for n in order], *[out[n][3] for n in order])
```

```python
import numpy as np
import jax
import jax.numpy as jnp
from jax import lax
from jax.experimental import pallas as pl
from jax.experimental.pallas import tpu as pltpu

f32 = jnp.float32
bf16 = jnp.bfloat16
MESH = pl.DeviceIdType.MESH

D = 1024
SEQ = 2048
BL = 2
HD = 64
AQ = 512
AKV = 128
BW = 512
INW = 2304
DFF = 4096
NMOD = 6
ROT = 16
THETA = 500000.0
EPS = 1e-6
NEG = -1e30
BLK = 128
TM = 512
NJ = SEQ // TM
LANES = 128
SUBLANES = 8
NHEAD = AQ // HD
QSCALE = HD ** -0.5
NCHIP = 4
NDEV = 8
VMEM_LIMIT = 56 << 20

LR, B1, B2, AEPS, WD, STEP = 0.001, 0.9, 0.999, 1e-08, 0.01, 10

OFF_G_ATTN_PRE, OFF_G_ATTN_POST, OFF_G_MIX_A, OFF_G_MIX_B = 0, 1024, 2048, 2560
OFF_G_MLP_PRE, OFF_G_MLP_POST, OFF_SINK, OFF_LOSS = 3072, 4096, 5120, 5248
PAYW = NMOD * D


def _cp(sem=None):
    return pltpu.CompilerParams(dimension_semantics=sem, vmem_limit_bytes=VMEM_LIMIT)


def _dot(a, b):
    return jnp.dot(a, b, preferred_element_type=f32)


def _dot_nt(a, b):
    return lax.dot_general(a, b, (((1,), (1,)), ((), ())), preferred_element_type=f32)


def _dot_tn(a, b):
    return lax.dot_general(a, b, (((0,), (0,)), ((), ())), preferred_element_type=f32)


def _rms(x):
    r = lax.rsqrt(jnp.mean(x * x, axis=-1, keepdims=True) + EPS)
    return x * r, r


def _rms_bwd(dy, y, r):
    return r * (dy - y * jnp.mean(dy * y, axis=-1, keepdims=True))


def _colsum(v):
    return jnp.sum(v, axis=0, keepdims=True)


def _rope(p, c, s1, s2):
    outs = []
    for c0 in range(0, p.shape[1], LANES):
        pc = p[:, c0:c0 + LANES]
        outs.append(pc * c + pltpu.roll(pc, LANES - ROT // 2, 1) * s1 + pltpu.roll(pc, ROT // 2, 1) * s2)
    return outs[0] if len(outs) == 1 else jnp.concatenate(outs, axis=1)


def _rope_t(g, c, s1, s2):
    outs = []
    for c0 in range(0, g.shape[1], LANES):
        gc = g[:, c0:c0 + LANES]
        outs.append(gc * c + pltpu.roll(gc * s1, ROT // 2, 1) + pltpu.roll(gc * s2, LANES - ROT // 2, 1))
    return outs[0] if len(outs) == 1 else jnp.concatenate(outs, axis=1)


def _perm_store(val, scr, out_ref, d):
    nc = val.shape[1] // LANES
    for c in range(nc):
        scr[c] = val[:, LANES * c:LANES * (c + 1)]
    for c in range(nc):
        for r in range(d):
            out_ref[r, :, LANES * c:LANES * (c + 1)] = scr[c, pl.ds(r, TM // d, stride=d), :].astype(out_ref.dtype)


def _perm_load(in_ref, scr, d):
    nc = in_ref.shape[-1] // LANES
    for c in range(nc):
        for r in range(d):
            scr[c, pl.ds(r, TM // d, stride=d), :] = in_ref[r, :, LANES * c:LANES * (c + 1)].astype(f32)
    return jnp.concatenate([scr[c] for c in range(nc)], axis=1)


def _per_query_head(kv):
    r = pltpu.roll(kv, HD, 1)
    lo = lax.broadcasted_iota(jnp.int32, kv.shape, 1) < HD
    return jnp.concatenate([jnp.where(lo, kv, r), jnp.where(lo, r, kv)], axis=1)


def _per_kv_head(g):
    g0, g1 = g[:, :LANES] + g[:, LANES:2 * LANES], g[:, 2 * LANES:3 * LANES] + g[:, 3 * LANES:]
    lo = lax.broadcasted_iota(jnp.int32, g0.shape, 1) < HD
    return jnp.where(lo, g0 + pltpu.roll(g0, HD, 1), g1 + pltpu.roll(g1, HD, 1))


def _tok(w):
    return pl.BlockSpec((None, TM, w), lambda b, j: (b, j, 0))


def _perm_spec(d, w):
    return pl.BlockSpec((None, d, TM // d, w), lambda b, j: (b, 0, j, 0))


def _full(shape):
    n = len(shape)
    return pl.BlockSpec(shape, lambda b, j: (0,) * n)


MOD_SPEC = pl.BlockSpec((None, NMOD, D), lambda b, j: (b, 0, 0))
ACCB_SPEC = pl.BlockSpec((None, SUBLANES, D), lambda b, j: (b, 0, 0))
ACCG_SPEC = pl.BlockSpec((SUBLANES, D), lambda b, j: (0, 0))
ACC_SHAPES = [jax.ShapeDtypeStruct((BL, SUBLANES, D), f32), jax.ShapeDtypeStruct((SUBLANES, D), f32)]


def _acc_init(accb_ref, accg_ref):
    b, j = pl.program_id(0), pl.program_id(1)

    @pl.when(j == 0)
    def _():
        accb_ref[...] = jnp.zeros_like(accb_ref)

    @pl.when((b == 0) & (j == 0))
    def _():
        accg_ref[...] = jnp.zeros_like(accg_ref)


def _rope_tables(pos_col, inv_lane):
    def body(p_ref, inv_ref, c_ref, s1_ref, s2_ref):
        ang = p_ref[...].astype(f32) * inv_ref[...]
        j = lax.broadcasted_iota(jnp.int32, (TM, LANES), 1) % HD
        cs, sn = jnp.cos(ang), jnp.sin(ang)
        c_ref[...] = jnp.where(j < ROT, cs, 1.0)
        s1_ref[...] = jnp.where(j < ROT // 2, -sn, 0.0)
        s2_ref[...] = jnp.where((j >= ROT // 2) & (j < ROT), sn, 0.0)

    n = BL * SEQ // TM
    return pl.pallas_call(
        body, name="rope_tables", grid=(n,),
        in_specs=[pl.BlockSpec((TM, 1), lambda i: (i, 0)), pl.BlockSpec((1, LANES), lambda i: (0, 0))],
        out_specs=[pl.BlockSpec((TM, LANES), lambda i: (i, 0))] * 3,
        out_shape=[jax.ShapeDtypeStruct((BL * SEQ, LANES), f32)] * 3,
    )(pos_col, inv_lane)


def _attn_in(x, mod, g_pre, w_in, tc, ts1, ts2):
    def body(x_ref, mod_ref, g_ref, wg_ref, c_ref, s1_ref, s2_ref,
             h_ref, qa_ref, ka_ref, va_ref, q1_ref, k1_ref, v1_ref, q4_ref, k4_ref, v4_ref, q16_ref, k16_ref, v16_ref,
             w_ref, scr):
        @pl.when((pl.program_id(0) == 0) & (pl.program_id(1) == 0))
        def _():
            w_ref[...] = jnp.concatenate([wg_ref[s] for s in range(NCHIP)], axis=1)

        xn, _ = _rms(x_ref[...])
        h = (xn * g_ref[...]) * (1.0 + mod_ref[1:2, :]) + mod_ref[0:1, :]
        hb = h.astype(bf16)
        h_ref[...] = hb
        proj = _dot(hb, w_ref[...])
        c, s1, s2 = c_ref[...], s1_ref[...], s2_ref[...]
        o1, o2, o3, o4, o5 = AQ, AQ + AKV, AQ + 2 * AKV, AQ + 2 * AKV + BW, AQ + 2 * AKV + 2 * BW
        qa_ref[...] = (_rope(proj[:, :o1], c, s1, s2) * QSCALE).astype(bf16)
        ka_ref[...] = _per_query_head(_rope(proj[:, o1:o2], c, s1, s2)).astype(bf16)
        va_ref[...] = _per_query_head(proj[:, o2:o3]).astype(bf16)
        qb = _rope(proj[:, o3:o4], c, s1, s2) * QSCALE
        kb = _rope(proj[:, o4:o5], c, s1, s2)
        vb = proj[:, o5:]
        for val, r1, r4, r16 in ((qb, q1_ref, q4_ref, q16_ref), (kb, k1_ref, k4_ref, k16_ref), (vb, v1_ref, v4_ref, v16_ref)):
            r1[...] = val.astype(bf16)
            _perm_store(val, scr, r4, 4)
            _perm_store(val, scr, r16, 16)

    nat = lambda w: jax.ShapeDtypeStruct((BL, SEQ, w), bf16)
    p4 = jax.ShapeDtypeStruct((BL, 4, SEQ // 4, BW), bf16)
    p16 = jax.ShapeDtypeStruct((BL, 16, SEQ // 16, BW), bf16)
    return pl.pallas_call(
        body, name="attn_in", grid=(BL, NJ),
        in_specs=[_tok(D), MOD_SPEC, _full((1, D)), _full((NCHIP, D, INW // NCHIP)), _tok(LANES), _tok(LANES), _tok(LANES)],
        out_specs=([_tok(D), _tok(AQ), _tok(2 * AKV), _tok(2 * AKV)] + [_tok(BW)] * 3 + [_perm_spec(4, BW)] * 3 + [_perm_spec(16, BW)] * 3
                   + [_full((D, INW))]),
        out_shape=[nat(D), nat(AQ), nat(2 * AKV), nat(2 * AKV)] + [nat(BW)] * 3 + [p4] * 3 + [p16] * 3
                  + [jax.ShapeDtypeStruct((D, INW), bf16)],
        scratch_shapes=[pltpu.VMEM((BW // LANES, TM, LANES), f32)],
        compiler_params=_cp(("arbitrary", "arbitrary")),
    )(x, mod, g_pre, w_in, tc, ts1, ts2)


def _kv_cat(cur_ref, prev_ref, p, cache):
    key = (id(cur_ref), p)
    if key not in cache:
        sl = slice(LANES * p, LANES * (p + 1))
        cache[key] = cur_ref[:, sl] if prev_ref is None else jnp.concatenate([prev_ref[:, sl], cur_ref[:, sl]], axis=0)
    return cache[key]


def _lane_half(a, hh):
    lo = lax.broadcasted_iota(jnp.int32, a.shape, 1) < HD
    return jnp.where(lo, a, jnp.zeros_like(a)) if hh == 0 else jnp.where(lo, jnp.zeros_like(a), a)


ATT_UNITS = 4


def _att_units(nb):
    return ATT_UNITS if nb == 1 else min(ATT_UNITS, nb)


def _attn_specs(n, nb, descending):
    u = _att_units(nb)
    if nb == 1:
        return (lambda ww: pl.BlockSpec((u, BLK, ww), lambda a, i: (a, 0, 0))), None, (n // u, 1)
    steps = nb // u
    at = (lambda i: steps - 1 - i) if descending else (lambda i: i)
    cur = lambda ww: pl.BlockSpec((None, u * BLK, ww), lambda a, i: (a, at(i), 0))
    prev = lambda ww: pl.BlockSpec((None, BLK, ww), lambda a, i: (a, jnp.maximum(u * at(i) - 1, 0), 0))
    return cur, prev, (n, steps)


def _attn_fwd(q, k, v, sink, *, max_dist, o_dtype, name):
    n, l, w = q.shape
    wk = k.shape[-1]
    nb = l // BLK
    has_sink = sink is not None

    def body(*refs):
        sink_ref = None
        if has_sink:
            sink_ref, refs = refs[0], refs[1:]
        if nb > 1:
            q_ref, kc_ref, kp_ref, vc_ref, vp_ref, o_ref, lse_ref = refs[:7]
            first = pl.program_id(1) == 0
            for u in range(_att_units(nb)):
                rows, before = pl.ds(BLK * u, BLK), pl.ds(BLK * (u - 1), BLK)
                unit(q_ref.at[rows, :], kc_ref.at[rows, :], kp_ref if u == 0 else kc_ref.at[before, :],
                     vc_ref.at[rows, :], vp_ref if u == 0 else vc_ref.at[before, :], o_ref.at[rows, :], lse_ref.at[rows, :],
                     jnp.logical_not(first) if u == 0 else True, sink_ref, *refs[7:])
        else:
            q_ref, kc_ref, vc_ref, o_ref, lse_ref = refs[:5]
            for u in range(_att_units(nb)):
                unit(q_ref.at[u], kc_ref.at[u], None, vc_ref.at[u], None, o_ref.at[u], lse_ref.at[u], None, sink_ref, *refs[5:])

    def unit(q_ref, kc_ref, kp_ref, vc_ref, vp_ref, o_ref, lse_ref, has_prev, sink_ref, sscr, pscr, dscr):
        qi = lax.broadcasted_iota(jnp.int32, (BLK, BLK), 0)
        kj = lax.broadcasted_iota(jnp.int32, (BLK, BLK), 1)
        tri = kj <= qi
        eye = kj == qi
        cache = {}
        for p in range(w // LANES):
            qpair = q_ref[:, LANES * p:LANES * (p + 1)]
            kcat = _kv_cat(kc_ref, kp_ref, p // share, cache)
            for hh in range(2):
                s = _dot_nt(_lane_half(qpair, hh), kcat)
                if nb > 1:
                    sp = s[:, :BLK] if has_prev is True else jnp.where(has_prev, s[:, :BLK], NEG)
                    sscr[2 * p + hh] = jnp.where(tri, s[:, BLK:], sp)
                    if diag:
                        dscr[2 * p + hh] = jnp.where(eye, sp, NEG)
                else:
                    sscr[2 * p + hh] = jnp.where(tri, s, NEG)
        lane = lax.broadcasted_iota(jnp.int32, (BLK, LANES), 1)
        lse_all = jnp.zeros((BLK, LANES), f32)
        for p in range(w // LANES):
            for hh in range(2):
                h = 2 * p + hh
                comb = sscr[h]
                if diag:
                    dtile = dscr[h]
                    m = jnp.max(jnp.maximum(comb, dtile), axis=-1, keepdims=True)
                else:
                    m = jnp.max(comb, axis=-1, keepdims=True)
                if has_sink:
                    sk = sink_ref[0, h]
                    m = jnp.maximum(m, sk)
                e = jnp.exp(comb - m)
                if diag:
                    ed = jnp.exp(dtile - m)
                    den = jnp.sum(e + ed, axis=-1, keepdims=True)
                else:
                    den = jnp.sum(e, axis=-1, keepdims=True)
                if has_sink:
                    den = den + jnp.exp(sk - m)
                inv = 1.0 / den
                if nb > 1:
                    pscr[h, :, :BLK] = (jnp.where(tri, ed if diag else 0.0, e) * inv).astype(bf16)
                    pscr[h, :, BLK:] = (jnp.where(tri, e, 0.0) * inv).astype(bf16)
                else:
                    pscr[h] = (e * inv).astype(bf16)
                lse_all = jnp.where(lane == h, jnp.broadcast_to(m + jnp.log(den), (BLK, LANES)), lse_all)
        lse_ref[...] = lse_all
        for p in range(w // LANES):
            vcat = _kv_cat(vc_ref, vp_ref, p // share, cache)
            o_ref[:, LANES * p:LANES * (p + 1)] = (_dot(pscr[2 * p], _lane_half(vcat, 0))
                                                   + _dot(pscr[2 * p + 1], _lane_half(vcat, 1))).astype(o_ref.dtype)

    assert max_dist in (BLK - 1, BLK) and w % wk == 0
    share = w // wk
    diag = nb > 1 and max_dist == BLK
    cur, prev, grid = _attn_specs(n, nb, False)
    in_specs = [cur(w), cur(wk)] + ([prev(wk)] if nb > 1 else []) + [cur(wk)] + ([prev(wk)] if nb > 1 else [])
    args = [q, k] + ([k] if nb > 1 else []) + [v] + ([v] if nb > 1 else [])
    if has_sink:
        in_specs = [pl.BlockSpec(memory_space=pltpu.SMEM)] + in_specs
        args = [sink] + args
    return pl.pallas_call(
        body, name=name, grid=grid, in_specs=in_specs,
        out_specs=[cur(w), cur(LANES)],
        out_shape=[jax.ShapeDtypeStruct((n, l, w), o_dtype), jax.ShapeDtypeStruct((n, l, LANES), f32)],
        scratch_shapes=[pltpu.VMEM((w // HD, BLK, BLK), f32), pltpu.VMEM((w // HD, BLK, 2 * BLK if nb > 1 else BLK), bf16),
                        pltpu.VMEM((w // HD if diag else 1, BLK, BLK), f32)],
        compiler_params=_cp(("arbitrary", "arbitrary")),
    )(*args)


def _attn_bwd(q, k, v, do, delta, lse, sink, *, max_dist, name):
    n, l, w = q.shape
    wk = k.shape[-1]
    nb = l // BLK
    has_sink = sink is not None

    def body(*refs):
        sink_ref = dsink_ref = ck = cv = None
        if has_sink:
            sink_ref, refs = refs[0], refs[1:]
        nin = 8 if nb > 1 else 6
        ins, rest = refs[:nin], refs[nin:]
        if has_sink:
            dq_ref, dk_ref, dv_ref, dsink_ref = rest[:4]
            rest = rest[4:]
        else:
            dq_ref, dk_ref, dv_ref = rest[:3]
            rest = rest[3:]
        step = pl.program_id(1)
        if has_sink:
            @pl.when((pl.program_id(0) == 0) & (step == 0))
            def _():
                dsink_ref[...] = jnp.zeros_like(dsink_ref)

        if nb > 1:
            q_ref, kc_ref, kp_ref, vc_ref, vp_ref, do_ref, delta_ref, lse_ref = ins
            ck, cv = rest[:2]

            @pl.when(step == 0)
            def _():
                ck[...] = jnp.zeros_like(ck)
                cv[...] = jnp.zeros_like(cv)

            last = step == nb // _att_units(nb) - 1
            for u in reversed(range(_att_units(nb))):
                rows, before = pl.ds(BLK * u, BLK), pl.ds(BLK * (u - 1), BLK)
                unit(q_ref.at[rows, :], kc_ref.at[rows, :], kp_ref if u == 0 else kc_ref.at[before, :],
                     vc_ref.at[rows, :], vp_ref if u == 0 else vc_ref.at[before, :], do_ref.at[rows, :],
                     delta_ref.at[rows, :], lse_ref.at[rows, :], dq_ref.at[rows, :], dk_ref.at[rows, :], dv_ref.at[rows, :],
                     jnp.logical_not(last) if u == 0 else True, sink_ref, dsink_ref, ck, cv, *rest[2:])
        else:
            q_ref, kc_ref, vc_ref, do_ref, delta_ref, lse_ref = ins
            for u in range(_att_units(nb)):
                unit(q_ref.at[u], kc_ref.at[u], None, vc_ref.at[u], None, do_ref.at[u], delta_ref.at[u], lse_ref.at[u],
                     dq_ref.at[u], dk_ref.at[u], dv_ref.at[u], None, sink_ref, dsink_ref, None, None, *rest)

    def unit(q_ref, kc_ref, kp_ref, vc_ref, vp_ref, do_ref, delta_ref, lse_ref, dq_ref, dk_ref, dv_ref, has_prev,
             sink_ref, dsink_ref, ck, cv, sscr, dpscr, pscr, dsscr, dscr=None, ddscr=None):
        lane = lax.broadcasted_iota(jnp.int32, (BLK, LANES), 1)
        qi = lax.broadcasted_iota(jnp.int32, (BLK, BLK), 0)
        kj = lax.broadcasted_iota(jnp.int32, (BLK, BLK), 1)
        tri = kj <= qi
        eye = kj == qi
        cache = {}
        kp, vp = kp_ref, vp_ref
        for p in range(w // LANES):
            sl = slice(LANES * p, LANES * (p + 1))
            qpair, dopair = q_ref[:, sl], do_ref[:, sl]
            kcat, vcat = _kv_cat(kc_ref, kp, p // share, cache), _kv_cat(vc_ref, vp, p // share, cache)
            for hh in range(2):
                h = 2 * p + hh
                s = _dot_nt(_lane_half(qpair, hh), kcat)
                dp = _dot_nt(_lane_half(dopair, hh), vcat)
                if nb > 1:
                    sp = s[:, :BLK] if has_prev is True else jnp.where(has_prev, s[:, :BLK], NEG)
                    sscr[h] = jnp.where(tri, s[:, BLK:], sp)
                    dpscr[h] = jnp.where(tri, dp[:, BLK:], dp[:, :BLK])
                    if diag:
                        dscr[h] = jnp.where(eye, sp, NEG)
                        ddscr[h] = dp[:, :BLK]
                else:
                    sscr[h] = jnp.where(tri, s, NEG)
                    dpscr[h] = dp
        for p in range(w // LANES):
            for hh in range(2):
                h = 2 * p + hh
                lse_b = jnp.broadcast_to(lse_ref[:, h:h + 1], (BLK, BLK))
                delta = jnp.broadcast_to(delta_ref[:, h:h + 1], (BLK, BLK))
                pr = jnp.exp(sscr[h] - lse_b)
                ds = pr * (dpscr[h] - delta)
                if nb > 1:
                    if diag:
                        prd = jnp.exp(dscr[h] - lse_b)
                        dsd = prd * (ddscr[h] - delta)
                    else:
                        prd = dsd = 0.0
                    pscr[h, :, :BLK] = jnp.where(tri, prd, pr).astype(bf16)
                    pscr[h, :, BLK:] = jnp.where(tri, pr, 0.0).astype(bf16)
                    dsscr[h, :, :BLK] = jnp.where(tri, dsd, ds).astype(bf16)
                    dsscr[h, :, BLK:] = jnp.where(tri, ds, 0.0).astype(bf16)
                else:
                    pscr[h] = pr.astype(bf16)
                    dsscr[h] = ds.astype(bf16)
                if has_sink:
                    dsk = -jnp.sum(jnp.where(lane == 0, jnp.exp(sink_ref[0, h] - lse_b) * delta, 0.0), keepdims=True)
                    dsink_ref[h:h + 1, :] += jnp.broadcast_to(dsk, (1, LANES))
        for p in range(w // LANES):
            sl = slice(LANES * p, LANES * (p + 1))
            qpair, dopair = q_ref[:, sl], do_ref[:, sl]
            kcat = _kv_cat(kc_ref, kp, p // share, cache)
            dq_ref[:, sl] = _dot(dsscr[2 * p], _lane_half(kcat, 0)) + _dot(dsscr[2 * p + 1], _lane_half(kcat, 1))
            dk_pair = _dot_tn(dsscr[2 * p], _lane_half(qpair, 0)) + _dot_tn(dsscr[2 * p + 1], _lane_half(qpair, 1))
            dv_pair = _dot_tn(pscr[2 * p], _lane_half(dopair, 0)) + _dot_tn(pscr[2 * p + 1], _lane_half(dopair, 1))
            if nb > 1:
                dk_ref[:, sl] = dk_pair[BLK:] + ck[:, sl]
                dv_ref[:, sl] = dv_pair[BLK:] + cv[:, sl]
                ck[:, sl] = dk_pair[:BLK]
                cv[:, sl] = dv_pair[:BLK]
            else:
                dk_ref[:, sl] = dk_pair
                dv_ref[:, sl] = dv_pair

    assert max_dist in (BLK - 1, BLK) and w % wk == 0
    share = w // wk
    diag = nb > 1 and max_dist == BLK
    cur, prev, grid = _attn_specs(n, nb, True)
    in_specs = ([cur(w), cur(wk)] + ([prev(wk)] if nb > 1 else []) + [cur(wk)] + ([prev(wk)] if nb > 1 else [])
                + [cur(w), cur(LANES), cur(LANES)])
    args = [q, k] + ([k] if nb > 1 else []) + [v] + ([v] if nb > 1 else []) + [do, delta, lse]
    out_specs = [cur(w)] * 3
    out_shape = [jax.ShapeDtypeStruct((n, l, w), f32)] * 3
    if has_sink:
        in_specs = [pl.BlockSpec(memory_space=pltpu.SMEM)] + in_specs
        args = [sink] + args
        out_specs.append(pl.BlockSpec((NHEAD, LANES), lambda a, i: (0, 0)))
        out_shape.append(jax.ShapeDtypeStruct((NHEAD, LANES), f32))
    nh = w // HD
    scratch = [pltpu.VMEM((BLK, w), f32), pltpu.VMEM((BLK, w), f32)] if nb > 1 else []
    scratch += [pltpu.VMEM((nh, BLK, BLK), f32)] * 2 + [pltpu.VMEM((nh, BLK, 2 * BLK if nb > 1 else BLK), bf16)] * 2
    if diag:
        scratch += [pltpu.VMEM((nh, BLK, BLK), f32)] * 2
    return pl.pallas_call(
        body, name=name, grid=grid, in_specs=in_specs, out_specs=out_specs, out_shape=out_shape,
        scratch_shapes=scratch, compiler_params=_cp(("arbitrary", "arbitrary")),
    )(*args)


def _split2(x):
    hi = x.astype(bf16)
    return hi, (x - hi.astype(f32)).astype(bf16)


def _heads_to_lanes(xc, e):
    return sum(_dot(t, e) for t in _split2(xc))


def _lanes_to_heads(x, g):
    return sum(_dot(t, g) for t in _split2(x))


HEAD_EXPAND = (np.arange(LANES)[:, None] == np.arange(BW)[None, :] // HD).astype(np.float32)
HEAD_SUM = HEAD_EXPAND.T.copy()


def _branch_weights(l1_ref, l4_ref, l16_ref, scr):
    l4v = _perm_load(l4_ref, scr, 4)
    l16v = _perm_load(l16_ref, scr, 16)
    l1v = l1_ref[...]
    m = jnp.maximum(jnp.maximum(l1v, l4v), l16v)
    e1, e4, e16 = jnp.exp(l1v - m), jnp.exp(l4v - m), jnp.exp(l16v - m)
    z = e1 + e4 + e16
    return e1 / z, e4 / z, e16 / z


def _mix_out(oa, o1, l1, o4, l4, o16, l16, g_mix_a, g_mix_b, w_out, x, mod, g_post):
    def body(oa_ref, o1_ref, l1_ref, o4_ref, l4_ref, o16_ref, l16_ref, ga_ref, gb_ref, w_ref, x_ref, mod_ref, gp_ref, e_ref,
             x1_ref, y_ref, mixed_ref, ob_ref, scr):
        w1, w4, w16 = _branch_weights(l1_ref, l4_ref, l16_ref, scr)
        e = e_ref[...]
        x1w, x4w = _heads_to_lanes(w1, e), _heads_to_lanes(w4, e)
        ob = (x1w * o1_ref[...].astype(f32) + x4w * _perm_load(o4_ref, scr, 4)
              + (1.0 - x1w - x4w) * _perm_load(o16_ref, scr, 16))
        ob_ref[...] = ob
        oan, _ = _rms(oa_ref[...])
        obn, _ = _rms(ob)
        mixed = jnp.concatenate([oan * ga_ref[...], obn * gb_ref[...]], axis=1).astype(bf16)
        mixed_ref[...] = mixed
        y = _dot(mixed, w_ref[...])
        y_ref[...] = y
        yn, _ = _rms(y)
        x1_ref[...] = x_ref[...] + mod_ref[2:3, :] * (yn * gp_ref[...])

    nat = lambda w, dt: jax.ShapeDtypeStruct((BL, SEQ, w), dt)
    return pl.pallas_call(
        body, name="mix_out", grid=(BL, NJ),
        in_specs=[_tok(AQ), _tok(BW), _tok(LANES), _perm_spec(4, BW), _perm_spec(4, LANES), _perm_spec(16, BW),
                  _perm_spec(16, LANES), _full((1, AQ)), _full((1, BW)), _full((D, D)), _tok(D), MOD_SPEC, _full((1, D)),
                  _full((LANES, BW))],
        out_specs=[_tok(D), _tok(D), _tok(D), _tok(BW)],
        out_shape=[nat(D, f32), nat(D, f32), nat(D, bf16), nat(BW, f32)],
        scratch_shapes=[pltpu.VMEM((BW // LANES, TM, LANES), f32)],
        compiler_params=_cp(("arbitrary", "arbitrary")),
    )(oa, o1, l1, o4, l4, o16, l16, g_mix_a, g_mix_b, w_out, x, mod, g_post, jnp.asarray(HEAD_EXPAND, bf16))


def _mlp_up(x1, mod, g_pre, w_up):
    def body(x_ref, mod_ref, g_ref, w_ref, h_ref, u_ref, a_ref):
        xn, _ = _rms(x_ref[...])
        h = (xn * g_ref[...]) * (1.0 + mod_ref[4:5, :]) + mod_ref[3:4, :]
        hb = h.astype(bf16)
        h_ref[...] = hb
        for s in range(NCHIP):
            u = _dot(hb, w_ref[s])
            u_ref[:, D * s:D * (s + 1)] = u.astype(bf16)
            a_ref[:, D * s:D * (s + 1)] = jnp.square(jnp.maximum(u, 0.0)).astype(bf16)

    nat = lambda w: jax.ShapeDtypeStruct((BL, SEQ, w), bf16)
    return pl.pallas_call(
        body, name="mlp_up", grid=(BL, NJ),
        in_specs=[_tok(D), MOD_SPEC, _full((1, D)), _full((NCHIP, D, D))],
        out_specs=[_tok(D), _tok(DFF), _tok(DFF)], out_shape=[nat(D), nat(DFF), nat(DFF)],
        compiler_params=_cp(("arbitrary", "arbitrary")),
    )(x1, mod, g_pre, w_up)


def _mlp_down(a, w_down, x1, target, mod, g_post):
    def body(a_ref, w_ref, x_ref, t_ref, mod_ref, g_ref, gx_ref, dy_ref, accb_ref, accg_ref):
        _acc_init(accb_ref, accg_ref)
        y2 = _dot(a_ref[...], w_ref[...])
        yn, r = _rms(y2)
        g = g_ref[...]
        gt = mod_ref[5:6, :]
        n2 = yn * g
        err = x_ref[...] + gt * n2 - t_ref[...]
        gout = err * (1.0 / D)
        gx_ref[...] = gout
        dn2 = gout * gt
        dy_ref[...] = _rms_bwd(dn2 * g, yn, r).astype(bf16)
        accb_ref[0:1, :] += _colsum(gout * n2)
        accg_ref[0:1, :] += _colsum(dn2 * yn)
        accg_ref[1:2, :] += jnp.broadcast_to(jnp.sum(err * err, keepdims=True), (1, D))

    return pl.pallas_call(
        body, name="mlp_down", grid=(BL, NJ),
        in_specs=[_tok(DFF), _full((DFF, D)), _tok(D), _tok(D), MOD_SPEC, _full((1, D))],
        out_specs=[_tok(D), _tok(D), ACCB_SPEC, ACCG_SPEC],
        out_shape=[jax.ShapeDtypeStruct((BL, SEQ, D), f32), jax.ShapeDtypeStruct((BL, SEQ, D), bf16)] + ACC_SHAPES,
        compiler_params=_cp(("arbitrary", "arbitrary")),
    )(a, w_down, x1, target, mod, g_post)


def _mlp_bwd(dy2, u, w_down, w_up, x1, gx, mod, g_pre):
    def body(dy_ref, u_ref, wd_hbm, wu_hbm, x_ref, gx_ref, mod_ref, g_ref, du_ref, gx1_ref, accb_ref, accg_ref, wd, wu, sem):
        _acc_init(accb_ref, accg_ref)
        first = (pl.program_id(0) == 0) & (pl.program_id(1) == 0)
        c1 = pltpu.make_async_copy(wd_hbm, wd, sem.at[0])
        c2 = pltpu.make_async_copy(wu_hbm, wu, sem.at[1])

        @pl.when(first)
        def _():
            c1.start()
            c2.start()
            c1.wait()

        dy = dy_ref[...]
        for s in range(NCHIP):
            sl = slice(D * s, D * (s + 1))
            da = _dot_nt(dy, wd[sl, :])
            du_ref[:, sl] = (da * (2.0 * jnp.maximum(u_ref[:, sl].astype(f32), 0.0))).astype(bf16)

        @pl.when(first)
        def _():
            c2.wait()

        dh = jnp.zeros((TM, D), f32)
        for s in range(NCHIP):
            dh = dh + _dot_nt(du_ref[:, D * s:D * (s + 1)], wu[s])
        xn, r = _rms(x_ref[...])
        g = g_ref[...]
        n = xn * g
        dn = dh * (1.0 + mod_ref[4:5, :])
        gx1_ref[...] = gx_ref[...] + _rms_bwd(dn * g, xn, r)
        accb_ref[0:1, :] += _colsum(dh * n)
        accb_ref[1:2, :] += _colsum(dh)
        accg_ref[0:1, :] += _colsum(dn * xn)

    anyspec = pl.BlockSpec(memory_space=pl.ANY)
    return pl.pallas_call(
        body, name="mlp_bwd", grid=(BL, NJ),
        in_specs=[_tok(D), _tok(DFF), anyspec, anyspec, _tok(D), _tok(D), MOD_SPEC, _full((1, D))],
        out_specs=[_tok(DFF), _tok(D), ACCB_SPEC, ACCG_SPEC],
        out_shape=[jax.ShapeDtypeStruct((BL, SEQ, DFF), bf16), jax.ShapeDtypeStruct((BL, SEQ, D), f32)] + ACC_SHAPES,
        scratch_shapes=[pltpu.VMEM((DFF, D), bf16), pltpu.VMEM((NCHIP, D, D), bf16), pltpu.SemaphoreType.DMA((2,))],
        compiler_params=_cp(("arbitrary", "arbitrary")),
    )(dy2, u, w_down, w_up, x1, gx, mod, g_pre)


def _matmul_tn(a, b, *, tn, col_blocked, name, out_dtype=f32):
    t, m = a.shape
    n = b.shape[1]
    tmm = min(m, 1024)
    tk = 2048 if tn <= 1024 else 1024
    nk = t // tk

    def body(a_ref, b_ref, o_ref, acc):
        k = pl.program_id(2)

        @pl.when(k == 0)
        def _():
            acc[...] = jnp.zeros_like(acc)

        acc[...] += _dot_tn(a_ref[...], b_ref[...])

        @pl.when(k == nk - 1)
        def _():
            o_ref[...] = acc[...].astype(out_dtype)

    if col_blocked:
        out_spec = pl.BlockSpec((None, tmm, tn), lambda i, j, k: (j, i, 0))
        out_shape = jax.ShapeDtypeStruct((n // tn, m, tn), out_dtype)
    else:
        out_spec = pl.BlockSpec((tmm, tn), lambda i, j, k: (i, j))
        out_shape = jax.ShapeDtypeStruct((m, n), out_dtype)
    return pl.pallas_call(
        body, name=name, grid=(m // tmm, n // tn, nk),
        in_specs=[pl.BlockSpec((tk, tmm), lambda i, j, k: (k, i)), pl.BlockSpec((tk, tn), lambda i, j, k: (k, j))],
        out_specs=out_spec, out_shape=out_shape, scratch_shapes=[pltpu.VMEM((tmm, tn), f32)],
        compiler_params=_cp(("arbitrary", "arbitrary", "arbitrary")),
    )(a, b)


def _grad_w_in(h, dproj):
    t = h.shape[0]
    tk = 1024
    nk = t // tk
    sw = INW // NCHIP

    def body(a_ref, b_ref, o_ref, acc):
        k = pl.program_id(0)

        @pl.when(k == 0)
        def _():
            acc[...] = jnp.zeros_like(acc)

        acc[...] += _dot_tn(a_ref[...], b_ref[...])

        @pl.when(k == nk - 1)
        def _():
            for s in range(NCHIP):
                o_ref[s] = acc[:, sw * s:sw * (s + 1)].astype(bf16)

    return pl.pallas_call(
        body, name="grad_w_in", grid=(nk,),
        in_specs=[pl.BlockSpec((tk, D), lambda k: (k, 0)), pl.BlockSpec((tk, INW), lambda k: (k, 0))],
        out_specs=pl.BlockSpec((NCHIP, D, sw), lambda k: (0, 0, 0)), out_shape=jax.ShapeDtypeStruct((NCHIP, D, sw), bf16),
        scratch_shapes=[pltpu.VMEM((D, INW), f32)], compiler_params=_cp(("arbitrary",)),
    )(h, dproj)


def _attn_out_bwd(gx1, y, mod, g_post, w_out, oa, ob, g_mix_a, g_mix_b, l1, l4, l16):
    def body(gx_ref, y_ref, mod_ref, gp_ref, w_ref, oa_ref, ob_ref, ga_ref, gb_ref, l1_ref, l4_ref, l16_ref, e_ref, g_ref,
             dy_ref, doa_ref, do1_ref, do4_ref, do16_ref, da_ref, d1_ref, d4_ref, d16_ref, accb_ref, accg_ref, scr):
        _acc_init(accb_ref, accg_ref)
        w1, w4, w16 = _branch_weights(l1_ref, l4_ref, l16_ref, scr)
        e, hs = e_ref[...], g_ref[...]
        gx1v = gx_ref[...]
        yn, ry = _rms(y_ref[...])
        gp = gp_ref[...]
        gt = mod_ref[2:3, :]
        dn1 = gx1v * gt
        dy = _rms_bwd(dn1 * gp, yn, ry).astype(bf16)
        dy_ref[...] = dy
        dmixed = _dot_nt(dy, w_ref[...])
        dma, dmb = dmixed[:, :AQ], dmixed[:, AQ:]
        oa, ob = oa_ref[...], ob_ref[...]
        oan, ra = _rms(oa)
        obn, rb = _rms(ob)
        doa = _rms_bwd(dma * ga_ref[...], oan, ra)
        doa_ref[...] = doa.astype(bf16)
        da_ref[...] = _lanes_to_heads(doa * oa, hs)
        dob = _rms_bwd(dmb * gb_ref[...], obn, rb)
        dd = _lanes_to_heads(dob * ob, hs)
        x1w, x4w = _heads_to_lanes(w1, e), _heads_to_lanes(w4, e)
        do1_ref[...] = (x1w * dob).astype(bf16)
        d1_ref[...] = w1 * dd
        _perm_store(x4w * dob, scr, do4_ref, 4)
        _perm_store(w4 * dd, scr, d4_ref, 4)
        _perm_store((1.0 - x1w - x4w) * dob, scr, do16_ref, 16)
        _perm_store(w16 * dd, scr, d16_ref, 16)
        accb_ref[0:1, :] += _colsum(gx1v * (yn * gp))
        accg_ref[0:1, :] += _colsum(dn1 * yn)
        accg_ref[1:2, :] += jnp.concatenate([_colsum(dma * oan), _colsum(dmb * obn)], axis=1)

    nat = lambda w, dt: jax.ShapeDtypeStruct((BL, SEQ, w), dt)
    return pl.pallas_call(
        body, name="attn_out_bwd", grid=(BL, NJ),
        in_specs=[_tok(D), _tok(D), MOD_SPEC, _full((1, D)), _full((D, D)), _tok(AQ), _tok(BW), _full((1, AQ)), _full((1, BW)),
                  _tok(LANES), _perm_spec(4, LANES), _perm_spec(16, LANES), _full((LANES, BW)), _full((BW, LANES))],
        out_specs=[_tok(D), _tok(AQ), _tok(BW), _perm_spec(4, BW), _perm_spec(16, BW),
                   _tok(LANES), _tok(LANES), _perm_spec(4, LANES), _perm_spec(16, LANES), ACCB_SPEC, ACCG_SPEC],
        out_shape=[nat(D, bf16), nat(AQ, bf16), nat(BW, bf16), jax.ShapeDtypeStruct((BL, 4, SEQ // 4, BW), bf16),
                   jax.ShapeDtypeStruct((BL, 16, SEQ // 16, BW), bf16), nat(LANES, f32), nat(LANES, f32),
                   jax.ShapeDtypeStruct((BL, 4, SEQ // 4, LANES), f32), jax.ShapeDtypeStruct((BL, 16, SEQ // 16, LANES), f32)]
                  + ACC_SHAPES,
        scratch_shapes=[pltpu.VMEM((BW // LANES, TM, LANES), f32)],
        compiler_params=_cp(("arbitrary", "arbitrary")),
    )(gx1, y, mod, g_post, w_out, oa, ob, g_mix_a, g_mix_b, l1, l4, l16, jnp.asarray(HEAD_EXPAND, bf16),
      jnp.asarray(HEAD_SUM, bf16))


def _attn_in_bwd(dqa, dka, dva, d1, d4, d16, tc, ts1, ts2, w_in, x, gx1, mod, g_pre):
    def body(dqa_ref, dka_ref, dva_ref, dq1_ref, dk1_ref, dv1_ref, dq4_ref, dk4_ref, dv4_ref, dq16_ref, dk16_ref, dv16_ref,
             c_ref, s1_ref, s2_ref, w_ref, x_ref, gx_ref, mod_ref, g_ref, dproj_ref, dx_ref, accb_ref, accg_ref, scr):
        _acc_init(accb_ref, accg_ref)
        c, s1, s2 = c_ref[...], s1_ref[...], s2_ref[...]
        tot = lambda r1, r4, r16: r1[...] + _perm_load(r4, scr, 4) + _perm_load(r16, scr, 16)
        dqb = tot(dq1_ref, dq4_ref, dq16_ref)
        dkb = tot(dk1_ref, dk4_ref, dk16_ref)
        dvb = tot(dv1_ref, dv4_ref, dv16_ref)
        dproj = jnp.concatenate([
            _rope_t(dqa_ref[...], c, s1, s2) * QSCALE, _rope_t(_per_kv_head(dka_ref[...]), c, s1, s2),
            _per_kv_head(dva_ref[...]),
            _rope_t(dqb, c, s1, s2) * QSCALE, _rope_t(dkb, c, s1, s2), dvb], axis=1).astype(bf16)
        dproj_ref[...] = dproj
        dh = _dot_nt(dproj, w_ref[...])
        xn, r = _rms(x_ref[...])
        g = g_ref[...]
        dn = dh * (1.0 + mod_ref[1:2, :])
        dx_ref[...] = gx_ref[...] + _rms_bwd(dn * g, xn, r)
        accb_ref[0:1, :] += _colsum(dh * (xn * g))
        accb_ref[1:2, :] += _colsum(dh)
        accg_ref[0:1, :] += _colsum(dn * xn)

    return pl.pallas_call(
        body, name="attn_in_bwd", grid=(BL, NJ),
        in_specs=[_tok(AQ), _tok(AQ), _tok(AQ)] + [_tok(BW)] * 3 + [_perm_spec(4, BW)] * 3 + [_perm_spec(16, BW)] * 3
                 + [_tok(LANES)] * 3 + [_full((D, INW)), _tok(D), _tok(D), MOD_SPEC, _full((1, D))],
        out_specs=[_tok(INW), _tok(D), ACCB_SPEC, ACCG_SPEC],
        out_shape=[jax.ShapeDtypeStruct((BL, SEQ, INW), bf16), jax.ShapeDtypeStruct((BL, SEQ, D), f32)] + ACC_SHAPES,
        scratch_shapes=[pltpu.VMEM((BW // LANES, TM, LANES), f32)],
        compiler_params=_cp(("arbitrary", "arbitrary")),
    )(dqa, dka, dva, *d1, *d4, *d16, tc, ts1, ts2, w_in, x, gx1, mod, g_pre)


def _inv_lane():
    inv = np.float32(THETA) ** (-np.arange(0, ROT, 2, dtype=np.float32) / np.float32(ROT))
    lane = np.arange(LANES) % HD
    return jnp.asarray(np.where(lane < ROT, inv[lane % (ROT // 2)], 0.0).astype(np.float32)[None, :])


def _local_step(x, positions, mod, target, inv_lane, first_weight, later_weights, grad_ready, g_attn_pre,
                g_attn_post, sink_a, g_mix_a, g_mix_b, g_mlp_pre, g_mlp_post):
    tabs = _rope_tables(positions.reshape(BL * SEQ, 1), inv_lane)
    w_in = first_weight(tuple(tabs))
    tc, ts1, ts2 = [t.reshape(BL, SEQ, LANES) for t in tabs]

    (h, qa, ka, va, q1, k1, v1, q4, k4, v4, q16, k16, v16, w_in) = _attn_in(x, mod, g_attn_pre, w_in, tc, ts1, ts2)
    seqs = lambda t: t.reshape(t.shape[0] * t.shape[1], t.shape[2], t.shape[3])
    q4, k4, v4, q16, k16, v16 = [seqs(t) for t in (q4, k4, v4, q16, k16, v16)]
    oa, la = _attn_fwd(qa, ka, va, sink_a, max_dist=BLK - 1, o_dtype=f32, name="attn_a_fwd")
    o1, l1 = _attn_fwd(q1, k1, v1, None, max_dist=BLK, o_dtype=bf16, name="attn_b1_fwd")
    o4, l4 = _attn_fwd(q4, k4, v4, None, max_dist=BLK, o_dtype=bf16, name="attn_b4_fwd")
    o16, l16 = _attn_fwd(q16, k16, v16, None, max_dist=BLK, o_dtype=bf16, name="attn_b16_fwd")
    b4 = lambda t: t.reshape(BL, 4, SEQ // 4, t.shape[-1])
    b16 = lambda t: t.reshape(BL, 16, SEQ // 16, t.shape[-1])
    w_out, mlp_weights, mod = later_weights((oa, o1, o4, o16), mod)
    x1, y, mixed, ob = _mix_out(oa, o1, l1, b4(o4), b4(l4), b16(o16), b16(l16), g_mix_a, g_mix_b, w_out, x, mod, g_attn_post)
    w_up, w_down = mlp_weights((x1,))
    h2, u, a = _mlp_up(x1, mod, g_mlp_pre, w_up)
    gx, dy2, accb_d, accg_d = _mlp_down(a, w_down, x1, target, mod, g_mlp_post)

    flat = lambda t: t.reshape(BL * SEQ, t.shape[-1])
    mod = grad_ready("w_down", _matmul_tn(flat(a), flat(dy2), tn=D, col_blocked=False, name="grad_w_down", out_dtype=bf16), mod)
    du, gx1, accb_m, accg_m = _mlp_bwd(dy2, u, w_down, w_up, x1, gx, mod, g_mlp_pre)
    mod = grad_ready("w_up", _matmul_tn(flat(h2), flat(du), tn=D, col_blocked=True, name="grad_w_up", out_dtype=bf16), mod)

    dy, doa, do1, do4, do16, da, dl1, dl4, dl16, accb_o, accg_o = _attn_out_bwd(
        gx1, y, mod, g_attn_post, w_out, oa, ob, g_mix_a, g_mix_b, l1, b4(l4), b16(l16))
    sink_behind = grad_ready("w_out", _matmul_tn(flat(mixed), flat(dy), tn=D, col_blocked=False, name="grad_w_out",
                                                  out_dtype=bf16), sink_a)
    dqa, dka, dva, dsink = _attn_bwd(qa, ka, va, doa, da, la, sink_behind, max_dist=BLK - 1, name="attn_a_bwd")
    d1 = _attn_bwd(q1, k1, v1, do1, dl1, l1, None, max_dist=BLK, name="attn_b1_bwd")
    d4 = _attn_bwd(q4, k4, v4, seqs(do4), seqs(dl4), l4, None, max_dist=BLK, name="attn_b4_bwd")
    d16 = _attn_bwd(q16, k16, v16, seqs(do16), seqs(dl16), l16, None, max_dist=BLK, name="attn_b16_bwd")
    dproj, grad_x, accb_i, accg_i = _attn_in_bwd(dqa, dka, dva, d1, [b4(t) for t in d4], [b16(t) for t in d16],
                                                 tc, ts1, ts2, w_in, x, gx1, mod, g_attn_pre)
    gw_in = _grad_w_in(flat(h), flat(dproj))
    dsink = grad_ready("w_in", gw_in, dsink)

    return grad_x, (accb_i, accb_o, accb_m, accb_d, accg_i, accg_o, accg_m, accg_d, dsink)


ADAW = NMOD * D // NCHIP


def _pos():
    return lax.axis_index("x"), lax.axis_index("y"), lax.axis_index("c")


def _flip(v, bit):
    return 1 - v if bit else v


def _all_peers(x, y, c):
    return [(_flip(x, k >> 2 & 1), _flip(y, k >> 1 & 1), _flip(c, k & 1)) for k in range(1, NDEV)]


def _other_chips(x, y):
    return [(1 - x, y), (x, 1 - y), (1 - x, 1 - y)]


def _rcopy(src, dst, send, recv, k, dev, k_recv=None):
    return pltpu.make_async_remote_copy(src_ref=src, dst_ref=dst, send_sem=send.at[k],
                                        recv_sem=recv.at[k if k_recv is None else k_recv],
                                        device_id=dev, device_id_type=MESH)


def _gather_small(src, buf, send, recv):
    x, y, c = _pos()
    me = 4 * x + 2 * y + c
    peers = _all_peers(x, y, c)
    sends = [_rcopy(src, buf.at[me], send, recv, k, p) for k, p in enumerate(peers)]
    for cp in sends:
        cp.start()
    for k, (px, py, pc) in enumerate(peers):
        _rcopy(src, buf.at[4 * px + 2 * py + pc], send, recv, k, (px, py, pc)).wait_recv()
    for cp in sends:
        cp.wait_send()
    return me


def _ada_fwd(c_in, w_ada, b_cols):
    def body(c_ref, w_hbm, b_ref, mod_ref, cond_ref, cbuf, mbuf, w_ref, s1, r1, s2, r2, wsem):
        x, y, c = _pos()
        chip = 2 * x + y
        wcopy = pltpu.make_async_copy(w_hbm, w_ref, wsem)
        wcopy.start()
        me = _gather_small(c_ref, cbuf, s1, r1)
        cbuf[me] = c_ref[...]
        for i in range(NDEV):
            cond_ref[BL * i:BL * (i + 1), :] = cbuf[i]
        call = cond_ref[...]
        cond = call / (1.0 + jnp.exp(-call))
        cond_ref[...] = cond
        wcopy.wait()
        mbuf[chip] = _dot(cond.astype(bf16), w_ref[...].astype(bf16)) + b_ref[...]
        chips = _other_chips(x, y)
        sends = [_rcopy(mbuf.at[chip], mbuf.at[chip], s2, r2, j, (px, py, c)) for j, (px, py) in enumerate(chips)]
        for cp in sends:
            cp.start()
        for j, (px, py) in enumerate(chips):
            _rcopy(mbuf.at[chip], mbuf.at[2 * px + py], s2, r2, j, (px, py, c)).wait_recv()
        for cp in sends:
            cp.wait_send()
        row = lax.broadcasted_iota(jnp.int32, (BL * NDEV, ADAW), 0)
        for s in range(NCHIP):
            slab = mbuf[s]
            for j in range(BL):
                mod_ref[j:j + 1, ADAW * s:ADAW * (s + 1)] = jnp.sum(jnp.where(row == BL * me + j, slab, 0.0), axis=0, keepdims=True)

    vm = pl.BlockSpec(memory_space=pltpu.VMEM)
    return pl.pallas_call(
        body, name="ada_fwd", in_specs=[vm, pl.BlockSpec(memory_space=pl.ANY), vm], out_specs=[vm, vm],
        out_shape=[jax.ShapeDtypeStruct((BL, NMOD * D), f32), jax.ShapeDtypeStruct((BL * NDEV, D), f32)],
        scratch_shapes=[pltpu.VMEM((NDEV, BL, D), f32), pltpu.VMEM((NCHIP, BL * NDEV, ADAW), f32),
                        pltpu.VMEM((D, ADAW), f32),
                        pltpu.SemaphoreType.DMA((NDEV - 1,)), pltpu.SemaphoreType.DMA((NDEV - 1,)),
                        pltpu.SemaphoreType.DMA((NCHIP - 1,)), pltpu.SemaphoreType.DMA((NCHIP - 1,)),
                        pltpu.SemaphoreType.DMA],
        compiler_params=pltpu.CompilerParams(vmem_limit_bytes=VMEM_LIMIT),
    )(c_in, w_ada, b_cols)


def _small_allreduce(accs, cond_all):
    def body(bi, bo, bm, bd, gi, go, gm, gd, dsink, cond_ref, gw_ref, gb_ref, small_ref, pay, pbuf, dall, s1, r1):
        x, y, c = _pos()
        chip = 2 * x + y
        pay[...] = jnp.zeros_like(pay)
        for b in range(BL):
            for k, (ref, r) in enumerate(((bi, 1), (bi, 0), (bo, 0), (bm, 1), (bm, 0), (bd, 0))):
                pay[b:b + 1, D * k:D * (k + 1)] = ref[b, r:r + 1, :]
        for off, ref, r in ((OFF_G_ATTN_PRE, gi, 0), (OFF_G_ATTN_POST, go, 0), (OFF_G_MIX_A, go, 1), (OFF_G_MLP_PRE, gm, 0),
                            (OFF_G_MLP_POST, gd, 0)):
            pay[BL:BL + 1, off:off + D] = ref[r:r + 1, :]
        eye = lax.broadcasted_iota(jnp.int32, (NHEAD, LANES), 0) == lax.broadcasted_iota(jnp.int32, (NHEAD, LANES), 1)
        pay[BL:BL + 1, OFF_SINK:OFF_SINK + LANES] = jnp.sum(jnp.where(eye, dsink[...], 0.0), axis=0, keepdims=True)
        pay[BL:BL + 1, OFF_LOSS:OFF_LOSS + LANES] = gd[1:2, 0:LANES]
        me = _gather_small(pay, pbuf, s1, r1)
        pbuf[me] = pay[...]
        small = pbuf[0, BL:BL + 1, :]
        for i in range(1, NDEV):
            small = small + pbuf[i, BL:BL + 1, :]
        small_ref[...] = small
        for i in range(NDEV):
            dall[BL * i:BL * (i + 1), :] = pbuf[i, 0:BL, :]
        gb_ref[...] = jnp.sum(dall[...], axis=0, keepdims=True)
        cols = jnp.zeros((BL * NDEV, ADAW), f32)
        for s in range(NCHIP):
            cols = cols + jnp.where(chip == s, dall[:, ADAW * s:ADAW * (s + 1)], 0.0)
        gw_ref[...] = _dot_tn(cond_ref[...].astype(bf16), cols.astype(bf16))

    vm = pl.BlockSpec(memory_space=pltpu.VMEM)
    return pl.pallas_call(
        body, name="small_allreduce", in_specs=[vm] * 10, out_specs=[vm] * 3,
        out_shape=[jax.ShapeDtypeStruct((D, ADAW), f32), jax.ShapeDtypeStruct((1, PAYW), f32), jax.ShapeDtypeStruct((1, PAYW), f32)],
        scratch_shapes=[pltpu.VMEM((4, PAYW), f32), pltpu.VMEM((NDEV, 4, PAYW), f32), pltpu.VMEM((BL * NDEV, PAYW), f32),
                        pltpu.SemaphoreType.DMA((NDEV - 1,)), pltpu.SemaphoreType.DMA((NDEV - 1,))],
        compiler_params=pltpu.CompilerParams(vmem_limit_bytes=VMEM_LIMIT),
    )(*accs, cond_all)


def _half(ref, c):
    r2 = ref.shape[0] // 2
    return ref.at[pl.ds(c * r2 if isinstance(c, int) else pl.multiple_of(c * r2, 16), r2), :]


HBM_SPEC = pl.BlockSpec(memory_space=pltpu.HBM)
SEM_SPEC = pl.BlockSpec(memory_space=pltpu.SEMAPHORE)
EFFECT = pltpu.SideEffectType.DATAFLOW_SIDE_EFFECTING
NLINK = NCHIP - 1


def _in_hbm(a):
    return pltpu.with_memory_space_constraint(a, pltpu.HBM)


NSEM = 8


def _split_start(name, srcs, land_shapes, builds, carry, after=(), lands=None):
    n = len(srcs)
    na, nc = len(after), len(carry)

    def body(*refs):
        src, land = refs[:n], refs[n:2 * n]
        kept = refs[2 * n + na:2 * n + na + nc]
        outs = refs[2 * n + na + nc:]
        send, recv, passed = outs[:n], outs[n:2 * n], outs[4 * n:]
        for t in range(n):
            for out_cp, _ in builds[t](src[t], land[t], send[t], recv[t]):
                out_cp.start()
        for a, b in zip(kept, passed):
            b[...] = a[...]

    if lands is None:
        lands = [lax.empty(s.shape, s.dtype) for s in land_shapes]
    lands = [_in_hbm(a) for a in lands]
    sems = [pltpu.SemaphoreType.DMA((NSEM,))] * (2 * n)
    thru = [pltpu.HBM(a.shape, a.dtype) for a in list(srcs) + lands]
    vm = pl.BlockSpec(memory_space=pltpu.VMEM)
    res = pl.pallas_call(
        body, name=name, out_shape=sems + thru + [jax.ShapeDtypeStruct(a.shape, a.dtype) for a in carry],
        in_specs=[HBM_SPEC] * (2 * n) + [pl.BlockSpec(memory_space=pl.ANY)] * na + [vm] * nc,
        out_specs=[SEM_SPEC] * (2 * n) + [HBM_SPEC] * (2 * n) + [vm] * nc,
        input_output_aliases={i: 2 * n + i for i in range(2 * n)},
        compiler_params=pltpu.CompilerParams(has_side_effects=EFFECT),
    )(*[_in_hbm(a) for a in srcs], *lands, *after, *carry)
    flight = [(res[2 * n + t], res[3 * n + t], res[t], res[n + t]) for t in range(n)]
    return flight, list(res[4 * n:])


def _split_wait(name, flight, builds, after):
    m = len(flight)
    na = len(after)

    def body(*refs):
        src, land, send, recv = refs[:m], refs[m:2 * m], refs[2 * m:3 * m], refs[3 * m:4 * m]
        for t in range(m):
            for out_cp, in_cp in builds[t](src[t], land[t], send[t], recv[t]):
                out_cp.wait_send()
                in_cp.wait_recv()

    ops = [f[0] for f in flight] + [f[1] for f in flight] + [f[2] for f in flight] + [f[3] for f in flight]
    res = pl.pallas_call(
        body, name=name, out_shape=[pltpu.HBM(a.shape, a.dtype) for a in ops[:2 * m]],
        in_specs=[HBM_SPEC] * (2 * m) + [SEM_SPEC] * (2 * m) + [pl.BlockSpec(memory_space=pl.ANY)] * na,
        out_specs=[HBM_SPEC] * (2 * m), input_output_aliases={i: i for i in range(2 * m)},
        compiler_params=pltpu.CompilerParams(has_side_effects=EFFECT),
    )(*ops, *after)
    return res[:m], res[m:2 * m]


def _weight_copies(src, land, send, recv):
    x, y, c = _pos()
    chip = 2 * x + y
    return [(_rcopy(_half(src, c), _half(land.at[chip], c), send, recv, j, (px, py, c)),
             _rcopy(_half(src, c), _half(land.at[2 * px + py], c), send, recv, j, (px, py, c)))
            for j, (px, py) in enumerate(_other_chips(x, y))]


NDIRECT = NDEV - 1


def _direct_grad_copies(src, land, send, recv):
    x, y, c = _pos()
    out, arrive = [], []
    for j, (px, py) in enumerate(_other_chips(x, y)):
        for hc in range(2):
            out.append(_rcopy(_half(src.at[2 * px + py], hc), land.at[2 * j + c], send, recv, 2 * j + hc, (px, py, hc),
                              k_recv=2 * j + c))
            arrive.append(_rcopy(_half(src.at[2 * px + py], hc), land.at[2 * j + hc], send, recv, 2 * j + hc, (px, py, hc)))
    own = _rcopy(_half(src.at[2 * x + y], 1 - c), land.at[NDIRECT - 1], send, recv, NDIRECT - 1, (x, y, 1 - c))
    return list(zip(out, arrive)) + [(own, own)]


def _pair_weight_copies(src, land, send, recv):
    x, y, c = _pos()
    sib = (x, y, 1 - c)
    cps = []
    for j, (px, py) in enumerate(_other_chips(x, y)):
        mine, theirs = _half(land.at[2 * px + py], c), _half(land.at[2 * px + py], 1 - c)
        cps.append((_rcopy(mine, mine, send, recv, j, sib), _rcopy(theirs, theirs, send, recv, j, sib)))
    own = _rcopy(src, land.at[2 * x + y], send, recv, NLINK, sib)
    return cps + [(own, own)]


RS_ROWS = 128


def _chip_add(own, landed, pos_arr, name):
    nl, r2, cw = landed.shape
    nr = r2 // RS_ROWS

    def body(s_ref, h_ref, q_ref, o_ref):
        acc = h_ref[...].astype(f32)
        for j in range(nl):
            acc = acc + q_ref[j].astype(f32)
        o_ref[...] = acc

    gs = pltpu.PrefetchScalarGridSpec(
        num_scalar_prefetch=1, grid=(nr,),
        in_specs=[pl.BlockSpec((None, RS_ROWS, cw), lambda j, s: (s[0], s[1] * nr + j, 0)),
                  pl.BlockSpec((nl, RS_ROWS, cw), lambda j, s: (0, j, 0))],
        out_specs=pl.BlockSpec((RS_ROWS, cw), lambda j, s: (s[1] * nr + j, 0)))
    return pl.pallas_call(body, name=name, grid_spec=gs, out_shape=jax.ShapeDtypeStruct((2 * r2, cw), f32),
                          compiler_params=_cp(("arbitrary",)))(pos_arr, own, landed)


def _pair_gather_copies(src, land, send, recv):
    x, y, c = _pos()
    sib = (x, y, 1 - c)
    return [(_rcopy(_half(land, c), _half(land, c), send, recv, 0, sib),
             _rcopy(_half(land, 1 - c), _half(land, 1 - c), send, recv, 0, sib))]


def _adamw_math(w, g, m, v):
    m = B1 * m + (1.0 - B1) * g
    v = B2 * v + (1.0 - B2) * jnp.square(g)
    m_hat = m / (1.0 - B1 ** STEP)
    v_hat = v / (1.0 - B2 ** STEP)
    return -LR * (m_hat / (jnp.sqrt(v_hat) + AEPS) + WD * w), m, v


ADAM_ROWS = 256


def _adamw(w, g, m, v, name):
    r, cw = w.shape

    def body(w_ref, g_ref, m_ref, v_ref, go_ref, d_ref, mo_ref, vo_ref):
        g = g_ref[...]
        go_ref[...] = g
        d_ref[...], mo_ref[...], vo_ref[...] = _adamw_math(w_ref[...], g, m_ref[...], v_ref[...])

    rows = max(k for k in range(8, ADAM_ROWS + 1, 8) if r % k == 0)
    spec = pl.BlockSpec((rows, cw), lambda i: (i, 0))
    return pl.pallas_call(body, name=name, grid=(r // rows,), in_specs=[spec] * 4, out_specs=[spec] * 4,
                          out_shape=[jax.ShapeDtypeStruct((r, cw), f32)] * 4, compiler_params=_cp(("arbitrary",)))(w, g, m, v)


SMALL = (("b_ada", None, PAYW), ("g_attn_pre", OFF_G_ATTN_PRE, D), ("g_attn_post", OFF_G_ATTN_POST, D), ("sink_a", OFF_SINK, 8),
         ("g_mix_a", OFF_G_MIX_A, AQ), ("g_mix_b", OFF_G_MIX_B, BW), ("g_mlp_pre", OFF_G_MLP_PRE, D), ("g_mlp_post", OFF_G_MLP_POST, D))


def _adamw_small(small, gb, params):
    n = len(SMALL)

    def body(*refs):
        small_ref, gb_ref = refs[:2]
        wmv = refs[2:2 + 3 * n]
        loss_ref = refs[2 + 3 * n]
        outs = refs[3 + 3 * n:]
        loss_ref[...] = small_ref[:, OFF_LOSS:OFF_LOSS + 1] * (0.5 / D)
        for i, (_, off, width) in enumerate(SMALL):
            g = gb_ref[...] if off is None else small_ref[:, off:off + width]
            w_ref, m_ref, v_ref = wmv[3 * i:3 * i + 3]
            outs[4 * i][...] = g
            outs[4 * i + 1][...], outs[4 * i + 2][...], outs[4 * i + 3][...] = _adamw_math(w_ref[...], g, m_ref[...], v_ref[...])

    vm = pl.BlockSpec(memory_space=pltpu.VMEM)
    out_shape = [jax.ShapeDtypeStruct((1, 1), f32)]
    for _, _, width in SMALL:
        out_shape += [jax.ShapeDtypeStruct((1, width), f32)] * 4
    flat = [a for wmv in params for a in wmv]
    res = pl.pallas_call(body, name="adamw_small", in_specs=[vm] * (2 + 3 * n), out_specs=[vm] * len(out_shape),
                         out_shape=out_shape)(small, gb, *flat)
    return res[0], {name: res[1 + 4 * i:5 + 4 * i] for i, (name, _, _) in enumerate(SMALL)}


def kernel(x, c, positions, w_ada, b_ada, g_attn_pre, g_attn_post, w_in, sink_a, g_mix_a, g_mix_b, w_out, g_mlp_pre, g_mlp_post, w_up, w_down, loss_target, m_w_ada, m_b_ada, m_g_attn_pre, m_g_attn_post, m_w_in, m_sink_a, m_g_mix_a, m_g_mix_b, m_w_out, m_g_mlp_pre, m_g_mlp_post, m_w_up, m_w_down, v_w_ada, v_b_ada, v_g_attn_pre, v_g_attn_post, v_w_in, v_sink_a, v_g_mix_a, v_g_mix_b, v_w_out, v_g_mlp_pre, v_g_mlp_post, v_w_up, v_w_down):
    given = dict(w_ada=w_ada, b_ada=b_ada, g_attn_pre=g_attn_pre, g_attn_post=g_attn_post, w_in=w_in, sink_a=sink_a, g_mix_a=g_mix_a,
                 g_mix_b=g_mix_b, w_out=w_out, g_mlp_pre=g_mlp_pre, g_mlp_post=g_mlp_post, w_up=w_up, w_down=w_down)
    moms = dict(w_ada=(m_w_ada, v_w_ada), b_ada=(m_b_ada, v_b_ada), g_attn_pre=(m_g_attn_pre, v_g_attn_pre),
                g_attn_post=(m_g_attn_post, v_g_attn_post), w_in=(m_w_in, v_w_in), sink_a=(m_sink_a, v_sink_a),
                g_mix_a=(m_g_mix_a, v_g_mix_a), g_mix_b=(m_g_mix_b, v_g_mix_b), w_out=(m_w_out, v_w_out),
                g_mlp_pre=(m_g_mlp_pre, v_g_mlp_pre), g_mlp_post=(m_g_mlp_post, v_g_mlp_post), w_up=(m_w_up, v_w_up),
                w_down=(m_w_down, v_w_down))
    order = ["w_ada", "b_ada", "g_attn_pre", "g_attn_post", "w_in", "sink_a", "g_mix_a", "g_mix_b", "w_out", "g_mlp_pre",
             "g_mlp_post", "w_up", "w_down"]
    xi, yi, ci = _pos()
    chip = 2 * xi + yi

    pos_arr = jnp.stack([chip, ci]).astype(jnp.int32)
    big = ("w_in", "w_out", "w_up", "w_down")

    b_cols = lax.dynamic_slice(b_ada, (0, chip * ADAW), (1, ADAW))
    mod, cond_all = _ada_fwd(c, w_ada[0], b_cols)
    gathered = [jax.ShapeDtypeStruct((NCHIP,) + given[n].shape[1:], bf16) for n in big]
    flight_in, (mod,) = _split_start("weights_start_first", [w_in[0].astype(bf16)], gathered[:1], [_weight_copies], [mod])
    mod, rest = lax.optimization_barrier((mod, [given[n][0] for n in big[1:]]))
    flight_rest, (mod, inv_lane) = _split_start("weights_start_rest", [w.astype(bf16) for w in rest], gathered[1:],
                                                [_weight_copies] * 3, [mod, _inv_lane()])
    mod = mod.reshape(BL, NMOD, D)

    def first_weight(after):
        srcs, lands = _split_wait("weights_wait_first", flight_in, [_weight_copies], after)
        cross, _ = _split_start("weights_pair_start_first", srcs, None, [_pair_weight_copies], [], lands=lands)
        _, (win_g,) = _split_wait("weights_pair_wait_first", cross, [_pair_weight_copies], ())
        return win_g

    def later_weights(after, carry):
        srcs, lands = _split_wait("weights_wait_rest", flight_rest, [_weight_copies] * 3, after)
        fl, (carry,) = _split_start("weights_pair_start_rest", srcs, None, [_pair_weight_copies] * 3, [carry], lands=lands)
        _, (wout_g,) = _split_wait("weights_pair_wait_out", fl[:1], [_pair_weight_copies], ())

        def mlp_weights(after):
            _, (wup_g, wdn_g) = _split_wait("weights_pair_wait_mlp", fl[1:], [_pair_weight_copies] * 2, after)
            return wup_g, wdn_g.reshape(DFF, D)

        return wout_g.reshape(D, D), mlp_weights, carry

    waiting, pending = {}, {}

    def send_grad(name, slab, carry):
        land = jax.ShapeDtypeStruct((NDIRECT, slab.shape[1] // 2, slab.shape[2]), bf16)
        pending[name], (carry,) = _split_start("grad_start_" + name, [slab], [land], [_direct_grad_copies], [carry])
        return carry

    def grad_ready(name, g, carry):
        slab = g if g.ndim == 3 else g.reshape(NCHIP, g.shape[0] // NCHIP, g.shape[1])
        if name == "w_in":
            waiting[name] = slab
            return carry
        return send_grad(name, slab, carry)

    grad_x, accs = _local_step(x, positions, mod, loss_target, inv_lane, first_weight, later_weights, grad_ready,
                               g_attn_pre, g_attn_post, sink_a, g_mix_a, g_mix_b, g_mlp_pre, g_mlp_post)

    grads, out = {}, {}

    def update(n):
        tr = (lambda a: a.T) if n == "w_in" else (lambda a: a)
        res = _adamw(tr(given[n][0]), tr(grads[n]), tr(moms[n][0][0]), tr(moms[n][1][0]), "adamw_" + n)
        out[n] = tuple(tr(a)[None] for a in res)
        return res[3]

    def finish(names, after):
        fl = sum((pending[n] for n in names), [])
        halves, landed = _split_wait("grad_wait_" + names[0], fl, [_direct_grad_copies] * len(names), after)
        flights = []
        for h, q, n in zip(halves, landed, names):
            full = _chip_add(h, q, pos_arr, "grad_chip_sum_" + n)
            flights.append(_split_start("grad_gather_start_" + n, [jnp.zeros((8, LANES), f32)], None, [_pair_gather_copies],
                                        [], lands=[full])[0])
        last = None
        for n, fl1 in zip(names, flights):
            after = (flights[-1][0][0],) if last is None and fl1 is not flights[-1] else () if last is None else (last,)
            _, (grads[n],) = _split_wait("grad_gather_wait_" + n, fl1, [_pair_gather_copies], after)
            last = update(n)
        return last

    grads["w_ada"], gb, small = _small_allreduce(accs, cond_all)
    small = send_grad("w_in", waiting["w_in"], small)
    last = finish(("w_down", "w_up", "w_out"), (small,))
    finish(("w_in",), (last, update("w_ada")))
    loss, res = _adamw_small(small, gb, [(given[n], moms[n][0], moms[n][1]) for n, _, _ in SMALL])
    for n, _, _ in SMALL:
        out[n] = tuple(res[n])
    return (loss.reshape(()), grad_x, *[out[n][0] for n in order], *[out[n][1] for n in order],
            *[out[n][2] for n in order], *[out[n][3] for n in order])
```

```python
import numpy as np
import jax
import jax.numpy as jnp
from jax import lax
from jax.experimental import pallas as pl
from jax.experimental.pallas import tpu as pltpu

f32 = jnp.float32
bf16 = jnp.bfloat16
MESH = pl.DeviceIdType.MESH

D = 1024
SEQ = 2048
BL = 2
HD = 64
AQ = 512
AKV = 128
BW = 512
INW = 2304
DFF = 4096
NMOD = 6
ROT = 16
THETA = 500000.0
EPS = 1e-6
NEG = -1e30
BLK = 128
TM = 512
NJ = SEQ // TM
LANES = 128
SUBLANES = 8
NHEAD = AQ // HD
QSCALE = HD ** -0.5
NCHIP = 4
NDEV = 8
VMEM_LIMIT = 56 << 20

LR, B1, B2, AEPS, WD, STEP = 0.001, 0.9, 0.999, 1e-08, 0.01, 10

OFF_G_ATTN_PRE, OFF_G_ATTN_POST, OFF_G_MIX_A, OFF_G_MIX_B = 0, 1024, 2048, 2560
OFF_G_MLP_PRE, OFF_G_MLP_POST, OFF_SINK, OFF_LOSS = 3072, 4096, 5120, 5248
PAYW = NMOD * D


def _cp(sem=None):
    return pltpu.CompilerParams(dimension_semantics=sem, vmem_limit_bytes=VMEM_LIMIT)


def _dot(a, b):
    return jnp.dot(a, b, preferred_element_type=f32)


def _dot_nt(a, b):
    return lax.dot_general(a, b, (((1,), (1,)), ((), ())), preferred_element_type=f32)


def _dot_tn(a, b):
    return lax.dot_general(a, b, (((0,), (0,)), ((), ())), preferred_element_type=f32)


def _rms(x):
    r = lax.rsqrt(jnp.mean(x * x, axis=-1, keepdims=True) + EPS)
    return x * r, r


def _rms_bwd(dy, y, r):
    return r * (dy - y * jnp.mean(dy * y, axis=-1, keepdims=True))


def _colsum(v):
    return jnp.sum(v, axis=0, keepdims=True)


def _rope(p, c, s1, s2):
    outs = []
    for c0 in range(0, p.shape[1], LANES):
        pc = p[:, c0:c0 + LANES]
        outs.append(pc * c + pltpu.roll(pc, LANES - ROT // 2, 1) * s1 + pltpu.roll(pc, ROT // 2, 1) * s2)
    return outs[0] if len(outs) == 1 else jnp.concatenate(outs, axis=1)


def _rope_t(g, c, s1, s2):
    outs = []
    for c0 in range(0, g.shape[1], LANES):
        gc = g[:, c0:c0 + LANES]
        outs.append(gc * c + pltpu.roll(gc * s1, ROT // 2, 1) + pltpu.roll(gc * s2, LANES - ROT // 2, 1))
    return outs[0] if len(outs) == 1 else jnp.concatenate(outs, axis=1)


def _perm_store(val, scr, out_ref, d):
    nc = val.shape[1] // LANES
    for c in range(nc):
        scr[c] = val[:, LANES * c:LANES * (c + 1)]
    for c in range(nc):
        for r in range(d):
            out_ref[r, :, LANES * c:LANES * (c + 1)] = scr[c, pl.ds(r, TM // d, stride=d), :].astype(out_ref.dtype)


def _perm_load(in_ref, scr, d):
    nc = in_ref.shape[-1] // LANES
    for c in range(nc):
        for r in range(d):
            scr[c, pl.ds(r, TM // d, stride=d), :] = in_ref[r, :, LANES * c:LANES * (c + 1)].astype(f32)
    return jnp.concatenate([scr[c] for c in range(nc)], axis=1)


def _per_query_head(kv):
    r = pltpu.roll(kv, HD, 1)
    lo = lax.broadcasted_iota(jnp.int32, kv.shape, 1) < HD
    return jnp.concatenate([jnp.where(lo, kv, r), jnp.where(lo, r, kv)], axis=1)


def _per_kv_head(g):
    g0, g1 = g[:, :LANES] + g[:, LANES:2 * LANES], g[:, 2 * LANES:3 * LANES] + g[:, 3 * LANES:]
    lo = lax.broadcasted_iota(jnp.int32, g0.shape, 1) < HD
    return jnp.where(lo, g0 + pltpu.roll(g0, HD, 1), g1 + pltpu.roll(g1, HD, 1))


def _tok(w):
    return pl.BlockSpec((None, TM, w), lambda b, j: (b, j, 0))


def _perm_spec(d, w):
    return pl.BlockSpec((None, d, TM // d, w), lambda b, j: (b, 0, j, 0))


def _full(shape):
    n = len(shape)
    return pl.BlockSpec(shape, lambda b, j: (0,) * n)


MOD_SPEC = pl.BlockSpec((None, NMOD, D), lambda b, j: (b, 0, 0))
ACCB_SPEC = pl.BlockSpec((None, SUBLANES, D), lambda b, j: (b, 0, 0))
ACCG_SPEC = pl.BlockSpec((SUBLANES, D), lambda b, j: (0, 0))
ACC_SHAPES = [jax.ShapeDtypeStruct((BL, SUBLANES, D), f32), jax.ShapeDtypeStruct((SUBLANES, D), f32)]


def _acc_init(accb_ref, accg_ref):
    b, j = pl.program_id(0), pl.program_id(1)

    @pl.when(j == 0)
    def _():
        accb_ref[...] = jnp.zeros_like(accb_ref)

    @pl.when((b == 0) & (j == 0))
    def _():
        accg_ref[...] = jnp.zeros_like(accg_ref)


def _rope_tables(pos_col, inv_lane):
    def body(p_ref, inv_ref, c_ref, s1_ref, s2_ref):
        ang = p_ref[...].astype(f32) * inv_ref[...]
        j = lax.broadcasted_iota(jnp.int32, (TM, LANES), 1) % HD
        cs, sn = jnp.cos(ang), jnp.sin(ang)
        c_ref[...] = jnp.where(j < ROT, cs, 1.0)
        s1_ref[...] = jnp.where(j < ROT // 2, -sn, 0.0)
        s2_ref[...] = jnp.where((j >= ROT // 2) & (j < ROT), sn, 0.0)

    n = BL * SEQ // TM
    return pl.pallas_call(
        body, name="rope_tables", grid=(n,),
        in_specs=[pl.BlockSpec((TM, 1), lambda i: (i, 0)), pl.BlockSpec((1, LANES), lambda i: (0, 0))],
        out_specs=[pl.BlockSpec((TM, LANES), lambda i: (i, 0))] * 3,
        out_shape=[jax.ShapeDtypeStruct((BL * SEQ, LANES), f32)] * 3,
    )(pos_col, inv_lane)


def _attn_in(x, mod, g_pre, w_in, tc, ts1, ts2):
    def body(x_ref, mod_ref, g_ref, wg_ref, c_ref, s1_ref, s2_ref,
             h_ref, qa_ref, ka_ref, va_ref, q1_ref, k1_ref, v1_ref, q4_ref, k4_ref, v4_ref, q16_ref, k16_ref, v16_ref,
             w_ref, scr):
        @pl.when((pl.program_id(0) == 0) & (pl.program_id(1) == 0))
        def _():
            w_ref[...] = jnp.concatenate([wg_ref[s] for s in range(NCHIP)], axis=1)

        xn, _ = _rms(x_ref[...])
        h = (xn * g_ref[...]) * (1.0 + mod_ref[1:2, :]) + mod_ref[0:1, :]
        hb = h.astype(bf16)
        h_ref[...] = hb
        proj = _dot(hb, w_ref[...])
        c, s1, s2 = c_ref[...], s1_ref[...], s2_ref[...]
        o1, o2, o3, o4, o5 = AQ, AQ + AKV, AQ + 2 * AKV, AQ + 2 * AKV + BW, AQ + 2 * AKV + 2 * BW
        qa_ref[...] = (_rope(proj[:, :o1], c, s1, s2) * QSCALE).astype(bf16)
        ka_ref[...] = _per_query_head(_rope(proj[:, o1:o2], c, s1, s2)).astype(bf16)
        va_ref[...] = _per_query_head(proj[:, o2:o3]).astype(bf16)
        qb = _rope(proj[:, o3:o4], c, s1, s2) * QSCALE
        kb = _rope(proj[:, o4:o5], c, s1, s2)
        vb = proj[:, o5:]
        for val, r1, r4, r16 in ((qb, q1_ref, q4_ref, q16_ref), (kb, k1_ref, k4_ref, k16_ref), (vb, v1_ref, v4_ref, v16_ref)):
            r1[...] = val.astype(bf16)
            _perm_store(val, scr, r4, 4)
            _perm_store(val, scr, r16, 16)

    nat = lambda w: jax.ShapeDtypeStruct((BL, SEQ, w), bf16)
    p4 = jax.ShapeDtypeStruct((BL, 4, SEQ // 4, BW), bf16)
    p16 = jax.ShapeDtypeStruct((BL, 16, SEQ // 16, BW), bf16)
    return pl.pallas_call(
        body, name="attn_in", grid=(BL, NJ),
        in_specs=[_tok(D), MOD_SPEC, _full((1, D)), _full((NCHIP, D, INW // NCHIP)), _tok(LANES), _tok(LANES), _tok(LANES)],
        out_specs=([_tok(D), _tok(AQ), _tok(2 * AKV), _tok(2 * AKV)] + [_tok(BW)] * 3 + [_perm_spec(4, BW)] * 3 + [_perm_spec(16, BW)] * 3
                   + [_full((D, INW))]),
        out_shape=[nat(D), nat(AQ), nat(2 * AKV), nat(2 * AKV)] + [nat(BW)] * 3 + [p4] * 3 + [p16] * 3
                  + [jax.ShapeDtypeStruct((D, INW), bf16)],
        scratch_shapes=[pltpu.VMEM((BW // LANES, TM, LANES), f32)],
        compiler_params=_cp(("arbitrary", "arbitrary")),
    )(x, mod, g_pre, w_in, tc, ts1, ts2)


def _kv_cat(cur_ref, prev_ref, p, cache):
    key = (id(cur_ref), p)
    if key not in cache:
        sl = slice(LANES * p, LANES * (p + 1))
        cache[key] = cur_ref[:, sl] if prev_ref is None else jnp.concatenate([prev_ref[:, sl], cur_ref[:, sl]], axis=0)
    return cache[key]


def _lane_half(a, hh):
    lo = lax.broadcasted_iota(jnp.int32, a.shape, 1) < HD
    return jnp.where(lo, a, jnp.zeros_like(a)) if hh == 0 else jnp.where(lo, jnp.zeros_like(a), a)


ATT_UNITS = 4


def _att_units(nb):
    return ATT_UNITS if nb == 1 else min(ATT_UNITS, nb)


def _attn_specs(n, nb, descending):
    u = _att_units(nb)
    if nb == 1:
        return (lambda ww: pl.BlockSpec((u, BLK, ww), lambda a, i: (a, 0, 0))), None, (n // u, 1)
    steps = nb // u
    at = (lambda i: steps - 1 - i) if descending else (lambda i: i)
    cur = lambda ww: pl.BlockSpec((None, u * BLK, ww), lambda a, i: (a, at(i), 0))
    prev = lambda ww: pl.BlockSpec((None, BLK, ww), lambda a, i: (a, jnp.maximum(u * at(i) - 1, 0), 0))
    return cur, prev, (n, steps)


def _attn_fwd(q, k, v, sink, *, max_dist, o_dtype, name):
    n, l, w = q.shape
    wk = k.shape[-1]
    nb = l // BLK
    has_sink = sink is not None

    def body(*refs):
        sink_ref = None
        if has_sink:
            sink_ref, refs = refs[0], refs[1:]
        if nb > 1:
            q_ref, kc_ref, kp_ref, vc_ref, vp_ref, o_ref, lse_ref = refs[:7]
            first = pl.program_id(1) == 0
            for u in range(_att_units(nb)):
                rows, before = pl.ds(BLK * u, BLK), pl.ds(BLK * (u - 1), BLK)
                unit(q_ref.at[rows, :], kc_ref.at[rows, :], kp_ref if u == 0 else kc_ref.at[before, :],
                     vc_ref.at[rows, :], vp_ref if u == 0 else vc_ref.at[before, :], o_ref.at[rows, :], lse_ref.at[rows, :],
                     jnp.logical_not(first) if u == 0 else True, sink_ref, *refs[7:])
        else:
            q_ref, kc_ref, vc_ref, o_ref, lse_ref = refs[:5]
            for u in range(_att_units(nb)):
                unit(q_ref.at[u], kc_ref.at[u], None, vc_ref.at[u], None, o_ref.at[u], lse_ref.at[u], None, sink_ref, *refs[5:])

    def unit(q_ref, kc_ref, kp_ref, vc_ref, vp_ref, o_ref, lse_ref, has_prev, sink_ref, sscr, pscr, dscr):
        qi = lax.broadcasted_iota(jnp.int32, (BLK, BLK), 0)
        kj = lax.broadcasted_iota(jnp.int32, (BLK, BLK), 1)
        tri = kj <= qi
        eye = kj == qi
        cache = {}
        for p in range(w // LANES):
            qpair = q_ref[:, LANES * p:LANES * (p + 1)]
            kcat = _kv_cat(kc_ref, kp_ref, p // share, cache)
            for hh in range(2):
                s = _dot_nt(_lane_half(qpair, hh), kcat)
                if nb > 1:
                    sp = s[:, :BLK] if has_prev is True else jnp.where(has_prev, s[:, :BLK], NEG)
                    sscr[2 * p + hh] = jnp.where(tri, s[:, BLK:], sp)
                    if diag:
                        dscr[2 * p + hh] = jnp.where(eye, sp, NEG)
                else:
                    sscr[2 * p + hh] = jnp.where(tri, s, NEG)
        lane = lax.broadcasted_iota(jnp.int32, (BLK, LANES), 1)
        lse_all = jnp.zeros((BLK, LANES), f32)
        for p in range(w // LANES):
            for hh in range(2):
                h = 2 * p + hh
                comb = sscr[h]
                if diag:
                    dtile = dscr[h]
                    m = jnp.max(jnp.maximum(comb, dtile), axis=-1, keepdims=True)
                else:
                    m = jnp.max(comb, axis=-1, keepdims=True)
                if has_sink:
                    sk = sink_ref[0, h]
                    m = jnp.maximum(m, sk)
                e = jnp.exp(comb - m)
                if diag:
                    ed = jnp.exp(dtile - m)
                    den = jnp.sum(e + ed, axis=-1, keepdims=True)
                else:
                    den = jnp.sum(e, axis=-1, keepdims=True)
                if has_sink:
                    den = den + jnp.exp(sk - m)
                inv = 1.0 / den
                if nb > 1:
                    pscr[h, :, :BLK] = (jnp.where(tri, ed if diag else 0.0, e) * inv).astype(bf16)
                    pscr[h, :, BLK:] = (jnp.where(tri, e, 0.0) * inv).astype(bf16)
                else:
                    pscr[h] = (e * inv).astype(bf16)
                lse_all = jnp.where(lane == h, jnp.broadcast_to(m + jnp.log(den), (BLK, LANES)), lse_all)
        lse_ref[...] = lse_all
        for p in range(w // LANES):
            vcat = _kv_cat(vc_ref, vp_ref, p // share, cache)
            o_ref[:, LANES * p:LANES * (p + 1)] = (_dot(pscr[2 * p], _lane_half(vcat, 0))
                                                   + _dot(pscr[2 * p + 1], _lane_half(vcat, 1))).astype(o_ref.dtype)

    assert max_dist in (BLK - 1, BLK) and w % wk == 0
    share = w // wk
    diag = nb > 1 and max_dist == BLK
    cur, prev, grid = _attn_specs(n, nb, False)
    in_specs = [cur(w), cur(wk)] + ([prev(wk)] if nb > 1 else []) + [cur(wk)] + ([prev(wk)] if nb > 1 else [])
    args = [q, k] + ([k] if nb > 1 else []) + [v] + ([v] if nb > 1 else [])
    if has_sink:
        in_specs = [pl.BlockSpec(memory_space=pltpu.SMEM)] + in_specs
        args = [sink] + args
    return pl.pallas_call(
        body, name=name, grid=grid, in_specs=in_specs,
        out_specs=[cur(w), cur(LANES)],
        out_shape=[jax.ShapeDtypeStruct((n, l, w), o_dtype), jax.ShapeDtypeStruct((n, l, LANES), f32)],
        scratch_shapes=[pltpu.VMEM((w // HD, BLK, BLK), f32), pltpu.VMEM((w // HD, BLK, 2 * BLK if nb > 1 else BLK), bf16),
                        pltpu.VMEM((w // HD if diag else 1, BLK, BLK), f32)],
        compiler_params=_cp(("arbitrary", "arbitrary")),
    )(*args)


def _attn_bwd(q, k, v, do, delta, lse, sink, *, max_dist, name):
    n, l, w = q.shape
    wk = k.shape[-1]
    nb = l // BLK
    has_sink = sink is not None

    def body(*refs):
        sink_ref = dsink_ref = ck = cv = None
        if has_sink:
            sink_ref, refs = refs[0], refs[1:]
        nin = 8 if nb > 1 else 6
        ins, rest = refs[:nin], refs[nin:]
        if has_sink:
            dq_ref, dk_ref, dv_ref, dsink_ref = rest[:4]
            rest = rest[4:]
        else:
            dq_ref, dk_ref, dv_ref = rest[:3]
            rest = rest[3:]
        step = pl.program_id(1)
        if has_sink:
            @pl.when((pl.program_id(0) == 0) & (step == 0))
            def _():
                dsink_ref[...] = jnp.zeros_like(dsink_ref)

        if nb > 1:
            q_ref, kc_ref, kp_ref, vc_ref, vp_ref, do_ref, delta_ref, lse_ref = ins
            ck, cv = rest[:2]

            @pl.when(step == 0)
            def _():
                ck[...] = jnp.zeros_like(ck)
                cv[...] = jnp.zeros_like(cv)

            last = step == nb // _att_units(nb) - 1
            for u in reversed(range(_att_units(nb))):
                rows, before = pl.ds(BLK * u, BLK), pl.ds(BLK * (u - 1), BLK)
                unit(q_ref.at[rows, :], kc_ref.at[rows, :], kp_ref if u == 0 else kc_ref.at[before, :],
                     vc_ref.at[rows, :], vp_ref if u == 0 else vc_ref.at[before, :], do_ref.at[rows, :],
                     delta_ref.at[rows, :], lse_ref.at[rows, :], dq_ref.at[rows, :], dk_ref.at[rows, :], dv_ref.at[rows, :],
                     jnp.logical_not(last) if u == 0 else True, sink_ref, dsink_ref, ck, cv, *rest[2:])
        else:
            q_ref, kc_ref, vc_ref, do_ref, delta_ref, lse_ref = ins
            for u in range(_att_units(nb)):
                unit(q_ref.at[u], kc_ref.at[u], None, vc_ref.at[u], None, do_ref.at[u], delta_ref.at[u], lse_ref.at[u],
                     dq_ref.at[u], dk_ref.at[u], dv_ref.at[u], None, sink_ref, dsink_ref, None, None, *rest)

    def unit(q_ref, kc_ref, kp_ref, vc_ref, vp_ref, do_ref, delta_ref, lse_ref, dq_ref, dk_ref, dv_ref, has_prev,
             sink_ref, dsink_ref, ck, cv, sscr, dpscr, pscr, dsscr, dscr=None, ddscr=None):
        lane = lax.broadcasted_iota(jnp.int32, (BLK, LANES), 1)
        qi = lax.broadcasted_iota(jnp.int32, (BLK, BLK), 0)
        kj = lax.broadcasted_iota(jnp.int32, (BLK, BLK), 1)
        tri = kj <= qi
        eye = kj == qi
        cache = {}
        kp, vp = kp_ref, vp_ref
        for p in range(w // LANES):
            sl = slice(LANES * p, LANES * (p + 1))
            qpair, dopair = q_ref[:, sl], do_ref[:, sl]
            kcat, vcat = _kv_cat(kc_ref, kp, p // share, cache), _kv_cat(vc_ref, vp, p // share, cache)
            for hh in range(2):
                h = 2 * p + hh
                s = _dot_nt(_lane_half(qpair, hh), kcat)
                dp = _dot_nt(_lane_half(dopair, hh), vcat)
                if nb > 1:
                    sp = s[:, :BLK] if has_prev is True else jnp.where(has_prev, s[:, :BLK], NEG)
                    sscr[h] = jnp.where(tri, s[:, BLK:], sp)
                    dpscr[h] = jnp.where(tri, dp[:, BLK:], dp[:, :BLK])
                    if diag:
                        dscr[h] = jnp.where(eye, sp, NEG)
                        ddscr[h] = dp[:, :BLK]
                else:
                    sscr[h] = jnp.where(tri, s, NEG)
                    dpscr[h] = dp
        for p in range(w // LANES):
            for hh in range(2):
                h = 2 * p + hh
                lse_b = jnp.broadcast_to(lse_ref[:, h:h + 1], (BLK, BLK))
                delta = jnp.broadcast_to(delta_ref[:, h:h + 1], (BLK, BLK))
                pr = jnp.exp(sscr[h] - lse_b)
                ds = pr * (dpscr[h] - delta)
                if nb > 1:
                    if diag:
                        prd = jnp.exp(dscr[h] - lse_b)
                        dsd = prd * (ddscr[h] - delta)
                    else:
                        prd = dsd = 0.0
                    pscr[h, :, :BLK] = jnp.where(tri, prd, pr).astype(bf16)
                    pscr[h, :, BLK:] = jnp.where(tri, pr, 0.0).astype(bf16)
                    dsscr[h, :, :BLK] = jnp.where(tri, dsd, ds).astype(bf16)
                    dsscr[h, :, BLK:] = jnp.where(tri, ds, 0.0).astype(bf16)
                else:
                    pscr[h] = pr.astype(bf16)
                    dsscr[h] = ds.astype(bf16)
                if has_sink:
                    dsk = -jnp.sum(jnp.where(lane == 0, jnp.exp(sink_ref[0, h] - lse_b) * delta, 0.0), keepdims=True)
                    dsink_ref[h:h + 1, :] += jnp.broadcast_to(dsk, (1, LANES))
        for p in range(w // LANES):
            sl = slice(LANES * p, LANES * (p + 1))
            qpair, dopair = q_ref[:, sl], do_ref[:, sl]
            kcat = _kv_cat(kc_ref, kp, p // share, cache)
            dq_ref[:, sl] = _dot(dsscr[2 * p], _lane_half(kcat, 0)) + _dot(dsscr[2 * p + 1], _lane_half(kcat, 1))
            dk_pair = _dot_tn(dsscr[2 * p], _lane_half(qpair, 0)) + _dot_tn(dsscr[2 * p + 1], _lane_half(qpair, 1))
            dv_pair = _dot_tn(pscr[2 * p], _lane_half(dopair, 0)) + _dot_tn(pscr[2 * p + 1], _lane_half(dopair, 1))
            if nb > 1:
                dk_ref[:, sl] = dk_pair[BLK:] + ck[:, sl]
                dv_ref[:, sl] = dv_pair[BLK:] + cv[:, sl]
                ck[:, sl] = dk_pair[:BLK]
                cv[:, sl] = dv_pair[:BLK]
            else:
                dk_ref[:, sl] = dk_pair
                dv_ref[:, sl] = dv_pair

    assert max_dist in (BLK - 1, BLK) and w % wk == 0
    share = w // wk
    diag = nb > 1 and max_dist == BLK
    cur, prev, grid = _attn_specs(n, nb, True)
    in_specs = ([cur(w), cur(wk)] + ([prev(wk)] if nb > 1 else []) + [cur(wk)] + ([prev(wk)] if nb > 1 else [])
                + [cur(w), cur(LANES), cur(LANES)])
    args = [q, k] + ([k] if nb > 1 else []) + [v] + ([v] if nb > 1 else []) + [do, delta, lse]
    out_specs = [cur(w)] * 3
    out_shape = [jax.ShapeDtypeStruct((n, l, w), f32)] * 3
    if has_sink:
        in_specs = [pl.BlockSpec(memory_space=pltpu.SMEM)] + in_specs
        args = [sink] + args
        out_specs.append(pl.BlockSpec((NHEAD, LANES), lambda a, i: (0, 0)))
        out_shape.append(jax.ShapeDtypeStruct((NHEAD, LANES), f32))
    nh = w // HD
    scratch = [pltpu.VMEM((BLK, w), f32), pltpu.VMEM((BLK, w), f32)] if nb > 1 else []
    scratch += [pltpu.VMEM((nh, BLK, BLK), f32)] * 2 + [pltpu.VMEM((nh, BLK, 2 * BLK if nb > 1 else BLK), bf16)] * 2
    if diag:
        scratch += [pltpu.VMEM((nh, BLK, BLK), f32)] * 2
    return pl.pallas_call(
        body, name=name, grid=grid, in_specs=in_specs, out_specs=out_specs, out_shape=out_shape,
        scratch_shapes=scratch, compiler_params=_cp(("arbitrary", "arbitrary")),
    )(*args)


def _split2(x):
    hi = x.astype(bf16)
    return hi, (x - hi.astype(f32)).astype(bf16)


def _heads_to_lanes(xc, e):
    return sum(_dot(t, e) for t in _split2(xc))


def _lanes_to_heads(x, g):
    return sum(_dot(t, g) for t in _split2(x))


HEAD_EXPAND = (np.arange(LANES)[:, None] == np.arange(BW)[None, :] // HD).astype(np.float32)
HEAD_SUM = HEAD_EXPAND.T.copy()


def _branch_weights(l1_ref, l4_ref, l16_ref, scr):
    l4v = _perm_load(l4_ref, scr, 4)
    l16v = _perm_load(l16_ref, scr, 16)
    l1v = l1_ref[...]
    m = jnp.maximum(jnp.maximum(l1v, l4v), l16v)
    e1, e4, e16 = jnp.exp(l1v - m), jnp.exp(l4v - m), jnp.exp(l16v - m)
    z = e1 + e4 + e16
    return e1 / z, e4 / z, e16 / z


def _mix_out(oa, o1, l1, o4, l4, o16, l16, g_mix_a, g_mix_b, w_out, x, mod, g_post):
    def body(oa_ref, o1_ref, l1_ref, o4_ref, l4_ref, o16_ref, l16_ref, ga_ref, gb_ref, w_ref, x_ref, mod_ref, gp_ref, e_ref,
             x1_ref, y_ref, mixed_ref, ob_ref, scr):
        w1, w4, w16 = _branch_weights(l1_ref, l4_ref, l16_ref, scr)
        e = e_ref[...]
        x1w, x4w = _heads_to_lanes(w1, e), _heads_to_lanes(w4, e)
        ob = (x1w * o1_ref[...].astype(f32) + x4w * _perm_load(o4_ref, scr, 4)
              + (1.0 - x1w - x4w) * _perm_load(o16_ref, scr, 16))
        ob_ref[...] = ob
        oan, _ = _rms(oa_ref[...])
        obn, _ = _rms(ob)
        mixed = jnp.concatenate([oan * ga_ref[...], obn * gb_ref[...]], axis=1).astype(bf16)
        mixed_ref[...] = mixed
        y = _dot(mixed, w_ref[...])
        y_ref[...] = y
        yn, _ = _rms(y)
        x1_ref[...] = x_ref[...] + mod_ref[2:3, :] * (yn * gp_ref[...])

    nat = lambda w, dt: jax.ShapeDtypeStruct((BL, SEQ, w), dt)
    return pl.pallas_call(
        body, name="mix_out", grid=(BL, NJ),
        in_specs=[_tok(AQ), _tok(BW), _tok(LANES), _perm_spec(4, BW), _perm_spec(4, LANES), _perm_spec(16, BW),
                  _perm_spec(16, LANES), _full((1, AQ)), _full((1, BW)), _full((D, D)), _tok(D), MOD_SPEC, _full((1, D)),
                  _full((LANES, BW))],
        out_specs=[_tok(D), _tok(D), _tok(D), _tok(BW)],
        out_shape=[nat(D, f32), nat(D, f32), nat(D, bf16), nat(BW, f32)],
        scratch_shapes=[pltpu.VMEM((BW // LANES, TM, LANES), f32)],
        compiler_params=_cp(("arbitrary", "arbitrary")),
    )(oa, o1, l1, o4, l4, o16, l16, g_mix_a, g_mix_b, w_out, x, mod, g_post, jnp.asarray(HEAD_EXPAND, bf16))


def _mlp_up(x1, mod, g_pre, w_up):
    def body(x_ref, mod_ref, g_ref, w_ref, h_ref, u_ref, a_ref):
        xn, _ = _rms(x_ref[...])
        h = (xn * g_ref[...]) * (1.0 + mod_ref[4:5, :]) + mod_ref[3:4, :]
        hb = h.astype(bf16)
        h_ref[...] = hb
        for s in range(NCHIP):
            u = _dot(hb, w_ref[s])
            u_ref[:, D * s:D * (s + 1)] = u.astype(bf16)
            a_ref[:, D * s:D * (s + 1)] = jnp.square(jnp.maximum(u, 0.0)).astype(bf16)

    nat = lambda w: jax.ShapeDtypeStruct((BL, SEQ, w), bf16)
    return pl.pallas_call(
        body, name="mlp_up", grid=(BL, NJ),
        in_specs=[_tok(D), MOD_SPEC, _full((1, D)), _full((NCHIP, D, D))],
        out_specs=[_tok(D), _tok(DFF), _tok(DFF)], out_shape=[nat(D), nat(DFF), nat(DFF)],
        compiler_params=_cp(("arbitrary", "arbitrary")),
    )(x1, mod, g_pre, w_up)


def _mlp_down(a, w_down, x1, target, mod, g_post):
    def body(a_ref, w_ref, x_ref, t_ref, mod_ref, g_ref, gx_ref, dy_ref, accb_ref, accg_ref):
        _acc_init(accb_ref, accg_ref)
        y2 = _dot(a_ref[...], w_ref[...])
        yn, r = _rms(y2)
        g = g_ref[...]
        gt = mod_ref[5:6, :]
        n2 = yn * g
        err = x_ref[...] + gt * n2 - t_ref[...]
        gout = err * (1.0 / D)
        gx_ref[...] = gout
        dn2 = gout * gt
        dy_ref[...] = _rms_bwd(dn2 * g, yn, r).astype(bf16)
        accb_ref[0:1, :] += _colsum(gout * n2)
        accg_ref[0:1, :] += _colsum(dn2 * yn)
        accg_ref[1:2, :] += jnp.broadcast_to(jnp.sum(err * err, keepdims=True), (1, D))

    return pl.pallas_call(
        body, name="mlp_down", grid=(BL, NJ),
        in_specs=[_tok(DFF), _full((DFF, D)), _tok(D), _tok(D), MOD_SPEC, _full((1, D))],
        out_specs=[_tok(D), _tok(D), ACCB_SPEC, ACCG_SPEC],
        out_shape=[jax.ShapeDtypeStruct((BL, SEQ, D), f32), jax.ShapeDtypeStruct((BL, SEQ, D), bf16)] + ACC_SHAPES,
        compiler_params=_cp(("arbitrary", "arbitrary")),
    )(a, w_down, x1, target, mod, g_post)


def _mlp_bwd(dy2, u, w_down, w_up, x1, gx, mod, g_pre):
    def body(dy_ref, u_ref, wd_hbm, wu_hbm, x_ref, gx_ref, mod_ref, g_ref, du_ref, gx1_ref, accb_ref, accg_ref, wd, wu, sem):
        _acc_init(accb_ref, accg_ref)
        first = (pl.program_id(0) == 0) & (pl.program_id(1) == 0)
        c1 = pltpu.make_async_copy(wd_hbm, wd, sem.at[0])
        c2 = pltpu.make_async_copy(wu_hbm, wu, sem.at[1])

        @pl.when(first)
        def _():
            c1.start()
            c2.start()
            c1.wait()

        dy = dy_ref[...]
        for s in range(NCHIP):
            sl = slice(D * s, D * (s + 1))
            da = _dot_nt(dy, wd[sl, :])
            du_ref[:, sl] = (da * (2.0 * jnp.maximum(u_ref[:, sl].astype(f32), 0.0))).astype(bf16)

        @pl.when(first)
        def _():
            c2.wait()

        dh = jnp.zeros((TM, D), f32)
        for s in range(NCHIP):
            dh = dh + _dot_nt(du_ref[:, D * s:D * (s + 1)], wu[s])
        xn, r = _rms(x_ref[...])
        g = g_ref[...]
        n = xn * g
        dn = dh * (1.0 + mod_ref[4:5, :])
        gx1_ref[...] = gx_ref[...] + _rms_bwd(dn * g, xn, r)
        accb_ref[0:1, :] += _colsum(dh * n)
        accb_ref[1:2, :] += _colsum(dh)
        accg_ref[0:1, :] += _colsum(dn * xn)

    anyspec = pl.BlockSpec(memory_space=pl.ANY)
    return pl.pallas_call(
        body, name="mlp_bwd", grid=(BL, NJ),
        in_specs=[_tok(D), _tok(DFF), anyspec, anyspec, _tok(D), _tok(D), MOD_SPEC, _full((1, D))],
        out_specs=[_tok(DFF), _tok(D), ACCB_SPEC, ACCG_SPEC],
        out_shape=[jax.ShapeDtypeStruct((BL, SEQ, DFF), bf16), jax.ShapeDtypeStruct((BL, SEQ, D), f32)] + ACC_SHAPES,
        scratch_shapes=[pltpu.VMEM((DFF, D), bf16), pltpu.VMEM((NCHIP, D, D), bf16), pltpu.SemaphoreType.DMA((2,))],
        compiler_params=_cp(("arbitrary", "arbitrary")),
    )(dy2, u, w_down, w_up, x1, gx, mod, g_pre)


def _matmul_tn(a, b, *, tn, col_blocked, name, out_dtype=f32):
    t, m = a.shape
    n = b.shape[1]
    tmm = min(m, 1024)
    tk = 2048 if tn <= 1024 else 1024
    nk = t // tk

    def body(a_ref, b_ref, o_ref, acc):
        k = pl.program_id(2)

        @pl.when(k == 0)
        def _():
            acc[...] = jnp.zeros_like(acc)

        acc[...] += _dot_tn(a_ref[...], b_ref[...])

        @pl.when(k == nk - 1)
        def _():
            o_ref[...] = acc[...].astype(out_dtype)

    if col_blocked:
        out_spec = pl.BlockSpec((None, tmm, tn), lambda i, j, k: (j, i, 0))
        out_shape = jax.ShapeDtypeStruct((n // tn, m, tn), out_dtype)
    else:
        out_spec = pl.BlockSpec((tmm, tn), lambda i, j, k: (i, j))
        out_shape = jax.ShapeDtypeStruct((m, n), out_dtype)
    return pl.pallas_call(
        body, name=name, grid=(m // tmm, n // tn, nk),
        in_specs=[pl.BlockSpec((tk, tmm), lambda i, j, k: (k, i)), pl.BlockSpec((tk, tn), lambda i, j, k: (k, j))],
        out_specs=out_spec, out_shape=out_shape, scratch_shapes=[pltpu.VMEM((tmm, tn), f32)],
        compiler_params=_cp(("arbitrary", "arbitrary", "arbitrary")),
    )(a, b)


def _grad_w_in(h, dproj):
    t = h.shape[0]
    tk = 1024
    nk = t // tk
    sw = INW // NCHIP

    def body(a_ref, b_ref, o_ref, acc):
        k = pl.program_id(0)

        @pl.when(k == 0)
        def _():
            acc[...] = jnp.zeros_like(acc)

        acc[...] += _dot_tn(a_ref[...], b_ref[...])

        @pl.when(k == nk - 1)
        def _():
            for s in range(NCHIP):
                o_ref[s] = acc[:, sw * s:sw * (s + 1)].astype(bf16)

    return pl.pallas_call(
        body, name="grad_w_in", grid=(nk,),
        in_specs=[pl.BlockSpec((tk, D), lambda k: (k, 0)), pl.BlockSpec((tk, INW), lambda k: (k, 0))],
        out_specs=pl.BlockSpec((NCHIP, D, sw), lambda k: (0, 0, 0)), out_shape=jax.ShapeDtypeStruct((NCHIP, D, sw), bf16),
        scratch_shapes=[pltpu.VMEM((D, INW), f32)], compiler_params=_cp(("arbitrary",)),
    )(h, dproj)


def _attn_out_bwd(gx1, y, mod, g_post, w_out, oa, ob, g_mix_a, g_mix_b, l1, l4, l16):
    def body(gx_ref, y_ref, mod_ref, gp_ref, w_ref, oa_ref, ob_ref, ga_ref, gb_ref, l1_ref, l4_ref, l16_ref, e_ref, g_ref,
             dy_ref, doa_ref, do1_ref, do4_ref, do16_ref, da_ref, d1_ref, d4_ref, d16_ref, accb_ref, accg_ref, scr):
        _acc_init(accb_ref, accg_ref)
        w1, w4, w16 = _branch_weights(l1_ref, l4_ref, l16_ref, scr)
        e, hs = e_ref[...], g_ref[...]
        gx1v = gx_ref[...]
        yn, ry = _rms(y_ref[...])
        gp = gp_ref[...]
        gt = mod_ref[2:3, :]
        dn1 = gx1v * gt
        dy = _rms_bwd(dn1 * gp, yn, ry).astype(bf16)
        dy_ref[...] = dy
        dmixed = _dot_nt(dy, w_ref[...])
        dma, dmb = dmixed[:, :AQ], dmixed[:, AQ:]
        oa, ob = oa_ref[...], ob_ref[...]
        oan, ra = _rms(oa)
        obn, rb = _rms(ob)
        doa = _rms_bwd(dma * ga_ref[...], oan, ra)
        doa_ref[...] = doa.astype(bf16)
        da_ref[...] = _lanes_to_heads(doa * oa, hs)
        dob = _rms_bwd(dmb * gb_ref[...], obn, rb)
        dd = _lanes_to_heads(dob * ob, hs)
        x1w, x4w = _heads_to_lanes(w1, e), _heads_to_lanes(w4, e)
        do1_ref[...] = (x1w * dob).astype(bf16)
        d1_ref[...] = w1 * dd
        _perm_store(x4w * dob, scr, do4_ref, 4)
        _perm_store(w4 * dd, scr, d4_ref, 4)
        _perm_store((1.0 - x1w - x4w) * dob, scr, do16_ref, 16)
        _perm_store(w16 * dd, scr, d16_ref, 16)
        accb_ref[0:1, :] += _colsum(gx1v * (yn * gp))
        accg_ref[0:1, :] += _colsum(dn1 * yn)
        accg_ref[1:2, :] += jnp.concatenate([_colsum(dma * oan), _colsum(dmb * obn)], axis=1)

    nat = lambda w, dt: jax.ShapeDtypeStruct((BL, SEQ, w), dt)
    return pl.pallas_call(
        body, name="attn_out_bwd", grid=(BL, NJ),
        in_specs=[_tok(D), _tok(D), MOD_SPEC, _full((1, D)), _full((D, D)), _tok(AQ), _tok(BW), _full((1, AQ)), _full((1, BW)),
                  _tok(LANES), _perm_spec(4, LANES), _perm_spec(16, LANES), _full((LANES, BW)), _full((BW, LANES))],
        out_specs=[_tok(D), _tok(AQ), _tok(BW), _perm_spec(4, BW), _perm_spec(16, BW),
                   _tok(LANES), _tok(LANES), _perm_spec(4, LANES), _perm_spec(16, LANES), ACCB_SPEC, ACCG_SPEC],
        out_shape=[nat(D, bf16), nat(AQ, bf16), nat(BW, bf16), jax.ShapeDtypeStruct((BL, 4, SEQ // 4, BW), bf16),
                   jax.ShapeDtypeStruct((BL, 16, SEQ // 16, BW), bf16), nat(LANES, f32), nat(LANES, f32),
                   jax.ShapeDtypeStruct((BL, 4, SEQ // 4, LANES), f32), jax.ShapeDtypeStruct((BL, 16, SEQ // 16, LANES), f32)]
                  + ACC_SHAPES,
        scratch_shapes=[pltpu.VMEM((BW // LANES, TM, LANES), f32)],
        compiler_params=_cp(("arbitrary", "arbitrary")),
    )(gx1, y, mod, g_post, w_out, oa, ob, g_mix_a, g_mix_b, l1, l4, l16, jnp.asarray(HEAD_EXPAND, bf16),
      jnp.asarray(HEAD_SUM, bf16))


def _attn_in_bwd(dqa, dka, dva, d1, d4, d16, tc, ts1, ts2, w_in, x, gx1, mod, g_pre):
    def body(dqa_ref, dka_ref, dva_ref, dq1_ref, dk1_ref, dv1_ref, dq4_ref, dk4_ref, dv4_ref, dq16_ref, dk16_ref, dv16_ref,
             c_ref, s1_ref, s2_ref, w_ref, x_ref, gx_ref, mod_ref, g_ref, dproj_ref, dx_ref, accb_ref, accg_ref, scr):
        _acc_init(accb_ref, accg_ref)
        c, s1, s2 = c_ref[...], s1_ref[...], s2_ref[...]
        tot = lambda r1, r4, r16: r1[...] + _perm_load(r4, scr, 4) + _perm_load(r16, scr, 16)
        dqb = tot(dq1_ref, dq4_ref, dq16_ref)
        dkb = tot(dk1_ref, dk4_ref, dk16_ref)
        dvb = tot(dv1_ref, dv4_ref, dv16_ref)
        dproj = jnp.concatenate([
            _rope_t(dqa_ref[...], c, s1, s2) * QSCALE, _rope_t(_per_kv_head(dka_ref[...]), c, s1, s2),
            _per_kv_head(dva_ref[...]),
            _rope_t(dqb, c, s1, s2) * QSCALE, _rope_t(dkb, c, s1, s2), dvb], axis=1).astype(bf16)
        dproj_ref[...] = dproj
        dh = _dot_nt(dproj, w_ref[...])
        xn, r = _rms(x_ref[...])
        g = g_ref[...]
        dn = dh * (1.0 + mod_ref[1:2, :])
        dx_ref[...] = gx_ref[...] + _rms_bwd(dn * g, xn, r)
        accb_ref[0:1, :] += _colsum(dh * (xn * g))
        accb_ref[1:2, :] += _colsum(dh)
        accg_ref[0:1, :] += _colsum(dn * xn)

    return pl.pallas_call(
        body, name="attn_in_bwd", grid=(BL, NJ),
        in_specs=[_tok(AQ), _tok(AQ), _tok(AQ)] + [_tok(BW)] * 3 + [_perm_spec(4, BW)] * 3 + [_perm_spec(16, BW)] * 3
                 + [_tok(LANES)] * 3 + [_full((D, INW)), _tok(D), _tok(D), MOD_SPEC, _full((1, D))],
        out_specs=[_tok(INW), _tok(D), ACCB_SPEC, ACCG_SPEC],
        out_shape=[jax.ShapeDtypeStruct((BL, SEQ, INW), bf16), jax.ShapeDtypeStruct((BL, SEQ, D), f32)] + ACC_SHAPES,
        scratch_shapes=[pltpu.VMEM((BW // LANES, TM, LANES), f32)],
        compiler_params=_cp(("arbitrary", "arbitrary")),
    )(dqa, dka, dva, *d1, *d4, *d16, tc, ts1, ts2, w_in, x, gx1, mod, g_pre)


def _inv_lane():
    inv = np.float32(THETA) ** (-np.arange(0, ROT, 2, dtype=np.float32) / np.float32(ROT))
    lane = np.arange(LANES) % HD
    return jnp.asarray(np.where(lane < ROT, inv[lane % (ROT // 2)], 0.0).astype(np.float32)[None, :])


def _local_step(x, positions, mod, target, inv_lane, first_weight, later_weights, grad_ready, g_attn_pre,
                g_attn_post, sink_a, g_mix_a, g_mix_b, g_mlp_pre, g_mlp_post):
    tabs = _rope_tables(positions.reshape(BL * SEQ, 1), inv_lane)
    w_in = first_weight(tuple(tabs))
    tc, ts1, ts2 = [t.reshape(BL, SEQ, LANES) for t in tabs]

    (h, qa, ka, va, q1, k1, v1, q4, k4, v4, q16, k16, v16, w_in) = _attn_in(x, mod, g_attn_pre, w_in, tc, ts1, ts2)
    seqs = lambda t: t.reshape(t.shape[0] * t.shape[1], t.shape[2], t.shape[3])
    q4, k4, v4, q16, k16, v16 = [seqs(t) for t in (q4, k4, v4, q16, k16, v16)]
    oa, la = _attn_fwd(qa, ka, va, sink_a, max_dist=BLK - 1, o_dtype=f32, name="attn_a_fwd")
    o1, l1 = _attn_fwd(q1, k1, v1, None, max_dist=BLK, o_dtype=bf16, name="attn_b1_fwd")
    o4, l4 = _attn_fwd(q4, k4, v4, None, max_dist=BLK, o_dtype=bf16, name="attn_b4_fwd")
    o16, l16 = _attn_fwd(q16, k16, v16, None, max_dist=BLK, o_dtype=bf16, name="attn_b16_fwd")
    b4 = lambda t: t.reshape(BL, 4, SEQ // 4, t.shape[-1])
    b16 = lambda t: t.reshape(BL, 16, SEQ // 16, t.shape[-1])
    w_out, mlp_weights, mod = later_weights((oa, o1, o4, o16), mod)
    x1, y, mixed, ob = _mix_out(oa, o1, l1, b4(o4), b4(l4), b16(o16), b16(l16), g_mix_a, g_mix_b, w_out, x, mod, g_attn_post)
    w_up, w_down = mlp_weights((x1,))
    h2, u, a = _mlp_up(x1, mod, g_mlp_pre, w_up)
    gx, dy2, accb_d, accg_d = _mlp_down(a, w_down, x1, target, mod, g_mlp_post)

    flat = lambda t: t.reshape(BL * SEQ, t.shape[-1])
    mod = grad_ready("w_down", _matmul_tn(flat(a), flat(dy2), tn=D, col_blocked=False, name="grad_w_down", out_dtype=bf16), mod)
    du, gx1, accb_m, accg_m = _mlp_bwd(dy2, u, w_down, w_up, x1, gx, mod, g_mlp_pre)
    mod = grad_ready("w_up", _matmul_tn(flat(h2), flat(du), tn=D, col_blocked=True, name="grad_w_up", out_dtype=bf16), mod)

    dy, doa, do1, do4, do16, da, dl1, dl4, dl16, accb_o, accg_o = _attn_out_bwd(
        gx1, y, mod, g_attn_post, w_out, oa, ob, g_mix_a, g_mix_b, l1, b4(l4), b16(l16))
    sink_behind = grad_ready("w_out", _matmul_tn(flat(mixed), flat(dy), tn=D, col_blocked=False, name="grad_w_out",
                                                  out_dtype=bf16), sink_a)
    dqa, dka, dva, dsink = _attn_bwd(qa, ka, va, doa, da, la, sink_behind, max_dist=BLK - 1, name="attn_a_bwd")
    d1 = _attn_bwd(q1, k1, v1, do1, dl1, l1, None, max_dist=BLK, name="attn_b1_bwd")
    d4 = _attn_bwd(q4, k4, v4, seqs(do4), seqs(dl4), l4, None, max_dist=BLK, name="attn_b4_bwd")
    d16 = _attn_bwd(q16, k16, v16, seqs(do16), seqs(dl16), l16, None, max_dist=BLK, name="attn_b16_bwd")
    dproj, grad_x, accb_i, accg_i = _attn_in_bwd(dqa, dka, dva, d1, [b4(t) for t in d4], [b16(t) for t in d16],
                                                 tc, ts1, ts2, w_in, x, gx1, mod, g_attn_pre)
    gw_in = _grad_w_in(flat(h), flat(dproj))
    dsink = grad_ready("w_in", gw_in, dsink)

    return grad_x, (accb_i, accb_o, accb_m, accb_d, accg_i, accg_o, accg_m, accg_d, dsink)


ADAW = NMOD * D // NCHIP


def _pos():
    return lax.axis_index("x"), lax.axis_index("y"), lax.axis_index("c")


def _flip(v, bit):
    return 1 - v if bit else v


def _all_peers(x, y, c):
    return [(_flip(x, k >> 2 & 1), _flip(y, k >> 1 & 1), _flip(c, k & 1)) for k in range(1, NDEV)]


def _other_chips(x, y):
    return [(1 - x, y), (x, 1 - y), (1 - x, 1 - y)]


def _rcopy(src, dst, send, recv, k, dev, k_recv=None):
    return pltpu.make_async_remote_copy(src_ref=src, dst_ref=dst, send_sem=send.at[k],
                                        recv_sem=recv.at[k if k_recv is None else k_recv],
                                        device_id=dev, device_id_type=MESH)


def _gather_small(src, buf, send, recv):
    x, y, c = _pos()
    me = 4 * x + 2 * y + c
    peers = _all_peers(x, y, c)
    sends = [_rcopy(src, buf.at[me], send, recv, k, p) for k, p in enumerate(peers)]
    for cp in sends:
        cp.start()
    for k, (px, py, pc) in enumerate(peers):
        _rcopy(src, buf.at[4 * px + 2 * py + pc], send, recv, k, (px, py, pc)).wait_recv()
    for cp in sends:
        cp.wait_send()
    return me


def _ada_fwd(c_in, w_ada, b_cols):
    def body(c_ref, w_hbm, b_ref, mod_ref, cond_ref, cbuf, mbuf, w_ref, s1, r1, s2, r2, wsem):
        x, y, c = _pos()
        chip = 2 * x + y
        wcopy = pltpu.make_async_copy(w_hbm, w_ref, wsem)
        wcopy.start()
        me = _gather_small(c_ref, cbuf, s1, r1)
        cbuf[me] = c_ref[...]
        for i in range(NDEV):
            cond_ref[BL * i:BL * (i + 1), :] = cbuf[i]
        call = cond_ref[...]
        cond = call / (1.0 + jnp.exp(-call))
        cond_ref[...] = cond
        wcopy.wait()
        mbuf[chip] = _dot(cond.astype(bf16), w_ref[...].astype(bf16)) + b_ref[...]
        chips = _other_chips(x, y)
        sends = [_rcopy(mbuf.at[chip], mbuf.at[chip], s2, r2, j, (px, py, c)) for j, (px, py) in enumerate(chips)]
        for cp in sends:
            cp.start()
        for j, (px, py) in enumerate(chips):
            _rcopy(mbuf.at[chip], mbuf.at[2 * px + py], s2, r2, j, (px, py, c)).wait_recv()
        for cp in sends:
            cp.wait_send()
        row = lax.broadcasted_iota(jnp.int32, (BL * NDEV, ADAW), 0)
        for s in range(NCHIP):
            slab = mbuf[s]
            for j in range(BL):
                mod_ref[j:j + 1, ADAW * s:ADAW * (s + 1)] = jnp.sum(jnp.where(row == BL * me + j, slab, 0.0), axis=0, keepdims=True)

    vm = pl.BlockSpec(memory_space=pltpu.VMEM)
    return pl.pallas_call(
        body, name="ada_fwd", in_specs=[vm, pl.BlockSpec(memory_space=pl.ANY), vm], out_specs=[vm, vm],
        out_shape=[jax.ShapeDtypeStruct((BL, NMOD * D), f32), jax.ShapeDtypeStruct((BL * NDEV, D), f32)],
        scratch_shapes=[pltpu.VMEM((NDEV, BL, D), f32), pltpu.VMEM((NCHIP, BL * NDEV, ADAW), f32),
                        pltpu.VMEM((D, ADAW), f32),
                        pltpu.SemaphoreType.DMA((NDEV - 1,)), pltpu.SemaphoreType.DMA((NDEV - 1,)),
                        pltpu.SemaphoreType.DMA((NCHIP - 1,)), pltpu.SemaphoreType.DMA((NCHIP - 1,)),
                        pltpu.SemaphoreType.DMA],
        compiler_params=pltpu.CompilerParams(vmem_limit_bytes=VMEM_LIMIT),
    )(c_in, w_ada, b_cols)


def _small_allreduce(accs, cond_all):
    def body(bi, bo, bm, bd, gi, go, gm, gd, dsink, cond_ref, gw_ref, gb_ref, small_ref, pay, pbuf, dall, s1, r1):
        x, y, c = _pos()
        chip = 2 * x + y
        pay[...] = jnp.zeros_like(pay)
        for b in range(BL):
            for k, (ref, r) in enumerate(((bi, 1), (bi, 0), (bo, 0), (bm, 1), (bm, 0), (bd, 0))):
                pay[b:b + 1, D * k:D * (k + 1)] = ref[b, r:r + 1, :]
        for off, ref, r in ((OFF_G_ATTN_PRE, gi, 0), (OFF_G_ATTN_POST, go, 0), (OFF_G_MIX_A, go, 1), (OFF_G_MLP_PRE, gm, 0),
                            (OFF_G_MLP_POST, gd, 0)):
            pay[BL:BL + 1, off:off + D] = ref[r:r + 1, :]
        eye = lax.broadcasted_iota(jnp.int32, (NHEAD, LANES), 0) == lax.broadcasted_iota(jnp.int32, (NHEAD, LANES), 1)
        pay[BL:BL + 1, OFF_SINK:OFF_SINK + LANES] = jnp.sum(jnp.where(eye, dsink[...], 0.0), axis=0, keepdims=True)
        pay[BL:BL + 1, OFF_LOSS:OFF_LOSS + LANES] = gd[1:2, 0:LANES]
        me = _gather_small(pay, pbuf, s1, r1)
        pbuf[me] = pay[...]
        small = pbuf[0, BL:BL + 1, :]
        for i in range(1, NDEV):
            small = small + pbuf[i, BL:BL + 1, :]
        small_ref[...] = small
        for i in range(NDEV):
            dall[BL * i:BL * (i + 1), :] = pbuf[i, 0:BL, :]
        gb_ref[...] = jnp.sum(dall[...], axis=0, keepdims=True)
        cols = jnp.zeros((BL * NDEV, ADAW), f32)
        for s in range(NCHIP):
            cols = cols + jnp.where(chip == s, dall[:, ADAW * s:ADAW * (s + 1)], 0.0)
        gw_ref[...] = _dot_tn(cond_ref[...].astype(bf16), cols.astype(bf16))

    vm = pl.BlockSpec(memory_space=pltpu.VMEM)
    return pl.pallas_call(
        body, name="small_allreduce", in_specs=[vm] * 10, out_specs=[vm] * 3,
        out_shape=[jax.ShapeDtypeStruct((D, ADAW), f32), jax.ShapeDtypeStruct((1, PAYW), f32), jax.ShapeDtypeStruct((1, PAYW), f32)],
        scratch_shapes=[pltpu.VMEM((4, PAYW), f32), pltpu.VMEM((NDEV, 4, PAYW), f32), pltpu.VMEM((BL * NDEV, PAYW), f32),
                        pltpu.SemaphoreType.DMA((NDEV - 1,)), pltpu.SemaphoreType.DMA((NDEV - 1,))],
        compiler_params=pltpu.CompilerParams(vmem_limit_bytes=VMEM_LIMIT),
    )(*accs, cond_all)


def _half(ref, c):
    r2 = ref.shape[0] // 2
    return ref.at[pl.ds(c * r2 if isinstance(c, int) else pl.multiple_of(c * r2, 16), r2), :]


HBM_SPEC = pl.BlockSpec(memory_space=pltpu.HBM)
SEM_SPEC = pl.BlockSpec(memory_space=pltpu.SEMAPHORE)
EFFECT = pltpu.SideEffectType.DATAFLOW_SIDE_EFFECTING
NLINK = NCHIP - 1


def _in_hbm(a):
    return pltpu.with_memory_space_constraint(a, pltpu.HBM)


NSEM = 8


def _split_start(name, srcs, land_shapes, builds, carry, after=(), lands=None):
    n = len(srcs)
    na, nc = len(after), len(carry)

    def body(*refs):
        src, land = refs[:n], refs[n:2 * n]
        kept = refs[2 * n + na:2 * n + na + nc]
        outs = refs[2 * n + na + nc:]
        send, recv, passed = outs[:n], outs[n:2 * n], outs[4 * n:]
        for t in range(n):
            for out_cp, _ in builds[t](src[t], land[t], send[t], recv[t]):
                out_cp.start()
        for a, b in zip(kept, passed):
            b[...] = a[...]

    if lands is None:
        lands = [lax.empty(s.shape, s.dtype) for s in land_shapes]
    lands = [_in_hbm(a) for a in lands]
    sems = [pltpu.SemaphoreType.DMA((NSEM,))] * (2 * n)
    thru = [pltpu.HBM(a.shape, a.dtype) for a in list(srcs) + lands]
    vm = pl.BlockSpec(memory_space=pltpu.VMEM)
    res = pl.pallas_call(
        body, name=name, out_shape=sems + thru + [jax.ShapeDtypeStruct(a.shape, a.dtype) for a in carry],
        in_specs=[HBM_SPEC] * (2 * n) + [pl.BlockSpec(memory_space=pl.ANY)] * na + [vm] * nc,
        out_specs=[SEM_SPEC] * (2 * n) + [HBM_SPEC] * (2 * n) + [vm] * nc,
        input_output_aliases={i: 2 * n + i for i in range(2 * n)},
        compiler_params=pltpu.CompilerParams(has_side_effects=EFFECT),
    )(*[_in_hbm(a) for a in srcs], *lands, *after, *carry)
    flight = [(res[2 * n + t], res[3 * n + t], res[t], res[n + t]) for t in range(n)]
    return flight, list(res[4 * n:])


def _split_wait(name, flight, builds, after):
    m = len(flight)
    na = len(after)

    def body(*refs):
        src, land, send, recv = refs[:m], refs[m:2 * m], refs[2 * m:3 * m], refs[3 * m:4 * m]
        for t in range(m):
            for out_cp, in_cp in builds[t](src[t], land[t], send[t], recv[t]):
                out_cp.wait_send()
                in_cp.wait_recv()

    ops = [f[0] for f in flight] + [f[1] for f in flight] + [f[2] for f in flight] + [f[3] for f in flight]
    res = pl.pallas_call(
        body, name=name, out_shape=[pltpu.HBM(a.shape, a.dtype) for a in ops[:2 * m]],
        in_specs=[HBM_SPEC] * (2 * m) + [SEM_SPEC] * (2 * m) + [pl.BlockSpec(memory_space=pl.ANY)] * na,
        out_specs=[HBM_SPEC] * (2 * m), input_output_aliases={i: i for i in range(2 * m)},
        compiler_params=pltpu.CompilerParams(has_side_effects=EFFECT),
    )(*ops, *after)
    return res[:m], res[m:2 * m]


def _weight_copies(src, land, send, recv):
    x, y, c = _pos()
    chip = 2 * x + y
    return [(_rcopy(_half(src, c), _half(land.at[chip], c), send, recv, j, (px, py, c)),
             _rcopy(_half(src, c), _half(land.at[2 * px + py], c), send, recv, j, (px, py, c)))
            for j, (px, py) in enumerate(_other_chips(x, y))]


NDIRECT = NDEV - 1


def _direct_grad_copies(src, land, send, recv):
    x, y, c = _pos()
    out, arrive = [], []
    for j, (px, py) in enumerate(_other_chips(x, y)):
        for hc in range(2):
            out.append(_rcopy(_half(src.at[2 * px + py], hc), land.at[2 * j + c], send, recv, 2 * j + hc, (px, py, hc),
                              k_recv=2 * j + c))
            arrive.append(_rcopy(_half(src.at[2 * px + py], hc), land.at[2 * j + hc], send, recv, 2 * j + hc, (px, py, hc)))
    own = _rcopy(_half(src.at[2 * x + y], 1 - c), land.at[NDIRECT - 1], send, recv, NDIRECT - 1, (x, y, 1 - c))
    return list(zip(out, arrive)) + [(own, own)]


def _pair_weight_copies(src, land, send, recv):
    x, y, c = _pos()
    sib = (x, y, 1 - c)
    cps = []
    for j, (px, py) in enumerate(_other_chips(x, y)):
        mine, theirs = _half(land.at[2 * px + py], c), _half(land.at[2 * px + py], 1 - c)
        cps.append((_rcopy(mine, mine, send, recv, j, sib), _rcopy(theirs, theirs, send, recv, j, sib)))
    own = _rcopy(src, land.at[2 * x + y], send, recv, NLINK, sib)
    return cps + [(own, own)]


RS_ROWS = 128


def _chip_add(own, landed, pos_arr, name):
    nl, r2, cw = landed.shape
    nr = r2 // RS_ROWS

    def body(s_ref, h_ref, q_ref, o_ref):
        acc = h_ref[...].astype(f32)
        for j in range(nl):
            acc = acc + q_ref[j].astype(f32)
        o_ref[...] = acc

    gs = pltpu.PrefetchScalarGridSpec(
        num_scalar_prefetch=1, grid=(nr,),
        in_specs=[pl.BlockSpec((None, RS_ROWS, cw), lambda j, s: (s[0], s[1] * nr + j, 0)),
                  pl.BlockSpec((nl, RS_ROWS, cw), lambda j, s: (0, j, 0))],
        out_specs=pl.BlockSpec((RS_ROWS, cw), lambda j, s: (s[1] * nr + j, 0)))
    return pl.pallas_call(body, name=name, grid_spec=gs, out_shape=jax.ShapeDtypeStruct((2 * r2, cw), f32),
                          compiler_params=_cp(("arbitrary",)))(pos_arr, own, landed)


def _pair_gather_copies(src, land, send, recv):
    x, y, c = _pos()
    sib = (x, y, 1 - c)
    return [(_rcopy(_half(land, c), _half(land, c), send, recv, 0, sib),
             _rcopy(_half(land, 1 - c), _half(land, 1 - c), send, recv, 0, sib))]


def _adamw_math(w, g, m, v):
    m = B1 * m + (1.0 - B1) * g
    v = B2 * v + (1.0 - B2) * jnp.square(g)
    m_hat = m / (1.0 - B1 ** STEP)
    v_hat = v / (1.0 - B2 ** STEP)
    return -LR * (m_hat / (jnp.sqrt(v_hat) + AEPS) + WD * w), m, v


ADAM_ROWS = 256


def _adamw(w, g, m, v, name, after=()):
    r, cw = w.shape
    na = len(after)

    def body(w_ref, g_ref, m_ref, v_ref, *rest):
        go_ref, d_ref, mo_ref, vo_ref = rest[na:]
        g = g_ref[...]
        go_ref[...] = g
        d_ref[...], mo_ref[...], vo_ref[...] = _adamw_math(w_ref[...], g, m_ref[...], v_ref[...])

    rows = max(k for k in range(8, ADAM_ROWS + 1, 8) if r % k == 0)
    spec = pl.BlockSpec((rows, cw), lambda i: (i, 0))
    return pl.pallas_call(body, name=name, grid=(r // rows,), in_specs=[spec] * 4 + [pl.BlockSpec(memory_space=pl.ANY)] * na,
                          out_specs=[spec] * 4, out_shape=[jax.ShapeDtypeStruct((r, cw), f32)] * 4,
                          compiler_params=_cp(("arbitrary",)))(w, g, m, v, *after)


SMALL = (("b_ada", None, PAYW), ("g_attn_pre", OFF_G_ATTN_PRE, D), ("g_attn_post", OFF_G_ATTN_POST, D), ("sink_a", OFF_SINK, 8),
         ("g_mix_a", OFF_G_MIX_A, AQ), ("g_mix_b", OFF_G_MIX_B, BW), ("g_mlp_pre", OFF_G_MLP_PRE, D), ("g_mlp_post", OFF_G_MLP_POST, D))


def _adamw_small(small, gb, params):
    n = len(SMALL)

    def body(*refs):
        small_ref, gb_ref = refs[:2]
        wmv = refs[2:2 + 3 * n]
        loss_ref = refs[2 + 3 * n]
        outs = refs[3 + 3 * n:]
        loss_ref[...] = small_ref[:, OFF_LOSS:OFF_LOSS + 1] * (0.5 / D)
        for i, (_, off, width) in enumerate(SMALL):
            g = gb_ref[...] if off is None else small_ref[:, off:off + width]
            w_ref, m_ref, v_ref = wmv[3 * i:3 * i + 3]
            outs[4 * i][...] = g
            outs[4 * i + 1][...], outs[4 * i + 2][...], outs[4 * i + 3][...] = _adamw_math(w_ref[...], g, m_ref[...], v_ref[...])

    vm = pl.BlockSpec(memory_space=pltpu.VMEM)
    out_shape = [jax.ShapeDtypeStruct((1, 1), f32)]
    for _, _, width in SMALL:
        out_shape += [jax.ShapeDtypeStruct((1, width), f32)] * 4
    flat = [a for wmv in params for a in wmv]
    res = pl.pallas_call(body, name="adamw_small", in_specs=[vm] * (2 + 3 * n), out_specs=[vm] * len(out_shape),
                         out_shape=out_shape)(small, gb, *flat)
    return res[0], {name: res[1 + 4 * i:5 + 4 * i] for i, (name, _, _) in enumerate(SMALL)}


def kernel(x, c, positions, w_ada, b_ada, g_attn_pre, g_attn_post, w_in, sink_a, g_mix_a, g_mix_b, w_out, g_mlp_pre, g_mlp_post, w_up, w_down, loss_target, m_w_ada, m_b_ada, m_g_attn_pre, m_g_attn_post, m_w_in, m_sink_a, m_g_mix_a, m_g_mix_b, m_w_out, m_g_mlp_pre, m_g_mlp_post, m_w_up, m_w_down, v_w_ada, v_b_ada, v_g_attn_pre, v_g_attn_post, v_w_in, v_sink_a, v_g_mix_a, v_g_mix_b, v_w_out, v_g_mlp_pre, v_g_mlp_post, v_w_up, v_w_down):
    given = dict(w_ada=w_ada, b_ada=b_ada, g_attn_pre=g_attn_pre, g_attn_post=g_attn_post, w_in=w_in, sink_a=sink_a, g_mix_a=g_mix_a,
                 g_mix_b=g_mix_b, w_out=w_out, g_mlp_pre=g_mlp_pre, g_mlp_post=g_mlp_post, w_up=w_up, w_down=w_down)
    moms = dict(w_ada=(m_w_ada, v_w_ada), b_ada=(m_b_ada, v_b_ada), g_attn_pre=(m_g_attn_pre, v_g_attn_pre),
                g_attn_post=(m_g_attn_post, v_g_attn_post), w_in=(m_w_in, v_w_in), sink_a=(m_sink_a, v_sink_a),
                g_mix_a=(m_g_mix_a, v_g_mix_a), g_mix_b=(m_g_mix_b, v_g_mix_b), w_out=(m_w_out, v_w_out),
                g_mlp_pre=(m_g_mlp_pre, v_g_mlp_pre), g_mlp_post=(m_g_mlp_post, v_g_mlp_post), w_up=(m_w_up, v_w_up),
                w_down=(m_w_down, v_w_down))
    order = ["w_ada", "b_ada", "g_attn_pre", "g_attn_post", "w_in", "sink_a", "g_mix_a", "g_mix_b", "w_out", "g_mlp_pre",
             "g_mlp_post", "w_up", "w_down"]
    xi, yi, ci = _pos()
    chip = 2 * xi + yi

    pos_arr = jnp.stack([chip, ci]).astype(jnp.int32)
    big = ("w_in", "w_out", "w_up", "w_down")

    b_cols = lax.dynamic_slice(b_ada, (0, chip * ADAW), (1, ADAW))
    mod, cond_all = _ada_fwd(c, w_ada[0], b_cols)
    gathered = [jax.ShapeDtypeStruct((NCHIP,) + given[n].shape[1:], bf16) for n in big]
    flight_in, (mod,) = _split_start("weights_start_first", [w_in[0].astype(bf16)], gathered[:1], [_weight_copies], [mod])
    mod, rest = lax.optimization_barrier((mod, [given[n][0] for n in big[1:]]))
    flight_rest, (mod, inv_lane) = _split_start("weights_start_rest", [w.astype(bf16) for w in rest], gathered[1:],
                                                [_weight_copies] * 3, [mod, _inv_lane()])
    mod = mod.reshape(BL, NMOD, D)

    def first_weight(after):
        srcs, lands = _split_wait("weights_wait_first", flight_in, [_weight_copies], after)
        cross, _ = _split_start("weights_pair_start_first", srcs, None, [_pair_weight_copies], [], lands=lands)
        _, (win_g,) = _split_wait("weights_pair_wait_first", cross, [_pair_weight_copies], ())
        return win_g

    def later_weights(after, carry):
        srcs, lands = _split_wait("weights_wait_rest", flight_rest, [_weight_copies] * 3, after)
        fl, (carry,) = _split_start("weights_pair_start_rest", srcs, None, [_pair_weight_copies] * 3, [carry], lands=lands)
        _, (wout_g,) = _split_wait("weights_pair_wait_out", fl[:1], [_pair_weight_copies], ())

        def mlp_weights(after):
            _, (wup_g, wdn_g) = _split_wait("weights_pair_wait_mlp", fl[1:], [_pair_weight_copies] * 2, after)
            return wup_g, wdn_g.reshape(DFF, D)

        return wout_g.reshape(D, D), mlp_weights, carry

    waiting, pending = {}, {}

    def send_grad(name, slab, carry):
        land = jax.ShapeDtypeStruct((NDIRECT, slab.shape[1] // 2, slab.shape[2]), bf16)
        pending[name], (carry,) = _split_start("grad_start_" + name, [slab], [land], [_direct_grad_copies], [carry])
        return carry

    def grad_ready(name, g, carry):
        slab = g if g.ndim == 3 else g.reshape(NCHIP, g.shape[0] // NCHIP, g.shape[1])
        if name == "w_in":
            waiting[name] = slab
            return carry
        return send_grad(name, slab, carry)

    grad_x, accs = _local_step(x, positions, mod, loss_target, inv_lane, first_weight, later_weights, grad_ready,
                               g_attn_pre, g_attn_post, sink_a, g_mix_a, g_mix_b, g_mlp_pre, g_mlp_post)

    grads, out = {}, {}

    def update(n, after=()):
        tr = (lambda a: a.T) if n == "w_in" else (lambda a: a)
        res = _adamw(tr(given[n][0]), tr(grads[n]), tr(moms[n][0][0]), tr(moms[n][1][0]), "adamw_" + n, after)
        out[n] = tuple(tr(a)[None] for a in res)
        return res[3]

    def finish(names, after, first=()):
        fl = sum((pending[n] for n in names), [])
        halves, landed = _split_wait("grad_wait_" + names[0], fl, [_direct_grad_copies] * len(names), after)
        flights, token = [], jnp.zeros((SUBLANES, LANES), f32)
        for h, q, n in zip(halves, landed, names):
            full = _chip_add(h, q, pos_arr, "grad_chip_sum_" + n)
            flights.append(_split_start("grad_gather_start_" + n, [token], None, [_pair_gather_copies], [], lands=[full])[0])
            token = flights[-1][0][0]
        last = [update(n, (token,)) for n in first]
        for n, fl1 in zip(names, flights):
            after = tuple(last) if last else () if fl1 is flights[-1] else (token,)
            _, (grads[n],) = _split_wait("grad_gather_wait_" + n, fl1, [_pair_gather_copies], after)
            last = [update(n)]
        return last[0]

    grads["w_ada"], gb, small = _small_allreduce(accs, cond_all)
    small = send_grad("w_in", waiting["w_in"], small)
    last = finish(("w_down", "w_up", "w_out"), (small,))
    finish(("w_in",), (last,), first=("w_ada",))
    loss, res = _adamw_small(small, gb, [(given[n], moms[n][0], moms[n][1]) for n, _, _ in SMALL])
    for n, _, _ in SMALL:
        out[n] = tuple(res[n])
    return (loss.reshape(()), grad_x, *[out[n][0] for n in order], *[out[n][1] for n in order],
            *[out[n][2] for n in order], *[out[n][3] for n in order])
```

```python
import numpy as np
import jax
import jax.numpy as jnp
from jax import lax
from jax.experimental import pallas as pl
from jax.experimental.pallas import tpu as pltpu

f32 = jnp.float32
bf16 = jnp.bfloat16
MESH = pl.DeviceIdType.MESH

D = 1024
SEQ = 2048
BL = 2
HD = 64
AQ = 512
AKV = 128
BW = 512
INW = 2304
DFF = 4096
NMOD = 6
ROT = 16
THETA = 500000.0
EPS = 1e-6
NEG = -1e30
BLK = 128
TM = 512
NJ = SEQ // TM
LANES = 128
SUBLANES = 8
NHEAD = AQ // HD
QSCALE = HD ** -0.5
NCHIP = 4
NDEV = 8
VMEM_LIMIT = 56 << 20

LR, B1, B2, AEPS, WD, STEP = 0.001, 0.9, 0.999, 1e-08, 0.01, 10

OFF_G_ATTN_PRE, OFF_G_ATTN_POST, OFF_G_MIX_A, OFF_G_MIX_B = 0, 1024, 2048, 2560
OFF_G_MLP_PRE, OFF_G_MLP_POST, OFF_SINK, OFF_LOSS = 3072, 4096, 5120, 5248
PAYW = NMOD * D


def _cp(sem=None):
    return pltpu.CompilerParams(dimension_semantics=sem, vmem_limit_bytes=VMEM_LIMIT)


def _dot(a, b):
    return jnp.dot(a, b, preferred_element_type=f32)


def _dot_nt(a, b):
    return lax.dot_general(a, b, (((1,), (1,)), ((), ())), preferred_element_type=f32)


def _dot_tn(a, b):
    return lax.dot_general(a, b, (((0,), (0,)), ((), ())), preferred_element_type=f32)


def _rms(x):
    r = lax.rsqrt(jnp.mean(x * x, axis=-1, keepdims=True) + EPS)
    return x * r, r


def _rms_bwd(dy, y, r):
    return r * (dy - y * jnp.mean(dy * y, axis=-1, keepdims=True))


def _colsum(v):
    return jnp.sum(v, axis=0, keepdims=True)


def _rope(p, c, s1, s2):
    outs = []
    for c0 in range(0, p.shape[1], LANES):
        pc = p[:, c0:c0 + LANES]
        outs.append(pc * c + pltpu.roll(pc, LANES - ROT // 2, 1) * s1 + pltpu.roll(pc, ROT // 2, 1) * s2)
    return outs[0] if len(outs) == 1 else jnp.concatenate(outs, axis=1)


def _rope_t(g, c, s1, s2):
    outs = []
    for c0 in range(0, g.shape[1], LANES):
        gc = g[:, c0:c0 + LANES]
        outs.append(gc * c + pltpu.roll(gc * s1, ROT // 2, 1) + pltpu.roll(gc * s2, LANES - ROT // 2, 1))
    return outs[0] if len(outs) == 1 else jnp.concatenate(outs, axis=1)


def _perm_store(val, scr, out_ref, d):
    nc = val.shape[1] // LANES
    for c in range(nc):
        scr[c] = val[:, LANES * c:LANES * (c + 1)]
    for c in range(nc):
        for r in range(d):
            out_ref[r, :, LANES * c:LANES * (c + 1)] = scr[c, pl.ds(r, TM // d, stride=d), :].astype(out_ref.dtype)


def _perm_load(in_ref, scr, d):
    nc = in_ref.shape[-1] // LANES
    for c in range(nc):
        for r in range(d):
            scr[c, pl.ds(r, TM // d, stride=d), :] = in_ref[r, :, LANES * c:LANES * (c + 1)].astype(f32)
    return jnp.concatenate([scr[c] for c in range(nc)], axis=1)


def _per_query_head(kv):
    r = pltpu.roll(kv, HD, 1)
    lo = lax.broadcasted_iota(jnp.int32, kv.shape, 1) < HD
    return jnp.concatenate([jnp.where(lo, kv, r), jnp.where(lo, r, kv)], axis=1)


def _per_kv_head(g):
    g0, g1 = g[:, :LANES] + g[:, LANES:2 * LANES], g[:, 2 * LANES:3 * LANES] + g[:, 3 * LANES:]
    lo = lax.broadcasted_iota(jnp.int32, g0.shape, 1) < HD
    return jnp.where(lo, g0 + pltpu.roll(g0, HD, 1), g1 + pltpu.roll(g1, HD, 1))


def _tok(w):
    return pl.BlockSpec((None, TM, w), lambda b, j: (b, j, 0))


def _perm_spec(d, w):
    return pl.BlockSpec((None, d, TM // d, w), lambda b, j: (b, 0, j, 0))


def _full(shape):
    n = len(shape)
    return pl.BlockSpec(shape, lambda b, j: (0,) * n)


MOD_SPEC = pl.BlockSpec((None, NMOD, D), lambda b, j: (b, 0, 0))
ACCB_SPEC = pl.BlockSpec((None, SUBLANES, D), lambda b, j: (b, 0, 0))
ACCG_SPEC = pl.BlockSpec((SUBLANES, D), lambda b, j: (0, 0))
ACC_SHAPES = [jax.ShapeDtypeStruct((BL, SUBLANES, D), f32), jax.ShapeDtypeStruct((SUBLANES, D), f32)]


def _acc_init(accb_ref, accg_ref):
    b, j = pl.program_id(0), pl.program_id(1)

    @pl.when(j == 0)
    def _():
        accb_ref[...] = jnp.zeros_like(accb_ref)

    @pl.when((b == 0) & (j == 0))
    def _():
        accg_ref[...] = jnp.zeros_like(accg_ref)


def _rope_tables(pos_col, inv_lane):
    def body(p_ref, inv_ref, c_ref, s1_ref, s2_ref):
        ang = p_ref[...].astype(f32) * inv_ref[...]
        j = lax.broadcasted_iota(jnp.int32, (TM, LANES), 1) % HD
        cs, sn = jnp.cos(ang), jnp.sin(ang)
        c_ref[...] = jnp.where(j < ROT, cs, 1.0)
        s1_ref[...] = jnp.where(j < ROT // 2, -sn, 0.0)
        s2_ref[...] = jnp.where((j >= ROT // 2) & (j < ROT), sn, 0.0)

    n = BL * SEQ // TM
    return pl.pallas_call(
        body, name="rope_tables", grid=(n,),
        in_specs=[pl.BlockSpec((TM, 1), lambda i: (i, 0)), pl.BlockSpec((1, LANES), lambda i: (0, 0))],
        out_specs=[pl.BlockSpec((TM, LANES), lambda i: (i, 0))] * 3,
        out_shape=[jax.ShapeDtypeStruct((BL * SEQ, LANES), f32)] * 3,
    )(pos_col, inv_lane)


def _attn_in(x, mod, g_pre, w_in, tc, ts1, ts2):
    def body(x_ref, mod_ref, g_ref, wg_ref, c_ref, s1_ref, s2_ref,
             h_ref, qa_ref, ka_ref, va_ref, q1_ref, k1_ref, v1_ref, q4_ref, k4_ref, v4_ref, q16_ref, k16_ref, v16_ref,
             w_ref, scr):
        @pl.when((pl.program_id(0) == 0) & (pl.program_id(1) == 0))
        def _():
            w_ref[...] = jnp.concatenate([wg_ref[s] for s in range(NCHIP)], axis=1)

        xn, _ = _rms(x_ref[...])
        h = (xn * g_ref[...]) * (1.0 + mod_ref[1:2, :]) + mod_ref[0:1, :]
        hb = h.astype(bf16)
        h_ref[...] = hb
        proj = _dot(hb, w_ref[...])
        c, s1, s2 = c_ref[...], s1_ref[...], s2_ref[...]
        o1, o2, o3, o4, o5 = AQ, AQ + AKV, AQ + 2 * AKV, AQ + 2 * AKV + BW, AQ + 2 * AKV + 2 * BW
        qa_ref[...] = (_rope(proj[:, :o1], c, s1, s2) * QSCALE).astype(bf16)
        ka_ref[...] = _per_query_head(_rope(proj[:, o1:o2], c, s1, s2)).astype(bf16)
        va_ref[...] = _per_query_head(proj[:, o2:o3]).astype(bf16)
        qb = _rope(proj[:, o3:o4], c, s1, s2) * QSCALE
        kb = _rope(proj[:, o4:o5], c, s1, s2)
        vb = proj[:, o5:]
        for val, r1, r4, r16 in ((qb, q1_ref, q4_ref, q16_ref), (kb, k1_ref, k4_ref, k16_ref), (vb, v1_ref, v4_ref, v16_ref)):
            r1[...] = val.astype(bf16)
            _perm_store(val, scr, r4, 4)
            _perm_store(val, scr, r16, 16)

    nat = lambda w: jax.ShapeDtypeStruct((BL, SEQ, w), bf16)
    p4 = jax.ShapeDtypeStruct((BL, 4, SEQ // 4, BW), bf16)
    p16 = jax.ShapeDtypeStruct((BL, 16, SEQ // 16, BW), bf16)
    return pl.pallas_call(
        body, name="attn_in", grid=(BL, NJ),
        in_specs=[_tok(D), MOD_SPEC, _full((1, D)), _full((NCHIP, D, INW // NCHIP)), _tok(LANES), _tok(LANES), _tok(LANES)],
        out_specs=([_tok(D), _tok(AQ), _tok(2 * AKV), _tok(2 * AKV)] + [_tok(BW)] * 3 + [_perm_spec(4, BW)] * 3 + [_perm_spec(16, BW)] * 3
                   + [_full((D, INW))]),
        out_shape=[nat(D), nat(AQ), nat(2 * AKV), nat(2 * AKV)] + [nat(BW)] * 3 + [p4] * 3 + [p16] * 3
                  + [jax.ShapeDtypeStruct((D, INW), bf16)],
        scratch_shapes=[pltpu.VMEM((BW // LANES, TM, LANES), f32)],
        compiler_params=_cp(("arbitrary", "arbitrary")),
    )(x, mod, g_pre, w_in, tc, ts1, ts2)


def _kv_cat(cur_ref, prev_ref, p, cache):
    key = (id(cur_ref), p)
    if key not in cache:
        sl = slice(LANES * p, LANES * (p + 1))
        cache[key] = cur_ref[:, sl] if prev_ref is None else jnp.concatenate([prev_ref[:, sl], cur_ref[:, sl]], axis=0)
    return cache[key]


def _lane_half(a, hh):
    lo = lax.broadcasted_iota(jnp.int32, a.shape, 1) < HD
    return jnp.where(lo, a, jnp.zeros_like(a)) if hh == 0 else jnp.where(lo, jnp.zeros_like(a), a)


ATT_UNITS = 4


def _att_units(nb):
    return ATT_UNITS if nb == 1 else min(ATT_UNITS, nb)


def _attn_specs(n, nb, descending):
    u = _att_units(nb)
    if nb == 1:
        return (lambda ww: pl.BlockSpec((u, BLK, ww), lambda a, i: (a, 0, 0))), None, (n // u, 1)
    steps = nb // u
    at = (lambda i: steps - 1 - i) if descending else (lambda i: i)
    cur = lambda ww: pl.BlockSpec((None, u * BLK, ww), lambda a, i: (a, at(i), 0))
    prev = lambda ww: pl.BlockSpec((None, BLK, ww), lambda a, i: (a, jnp.maximum(u * at(i) - 1, 0), 0))
    return cur, prev, (n, steps)


def _attn_fwd(q, k, v, sink, *, max_dist, o_dtype, name):
    n, l, w = q.shape
    wk = k.shape[-1]
    nb = l // BLK
    has_sink = sink is not None

    def body(*refs):
        sink_ref = None
        if has_sink:
            sink_ref, refs = refs[0], refs[1:]
        if nb > 1:
            q_ref, kc_ref, kp_ref, vc_ref, vp_ref, o_ref, lse_ref = refs[:7]
            first = pl.program_id(1) == 0
            for u in range(_att_units(nb)):
                rows, before = pl.ds(BLK * u, BLK), pl.ds(BLK * (u - 1), BLK)
                unit(q_ref.at[rows, :], kc_ref.at[rows, :], kp_ref if u == 0 else kc_ref.at[before, :],
                     vc_ref.at[rows, :], vp_ref if u == 0 else vc_ref.at[before, :], o_ref.at[rows, :], lse_ref.at[rows, :],
                     jnp.logical_not(first) if u == 0 else True, sink_ref, *refs[7:])
        else:
            q_ref, kc_ref, vc_ref, o_ref, lse_ref = refs[:5]
            for u in range(_att_units(nb)):
                unit(q_ref.at[u], kc_ref.at[u], None, vc_ref.at[u], None, o_ref.at[u], lse_ref.at[u], None, sink_ref, *refs[5:])

    def unit(q_ref, kc_ref, kp_ref, vc_ref, vp_ref, o_ref, lse_ref, has_prev, sink_ref, sscr, pscr, dscr):
        qi = lax.broadcasted_iota(jnp.int32, (BLK, BLK), 0)
        kj = lax.broadcasted_iota(jnp.int32, (BLK, BLK), 1)
        tri = kj <= qi
        eye = kj == qi
        cache = {}
        for p in range(w // LANES):
            qpair = q_ref[:, LANES * p:LANES * (p + 1)]
            kcat = _kv_cat(kc_ref, kp_ref, p // share, cache)
            for hh in range(2):
                s = _dot_nt(_lane_half(qpair, hh), kcat)
                if nb > 1:
                    sp = s[:, :BLK] if has_prev is True else jnp.where(has_prev, s[:, :BLK], NEG)
                    sscr[2 * p + hh] = jnp.where(tri, s[:, BLK:], sp)
                    if diag:
                        dscr[2 * p + hh] = jnp.where(eye, sp, NEG)
                else:
                    sscr[2 * p + hh] = jnp.where(tri, s, NEG)
        lane = lax.broadcasted_iota(jnp.int32, (BLK, LANES), 1)
        lse_all = jnp.zeros((BLK, LANES), f32)
        for p in range(w // LANES):
            for hh in range(2):
                h = 2 * p + hh
                comb = sscr[h]
                if diag:
                    dtile = dscr[h]
                    m = jnp.max(jnp.maximum(comb, dtile), axis=-1, keepdims=True)
                else:
                    m = jnp.max(comb, axis=-1, keepdims=True)
                if has_sink:
                    sk = sink_ref[0, h]
                    m = jnp.maximum(m, sk)
                e = jnp.exp(comb - m)
                if diag:
                    ed = jnp.exp(dtile - m)
                    den = jnp.sum(e + ed, axis=-1, keepdims=True)
                else:
                    den = jnp.sum(e, axis=-1, keepdims=True)
                if has_sink:
                    den = den + jnp.exp(sk - m)
                inv = 1.0 / den
                if nb > 1:
                    pscr[h, :, :BLK] = (jnp.where(tri, ed if diag else 0.0, e) * inv).astype(bf16)
                    pscr[h, :, BLK:] = (jnp.where(tri, e, 0.0) * inv).astype(bf16)
                else:
                    pscr[h] = (e * inv).astype(bf16)
                lse_all = jnp.where(lane == h, jnp.broadcast_to(m + jnp.log(den), (BLK, LANES)), lse_all)
        lse_ref[...] = lse_all
        for p in range(w // LANES):
            vcat = _kv_cat(vc_ref, vp_ref, p // share, cache)
            o_ref[:, LANES * p:LANES * (p + 1)] = (_dot(pscr[2 * p], _lane_half(vcat, 0))
                                                   + _dot(pscr[2 * p + 1], _lane_half(vcat, 1))).astype(o_ref.dtype)

    assert max_dist in (BLK - 1, BLK) and w % wk == 0
    share = w // wk
    diag = nb > 1 and max_dist == BLK
    cur, prev, grid = _attn_specs(n, nb, False)
    in_specs = [cur(w), cur(wk)] + ([prev(wk)] if nb > 1 else []) + [cur(wk)] + ([prev(wk)] if nb > 1 else [])
    args = [q, k] + ([k] if nb > 1 else []) + [v] + ([v] if nb > 1 else [])
    if has_sink:
        in_specs = [pl.BlockSpec(memory_space=pltpu.SMEM)] + in_specs
        args = [sink] + args
    return pl.pallas_call(
        body, name=name, grid=grid, in_specs=in_specs,
        out_specs=[cur(w), cur(LANES)],
        out_shape=[jax.ShapeDtypeStruct((n, l, w), o_dtype), jax.ShapeDtypeStruct((n, l, LANES), f32)],
        scratch_shapes=[pltpu.VMEM((w // HD, BLK, BLK), f32), pltpu.VMEM((w // HD, BLK, 2 * BLK if nb > 1 else BLK), bf16),
                        pltpu.VMEM((w // HD if diag else 1, BLK, BLK), f32)],
        compiler_params=_cp(("arbitrary", "arbitrary")),
    )(*args)


def _attn_bwd(q, k, v, do, delta, lse, sink, *, max_dist, name):
    n, l, w = q.shape
    wk = k.shape[-1]
    nb = l // BLK
    has_sink = sink is not None

    def body(*refs):
        sink_ref = dsink_ref = ck = cv = None
        if has_sink:
            sink_ref, refs = refs[0], refs[1:]
        nin = 8 if nb > 1 else 6
        ins, rest = refs[:nin], refs[nin:]
        if has_sink:
            dq_ref, dk_ref, dv_ref, dsink_ref = rest[:4]
            rest = rest[4:]
        else:
            dq_ref, dk_ref, dv_ref = rest[:3]
            rest = rest[3:]
        step = pl.program_id(1)
        if has_sink:
            @pl.when((pl.program_id(0) == 0) & (step == 0))
            def _():
                dsink_ref[...] = jnp.zeros_like(dsink_ref)

        if nb > 1:
            q_ref, kc_ref, kp_ref, vc_ref, vp_ref, do_ref, delta_ref, lse_ref = ins
            ck, cv = rest[:2]

            @pl.when(step == 0)
            def _():
                ck[...] = jnp.zeros_like(ck)
                cv[...] = jnp.zeros_like(cv)

            last = step == nb // _att_units(nb) - 1
            for u in reversed(range(_att_units(nb))):
                rows, before = pl.ds(BLK * u, BLK), pl.ds(BLK * (u - 1), BLK)
                unit(q_ref.at[rows, :], kc_ref.at[rows, :], kp_ref if u == 0 else kc_ref.at[before, :],
                     vc_ref.at[rows, :], vp_ref if u == 0 else vc_ref.at[before, :], do_ref.at[rows, :],
                     delta_ref.at[rows, :], lse_ref.at[rows, :], dq_ref.at[rows, :], dk_ref.at[rows, :], dv_ref.at[rows, :],
                     jnp.logical_not(last) if u == 0 else True, sink_ref, dsink_ref, ck, cv, *rest[2:])
        else:
            q_ref, kc_ref, vc_ref, do_ref, delta_ref, lse_ref = ins
            for u in range(_att_units(nb)):
                unit(q_ref.at[u], kc_ref.at[u], None, vc_ref.at[u], None, do_ref.at[u], delta_ref.at[u], lse_ref.at[u],
                     dq_ref.at[u], dk_ref.at[u], dv_ref.at[u], None, sink_ref, dsink_ref, None, None, *rest)

    def unit(q_ref, kc_ref, kp_ref, vc_ref, vp_ref, do_ref, delta_ref, lse_ref, dq_ref, dk_ref, dv_ref, has_prev,
             sink_ref, dsink_ref, ck, cv, sscr, dpscr, pscr, dsscr, dscr=None, ddscr=None):
        lane = lax.broadcasted_iota(jnp.int32, (BLK, LANES), 1)
        qi = lax.broadcasted_iota(jnp.int32, (BLK, BLK), 0)
        kj = lax.broadcasted_iota(jnp.int32, (BLK, BLK), 1)
        tri = kj <= qi
        eye = kj == qi
        cache = {}
        kp, vp = kp_ref, vp_ref
        for p in range(w // LANES):
            sl = slice(LANES * p, LANES * (p + 1))
            qpair, dopair = q_ref[:, sl], do_ref[:, sl]
            kcat, vcat = _kv_cat(kc_ref, kp, p // share, cache), _kv_cat(vc_ref, vp, p // share, cache)
            for hh in range(2):
                h = 2 * p + hh
                s = _dot_nt(_lane_half(qpair, hh), kcat)
                dp = _dot_nt(_lane_half(dopair, hh), vcat)
                if nb > 1:
                    sp = s[:, :BLK] if has_prev is True else jnp.where(has_prev, s[:, :BLK], NEG)
                    sscr[h] = jnp.where(tri, s[:, BLK:], sp)
                    dpscr[h] = jnp.where(tri, dp[:, BLK:], dp[:, :BLK])
                    if diag:
                        dscr[h] = jnp.where(eye, sp, NEG)
                        ddscr[h] = dp[:, :BLK]
                else:
                    sscr[h] = jnp.where(tri, s, NEG)
                    dpscr[h] = dp
        for p in range(w // LANES):
            for hh in range(2):
                h = 2 * p + hh
                lse_b = jnp.broadcast_to(lse_ref[:, h:h + 1], (BLK, BLK))
                delta = jnp.broadcast_to(delta_ref[:, h:h + 1], (BLK, BLK))
                pr = jnp.exp(sscr[h] - lse_b)
                ds = pr * (dpscr[h] - delta)
                if nb > 1:
                    if diag:
                        prd = jnp.exp(dscr[h] - lse_b)
                        dsd = prd * (ddscr[h] - delta)
                    else:
                        prd = dsd = 0.0
                    pscr[h, :, :BLK] = jnp.where(tri, prd, pr).astype(bf16)
                    pscr[h, :, BLK:] = jnp.where(tri, pr, 0.0).astype(bf16)
                    dsscr[h, :, :BLK] = jnp.where(tri, dsd, ds).astype(bf16)
                    dsscr[h, :, BLK:] = jnp.where(tri, ds, 0.0).astype(bf16)
                else:
                    pscr[h] = pr.astype(bf16)
                    dsscr[h] = ds.astype(bf16)
                if has_sink:
                    dsk = -jnp.sum(jnp.where(lane == 0, jnp.exp(sink_ref[0, h] - lse_b) * delta, 0.0), keepdims=True)
                    dsink_ref[h:h + 1, :] += jnp.broadcast_to(dsk, (1, LANES))
        for p in range(w // LANES):
            sl = slice(LANES * p, LANES * (p + 1))
            qpair, dopair = q_ref[:, sl], do_ref[:, sl]
            kcat = _kv_cat(kc_ref, kp, p // share, cache)
            dq_ref[:, sl] = _dot(dsscr[2 * p], _lane_half(kcat, 0)) + _dot(dsscr[2 * p + 1], _lane_half(kcat, 1))
            dk_pair = _dot_tn(dsscr[2 * p], _lane_half(qpair, 0)) + _dot_tn(dsscr[2 * p + 1], _lane_half(qpair, 1))
            dv_pair = _dot_tn(pscr[2 * p], _lane_half(dopair, 0)) + _dot_tn(pscr[2 * p + 1], _lane_half(dopair, 1))
            if nb > 1:
                dk_ref[:, sl] = dk_pair[BLK:] + ck[:, sl]
                dv_ref[:, sl] = dv_pair[BLK:] + cv[:, sl]
                ck[:, sl] = dk_pair[:BLK]
                cv[:, sl] = dv_pair[:BLK]
            else:
                dk_ref[:, sl] = dk_pair
                dv_ref[:, sl] = dv_pair

    assert max_dist in (BLK - 1, BLK) and w % wk == 0
    share = w // wk
    diag = nb > 1 and max_dist == BLK
    cur, prev, grid = _attn_specs(n, nb, True)
    in_specs = ([cur(w), cur(wk)] + ([prev(wk)] if nb > 1 else []) + [cur(wk)] + ([prev(wk)] if nb > 1 else [])
                + [cur(w), cur(LANES), cur(LANES)])
    args = [q, k] + ([k] if nb > 1 else []) + [v] + ([v] if nb > 1 else []) + [do, delta, lse]
    out_specs = [cur(w)] * 3
    out_shape = [jax.ShapeDtypeStruct((n, l, w), f32)] * 3
    if has_sink:
        in_specs = [pl.BlockSpec(memory_space=pltpu.SMEM)] + in_specs
        args = [sink] + args
        out_specs.append(pl.BlockSpec((NHEAD, LANES), lambda a, i: (0, 0)))
        out_shape.append(jax.ShapeDtypeStruct((NHEAD, LANES), f32))
    nh = w // HD
    scratch = [pltpu.VMEM((BLK, w), f32), pltpu.VMEM((BLK, w), f32)] if nb > 1 else []
    scratch += [pltpu.VMEM((nh, BLK, BLK), f32)] * 2 + [pltpu.VMEM((nh, BLK, 2 * BLK if nb > 1 else BLK), bf16)] * 2
    if diag:
        scratch += [pltpu.VMEM((nh, BLK, BLK), f32)] * 2
    return pl.pallas_call(
        body, name=name, grid=grid, in_specs=in_specs, out_specs=out_specs, out_shape=out_shape,
        scratch_shapes=scratch, compiler_params=_cp(("arbitrary", "arbitrary")),
    )(*args)


def _split2(x):
    hi = x.astype(bf16)
    return hi, (x - hi.astype(f32)).astype(bf16)


def _heads_to_lanes(xc, e):
    return sum(_dot(t, e) for t in _split2(xc))


def _lanes_to_heads(x, g):
    return sum(_dot(t, g) for t in _split2(x))


HEAD_EXPAND = (np.arange(LANES)[:, None] == np.arange(BW)[None, :] // HD).astype(np.float32)
HEAD_SUM = HEAD_EXPAND.T.copy()


def _branch_weights(l1_ref, l4_ref, l16_ref, scr):
    l4v = _perm_load(l4_ref, scr, 4)
    l16v = _perm_load(l16_ref, scr, 16)
    l1v = l1_ref[...]
    m = jnp.maximum(jnp.maximum(l1v, l4v), l16v)
    e1, e4, e16 = jnp.exp(l1v - m), jnp.exp(l4v - m), jnp.exp(l16v - m)
    z = e1 + e4 + e16
    return e1 / z, e4 / z, e16 / z


def _mix_out(oa, o1, l1, o4, l4, o16, l16, g_mix_a, g_mix_b, w_out, x, mod, g_post):
    def body(oa_ref, o1_ref, l1_ref, o4_ref, l4_ref, o16_ref, l16_ref, ga_ref, gb_ref, w_ref, x_ref, mod_ref, gp_ref, e_ref,
             x1_ref, y_ref, mixed_ref, ob_ref, scr):
        w1, w4, w16 = _branch_weights(l1_ref, l4_ref, l16_ref, scr)
        e = e_ref[...]
        x1w, x4w = _heads_to_lanes(w1, e), _heads_to_lanes(w4, e)
        ob = (x1w * o1_ref[...].astype(f32) + x4w * _perm_load(o4_ref, scr, 4)
              + (1.0 - x1w - x4w) * _perm_load(o16_ref, scr, 16))
        ob_ref[...] = ob
        oan, _ = _rms(oa_ref[...])
        obn, _ = _rms(ob)
        mixed = jnp.concatenate([oan * ga_ref[...], obn * gb_ref[...]], axis=1).astype(bf16)
        mixed_ref[...] = mixed
        y = _dot(mixed, w_ref[...])
        y_ref[...] = y
        yn, _ = _rms(y)
        x1_ref[...] = x_ref[...] + mod_ref[2:3, :] * (yn * gp_ref[...])

    nat = lambda w, dt: jax.ShapeDtypeStruct((BL, SEQ, w), dt)
    return pl.pallas_call(
        body, name="mix_out", grid=(BL, NJ),
        in_specs=[_tok(AQ), _tok(BW), _tok(LANES), _perm_spec(4, BW), _perm_spec(4, LANES), _perm_spec(16, BW),
                  _perm_spec(16, LANES), _full((1, AQ)), _full((1, BW)), _full((D, D)), _tok(D), MOD_SPEC, _full((1, D)),
                  _full((LANES, BW))],
        out_specs=[_tok(D), _tok(D), _tok(D), _tok(BW)],
        out_shape=[nat(D, f32), nat(D, f32), nat(D, bf16), nat(BW, f32)],
        scratch_shapes=[pltpu.VMEM((BW // LANES, TM, LANES), f32)],
        compiler_params=_cp(("arbitrary", "arbitrary")),
    )(oa, o1, l1, o4, l4, o16, l16, g_mix_a, g_mix_b, w_out, x, mod, g_post, jnp.asarray(HEAD_EXPAND, bf16))


def _mlp_up(x1, mod, g_pre, w_up):
    def body(x_ref, mod_ref, g_ref, w_ref, h_ref, u_ref, a_ref):
        xn, _ = _rms(x_ref[...])
        h = (xn * g_ref[...]) * (1.0 + mod_ref[4:5, :]) + mod_ref[3:4, :]
        hb = h.astype(bf16)
        h_ref[...] = hb
        for s in range(NCHIP):
            u = _dot(hb, w_ref[s])
            u_ref[:, D * s:D * (s + 1)] = u.astype(bf16)
            a_ref[:, D * s:D * (s + 1)] = jnp.square(jnp.maximum(u, 0.0)).astype(bf16)

    nat = lambda w: jax.ShapeDtypeStruct((BL, SEQ, w), bf16)
    return pl.pallas_call(
        body, name="mlp_up", grid=(BL, NJ),
        in_specs=[_tok(D), MOD_SPEC, _full((1, D)), _full((NCHIP, D, D))],
        out_specs=[_tok(D), _tok(DFF), _tok(DFF)], out_shape=[nat(D), nat(DFF), nat(DFF)],
        compiler_params=_cp(("arbitrary", "arbitrary")),
    )(x1, mod, g_pre, w_up)


def _mlp_down(a, w_down, x1, target, mod, g_post):
    def body(a_ref, w_ref, x_ref, t_ref, mod_ref, g_ref, gx_ref, dy_ref, accb_ref, accg_ref):
        _acc_init(accb_ref, accg_ref)
        y2 = _dot(a_ref[...], w_ref[...])
        yn, r = _rms(y2)
        g = g_ref[...]
        gt = mod_ref[5:6, :]
        n2 = yn * g
        err = x_ref[...] + gt * n2 - t_ref[...]
        gout = err * (1.0 / D)
        gx_ref[...] = gout
        dn2 = gout * gt
        dy_ref[...] = _rms_bwd(dn2 * g, yn, r).astype(bf16)
        accb_ref[0:1, :] += _colsum(gout * n2)
        accg_ref[0:1, :] += _colsum(dn2 * yn)
        accg_ref[1:2, :] += jnp.broadcast_to(jnp.sum(err * err, keepdims=True), (1, D))

    return pl.pallas_call(
        body, name="mlp_down", grid=(BL, NJ),
        in_specs=[_tok(DFF), _full((DFF, D)), _tok(D), _tok(D), MOD_SPEC, _full((1, D))],
        out_specs=[_tok(D), _tok(D), ACCB_SPEC, ACCG_SPEC],
        out_shape=[jax.ShapeDtypeStruct((BL, SEQ, D), f32), jax.ShapeDtypeStruct((BL, SEQ, D), bf16)] + ACC_SHAPES,
        compiler_params=_cp(("arbitrary", "arbitrary")),
    )(a, w_down, x1, target, mod, g_post)


def _mlp_bwd(dy2, u, w_down, w_up, x1, gx, mod, g_pre):
    def body(dy_ref, u_ref, wd_hbm, wu_hbm, x_ref, gx_ref, mod_ref, g_ref, du_ref, gx1_ref, accb_ref, accg_ref, wd, wu, sem):
        _acc_init(accb_ref, accg_ref)
        first = (pl.program_id(0) == 0) & (pl.program_id(1) == 0)
        c1 = pltpu.make_async_copy(wd_hbm, wd, sem.at[0])
        c2 = pltpu.make_async_copy(wu_hbm, wu, sem.at[1])

        @pl.when(first)
        def _():
            c1.start()
            c2.start()
            c1.wait()

        dy = dy_ref[...]
        for s in range(NCHIP):
            sl = slice(D * s, D * (s + 1))
            da = _dot_nt(dy, wd[sl, :])
            du_ref[:, sl] = (da * (2.0 * jnp.maximum(u_ref[:, sl].astype(f32), 0.0))).astype(bf16)

        @pl.when(first)
        def _():
            c2.wait()

        dh = jnp.zeros((TM, D), f32)
        for s in range(NCHIP):
            dh = dh + _dot_nt(du_ref[:, D * s:D * (s + 1)], wu[s])
        xn, r = _rms(x_ref[...])
        g = g_ref[...]
        n = xn * g
        dn = dh * (1.0 + mod_ref[4:5, :])
        gx1_ref[...] = gx_ref[...] + _rms_bwd(dn * g, xn, r)
        accb_ref[0:1, :] += _colsum(dh * n)
        accb_ref[1:2, :] += _colsum(dh)
        accg_ref[0:1, :] += _colsum(dn * xn)

    anyspec = pl.BlockSpec(memory_space=pl.ANY)
    return pl.pallas_call(
        body, name="mlp_bwd", grid=(BL, NJ),
        in_specs=[_tok(D), _tok(DFF), anyspec, anyspec, _tok(D), _tok(D), MOD_SPEC, _full((1, D))],
        out_specs=[_tok(DFF), _tok(D), ACCB_SPEC, ACCG_SPEC],
        out_shape=[jax.ShapeDtypeStruct((BL, SEQ, DFF), bf16), jax.ShapeDtypeStruct((BL, SEQ, D), f32)] + ACC_SHAPES,
        scratch_shapes=[pltpu.VMEM((DFF, D), bf16), pltpu.VMEM((NCHIP, D, D), bf16), pltpu.SemaphoreType.DMA((2,))],
        compiler_params=_cp(("arbitrary", "arbitrary")),
    )(dy2, u, w_down, w_up, x1, gx, mod, g_pre)


def _matmul_tn(a, b, *, tn, col_blocked, name, out_dtype=f32):
    t, m = a.shape
    n = b.shape[1]
    tmm = min(m, 1024)
    tk = 2048 if tn <= 1024 else 1024
    nk = t // tk

    def body(a_ref, b_ref, o_ref, acc):
        k = pl.program_id(2)

        @pl.when(k == 0)
        def _():
            acc[...] = jnp.zeros_like(acc)

        acc[...] += _dot_tn(a_ref[...], b_ref[...])

        @pl.when(k == nk - 1)
        def _():
            o_ref[...] = acc[...].astype(out_dtype)

    if col_blocked:
        out_spec = pl.BlockSpec((None, tmm, tn), lambda i, j, k: (j, i, 0))
        out_shape = jax.ShapeDtypeStruct((n // tn, m, tn), out_dtype)
    else:
        out_spec = pl.BlockSpec((tmm, tn), lambda i, j, k: (i, j))
        out_shape = jax.ShapeDtypeStruct((m, n), out_dtype)
    return pl.pallas_call(
        body, name=name, grid=(m // tmm, n // tn, nk),
        in_specs=[pl.BlockSpec((tk, tmm), lambda i, j, k: (k, i)), pl.BlockSpec((tk, tn), lambda i, j, k: (k, j))],
        out_specs=out_spec, out_shape=out_shape, scratch_shapes=[pltpu.VMEM((tmm, tn), f32)],
        compiler_params=_cp(("arbitrary", "arbitrary", "arbitrary")),
    )(a, b)


def _grad_w_in(h, dproj):
    t = h.shape[0]
    tk = 1024
    nk = t // tk
    sw = INW // NCHIP

    def body(a_ref, b_ref, o_ref, acc):
        k = pl.program_id(0)

        @pl.when(k == 0)
        def _():
            acc[...] = jnp.zeros_like(acc)

        acc[...] += _dot_tn(a_ref[...], b_ref[...])

        @pl.when(k == nk - 1)
        def _():
            for s in range(NCHIP):
                o_ref[s] = acc[:, sw * s:sw * (s + 1)].astype(bf16)

    return pl.pallas_call(
        body, name="grad_w_in", grid=(nk,),
        in_specs=[pl.BlockSpec((tk, D), lambda k: (k, 0)), pl.BlockSpec((tk, INW), lambda k: (k, 0))],
        out_specs=pl.BlockSpec((NCHIP, D, sw), lambda k: (0, 0, 0)), out_shape=jax.ShapeDtypeStruct((NCHIP, D, sw), bf16),
        scratch_shapes=[pltpu.VMEM((D, INW), f32)], compiler_params=_cp(("arbitrary",)),
    )(h, dproj)


def _attn_out_bwd(gx1, y, mod, g_post, w_out, oa, ob, g_mix_a, g_mix_b, l1, l4, l16):
    def body(gx_ref, y_ref, mod_ref, gp_ref, w_ref, oa_ref, ob_ref, ga_ref, gb_ref, l1_ref, l4_ref, l16_ref, e_ref, g_ref,
             dy_ref, doa_ref, do1_ref, do4_ref, do16_ref, da_ref, d1_ref, d4_ref, d16_ref, accb_ref, accg_ref, scr):
        _acc_init(accb_ref, accg_ref)
        w1, w4, w16 = _branch_weights(l1_ref, l4_ref, l16_ref, scr)
        e, hs = e_ref[...], g_ref[...]
        gx1v = gx_ref[...]
        yn, ry = _rms(y_ref[...])
        gp = gp_ref[...]
        gt = mod_ref[2:3, :]
        dn1 = gx1v * gt
        dy = _rms_bwd(dn1 * gp, yn, ry).astype(bf16)
        dy_ref[...] = dy
        dmixed = _dot_nt(dy, w_ref[...])
        dma, dmb = dmixed[:, :AQ], dmixed[:, AQ:]
        oa, ob = oa_ref[...], ob_ref[...]
        oan, ra = _rms(oa)
        obn, rb = _rms(ob)
        doa = _rms_bwd(dma * ga_ref[...], oan, ra)
        doa_ref[...] = doa.astype(bf16)
        da_ref[...] = _lanes_to_heads(doa * oa, hs)
        dob = _rms_bwd(dmb * gb_ref[...], obn, rb)
        dd = _lanes_to_heads(dob * ob, hs)
        x1w, x4w = _heads_to_lanes(w1, e), _heads_to_lanes(w4, e)
        do1_ref[...] = (x1w * dob).astype(bf16)
        d1_ref[...] = w1 * dd
        _perm_store(x4w * dob, scr, do4_ref, 4)
        _perm_store(w4 * dd, scr, d4_ref, 4)
        _perm_store((1.0 - x1w - x4w) * dob, scr, do16_ref, 16)
        _perm_store(w16 * dd, scr, d16_ref, 16)
        accb_ref[0:1, :] += _colsum(gx1v * (yn * gp))
        accg_ref[0:1, :] += _colsum(dn1 * yn)
        accg_ref[1:2, :] += jnp.concatenate([_colsum(dma * oan), _colsum(dmb * obn)], axis=1)

    nat = lambda w, dt: jax.ShapeDtypeStruct((BL, SEQ, w), dt)
    return pl.pallas_call(
        body, name="attn_out_bwd", grid=(BL, NJ),
        in_specs=[_tok(D), _tok(D), MOD_SPEC, _full((1, D)), _full((D, D)), _tok(AQ), _tok(BW), _full((1, AQ)), _full((1, BW)),
                  _tok(LANES), _perm_spec(4, LANES), _perm_spec(16, LANES), _full((LANES, BW)), _full((BW, LANES))],
        out_specs=[_tok(D), _tok(AQ), _tok(BW), _perm_spec(4, BW), _perm_spec(16, BW),
                   _tok(LANES), _tok(LANES), _perm_spec(4, LANES), _perm_spec(16, LANES), ACCB_SPEC, ACCG_SPEC],
        out_shape=[nat(D, bf16), nat(AQ, bf16), nat(BW, bf16), jax.ShapeDtypeStruct((BL, 4, SEQ // 4, BW), bf16),
                   jax.ShapeDtypeStruct((BL, 16, SEQ // 16, BW), bf16), nat(LANES, f32), nat(LANES, f32),
                   jax.ShapeDtypeStruct((BL, 4, SEQ // 4, LANES), f32), jax.ShapeDtypeStruct((BL, 16, SEQ // 16, LANES), f32)]
                  + ACC_SHAPES,
        scratch_shapes=[pltpu.VMEM((BW // LANES, TM, LANES), f32)],
        compiler_params=_cp(("arbitrary", "arbitrary")),
    )(gx1, y, mod, g_post, w_out, oa, ob, g_mix_a, g_mix_b, l1, l4, l16, jnp.asarray(HEAD_EXPAND, bf16),
      jnp.asarray(HEAD_SUM, bf16))


def _attn_in_bwd(dqa, dka, dva, d1, d4, d16, tc, ts1, ts2, w_in, x, gx1, mod, g_pre):
    def body(dqa_ref, dka_ref, dva_ref, dq1_ref, dk1_ref, dv1_ref, dq4_ref, dk4_ref, dv4_ref, dq16_ref, dk16_ref, dv16_ref,
             c_ref, s1_ref, s2_ref, w_ref, x_ref, gx_ref, mod_ref, g_ref, dproj_ref, dx_ref, accb_ref, accg_ref, scr):
        _acc_init(accb_ref, accg_ref)
        c, s1, s2 = c_ref[...], s1_ref[...], s2_ref[...]
        tot = lambda r1, r4, r16: r1[...] + _perm_load(r4, scr, 4) + _perm_load(r16, scr, 16)
        dqb = tot(dq1_ref, dq4_ref, dq16_ref)
        dkb = tot(dk1_ref, dk4_ref, dk16_ref)
        dvb = tot(dv1_ref, dv4_ref, dv16_ref)
        dproj = jnp.concatenate([
            _rope_t(dqa_ref[...], c, s1, s2) * QSCALE, _rope_t(_per_kv_head(dka_ref[...]), c, s1, s2),
            _per_kv_head(dva_ref[...]),
            _rope_t(dqb, c, s1, s2) * QSCALE, _rope_t(dkb, c, s1, s2), dvb], axis=1).astype(bf16)
        dproj_ref[...] = dproj
        dh = _dot_nt(dproj, w_ref[...])
        xn, r = _rms(x_ref[...])
        g = g_ref[...]
        dn = dh * (1.0 + mod_ref[1:2, :])
        dx_ref[...] = gx_ref[...] + _rms_bwd(dn * g, xn, r)
        accb_ref[0:1, :] += _colsum(dh * (xn * g))
        accb_ref[1:2, :] += _colsum(dh)
        accg_ref[0:1, :] += _colsum(dn * xn)

    return pl.pallas_call(
        body, name="attn_in_bwd", grid=(BL, NJ),
        in_specs=[_tok(AQ), _tok(AQ), _tok(AQ)] + [_tok(BW)] * 3 + [_perm_spec(4, BW)] * 3 + [_perm_spec(16, BW)] * 3
                 + [_tok(LANES)] * 3 + [_full((D, INW)), _tok(D), _tok(D), MOD_SPEC, _full((1, D))],
        out_specs=[_tok(INW), _tok(D), ACCB_SPEC, ACCG_SPEC],
        out_shape=[jax.ShapeDtypeStruct((BL, SEQ, INW), bf16), jax.ShapeDtypeStruct((BL, SEQ, D), f32)] + ACC_SHAPES,
        scratch_shapes=[pltpu.VMEM((BW // LANES, TM, LANES), f32)],
        compiler_params=_cp(("arbitrary", "arbitrary")),
    )(dqa, dka, dva, *d1, *d4, *d16, tc, ts1, ts2, w_in, x, gx1, mod, g_pre)


def _inv_lane():
    inv = np.float32(THETA) ** (-np.arange(0, ROT, 2, dtype=np.float32) / np.float32(ROT))
    lane = np.arange(LANES) % HD
    return jnp.asarray(np.where(lane < ROT, inv[lane % (ROT // 2)], 0.0).astype(np.float32)[None, :])


def _local_step(x, positions, mod, target, inv_lane, first_weight, later_weights, grad_ready, g_attn_pre,
                g_attn_post, sink_a, g_mix_a, g_mix_b, g_mlp_pre, g_mlp_post):
    tabs = _rope_tables(positions.reshape(BL * SEQ, 1), inv_lane)
    w_in = first_weight(tuple(tabs))
    tc, ts1, ts2 = [t.reshape(BL, SEQ, LANES) for t in tabs]

    (h, qa, ka, va, q1, k1, v1, q4, k4, v4, q16, k16, v16, w_in) = _attn_in(x, mod, g_attn_pre, w_in, tc, ts1, ts2)
    seqs = lambda t: t.reshape(t.shape[0] * t.shape[1], t.shape[2], t.shape[3])
    q4, k4, v4, q16, k16, v16 = [seqs(t) for t in (q4, k4, v4, q16, k16, v16)]
    oa, la = _attn_fwd(qa, ka, va, sink_a, max_dist=BLK - 1, o_dtype=f32, name="attn_a_fwd")
    o1, l1 = _attn_fwd(q1, k1, v1, None, max_dist=BLK, o_dtype=bf16, name="attn_b1_fwd")
    o4, l4 = _attn_fwd(q4, k4, v4, None, max_dist=BLK, o_dtype=bf16, name="attn_b4_fwd")
    o16, l16 = _attn_fwd(q16, k16, v16, None, max_dist=BLK, o_dtype=bf16, name="attn_b16_fwd")
    b4 = lambda t: t.reshape(BL, 4, SEQ // 4, t.shape[-1])
    b16 = lambda t: t.reshape(BL, 16, SEQ // 16, t.shape[-1])
    w_out, mlp_weights, mod = later_weights((oa, o1, o4, o16), mod)
    x1, y, mixed, ob = _mix_out(oa, o1, l1, b4(o4), b4(l4), b16(o16), b16(l16), g_mix_a, g_mix_b, w_out, x, mod, g_attn_post)
    w_up, w_down = mlp_weights((x1,))
    h2, u, a = _mlp_up(x1, mod, g_mlp_pre, w_up)
    gx, dy2, accb_d, accg_d = _mlp_down(a, w_down, x1, target, mod, g_mlp_post)

    flat = lambda t: t.reshape(BL * SEQ, t.shape[-1])
    mod = grad_ready("w_down", _matmul_tn(flat(a), flat(dy2), tn=D, col_blocked=False, name="grad_w_down", out_dtype=bf16), mod)
    du, gx1, accb_m, accg_m = _mlp_bwd(dy2, u, w_down, w_up, x1, gx, mod, g_mlp_pre)
    mod = grad_ready("w_up", _matmul_tn(flat(h2), flat(du), tn=D, col_blocked=True, name="grad_w_up", out_dtype=bf16), mod)

    dy, doa, do1, do4, do16, da, dl1, dl4, dl16, accb_o, accg_o = _attn_out_bwd(
        gx1, y, mod, g_attn_post, w_out, oa, ob, g_mix_a, g_mix_b, l1, b4(l4), b16(l16))
    sink_behind = grad_ready("w_out", _matmul_tn(flat(mixed), flat(dy), tn=D, col_blocked=False, name="grad_w_out",
                                                  out_dtype=bf16), sink_a)
    dqa, dka, dva, dsink = _attn_bwd(qa, ka, va, doa, da, la, sink_behind, max_dist=BLK - 1, name="attn_a_bwd")
    d1 = _attn_bwd(q1, k1, v1, do1, dl1, l1, None, max_dist=BLK, name="attn_b1_bwd")
    d4 = _attn_bwd(q4, k4, v4, seqs(do4), seqs(dl4), l4, None, max_dist=BLK, name="attn_b4_bwd")
    d16 = _attn_bwd(q16, k16, v16, seqs(do16), seqs(dl16), l16, None, max_dist=BLK, name="attn_b16_bwd")
    dproj, grad_x, accb_i, accg_i = _attn_in_bwd(dqa, dka, dva, d1, [b4(t) for t in d4], [b16(t) for t in d16],
                                                 tc, ts1, ts2, w_in, x, gx1, mod, g_attn_pre)
    gw_in = _grad_w_in(flat(h), flat(dproj))
    dsink = grad_ready("w_in", gw_in, dsink)

    return grad_x, (accb_i, accb_o, accb_m, accb_d, accg_i, accg_o, accg_m, accg_d, dsink)


ADAW = NMOD * D // NCHIP


def _pos():
    return lax.axis_index("x"), lax.axis_index("y"), lax.axis_index("c")


def _flip(v, bit):
    return 1 - v if bit else v


def _all_peers(x, y, c):
    return [(_flip(x, k >> 2 & 1), _flip(y, k >> 1 & 1), _flip(c, k & 1)) for k in range(1, NDEV)]


def _other_chips(x, y):
    return [(1 - x, y), (x, 1 - y), (1 - x, 1 - y)]


def _rcopy(src, dst, send, recv, k, dev, k_recv=None):
    return pltpu.make_async_remote_copy(src_ref=src, dst_ref=dst, send_sem=send.at[k],
                                        recv_sem=recv.at[k if k_recv is None else k_recv],
                                        device_id=dev, device_id_type=MESH)


def _gather_small(src, buf, send, recv):
    x, y, c = _pos()
    me = 4 * x + 2 * y + c
    peers = _all_peers(x, y, c)
    sends = [_rcopy(src, buf.at[me], send, recv, k, p) for k, p in enumerate(peers)]
    for cp in sends:
        cp.start()
    for k, (px, py, pc) in enumerate(peers):
        _rcopy(src, buf.at[4 * px + 2 * py + pc], send, recv, k, (px, py, pc)).wait_recv()
    for cp in sends:
        cp.wait_send()
    return me


def _ada_fwd(c_in, w_ada, b_cols):
    def body(c_ref, w_hbm, b_ref, mod_ref, cond_ref, cbuf, mbuf, w_ref, s1, r1, s2, r2, wsem):
        x, y, c = _pos()
        chip = 2 * x + y
        wcopy = pltpu.make_async_copy(w_hbm, w_ref, wsem)
        wcopy.start()
        me = _gather_small(c_ref, cbuf, s1, r1)
        cbuf[me] = c_ref[...]
        for i in range(NDEV):
            cond_ref[BL * i:BL * (i + 1), :] = cbuf[i]
        call = cond_ref[...]
        cond = call / (1.0 + jnp.exp(-call))
        cond_ref[...] = cond
        wcopy.wait()
        mbuf[chip] = _dot(cond.astype(bf16), w_ref[...].astype(bf16)) + b_ref[...]
        chips = _other_chips(x, y)
        sends = [_rcopy(mbuf.at[chip], mbuf.at[chip], s2, r2, j, (px, py, c)) for j, (px, py) in enumerate(chips)]
        for cp in sends:
            cp.start()
        for j, (px, py) in enumerate(chips):
            _rcopy(mbuf.at[chip], mbuf.at[2 * px + py], s2, r2, j, (px, py, c)).wait_recv()
        for cp in sends:
            cp.wait_send()
        row = lax.broadcasted_iota(jnp.int32, (BL * NDEV, ADAW), 0)
        for s in range(NCHIP):
            slab = mbuf[s]
            for j in range(BL):
                mod_ref[j:j + 1, ADAW * s:ADAW * (s + 1)] = jnp.sum(jnp.where(row == BL * me + j, slab, 0.0), axis=0, keepdims=True)

    vm = pl.BlockSpec(memory_space=pltpu.VMEM)
    return pl.pallas_call(
        body, name="ada_fwd", in_specs=[vm, pl.BlockSpec(memory_space=pl.ANY), vm], out_specs=[vm, vm],
        out_shape=[jax.ShapeDtypeStruct((BL, NMOD * D), f32), jax.ShapeDtypeStruct((BL * NDEV, D), f32)],
        scratch_shapes=[pltpu.VMEM((NDEV, BL, D), f32), pltpu.VMEM((NCHIP, BL * NDEV, ADAW), f32),
                        pltpu.VMEM((D, ADAW), f32),
                        pltpu.SemaphoreType.DMA((NDEV - 1,)), pltpu.SemaphoreType.DMA((NDEV - 1,)),
                        pltpu.SemaphoreType.DMA((NCHIP - 1,)), pltpu.SemaphoreType.DMA((NCHIP - 1,)),
                        pltpu.SemaphoreType.DMA],
        compiler_params=pltpu.CompilerParams(vmem_limit_bytes=VMEM_LIMIT),
    )(c_in, w_ada, b_cols)


def _small_allreduce(accs, cond_all):
    def body(bi, bo, bm, bd, gi, go, gm, gd, dsink, cond_ref, gw_ref, gb_ref, small_ref, pay, pbuf, dall, s1, r1):
        x, y, c = _pos()
        chip = 2 * x + y
        pay[...] = jnp.zeros_like(pay)
        for b in range(BL):
            for k, (ref, r) in enumerate(((bi, 1), (bi, 0), (bo, 0), (bm, 1), (bm, 0), (bd, 0))):
                pay[b:b + 1, D * k:D * (k + 1)] = ref[b, r:r + 1, :]
        for off, ref, r in ((OFF_G_ATTN_PRE, gi, 0), (OFF_G_ATTN_POST, go, 0), (OFF_G_MIX_A, go, 1), (OFF_G_MLP_PRE, gm, 0),
                            (OFF_G_MLP_POST, gd, 0)):
            pay[BL:BL + 1, off:off + D] = ref[r:r + 1, :]
        eye = lax.broadcasted_iota(jnp.int32, (NHEAD, LANES), 0) == lax.broadcasted_iota(jnp.int32, (NHEAD, LANES), 1)
        pay[BL:BL + 1, OFF_SINK:OFF_SINK + LANES] = jnp.sum(jnp.where(eye, dsink[...], 0.0), axis=0, keepdims=True)
        pay[BL:BL + 1, OFF_LOSS:OFF_LOSS + LANES] = gd[1:2, 0:LANES]
        me = _gather_small(pay, pbuf, s1, r1)
        pbuf[me] = pay[...]
        small = pbuf[0, BL:BL + 1, :]
        for i in range(1, NDEV):
            small = small + pbuf[i, BL:BL + 1, :]
        small_ref[...] = small
        for i in range(NDEV):
            dall[BL * i:BL * (i + 1), :] = pbuf[i, 0:BL, :]
        gb_ref[...] = jnp.sum(dall[...], axis=0, keepdims=True)
        cols = jnp.zeros((BL * NDEV, ADAW), f32)
        for s in range(NCHIP):
            cols = cols + jnp.where(chip == s, dall[:, ADAW * s:ADAW * (s + 1)], 0.0)
        gw_ref[...] = _dot_tn(cond_ref[...].astype(bf16), cols.astype(bf16))

    vm = pl.BlockSpec(memory_space=pltpu.VMEM)
    return pl.pallas_call(
        body, name="small_allreduce", in_specs=[vm] * 10, out_specs=[vm] * 3,
        out_shape=[jax.ShapeDtypeStruct((D, ADAW), f32), jax.ShapeDtypeStruct((1, PAYW), f32), jax.ShapeDtypeStruct((1, PAYW), f32)],
        scratch_shapes=[pltpu.VMEM((4, PAYW), f32), pltpu.VMEM((NDEV, 4, PAYW), f32), pltpu.VMEM((BL * NDEV, PAYW), f32),
                        pltpu.SemaphoreType.DMA((NDEV - 1,)), pltpu.SemaphoreType.DMA((NDEV - 1,))],
        compiler_params=pltpu.CompilerParams(vmem_limit_bytes=VMEM_LIMIT),
    )(*accs, cond_all)


def _half(ref, c):
    r2 = ref.shape[0] // 2
    return ref.at[pl.ds(c * r2 if isinstance(c, int) else pl.multiple_of(c * r2, 16), r2), :]


HBM_SPEC = pl.BlockSpec(memory_space=pltpu.HBM)
SEM_SPEC = pl.BlockSpec(memory_space=pltpu.SEMAPHORE)
EFFECT = pltpu.SideEffectType.DATAFLOW_SIDE_EFFECTING
NLINK = NCHIP - 1


def _in_hbm(a):
    return pltpu.with_memory_space_constraint(a, pltpu.HBM)


NSEM = 8


def _split_start(name, srcs, land_shapes, builds, carry, after=(), lands=None):
    n = len(srcs)
    na, nc = len(after), len(carry)

    def body(*refs):
        src, land = refs[:n], refs[n:2 * n]
        kept = refs[2 * n + na:2 * n + na + nc]
        outs = refs[2 * n + na + nc:]
        send, recv, passed = outs[:n], outs[n:2 * n], outs[4 * n:]
        for t in range(n):
            for out_cp, _ in builds[t](src[t], land[t], send[t], recv[t]):
                out_cp.start()
        for a, b in zip(kept, passed):
            b[...] = a[...]

    if lands is None:
        lands = [lax.empty(s.shape, s.dtype) for s in land_shapes]
    lands = [_in_hbm(a) for a in lands]
    sems = [pltpu.SemaphoreType.DMA((NSEM,))] * (2 * n)
    thru = [pltpu.HBM(a.shape, a.dtype) for a in list(srcs) + lands]
    vm = pl.BlockSpec(memory_space=pltpu.VMEM)
    res = pl.pallas_call(
        body, name=name, out_shape=sems + thru + [jax.ShapeDtypeStruct(a.shape, a.dtype) for a in carry],
        in_specs=[HBM_SPEC] * (2 * n) + [pl.BlockSpec(memory_space=pl.ANY)] * na + [vm] * nc,
        out_specs=[SEM_SPEC] * (2 * n) + [HBM_SPEC] * (2 * n) + [vm] * nc,
        input_output_aliases={i: 2 * n + i for i in range(2 * n)},
        compiler_params=pltpu.CompilerParams(has_side_effects=EFFECT),
    )(*[_in_hbm(a) for a in srcs], *lands, *after, *carry)
    flight = [(res[2 * n + t], res[3 * n + t], res[t], res[n + t]) for t in range(n)]
    return flight, list(res[4 * n:])


def _split_wait(name, flight, builds, after):
    m = len(flight)
    na = len(after)

    def body(*refs):
        src, land, send, recv = refs[:m], refs[m:2 * m], refs[2 * m:3 * m], refs[3 * m:4 * m]
        for t in range(m):
            for out_cp, in_cp in builds[t](src[t], land[t], send[t], recv[t]):
                out_cp.wait_send()
                in_cp.wait_recv()

    ops = [f[0] for f in flight] + [f[1] for f in flight] + [f[2] for f in flight] + [f[3] for f in flight]
    res = pl.pallas_call(
        body, name=name, out_shape=[pltpu.HBM(a.shape, a.dtype) for a in ops[:2 * m]],
        in_specs=[HBM_SPEC] * (2 * m) + [SEM_SPEC] * (2 * m) + [pl.BlockSpec(memory_space=pl.ANY)] * na,
        out_specs=[HBM_SPEC] * (2 * m), input_output_aliases={i: i for i in range(2 * m)},
        compiler_params=pltpu.CompilerParams(has_side_effects=EFFECT),
    )(*ops, *after)
    return res[:m], res[m:2 * m]


def _weight_copies(src, land, send, recv):
    x, y, c = _pos()
    chip = 2 * x + y
    return [(_rcopy(_half(src, c), _half(land.at[chip], c), send, recv, j, (px, py, c)),
             _rcopy(_half(src, c), _half(land.at[2 * px + py], c), send, recv, j, (px, py, c)))
            for j, (px, py) in enumerate(_other_chips(x, y))]


NDIRECT = NDEV - 1


def _direct_grad_copies(src, land, send, recv):
    x, y, c = _pos()
    out, arrive = [], []
    for j, (px, py) in enumerate(_other_chips(x, y)):
        for hc in range(2):
            out.append(_rcopy(_half(src.at[2 * px + py], hc), land.at[2 * j + c], send, recv, 2 * j + hc, (px, py, hc),
                              k_recv=2 * j + c))
            arrive.append(_rcopy(_half(src.at[2 * px + py], hc), land.at[2 * j + hc], send, recv, 2 * j + hc, (px, py, hc)))
    own = _rcopy(_half(src.at[2 * x + y], 1 - c), land.at[NDIRECT - 1], send, recv, NDIRECT - 1, (x, y, 1 - c))
    return list(zip(out, arrive)) + [(own, own)]


def _pair_weight_copies(src, land, send, recv):
    x, y, c = _pos()
    sib = (x, y, 1 - c)
    cps = []
    for j, (px, py) in enumerate(_other_chips(x, y)):
        mine, theirs = _half(land.at[2 * px + py], c), _half(land.at[2 * px + py], 1 - c)
        cps.append((_rcopy(mine, mine, send, recv, j, sib), _rcopy(theirs, theirs, send, recv, j, sib)))
    own = _rcopy(src, land.at[2 * x + y], send, recv, NLINK, sib)
    return cps + [(own, own)]


RS_ROWS = 256


def _chip_add(own, landed, pos_arr, name):
    nl, r2, cw = landed.shape
    rows = min(RS_ROWS, r2)
    nr = r2 // rows

    def body(s_ref, h_ref, q_ref, o_ref):
        acc = h_ref[...].astype(f32)
        for j in range(nl):
            acc = acc + q_ref[j].astype(f32)
        o_ref[...] = acc

    gs = pltpu.PrefetchScalarGridSpec(
        num_scalar_prefetch=1, grid=(nr,),
        in_specs=[pl.BlockSpec((None, rows, cw), lambda j, s: (s[0], s[1] * nr + j, 0)),
                  pl.BlockSpec((nl, rows, cw), lambda j, s: (0, j, 0))],
        out_specs=pl.BlockSpec((rows, cw), lambda j, s: (s[1] * nr + j, 0)))
    return pl.pallas_call(body, name=name, grid_spec=gs, out_shape=jax.ShapeDtypeStruct((2 * r2, cw), f32),
                          compiler_params=_cp(("arbitrary",)))(pos_arr, own, landed)


def _pair_gather_copies(src, land, send, recv):
    x, y, c = _pos()
    sib = (x, y, 1 - c)
    return [(_rcopy(_half(land, c), _half(land, c), send, recv, 0, sib),
             _rcopy(_half(land, 1 - c), _half(land, 1 - c), send, recv, 0, sib))]


def _adamw_math(w, g, m, v):
    m = B1 * m + (1.0 - B1) * g
    v = B2 * v + (1.0 - B2) * jnp.square(g)
    m_hat = m / (1.0 - B1 ** STEP)
    v_hat = v / (1.0 - B2 ** STEP)
    return -LR * (m_hat / (jnp.sqrt(v_hat) + AEPS) + WD * w), m, v


ADAM_BLOCK = 512 * 1024


def _adamw(w, g, m, v, name, after=()):
    r, cw = w.shape
    na = len(after)

    def body(w_ref, g_ref, m_ref, v_ref, *rest):
        go_ref, d_ref, mo_ref, vo_ref = rest[na:]
        g = g_ref[...]
        go_ref[...] = g
        d_ref[...], mo_ref[...], vo_ref[...] = _adamw_math(w_ref[...], g, m_ref[...], v_ref[...])

    rows = max(k for k in range(SUBLANES, ADAM_BLOCK // cw + 1, SUBLANES) if r % k == 0)
    spec = pl.BlockSpec((rows, cw), lambda i: (i, 0))
    return pl.pallas_call(body, name=name, grid=(r // rows,), in_specs=[spec] * 4 + [pl.BlockSpec(memory_space=pl.ANY)] * na,
                          out_specs=[spec] * 4, out_shape=[jax.ShapeDtypeStruct((r, cw), f32)] * 4,
                          compiler_params=_cp(("arbitrary",)))(w, g, m, v, *after)


SMALL = (("b_ada", None, PAYW), ("g_attn_pre", OFF_G_ATTN_PRE, D), ("g_attn_post", OFF_G_ATTN_POST, D), ("sink_a", OFF_SINK, 8),
         ("g_mix_a", OFF_G_MIX_A, AQ), ("g_mix_b", OFF_G_MIX_B, BW), ("g_mlp_pre", OFF_G_MLP_PRE, D), ("g_mlp_post", OFF_G_MLP_POST, D))


def _adamw_small(small, gb, params):
    n = len(SMALL)

    def body(*refs):
        small_ref, gb_ref = refs[:2]
        wmv = refs[2:2 + 3 * n]
        loss_ref = refs[2 + 3 * n]
        outs = refs[3 + 3 * n:]
        loss_ref[...] = small_ref[:, OFF_LOSS:OFF_LOSS + 1] * (0.5 / D)
        for i, (_, off, width) in enumerate(SMALL):
            g = gb_ref[...] if off is None else small_ref[:, off:off + width]
            w_ref, m_ref, v_ref = wmv[3 * i:3 * i + 3]
            outs[4 * i][...] = g
            outs[4 * i + 1][...], outs[4 * i + 2][...], outs[4 * i + 3][...] = _adamw_math(w_ref[...], g, m_ref[...], v_ref[...])

    vm = pl.BlockSpec(memory_space=pltpu.VMEM)
    out_shape = [jax.ShapeDtypeStruct((1, 1), f32)]
    for _, _, width in SMALL:
        out_shape += [jax.ShapeDtypeStruct((1, width), f32)] * 4
    flat = [a for wmv in params for a in wmv]
    res = pl.pallas_call(body, name="adamw_small", in_specs=[vm] * (2 + 3 * n), out_specs=[vm] * len(out_shape),
                         out_shape=out_shape)(small, gb, *flat)
    return res[0], {name: res[1 + 4 * i:5 + 4 * i] for i, (name, _, _) in enumerate(SMALL)}


def kernel(x, c, positions, w_ada, b_ada, g_attn_pre, g_attn_post, w_in, sink_a, g_mix_a, g_mix_b, w_out, g_mlp_pre, g_mlp_post, w_up, w_down, loss_target, m_w_ada, m_b_ada, m_g_attn_pre, m_g_attn_post, m_w_in, m_sink_a, m_g_mix_a, m_g_mix_b, m_w_out, m_g_mlp_pre, m_g_mlp_post, m_w_up, m_w_down, v_w_ada, v_b_ada, v_g_attn_pre, v_g_attn_post, v_w_in, v_sink_a, v_g_mix_a, v_g_mix_b, v_w_out, v_g_mlp_pre, v_g_mlp_post, v_w_up, v_w_down):
    given = dict(w_ada=w_ada, b_ada=b_ada, g_attn_pre=g_attn_pre, g_attn_post=g_attn_post, w_in=w_in, sink_a=sink_a, g_mix_a=g_mix_a,
                 g_mix_b=g_mix_b, w_out=w_out, g_mlp_pre=g_mlp_pre, g_mlp_post=g_mlp_post, w_up=w_up, w_down=w_down)
    moms = dict(w_ada=(m_w_ada, v_w_ada), b_ada=(m_b_ada, v_b_ada), g_attn_pre=(m_g_attn_pre, v_g_attn_pre),
                g_attn_post=(m_g_attn_post, v_g_attn_post), w_in=(m_w_in, v_w_in), sink_a=(m_sink_a, v_sink_a),
                g_mix_a=(m_g_mix_a, v_g_mix_a), g_mix_b=(m_g_mix_b, v_g_mix_b), w_out=(m_w_out, v_w_out),
                g_mlp_pre=(m_g_mlp_pre, v_g_mlp_pre), g_mlp_post=(m_g_mlp_post, v_g_mlp_post), w_up=(m_w_up, v_w_up),
                w_down=(m_w_down, v_w_down))
    order = ["w_ada", "b_ada", "g_attn_pre", "g_attn_post", "w_in", "sink_a", "g_mix_a", "g_mix_b", "w_out", "g_mlp_pre",
             "g_mlp_post", "w_up", "w_down"]
    xi, yi, ci = _pos()
    chip = 2 * xi + yi

    pos_arr = jnp.stack([chip, ci]).astype(jnp.int32)
    big = ("w_in", "w_out", "w_up", "w_down")

    b_cols = lax.dynamic_slice(b_ada, (0, chip * ADAW), (1, ADAW))
    mod, cond_all = _ada_fwd(c, w_ada[0], b_cols)
    gathered = [jax.ShapeDtypeStruct((NCHIP,) + given[n].shape[1:], bf16) for n in big]
    flight_in, (mod,) = _split_start("weights_start_first", [w_in[0].astype(bf16)], gathered[:1], [_weight_copies], [mod])
    mod, rest = lax.optimization_barrier((mod, [given[n][0] for n in big[1:]]))
    flight_rest, (mod, inv_lane) = _split_start("weights_start_rest", [w.astype(bf16) for w in rest], gathered[1:],
                                                [_weight_copies] * 3, [mod, _inv_lane()])
    mod = mod.reshape(BL, NMOD, D)

    def first_weight(after):
        srcs, lands = _split_wait("weights_wait_first", flight_in, [_weight_copies], after)
        cross, _ = _split_start("weights_pair_start_first", srcs, None, [_pair_weight_copies], [], lands=lands)
        _, (win_g,) = _split_wait("weights_pair_wait_first", cross, [_pair_weight_copies], ())
        return win_g

    def later_weights(after, carry):
        srcs, lands = _split_wait("weights_wait_rest", flight_rest, [_weight_copies] * 3, after)
        fl, (carry,) = _split_start("weights_pair_start_rest", srcs, None, [_pair_weight_copies] * 3, [carry], lands=lands)
        _, (wout_g,) = _split_wait("weights_pair_wait_out", fl[:1], [_pair_weight_copies], ())

        def mlp_weights(after):
            _, (wup_g, wdn_g) = _split_wait("weights_pair_wait_mlp", fl[1:], [_pair_weight_copies] * 2, after)
            return wup_g, wdn_g.reshape(DFF, D)

        return wout_g.reshape(D, D), mlp_weights, carry

    waiting, pending = {}, {}

    def send_grad(name, slab, carry):
        land = jax.ShapeDtypeStruct((NDIRECT, slab.shape[1] // 2, slab.shape[2]), bf16)
        pending[name], (carry,) = _split_start("grad_start_" + name, [slab], [land], [_direct_grad_copies], [carry])
        return carry

    def grad_ready(name, g, carry):
        slab = g if g.ndim == 3 else g.reshape(NCHIP, g.shape[0] // NCHIP, g.shape[1])
        if name == "w_in":
            waiting[name] = slab
            return carry
        return send_grad(name, slab, carry)

    grad_x, accs = _local_step(x, positions, mod, loss_target, inv_lane, first_weight, later_weights, grad_ready,
                               g_attn_pre, g_attn_post, sink_a, g_mix_a, g_mix_b, g_mlp_pre, g_mlp_post)

    grads, out = {}, {}

    def update(n, after=()):
        tr = (lambda a: a.T) if n == "w_in" else (lambda a: a)
        res = _adamw(tr(given[n][0]), tr(grads[n]), tr(moms[n][0][0]), tr(moms[n][1][0]), "adamw_" + n, after)
        out[n] = tuple(tr(a)[None] for a in res)
        return res[3]

    def finish(names, after, first=()):
        fl = sum((pending[n] for n in names), [])
        halves, landed = _split_wait("grad_wait_" + names[0], fl, [_direct_grad_copies] * len(names), after)
        flights, token = [], jnp.zeros((SUBLANES, LANES), f32)
        for h, q, n in zip(halves, landed, names):
            full = _chip_add(h, q, pos_arr, "grad_chip_sum_" + n)
            flights.append(_split_start("grad_gather_start_" + n, [token], None, [_pair_gather_copies], [], lands=[full])[0])
            token = flights[-1][0][0]
        last = [update(n, (token,)) for n in first]
        for n, fl1 in zip(names, flights):
            after = tuple(last) if last else () if fl1 is flights[-1] else (token,)
            _, (grads[n],) = _split_wait("grad_gather_wait_" + n, fl1, [_pair_gather_copies], after)
            last = [update(n)]
        return last[0]

    grads["w_ada"], gb, small = _small_allreduce(accs, cond_all)
    small = send_grad("w_in", waiting["w_in"], small)
    last = finish(("w_down", "w_up", "w_out"), (small,))
    finish(("w_in",), (last,), first=("w_ada",))
    loss, res = _adamw_small(small, gb, [(given[n], moms[n][0], moms[n][1]) for n, _, _ in SMALL])
    for n, _, _ in SMALL:
        out[n] = tuple(res[n])
    return (loss.reshape(()), grad_x, *[out[n][0] for n in order], *[out[n][1] for n in order],
            *[out[n][2] for n in order], *[out[n][3] for n in order])
```

```python
import numpy as np
import jax
import jax.numpy as jnp
from jax import lax
from jax.experimental import pallas as pl
from jax.experimental.pallas import tpu as pltpu

f32 = jnp.float32
bf16 = jnp.bfloat16
MESH = pl.DeviceIdType.MESH

D = 1024
SEQ = 2048
BL = 2
HD = 64
AQ = 512
AKV = 128
BW = 512
INW = 2304
DFF = 4096
NMOD = 6
ROT = 16
THETA = 500000.0
EPS = 1e-6
NEG = -1e30
BLK = 128
TM = 512
NJ = SEQ // TM
LANES = 128
SUBLANES = 8
NHEAD = AQ // HD
QSCALE = HD ** -0.5
NCHIP = 4
NDEV = 8
VMEM_LIMIT = 56 << 20

LR, B1, B2, AEPS, WD, STEP = 0.001, 0.9, 0.999, 1e-08, 0.01, 10

OFF_G_ATTN_PRE, OFF_G_ATTN_POST, OFF_G_MIX_A, OFF_G_MIX_B = 0, 1024, 2048, 2560
OFF_G_MLP_PRE, OFF_G_MLP_POST, OFF_SINK, OFF_LOSS = 3072, 4096, 5120, 5248
PAYW = NMOD * D


def _cp(sem=None):
    return pltpu.CompilerParams(dimension_semantics=sem, vmem_limit_bytes=VMEM_LIMIT)


def _dot(a, b):
    return jnp.dot(a, b, preferred_element_type=f32)


def _dot_nt(a, b):
    return lax.dot_general(a, b, (((1,), (1,)), ((), ())), preferred_element_type=f32)


def _dot_tn(a, b):
    return lax.dot_general(a, b, (((0,), (0,)), ((), ())), preferred_element_type=f32)


def _rms(x):
    r = lax.rsqrt(jnp.mean(x * x, axis=-1, keepdims=True) + EPS)
    return x * r, r


def _rms_bwd(dy, y, r):
    return r * (dy - y * jnp.mean(dy * y, axis=-1, keepdims=True))


def _colsum(v):
    return jnp.sum(v, axis=0, keepdims=True)


def _rope(p, c, s1, s2):
    outs = []
    for c0 in range(0, p.shape[1], LANES):
        pc = p[:, c0:c0 + LANES]
        outs.append(pc * c + pltpu.roll(pc, LANES - ROT // 2, 1) * s1 + pltpu.roll(pc, ROT // 2, 1) * s2)
    return outs[0] if len(outs) == 1 else jnp.concatenate(outs, axis=1)


def _rope_t(g, c, s1, s2):
    outs = []
    for c0 in range(0, g.shape[1], LANES):
        gc = g[:, c0:c0 + LANES]
        outs.append(gc * c + pltpu.roll(gc * s1, ROT // 2, 1) + pltpu.roll(gc * s2, LANES - ROT // 2, 1))
    return outs[0] if len(outs) == 1 else jnp.concatenate(outs, axis=1)


def _perm_store(val, scr, out_ref, d):
    nc = val.shape[1] // LANES
    for c in range(nc):
        scr[c] = val[:, LANES * c:LANES * (c + 1)]
    for c in range(nc):
        for r in range(d):
            out_ref[r, :, LANES * c:LANES * (c + 1)] = scr[c, pl.ds(r, TM // d, stride=d), :].astype(out_ref.dtype)


def _perm_load(in_ref, scr, d):
    nc = in_ref.shape[-1] // LANES
    for c in range(nc):
        for r in range(d):
            scr[c, pl.ds(r, TM // d, stride=d), :] = in_ref[r, :, LANES * c:LANES * (c + 1)].astype(f32)
    return jnp.concatenate([scr[c] for c in range(nc)], axis=1)


def _per_query_head(kv):
    r = pltpu.roll(kv, HD, 1)
    lo = lax.broadcasted_iota(jnp.int32, kv.shape, 1) < HD
    return jnp.concatenate([jnp.where(lo, kv, r), jnp.where(lo, r, kv)], axis=1)


def _per_kv_head(g):
    g0, g1 = g[:, :LANES] + g[:, LANES:2 * LANES], g[:, 2 * LANES:3 * LANES] + g[:, 3 * LANES:]
    lo = lax.broadcasted_iota(jnp.int32, g0.shape, 1) < HD
    return jnp.where(lo, g0 + pltpu.roll(g0, HD, 1), g1 + pltpu.roll(g1, HD, 1))


def _tok(w):
    return pl.BlockSpec((None, TM, w), lambda b, j: (b, j, 0))


def _perm_spec(d, w):
    return pl.BlockSpec((None, d, TM // d, w), lambda b, j: (b, 0, j, 0))


def _full(shape):
    n = len(shape)
    return pl.BlockSpec(shape, lambda b, j: (0,) * n)


MOD_SPEC = pl.BlockSpec((None, NMOD, D), lambda b, j: (b, 0, 0))
ACCB_SPEC = pl.BlockSpec((None, SUBLANES, D), lambda b, j: (b, 0, 0))
ACCG_SPEC = pl.BlockSpec((SUBLANES, D), lambda b, j: (0, 0))
ACC_SHAPES = [jax.ShapeDtypeStruct((BL, SUBLANES, D), f32), jax.ShapeDtypeStruct((SUBLANES, D), f32)]


def _acc_init(accb_ref, accg_ref):
    b, j = pl.program_id(0), pl.program_id(1)

    @pl.when(j == 0)
    def _():
        accb_ref[...] = jnp.zeros_like(accb_ref)

    @pl.when((b == 0) & (j == 0))
    def _():
        accg_ref[...] = jnp.zeros_like(accg_ref)


def _rope_tables(pos_col, inv_lane):
    def body(p_ref, inv_ref, c_ref, s1_ref, s2_ref):
        ang = p_ref[...].astype(f32) * inv_ref[...]
        j = lax.broadcasted_iota(jnp.int32, (TM, LANES), 1) % HD
        cs, sn = jnp.cos(ang), jnp.sin(ang)
        c_ref[...] = jnp.where(j < ROT, cs, 1.0)
        s1_ref[...] = jnp.where(j < ROT // 2, -sn, 0.0)
        s2_ref[...] = jnp.where((j >= ROT // 2) & (j < ROT), sn, 0.0)

    n = BL * SEQ // TM
    return pl.pallas_call(
        body, name="rope_tables", grid=(n,),
        in_specs=[pl.BlockSpec((TM, 1), lambda i: (i, 0)), pl.BlockSpec((1, LANES), lambda i: (0, 0))],
        out_specs=[pl.BlockSpec((TM, LANES), lambda i: (i, 0))] * 3,
        out_shape=[jax.ShapeDtypeStruct((BL * SEQ, LANES), f32)] * 3,
    )(pos_col, inv_lane)


def _attn_in(x, mod, g_pre, w_in, tc, ts1, ts2):
    def body(x_ref, mod_ref, g_ref, wg_ref, c_ref, s1_ref, s2_ref,
             h_ref, qa_ref, ka_ref, va_ref, q1_ref, k1_ref, v1_ref, q4_ref, k4_ref, v4_ref, q16_ref, k16_ref, v16_ref,
             w_ref, scr):
        @pl.when((pl.program_id(0) == 0) & (pl.program_id(1) == 0))
        def _():
            w_ref[...] = jnp.concatenate([wg_ref[s] for s in range(NCHIP)], axis=1)

        xn, _ = _rms(x_ref[...])
        h = (xn * g_ref[...]) * (1.0 + mod_ref[1:2, :]) + mod_ref[0:1, :]
        hb = h.astype(bf16)
        h_ref[...] = hb
        proj = _dot(hb, w_ref[...])
        c, s1, s2 = c_ref[...], s1_ref[...], s2_ref[...]
        o1, o2, o3, o4, o5 = AQ, AQ + AKV, AQ + 2 * AKV, AQ + 2 * AKV + BW, AQ + 2 * AKV + 2 * BW
        qa_ref[...] = (_rope(proj[:, :o1], c, s1, s2) * QSCALE).astype(bf16)
        ka_ref[...] = _per_query_head(_rope(proj[:, o1:o2], c, s1, s2)).astype(bf16)
        va_ref[...] = _per_query_head(proj[:, o2:o3]).astype(bf16)
        qb = _rope(proj[:, o3:o4], c, s1, s2) * QSCALE
        kb = _rope(proj[:, o4:o5], c, s1, s2)
        vb = proj[:, o5:]
        for val, r1, r4, r16 in ((qb, q1_ref, q4_ref, q16_ref), (kb, k1_ref, k4_ref, k16_ref), (vb, v1_ref, v4_ref, v16_ref)):
            r1[...] = val.astype(bf16)
            _perm_store(val, scr, r4, 4)
            _perm_store(val, scr, r16, 16)

    nat = lambda w: jax.ShapeDtypeStruct((BL, SEQ, w), bf16)
    p4 = jax.ShapeDtypeStruct((BL, 4, SEQ // 4, BW), bf16)
    p16 = jax.ShapeDtypeStruct((BL, 16, SEQ // 16, BW), bf16)
    return pl.pallas_call(
        body, name="attn_in", grid=(BL, NJ),
        in_specs=[_tok(D), MOD_SPEC, _full((1, D)), _full((NCHIP, D, INW // NCHIP)), _tok(LANES), _tok(LANES), _tok(LANES)],
        out_specs=([_tok(D), _tok(AQ), _tok(2 * AKV), _tok(2 * AKV)] + [_tok(BW)] * 3 + [_perm_spec(4, BW)] * 3 + [_perm_spec(16, BW)] * 3
                   + [_full((D, INW))]),
        out_shape=[nat(D), nat(AQ), nat(2 * AKV), nat(2 * AKV)] + [nat(BW)] * 3 + [p4] * 3 + [p16] * 3
                  + [jax.ShapeDtypeStruct((D, INW), bf16)],
        scratch_shapes=[pltpu.VMEM((BW // LANES, TM, LANES), f32)],
        compiler_params=_cp(("arbitrary", "arbitrary")),
    )(x, mod, g_pre, w_in, tc, ts1, ts2)


def _kv_cat(cur_ref, prev_ref, p, cache):
    key = (id(cur_ref), p)
    if key not in cache:
        sl = slice(LANES * p, LANES * (p + 1))
        cache[key] = cur_ref[:, sl] if prev_ref is None else jnp.concatenate([prev_ref[:, sl], cur_ref[:, sl]], axis=0)
    return cache[key]


def _lane_half(a, hh):
    lo = lax.broadcasted_iota(jnp.int32, a.shape, 1) < HD
    return jnp.where(lo, a, jnp.zeros_like(a)) if hh == 0 else jnp.where(lo, jnp.zeros_like(a), a)


ATT_UNITS = 4


def _att_units(nb):
    return ATT_UNITS if nb == 1 else min(ATT_UNITS, nb)


def _attn_specs(n, nb, descending):
    u = _att_units(nb)
    if nb == 1:
        return (lambda ww: pl.BlockSpec((u, BLK, ww), lambda a, i: (a, 0, 0))), None, (n // u, 1)
    steps = nb // u
    at = (lambda i: steps - 1 - i) if descending else (lambda i: i)
    cur = lambda ww: pl.BlockSpec((None, u * BLK, ww), lambda a, i: (a, at(i), 0))
    prev = lambda ww: pl.BlockSpec((None, BLK, ww), lambda a, i: (a, jnp.maximum(u * at(i) - 1, 0), 0))
    return cur, prev, (n, steps)


def _attn_fwd(q, k, v, sink, *, max_dist, o_dtype, name):
    n, l, w = q.shape
    wk = k.shape[-1]
    nb = l // BLK
    has_sink = sink is not None

    def body(*refs):
        sink_ref = None
        if has_sink:
            sink_ref, refs = refs[0], refs[1:]
        if nb > 1:
            q_ref, kc_ref, kp_ref, vc_ref, vp_ref, o_ref, lse_ref = refs[:7]
            first = pl.program_id(1) == 0
            for u in range(_att_units(nb)):
                rows, before = pl.ds(BLK * u, BLK), pl.ds(BLK * (u - 1), BLK)
                unit(q_ref.at[rows, :], kc_ref.at[rows, :], kp_ref if u == 0 else kc_ref.at[before, :],
                     vc_ref.at[rows, :], vp_ref if u == 0 else vc_ref.at[before, :], o_ref.at[rows, :], lse_ref.at[rows, :],
                     jnp.logical_not(first) if u == 0 else True, sink_ref, *refs[7:])
        else:
            q_ref, kc_ref, vc_ref, o_ref, lse_ref = refs[:5]
            for u in range(_att_units(nb)):
                unit(q_ref.at[u], kc_ref.at[u], None, vc_ref.at[u], None, o_ref.at[u], lse_ref.at[u], None, sink_ref, *refs[5:])

    def unit(q_ref, kc_ref, kp_ref, vc_ref, vp_ref, o_ref, lse_ref, has_prev, sink_ref, sscr, pscr, dscr):
        qi = lax.broadcasted_iota(jnp.int32, (BLK, BLK), 0)
        kj = lax.broadcasted_iota(jnp.int32, (BLK, BLK), 1)
        tri = kj <= qi
        eye = kj == qi
        cache = {}
        for p in range(w // LANES):
            qpair = q_ref[:, LANES * p:LANES * (p + 1)]
            kcat = _kv_cat(kc_ref, kp_ref, p // share, cache)
            for hh in range(2):
                s = _dot_nt(_lane_half(qpair, hh), kcat)
                if nb > 1:
                    sp = s[:, :BLK] if has_prev is True else jnp.where(has_prev, s[:, :BLK], NEG)
                    sscr[2 * p + hh] = jnp.where(tri, s[:, BLK:], sp)
                    if diag:
                        dscr[2 * p + hh] = jnp.where(eye, sp, NEG)
                else:
                    sscr[2 * p + hh] = jnp.where(tri, s, NEG)
        lane = lax.broadcasted_iota(jnp.int32, (BLK, LANES), 1)
        lse_all = jnp.zeros((BLK, LANES), f32)
        for p in range(w // LANES):
            for hh in range(2):
                h = 2 * p + hh
                comb = sscr[h]
                if diag:
                    dtile = dscr[h]
                    m = jnp.max(jnp.maximum(comb, dtile), axis=-1, keepdims=True)
                else:
                    m = jnp.max(comb, axis=-1, keepdims=True)
                if has_sink:
                    sk = sink_ref[0, h]
                    m = jnp.maximum(m, sk)
                e = jnp.exp(comb - m)
                if diag:
                    ed = jnp.exp(dtile - m)
                    den = jnp.sum(e + ed, axis=-1, keepdims=True)
                else:
                    den = jnp.sum(e, axis=-1, keepdims=True)
                if has_sink:
                    den = den + jnp.exp(sk - m)
                inv = 1.0 / den
                if nb > 1:
                    pscr[h, :, :BLK] = (jnp.where(tri, ed if diag else 0.0, e) * inv).astype(bf16)
                    pscr[h, :, BLK:] = (jnp.where(tri, e, 0.0) * inv).astype(bf16)
                else:
                    pscr[h] = (e * inv).astype(bf16)
                lse_all = jnp.where(lane == h, jnp.broadcast_to(m + jnp.log(den), (BLK, LANES)), lse_all)
        lse_ref[...] = lse_all
        for p in range(w // LANES):
            vcat = _kv_cat(vc_ref, vp_ref, p // share, cache)
            o_ref[:, LANES * p:LANES * (p + 1)] = (_dot(pscr[2 * p], _lane_half(vcat, 0))
                                                   + _dot(pscr[2 * p + 1], _lane_half(vcat, 1))).astype(o_ref.dtype)

    assert max_dist in (BLK - 1, BLK) and w % wk == 0
    share = w // wk
    diag = nb > 1 and max_dist == BLK
    cur, prev, grid = _attn_specs(n, nb, False)
    in_specs = [cur(w), cur(wk)] + ([prev(wk)] if nb > 1 else []) + [cur(wk)] + ([prev(wk)] if nb > 1 else [])
    args = [q, k] + ([k] if nb > 1 else []) + [v] + ([v] if nb > 1 else [])
    if has_sink:
        in_specs = [pl.BlockSpec(memory_space=pltpu.SMEM)] + in_specs
        args = [sink] + args
    return pl.pallas_call(
        body, name=name, grid=grid, in_specs=in_specs,
        out_specs=[cur(w), cur(LANES)],
        out_shape=[jax.ShapeDtypeStruct((n, l, w), o_dtype), jax.ShapeDtypeStruct((n, l, LANES), f32)],
        scratch_shapes=[pltpu.VMEM((w // HD, BLK, BLK), f32), pltpu.VMEM((w // HD, BLK, 2 * BLK if nb > 1 else BLK), bf16),
                        pltpu.VMEM((w // HD if diag else 1, BLK, BLK), f32)],
        compiler_params=_cp(("arbitrary", "arbitrary")),
    )(*args)


def _attn_bwd(q, k, v, do, delta, lse, sink, *, max_dist, name):
    n, l, w = q.shape
    wk = k.shape[-1]
    nb = l // BLK
    has_sink = sink is not None

    def body(*refs):
        sink_ref = dsink_ref = ck = cv = None
        if has_sink:
            sink_ref, refs = refs[0], refs[1:]
        nin = 8 if nb > 1 else 6
        ins, rest = refs[:nin], refs[nin:]
        if has_sink:
            dq_ref, dk_ref, dv_ref, dsink_ref = rest[:4]
            rest = rest[4:]
        else:
            dq_ref, dk_ref, dv_ref = rest[:3]
            rest = rest[3:]
        step = pl.program_id(1)
        if has_sink:
            @pl.when((pl.program_id(0) == 0) & (step == 0))
            def _():
                dsink_ref[...] = jnp.zeros_like(dsink_ref)

        if nb > 1:
            q_ref, kc_ref, kp_ref, vc_ref, vp_ref, do_ref, delta_ref, lse_ref = ins
            ck, cv = rest[:2]

            @pl.when(step == 0)
            def _():
                ck[...] = jnp.zeros_like(ck)
                cv[...] = jnp.zeros_like(cv)

            last = step == nb // _att_units(nb) - 1
            for u in reversed(range(_att_units(nb))):
                rows, before = pl.ds(BLK * u, BLK), pl.ds(BLK * (u - 1), BLK)
                unit(q_ref.at[rows, :], kc_ref.at[rows, :], kp_ref if u == 0 else kc_ref.at[before, :],
                     vc_ref.at[rows, :], vp_ref if u == 0 else vc_ref.at[before, :], do_ref.at[rows, :],
                     delta_ref.at[rows, :], lse_ref.at[rows, :], dq_ref.at[rows, :], dk_ref.at[rows, :], dv_ref.at[rows, :],
                     jnp.logical_not(last) if u == 0 else True, sink_ref, dsink_ref, ck, cv, *rest[2:])
        else:
            q_ref, kc_ref, vc_ref, do_ref, delta_ref, lse_ref = ins
            for u in range(_att_units(nb)):
                unit(q_ref.at[u], kc_ref.at[u], None, vc_ref.at[u], None, do_ref.at[u], delta_ref.at[u], lse_ref.at[u],
                     dq_ref.at[u], dk_ref.at[u], dv_ref.at[u], None, sink_ref, dsink_ref, None, None, *rest)

    def unit(q_ref, kc_ref, kp_ref, vc_ref, vp_ref, do_ref, delta_ref, lse_ref, dq_ref, dk_ref, dv_ref, has_prev,
             sink_ref, dsink_ref, ck, cv, sscr, dpscr, pscr, dsscr, dscr=None, ddscr=None):
        lane = lax.broadcasted_iota(jnp.int32, (BLK, LANES), 1)
        qi = lax.broadcasted_iota(jnp.int32, (BLK, BLK), 0)
        kj = lax.broadcasted_iota(jnp.int32, (BLK, BLK), 1)
        tri = kj <= qi
        eye = kj == qi
        cache = {}
        kp, vp = kp_ref, vp_ref
        for p in range(w // LANES):
            sl = slice(LANES * p, LANES * (p + 1))
            qpair, dopair = q_ref[:, sl], do_ref[:, sl]
            kcat, vcat = _kv_cat(kc_ref, kp, p // share, cache), _kv_cat(vc_ref, vp, p // share, cache)
            for hh in range(2):
                h = 2 * p + hh
                s = _dot_nt(_lane_half(qpair, hh), kcat)
                dp = _dot_nt(_lane_half(dopair, hh), vcat)
                if nb > 1:
                    sp = s[:, :BLK] if has_prev is True else jnp.where(has_prev, s[:, :BLK], NEG)
                    sscr[h] = jnp.where(tri, s[:, BLK:], sp)
                    dpscr[h] = jnp.where(tri, dp[:, BLK:], dp[:, :BLK])
                    if diag:
                        dscr[h] = jnp.where(eye, sp, NEG)
                        ddscr[h] = dp[:, :BLK]
                else:
                    sscr[h] = jnp.where(tri, s, NEG)
                    dpscr[h] = dp
        for p in range(w // LANES):
            for hh in range(2):
                h = 2 * p + hh
                lse_b = jnp.broadcast_to(lse_ref[:, h:h + 1], (BLK, BLK))
                delta = jnp.broadcast_to(delta_ref[:, h:h + 1], (BLK, BLK))
                pr = jnp.exp(sscr[h] - lse_b)
                ds = pr * (dpscr[h] - delta)
                if nb > 1:
                    if diag:
                        prd = jnp.exp(dscr[h] - lse_b)
                        dsd = prd * (ddscr[h] - delta)
                    else:
                        prd = dsd = 0.0
                    pscr[h, :, :BLK] = jnp.where(tri, prd, pr).astype(bf16)
                    pscr[h, :, BLK:] = jnp.where(tri, pr, 0.0).astype(bf16)
                    dsscr[h, :, :BLK] = jnp.where(tri, dsd, ds).astype(bf16)
                    dsscr[h, :, BLK:] = jnp.where(tri, ds, 0.0).astype(bf16)
                else:
                    pscr[h] = pr.astype(bf16)
                    dsscr[h] = ds.astype(bf16)
                if has_sink:
                    dsk = -jnp.sum(jnp.where(lane == 0, jnp.exp(sink_ref[0, h] - lse_b) * delta, 0.0), keepdims=True)
                    dsink_ref[h:h + 1, :] += jnp.broadcast_to(dsk, (1, LANES))
        for p in range(w // LANES):
            sl = slice(LANES * p, LANES * (p + 1))
            qpair, dopair = q_ref[:, sl], do_ref[:, sl]
            kcat = _kv_cat(kc_ref, kp, p // share, cache)
            dq_ref[:, sl] = _dot(dsscr[2 * p], _lane_half(kcat, 0)) + _dot(dsscr[2 * p + 1], _lane_half(kcat, 1))
            dk_pair = _dot_tn(dsscr[2 * p], _lane_half(qpair, 0)) + _dot_tn(dsscr[2 * p + 1], _lane_half(qpair, 1))
            dv_pair = _dot_tn(pscr[2 * p], _lane_half(dopair, 0)) + _dot_tn(pscr[2 * p + 1], _lane_half(dopair, 1))
            if nb > 1:
                dk_ref[:, sl] = dk_pair[BLK:] + ck[:, sl]
                dv_ref[:, sl] = dv_pair[BLK:] + cv[:, sl]
                ck[:, sl] = dk_pair[:BLK]
                cv[:, sl] = dv_pair[:BLK]
            else:
                dk_ref[:, sl] = dk_pair
                dv_ref[:, sl] = dv_pair

    assert max_dist in (BLK - 1, BLK) and w % wk == 0
    share = w // wk
    diag = nb > 1 and max_dist == BLK
    cur, prev, grid = _attn_specs(n, nb, True)
    in_specs = ([cur(w), cur(wk)] + ([prev(wk)] if nb > 1 else []) + [cur(wk)] + ([prev(wk)] if nb > 1 else [])
                + [cur(w), cur(LANES), cur(LANES)])
    args = [q, k] + ([k] if nb > 1 else []) + [v] + ([v] if nb > 1 else []) + [do, delta, lse]
    out_specs = [cur(w)] * 3
    out_shape = [jax.ShapeDtypeStruct((n, l, w), f32)] * 3
    if has_sink:
        in_specs = [pl.BlockSpec(memory_space=pltpu.SMEM)] + in_specs
        args = [sink] + args
        out_specs.append(pl.BlockSpec((NHEAD, LANES), lambda a, i: (0, 0)))
        out_shape.append(jax.ShapeDtypeStruct((NHEAD, LANES), f32))
    nh = w // HD
    scratch = [pltpu.VMEM((BLK, w), f32), pltpu.VMEM((BLK, w), f32)] if nb > 1 else []
    scratch += [pltpu.VMEM((nh, BLK, BLK), f32)] * 2 + [pltpu.VMEM((nh, BLK, 2 * BLK if nb > 1 else BLK), bf16)] * 2
    if diag:
        scratch += [pltpu.VMEM((nh, BLK, BLK), f32)] * 2
    return pl.pallas_call(
        body, name=name, grid=grid, in_specs=in_specs, out_specs=out_specs, out_shape=out_shape,
        scratch_shapes=scratch, compiler_params=_cp(("arbitrary", "arbitrary")),
    )(*args)


def _split2(x):
    hi = x.astype(bf16)
    return hi, (x - hi.astype(f32)).astype(bf16)


def _heads_to_lanes(xc, e):
    return sum(_dot(t, e) for t in _split2(xc))


def _lanes_to_heads(x, g):
    return sum(_dot(t, g) for t in _split2(x))


HEAD_EXPAND = (np.arange(LANES)[:, None] == np.arange(BW)[None, :] // HD).astype(np.float32)
HEAD_SUM = HEAD_EXPAND.T.copy()


def _branch_weights(l1_ref, l4_ref, l16_ref, scr):
    l4v = _perm_load(l4_ref, scr, 4)
    l16v = _perm_load(l16_ref, scr, 16)
    l1v = l1_ref[...]
    m = jnp.maximum(jnp.maximum(l1v, l4v), l16v)
    e1, e4, e16 = jnp.exp(l1v - m), jnp.exp(l4v - m), jnp.exp(l16v - m)
    z = e1 + e4 + e16
    return e1 / z, e4 / z, e16 / z


def _mix_out(oa, o1, l1, o4, l4, o16, l16, g_mix_a, g_mix_b, w_out, x, mod, g_post):
    def body(oa_ref, o1_ref, l1_ref, o4_ref, l4_ref, o16_ref, l16_ref, ga_ref, gb_ref, w_ref, x_ref, mod_ref, gp_ref, e_ref,
             x1_ref, y_ref, mixed_ref, ob_ref, scr):
        w1, w4, w16 = _branch_weights(l1_ref, l4_ref, l16_ref, scr)
        e = e_ref[...]
        x1w, x4w = _heads_to_lanes(w1, e), _heads_to_lanes(w4, e)
        ob = (x1w * o1_ref[...].astype(f32) + x4w * _perm_load(o4_ref, scr, 4)
              + (1.0 - x1w - x4w) * _perm_load(o16_ref, scr, 16))
        ob_ref[...] = ob
        oan, _ = _rms(oa_ref[...])
        obn, _ = _rms(ob)
        mixed = jnp.concatenate([oan * ga_ref[...], obn * gb_ref[...]], axis=1).astype(bf16)
        mixed_ref[...] = mixed
        y = _dot(mixed, w_ref[...])
        y_ref[...] = y
        yn, _ = _rms(y)
        x1_ref[...] = x_ref[...] + mod_ref[2:3, :] * (yn * gp_ref[...])

    nat = lambda w, dt: jax.ShapeDtypeStruct((BL, SEQ, w), dt)
    return pl.pallas_call(
        body, name="mix_out", grid=(BL, NJ),
        in_specs=[_tok(AQ), _tok(BW), _tok(LANES), _perm_spec(4, BW), _perm_spec(4, LANES), _perm_spec(16, BW),
                  _perm_spec(16, LANES), _full((1, AQ)), _full((1, BW)), _full((D, D)), _tok(D), MOD_SPEC, _full((1, D)),
                  _full((LANES, BW))],
        out_specs=[_tok(D), _tok(D), _tok(D), _tok(BW)],
        out_shape=[nat(D, f32), nat(D, f32), nat(D, bf16), nat(BW, f32)],
        scratch_shapes=[pltpu.VMEM((BW // LANES, TM, LANES), f32)],
        compiler_params=_cp(("arbitrary", "arbitrary")),
    )(oa, o1, l1, o4, l4, o16, l16, g_mix_a, g_mix_b, w_out, x, mod, g_post, jnp.asarray(HEAD_EXPAND, bf16))


def _mlp_up(x1, mod, g_pre, w_up):
    def body(x_ref, mod_ref, g_ref, w_ref, h_ref, u_ref, a_ref):
        xn, _ = _rms(x_ref[...])
        h = (xn * g_ref[...]) * (1.0 + mod_ref[4:5, :]) + mod_ref[3:4, :]
        hb = h.astype(bf16)
        h_ref[...] = hb
        for s in range(NCHIP):
            u = _dot(hb, w_ref[s])
            u_ref[:, D * s:D * (s + 1)] = u.astype(bf16)
            a_ref[:, D * s:D * (s + 1)] = jnp.square(jnp.maximum(u, 0.0)).astype(bf16)

    nat = lambda w: jax.ShapeDtypeStruct((BL, SEQ, w), bf16)
    return pl.pallas_call(
        body, name="mlp_up", grid=(BL, NJ),
        in_specs=[_tok(D), MOD_SPEC, _full((1, D)), _full((NCHIP, D, D))],
        out_specs=[_tok(D), _tok(DFF), _tok(DFF)], out_shape=[nat(D), nat(DFF), nat(DFF)],
        compiler_params=_cp(("arbitrary", "arbitrary")),
    )(x1, mod, g_pre, w_up)


def _mlp_down(a, w_down, x1, target, mod, g_post):
    def body(a_ref, w_ref, x_ref, t_ref, mod_ref, g_ref, gx_ref, dy_ref, accb_ref, accg_ref):
        _acc_init(accb_ref, accg_ref)
        y2 = _dot(a_ref[...], w_ref[...])
        yn, r = _rms(y2)
        g = g_ref[...]
        gt = mod_ref[5:6, :]
        n2 = yn * g
        err = x_ref[...] + gt * n2 - t_ref[...]
        gout = err * (1.0 / D)
        gx_ref[...] = gout
        dn2 = gout * gt
        dy_ref[...] = _rms_bwd(dn2 * g, yn, r).astype(bf16)
        accb_ref[0:1, :] += _colsum(gout * n2)
        accg_ref[0:1, :] += _colsum(dn2 * yn)
        accg_ref[1:2, :] += jnp.broadcast_to(jnp.sum(err * err, keepdims=True), (1, D))

    return pl.pallas_call(
        body, name="mlp_down", grid=(BL, NJ),
        in_specs=[_tok(DFF), _full((DFF, D)), _tok(D), _tok(D), MOD_SPEC, _full((1, D))],
        out_specs=[_tok(D), _tok(D), ACCB_SPEC, ACCG_SPEC],
        out_shape=[jax.ShapeDtypeStruct((BL, SEQ, D), f32), jax.ShapeDtypeStruct((BL, SEQ, D), bf16)] + ACC_SHAPES,
        compiler_params=_cp(("arbitrary", "arbitrary")),
    )(a, w_down, x1, target, mod, g_post)


def _mlp_bwd(dy2, u, w_down, w_up, x1, gx, mod, g_pre):
    def body(dy_ref, u_ref, wd_hbm, wu_hbm, x_ref, gx_ref, mod_ref, g_ref, du_ref, gx1_ref, accb_ref, accg_ref, wd, wu, sem):
        _acc_init(accb_ref, accg_ref)
        first = (pl.program_id(0) == 0) & (pl.program_id(1) == 0)
        c1 = pltpu.make_async_copy(wd_hbm, wd, sem.at[0])
        c2 = pltpu.make_async_copy(wu_hbm, wu, sem.at[1])

        @pl.when(first)
        def _():
            c1.start()
            c2.start()
            c1.wait()

        dy = dy_ref[...]
        for s in range(NCHIP):
            sl = slice(D * s, D * (s + 1))
            da = _dot_nt(dy, wd[sl, :])
            du_ref[:, sl] = (da * (2.0 * jnp.maximum(u_ref[:, sl].astype(f32), 0.0))).astype(bf16)

        @pl.when(first)
        def _():
            c2.wait()

        dh = jnp.zeros((TM, D), f32)
        for s in range(NCHIP):
            dh = dh + _dot_nt(du_ref[:, D * s:D * (s + 1)], wu[s])
        xn, r = _rms(x_ref[...])
        g = g_ref[...]
        n = xn * g
        dn = dh * (1.0 + mod_ref[4:5, :])
        gx1_ref[...] = gx_ref[...] + _rms_bwd(dn * g, xn, r)
        accb_ref[0:1, :] += _colsum(dh * n)
        accb_ref[1:2, :] += _colsum(dh)
        accg_ref[0:1, :] += _colsum(dn * xn)

    anyspec = pl.BlockSpec(memory_space=pl.ANY)
    return pl.pallas_call(
        body, name="mlp_bwd", grid=(BL, NJ),
        in_specs=[_tok(D), _tok(DFF), anyspec, anyspec, _tok(D), _tok(D), MOD_SPEC, _full((1, D))],
        out_specs=[_tok(DFF), _tok(D), ACCB_SPEC, ACCG_SPEC],
        out_shape=[jax.ShapeDtypeStruct((BL, SEQ, DFF), bf16), jax.ShapeDtypeStruct((BL, SEQ, D), f32)] + ACC_SHAPES,
        scratch_shapes=[pltpu.VMEM((DFF, D), bf16), pltpu.VMEM((NCHIP, D, D), bf16), pltpu.SemaphoreType.DMA((2,))],
        compiler_params=_cp(("arbitrary", "arbitrary")),
    )(dy2, u, w_down, w_up, x1, gx, mod, g_pre)


def _matmul_tn(a, b, *, tn, col_blocked, name, out_dtype=f32):
    t, m = a.shape
    n = b.shape[1]
    tmm = min(m, 1024)
    tk = 2048 if tn <= 1024 else 1024
    nk = t // tk

    def body(a_ref, b_ref, o_ref, acc):
        k = pl.program_id(2)

        @pl.when(k == 0)
        def _():
            acc[...] = jnp.zeros_like(acc)

        acc[...] += _dot_tn(a_ref[...], b_ref[...])

        @pl.when(k == nk - 1)
        def _():
            o_ref[...] = acc[...].astype(out_dtype)

    if col_blocked:
        out_spec = pl.BlockSpec((None, tmm, tn), lambda i, j, k: (j, i, 0))
        out_shape = jax.ShapeDtypeStruct((n // tn, m, tn), out_dtype)
    else:
        out_spec = pl.BlockSpec((tmm, tn), lambda i, j, k: (i, j))
        out_shape = jax.ShapeDtypeStruct((m, n), out_dtype)
    return pl.pallas_call(
        body, name=name, grid=(m // tmm, n // tn, nk),
        in_specs=[pl.BlockSpec((tk, tmm), lambda i, j, k: (k, i)), pl.BlockSpec((tk, tn), lambda i, j, k: (k, j))],
        out_specs=out_spec, out_shape=out_shape, scratch_shapes=[pltpu.VMEM((tmm, tn), f32)],
        compiler_params=_cp(("arbitrary", "arbitrary", "arbitrary")),
    )(a, b)


def _grad_w_in(h, dproj):
    t = h.shape[0]
    tk = 1024
    nk = t // tk
    sw = INW // NCHIP

    def body(a_ref, b_ref, o_ref, acc):
        k = pl.program_id(0)

        @pl.when(k == 0)
        def _():
            acc[...] = jnp.zeros_like(acc)

        acc[...] += _dot_tn(a_ref[...], b_ref[...])

        @pl.when(k == nk - 1)
        def _():
            for s in range(NCHIP):
                o_ref[s] = acc[:, sw * s:sw * (s + 1)].astype(bf16)

    return pl.pallas_call(
        body, name="grad_w_in", grid=(nk,),
        in_specs=[pl.BlockSpec((tk, D), lambda k: (k, 0)), pl.BlockSpec((tk, INW), lambda k: (k, 0))],
        out_specs=pl.BlockSpec((NCHIP, D, sw), lambda k: (0, 0, 0)), out_shape=jax.ShapeDtypeStruct((NCHIP, D, sw), bf16),
        scratch_shapes=[pltpu.VMEM((D, INW), f32)], compiler_params=_cp(("arbitrary",)),
    )(h, dproj)


def _attn_out_bwd(gx1, y, mod, g_post, w_out, oa, ob, g_mix_a, g_mix_b, l1, l4, l16):
    def body(gx_ref, y_ref, mod_ref, gp_ref, w_ref, oa_ref, ob_ref, ga_ref, gb_ref, l1_ref, l4_ref, l16_ref, e_ref, g_ref,
             dy_ref, doa_ref, do1_ref, do4_ref, do16_ref, da_ref, d1_ref, d4_ref, d16_ref, accb_ref, accg_ref, scr):
        _acc_init(accb_ref, accg_ref)
        w1, w4, w16 = _branch_weights(l1_ref, l4_ref, l16_ref, scr)
        e, hs = e_ref[...], g_ref[...]
        gx1v = gx_ref[...]
        yn, ry = _rms(y_ref[...])
        gp = gp_ref[...]
        gt = mod_ref[2:3, :]
        dn1 = gx1v * gt
        dy = _rms_bwd(dn1 * gp, yn, ry).astype(bf16)
        dy_ref[...] = dy
        dmixed = _dot_nt(dy, w_ref[...])
        dma, dmb = dmixed[:, :AQ], dmixed[:, AQ:]
        oa, ob = oa_ref[...], ob_ref[...]
        oan, ra = _rms(oa)
        obn, rb = _rms(ob)
        doa = _rms_bwd(dma * ga_ref[...], oan, ra)
        doa_ref[...] = doa.astype(bf16)
        da_ref[...] = _lanes_to_heads(doa * oa, hs)
        dob = _rms_bwd(dmb * gb_ref[...], obn, rb)
        dd = _lanes_to_heads(dob * ob, hs)
        x1w, x4w = _heads_to_lanes(w1, e), _heads_to_lanes(w4, e)
        do1_ref[...] = (x1w * dob).astype(bf16)
        d1_ref[...] = w1 * dd
        _perm_store(x4w * dob, scr, do4_ref, 4)
        _perm_store(w4 * dd, scr, d4_ref, 4)
        _perm_store((1.0 - x1w - x4w) * dob, scr, do16_ref, 16)
        _perm_store(w16 * dd, scr, d16_ref, 16)
        accb_ref[0:1, :] += _colsum(gx1v * (yn * gp))
        accg_ref[0:1, :] += _colsum(dn1 * yn)
        accg_ref[1:2, :] += jnp.concatenate([_colsum(dma * oan), _colsum(dmb * obn)], axis=1)

    nat = lambda w, dt: jax.ShapeDtypeStruct((BL, SEQ, w), dt)
    return pl.pallas_call(
        body, name="attn_out_bwd", grid=(BL, NJ),
        in_specs=[_tok(D), _tok(D), MOD_SPEC, _full((1, D)), _full((D, D)), _tok(AQ), _tok(BW), _full((1, AQ)), _full((1, BW)),
                  _tok(LANES), _perm_spec(4, LANES), _perm_spec(16, LANES), _full((LANES, BW)), _full((BW, LANES))],
        out_specs=[_tok(D), _tok(AQ), _tok(BW), _perm_spec(4, BW), _perm_spec(16, BW),
                   _tok(LANES), _tok(LANES), _perm_spec(4, LANES), _perm_spec(16, LANES), ACCB_SPEC, ACCG_SPEC],
        out_shape=[nat(D, bf16), nat(AQ, bf16), nat(BW, bf16), jax.ShapeDtypeStruct((BL, 4, SEQ // 4, BW), bf16),
                   jax.ShapeDtypeStruct((BL, 16, SEQ // 16, BW), bf16), nat(LANES, f32), nat(LANES, f32),
                   jax.ShapeDtypeStruct((BL, 4, SEQ // 4, LANES), f32), jax.ShapeDtypeStruct((BL, 16, SEQ // 16, LANES), f32)]
                  + ACC_SHAPES,
        scratch_shapes=[pltpu.VMEM((BW // LANES, TM, LANES), f32)],
        compiler_params=_cp(("arbitrary", "arbitrary")),
    )(gx1, y, mod, g_post, w_out, oa, ob, g_mix_a, g_mix_b, l1, l4, l16, jnp.asarray(HEAD_EXPAND, bf16),
      jnp.asarray(HEAD_SUM, bf16))


def _attn_in_bwd(dqa, dka, dva, d1, d4, d16, tc, ts1, ts2, w_in, x, gx1, mod, g_pre):
    def body(dqa_ref, dka_ref, dva_ref, dq1_ref, dk1_ref, dv1_ref, dq4_ref, dk4_ref, dv4_ref, dq16_ref, dk16_ref, dv16_ref,
             c_ref, s1_ref, s2_ref, w_ref, x_ref, gx_ref, mod_ref, g_ref, dproj_ref, dx_ref, accb_ref, accg_ref, scr):
        _acc_init(accb_ref, accg_ref)
        c, s1, s2 = c_ref[...], s1_ref[...], s2_ref[...]
        tot = lambda r1, r4, r16: r1[...] + _perm_load(r4, scr, 4) + _perm_load(r16, scr, 16)
        dqb = tot(dq1_ref, dq4_ref, dq16_ref)
        dkb = tot(dk1_ref, dk4_ref, dk16_ref)
        dvb = tot(dv1_ref, dv4_ref, dv16_ref)
        dproj = jnp.concatenate([
            _rope_t(dqa_ref[...], c, s1, s2) * QSCALE, _rope_t(_per_kv_head(dka_ref[...]), c, s1, s2),
            _per_kv_head(dva_ref[...]),
            _rope_t(dqb, c, s1, s2) * QSCALE, _rope_t(dkb, c, s1, s2), dvb], axis=1).astype(bf16)
        dproj_ref[...] = dproj
        dh = _dot_nt(dproj, w_ref[...])
        xn, r = _rms(x_ref[...])
        g = g_ref[...]
        dn = dh * (1.0 + mod_ref[1:2, :])
        dx_ref[...] = gx_ref[...] + _rms_bwd(dn * g, xn, r)
        accb_ref[0:1, :] += _colsum(dh * (xn * g))
        accb_ref[1:2, :] += _colsum(dh)
        accg_ref[0:1, :] += _colsum(dn * xn)

    return pl.pallas_call(
        body, name="attn_in_bwd", grid=(BL, NJ),
        in_specs=[_tok(AQ), _tok(AQ), _tok(AQ)] + [_tok(BW)] * 3 + [_perm_spec(4, BW)] * 3 + [_perm_spec(16, BW)] * 3
                 + [_tok(LANES)] * 3 + [_full((D, INW)), _tok(D), _tok(D), MOD_SPEC, _full((1, D))],
        out_specs=[_tok(INW), _tok(D), ACCB_SPEC, ACCG_SPEC],
        out_shape=[jax.ShapeDtypeStruct((BL, SEQ, INW), bf16), jax.ShapeDtypeStruct((BL, SEQ, D), f32)] + ACC_SHAPES,
        scratch_shapes=[pltpu.VMEM((BW // LANES, TM, LANES), f32)],
        compiler_params=_cp(("arbitrary", "arbitrary")),
    )(dqa, dka, dva, *d1, *d4, *d16, tc, ts1, ts2, w_in, x, gx1, mod, g_pre)


def _inv_lane():
    inv = np.float32(THETA) ** (-np.arange(0, ROT, 2, dtype=np.float32) / np.float32(ROT))
    lane = np.arange(LANES) % HD
    return jnp.asarray(np.where(lane < ROT, inv[lane % (ROT // 2)], 0.0).astype(np.float32)[None, :])


def _local_step(x, positions, mod, target, inv_lane, first_weight, later_weights, grad_ready, g_attn_pre,
                g_attn_post, sink_a, g_mix_a, g_mix_b, g_mlp_pre, g_mlp_post):
    tabs = _rope_tables(positions.reshape(BL * SEQ, 1), inv_lane)
    w_in = first_weight(tuple(tabs))
    tc, ts1, ts2 = [t.reshape(BL, SEQ, LANES) for t in tabs]

    (h, qa, ka, va, q1, k1, v1, q4, k4, v4, q16, k16, v16, w_in) = _attn_in(x, mod, g_attn_pre, w_in, tc, ts1, ts2)
    seqs = lambda t: t.reshape(t.shape[0] * t.shape[1], t.shape[2], t.shape[3])
    q4, k4, v4, q16, k16, v16 = [seqs(t) for t in (q4, k4, v4, q16, k16, v16)]
    oa, la = _attn_fwd(qa, ka, va, sink_a, max_dist=BLK - 1, o_dtype=f32, name="attn_a_fwd")
    o1, l1 = _attn_fwd(q1, k1, v1, None, max_dist=BLK, o_dtype=bf16, name="attn_b1_fwd")
    o4, l4 = _attn_fwd(q4, k4, v4, None, max_dist=BLK, o_dtype=bf16, name="attn_b4_fwd")
    o16, l16 = _attn_fwd(q16, k16, v16, None, max_dist=BLK, o_dtype=bf16, name="attn_b16_fwd")
    b4 = lambda t: t.reshape(BL, 4, SEQ // 4, t.shape[-1])
    b16 = lambda t: t.reshape(BL, 16, SEQ // 16, t.shape[-1])
    w_out, mlp_weights, mod = later_weights((oa, o1, o4, o16), mod)
    x1, y, mixed, ob = _mix_out(oa, o1, l1, b4(o4), b4(l4), b16(o16), b16(l16), g_mix_a, g_mix_b, w_out, x, mod, g_attn_post)
    w_up, w_down = mlp_weights((x1,))
    h2, u, a = _mlp_up(x1, mod, g_mlp_pre, w_up)
    gx, dy2, accb_d, accg_d = _mlp_down(a, w_down, x1, target, mod, g_mlp_post)

    flat = lambda t: t.reshape(BL * SEQ, t.shape[-1])
    mod = grad_ready("w_down", _matmul_tn(flat(a), flat(dy2), tn=D, col_blocked=False, name="grad_w_down", out_dtype=bf16), mod)
    du, gx1, accb_m, accg_m = _mlp_bwd(dy2, u, w_down, w_up, x1, gx, mod, g_mlp_pre)
    mod = grad_ready("w_up", _matmul_tn(flat(h2), flat(du), tn=D, col_blocked=True, name="grad_w_up", out_dtype=bf16), mod)

    dy, doa, do1, do4, do16, da, dl1, dl4, dl16, accb_o, accg_o = _attn_out_bwd(
        gx1, y, mod, g_attn_post, w_out, oa, ob, g_mix_a, g_mix_b, l1, b4(l4), b16(l16))
    sink_behind = grad_ready("w_out", _matmul_tn(flat(mixed), flat(dy), tn=D, col_blocked=False, name="grad_w_out",
                                                  out_dtype=bf16), sink_a)
    dqa, dka, dva, dsink = _attn_bwd(qa, ka, va, doa, da, la, sink_behind, max_dist=BLK - 1, name="attn_a_bwd")
    d1 = _attn_bwd(q1, k1, v1, do1, dl1, l1, None, max_dist=BLK, name="attn_b1_bwd")
    d4 = _attn_bwd(q4, k4, v4, seqs(do4), seqs(dl4), l4, None, max_dist=BLK, name="attn_b4_bwd")
    d16 = _attn_bwd(q16, k16, v16, seqs(do16), seqs(dl16), l16, None, max_dist=BLK, name="attn_b16_bwd")
    dproj, grad_x, accb_i, accg_i = _attn_in_bwd(dqa, dka, dva, d1, [b4(t) for t in d4], [b16(t) for t in d16],
                                                 tc, ts1, ts2, w_in, x, gx1, mod, g_attn_pre)
    gw_in = _grad_w_in(flat(h), flat(dproj))
    dsink = grad_ready("w_in", gw_in, dsink)

    return grad_x, (accb_i, accb_o, accb_m, accb_d, accg_i, accg_o, accg_m, accg_d, dsink)


ADAW = NMOD * D // NCHIP


def _pos():
    return lax.axis_index("x"), lax.axis_index("y"), lax.axis_index("c")


def _flip(v, bit):
    return 1 - v if bit else v


def _all_peers(x, y, c):
    return [(_flip(x, k >> 2 & 1), _flip(y, k >> 1 & 1), _flip(c, k & 1)) for k in range(1, NDEV)]


def _other_chips(x, y):
    return [(1 - x, y), (x, 1 - y), (1 - x, 1 - y)]


def _rcopy(src, dst, send, recv, k, dev, k_recv=None):
    return pltpu.make_async_remote_copy(src_ref=src, dst_ref=dst, send_sem=send.at[k],
                                        recv_sem=recv.at[k if k_recv is None else k_recv],
                                        device_id=dev, device_id_type=MESH)


def _gather_small(src, buf, send, recv):
    x, y, c = _pos()
    me = 4 * x + 2 * y + c
    peers = _all_peers(x, y, c)
    sends = [_rcopy(src, buf.at[me], send, recv, k, p) for k, p in enumerate(peers)]
    for cp in sends:
        cp.start()
    for k, (px, py, pc) in enumerate(peers):
        _rcopy(src, buf.at[4 * px + 2 * py + pc], send, recv, k, (px, py, pc)).wait_recv()
    for cp in sends:
        cp.wait_send()
    return me


def _ada_fwd(c_in, w_ada, b_cols):
    def body(c_ref, w_hbm, b_ref, mod_ref, cond_ref, cbuf, mbuf, w_ref, s1, r1, s2, r2, wsem):
        x, y, c = _pos()
        chip = 2 * x + y
        wcopy = pltpu.make_async_copy(w_hbm, w_ref, wsem)
        wcopy.start()
        me = _gather_small(c_ref, cbuf, s1, r1)
        cbuf[me] = c_ref[...]
        for i in range(NDEV):
            cond_ref[BL * i:BL * (i + 1), :] = cbuf[i]
        call = cond_ref[...]
        cond = call / (1.0 + jnp.exp(-call))
        cond_ref[...] = cond
        wcopy.wait()
        mbuf[chip] = _dot(cond.astype(bf16), w_ref[...].astype(bf16)) + b_ref[...]
        chips = _other_chips(x, y)
        sends = [_rcopy(mbuf.at[chip], mbuf.at[chip], s2, r2, j, (px, py, c)) for j, (px, py) in enumerate(chips)]
        for cp in sends:
            cp.start()
        for j, (px, py) in enumerate(chips):
            _rcopy(mbuf.at[chip], mbuf.at[2 * px + py], s2, r2, j, (px, py, c)).wait_recv()
        for cp in sends:
            cp.wait_send()
        row = lax.broadcasted_iota(jnp.int32, (BL * NDEV, ADAW), 0)
        for s in range(NCHIP):
            slab = mbuf[s]
            for j in range(BL):
                mod_ref[j:j + 1, ADAW * s:ADAW * (s + 1)] = jnp.sum(jnp.where(row == BL * me + j, slab, 0.0), axis=0, keepdims=True)

    vm = pl.BlockSpec(memory_space=pltpu.VMEM)
    return pl.pallas_call(
        body, name="ada_fwd", in_specs=[vm, pl.BlockSpec(memory_space=pl.ANY), vm], out_specs=[vm, vm],
        out_shape=[jax.ShapeDtypeStruct((BL, NMOD * D), f32), jax.ShapeDtypeStruct((BL * NDEV, D), f32)],
        scratch_shapes=[pltpu.VMEM((NDEV, BL, D), f32), pltpu.VMEM((NCHIP, BL * NDEV, ADAW), f32),
                        pltpu.VMEM((D, ADAW), f32),
                        pltpu.SemaphoreType.DMA((NDEV - 1,)), pltpu.SemaphoreType.DMA((NDEV - 1,)),
                        pltpu.SemaphoreType.DMA((NCHIP - 1,)), pltpu.SemaphoreType.DMA((NCHIP - 1,)),
                        pltpu.SemaphoreType.DMA],
        compiler_params=pltpu.CompilerParams(vmem_limit_bytes=VMEM_LIMIT),
    )(c_in, w_ada, b_cols)


PAY_ROWS = 4


def _small_pack(accs):
    def body(bi, bo, bm, bd, gi, go, gm, gd, dsink, pay):
        pay[...] = jnp.zeros_like(pay)
        for b in range(BL):
            for k, (ref, r) in enumerate(((bi, 1), (bi, 0), (bo, 0), (bm, 1), (bm, 0), (bd, 0))):
                pay[b:b + 1, D * k:D * (k + 1)] = ref[b, r:r + 1, :]
        for off, ref, r in ((OFF_G_ATTN_PRE, gi, 0), (OFF_G_ATTN_POST, go, 0), (OFF_G_MIX_A, go, 1), (OFF_G_MLP_PRE, gm, 0),
                            (OFF_G_MLP_POST, gd, 0)):
            pay[BL:BL + 1, off:off + D] = ref[r:r + 1, :]
        eye = lax.broadcasted_iota(jnp.int32, (NHEAD, LANES), 0) == lax.broadcasted_iota(jnp.int32, (NHEAD, LANES), 1)
        pay[BL:BL + 1, OFF_SINK:OFF_SINK + LANES] = jnp.sum(jnp.where(eye, dsink[...], 0.0), axis=0, keepdims=True)
        pay[BL:BL + 1, OFF_LOSS:OFF_LOSS + LANES] = gd[1:2, 0:LANES]

    vm = pl.BlockSpec(memory_space=pltpu.VMEM)
    return pl.pallas_call(body, name="small_pack", in_specs=[vm] * 9, out_specs=vm,
                          out_shape=jax.ShapeDtypeStruct((PAY_ROWS, PAYW), f32))(*accs)


def _small_copies(src, land, send, recv):
    x, y, c = _pos()
    me = 4 * x + 2 * y + c
    return [(_rcopy(src, land.at[me], send, recv, k, p), _rcopy(src, land.at[4 * p[0] + 2 * p[1] + p[2]], send, recv, k, p))
            for k, p in enumerate(_all_peers(x, y, c))]


def _small_sum(own, landed, cond_all):
    def body(pay, land, cond_ref, gw_ref, gb_ref, small_ref, pbuf, dall):
        x, y, c = _pos()
        chip = 2 * x + y
        me = 4 * x + 2 * y + c
        for i in range(NDEV):
            @pl.when(me == i)
            def _():
                pbuf[i] = pay[...]

            @pl.when(me != i)
            def _():
                pbuf[i] = land[i]
        small = pbuf[0, BL:BL + 1, :]
        for i in range(1, NDEV):
            small = small + pbuf[i, BL:BL + 1, :]
        small_ref[...] = small
        for i in range(NDEV):
            dall[BL * i:BL * (i + 1), :] = pbuf[i, 0:BL, :]
        gb_ref[...] = jnp.sum(dall[...], axis=0, keepdims=True)
        cols = jnp.zeros((BL * NDEV, ADAW), f32)
        for s in range(NCHIP):
            cols = cols + jnp.where(chip == s, dall[:, ADAW * s:ADAW * (s + 1)], 0.0)
        gw_ref[...] = _dot_tn(cond_ref[...].astype(bf16), cols.astype(bf16))

    vm = pl.BlockSpec(memory_space=pltpu.VMEM)
    return pl.pallas_call(
        body, name="small_sum", in_specs=[vm] * 3, out_specs=[vm] * 3,
        out_shape=[jax.ShapeDtypeStruct((D, ADAW), f32), jax.ShapeDtypeStruct((1, PAYW), f32), jax.ShapeDtypeStruct((1, PAYW), f32)],
        scratch_shapes=[pltpu.VMEM((NDEV, PAY_ROWS, PAYW), f32), pltpu.VMEM((BL * NDEV, PAYW), f32)],
        compiler_params=pltpu.CompilerParams(vmem_limit_bytes=VMEM_LIMIT),
    )(own, landed, cond_all)


def _half(ref, c):
    r2 = ref.shape[0] // 2
    return ref.at[pl.ds(c * r2 if isinstance(c, int) else pl.multiple_of(c * r2, 16), r2), :]


HBM_SPEC = pl.BlockSpec(memory_space=pltpu.HBM)
SEM_SPEC = pl.BlockSpec(memory_space=pltpu.SEMAPHORE)
EFFECT = pltpu.SideEffectType.DATAFLOW_SIDE_EFFECTING
NLINK = NCHIP - 1


def _in_hbm(a):
    return pltpu.with_memory_space_constraint(a, pltpu.HBM)


NSEM = 8


def _split_start(name, srcs, land_shapes, builds, carry, after=(), lands=None):
    n = len(srcs)
    na, nc = len(after), len(carry)

    def body(*refs):
        src, land = refs[:n], refs[n:2 * n]
        kept = refs[2 * n + na:2 * n + na + nc]
        outs = refs[2 * n + na + nc:]
        send, recv, passed = outs[:n], outs[n:2 * n], outs[4 * n:]
        for t in range(n):
            for out_cp, _ in builds[t](src[t], land[t], send[t], recv[t]):
                out_cp.start()
        for a, b in zip(kept, passed):
            b[...] = a[...]

    if lands is None:
        lands = [lax.empty(s.shape, s.dtype) for s in land_shapes]
    lands = [_in_hbm(a) for a in lands]
    sems = [pltpu.SemaphoreType.DMA((NSEM,))] * (2 * n)
    thru = [pltpu.HBM(a.shape, a.dtype) for a in list(srcs) + lands]
    vm = pl.BlockSpec(memory_space=pltpu.VMEM)
    res = pl.pallas_call(
        body, name=name, out_shape=sems + thru + [jax.ShapeDtypeStruct(a.shape, a.dtype) for a in carry],
        in_specs=[HBM_SPEC] * (2 * n) + [pl.BlockSpec(memory_space=pl.ANY)] * na + [vm] * nc,
        out_specs=[SEM_SPEC] * (2 * n) + [HBM_SPEC] * (2 * n) + [vm] * nc,
        input_output_aliases={i: 2 * n + i for i in range(2 * n)},
        compiler_params=pltpu.CompilerParams(has_side_effects=EFFECT),
    )(*[_in_hbm(a) for a in srcs], *lands, *after, *carry)
    flight = [(res[2 * n + t], res[3 * n + t], res[t], res[n + t]) for t in range(n)]
    return flight, list(res[4 * n:])


def _split_wait(name, flight, builds, after):
    m = len(flight)
    na = len(after)

    def body(*refs):
        src, land, send, recv = refs[:m], refs[m:2 * m], refs[2 * m:3 * m], refs[3 * m:4 * m]
        for t in range(m):
            for out_cp, in_cp in builds[t](src[t], land[t], send[t], recv[t]):
                out_cp.wait_send()
                in_cp.wait_recv()

    ops = [f[0] for f in flight] + [f[1] for f in flight] + [f[2] for f in flight] + [f[3] for f in flight]
    res = pl.pallas_call(
        body, name=name, out_shape=[pltpu.HBM(a.shape, a.dtype) for a in ops[:2 * m]],
        in_specs=[HBM_SPEC] * (2 * m) + [SEM_SPEC] * (2 * m) + [pl.BlockSpec(memory_space=pl.ANY)] * na,
        out_specs=[HBM_SPEC] * (2 * m), input_output_aliases={i: i for i in range(2 * m)},
        compiler_params=pltpu.CompilerParams(has_side_effects=EFFECT),
    )(*ops, *after)
    return res[:m], res[m:2 * m]


def _weight_copies(src, land, send, recv):
    x, y, c = _pos()
    chip = 2 * x + y
    return [(_rcopy(_half(src, c), _half(land.at[chip], c), send, recv, j, (px, py, c)),
             _rcopy(_half(src, c), _half(land.at[2 * px + py], c), send, recv, j, (px, py, c)))
            for j, (px, py) in enumerate(_other_chips(x, y))]


NDIRECT = NDEV - 1


def _direct_grad_copies(src, land, send, recv):
    x, y, c = _pos()
    out, arrive = [], []
    for j, (px, py) in enumerate(_other_chips(x, y)):
        for hc in range(2):
            out.append(_rcopy(_half(src.at[2 * px + py], hc), land.at[2 * j + c], send, recv, 2 * j + hc, (px, py, hc),
                              k_recv=2 * j + c))
            arrive.append(_rcopy(_half(src.at[2 * px + py], hc), land.at[2 * j + hc], send, recv, 2 * j + hc, (px, py, hc)))
    own = _rcopy(_half(src.at[2 * x + y], 1 - c), land.at[NDIRECT - 1], send, recv, NDIRECT - 1, (x, y, 1 - c))
    return list(zip(out, arrive)) + [(own, own)]


def _pair_weight_copies(src, land, send, recv):
    x, y, c = _pos()
    sib = (x, y, 1 - c)
    cps = []
    for j, (px, py) in enumerate(_other_chips(x, y)):
        mine, theirs = _half(land.at[2 * px + py], c), _half(land.at[2 * px + py], 1 - c)
        cps.append((_rcopy(mine, mine, send, recv, j, sib), _rcopy(theirs, theirs, send, recv, j, sib)))
    own = _rcopy(src, land.at[2 * x + y], send, recv, NLINK, sib)
    return cps + [(own, own)]


RS_ROWS = 256


def _chip_add(own, landed, pos_arr, name):
    nl, r2, cw = landed.shape
    rows = min(RS_ROWS, r2)
    nr = r2 // rows

    def body(s_ref, h_ref, q_ref, o_ref):
        acc = h_ref[...].astype(f32)
        for j in range(nl):
            acc = acc + q_ref[j].astype(f32)
        o_ref[...] = acc

    gs = pltpu.PrefetchScalarGridSpec(
        num_scalar_prefetch=1, grid=(nr,),
        in_specs=[pl.BlockSpec((None, rows, cw), lambda j, s: (s[0], s[1] * nr + j, 0)),
                  pl.BlockSpec((nl, rows, cw), lambda j, s: (0, j, 0))],
        out_specs=pl.BlockSpec((rows, cw), lambda j, s: (s[1] * nr + j, 0)))
    return pl.pallas_call(body, name=name, grid_spec=gs, out_shape=jax.ShapeDtypeStruct((2 * r2, cw), f32),
                          compiler_params=_cp(("arbitrary",)))(pos_arr, own, landed)


def _pair_gather_copies(src, land, send, recv):
    x, y, c = _pos()
    sib = (x, y, 1 - c)
    return [(_rcopy(_half(land, c), _half(land, c), send, recv, 0, sib),
             _rcopy(_half(land, 1 - c), _half(land, 1 - c), send, recv, 0, sib))]


def _adamw_math(w, g, m, v):
    m = B1 * m + (1.0 - B1) * g
    v = B2 * v + (1.0 - B2) * jnp.square(g)
    m_hat = m / (1.0 - B1 ** STEP)
    v_hat = v / (1.0 - B2 ** STEP)
    return -LR * (m_hat / (jnp.sqrt(v_hat) + AEPS) + WD * w), m, v


ADAM_BLOCK = 512 * 1024


def _adamw(w, g, m, v, name, after=()):
    r, cw = w.shape
    na = len(after)

    def body(w_ref, g_ref, m_ref, v_ref, *rest):
        go_ref, d_ref, mo_ref, vo_ref = rest[na:]
        g = g_ref[...]
        go_ref[...] = g
        d_ref[...], mo_ref[...], vo_ref[...] = _adamw_math(w_ref[...], g, m_ref[...], v_ref[...])

    rows = max(k for k in range(SUBLANES, ADAM_BLOCK // cw + 1, SUBLANES) if r % k == 0)
    spec = pl.BlockSpec((rows, cw), lambda i: (i, 0))
    return pl.pallas_call(body, name=name, grid=(r // rows,), in_specs=[spec] * 4 + [pl.BlockSpec(memory_space=pl.ANY)] * na,
                          out_specs=[spec] * 4, out_shape=[jax.ShapeDtypeStruct((r, cw), f32)] * 4,
                          compiler_params=_cp(("arbitrary",)))(w, g, m, v, *after)


SMALL = (("b_ada", None, PAYW), ("g_attn_pre", OFF_G_ATTN_PRE, D), ("g_attn_post", OFF_G_ATTN_POST, D), ("sink_a", OFF_SINK, 8),
         ("g_mix_a", OFF_G_MIX_A, AQ), ("g_mix_b", OFF_G_MIX_B, BW), ("g_mlp_pre", OFF_G_MLP_PRE, D), ("g_mlp_post", OFF_G_MLP_POST, D))


def _adamw_small(small, gb, params):
    n = len(SMALL)

    def body(*refs):
        small_ref, gb_ref = refs[:2]
        wmv = refs[2:2 + 3 * n]
        loss_ref = refs[2 + 3 * n]
        outs = refs[3 + 3 * n:]
        loss_ref[...] = small_ref[:, OFF_LOSS:OFF_LOSS + 1] * (0.5 / D)
        for i, (_, off, width) in enumerate(SMALL):
            g = gb_ref[...] if off is None else small_ref[:, off:off + width]
            w_ref, m_ref, v_ref = wmv[3 * i:3 * i + 3]
            outs[4 * i][...] = g
            outs[4 * i + 1][...], outs[4 * i + 2][...], outs[4 * i + 3][...] = _adamw_math(w_ref[...], g, m_ref[...], v_ref[...])

    vm = pl.BlockSpec(memory_space=pltpu.VMEM)
    out_shape = [jax.ShapeDtypeStruct((1, 1), f32)]
    for _, _, width in SMALL:
        out_shape += [jax.ShapeDtypeStruct((1, width), f32)] * 4
    flat = [a for wmv in params for a in wmv]
    res = pl.pallas_call(body, name="adamw_small", in_specs=[vm] * (2 + 3 * n), out_specs=[vm] * len(out_shape),
                         out_shape=out_shape)(small, gb, *flat)
    return res[0], {name: res[1 + 4 * i:5 + 4 * i] for i, (name, _, _) in enumerate(SMALL)}


def kernel(x, c, positions, w_ada, b_ada, g_attn_pre, g_attn_post, w_in, sink_a, g_mix_a, g_mix_b, w_out, g_mlp_pre, g_mlp_post, w_up, w_down, loss_target, m_w_ada, m_b_ada, m_g_attn_pre, m_g_attn_post, m_w_in, m_sink_a, m_g_mix_a, m_g_mix_b, m_w_out, m_g_mlp_pre, m_g_mlp_post, m_w_up, m_w_down, v_w_ada, v_b_ada, v_g_attn_pre, v_g_attn_post, v_w_in, v_sink_a, v_g_mix_a, v_g_mix_b, v_w_out, v_g_mlp_pre, v_g_mlp_post, v_w_up, v_w_down):
    given = dict(w_ada=w_ada, b_ada=b_ada, g_attn_pre=g_attn_pre, g_attn_post=g_attn_post, w_in=w_in, sink_a=sink_a, g_mix_a=g_mix_a,
                 g_mix_b=g_mix_b, w_out=w_out, g_mlp_pre=g_mlp_pre, g_mlp_post=g_mlp_post, w_up=w_up, w_down=w_down)
    moms = dict(w_ada=(m_w_ada, v_w_ada), b_ada=(m_b_ada, v_b_ada), g_attn_pre=(m_g_attn_pre, v_g_attn_pre),
                g_attn_post=(m_g_attn_post, v_g_attn_post), w_in=(m_w_in, v_w_in), sink_a=(m_sink_a, v_sink_a),
                g_mix_a=(m_g_mix_a, v_g_mix_a), g_mix_b=(m_g_mix_b, v_g_mix_b), w_out=(m_w_out, v_w_out),
                g_mlp_pre=(m_g_mlp_pre, v_g_mlp_pre), g_mlp_post=(m_g_mlp_post, v_g_mlp_post), w_up=(m_w_up, v_w_up),
                w_down=(m_w_down, v_w_down))
    order = ["w_ada", "b_ada", "g_attn_pre", "g_attn_post", "w_in", "sink_a", "g_mix_a", "g_mix_b", "w_out", "g_mlp_pre",
             "g_mlp_post", "w_up", "w_down"]
    xi, yi, ci = _pos()
    chip = 2 * xi + yi

    pos_arr = jnp.stack([chip, ci]).astype(jnp.int32)
    big = ("w_in", "w_out", "w_up", "w_down")

    b_cols = lax.dynamic_slice(b_ada, (0, chip * ADAW), (1, ADAW))
    mod, cond_all = _ada_fwd(c, w_ada[0], b_cols)
    gathered = [jax.ShapeDtypeStruct((NCHIP,) + given[n].shape[1:], bf16) for n in big]
    flight_in, (mod,) = _split_start("weights_start_first", [w_in[0].astype(bf16)], gathered[:1], [_weight_copies], [mod])
    mod, rest = lax.optimization_barrier((mod, [given[n][0] for n in big[1:]]))
    flight_rest, (mod, inv_lane) = _split_start("weights_start_rest", [w.astype(bf16) for w in rest], gathered[1:],
                                                [_weight_copies] * 3, [mod, _inv_lane()])
    mod = mod.reshape(BL, NMOD, D)

    def first_weight(after):
        srcs, lands = _split_wait("weights_wait_first", flight_in, [_weight_copies], after)
        cross, _ = _split_start("weights_pair_start_first", srcs, None, [_pair_weight_copies], [], lands=lands)
        _, (win_g,) = _split_wait("weights_pair_wait_first", cross, [_pair_weight_copies], ())
        return win_g

    def later_weights(after, carry):
        srcs, lands = _split_wait("weights_wait_rest", flight_rest, [_weight_copies] * 3, after)
        fl, (carry,) = _split_start("weights_pair_start_rest", srcs, None, [_pair_weight_copies] * 3, [carry], lands=lands)
        _, (wout_g,) = _split_wait("weights_pair_wait_out", fl[:1], [_pair_weight_copies], ())

        def mlp_weights(after):
            _, (wup_g, wdn_g) = _split_wait("weights_pair_wait_mlp", fl[1:], [_pair_weight_copies] * 2, after)
            return wup_g, wdn_g.reshape(DFF, D)

        return wout_g.reshape(D, D), mlp_weights, carry

    waiting, pending = {}, {}

    def send_grad(name, slab, carry):
        land = jax.ShapeDtypeStruct((NDIRECT, slab.shape[1] // 2, slab.shape[2]), bf16)
        pending[name], (carry,) = _split_start("grad_start_" + name, [slab], [land], [_direct_grad_copies], [carry])
        return carry

    def grad_ready(name, g, carry):
        slab = g if g.ndim == 3 else g.reshape(NCHIP, g.shape[0] // NCHIP, g.shape[1])
        if name == "w_in":
            waiting[name] = slab
            return carry
        return send_grad(name, slab, carry)

    grad_x, accs = _local_step(x, positions, mod, loss_target, inv_lane, first_weight, later_weights, grad_ready,
                               g_attn_pre, g_attn_post, sink_a, g_mix_a, g_mix_b, g_mlp_pre, g_mlp_post)

    grads, out = {}, {}

    def update(n, after=()):
        tr = (lambda a: a.T) if n == "w_in" else (lambda a: a)
        res = _adamw(tr(given[n][0]), tr(grads[n]), tr(moms[n][0][0]), tr(moms[n][1][0]), "adamw_" + n, after)
        out[n] = tuple(tr(a)[None] for a in res)
        return res[3]

    def finish(names, after, first=()):
        fl = sum((pending[n] for n in names), [])
        halves, landed = _split_wait("grad_wait_" + names[0], fl, [_direct_grad_copies] * len(names), after)
        flights, token = [], jnp.zeros((SUBLANES, LANES), f32)
        for h, q, n in zip(halves, landed, names):
            full = _chip_add(h, q, pos_arr, "grad_chip_sum_" + n)
            flights.append(_split_start("grad_gather_start_" + n, [token], None, [_pair_gather_copies], [], lands=[full])[0])
            token = flights[-1][0][0]
        last = [update(n, (token,)) for n in first]
        for n, fl1 in zip(names, flights):
            after = tuple(last) if last else () if fl1 is flights[-1] else (token,)
            _, (grads[n],) = _split_wait("grad_gather_wait_" + n, fl1, [_pair_gather_copies], after)
            last = [update(n)]
        return last[0]

    fl_small, (cond_all,) = _split_start("small_start", [_small_pack(accs)], [jax.ShapeDtypeStruct((NDEV, PAY_ROWS, PAYW), f32)],
                                         [_small_copies], [cond_all])
    cond_all = send_grad("w_in", waiting["w_in"], cond_all)
    last = finish(("w_down", "w_up", "w_out"), (cond_all,))
    (pay,), (landed,) = _split_wait("small_wait", fl_small, [_small_copies], (last,))
    grads["w_ada"], gb, small = _small_sum(pay, landed, cond_all)
    finish(("w_in",), (small,), first=("w_ada",))
    loss, res = _adamw_small(small, gb, [(given[n], moms[n][0], moms[n][1]) for n, _, _ in SMALL])
    for n, _, _ in SMALL:
        out[n] = tuple(res[n])
    return (loss.reshape(()), grad_x, *[out[n][0] for n in order], *[out[n][1] for n in order],
            *[out[n][2] for n in order], *[out[n][3] for n in order])
```

```python
import numpy as np
import jax
import jax.numpy as jnp
from jax import lax
from jax.experimental import pallas as pl
from jax.experimental.pallas import tpu as pltpu

f32 = jnp.float32
bf16 = jnp.bfloat16
MESH = pl.DeviceIdType.MESH

D = 1024
SEQ = 2048
BL = 2
HD = 64
AQ = 512
AKV = 128
BW = 512
INW = 2304
DFF = 4096
NMOD = 6
ROT = 16
THETA = 500000.0
EPS = 1e-6
NEG = -1e30
BLK = 128
TM = 512
NJ = SEQ // TM
LANES = 128
SUBLANES = 8
NHEAD = AQ // HD
QSCALE = HD ** -0.5
NCHIP = 4
NDEV = 8
VMEM_LIMIT = 56 << 20

LR, B1, B2, AEPS, WD, STEP = 0.001, 0.9, 0.999, 1e-08, 0.01, 10

OFF_G_ATTN_PRE, OFF_G_ATTN_POST, OFF_G_MIX_A, OFF_G_MIX_B = 0, 1024, 2048, 2560
OFF_G_MLP_PRE, OFF_G_MLP_POST, OFF_SINK, OFF_LOSS = 3072, 4096, 5120, 5248
PAYW = NMOD * D


def _cp(sem=None):
    return pltpu.CompilerParams(dimension_semantics=sem, vmem_limit_bytes=VMEM_LIMIT)


def _dot(a, b):
    return jnp.dot(a, b, preferred_element_type=f32)


def _dot_nt(a, b):
    return lax.dot_general(a, b, (((1,), (1,)), ((), ())), preferred_element_type=f32)


def _dot_tn(a, b):
    return lax.dot_general(a, b, (((0,), (0,)), ((), ())), preferred_element_type=f32)


def _rms(x):
    r = lax.rsqrt(jnp.mean(x * x, axis=-1, keepdims=True) + EPS)
    return x * r, r


def _rms_bwd(dy, y, r):
    return r * (dy - y * jnp.mean(dy * y, axis=-1, keepdims=True))


def _colsum(v):
    return jnp.sum(v, axis=0, keepdims=True)


def _rope(p, c, s1, s2):
    outs = []
    for c0 in range(0, p.shape[1], LANES):
        pc = p[:, c0:c0 + LANES]
        outs.append(pc * c + pltpu.roll(pc, LANES - ROT // 2, 1) * s1 + pltpu.roll(pc, ROT // 2, 1) * s2)
    return outs[0] if len(outs) == 1 else jnp.concatenate(outs, axis=1)


def _rope_t(g, c, s1, s2):
    outs = []
    for c0 in range(0, g.shape[1], LANES):
        gc = g[:, c0:c0 + LANES]
        outs.append(gc * c + pltpu.roll(gc * s1, ROT // 2, 1) + pltpu.roll(gc * s2, LANES - ROT // 2, 1))
    return outs[0] if len(outs) == 1 else jnp.concatenate(outs, axis=1)


def _perm_store(val, scr, out_ref, d):
    nc = val.shape[1] // LANES
    for c in range(nc):
        scr[c] = val[:, LANES * c:LANES * (c + 1)]
    for c in range(nc):
        for r in range(d):
            out_ref[r, :, LANES * c:LANES * (c + 1)] = scr[c, pl.ds(r, TM // d, stride=d), :].astype(out_ref.dtype)


def _perm_load(in_ref, scr, d):
    nc = in_ref.shape[-1] // LANES
    for c in range(nc):
        for r in range(d):
            scr[c, pl.ds(r, TM // d, stride=d), :] = in_ref[r, :, LANES * c:LANES * (c + 1)].astype(f32)
    return jnp.concatenate([scr[c] for c in range(nc)], axis=1)


def _per_query_head(kv):
    r = pltpu.roll(kv, HD, 1)
    lo = lax.broadcasted_iota(jnp.int32, kv.shape, 1) < HD
    return jnp.concatenate([jnp.where(lo, kv, r), jnp.where(lo, r, kv)], axis=1)


def _per_kv_head(g):
    g0, g1 = g[:, :LANES] + g[:, LANES:2 * LANES], g[:, 2 * LANES:3 * LANES] + g[:, 3 * LANES:]
    lo = lax.broadcasted_iota(jnp.int32, g0.shape, 1) < HD
    return jnp.where(lo, g0 + pltpu.roll(g0, HD, 1), g1 + pltpu.roll(g1, HD, 1))


def _tok(w):
    return pl.BlockSpec((None, TM, w), lambda b, j: (b, j, 0))


def _perm_spec(d, w):
    return pl.BlockSpec((None, d, TM // d, w), lambda b, j: (b, 0, j, 0))


def _full(shape):
    n = len(shape)
    return pl.BlockSpec(shape, lambda b, j: (0,) * n)


MOD_SPEC = pl.BlockSpec((None, NMOD, D), lambda b, j: (b, 0, 0))
ACCB_SPEC = pl.BlockSpec((None, SUBLANES, D), lambda b, j: (b, 0, 0))
ACCG_SPEC = pl.BlockSpec((SUBLANES, D), lambda b, j: (0, 0))
ACC_SHAPES = [jax.ShapeDtypeStruct((BL, SUBLANES, D), f32), jax.ShapeDtypeStruct((SUBLANES, D), f32)]


def _acc_init(accb_ref, accg_ref):
    b, j = pl.program_id(0), pl.program_id(1)

    @pl.when(j == 0)
    def _():
        accb_ref[...] = jnp.zeros_like(accb_ref)

    @pl.when((b == 0) & (j == 0))
    def _():
        accg_ref[...] = jnp.zeros_like(accg_ref)


def _rope_tables(pos_col, inv_lane):
    def body(p_ref, inv_ref, c_ref, s1_ref, s2_ref):
        ang = p_ref[...].astype(f32) * inv_ref[...]
        j = lax.broadcasted_iota(jnp.int32, (TM, LANES), 1) % HD
        cs, sn = jnp.cos(ang), jnp.sin(ang)
        c_ref[...] = jnp.where(j < ROT, cs, 1.0)
        s1_ref[...] = jnp.where(j < ROT // 2, -sn, 0.0)
        s2_ref[...] = jnp.where((j >= ROT // 2) & (j < ROT), sn, 0.0)

    n = BL * SEQ // TM
    return pl.pallas_call(
        body, name="rope_tables", grid=(n,),
        in_specs=[pl.BlockSpec((TM, 1), lambda i: (i, 0)), pl.BlockSpec((1, LANES), lambda i: (0, 0))],
        out_specs=[pl.BlockSpec((TM, LANES), lambda i: (i, 0))] * 3,
        out_shape=[jax.ShapeDtypeStruct((BL * SEQ, LANES), f32)] * 3,
    )(pos_col, inv_lane)


def _attn_in(x, mod, g_pre, w_in, tc, ts1, ts2):
    def body(x_ref, mod_ref, g_ref, wg_ref, c_ref, s1_ref, s2_ref,
             h_ref, qa_ref, ka_ref, va_ref, q1_ref, k1_ref, v1_ref, q4_ref, k4_ref, v4_ref, q16_ref, k16_ref, v16_ref,
             w_ref, scr):
        @pl.when((pl.program_id(0) == 0) & (pl.program_id(1) == 0))
        def _():
            w_ref[...] = jnp.concatenate([wg_ref[s] for s in range(NCHIP)], axis=1)

        xn, _ = _rms(x_ref[...])
        h = (xn * g_ref[...]) * (1.0 + mod_ref[1:2, :]) + mod_ref[0:1, :]
        hb = h.astype(bf16)
        h_ref[...] = hb
        proj = _dot(hb, w_ref[...])
        c, s1, s2 = c_ref[...], s1_ref[...], s2_ref[...]
        o1, o2, o3, o4, o5 = AQ, AQ + AKV, AQ + 2 * AKV, AQ + 2 * AKV + BW, AQ + 2 * AKV + 2 * BW
        qa_ref[...] = (_rope(proj[:, :o1], c, s1, s2) * QSCALE).astype(bf16)
        ka_ref[...] = _per_query_head(_rope(proj[:, o1:o2], c, s1, s2)).astype(bf16)
        va_ref[...] = _per_query_head(proj[:, o2:o3]).astype(bf16)
        qb = _rope(proj[:, o3:o4], c, s1, s2) * QSCALE
        kb = _rope(proj[:, o4:o5], c, s1, s2)
        vb = proj[:, o5:]
        for val, r1, r4, r16 in ((qb, q1_ref, q4_ref, q16_ref), (kb, k1_ref, k4_ref, k16_ref), (vb, v1_ref, v4_ref, v16_ref)):
            r1[...] = val.astype(bf16)
            _perm_store(val, scr, r4, 4)
            _perm_store(val, scr, r16, 16)

    nat = lambda w: jax.ShapeDtypeStruct((BL, SEQ, w), bf16)
    p4 = jax.ShapeDtypeStruct((BL, 4, SEQ // 4, BW), bf16)
    p16 = jax.ShapeDtypeStruct((BL, 16, SEQ // 16, BW), bf16)
    return pl.pallas_call(
        body, name="attn_in", grid=(BL, NJ),
        in_specs=[_tok(D), MOD_SPEC, _full((1, D)), _full((NCHIP, D, INW // NCHIP)), _tok(LANES), _tok(LANES), _tok(LANES)],
        out_specs=([_tok(D), _tok(AQ), _tok(2 * AKV), _tok(2 * AKV)] + [_tok(BW)] * 3 + [_perm_spec(4, BW)] * 3 + [_perm_spec(16, BW)] * 3
                   + [_full((D, INW))]),
        out_shape=[nat(D), nat(AQ), nat(2 * AKV), nat(2 * AKV)] + [nat(BW)] * 3 + [p4] * 3 + [p16] * 3
                  + [jax.ShapeDtypeStruct((D, INW), bf16)],
        scratch_shapes=[pltpu.VMEM((BW // LANES, TM, LANES), f32)],
        compiler_params=_cp(("arbitrary", "arbitrary")),
    )(x, mod, g_pre, w_in, tc, ts1, ts2)


def _kv_cat(cur_ref, prev_ref, p, cache):
    key = (id(cur_ref), p)
    if key not in cache:
        sl = slice(LANES * p, LANES * (p + 1))
        cache[key] = cur_ref[:, sl] if prev_ref is None else jnp.concatenate([prev_ref[:, sl], cur_ref[:, sl]], axis=0)
    return cache[key]


def _lane_half(a, hh):
    lo = lax.broadcasted_iota(jnp.int32, a.shape, 1) < HD
    return jnp.where(lo, a, jnp.zeros_like(a)) if hh == 0 else jnp.where(lo, jnp.zeros_like(a), a)


ATT_UNITS = 4


def _att_units(nb):
    return ATT_UNITS if nb == 1 else min(ATT_UNITS, nb)


def _attn_specs(n, nb, descending):
    u = _att_units(nb)
    if nb == 1:
        return (lambda ww: pl.BlockSpec((u, BLK, ww), lambda a, i: (a, 0, 0))), None, (n // u, 1)
    steps = nb // u
    at = (lambda i: steps - 1 - i) if descending else (lambda i: i)
    cur = lambda ww: pl.BlockSpec((None, u * BLK, ww), lambda a, i: (a, at(i), 0))
    prev = lambda ww: pl.BlockSpec((None, BLK, ww), lambda a, i: (a, jnp.maximum(u * at(i) - 1, 0), 0))
    return cur, prev, (n, steps)


def _attn_fwd(q, k, v, sink, *, max_dist, o_dtype, name):
    n, l, w = q.shape
    wk = k.shape[-1]
    nb = l // BLK
    has_sink = sink is not None

    def body(*refs):
        sink_ref = None
        if has_sink:
            sink_ref, refs = refs[0], refs[1:]
        if nb > 1:
            q_ref, kc_ref, kp_ref, vc_ref, vp_ref, o_ref, lse_ref = refs[:7]
            first = pl.program_id(1) == 0
            for u in range(_att_units(nb)):
                rows, before = pl.ds(BLK * u, BLK), pl.ds(BLK * (u - 1), BLK)
                unit(q_ref.at[rows, :], kc_ref.at[rows, :], kp_ref if u == 0 else kc_ref.at[before, :],
                     vc_ref.at[rows, :], vp_ref if u == 0 else vc_ref.at[before, :], o_ref.at[rows, :], lse_ref.at[rows, :],
                     jnp.logical_not(first) if u == 0 else True, sink_ref, *refs[7:])
        else:
            q_ref, kc_ref, vc_ref, o_ref, lse_ref = refs[:5]
            for u in range(_att_units(nb)):
                unit(q_ref.at[u], kc_ref.at[u], None, vc_ref.at[u], None, o_ref.at[u], lse_ref.at[u], None, sink_ref, *refs[5:])

    def unit(q_ref, kc_ref, kp_ref, vc_ref, vp_ref, o_ref, lse_ref, has_prev, sink_ref, sscr, pscr, dscr):
        qi = lax.broadcasted_iota(jnp.int32, (BLK, BLK), 0)
        kj = lax.broadcasted_iota(jnp.int32, (BLK, BLK), 1)
        tri = kj <= qi
        eye = kj == qi
        cache = {}
        for p in range(w // LANES):
            qpair = q_ref[:, LANES * p:LANES * (p + 1)]
            kcat = _kv_cat(kc_ref, kp_ref, p // share, cache)
            for hh in range(2):
                s = _dot_nt(_lane_half(qpair, hh), kcat)
                if nb > 1:
                    sp = s[:, :BLK] if has_prev is True else jnp.where(has_prev, s[:, :BLK], NEG)
                    sscr[2 * p + hh] = jnp.where(tri, s[:, BLK:], sp)
                    if diag:
                        dscr[2 * p + hh] = jnp.where(eye, sp, NEG)
                else:
                    sscr[2 * p + hh] = jnp.where(tri, s, NEG)
        lane = lax.broadcasted_iota(jnp.int32, (BLK, LANES), 1)
        lse_all = jnp.zeros((BLK, LANES), f32)
        for p in range(w // LANES):
            for hh in range(2):
                h = 2 * p + hh
                comb = sscr[h]
                if diag:
                    dtile = dscr[h]
                    m = jnp.max(jnp.maximum(comb, dtile), axis=-1, keepdims=True)
                else:
                    m = jnp.max(comb, axis=-1, keepdims=True)
                if has_sink:
                    sk = sink_ref[0, h]
                    m = jnp.maximum(m, sk)
                e = jnp.exp(comb - m)
                if diag:
                    ed = jnp.exp(dtile - m)
                    den = jnp.sum(e + ed, axis=-1, keepdims=True)
                else:
                    den = jnp.sum(e, axis=-1, keepdims=True)
                if has_sink:
                    den = den + jnp.exp(sk - m)
                inv = 1.0 / den
                if nb > 1:
                    pscr[h, :, :BLK] = (jnp.where(tri, ed if diag else 0.0, e) * inv).astype(bf16)
                    pscr[h, :, BLK:] = (jnp.where(tri, e, 0.0) * inv).astype(bf16)
                else:
                    pscr[h] = (e * inv).astype(bf16)
                lse_all = jnp.where(lane == h, jnp.broadcast_to(m + jnp.log(den), (BLK, LANES)), lse_all)
        lse_ref[...] = lse_all
        for p in range(w // LANES):
            vcat = _kv_cat(vc_ref, vp_ref, p // share, cache)
            o_ref[:, LANES * p:LANES * (p + 1)] = (_dot(pscr[2 * p], _lane_half(vcat, 0))
                                                   + _dot(pscr[2 * p + 1], _lane_half(vcat, 1))).astype(o_ref.dtype)

    assert max_dist in (BLK - 1, BLK) and w % wk == 0
    share = w // wk
    diag = nb > 1 and max_dist == BLK
    cur, prev, grid = _attn_specs(n, nb, False)
    in_specs = [cur(w), cur(wk)] + ([prev(wk)] if nb > 1 else []) + [cur(wk)] + ([prev(wk)] if nb > 1 else [])
    args = [q, k] + ([k] if nb > 1 else []) + [v] + ([v] if nb > 1 else [])
    if has_sink:
        in_specs = [pl.BlockSpec(memory_space=pltpu.SMEM)] + in_specs
        args = [sink] + args
    return pl.pallas_call(
        body, name=name, grid=grid, in_specs=in_specs,
        out_specs=[cur(w), cur(LANES)],
        out_shape=[jax.ShapeDtypeStruct((n, l, w), o_dtype), jax.ShapeDtypeStruct((n, l, LANES), f32)],
        scratch_shapes=[pltpu.VMEM((w // HD, BLK, BLK), f32), pltpu.VMEM((w // HD, BLK, 2 * BLK if nb > 1 else BLK), bf16),
                        pltpu.VMEM((w // HD if diag else 1, BLK, BLK), f32)],
        compiler_params=_cp(("arbitrary", "arbitrary")),
    )(*args)


def _attn_bwd(q, k, v, do, delta, lse, sink, *, max_dist, name):
    n, l, w = q.shape
    wk = k.shape[-1]
    nb = l // BLK
    has_sink = sink is not None

    def body(*refs):
        sink_ref = dsink_ref = ck = cv = None
        if has_sink:
            sink_ref, refs = refs[0], refs[1:]
        nin = 8 if nb > 1 else 6
        ins, rest = refs[:nin], refs[nin:]
        if has_sink:
            dq_ref, dk_ref, dv_ref, dsink_ref = rest[:4]
            rest = rest[4:]
        else:
            dq_ref, dk_ref, dv_ref = rest[:3]
            rest = rest[3:]
        step = pl.program_id(1)
        if has_sink:
            @pl.when((pl.program_id(0) == 0) & (step == 0))
            def _():
                dsink_ref[...] = jnp.zeros_like(dsink_ref)

        if nb > 1:
            q_ref, kc_ref, kp_ref, vc_ref, vp_ref, do_ref, delta_ref, lse_ref = ins
            ck, cv = rest[:2]

            @pl.when(step == 0)
            def _():
                ck[...] = jnp.zeros_like(ck)
                cv[...] = jnp.zeros_like(cv)

            last = step == nb // _att_units(nb) - 1
            for u in reversed(range(_att_units(nb))):
                rows, before = pl.ds(BLK * u, BLK), pl.ds(BLK * (u - 1), BLK)
                unit(q_ref.at[rows, :], kc_ref.at[rows, :], kp_ref if u == 0 else kc_ref.at[before, :],
                     vc_ref.at[rows, :], vp_ref if u == 0 else vc_ref.at[before, :], do_ref.at[rows, :],
                     delta_ref.at[rows, :], lse_ref.at[rows, :], dq_ref.at[rows, :], dk_ref.at[rows, :], dv_ref.at[rows, :],
                     jnp.logical_not(last) if u == 0 else True, sink_ref, dsink_ref, ck, cv, *rest[2:])
        else:
            q_ref, kc_ref, vc_ref, do_ref, delta_ref, lse_ref = ins
            for u in range(_att_units(nb)):
                unit(q_ref.at[u], kc_ref.at[u], None, vc_ref.at[u], None, do_ref.at[u], delta_ref.at[u], lse_ref.at[u],
                     dq_ref.at[u], dk_ref.at[u], dv_ref.at[u], None, sink_ref, dsink_ref, None, None, *rest)

    def unit(q_ref, kc_ref, kp_ref, vc_ref, vp_ref, do_ref, delta_ref, lse_ref, dq_ref, dk_ref, dv_ref, has_prev,
             sink_ref, dsink_ref, ck, cv, sscr, dpscr, pscr, dsscr, dscr=None, ddscr=None):
        lane = lax.broadcasted_iota(jnp.int32, (BLK, LANES), 1)
        qi = lax.broadcasted_iota(jnp.int32, (BLK, BLK), 0)
        kj = lax.broadcasted_iota(jnp.int32, (BLK, BLK), 1)
        tri = kj <= qi
        eye = kj == qi
        cache = {}
        kp, vp = kp_ref, vp_ref
        for p in range(w // LANES):
            sl = slice(LANES * p, LANES * (p + 1))
            qpair, dopair = q_ref[:, sl], do_ref[:, sl]
            kcat, vcat = _kv_cat(kc_ref, kp, p // share, cache), _kv_cat(vc_ref, vp, p // share, cache)
            for hh in range(2):
                h = 2 * p + hh
                s = _dot_nt(_lane_half(qpair, hh), kcat)
                dp = _dot_nt(_lane_half(dopair, hh), vcat)
                if nb > 1:
                    sp = s[:, :BLK] if has_prev is True else jnp.where(has_prev, s[:, :BLK], NEG)
                    sscr[h] = jnp.where(tri, s[:, BLK:], sp)
                    dpscr[h] = jnp.where(tri, dp[:, BLK:], dp[:, :BLK])
                    if diag:
                        dscr[h] = jnp.where(eye, sp, NEG)
                        ddscr[h] = dp[:, :BLK]
                else:
                    sscr[h] = jnp.where(tri, s, NEG)
                    dpscr[h] = dp
        for p in range(w // LANES):
            for hh in range(2):
                h = 2 * p + hh
                lse_b = jnp.broadcast_to(lse_ref[:, h:h + 1], (BLK, BLK))
                delta = jnp.broadcast_to(delta_ref[:, h:h + 1], (BLK, BLK))
                pr = jnp.exp(sscr[h] - lse_b)
                ds = pr * (dpscr[h] - delta)
                if nb > 1:
                    if diag:
                        prd = jnp.exp(dscr[h] - lse_b)
                        dsd = prd * (ddscr[h] - delta)
                    else:
                        prd = dsd = 0.0
                    pscr[h, :, :BLK] = jnp.where(tri, prd, pr).astype(bf16)
                    pscr[h, :, BLK:] = jnp.where(tri, pr, 0.0).astype(bf16)
                    dsscr[h, :, :BLK] = jnp.where(tri, dsd, ds).astype(bf16)
                    dsscr[h, :, BLK:] = jnp.where(tri, ds, 0.0).astype(bf16)
                else:
                    pscr[h] = pr.astype(bf16)
                    dsscr[h] = ds.astype(bf16)
                if has_sink:
                    dsk = -jnp.sum(jnp.where(lane == 0, jnp.exp(sink_ref[0, h] - lse_b) * delta, 0.0), keepdims=True)
                    dsink_ref[h:h + 1, :] += jnp.broadcast_to(dsk, (1, LANES))
        for p in range(w // LANES):
            sl = slice(LANES * p, LANES * (p + 1))
            qpair, dopair = q_ref[:, sl], do_ref[:, sl]
            kcat = _kv_cat(kc_ref, kp, p // share, cache)
            dq_ref[:, sl] = _dot(dsscr[2 * p], _lane_half(kcat, 0)) + _dot(dsscr[2 * p + 1], _lane_half(kcat, 1))
            dk_pair = _dot_tn(dsscr[2 * p], _lane_half(qpair, 0)) + _dot_tn(dsscr[2 * p + 1], _lane_half(qpair, 1))
            dv_pair = _dot_tn(pscr[2 * p], _lane_half(dopair, 0)) + _dot_tn(pscr[2 * p + 1], _lane_half(dopair, 1))
            if nb > 1:
                dk_ref[:, sl] = dk_pair[BLK:] + ck[:, sl]
                dv_ref[:, sl] = dv_pair[BLK:] + cv[:, sl]
                ck[:, sl] = dk_pair[:BLK]
                cv[:, sl] = dv_pair[:BLK]
            else:
                dk_ref[:, sl] = dk_pair
                dv_ref[:, sl] = dv_pair

    assert max_dist in (BLK - 1, BLK) and w % wk == 0
    share = w // wk
    diag = nb > 1 and max_dist == BLK
    cur, prev, grid = _attn_specs(n, nb, True)
    in_specs = ([cur(w), cur(wk)] + ([prev(wk)] if nb > 1 else []) + [cur(wk)] + ([prev(wk)] if nb > 1 else [])
                + [cur(w), cur(LANES), cur(LANES)])
    args = [q, k] + ([k] if nb > 1 else []) + [v] + ([v] if nb > 1 else []) + [do, delta, lse]
    out_specs = [cur(w)] * 3
    out_shape = [jax.ShapeDtypeStruct((n, l, w), f32)] * 3
    if has_sink:
        in_specs = [pl.BlockSpec(memory_space=pltpu.SMEM)] + in_specs
        args = [sink] + args
        out_specs.append(pl.BlockSpec((NHEAD, LANES), lambda a, i: (0, 0)))
        out_shape.append(jax.ShapeDtypeStruct((NHEAD, LANES), f32))
    nh = w // HD
    scratch = [pltpu.VMEM((BLK, w), f32), pltpu.VMEM((BLK, w), f32)] if nb > 1 else []
    scratch += [pltpu.VMEM((nh, BLK, BLK), f32)] * 2 + [pltpu.VMEM((nh, BLK, 2 * BLK if nb > 1 else BLK), bf16)] * 2
    if diag:
        scratch += [pltpu.VMEM((nh, BLK, BLK), f32)] * 2
    return pl.pallas_call(
        body, name=name, grid=grid, in_specs=in_specs, out_specs=out_specs, out_shape=out_shape,
        scratch_shapes=scratch, compiler_params=_cp(("arbitrary", "arbitrary")),
    )(*args)


def _split2(x):
    hi = x.astype(bf16)
    return hi, (x - hi.astype(f32)).astype(bf16)


def _heads_to_lanes(xc, e):
    return sum(_dot(t, e) for t in _split2(xc))


def _lanes_to_heads(x, g):
    return sum(_dot(t, g) for t in _split2(x))


HEAD_EXPAND = (np.arange(LANES)[:, None] == np.arange(BW)[None, :] // HD).astype(np.float32)
HEAD_SUM = HEAD_EXPAND.T.copy()


def _branch_weights(l1_ref, l4_ref, l16_ref, scr):
    l4v = _perm_load(l4_ref, scr, 4)
    l16v = _perm_load(l16_ref, scr, 16)
    l1v = l1_ref[...]
    m = jnp.maximum(jnp.maximum(l1v, l4v), l16v)
    e1, e4, e16 = jnp.exp(l1v - m), jnp.exp(l4v - m), jnp.exp(l16v - m)
    z = e1 + e4 + e16
    return e1 / z, e4 / z, e16 / z


def _mix_out(oa, o1, l1, o4, l4, o16, l16, g_mix_a, g_mix_b, w_out, x, mod, g_post):
    def body(oa_ref, o1_ref, l1_ref, o4_ref, l4_ref, o16_ref, l16_ref, ga_ref, gb_ref, w_ref, x_ref, mod_ref, gp_ref, e_ref,
             x1_ref, y_ref, mixed_ref, ob_ref, scr):
        w1, w4, w16 = _branch_weights(l1_ref, l4_ref, l16_ref, scr)
        e = e_ref[...]
        x1w, x4w = _heads_to_lanes(w1, e), _heads_to_lanes(w4, e)
        ob = (x1w * o1_ref[...].astype(f32) + x4w * _perm_load(o4_ref, scr, 4)
              + (1.0 - x1w - x4w) * _perm_load(o16_ref, scr, 16))
        ob_ref[...] = ob
        oan, _ = _rms(oa_ref[...])
        obn, _ = _rms(ob)
        mixed = jnp.concatenate([oan * ga_ref[...], obn * gb_ref[...]], axis=1).astype(bf16)
        mixed_ref[...] = mixed
        y = _dot(mixed, w_ref[...])
        y_ref[...] = y
        yn, _ = _rms(y)
        x1_ref[...] = x_ref[...] + mod_ref[2:3, :] * (yn * gp_ref[...])

    nat = lambda w, dt: jax.ShapeDtypeStruct((BL, SEQ, w), dt)
    return pl.pallas_call(
        body, name="mix_out", grid=(BL, NJ),
        in_specs=[_tok(AQ), _tok(BW), _tok(LANES), _perm_spec(4, BW), _perm_spec(4, LANES), _perm_spec(16, BW),
                  _perm_spec(16, LANES), _full((1, AQ)), _full((1, BW)), _full((D, D)), _tok(D), MOD_SPEC, _full((1, D)),
                  _full((LANES, BW))],
        out_specs=[_tok(D), _tok(D), _tok(D), _tok(BW)],
        out_shape=[nat(D, f32), nat(D, f32), nat(D, bf16), nat(BW, f32)],
        scratch_shapes=[pltpu.VMEM((BW // LANES, TM, LANES), f32)],
        compiler_params=_cp(("arbitrary", "arbitrary")),
    )(oa, o1, l1, o4, l4, o16, l16, g_mix_a, g_mix_b, w_out, x, mod, g_post, jnp.asarray(HEAD_EXPAND, bf16))


def _mlp_up(x1, mod, g_pre, w_up):
    def body(x_ref, mod_ref, g_ref, w_ref, h_ref, u_ref, a_ref):
        xn, _ = _rms(x_ref[...])
        h = (xn * g_ref[...]) * (1.0 + mod_ref[4:5, :]) + mod_ref[3:4, :]
        hb = h.astype(bf16)
        h_ref[...] = hb
        for s in range(NCHIP):
            u = _dot(hb, w_ref[s])
            u_ref[:, D * s:D * (s + 1)] = u.astype(bf16)
            a_ref[:, D * s:D * (s + 1)] = jnp.square(jnp.maximum(u, 0.0)).astype(bf16)

    nat = lambda w: jax.ShapeDtypeStruct((BL, SEQ, w), bf16)
    return pl.pallas_call(
        body, name="mlp_up", grid=(BL, NJ),
        in_specs=[_tok(D), MOD_SPEC, _full((1, D)), _full((NCHIP, D, D))],
        out_specs=[_tok(D), _tok(DFF), _tok(DFF)], out_shape=[nat(D), nat(DFF), nat(DFF)],
        compiler_params=_cp(("arbitrary", "arbitrary")),
    )(x1, mod, g_pre, w_up)


def _mlp_down(a, w_down, x1, target, mod, g_post):
    def body(a_ref, w_ref, x_ref, t_ref, mod_ref, g_ref, gx_ref, dy_ref, accb_ref, accg_ref):
        _acc_init(accb_ref, accg_ref)
        y2 = _dot(a_ref[...], w_ref[...])
        yn, r = _rms(y2)
        g = g_ref[...]
        gt = mod_ref[5:6, :]
        n2 = yn * g
        err = x_ref[...] + gt * n2 - t_ref[...]
        gout = err * (1.0 / D)
        gx_ref[...] = gout
        dn2 = gout * gt
        dy_ref[...] = _rms_bwd(dn2 * g, yn, r).astype(bf16)
        accb_ref[0:1, :] += _colsum(gout * n2)
        accg_ref[0:1, :] += _colsum(dn2 * yn)
        accg_ref[1:2, :] += jnp.broadcast_to(jnp.sum(err * err, keepdims=True), (1, D))

    return pl.pallas_call(
        body, name="mlp_down", grid=(BL, NJ),
        in_specs=[_tok(DFF), _full((DFF, D)), _tok(D), _tok(D), MOD_SPEC, _full((1, D))],
        out_specs=[_tok(D), _tok(D), ACCB_SPEC, ACCG_SPEC],
        out_shape=[jax.ShapeDtypeStruct((BL, SEQ, D), f32), jax.ShapeDtypeStruct((BL, SEQ, D), bf16)] + ACC_SHAPES,
        compiler_params=_cp(("arbitrary", "arbitrary")),
    )(a, w_down, x1, target, mod, g_post)


def _mlp_bwd(dy2, u, w_down, w_up, x1, gx, mod, g_pre):
    def body(dy_ref, u_ref, wd_hbm, wu_hbm, x_ref, gx_ref, mod_ref, g_ref, du_ref, gx1_ref, accb_ref, accg_ref, wd, wu, sem):
        _acc_init(accb_ref, accg_ref)
        first = (pl.program_id(0) == 0) & (pl.program_id(1) == 0)
        c1 = pltpu.make_async_copy(wd_hbm, wd, sem.at[0])
        c2 = pltpu.make_async_copy(wu_hbm, wu, sem.at[1])

        @pl.when(first)
        def _():
            c1.start()
            c2.start()
            c1.wait()

        dy = dy_ref[...]
        for s in range(NCHIP):
            sl = slice(D * s, D * (s + 1))
            da = _dot_nt(dy, wd[sl, :])
            du_ref[:, sl] = (da * (2.0 * jnp.maximum(u_ref[:, sl].astype(f32), 0.0))).astype(bf16)

        @pl.when(first)
        def _():
            c2.wait()

        dh = jnp.zeros((TM, D), f32)
        for s in range(NCHIP):
            dh = dh + _dot_nt(du_ref[:, D * s:D * (s + 1)], wu[s])
        xn, r = _rms(x_ref[...])
        g = g_ref[...]
        n = xn * g
        dn = dh * (1.0 + mod_ref[4:5, :])
        gx1_ref[...] = gx_ref[...] + _rms_bwd(dn * g, xn, r)
        accb_ref[0:1, :] += _colsum(dh * n)
        accb_ref[1:2, :] += _colsum(dh)
        accg_ref[0:1, :] += _colsum(dn * xn)

    anyspec = pl.BlockSpec(memory_space=pl.ANY)
    return pl.pallas_call(
        body, name="mlp_bwd", grid=(BL, NJ),
        in_specs=[_tok(D), _tok(DFF), anyspec, anyspec, _tok(D), _tok(D), MOD_SPEC, _full((1, D))],
        out_specs=[_tok(DFF), _tok(D), ACCB_SPEC, ACCG_SPEC],
        out_shape=[jax.ShapeDtypeStruct((BL, SEQ, DFF), bf16), jax.ShapeDtypeStruct((BL, SEQ, D), f32)] + ACC_SHAPES,
        scratch_shapes=[pltpu.VMEM((DFF, D), bf16), pltpu.VMEM((NCHIP, D, D), bf16), pltpu.SemaphoreType.DMA((2,))],
        compiler_params=_cp(("arbitrary", "arbitrary")),
    )(dy2, u, w_down, w_up, x1, gx, mod, g_pre)


def _matmul_tn(a, b, *, tn, col_blocked, name, out_dtype=f32):
    t, m = a.shape
    n = b.shape[1]
    tmm = min(m, 1024)
    tk = 2048 if tn <= 1024 else 1024
    nk = t // tk

    def body(a_ref, b_ref, o_ref, acc):
        k = pl.program_id(2)

        @pl.when(k == 0)
        def _():
            acc[...] = jnp.zeros_like(acc)

        acc[...] += _dot_tn(a_ref[...], b_ref[...])

        @pl.when(k == nk - 1)
        def _():
            o_ref[...] = acc[...].astype(out_dtype)

    if col_blocked:
        out_spec = pl.BlockSpec((None, tmm, tn), lambda i, j, k: (j, i, 0))
        out_shape = jax.ShapeDtypeStruct((n // tn, m, tn), out_dtype)
    else:
        out_spec = pl.BlockSpec((tmm, tn), lambda i, j, k: (i, j))
        out_shape = jax.ShapeDtypeStruct((m, n), out_dtype)
    return pl.pallas_call(
        body, name=name, grid=(m // tmm, n // tn, nk),
        in_specs=[pl.BlockSpec((tk, tmm), lambda i, j, k: (k, i)), pl.BlockSpec((tk, tn), lambda i, j, k: (k, j))],
        out_specs=out_spec, out_shape=out_shape, scratch_shapes=[pltpu.VMEM((tmm, tn), f32)],
        compiler_params=_cp(("arbitrary", "arbitrary", "arbitrary")),
    )(a, b)


def _grad_w_in(h, dproj):
    t = h.shape[0]
    tk = 1024
    nk = t // tk
    sw = INW // NCHIP

    def body(a_ref, b_ref, o_ref, acc):
        k = pl.program_id(0)

        @pl.when(k == 0)
        def _():
            acc[...] = jnp.zeros_like(acc)

        acc[...] += _dot_tn(a_ref[...], b_ref[...])

        @pl.when(k == nk - 1)
        def _():
            for s in range(NCHIP):
                o_ref[s] = acc[:, sw * s:sw * (s + 1)].astype(bf16)

    return pl.pallas_call(
        body, name="grad_w_in", grid=(nk,),
        in_specs=[pl.BlockSpec((tk, D), lambda k: (k, 0)), pl.BlockSpec((tk, INW), lambda k: (k, 0))],
        out_specs=pl.BlockSpec((NCHIP, D, sw), lambda k: (0, 0, 0)), out_shape=jax.ShapeDtypeStruct((NCHIP, D, sw), bf16),
        scratch_shapes=[pltpu.VMEM((D, INW), f32)], compiler_params=_cp(("arbitrary",)),
    )(h, dproj)


def _attn_out_bwd(gx1, y, mod, g_post, w_out, oa, ob, g_mix_a, g_mix_b, l1, l4, l16):
    def body(gx_ref, y_ref, mod_ref, gp_ref, w_ref, oa_ref, ob_ref, ga_ref, gb_ref, l1_ref, l4_ref, l16_ref, e_ref, g_ref,
             dy_ref, doa_ref, do1_ref, do4_ref, do16_ref, da_ref, d1_ref, d4_ref, d16_ref, accb_ref, accg_ref, scr):
        _acc_init(accb_ref, accg_ref)
        w1, w4, w16 = _branch_weights(l1_ref, l4_ref, l16_ref, scr)
        e, hs = e_ref[...], g_ref[...]
        gx1v = gx_ref[...]
        yn, ry = _rms(y_ref[...])
        gp = gp_ref[...]
        gt = mod_ref[2:3, :]
        dn1 = gx1v * gt
        dy = _rms_bwd(dn1 * gp, yn, ry).astype(bf16)
        dy_ref[...] = dy
        dmixed = _dot_nt(dy, w_ref[...])
        dma, dmb = dmixed[:, :AQ], dmixed[:, AQ:]
        oa, ob = oa_ref[...], ob_ref[...]
        oan, ra = _rms(oa)
        obn, rb = _rms(ob)
        doa = _rms_bwd(dma * ga_ref[...], oan, ra)
        doa_ref[...] = doa.astype(bf16)
        da_ref[...] = _lanes_to_heads(doa * oa, hs)
        dob = _rms_bwd(dmb * gb_ref[...], obn, rb)
        dd = _lanes_to_heads(dob * ob, hs)
        x1w, x4w = _heads_to_lanes(w1, e), _heads_to_lanes(w4, e)
        do1_ref[...] = (x1w * dob).astype(bf16)
        d1_ref[...] = w1 * dd
        _perm_store(x4w * dob, scr, do4_ref, 4)
        _perm_store(w4 * dd, scr, d4_ref, 4)
        _perm_store((1.0 - x1w - x4w) * dob, scr, do16_ref, 16)
        _perm_store(w16 * dd, scr, d16_ref, 16)
        accb_ref[0:1, :] += _colsum(gx1v * (yn * gp))
        accg_ref[0:1, :] += _colsum(dn1 * yn)
        accg_ref[1:2, :] += jnp.concatenate([_colsum(dma * oan), _colsum(dmb * obn)], axis=1)

    nat = lambda w, dt: jax.ShapeDtypeStruct((BL, SEQ, w), dt)
    return pl.pallas_call(
        body, name="attn_out_bwd", grid=(BL, NJ),
        in_specs=[_tok(D), _tok(D), MOD_SPEC, _full((1, D)), _full((D, D)), _tok(AQ), _tok(BW), _full((1, AQ)), _full((1, BW)),
                  _tok(LANES), _perm_spec(4, LANES), _perm_spec(16, LANES), _full((LANES, BW)), _full((BW, LANES))],
        out_specs=[_tok(D), _tok(AQ), _tok(BW), _perm_spec(4, BW), _perm_spec(16, BW),
                   _tok(LANES), _tok(LANES), _perm_spec(4, LANES), _perm_spec(16, LANES), ACCB_SPEC, ACCG_SPEC],
        out_shape=[nat(D, bf16), nat(AQ, bf16), nat(BW, bf16), jax.ShapeDtypeStruct((BL, 4, SEQ // 4, BW), bf16),
                   jax.ShapeDtypeStruct((BL, 16, SEQ // 16, BW), bf16), nat(LANES, f32), nat(LANES, f32),
                   jax.ShapeDtypeStruct((BL, 4, SEQ // 4, LANES), f32), jax.ShapeDtypeStruct((BL, 16, SEQ // 16, LANES), f32)]
                  + ACC_SHAPES,
        scratch_shapes=[pltpu.VMEM((BW // LANES, TM, LANES), f32)],
        compiler_params=_cp(("arbitrary", "arbitrary")),
    )(gx1, y, mod, g_post, w_out, oa, ob, g_mix_a, g_mix_b, l1, l4, l16, jnp.asarray(HEAD_EXPAND, bf16),
      jnp.asarray(HEAD_SUM, bf16))


def _attn_in_bwd(dqa, dka, dva, d1, d4, d16, tc, ts1, ts2, w_in, x, gx1, mod, g_pre):
    def body(dqa_ref, dka_ref, dva_ref, dq1_ref, dk1_ref, dv1_ref, dq4_ref, dk4_ref, dv4_ref, dq16_ref, dk16_ref, dv16_ref,
             c_ref, s1_ref, s2_ref, w_ref, x_ref, gx_ref, mod_ref, g_ref, dproj_ref, dx_ref, accb_ref, accg_ref, scr):
        _acc_init(accb_ref, accg_ref)
        c, s1, s2 = c_ref[...], s1_ref[...], s2_ref[...]
        tot = lambda r1, r4, r16: r1[...] + _perm_load(r4, scr, 4) + _perm_load(r16, scr, 16)
        dqb = tot(dq1_ref, dq4_ref, dq16_ref)
        dkb = tot(dk1_ref, dk4_ref, dk16_ref)
        dvb = tot(dv1_ref, dv4_ref, dv16_ref)
        dproj = jnp.concatenate([
            _rope_t(dqa_ref[...], c, s1, s2) * QSCALE, _rope_t(_per_kv_head(dka_ref[...]), c, s1, s2),
            _per_kv_head(dva_ref[...]),
            _rope_t(dqb, c, s1, s2) * QSCALE, _rope_t(dkb, c, s1, s2), dvb], axis=1).astype(bf16)
        dproj_ref[...] = dproj
        dh = _dot_nt(dproj, w_ref[...])
        xn, r = _rms(x_ref[...])
        g = g_ref[...]
        dn = dh * (1.0 + mod_ref[1:2, :])
        dx_ref[...] = gx_ref[...] + _rms_bwd(dn * g, xn, r)
        accb_ref[0:1, :] += _colsum(dh * (xn * g))
        accb_ref[1:2, :] += _colsum(dh)
        accg_ref[0:1, :] += _colsum(dn * xn)

    return pl.pallas_call(
        body, name="attn_in_bwd", grid=(BL, NJ),
        in_specs=[_tok(AQ), _tok(AQ), _tok(AQ)] + [_tok(BW)] * 3 + [_perm_spec(4, BW)] * 3 + [_perm_spec(16, BW)] * 3
                 + [_tok(LANES)] * 3 + [_full((D, INW)), _tok(D), _tok(D), MOD_SPEC, _full((1, D))],
        out_specs=[_tok(INW), _tok(D), ACCB_SPEC, ACCG_SPEC],
        out_shape=[jax.ShapeDtypeStruct((BL, SEQ, INW), bf16), jax.ShapeDtypeStruct((BL, SEQ, D), f32)] + ACC_SHAPES,
        scratch_shapes=[pltpu.VMEM((BW // LANES, TM, LANES), f32)],
        compiler_params=_cp(("arbitrary", "arbitrary")),
    )(dqa, dka, dva, *d1, *d4, *d16, tc, ts1, ts2, w_in, x, gx1, mod, g_pre)


def _inv_lane():
    inv = np.float32(THETA) ** (-np.arange(0, ROT, 2, dtype=np.float32) / np.float32(ROT))
    lane = np.arange(LANES) % HD
    return jnp.asarray(np.where(lane < ROT, inv[lane % (ROT // 2)], 0.0).astype(np.float32)[None, :])


def _local_step(x, tabs, mod, target, w_in, later_weights, grad_ready, g_attn_pre,
                g_attn_post, sink_a, g_mix_a, g_mix_b, g_mlp_pre, g_mlp_post):
    tc, ts1, ts2 = [t.reshape(BL, SEQ, LANES) for t in tabs]

    (h, qa, ka, va, q1, k1, v1, q4, k4, v4, q16, k16, v16, w_in) = _attn_in(x, mod, g_attn_pre, w_in, tc, ts1, ts2)
    seqs = lambda t: t.reshape(t.shape[0] * t.shape[1], t.shape[2], t.shape[3])
    q4, k4, v4, q16, k16, v16 = [seqs(t) for t in (q4, k4, v4, q16, k16, v16)]
    oa, la = _attn_fwd(qa, ka, va, sink_a, max_dist=BLK - 1, o_dtype=f32, name="attn_a_fwd")
    o1, l1 = _attn_fwd(q1, k1, v1, None, max_dist=BLK, o_dtype=bf16, name="attn_b1_fwd")
    o4, l4 = _attn_fwd(q4, k4, v4, None, max_dist=BLK, o_dtype=bf16, name="attn_b4_fwd")
    o16, l16 = _attn_fwd(q16, k16, v16, None, max_dist=BLK, o_dtype=bf16, name="attn_b16_fwd")
    b4 = lambda t: t.reshape(BL, 4, SEQ // 4, t.shape[-1])
    b16 = lambda t: t.reshape(BL, 16, SEQ // 16, t.shape[-1])
    w_out, mlp_weights, mod = later_weights((oa, o1, o4, o16), mod)
    x1, y, mixed, ob = _mix_out(oa, o1, l1, b4(o4), b4(l4), b16(o16), b16(l16), g_mix_a, g_mix_b, w_out, x, mod, g_attn_post)
    w_up, w_down = mlp_weights((x1,))
    h2, u, a = _mlp_up(x1, mod, g_mlp_pre, w_up)
    gx, dy2, accb_d, accg_d = _mlp_down(a, w_down, x1, target, mod, g_mlp_post)

    flat = lambda t: t.reshape(BL * SEQ, t.shape[-1])
    mod = grad_ready("w_down", _matmul_tn(flat(a), flat(dy2), tn=D, col_blocked=False, name="grad_w_down", out_dtype=bf16), mod)
    du, gx1, accb_m, accg_m = _mlp_bwd(dy2, u, w_down, w_up, x1, gx, mod, g_mlp_pre)
    mod = grad_ready("w_up", _matmul_tn(flat(h2), flat(du), tn=D, col_blocked=True, name="grad_w_up", out_dtype=bf16), mod)

    dy, doa, do1, do4, do16, da, dl1, dl4, dl16, accb_o, accg_o = _attn_out_bwd(
        gx1, y, mod, g_attn_post, w_out, oa, ob, g_mix_a, g_mix_b, l1, b4(l4), b16(l16))
    sink_behind = grad_ready("w_out", _matmul_tn(flat(mixed), flat(dy), tn=D, col_blocked=False, name="grad_w_out",
                                                  out_dtype=bf16), sink_a)
    dqa, dka, dva, dsink = _attn_bwd(qa, ka, va, doa, da, la, sink_behind, max_dist=BLK - 1, name="attn_a_bwd")
    d1 = _attn_bwd(q1, k1, v1, do1, dl1, l1, None, max_dist=BLK, name="attn_b1_bwd")
    d4 = _attn_bwd(q4, k4, v4, seqs(do4), seqs(dl4), l4, None, max_dist=BLK, name="attn_b4_bwd")
    d16 = _attn_bwd(q16, k16, v16, seqs(do16), seqs(dl16), l16, None, max_dist=BLK, name="attn_b16_bwd")
    dproj, grad_x, accb_i, accg_i = _attn_in_bwd(dqa, dka, dva, d1, [b4(t) for t in d4], [b16(t) for t in d16],
                                                 tc, ts1, ts2, w_in, x, gx1, mod, g_attn_pre)
    gw_in = _grad_w_in(flat(h), flat(dproj))
    dsink = grad_ready("w_in", gw_in, dsink)

    return grad_x, (accb_i, accb_o, accb_m, accb_d, accg_i, accg_o, accg_m, accg_d, dsink)


ADAW = NMOD * D // NCHIP


def _pos():
    return lax.axis_index("x"), lax.axis_index("y"), lax.axis_index("c")


def _flip(v, bit):
    return 1 - v if bit else v


def _all_peers(x, y, c):
    return [(_flip(x, k >> 2 & 1), _flip(y, k >> 1 & 1), _flip(c, k & 1)) for k in range(1, NDEV)]


def _other_chips(x, y):
    return [(1 - x, y), (x, 1 - y), (1 - x, 1 - y)]


def _rcopy(src, dst, send, recv, k, dev, k_recv=None):
    return pltpu.make_async_remote_copy(src_ref=src, dst_ref=dst, send_sem=send.at[k],
                                        recv_sem=recv.at[k if k_recv is None else k_recv],
                                        device_id=dev, device_id_type=MESH)


def _gather_small(src, buf, send, recv):
    x, y, c = _pos()
    me = 4 * x + 2 * y + c
    peers = _all_peers(x, y, c)
    sends = [_rcopy(src, buf.at[me], send, recv, k, p) for k, p in enumerate(peers)]
    for cp in sends:
        cp.start()
    for k, (px, py, pc) in enumerate(peers):
        _rcopy(src, buf.at[4 * px + 2 * py + pc], send, recv, k, (px, py, pc)).wait_recv()
    for cp in sends:
        cp.wait_send()
    return me


def _ada_fwd(c_in, w_ada, b_cols):
    def body(c_ref, w_hbm, b_ref, mod_ref, cond_ref, cbuf, mbuf, w_ref, s1, r1, s2, r2, wsem):
        x, y, c = _pos()
        chip = 2 * x + y
        wcopy = pltpu.make_async_copy(w_hbm, w_ref, wsem)
        wcopy.start()
        me = _gather_small(c_ref, cbuf, s1, r1)
        cbuf[me] = c_ref[...]
        for i in range(NDEV):
            cond_ref[BL * i:BL * (i + 1), :] = cbuf[i]
        call = cond_ref[...]
        cond = call / (1.0 + jnp.exp(-call))
        cond_ref[...] = cond
        wcopy.wait()
        mbuf[chip] = _dot(cond.astype(bf16), w_ref[...].astype(bf16)) + b_ref[...]
        chips = _other_chips(x, y)
        sends = [_rcopy(mbuf.at[chip], mbuf.at[chip], s2, r2, j, (px, py, c)) for j, (px, py) in enumerate(chips)]
        for cp in sends:
            cp.start()
        for j, (px, py) in enumerate(chips):
            _rcopy(mbuf.at[chip], mbuf.at[2 * px + py], s2, r2, j, (px, py, c)).wait_recv()
        for cp in sends:
            cp.wait_send()
        row = lax.broadcasted_iota(jnp.int32, (BL * NDEV, ADAW), 0)
        for s in range(NCHIP):
            slab = mbuf[s]
            for j in range(BL):
                mod_ref[j:j + 1, ADAW * s:ADAW * (s + 1)] = jnp.sum(jnp.where(row == BL * me + j, slab, 0.0), axis=0, keepdims=True)

    vm = pl.BlockSpec(memory_space=pltpu.VMEM)
    return pl.pallas_call(
        body, name="ada_fwd", in_specs=[vm, pl.BlockSpec(memory_space=pl.ANY), vm], out_specs=[vm, vm],
        out_shape=[jax.ShapeDtypeStruct((BL, NMOD * D), f32), jax.ShapeDtypeStruct((BL * NDEV, D), f32)],
        scratch_shapes=[pltpu.VMEM((NDEV, BL, D), f32), pltpu.VMEM((NCHIP, BL * NDEV, ADAW), f32),
                        pltpu.VMEM((D, ADAW), f32),
                        pltpu.SemaphoreType.DMA((NDEV - 1,)), pltpu.SemaphoreType.DMA((NDEV - 1,)),
                        pltpu.SemaphoreType.DMA((NCHIP - 1,)), pltpu.SemaphoreType.DMA((NCHIP - 1,)),
                        pltpu.SemaphoreType.DMA],
        compiler_params=pltpu.CompilerParams(vmem_limit_bytes=VMEM_LIMIT),
    )(c_in, w_ada, b_cols)


PAY_ROWS = 4


def _small_pack(accs):
    def body(bi, bo, bm, bd, gi, go, gm, gd, dsink, pay):
        pay[...] = jnp.zeros_like(pay)
        for b in range(BL):
            for k, (ref, r) in enumerate(((bi, 1), (bi, 0), (bo, 0), (bm, 1), (bm, 0), (bd, 0))):
                pay[b:b + 1, D * k:D * (k + 1)] = ref[b, r:r + 1, :]
        for off, ref, r in ((OFF_G_ATTN_PRE, gi, 0), (OFF_G_ATTN_POST, go, 0), (OFF_G_MIX_A, go, 1), (OFF_G_MLP_PRE, gm, 0),
                            (OFF_G_MLP_POST, gd, 0)):
            pay[BL:BL + 1, off:off + D] = ref[r:r + 1, :]
        eye = lax.broadcasted_iota(jnp.int32, (NHEAD, LANES), 0) == lax.broadcasted_iota(jnp.int32, (NHEAD, LANES), 1)
        pay[BL:BL + 1, OFF_SINK:OFF_SINK + LANES] = jnp.sum(jnp.where(eye, dsink[...], 0.0), axis=0, keepdims=True)
        pay[BL:BL + 1, OFF_LOSS:OFF_LOSS + LANES] = gd[1:2, 0:LANES]

    vm = pl.BlockSpec(memory_space=pltpu.VMEM)
    return pl.pallas_call(body, name="small_pack", in_specs=[vm] * 9, out_specs=vm,
                          out_shape=jax.ShapeDtypeStruct((PAY_ROWS, PAYW), f32))(*accs)


def _small_copies(src, land, send, recv):
    x, y, c = _pos()
    me = 4 * x + 2 * y + c
    return [(_rcopy(src, land.at[me], send, recv, k, p), _rcopy(src, land.at[4 * p[0] + 2 * p[1] + p[2]], send, recv, k, p))
            for k, p in enumerate(_all_peers(x, y, c))]


def _small_sum(own, landed, cond_all):
    def body(pay, land, cond_ref, gw_ref, gb_ref, small_ref, pbuf, dall):
        x, y, c = _pos()
        chip = 2 * x + y
        me = 4 * x + 2 * y + c
        for i in range(NDEV):
            @pl.when(me == i)
            def _():
                pbuf[i] = pay[...]

            @pl.when(me != i)
            def _():
                pbuf[i] = land[i]
        small = pbuf[0, BL:BL + 1, :]
        for i in range(1, NDEV):
            small = small + pbuf[i, BL:BL + 1, :]
        small_ref[...] = small
        for i in range(NDEV):
            dall[BL * i:BL * (i + 1), :] = pbuf[i, 0:BL, :]
        gb_ref[...] = jnp.sum(dall[...], axis=0, keepdims=True)
        cols = jnp.zeros((BL * NDEV, ADAW), f32)
        for s in range(NCHIP):
            cols = cols + jnp.where(chip == s, dall[:, ADAW * s:ADAW * (s + 1)], 0.0)
        gw_ref[...] = _dot_tn(cond_ref[...].astype(bf16), cols.astype(bf16))

    vm = pl.BlockSpec(memory_space=pltpu.VMEM)
    return pl.pallas_call(
        body, name="small_sum", in_specs=[vm] * 3, out_specs=[vm] * 3,
        out_shape=[jax.ShapeDtypeStruct((D, ADAW), f32), jax.ShapeDtypeStruct((1, PAYW), f32), jax.ShapeDtypeStruct((1, PAYW), f32)],
        scratch_shapes=[pltpu.VMEM((NDEV, PAY_ROWS, PAYW), f32), pltpu.VMEM((BL * NDEV, PAYW), f32)],
        compiler_params=pltpu.CompilerParams(vmem_limit_bytes=VMEM_LIMIT),
    )(own, landed, cond_all)


def _half(ref, c):
    r2 = ref.shape[0] // 2
    return ref.at[pl.ds(c * r2 if isinstance(c, int) else pl.multiple_of(c * r2, 16), r2), :]


HBM_SPEC = pl.BlockSpec(memory_space=pltpu.HBM)
SEM_SPEC = pl.BlockSpec(memory_space=pltpu.SEMAPHORE)
EFFECT = pltpu.SideEffectType.DATAFLOW_SIDE_EFFECTING
NLINK = NCHIP - 1


def _in_hbm(a):
    return pltpu.with_memory_space_constraint(a, pltpu.HBM)


NSEM = 8


def _split_start(name, srcs, land_shapes, builds, carry, after=(), lands=None):
    n = len(srcs)
    na, nc = len(after), len(carry)

    def body(*refs):
        src, land = refs[:n], refs[n:2 * n]
        kept = refs[2 * n + na:2 * n + na + nc]
        outs = refs[2 * n + na + nc:]
        send, recv, passed = outs[:n], outs[n:2 * n], outs[4 * n:]
        for t in range(n):
            for out_cp, _ in builds[t](src[t], land[t], send[t], recv[t]):
                out_cp.start()
        for a, b in zip(kept, passed):
            b[...] = a[...]

    if lands is None:
        lands = [lax.empty(s.shape, s.dtype) for s in land_shapes]
    lands = [_in_hbm(a) for a in lands]
    sems = [pltpu.SemaphoreType.DMA((NSEM,))] * (2 * n)
    thru = [pltpu.HBM(a.shape, a.dtype) for a in list(srcs) + lands]
    vm = pl.BlockSpec(memory_space=pltpu.VMEM)
    res = pl.pallas_call(
        body, name=name, out_shape=sems + thru + [jax.ShapeDtypeStruct(a.shape, a.dtype) for a in carry],
        in_specs=[HBM_SPEC] * (2 * n) + [pl.BlockSpec(memory_space=pl.ANY)] * na + [vm] * nc,
        out_specs=[SEM_SPEC] * (2 * n) + [HBM_SPEC] * (2 * n) + [vm] * nc,
        input_output_aliases={i: 2 * n + i for i in range(2 * n)},
        compiler_params=pltpu.CompilerParams(has_side_effects=EFFECT),
    )(*[_in_hbm(a) for a in srcs], *lands, *after, *carry)
    flight = [(res[2 * n + t], res[3 * n + t], res[t], res[n + t]) for t in range(n)]
    return flight, list(res[4 * n:])


def _split_wait(name, flight, builds, after):
    m = len(flight)
    na = len(after)

    def body(*refs):
        src, land, send, recv = refs[:m], refs[m:2 * m], refs[2 * m:3 * m], refs[3 * m:4 * m]
        for t in range(m):
            for out_cp, in_cp in builds[t](src[t], land[t], send[t], recv[t]):
                out_cp.wait_send()
                in_cp.wait_recv()

    ops = [f[0] for f in flight] + [f[1] for f in flight] + [f[2] for f in flight] + [f[3] for f in flight]
    res = pl.pallas_call(
        body, name=name, out_shape=[pltpu.HBM(a.shape, a.dtype) for a in ops[:2 * m]],
        in_specs=[HBM_SPEC] * (2 * m) + [SEM_SPEC] * (2 * m) + [pl.BlockSpec(memory_space=pl.ANY)] * na,
        out_specs=[HBM_SPEC] * (2 * m), input_output_aliases={i: i for i in range(2 * m)},
        compiler_params=pltpu.CompilerParams(has_side_effects=EFFECT),
    )(*ops, *after)
    return res[:m], res[m:2 * m]


def _weight_copies(src, land, send, recv):
    x, y, c = _pos()
    chip = 2 * x + y
    return [(_rcopy(_half(src, c), _half(land.at[chip], c), send, recv, j, (px, py, c)),
             _rcopy(_half(src, c), _half(land.at[2 * px + py], c), send, recv, j, (px, py, c)))
            for j, (px, py) in enumerate(_other_chips(x, y))]


NDIRECT = NDEV - 1


def _direct_grad_copies(src, land, send, recv):
    x, y, c = _pos()
    out, arrive = [], []
    for j, (px, py) in enumerate(_other_chips(x, y)):
        for hc in range(2):
            out.append(_rcopy(_half(src.at[2 * px + py], hc), land.at[2 * j + c], send, recv, 2 * j + hc, (px, py, hc),
                              k_recv=2 * j + c))
            arrive.append(_rcopy(_half(src.at[2 * px + py], hc), land.at[2 * j + hc], send, recv, 2 * j + hc, (px, py, hc)))
    own = _rcopy(_half(src.at[2 * x + y], 1 - c), land.at[NDIRECT - 1], send, recv, NDIRECT - 1, (x, y, 1 - c))
    return list(zip(out, arrive)) + [(own, own)]


def _pair_weight_copies(src, land, send, recv):
    x, y, c = _pos()
    sib = (x, y, 1 - c)
    cps = []
    for j, (px, py) in enumerate(_other_chips(x, y)):
        mine, theirs = _half(land.at[2 * px + py], c), _half(land.at[2 * px + py], 1 - c)
        cps.append((_rcopy(mine, mine, send, recv, j, sib), _rcopy(theirs, theirs, send, recv, j, sib)))
    own = _rcopy(src, land.at[2 * x + y], send, recv, NLINK, sib)
    return cps + [(own, own)]


RS_ROWS = 256


def _chip_add(own, landed, pos_arr, name):
    nl, r2, cw = landed.shape
    rows = min(RS_ROWS, r2)
    nr = r2 // rows

    def body(s_ref, h_ref, q_ref, o_ref):
        acc = h_ref[...].astype(f32)
        for j in range(nl):
            acc = acc + q_ref[j].astype(f32)
        o_ref[...] = acc

    gs = pltpu.PrefetchScalarGridSpec(
        num_scalar_prefetch=1, grid=(nr,),
        in_specs=[pl.BlockSpec((None, rows, cw), lambda j, s: (s[0], s[1] * nr + j, 0)),
                  pl.BlockSpec((nl, rows, cw), lambda j, s: (0, j, 0))],
        out_specs=pl.BlockSpec((rows, cw), lambda j, s: (s[1] * nr + j, 0)))
    return pl.pallas_call(body, name=name, grid_spec=gs, out_shape=jax.ShapeDtypeStruct((2 * r2, cw), f32),
                          compiler_params=_cp(("arbitrary",)))(pos_arr, own, landed)


def _pair_gather_copies(src, land, send, recv):
    x, y, c = _pos()
    sib = (x, y, 1 - c)
    return [(_rcopy(_half(land, c), _half(land, c), send, recv, 0, sib),
             _rcopy(_half(land, 1 - c), _half(land, 1 - c), send, recv, 0, sib))]


def _adamw_math(w, g, m, v):
    m = B1 * m + (1.0 - B1) * g
    v = B2 * v + (1.0 - B2) * jnp.square(g)
    m_hat = m / (1.0 - B1 ** STEP)
    v_hat = v / (1.0 - B2 ** STEP)
    return -LR * (m_hat / (jnp.sqrt(v_hat) + AEPS) + WD * w), m, v


ADAM_BLOCK = 512 * 1024


def _adamw(w, g, m, v, name, after=()):
    r, cw = w.shape
    na = len(after)

    def body(w_ref, g_ref, m_ref, v_ref, *rest):
        go_ref, d_ref, mo_ref, vo_ref = rest[na:]
        g = g_ref[...]
        go_ref[...] = g
        d_ref[...], mo_ref[...], vo_ref[...] = _adamw_math(w_ref[...], g, m_ref[...], v_ref[...])

    rows = max(k for k in range(SUBLANES, ADAM_BLOCK // cw + 1, SUBLANES) if r % k == 0)
    spec = pl.BlockSpec((rows, cw), lambda i: (i, 0))
    return pl.pallas_call(body, name=name, grid=(r // rows,), in_specs=[spec] * 4 + [pl.BlockSpec(memory_space=pl.ANY)] * na,
                          out_specs=[spec] * 4, out_shape=[jax.ShapeDtypeStruct((r, cw), f32)] * 4,
                          compiler_params=_cp(("arbitrary",)))(w, g, m, v, *after)


SMALL = (("b_ada", None, PAYW), ("g_attn_pre", OFF_G_ATTN_PRE, D), ("g_attn_post", OFF_G_ATTN_POST, D), ("sink_a", OFF_SINK, 8),
         ("g_mix_a", OFF_G_MIX_A, AQ), ("g_mix_b", OFF_G_MIX_B, BW), ("g_mlp_pre", OFF_G_MLP_PRE, D), ("g_mlp_post", OFF_G_MLP_POST, D))


def _adamw_small(small, gb, params):
    n = len(SMALL)

    def body(*refs):
        small_ref, gb_ref = refs[:2]
        wmv = refs[2:2 + 3 * n]
        loss_ref = refs[2 + 3 * n]
        outs = refs[3 + 3 * n:]
        loss_ref[...] = small_ref[:, OFF_LOSS:OFF_LOSS + 1] * (0.5 / D)
        for i, (_, off, width) in enumerate(SMALL):
            g = gb_ref[...] if off is None else small_ref[:, off:off + width]
            w_ref, m_ref, v_ref = wmv[3 * i:3 * i + 3]
            outs[4 * i][...] = g
            outs[4 * i + 1][...], outs[4 * i + 2][...], outs[4 * i + 3][...] = _adamw_math(w_ref[...], g, m_ref[...], v_ref[...])

    vm = pl.BlockSpec(memory_space=pltpu.VMEM)
    out_shape = [jax.ShapeDtypeStruct((1, 1), f32)]
    for _, _, width in SMALL:
        out_shape += [jax.ShapeDtypeStruct((1, width), f32)] * 4
    flat = [a for wmv in params for a in wmv]
    res = pl.pallas_call(body, name="adamw_small", in_specs=[vm] * (2 + 3 * n), out_specs=[vm] * len(out_shape),
                         out_shape=out_shape)(small, gb, *flat)
    return res[0], {name: res[1 + 4 * i:5 + 4 * i] for i, (name, _, _) in enumerate(SMALL)}


def kernel(x, c, positions, w_ada, b_ada, g_attn_pre, g_attn_post, w_in, sink_a, g_mix_a, g_mix_b, w_out, g_mlp_pre, g_mlp_post, w_up, w_down, loss_target, m_w_ada, m_b_ada, m_g_attn_pre, m_g_attn_post, m_w_in, m_sink_a, m_g_mix_a, m_g_mix_b, m_w_out, m_g_mlp_pre, m_g_mlp_post, m_w_up, m_w_down, v_w_ada, v_b_ada, v_g_attn_pre, v_g_attn_post, v_w_in, v_sink_a, v_g_mix_a, v_g_mix_b, v_w_out, v_g_mlp_pre, v_g_mlp_post, v_w_up, v_w_down):
    given = dict(w_ada=w_ada, b_ada=b_ada, g_attn_pre=g_attn_pre, g_attn_post=g_attn_post, w_in=w_in, sink_a=sink_a, g_mix_a=g_mix_a,
                 g_mix_b=g_mix_b, w_out=w_out, g_mlp_pre=g_mlp_pre, g_mlp_post=g_mlp_post, w_up=w_up, w_down=w_down)
    moms = dict(w_ada=(m_w_ada, v_w_ada), b_ada=(m_b_ada, v_b_ada), g_attn_pre=(m_g_attn_pre, v_g_attn_pre),
                g_attn_post=(m_g_attn_post, v_g_attn_post), w_in=(m_w_in, v_w_in), sink_a=(m_sink_a, v_sink_a),
                g_mix_a=(m_g_mix_a, v_g_mix_a), g_mix_b=(m_g_mix_b, v_g_mix_b), w_out=(m_w_out, v_w_out),
                g_mlp_pre=(m_g_mlp_pre, v_g_mlp_pre), g_mlp_post=(m_g_mlp_post, v_g_mlp_post), w_up=(m_w_up, v_w_up),
                w_down=(m_w_down, v_w_down))
    order = ["w_ada", "b_ada", "g_attn_pre", "g_attn_post", "w_in", "sink_a", "g_mix_a", "g_mix_b", "w_out", "g_mlp_pre",
             "g_mlp_post", "w_up", "w_down"]
    xi, yi, ci = _pos()
    chip = 2 * xi + yi

    pos_arr = jnp.stack([chip, ci]).astype(jnp.int32)
    big = ("w_in", "w_out", "w_up", "w_down")

    gathered = [jax.ShapeDtypeStruct((NCHIP,) + given[n].shape[1:], bf16) for n in big]
    flight_in, (c, inv_lane) = _split_start("weights_start_first", [w_in[0].astype(bf16)], gathered[:1], [_weight_copies],
                                            [c, _inv_lane()])
    inv_lane, rest = lax.optimization_barrier((inv_lane, [given[n][0] for n in big[1:]]))
    tabs = _rope_tables(positions.reshape(BL * SEQ, 1), inv_lane)
    c, tabs, rest = lax.optimization_barrier((c, tabs, [w.astype(bf16) for w in rest]))
    b_cols = lax.dynamic_slice(b_ada, (0, chip * ADAW), (1, ADAW))
    mod, cond_all = _ada_fwd(c, w_ada[0], b_cols)

    srcs, lands = _split_wait("weights_wait_first", flight_in, [_weight_copies], (mod,))
    cross, (mod,) = _split_start("weights_pair_start_first", srcs, None, [_pair_weight_copies], [mod], lands=lands)
    flight_rest, (mod,) = _split_start("weights_start_rest", rest, gathered[1:], [_weight_copies] * 3, [mod])
    _, (win_g,) = _split_wait("weights_pair_wait_first", cross, [_pair_weight_copies], (mod,))
    mod = mod.reshape(BL, NMOD, D)

    def later_weights(after, carry):
        srcs, lands = _split_wait("weights_wait_rest", flight_rest, [_weight_copies] * 3, after)
        fl, (carry,) = _split_start("weights_pair_start_rest", srcs, None, [_pair_weight_copies] * 3, [carry], lands=lands)
        _, (wout_g,) = _split_wait("weights_pair_wait_out", fl[:1], [_pair_weight_copies], ())

        def mlp_weights(after):
            _, (wup_g, wdn_g) = _split_wait("weights_pair_wait_mlp", fl[1:], [_pair_weight_copies] * 2, after)
            return wup_g, wdn_g.reshape(DFF, D)

        return wout_g.reshape(D, D), mlp_weights, carry

    waiting, pending = {}, {}

    def send_grad(name, slab, carry):
        land = jax.ShapeDtypeStruct((NDIRECT, slab.shape[1] // 2, slab.shape[2]), bf16)
        pending[name], (carry,) = _split_start("grad_start_" + name, [slab], [land], [_direct_grad_copies], [carry])
        return carry

    def grad_ready(name, g, carry):
        slab = g if g.ndim == 3 else g.reshape(NCHIP, g.shape[0] // NCHIP, g.shape[1])
        if name == "w_in":
            waiting[name] = slab
            return carry
        return send_grad(name, slab, carry)

    grad_x, accs = _local_step(x, tabs, mod, loss_target, win_g, later_weights, grad_ready,
                               g_attn_pre, g_attn_post, sink_a, g_mix_a, g_mix_b, g_mlp_pre, g_mlp_post)

    grads, out = {}, {}

    def update(n, after=()):
        tr = (lambda a: a.T) if n == "w_in" else (lambda a: a)
        res = _adamw(tr(given[n][0]), tr(grads[n]), tr(moms[n][0][0]), tr(moms[n][1][0]), "adamw_" + n, after)
        out[n] = tuple(tr(a)[None] for a in res)
        return res[3]

    def finish(names, after, first=()):
        fl = sum((pending[n] for n in names), [])
        halves, landed = _split_wait("grad_wait_" + names[0], fl, [_direct_grad_copies] * len(names), after)
        flights, token = [], jnp.zeros((SUBLANES, LANES), f32)
        for h, q, n in zip(halves, landed, names):
            full = _chip_add(h, q, pos_arr, "grad_chip_sum_" + n)
            flights.append(_split_start("grad_gather_start_" + n, [token], None, [_pair_gather_copies], [], lands=[full])[0])
            token = flights[-1][0][0]
        last = [update(n, (token,)) for n in first]
        for n, fl1 in zip(names, flights):
            after = tuple(last) if last else () if fl1 is flights[-1] else (token,)
            _, (grads[n],) = _split_wait("grad_gather_wait_" + n, fl1, [_pair_gather_copies], after)
            last = [update(n)]
        return last[0]

    fl_small, (cond_all,) = _split_start("small_start", [_small_pack(accs)], [jax.ShapeDtypeStruct((NDEV, PAY_ROWS, PAYW), f32)],
                                         [_small_copies], [cond_all])
    cond_all = send_grad("w_in", waiting["w_in"], cond_all)
    last = finish(("w_down", "w_up", "w_out"), (cond_all,))
    (pay,), (landed,) = _split_wait("small_wait", fl_small, [_small_copies], (last,))
    grads["w_ada"], gb, small = _small_sum(pay, landed, cond_all)
    finish(("w_in",), (small,), first=("w_ada",))
    loss, res = _adamw_small(small, gb, [(given[n], moms[n][0], moms[n][1]) for n, _, _ in SMALL])
    for n, _, _ in SMALL:
        out[n] = tuple(res[n])
    return (loss.reshape(()), grad_x, *[out[n][0] for n in order], *[out[n][1] for n in order],
            *[out[n][2] for n in order], *[out[n][3] for n in order])
```

```python
import numpy as np
import jax
import jax.numpy as jnp
from jax import lax
from jax.experimental import pallas as pl
from jax.experimental.pallas import tpu as pltpu

f32 = jnp.float32
bf16 = jnp.bfloat16
MESH = pl.DeviceIdType.MESH

D = 1024
SEQ = 2048
BL = 2
HD = 64
AQ = 512
AKV = 128
BW = 512
INW = 2304
DFF = 4096
NMOD = 6
ROT = 16
THETA = 500000.0
EPS = 1e-6
NEG = -1e30
BLK = 128
TM = 512
NJ = SEQ // TM
LANES = 128
SUBLANES = 8
NHEAD = AQ // HD
QSCALE = HD ** -0.5
NCHIP = 4
NDEV = 8
VMEM_LIMIT = 56 << 20

LR, B1, B2, AEPS, WD, STEP = 0.001, 0.9, 0.999, 1e-08, 0.01, 10

OFF_G_ATTN_PRE, OFF_G_ATTN_POST, OFF_G_MIX_A, OFF_G_MIX_B = 0, 1024, 2048, 2560
OFF_G_MLP_PRE, OFF_G_MLP_POST, OFF_SINK, OFF_LOSS = 3072, 4096, 5120, 5248
PAYW = NMOD * D


def _cp(sem=None):
    return pltpu.CompilerParams(dimension_semantics=sem, vmem_limit_bytes=VMEM_LIMIT)


def _dot(a, b):
    return jnp.dot(a, b, preferred_element_type=f32)


def _dot_nt(a, b):
    return lax.dot_general(a, b, (((1,), (1,)), ((), ())), preferred_element_type=f32)


def _dot_tn(a, b):
    return lax.dot_general(a, b, (((0,), (0,)), ((), ())), preferred_element_type=f32)


def _rms(x):
    r = lax.rsqrt(jnp.mean(x * x, axis=-1, keepdims=True) + EPS)
    return x * r, r


def _rms_bwd(dy, y, r):
    return r * (dy - y * jnp.mean(dy * y, axis=-1, keepdims=True))


def _colsum(v):
    return jnp.sum(v, axis=0, keepdims=True)


def _rope(p, c, s1, s2):
    outs = []
    for c0 in range(0, p.shape[1], LANES):
        pc = p[:, c0:c0 + LANES]
        outs.append(pc * c + pltpu.roll(pc, LANES - ROT // 2, 1) * s1 + pltpu.roll(pc, ROT // 2, 1) * s2)
    return outs[0] if len(outs) == 1 else jnp.concatenate(outs, axis=1)


def _rope_t(g, c, s1, s2):
    outs = []
    for c0 in range(0, g.shape[1], LANES):
        gc = g[:, c0:c0 + LANES]
        outs.append(gc * c + pltpu.roll(gc * s1, ROT // 2, 1) + pltpu.roll(gc * s2, LANES - ROT // 2, 1))
    return outs[0] if len(outs) == 1 else jnp.concatenate(outs, axis=1)


def _perm_store(val, scr, out_ref, d):
    nc = val.shape[1] // LANES
    for c in range(nc):
        scr[c] = val[:, LANES * c:LANES * (c + 1)]
    for c in range(nc):
        for r in range(d):
            out_ref[r, :, LANES * c:LANES * (c + 1)] = scr[c, pl.ds(r, TM // d, stride=d), :].astype(out_ref.dtype)


def _perm_load(in_ref, scr, d):
    nc = in_ref.shape[-1] // LANES
    for c in range(nc):
        for r in range(d):
            scr[c, pl.ds(r, TM // d, stride=d), :] = in_ref[r, :, LANES * c:LANES * (c + 1)].astype(f32)
    return jnp.concatenate([scr[c] for c in range(nc)], axis=1)


def _per_query_head(kv):
    r = pltpu.roll(kv, HD, 1)
    lo = lax.broadcasted_iota(jnp.int32, kv.shape, 1) < HD
    return jnp.concatenate([jnp.where(lo, kv, r), jnp.where(lo, r, kv)], axis=1)


def _per_kv_head(g):
    g0, g1 = g[:, :LANES] + g[:, LANES:2 * LANES], g[:, 2 * LANES:3 * LANES] + g[:, 3 * LANES:]
    lo = lax.broadcasted_iota(jnp.int32, g0.shape, 1) < HD
    return jnp.where(lo, g0 + pltpu.roll(g0, HD, 1), g1 + pltpu.roll(g1, HD, 1))


def _tok(w):
    return pl.BlockSpec((None, TM, w), lambda b, j: (b, j, 0))


def _perm_spec(d, w):
    return pl.BlockSpec((None, d, TM // d, w), lambda b, j: (b, 0, j, 0))


def _full(shape):
    n = len(shape)
    return pl.BlockSpec(shape, lambda b, j: (0,) * n)


MOD_SPEC = pl.BlockSpec((None, NMOD, D), lambda b, j: (b, 0, 0))
ACCB_SPEC = pl.BlockSpec((None, SUBLANES, D), lambda b, j: (b, 0, 0))
ACCG_SPEC = pl.BlockSpec((SUBLANES, D), lambda b, j: (0, 0))
ACC_SHAPES = [jax.ShapeDtypeStruct((BL, SUBLANES, D), f32), jax.ShapeDtypeStruct((SUBLANES, D), f32)]


def _acc_init(accb_ref, accg_ref):
    b, j = pl.program_id(0), pl.program_id(1)

    @pl.when(j == 0)
    def _():
        accb_ref[...] = jnp.zeros_like(accb_ref)

    @pl.when((b == 0) & (j == 0))
    def _():
        accg_ref[...] = jnp.zeros_like(accg_ref)


def _rope_tables(pos_col, inv_lane):
    def body(p_ref, inv_ref, c_ref, s1_ref, s2_ref):
        ang = p_ref[...].astype(f32) * inv_ref[...]
        j = lax.broadcasted_iota(jnp.int32, (TM, LANES), 1) % HD
        cs, sn = jnp.cos(ang), jnp.sin(ang)
        c_ref[...] = jnp.where(j < ROT, cs, 1.0)
        s1_ref[...] = jnp.where(j < ROT // 2, -sn, 0.0)
        s2_ref[...] = jnp.where((j >= ROT // 2) & (j < ROT), sn, 0.0)

    n = BL * SEQ // TM
    return pl.pallas_call(
        body, name="rope_tables", grid=(n,),
        in_specs=[pl.BlockSpec((TM, 1), lambda i: (i, 0)), pl.BlockSpec((1, LANES), lambda i: (0, 0))],
        out_specs=[pl.BlockSpec((TM, LANES), lambda i: (i, 0))] * 3,
        out_shape=[jax.ShapeDtypeStruct((BL * SEQ, LANES), f32)] * 3,
    )(pos_col, inv_lane)


def _attn_in(x, mod, g_pre, w_in, tc, ts1, ts2):
    def body(x_ref, mod_ref, g_ref, wg_ref, c_ref, s1_ref, s2_ref,
             h_ref, qa_ref, ka_ref, va_ref, q1_ref, k1_ref, v1_ref, q4_ref, k4_ref, v4_ref, q16_ref, k16_ref, v16_ref,
             w_ref, scr):
        @pl.when((pl.program_id(0) == 0) & (pl.program_id(1) == 0))
        def _():
            w_ref[...] = jnp.concatenate([wg_ref[s] for s in range(NCHIP)], axis=1)

        xn, _ = _rms(x_ref[...])
        h = (xn * g_ref[...]) * (1.0 + mod_ref[1:2, :]) + mod_ref[0:1, :]
        hb = h.astype(bf16)
        h_ref[...] = hb
        proj = _dot(hb, w_ref[...])
        c, s1, s2 = c_ref[...], s1_ref[...], s2_ref[...]
        o1, o2, o3, o4, o5 = AQ, AQ + AKV, AQ + 2 * AKV, AQ + 2 * AKV + BW, AQ + 2 * AKV + 2 * BW
        qa_ref[...] = (_rope(proj[:, :o1], c, s1, s2) * QSCALE).astype(bf16)
        ka_ref[...] = _per_query_head(_rope(proj[:, o1:o2], c, s1, s2)).astype(bf16)
        va_ref[...] = _per_query_head(proj[:, o2:o3]).astype(bf16)
        qb = _rope(proj[:, o3:o4], c, s1, s2) * QSCALE
        kb = _rope(proj[:, o4:o5], c, s1, s2)
        vb = proj[:, o5:]
        for val, r1, r4, r16 in ((qb, q1_ref, q4_ref, q16_ref), (kb, k1_ref, k4_ref, k16_ref), (vb, v1_ref, v4_ref, v16_ref)):
            r1[...] = val.astype(bf16)
            _perm_store(val, scr, r4, 4)
            _perm_store(val, scr, r16, 16)

    nat = lambda w: jax.ShapeDtypeStruct((BL, SEQ, w), bf16)
    p4 = jax.ShapeDtypeStruct((BL, 4, SEQ // 4, BW), bf16)
    p16 = jax.ShapeDtypeStruct((BL, 16, SEQ // 16, BW), bf16)
    return pl.pallas_call(
        body, name="attn_in", grid=(BL, NJ),
        in_specs=[_tok(D), MOD_SPEC, _full((1, D)), _full((NCHIP, D, INW // NCHIP)), _tok(LANES), _tok(LANES), _tok(LANES)],
        out_specs=([_tok(D), _tok(AQ), _tok(2 * AKV), _tok(2 * AKV)] + [_tok(BW)] * 3 + [_perm_spec(4, BW)] * 3 + [_perm_spec(16, BW)] * 3
                   + [_full((D, INW))]),
        out_shape=[nat(D), nat(AQ), nat(2 * AKV), nat(2 * AKV)] + [nat(BW)] * 3 + [p4] * 3 + [p16] * 3
                  + [jax.ShapeDtypeStruct((D, INW), bf16)],
        scratch_shapes=[pltpu.VMEM((BW // LANES, TM, LANES), f32)],
        compiler_params=_cp(("arbitrary", "arbitrary")),
    )(x, mod, g_pre, w_in, tc, ts1, ts2)


def _kv_cat(cur_ref, prev_ref, p, cache):
    key = (id(cur_ref), p)
    if key not in cache:
        sl = slice(LANES * p, LANES * (p + 1))
        cache[key] = cur_ref[:, sl] if prev_ref is None else jnp.concatenate([prev_ref[:, sl], cur_ref[:, sl]], axis=0)
    return cache[key]


def _lane_half(a, hh):
    lo = lax.broadcasted_iota(jnp.int32, a.shape, 1) < HD
    return jnp.where(lo, a, jnp.zeros_like(a)) if hh == 0 else jnp.where(lo, jnp.zeros_like(a), a)


ATT_UNITS = 4


def _att_units(nb):
    return ATT_UNITS if nb == 1 else min(ATT_UNITS, nb)


def _attn_specs(n, nb, descending):
    u = _att_units(nb)
    if nb == 1:
        return (lambda ww: pl.BlockSpec((u, BLK, ww), lambda a, i: (a, 0, 0))), None, (n // u, 1)
    steps = nb // u
    at = (lambda i: steps - 1 - i) if descending else (lambda i: i)
    cur = lambda ww: pl.BlockSpec((None, u * BLK, ww), lambda a, i: (a, at(i), 0))
    prev = lambda ww: pl.BlockSpec((None, BLK, ww), lambda a, i: (a, jnp.maximum(u * at(i) - 1, 0), 0))
    return cur, prev, (n, steps)


def _attn_fwd(q, k, v, sink, *, max_dist, o_dtype, name):
    n, l, w = q.shape
    wk = k.shape[-1]
    nb = l // BLK
    has_sink = sink is not None

    def body(*refs):
        sink_ref = None
        if has_sink:
            sink_ref, refs = refs[0], refs[1:]
        if nb > 1:
            q_ref, kc_ref, kp_ref, vc_ref, vp_ref, o_ref, lse_ref = refs[:7]
            first = pl.program_id(1) == 0
            for u in range(_att_units(nb)):
                rows, before = pl.ds(BLK * u, BLK), pl.ds(BLK * (u - 1), BLK)
                unit(q_ref.at[rows, :], kc_ref.at[rows, :], kp_ref if u == 0 else kc_ref.at[before, :],
                     vc_ref.at[rows, :], vp_ref if u == 0 else vc_ref.at[before, :], o_ref.at[rows, :], lse_ref.at[rows, :],
                     jnp.logical_not(first) if u == 0 else True, sink_ref, *refs[7:])
        else:
            q_ref, kc_ref, vc_ref, o_ref, lse_ref = refs[:5]
            for u in range(_att_units(nb)):
                unit(q_ref.at[u], kc_ref.at[u], None, vc_ref.at[u], None, o_ref.at[u], lse_ref.at[u], None, sink_ref, *refs[5:])

    def unit(q_ref, kc_ref, kp_ref, vc_ref, vp_ref, o_ref, lse_ref, has_prev, sink_ref, sscr, pscr, dscr):
        qi = lax.broadcasted_iota(jnp.int32, (BLK, BLK), 0)
        kj = lax.broadcasted_iota(jnp.int32, (BLK, BLK), 1)
        tri = kj <= qi
        eye = kj == qi
        cache = {}
        for p in range(w // LANES):
            qpair = q_ref[:, LANES * p:LANES * (p + 1)]
            kcat = _kv_cat(kc_ref, kp_ref, p // share, cache)
            for hh in range(2):
                s = _dot_nt(_lane_half(qpair, hh), kcat)
                if nb > 1:
                    sp = s[:, :BLK] if has_prev is True else jnp.where(has_prev, s[:, :BLK], NEG)
                    sscr[2 * p + hh] = jnp.where(tri, s[:, BLK:], sp)
                    if diag:
                        dscr[2 * p + hh] = jnp.where(eye, sp, NEG)
                else:
                    sscr[2 * p + hh] = jnp.where(tri, s, NEG)
        lane = lax.broadcasted_iota(jnp.int32, (BLK, LANES), 1)
        lse_all = jnp.zeros((BLK, LANES), f32)
        for p in range(w // LANES):
            for hh in range(2):
                h = 2 * p + hh
                comb = sscr[h]
                if diag:
                    dtile = dscr[h]
                    m = jnp.max(jnp.maximum(comb, dtile), axis=-1, keepdims=True)
                else:
                    m = jnp.max(comb, axis=-1, keepdims=True)
                if has_sink:
                    sk = sink_ref[0, h]
                    m = jnp.maximum(m, sk)
                e = jnp.exp(comb - m)
                if diag:
                    ed = jnp.exp(dtile - m)
                    den = jnp.sum(e + ed, axis=-1, keepdims=True)
                else:
                    den = jnp.sum(e, axis=-1, keepdims=True)
                if has_sink:
                    den = den + jnp.exp(sk - m)
                inv = 1.0 / den
                if nb > 1:
                    pscr[h, :, :BLK] = (jnp.where(tri, ed if diag else 0.0, e) * inv).astype(bf16)
                    pscr[h, :, BLK:] = (jnp.where(tri, e, 0.0) * inv).astype(bf16)
                else:
                    pscr[h] = (e * inv).astype(bf16)
                lse_all = jnp.where(lane == h, jnp.broadcast_to(m + jnp.log(den), (BLK, LANES)), lse_all)
        lse_ref[...] = lse_all
        for p in range(w // LANES):
            vcat = _kv_cat(vc_ref, vp_ref, p // share, cache)
            o_ref[:, LANES * p:LANES * (p + 1)] = (_dot(pscr[2 * p], _lane_half(vcat, 0))
                                                   + _dot(pscr[2 * p + 1], _lane_half(vcat, 1))).astype(o_ref.dtype)

    assert max_dist in (BLK - 1, BLK) and w % wk == 0
    share = w // wk
    diag = nb > 1 and max_dist == BLK
    cur, prev, grid = _attn_specs(n, nb, False)
    in_specs = [cur(w), cur(wk)] + ([prev(wk)] if nb > 1 else []) + [cur(wk)] + ([prev(wk)] if nb > 1 else [])
    args = [q, k] + ([k] if nb > 1 else []) + [v] + ([v] if nb > 1 else [])
    if has_sink:
        in_specs = [pl.BlockSpec(memory_space=pltpu.SMEM)] + in_specs
        args = [sink] + args
    return pl.pallas_call(
        body, name=name, grid=grid, in_specs=in_specs,
        out_specs=[cur(w), cur(LANES)],
        out_shape=[jax.ShapeDtypeStruct((n, l, w), o_dtype), jax.ShapeDtypeStruct((n, l, LANES), f32)],
        scratch_shapes=[pltpu.VMEM((w // HD, BLK, BLK), f32), pltpu.VMEM((w // HD, BLK, 2 * BLK if nb > 1 else BLK), bf16),
                        pltpu.VMEM((w // HD if diag else 1, BLK, BLK), f32)],
        compiler_params=_cp(("arbitrary", "arbitrary")),
    )(*args)


def _attn_bwd(q, k, v, do, delta, lse, sink, *, max_dist, name):
    n, l, w = q.shape
    wk = k.shape[-1]
    nb = l // BLK
    has_sink = sink is not None

    def body(*refs):
        sink_ref = dsink_ref = ck = cv = None
        if has_sink:
            sink_ref, refs = refs[0], refs[1:]
        nin = 8 if nb > 1 else 6
        ins, rest = refs[:nin], refs[nin:]
        if has_sink:
            dq_ref, dk_ref, dv_ref, dsink_ref = rest[:4]
            rest = rest[4:]
        else:
            dq_ref, dk_ref, dv_ref = rest[:3]
            rest = rest[3:]
        step = pl.program_id(1)
        if has_sink:
            @pl.when((pl.program_id(0) == 0) & (step == 0))
            def _():
                dsink_ref[...] = jnp.zeros_like(dsink_ref)

        if nb > 1:
            q_ref, kc_ref, kp_ref, vc_ref, vp_ref, do_ref, delta_ref, lse_ref = ins
            ck, cv = rest[:2]

            @pl.when(step == 0)
            def _():
                ck[...] = jnp.zeros_like(ck)
                cv[...] = jnp.zeros_like(cv)

            last = step == nb // _att_units(nb) - 1
            for u in reversed(range(_att_units(nb))):
                rows, before = pl.ds(BLK * u, BLK), pl.ds(BLK * (u - 1), BLK)
                unit(q_ref.at[rows, :], kc_ref.at[rows, :], kp_ref if u == 0 else kc_ref.at[before, :],
                     vc_ref.at[rows, :], vp_ref if u == 0 else vc_ref.at[before, :], do_ref.at[rows, :],
                     delta_ref.at[rows, :], lse_ref.at[rows, :], dq_ref.at[rows, :], dk_ref.at[rows, :], dv_ref.at[rows, :],
                     jnp.logical_not(last) if u == 0 else True, sink_ref, dsink_ref, ck, cv, *rest[2:])
        else:
            q_ref, kc_ref, vc_ref, do_ref, delta_ref, lse_ref = ins
            for u in range(_att_units(nb)):
                unit(q_ref.at[u], kc_ref.at[u], None, vc_ref.at[u], None, do_ref.at[u], delta_ref.at[u], lse_ref.at[u],
                     dq_ref.at[u], dk_ref.at[u], dv_ref.at[u], None, sink_ref, dsink_ref, None, None, *rest)

    def unit(q_ref, kc_ref, kp_ref, vc_ref, vp_ref, do_ref, delta_ref, lse_ref, dq_ref, dk_ref, dv_ref, has_prev,
             sink_ref, dsink_ref, ck, cv, sscr, dpscr, pscr, dsscr, dscr=None, ddscr=None):
        lane = lax.broadcasted_iota(jnp.int32, (BLK, LANES), 1)
        qi = lax.broadcasted_iota(jnp.int32, (BLK, BLK), 0)
        kj = lax.broadcasted_iota(jnp.int32, (BLK, BLK), 1)
        tri = kj <= qi
        eye = kj == qi
        cache = {}
        kp, vp = kp_ref, vp_ref
        for p in range(w // LANES):
            sl = slice(LANES * p, LANES * (p + 1))
            qpair, dopair = q_ref[:, sl], do_ref[:, sl]
            kcat, vcat = _kv_cat(kc_ref, kp, p // share, cache), _kv_cat(vc_ref, vp, p // share, cache)
            for hh in range(2):
                h = 2 * p + hh
                s = _dot_nt(_lane_half(qpair, hh), kcat)
                dp = _dot_nt(_lane_half(dopair, hh), vcat)
                if nb > 1:
                    sp = s[:, :BLK] if has_prev is True else jnp.where(has_prev, s[:, :BLK], NEG)
                    sscr[h] = jnp.where(tri, s[:, BLK:], sp)
                    dpscr[h] = jnp.where(tri, dp[:, BLK:], dp[:, :BLK])
                    if diag:
                        dscr[h] = jnp.where(eye, sp, NEG)
                        ddscr[h] = dp[:, :BLK]
                else:
                    sscr[h] = jnp.where(tri, s, NEG)
                    dpscr[h] = dp
        for p in range(w // LANES):
            for hh in range(2):
                h = 2 * p + hh
                lse_b = jnp.broadcast_to(lse_ref[:, h:h + 1], (BLK, BLK))
                delta = jnp.broadcast_to(delta_ref[:, h:h + 1], (BLK, BLK))
                pr = jnp.exp(sscr[h] - lse_b)
                ds = pr * (dpscr[h] - delta)
                if nb > 1:
                    if diag:
                        prd = jnp.exp(dscr[h] - lse_b)
                        dsd = prd * (ddscr[h] - delta)
                    else:
                        prd = dsd = 0.0
                    pscr[h, :, :BLK] = jnp.where(tri, prd, pr).astype(bf16)
                    pscr[h, :, BLK:] = jnp.where(tri, pr, 0.0).astype(bf16)
                    dsscr[h, :, :BLK] = jnp.where(tri, dsd, ds).astype(bf16)
                    dsscr[h, :, BLK:] = jnp.where(tri, ds, 0.0).astype(bf16)
                else:
                    pscr[h] = pr.astype(bf16)
                    dsscr[h] = ds.astype(bf16)
                if has_sink:
                    dsk = -jnp.sum(jnp.where(lane == 0, jnp.exp(sink_ref[0, h] - lse_b) * delta, 0.0), keepdims=True)
                    dsink_ref[h:h + 1, :] += jnp.broadcast_to(dsk, (1, LANES))
        for p in range(w // LANES):
            sl = slice(LANES * p, LANES * (p + 1))
            qpair, dopair = q_ref[:, sl], do_ref[:, sl]
            kcat = _kv_cat(kc_ref, kp, p // share, cache)
            dq_ref[:, sl] = _dot(dsscr[2 * p], _lane_half(kcat, 0)) + _dot(dsscr[2 * p + 1], _lane_half(kcat, 1))
            dk_pair = _dot_tn(dsscr[2 * p], _lane_half(qpair, 0)) + _dot_tn(dsscr[2 * p + 1], _lane_half(qpair, 1))
            dv_pair = _dot_tn(pscr[2 * p], _lane_half(dopair, 0)) + _dot_tn(pscr[2 * p + 1], _lane_half(dopair, 1))
            if nb > 1:
                dk_ref[:, sl] = dk_pair[BLK:] + ck[:, sl]
                dv_ref[:, sl] = dv_pair[BLK:] + cv[:, sl]
                ck[:, sl] = dk_pair[:BLK]
                cv[:, sl] = dv_pair[:BLK]
            else:
                dk_ref[:, sl] = dk_pair
                dv_ref[:, sl] = dv_pair

    assert max_dist in (BLK - 1, BLK) and w % wk == 0
    share = w // wk
    diag = nb > 1 and max_dist == BLK
    cur, prev, grid = _attn_specs(n, nb, True)
    in_specs = ([cur(w), cur(wk)] + ([prev(wk)] if nb > 1 else []) + [cur(wk)] + ([prev(wk)] if nb > 1 else [])
                + [cur(w), cur(LANES), cur(LANES)])
    args = [q, k] + ([k] if nb > 1 else []) + [v] + ([v] if nb > 1 else []) + [do, delta, lse]
    out_specs = [cur(w)] * 3
    out_shape = [jax.ShapeDtypeStruct((n, l, w), f32)] * 3
    if has_sink:
        in_specs = [pl.BlockSpec(memory_space=pltpu.SMEM)] + in_specs
        args = [sink] + args
        out_specs.append(pl.BlockSpec((NHEAD, LANES), lambda a, i: (0, 0)))
        out_shape.append(jax.ShapeDtypeStruct((NHEAD, LANES), f32))
    nh = w // HD
    scratch = [pltpu.VMEM((BLK, w), f32), pltpu.VMEM((BLK, w), f32)] if nb > 1 else []
    scratch += [pltpu.VMEM((nh, BLK, BLK), f32)] * 2 + [pltpu.VMEM((nh, BLK, 2 * BLK if nb > 1 else BLK), bf16)] * 2
    if diag:
        scratch += [pltpu.VMEM((nh, BLK, BLK), f32)] * 2
    return pl.pallas_call(
        body, name=name, grid=grid, in_specs=in_specs, out_specs=out_specs, out_shape=out_shape,
        scratch_shapes=scratch, compiler_params=_cp(("arbitrary", "arbitrary")),
    )(*args)


def _split2(x):
    hi = x.astype(bf16)
    return hi, (x - hi.astype(f32)).astype(bf16)


def _heads_to_lanes(xc, e):
    return sum(_dot(t, e) for t in _split2(xc))


def _lanes_to_heads(x, g):
    return sum(_dot(t, g) for t in _split2(x))


HEAD_EXPAND = (np.arange(LANES)[:, None] == np.arange(BW)[None, :] // HD).astype(np.float32)
HEAD_SUM = HEAD_EXPAND.T.copy()


def _branch_weights(l1_ref, l4_ref, l16_ref, scr):
    l4v = _perm_load(l4_ref, scr, 4)
    l16v = _perm_load(l16_ref, scr, 16)
    l1v = l1_ref[...]
    m = jnp.maximum(jnp.maximum(l1v, l4v), l16v)
    e1, e4, e16 = jnp.exp(l1v - m), jnp.exp(l4v - m), jnp.exp(l16v - m)
    z = e1 + e4 + e16
    return e1 / z, e4 / z, e16 / z


def _mix_out(oa, o1, l1, o4, l4, o16, l16, g_mix_a, g_mix_b, w_out, x, mod, g_post):
    def body(oa_ref, o1_ref, l1_ref, o4_ref, l4_ref, o16_ref, l16_ref, ga_ref, gb_ref, w_ref, x_ref, mod_ref, gp_ref, e_ref,
             x1_ref, y_ref, mixed_ref, ob_ref, scr):
        w1, w4, w16 = _branch_weights(l1_ref, l4_ref, l16_ref, scr)
        e = e_ref[...]
        x1w, x4w = _heads_to_lanes(w1, e), _heads_to_lanes(w4, e)
        ob = (x1w * o1_ref[...].astype(f32) + x4w * _perm_load(o4_ref, scr, 4)
              + (1.0 - x1w - x4w) * _perm_load(o16_ref, scr, 16))
        ob_ref[...] = ob
        oan, _ = _rms(oa_ref[...])
        obn, _ = _rms(ob)
        mixed = jnp.concatenate([oan * ga_ref[...], obn * gb_ref[...]], axis=1).astype(bf16)
        mixed_ref[...] = mixed
        y = _dot(mixed, w_ref[...])
        y_ref[...] = y
        yn, _ = _rms(y)
        x1_ref[...] = x_ref[...] + mod_ref[2:3, :] * (yn * gp_ref[...])

    nat = lambda w, dt: jax.ShapeDtypeStruct((BL, SEQ, w), dt)
    return pl.pallas_call(
        body, name="mix_out", grid=(BL, NJ),
        in_specs=[_tok(AQ), _tok(BW), _tok(LANES), _perm_spec(4, BW), _perm_spec(4, LANES), _perm_spec(16, BW),
                  _perm_spec(16, LANES), _full((1, AQ)), _full((1, BW)), _full((D, D)), _tok(D), MOD_SPEC, _full((1, D)),
                  _full((LANES, BW))],
        out_specs=[_tok(D), _tok(D), _tok(D), _tok(BW)],
        out_shape=[nat(D, f32), nat(D, f32), nat(D, bf16), nat(BW, f32)],
        scratch_shapes=[pltpu.VMEM((BW // LANES, TM, LANES), f32)],
        compiler_params=_cp(("arbitrary", "arbitrary")),
    )(oa, o1, l1, o4, l4, o16, l16, g_mix_a, g_mix_b, w_out, x, mod, g_post, jnp.asarray(HEAD_EXPAND, bf16))


def _mlp_up(x1, mod, g_pre, w_up):
    def body(x_ref, mod_ref, g_ref, w_ref, h_ref, u_ref, a_ref):
        xn, _ = _rms(x_ref[...])
        h = (xn * g_ref[...]) * (1.0 + mod_ref[4:5, :]) + mod_ref[3:4, :]
        hb = h.astype(bf16)
        h_ref[...] = hb
        for s in range(NCHIP):
            u = _dot(hb, w_ref[s])
            u_ref[:, D * s:D * (s + 1)] = u.astype(bf16)
            a_ref[:, D * s:D * (s + 1)] = jnp.square(jnp.maximum(u, 0.0)).astype(bf16)

    nat = lambda w: jax.ShapeDtypeStruct((BL, SEQ, w), bf16)
    return pl.pallas_call(
        body, name="mlp_up", grid=(BL, NJ),
        in_specs=[_tok(D), MOD_SPEC, _full((1, D)), _full((NCHIP, D, D))],
        out_specs=[_tok(D), _tok(DFF), _tok(DFF)], out_shape=[nat(D), nat(DFF), nat(DFF)],
        compiler_params=_cp(("arbitrary", "arbitrary")),
    )(x1, mod, g_pre, w_up)


def _mlp_down(a, w_down, x1, target, mod, g_post):
    def body(a_ref, w_ref, x_ref, t_ref, mod_ref, g_ref, gx_ref, dy_ref, accb_ref, accg_ref):
        _acc_init(accb_ref, accg_ref)
        y2 = _dot(a_ref[...], w_ref[...])
        yn, r = _rms(y2)
        g = g_ref[...]
        gt = mod_ref[5:6, :]
        n2 = yn * g
        err = x_ref[...] + gt * n2 - t_ref[...]
        gout = err * (1.0 / D)
        gx_ref[...] = gout
        dn2 = gout * gt
        dy_ref[...] = _rms_bwd(dn2 * g, yn, r).astype(bf16)
        accb_ref[0:1, :] += _colsum(gout * n2)
        accg_ref[0:1, :] += _colsum(dn2 * yn)
        accg_ref[1:2, :] += jnp.broadcast_to(jnp.sum(err * err, keepdims=True), (1, D))

    return pl.pallas_call(
        body, name="mlp_down", grid=(BL, NJ),
        in_specs=[_tok(DFF), _full((DFF, D)), _tok(D), _tok(D), MOD_SPEC, _full((1, D))],
        out_specs=[_tok(D), _tok(D), ACCB_SPEC, ACCG_SPEC],
        out_shape=[jax.ShapeDtypeStruct((BL, SEQ, D), f32), jax.ShapeDtypeStruct((BL, SEQ, D), bf16)] + ACC_SHAPES,
        compiler_params=_cp(("arbitrary", "arbitrary")),
    )(a, w_down, x1, target, mod, g_post)


def _mlp_bwd(dy2, u, w_down, w_up, x1, gx, mod, g_pre):
    def body(dy_ref, u_ref, wd_hbm, wu_hbm, x_ref, gx_ref, mod_ref, g_ref, du_ref, gx1_ref, accb_ref, accg_ref, wd, wu, sem):
        _acc_init(accb_ref, accg_ref)
        first = (pl.program_id(0) == 0) & (pl.program_id(1) == 0)
        c1 = pltpu.make_async_copy(wd_hbm, wd, sem.at[0])
        c2 = pltpu.make_async_copy(wu_hbm, wu, sem.at[1])

        @pl.when(first)
        def _():
            c1.start()
            c2.start()
            c1.wait()

        dy = dy_ref[...]
        for s in range(NCHIP):
            sl = slice(D * s, D * (s + 1))
            da = _dot_nt(dy, wd[sl, :])
            du_ref[:, sl] = (da * (2.0 * jnp.maximum(u_ref[:, sl].astype(f32), 0.0))).astype(bf16)

        @pl.when(first)
        def _():
            c2.wait()

        dh = jnp.zeros((TM, D), f32)
        for s in range(NCHIP):
            dh = dh + _dot_nt(du_ref[:, D * s:D * (s + 1)], wu[s])
        xn, r = _rms(x_ref[...])
        g = g_ref[...]
        n = xn * g
        dn = dh * (1.0 + mod_ref[4:5, :])
        gx1_ref[...] = gx_ref[...] + _rms_bwd(dn * g, xn, r)
        accb_ref[0:1, :] += _colsum(dh * n)
        accb_ref[1:2, :] += _colsum(dh)
        accg_ref[0:1, :] += _colsum(dn * xn)

    anyspec = pl.BlockSpec(memory_space=pl.ANY)
    return pl.pallas_call(
        body, name="mlp_bwd", grid=(BL, NJ),
        in_specs=[_tok(D), _tok(DFF), anyspec, anyspec, _tok(D), _tok(D), MOD_SPEC, _full((1, D))],
        out_specs=[_tok(DFF), _tok(D), ACCB_SPEC, ACCG_SPEC],
        out_shape=[jax.ShapeDtypeStruct((BL, SEQ, DFF), bf16), jax.ShapeDtypeStruct((BL, SEQ, D), f32)] + ACC_SHAPES,
        scratch_shapes=[pltpu.VMEM((DFF, D), bf16), pltpu.VMEM((NCHIP, D, D), bf16), pltpu.SemaphoreType.DMA((2,))],
        compiler_params=_cp(("arbitrary", "arbitrary")),
    )(dy2, u, w_down, w_up, x1, gx, mod, g_pre)


def _matmul_tn(a, b, *, tn, col_blocked, name, out_dtype=f32):
    t, m = a.shape
    n = b.shape[1]
    tmm = min(m, 1024)
    tk = 2048 if tn <= 1024 else 1024
    nk = t // tk

    def body(a_ref, b_ref, o_ref, acc):
        k = pl.program_id(2)

        @pl.when(k == 0)
        def _():
            acc[...] = jnp.zeros_like(acc)

        acc[...] += _dot_tn(a_ref[...], b_ref[...])

        @pl.when(k == nk - 1)
        def _():
            o_ref[...] = acc[...].astype(out_dtype)

    if col_blocked:
        out_spec = pl.BlockSpec((None, tmm, tn), lambda i, j, k: (j, i, 0))
        out_shape = jax.ShapeDtypeStruct((n // tn, m, tn), out_dtype)
    else:
        out_spec = pl.BlockSpec((tmm, tn), lambda i, j, k: (i, j))
        out_shape = jax.ShapeDtypeStruct((m, n), out_dtype)
    return pl.pallas_call(
        body, name=name, grid=(m // tmm, n // tn, nk),
        in_specs=[pl.BlockSpec((tk, tmm), lambda i, j, k: (k, i)), pl.BlockSpec((tk, tn), lambda i, j, k: (k, j))],
        out_specs=out_spec, out_shape=out_shape, scratch_shapes=[pltpu.VMEM((tmm, tn), f32)],
        compiler_params=_cp(("arbitrary", "arbitrary", "arbitrary")),
    )(a, b)


def _grad_w_in(h, dproj):
    t = h.shape[0]
    tk = 1024
    nk = t // tk
    sw = INW // NCHIP

    def body(a_ref, b_ref, o_ref, acc):
        k = pl.program_id(0)

        @pl.when(k == 0)
        def _():
            acc[...] = jnp.zeros_like(acc)

        acc[...] += _dot_tn(a_ref[...], b_ref[...])

        @pl.when(k == nk - 1)
        def _():
            for s in range(NCHIP):
                o_ref[s] = acc[:, sw * s:sw * (s + 1)].astype(bf16)

    return pl.pallas_call(
        body, name="grad_w_in", grid=(nk,),
        in_specs=[pl.BlockSpec((tk, D), lambda k: (k, 0)), pl.BlockSpec((tk, INW), lambda k: (k, 0))],
        out_specs=pl.BlockSpec((NCHIP, D, sw), lambda k: (0, 0, 0)), out_shape=jax.ShapeDtypeStruct((NCHIP, D, sw), bf16),
        scratch_shapes=[pltpu.VMEM((D, INW), f32)], compiler_params=_cp(("arbitrary",)),
    )(h, dproj)


def _attn_out_bwd(gx1, y, mod, g_post, w_out, oa, ob, g_mix_a, g_mix_b, l1, l4, l16):
    def body(gx_ref, y_ref, mod_ref, gp_ref, w_ref, oa_ref, ob_ref, ga_ref, gb_ref, l1_ref, l4_ref, l16_ref, e_ref, g_ref,
             dy_ref, doa_ref, do1_ref, do4_ref, do16_ref, da_ref, d1_ref, d4_ref, d16_ref, accb_ref, accg_ref, scr):
        _acc_init(accb_ref, accg_ref)
        w1, w4, w16 = _branch_weights(l1_ref, l4_ref, l16_ref, scr)
        e, hs = e_ref[...], g_ref[...]
        gx1v = gx_ref[...]
        yn, ry = _rms(y_ref[...])
        gp = gp_ref[...]
        gt = mod_ref[2:3, :]
        dn1 = gx1v * gt
        dy = _rms_bwd(dn1 * gp, yn, ry).astype(bf16)
        dy_ref[...] = dy
        dmixed = _dot_nt(dy, w_ref[...])
        dma, dmb = dmixed[:, :AQ], dmixed[:, AQ:]
        oa, ob = oa_ref[...], ob_ref[...]
        oan, ra = _rms(oa)
        obn, rb = _rms(ob)
        doa = _rms_bwd(dma * ga_ref[...], oan, ra)
        doa_ref[...] = doa.astype(bf16)
        da_ref[...] = _lanes_to_heads(doa * oa, hs)
        dob = _rms_bwd(dmb * gb_ref[...], obn, rb)
        dd = _lanes_to_heads(dob * ob, hs)
        x1w, x4w = _heads_to_lanes(w1, e), _heads_to_lanes(w4, e)
        do1_ref[...] = (x1w * dob).astype(bf16)
        d1_ref[...] = w1 * dd
        _perm_store(x4w * dob, scr, do4_ref, 4)
        _perm_store(w4 * dd, scr, d4_ref, 4)
        _perm_store((1.0 - x1w - x4w) * dob, scr, do16_ref, 16)
        _perm_store(w16 * dd, scr, d16_ref, 16)
        accb_ref[0:1, :] += _colsum(gx1v * (yn * gp))
        accg_ref[0:1, :] += _colsum(dn1 * yn)
        accg_ref[1:2, :] += jnp.concatenate([_colsum(dma * oan), _colsum(dmb * obn)], axis=1)

    nat = lambda w, dt: jax.ShapeDtypeStruct((BL, SEQ, w), dt)
    return pl.pallas_call(
        body, name="attn_out_bwd", grid=(BL, NJ),
        in_specs=[_tok(D), _tok(D), MOD_SPEC, _full((1, D)), _full((D, D)), _tok(AQ), _tok(BW), _full((1, AQ)), _full((1, BW)),
                  _tok(LANES), _perm_spec(4, LANES), _perm_spec(16, LANES), _full((LANES, BW)), _full((BW, LANES))],
        out_specs=[_tok(D), _tok(AQ), _tok(BW), _perm_spec(4, BW), _perm_spec(16, BW),
                   _tok(LANES), _tok(LANES), _perm_spec(4, LANES), _perm_spec(16, LANES), ACCB_SPEC, ACCG_SPEC],
        out_shape=[nat(D, bf16), nat(AQ, bf16), nat(BW, bf16), jax.ShapeDtypeStruct((BL, 4, SEQ // 4, BW), bf16),
                   jax.ShapeDtypeStruct((BL, 16, SEQ // 16, BW), bf16), nat(LANES, f32), nat(LANES, f32),
                   jax.ShapeDtypeStruct((BL, 4, SEQ // 4, LANES), f32), jax.ShapeDtypeStruct((BL, 16, SEQ // 16, LANES), f32)]
                  + ACC_SHAPES,
        scratch_shapes=[pltpu.VMEM((BW // LANES, TM, LANES), f32)],
        compiler_params=_cp(("arbitrary", "arbitrary")),
    )(gx1, y, mod, g_post, w_out, oa, ob, g_mix_a, g_mix_b, l1, l4, l16, jnp.asarray(HEAD_EXPAND, bf16),
      jnp.asarray(HEAD_SUM, bf16))


def _attn_in_bwd(dqa, dka, dva, d1, d4, d16, tc, ts1, ts2, w_in, x, gx1, mod, g_pre):
    def body(dqa_ref, dka_ref, dva_ref, dq1_ref, dk1_ref, dv1_ref, dq4_ref, dk4_ref, dv4_ref, dq16_ref, dk16_ref, dv16_ref,
             c_ref, s1_ref, s2_ref, w_ref, x_ref, gx_ref, mod_ref, g_ref, dproj_ref, dx_ref, accb_ref, accg_ref, scr):
        _acc_init(accb_ref, accg_ref)
        c, s1, s2 = c_ref[...], s1_ref[...], s2_ref[...]
        tot = lambda r1, r4, r16: r1[...] + _perm_load(r4, scr, 4) + _perm_load(r16, scr, 16)
        dqb = tot(dq1_ref, dq4_ref, dq16_ref)
        dkb = tot(dk1_ref, dk4_ref, dk16_ref)
        dvb = tot(dv1_ref, dv4_ref, dv16_ref)
        dproj = jnp.concatenate([
            _rope_t(dqa_ref[...], c, s1, s2) * QSCALE, _rope_t(_per_kv_head(dka_ref[...]), c, s1, s2),
            _per_kv_head(dva_ref[...]),
            _rope_t(dqb, c, s1, s2) * QSCALE, _rope_t(dkb, c, s1, s2), dvb], axis=1).astype(bf16)
        dproj_ref[...] = dproj
        dh = _dot_nt(dproj, w_ref[...])
        xn, r = _rms(x_ref[...])
        g = g_ref[...]
        dn = dh * (1.0 + mod_ref[1:2, :])
        dx_ref[...] = gx_ref[...] + _rms_bwd(dn * g, xn, r)
        accb_ref[0:1, :] += _colsum(dh * (xn * g))
        accb_ref[1:2, :] += _colsum(dh)
        accg_ref[0:1, :] += _colsum(dn * xn)

    return pl.pallas_call(
        body, name="attn_in_bwd", grid=(BL, NJ),
        in_specs=[_tok(AQ), _tok(AQ), _tok(AQ)] + [_tok(BW)] * 3 + [_perm_spec(4, BW)] * 3 + [_perm_spec(16, BW)] * 3
                 + [_tok(LANES)] * 3 + [_full((D, INW)), _tok(D), _tok(D), MOD_SPEC, _full((1, D))],
        out_specs=[_tok(INW), _tok(D), ACCB_SPEC, ACCG_SPEC],
        out_shape=[jax.ShapeDtypeStruct((BL, SEQ, INW), bf16), jax.ShapeDtypeStruct((BL, SEQ, D), f32)] + ACC_SHAPES,
        scratch_shapes=[pltpu.VMEM((BW // LANES, TM, LANES), f32)],
        compiler_params=_cp(("arbitrary", "arbitrary")),
    )(dqa, dka, dva, *d1, *d4, *d16, tc, ts1, ts2, w_in, x, gx1, mod, g_pre)


def _inv_lane():
    inv = np.float32(THETA) ** (-np.arange(0, ROT, 2, dtype=np.float32) / np.float32(ROT))
    lane = np.arange(LANES) % HD
    return jnp.asarray(np.where(lane < ROT, inv[lane % (ROT // 2)], 0.0).astype(np.float32)[None, :])


def _local_step(x, tabs, mod, target, w_in, later_weights, grad_ready, g_attn_pre,
                g_attn_post, sink_a, g_mix_a, g_mix_b, g_mlp_pre, g_mlp_post):
    tc, ts1, ts2 = [t.reshape(BL, SEQ, LANES) for t in tabs]

    (h, qa, ka, va, q1, k1, v1, q4, k4, v4, q16, k16, v16, w_in) = _attn_in(x, mod, g_attn_pre, w_in, tc, ts1, ts2)
    seqs = lambda t: t.reshape(t.shape[0] * t.shape[1], t.shape[2], t.shape[3])
    q4, k4, v4, q16, k16, v16 = [seqs(t) for t in (q4, k4, v4, q16, k16, v16)]
    oa, la = _attn_fwd(qa, ka, va, sink_a, max_dist=BLK - 1, o_dtype=f32, name="attn_a_fwd")
    o1, l1 = _attn_fwd(q1, k1, v1, None, max_dist=BLK, o_dtype=bf16, name="attn_b1_fwd")
    o4, l4 = _attn_fwd(q4, k4, v4, None, max_dist=BLK, o_dtype=bf16, name="attn_b4_fwd")
    o16, l16 = _attn_fwd(q16, k16, v16, None, max_dist=BLK, o_dtype=bf16, name="attn_b16_fwd")
    b4 = lambda t: t.reshape(BL, 4, SEQ // 4, t.shape[-1])
    b16 = lambda t: t.reshape(BL, 16, SEQ // 16, t.shape[-1])
    w_out, mlp_weights, mod = later_weights((oa, o1, o4, o16), mod)
    x1, y, mixed, ob = _mix_out(oa, o1, l1, b4(o4), b4(l4), b16(o16), b16(l16), g_mix_a, g_mix_b, w_out, x, mod, g_attn_post)
    w_up, w_down = mlp_weights((x1,))
    h2, u, a = _mlp_up(x1, mod, g_mlp_pre, w_up)
    gx, dy2, accb_d, accg_d = _mlp_down(a, w_down, x1, target, mod, g_mlp_post)

    flat = lambda t: t.reshape(BL * SEQ, t.shape[-1])
    mod = grad_ready("w_down", _matmul_tn(flat(a), flat(dy2), tn=D, col_blocked=False, name="grad_w_down", out_dtype=bf16), mod)
    du, gx1, accb_m, accg_m = _mlp_bwd(dy2, u, w_down, w_up, x1, gx, mod, g_mlp_pre)
    mod = grad_ready("w_up", _matmul_tn(flat(h2), flat(du), tn=D, col_blocked=True, name="grad_w_up", out_dtype=bf16), mod)

    dy, doa, do1, do4, do16, da, dl1, dl4, dl16, accb_o, accg_o = _attn_out_bwd(
        gx1, y, mod, g_attn_post, w_out, oa, ob, g_mix_a, g_mix_b, l1, b4(l4), b16(l16))
    sink_behind = grad_ready("w_out", _matmul_tn(flat(mixed), flat(dy), tn=D, col_blocked=False, name="grad_w_out",
                                                  out_dtype=bf16), sink_a)
    dqa, dka, dva, dsink = _attn_bwd(qa, ka, va, doa, da, la, sink_behind, max_dist=BLK - 1, name="attn_a_bwd")
    d1 = _attn_bwd(q1, k1, v1, do1, dl1, l1, None, max_dist=BLK, name="attn_b1_bwd")
    d4 = _attn_bwd(q4, k4, v4, seqs(do4), seqs(dl4), l4, None, max_dist=BLK, name="attn_b4_bwd")
    d16 = _attn_bwd(q16, k16, v16, seqs(do16), seqs(dl16), l16, None, max_dist=BLK, name="attn_b16_bwd")
    dproj, grad_x, accb_i, accg_i = _attn_in_bwd(dqa, dka, dva, d1, [b4(t) for t in d4], [b16(t) for t in d16],
                                                 tc, ts1, ts2, w_in, x, gx1, mod, g_attn_pre)
    gw_in = _grad_w_in(flat(h), flat(dproj))
    dsink = grad_ready("w_in", gw_in, dsink)

    return grad_x, (accb_i, accb_o, accb_m, accb_d, accg_i, accg_o, accg_m, accg_d, dsink)


ADAW = NMOD * D // NCHIP


def _pos():
    return lax.axis_index("x"), lax.axis_index("y"), lax.axis_index("c")


def _flip(v, bit):
    return 1 - v if bit else v


def _all_peers(x, y, c):
    return [(_flip(x, k >> 2 & 1), _flip(y, k >> 1 & 1), _flip(c, k & 1)) for k in range(1, NDEV)]


def _other_chips(x, y):
    return [(1 - x, y), (x, 1 - y), (1 - x, 1 - y)]


def _rcopy(src, dst, send, recv, k, dev, k_recv=None):
    return pltpu.make_async_remote_copy(src_ref=src, dst_ref=dst, send_sem=send.at[k],
                                        recv_sem=recv.at[k if k_recv is None else k_recv],
                                        device_id=dev, device_id_type=MESH)


def _gather_small(src, buf, send, recv):
    x, y, c = _pos()
    me = 4 * x + 2 * y + c
    peers = _all_peers(x, y, c)
    sends = [_rcopy(src, buf.at[me], send, recv, k, p) for k, p in enumerate(peers)]
    for cp in sends:
        cp.start()
    for k, (px, py, pc) in enumerate(peers):
        _rcopy(src, buf.at[4 * px + 2 * py + pc], send, recv, k, (px, py, pc)).wait_recv()
    for cp in sends:
        cp.wait_send()
    return me


def _ada_fwd(c_in, w_ada, b_cols):
    def body(c_ref, w_hbm, b_ref, mod_ref, cond_ref, cbuf, mbuf, w_ref, s1, r1, s2, r2, wsem):
        x, y, c = _pos()
        chip = 2 * x + y
        wcopy = pltpu.make_async_copy(w_hbm, w_ref, wsem)
        wcopy.start()
        me = _gather_small(c_ref, cbuf, s1, r1)
        cbuf[me] = c_ref[...]
        for i in range(NDEV):
            cond_ref[BL * i:BL * (i + 1), :] = cbuf[i]
        call = cond_ref[...]
        cond = call / (1.0 + jnp.exp(-call))
        cond_ref[...] = cond
        wcopy.wait()
        mbuf[chip] = _dot(cond.astype(bf16), w_ref[...].astype(bf16)) + b_ref[...]
        chips = _other_chips(x, y)
        sends = [_rcopy(mbuf.at[chip], mbuf.at[chip], s2, r2, j, (px, py, c)) for j, (px, py) in enumerate(chips)]
        for cp in sends:
            cp.start()
        for j, (px, py) in enumerate(chips):
            _rcopy(mbuf.at[chip], mbuf.at[2 * px + py], s2, r2, j, (px, py, c)).wait_recv()
        for cp in sends:
            cp.wait_send()
        row = lax.broadcasted_iota(jnp.int32, (BL * NDEV, ADAW), 0)
        for s in range(NCHIP):
            slab = mbuf[s]
            for j in range(BL):
                mod_ref[j:j + 1, ADAW * s:ADAW * (s + 1)] = jnp.sum(jnp.where(row == BL * me + j, slab, 0.0), axis=0, keepdims=True)

    vm = pl.BlockSpec(memory_space=pltpu.VMEM)
    return pl.pallas_call(
        body, name="ada_fwd", in_specs=[vm, pl.BlockSpec(memory_space=pl.ANY), vm], out_specs=[vm, vm],
        out_shape=[jax.ShapeDtypeStruct((BL, NMOD * D), f32), jax.ShapeDtypeStruct((BL * NDEV, D), f32)],
        scratch_shapes=[pltpu.VMEM((NDEV, BL, D), f32), pltpu.VMEM((NCHIP, BL * NDEV, ADAW), f32),
                        pltpu.VMEM((D, ADAW), f32),
                        pltpu.SemaphoreType.DMA((NDEV - 1,)), pltpu.SemaphoreType.DMA((NDEV - 1,)),
                        pltpu.SemaphoreType.DMA((NCHIP - 1,)), pltpu.SemaphoreType.DMA((NCHIP - 1,)),
                        pltpu.SemaphoreType.DMA],
        compiler_params=pltpu.CompilerParams(vmem_limit_bytes=VMEM_LIMIT),
    )(c_in, w_ada, b_cols)


PAY_ROWS = 4


def _small_pack(accs):
    def body(bi, bo, bm, bd, gi, go, gm, gd, dsink, pay):
        pay[...] = jnp.zeros_like(pay)
        for b in range(BL):
            for k, (ref, r) in enumerate(((bi, 1), (bi, 0), (bo, 0), (bm, 1), (bm, 0), (bd, 0))):
                pay[b:b + 1, D * k:D * (k + 1)] = ref[b, r:r + 1, :]
        for off, ref, r in ((OFF_G_ATTN_PRE, gi, 0), (OFF_G_ATTN_POST, go, 0), (OFF_G_MIX_A, go, 1), (OFF_G_MLP_PRE, gm, 0),
                            (OFF_G_MLP_POST, gd, 0)):
            pay[BL:BL + 1, off:off + D] = ref[r:r + 1, :]
        eye = lax.broadcasted_iota(jnp.int32, (NHEAD, LANES), 0) == lax.broadcasted_iota(jnp.int32, (NHEAD, LANES), 1)
        pay[BL:BL + 1, OFF_SINK:OFF_SINK + LANES] = jnp.sum(jnp.where(eye, dsink[...], 0.0), axis=0, keepdims=True)
        pay[BL:BL + 1, OFF_LOSS:OFF_LOSS + LANES] = gd[1:2, 0:LANES]

    vm = pl.BlockSpec(memory_space=pltpu.VMEM)
    return pl.pallas_call(body, name="small_pack", in_specs=[vm] * 9, out_specs=vm,
                          out_shape=jax.ShapeDtypeStruct((PAY_ROWS, PAYW), f32))(*accs)


def _small_copies(src, land, send, recv):
    x, y, c = _pos()
    me = 4 * x + 2 * y + c
    return [(_rcopy(src, land.at[me], send, recv, k, p), _rcopy(src, land.at[4 * p[0] + 2 * p[1] + p[2]], send, recv, k, p))
            for k, p in enumerate(_all_peers(x, y, c))]


def _small_sum(own, landed, cond_all):
    def body(pay, land, cond_ref, gw_ref, gb_ref, small_ref, pbuf, dall):
        x, y, c = _pos()
        chip = 2 * x + y
        me = 4 * x + 2 * y + c
        for i in range(NDEV):
            @pl.when(me == i)
            def _():
                pbuf[i] = pay[...]

            @pl.when(me != i)
            def _():
                pbuf[i] = land[i]
        small = pbuf[0, BL:BL + 1, :]
        for i in range(1, NDEV):
            small = small + pbuf[i, BL:BL + 1, :]
        small_ref[...] = small
        for i in range(NDEV):
            dall[BL * i:BL * (i + 1), :] = pbuf[i, 0:BL, :]
        gb_ref[...] = jnp.sum(dall[...], axis=0, keepdims=True)
        cols = jnp.zeros((BL * NDEV, ADAW), f32)
        for s in range(NCHIP):
            cols = cols + jnp.where(chip == s, dall[:, ADAW * s:ADAW * (s + 1)], 0.0)
        gw_ref[...] = _dot_tn(cond_ref[...].astype(bf16), cols.astype(bf16))

    vm = pl.BlockSpec(memory_space=pltpu.VMEM)
    return pl.pallas_call(
        body, name="small_sum", in_specs=[vm] * 3, out_specs=[vm] * 3,
        out_shape=[jax.ShapeDtypeStruct((D, ADAW), f32), jax.ShapeDtypeStruct((1, PAYW), f32), jax.ShapeDtypeStruct((1, PAYW), f32)],
        scratch_shapes=[pltpu.VMEM((NDEV, PAY_ROWS, PAYW), f32), pltpu.VMEM((BL * NDEV, PAYW), f32)],
        compiler_params=pltpu.CompilerParams(vmem_limit_bytes=VMEM_LIMIT),
    )(own, landed, cond_all)


def _half(ref, c):
    r2 = ref.shape[0] // 2
    return ref.at[pl.ds(c * r2 if isinstance(c, int) else pl.multiple_of(c * r2, 16), r2), :]


HBM_SPEC = pl.BlockSpec(memory_space=pltpu.HBM)
SEM_SPEC = pl.BlockSpec(memory_space=pltpu.SEMAPHORE)
EFFECT = pltpu.SideEffectType.DATAFLOW_SIDE_EFFECTING
NLINK = NCHIP - 1


def _in_hbm(a):
    return pltpu.with_memory_space_constraint(a, pltpu.HBM)


NSEM = 8


def _split_start(name, srcs, land_shapes, builds, carry, after=(), lands=None):
    n = len(srcs)
    na, nc = len(after), len(carry)

    def body(*refs):
        src, land = refs[:n], refs[n:2 * n]
        kept = refs[2 * n + na:2 * n + na + nc]
        outs = refs[2 * n + na + nc:]
        send, recv, passed = outs[:n], outs[n:2 * n], outs[4 * n:]
        for t in range(n):
            for out_cp, _ in builds[t](src[t], land[t], send[t], recv[t]):
                out_cp.start()
        for a, b in zip(kept, passed):
            b[...] = a[...]

    if lands is None:
        lands = [lax.empty(s.shape, s.dtype) for s in land_shapes]
    lands = [_in_hbm(a) for a in lands]
    sems = [pltpu.SemaphoreType.DMA((NSEM,))] * (2 * n)
    thru = [pltpu.HBM(a.shape, a.dtype) for a in list(srcs) + lands]
    vm = pl.BlockSpec(memory_space=pltpu.VMEM)
    res = pl.pallas_call(
        body, name=name, out_shape=sems + thru + [jax.ShapeDtypeStruct(a.shape, a.dtype) for a in carry],
        in_specs=[HBM_SPEC] * (2 * n) + [pl.BlockSpec(memory_space=pl.ANY)] * na + [vm] * nc,
        out_specs=[SEM_SPEC] * (2 * n) + [HBM_SPEC] * (2 * n) + [vm] * nc,
        input_output_aliases={i: 2 * n + i for i in range(2 * n)},
        compiler_params=pltpu.CompilerParams(has_side_effects=EFFECT),
    )(*[_in_hbm(a) for a in srcs], *lands, *after, *carry)
    flight = [(res[2 * n + t], res[3 * n + t], res[t], res[n + t]) for t in range(n)]
    return flight, list(res[4 * n:])


def _split_wait(name, flight, builds, after):
    m = len(flight)
    na = len(after)

    def body(*refs):
        src, land, send, recv = refs[:m], refs[m:2 * m], refs[2 * m:3 * m], refs[3 * m:4 * m]
        for t in range(m):
            for out_cp, in_cp in builds[t](src[t], land[t], send[t], recv[t]):
                out_cp.wait_send()
                in_cp.wait_recv()

    ops = [f[0] for f in flight] + [f[1] for f in flight] + [f[2] for f in flight] + [f[3] for f in flight]
    res = pl.pallas_call(
        body, name=name, out_shape=[pltpu.HBM(a.shape, a.dtype) for a in ops[:2 * m]],
        in_specs=[HBM_SPEC] * (2 * m) + [SEM_SPEC] * (2 * m) + [pl.BlockSpec(memory_space=pl.ANY)] * na,
        out_specs=[HBM_SPEC] * (2 * m), input_output_aliases={i: i for i in range(2 * m)},
        compiler_params=pltpu.CompilerParams(has_side_effects=EFFECT),
    )(*ops, *after)
    return res[:m], res[m:2 * m]


def _weight_copies(src, land, send, recv):
    x, y, c = _pos()
    chip = 2 * x + y
    return [(_rcopy(_half(src, c), _half(land.at[chip], c), send, recv, j, (px, py, c)),
             _rcopy(_half(src, c), _half(land.at[2 * px + py], c), send, recv, j, (px, py, c)))
            for j, (px, py) in enumerate(_other_chips(x, y))]


NDIRECT = NDEV - 1


def _direct_grad_copies(src, land, send, recv):
    x, y, c = _pos()
    out, arrive = [], []
    for j, (px, py) in enumerate(_other_chips(x, y)):
        for hc in range(2):
            out.append(_rcopy(_half(src.at[2 * px + py], hc), land.at[2 * j + c], send, recv, 2 * j + hc, (px, py, hc),
                              k_recv=2 * j + c))
            arrive.append(_rcopy(_half(src.at[2 * px + py], hc), land.at[2 * j + hc], send, recv, 2 * j + hc, (px, py, hc)))
    own = _rcopy(_half(src.at[2 * x + y], 1 - c), land.at[NDIRECT - 1], send, recv, NDIRECT - 1, (x, y, 1 - c))
    return list(zip(out, arrive)) + [(own, own)]


def _pair_weight_copies(src, land, send, recv):
    x, y, c = _pos()
    sib = (x, y, 1 - c)
    cps = []
    for j, (px, py) in enumerate(_other_chips(x, y)):
        mine, theirs = _half(land.at[2 * px + py], c), _half(land.at[2 * px + py], 1 - c)
        cps.append((_rcopy(mine, mine, send, recv, j, sib), _rcopy(theirs, theirs, send, recv, j, sib)))
    own = _rcopy(src, land.at[2 * x + y], send, recv, NLINK, sib)
    return cps + [(own, own)]


RS_ROWS = 256


def _chip_add(own, landed, pos_arr, name):
    nl, r2, cw = landed.shape
    rows = min(RS_ROWS, r2)
    nr = r2 // rows

    def body(s_ref, h_ref, q_ref, o_ref):
        acc = h_ref[...].astype(f32)
        for j in range(nl):
            acc = acc + q_ref[j].astype(f32)
        o_ref[...] = acc

    gs = pltpu.PrefetchScalarGridSpec(
        num_scalar_prefetch=1, grid=(nr,),
        in_specs=[pl.BlockSpec((None, rows, cw), lambda j, s: (s[0], s[1] * nr + j, 0)),
                  pl.BlockSpec((nl, rows, cw), lambda j, s: (0, j, 0))],
        out_specs=pl.BlockSpec((rows, cw), lambda j, s: (s[1] * nr + j, 0)))
    return pl.pallas_call(body, name=name, grid_spec=gs, out_shape=jax.ShapeDtypeStruct((2 * r2, cw), f32),
                          compiler_params=_cp(("arbitrary",)))(pos_arr, own, landed)


def _pair_gather_copies(src, land, send, recv):
    x, y, c = _pos()
    sib = (x, y, 1 - c)
    return [(_rcopy(_half(land, c), _half(land, c), send, recv, 0, sib),
             _rcopy(_half(land, 1 - c), _half(land, 1 - c), send, recv, 0, sib))]


def _adamw_math(w, g, m, v):
    m = B1 * m + (1.0 - B1) * g
    v = B2 * v + (1.0 - B2) * jnp.square(g)
    m_hat = m / (1.0 - B1 ** STEP)
    v_hat = v / (1.0 - B2 ** STEP)
    return -LR * (m_hat / (jnp.sqrt(v_hat) + AEPS) + WD * w), m, v


ADAM_BLOCK = 512 * 1024


def _adamw(w, g, m, v, name, after=(), landed=True):
    r, cw = w.shape
    na = len(after)

    def body(w_ref, g_ref, m_ref, v_ref, *rest):
        outs = rest[na:]
        g = g_ref[...]
        if landed:
            outs[0][...] = g
        outs[-3][...], outs[-2][...], outs[-1][...] = _adamw_math(w_ref[...], g, m_ref[...], v_ref[...])

    rows = max(k for k in range(SUBLANES, ADAM_BLOCK // cw + 1, SUBLANES) if r % k == 0)
    spec = pl.BlockSpec((rows, cw), lambda i: (i, 0))
    nout = 4 if landed else 3
    res = pl.pallas_call(body, name=name, grid=(r // rows,), in_specs=[spec] * 4 + [pl.BlockSpec(memory_space=pl.ANY)] * na,
                         out_specs=[spec] * nout, out_shape=[jax.ShapeDtypeStruct((r, cw), f32)] * nout,
                         compiler_params=_cp(("arbitrary",)))(w, g, m, v, *after)
    return list(res) if landed else [g, *res]


SMALL = (("b_ada", None, PAYW), ("g_attn_pre", OFF_G_ATTN_PRE, D), ("g_attn_post", OFF_G_ATTN_POST, D), ("sink_a", OFF_SINK, 8),
         ("g_mix_a", OFF_G_MIX_A, AQ), ("g_mix_b", OFF_G_MIX_B, BW), ("g_mlp_pre", OFF_G_MLP_PRE, D), ("g_mlp_post", OFF_G_MLP_POST, D))


def _adamw_small(small, gb, params):
    n = len(SMALL)

    def body(*refs):
        small_ref, gb_ref = refs[:2]
        wmv = refs[2:2 + 3 * n]
        loss_ref = refs[2 + 3 * n]
        outs = refs[3 + 3 * n:]
        loss_ref[...] = small_ref[:, OFF_LOSS:OFF_LOSS + 1] * (0.5 / D)
        for i, (_, off, width) in enumerate(SMALL):
            g = gb_ref[...] if off is None else small_ref[:, off:off + width]
            w_ref, m_ref, v_ref = wmv[3 * i:3 * i + 3]
            outs[4 * i][...] = g
            outs[4 * i + 1][...], outs[4 * i + 2][...], outs[4 * i + 3][...] = _adamw_math(w_ref[...], g, m_ref[...], v_ref[...])

    vm = pl.BlockSpec(memory_space=pltpu.VMEM)
    out_shape = [jax.ShapeDtypeStruct((1, 1), f32)]
    for _, _, width in SMALL:
        out_shape += [jax.ShapeDtypeStruct((1, width), f32)] * 4
    flat = [a for wmv in params for a in wmv]
    res = pl.pallas_call(body, name="adamw_small", in_specs=[vm] * (2 + 3 * n), out_specs=[vm] * len(out_shape),
                         out_shape=out_shape)(small, gb, *flat)
    return res[0], {name: res[1 + 4 * i:5 + 4 * i] for i, (name, _, _) in enumerate(SMALL)}


def kernel(x, c, positions, w_ada, b_ada, g_attn_pre, g_attn_post, w_in, sink_a, g_mix_a, g_mix_b, w_out, g_mlp_pre, g_mlp_post, w_up, w_down, loss_target, m_w_ada, m_b_ada, m_g_attn_pre, m_g_attn_post, m_w_in, m_sink_a, m_g_mix_a, m_g_mix_b, m_w_out, m_g_mlp_pre, m_g_mlp_post, m_w_up, m_w_down, v_w_ada, v_b_ada, v_g_attn_pre, v_g_attn_post, v_w_in, v_sink_a, v_g_mix_a, v_g_mix_b, v_w_out, v_g_mlp_pre, v_g_mlp_post, v_w_up, v_w_down):
    given = dict(w_ada=w_ada, b_ada=b_ada, g_attn_pre=g_attn_pre, g_attn_post=g_attn_post, w_in=w_in, sink_a=sink_a, g_mix_a=g_mix_a,
                 g_mix_b=g_mix_b, w_out=w_out, g_mlp_pre=g_mlp_pre, g_mlp_post=g_mlp_post, w_up=w_up, w_down=w_down)
    moms = dict(w_ada=(m_w_ada, v_w_ada), b_ada=(m_b_ada, v_b_ada), g_attn_pre=(m_g_attn_pre, v_g_attn_pre),
                g_attn_post=(m_g_attn_post, v_g_attn_post), w_in=(m_w_in, v_w_in), sink_a=(m_sink_a, v_sink_a),
                g_mix_a=(m_g_mix_a, v_g_mix_a), g_mix_b=(m_g_mix_b, v_g_mix_b), w_out=(m_w_out, v_w_out),
                g_mlp_pre=(m_g_mlp_pre, v_g_mlp_pre), g_mlp_post=(m_g_mlp_post, v_g_mlp_post), w_up=(m_w_up, v_w_up),
                w_down=(m_w_down, v_w_down))
    order = ["w_ada", "b_ada", "g_attn_pre", "g_attn_post", "w_in", "sink_a", "g_mix_a", "g_mix_b", "w_out", "g_mlp_pre",
             "g_mlp_post", "w_up", "w_down"]
    xi, yi, ci = _pos()
    chip = 2 * xi + yi

    pos_arr = jnp.stack([chip, ci]).astype(jnp.int32)
    big = ("w_in", "w_out", "w_up", "w_down")

    gathered = [jax.ShapeDtypeStruct((NCHIP,) + given[n].shape[1:], bf16) for n in big]
    flight_in, (c, inv_lane) = _split_start("weights_start_first", [w_in[0].astype(bf16)], gathered[:1], [_weight_copies],
                                            [c, _inv_lane()])
    inv_lane, rest = lax.optimization_barrier((inv_lane, [given[n][0] for n in big[1:]]))
    tabs = _rope_tables(positions.reshape(BL * SEQ, 1), inv_lane)
    c, tabs, rest = lax.optimization_barrier((c, tabs, [w.astype(bf16) for w in rest]))
    b_cols = lax.dynamic_slice(b_ada, (0, chip * ADAW), (1, ADAW))
    mod, cond_all = _ada_fwd(c, w_ada[0], b_cols)

    srcs, lands = _split_wait("weights_wait_first", flight_in, [_weight_copies], (mod,))
    cross, (mod,) = _split_start("weights_pair_start_first", srcs, None, [_pair_weight_copies], [mod], lands=lands)
    flight_rest, (mod,) = _split_start("weights_start_rest", rest, gathered[1:], [_weight_copies] * 3, [mod])
    _, (win_g,) = _split_wait("weights_pair_wait_first", cross, [_pair_weight_copies], (mod,))
    mod = mod.reshape(BL, NMOD, D)

    def later_weights(after, carry):
        srcs, lands = _split_wait("weights_wait_rest", flight_rest, [_weight_copies] * 3, after)
        fl, (carry,) = _split_start("weights_pair_start_rest", srcs, None, [_pair_weight_copies] * 3, [carry], lands=lands)
        _, (wout_g,) = _split_wait("weights_pair_wait_out", fl[:1], [_pair_weight_copies], ())

        def mlp_weights(after):
            _, (wup_g, wdn_g) = _split_wait("weights_pair_wait_mlp", fl[1:], [_pair_weight_copies] * 2, after)
            return wup_g, wdn_g.reshape(DFF, D)

        return wout_g.reshape(D, D), mlp_weights, carry

    waiting, pending = {}, {}

    def send_grads(carry):
        names = list(waiting)
        slabs = [waiting.pop(n) for n in names]
        lands = [jax.ShapeDtypeStruct((NDIRECT, s.shape[1] // 2, s.shape[2]), bf16) for s in slabs]
        fl, (carry,) = _split_start("grad_start_" + names[-1], slabs, lands, [_direct_grad_copies] * len(names), [carry])
        for n, f in zip(names, fl):
            pending[n] = [f]
        return carry

    def grad_ready(name, g, carry):
        waiting[name] = g if g.ndim == 3 else g.reshape(NCHIP, g.shape[0] // NCHIP, g.shape[1])
        return send_grads(carry) if name in ("w_up", "w_out") else carry

    grad_x, accs = _local_step(x, tabs, mod, loss_target, win_g, later_weights, grad_ready,
                               g_attn_pre, g_attn_post, sink_a, g_mix_a, g_mix_b, g_mlp_pre, g_mlp_post)

    grads, out = {}, {}

    def update(n, after=()):
        tr = (lambda a: a.T) if n == "w_in" else (lambda a: a)
        res = _adamw(tr(given[n][0]), tr(grads[n]), tr(moms[n][0][0]), tr(moms[n][1][0]), "adamw_" + n, after,
                     landed=n != "w_ada")
        out[n] = tuple(tr(a)[None] for a in res)
        return res[3]

    def finish(names, after, first=()):
        fl = sum((pending[n] for n in names), [])
        halves, landed = _split_wait("grad_wait_" + names[0], fl, [_direct_grad_copies] * len(names), after)
        flights, token = [], jnp.zeros((SUBLANES, LANES), f32)
        for h, q, n in zip(halves, landed, names):
            full = _chip_add(h, q, pos_arr, "grad_chip_sum_" + n)
            flights.append(_split_start("grad_gather_start_" + n, [token], None, [_pair_gather_copies], [], lands=[full])[0])
            token = flights[-1][0][0]
        last = [update(n, (token,)) for n in first]
        for n, fl1 in zip(names, flights):
            after = tuple(last) if last else () if fl1 is flights[-1] else (token,)
            _, (grads[n],) = _split_wait("grad_gather_wait_" + n, fl1, [_pair_gather_copies], after)
            last = [update(n)]
        return last[0]

    fl_small, (cond_all,) = _split_start("small_start", [_small_pack(accs)], [jax.ShapeDtypeStruct((NDEV, PAY_ROWS, PAYW), f32)],
                                         [_small_copies], [cond_all])
    cond_all = send_grads(cond_all)
    last = finish(("w_down", "w_up", "w_out"), (cond_all,))
    (pay,), (landed,) = _split_wait("small_wait", fl_small, [_small_copies], (last,))
    grads["w_ada"], gb, small = _small_sum(pay, landed, cond_all)
    finish(("w_in",), (small,), first=("w_ada",))
    loss, res = _adamw_small(small, gb, [(given[n], moms[n][0], moms[n][1]) for n, _, _ in SMALL])
    for n, _, _ in SMALL:
        out[n] = tuple(res[n])
    return (loss.reshape(()), grad_x, *[out[n][0] for n in order], *[out[n][1] for n in order],
            *[out[n][2] for n in order], *[out[n][3] for n in order])
```

```python
import numpy as np
import jax
import jax.numpy as jnp
from jax import lax
from jax.experimental import pallas as pl
from jax.experimental.pallas import tpu as pltpu

f32 = jnp.float32
bf16 = jnp.bfloat16
MESH = pl.DeviceIdType.MESH

D = 1024
SEQ = 2048
BL = 2
HD = 64
AQ = 512
AKV = 128
BW = 512
INW = 2304
DFF = 4096
NMOD = 6
ROT = 16
THETA = 500000.0
EPS = 1e-6
NEG = -1e30
BLK = 128
TM = 512
NJ = SEQ // TM
LANES = 128
SUBLANES = 8
NHEAD = AQ // HD
QSCALE = HD ** -0.5
NCHIP = 4
NDEV = 8
VMEM_LIMIT = 56 << 20

LR, B1, B2, AEPS, WD, STEP = 0.001, 0.9, 0.999, 1e-08, 0.01, 10

OFF_G_ATTN_PRE, OFF_G_ATTN_POST, OFF_G_MIX_A, OFF_G_MIX_B = 0, 1024, 2048, 2560
OFF_G_MLP_PRE, OFF_G_MLP_POST, OFF_SINK, OFF_LOSS = 3072, 4096, 5120, 5248
PAYW = NMOD * D


def _cp(sem=None):
    return pltpu.CompilerParams(dimension_semantics=sem, vmem_limit_bytes=VMEM_LIMIT)


def _dot(a, b):
    return jnp.dot(a, b, preferred_element_type=f32)


def _dot_nt(a, b):
    return lax.dot_general(a, b, (((1,), (1,)), ((), ())), preferred_element_type=f32)


def _dot_tn(a, b):
    return lax.dot_general(a, b, (((0,), (0,)), ((), ())), preferred_element_type=f32)


def _rms(x):
    r = lax.rsqrt(jnp.mean(x * x, axis=-1, keepdims=True) + EPS)
    return x * r, r


def _rms_bwd(dy, y, r):
    return r * (dy - y * jnp.mean(dy * y, axis=-1, keepdims=True))


def _colsum(v):
    return jnp.sum(v, axis=0, keepdims=True)


def _rope(p, c, s1, s2):
    outs = []
    for c0 in range(0, p.shape[1], LANES):
        pc = p[:, c0:c0 + LANES]
        outs.append(pc * c + pltpu.roll(pc, LANES - ROT // 2, 1) * s1 + pltpu.roll(pc, ROT // 2, 1) * s2)
    return outs[0] if len(outs) == 1 else jnp.concatenate(outs, axis=1)


def _rope_t(g, c, s1, s2):
    outs = []
    for c0 in range(0, g.shape[1], LANES):
        gc = g[:, c0:c0 + LANES]
        outs.append(gc * c + pltpu.roll(gc * s1, ROT // 2, 1) + pltpu.roll(gc * s2, LANES - ROT // 2, 1))
    return outs[0] if len(outs) == 1 else jnp.concatenate(outs, axis=1)


def _perm_store(val, scr, out_ref, d):
    nc = val.shape[1] // LANES
    for c in range(nc):
        scr[c] = val[:, LANES * c:LANES * (c + 1)]
    for c in range(nc):
        for r in range(d):
            out_ref[r, :, LANES * c:LANES * (c + 1)] = scr[c, pl.ds(r, TM // d, stride=d), :].astype(out_ref.dtype)


def _perm_load(in_ref, scr, d):
    nc = in_ref.shape[-1] // LANES
    for c in range(nc):
        for r in range(d):
            scr[c, pl.ds(r, TM // d, stride=d), :] = in_ref[r, :, LANES * c:LANES * (c + 1)].astype(f32)
    return jnp.concatenate([scr[c] for c in range(nc)], axis=1)


def _per_query_head(kv):
    r = pltpu.roll(kv, HD, 1)
    lo = lax.broadcasted_iota(jnp.int32, kv.shape, 1) < HD
    return jnp.concatenate([jnp.where(lo, kv, r), jnp.where(lo, r, kv)], axis=1)


def _per_kv_head(g):
    g0, g1 = g[:, :LANES] + g[:, LANES:2 * LANES], g[:, 2 * LANES:3 * LANES] + g[:, 3 * LANES:]
    lo = lax.broadcasted_iota(jnp.int32, g0.shape, 1) < HD
    return jnp.where(lo, g0 + pltpu.roll(g0, HD, 1), g1 + pltpu.roll(g1, HD, 1))


def _tok(w):
    return pl.BlockSpec((None, TM, w), lambda b, j: (b, j, 0))


def _perm_spec(d, w):
    return pl.BlockSpec((None, d, TM // d, w), lambda b, j: (b, 0, j, 0))


def _full(shape):
    n = len(shape)
    return pl.BlockSpec(shape, lambda b, j: (0,) * n)


MOD_SPEC = pl.BlockSpec((None, NMOD, D), lambda b, j: (b, 0, 0))
ACCB_SPEC = pl.BlockSpec((None, SUBLANES, D), lambda b, j: (b, 0, 0))
ACCG_SPEC = pl.BlockSpec((SUBLANES, D), lambda b, j: (0, 0))
ACC_SHAPES = [jax.ShapeDtypeStruct((BL, SUBLANES, D), f32), jax.ShapeDtypeStruct((SUBLANES, D), f32)]


def _acc_init(accb_ref, accg_ref):
    b, j = pl.program_id(0), pl.program_id(1)

    @pl.when(j == 0)
    def _():
        accb_ref[...] = jnp.zeros_like(accb_ref)

    @pl.when((b == 0) & (j == 0))
    def _():
        accg_ref[...] = jnp.zeros_like(accg_ref)


def _rope_tables(pos_col, inv_lane):
    def body(p_ref, inv_ref, c_ref, s1_ref, s2_ref):
        ang = p_ref[...].astype(f32) * inv_ref[...]
        j = lax.broadcasted_iota(jnp.int32, (TM, LANES), 1) % HD
        cs, sn = jnp.cos(ang), jnp.sin(ang)
        c_ref[...] = jnp.where(j < ROT, cs, 1.0)
        s1_ref[...] = jnp.where(j < ROT // 2, -sn, 0.0)
        s2_ref[...] = jnp.where((j >= ROT // 2) & (j < ROT), sn, 0.0)

    n = BL * SEQ // TM
    return pl.pallas_call(
        body, name="rope_tables", grid=(n,),
        in_specs=[pl.BlockSpec((TM, 1), lambda i: (i, 0)), pl.BlockSpec((1, LANES), lambda i: (0, 0))],
        out_specs=[pl.BlockSpec((TM, LANES), lambda i: (i, 0))] * 3,
        out_shape=[jax.ShapeDtypeStruct((BL * SEQ, LANES), f32)] * 3,
    )(pos_col, inv_lane)


def _attn_in(x, mod, g_pre, w_in, tc, ts1, ts2):
    def body(x_ref, mod_ref, g_ref, wg_ref, c_ref, s1_ref, s2_ref,
             h_ref, qa_ref, ka_ref, va_ref, q1_ref, k1_ref, v1_ref, q4_ref, k4_ref, v4_ref, q16_ref, k16_ref, v16_ref,
             w_ref, scr):
        @pl.when((pl.program_id(0) == 0) & (pl.program_id(1) == 0))
        def _():
            w_ref[...] = jnp.concatenate([wg_ref[s] for s in range(NCHIP)], axis=1)

        xn, _ = _rms(x_ref[...])
        h = (xn * g_ref[...]) * (1.0 + mod_ref[1:2, :]) + mod_ref[0:1, :]
        hb = h.astype(bf16)
        h_ref[...] = hb
        proj = _dot(hb, w_ref[...])
        c, s1, s2 = c_ref[...], s1_ref[...], s2_ref[...]
        o1, o2, o3, o4, o5 = AQ, AQ + AKV, AQ + 2 * AKV, AQ + 2 * AKV + BW, AQ + 2 * AKV + 2 * BW
        qa_ref[...] = (_rope(proj[:, :o1], c, s1, s2) * QSCALE).astype(bf16)
        ka_ref[...] = _per_query_head(_rope(proj[:, o1:o2], c, s1, s2)).astype(bf16)
        va_ref[...] = _per_query_head(proj[:, o2:o3]).astype(bf16)
        qb = _rope(proj[:, o3:o4], c, s1, s2) * QSCALE
        kb = _rope(proj[:, o4:o5], c, s1, s2)
        vb = proj[:, o5:]
        for val, r1, r4, r16 in ((qb, q1_ref, q4_ref, q16_ref), (kb, k1_ref, k4_ref, k16_ref), (vb, v1_ref, v4_ref, v16_ref)):
            r1[...] = val.astype(bf16)
            _perm_store(val, scr, r4, 4)
            _perm_store(val, scr, r16, 16)

    nat = lambda w: jax.ShapeDtypeStruct((BL, SEQ, w), bf16)
    p4 = jax.ShapeDtypeStruct((BL, 4, SEQ // 4, BW), bf16)
    p16 = jax.ShapeDtypeStruct((BL, 16, SEQ // 16, BW), bf16)
    return pl.pallas_call(
        body, name="attn_in", grid=(BL, NJ),
        in_specs=[_tok(D), MOD_SPEC, _full((1, D)), _full((NCHIP, D, INW // NCHIP)), _tok(LANES), _tok(LANES), _tok(LANES)],
        out_specs=([_tok(D), _tok(AQ), _tok(2 * AKV), _tok(2 * AKV)] + [_tok(BW)] * 3 + [_perm_spec(4, BW)] * 3 + [_perm_spec(16, BW)] * 3
                   + [_full((D, INW))]),
        out_shape=[nat(D), nat(AQ), nat(2 * AKV), nat(2 * AKV)] + [nat(BW)] * 3 + [p4] * 3 + [p16] * 3
                  + [jax.ShapeDtypeStruct((D, INW), bf16)],
        scratch_shapes=[pltpu.VMEM((BW // LANES, TM, LANES), f32)],
        compiler_params=_cp(("arbitrary", "arbitrary")),
    )(x, mod, g_pre, w_in, tc, ts1, ts2)


def _kv_cat(cur_ref, prev_ref, p, cache):
    key = (id(cur_ref), p)
    if key not in cache:
        sl = slice(LANES * p, LANES * (p + 1))
        cache[key] = cur_ref[:, sl] if prev_ref is None else jnp.concatenate([prev_ref[:, sl], cur_ref[:, sl]], axis=0)
    return cache[key]


def _lane_half(a, hh):
    lo = lax.broadcasted_iota(jnp.int32, a.shape, 1) < HD
    return jnp.where(lo, a, jnp.zeros_like(a)) if hh == 0 else jnp.where(lo, jnp.zeros_like(a), a)


ATT_UNITS = 4


def _att_units(nb):
    return ATT_UNITS if nb == 1 else min(ATT_UNITS, nb)


def _attn_specs(n, nb, descending):
    u = _att_units(nb)
    if nb == 1:
        return (lambda ww: pl.BlockSpec((u, BLK, ww), lambda a, i: (a, 0, 0))), None, (n // u, 1)
    steps = nb // u
    at = (lambda i: steps - 1 - i) if descending else (lambda i: i)
    cur = lambda ww: pl.BlockSpec((None, u * BLK, ww), lambda a, i: (a, at(i), 0))
    prev = lambda ww: pl.BlockSpec((None, BLK, ww), lambda a, i: (a, jnp.maximum(u * at(i) - 1, 0), 0))
    return cur, prev, (n, steps)


def _attn_fwd(q, k, v, sink, *, max_dist, o_dtype, name):
    n, l, w = q.shape
    wk = k.shape[-1]
    nb = l // BLK
    has_sink = sink is not None

    def body(*refs):
        sink_ref = None
        if has_sink:
            sink_ref, refs = refs[0], refs[1:]
        if nb > 1:
            q_ref, kc_ref, kp_ref, vc_ref, vp_ref, o_ref, lse_ref = refs[:7]
            first = pl.program_id(1) == 0
            for u in range(_att_units(nb)):
                rows, before = pl.ds(BLK * u, BLK), pl.ds(BLK * (u - 1), BLK)
                unit(q_ref.at[rows, :], kc_ref.at[rows, :], kp_ref if u == 0 else kc_ref.at[before, :],
                     vc_ref.at[rows, :], vp_ref if u == 0 else vc_ref.at[before, :], o_ref.at[rows, :], lse_ref.at[rows, :],
                     jnp.logical_not(first) if u == 0 else True, sink_ref, *refs[7:])
        else:
            q_ref, kc_ref, vc_ref, o_ref, lse_ref = refs[:5]
            for u in range(_att_units(nb)):
                unit(q_ref.at[u], kc_ref.at[u], None, vc_ref.at[u], None, o_ref.at[u], lse_ref.at[u], None, sink_ref, *refs[5:])

    def unit(q_ref, kc_ref, kp_ref, vc_ref, vp_ref, o_ref, lse_ref, has_prev, sink_ref, sscr, pscr, dscr):
        qi = lax.broadcasted_iota(jnp.int32, (BLK, BLK), 0)
        kj = lax.broadcasted_iota(jnp.int32, (BLK, BLK), 1)
        tri = kj <= qi
        eye = kj == qi
        cache = {}
        for p in range(w // LANES):
            qpair = q_ref[:, LANES * p:LANES * (p + 1)]
            kcat = _kv_cat(kc_ref, kp_ref, p // share, cache)
            for hh in range(2):
                s = _dot_nt(_lane_half(qpair, hh), kcat)
                if nb > 1:
                    sp = s[:, :BLK] if has_prev is True else jnp.where(has_prev, s[:, :BLK], NEG)
                    sscr[2 * p + hh] = jnp.where(tri, s[:, BLK:], sp)
                    if diag:
                        dscr[2 * p + hh] = jnp.where(eye, sp, NEG)
                else:
                    sscr[2 * p + hh] = jnp.where(tri, s, NEG)
        lane = lax.broadcasted_iota(jnp.int32, (BLK, LANES), 1)
        lse_all = jnp.zeros((BLK, LANES), f32)
        for p in range(w // LANES):
            for hh in range(2):
                h = 2 * p + hh
                comb = sscr[h]
                if diag:
                    dtile = dscr[h]
                    m = jnp.max(jnp.maximum(comb, dtile), axis=-1, keepdims=True)
                else:
                    m = jnp.max(comb, axis=-1, keepdims=True)
                if has_sink:
                    sk = sink_ref[0, h]
                    m = jnp.maximum(m, sk)
                e = jnp.exp(comb - m)
                if diag:
                    ed = jnp.exp(dtile - m)
                    den = jnp.sum(e + ed, axis=-1, keepdims=True)
                else:
                    den = jnp.sum(e, axis=-1, keepdims=True)
                if has_sink:
                    den = den + jnp.exp(sk - m)
                inv = 1.0 / den
                if nb > 1:
                    pscr[h, :, :BLK] = (jnp.where(tri, ed if diag else 0.0, e) * inv).astype(bf16)
                    pscr[h, :, BLK:] = (jnp.where(tri, e, 0.0) * inv).astype(bf16)
                else:
                    pscr[h] = (e * inv).astype(bf16)
                lse_all = jnp.where(lane == h, jnp.broadcast_to(m + jnp.log(den), (BLK, LANES)), lse_all)
        lse_ref[...] = lse_all
        for p in range(w // LANES):
            vcat = _kv_cat(vc_ref, vp_ref, p // share, cache)
            o_ref[:, LANES * p:LANES * (p + 1)] = (_dot(pscr[2 * p], _lane_half(vcat, 0))
                                                   + _dot(pscr[2 * p + 1], _lane_half(vcat, 1))).astype(o_ref.dtype)

    assert max_dist in (BLK - 1, BLK) and w % wk == 0
    share = w // wk
    diag = nb > 1 and max_dist == BLK
    cur, prev, grid = _attn_specs(n, nb, False)
    in_specs = [cur(w), cur(wk)] + ([prev(wk)] if nb > 1 else []) + [cur(wk)] + ([prev(wk)] if nb > 1 else [])
    args = [q, k] + ([k] if nb > 1 else []) + [v] + ([v] if nb > 1 else [])
    if has_sink:
        in_specs = [pl.BlockSpec(memory_space=pltpu.SMEM)] + in_specs
        args = [sink] + args
    return pl.pallas_call(
        body, name=name, grid=grid, in_specs=in_specs,
        out_specs=[cur(w), cur(LANES)],
        out_shape=[jax.ShapeDtypeStruct((n, l, w), o_dtype), jax.ShapeDtypeStruct((n, l, LANES), f32)],
        scratch_shapes=[pltpu.VMEM((w // HD, BLK, BLK), f32), pltpu.VMEM((w // HD, BLK, 2 * BLK if nb > 1 else BLK), bf16),
                        pltpu.VMEM((w // HD if diag else 1, BLK, BLK), f32)],
        compiler_params=_cp(("arbitrary", "arbitrary")),
    )(*args)


def _attn_bwd(q, k, v, do, delta, lse, sink, *, max_dist, name):
    n, l, w = q.shape
    wk = k.shape[-1]
    nb = l // BLK
    has_sink = sink is not None

    def body(*refs):
        sink_ref = dsink_ref = ck = cv = None
        if has_sink:
            sink_ref, refs = refs[0], refs[1:]
        nin = 8 if nb > 1 else 6
        ins, rest = refs[:nin], refs[nin:]
        if has_sink:
            dq_ref, dk_ref, dv_ref, dsink_ref = rest[:4]
            rest = rest[4:]
        else:
            dq_ref, dk_ref, dv_ref = rest[:3]
            rest = rest[3:]
        step = pl.program_id(1)
        if has_sink:
            @pl.when((pl.program_id(0) == 0) & (step == 0))
            def _():
                dsink_ref[...] = jnp.zeros_like(dsink_ref)

        if nb > 1:
            q_ref, kc_ref, kp_ref, vc_ref, vp_ref, do_ref, delta_ref, lse_ref = ins
            ck, cv = rest[:2]

            @pl.when(step == 0)
            def _():
                ck[...] = jnp.zeros_like(ck)
                cv[...] = jnp.zeros_like(cv)

            last = step == nb // _att_units(nb) - 1
            for u in reversed(range(_att_units(nb))):
                rows, before = pl.ds(BLK * u, BLK), pl.ds(BLK * (u - 1), BLK)
                unit(q_ref.at[rows, :], kc_ref.at[rows, :], kp_ref if u == 0 else kc_ref.at[before, :],
                     vc_ref.at[rows, :], vp_ref if u == 0 else vc_ref.at[before, :], do_ref.at[rows, :],
                     delta_ref.at[rows, :], lse_ref.at[rows, :], dq_ref.at[rows, :], dk_ref.at[rows, :], dv_ref.at[rows, :],
                     jnp.logical_not(last) if u == 0 else True, sink_ref, dsink_ref, ck, cv, *rest[2:])
        else:
            q_ref, kc_ref, vc_ref, do_ref, delta_ref, lse_ref = ins
            for u in range(_att_units(nb)):
                unit(q_ref.at[u], kc_ref.at[u], None, vc_ref.at[u], None, do_ref.at[u], delta_ref.at[u], lse_ref.at[u],
                     dq_ref.at[u], dk_ref.at[u], dv_ref.at[u], None, sink_ref, dsink_ref, None, None, *rest)

    def unit(q_ref, kc_ref, kp_ref, vc_ref, vp_ref, do_ref, delta_ref, lse_ref, dq_ref, dk_ref, dv_ref, has_prev,
             sink_ref, dsink_ref, ck, cv, sscr, dpscr, pscr, dsscr, dscr=None, ddscr=None):
        lane = lax.broadcasted_iota(jnp.int32, (BLK, LANES), 1)
        qi = lax.broadcasted_iota(jnp.int32, (BLK, BLK), 0)
        kj = lax.broadcasted_iota(jnp.int32, (BLK, BLK), 1)
        tri = kj <= qi
        eye = kj == qi
        cache = {}
        kp, vp = kp_ref, vp_ref
        for p in range(w // LANES):
            sl = slice(LANES * p, LANES * (p + 1))
            qpair, dopair = q_ref[:, sl], do_ref[:, sl]
            kcat, vcat = _kv_cat(kc_ref, kp, p // share, cache), _kv_cat(vc_ref, vp, p // share, cache)
            for hh in range(2):
                h = 2 * p + hh
                s = _dot_nt(_lane_half(qpair, hh), kcat)
                dp = _dot_nt(_lane_half(dopair, hh), vcat)
                if nb > 1:
                    sp = s[:, :BLK] if has_prev is True else jnp.where(has_prev, s[:, :BLK], NEG)
                    sscr[h] = jnp.where(tri, s[:, BLK:], sp)
                    dpscr[h] = jnp.where(tri, dp[:, BLK:], dp[:, :BLK])
                    if diag:
                        dscr[h] = jnp.where(eye, sp, NEG)
                        ddscr[h] = dp[:, :BLK]
                else:
                    sscr[h] = jnp.where(tri, s, NEG)
                    dpscr[h] = dp
        for p in range(w // LANES):
            for hh in range(2):
                h = 2 * p + hh
                lse_b = jnp.broadcast_to(lse_ref[:, h:h + 1], (BLK, BLK))
                delta = jnp.broadcast_to(delta_ref[:, h:h + 1], (BLK, BLK))
                pr = jnp.exp(sscr[h] - lse_b)
                ds = pr * (dpscr[h] - delta)
                if nb > 1:
                    if diag:
                        prd = jnp.exp(dscr[h] - lse_b)
                        dsd = prd * (ddscr[h] - delta)
                    else:
                        prd = dsd = 0.0
                    pscr[h, :, :BLK] = jnp.where(tri, prd, pr).astype(bf16)
                    pscr[h, :, BLK:] = jnp.where(tri, pr, 0.0).astype(bf16)
                    dsscr[h, :, :BLK] = jnp.where(tri, dsd, ds).astype(bf16)
                    dsscr[h, :, BLK:] = jnp.where(tri, ds, 0.0).astype(bf16)
                else:
                    pscr[h] = pr.astype(bf16)
                    dsscr[h] = ds.astype(bf16)
                if has_sink:
                    dsk = -jnp.sum(jnp.where(lane == 0, jnp.exp(sink_ref[0, h] - lse_b) * delta, 0.0), keepdims=True)
                    dsink_ref[h:h + 1, :] += jnp.broadcast_to(dsk, (1, LANES))
        for p in range(w // LANES):
            sl = slice(LANES * p, LANES * (p + 1))
            qpair, dopair = q_ref[:, sl], do_ref[:, sl]
            kcat = _kv_cat(kc_ref, kp, p // share, cache)
            dq_ref[:, sl] = _dot(dsscr[2 * p], _lane_half(kcat, 0)) + _dot(dsscr[2 * p + 1], _lane_half(kcat, 1))
            dk_pair = _dot_tn(dsscr[2 * p], _lane_half(qpair, 0)) + _dot_tn(dsscr[2 * p + 1], _lane_half(qpair, 1))
            dv_pair = _dot_tn(pscr[2 * p], _lane_half(dopair, 0)) + _dot_tn(pscr[2 * p + 1], _lane_half(dopair, 1))
            if nb > 1:
                dk_ref[:, sl] = dk_pair[BLK:] + ck[:, sl]
                dv_ref[:, sl] = dv_pair[BLK:] + cv[:, sl]
                ck[:, sl] = dk_pair[:BLK]
                cv[:, sl] = dv_pair[:BLK]
            else:
                dk_ref[:, sl] = dk_pair
                dv_ref[:, sl] = dv_pair

    assert max_dist in (BLK - 1, BLK) and w % wk == 0
    share = w // wk
    diag = nb > 1 and max_dist == BLK
    cur, prev, grid = _attn_specs(n, nb, True)
    in_specs = ([cur(w), cur(wk)] + ([prev(wk)] if nb > 1 else []) + [cur(wk)] + ([prev(wk)] if nb > 1 else [])
                + [cur(w), cur(LANES), cur(LANES)])
    args = [q, k] + ([k] if nb > 1 else []) + [v] + ([v] if nb > 1 else []) + [do, delta, lse]
    out_specs = [cur(w)] * 3
    out_shape = [jax.ShapeDtypeStruct((n, l, w), f32)] * 3
    if has_sink:
        in_specs = [pl.BlockSpec(memory_space=pltpu.SMEM)] + in_specs
        args = [sink] + args
        out_specs.append(pl.BlockSpec((NHEAD, LANES), lambda a, i: (0, 0)))
        out_shape.append(jax.ShapeDtypeStruct((NHEAD, LANES), f32))
    nh = w // HD
    scratch = [pltpu.VMEM((BLK, w), f32), pltpu.VMEM((BLK, w), f32)] if nb > 1 else []
    scratch += [pltpu.VMEM((nh, BLK, BLK), f32)] * 2 + [pltpu.VMEM((nh, BLK, 2 * BLK if nb > 1 else BLK), bf16)] * 2
    if diag:
        scratch += [pltpu.VMEM((nh, BLK, BLK), f32)] * 2
    return pl.pallas_call(
        body, name=name, grid=grid, in_specs=in_specs, out_specs=out_specs, out_shape=out_shape,
        scratch_shapes=scratch, compiler_params=_cp(("arbitrary", "arbitrary")),
    )(*args)


def _split2(x):
    hi = x.astype(bf16)
    return hi, (x - hi.astype(f32)).astype(bf16)


def _heads_to_lanes(xc, e):
    return sum(_dot(t, e) for t in _split2(xc))


def _lanes_to_heads(x, g):
    return sum(_dot(t, g) for t in _split2(x))


HEAD_EXPAND = (np.arange(LANES)[:, None] == np.arange(BW)[None, :] // HD).astype(np.float32)
HEAD_SUM = HEAD_EXPAND.T.copy()


def _branch_weights(l1_ref, l4_ref, l16_ref, scr):
    l4v = _perm_load(l4_ref, scr, 4)
    l16v = _perm_load(l16_ref, scr, 16)
    l1v = l1_ref[...]
    m = jnp.maximum(jnp.maximum(l1v, l4v), l16v)
    e1, e4, e16 = jnp.exp(l1v - m), jnp.exp(l4v - m), jnp.exp(l16v - m)
    z = e1 + e4 + e16
    return e1 / z, e4 / z, e16 / z


def _mix_out(oa, o1, l1, o4, l4, o16, l16, g_mix_a, g_mix_b, w_out, x, mod, g_post):
    def body(oa_ref, o1_ref, l1_ref, o4_ref, l4_ref, o16_ref, l16_ref, ga_ref, gb_ref, w_ref, x_ref, mod_ref, gp_ref, e_ref,
             x1_ref, y_ref, mixed_ref, ob_ref, scr):
        w1, w4, w16 = _branch_weights(l1_ref, l4_ref, l16_ref, scr)
        e = e_ref[...]
        x1w, x4w = _heads_to_lanes(w1, e), _heads_to_lanes(w4, e)
        ob = (x1w * o1_ref[...].astype(f32) + x4w * _perm_load(o4_ref, scr, 4)
              + (1.0 - x1w - x4w) * _perm_load(o16_ref, scr, 16))
        ob_ref[...] = ob
        oan, _ = _rms(oa_ref[...])
        obn, _ = _rms(ob)
        mixed = jnp.concatenate([oan * ga_ref[...], obn * gb_ref[...]], axis=1).astype(bf16)
        mixed_ref[...] = mixed
        y = _dot(mixed, w_ref[...])
        y_ref[...] = y
        yn, _ = _rms(y)
        x1_ref[...] = x_ref[...] + mod_ref[2:3, :] * (yn * gp_ref[...])

    nat = lambda w, dt: jax.ShapeDtypeStruct((BL, SEQ, w), dt)
    return pl.pallas_call(
        body, name="mix_out", grid=(BL, NJ),
        in_specs=[_tok(AQ), _tok(BW), _tok(LANES), _perm_spec(4, BW), _perm_spec(4, LANES), _perm_spec(16, BW),
                  _perm_spec(16, LANES), _full((1, AQ)), _full((1, BW)), _full((D, D)), _tok(D), MOD_SPEC, _full((1, D)),
                  _full((LANES, BW))],
        out_specs=[_tok(D), _tok(D), _tok(D), _tok(BW)],
        out_shape=[nat(D, f32), nat(D, f32), nat(D, bf16), nat(BW, f32)],
        scratch_shapes=[pltpu.VMEM((BW // LANES, TM, LANES), f32)],
        compiler_params=_cp(("arbitrary", "arbitrary")),
    )(oa, o1, l1, o4, l4, o16, l16, g_mix_a, g_mix_b, w_out, x, mod, g_post, jnp.asarray(HEAD_EXPAND, bf16))


def _mlp_up(x1, mod, g_pre, w_up):
    def body(x_ref, mod_ref, g_ref, w_ref, h_ref, u_ref, a_ref):
        xn, _ = _rms(x_ref[...])
        h = (xn * g_ref[...]) * (1.0 + mod_ref[4:5, :]) + mod_ref[3:4, :]
        hb = h.astype(bf16)
        h_ref[...] = hb
        for s in range(NCHIP):
            u = _dot(hb, w_ref[s])
            u_ref[:, D * s:D * (s + 1)] = u.astype(bf16)
            a_ref[:, D * s:D * (s + 1)] = jnp.square(jnp.maximum(u, 0.0)).astype(bf16)

    nat = lambda w: jax.ShapeDtypeStruct((BL, SEQ, w), bf16)
    return pl.pallas_call(
        body, name="mlp_up", grid=(BL, NJ),
        in_specs=[_tok(D), MOD_SPEC, _full((1, D)), _full((NCHIP, D, D))],
        out_specs=[_tok(D), _tok(DFF), _tok(DFF)], out_shape=[nat(D), nat(DFF), nat(DFF)],
        compiler_params=_cp(("arbitrary", "arbitrary")),
    )(x1, mod, g_pre, w_up)


def _mlp_down(a, w_down, x1, target, mod, g_post):
    def body(a_ref, w_ref, x_ref, t_ref, mod_ref, g_ref, gx_ref, dy_ref, accb_ref, accg_ref):
        _acc_init(accb_ref, accg_ref)
        y2 = _dot(a_ref[...], w_ref[...])
        yn, r = _rms(y2)
        g = g_ref[...]
        gt = mod_ref[5:6, :]
        n2 = yn * g
        err = x_ref[...] + gt * n2 - t_ref[...]
        gout = err * (1.0 / D)
        gx_ref[...] = gout
        dn2 = gout * gt
        dy_ref[...] = _rms_bwd(dn2 * g, yn, r).astype(bf16)
        accb_ref[0:1, :] += _colsum(gout * n2)
        accg_ref[0:1, :] += _colsum(dn2 * yn)
        accg_ref[1:2, :] += jnp.broadcast_to(jnp.sum(err * err, keepdims=True), (1, D))

    return pl.pallas_call(
        body, name="mlp_down", grid=(BL, NJ),
        in_specs=[_tok(DFF), _full((DFF, D)), _tok(D), _tok(D), MOD_SPEC, _full((1, D))],
        out_specs=[_tok(D), _tok(D), ACCB_SPEC, ACCG_SPEC],
        out_shape=[jax.ShapeDtypeStruct((BL, SEQ, D), f32), jax.ShapeDtypeStruct((BL, SEQ, D), bf16)] + ACC_SHAPES,
        compiler_params=_cp(("arbitrary", "arbitrary")),
    )(a, w_down, x1, target, mod, g_post)


def _mlp_bwd(dy2, u, w_down, w_up, x1, gx, mod, g_pre):
    def body(dy_ref, u_ref, wd_hbm, wu_hbm, x_ref, gx_ref, mod_ref, g_ref, du_ref, gx1_ref, accb_ref, accg_ref, wd, wu, sem):
        _acc_init(accb_ref, accg_ref)
        first = (pl.program_id(0) == 0) & (pl.program_id(1) == 0)
        c1 = pltpu.make_async_copy(wd_hbm, wd, sem.at[0])
        c2 = pltpu.make_async_copy(wu_hbm, wu, sem.at[1])

        @pl.when(first)
        def _():
            c1.start()
            c2.start()
            c1.wait()

        dy = dy_ref[...]
        for s in range(NCHIP):
            sl = slice(D * s, D * (s + 1))
            da = _dot_nt(dy, wd[sl, :])
            du_ref[:, sl] = (da * (2.0 * jnp.maximum(u_ref[:, sl].astype(f32), 0.0))).astype(bf16)

        @pl.when(first)
        def _():
            c2.wait()

        dh = jnp.zeros((TM, D), f32)
        for s in range(NCHIP):
            dh = dh + _dot_nt(du_ref[:, D * s:D * (s + 1)], wu[s])
        xn, r = _rms(x_ref[...])
        g = g_ref[...]
        n = xn * g
        dn = dh * (1.0 + mod_ref[4:5, :])
        gx1_ref[...] = gx_ref[...] + _rms_bwd(dn * g, xn, r)
        accb_ref[0:1, :] += _colsum(dh * n)
        accb_ref[1:2, :] += _colsum(dh)
        accg_ref[0:1, :] += _colsum(dn * xn)

    anyspec = pl.BlockSpec(memory_space=pl.ANY)
    return pl.pallas_call(
        body, name="mlp_bwd", grid=(BL, NJ),
        in_specs=[_tok(D), _tok(DFF), anyspec, anyspec, _tok(D), _tok(D), MOD_SPEC, _full((1, D))],
        out_specs=[_tok(DFF), _tok(D), ACCB_SPEC, ACCG_SPEC],
        out_shape=[jax.ShapeDtypeStruct((BL, SEQ, DFF), bf16), jax.ShapeDtypeStruct((BL, SEQ, D), f32)] + ACC_SHAPES,
        scratch_shapes=[pltpu.VMEM((DFF, D), bf16), pltpu.VMEM((NCHIP, D, D), bf16), pltpu.SemaphoreType.DMA((2,))],
        compiler_params=_cp(("arbitrary", "arbitrary")),
    )(dy2, u, w_down, w_up, x1, gx, mod, g_pre)


def _matmul_tn(a, b, *, tn, col_blocked, name, out_dtype=f32):
    t, m = a.shape
    n = b.shape[1]
    tmm = min(m, 1024)
    tk = 2048 if tn <= 1024 else 1024
    nk = t // tk

    def body(a_ref, b_ref, o_ref, acc):
        k = pl.program_id(2)

        @pl.when(k == 0)
        def _():
            acc[...] = jnp.zeros_like(acc)

        acc[...] += _dot_tn(a_ref[...], b_ref[...])

        @pl.when(k == nk - 1)
        def _():
            o_ref[...] = acc[...].astype(out_dtype)

    if col_blocked:
        out_spec = pl.BlockSpec((None, tmm, tn), lambda i, j, k: (j, i, 0))
        out_shape = jax.ShapeDtypeStruct((n // tn, m, tn), out_dtype)
    else:
        out_spec = pl.BlockSpec((tmm, tn), lambda i, j, k: (i, j))
        out_shape = jax.ShapeDtypeStruct((m, n), out_dtype)
    return pl.pallas_call(
        body, name=name, grid=(m // tmm, n // tn, nk),
        in_specs=[pl.BlockSpec((tk, tmm), lambda i, j, k: (k, i)), pl.BlockSpec((tk, tn), lambda i, j, k: (k, j))],
        out_specs=out_spec, out_shape=out_shape, scratch_shapes=[pltpu.VMEM((tmm, tn), f32)],
        compiler_params=_cp(("arbitrary", "arbitrary", "arbitrary")),
    )(a, b)


def _grad_w_in(h, dproj):
    t = h.shape[0]
    tk = 1024
    nk = t // tk
    sw = INW // NCHIP

    def body(a_ref, b_ref, o_ref, acc):
        k = pl.program_id(0)

        @pl.when(k == 0)
        def _():
            acc[...] = jnp.zeros_like(acc)

        acc[...] += _dot_tn(a_ref[...], b_ref[...])

        @pl.when(k == nk - 1)
        def _():
            for s in range(NCHIP):
                o_ref[s] = acc[:, sw * s:sw * (s + 1)].astype(bf16)

    return pl.pallas_call(
        body, name="grad_w_in", grid=(nk,),
        in_specs=[pl.BlockSpec((tk, D), lambda k: (k, 0)), pl.BlockSpec((tk, INW), lambda k: (k, 0))],
        out_specs=pl.BlockSpec((NCHIP, D, sw), lambda k: (0, 0, 0)), out_shape=jax.ShapeDtypeStruct((NCHIP, D, sw), bf16),
        scratch_shapes=[pltpu.VMEM((D, INW), f32)], compiler_params=_cp(("arbitrary",)),
    )(h, dproj)


def _attn_out_bwd(gx1, y, mod, g_post, w_out, oa, ob, g_mix_a, g_mix_b, l1, l4, l16):
    def body(gx_ref, y_ref, mod_ref, gp_ref, w_ref, oa_ref, ob_ref, ga_ref, gb_ref, l1_ref, l4_ref, l16_ref, e_ref, g_ref,
             dy_ref, doa_ref, do1_ref, do4_ref, do16_ref, da_ref, d1_ref, d4_ref, d16_ref, accb_ref, accg_ref, scr):
        _acc_init(accb_ref, accg_ref)
        w1, w4, w16 = _branch_weights(l1_ref, l4_ref, l16_ref, scr)
        e, hs = e_ref[...], g_ref[...]
        gx1v = gx_ref[...]
        yn, ry = _rms(y_ref[...])
        gp = gp_ref[...]
        gt = mod_ref[2:3, :]
        dn1 = gx1v * gt
        dy = _rms_bwd(dn1 * gp, yn, ry).astype(bf16)
        dy_ref[...] = dy
        dmixed = _dot_nt(dy, w_ref[...])
        dma, dmb = dmixed[:, :AQ], dmixed[:, AQ:]
        oa, ob = oa_ref[...], ob_ref[...]
        oan, ra = _rms(oa)
        obn, rb = _rms(ob)
        doa = _rms_bwd(dma * ga_ref[...], oan, ra)
        doa_ref[...] = doa.astype(bf16)
        da_ref[...] = _lanes_to_heads(doa * oa, hs)
        dob = _rms_bwd(dmb * gb_ref[...], obn, rb)
        dd = _lanes_to_heads(dob * ob, hs)
        x1w, x4w = _heads_to_lanes(w1, e), _heads_to_lanes(w4, e)
        do1_ref[...] = (x1w * dob).astype(bf16)
        d1_ref[...] = w1 * dd
        _perm_store(x4w * dob, scr, do4_ref, 4)
        _perm_store(w4 * dd, scr, d4_ref, 4)
        _perm_store((1.0 - x1w - x4w) * dob, scr, do16_ref, 16)
        _perm_store(w16 * dd, scr, d16_ref, 16)
        accb_ref[0:1, :] += _colsum(gx1v * (yn * gp))
        accg_ref[0:1, :] += _colsum(dn1 * yn)
        accg_ref[1:2, :] += jnp.concatenate([_colsum(dma * oan), _colsum(dmb * obn)], axis=1)

    nat = lambda w, dt: jax.ShapeDtypeStruct((BL, SEQ, w), dt)
    return pl.pallas_call(
        body, name="attn_out_bwd", grid=(BL, NJ),
        in_specs=[_tok(D), _tok(D), MOD_SPEC, _full((1, D)), _full((D, D)), _tok(AQ), _tok(BW), _full((1, AQ)), _full((1, BW)),
                  _tok(LANES), _perm_spec(4, LANES), _perm_spec(16, LANES), _full((LANES, BW)), _full((BW, LANES))],
        out_specs=[_tok(D), _tok(AQ), _tok(BW), _perm_spec(4, BW), _perm_spec(16, BW),
                   _tok(LANES), _tok(LANES), _perm_spec(4, LANES), _perm_spec(16, LANES), ACCB_SPEC, ACCG_SPEC],
        out_shape=[nat(D, bf16), nat(AQ, bf16), nat(BW, bf16), jax.ShapeDtypeStruct((BL, 4, SEQ // 4, BW), bf16),
                   jax.ShapeDtypeStruct((BL, 16, SEQ // 16, BW), bf16), nat(LANES, f32), nat(LANES, f32),
                   jax.ShapeDtypeStruct((BL, 4, SEQ // 4, LANES), f32), jax.ShapeDtypeStruct((BL, 16, SEQ // 16, LANES), f32)]
                  + ACC_SHAPES,
        scratch_shapes=[pltpu.VMEM((BW // LANES, TM, LANES), f32)],
        compiler_params=_cp(("arbitrary", "arbitrary")),
    )(gx1, y, mod, g_post, w_out, oa, ob, g_mix_a, g_mix_b, l1, l4, l16, jnp.asarray(HEAD_EXPAND, bf16),
      jnp.asarray(HEAD_SUM, bf16))


def _attn_in_bwd(dqa, dka, dva, d1, d4, d16, tc, ts1, ts2, w_in, x, gx1, mod, g_pre):
    def body(dqa_ref, dka_ref, dva_ref, dq1_ref, dk1_ref, dv1_ref, dq4_ref, dk4_ref, dv4_ref, dq16_ref, dk16_ref, dv16_ref,
             c_ref, s1_ref, s2_ref, w_ref, x_ref, gx_ref, mod_ref, g_ref, dproj_ref, dx_ref, accb_ref, accg_ref, scr):
        _acc_init(accb_ref, accg_ref)
        c, s1, s2 = c_ref[...], s1_ref[...], s2_ref[...]
        tot = lambda r1, r4, r16: r1[...] + _perm_load(r4, scr, 4) + _perm_load(r16, scr, 16)
        dqb = tot(dq1_ref, dq4_ref, dq16_ref)
        dkb = tot(dk1_ref, dk4_ref, dk16_ref)
        dvb = tot(dv1_ref, dv4_ref, dv16_ref)
        dproj = jnp.concatenate([
            _rope_t(dqa_ref[...], c, s1, s2) * QSCALE, _rope_t(_per_kv_head(dka_ref[...]), c, s1, s2),
            _per_kv_head(dva_ref[...]),
            _rope_t(dqb, c, s1, s2) * QSCALE, _rope_t(dkb, c, s1, s2), dvb], axis=1).astype(bf16)
        dproj_ref[...] = dproj
        dh = _dot_nt(dproj, w_ref[...])
        xn, r = _rms(x_ref[...])
        g = g_ref[...]
        dn = dh * (1.0 + mod_ref[1:2, :])
        dx_ref[...] = gx_ref[...] + _rms_bwd(dn * g, xn, r)
        accb_ref[0:1, :] += _colsum(dh * (xn * g))
        accb_ref[1:2, :] += _colsum(dh)
        accg_ref[0:1, :] += _colsum(dn * xn)

    return pl.pallas_call(
        body, name="attn_in_bwd", grid=(BL, NJ),
        in_specs=[_tok(AQ), _tok(AQ), _tok(AQ)] + [_tok(BW)] * 3 + [_perm_spec(4, BW)] * 3 + [_perm_spec(16, BW)] * 3
                 + [_tok(LANES)] * 3 + [_full((D, INW)), _tok(D), _tok(D), MOD_SPEC, _full((1, D))],
        out_specs=[_tok(INW), _tok(D), ACCB_SPEC, ACCG_SPEC],
        out_shape=[jax.ShapeDtypeStruct((BL, SEQ, INW), bf16), jax.ShapeDtypeStruct((BL, SEQ, D), f32)] + ACC_SHAPES,
        scratch_shapes=[pltpu.VMEM((BW // LANES, TM, LANES), f32)],
        compiler_params=_cp(("arbitrary", "arbitrary")),
    )(dqa, dka, dva, *d1, *d4, *d16, tc, ts1, ts2, w_in, x, gx1, mod, g_pre)


def _inv_lane():
    inv = np.float32(THETA) ** (-np.arange(0, ROT, 2, dtype=np.float32) / np.float32(ROT))
    lane = np.arange(LANES) % HD
    return jnp.asarray(np.where(lane < ROT, inv[lane % (ROT // 2)], 0.0).astype(np.float32)[None, :])


def _local_step(x, tabs, mod, target, w_in, later_weights, grad_ready, g_attn_pre,
                g_attn_post, sink_a, g_mix_a, g_mix_b, g_mlp_pre, g_mlp_post):
    tc, ts1, ts2 = [t.reshape(BL, SEQ, LANES) for t in tabs]

    (h, qa, ka, va, q1, k1, v1, q4, k4, v4, q16, k16, v16, w_in) = _attn_in(x, mod, g_attn_pre, w_in, tc, ts1, ts2)
    seqs = lambda t: t.reshape(t.shape[0] * t.shape[1], t.shape[2], t.shape[3])
    q4, k4, v4, q16, k16, v16 = [seqs(t) for t in (q4, k4, v4, q16, k16, v16)]
    oa, la = _attn_fwd(qa, ka, va, sink_a, max_dist=BLK - 1, o_dtype=f32, name="attn_a_fwd")
    o1, l1 = _attn_fwd(q1, k1, v1, None, max_dist=BLK, o_dtype=bf16, name="attn_b1_fwd")
    o4, l4 = _attn_fwd(q4, k4, v4, None, max_dist=BLK, o_dtype=bf16, name="attn_b4_fwd")
    o16, l16 = _attn_fwd(q16, k16, v16, None, max_dist=BLK, o_dtype=bf16, name="attn_b16_fwd")
    b4 = lambda t: t.reshape(BL, 4, SEQ // 4, t.shape[-1])
    b16 = lambda t: t.reshape(BL, 16, SEQ // 16, t.shape[-1])
    w_out, mlp_weights, mod = later_weights((oa, o1, o4, o16), mod)
    x1, y, mixed, ob = _mix_out(oa, o1, l1, b4(o4), b4(l4), b16(o16), b16(l16), g_mix_a, g_mix_b, w_out, x, mod, g_attn_post)
    w_up, w_down = mlp_weights((x1,))
    h2, u, a = _mlp_up(x1, mod, g_mlp_pre, w_up)
    gx, dy2, accb_d, accg_d = _mlp_down(a, w_down, x1, target, mod, g_mlp_post)

    flat = lambda t: t.reshape(BL * SEQ, t.shape[-1])
    mod = grad_ready("w_down", _matmul_tn(flat(a), flat(dy2), tn=D, col_blocked=False, name="grad_w_down", out_dtype=bf16), mod)
    du, gx1, accb_m, accg_m = _mlp_bwd(dy2, u, w_down, w_up, x1, gx, mod, g_mlp_pre)
    mod = grad_ready("w_up", _matmul_tn(flat(h2), flat(du), tn=D, col_blocked=True, name="grad_w_up", out_dtype=bf16), mod)

    dy, doa, do1, do4, do16, da, dl1, dl4, dl16, accb_o, accg_o = _attn_out_bwd(
        gx1, y, mod, g_attn_post, w_out, oa, ob, g_mix_a, g_mix_b, l1, b4(l4), b16(l16))
    sink_behind = grad_ready("w_out", _matmul_tn(flat(mixed), flat(dy), tn=D, col_blocked=False, name="grad_w_out",
                                                  out_dtype=bf16), sink_a)
    dqa, dka, dva, dsink = _attn_bwd(qa, ka, va, doa, da, la, sink_behind, max_dist=BLK - 1, name="attn_a_bwd")
    d1 = _attn_bwd(q1, k1, v1, do1, dl1, l1, None, max_dist=BLK, name="attn_b1_bwd")
    d4 = _attn_bwd(q4, k4, v4, seqs(do4), seqs(dl4), l4, None, max_dist=BLK, name="attn_b4_bwd")
    d16 = _attn_bwd(q16, k16, v16, seqs(do16), seqs(dl16), l16, None, max_dist=BLK, name="attn_b16_bwd")
    dproj, grad_x, accb_i, accg_i = _attn_in_bwd(dqa, dka, dva, d1, [b4(t) for t in d4], [b16(t) for t in d16],
                                                 tc, ts1, ts2, w_in, x, gx1, mod, g_attn_pre)
    gw_in = _grad_w_in(flat(h), flat(dproj))
    dsink = grad_ready("w_in", gw_in, dsink)

    return grad_x, (accb_i, accb_o, accb_m, accb_d, accg_i, accg_o, accg_m, accg_d, dsink)


ADAW = NMOD * D // NCHIP


def _pos():
    return lax.axis_index("x"), lax.axis_index("y"), lax.axis_index("c")


def _flip(v, bit):
    return 1 - v if bit else v


def _all_peers(x, y, c):
    return [(_flip(x, k >> 2 & 1), _flip(y, k >> 1 & 1), _flip(c, k & 1)) for k in range(1, NDEV)]


def _other_chips(x, y):
    return [(1 - x, y), (x, 1 - y), (1 - x, 1 - y)]


def _rcopy(src, dst, send, recv, k, dev, k_recv=None):
    return pltpu.make_async_remote_copy(src_ref=src, dst_ref=dst, send_sem=send.at[k],
                                        recv_sem=recv.at[k if k_recv is None else k_recv],
                                        device_id=dev, device_id_type=MESH)


def _small_copies(src, land, send, recv):
    x, y, c = _pos()
    me = 4 * x + 2 * y + c
    return [(_rcopy(src, land.at[me], send, recv, k, p), _rcopy(src, land.at[4 * p[0] + 2 * p[1] + p[2]], send, recv, k, p))
            for k, p in enumerate(_all_peers(x, y, c))]


def _ada_fwd(c_in, landed, w_ada, b_cols):
    def body(c_ref, land, w_hbm, b_ref, mod_ref, cond_ref, mbuf, w_ref, s2, r2, wsem):
        x, y, c = _pos()
        chip = 2 * x + y
        me = 4 * x + 2 * y + c
        wcopy = pltpu.make_async_copy(w_hbm, w_ref, wsem)
        wcopy.start()
        for i in range(NDEV):
            @pl.when(me == i)
            def _():
                cond_ref[BL * i:BL * (i + 1), :] = c_ref[...]

            @pl.when(me != i)
            def _():
                cond_ref[BL * i:BL * (i + 1), :] = land[i]
        call = cond_ref[...]
        cond = call / (1.0 + jnp.exp(-call))
        cond_ref[...] = cond
        wcopy.wait()
        mbuf[chip] = _dot(cond.astype(bf16), w_ref[...].astype(bf16)) + b_ref[...]
        chips = _other_chips(x, y)
        sends = [_rcopy(mbuf.at[chip], mbuf.at[chip], s2, r2, j, (px, py, c)) for j, (px, py) in enumerate(chips)]
        for cp in sends:
            cp.start()
        for j, (px, py) in enumerate(chips):
            _rcopy(mbuf.at[chip], mbuf.at[2 * px + py], s2, r2, j, (px, py, c)).wait_recv()
        for cp in sends:
            cp.wait_send()
        row = lax.broadcasted_iota(jnp.int32, (BL * NDEV, ADAW), 0)
        for s in range(NCHIP):
            slab = mbuf[s]
            for j in range(BL):
                mod_ref[j:j + 1, ADAW * s:ADAW * (s + 1)] = jnp.sum(jnp.where(row == BL * me + j, slab, 0.0), axis=0, keepdims=True)

    vm = pl.BlockSpec(memory_space=pltpu.VMEM)
    return pl.pallas_call(
        body, name="ada_fwd", in_specs=[vm, vm, pl.BlockSpec(memory_space=pl.ANY), vm], out_specs=[vm, vm],
        out_shape=[jax.ShapeDtypeStruct((BL, NMOD * D), f32), jax.ShapeDtypeStruct((BL * NDEV, D), f32)],
        scratch_shapes=[pltpu.VMEM((NCHIP, BL * NDEV, ADAW), f32), pltpu.VMEM((D, ADAW), f32),
                        pltpu.SemaphoreType.DMA((NCHIP - 1,)), pltpu.SemaphoreType.DMA((NCHIP - 1,)),
                        pltpu.SemaphoreType.DMA],
        compiler_params=pltpu.CompilerParams(vmem_limit_bytes=VMEM_LIMIT),
    )(c_in, landed, w_ada, b_cols)


PAY_ROWS = 4


def _small_pack(accs):
    def body(bi, bo, bm, bd, gi, go, gm, gd, dsink, pay):
        pay[...] = jnp.zeros_like(pay)
        for b in range(BL):
            for k, (ref, r) in enumerate(((bi, 1), (bi, 0), (bo, 0), (bm, 1), (bm, 0), (bd, 0))):
                pay[b:b + 1, D * k:D * (k + 1)] = ref[b, r:r + 1, :]
        for off, ref, r in ((OFF_G_ATTN_PRE, gi, 0), (OFF_G_ATTN_POST, go, 0), (OFF_G_MIX_A, go, 1), (OFF_G_MLP_PRE, gm, 0),
                            (OFF_G_MLP_POST, gd, 0)):
            pay[BL:BL + 1, off:off + D] = ref[r:r + 1, :]
        eye = lax.broadcasted_iota(jnp.int32, (NHEAD, LANES), 0) == lax.broadcasted_iota(jnp.int32, (NHEAD, LANES), 1)
        pay[BL:BL + 1, OFF_SINK:OFF_SINK + LANES] = jnp.sum(jnp.where(eye, dsink[...], 0.0), axis=0, keepdims=True)
        pay[BL:BL + 1, OFF_LOSS:OFF_LOSS + LANES] = gd[1:2, 0:LANES]

    vm = pl.BlockSpec(memory_space=pltpu.VMEM)
    return pl.pallas_call(body, name="small_pack", in_specs=[vm] * 9, out_specs=vm,
                          out_shape=jax.ShapeDtypeStruct((PAY_ROWS, PAYW), f32))(*accs)


def _small_sum(own, landed, cond_all):
    def body(pay, land, cond_ref, gw_ref, gb_ref, small_ref, pbuf, dall):
        x, y, c = _pos()
        chip = 2 * x + y
        me = 4 * x + 2 * y + c
        for i in range(NDEV):
            @pl.when(me == i)
            def _():
                pbuf[i] = pay[...]

            @pl.when(me != i)
            def _():
                pbuf[i] = land[i]
        small = pbuf[0, BL:BL + 1, :]
        for i in range(1, NDEV):
            small = small + pbuf[i, BL:BL + 1, :]
        small_ref[...] = small
        for i in range(NDEV):
            dall[BL * i:BL * (i + 1), :] = pbuf[i, 0:BL, :]
        gb_ref[...] = jnp.sum(dall[...], axis=0, keepdims=True)
        cols = jnp.zeros((BL * NDEV, ADAW), f32)
        for s in range(NCHIP):
            cols = cols + jnp.where(chip == s, dall[:, ADAW * s:ADAW * (s + 1)], 0.0)
        gw_ref[...] = _dot_tn(cond_ref[...].astype(bf16), cols.astype(bf16))

    vm = pl.BlockSpec(memory_space=pltpu.VMEM)
    return pl.pallas_call(
        body, name="small_sum", in_specs=[vm] * 3, out_specs=[vm] * 3,
        out_shape=[jax.ShapeDtypeStruct((D, ADAW), f32), jax.ShapeDtypeStruct((1, PAYW), f32), jax.ShapeDtypeStruct((1, PAYW), f32)],
        scratch_shapes=[pltpu.VMEM((NDEV, PAY_ROWS, PAYW), f32), pltpu.VMEM((BL * NDEV, PAYW), f32)],
        compiler_params=pltpu.CompilerParams(vmem_limit_bytes=VMEM_LIMIT),
    )(own, landed, cond_all)


def _half(ref, c):
    r2 = ref.shape[0] // 2
    return ref.at[pl.ds(c * r2 if isinstance(c, int) else pl.multiple_of(c * r2, 16), r2), :]


HBM_SPEC = pl.BlockSpec(memory_space=pltpu.HBM)
SEM_SPEC = pl.BlockSpec(memory_space=pltpu.SEMAPHORE)
EFFECT = pltpu.SideEffectType.DATAFLOW_SIDE_EFFECTING
NLINK = NCHIP - 1


def _in_hbm(a):
    return pltpu.with_memory_space_constraint(a, pltpu.HBM)


NSEM = 8


def _split_start(name, srcs, land_shapes, builds, carry, after=(), lands=None):
    n = len(srcs)
    na, nc = len(after), len(carry)

    def body(*refs):
        src, land = refs[:n], refs[n:2 * n]
        kept = refs[2 * n + na:2 * n + na + nc]
        outs = refs[2 * n + na + nc:]
        send, recv, passed = outs[:n], outs[n:2 * n], outs[4 * n:]
        for t in range(n):
            for out_cp, _ in builds[t](src[t], land[t], send[t], recv[t]):
                out_cp.start()
        for a, b in zip(kept, passed):
            b[...] = a[...]

    if lands is None:
        lands = [lax.empty(s.shape, s.dtype) for s in land_shapes]
    lands = [_in_hbm(a) for a in lands]
    sems = [pltpu.SemaphoreType.DMA((NSEM,))] * (2 * n)
    thru = [pltpu.HBM(a.shape, a.dtype) for a in list(srcs) + lands]
    vm = pl.BlockSpec(memory_space=pltpu.VMEM)
    res = pl.pallas_call(
        body, name=name, out_shape=sems + thru + [jax.ShapeDtypeStruct(a.shape, a.dtype) for a in carry],
        in_specs=[HBM_SPEC] * (2 * n) + [pl.BlockSpec(memory_space=pl.ANY)] * na + [vm] * nc,
        out_specs=[SEM_SPEC] * (2 * n) + [HBM_SPEC] * (2 * n) + [vm] * nc,
        input_output_aliases={i: 2 * n + i for i in range(2 * n)},
        compiler_params=pltpu.CompilerParams(has_side_effects=EFFECT),
    )(*[_in_hbm(a) for a in srcs], *lands, *after, *carry)
    flight = [(res[2 * n + t], res[3 * n + t], res[t], res[n + t]) for t in range(n)]
    return flight, list(res[4 * n:])


def _split_wait(name, flight, builds, after):
    m = len(flight)
    na = len(after)

    def body(*refs):
        src, land, send, recv = refs[:m], refs[m:2 * m], refs[2 * m:3 * m], refs[3 * m:4 * m]
        for t in range(m):
            for out_cp, in_cp in builds[t](src[t], land[t], send[t], recv[t]):
                out_cp.wait_send()
                in_cp.wait_recv()

    ops = [f[0] for f in flight] + [f[1] for f in flight] + [f[2] for f in flight] + [f[3] for f in flight]
    res = pl.pallas_call(
        body, name=name, out_shape=[pltpu.HBM(a.shape, a.dtype) for a in ops[:2 * m]],
        in_specs=[HBM_SPEC] * (2 * m) + [SEM_SPEC] * (2 * m) + [pl.BlockSpec(memory_space=pl.ANY)] * na,
        out_specs=[HBM_SPEC] * (2 * m), input_output_aliases={i: i for i in range(2 * m)},
        compiler_params=pltpu.CompilerParams(has_side_effects=EFFECT),
    )(*ops, *after)
    return res[:m], res[m:2 * m]


def _weight_copies(src, land, send, recv):
    x, y, c = _pos()
    chip = 2 * x + y
    return [(_rcopy(_half(src, c), _half(land.at[chip], c), send, recv, j, (px, py, c)),
             _rcopy(_half(src, c), _half(land.at[2 * px + py], c), send, recv, j, (px, py, c)))
            for j, (px, py) in enumerate(_other_chips(x, y))]


NDIRECT = NDEV - 1


def _direct_grad_copies(src, land, send, recv):
    x, y, c = _pos()
    out, arrive = [], []
    for j, (px, py) in enumerate(_other_chips(x, y)):
        for hc in range(2):
            out.append(_rcopy(_half(src.at[2 * px + py], hc), land.at[2 * j + c], send, recv, 2 * j + hc, (px, py, hc),
                              k_recv=2 * j + c))
            arrive.append(_rcopy(_half(src.at[2 * px + py], hc), land.at[2 * j + hc], send, recv, 2 * j + hc, (px, py, hc)))
    own = _rcopy(_half(src.at[2 * x + y], 1 - c), land.at[NDIRECT - 1], send, recv, NDIRECT - 1, (x, y, 1 - c))
    return list(zip(out, arrive)) + [(own, own)]


def _pair_weight_copies(src, land, send, recv):
    x, y, c = _pos()
    sib = (x, y, 1 - c)
    cps = []
    for j, (px, py) in enumerate(_other_chips(x, y)):
        mine, theirs = _half(land.at[2 * px + py], c), _half(land.at[2 * px + py], 1 - c)
        cps.append((_rcopy(mine, mine, send, recv, j, sib), _rcopy(theirs, theirs, send, recv, j, sib)))
    own = _rcopy(src, land.at[2 * x + y], send, recv, NLINK, sib)
    return cps + [(own, own)]


RS_ROWS = 256


def _chip_add(own, landed, pos_arr, name):
    nl, r2, cw = landed.shape
    rows = min(RS_ROWS, r2)
    nr = r2 // rows

    def body(s_ref, h_ref, q_ref, o_ref):
        acc = h_ref[...].astype(f32)
        for j in range(nl):
            acc = acc + q_ref[j].astype(f32)
        o_ref[...] = acc

    gs = pltpu.PrefetchScalarGridSpec(
        num_scalar_prefetch=1, grid=(nr,),
        in_specs=[pl.BlockSpec((None, rows, cw), lambda j, s: (s[0], s[1] * nr + j, 0)),
                  pl.BlockSpec((nl, rows, cw), lambda j, s: (0, j, 0))],
        out_specs=pl.BlockSpec((rows, cw), lambda j, s: (s[1] * nr + j, 0)))
    return pl.pallas_call(body, name=name, grid_spec=gs, out_shape=jax.ShapeDtypeStruct((2 * r2, cw), f32),
                          compiler_params=_cp(("arbitrary",)))(pos_arr, own, landed)


def _pair_gather_copies(src, land, send, recv):
    x, y, c = _pos()
    sib = (x, y, 1 - c)
    return [(_rcopy(_half(land, c), _half(land, c), send, recv, 0, sib),
             _rcopy(_half(land, 1 - c), _half(land, 1 - c), send, recv, 0, sib))]


def _adamw_math(w, g, m, v):
    m = B1 * m + (1.0 - B1) * g
    v = B2 * v + (1.0 - B2) * jnp.square(g)
    m_hat = m / (1.0 - B1 ** STEP)
    v_hat = v / (1.0 - B2 ** STEP)
    return -LR * (m_hat / (jnp.sqrt(v_hat) + AEPS) + WD * w), m, v


ADAM_BLOCK = 512 * 1024


def _adamw(w, g, m, v, name, after=(), landed=True):
    r, cw = w.shape
    na = len(after)

    def body(w_ref, g_ref, m_ref, v_ref, *rest):
        outs = rest[na:]
        g = g_ref[...]
        if landed:
            outs[0][...] = g
        outs[-3][...], outs[-2][...], outs[-1][...] = _adamw_math(w_ref[...], g, m_ref[...], v_ref[...])

    rows = max(k for k in range(SUBLANES, ADAM_BLOCK // cw + 1, SUBLANES) if r % k == 0)
    spec = pl.BlockSpec((rows, cw), lambda i: (i, 0))
    nout = 4 if landed else 3
    res = pl.pallas_call(body, name=name, grid=(r // rows,), in_specs=[spec] * 4 + [pl.BlockSpec(memory_space=pl.ANY)] * na,
                         out_specs=[spec] * nout, out_shape=[jax.ShapeDtypeStruct((r, cw), f32)] * nout,
                         compiler_params=_cp(("arbitrary",)))(w, g, m, v, *after)
    return list(res) if landed else [g, *res]


SMALL = (("b_ada", None, PAYW), ("g_attn_pre", OFF_G_ATTN_PRE, D), ("g_attn_post", OFF_G_ATTN_POST, D), ("sink_a", OFF_SINK, 8),
         ("g_mix_a", OFF_G_MIX_A, AQ), ("g_mix_b", OFF_G_MIX_B, BW), ("g_mlp_pre", OFF_G_MLP_PRE, D), ("g_mlp_post", OFF_G_MLP_POST, D))


def _adamw_small(small, gb, params):
    n = len(SMALL)

    def body(*refs):
        small_ref, gb_ref = refs[:2]
        wmv = refs[2:2 + 3 * n]
        loss_ref = refs[2 + 3 * n]
        outs = refs[3 + 3 * n:]
        loss_ref[...] = small_ref[:, OFF_LOSS:OFF_LOSS + 1] * (0.5 / D)
        for i, (_, off, width) in enumerate(SMALL):
            g = gb_ref[...] if off is None else small_ref[:, off:off + width]
            w_ref, m_ref, v_ref = wmv[3 * i:3 * i + 3]
            outs[4 * i][...] = g
            outs[4 * i + 1][...], outs[4 * i + 2][...], outs[4 * i + 3][...] = _adamw_math(w_ref[...], g, m_ref[...], v_ref[...])

    vm = pl.BlockSpec(memory_space=pltpu.VMEM)
    out_shape = [jax.ShapeDtypeStruct((1, 1), f32)]
    for _, _, width in SMALL:
        out_shape += [jax.ShapeDtypeStruct((1, width), f32)] * 4
    flat = [a for wmv in params for a in wmv]
    res = pl.pallas_call(body, name="adamw_small", in_specs=[vm] * (2 + 3 * n), out_specs=[vm] * len(out_shape),
                         out_shape=out_shape)(small, gb, *flat)
    return res[0], {name: res[1 + 4 * i:5 + 4 * i] for i, (name, _, _) in enumerate(SMALL)}


def kernel(x, c, positions, w_ada, b_ada, g_attn_pre, g_attn_post, w_in, sink_a, g_mix_a, g_mix_b, w_out, g_mlp_pre, g_mlp_post, w_up, w_down, loss_target, m_w_ada, m_b_ada, m_g_attn_pre, m_g_attn_post, m_w_in, m_sink_a, m_g_mix_a, m_g_mix_b, m_w_out, m_g_mlp_pre, m_g_mlp_post, m_w_up, m_w_down, v_w_ada, v_b_ada, v_g_attn_pre, v_g_attn_post, v_w_in, v_sink_a, v_g_mix_a, v_g_mix_b, v_w_out, v_g_mlp_pre, v_g_mlp_post, v_w_up, v_w_down):
    given = dict(w_ada=w_ada, b_ada=b_ada, g_attn_pre=g_attn_pre, g_attn_post=g_attn_post, w_in=w_in, sink_a=sink_a, g_mix_a=g_mix_a,
                 g_mix_b=g_mix_b, w_out=w_out, g_mlp_pre=g_mlp_pre, g_mlp_post=g_mlp_post, w_up=w_up, w_down=w_down)
    moms = dict(w_ada=(m_w_ada, v_w_ada), b_ada=(m_b_ada, v_b_ada), g_attn_pre=(m_g_attn_pre, v_g_attn_pre),
                g_attn_post=(m_g_attn_post, v_g_attn_post), w_in=(m_w_in, v_w_in), sink_a=(m_sink_a, v_sink_a),
                g_mix_a=(m_g_mix_a, v_g_mix_a), g_mix_b=(m_g_mix_b, v_g_mix_b), w_out=(m_w_out, v_w_out),
                g_mlp_pre=(m_g_mlp_pre, v_g_mlp_pre), g_mlp_post=(m_g_mlp_post, v_g_mlp_post), w_up=(m_w_up, v_w_up),
                w_down=(m_w_down, v_w_down))
    order = ["w_ada", "b_ada", "g_attn_pre", "g_attn_post", "w_in", "sink_a", "g_mix_a", "g_mix_b", "w_out", "g_mlp_pre",
             "g_mlp_post", "w_up", "w_down"]
    xi, yi, ci = _pos()
    chip = 2 * xi + yi

    pos_arr = jnp.stack([chip, ci]).astype(jnp.int32)
    big = ("w_in", "w_out", "w_up", "w_down")

    gathered = [jax.ShapeDtypeStruct((NCHIP,) + given[n].shape[1:], bf16) for n in big]
    (flight_c, *flight_in), (inv_lane,) = _split_start(
        "weights_start_first", [c, w_in[0].astype(bf16)], [jax.ShapeDtypeStruct((NDEV, BL, D), f32), gathered[0]],
        [_small_copies, _weight_copies], [_inv_lane()])
    inv_lane, rest = lax.optimization_barrier((inv_lane, [given[n][0] for n in big[1:]]))
    tabs = _rope_tables(positions.reshape(BL * SEQ, 1), inv_lane)
    rest = [w.astype(bf16) for w in rest]
    b_cols = lax.dynamic_slice(b_ada, (0, chip * ADAW), (1, ADAW))
    (c_own,), (c_all,) = _split_wait("cond_wait", [flight_c], [_small_copies], (*tabs, *rest))
    mod, cond_all = _ada_fwd(c_own, c_all, w_ada[0], b_cols)

    srcs, lands = _split_wait("weights_wait_first", flight_in, [_weight_copies], (mod,))
    cross, (mod,) = _split_start("weights_pair_start_first", srcs, None, [_pair_weight_copies], [mod], lands=lands)
    flight_rest, (mod,) = _split_start("weights_start_rest", rest, gathered[1:], [_weight_copies] * 3, [mod])
    _, (win_g,) = _split_wait("weights_pair_wait_first", cross, [_pair_weight_copies], (mod,))
    mod = mod.reshape(BL, NMOD, D)

    def later_weights(after, carry):
        srcs, lands = _split_wait("weights_wait_rest", flight_rest, [_weight_copies] * 3, after)
        fl, (carry,) = _split_start("weights_pair_start_rest", srcs, None, [_pair_weight_copies] * 3, [carry], lands=lands)
        _, (wout_g,) = _split_wait("weights_pair_wait_out", fl[:1], [_pair_weight_copies], ())

        def mlp_weights(after):
            _, (wup_g, wdn_g) = _split_wait("weights_pair_wait_mlp", fl[1:], [_pair_weight_copies] * 2, after)
            return wup_g, wdn_g.reshape(DFF, D)

        return wout_g.reshape(D, D), mlp_weights, carry

    waiting, pending = {}, {}

    def send_grads(carry):
        names = list(waiting)
        slabs = [waiting.pop(n) for n in names]
        lands = [jax.ShapeDtypeStruct((NDIRECT, s.shape[1] // 2, s.shape[2]), bf16) for s in slabs]
        fl, (carry,) = _split_start("grad_start_" + names[-1], slabs, lands, [_direct_grad_copies] * len(names), [carry])
        for n, f in zip(names, fl):
            pending[n] = [f]
        return carry

    def grad_ready(name, g, carry):
        waiting[name] = g if g.ndim == 3 else g.reshape(NCHIP, g.shape[0] // NCHIP, g.shape[1])
        return send_grads(carry) if name in ("w_up", "w_out") else carry

    grad_x, accs = _local_step(x, tabs, mod, loss_target, win_g, later_weights, grad_ready,
                               g_attn_pre, g_attn_post, sink_a, g_mix_a, g_mix_b, g_mlp_pre, g_mlp_post)

    grads, out = {}, {}

    def update(n, after=()):
        tr = (lambda a: a.T) if n == "w_in" else (lambda a: a)
        res = _adamw(tr(given[n][0]), tr(grads[n]), tr(moms[n][0][0]), tr(moms[n][1][0]), "adamw_" + n, after,
                     landed=n != "w_ada")
        out[n] = tuple(tr(a)[None] for a in res)
        return res[3]

    def finish(names, after, first=()):
        fl = sum((pending[n] for n in names), [])
        halves, landed = _split_wait("grad_wait_" + names[0], fl, [_direct_grad_copies] * len(names), after)
        flights, token = [], jnp.zeros((SUBLANES, LANES), f32)
        for h, q, n in zip(halves, landed, names):
            full = _chip_add(h, q, pos_arr, "grad_chip_sum_" + n)
            flights.append(_split_start("grad_gather_start_" + n, [token], None, [_pair_gather_copies], [], lands=[full])[0])
            token = flights[-1][0][0]
        last = [update(n, (token,)) for n in first]
        for n, fl1 in zip(names, flights):
            after = tuple(last) if last else () if fl1 is flights[-1] else (token,)
            _, (grads[n],) = _split_wait("grad_gather_wait_" + n, fl1, [_pair_gather_copies], after)
            last = [update(n)]
        return last[0]

    fl_small, (cond_all,) = _split_start("small_start", [_small_pack(accs)], [jax.ShapeDtypeStruct((NDEV, PAY_ROWS, PAYW), f32)],
                                         [_small_copies], [cond_all])
    cond_all = send_grads(cond_all)
    last = finish(("w_down", "w_up", "w_out"), (cond_all,))
    (pay,), (landed,) = _split_wait("small_wait", fl_small, [_small_copies], (last,))
    grads["w_ada"], gb, small = _small_sum(pay, landed, cond_all)
    finish(("w_in",), (small,), first=("w_ada",))
    loss, res = _adamw_small(small, gb, [(given[n], moms[n][0], moms[n][1]) for n, _, _ in SMALL])
    for n, _, _ in SMALL:
        out[n] = tuple(res[n])
    return (loss.reshape(()), grad_x, *[out[n][0] for n in order], *[out[n][1] for n in order],
            *[out[n][2] for n in order], *[out[n][3] for n in order])
```

```python
import numpy as np
import jax
import jax.numpy as jnp
from jax import lax
from jax.experimental import pallas as pl
from jax.experimental.pallas import tpu as pltpu

f32 = jnp.float32
bf16 = jnp.bfloat16
MESH = pl.DeviceIdType.MESH

D = 1024
SEQ = 2048
BL = 2
HD = 64
AQ = 512
AKV = 128
BW = 512
INW = 2304
DFF = 4096
NMOD = 6
ROT = 16
THETA = 500000.0
EPS = 1e-6
NEG = -1e30
BLK = 128
TM = 512
NJ = SEQ // TM
LANES = 128
SUBLANES = 8
NHEAD = AQ // HD
QSCALE = HD ** -0.5
NCHIP = 4
NDEV = 8
VMEM_LIMIT = 56 << 20

LR, B1, B2, AEPS, WD, STEP = 0.001, 0.9, 0.999, 1e-08, 0.01, 10

OFF_G_ATTN_PRE, OFF_G_ATTN_POST, OFF_G_MIX_A, OFF_G_MIX_B = 0, 1024, 2048, 2560
OFF_G_MLP_PRE, OFF_G_MLP_POST, OFF_SINK, OFF_LOSS = 3072, 4096, 5120, 5248
PAYW = NMOD * D


def _cp(sem=None):
    return pltpu.CompilerParams(dimension_semantics=sem, vmem_limit_bytes=VMEM_LIMIT)


def _dot(a, b):
    return jnp.dot(a, b, preferred_element_type=f32)


def _dot_nt(a, b):
    return lax.dot_general(a, b, (((1,), (1,)), ((), ())), preferred_element_type=f32)


def _dot_tn(a, b):
    return lax.dot_general(a, b, (((0,), (0,)), ((), ())), preferred_element_type=f32)


def _rms(x):
    r = lax.rsqrt(jnp.mean(x * x, axis=-1, keepdims=True) + EPS)
    return x * r, r


def _rms_bwd(dy, y, r):
    return r * (dy - y * jnp.mean(dy * y, axis=-1, keepdims=True))


def _colsum(v):
    return jnp.sum(v, axis=0, keepdims=True)


def _rope(p, c, s1, s2):
    outs = []
    for c0 in range(0, p.shape[1], LANES):
        pc = p[:, c0:c0 + LANES]
        outs.append(pc * c + pltpu.roll(pc, LANES - ROT // 2, 1) * s1 + pltpu.roll(pc, ROT // 2, 1) * s2)
    return outs[0] if len(outs) == 1 else jnp.concatenate(outs, axis=1)


def _rope_t(g, c, s1, s2):
    outs = []
    for c0 in range(0, g.shape[1], LANES):
        gc = g[:, c0:c0 + LANES]
        outs.append(gc * c + pltpu.roll(gc * s1, ROT // 2, 1) + pltpu.roll(gc * s2, LANES - ROT // 2, 1))
    return outs[0] if len(outs) == 1 else jnp.concatenate(outs, axis=1)


def _perm_store(val, scr, out_ref, d):
    nc = val.shape[1] // LANES
    for c in range(nc):
        scr[c] = val[:, LANES * c:LANES * (c + 1)]
    for c in range(nc):
        for r in range(d):
            out_ref[r, :, LANES * c:LANES * (c + 1)] = scr[c, pl.ds(r, TM // d, stride=d), :].astype(out_ref.dtype)


def _perm_load(in_ref, scr, d):
    nc = in_ref.shape[-1] // LANES
    for c in range(nc):
        for r in range(d):
            scr[c, pl.ds(r, TM // d, stride=d), :] = in_ref[r, :, LANES * c:LANES * (c + 1)].astype(f32)
    return jnp.concatenate([scr[c] for c in range(nc)], axis=1)


def _per_query_head(kv):
    r = pltpu.roll(kv, HD, 1)
    lo = lax.broadcasted_iota(jnp.int32, kv.shape, 1) < HD
    return jnp.concatenate([jnp.where(lo, kv, r), jnp.where(lo, r, kv)], axis=1)


def _per_kv_head(g):
    g0, g1 = g[:, :LANES] + g[:, LANES:2 * LANES], g[:, 2 * LANES:3 * LANES] + g[:, 3 * LANES:]
    lo = lax.broadcasted_iota(jnp.int32, g0.shape, 1) < HD
    return jnp.where(lo, g0 + pltpu.roll(g0, HD, 1), g1 + pltpu.roll(g1, HD, 1))


def _tok(w):
    return pl.BlockSpec((None, TM, w), lambda b, j: (b, j, 0))


def _perm_spec(d, w):
    return pl.BlockSpec((None, d, TM // d, w), lambda b, j: (b, 0, j, 0))


def _full(shape):
    n = len(shape)
    return pl.BlockSpec(shape, lambda b, j: (0,) * n)


MOD_SPEC = pl.BlockSpec((None, NMOD, D), lambda b, j: (b, 0, 0))
ACCB_SPEC = pl.BlockSpec((None, SUBLANES, D), lambda b, j: (b, 0, 0))
ACCG_SPEC = pl.BlockSpec((SUBLANES, D), lambda b, j: (0, 0))
ACC_SHAPES = [jax.ShapeDtypeStruct((BL, SUBLANES, D), f32), jax.ShapeDtypeStruct((SUBLANES, D), f32)]


def _acc_init(accb_ref, accg_ref):
    b, j = pl.program_id(0), pl.program_id(1)

    @pl.when(j == 0)
    def _():
        accb_ref[...] = jnp.zeros_like(accb_ref)

    @pl.when((b == 0) & (j == 0))
    def _():
        accg_ref[...] = jnp.zeros_like(accg_ref)


def _rope_tables(pos_col, inv_lane):
    def body(p_ref, inv_ref, c_ref, s1_ref, s2_ref):
        ang = p_ref[...].astype(f32) * inv_ref[...]
        j = lax.broadcasted_iota(jnp.int32, (TM, LANES), 1) % HD
        cs, sn = jnp.cos(ang), jnp.sin(ang)
        c_ref[...] = jnp.where(j < ROT, cs, 1.0)
        s1_ref[...] = jnp.where(j < ROT // 2, -sn, 0.0)
        s2_ref[...] = jnp.where((j >= ROT // 2) & (j < ROT), sn, 0.0)

    n = BL * SEQ // TM
    return pl.pallas_call(
        body, name="rope_tables", grid=(n,),
        in_specs=[pl.BlockSpec((TM, 1), lambda i: (i, 0)), pl.BlockSpec((1, LANES), lambda i: (0, 0))],
        out_specs=[pl.BlockSpec((TM, LANES), lambda i: (i, 0))] * 3,
        out_shape=[jax.ShapeDtypeStruct((BL * SEQ, LANES), f32)] * 3,
    )(pos_col, inv_lane)


def _attn_in(x, mod, g_pre, w_in, tc, ts1, ts2):
    def body(x_ref, mod_ref, g_ref, wg_ref, c_ref, s1_ref, s2_ref,
             h_ref, qa_ref, ka_ref, va_ref, q1_ref, k1_ref, v1_ref, q4_ref, k4_ref, v4_ref, q16_ref, k16_ref, v16_ref,
             w_ref, scr):
        @pl.when((pl.program_id(0) == 0) & (pl.program_id(1) == 0))
        def _():
            w_ref[...] = jnp.concatenate([wg_ref[s] for s in range(NCHIP)], axis=1)

        xn, _ = _rms(x_ref[...])
        h = (xn * g_ref[...]) * (1.0 + mod_ref[1:2, :]) + mod_ref[0:1, :]
        hb = h.astype(bf16)
        h_ref[...] = hb
        proj = _dot(hb, w_ref[...])
        c, s1, s2 = c_ref[...], s1_ref[...], s2_ref[...]
        o1, o2, o3, o4, o5 = AQ, AQ + AKV, AQ + 2 * AKV, AQ + 2 * AKV + BW, AQ + 2 * AKV + 2 * BW
        qa_ref[...] = (_rope(proj[:, :o1], c, s1, s2) * QSCALE).astype(bf16)
        ka_ref[...] = _per_query_head(_rope(proj[:, o1:o2], c, s1, s2)).astype(bf16)
        va_ref[...] = _per_query_head(proj[:, o2:o3]).astype(bf16)
        qb = _rope(proj[:, o3:o4], c, s1, s2) * QSCALE
        kb = _rope(proj[:, o4:o5], c, s1, s2)
        vb = proj[:, o5:]
        for val, r1, r4, r16 in ((qb, q1_ref, q4_ref, q16_ref), (kb, k1_ref, k4_ref, k16_ref), (vb, v1_ref, v4_ref, v16_ref)):
            r1[...] = val.astype(bf16)
            _perm_store(val, scr, r4, 4)
            _perm_store(val, scr, r16, 16)

    nat = lambda w: jax.ShapeDtypeStruct((BL, SEQ, w), bf16)
    p4 = jax.ShapeDtypeStruct((BL, 4, SEQ // 4, BW), bf16)
    p16 = jax.ShapeDtypeStruct((BL, 16, SEQ // 16, BW), bf16)
    return pl.pallas_call(
        body, name="attn_in", grid=(BL, NJ),
        in_specs=[_tok(D), MOD_SPEC, _full((1, D)), _full((NCHIP, D, INW // NCHIP)), _tok(LANES), _tok(LANES), _tok(LANES)],
        out_specs=([_tok(D), _tok(AQ), _tok(2 * AKV), _tok(2 * AKV)] + [_tok(BW)] * 3 + [_perm_spec(4, BW)] * 3 + [_perm_spec(16, BW)] * 3
                   + [_full((D, INW))]),
        out_shape=[nat(D), nat(AQ), nat(2 * AKV), nat(2 * AKV)] + [nat(BW)] * 3 + [p4] * 3 + [p16] * 3
                  + [jax.ShapeDtypeStruct((D, INW), bf16)],
        scratch_shapes=[pltpu.VMEM((BW // LANES, TM, LANES), f32)],
        compiler_params=_cp(("arbitrary", "arbitrary")),
    )(x, mod, g_pre, w_in, tc, ts1, ts2)


def _kv_cat(cur_ref, prev_ref, p, cache):
    key = (id(cur_ref), p)
    if key not in cache:
        sl = slice(LANES * p, LANES * (p + 1))
        cache[key] = cur_ref[:, sl] if prev_ref is None else jnp.concatenate([prev_ref[:, sl], cur_ref[:, sl]], axis=0)
    return cache[key]


def _lane_half(a, hh):
    lo = lax.broadcasted_iota(jnp.int32, a.shape, 1) < HD
    return jnp.where(lo, a, jnp.zeros_like(a)) if hh == 0 else jnp.where(lo, jnp.zeros_like(a), a)


ATT_UNITS = 4


def _att_units(nb):
    return ATT_UNITS if nb == 1 else min(ATT_UNITS, nb)


def _attn_specs(n, nb, descending):
    u = _att_units(nb)
    if nb == 1:
        return (lambda ww: pl.BlockSpec((u, BLK, ww), lambda a, i: (a, 0, 0))), None, (n // u, 1)
    steps = nb // u
    at = (lambda i: steps - 1 - i) if descending else (lambda i: i)
    cur = lambda ww: pl.BlockSpec((None, u * BLK, ww), lambda a, i: (a, at(i), 0))
    prev = lambda ww: pl.BlockSpec((None, BLK, ww), lambda a, i: (a, jnp.maximum(u * at(i) - 1, 0), 0))
    return cur, prev, (n, steps)


def _attn_fwd(q, k, v, sink, *, max_dist, o_dtype, name):
    n, l, w = q.shape
    wk = k.shape[-1]
    nb = l // BLK
    has_sink = sink is not None

    def body(*refs):
        sink_ref = None
        if has_sink:
            sink_ref, refs = refs[0], refs[1:]
        if nb > 1:
            q_ref, kc_ref, kp_ref, vc_ref, vp_ref, o_ref, lse_ref = refs[:7]
            first = pl.program_id(1) == 0
            for u in range(_att_units(nb)):
                rows, before = pl.ds(BLK * u, BLK), pl.ds(BLK * (u - 1), BLK)
                unit(q_ref.at[rows, :], kc_ref.at[rows, :], kp_ref if u == 0 else kc_ref.at[before, :],
                     vc_ref.at[rows, :], vp_ref if u == 0 else vc_ref.at[before, :], o_ref.at[rows, :], lse_ref.at[rows, :],
                     jnp.logical_not(first) if u == 0 else True, sink_ref, *refs[7:])
        else:
            q_ref, kc_ref, vc_ref, o_ref, lse_ref = refs[:5]
            for u in range(_att_units(nb)):
                unit(q_ref.at[u], kc_ref.at[u], None, vc_ref.at[u], None, o_ref.at[u], lse_ref.at[u], None, sink_ref, *refs[5:])

    def unit(q_ref, kc_ref, kp_ref, vc_ref, vp_ref, o_ref, lse_ref, has_prev, sink_ref, sscr, pscr, dscr):
        qi = lax.broadcasted_iota(jnp.int32, (BLK, BLK), 0)
        kj = lax.broadcasted_iota(jnp.int32, (BLK, BLK), 1)
        tri = kj <= qi
        eye = kj == qi
        cache = {}
        for p in range(w // LANES):
            qpair = q_ref[:, LANES * p:LANES * (p + 1)]
            kcat = _kv_cat(kc_ref, kp_ref, p // share, cache)
            for hh in range(2):
                s = _dot_nt(_lane_half(qpair, hh), kcat)
                if nb > 1:
                    sp = s[:, :BLK] if has_prev is True else jnp.where(has_prev, s[:, :BLK], NEG)
                    sscr[2 * p + hh] = jnp.where(tri, s[:, BLK:], sp)
                    if diag:
                        dscr[2 * p + hh] = jnp.where(eye, sp, NEG)
                else:
                    sscr[2 * p + hh] = jnp.where(tri, s, NEG)
        lane = lax.broadcasted_iota(jnp.int32, (BLK, LANES), 1)
        lse_all = jnp.zeros((BLK, LANES), f32)
        for p in range(w // LANES):
            for hh in range(2):
                h = 2 * p + hh
                comb = sscr[h]
                if diag:
                    dtile = dscr[h]
                    m = jnp.max(jnp.maximum(comb, dtile), axis=-1, keepdims=True)
                else:
                    m = jnp.max(comb, axis=-1, keepdims=True)
                if has_sink:
                    sk = sink_ref[0, h]
                    m = jnp.maximum(m, sk)
                e = jnp.exp(comb - m)
                if diag:
                    ed = jnp.exp(dtile - m)
                    den = jnp.sum(e + ed, axis=-1, keepdims=True)
                else:
                    den = jnp.sum(e, axis=-1, keepdims=True)
                if has_sink:
                    den = den + jnp.exp(sk - m)
                inv = 1.0 / den
                if nb > 1:
                    pscr[h, :, :BLK] = (jnp.where(tri, ed if diag else 0.0, e) * inv).astype(bf16)
                    pscr[h, :, BLK:] = (jnp.where(tri, e, 0.0) * inv).astype(bf16)
                else:
                    pscr[h] = (e * inv).astype(bf16)
                lse_all = jnp.where(lane == h, jnp.broadcast_to(m + jnp.log(den), (BLK, LANES)), lse_all)
        lse_ref[...] = lse_all
        for p in range(w // LANES):
            vcat = _kv_cat(vc_ref, vp_ref, p // share, cache)
            o_ref[:, LANES * p:LANES * (p + 1)] = (_dot(pscr[2 * p], _lane_half(vcat, 0))
                                                   + _dot(pscr[2 * p + 1], _lane_half(vcat, 1))).astype(o_ref.dtype)

    assert max_dist in (BLK - 1, BLK) and w % wk == 0
    share = w // wk
    diag = nb > 1 and max_dist == BLK
    cur, prev, grid = _attn_specs(n, nb, False)
    in_specs = [cur(w), cur(wk)] + ([prev(wk)] if nb > 1 else []) + [cur(wk)] + ([prev(wk)] if nb > 1 else [])
    args = [q, k] + ([k] if nb > 1 else []) + [v] + ([v] if nb > 1 else [])
    if has_sink:
        in_specs = [pl.BlockSpec(memory_space=pltpu.SMEM)] + in_specs
        args = [sink] + args
    return pl.pallas_call(
        body, name=name, grid=grid, in_specs=in_specs,
        out_specs=[cur(w), cur(LANES)],
        out_shape=[jax.ShapeDtypeStruct((n, l, w), o_dtype), jax.ShapeDtypeStruct((n, l, LANES), f32)],
        scratch_shapes=[pltpu.VMEM((w // HD, BLK, BLK), f32), pltpu.VMEM((w // HD, BLK, 2 * BLK if nb > 1 else BLK), bf16),
                        pltpu.VMEM((w // HD if diag else 1, BLK, BLK), f32)],
        compiler_params=_cp(("arbitrary", "arbitrary")),
    )(*args)


def _attn_bwd(q, k, v, do, delta, lse, sink, *, max_dist, name):
    n, l, w = q.shape
    wk = k.shape[-1]
    nb = l // BLK
    has_sink = sink is not None

    def body(*refs):
        sink_ref = dsink_ref = ck = cv = None
        if has_sink:
            sink_ref, refs = refs[0], refs[1:]
        nin = 8 if nb > 1 else 6
        ins, rest = refs[:nin], refs[nin:]
        if has_sink:
            dq_ref, dk_ref, dv_ref, dsink_ref = rest[:4]
            rest = rest[4:]
        else:
            dq_ref, dk_ref, dv_ref = rest[:3]
            rest = rest[3:]
        step = pl.program_id(1)
        if has_sink:
            @pl.when((pl.program_id(0) == 0) & (step == 0))
            def _():
                dsink_ref[...] = jnp.zeros_like(dsink_ref)

        if nb > 1:
            q_ref, kc_ref, kp_ref, vc_ref, vp_ref, do_ref, delta_ref, lse_ref = ins
            ck, cv = rest[:2]

            @pl.when(step == 0)
            def _():
                ck[...] = jnp.zeros_like(ck)
                cv[...] = jnp.zeros_like(cv)

            last = step == nb // _att_units(nb) - 1
            for u in reversed(range(_att_units(nb))):
                rows, before = pl.ds(BLK * u, BLK), pl.ds(BLK * (u - 1), BLK)
                unit(q_ref.at[rows, :], kc_ref.at[rows, :], kp_ref if u == 0 else kc_ref.at[before, :],
                     vc_ref.at[rows, :], vp_ref if u == 0 else vc_ref.at[before, :], do_ref.at[rows, :],
                     delta_ref.at[rows, :], lse_ref.at[rows, :], dq_ref.at[rows, :], dk_ref.at[rows, :], dv_ref.at[rows, :],
                     jnp.logical_not(last) if u == 0 else True, sink_ref, dsink_ref, ck, cv, *rest[2:])
        else:
            q_ref, kc_ref, vc_ref, do_ref, delta_ref, lse_ref = ins
            for u in range(_att_units(nb)):
                unit(q_ref.at[u], kc_ref.at[u], None, vc_ref.at[u], None, do_ref.at[u], delta_ref.at[u], lse_ref.at[u],
                     dq_ref.at[u], dk_ref.at[u], dv_ref.at[u], None, sink_ref, dsink_ref, None, None, *rest)

    def unit(q_ref, kc_ref, kp_ref, vc_ref, vp_ref, do_ref, delta_ref, lse_ref, dq_ref, dk_ref, dv_ref, has_prev,
             sink_ref, dsink_ref, ck, cv, sscr, dpscr, pscr, dsscr, dscr=None, ddscr=None):
        lane = lax.broadcasted_iota(jnp.int32, (BLK, LANES), 1)
        qi = lax.broadcasted_iota(jnp.int32, (BLK, BLK), 0)
        kj = lax.broadcasted_iota(jnp.int32, (BLK, BLK), 1)
        tri = kj <= qi
        eye = kj == qi
        cache = {}
        kp, vp = kp_ref, vp_ref
        for p in range(w // LANES):
            sl = slice(LANES * p, LANES * (p + 1))
            qpair, dopair = q_ref[:, sl], do_ref[:, sl]
            kcat, vcat = _kv_cat(kc_ref, kp, p // share, cache), _kv_cat(vc_ref, vp, p // share, cache)
            for hh in range(2):
                h = 2 * p + hh
                s = _dot_nt(_lane_half(qpair, hh), kcat)
                dp = _dot_nt(_lane_half(dopair, hh), vcat)
                if nb > 1:
                    sp = s[:, :BLK] if has_prev is True else jnp.where(has_prev, s[:, :BLK], NEG)
                    sscr[h] = jnp.where(tri, s[:, BLK:], sp)
                    dpscr[h] = jnp.where(tri, dp[:, BLK:], dp[:, :BLK])
                    if diag:
                        dscr[h] = jnp.where(eye, sp, NEG)
                        ddscr[h] = dp[:, :BLK]
                else:
                    sscr[h] = jnp.where(tri, s, NEG)
                    dpscr[h] = dp
        for p in range(w // LANES):
            for hh in range(2):
                h = 2 * p + hh
                lse_b = jnp.broadcast_to(lse_ref[:, h:h + 1], (BLK, BLK))
                delta = jnp.broadcast_to(delta_ref[:, h:h + 1], (BLK, BLK))
                pr = jnp.exp(sscr[h] - lse_b)
                ds = pr * (dpscr[h] - delta)
                if nb > 1:
                    if diag:
                        prd = jnp.exp(dscr[h] - lse_b)
                        dsd = prd * (ddscr[h] - delta)
                    else:
                        prd = dsd = 0.0
                    pscr[h, :, :BLK] = jnp.where(tri, prd, pr).astype(bf16)
                    pscr[h, :, BLK:] = jnp.where(tri, pr, 0.0).astype(bf16)
                    dsscr[h, :, :BLK] = jnp.where(tri, dsd, ds).astype(bf16)
                    dsscr[h, :, BLK:] = jnp.where(tri, ds, 0.0).astype(bf16)
                else:
                    pscr[h] = pr.astype(bf16)
                    dsscr[h] = ds.astype(bf16)
                if has_sink:
                    dsk = -jnp.sum(jnp.where(lane == 0, jnp.exp(sink_ref[0, h] - lse_b) * delta, 0.0), keepdims=True)
                    dsink_ref[h:h + 1, :] += jnp.broadcast_to(dsk, (1, LANES))
        for p in range(w // LANES):
            sl = slice(LANES * p, LANES * (p + 1))
            qpair, dopair = q_ref[:, sl], do_ref[:, sl]
            kcat = _kv_cat(kc_ref, kp, p // share, cache)
            dq_ref[:, sl] = _dot(dsscr[2 * p], _lane_half(kcat, 0)) + _dot(dsscr[2 * p + 1], _lane_half(kcat, 1))
            dk_pair = _dot_tn(dsscr[2 * p], _lane_half(qpair, 0)) + _dot_tn(dsscr[2 * p + 1], _lane_half(qpair, 1))
            dv_pair = _dot_tn(pscr[2 * p], _lane_half(dopair, 0)) + _dot_tn(pscr[2 * p + 1], _lane_half(dopair, 1))
            if nb > 1:
                dk_ref[:, sl] = dk_pair[BLK:] + ck[:, sl]
                dv_ref[:, sl] = dv_pair[BLK:] + cv[:, sl]
                ck[:, sl] = dk_pair[:BLK]
                cv[:, sl] = dv_pair[:BLK]
            else:
                dk_ref[:, sl] = dk_pair
                dv_ref[:, sl] = dv_pair

    assert max_dist in (BLK - 1, BLK) and w % wk == 0
    share = w // wk
    diag = nb > 1 and max_dist == BLK
    cur, prev, grid = _attn_specs(n, nb, True)
    in_specs = ([cur(w), cur(wk)] + ([prev(wk)] if nb > 1 else []) + [cur(wk)] + ([prev(wk)] if nb > 1 else [])
                + [cur(w), cur(LANES), cur(LANES)])
    args = [q, k] + ([k] if nb > 1 else []) + [v] + ([v] if nb > 1 else []) + [do, delta, lse]
    out_specs = [cur(w)] * 3
    out_shape = [jax.ShapeDtypeStruct((n, l, w), f32)] * 3
    if has_sink:
        in_specs = [pl.BlockSpec(memory_space=pltpu.SMEM)] + in_specs
        args = [sink] + args
        out_specs.append(pl.BlockSpec((NHEAD, LANES), lambda a, i: (0, 0)))
        out_shape.append(jax.ShapeDtypeStruct((NHEAD, LANES), f32))
    nh = w // HD
    scratch = [pltpu.VMEM((BLK, w), f32), pltpu.VMEM((BLK, w), f32)] if nb > 1 else []
    scratch += [pltpu.VMEM((nh, BLK, BLK), f32)] * 2 + [pltpu.VMEM((nh, BLK, 2 * BLK if nb > 1 else BLK), bf16)] * 2
    if diag:
        scratch += [pltpu.VMEM((nh, BLK, BLK), f32)] * 2
    return pl.pallas_call(
        body, name=name, grid=grid, in_specs=in_specs, out_specs=out_specs, out_shape=out_shape,
        scratch_shapes=scratch, compiler_params=_cp(("arbitrary", "arbitrary")),
    )(*args)


def _split2(x):
    hi = x.astype(bf16)
    return hi, (x - hi.astype(f32)).astype(bf16)


def _heads_to_lanes(xc, e):
    return sum(_dot(t, e) for t in _split2(xc))


def _lanes_to_heads(x, g):
    return sum(_dot(t, g) for t in _split2(x))


HEAD_EXPAND = (np.arange(LANES)[:, None] == np.arange(BW)[None, :] // HD).astype(np.float32)
HEAD_SUM = HEAD_EXPAND.T.copy()


def _branch_weights(l1_ref, l4_ref, l16_ref, scr):
    l4v = _perm_load(l4_ref, scr, 4)
    l16v = _perm_load(l16_ref, scr, 16)
    l1v = l1_ref[...]
    m = jnp.maximum(jnp.maximum(l1v, l4v), l16v)
    e1, e4, e16 = jnp.exp(l1v - m), jnp.exp(l4v - m), jnp.exp(l16v - m)
    z = e1 + e4 + e16
    return e1 / z, e4 / z, e16 / z


def _mix_out(oa, o1, l1, o4, l4, o16, l16, g_mix_a, g_mix_b, w_out, x, mod, g_post):
    def body(oa_ref, o1_ref, l1_ref, o4_ref, l4_ref, o16_ref, l16_ref, ga_ref, gb_ref, w_ref, x_ref, mod_ref, gp_ref, e_ref,
             x1_ref, y_ref, mixed_ref, ob_ref, scr):
        w1, w4, w16 = _branch_weights(l1_ref, l4_ref, l16_ref, scr)
        e = e_ref[...]
        x1w, x4w = _heads_to_lanes(w1, e), _heads_to_lanes(w4, e)
        ob = (x1w * o1_ref[...].astype(f32) + x4w * _perm_load(o4_ref, scr, 4)
              + (1.0 - x1w - x4w) * _perm_load(o16_ref, scr, 16))
        ob_ref[...] = ob
        oan, _ = _rms(oa_ref[...])
        obn, _ = _rms(ob)
        mixed = jnp.concatenate([oan * ga_ref[...], obn * gb_ref[...]], axis=1).astype(bf16)
        mixed_ref[...] = mixed
        y = _dot(mixed, w_ref[...])
        y_ref[...] = y
        yn, _ = _rms(y)
        x1_ref[...] = x_ref[...] + mod_ref[2:3, :] * (yn * gp_ref[...])

    nat = lambda w, dt: jax.ShapeDtypeStruct((BL, SEQ, w), dt)
    return pl.pallas_call(
        body, name="mix_out", grid=(BL, NJ),
        in_specs=[_tok(AQ), _tok(BW), _tok(LANES), _perm_spec(4, BW), _perm_spec(4, LANES), _perm_spec(16, BW),
                  _perm_spec(16, LANES), _full((1, AQ)), _full((1, BW)), _full((D, D)), _tok(D), MOD_SPEC, _full((1, D)),
                  _full((LANES, BW))],
        out_specs=[_tok(D), _tok(D), _tok(D), _tok(BW)],
        out_shape=[nat(D, f32), nat(D, f32), nat(D, bf16), nat(BW, f32)],
        scratch_shapes=[pltpu.VMEM((BW // LANES, TM, LANES), f32)],
        compiler_params=_cp(("arbitrary", "arbitrary")),
    )(oa, o1, l1, o4, l4, o16, l16, g_mix_a, g_mix_b, w_out, x, mod, g_post, jnp.asarray(HEAD_EXPAND, bf16))


def _mlp_up(x1, mod, g_pre, w_up):
    def body(x_ref, mod_ref, g_ref, w_ref, h_ref, u_ref, a_ref):
        xn, _ = _rms(x_ref[...])
        h = (xn * g_ref[...]) * (1.0 + mod_ref[4:5, :]) + mod_ref[3:4, :]
        hb = h.astype(bf16)
        h_ref[...] = hb
        for s in range(NCHIP):
            u = _dot(hb, w_ref[s])
            u_ref[:, D * s:D * (s + 1)] = u.astype(bf16)
            a_ref[:, D * s:D * (s + 1)] = jnp.square(jnp.maximum(u, 0.0)).astype(bf16)

    nat = lambda w: jax.ShapeDtypeStruct((BL, SEQ, w), bf16)
    return pl.pallas_call(
        body, name="mlp_up", grid=(BL, NJ),
        in_specs=[_tok(D), MOD_SPEC, _full((1, D)), _full((NCHIP, D, D))],
        out_specs=[_tok(D), _tok(DFF), _tok(DFF)], out_shape=[nat(D), nat(DFF), nat(DFF)],
        compiler_params=_cp(("arbitrary", "arbitrary")),
    )(x1, mod, g_pre, w_up)


def _mlp_down(a, w_down, x1, target, mod, g_post):
    def body(a_ref, w_ref, x_ref, t_ref, mod_ref, g_ref, gx_ref, dy_ref, accb_ref, accg_ref):
        _acc_init(accb_ref, accg_ref)
        y2 = _dot(a_ref[...], w_ref[...])
        yn, r = _rms(y2)
        g = g_ref[...]
        gt = mod_ref[5:6, :]
        n2 = yn * g
        err = x_ref[...] + gt * n2 - t_ref[...]
        gout = err * (1.0 / D)
        gx_ref[...] = gout
        dn2 = gout * gt
        dy_ref[...] = _rms_bwd(dn2 * g, yn, r).astype(bf16)
        accb_ref[0:1, :] += _colsum(gout * n2)
        accg_ref[0:1, :] += _colsum(dn2 * yn)
        accg_ref[1:2, :] += jnp.broadcast_to(jnp.sum(err * err, keepdims=True), (1, D))

    return pl.pallas_call(
        body, name="mlp_down", grid=(BL, NJ),
        in_specs=[_tok(DFF), _full((DFF, D)), _tok(D), _tok(D), MOD_SPEC, _full((1, D))],
        out_specs=[_tok(D), _tok(D), ACCB_SPEC, ACCG_SPEC],
        out_shape=[jax.ShapeDtypeStruct((BL, SEQ, D), f32), jax.ShapeDtypeStruct((BL, SEQ, D), bf16)] + ACC_SHAPES,
        compiler_params=_cp(("arbitrary", "arbitrary")),
    )(a, w_down, x1, target, mod, g_post)


def _mlp_bwd(dy2, u, w_down, w_up, x1, gx, mod, g_pre):
    def body(dy_ref, u_ref, wd_hbm, wu_hbm, x_ref, gx_ref, mod_ref, g_ref, du_ref, gx1_ref, accb_ref, accg_ref, wd, wu, sem):
        _acc_init(accb_ref, accg_ref)
        first = (pl.program_id(0) == 0) & (pl.program_id(1) == 0)
        c1 = pltpu.make_async_copy(wd_hbm, wd, sem.at[0])
        c2 = pltpu.make_async_copy(wu_hbm, wu, sem.at[1])

        @pl.when(first)
        def _():
            c1.start()
            c2.start()
            c1.wait()

        dy = dy_ref[...]
        for s in range(NCHIP):
            sl = slice(D * s, D * (s + 1))
            da = _dot_nt(dy, wd[sl, :])
            du_ref[:, sl] = (da * (2.0 * jnp.maximum(u_ref[:, sl].astype(f32), 0.0))).astype(bf16)

        @pl.when(first)
        def _():
            c2.wait()

        dh = jnp.zeros((TM, D), f32)
        for s in range(NCHIP):
            dh = dh + _dot_nt(du_ref[:, D * s:D * (s + 1)], wu[s])
        xn, r = _rms(x_ref[...])
        g = g_ref[...]
        n = xn * g
        dn = dh * (1.0 + mod_ref[4:5, :])
        gx1_ref[...] = gx_ref[...] + _rms_bwd(dn * g, xn, r)
        accb_ref[0:1, :] += _colsum(dh * n)
        accb_ref[1:2, :] += _colsum(dh)
        accg_ref[0:1, :] += _colsum(dn * xn)

    anyspec = pl.BlockSpec(memory_space=pl.ANY)
    return pl.pallas_call(
        body, name="mlp_bwd", grid=(BL, NJ),
        in_specs=[_tok(D), _tok(DFF), anyspec, anyspec, _tok(D), _tok(D), MOD_SPEC, _full((1, D))],
        out_specs=[_tok(DFF), _tok(D), ACCB_SPEC, ACCG_SPEC],
        out_shape=[jax.ShapeDtypeStruct((BL, SEQ, DFF), bf16), jax.ShapeDtypeStruct((BL, SEQ, D), f32)] + ACC_SHAPES,
        scratch_shapes=[pltpu.VMEM((DFF, D), bf16), pltpu.VMEM((NCHIP, D, D), bf16), pltpu.SemaphoreType.DMA((2,))],
        compiler_params=_cp(("arbitrary", "arbitrary")),
    )(dy2, u, w_down, w_up, x1, gx, mod, g_pre)


def _matmul_tn(a, b, *, tn, col_blocked, name, out_dtype=f32):
    t, m = a.shape
    n = b.shape[1]
    tmm = min(m, 1024)
    tk = 2048 if tn <= 1024 else 1024
    nk = t // tk

    def body(a_ref, b_ref, o_ref, acc):
        k = pl.program_id(2)

        @pl.when(k == 0)
        def _():
            acc[...] = jnp.zeros_like(acc)

        acc[...] += _dot_tn(a_ref[...], b_ref[...])

        @pl.when(k == nk - 1)
        def _():
            o_ref[...] = acc[...].astype(out_dtype)

    if col_blocked:
        out_spec = pl.BlockSpec((None, tmm, tn), lambda i, j, k: (j, i, 0))
        out_shape = jax.ShapeDtypeStruct((n // tn, m, tn), out_dtype)
    else:
        out_spec = pl.BlockSpec((tmm, tn), lambda i, j, k: (i, j))
        out_shape = jax.ShapeDtypeStruct((m, n), out_dtype)
    return pl.pallas_call(
        body, name=name, grid=(m // tmm, n // tn, nk),
        in_specs=[pl.BlockSpec((tk, tmm), lambda i, j, k: (k, i)), pl.BlockSpec((tk, tn), lambda i, j, k: (k, j))],
        out_specs=out_spec, out_shape=out_shape, scratch_shapes=[pltpu.VMEM((tmm, tn), f32)],
        compiler_params=_cp(("arbitrary", "arbitrary", "arbitrary")),
    )(a, b)


def _grad_w_in(h, dproj):
    t = h.shape[0]
    tk = 1024
    nk = t // tk
    sw = INW // NCHIP

    def body(a_ref, b_ref, o_ref, acc):
        k = pl.program_id(0)

        @pl.when(k == 0)
        def _():
            acc[...] = jnp.zeros_like(acc)

        acc[...] += _dot_tn(a_ref[...], b_ref[...])

        @pl.when(k == nk - 1)
        def _():
            for s in range(NCHIP):
                o_ref[s] = acc[:, sw * s:sw * (s + 1)].astype(bf16)

    return pl.pallas_call(
        body, name="grad_w_in", grid=(nk,),
        in_specs=[pl.BlockSpec((tk, D), lambda k: (k, 0)), pl.BlockSpec((tk, INW), lambda k: (k, 0))],
        out_specs=pl.BlockSpec((NCHIP, D, sw), lambda k: (0, 0, 0)), out_shape=jax.ShapeDtypeStruct((NCHIP, D, sw), bf16),
        scratch_shapes=[pltpu.VMEM((D, INW), f32)], compiler_params=_cp(("arbitrary",)),
    )(h, dproj)


def _attn_out_bwd(gx1, y, mod, g_post, w_out, oa, ob, g_mix_a, g_mix_b, l1, l4, l16):
    def body(gx_ref, y_ref, mod_ref, gp_ref, w_ref, oa_ref, ob_ref, ga_ref, gb_ref, l1_ref, l4_ref, l16_ref, e_ref, g_ref,
             dy_ref, doa_ref, do1_ref, do4_ref, do16_ref, da_ref, d1_ref, d4_ref, d16_ref, accb_ref, accg_ref, scr):
        _acc_init(accb_ref, accg_ref)
        w1, w4, w16 = _branch_weights(l1_ref, l4_ref, l16_ref, scr)
        e, hs = e_ref[...], g_ref[...]
        gx1v = gx_ref[...]
        yn, ry = _rms(y_ref[...])
        gp = gp_ref[...]
        gt = mod_ref[2:3, :]
        dn1 = gx1v * gt
        dy = _rms_bwd(dn1 * gp, yn, ry).astype(bf16)
        dy_ref[...] = dy
        dmixed = _dot_nt(dy, w_ref[...])
        dma, dmb = dmixed[:, :AQ], dmixed[:, AQ:]
        oa, ob = oa_ref[...], ob_ref[...]
        oan, ra = _rms(oa)
        obn, rb = _rms(ob)
        doa = _rms_bwd(dma * ga_ref[...], oan, ra)
        doa_ref[...] = doa.astype(bf16)
        da_ref[...] = _lanes_to_heads(doa * oa, hs)
        dob = _rms_bwd(dmb * gb_ref[...], obn, rb)
        dd = _lanes_to_heads(dob * ob, hs)
        x1w, x4w = _heads_to_lanes(w1, e), _heads_to_lanes(w4, e)
        do1_ref[...] = (x1w * dob).astype(bf16)
        d1_ref[...] = w1 * dd
        _perm_store(x4w * dob, scr, do4_ref, 4)
        _perm_store(w4 * dd, scr, d4_ref, 4)
        _perm_store((1.0 - x1w - x4w) * dob, scr, do16_ref, 16)
        _perm_store(w16 * dd, scr, d16_ref, 16)
        accb_ref[0:1, :] += _colsum(gx1v * (yn * gp))
        accg_ref[0:1, :] += _colsum(dn1 * yn)
        accg_ref[1:2, :] += jnp.concatenate([_colsum(dma * oan), _colsum(dmb * obn)], axis=1)

    nat = lambda w, dt: jax.ShapeDtypeStruct((BL, SEQ, w), dt)
    return pl.pallas_call(
        body, name="attn_out_bwd", grid=(BL, NJ),
        in_specs=[_tok(D), _tok(D), MOD_SPEC, _full((1, D)), _full((D, D)), _tok(AQ), _tok(BW), _full((1, AQ)), _full((1, BW)),
                  _tok(LANES), _perm_spec(4, LANES), _perm_spec(16, LANES), _full((LANES, BW)), _full((BW, LANES))],
        out_specs=[_tok(D), _tok(AQ), _tok(BW), _perm_spec(4, BW), _perm_spec(16, BW),
                   _tok(LANES), _tok(LANES), _perm_spec(4, LANES), _perm_spec(16, LANES), ACCB_SPEC, ACCG_SPEC],
        out_shape=[nat(D, bf16), nat(AQ, bf16), nat(BW, bf16), jax.ShapeDtypeStruct((BL, 4, SEQ // 4, BW), bf16),
                   jax.ShapeDtypeStruct((BL, 16, SEQ // 16, BW), bf16), nat(LANES, f32), nat(LANES, f32),
                   jax.ShapeDtypeStruct((BL, 4, SEQ // 4, LANES), f32), jax.ShapeDtypeStruct((BL, 16, SEQ // 16, LANES), f32)]
                  + ACC_SHAPES,
        scratch_shapes=[pltpu.VMEM((BW // LANES, TM, LANES), f32)],
        compiler_params=_cp(("arbitrary", "arbitrary")),
    )(gx1, y, mod, g_post, w_out, oa, ob, g_mix_a, g_mix_b, l1, l4, l16, jnp.asarray(HEAD_EXPAND, bf16),
      jnp.asarray(HEAD_SUM, bf16))


def _attn_in_bwd(dqa, dka, dva, d1, d4, d16, tc, ts1, ts2, w_in, x, gx1, mod, g_pre):
    def body(dqa_ref, dka_ref, dva_ref, dq1_ref, dk1_ref, dv1_ref, dq4_ref, dk4_ref, dv4_ref, dq16_ref, dk16_ref, dv16_ref,
             c_ref, s1_ref, s2_ref, w_ref, x_ref, gx_ref, mod_ref, g_ref, dproj_ref, dx_ref, accb_ref, accg_ref, scr):
        _acc_init(accb_ref, accg_ref)
        c, s1, s2 = c_ref[...], s1_ref[...], s2_ref[...]
        tot = lambda r1, r4, r16: r1[...] + _perm_load(r4, scr, 4) + _perm_load(r16, scr, 16)
        dqb = tot(dq1_ref, dq4_ref, dq16_ref)
        dkb = tot(dk1_ref, dk4_ref, dk16_ref)
        dvb = tot(dv1_ref, dv4_ref, dv16_ref)
        dproj = jnp.concatenate([
            _rope_t(dqa_ref[...], c, s1, s2) * QSCALE, _rope_t(_per_kv_head(dka_ref[...]), c, s1, s2),
            _per_kv_head(dva_ref[...]),
            _rope_t(dqb, c, s1, s2) * QSCALE, _rope_t(dkb, c, s1, s2), dvb], axis=1).astype(bf16)
        dproj_ref[...] = dproj
        dh = _dot_nt(dproj, w_ref[...])
        xn, r = _rms(x_ref[...])
        g = g_ref[...]
        dn = dh * (1.0 + mod_ref[1:2, :])
        dx_ref[...] = gx_ref[...] + _rms_bwd(dn * g, xn, r)
        accb_ref[0:1, :] += _colsum(dh * (xn * g))
        accb_ref[1:2, :] += _colsum(dh)
        accg_ref[0:1, :] += _colsum(dn * xn)

    return pl.pallas_call(
        body, name="attn_in_bwd", grid=(BL, NJ),
        in_specs=[_tok(AQ), _tok(AQ), _tok(AQ)] + [_tok(BW)] * 3 + [_perm_spec(4, BW)] * 3 + [_perm_spec(16, BW)] * 3
                 + [_tok(LANES)] * 3 + [_full((D, INW)), _tok(D), _tok(D), MOD_SPEC, _full((1, D))],
        out_specs=[_tok(INW), _tok(D), ACCB_SPEC, ACCG_SPEC],
        out_shape=[jax.ShapeDtypeStruct((BL, SEQ, INW), bf16), jax.ShapeDtypeStruct((BL, SEQ, D), f32)] + ACC_SHAPES,
        scratch_shapes=[pltpu.VMEM((BW // LANES, TM, LANES), f32)],
        compiler_params=_cp(("arbitrary", "arbitrary")),
    )(dqa, dka, dva, *d1, *d4, *d16, tc, ts1, ts2, w_in, x, gx1, mod, g_pre)


def _inv_lane():
    inv = np.float32(THETA) ** (-np.arange(0, ROT, 2, dtype=np.float32) / np.float32(ROT))
    lane = np.arange(LANES) % HD
    return jnp.asarray(np.where(lane < ROT, inv[lane % (ROT // 2)], 0.0).astype(np.float32)[None, :])


def _local_step(x, tabs, mod, target, w_in, later_weights, grad_ready, g_attn_pre,
                g_attn_post, sink_a, g_mix_a, g_mix_b, g_mlp_pre, g_mlp_post):
    tc, ts1, ts2 = [t.reshape(BL, SEQ, LANES) for t in tabs]

    (h, qa, ka, va, q1, k1, v1, q4, k4, v4, q16, k16, v16, w_in) = _attn_in(x, mod, g_attn_pre, w_in, tc, ts1, ts2)
    seqs = lambda t: t.reshape(t.shape[0] * t.shape[1], t.shape[2], t.shape[3])
    q4, k4, v4, q16, k16, v16 = [seqs(t) for t in (q4, k4, v4, q16, k16, v16)]
    oa, la = _attn_fwd(qa, ka, va, sink_a, max_dist=BLK - 1, o_dtype=f32, name="attn_a_fwd")
    o1, l1 = _attn_fwd(q1, k1, v1, None, max_dist=BLK, o_dtype=bf16, name="attn_b1_fwd")
    o4, l4 = _attn_fwd(q4, k4, v4, None, max_dist=BLK, o_dtype=bf16, name="attn_b4_fwd")
    o16, l16 = _attn_fwd(q16, k16, v16, None, max_dist=BLK, o_dtype=bf16, name="attn_b16_fwd")
    b4 = lambda t: t.reshape(BL, 4, SEQ // 4, t.shape[-1])
    b16 = lambda t: t.reshape(BL, 16, SEQ // 16, t.shape[-1])
    w_out, mlp_weights, mod = later_weights((oa, o1, o4, o16), mod)
    x1, y, mixed, ob = _mix_out(oa, o1, l1, b4(o4), b4(l4), b16(o16), b16(l16), g_mix_a, g_mix_b, w_out, x, mod, g_attn_post)
    w_up, w_down = mlp_weights((x1,))
    h2, u, a = _mlp_up(x1, mod, g_mlp_pre, w_up)
    gx, dy2, accb_d, accg_d = _mlp_down(a, w_down, x1, target, mod, g_mlp_post)

    flat = lambda t: t.reshape(BL * SEQ, t.shape[-1])
    mod = grad_ready("w_down", _matmul_tn(flat(a), flat(dy2), tn=D, col_blocked=False, name="grad_w_down", out_dtype=bf16), mod)
    du, gx1, accb_m, accg_m = _mlp_bwd(dy2, u, w_down, w_up, x1, gx, mod, g_mlp_pre)
    mod = grad_ready("w_up", _matmul_tn(flat(h2), flat(du), tn=D, col_blocked=True, name="grad_w_up", out_dtype=bf16), mod)

    dy, doa, do1, do4, do16, da, dl1, dl4, dl16, accb_o, accg_o = _attn_out_bwd(
        gx1, y, mod, g_attn_post, w_out, oa, ob, g_mix_a, g_mix_b, l1, b4(l4), b16(l16))
    sink_behind = grad_ready("w_out", _matmul_tn(flat(mixed), flat(dy), tn=D, col_blocked=False, name="grad_w_out",
                                                  out_dtype=bf16), sink_a)
    dqa, dka, dva, dsink = _attn_bwd(qa, ka, va, doa, da, la, sink_behind, max_dist=BLK - 1, name="attn_a_bwd")
    d1 = _attn_bwd(q1, k1, v1, do1, dl1, l1, None, max_dist=BLK, name="attn_b1_bwd")
    d4 = _attn_bwd(q4, k4, v4, seqs(do4), seqs(dl4), l4, None, max_dist=BLK, name="attn_b4_bwd")
    d16 = _attn_bwd(q16, k16, v16, seqs(do16), seqs(dl16), l16, None, max_dist=BLK, name="attn_b16_bwd")
    dproj, grad_x, accb_i, accg_i = _attn_in_bwd(dqa, dka, dva, d1, [b4(t) for t in d4], [b16(t) for t in d16],
                                                 tc, ts1, ts2, w_in, x, gx1, mod, g_attn_pre)
    gw_in = _grad_w_in(flat(h), flat(dproj))
    dsink = grad_ready("w_in", gw_in, dsink)

    return grad_x, (accb_i, accb_o, accb_m, accb_d, accg_i, accg_o, accg_m, accg_d, dsink)


ADAW = NMOD * D // NCHIP


def _pos():
    return lax.axis_index("x"), lax.axis_index("y"), lax.axis_index("c")


def _flip(v, bit):
    return 1 - v if bit else v


def _all_peers(x, y, c):
    return [(_flip(x, k >> 2 & 1), _flip(y, k >> 1 & 1), _flip(c, k & 1)) for k in range(1, NDEV)]


def _other_chips(x, y):
    return [(1 - x, y), (x, 1 - y), (1 - x, 1 - y)]


def _rcopy(src, dst, send, recv, k, dev, k_recv=None):
    return pltpu.make_async_remote_copy(src_ref=src, dst_ref=dst, send_sem=send.at[k],
                                        recv_sem=recv.at[k if k_recv is None else k_recv],
                                        device_id=dev, device_id_type=MESH)


def _small_copies(src, land, send, recv):
    x, y, c = _pos()
    me = 4 * x + 2 * y + c
    return [(_rcopy(src, land.at[me], send, recv, k, p), _rcopy(src, land.at[4 * p[0] + 2 * p[1] + p[2]], send, recv, k, p))
            for k, p in enumerate(_all_peers(x, y, c))]


def _ada_fwd(c_in, landed, w_ada, b_cols):
    def body(c_ref, land, w_hbm, b_ref, mod_ref, cond_ref, mbuf, w_ref, s2, r2, wsem):
        x, y, c = _pos()
        chip = 2 * x + y
        me = 4 * x + 2 * y + c
        wcopy = pltpu.make_async_copy(w_hbm, w_ref, wsem)
        wcopy.start()
        for i in range(NDEV):
            @pl.when(me == i)
            def _():
                cond_ref[BL * i:BL * (i + 1), :] = c_ref[...]

            @pl.when(me != i)
            def _():
                cond_ref[BL * i:BL * (i + 1), :] = land[i]
        call = cond_ref[...]
        cond = call / (1.0 + jnp.exp(-call))
        cond_ref[...] = cond
        wcopy.wait()
        mbuf[chip] = _dot(cond.astype(bf16), w_ref[...].astype(bf16)) + b_ref[...]
        chips = _other_chips(x, y)
        sends = [_rcopy(mbuf.at[chip], mbuf.at[chip], s2, r2, j, (px, py, c)) for j, (px, py) in enumerate(chips)]
        for cp in sends:
            cp.start()
        for j, (px, py) in enumerate(chips):
            _rcopy(mbuf.at[chip], mbuf.at[2 * px + py], s2, r2, j, (px, py, c)).wait_recv()
        for cp in sends:
            cp.wait_send()
        row = lax.broadcasted_iota(jnp.int32, (BL * NDEV, ADAW), 0)
        for s in range(NCHIP):
            slab = mbuf[s]
            for j in range(BL):
                mod_ref[j:j + 1, ADAW * s:ADAW * (s + 1)] = jnp.sum(jnp.where(row == BL * me + j, slab, 0.0), axis=0, keepdims=True)

    vm = pl.BlockSpec(memory_space=pltpu.VMEM)
    return pl.pallas_call(
        body, name="ada_fwd", in_specs=[vm, vm, pl.BlockSpec(memory_space=pl.ANY), vm], out_specs=[vm, vm],
        out_shape=[jax.ShapeDtypeStruct((BL, NMOD * D), f32), jax.ShapeDtypeStruct((BL * NDEV, D), f32)],
        scratch_shapes=[pltpu.VMEM((NCHIP, BL * NDEV, ADAW), f32), pltpu.VMEM((D, ADAW), f32),
                        pltpu.SemaphoreType.DMA((NCHIP - 1,)), pltpu.SemaphoreType.DMA((NCHIP - 1,)),
                        pltpu.SemaphoreType.DMA],
        compiler_params=pltpu.CompilerParams(vmem_limit_bytes=VMEM_LIMIT),
    )(c_in, landed, w_ada, b_cols)


PAY_ROWS = 4


def _small_pack(accs):
    def body(bi, bo, bm, bd, gi, go, gm, gd, dsink, pay):
        pay[...] = jnp.zeros_like(pay)
        for b in range(BL):
            for k, (ref, r) in enumerate(((bi, 1), (bi, 0), (bo, 0), (bm, 1), (bm, 0), (bd, 0))):
                pay[b:b + 1, D * k:D * (k + 1)] = ref[b, r:r + 1, :]
        for off, ref, r in ((OFF_G_ATTN_PRE, gi, 0), (OFF_G_ATTN_POST, go, 0), (OFF_G_MIX_A, go, 1), (OFF_G_MLP_PRE, gm, 0),
                            (OFF_G_MLP_POST, gd, 0)):
            pay[BL:BL + 1, off:off + D] = ref[r:r + 1, :]
        eye = lax.broadcasted_iota(jnp.int32, (NHEAD, LANES), 0) == lax.broadcasted_iota(jnp.int32, (NHEAD, LANES), 1)
        pay[BL:BL + 1, OFF_SINK:OFF_SINK + LANES] = jnp.sum(jnp.where(eye, dsink[...], 0.0), axis=0, keepdims=True)
        pay[BL:BL + 1, OFF_LOSS:OFF_LOSS + LANES] = gd[1:2, 0:LANES]

    vm = pl.BlockSpec(memory_space=pltpu.VMEM)
    return pl.pallas_call(body, name="small_pack", in_specs=[vm] * 9, out_specs=vm,
                          out_shape=jax.ShapeDtypeStruct((PAY_ROWS, PAYW), f32))(*accs)


def _small_sum(own, landed, cond_all):
    def body(pay, land, cond_ref, gw_ref, gb_ref, small_ref, pbuf, dall):
        x, y, c = _pos()
        chip = 2 * x + y
        me = 4 * x + 2 * y + c
        for i in range(NDEV):
            @pl.when(me == i)
            def _():
                pbuf[i] = pay[...]

            @pl.when(me != i)
            def _():
                pbuf[i] = land[i]
        small = pbuf[0, BL:BL + 1, :]
        for i in range(1, NDEV):
            small = small + pbuf[i, BL:BL + 1, :]
        small_ref[...] = small
        for i in range(NDEV):
            dall[BL * i:BL * (i + 1), :] = pbuf[i, 0:BL, :]
        gb_ref[...] = jnp.sum(dall[...], axis=0, keepdims=True)
        cols = jnp.zeros((BL * NDEV, ADAW), f32)
        for s in range(NCHIP):
            cols = cols + jnp.where(chip == s, dall[:, ADAW * s:ADAW * (s + 1)], 0.0)
        gw_ref[...] = _dot_tn(cond_ref[...].astype(bf16), cols.astype(bf16))

    vm = pl.BlockSpec(memory_space=pltpu.VMEM)
    return pl.pallas_call(
        body, name="small_sum", in_specs=[vm] * 3, out_specs=[vm] * 3,
        out_shape=[jax.ShapeDtypeStruct((D, ADAW), f32), jax.ShapeDtypeStruct((1, PAYW), f32), jax.ShapeDtypeStruct((1, PAYW), f32)],
        scratch_shapes=[pltpu.VMEM((NDEV, PAY_ROWS, PAYW), f32), pltpu.VMEM((BL * NDEV, PAYW), f32)],
        compiler_params=pltpu.CompilerParams(vmem_limit_bytes=VMEM_LIMIT),
    )(own, landed, cond_all)


def _half(ref, c):
    r2 = ref.shape[0] // 2
    return ref.at[pl.ds(c * r2 if isinstance(c, int) else pl.multiple_of(c * r2, 16), r2), :]


HBM_SPEC = pl.BlockSpec(memory_space=pltpu.HBM)
SEM_SPEC = pl.BlockSpec(memory_space=pltpu.SEMAPHORE)
EFFECT = pltpu.SideEffectType.DATAFLOW_SIDE_EFFECTING
NLINK = NCHIP - 1


def _in_hbm(a):
    return pltpu.with_memory_space_constraint(a, pltpu.HBM)


NSEM = 8


def _split_start(name, srcs, land_shapes, builds, carry, after=(), lands=None):
    n = len(srcs)
    na, nc = len(after), len(carry)

    def body(*refs):
        src, land = refs[:n], refs[n:2 * n]
        kept = refs[2 * n + na:2 * n + na + nc]
        outs = refs[2 * n + na + nc:]
        send, recv, passed = outs[:n], outs[n:2 * n], outs[4 * n:]
        for t in range(n):
            for out_cp, _ in builds[t](src[t], land[t], send[t], recv[t]):
                out_cp.start()
        for a, b in zip(kept, passed):
            b[...] = a[...]

    if lands is None:
        lands = [lax.empty(s.shape, s.dtype) for s in land_shapes]
    lands = [_in_hbm(a) for a in lands]
    sems = [pltpu.SemaphoreType.DMA((NSEM,))] * (2 * n)
    thru = [pltpu.HBM(a.shape, a.dtype) for a in list(srcs) + lands]
    vm = pl.BlockSpec(memory_space=pltpu.VMEM)
    res = pl.pallas_call(
        body, name=name, out_shape=sems + thru + [jax.ShapeDtypeStruct(a.shape, a.dtype) for a in carry],
        in_specs=[HBM_SPEC] * (2 * n) + [pl.BlockSpec(memory_space=pl.ANY)] * na + [vm] * nc,
        out_specs=[SEM_SPEC] * (2 * n) + [HBM_SPEC] * (2 * n) + [vm] * nc,
        input_output_aliases={i: 2 * n + i for i in range(2 * n)},
        compiler_params=pltpu.CompilerParams(has_side_effects=EFFECT),
    )(*[_in_hbm(a) for a in srcs], *lands, *after, *carry)
    flight = [(res[2 * n + t], res[3 * n + t], res[t], res[n + t]) for t in range(n)]
    return flight, list(res[4 * n:])


def _split_wait(name, flight, builds, after):
    m = len(flight)
    na = len(after)

    def body(*refs):
        src, land, send, recv = refs[:m], refs[m:2 * m], refs[2 * m:3 * m], refs[3 * m:4 * m]
        for t in range(m):
            for out_cp, in_cp in builds[t](src[t], land[t], send[t], recv[t]):
                out_cp.wait_send()
                in_cp.wait_recv()

    ops = [f[0] for f in flight] + [f[1] for f in flight] + [f[2] for f in flight] + [f[3] for f in flight]
    res = pl.pallas_call(
        body, name=name, out_shape=[pltpu.HBM(a.shape, a.dtype) for a in ops[:2 * m]],
        in_specs=[HBM_SPEC] * (2 * m) + [SEM_SPEC] * (2 * m) + [pl.BlockSpec(memory_space=pl.ANY)] * na,
        out_specs=[HBM_SPEC] * (2 * m), input_output_aliases={i: i for i in range(2 * m)},
        compiler_params=pltpu.CompilerParams(has_side_effects=EFFECT),
    )(*ops, *after)
    return res[:m], res[m:2 * m]


def _weight_copies(src, land, send, recv):
    x, y, c = _pos()
    chip = 2 * x + y
    return [(_rcopy(_half(src, c), _half(land.at[chip], c), send, recv, j, (px, py, c)),
             _rcopy(_half(src, c), _half(land.at[2 * px + py], c), send, recv, j, (px, py, c)))
            for j, (px, py) in enumerate(_other_chips(x, y))]


NDIRECT = NDEV - 1


def _direct_grad_copies(src, land, send, recv):
    x, y, c = _pos()
    out, arrive = [], []
    for j, (px, py) in enumerate(_other_chips(x, y)):
        for hc in range(2):
            out.append(_rcopy(_half(src.at[2 * px + py], hc), land.at[2 * j + c], send, recv, 2 * j + hc, (px, py, hc),
                              k_recv=2 * j + c))
            arrive.append(_rcopy(_half(src.at[2 * px + py], hc), land.at[2 * j + hc], send, recv, 2 * j + hc, (px, py, hc)))
    own = _rcopy(_half(src.at[2 * x + y], 1 - c), land.at[NDIRECT - 1], send, recv, NDIRECT - 1, (x, y, 1 - c))
    return list(zip(out, arrive)) + [(own, own)]


def _pair_weight_copies(src, land, send, recv):
    x, y, c = _pos()
    sib = (x, y, 1 - c)
    cps = []
    for j, (px, py) in enumerate(_other_chips(x, y)):
        mine, theirs = _half(land.at[2 * px + py], c), _half(land.at[2 * px + py], 1 - c)
        cps.append((_rcopy(mine, mine, send, recv, j, sib), _rcopy(theirs, theirs, send, recv, j, sib)))
    own = _rcopy(src, land.at[2 * x + y], send, recv, NLINK, sib)
    return cps + [(own, own)]


RS_ROWS = 256


def _chip_add(own, landed, pos_arr, name):
    nl, r2, cw = landed.shape
    rows = min(RS_ROWS, r2)
    nr = r2 // rows

    def body(s_ref, h_ref, q_ref, o_ref):
        acc = h_ref[...].astype(f32)
        for j in range(nl):
            acc = acc + q_ref[j].astype(f32)
        o_ref[...] = acc

    gs = pltpu.PrefetchScalarGridSpec(
        num_scalar_prefetch=1, grid=(nr,),
        in_specs=[pl.BlockSpec((None, rows, cw), lambda j, s: (s[0], s[1] * nr + j, 0)),
                  pl.BlockSpec((nl, rows, cw), lambda j, s: (0, j, 0))],
        out_specs=pl.BlockSpec((rows, cw), lambda j, s: (s[1] * nr + j, 0)))
    return pl.pallas_call(body, name=name, grid_spec=gs, out_shape=jax.ShapeDtypeStruct((2 * r2, cw), f32),
                          compiler_params=_cp(("arbitrary",)))(pos_arr, own, landed)


def _pair_gather_copies(src, land, send, recv):
    x, y, c = _pos()
    sib = (x, y, 1 - c)
    return [(_rcopy(_half(land, c), _half(land, c), send, recv, 0, sib),
             _rcopy(_half(land, 1 - c), _half(land, 1 - c), send, recv, 0, sib))]


def _adamw_math(w, g, m, v):
    m = B1 * m + (1.0 - B1) * g
    v = B2 * v + (1.0 - B2) * jnp.square(g)
    m_hat = m / (1.0 - B1 ** STEP)
    v_hat = v / (1.0 - B2 ** STEP)
    return -LR * (m_hat / (jnp.sqrt(v_hat) + AEPS) + WD * w), m, v


ADAM_BLOCK = 512 * 1024


def _adamw(w, g, m, v, name, after=(), landed=True):
    r, cw = w.shape
    na = len(after)

    def body(w_ref, g_ref, m_ref, v_ref, *rest):
        outs = rest[na:]
        g = g_ref[...]
        if landed:
            outs[0][...] = g
        outs[-3][...], outs[-2][...], outs[-1][...] = _adamw_math(w_ref[...], g, m_ref[...], v_ref[...])

    rows = max(k for k in range(SUBLANES, ADAM_BLOCK // cw + 1, SUBLANES) if r % k == 0)
    spec = pl.BlockSpec((rows, cw), lambda i: (i, 0))
    nout = 4 if landed else 3
    res = pl.pallas_call(body, name=name, grid=(r // rows,), in_specs=[spec] * 4 + [pl.BlockSpec(memory_space=pl.ANY)] * na,
                         out_specs=[spec] * nout, out_shape=[jax.ShapeDtypeStruct((r, cw), f32)] * nout,
                         compiler_params=_cp(("arbitrary",)))(w, g, m, v, *after)
    return list(res) if landed else [g, *res]


SMALL = (("b_ada", None, PAYW), ("g_attn_pre", OFF_G_ATTN_PRE, D), ("g_attn_post", OFF_G_ATTN_POST, D), ("sink_a", OFF_SINK, 8),
         ("g_mix_a", OFF_G_MIX_A, AQ), ("g_mix_b", OFF_G_MIX_B, BW), ("g_mlp_pre", OFF_G_MLP_PRE, D), ("g_mlp_post", OFF_G_MLP_POST, D))


def _adamw_small(small, gb, params):
    n = len(SMALL)

    def body(*refs):
        small_ref, gb_ref = refs[:2]
        wmv = refs[2:2 + 3 * n]
        loss_ref = refs[2 + 3 * n]
        outs = refs[3 + 3 * n:]
        loss_ref[...] = small_ref[:, OFF_LOSS:OFF_LOSS + 1] * (0.5 / D)
        for i, (_, off, width) in enumerate(SMALL):
            g = gb_ref[...] if off is None else small_ref[:, off:off + width]
            w_ref, m_ref, v_ref = wmv[3 * i:3 * i + 3]
            outs[4 * i][...] = g
            outs[4 * i + 1][...], outs[4 * i + 2][...], outs[4 * i + 3][...] = _adamw_math(w_ref[...], g, m_ref[...], v_ref[...])

    vm = pl.BlockSpec(memory_space=pltpu.VMEM)
    out_shape = [jax.ShapeDtypeStruct((1, 1), f32)]
    for _, _, width in SMALL:
        out_shape += [jax.ShapeDtypeStruct((1, width), f32)] * 4
    flat = [a for wmv in params for a in wmv]
    res = pl.pallas_call(body, name="adamw_small", in_specs=[vm] * (2 + 3 * n), out_specs=[vm] * len(out_shape),
                         out_shape=out_shape)(small, gb, *flat)
    return res[0], {name: res[1 + 4 * i:5 + 4 * i] for i, (name, _, _) in enumerate(SMALL)}


def kernel(x, c, positions, w_ada, b_ada, g_attn_pre, g_attn_post, w_in, sink_a, g_mix_a, g_mix_b, w_out, g_mlp_pre, g_mlp_post, w_up, w_down, loss_target, m_w_ada, m_b_ada, m_g_attn_pre, m_g_attn_post, m_w_in, m_sink_a, m_g_mix_a, m_g_mix_b, m_w_out, m_g_mlp_pre, m_g_mlp_post, m_w_up, m_w_down, v_w_ada, v_b_ada, v_g_attn_pre, v_g_attn_post, v_w_in, v_sink_a, v_g_mix_a, v_g_mix_b, v_w_out, v_g_mlp_pre, v_g_mlp_post, v_w_up, v_w_down):
    given = dict(w_ada=w_ada, b_ada=b_ada, g_attn_pre=g_attn_pre, g_attn_post=g_attn_post, w_in=w_in, sink_a=sink_a, g_mix_a=g_mix_a,
                 g_mix_b=g_mix_b, w_out=w_out, g_mlp_pre=g_mlp_pre, g_mlp_post=g_mlp_post, w_up=w_up, w_down=w_down)
    moms = dict(w_ada=(m_w_ada, v_w_ada), b_ada=(m_b_ada, v_b_ada), g_attn_pre=(m_g_attn_pre, v_g_attn_pre),
                g_attn_post=(m_g_attn_post, v_g_attn_post), w_in=(m_w_in, v_w_in), sink_a=(m_sink_a, v_sink_a),
                g_mix_a=(m_g_mix_a, v_g_mix_a), g_mix_b=(m_g_mix_b, v_g_mix_b), w_out=(m_w_out, v_w_out),
                g_mlp_pre=(m_g_mlp_pre, v_g_mlp_pre), g_mlp_post=(m_g_mlp_post, v_g_mlp_post), w_up=(m_w_up, v_w_up),
                w_down=(m_w_down, v_w_down))
    order = ["w_ada", "b_ada", "g_attn_pre", "g_attn_post", "w_in", "sink_a", "g_mix_a", "g_mix_b", "w_out", "g_mlp_pre",
             "g_mlp_post", "w_up", "w_down"]
    xi, yi, ci = _pos()
    chip = 2 * xi + yi

    pos_arr = jnp.stack([chip, ci]).astype(jnp.int32)
    big = ("w_in", "w_out", "w_up", "w_down")

    gathered = [jax.ShapeDtypeStruct((NCHIP,) + given[n].shape[1:], bf16) for n in big]
    (flight_c, *flight_in), (inv_lane,) = _split_start(
        "weights_start_first", [c, w_in[0].astype(bf16)], [jax.ShapeDtypeStruct((NDEV, BL, D), f32), gathered[0]],
        [_small_copies, _weight_copies], [_inv_lane()])
    inv_lane, rest = lax.optimization_barrier((inv_lane, [given[n][0] for n in big[1:]]))
    tabs = _rope_tables(positions.reshape(BL * SEQ, 1), inv_lane)
    rest = [w.astype(bf16) for w in rest]
    b_cols = lax.dynamic_slice(b_ada, (0, chip * ADAW), (1, ADAW))
    (c_own,), (c_all,) = _split_wait("cond_wait", [flight_c], [_small_copies], (*tabs, *rest))
    mod, cond_all = _ada_fwd(c_own, c_all, w_ada[0], b_cols)

    srcs, lands = _split_wait("weights_wait_first", flight_in, [_weight_copies], (mod,))
    cross, (mod,) = _split_start("weights_pair_start_first", srcs, None, [_pair_weight_copies], [mod], lands=lands)
    flight_rest, (mod,) = _split_start("weights_start_rest", rest, gathered[1:], [_weight_copies] * 3, [mod])
    _, (win_g,) = _split_wait("weights_pair_wait_first", cross, [_pair_weight_copies], (mod,))
    mod = mod.reshape(BL, NMOD, D)

    def later_weights(after, carry):
        srcs, lands = _split_wait("weights_wait_rest", flight_rest, [_weight_copies] * 3, after)
        fl, (carry,) = _split_start("weights_pair_start_rest", srcs, None, [_pair_weight_copies] * 3, [carry], lands=lands)
        _, (wout_g,) = _split_wait("weights_pair_wait_out", fl[:1], [_pair_weight_copies], ())

        def mlp_weights(after):
            _, (wup_g, wdn_g) = _split_wait("weights_pair_wait_mlp", fl[1:], [_pair_weight_copies] * 2, after)
            return wup_g, wdn_g.reshape(DFF, D)

        return wout_g.reshape(D, D), mlp_weights, carry

    waiting, pending = {}, {}

    def send_grads(carry):
        names = list(waiting)
        slabs = [waiting.pop(n) for n in names]
        lands = [jax.ShapeDtypeStruct((NDIRECT, s.shape[1] // 2, s.shape[2]), bf16) for s in slabs]
        fl, (carry,) = _split_start("grad_start_" + names[-1], slabs, lands, [_direct_grad_copies] * len(names), [carry])
        for n, f in zip(names, fl):
            pending[n] = [f]
        return carry

    def grad_ready(name, g, carry):
        waiting[name] = g if g.ndim == 3 else g.reshape(NCHIP, g.shape[0] // NCHIP, g.shape[1])
        return send_grads(carry) if name == "w_out" else carry

    grad_x, accs = _local_step(x, tabs, mod, loss_target, win_g, later_weights, grad_ready,
                               g_attn_pre, g_attn_post, sink_a, g_mix_a, g_mix_b, g_mlp_pre, g_mlp_post)

    grads, out = {}, {}

    def update(n, after=()):
        tr = (lambda a: a.T) if n == "w_in" else (lambda a: a)
        res = _adamw(tr(given[n][0]), tr(grads[n]), tr(moms[n][0][0]), tr(moms[n][1][0]), "adamw_" + n, after,
                     landed=n != "w_ada")
        out[n] = tuple(tr(a)[None] for a in res)
        return res[3]

    def finish(names, after, first=()):
        fl = sum((pending[n] for n in names), [])
        halves, landed = _split_wait("grad_wait_" + names[0], fl, [_direct_grad_copies] * len(names), after)
        flights, token = [], jnp.zeros((SUBLANES, LANES), f32)
        for h, q, n in zip(halves, landed, names):
            full = _chip_add(h, q, pos_arr, "grad_chip_sum_" + n)
            flights.append(_split_start("grad_gather_start_" + n, [token], None, [_pair_gather_copies], [], lands=[full])[0])
            token = flights[-1][0][0]
        last = [update(n, (token,)) for n in first]
        for n, fl1 in zip(names, flights):
            after = tuple(last) if last else () if fl1 is flights[-1] else (token,)
            _, (grads[n],) = _split_wait("grad_gather_wait_" + n, fl1, [_pair_gather_copies], after)
            last = [update(n)]
        return last[0]

    fl_small, (cond_all,) = _split_start("small_start", [_small_pack(accs)], [jax.ShapeDtypeStruct((NDEV, PAY_ROWS, PAYW), f32)],
                                         [_small_copies], [cond_all])
    cond_all = send_grads(cond_all)
    last = finish(("w_down", "w_up", "w_out"), (cond_all,))
    (pay,), (landed,) = _split_wait("small_wait", fl_small, [_small_copies], (last,))
    grads["w_ada"], gb, small = _small_sum(pay, landed, cond_all)
    finish(("w_in",), (small,), first=("w_ada",))
    loss, res = _adamw_small(small, gb, [(given[n], moms[n][0], moms[n][1]) for n, _, _ in SMALL])
    for n, _, _ in SMALL:
        out[n] = tuple(res[n])
    return (loss.reshape(()), grad_x, *[out[n][0] for n in order], *[out[n][1] for n in order],
            *[out[n][2] for n in order], *[out[n][3] for n in order])
```

```python
import numpy as np
import jax
import jax.numpy as jnp
from jax import lax
from jax.experimental import pallas as pl
from jax.experimental.pallas import tpu as pltpu

f32 = jnp.float32
bf16 = jnp.bfloat16
MESH = pl.DeviceIdType.MESH

D = 1024
SEQ = 2048
BL = 2
HD = 64
AQ = 512
AKV = 128
BW = 512
INW = 2304
DFF = 4096
NMOD = 6
ROT = 16
THETA = 500000.0
EPS = 1e-6
NEG = -1e30
BLK = 128
TM = 512
NJ = SEQ // TM
LANES = 128
SUBLANES = 8
NHEAD = AQ // HD
QSCALE = HD ** -0.5
NCHIP = 4
NDEV = 8
VMEM_LIMIT = 56 << 20

LR, B1, B2, AEPS, WD, STEP = 0.001, 0.9, 0.999, 1e-08, 0.01, 10

OFF_G_ATTN_PRE, OFF_G_ATTN_POST, OFF_G_MIX_A, OFF_G_MIX_B = 0, 1024, 2048, 2560
OFF_G_MLP_PRE, OFF_G_MLP_POST, OFF_SINK, OFF_LOSS = 3072, 4096, 5120, 5248
PAYW = NMOD * D


def _cp(sem=None):
    return pltpu.CompilerParams(dimension_semantics=sem, vmem_limit_bytes=VMEM_LIMIT)


def _dot(a, b):
    return jnp.dot(a, b, preferred_element_type=f32)


def _dot_nt(a, b):
    return lax.dot_general(a, b, (((1,), (1,)), ((), ())), preferred_element_type=f32)


def _dot_tn(a, b):
    return lax.dot_general(a, b, (((0,), (0,)), ((), ())), preferred_element_type=f32)


def _rms(x):
    r = lax.rsqrt(jnp.mean(x * x, axis=-1, keepdims=True) + EPS)
    return x * r, r


def _rms_bwd(dy, y, r):
    return r * (dy - y * jnp.mean(dy * y, axis=-1, keepdims=True))


def _colsum(v):
    return jnp.sum(v, axis=0, keepdims=True)


def _rope(p, c, s1, s2):
    outs = []
    for c0 in range(0, p.shape[1], LANES):
        pc = p[:, c0:c0 + LANES]
        outs.append(pc * c + pltpu.roll(pc, LANES - ROT // 2, 1) * s1 + pltpu.roll(pc, ROT // 2, 1) * s2)
    return outs[0] if len(outs) == 1 else jnp.concatenate(outs, axis=1)


def _rope_t(g, c, s1, s2):
    outs = []
    for c0 in range(0, g.shape[1], LANES):
        gc = g[:, c0:c0 + LANES]
        outs.append(gc * c + pltpu.roll(gc * s1, ROT // 2, 1) + pltpu.roll(gc * s2, LANES - ROT // 2, 1))
    return outs[0] if len(outs) == 1 else jnp.concatenate(outs, axis=1)


def _perm_store(val, scr, out_ref, d):
    nc = val.shape[1] // LANES
    for c in range(nc):
        scr[c] = val[:, LANES * c:LANES * (c + 1)]
    for c in range(nc):
        for r in range(d):
            out_ref[r, :, LANES * c:LANES * (c + 1)] = scr[c, pl.ds(r, TM // d, stride=d), :].astype(out_ref.dtype)


def _perm_load(in_ref, scr, d):
    nc = in_ref.shape[-1] // LANES
    for c in range(nc):
        for r in range(d):
            scr[c, pl.ds(r, TM // d, stride=d), :] = in_ref[r, :, LANES * c:LANES * (c + 1)].astype(f32)
    return jnp.concatenate([scr[c] for c in range(nc)], axis=1)


def _per_query_head(kv):
    r = pltpu.roll(kv, HD, 1)
    lo = lax.broadcasted_iota(jnp.int32, kv.shape, 1) < HD
    return jnp.concatenate([jnp.where(lo, kv, r), jnp.where(lo, r, kv)], axis=1)


def _per_kv_head(g):
    g0, g1 = g[:, :LANES] + g[:, LANES:2 * LANES], g[:, 2 * LANES:3 * LANES] + g[:, 3 * LANES:]
    lo = lax.broadcasted_iota(jnp.int32, g0.shape, 1) < HD
    return jnp.where(lo, g0 + pltpu.roll(g0, HD, 1), g1 + pltpu.roll(g1, HD, 1))


def _tok(w):
    return pl.BlockSpec((None, TM, w), lambda b, j: (b, j, 0))


def _perm_spec(d, w):
    return pl.BlockSpec((None, d, TM // d, w), lambda b, j: (b, 0, j, 0))


def _full(shape):
    n = len(shape)
    return pl.BlockSpec(shape, lambda b, j: (0,) * n)


MOD_SPEC = pl.BlockSpec((None, NMOD, D), lambda b, j: (b, 0, 0))
ACCB_SPEC = pl.BlockSpec((None, SUBLANES, D), lambda b, j: (b, 0, 0))
ACCG_SPEC = pl.BlockSpec((SUBLANES, D), lambda b, j: (0, 0))
ACC_SHAPES = [jax.ShapeDtypeStruct((BL, SUBLANES, D), f32), jax.ShapeDtypeStruct((SUBLANES, D), f32)]


def _acc_init(accb_ref, accg_ref):
    b, j = pl.program_id(0), pl.program_id(1)

    @pl.when(j == 0)
    def _():
        accb_ref[...] = jnp.zeros_like(accb_ref)

    @pl.when((b == 0) & (j == 0))
    def _():
        accg_ref[...] = jnp.zeros_like(accg_ref)


def _rope_tables(pos_col, inv_lane):
    def body(p_ref, inv_ref, c_ref, s1_ref, s2_ref):
        ang = p_ref[...].astype(f32) * inv_ref[...]
        j = lax.broadcasted_iota(jnp.int32, (TM, LANES), 1) % HD
        cs, sn = jnp.cos(ang), jnp.sin(ang)
        c_ref[...] = jnp.where(j < ROT, cs, 1.0)
        s1_ref[...] = jnp.where(j < ROT // 2, -sn, 0.0)
        s2_ref[...] = jnp.where((j >= ROT // 2) & (j < ROT), sn, 0.0)

    n = BL * SEQ // TM
    return pl.pallas_call(
        body, name="rope_tables", grid=(n,),
        in_specs=[pl.BlockSpec((TM, 1), lambda i: (i, 0)), pl.BlockSpec((1, LANES), lambda i: (0, 0))],
        out_specs=[pl.BlockSpec((TM, LANES), lambda i: (i, 0))] * 3,
        out_shape=[jax.ShapeDtypeStruct((BL * SEQ, LANES), f32)] * 3,
    )(pos_col, inv_lane)


def _attn_in(x, mod, g_pre, w_in, tc, ts1, ts2):
    def body(x_ref, mod_ref, g_ref, wg_ref, c_ref, s1_ref, s2_ref,
             h_ref, qa_ref, ka_ref, va_ref, q1_ref, k1_ref, v1_ref, q4_ref, k4_ref, v4_ref, q16_ref, k16_ref, v16_ref,
             w_ref, scr):
        @pl.when((pl.program_id(0) == 0) & (pl.program_id(1) == 0))
        def _():
            w_ref[...] = jnp.concatenate([wg_ref[s] for s in range(NCHIP)], axis=1)

        xn, _ = _rms(x_ref[...])
        h = (xn * g_ref[...]) * (1.0 + mod_ref[1:2, :]) + mod_ref[0:1, :]
        hb = h.astype(bf16)
        h_ref[...] = hb
        proj = _dot(hb, w_ref[...])
        c, s1, s2 = c_ref[...], s1_ref[...], s2_ref[...]
        o1, o2, o3, o4, o5 = AQ, AQ + AKV, AQ + 2 * AKV, AQ + 2 * AKV + BW, AQ + 2 * AKV + 2 * BW
        qa_ref[...] = (_rope(proj[:, :o1], c, s1, s2) * QSCALE).astype(bf16)
        ka_ref[...] = _per_query_head(_rope(proj[:, o1:o2], c, s1, s2)).astype(bf16)
        va_ref[...] = _per_query_head(proj[:, o2:o3]).astype(bf16)
        qb = _rope(proj[:, o3:o4], c, s1, s2) * QSCALE
        kb = _rope(proj[:, o4:o5], c, s1, s2)
        vb = proj[:, o5:]
        for val, r1, r4, r16 in ((qb, q1_ref, q4_ref, q16_ref), (kb, k1_ref, k4_ref, k16_ref), (vb, v1_ref, v4_ref, v16_ref)):
            r1[...] = val.astype(bf16)
            _perm_store(val, scr, r4, 4)
            _perm_store(val, scr, r16, 16)

    nat = lambda w: jax.ShapeDtypeStruct((BL, SEQ, w), bf16)
    p4 = jax.ShapeDtypeStruct((BL, 4, SEQ // 4, BW), bf16)
    p16 = jax.ShapeDtypeStruct((BL, 16, SEQ // 16, BW), bf16)
    return pl.pallas_call(
        body, name="attn_in", grid=(BL, NJ),
        in_specs=[_tok(D), MOD_SPEC, _full((1, D)), _full((NCHIP, D, INW // NCHIP)), _tok(LANES), _tok(LANES), _tok(LANES)],
        out_specs=([_tok(D), _tok(AQ), _tok(2 * AKV), _tok(2 * AKV)] + [_tok(BW)] * 3 + [_perm_spec(4, BW)] * 3 + [_perm_spec(16, BW)] * 3
                   + [_full((D, INW))]),
        out_shape=[nat(D), nat(AQ), nat(2 * AKV), nat(2 * AKV)] + [nat(BW)] * 3 + [p4] * 3 + [p16] * 3
                  + [jax.ShapeDtypeStruct((D, INW), bf16)],
        scratch_shapes=[pltpu.VMEM((BW // LANES, TM, LANES), f32)],
        compiler_params=_cp(("arbitrary", "arbitrary")),
    )(x, mod, g_pre, w_in, tc, ts1, ts2)


def _kv_cat(cur_ref, prev_ref, p, cache):
    key = (id(cur_ref), p)
    if key not in cache:
        sl = slice(LANES * p, LANES * (p + 1))
        cache[key] = cur_ref[:, sl] if prev_ref is None else jnp.concatenate([prev_ref[:, sl], cur_ref[:, sl]], axis=0)
    return cache[key]


def _lane_half(a, hh):
    lo = lax.broadcasted_iota(jnp.int32, a.shape, 1) < HD
    return jnp.where(lo, a, jnp.zeros_like(a)) if hh == 0 else jnp.where(lo, jnp.zeros_like(a), a)


ATT_UNITS = 4


def _att_units(nb):
    return ATT_UNITS if nb == 1 else min(ATT_UNITS, nb)


def _attn_specs(n, nb, descending):
    u = _att_units(nb)
    if nb == 1:
        return (lambda ww: pl.BlockSpec((u, BLK, ww), lambda a, i: (a, 0, 0))), None, (n // u, 1)
    steps = nb // u
    at = (lambda i: steps - 1 - i) if descending else (lambda i: i)
    cur = lambda ww: pl.BlockSpec((None, u * BLK, ww), lambda a, i: (a, at(i), 0))
    prev = lambda ww: pl.BlockSpec((None, BLK, ww), lambda a, i: (a, jnp.maximum(u * at(i) - 1, 0), 0))
    return cur, prev, (n, steps)


def _attn_fwd(q, k, v, sink, *, max_dist, o_dtype, name):
    n, l, w = q.shape
    wk = k.shape[-1]
    nb = l // BLK
    has_sink = sink is not None

    def body(*refs):
        sink_ref = None
        if has_sink:
            sink_ref, refs = refs[0], refs[1:]
        if nb > 1:
            q_ref, kc_ref, kp_ref, vc_ref, vp_ref, o_ref, lse_ref = refs[:7]
            first = pl.program_id(1) == 0
            for u in range(_att_units(nb)):
                rows, before = pl.ds(BLK * u, BLK), pl.ds(BLK * (u - 1), BLK)
                unit(q_ref.at[rows, :], kc_ref.at[rows, :], kp_ref if u == 0 else kc_ref.at[before, :],
                     vc_ref.at[rows, :], vp_ref if u == 0 else vc_ref.at[before, :], o_ref.at[rows, :], lse_ref.at[rows, :],
                     jnp.logical_not(first) if u == 0 else True, sink_ref, *refs[7:])
        else:
            q_ref, kc_ref, vc_ref, o_ref, lse_ref = refs[:5]
            for u in range(_att_units(nb)):
                unit(q_ref.at[u], kc_ref.at[u], None, vc_ref.at[u], None, o_ref.at[u], lse_ref.at[u], None, sink_ref, *refs[5:])

    def unit(q_ref, kc_ref, kp_ref, vc_ref, vp_ref, o_ref, lse_ref, has_prev, sink_ref, sscr, pscr, dscr):
        qi = lax.broadcasted_iota(jnp.int32, (BLK, BLK), 0)
        kj = lax.broadcasted_iota(jnp.int32, (BLK, BLK), 1)
        tri = kj <= qi
        eye = kj == qi
        cache = {}
        for p in range(w // LANES):
            qpair = q_ref[:, LANES * p:LANES * (p + 1)]
            kcat = _kv_cat(kc_ref, kp_ref, p // share, cache)
            for hh in range(2):
                s = _dot_nt(_lane_half(qpair, hh), kcat)
                if nb > 1:
                    sp = s[:, :BLK] if has_prev is True else jnp.where(has_prev, s[:, :BLK], NEG)
                    sscr[2 * p + hh] = jnp.where(tri, s[:, BLK:], sp)
                    if diag:
                        dscr[2 * p + hh] = jnp.where(eye, sp, NEG)
                else:
                    sscr[2 * p + hh] = jnp.where(tri, s, NEG)
        lane = lax.broadcasted_iota(jnp.int32, (BLK, LANES), 1)
        lse_all = jnp.zeros((BLK, LANES), f32)
        for p in range(w // LANES):
            for hh in range(2):
                h = 2 * p + hh
                comb = sscr[h]
                if diag:
                    dtile = dscr[h]
                    m = jnp.max(jnp.maximum(comb, dtile), axis=-1, keepdims=True)
                else:
                    m = jnp.max(comb, axis=-1, keepdims=True)
                if has_sink:
                    sk = sink_ref[0, h]
                    m = jnp.maximum(m, sk)
                e = jnp.exp(comb - m)
                if diag:
                    ed = jnp.exp(dtile - m)
                    den = jnp.sum(e + ed, axis=-1, keepdims=True)
                else:
                    den = jnp.sum(e, axis=-1, keepdims=True)
                if has_sink:
                    den = den + jnp.exp(sk - m)
                inv = 1.0 / den
                if nb > 1:
                    pscr[h, :, :BLK] = (jnp.where(tri, ed if diag else 0.0, e) * inv).astype(bf16)
                    pscr[h, :, BLK:] = (jnp.where(tri, e, 0.0) * inv).astype(bf16)
                else:
                    pscr[h] = (e * inv).astype(bf16)
                lse_all = jnp.where(lane == h, jnp.broadcast_to(m + jnp.log(den), (BLK, LANES)), lse_all)
        lse_ref[...] = lse_all
        for p in range(w // LANES):
            vcat = _kv_cat(vc_ref, vp_ref, p // share, cache)
            o_ref[:, LANES * p:LANES * (p + 1)] = (_dot(pscr[2 * p], _lane_half(vcat, 0))
                                                   + _dot(pscr[2 * p + 1], _lane_half(vcat, 1))).astype(o_ref.dtype)

    assert max_dist in (BLK - 1, BLK) and w % wk == 0
    share = w // wk
    diag = nb > 1 and max_dist == BLK
    cur, prev, grid = _attn_specs(n, nb, False)
    in_specs = [cur(w), cur(wk)] + ([prev(wk)] if nb > 1 else []) + [cur(wk)] + ([prev(wk)] if nb > 1 else [])
    args = [q, k] + ([k] if nb > 1 else []) + [v] + ([v] if nb > 1 else [])
    if has_sink:
        in_specs = [pl.BlockSpec(memory_space=pltpu.SMEM)] + in_specs
        args = [sink] + args
    return pl.pallas_call(
        body, name=name, grid=grid, in_specs=in_specs,
        out_specs=[cur(w), cur(LANES)],
        out_shape=[jax.ShapeDtypeStruct((n, l, w), o_dtype), jax.ShapeDtypeStruct((n, l, LANES), f32)],
        scratch_shapes=[pltpu.VMEM((w // HD, BLK, BLK), f32), pltpu.VMEM((w // HD, BLK, 2 * BLK if nb > 1 else BLK), bf16),
                        pltpu.VMEM((w // HD if diag else 1, BLK, BLK), f32)],
        compiler_params=_cp(("arbitrary", "arbitrary")),
    )(*args)


def _attn_bwd(q, k, v, do, delta, lse, sink, *, max_dist, name):
    n, l, w = q.shape
    wk = k.shape[-1]
    nb = l // BLK
    has_sink = sink is not None

    def body(*refs):
        sink_ref = dsink_ref = ck = cv = None
        if has_sink:
            sink_ref, refs = refs[0], refs[1:]
        nin = 8 if nb > 1 else 6
        ins, rest = refs[:nin], refs[nin:]
        if has_sink:
            dq_ref, dk_ref, dv_ref, dsink_ref = rest[:4]
            rest = rest[4:]
        else:
            dq_ref, dk_ref, dv_ref = rest[:3]
            rest = rest[3:]
        step = pl.program_id(1)
        if has_sink:
            @pl.when((pl.program_id(0) == 0) & (step == 0))
            def _():
                dsink_ref[...] = jnp.zeros_like(dsink_ref)

        if nb > 1:
            q_ref, kc_ref, kp_ref, vc_ref, vp_ref, do_ref, delta_ref, lse_ref = ins
            ck, cv = rest[:2]

            @pl.when(step == 0)
            def _():
                ck[...] = jnp.zeros_like(ck)
                cv[...] = jnp.zeros_like(cv)

            last = step == nb // _att_units(nb) - 1
            for u in reversed(range(_att_units(nb))):
                rows, before = pl.ds(BLK * u, BLK), pl.ds(BLK * (u - 1), BLK)
                unit(q_ref.at[rows, :], kc_ref.at[rows, :], kp_ref if u == 0 else kc_ref.at[before, :],
                     vc_ref.at[rows, :], vp_ref if u == 0 else vc_ref.at[before, :], do_ref.at[rows, :],
                     delta_ref.at[rows, :], lse_ref.at[rows, :], dq_ref.at[rows, :], dk_ref.at[rows, :], dv_ref.at[rows, :],
                     jnp.logical_not(last) if u == 0 else True, sink_ref, dsink_ref, ck, cv, *rest[2:])
        else:
            q_ref, kc_ref, vc_ref, do_ref, delta_ref, lse_ref = ins
            for u in range(_att_units(nb)):
                unit(q_ref.at[u], kc_ref.at[u], None, vc_ref.at[u], None, do_ref.at[u], delta_ref.at[u], lse_ref.at[u],
                     dq_ref.at[u], dk_ref.at[u], dv_ref.at[u], None, sink_ref, dsink_ref, None, None, *rest)

    def unit(q_ref, kc_ref, kp_ref, vc_ref, vp_ref, do_ref, delta_ref, lse_ref, dq_ref, dk_ref, dv_ref, has_prev,
             sink_ref, dsink_ref, ck, cv, sscr, dpscr, pscr, dsscr, dscr=None, ddscr=None):
        lane = lax.broadcasted_iota(jnp.int32, (BLK, LANES), 1)
        qi = lax.broadcasted_iota(jnp.int32, (BLK, BLK), 0)
        kj = lax.broadcasted_iota(jnp.int32, (BLK, BLK), 1)
        tri = kj <= qi
        eye = kj == qi
        cache = {}
        kp, vp = kp_ref, vp_ref
        for p in range(w // LANES):
            sl = slice(LANES * p, LANES * (p + 1))
            qpair, dopair = q_ref[:, sl], do_ref[:, sl]
            kcat, vcat = _kv_cat(kc_ref, kp, p // share, cache), _kv_cat(vc_ref, vp, p // share, cache)
            for hh in range(2):
                h = 2 * p + hh
                s = _dot_nt(_lane_half(qpair, hh), kcat)
                dp = _dot_nt(_lane_half(dopair, hh), vcat)
                if nb > 1:
                    sp = s[:, :BLK] if has_prev is True else jnp.where(has_prev, s[:, :BLK], NEG)
                    sscr[h] = jnp.where(tri, s[:, BLK:], sp)
                    dpscr[h] = jnp.where(tri, dp[:, BLK:], dp[:, :BLK])
                    if diag:
                        dscr[h] = jnp.where(eye, sp, NEG)
                        ddscr[h] = dp[:, :BLK]
                else:
                    sscr[h] = jnp.where(tri, s, NEG)
                    dpscr[h] = dp
        for p in range(w // LANES):
            for hh in range(2):
                h = 2 * p + hh
                lse_b = jnp.broadcast_to(lse_ref[:, h:h + 1], (BLK, BLK))
                delta = jnp.broadcast_to(delta_ref[:, h:h + 1], (BLK, BLK))
                pr = jnp.exp(sscr[h] - lse_b)
                ds = pr * (dpscr[h] - delta)
                if nb > 1:
                    if diag:
                        prd = jnp.exp(dscr[h] - lse_b)
                        dsd = prd * (ddscr[h] - delta)
                    else:
                        prd = dsd = 0.0
                    pscr[h, :, :BLK] = jnp.where(tri, prd, pr).astype(bf16)
                    pscr[h, :, BLK:] = jnp.where(tri, pr, 0.0).astype(bf16)
                    dsscr[h, :, :BLK] = jnp.where(tri, dsd, ds).astype(bf16)
                    dsscr[h, :, BLK:] = jnp.where(tri, ds, 0.0).astype(bf16)
                else:
                    pscr[h] = pr.astype(bf16)
                    dsscr[h] = ds.astype(bf16)
                if has_sink:
                    dsk = -jnp.sum(jnp.where(lane == 0, jnp.exp(sink_ref[0, h] - lse_b) * delta, 0.0), keepdims=True)
                    dsink_ref[h:h + 1, :] += jnp.broadcast_to(dsk, (1, LANES))
        for p in range(w // LANES):
            sl = slice(LANES * p, LANES * (p + 1))
            qpair, dopair = q_ref[:, sl], do_ref[:, sl]
            kcat = _kv_cat(kc_ref, kp, p // share, cache)
            dq_ref[:, sl] = _dot(dsscr[2 * p], _lane_half(kcat, 0)) + _dot(dsscr[2 * p + 1], _lane_half(kcat, 1))
            dk_pair = _dot_tn(dsscr[2 * p], _lane_half(qpair, 0)) + _dot_tn(dsscr[2 * p + 1], _lane_half(qpair, 1))
            dv_pair = _dot_tn(pscr[2 * p], _lane_half(dopair, 0)) + _dot_tn(pscr[2 * p + 1], _lane_half(dopair, 1))
            if nb > 1:
                dk_ref[:, sl] = dk_pair[BLK:] + ck[:, sl]
                dv_ref[:, sl] = dv_pair[BLK:] + cv[:, sl]
                ck[:, sl] = dk_pair[:BLK]
                cv[:, sl] = dv_pair[:BLK]
            else:
                dk_ref[:, sl] = dk_pair
                dv_ref[:, sl] = dv_pair

    assert max_dist in (BLK - 1, BLK) and w % wk == 0
    share = w // wk
    diag = nb > 1 and max_dist == BLK
    cur, prev, grid = _attn_specs(n, nb, True)
    in_specs = ([cur(w), cur(wk)] + ([prev(wk)] if nb > 1 else []) + [cur(wk)] + ([prev(wk)] if nb > 1 else [])
                + [cur(w), cur(LANES), cur(LANES)])
    args = [q, k] + ([k] if nb > 1 else []) + [v] + ([v] if nb > 1 else []) + [do, delta, lse]
    out_specs = [cur(w)] * 3
    out_shape = [jax.ShapeDtypeStruct((n, l, w), f32)] * 3
    if has_sink:
        in_specs = [pl.BlockSpec(memory_space=pltpu.SMEM)] + in_specs
        args = [sink] + args
        out_specs.append(pl.BlockSpec((NHEAD, LANES), lambda a, i: (0, 0)))
        out_shape.append(jax.ShapeDtypeStruct((NHEAD, LANES), f32))
    nh = w // HD
    scratch = [pltpu.VMEM((BLK, w), f32), pltpu.VMEM((BLK, w), f32)] if nb > 1 else []
    scratch += [pltpu.VMEM((nh, BLK, BLK), f32)] * 2 + [pltpu.VMEM((nh, BLK, 2 * BLK if nb > 1 else BLK), bf16)] * 2
    if diag:
        scratch += [pltpu.VMEM((nh, BLK, BLK), f32)] * 2
    return pl.pallas_call(
        body, name=name, grid=grid, in_specs=in_specs, out_specs=out_specs, out_shape=out_shape,
        scratch_shapes=scratch, compiler_params=_cp(("arbitrary", "arbitrary")),
    )(*args)


def _split2(x):
    hi = x.astype(bf16)
    return hi, (x - hi.astype(f32)).astype(bf16)


def _heads_to_lanes(xc, e):
    return sum(_dot(t, e) for t in _split2(xc))


def _lanes_to_heads(x, g):
    return sum(_dot(t, g) for t in _split2(x))


HEAD_EXPAND = (np.arange(LANES)[:, None] == np.arange(BW)[None, :] // HD).astype(np.float32)
HEAD_SUM = HEAD_EXPAND.T.copy()


def _branch_weights(l1_ref, l4_ref, l16_ref, scr):
    l4v = _perm_load(l4_ref, scr, 4)
    l16v = _perm_load(l16_ref, scr, 16)
    l1v = l1_ref[...]
    m = jnp.maximum(jnp.maximum(l1v, l4v), l16v)
    e1, e4, e16 = jnp.exp(l1v - m), jnp.exp(l4v - m), jnp.exp(l16v - m)
    z = e1 + e4 + e16
    return e1 / z, e4 / z, e16 / z


def _mix_out(oa, o1, l1, o4, l4, o16, l16, g_mix_a, g_mix_b, w_out, x, mod, g_post):
    def body(oa_ref, o1_ref, l1_ref, o4_ref, l4_ref, o16_ref, l16_ref, ga_ref, gb_ref, w_ref, x_ref, mod_ref, gp_ref, e_ref,
             x1_ref, y_ref, mixed_ref, ob_ref, scr):
        w1, w4, w16 = _branch_weights(l1_ref, l4_ref, l16_ref, scr)
        e = e_ref[...]
        x1w, x4w = _heads_to_lanes(w1, e), _heads_to_lanes(w4, e)
        ob = (x1w * o1_ref[...].astype(f32) + x4w * _perm_load(o4_ref, scr, 4)
              + (1.0 - x1w - x4w) * _perm_load(o16_ref, scr, 16))
        ob_ref[...] = ob
        oan, _ = _rms(oa_ref[...])
        obn, _ = _rms(ob)
        mixed = jnp.concatenate([oan * ga_ref[...], obn * gb_ref[...]], axis=1).astype(bf16)
        mixed_ref[...] = mixed
        y = _dot(mixed, w_ref[...])
        y_ref[...] = y
        yn, _ = _rms(y)
        x1_ref[...] = x_ref[...] + mod_ref[2:3, :] * (yn * gp_ref[...])

    nat = lambda w, dt: jax.ShapeDtypeStruct((BL, SEQ, w), dt)
    return pl.pallas_call(
        body, name="mix_out", grid=(BL, NJ),
        in_specs=[_tok(AQ), _tok(BW), _tok(LANES), _perm_spec(4, BW), _perm_spec(4, LANES), _perm_spec(16, BW),
                  _perm_spec(16, LANES), _full((1, AQ)), _full((1, BW)), _full((D, D)), _tok(D), MOD_SPEC, _full((1, D)),
                  _full((LANES, BW))],
        out_specs=[_tok(D), _tok(D), _tok(D), _tok(BW)],
        out_shape=[nat(D, f32), nat(D, f32), nat(D, bf16), nat(BW, f32)],
        scratch_shapes=[pltpu.VMEM((BW // LANES, TM, LANES), f32)],
        compiler_params=_cp(("arbitrary", "arbitrary")),
    )(oa, o1, l1, o4, l4, o16, l16, g_mix_a, g_mix_b, w_out, x, mod, g_post, jnp.asarray(HEAD_EXPAND, bf16))


def _mlp_up(x1, mod, g_pre, w_up):
    def body(x_ref, mod_ref, g_ref, w_ref, h_ref, u_ref, a_ref):
        xn, _ = _rms(x_ref[...])
        h = (xn * g_ref[...]) * (1.0 + mod_ref[4:5, :]) + mod_ref[3:4, :]
        hb = h.astype(bf16)
        h_ref[...] = hb
        for s in range(NCHIP):
            u = _dot(hb, w_ref[s])
            u_ref[:, D * s:D * (s + 1)] = u.astype(bf16)
            a_ref[:, D * s:D * (s + 1)] = jnp.square(jnp.maximum(u, 0.0)).astype(bf16)

    nat = lambda w: jax.ShapeDtypeStruct((BL, SEQ, w), bf16)
    return pl.pallas_call(
        body, name="mlp_up", grid=(BL, NJ),
        in_specs=[_tok(D), MOD_SPEC, _full((1, D)), _full((NCHIP, D, D))],
        out_specs=[_tok(D), _tok(DFF), _tok(DFF)], out_shape=[nat(D), nat(DFF), nat(DFF)],
        compiler_params=_cp(("arbitrary", "arbitrary")),
    )(x1, mod, g_pre, w_up)


def _mlp_down(a, w_down, x1, target, mod, g_post):
    def body(a_ref, w_ref, x_ref, t_ref, mod_ref, g_ref, gx_ref, dy_ref, accb_ref, accg_ref):
        _acc_init(accb_ref, accg_ref)
        y2 = _dot(a_ref[...], w_ref[...])
        yn, r = _rms(y2)
        g = g_ref[...]
        gt = mod_ref[5:6, :]
        n2 = yn * g
        err = x_ref[...] + gt * n2 - t_ref[...]
        gout = err * (1.0 / D)
        gx_ref[...] = gout
        dn2 = gout * gt
        dy_ref[...] = _rms_bwd(dn2 * g, yn, r).astype(bf16)
        accb_ref[0:1, :] += _colsum(gout * n2)
        accg_ref[0:1, :] += _colsum(dn2 * yn)
        accg_ref[1:2, :] += jnp.broadcast_to(jnp.sum(err * err, keepdims=True), (1, D))

    return pl.pallas_call(
        body, name="mlp_down", grid=(BL, NJ),
        in_specs=[_tok(DFF), _full((DFF, D)), _tok(D), _tok(D), MOD_SPEC, _full((1, D))],
        out_specs=[_tok(D), _tok(D), ACCB_SPEC, ACCG_SPEC],
        out_shape=[jax.ShapeDtypeStruct((BL, SEQ, D), f32), jax.ShapeDtypeStruct((BL, SEQ, D), bf16)] + ACC_SHAPES,
        compiler_params=_cp(("arbitrary", "arbitrary")),
    )(a, w_down, x1, target, mod, g_post)


def _mlp_bwd(dy2, u, w_down, w_up, x1, gx, mod, g_pre):
    def body(dy_ref, u_ref, wd_hbm, wu_hbm, x_ref, gx_ref, mod_ref, g_ref, du_ref, gx1_ref, accb_ref, accg_ref, wd, wu, sem):
        _acc_init(accb_ref, accg_ref)
        first = (pl.program_id(0) == 0) & (pl.program_id(1) == 0)
        c1 = pltpu.make_async_copy(wd_hbm, wd, sem.at[0])
        c2 = pltpu.make_async_copy(wu_hbm, wu, sem.at[1])

        @pl.when(first)
        def _():
            c1.start()
            c2.start()
            c1.wait()

        dy = dy_ref[...]
        for s in range(NCHIP):
            sl = slice(D * s, D * (s + 1))
            da = _dot_nt(dy, wd[sl, :])
            du_ref[:, sl] = (da * (2.0 * jnp.maximum(u_ref[:, sl].astype(f32), 0.0))).astype(bf16)

        @pl.when(first)
        def _():
            c2.wait()

        dh = jnp.zeros((TM, D), f32)
        for s in range(NCHIP):
            dh = dh + _dot_nt(du_ref[:, D * s:D * (s + 1)], wu[s])
        xn, r = _rms(x_ref[...])
        g = g_ref[...]
        n = xn * g
        dn = dh * (1.0 + mod_ref[4:5, :])
        gx1_ref[...] = gx_ref[...] + _rms_bwd(dn * g, xn, r)
        accb_ref[0:1, :] += _colsum(dh * n)
        accb_ref[1:2, :] += _colsum(dh)
        accg_ref[0:1, :] += _colsum(dn * xn)

    anyspec = pl.BlockSpec(memory_space=pl.ANY)
    return pl.pallas_call(
        body, name="mlp_bwd", grid=(BL, NJ),
        in_specs=[_tok(D), _tok(DFF), anyspec, anyspec, _tok(D), _tok(D), MOD_SPEC, _full((1, D))],
        out_specs=[_tok(DFF), _tok(D), ACCB_SPEC, ACCG_SPEC],
        out_shape=[jax.ShapeDtypeStruct((BL, SEQ, DFF), bf16), jax.ShapeDtypeStruct((BL, SEQ, D), f32)] + ACC_SHAPES,
        scratch_shapes=[pltpu.VMEM((DFF, D), bf16), pltpu.VMEM((NCHIP, D, D), bf16), pltpu.SemaphoreType.DMA((2,))],
        compiler_params=_cp(("arbitrary", "arbitrary")),
    )(dy2, u, w_down, w_up, x1, gx, mod, g_pre)


def _matmul_tn(a, b, *, tn, col_blocked, name, out_dtype=f32):
    t, m = a.shape
    n = b.shape[1]
    tmm = min(m, 1024)
    tk = 2048 if tn <= 1024 else 1024
    nk = t // tk

    def body(a_ref, b_ref, o_ref, acc):
        k = pl.program_id(2)

        @pl.when(k == 0)
        def _():
            acc[...] = jnp.zeros_like(acc)

        acc[...] += _dot_tn(a_ref[...], b_ref[...])

        @pl.when(k == nk - 1)
        def _():
            o_ref[...] = acc[...].astype(out_dtype)

    if col_blocked:
        out_spec = pl.BlockSpec((None, tmm, tn), lambda i, j, k: (j, i, 0))
        out_shape = jax.ShapeDtypeStruct((n // tn, m, tn), out_dtype)
    else:
        out_spec = pl.BlockSpec((tmm, tn), lambda i, j, k: (i, j))
        out_shape = jax.ShapeDtypeStruct((m, n), out_dtype)
    return pl.pallas_call(
        body, name=name, grid=(m // tmm, n // tn, nk),
        in_specs=[pl.BlockSpec((tk, tmm), lambda i, j, k: (k, i)), pl.BlockSpec((tk, tn), lambda i, j, k: (k, j))],
        out_specs=out_spec, out_shape=out_shape, scratch_shapes=[pltpu.VMEM((tmm, tn), f32)],
        compiler_params=_cp(("arbitrary", "arbitrary", "arbitrary")),
    )(a, b)


def _grad_w_in(h, dproj):
    t = h.shape[0]
    tk = 1024
    nk = t // tk
    sw = INW // NCHIP

    def body(a_ref, b_ref, o_ref, acc):
        k = pl.program_id(0)

        @pl.when(k == 0)
        def _():
            acc[...] = jnp.zeros_like(acc)

        acc[...] += _dot_tn(a_ref[...], b_ref[...])

        @pl.when(k == nk - 1)
        def _():
            for s in range(NCHIP):
                o_ref[s] = acc[:, sw * s:sw * (s + 1)].astype(bf16)

    return pl.pallas_call(
        body, name="grad_w_in", grid=(nk,),
        in_specs=[pl.BlockSpec((tk, D), lambda k: (k, 0)), pl.BlockSpec((tk, INW), lambda k: (k, 0))],
        out_specs=pl.BlockSpec((NCHIP, D, sw), lambda k: (0, 0, 0)), out_shape=jax.ShapeDtypeStruct((NCHIP, D, sw), bf16),
        scratch_shapes=[pltpu.VMEM((D, INW), f32)], compiler_params=_cp(("arbitrary",)),
    )(h, dproj)


def _attn_out_bwd(gx1, y, mod, g_post, w_out, oa, ob, g_mix_a, g_mix_b, l1, l4, l16):
    def body(gx_ref, y_ref, mod_ref, gp_ref, w_ref, oa_ref, ob_ref, ga_ref, gb_ref, l1_ref, l4_ref, l16_ref, e_ref, g_ref,
             dy_ref, doa_ref, do1_ref, do4_ref, do16_ref, da_ref, d1_ref, d4_ref, d16_ref, accb_ref, accg_ref, scr):
        _acc_init(accb_ref, accg_ref)
        w1, w4, w16 = _branch_weights(l1_ref, l4_ref, l16_ref, scr)
        e, hs = e_ref[...], g_ref[...]
        gx1v = gx_ref[...]
        yn, ry = _rms(y_ref[...])
        gp = gp_ref[...]
        gt = mod_ref[2:3, :]
        dn1 = gx1v * gt
        dy = _rms_bwd(dn1 * gp, yn, ry).astype(bf16)
        dy_ref[...] = dy
        dmixed = _dot_nt(dy, w_ref[...])
        dma, dmb = dmixed[:, :AQ], dmixed[:, AQ:]
        oa, ob = oa_ref[...], ob_ref[...]
        oan, ra = _rms(oa)
        obn, rb = _rms(ob)
        doa = _rms_bwd(dma * ga_ref[...], oan, ra)
        doa_ref[...] = doa.astype(bf16)
        da_ref[...] = _lanes_to_heads(doa * oa, hs)
        dob = _rms_bwd(dmb * gb_ref[...], obn, rb)
        dd = _lanes_to_heads(dob * ob, hs)
        x1w, x4w = _heads_to_lanes(w1, e), _heads_to_lanes(w4, e)
        do1_ref[...] = (x1w * dob).astype(bf16)
        d1_ref[...] = w1 * dd
        _perm_store(x4w * dob, scr, do4_ref, 4)
        _perm_store(w4 * dd, scr, d4_ref, 4)
        _perm_store((1.0 - x1w - x4w) * dob, scr, do16_ref, 16)
        _perm_store(w16 * dd, scr, d16_ref, 16)
        accb_ref[0:1, :] += _colsum(gx1v * (yn * gp))
        accg_ref[0:1, :] += _colsum(dn1 * yn)
        accg_ref[1:2, :] += jnp.concatenate([_colsum(dma * oan), _colsum(dmb * obn)], axis=1)

    nat = lambda w, dt: jax.ShapeDtypeStruct((BL, SEQ, w), dt)
    return pl.pallas_call(
        body, name="attn_out_bwd", grid=(BL, NJ),
        in_specs=[_tok(D), _tok(D), MOD_SPEC, _full((1, D)), _full((D, D)), _tok(AQ), _tok(BW), _full((1, AQ)), _full((1, BW)),
                  _tok(LANES), _perm_spec(4, LANES), _perm_spec(16, LANES), _full((LANES, BW)), _full((BW, LANES))],
        out_specs=[_tok(D), _tok(AQ), _tok(BW), _perm_spec(4, BW), _perm_spec(16, BW),
                   _tok(LANES), _tok(LANES), _perm_spec(4, LANES), _perm_spec(16, LANES), ACCB_SPEC, ACCG_SPEC],
        out_shape=[nat(D, bf16), nat(AQ, bf16), nat(BW, bf16), jax.ShapeDtypeStruct((BL, 4, SEQ // 4, BW), bf16),
                   jax.ShapeDtypeStruct((BL, 16, SEQ // 16, BW), bf16), nat(LANES, f32), nat(LANES, f32),
                   jax.ShapeDtypeStruct((BL, 4, SEQ // 4, LANES), f32), jax.ShapeDtypeStruct((BL, 16, SEQ // 16, LANES), f32)]
                  + ACC_SHAPES,
        scratch_shapes=[pltpu.VMEM((BW // LANES, TM, LANES), f32)],
        compiler_params=_cp(("arbitrary", "arbitrary")),
    )(gx1, y, mod, g_post, w_out, oa, ob, g_mix_a, g_mix_b, l1, l4, l16, jnp.asarray(HEAD_EXPAND, bf16),
      jnp.asarray(HEAD_SUM, bf16))


def _attn_in_bwd(dqa, dka, dva, d1, d4, d16, tc, ts1, ts2, w_in, x, gx1, mod, g_pre):
    def body(dqa_ref, dka_ref, dva_ref, dq1_ref, dk1_ref, dv1_ref, dq4_ref, dk4_ref, dv4_ref, dq16_ref, dk16_ref, dv16_ref,
             c_ref, s1_ref, s2_ref, w_ref, x_ref, gx_ref, mod_ref, g_ref, dproj_ref, dx_ref, accb_ref, accg_ref, scr):
        _acc_init(accb_ref, accg_ref)
        c, s1, s2 = c_ref[...], s1_ref[...], s2_ref[...]
        tot = lambda r1, r4, r16: r1[...] + _perm_load(r4, scr, 4) + _perm_load(r16, scr, 16)
        dqb = tot(dq1_ref, dq4_ref, dq16_ref)
        dkb = tot(dk1_ref, dk4_ref, dk16_ref)
        dvb = tot(dv1_ref, dv4_ref, dv16_ref)
        dproj = jnp.concatenate([
            _rope_t(dqa_ref[...], c, s1, s2) * QSCALE, _rope_t(_per_kv_head(dka_ref[...]), c, s1, s2),
            _per_kv_head(dva_ref[...]),
            _rope_t(dqb, c, s1, s2) * QSCALE, _rope_t(dkb, c, s1, s2), dvb], axis=1).astype(bf16)
        dproj_ref[...] = dproj
        dh = _dot_nt(dproj, w_ref[...])
        xn, r = _rms(x_ref[...])
        g = g_ref[...]
        dn = dh * (1.0 + mod_ref[1:2, :])
        dx_ref[...] = gx_ref[...] + _rms_bwd(dn * g, xn, r)
        accb_ref[0:1, :] += _colsum(dh * (xn * g))
        accb_ref[1:2, :] += _colsum(dh)
        accg_ref[0:1, :] += _colsum(dn * xn)

    return pl.pallas_call(
        body, name="attn_in_bwd", grid=(BL, NJ),
        in_specs=[_tok(AQ), _tok(AQ), _tok(AQ)] + [_tok(BW)] * 3 + [_perm_spec(4, BW)] * 3 + [_perm_spec(16, BW)] * 3
                 + [_tok(LANES)] * 3 + [_full((D, INW)), _tok(D), _tok(D), MOD_SPEC, _full((1, D))],
        out_specs=[_tok(INW), _tok(D), ACCB_SPEC, ACCG_SPEC],
        out_shape=[jax.ShapeDtypeStruct((BL, SEQ, INW), bf16), jax.ShapeDtypeStruct((BL, SEQ, D), f32)] + ACC_SHAPES,
        scratch_shapes=[pltpu.VMEM((BW // LANES, TM, LANES), f32)],
        compiler_params=_cp(("arbitrary", "arbitrary")),
    )(dqa, dka, dva, *d1, *d4, *d16, tc, ts1, ts2, w_in, x, gx1, mod, g_pre)


def _inv_lane():
    inv = np.float32(THETA) ** (-np.arange(0, ROT, 2, dtype=np.float32) / np.float32(ROT))
    lane = np.arange(LANES) % HD
    return jnp.asarray(np.where(lane < ROT, inv[lane % (ROT // 2)], 0.0).astype(np.float32)[None, :])


def _local_step(x, tabs, mod, target, w_in, later_weights, grad_ready, g_attn_pre,
                g_attn_post, sink_a, g_mix_a, g_mix_b, g_mlp_pre, g_mlp_post):
    tc, ts1, ts2 = [t.reshape(BL, SEQ, LANES) for t in tabs]

    (h, qa, ka, va, q1, k1, v1, q4, k4, v4, q16, k16, v16, w_in) = _attn_in(x, mod, g_attn_pre, w_in, tc, ts1, ts2)
    seqs = lambda t: t.reshape(t.shape[0] * t.shape[1], t.shape[2], t.shape[3])
    q4, k4, v4, q16, k16, v16 = [seqs(t) for t in (q4, k4, v4, q16, k16, v16)]
    oa, la = _attn_fwd(qa, ka, va, sink_a, max_dist=BLK - 1, o_dtype=f32, name="attn_a_fwd")
    o1, l1 = _attn_fwd(q1, k1, v1, None, max_dist=BLK, o_dtype=bf16, name="attn_b1_fwd")
    o4, l4 = _attn_fwd(q4, k4, v4, None, max_dist=BLK, o_dtype=bf16, name="attn_b4_fwd")
    o16, l16 = _attn_fwd(q16, k16, v16, None, max_dist=BLK, o_dtype=bf16, name="attn_b16_fwd")
    b4 = lambda t: t.reshape(BL, 4, SEQ // 4, t.shape[-1])
    b16 = lambda t: t.reshape(BL, 16, SEQ // 16, t.shape[-1])
    w_out, mlp_weights, mod = later_weights((oa, o1, o4, o16), mod)
    x1, y, mixed, ob = _mix_out(oa, o1, l1, b4(o4), b4(l4), b16(o16), b16(l16), g_mix_a, g_mix_b, w_out, x, mod, g_attn_post)
    w_up, w_down = mlp_weights((x1,))
    h2, u, a = _mlp_up(x1, mod, g_mlp_pre, w_up)
    gx, dy2, accb_d, accg_d = _mlp_down(a, w_down, x1, target, mod, g_mlp_post)

    flat = lambda t: t.reshape(BL * SEQ, t.shape[-1])
    mod = grad_ready("w_down", _matmul_tn(flat(a), flat(dy2), tn=D, col_blocked=False, name="grad_w_down", out_dtype=bf16), mod)
    du, gx1, accb_m, accg_m = _mlp_bwd(dy2, u, w_down, w_up, x1, gx, mod, g_mlp_pre)
    mod = grad_ready("w_up", _matmul_tn(flat(h2), flat(du), tn=D, col_blocked=True, name="grad_w_up", out_dtype=bf16), mod)

    dy, doa, do1, do4, do16, da, dl1, dl4, dl16, accb_o, accg_o = _attn_out_bwd(
        gx1, y, mod, g_attn_post, w_out, oa, ob, g_mix_a, g_mix_b, l1, b4(l4), b16(l16))
    sink_behind = grad_ready("w_out", _matmul_tn(flat(mixed), flat(dy), tn=D, col_blocked=False, name="grad_w_out",
                                                  out_dtype=bf16), sink_a)
    dqa, dka, dva, dsink = _attn_bwd(qa, ka, va, doa, da, la, sink_behind, max_dist=BLK - 1, name="attn_a_bwd")
    d1 = _attn_bwd(q1, k1, v1, do1, dl1, l1, None, max_dist=BLK, name="attn_b1_bwd")
    d4 = _attn_bwd(q4, k4, v4, seqs(do4), seqs(dl4), l4, None, max_dist=BLK, name="attn_b4_bwd")
    d16 = _attn_bwd(q16, k16, v16, seqs(do16), seqs(dl16), l16, None, max_dist=BLK, name="attn_b16_bwd")
    dproj, grad_x, accb_i, accg_i = _attn_in_bwd(dqa, dka, dva, d1, [b4(t) for t in d4], [b16(t) for t in d16],
                                                 tc, ts1, ts2, w_in, x, gx1, mod, g_attn_pre)
    gw_in = _grad_w_in(flat(h), flat(dproj))
    dsink = grad_ready("w_in", gw_in, dsink)

    return grad_x, (accb_i, accb_o, accb_m, accb_d, accg_i, accg_o, accg_m, accg_d, dsink)


ADAW = NMOD * D // NCHIP


def _pos():
    return lax.axis_index("x"), lax.axis_index("y"), lax.axis_index("c")


def _flip(v, bit):
    return 1 - v if bit else v


def _all_peers(x, y, c):
    return [(_flip(x, k >> 2 & 1), _flip(y, k >> 1 & 1), _flip(c, k & 1)) for k in range(1, NDEV)]


def _other_chips(x, y):
    return [(1 - x, y), (x, 1 - y), (1 - x, 1 - y)]


def _rcopy(src, dst, send, recv, k, dev, k_recv=None):
    return pltpu.make_async_remote_copy(src_ref=src, dst_ref=dst, send_sem=send.at[k],
                                        recv_sem=recv.at[k if k_recv is None else k_recv],
                                        device_id=dev, device_id_type=MESH)


def _small_copies(src, land, send, recv):
    x, y, c = _pos()
    me = 4 * x + 2 * y + c
    return [(_rcopy(src, land.at[me], send, recv, k, p), _rcopy(src, land.at[4 * p[0] + 2 * p[1] + p[2]], send, recv, k, p))
            for k, p in enumerate(_all_peers(x, y, c))]


def _ada_fwd(c_in, landed, w_ada, b_cols):
    def body(c_ref, land, w_hbm, b_ref, mod_ref, cond_ref, mbuf, w_ref, s2, r2, wsem):
        x, y, c = _pos()
        chip = 2 * x + y
        me = 4 * x + 2 * y + c
        wcopy = pltpu.make_async_copy(w_hbm, w_ref, wsem)
        wcopy.start()
        for i in range(NDEV):
            @pl.when(me == i)
            def _():
                cond_ref[BL * i:BL * (i + 1), :] = c_ref[...]

            @pl.when(me != i)
            def _():
                cond_ref[BL * i:BL * (i + 1), :] = land[i]
        call = cond_ref[...]
        cond = call / (1.0 + jnp.exp(-call))
        cond_ref[...] = cond
        wcopy.wait()
        mbuf[chip] = _dot(cond.astype(bf16), w_ref[...].astype(bf16)) + b_ref[...]
        chips = _other_chips(x, y)
        sends = [_rcopy(mbuf.at[chip], mbuf.at[chip], s2, r2, j, (px, py, c)) for j, (px, py) in enumerate(chips)]
        for cp in sends:
            cp.start()
        for j, (px, py) in enumerate(chips):
            _rcopy(mbuf.at[chip], mbuf.at[2 * px + py], s2, r2, j, (px, py, c)).wait_recv()
        for cp in sends:
            cp.wait_send()
        row = lax.broadcasted_iota(jnp.int32, (BL * NDEV, ADAW), 0)
        for s in range(NCHIP):
            slab = mbuf[s]
            for j in range(BL):
                mod_ref[j:j + 1, ADAW * s:ADAW * (s + 1)] = jnp.sum(jnp.where(row == BL * me + j, slab, 0.0), axis=0, keepdims=True)

    vm = pl.BlockSpec(memory_space=pltpu.VMEM)
    return pl.pallas_call(
        body, name="ada_fwd", in_specs=[vm, vm, pl.BlockSpec(memory_space=pl.ANY), vm], out_specs=[vm, vm],
        out_shape=[jax.ShapeDtypeStruct((BL, NMOD * D), f32), jax.ShapeDtypeStruct((BL * NDEV, D), f32)],
        scratch_shapes=[pltpu.VMEM((NCHIP, BL * NDEV, ADAW), f32), pltpu.VMEM((D, ADAW), f32),
                        pltpu.SemaphoreType.DMA((NCHIP - 1,)), pltpu.SemaphoreType.DMA((NCHIP - 1,)),
                        pltpu.SemaphoreType.DMA],
        compiler_params=pltpu.CompilerParams(vmem_limit_bytes=VMEM_LIMIT),
    )(c_in, landed, w_ada, b_cols)


PAY_ROWS = 4


def _small_pack(accs):
    def body(bi, bo, bm, bd, gi, go, gm, gd, dsink, pay):
        pay[...] = jnp.zeros_like(pay)
        for b in range(BL):
            for k, (ref, r) in enumerate(((bi, 1), (bi, 0), (bo, 0), (bm, 1), (bm, 0), (bd, 0))):
                pay[b:b + 1, D * k:D * (k + 1)] = ref[b, r:r + 1, :]
        for off, ref, r in ((OFF_G_ATTN_PRE, gi, 0), (OFF_G_ATTN_POST, go, 0), (OFF_G_MIX_A, go, 1), (OFF_G_MLP_PRE, gm, 0),
                            (OFF_G_MLP_POST, gd, 0)):
            pay[BL:BL + 1, off:off + D] = ref[r:r + 1, :]
        eye = lax.broadcasted_iota(jnp.int32, (NHEAD, LANES), 0) == lax.broadcasted_iota(jnp.int32, (NHEAD, LANES), 1)
        pay[BL:BL + 1, OFF_SINK:OFF_SINK + LANES] = jnp.sum(jnp.where(eye, dsink[...], 0.0), axis=0, keepdims=True)
        pay[BL:BL + 1, OFF_LOSS:OFF_LOSS + LANES] = gd[1:2, 0:LANES]

    vm = pl.BlockSpec(memory_space=pltpu.VMEM)
    return pl.pallas_call(body, name="small_pack", in_specs=[vm] * 9, out_specs=vm,
                          out_shape=jax.ShapeDtypeStruct((PAY_ROWS, PAYW), f32))(*accs)


def _small_sum(own, landed, cond_all):
    def body(pay, land, cond_ref, gw_ref, gb_ref, small_ref, pbuf, dall):
        x, y, c = _pos()
        chip = 2 * x + y
        me = 4 * x + 2 * y + c
        for i in range(NDEV):
            @pl.when(me == i)
            def _():
                pbuf[i] = pay[...]

            @pl.when(me != i)
            def _():
                pbuf[i] = land[i]
        small = pbuf[0, BL:BL + 1, :]
        for i in range(1, NDEV):
            small = small + pbuf[i, BL:BL + 1, :]
        small_ref[...] = small
        for i in range(NDEV):
            dall[BL * i:BL * (i + 1), :] = pbuf[i, 0:BL, :]
        gb_ref[...] = jnp.sum(dall[...], axis=0, keepdims=True)
        cols = jnp.zeros((BL * NDEV, ADAW), f32)
        for s in range(NCHIP):
            cols = cols + jnp.where(chip == s, dall[:, ADAW * s:ADAW * (s + 1)], 0.0)
        gw_ref[...] = _dot_tn(cond_ref[...].astype(bf16), cols.astype(bf16))

    vm = pl.BlockSpec(memory_space=pltpu.VMEM)
    return pl.pallas_call(
        body, name="small_sum", in_specs=[vm] * 3, out_specs=[vm] * 3,
        out_shape=[jax.ShapeDtypeStruct((D, ADAW), f32), jax.ShapeDtypeStruct((1, PAYW), f32), jax.ShapeDtypeStruct((1, PAYW), f32)],
        scratch_shapes=[pltpu.VMEM((NDEV, PAY_ROWS, PAYW), f32), pltpu.VMEM((BL * NDEV, PAYW), f32)],
        compiler_params=pltpu.CompilerParams(vmem_limit_bytes=VMEM_LIMIT),
    )(own, landed, cond_all)


def _half(ref, c):
    r2 = ref.shape[0] // 2
    return ref.at[pl.ds(c * r2 if isinstance(c, int) else pl.multiple_of(c * r2, 16), r2), :]


HBM_SPEC = pl.BlockSpec(memory_space=pltpu.HBM)
SEM_SPEC = pl.BlockSpec(memory_space=pltpu.SEMAPHORE)
EFFECT = pltpu.SideEffectType.DATAFLOW_SIDE_EFFECTING
NLINK = NCHIP - 1


def _in_hbm(a):
    return pltpu.with_memory_space_constraint(a, pltpu.HBM)


NSEM = 8


def _split_start(name, srcs, land_shapes, builds, carry, after=(), lands=None):
    n = len(srcs)
    na, nc = len(after), len(carry)

    def body(*refs):
        src, land = refs[:n], refs[n:2 * n]
        kept = refs[2 * n + na:2 * n + na + nc]
        outs = refs[2 * n + na + nc:]
        send, recv, passed = outs[:n], outs[n:2 * n], outs[4 * n:]
        for t in range(n):
            for out_cp, _ in builds[t](src[t], land[t], send[t], recv[t]):
                out_cp.start()
        for a, b in zip(kept, passed):
            b[...] = a[...]

    if lands is None:
        lands = [lax.empty(s.shape, s.dtype) for s in land_shapes]
    lands = [_in_hbm(a) for a in lands]
    sems = [pltpu.SemaphoreType.DMA((NSEM,))] * (2 * n)
    thru = [pltpu.HBM(a.shape, a.dtype) for a in list(srcs) + lands]
    vm = pl.BlockSpec(memory_space=pltpu.VMEM)
    res = pl.pallas_call(
        body, name=name, out_shape=sems + thru + [jax.ShapeDtypeStruct(a.shape, a.dtype) for a in carry],
        in_specs=[HBM_SPEC] * (2 * n) + [pl.BlockSpec(memory_space=pl.ANY)] * na + [vm] * nc,
        out_specs=[SEM_SPEC] * (2 * n) + [HBM_SPEC] * (2 * n) + [vm] * nc,
        input_output_aliases={i: 2 * n + i for i in range(2 * n)},
        compiler_params=pltpu.CompilerParams(has_side_effects=EFFECT),
    )(*[_in_hbm(a) for a in srcs], *lands, *after, *carry)
    flight = [(res[2 * n + t], res[3 * n + t], res[t], res[n + t]) for t in range(n)]
    return flight, list(res[4 * n:])


def _split_wait(name, flight, builds, after):
    m = len(flight)
    na = len(after)

    def body(*refs):
        src, land, send, recv = refs[:m], refs[m:2 * m], refs[2 * m:3 * m], refs[3 * m:4 * m]
        for t in range(m):
            for out_cp, in_cp in builds[t](src[t], land[t], send[t], recv[t]):
                out_cp.wait_send()
                in_cp.wait_recv()

    ops = [f[0] for f in flight] + [f[1] for f in flight] + [f[2] for f in flight] + [f[3] for f in flight]
    res = pl.pallas_call(
        body, name=name, out_shape=[pltpu.HBM(a.shape, a.dtype) for a in ops[:2 * m]],
        in_specs=[HBM_SPEC] * (2 * m) + [SEM_SPEC] * (2 * m) + [pl.BlockSpec(memory_space=pl.ANY)] * na,
        out_specs=[HBM_SPEC] * (2 * m), input_output_aliases={i: i for i in range(2 * m)},
        compiler_params=pltpu.CompilerParams(has_side_effects=EFFECT),
    )(*ops, *after)
    return res[:m], res[m:2 * m]


def _weight_copies(src, land, send, recv):
    x, y, c = _pos()
    chip = 2 * x + y
    return [(_rcopy(_half(src, c), _half(land.at[chip], c), send, recv, j, (px, py, c)),
             _rcopy(_half(src, c), _half(land.at[2 * px + py], c), send, recv, j, (px, py, c)))
            for j, (px, py) in enumerate(_other_chips(x, y))]


NDIRECT = NDEV - 1


def _direct_grad_copies(src, land, send, recv):
    x, y, c = _pos()
    out, arrive = [], []
    for j, (px, py) in enumerate(_other_chips(x, y)):
        for hc in range(2):
            out.append(_rcopy(_half(src.at[2 * px + py], hc), land.at[2 * j + c], send, recv, 2 * j + hc, (px, py, hc),
                              k_recv=2 * j + c))
            arrive.append(_rcopy(_half(src.at[2 * px + py], hc), land.at[2 * j + hc], send, recv, 2 * j + hc, (px, py, hc)))
    own = _rcopy(_half(src.at[2 * x + y], 1 - c), land.at[NDIRECT - 1], send, recv, NDIRECT - 1, (x, y, 1 - c))
    return list(zip(out, arrive)) + [(own, own)]


def _pair_weight_copies(src, land, send, recv):
    x, y, c = _pos()
    sib = (x, y, 1 - c)
    cps = []
    for j, (px, py) in enumerate(_other_chips(x, y)):
        mine, theirs = _half(land.at[2 * px + py], c), _half(land.at[2 * px + py], 1 - c)
        cps.append((_rcopy(mine, mine, send, recv, j, sib), _rcopy(theirs, theirs, send, recv, j, sib)))
    own = _rcopy(src, land.at[2 * x + y], send, recv, NLINK, sib)
    return cps + [(own, own)]


RS_ROWS = 256


def _chip_add(own, landed, pos_arr, name):
    nl, r2, cw = landed.shape
    rows = min(RS_ROWS, r2)
    nr = r2 // rows

    def body(s_ref, h_ref, q_ref, o_ref):
        acc = h_ref[...].astype(f32)
        for j in range(nl):
            acc = acc + q_ref[j].astype(f32)
        o_ref[...] = acc

    gs = pltpu.PrefetchScalarGridSpec(
        num_scalar_prefetch=1, grid=(nr,),
        in_specs=[pl.BlockSpec((None, rows, cw), lambda j, s: (s[0], s[1] * nr + j, 0)),
                  pl.BlockSpec((nl, rows, cw), lambda j, s: (0, j, 0))],
        out_specs=pl.BlockSpec((rows, cw), lambda j, s: (s[1] * nr + j, 0)))
    return pl.pallas_call(body, name=name, grid_spec=gs, out_shape=jax.ShapeDtypeStruct((2 * r2, cw), f32),
                          compiler_params=_cp(("arbitrary",)))(pos_arr, own, landed)


def _pair_gather_copies(src, land, send, recv):
    x, y, c = _pos()
    sib = (x, y, 1 - c)
    return [(_rcopy(_half(land, c), _half(land, c), send, recv, 0, sib),
             _rcopy(_half(land, 1 - c), _half(land, 1 - c), send, recv, 0, sib))]


def _adamw_math(w, g, m, v):
    m = B1 * m + (1.0 - B1) * g
    v = B2 * v + (1.0 - B2) * jnp.square(g)
    m_hat = m / (1.0 - B1 ** STEP)
    v_hat = v / (1.0 - B2 ** STEP)
    return -LR * (m_hat / (jnp.sqrt(v_hat) + AEPS) + WD * w), m, v


ADAM_BLOCK = 512 * 1024


def _adamw(w, g, m, v, name, after=(), landed=True):
    r, cw = w.shape
    na = len(after)

    def body(w_ref, g_ref, m_ref, v_ref, *rest):
        outs = rest[na:]
        g = g_ref[...]
        if landed:
            outs[0][...] = g
        outs[-3][...], outs[-2][...], outs[-1][...] = _adamw_math(w_ref[...], g, m_ref[...], v_ref[...])

    rows = max(k for k in range(SUBLANES, ADAM_BLOCK // cw + 1, SUBLANES) if r % k == 0)
    spec = pl.BlockSpec((rows, cw), lambda i: (i, 0))
    nout = 4 if landed else 3
    res = pl.pallas_call(body, name=name, grid=(r // rows,), in_specs=[spec] * 4 + [pl.BlockSpec(memory_space=pl.ANY)] * na,
                         out_specs=[spec] * nout, out_shape=[jax.ShapeDtypeStruct((r, cw), f32)] * nout,
                         compiler_params=_cp(("arbitrary",)))(w, g, m, v, *after)
    return list(res) if landed else [g, *res]


SMALL = (("b_ada", None, PAYW), ("g_attn_pre", OFF_G_ATTN_PRE, D), ("g_attn_post", OFF_G_ATTN_POST, D), ("sink_a", OFF_SINK, 8),
         ("g_mix_a", OFF_G_MIX_A, AQ), ("g_mix_b", OFF_G_MIX_B, BW), ("g_mlp_pre", OFF_G_MLP_PRE, D), ("g_mlp_post", OFF_G_MLP_POST, D))


def _adamw_small(small, gb, params):
    n = len(SMALL)

    def body(*refs):
        small_ref, gb_ref = refs[:2]
        wmv = refs[2:2 + 3 * n]
        loss_ref = refs[2 + 3 * n]
        outs = refs[3 + 3 * n:]
        loss_ref[...] = small_ref[:, OFF_LOSS:OFF_LOSS + 1] * (0.5 / D)
        for i, (_, off, width) in enumerate(SMALL):
            g = gb_ref[...] if off is None else small_ref[:, off:off + width]
            w_ref, m_ref, v_ref = wmv[3 * i:3 * i + 3]
            outs[4 * i][...] = g
            outs[4 * i + 1][...], outs[4 * i + 2][...], outs[4 * i + 3][...] = _adamw_math(w_ref[...], g, m_ref[...], v_ref[...])

    vm = pl.BlockSpec(memory_space=pltpu.VMEM)
    out_shape = [jax.ShapeDtypeStruct((1, 1), f32)]
    for _, _, width in SMALL:
        out_shape += [jax.ShapeDtypeStruct((1, width), f32)] * 4
    flat = [a for wmv in params for a in wmv]
    res = pl.pallas_call(body, name="adamw_small", in_specs=[vm] * (2 + 3 * n), out_specs=[vm] * len(out_shape),
                         out_shape=out_shape)(small, gb, *flat)
    return res[0], {name: res[1 + 4 * i:5 + 4 * i] for i, (name, _, _) in enumerate(SMALL)}


def kernel(x, c, positions, w_ada, b_ada, g_attn_pre, g_attn_post, w_in, sink_a, g_mix_a, g_mix_b, w_out, g_mlp_pre, g_mlp_post, w_up, w_down, loss_target, m_w_ada, m_b_ada, m_g_attn_pre, m_g_attn_post, m_w_in, m_sink_a, m_g_mix_a, m_g_mix_b, m_w_out, m_g_mlp_pre, m_g_mlp_post, m_w_up, m_w_down, v_w_ada, v_b_ada, v_g_attn_pre, v_g_attn_post, v_w_in, v_sink_a, v_g_mix_a, v_g_mix_b, v_w_out, v_g_mlp_pre, v_g_mlp_post, v_w_up, v_w_down):
    given = dict(w_ada=w_ada, b_ada=b_ada, g_attn_pre=g_attn_pre, g_attn_post=g_attn_post, w_in=w_in, sink_a=sink_a, g_mix_a=g_mix_a,
                 g_mix_b=g_mix_b, w_out=w_out, g_mlp_pre=g_mlp_pre, g_mlp_post=g_mlp_post, w_up=w_up, w_down=w_down)
    moms = dict(w_ada=(m_w_ada, v_w_ada), b_ada=(m_b_ada, v_b_ada), g_attn_pre=(m_g_attn_pre, v_g_attn_pre),
                g_attn_post=(m_g_attn_post, v_g_attn_post), w_in=(m_w_in, v_w_in), sink_a=(m_sink_a, v_sink_a),
                g_mix_a=(m_g_mix_a, v_g_mix_a), g_mix_b=(m_g_mix_b, v_g_mix_b), w_out=(m_w_out, v_w_out),
                g_mlp_pre=(m_g_mlp_pre, v_g_mlp_pre), g_mlp_post=(m_g_mlp_post, v_g_mlp_post), w_up=(m_w_up, v_w_up),
                w_down=(m_w_down, v_w_down))
    order = ["w_ada", "b_ada", "g_attn_pre", "g_attn_post", "w_in", "sink_a", "g_mix_a", "g_mix_b", "w_out", "g_mlp_pre",
             "g_mlp_post", "w_up", "w_down"]
    xi, yi, ci = _pos()
    chip = 2 * xi + yi

    pos_arr = jnp.stack([chip, ci]).astype(jnp.int32)
    big = ("w_in", "w_out", "w_up", "w_down")

    gathered = [jax.ShapeDtypeStruct((NCHIP,) + given[n].shape[1:], bf16) for n in big]
    (flight_c, *flight_in), (inv_lane,) = _split_start(
        "weights_start_first", [c, w_in[0].astype(bf16)], [jax.ShapeDtypeStruct((NDEV, BL, D), f32), gathered[0]],
        [_small_copies, _weight_copies], [_inv_lane()])
    inv_lane, rest = lax.optimization_barrier((inv_lane, [given[n][0] for n in big[1:]]))
    tabs = _rope_tables(positions.reshape(BL * SEQ, 1), inv_lane)
    rest = [w.astype(bf16) for w in rest]
    b_cols = lax.dynamic_slice(b_ada, (0, chip * ADAW), (1, ADAW))
    (c_own,), (c_all,) = _split_wait("cond_wait", [flight_c], [_small_copies], (*tabs, *rest))
    mod, cond_all = _ada_fwd(c_own, c_all, w_ada[0], b_cols)

    srcs, lands = _split_wait("weights_wait_first", flight_in, [_weight_copies], (mod,))
    cross, (mod,) = _split_start("weights_pair_start_first", srcs, None, [_pair_weight_copies], [mod], lands=lands)
    flight_rest, (mod,) = _split_start("weights_start_rest", rest, gathered[1:], [_weight_copies] * 3, [mod])
    _, (win_g,) = _split_wait("weights_pair_wait_first", cross, [_pair_weight_copies], (mod,))
    mod = mod.reshape(BL, NMOD, D)

    def later_weights(after, carry):
        srcs, lands = _split_wait("weights_wait_rest", flight_rest, [_weight_copies] * 3, after)
        fl, (carry,) = _split_start("weights_pair_start_rest", srcs, None, [_pair_weight_copies] * 3, [carry], lands=lands)
        _, (wout_g,) = _split_wait("weights_pair_wait_out", fl[:1], [_pair_weight_copies], ())

        def mlp_weights(after):
            _, (wup_g, wdn_g) = _split_wait("weights_pair_wait_mlp", fl[1:], [_pair_weight_copies] * 2, after)
            return wup_g, wdn_g.reshape(DFF, D)

        return wout_g.reshape(D, D), mlp_weights, carry

    waiting, pending = {}, {}

    def send_grads(carry):
        names = list(waiting)
        slabs = [waiting.pop(n) for n in names]
        lands = [jax.ShapeDtypeStruct((NDIRECT, s.shape[1] // 2, s.shape[2]), bf16) for s in slabs]
        fl, (carry,) = _split_start("grad_start_" + names[-1], slabs, lands, [_direct_grad_copies] * len(names), [carry])
        for n, f in zip(names, fl):
            pending[n] = [f]
        return carry

    def grad_ready(name, g, carry):
        waiting[name] = g if g.ndim == 3 else g.reshape(NCHIP, g.shape[0] // NCHIP, g.shape[1])
        return send_grads(carry) if name == "w_out" else carry

    grad_x, accs = _local_step(x, tabs, mod, loss_target, win_g, later_weights, grad_ready,
                               g_attn_pre, g_attn_post, sink_a, g_mix_a, g_mix_b, g_mlp_pre, g_mlp_post)

    grads, out = {}, {}

    def update(n, after=()):
        tr = (lambda a: a.T) if n == "w_in" else (lambda a: a)
        res = _adamw(tr(given[n][0]), tr(grads[n]), tr(moms[n][0][0]), tr(moms[n][1][0]), "adamw_" + n, after,
                     landed=n != "w_ada")
        out[n] = tuple(tr(a)[None] for a in res)
        return res[3]

    def finish(names, after, first=()):
        fl = sum((pending[n] for n in names), [])
        halves, landed = _split_wait("grad_wait_" + names[0], fl, [_direct_grad_copies] * len(names), after)
        flights, token = [], jnp.zeros((SUBLANES, LANES), f32)
        for h, q, n in zip(halves, landed, names):
            full = _chip_add(h, q, pos_arr, "grad_chip_sum_" + n)
            flights.append(_split_start("grad_gather_start_" + n, [token], None, [_pair_gather_copies], [], lands=[full])[0])
            token = flights[-1][0][0]
        last = [update(n, (token,)) for n in first]
        for n, fl1 in zip(names, flights):
            after = tuple(last) if last else () if fl1 is flights[-1] else (token,)
            _, (grads[n],) = _split_wait("grad_gather_wait_" + n, fl1, [_pair_gather_copies], after)
            last = [update(n)]
        return last[0]

    slab = waiting.pop("w_in")
    (fl_small, fl_in), (cond_all,) = _split_start(
        "small_start", [_small_pack(accs), slab],
        [jax.ShapeDtypeStruct((NDEV, PAY_ROWS, PAYW), f32), jax.ShapeDtypeStruct((NDIRECT, slab.shape[1] // 2, slab.shape[2]), bf16)],
        [_small_copies, _direct_grad_copies], [cond_all])
    pending["w_in"] = [fl_in]
    last = finish(("w_down", "w_up", "w_out"), (cond_all,))
    (pay,), (landed,) = _split_wait("small_wait", [fl_small], [_small_copies], (last,))
    grads["w_ada"], gb, small = _small_sum(pay, landed, cond_all)
    finish(("w_in",), (small,), first=("w_ada",))
    loss, res = _adamw_small(small, gb, [(given[n], moms[n][0], moms[n][1]) for n, _, _ in SMALL])
    for n, _, _ in SMALL:
        out[n] = tuple(res[n])
    return (loss.reshape(()), grad_x, *[out[n][0] for n in order], *[out[n][1] for n in order],
            *[out[n][2] for n in order], *[out[n][3] for n in order])
```

```python
import numpy as np
import jax
import jax.numpy as jnp
from jax import lax
from jax.experimental import pallas as pl
from jax.experimental.pallas import tpu as pltpu

f32 = jnp.float32
bf16 = jnp.bfloat16
MESH = pl.DeviceIdType.MESH

D = 1024
SEQ = 2048
BL = 2
HD = 64
AQ = 512
AKV = 128
BW = 512
INW = 2304
DFF = 4096
NMOD = 6
ROT = 16
THETA = 500000.0
EPS = 1e-6
NEG = -1e30
BLK = 128
TM = 512
NJ = SEQ // TM
LANES = 128
SUBLANES = 8
NHEAD = AQ // HD
QSCALE = HD ** -0.5
NCHIP = 4
NDEV = 8
VMEM_LIMIT = 56 << 20

LR, B1, B2, AEPS, WD, STEP = 0.001, 0.9, 0.999, 1e-08, 0.01, 10

OFF_G_ATTN_PRE, OFF_G_ATTN_POST, OFF_G_MIX_A, OFF_G_MIX_B = 0, 1024, 2048, 2560
OFF_G_MLP_PRE, OFF_G_MLP_POST, OFF_SINK, OFF_LOSS = 3072, 4096, 5120, 5248
PAYW = NMOD * D


def _cp(sem=None):
    return pltpu.CompilerParams(dimension_semantics=sem, vmem_limit_bytes=VMEM_LIMIT)


def _dot(a, b):
    return jnp.dot(a, b, preferred_element_type=f32)


def _dot_nt(a, b):
    return lax.dot_general(a, b, (((1,), (1,)), ((), ())), preferred_element_type=f32)


def _dot_tn(a, b):
    return lax.dot_general(a, b, (((0,), (0,)), ((), ())), preferred_element_type=f32)


def _rms(x):
    r = lax.rsqrt(jnp.mean(x * x, axis=-1, keepdims=True) + EPS)
    return x * r, r


def _rms_bwd(dy, y, r):
    return r * (dy - y * jnp.mean(dy * y, axis=-1, keepdims=True))


def _colsum(v):
    return jnp.sum(v, axis=0, keepdims=True)


def _rope(p, c, s1, s2):
    outs = []
    for c0 in range(0, p.shape[1], LANES):
        pc = p[:, c0:c0 + LANES]
        outs.append(pc * c + pltpu.roll(pc, LANES - ROT // 2, 1) * s1 + pltpu.roll(pc, ROT // 2, 1) * s2)
    return outs[0] if len(outs) == 1 else jnp.concatenate(outs, axis=1)


def _rope_t(g, c, s1, s2):
    outs = []
    for c0 in range(0, g.shape[1], LANES):
        gc = g[:, c0:c0 + LANES]
        outs.append(gc * c + pltpu.roll(gc * s1, ROT // 2, 1) + pltpu.roll(gc * s2, LANES - ROT // 2, 1))
    return outs[0] if len(outs) == 1 else jnp.concatenate(outs, axis=1)


def _perm_store(val, scr, out_ref, d):
    nc = val.shape[1] // LANES
    for c in range(nc):
        scr[c] = val[:, LANES * c:LANES * (c + 1)]
    for c in range(nc):
        for r in range(d):
            out_ref[r, :, LANES * c:LANES * (c + 1)] = scr[c, pl.ds(r, TM // d, stride=d), :].astype(out_ref.dtype)


def _perm_load(in_ref, scr, d):
    nc = in_ref.shape[-1] // LANES
    for c in range(nc):
        for r in range(d):
            scr[c, pl.ds(r, TM // d, stride=d), :] = in_ref[r, :, LANES * c:LANES * (c + 1)].astype(f32)
    return jnp.concatenate([scr[c] for c in range(nc)], axis=1)


def _per_query_head(kv):
    r = pltpu.roll(kv, HD, 1)
    lo = lax.broadcasted_iota(jnp.int32, kv.shape, 1) < HD
    return jnp.concatenate([jnp.where(lo, kv, r), jnp.where(lo, r, kv)], axis=1)


def _per_kv_head(g):
    g0, g1 = g[:, :LANES] + g[:, LANES:2 * LANES], g[:, 2 * LANES:3 * LANES] + g[:, 3 * LANES:]
    lo = lax.broadcasted_iota(jnp.int32, g0.shape, 1) < HD
    return jnp.where(lo, g0 + pltpu.roll(g0, HD, 1), g1 + pltpu.roll(g1, HD, 1))


def _tok(w):
    return pl.BlockSpec((None, TM, w), lambda b, j: (b, j, 0))


def _perm_spec(d, w):
    return pl.BlockSpec((None, d, TM // d, w), lambda b, j: (b, 0, j, 0))


def _full(shape):
    n = len(shape)
    return pl.BlockSpec(shape, lambda b, j: (0,) * n)


MOD_SPEC = pl.BlockSpec((None, NMOD, D), lambda b, j: (b, 0, 0))
ACCB_SPEC = pl.BlockSpec((None, SUBLANES, D), lambda b, j: (b, 0, 0))
ACCG_SPEC = pl.BlockSpec((SUBLANES, D), lambda b, j: (0, 0))
ACC_SHAPES = [jax.ShapeDtypeStruct((BL, SUBLANES, D), f32), jax.ShapeDtypeStruct((SUBLANES, D), f32)]


def _acc_init(accb_ref, accg_ref):
    b, j = pl.program_id(0), pl.program_id(1)

    @pl.when(j == 0)
    def _():
        accb_ref[...] = jnp.zeros_like(accb_ref)

    @pl.when((b == 0) & (j == 0))
    def _():
        accg_ref[...] = jnp.zeros_like(accg_ref)


def _rope_tables(pos_col, inv_lane):
    def body(p_ref, inv_ref, c_ref, s1_ref, s2_ref):
        ang = p_ref[...].astype(f32) * inv_ref[...]
        j = lax.broadcasted_iota(jnp.int32, (TM, LANES), 1) % HD
        cs, sn = jnp.cos(ang), jnp.sin(ang)
        c_ref[...] = jnp.where(j < ROT, cs, 1.0)
        s1_ref[...] = jnp.where(j < ROT // 2, -sn, 0.0)
        s2_ref[...] = jnp.where((j >= ROT // 2) & (j < ROT), sn, 0.0)

    n = BL * SEQ // TM
    return pl.pallas_call(
        body, name="rope_tables", grid=(n,),
        in_specs=[pl.BlockSpec((TM, 1), lambda i: (i, 0)), pl.BlockSpec((1, LANES), lambda i: (0, 0))],
        out_specs=[pl.BlockSpec((TM, LANES), lambda i: (i, 0))] * 3,
        out_shape=[jax.ShapeDtypeStruct((BL * SEQ, LANES), f32)] * 3,
    )(pos_col, inv_lane)


def _attn_in(x, mod, g_pre, w_in, tc, ts1, ts2):
    def body(x_ref, mod_ref, g_ref, wg_ref, c_ref, s1_ref, s2_ref,
             h_ref, qa_ref, ka_ref, va_ref, q1_ref, k1_ref, v1_ref, q4_ref, k4_ref, v4_ref, q16_ref, k16_ref, v16_ref,
             w_ref, scr):
        @pl.when((pl.program_id(0) == 0) & (pl.program_id(1) == 0))
        def _():
            w_ref[...] = jnp.concatenate([wg_ref[s] for s in range(NCHIP)], axis=1)

        xn, _ = _rms(x_ref[...])
        h = (xn * g_ref[...]) * (1.0 + mod_ref[1:2, :]) + mod_ref[0:1, :]
        hb = h.astype(bf16)
        h_ref[...] = hb
        proj = _dot(hb, w_ref[...])
        c, s1, s2 = c_ref[...], s1_ref[...], s2_ref[...]
        o1, o2, o3, o4, o5 = AQ, AQ + AKV, AQ + 2 * AKV, AQ + 2 * AKV + BW, AQ + 2 * AKV + 2 * BW
        qa_ref[...] = (_rope(proj[:, :o1], c, s1, s2) * QSCALE).astype(bf16)
        ka_ref[...] = _per_query_head(_rope(proj[:, o1:o2], c, s1, s2)).astype(bf16)
        va_ref[...] = _per_query_head(proj[:, o2:o3]).astype(bf16)
        qb = _rope(proj[:, o3:o4], c, s1, s2) * QSCALE
        kb = _rope(proj[:, o4:o5], c, s1, s2)
        vb = proj[:, o5:]
        for val, r1, r4, r16 in ((qb, q1_ref, q4_ref, q16_ref), (kb, k1_ref, k4_ref, k16_ref), (vb, v1_ref, v4_ref, v16_ref)):
            r1[...] = val.astype(bf16)
            _perm_store(val, scr, r4, 4)
            _perm_store(val, scr, r16, 16)

    nat = lambda w: jax.ShapeDtypeStruct((BL, SEQ, w), bf16)
    p4 = jax.ShapeDtypeStruct((BL, 4, SEQ // 4, BW), bf16)
    p16 = jax.ShapeDtypeStruct((BL, 16, SEQ // 16, BW), bf16)
    return pl.pallas_call(
        body, name="attn_in", grid=(BL, NJ),
        in_specs=[_tok(D), MOD_SPEC, _full((1, D)), _full((NCHIP, D, INW // NCHIP)), _tok(LANES), _tok(LANES), _tok(LANES)],
        out_specs=([_tok(D), _tok(AQ), _tok(2 * AKV), _tok(2 * AKV)] + [_tok(BW)] * 3 + [_perm_spec(4, BW)] * 3 + [_perm_spec(16, BW)] * 3
                   + [_full((D, INW))]),
        out_shape=[nat(D), nat(AQ), nat(2 * AKV), nat(2 * AKV)] + [nat(BW)] * 3 + [p4] * 3 + [p16] * 3
                  + [jax.ShapeDtypeStruct((D, INW), bf16)],
        scratch_shapes=[pltpu.VMEM((BW // LANES, TM, LANES), f32)],
        compiler_params=_cp(("arbitrary", "arbitrary")),
    )(x, mod, g_pre, w_in, tc, ts1, ts2)


def _kv_cat(cur_ref, prev_ref, p, cache):
    key = (id(cur_ref), p)
    if key not in cache:
        sl = slice(LANES * p, LANES * (p + 1))
        cache[key] = cur_ref[:, sl] if prev_ref is None else jnp.concatenate([prev_ref[:, sl], cur_ref[:, sl]], axis=0)
    return cache[key]


def _lane_half(a, hh):
    lo = lax.broadcasted_iota(jnp.int32, a.shape, 1) < HD
    return jnp.where(lo, a, jnp.zeros_like(a)) if hh == 0 else jnp.where(lo, jnp.zeros_like(a), a)


ATT_UNITS = 4


def _att_units(nb):
    return ATT_UNITS if nb == 1 else min(ATT_UNITS, nb)


def _attn_specs(n, nb, descending):
    u = _att_units(nb)
    if nb == 1:
        return (lambda ww: pl.BlockSpec((u, BLK, ww), lambda a, i: (a, 0, 0))), None, (n // u, 1)
    steps = nb // u
    at = (lambda i: steps - 1 - i) if descending else (lambda i: i)
    cur = lambda ww: pl.BlockSpec((None, u * BLK, ww), lambda a, i: (a, at(i), 0))
    prev = lambda ww: pl.BlockSpec((None, BLK, ww), lambda a, i: (a, jnp.maximum(u * at(i) - 1, 0), 0))
    return cur, prev, (n, steps)


def _attn_fwd(q, k, v, sink, *, max_dist, o_dtype, name):
    n, l, w = q.shape
    wk = k.shape[-1]
    nb = l // BLK
    has_sink = sink is not None

    def body(*refs):
        sink_ref = None
        if has_sink:
            sink_ref, refs = refs[0], refs[1:]
        if nb > 1:
            q_ref, kc_ref, kp_ref, vc_ref, vp_ref, o_ref, lse_ref = refs[:7]
            first = pl.program_id(1) == 0
            for u in range(_att_units(nb)):
                rows, before = pl.ds(BLK * u, BLK), pl.ds(BLK * (u - 1), BLK)
                unit(q_ref.at[rows, :], kc_ref.at[rows, :], kp_ref if u == 0 else kc_ref.at[before, :],
                     vc_ref.at[rows, :], vp_ref if u == 0 else vc_ref.at[before, :], o_ref.at[rows, :], lse_ref.at[rows, :],
                     jnp.logical_not(first) if u == 0 else True, sink_ref, *refs[7:])
        else:
            q_ref, kc_ref, vc_ref, o_ref, lse_ref = refs[:5]
            for u in range(_att_units(nb)):
                unit(q_ref.at[u], kc_ref.at[u], None, vc_ref.at[u], None, o_ref.at[u], lse_ref.at[u], None, sink_ref, *refs[5:])

    def unit(q_ref, kc_ref, kp_ref, vc_ref, vp_ref, o_ref, lse_ref, has_prev, sink_ref, sscr, pscr, dscr):
        qi = lax.broadcasted_iota(jnp.int32, (BLK, BLK), 0)
        kj = lax.broadcasted_iota(jnp.int32, (BLK, BLK), 1)
        tri = kj <= qi
        eye = kj == qi
        cache = {}
        for p in range(w // LANES):
            qpair = q_ref[:, LANES * p:LANES * (p + 1)]
            kcat = _kv_cat(kc_ref, kp_ref, p // share, cache)
            for hh in range(2):
                s = _dot_nt(_lane_half(qpair, hh), kcat)
                if nb > 1:
                    sp = s[:, :BLK] if has_prev is True else jnp.where(has_prev, s[:, :BLK], NEG)
                    sscr[2 * p + hh] = jnp.where(tri, s[:, BLK:], sp)
                    if diag:
                        dscr[2 * p + hh] = jnp.where(eye, sp, NEG)
                else:
                    sscr[2 * p + hh] = jnp.where(tri, s, NEG)
        lane = lax.broadcasted_iota(jnp.int32, (BLK, LANES), 1)
        lse_all = jnp.zeros((BLK, LANES), f32)
        for p in range(w // LANES):
            for hh in range(2):
                h = 2 * p + hh
                comb = sscr[h]
                if diag:
                    dtile = dscr[h]
                    m = jnp.max(jnp.maximum(comb, dtile), axis=-1, keepdims=True)
                else:
                    m = jnp.max(comb, axis=-1, keepdims=True)
                if has_sink:
                    sk = sink_ref[0, h]
                    m = jnp.maximum(m, sk)
                e = jnp.exp(comb - m)
                if diag:
                    ed = jnp.exp(dtile - m)
                    den = jnp.sum(e + ed, axis=-1, keepdims=True)
                else:
                    den = jnp.sum(e, axis=-1, keepdims=True)
                if has_sink:
                    den = den + jnp.exp(sk - m)
                inv = 1.0 / den
                if nb > 1:
                    pscr[h, :, :BLK] = (jnp.where(tri, ed if diag else 0.0, e) * inv).astype(bf16)
                    pscr[h, :, BLK:] = (jnp.where(tri, e, 0.0) * inv).astype(bf16)
                else:
                    pscr[h] = (e * inv).astype(bf16)
                lse_all = jnp.where(lane == h, jnp.broadcast_to(m + jnp.log(den), (BLK, LANES)), lse_all)
        lse_ref[...] = lse_all
        for p in range(w // LANES):
            vcat = _kv_cat(vc_ref, vp_ref, p // share, cache)
            o_ref[:, LANES * p:LANES * (p + 1)] = (_dot(pscr[2 * p], _lane_half(vcat, 0))
                                                   + _dot(pscr[2 * p + 1], _lane_half(vcat, 1))).astype(o_ref.dtype)

    assert max_dist in (BLK - 1, BLK) and w % wk == 0
    share = w // wk
    diag = nb > 1 and max_dist == BLK
    cur, prev, grid = _attn_specs(n, nb, False)
    in_specs = [cur(w), cur(wk)] + ([prev(wk)] if nb > 1 else []) + [cur(wk)] + ([prev(wk)] if nb > 1 else [])
    args = [q, k] + ([k] if nb > 1 else []) + [v] + ([v] if nb > 1 else [])
    if has_sink:
        in_specs = [pl.BlockSpec(memory_space=pltpu.SMEM)] + in_specs
        args = [sink] + args
    return pl.pallas_call(
        body, name=name, grid=grid, in_specs=in_specs,
        out_specs=[cur(w), cur(LANES)],
        out_shape=[jax.ShapeDtypeStruct((n, l, w), o_dtype), jax.ShapeDtypeStruct((n, l, LANES), f32)],
        scratch_shapes=[pltpu.VMEM((w // HD, BLK, BLK), f32), pltpu.VMEM((w // HD, BLK, 2 * BLK if nb > 1 else BLK), bf16),
                        pltpu.VMEM((w // HD if diag else 1, BLK, BLK), f32)],
        compiler_params=_cp(("arbitrary", "arbitrary")),
    )(*args)


def _attn_bwd(q, k, v, do, delta, lse, sink, *, max_dist, name):
    n, l, w = q.shape
    wk = k.shape[-1]
    nb = l // BLK
    has_sink = sink is not None

    def body(*refs):
        sink_ref = dsink_ref = ck = cv = None
        if has_sink:
            sink_ref, refs = refs[0], refs[1:]
        nin = 8 if nb > 1 else 6
        ins, rest = refs[:nin], refs[nin:]
        if has_sink:
            dq_ref, dk_ref, dv_ref, dsink_ref = rest[:4]
            rest = rest[4:]
        else:
            dq_ref, dk_ref, dv_ref = rest[:3]
            rest = rest[3:]
        step = pl.program_id(1)
        if has_sink:
            @pl.when((pl.program_id(0) == 0) & (step == 0))
            def _():
                dsink_ref[...] = jnp.zeros_like(dsink_ref)

        if nb > 1:
            q_ref, kc_ref, kp_ref, vc_ref, vp_ref, do_ref, delta_ref, lse_ref = ins
            ck, cv = rest[:2]

            @pl.when(step == 0)
            def _():
                ck[...] = jnp.zeros_like(ck)
                cv[...] = jnp.zeros_like(cv)

            last = step == nb // _att_units(nb) - 1
            for u in reversed(range(_att_units(nb))):
                rows, before = pl.ds(BLK * u, BLK), pl.ds(BLK * (u - 1), BLK)
                unit(q_ref.at[rows, :], kc_ref.at[rows, :], kp_ref if u == 0 else kc_ref.at[before, :],
                     vc_ref.at[rows, :], vp_ref if u == 0 else vc_ref.at[before, :], do_ref.at[rows, :],
                     delta_ref.at[rows, :], lse_ref.at[rows, :], dq_ref.at[rows, :], dk_ref.at[rows, :], dv_ref.at[rows, :],
                     jnp.logical_not(last) if u == 0 else True, sink_ref, dsink_ref, ck, cv, *rest[2:])
        else:
            q_ref, kc_ref, vc_ref, do_ref, delta_ref, lse_ref = ins
            for u in range(_att_units(nb)):
                unit(q_ref.at[u], kc_ref.at[u], None, vc_ref.at[u], None, do_ref.at[u], delta_ref.at[u], lse_ref.at[u],
                     dq_ref.at[u], dk_ref.at[u], dv_ref.at[u], None, sink_ref, dsink_ref, None, None, *rest)

    def unit(q_ref, kc_ref, kp_ref, vc_ref, vp_ref, do_ref, delta_ref, lse_ref, dq_ref, dk_ref, dv_ref, has_prev,
             sink_ref, dsink_ref, ck, cv, sscr, dpscr, pscr, dsscr, dscr=None, ddscr=None):
        lane = lax.broadcasted_iota(jnp.int32, (BLK, LANES), 1)
        qi = lax.broadcasted_iota(jnp.int32, (BLK, BLK), 0)
        kj = lax.broadcasted_iota(jnp.int32, (BLK, BLK), 1)
        tri = kj <= qi
        eye = kj == qi
        cache = {}
        kp, vp = kp_ref, vp_ref
        for p in range(w // LANES):
            sl = slice(LANES * p, LANES * (p + 1))
            qpair, dopair = q_ref[:, sl], do_ref[:, sl]
            kcat, vcat = _kv_cat(kc_ref, kp, p // share, cache), _kv_cat(vc_ref, vp, p // share, cache)
            for hh in range(2):
                h = 2 * p + hh
                s = _dot_nt(_lane_half(qpair, hh), kcat)
                dp = _dot_nt(_lane_half(dopair, hh), vcat)
                if nb > 1:
                    sp = s[:, :BLK] if has_prev is True else jnp.where(has_prev, s[:, :BLK], NEG)
                    sscr[h] = jnp.where(tri, s[:, BLK:], sp)
                    dpscr[h] = jnp.where(tri, dp[:, BLK:], dp[:, :BLK])
                    if diag:
                        dscr[h] = jnp.where(eye, sp, NEG)
                        ddscr[h] = dp[:, :BLK]
                else:
                    sscr[h] = jnp.where(tri, s, NEG)
                    dpscr[h] = dp
        for p in range(w // LANES):
            for hh in range(2):
                h = 2 * p + hh
                lse_b = jnp.broadcast_to(lse_ref[:, h:h + 1], (BLK, BLK))
                delta = jnp.broadcast_to(delta_ref[:, h:h + 1], (BLK, BLK))
                pr = jnp.exp(sscr[h] - lse_b)
                ds = pr * (dpscr[h] - delta)
                if nb > 1:
                    if diag:
                        prd = jnp.exp(dscr[h] - lse_b)
                        dsd = prd * (ddscr[h] - delta)
                    else:
                        prd = dsd = 0.0
                    pscr[h, :, :BLK] = jnp.where(tri, prd, pr).astype(bf16)
                    pscr[h, :, BLK:] = jnp.where(tri, pr, 0.0).astype(bf16)
                    dsscr[h, :, :BLK] = jnp.where(tri, dsd, ds).astype(bf16)
                    dsscr[h, :, BLK:] = jnp.where(tri, ds, 0.0).astype(bf16)
                else:
                    pscr[h] = pr.astype(bf16)
                    dsscr[h] = ds.astype(bf16)
                if has_sink:
                    dsk = -jnp.sum(jnp.where(lane == 0, jnp.exp(sink_ref[0, h] - lse_b) * delta, 0.0), keepdims=True)
                    dsink_ref[h:h + 1, :] += jnp.broadcast_to(dsk, (1, LANES))
        for p in range(w // LANES):
            sl = slice(LANES * p, LANES * (p + 1))
            qpair, dopair = q_ref[:, sl], do_ref[:, sl]
            kcat = _kv_cat(kc_ref, kp, p // share, cache)
            dq_ref[:, sl] = _dot(dsscr[2 * p], _lane_half(kcat, 0)) + _dot(dsscr[2 * p + 1], _lane_half(kcat, 1))
            dk_pair = _dot_tn(dsscr[2 * p], _lane_half(qpair, 0)) + _dot_tn(dsscr[2 * p + 1], _lane_half(qpair, 1))
            dv_pair = _dot_tn(pscr[2 * p], _lane_half(dopair, 0)) + _dot_tn(pscr[2 * p + 1], _lane_half(dopair, 1))
            if nb > 1:
                dk_ref[:, sl] = dk_pair[BLK:] + ck[:, sl]
                dv_ref[:, sl] = dv_pair[BLK:] + cv[:, sl]
                ck[:, sl] = dk_pair[:BLK]
                cv[:, sl] = dv_pair[:BLK]
            else:
                dk_ref[:, sl] = dk_pair
                dv_ref[:, sl] = dv_pair

    assert max_dist in (BLK - 1, BLK) and w % wk == 0
    share = w // wk
    diag = nb > 1 and max_dist == BLK
    cur, prev, grid = _attn_specs(n, nb, True)
    in_specs = ([cur(w), cur(wk)] + ([prev(wk)] if nb > 1 else []) + [cur(wk)] + ([prev(wk)] if nb > 1 else [])
                + [cur(w), cur(LANES), cur(LANES)])
    args = [q, k] + ([k] if nb > 1 else []) + [v] + ([v] if nb > 1 else []) + [do, delta, lse]
    out_specs = [cur(w)] * 3
    out_shape = [jax.ShapeDtypeStruct((n, l, w), f32)] * 3
    if has_sink:
        in_specs = [pl.BlockSpec(memory_space=pltpu.SMEM)] + in_specs
        args = [sink] + args
        out_specs.append(pl.BlockSpec((NHEAD, LANES), lambda a, i: (0, 0)))
        out_shape.append(jax.ShapeDtypeStruct((NHEAD, LANES), f32))
    nh = w // HD
    scratch = [pltpu.VMEM((BLK, w), f32), pltpu.VMEM((BLK, w), f32)] if nb > 1 else []
    scratch += [pltpu.VMEM((nh, BLK, BLK), f32)] * 2 + [pltpu.VMEM((nh, BLK, 2 * BLK if nb > 1 else BLK), bf16)] * 2
    if diag:
        scratch += [pltpu.VMEM((nh, BLK, BLK), f32)] * 2
    return pl.pallas_call(
        body, name=name, grid=grid, in_specs=in_specs, out_specs=out_specs, out_shape=out_shape,
        scratch_shapes=scratch, compiler_params=_cp(("arbitrary", "arbitrary")),
    )(*args)


def _split2(x):
    hi = x.astype(bf16)
    return hi, (x - hi.astype(f32)).astype(bf16)


def _heads_to_lanes(xc, e):
    return sum(_dot(t, e) for t in _split2(xc))


def _lanes_to_heads(x, g):
    return sum(_dot(t, g) for t in _split2(x))


HEAD_EXPAND = (np.arange(LANES)[:, None] == np.arange(BW)[None, :] // HD).astype(np.float32)
HEAD_SUM = HEAD_EXPAND.T.copy()


def _branch_weights(l1_ref, l4_ref, l16_ref, scr):
    l4v = _perm_load(l4_ref, scr, 4)
    l16v = _perm_load(l16_ref, scr, 16)
    l1v = l1_ref[...]
    m = jnp.maximum(jnp.maximum(l1v, l4v), l16v)
    e1, e4, e16 = jnp.exp(l1v - m), jnp.exp(l4v - m), jnp.exp(l16v - m)
    z = e1 + e4 + e16
    return e1 / z, e4 / z, e16 / z


def _mix_out(oa, o1, l1, o4, l4, o16, l16, g_mix_a, g_mix_b, w_out, x, mod, g_post):
    def body(oa_ref, o1_ref, l1_ref, o4_ref, l4_ref, o16_ref, l16_ref, ga_ref, gb_ref, w_ref, x_ref, mod_ref, gp_ref, e_ref,
             x1_ref, y_ref, mixed_ref, ob_ref, scr):
        w1, w4, w16 = _branch_weights(l1_ref, l4_ref, l16_ref, scr)
        e = e_ref[...]
        x1w, x4w = _heads_to_lanes(w1, e), _heads_to_lanes(w4, e)
        ob = (x1w * o1_ref[...].astype(f32) + x4w * _perm_load(o4_ref, scr, 4)
              + (1.0 - x1w - x4w) * _perm_load(o16_ref, scr, 16))
        ob_ref[...] = ob
        oan, _ = _rms(oa_ref[...])
        obn, _ = _rms(ob)
        mixed = jnp.concatenate([oan * ga_ref[...], obn * gb_ref[...]], axis=1).astype(bf16)
        mixed_ref[...] = mixed
        y = _dot(mixed, w_ref[...])
        y_ref[...] = y
        yn, _ = _rms(y)
        x1_ref[...] = x_ref[...] + mod_ref[2:3, :] * (yn * gp_ref[...])

    nat = lambda w, dt: jax.ShapeDtypeStruct((BL, SEQ, w), dt)
    return pl.pallas_call(
        body, name="mix_out", grid=(BL, NJ),
        in_specs=[_tok(AQ), _tok(BW), _tok(LANES), _perm_spec(4, BW), _perm_spec(4, LANES), _perm_spec(16, BW),
                  _perm_spec(16, LANES), _full((1, AQ)), _full((1, BW)), _full((D, D)), _tok(D), MOD_SPEC, _full((1, D)),
                  _full((LANES, BW))],
        out_specs=[_tok(D), _tok(D), _tok(D), _tok(BW)],
        out_shape=[nat(D, f32), nat(D, f32), nat(D, bf16), nat(BW, f32)],
        scratch_shapes=[pltpu.VMEM((BW // LANES, TM, LANES), f32)],
        compiler_params=_cp(("arbitrary", "arbitrary")),
    )(oa, o1, l1, o4, l4, o16, l16, g_mix_a, g_mix_b, w_out, x, mod, g_post, jnp.asarray(HEAD_EXPAND, bf16))


def _mlp_up(x1, mod, g_pre, w_up):
    def body(x_ref, mod_ref, g_ref, w_ref, h_ref, u_ref, a_ref):
        xn, _ = _rms(x_ref[...])
        h = (xn * g_ref[...]) * (1.0 + mod_ref[4:5, :]) + mod_ref[3:4, :]
        hb = h.astype(bf16)
        h_ref[...] = hb
        for s in range(NCHIP):
            u = _dot(hb, w_ref[s])
            u_ref[:, D * s:D * (s + 1)] = u.astype(bf16)
            a_ref[:, D * s:D * (s + 1)] = jnp.square(jnp.maximum(u, 0.0)).astype(bf16)

    nat = lambda w: jax.ShapeDtypeStruct((BL, SEQ, w), bf16)
    return pl.pallas_call(
        body, name="mlp_up", grid=(BL, NJ),
        in_specs=[_tok(D), MOD_SPEC, _full((1, D)), _full((NCHIP, D, D))],
        out_specs=[_tok(D), _tok(DFF), _tok(DFF)], out_shape=[nat(D), nat(DFF), nat(DFF)],
        compiler_params=_cp(("arbitrary", "arbitrary")),
    )(x1, mod, g_pre, w_up)


def _mlp_down(a, w_down, x1, target, mod, g_post):
    def body(a_ref, w_ref, x_ref, t_ref, mod_ref, g_ref, gx_ref, dy_ref, accb_ref, accg_ref):
        _acc_init(accb_ref, accg_ref)
        y2 = _dot(a_ref[...], w_ref[...])
        yn, r = _rms(y2)
        g = g_ref[...]
        gt = mod_ref[5:6, :]
        n2 = yn * g
        err = x_ref[...] + gt * n2 - t_ref[...]
        gout = err * (1.0 / D)
        gx_ref[...] = gout
        dn2 = gout * gt
        dy_ref[...] = _rms_bwd(dn2 * g, yn, r).astype(bf16)
        accb_ref[0:1, :] += _colsum(gout * n2)
        accg_ref[0:1, :] += _colsum(dn2 * yn)
        accg_ref[1:2, :] += jnp.broadcast_to(jnp.sum(err * err, keepdims=True), (1, D))

    return pl.pallas_call(
        body, name="mlp_down", grid=(BL, NJ),
        in_specs=[_tok(DFF), _full((DFF, D)), _tok(D), _tok(D), MOD_SPEC, _full((1, D))],
        out_specs=[_tok(D), _tok(D), ACCB_SPEC, ACCG_SPEC],
        out_shape=[jax.ShapeDtypeStruct((BL, SEQ, D), f32), jax.ShapeDtypeStruct((BL, SEQ, D), bf16)] + ACC_SHAPES,
        compiler_params=_cp(("arbitrary", "arbitrary")),
    )(a, w_down, x1, target, mod, g_post)


def _mlp_bwd(dy2, u, w_down, w_up, x1, gx, mod, g_pre):
    def body(dy_ref, u_ref, wd_hbm, wu_hbm, x_ref, gx_ref, mod_ref, g_ref, du_ref, gx1_ref, accb_ref, accg_ref, wd, wu, sem):
        _acc_init(accb_ref, accg_ref)
        first = (pl.program_id(0) == 0) & (pl.program_id(1) == 0)
        c1 = pltpu.make_async_copy(wd_hbm, wd, sem.at[0])
        c2 = pltpu.make_async_copy(wu_hbm, wu, sem.at[1])

        @pl.when(first)
        def _():
            c1.start()
            c2.start()
            c1.wait()

        dy = dy_ref[...]
        for s in range(NCHIP):
            sl = slice(D * s, D * (s + 1))
            da = _dot_nt(dy, wd[sl, :])
            du_ref[:, sl] = (da * (2.0 * jnp.maximum(u_ref[:, sl].astype(f32), 0.0))).astype(bf16)

        @pl.when(first)
        def _():
            c2.wait()

        dh = jnp.zeros((TM, D), f32)
        for s in range(NCHIP):
            dh = dh + _dot_nt(du_ref[:, D * s:D * (s + 1)], wu[s])
        xn, r = _rms(x_ref[...])
        g = g_ref[...]
        n = xn * g
        dn = dh * (1.0 + mod_ref[4:5, :])
        gx1_ref[...] = gx_ref[...] + _rms_bwd(dn * g, xn, r)
        accb_ref[0:1, :] += _colsum(dh * n)
        accb_ref[1:2, :] += _colsum(dh)
        accg_ref[0:1, :] += _colsum(dn * xn)

    anyspec = pl.BlockSpec(memory_space=pl.ANY)
    return pl.pallas_call(
        body, name="mlp_bwd", grid=(BL, NJ),
        in_specs=[_tok(D), _tok(DFF), anyspec, anyspec, _tok(D), _tok(D), MOD_SPEC, _full((1, D))],
        out_specs=[_tok(DFF), _tok(D), ACCB_SPEC, ACCG_SPEC],
        out_shape=[jax.ShapeDtypeStruct((BL, SEQ, DFF), bf16), jax.ShapeDtypeStruct((BL, SEQ, D), f32)] + ACC_SHAPES,
        scratch_shapes=[pltpu.VMEM((DFF, D), bf16), pltpu.VMEM((NCHIP, D, D), bf16), pltpu.SemaphoreType.DMA((2,))],
        compiler_params=_cp(("arbitrary", "arbitrary")),
    )(dy2, u, w_down, w_up, x1, gx, mod, g_pre)


def _matmul_tn(a, b, *, tn, col_blocked, name, out_dtype=f32):
    t, m = a.shape
    n = b.shape[1]
    tmm = min(m, 1024)
    tk = 2048 if tn <= 1024 else 1024
    nk = t // tk

    def body(a_ref, b_ref, o_ref, acc):
        k = pl.program_id(2)

        @pl.when(k == 0)
        def _():
            acc[...] = jnp.zeros_like(acc)

        acc[...] += _dot_tn(a_ref[...], b_ref[...])

        @pl.when(k == nk - 1)
        def _():
            o_ref[...] = acc[...].astype(out_dtype)

    if col_blocked:
        out_spec = pl.BlockSpec((None, tmm, tn), lambda i, j, k: (j, i, 0))
        out_shape = jax.ShapeDtypeStruct((n // tn, m, tn), out_dtype)
    else:
        out_spec = pl.BlockSpec((tmm, tn), lambda i, j, k: (i, j))
        out_shape = jax.ShapeDtypeStruct((m, n), out_dtype)
    return pl.pallas_call(
        body, name=name, grid=(m // tmm, n // tn, nk),
        in_specs=[pl.BlockSpec((tk, tmm), lambda i, j, k: (k, i)), pl.BlockSpec((tk, tn), lambda i, j, k: (k, j))],
        out_specs=out_spec, out_shape=out_shape, scratch_shapes=[pltpu.VMEM((tmm, tn), f32)],
        compiler_params=_cp(("arbitrary", "arbitrary", "arbitrary")),
    )(a, b)


def _grad_w_in(h, dproj):
    t = h.shape[0]
    tk = 1024
    nk = t // tk
    sw = INW // NCHIP

    def body(a_ref, b_ref, o_ref, acc):
        k = pl.program_id(0)

        @pl.when(k == 0)
        def _():
            acc[...] = jnp.zeros_like(acc)

        acc[...] += _dot_tn(a_ref[...], b_ref[...])

        @pl.when(k == nk - 1)
        def _():
            for s in range(NCHIP):
                o_ref[s] = acc[:, sw * s:sw * (s + 1)].astype(bf16)

    return pl.pallas_call(
        body, name="grad_w_in", grid=(nk,),
        in_specs=[pl.BlockSpec((tk, D), lambda k: (k, 0)), pl.BlockSpec((tk, INW), lambda k: (k, 0))],
        out_specs=pl.BlockSpec((NCHIP, D, sw), lambda k: (0, 0, 0)), out_shape=jax.ShapeDtypeStruct((NCHIP, D, sw), bf16),
        scratch_shapes=[pltpu.VMEM((D, INW), f32)], compiler_params=_cp(("arbitrary",)),
    )(h, dproj)


def _attn_out_bwd(gx1, y, mod, g_post, w_out, oa, ob, g_mix_a, g_mix_b, l1, l4, l16):
    def body(gx_ref, y_ref, mod_ref, gp_ref, w_ref, oa_ref, ob_ref, ga_ref, gb_ref, l1_ref, l4_ref, l16_ref, e_ref, g_ref,
             dy_ref, doa_ref, do1_ref, do4_ref, do16_ref, da_ref, d1_ref, d4_ref, d16_ref, accb_ref, accg_ref, scr):
        _acc_init(accb_ref, accg_ref)
        w1, w4, w16 = _branch_weights(l1_ref, l4_ref, l16_ref, scr)
        e, hs = e_ref[...], g_ref[...]
        gx1v = gx_ref[...]
        yn, ry = _rms(y_ref[...])
        gp = gp_ref[...]
        gt = mod_ref[2:3, :]
        dn1 = gx1v * gt
        dy = _rms_bwd(dn1 * gp, yn, ry).astype(bf16)
        dy_ref[...] = dy
        dmixed = _dot_nt(dy, w_ref[...])
        dma, dmb = dmixed[:, :AQ], dmixed[:, AQ:]
        oa, ob = oa_ref[...], ob_ref[...]
        oan, ra = _rms(oa)
        obn, rb = _rms(ob)
        doa = _rms_bwd(dma * ga_ref[...], oan, ra)
        doa_ref[...] = doa.astype(bf16)
        da_ref[...] = _lanes_to_heads(doa * oa, hs)
        dob = _rms_bwd(dmb * gb_ref[...], obn, rb)
        dd = _lanes_to_heads(dob * ob, hs)
        x1w, x4w = _heads_to_lanes(w1, e), _heads_to_lanes(w4, e)
        do1_ref[...] = (x1w * dob).astype(bf16)
        d1_ref[...] = w1 * dd
        _perm_store(x4w * dob, scr, do4_ref, 4)
        _perm_store(w4 * dd, scr, d4_ref, 4)
        _perm_store((1.0 - x1w - x4w) * dob, scr, do16_ref, 16)
        _perm_store(w16 * dd, scr, d16_ref, 16)
        accb_ref[0:1, :] += _colsum(gx1v * (yn * gp))
        accg_ref[0:1, :] += _colsum(dn1 * yn)
        accg_ref[1:2, :] += jnp.concatenate([_colsum(dma * oan), _colsum(dmb * obn)], axis=1)

    nat = lambda w, dt: jax.ShapeDtypeStruct((BL, SEQ, w), dt)
    return pl.pallas_call(
        body, name="attn_out_bwd", grid=(BL, NJ),
        in_specs=[_tok(D), _tok(D), MOD_SPEC, _full((1, D)), _full((D, D)), _tok(AQ), _tok(BW), _full((1, AQ)), _full((1, BW)),
                  _tok(LANES), _perm_spec(4, LANES), _perm_spec(16, LANES), _full((LANES, BW)), _full((BW, LANES))],
        out_specs=[_tok(D), _tok(AQ), _tok(BW), _perm_spec(4, BW), _perm_spec(16, BW),
                   _tok(LANES), _tok(LANES), _perm_spec(4, LANES), _perm_spec(16, LANES), ACCB_SPEC, ACCG_SPEC],
        out_shape=[nat(D, bf16), nat(AQ, bf16), nat(BW, bf16), jax.ShapeDtypeStruct((BL, 4, SEQ // 4, BW), bf16),
                   jax.ShapeDtypeStruct((BL, 16, SEQ // 16, BW), bf16), nat(LANES, f32), nat(LANES, f32),
                   jax.ShapeDtypeStruct((BL, 4, SEQ // 4, LANES), f32), jax.ShapeDtypeStruct((BL, 16, SEQ // 16, LANES), f32)]
                  + ACC_SHAPES,
        scratch_shapes=[pltpu.VMEM((BW // LANES, TM, LANES), f32)],
        compiler_params=_cp(("arbitrary", "arbitrary")),
    )(gx1, y, mod, g_post, w_out, oa, ob, g_mix_a, g_mix_b, l1, l4, l16, jnp.asarray(HEAD_EXPAND, bf16),
      jnp.asarray(HEAD_SUM, bf16))


def _attn_in_bwd(dqa, dka, dva, d1, d4, d16, tc, ts1, ts2, w_in, x, gx1, mod, g_pre):
    def body(dqa_ref, dka_ref, dva_ref, dq1_ref, dk1_ref, dv1_ref, dq4_ref, dk4_ref, dv4_ref, dq16_ref, dk16_ref, dv16_ref,
             c_ref, s1_ref, s2_ref, w_ref, x_ref, gx_ref, mod_ref, g_ref, dproj_ref, dx_ref, accb_ref, accg_ref, scr):
        _acc_init(accb_ref, accg_ref)
        c, s1, s2 = c_ref[...], s1_ref[...], s2_ref[...]
        tot = lambda r1, r4, r16: r1[...] + _perm_load(r4, scr, 4) + _perm_load(r16, scr, 16)
        dqb = tot(dq1_ref, dq4_ref, dq16_ref)
        dkb = tot(dk1_ref, dk4_ref, dk16_ref)
        dvb = tot(dv1_ref, dv4_ref, dv16_ref)
        dproj = jnp.concatenate([
            _rope_t(dqa_ref[...], c, s1, s2) * QSCALE, _rope_t(_per_kv_head(dka_ref[...]), c, s1, s2),
            _per_kv_head(dva_ref[...]),
            _rope_t(dqb, c, s1, s2) * QSCALE, _rope_t(dkb, c, s1, s2), dvb], axis=1).astype(bf16)
        dproj_ref[...] = dproj
        dh = _dot_nt(dproj, w_ref[...])
        xn, r = _rms(x_ref[...])
        g = g_ref[...]
        dn = dh * (1.0 + mod_ref[1:2, :])
        dx_ref[...] = gx_ref[...] + _rms_bwd(dn * g, xn, r)
        accb_ref[0:1, :] += _colsum(dh * (xn * g))
        accb_ref[1:2, :] += _colsum(dh)
        accg_ref[0:1, :] += _colsum(dn * xn)

    return pl.pallas_call(
        body, name="attn_in_bwd", grid=(BL, NJ),
        in_specs=[_tok(AQ), _tok(AQ), _tok(AQ)] + [_tok(BW)] * 3 + [_perm_spec(4, BW)] * 3 + [_perm_spec(16, BW)] * 3
                 + [_tok(LANES)] * 3 + [_full((D, INW)), _tok(D), _tok(D), MOD_SPEC, _full((1, D))],
        out_specs=[_tok(INW), _tok(D), ACCB_SPEC, ACCG_SPEC],
        out_shape=[jax.ShapeDtypeStruct((BL, SEQ, INW), bf16), jax.ShapeDtypeStruct((BL, SEQ, D), f32)] + ACC_SHAPES,
        scratch_shapes=[pltpu.VMEM((BW // LANES, TM, LANES), f32)],
        compiler_params=_cp(("arbitrary", "arbitrary")),
    )(dqa, dka, dva, *d1, *d4, *d16, tc, ts1, ts2, w_in, x, gx1, mod, g_pre)


def _inv_lane():
    inv = np.float32(THETA) ** (-np.arange(0, ROT, 2, dtype=np.float32) / np.float32(ROT))
    lane = np.arange(LANES) % HD
    return jnp.asarray(np.where(lane < ROT, inv[lane % (ROT // 2)], 0.0).astype(np.float32)[None, :])


def _local_step(x, tabs, mod, target, w_in, later_weights, grad_ready, g_attn_pre,
                g_attn_post, sink_a, g_mix_a, g_mix_b, g_mlp_pre, g_mlp_post):
    tc, ts1, ts2 = [t.reshape(BL, SEQ, LANES) for t in tabs]

    (h, qa, ka, va, q1, k1, v1, q4, k4, v4, q16, k16, v16, w_in) = _attn_in(x, mod, g_attn_pre, w_in, tc, ts1, ts2)
    seqs = lambda t: t.reshape(t.shape[0] * t.shape[1], t.shape[2], t.shape[3])
    q4, k4, v4, q16, k16, v16 = [seqs(t) for t in (q4, k4, v4, q16, k16, v16)]
    oa, la = _attn_fwd(qa, ka, va, sink_a, max_dist=BLK - 1, o_dtype=f32, name="attn_a_fwd")
    o1, l1 = _attn_fwd(q1, k1, v1, None, max_dist=BLK, o_dtype=bf16, name="attn_b1_fwd")
    o4, l4 = _attn_fwd(q4, k4, v4, None, max_dist=BLK, o_dtype=bf16, name="attn_b4_fwd")
    o16, l16 = _attn_fwd(q16, k16, v16, None, max_dist=BLK, o_dtype=bf16, name="attn_b16_fwd")
    b4 = lambda t: t.reshape(BL, 4, SEQ // 4, t.shape[-1])
    b16 = lambda t: t.reshape(BL, 16, SEQ // 16, t.shape[-1])
    w_out, mlp_weights, mod = later_weights((oa, o1, o4, o16), mod)
    x1, y, mixed, ob = _mix_out(oa, o1, l1, b4(o4), b4(l4), b16(o16), b16(l16), g_mix_a, g_mix_b, w_out, x, mod, g_attn_post)
    w_up, w_down = mlp_weights((x1,))
    h2, u, a = _mlp_up(x1, mod, g_mlp_pre, w_up)
    gx, dy2, accb_d, accg_d = _mlp_down(a, w_down, x1, target, mod, g_mlp_post)

    flat = lambda t: t.reshape(BL * SEQ, t.shape[-1])
    mod = grad_ready("w_down", _matmul_tn(flat(a), flat(dy2), tn=D, col_blocked=False, name="grad_w_down", out_dtype=bf16), mod)
    du, gx1, accb_m, accg_m = _mlp_bwd(dy2, u, w_down, w_up, x1, gx, mod, g_mlp_pre)
    mod = grad_ready("w_up", _matmul_tn(flat(h2), flat(du), tn=D, col_blocked=True, name="grad_w_up", out_dtype=bf16), mod)

    dy, doa, do1, do4, do16, da, dl1, dl4, dl16, accb_o, accg_o = _attn_out_bwd(
        gx1, y, mod, g_attn_post, w_out, oa, ob, g_mix_a, g_mix_b, l1, b4(l4), b16(l16))
    sink_behind = grad_ready("w_out", _matmul_tn(flat(mixed), flat(dy), tn=D, col_blocked=False, name="grad_w_out",
                                                  out_dtype=bf16), sink_a)
    dqa, dka, dva, dsink = _attn_bwd(qa, ka, va, doa, da, la, sink_behind, max_dist=BLK - 1, name="attn_a_bwd")
    d1 = _attn_bwd(q1, k1, v1, do1, dl1, l1, None, max_dist=BLK, name="attn_b1_bwd")
    d4 = _attn_bwd(q4, k4, v4, seqs(do4), seqs(dl4), l4, None, max_dist=BLK, name="attn_b4_bwd")
    d16 = _attn_bwd(q16, k16, v16, seqs(do16), seqs(dl16), l16, None, max_dist=BLK, name="attn_b16_bwd")
    dproj, grad_x, accb_i, accg_i = _attn_in_bwd(dqa, dka, dva, d1, [b4(t) for t in d4], [b16(t) for t in d16],
                                                 tc, ts1, ts2, w_in, x, gx1, mod, g_attn_pre)
    gw_in = _grad_w_in(flat(h), flat(dproj))
    dsink = grad_ready("w_in", gw_in, dsink)

    return grad_x, (accb_i, accb_o, accb_m, accb_d, accg_i, accg_o, accg_m, accg_d, dsink)


ADAW = NMOD * D // NCHIP


def _pos():
    return lax.axis_index("x"), lax.axis_index("y"), lax.axis_index("c")


def _flip(v, bit):
    return 1 - v if bit else v


def _all_peers(x, y, c):
    return [(_flip(x, k >> 2 & 1), _flip(y, k >> 1 & 1), _flip(c, k & 1)) for k in range(1, NDEV)]


def _other_chips(x, y):
    return [(1 - x, y), (x, 1 - y), (1 - x, 1 - y)]


def _rcopy(src, dst, send, recv, k, dev, k_recv=None):
    return pltpu.make_async_remote_copy(src_ref=src, dst_ref=dst, send_sem=send.at[k],
                                        recv_sem=recv.at[k if k_recv is None else k_recv],
                                        device_id=dev, device_id_type=MESH)


def _small_copies(src, land, send, recv):
    x, y, c = _pos()
    me = 4 * x + 2 * y + c
    return [(_rcopy(src, land.at[me], send, recv, k, p), _rcopy(src, land.at[4 * p[0] + 2 * p[1] + p[2]], send, recv, k, p))
            for k, p in enumerate(_all_peers(x, y, c))]


def _ada_fwd(c_in, landed, w_ada, b_cols):
    def body(c_ref, land, w_hbm, b_ref, mod_ref, cond_ref, mbuf, w_ref, s2, r2, wsem):
        x, y, c = _pos()
        chip = 2 * x + y
        me = 4 * x + 2 * y + c
        wcopy = pltpu.make_async_copy(w_hbm, w_ref, wsem)
        wcopy.start()
        for i in range(NDEV):
            @pl.when(me == i)
            def _():
                cond_ref[BL * i:BL * (i + 1), :] = c_ref[...]

            @pl.when(me != i)
            def _():
                cond_ref[BL * i:BL * (i + 1), :] = land[i]
        call = cond_ref[...]
        cond = call / (1.0 + jnp.exp(-call))
        cond_ref[...] = cond
        wcopy.wait()
        mbuf[chip] = _dot(cond.astype(bf16), w_ref[...].astype(bf16)) + b_ref[...]
        chips = _other_chips(x, y)
        sends = [_rcopy(mbuf.at[chip], mbuf.at[chip], s2, r2, j, (px, py, c)) for j, (px, py) in enumerate(chips)]
        for cp in sends:
            cp.start()
        for j, (px, py) in enumerate(chips):
            _rcopy(mbuf.at[chip], mbuf.at[2 * px + py], s2, r2, j, (px, py, c)).wait_recv()
        for cp in sends:
            cp.wait_send()
        row = lax.broadcasted_iota(jnp.int32, (BL * NDEV, ADAW), 0)
        for s in range(NCHIP):
            slab = mbuf[s]
            for j in range(BL):
                mod_ref[j:j + 1, ADAW * s:ADAW * (s + 1)] = jnp.sum(jnp.where(row == BL * me + j, slab, 0.0), axis=0, keepdims=True)

    vm = pl.BlockSpec(memory_space=pltpu.VMEM)
    return pl.pallas_call(
        body, name="ada_fwd", in_specs=[vm, vm, pl.BlockSpec(memory_space=pl.ANY), vm], out_specs=[vm, vm],
        out_shape=[jax.ShapeDtypeStruct((BL, NMOD * D), f32), jax.ShapeDtypeStruct((BL * NDEV, D), f32)],
        scratch_shapes=[pltpu.VMEM((NCHIP, BL * NDEV, ADAW), f32), pltpu.VMEM((D, ADAW), f32),
                        pltpu.SemaphoreType.DMA((NCHIP - 1,)), pltpu.SemaphoreType.DMA((NCHIP - 1,)),
                        pltpu.SemaphoreType.DMA],
        compiler_params=pltpu.CompilerParams(vmem_limit_bytes=VMEM_LIMIT),
    )(c_in, landed, w_ada, b_cols)


PAY_ROWS = 4


def _small_pack(accs):
    def body(bi, bo, bm, bd, gi, go, gm, gd, dsink, pay):
        pay[...] = jnp.zeros_like(pay)
        for b in range(BL):
            for k, (ref, r) in enumerate(((bi, 1), (bi, 0), (bo, 0), (bm, 1), (bm, 0), (bd, 0))):
                pay[b:b + 1, D * k:D * (k + 1)] = ref[b, r:r + 1, :]
        for off, ref, r in ((OFF_G_ATTN_PRE, gi, 0), (OFF_G_ATTN_POST, go, 0), (OFF_G_MIX_A, go, 1), (OFF_G_MLP_PRE, gm, 0),
                            (OFF_G_MLP_POST, gd, 0)):
            pay[BL:BL + 1, off:off + D] = ref[r:r + 1, :]
        eye = lax.broadcasted_iota(jnp.int32, (NHEAD, LANES), 0) == lax.broadcasted_iota(jnp.int32, (NHEAD, LANES), 1)
        pay[BL:BL + 1, OFF_SINK:OFF_SINK + LANES] = jnp.sum(jnp.where(eye, dsink[...], 0.0), axis=0, keepdims=True)
        pay[BL:BL + 1, OFF_LOSS:OFF_LOSS + LANES] = gd[1:2, 0:LANES]

    vm = pl.BlockSpec(memory_space=pltpu.VMEM)
    return pl.pallas_call(body, name="small_pack", in_specs=[vm] * 9, out_specs=vm,
                          out_shape=jax.ShapeDtypeStruct((PAY_ROWS, PAYW), f32))(*accs)


def _small_sum(own, landed, cond_all):
    def body(pay, land, cond_ref, gw_ref, gb_ref, small_ref, pbuf, dall):
        x, y, c = _pos()
        chip = 2 * x + y
        me = 4 * x + 2 * y + c
        for i in range(NDEV):
            @pl.when(me == i)
            def _():
                pbuf[i] = pay[...]

            @pl.when(me != i)
            def _():
                pbuf[i] = land[i]
        small = pbuf[0, BL:BL + 1, :]
        for i in range(1, NDEV):
            small = small + pbuf[i, BL:BL + 1, :]
        small_ref[...] = small
        for i in range(NDEV):
            dall[BL * i:BL * (i + 1), :] = pbuf[i, 0:BL, :]
        gb_ref[...] = jnp.sum(dall[...], axis=0, keepdims=True)
        cols = jnp.zeros((BL * NDEV, ADAW), f32)
        for s in range(NCHIP):
            cols = cols + jnp.where(chip == s, dall[:, ADAW * s:ADAW * (s + 1)], 0.0)
        gw_ref[...] = _dot_tn(cond_ref[...].astype(bf16), cols.astype(bf16))

    vm = pl.BlockSpec(memory_space=pltpu.VMEM)
    return pl.pallas_call(
        body, name="small_sum", in_specs=[vm] * 3, out_specs=[vm] * 3,
        out_shape=[jax.ShapeDtypeStruct((D, ADAW), f32), jax.ShapeDtypeStruct((1, PAYW), f32), jax.ShapeDtypeStruct((1, PAYW), f32)],
        scratch_shapes=[pltpu.VMEM((NDEV, PAY_ROWS, PAYW), f32), pltpu.VMEM((BL * NDEV, PAYW), f32)],
        compiler_params=pltpu.CompilerParams(vmem_limit_bytes=VMEM_LIMIT),
    )(own, landed, cond_all)


def _half(ref, c):
    r2 = ref.shape[0] // 2
    return ref.at[pl.ds(c * r2 if isinstance(c, int) else pl.multiple_of(c * r2, 16), r2), :]


HBM_SPEC = pl.BlockSpec(memory_space=pltpu.HBM)
SEM_SPEC = pl.BlockSpec(memory_space=pltpu.SEMAPHORE)
EFFECT = pltpu.SideEffectType.DATAFLOW_SIDE_EFFECTING
NLINK = NCHIP - 1


def _in_hbm(a):
    return pltpu.with_memory_space_constraint(a, pltpu.HBM)


NSEM = 8


def _split_start(name, srcs, land_shapes, builds, carry, after=(), lands=None):
    n = len(srcs)
    na, nc = len(after), len(carry)

    def body(*refs):
        src, land = refs[:n], refs[n:2 * n]
        kept = refs[2 * n + na:2 * n + na + nc]
        outs = refs[2 * n + na + nc:]
        send, recv, passed = outs[:n], outs[n:2 * n], outs[4 * n:]
        for t in range(n):
            for out_cp, _ in builds[t](src[t], land[t], send[t], recv[t]):
                out_cp.start()
        for a, b in zip(kept, passed):
            b[...] = a[...]

    if lands is None:
        lands = [lax.empty(s.shape, s.dtype) for s in land_shapes]
    lands = [_in_hbm(a) for a in lands]
    sems = [pltpu.SemaphoreType.DMA((NSEM,))] * (2 * n)
    thru = [pltpu.HBM(a.shape, a.dtype) for a in list(srcs) + lands]
    vm = pl.BlockSpec(memory_space=pltpu.VMEM)
    res = pl.pallas_call(
        body, name=name, out_shape=sems + thru + [jax.ShapeDtypeStruct(a.shape, a.dtype) for a in carry],
        in_specs=[HBM_SPEC] * (2 * n) + [pl.BlockSpec(memory_space=pl.ANY)] * na + [vm] * nc,
        out_specs=[SEM_SPEC] * (2 * n) + [HBM_SPEC] * (2 * n) + [vm] * nc,
        input_output_aliases={i: 2 * n + i for i in range(2 * n)},
        compiler_params=pltpu.CompilerParams(has_side_effects=EFFECT),
    )(*[_in_hbm(a) for a in srcs], *lands, *after, *carry)
    flight = [(res[2 * n + t], res[3 * n + t], res[t], res[n + t]) for t in range(n)]
    return flight, list(res[4 * n:])


def _split_wait(name, flight, builds, after):
    m = len(flight)
    na = len(after)

    def body(*refs):
        src, land, send, recv = refs[:m], refs[m:2 * m], refs[2 * m:3 * m], refs[3 * m:4 * m]
        for t in range(m):
            for out_cp, in_cp in builds[t](src[t], land[t], send[t], recv[t]):
                out_cp.wait_send()
                in_cp.wait_recv()

    ops = [f[0] for f in flight] + [f[1] for f in flight] + [f[2] for f in flight] + [f[3] for f in flight]
    res = pl.pallas_call(
        body, name=name, out_shape=[pltpu.HBM(a.shape, a.dtype) for a in ops[:2 * m]],
        in_specs=[HBM_SPEC] * (2 * m) + [SEM_SPEC] * (2 * m) + [pl.BlockSpec(memory_space=pl.ANY)] * na,
        out_specs=[HBM_SPEC] * (2 * m), input_output_aliases={i: i for i in range(2 * m)},
        compiler_params=pltpu.CompilerParams(has_side_effects=EFFECT),
    )(*ops, *after)
    return res[:m], res[m:2 * m]


def _weight_copies(src, land, send, recv):
    x, y, c = _pos()
    chip = 2 * x + y
    return [(_rcopy(_half(src, c), _half(land.at[chip], c), send, recv, j, (px, py, c)),
             _rcopy(_half(src, c), _half(land.at[2 * px + py], c), send, recv, j, (px, py, c)))
            for j, (px, py) in enumerate(_other_chips(x, y))]


NDIRECT = NDEV - 1


def _direct_grad_copies(src, land, send, recv):
    x, y, c = _pos()
    out, arrive = [], []
    for j, (px, py) in enumerate(_other_chips(x, y)):
        for hc in range(2):
            out.append(_rcopy(_half(src.at[2 * px + py], hc), land.at[2 * j + c], send, recv, 2 * j + hc, (px, py, hc),
                              k_recv=2 * j + c))
            arrive.append(_rcopy(_half(src.at[2 * px + py], hc), land.at[2 * j + hc], send, recv, 2 * j + hc, (px, py, hc)))
    own = _rcopy(_half(src.at[2 * x + y], 1 - c), land.at[NDIRECT - 1], send, recv, NDIRECT - 1, (x, y, 1 - c))
    return list(zip(out, arrive)) + [(own, own)]


def _pair_weight_copies(src, land, send, recv):
    x, y, c = _pos()
    sib = (x, y, 1 - c)
    cps = []
    for j, (px, py) in enumerate(_other_chips(x, y)):
        mine, theirs = _half(land.at[2 * px + py], c), _half(land.at[2 * px + py], 1 - c)
        cps.append((_rcopy(mine, mine, send, recv, j, sib), _rcopy(theirs, theirs, send, recv, j, sib)))
    own = _rcopy(src, land.at[2 * x + y], send, recv, NLINK, sib)
    return cps + [(own, own)]


RS_ROWS = 256


def _chip_add(own, landed, pos_arr, name):
    nl, r2, cw = landed.shape
    rows = min(RS_ROWS, r2)
    nr = r2 // rows

    def body(s_ref, h_ref, q_ref, o_ref):
        acc = h_ref[...].astype(f32)
        for j in range(nl):
            acc = acc + q_ref[j].astype(f32)
        o_ref[...] = acc

    gs = pltpu.PrefetchScalarGridSpec(
        num_scalar_prefetch=1, grid=(nr,),
        in_specs=[pl.BlockSpec((None, rows, cw), lambda j, s: (s[0], s[1] * nr + j, 0)),
                  pl.BlockSpec((nl, rows, cw), lambda j, s: (0, j, 0))],
        out_specs=pl.BlockSpec((rows, cw), lambda j, s: (s[1] * nr + j, 0)))
    return pl.pallas_call(body, name=name, grid_spec=gs, out_shape=jax.ShapeDtypeStruct((2 * r2, cw), f32),
                          compiler_params=_cp(("arbitrary",)))(pos_arr, own, landed)


def _pair_gather_copies(src, land, send, recv):
    x, y, c = _pos()
    sib = (x, y, 1 - c)
    return [(_rcopy(_half(land, c), _half(land, c), send, recv, 0, sib),
             _rcopy(_half(land, 1 - c), _half(land, 1 - c), send, recv, 0, sib))]


def _adamw_math(w, g, m, v):
    m = B1 * m + (1.0 - B1) * g
    v = B2 * v + (1.0 - B2) * jnp.square(g)
    m_hat = m / (1.0 - B1 ** STEP)
    v_hat = v / (1.0 - B2 ** STEP)
    return -LR * (m_hat / (jnp.sqrt(v_hat) + AEPS) + WD * w), m, v


ADAM_BLOCK = 512 * 1024


def _adamw(w, g, m, v, name, after=(), landed=True):
    r, cw = w.shape
    na = len(after)

    def body(w_ref, g_ref, m_ref, v_ref, *rest):
        outs = rest[na:]
        g = g_ref[...]
        if landed:
            outs[0][...] = g
        outs[-3][...], outs[-2][...], outs[-1][...] = _adamw_math(w_ref[...], g, m_ref[...], v_ref[...])

    rows = max(k for k in range(SUBLANES, ADAM_BLOCK // cw + 1, SUBLANES) if r % k == 0)
    spec = pl.BlockSpec((rows, cw), lambda i: (i, 0))
    nout = 4 if landed else 3
    res = pl.pallas_call(body, name=name, grid=(r // rows,), in_specs=[spec] * 4 + [pl.BlockSpec(memory_space=pl.ANY)] * na,
                         out_specs=[spec] * nout, out_shape=[jax.ShapeDtypeStruct((r, cw), f32)] * nout,
                         compiler_params=_cp(("arbitrary",)))(w, g, m, v, *after)
    return list(res) if landed else [g, *res]


SMALL = (("b_ada", None, PAYW), ("g_attn_pre", OFF_G_ATTN_PRE, D), ("g_attn_post", OFF_G_ATTN_POST, D), ("sink_a", OFF_SINK, 8),
         ("g_mix_a", OFF_G_MIX_A, AQ), ("g_mix_b", OFF_G_MIX_B, BW), ("g_mlp_pre", OFF_G_MLP_PRE, D), ("g_mlp_post", OFF_G_MLP_POST, D))


def _adamw_small(small, gb, params):
    n = len(SMALL)

    def body(*refs):
        small_ref, gb_ref = refs[:2]
        wmv = refs[2:2 + 3 * n]
        loss_ref = refs[2 + 3 * n]
        outs = refs[3 + 3 * n:]
        loss_ref[...] = small_ref[:, OFF_LOSS:OFF_LOSS + 1] * (0.5 / D)
        for i, (_, off, width) in enumerate(SMALL):
            g = gb_ref[...] if off is None else small_ref[:, off:off + width]
            w_ref, m_ref, v_ref = wmv[3 * i:3 * i + 3]
            outs[4 * i][...] = g
            outs[4 * i + 1][...], outs[4 * i + 2][...], outs[4 * i + 3][...] = _adamw_math(w_ref[...], g, m_ref[...], v_ref[...])

    vm = pl.BlockSpec(memory_space=pltpu.VMEM)
    out_shape = [jax.ShapeDtypeStruct((1, 1), f32)]
    for _, _, width in SMALL:
        out_shape += [jax.ShapeDtypeStruct((1, width), f32)] * 4
    flat = [a for wmv in params for a in wmv]
    res = pl.pallas_call(body, name="adamw_small", in_specs=[vm] * (2 + 3 * n), out_specs=[vm] * len(out_shape),
                         out_shape=out_shape)(small, gb, *flat)
    return res[0], {name: res[1 + 4 * i:5 + 4 * i] for i, (name, _, _) in enumerate(SMALL)}


def kernel(x, c, positions, w_ada, b_ada, g_attn_pre, g_attn_post, w_in, sink_a, g_mix_a, g_mix_b, w_out, g_mlp_pre, g_mlp_post, w_up, w_down, loss_target, m_w_ada, m_b_ada, m_g_attn_pre, m_g_attn_post, m_w_in, m_sink_a, m_g_mix_a, m_g_mix_b, m_w_out, m_g_mlp_pre, m_g_mlp_post, m_w_up, m_w_down, v_w_ada, v_b_ada, v_g_attn_pre, v_g_attn_post, v_w_in, v_sink_a, v_g_mix_a, v_g_mix_b, v_w_out, v_g_mlp_pre, v_g_mlp_post, v_w_up, v_w_down):
    given = dict(w_ada=w_ada, b_ada=b_ada, g_attn_pre=g_attn_pre, g_attn_post=g_attn_post, w_in=w_in, sink_a=sink_a, g_mix_a=g_mix_a,
                 g_mix_b=g_mix_b, w_out=w_out, g_mlp_pre=g_mlp_pre, g_mlp_post=g_mlp_post, w_up=w_up, w_down=w_down)
    moms = dict(w_ada=(m_w_ada, v_w_ada), b_ada=(m_b_ada, v_b_ada), g_attn_pre=(m_g_attn_pre, v_g_attn_pre),
                g_attn_post=(m_g_attn_post, v_g_attn_post), w_in=(m_w_in, v_w_in), sink_a=(m_sink_a, v_sink_a),
                g_mix_a=(m_g_mix_a, v_g_mix_a), g_mix_b=(m_g_mix_b, v_g_mix_b), w_out=(m_w_out, v_w_out),
                g_mlp_pre=(m_g_mlp_pre, v_g_mlp_pre), g_mlp_post=(m_g_mlp_post, v_g_mlp_post), w_up=(m_w_up, v_w_up),
                w_down=(m_w_down, v_w_down))
    order = ["w_ada", "b_ada", "g_attn_pre", "g_attn_post", "w_in", "sink_a", "g_mix_a", "g_mix_b", "w_out", "g_mlp_pre",
             "g_mlp_post", "w_up", "w_down"]
    xi, yi, ci = _pos()
    chip = 2 * xi + yi

    pos_arr = jnp.stack([chip, ci]).astype(jnp.int32)
    big = ("w_in", "w_out", "w_up", "w_down")

    gathered = [jax.ShapeDtypeStruct((NCHIP,) + given[n].shape[1:], bf16) for n in big]
    (flight_c, *flight_in), (inv_lane,) = _split_start(
        "weights_start_first", [c, w_in[0].astype(bf16)], [jax.ShapeDtypeStruct((NDEV, BL, D), f32), gathered[0]],
        [_small_copies, _weight_copies], [_inv_lane()])
    inv_lane, rest = lax.optimization_barrier((inv_lane, [given[n][0] for n in big[1:]]))
    tabs = _rope_tables(positions.reshape(BL * SEQ, 1), inv_lane)
    rest = [w.astype(bf16) for w in rest]
    b_cols = lax.dynamic_slice(b_ada, (0, chip * ADAW), (1, ADAW))
    (c_own,), (c_all,) = _split_wait("cond_wait", [flight_c], [_small_copies], (*tabs, *rest))
    mod, cond_all = _ada_fwd(c_own, c_all, w_ada[0], b_cols)

    srcs, lands = _split_wait("weights_wait_first", flight_in, [_weight_copies], (mod,))
    cross, (mod,) = _split_start("weights_pair_start_first", srcs, None, [_pair_weight_copies], [mod], lands=lands)
    flight_rest, (mod,) = _split_start("weights_start_rest", rest, gathered[1:], [_weight_copies] * 3, [mod])
    _, (win_g,) = _split_wait("weights_pair_wait_first", cross, [_pair_weight_copies], (mod,))
    mod = mod.reshape(BL, NMOD, D)

    def later_weights(after, carry):
        srcs, lands = _split_wait("weights_wait_rest", flight_rest, [_weight_copies] * 3, after)
        fl, (carry,) = _split_start("weights_pair_start_rest", srcs, None, [_pair_weight_copies] * 3, [carry], lands=lands)
        _, (wout_g,) = _split_wait("weights_pair_wait_out", fl[:1], [_pair_weight_copies], ())

        def mlp_weights(after):
            _, (wup_g, wdn_g) = _split_wait("weights_pair_wait_mlp", fl[1:], [_pair_weight_copies] * 2, after)
            return wup_g, wdn_g.reshape(DFF, D)

        return wout_g.reshape(D, D), mlp_weights, carry

    waiting, pending = {}, {}

    def send_grads(carry):
        names = list(waiting)
        slabs = [waiting.pop(n) for n in names]
        lands = [jax.ShapeDtypeStruct((NDIRECT, s.shape[1] // 2, s.shape[2]), bf16) for s in slabs]
        fl, (carry,) = _split_start("grad_start_" + names[-1], slabs, lands, [_direct_grad_copies] * len(names), [carry])
        for n, f in zip(names, fl):
            pending[n] = [f]
        return carry

    def grad_ready(name, g, carry):
        waiting[name] = g if g.ndim == 3 else g.reshape(NCHIP, g.shape[0] // NCHIP, g.shape[1])
        return send_grads(carry) if name == "w_out" else carry

    grad_x, accs = _local_step(x, tabs, mod, loss_target, win_g, later_weights, grad_ready,
                               g_attn_pre, g_attn_post, sink_a, g_mix_a, g_mix_b, g_mlp_pre, g_mlp_post)

    grads, out = {}, {}

    def update(n, after=()):
        tr = (lambda a: a.T) if n == "w_in" else (lambda a: a)
        res = _adamw(tr(given[n][0]), tr(grads[n]), tr(moms[n][0][0]), tr(moms[n][1][0]), "adamw_" + n, after,
                     landed=n != "w_ada")
        out[n] = tuple(tr(a)[None] for a in res)
        return res[3]

    def reduce(names, after):
        fl = sum((pending[n] for n in names), [])
        halves, landed = _split_wait("grad_wait_" + names[0], fl, [_direct_grad_copies] * len(names), after)
        fulls = [_chip_add(h, q, pos_arr, "grad_chip_sum_" + n) for h, q, n in zip(halves, landed, names)]
        tokens = [jnp.full((SUBLANES, LANES), float(i), f32) for i in range(len(names))]
        return _split_start("grad_gather_start_" + names[0], tokens, None, [_pair_gather_copies] * len(names), [],
                            lands=fulls)[0]

    def gathered_update(n, flight, after):
        _, (grads[n],) = _split_wait("grad_gather_wait_" + n, [flight], [_pair_gather_copies], after)
        return update(n)

    slab = waiting.pop("w_in")
    (fl_small, fl_in), (cond_all,) = _split_start(
        "small_start", [_small_pack(accs), slab],
        [jax.ShapeDtypeStruct((NDEV, PAY_ROWS, PAYW), f32), jax.ShapeDtypeStruct((NDIRECT, slab.shape[1] // 2, slab.shape[2]), bf16)],
        [_small_copies, _direct_grad_copies], [cond_all])
    pending["w_in"] = [fl_in]
    fl_down, fl_up, fl_out = reduce(("w_down", "w_up", "w_out"), (cond_all,))
    (pay,), (landed,) = _split_wait("small_wait", [fl_small], [_small_copies], (fl_out[0],))
    grads["w_ada"], gb, small = _small_sum(pay, landed, cond_all)
    last = update("w_ada")
    last = gathered_update("w_down", fl_down, (last,))
    last = gathered_update("w_up", fl_up, (last,))
    (fl_in,) = reduce(("w_in",), (last,))
    last = gathered_update("w_out", fl_out, (fl_in[0],))
    gathered_update("w_in", fl_in, (last,))
    loss, res = _adamw_small(small, gb, [(given[n], moms[n][0], moms[n][1]) for n, _, _ in SMALL])
    for n, _, _ in SMALL:
        out[n] = tuple(res[n])
    return (loss.reshape(()), grad_x, *[out[n][0] for n in order], *[out[n][1] for n in order],
            *[out[n][2] for n in order], *[out[n][3] for n in order])
```

```python
import numpy as np
import jax
import jax.numpy as jnp
from jax import lax
from jax.experimental import pallas as pl
from jax.experimental.pallas import tpu as pltpu

f32 = jnp.float32
bf16 = jnp.bfloat16
MESH = pl.DeviceIdType.MESH

D = 1024
SEQ = 2048
BL = 2
HD = 64
AQ = 512
AKV = 128
BW = 512
INW = 2304
DFF = 4096
NMOD = 6
ROT = 16
THETA = 500000.0
EPS = 1e-6
NEG = -1e30
BLK = 128
TM = 512
NJ = SEQ // TM
LANES = 128
SUBLANES = 8
NHEAD = AQ // HD
QSCALE = HD ** -0.5
NCHIP = 4
NDEV = 8
VMEM_LIMIT = 56 << 20

LR, B1, B2, AEPS, WD, STEP = 0.001, 0.9, 0.999, 1e-08, 0.01, 10

OFF_G_ATTN_PRE, OFF_G_ATTN_POST, OFF_G_MIX_A, OFF_G_MIX_B = 0, 1024, 2048, 2560
OFF_G_MLP_PRE, OFF_G_MLP_POST, OFF_SINK, OFF_LOSS = 3072, 4096, 5120, 5248
PAYW = NMOD * D


def _cp(sem=None):
    return pltpu.CompilerParams(dimension_semantics=sem, vmem_limit_bytes=VMEM_LIMIT)


def _dot(a, b):
    return jnp.dot(a, b, preferred_element_type=f32)


def _dot_nt(a, b):
    return lax.dot_general(a, b, (((1,), (1,)), ((), ())), preferred_element_type=f32)


def _dot_tn(a, b):
    return lax.dot_general(a, b, (((0,), (0,)), ((), ())), preferred_element_type=f32)


def _rms(x):
    r = lax.rsqrt(jnp.mean(x * x, axis=-1, keepdims=True) + EPS)
    return x * r, r


def _rms_bwd(dy, y, r):
    return r * (dy - y * jnp.mean(dy * y, axis=-1, keepdims=True))


def _colsum(v):
    return jnp.sum(v, axis=0, keepdims=True)


def _rope(p, c, s1, s2):
    outs = []
    for c0 in range(0, p.shape[1], LANES):
        pc = p[:, c0:c0 + LANES]
        outs.append(pc * c + pltpu.roll(pc, LANES - ROT // 2, 1) * s1 + pltpu.roll(pc, ROT // 2, 1) * s2)
    return outs[0] if len(outs) == 1 else jnp.concatenate(outs, axis=1)


def _rope_t(g, c, s1, s2):
    outs = []
    for c0 in range(0, g.shape[1], LANES):
        gc = g[:, c0:c0 + LANES]
        outs.append(gc * c + pltpu.roll(gc * s1, ROT // 2, 1) + pltpu.roll(gc * s2, LANES - ROT // 2, 1))
    return outs[0] if len(outs) == 1 else jnp.concatenate(outs, axis=1)


def _perm_store(val, scr, out_ref, d):
    nc = val.shape[1] // LANES
    for c in range(nc):
        scr[c] = val[:, LANES * c:LANES * (c + 1)]
    for c in range(nc):
        for r in range(d):
            out_ref[r, :, LANES * c:LANES * (c + 1)] = scr[c, pl.ds(r, TM // d, stride=d), :].astype(out_ref.dtype)


def _perm_load(in_ref, scr, d):
    nc = in_ref.shape[-1] // LANES
    for c in range(nc):
        for r in range(d):
            scr[c, pl.ds(r, TM // d, stride=d), :] = in_ref[r, :, LANES * c:LANES * (c + 1)].astype(f32)
    return jnp.concatenate([scr[c] for c in range(nc)], axis=1)


def _per_query_head(kv):
    r = pltpu.roll(kv, HD, 1)
    lo = lax.broadcasted_iota(jnp.int32, kv.shape, 1) < HD
    return jnp.concatenate([jnp.where(lo, kv, r), jnp.where(lo, r, kv)], axis=1)


def _per_kv_head(g):
    g0, g1 = g[:, :LANES] + g[:, LANES:2 * LANES], g[:, 2 * LANES:3 * LANES] + g[:, 3 * LANES:]
    lo = lax.broadcasted_iota(jnp.int32, g0.shape, 1) < HD
    return jnp.where(lo, g0 + pltpu.roll(g0, HD, 1), g1 + pltpu.roll(g1, HD, 1))


def _tok(w):
    return pl.BlockSpec((None, TM, w), lambda b, j: (b, j, 0))


def _perm_spec(d, w):
    return pl.BlockSpec((None, d, TM // d, w), lambda b, j: (b, 0, j, 0))


def _full(shape):
    n = len(shape)
    return pl.BlockSpec(shape, lambda b, j: (0,) * n)


MOD_SPEC = pl.BlockSpec((None, NMOD, D), lambda b, j: (b, 0, 0))
ACCB_SPEC = pl.BlockSpec((None, SUBLANES, D), lambda b, j: (b, 0, 0))
ACCG_SPEC = pl.BlockSpec((SUBLANES, D), lambda b, j: (0, 0))
ACC_SHAPES = [jax.ShapeDtypeStruct((BL, SUBLANES, D), f32), jax.ShapeDtypeStruct((SUBLANES, D), f32)]


def _acc_init(accb_ref, accg_ref):
    b, j = pl.program_id(0), pl.program_id(1)

    @pl.when(j == 0)
    def _():
        accb_ref[...] = jnp.zeros_like(accb_ref)

    @pl.when((b == 0) & (j == 0))
    def _():
        accg_ref[...] = jnp.zeros_like(accg_ref)


def _rope_tables(pos_col, inv_lane):
    def body(p_ref, inv_ref, c_ref, s1_ref, s2_ref):
        ang = p_ref[...].astype(f32) * inv_ref[...]
        j = lax.broadcasted_iota(jnp.int32, (TM, LANES), 1) % HD
        cs, sn = jnp.cos(ang), jnp.sin(ang)
        c_ref[...] = jnp.where(j < ROT, cs, 1.0)
        s1_ref[...] = jnp.where(j < ROT // 2, -sn, 0.0)
        s2_ref[...] = jnp.where((j >= ROT // 2) & (j < ROT), sn, 0.0)

    n = BL * SEQ // TM
    return pl.pallas_call(
        body, name="rope_tables", grid=(n,),
        in_specs=[pl.BlockSpec((TM, 1), lambda i: (i, 0)), pl.BlockSpec((1, LANES), lambda i: (0, 0))],
        out_specs=[pl.BlockSpec((TM, LANES), lambda i: (i, 0))] * 3,
        out_shape=[jax.ShapeDtypeStruct((BL * SEQ, LANES), f32)] * 3,
    )(pos_col, inv_lane)


def _attn_in(x, mod, g_pre, w_in, tc, ts1, ts2):
    def body(x_ref, mod_ref, g_ref, wg_ref, c_ref, s1_ref, s2_ref,
             h_ref, qa_ref, ka_ref, va_ref, q1_ref, k1_ref, v1_ref, q4_ref, k4_ref, v4_ref, q16_ref, k16_ref, v16_ref,
             w_ref, scr):
        @pl.when((pl.program_id(0) == 0) & (pl.program_id(1) == 0))
        def _():
            w_ref[...] = jnp.concatenate([wg_ref[s] for s in range(NCHIP)], axis=1)

        xn, _ = _rms(x_ref[...])
        h = (xn * g_ref[...]) * (1.0 + mod_ref[1:2, :]) + mod_ref[0:1, :]
        hb = h.astype(bf16)
        h_ref[...] = hb
        proj = _dot(hb, w_ref[...])
        c, s1, s2 = c_ref[...], s1_ref[...], s2_ref[...]
        o1, o2, o3, o4, o5 = AQ, AQ + AKV, AQ + 2 * AKV, AQ + 2 * AKV + BW, AQ + 2 * AKV + 2 * BW
        qa_ref[...] = (_rope(proj[:, :o1], c, s1, s2) * QSCALE).astype(bf16)
        ka_ref[...] = _per_query_head(_rope(proj[:, o1:o2], c, s1, s2)).astype(bf16)
        va_ref[...] = _per_query_head(proj[:, o2:o3]).astype(bf16)
        qb = _rope(proj[:, o3:o4], c, s1, s2) * QSCALE
        kb = _rope(proj[:, o4:o5], c, s1, s2)
        vb = proj[:, o5:]
        for val, r1, r4, r16 in ((qb, q1_ref, q4_ref, q16_ref), (kb, k1_ref, k4_ref, k16_ref), (vb, v1_ref, v4_ref, v16_ref)):
            r1[...] = val.astype(bf16)
            _perm_store(val, scr, r4, 4)
            _perm_store(val, scr, r16, 16)

    nat = lambda w: jax.ShapeDtypeStruct((BL, SEQ, w), bf16)
    p4 = jax.ShapeDtypeStruct((BL, 4, SEQ // 4, BW), bf16)
    p16 = jax.ShapeDtypeStruct((BL, 16, SEQ // 16, BW), bf16)
    return pl.pallas_call(
        body, name="attn_in", grid=(BL, NJ),
        in_specs=[_tok(D), MOD_SPEC, _full((1, D)), _full((NCHIP, D, INW // NCHIP)), _tok(LANES), _tok(LANES), _tok(LANES)],
        out_specs=([_tok(D), _tok(AQ), _tok(2 * AKV), _tok(2 * AKV)] + [_tok(BW)] * 3 + [_perm_spec(4, BW)] * 3 + [_perm_spec(16, BW)] * 3
                   + [_full((D, INW))]),
        out_shape=[nat(D), nat(AQ), nat(2 * AKV), nat(2 * AKV)] + [nat(BW)] * 3 + [p4] * 3 + [p16] * 3
                  + [jax.ShapeDtypeStruct((D, INW), bf16)],
        scratch_shapes=[pltpu.VMEM((BW // LANES, TM, LANES), f32)],
        compiler_params=_cp(("arbitrary", "arbitrary")),
    )(x, mod, g_pre, w_in, tc, ts1, ts2)


def _kv_cat(cur_ref, prev_ref, p, cache):
    key = (id(cur_ref), p)
    if key not in cache:
        sl = slice(LANES * p, LANES * (p + 1))
        cache[key] = cur_ref[:, sl] if prev_ref is None else jnp.concatenate([prev_ref[:, sl], cur_ref[:, sl]], axis=0)
    return cache[key]


def _lane_half(a, hh):
    lo = lax.broadcasted_iota(jnp.int32, a.shape, 1) < HD
    return jnp.where(lo, a, jnp.zeros_like(a)) if hh == 0 else jnp.where(lo, jnp.zeros_like(a), a)


ATT_UNITS = 4


def _att_units(nb):
    return ATT_UNITS if nb == 1 else min(ATT_UNITS, nb)


def _attn_specs(n, nb, descending):
    u = _att_units(nb)
    if nb == 1:
        return (lambda ww: pl.BlockSpec((u, BLK, ww), lambda a, i: (a, 0, 0))), None, (n // u, 1)
    steps = nb // u
    at = (lambda i: steps - 1 - i) if descending else (lambda i: i)
    cur = lambda ww: pl.BlockSpec((None, u * BLK, ww), lambda a, i: (a, at(i), 0))
    prev = lambda ww: pl.BlockSpec((None, BLK, ww), lambda a, i: (a, jnp.maximum(u * at(i) - 1, 0), 0))
    return cur, prev, (n, steps)


def _attn_fwd(q, k, v, sink, *, max_dist, o_dtype, name):
    n, l, w = q.shape
    wk = k.shape[-1]
    nb = l // BLK
    has_sink = sink is not None

    def body(*refs):
        sink_ref = None
        if has_sink:
            sink_ref, refs = refs[0], refs[1:]
        if nb > 1:
            q_ref, kc_ref, kp_ref, vc_ref, vp_ref, o_ref, lse_ref = refs[:7]
            first = pl.program_id(1) == 0
            for u in range(_att_units(nb)):
                rows, before = pl.ds(BLK * u, BLK), pl.ds(BLK * (u - 1), BLK)
                unit(q_ref.at[rows, :], kc_ref.at[rows, :], kp_ref if u == 0 else kc_ref.at[before, :],
                     vc_ref.at[rows, :], vp_ref if u == 0 else vc_ref.at[before, :], o_ref.at[rows, :], lse_ref.at[rows, :],
                     jnp.logical_not(first) if u == 0 else True, sink_ref, *refs[7:])
        else:
            q_ref, kc_ref, vc_ref, o_ref, lse_ref = refs[:5]
            for u in range(_att_units(nb)):
                unit(q_ref.at[u], kc_ref.at[u], None, vc_ref.at[u], None, o_ref.at[u], lse_ref.at[u], None, sink_ref, *refs[5:])

    def unit(q_ref, kc_ref, kp_ref, vc_ref, vp_ref, o_ref, lse_ref, has_prev, sink_ref, sscr, pscr, dscr):
        qi = lax.broadcasted_iota(jnp.int32, (BLK, BLK), 0)
        kj = lax.broadcasted_iota(jnp.int32, (BLK, BLK), 1)
        tri = kj <= qi
        eye = kj == qi
        cache = {}
        for p in range(w // LANES):
            qpair = q_ref[:, LANES * p:LANES * (p + 1)]
            kcat = _kv_cat(kc_ref, kp_ref, p // share, cache)
            for hh in range(2):
                s = _dot_nt(_lane_half(qpair, hh), kcat)
                if nb > 1:
                    sp = s[:, :BLK] if has_prev is True else jnp.where(has_prev, s[:, :BLK], NEG)
                    sscr[2 * p + hh] = jnp.where(tri, s[:, BLK:], sp)
                    if diag:
                        dscr[2 * p + hh] = jnp.where(eye, sp, NEG)
                else:
                    sscr[2 * p + hh] = jnp.where(tri, s, NEG)
        lane = lax.broadcasted_iota(jnp.int32, (BLK, LANES), 1)
        lse_all = jnp.zeros((BLK, LANES), f32)
        for p in range(w // LANES):
            for hh in range(2):
                h = 2 * p + hh
                comb = sscr[h]
                if diag:
                    dtile = dscr[h]
                    m = jnp.max(jnp.maximum(comb, dtile), axis=-1, keepdims=True)
                else:
                    m = jnp.max(comb, axis=-1, keepdims=True)
                if has_sink:
                    sk = sink_ref[0, h]
                    m = jnp.maximum(m, sk)
                e = jnp.exp(comb - m)
                if diag:
                    ed = jnp.exp(dtile - m)
                    den = jnp.sum(e + ed, axis=-1, keepdims=True)
                else:
                    den = jnp.sum(e, axis=-1, keepdims=True)
                if has_sink:
                    den = den + jnp.exp(sk - m)
                inv = 1.0 / den
                if nb > 1:
                    pscr[h, :, :BLK] = (jnp.where(tri, ed if diag else 0.0, e) * inv).astype(bf16)
                    pscr[h, :, BLK:] = (jnp.where(tri, e, 0.0) * inv).astype(bf16)
                else:
                    pscr[h] = (e * inv).astype(bf16)
                lse_all = jnp.where(lane == h, jnp.broadcast_to(m + jnp.log(den), (BLK, LANES)), lse_all)
        lse_ref[...] = lse_all
        for p in range(w // LANES):
            vcat = _kv_cat(vc_ref, vp_ref, p // share, cache)
            o_ref[:, LANES * p:LANES * (p + 1)] = (_dot(pscr[2 * p], _lane_half(vcat, 0))
                                                   + _dot(pscr[2 * p + 1], _lane_half(vcat, 1))).astype(o_ref.dtype)

    assert max_dist in (BLK - 1, BLK) and w % wk == 0
    share = w // wk
    diag = nb > 1 and max_dist == BLK
    cur, prev, grid = _attn_specs(n, nb, False)
    in_specs = [cur(w), cur(wk)] + ([prev(wk)] if nb > 1 else []) + [cur(wk)] + ([prev(wk)] if nb > 1 else [])
    args = [q, k] + ([k] if nb > 1 else []) + [v] + ([v] if nb > 1 else [])
    if has_sink:
        in_specs = [pl.BlockSpec(memory_space=pltpu.SMEM)] + in_specs
        args = [sink] + args
    return pl.pallas_call(
        body, name=name, grid=grid, in_specs=in_specs,
        out_specs=[cur(w), cur(LANES)],
        out_shape=[jax.ShapeDtypeStruct((n, l, w), o_dtype), jax.ShapeDtypeStruct((n, l, LANES), f32)],
        scratch_shapes=[pltpu.VMEM((w // HD, BLK, BLK), f32), pltpu.VMEM((w // HD, BLK, 2 * BLK if nb > 1 else BLK), bf16),
                        pltpu.VMEM((w // HD if diag else 1, BLK, BLK), f32)],
        compiler_params=_cp(("arbitrary", "arbitrary")),
    )(*args)


def _attn_bwd(q, k, v, do, delta, lse, sink, *, max_dist, name):
    n, l, w = q.shape
    wk = k.shape[-1]
    nb = l // BLK
    has_sink = sink is not None

    def body(*refs):
        sink_ref = dsink_ref = ck = cv = None
        if has_sink:
            sink_ref, refs = refs[0], refs[1:]
        nin = 8 if nb > 1 else 6
        ins, rest = refs[:nin], refs[nin:]
        if has_sink:
            dq_ref, dk_ref, dv_ref, dsink_ref = rest[:4]
            rest = rest[4:]
        else:
            dq_ref, dk_ref, dv_ref = rest[:3]
            rest = rest[3:]
        step = pl.program_id(1)
        if has_sink:
            @pl.when((pl.program_id(0) == 0) & (step == 0))
            def _():
                dsink_ref[...] = jnp.zeros_like(dsink_ref)

        if nb > 1:
            q_ref, kc_ref, kp_ref, vc_ref, vp_ref, do_ref, delta_ref, lse_ref = ins
            ck, cv = rest[:2]

            @pl.when(step == 0)
            def _():
                ck[...] = jnp.zeros_like(ck)
                cv[...] = jnp.zeros_like(cv)

            last = step == nb // _att_units(nb) - 1
            for u in reversed(range(_att_units(nb))):
                rows, before = pl.ds(BLK * u, BLK), pl.ds(BLK * (u - 1), BLK)
                unit(q_ref.at[rows, :], kc_ref.at[rows, :], kp_ref if u == 0 else kc_ref.at[before, :],
                     vc_ref.at[rows, :], vp_ref if u == 0 else vc_ref.at[before, :], do_ref.at[rows, :],
                     delta_ref.at[rows, :], lse_ref.at[rows, :], dq_ref.at[rows, :], dk_ref.at[rows, :], dv_ref.at[rows, :],
                     jnp.logical_not(last) if u == 0 else True, sink_ref, dsink_ref, ck, cv, *rest[2:])
        else:
            q_ref, kc_ref, vc_ref, do_ref, delta_ref, lse_ref = ins
            for u in range(_att_units(nb)):
                unit(q_ref.at[u], kc_ref.at[u], None, vc_ref.at[u], None, do_ref.at[u], delta_ref.at[u], lse_ref.at[u],
                     dq_ref.at[u], dk_ref.at[u], dv_ref.at[u], None, sink_ref, dsink_ref, None, None, *rest)

    def unit(q_ref, kc_ref, kp_ref, vc_ref, vp_ref, do_ref, delta_ref, lse_ref, dq_ref, dk_ref, dv_ref, has_prev,
             sink_ref, dsink_ref, ck, cv, sscr, dpscr, pscr, dsscr, dscr=None, ddscr=None):
        lane = lax.broadcasted_iota(jnp.int32, (BLK, LANES), 1)
        qi = lax.broadcasted_iota(jnp.int32, (BLK, BLK), 0)
        kj = lax.broadcasted_iota(jnp.int32, (BLK, BLK), 1)
        tri = kj <= qi
        eye = kj == qi
        cache = {}
        kp, vp = kp_ref, vp_ref
        for p in range(w // LANES):
            sl = slice(LANES * p, LANES * (p + 1))
            qpair, dopair = q_ref[:, sl], do_ref[:, sl]
            kcat, vcat = _kv_cat(kc_ref, kp, p // share, cache), _kv_cat(vc_ref, vp, p // share, cache)
            for hh in range(2):
                h = 2 * p + hh
                s = _dot_nt(_lane_half(qpair, hh), kcat)
                dp = _dot_nt(_lane_half(dopair, hh), vcat)
                if nb > 1:
                    sp = s[:, :BLK] if has_prev is True else jnp.where(has_prev, s[:, :BLK], NEG)
                    sscr[h] = jnp.where(tri, s[:, BLK:], sp)
                    dpscr[h] = jnp.where(tri, dp[:, BLK:], dp[:, :BLK])
                    if diag:
                        dscr[h] = jnp.where(eye, sp, NEG)
                        ddscr[h] = dp[:, :BLK]
                else:
                    sscr[h] = jnp.where(tri, s, NEG)
                    dpscr[h] = dp
        for p in range(w // LANES):
            for hh in range(2):
                h = 2 * p + hh
                lse_b = jnp.broadcast_to(lse_ref[:, h:h + 1], (BLK, BLK))
                delta = jnp.broadcast_to(delta_ref[:, h:h + 1], (BLK, BLK))
                pr = jnp.exp(sscr[h] - lse_b)
                ds = pr * (dpscr[h] - delta)
                if nb > 1:
                    if diag:
                        prd = jnp.exp(dscr[h] - lse_b)
                        dsd = prd * (ddscr[h] - delta)
                    else:
                        prd = dsd = 0.0
                    pscr[h, :, :BLK] = jnp.where(tri, prd, pr).astype(bf16)
                    pscr[h, :, BLK:] = jnp.where(tri, pr, 0.0).astype(bf16)
                    dsscr[h, :, :BLK] = jnp.where(tri, dsd, ds).astype(bf16)
                    dsscr[h, :, BLK:] = jnp.where(tri, ds, 0.0).astype(bf16)
                else:
                    pscr[h] = pr.astype(bf16)
                    dsscr[h] = ds.astype(bf16)
                if has_sink:
                    dsk = -jnp.sum(jnp.where(lane == 0, jnp.exp(sink_ref[0, h] - lse_b) * delta, 0.0), keepdims=True)
                    dsink_ref[h:h + 1, :] += jnp.broadcast_to(dsk, (1, LANES))
        for p in range(w // LANES):
            sl = slice(LANES * p, LANES * (p + 1))
            qpair, dopair = q_ref[:, sl], do_ref[:, sl]
            kcat = _kv_cat(kc_ref, kp, p // share, cache)
            dq_ref[:, sl] = _dot(dsscr[2 * p], _lane_half(kcat, 0)) + _dot(dsscr[2 * p + 1], _lane_half(kcat, 1))
            dk_pair = _dot_tn(dsscr[2 * p], _lane_half(qpair, 0)) + _dot_tn(dsscr[2 * p + 1], _lane_half(qpair, 1))
            dv_pair = _dot_tn(pscr[2 * p], _lane_half(dopair, 0)) + _dot_tn(pscr[2 * p + 1], _lane_half(dopair, 1))
            if nb > 1:
                dk_ref[:, sl] = dk_pair[BLK:] + ck[:, sl]
                dv_ref[:, sl] = dv_pair[BLK:] + cv[:, sl]
                ck[:, sl] = dk_pair[:BLK]
                cv[:, sl] = dv_pair[:BLK]
            else:
                dk_ref[:, sl] = dk_pair
                dv_ref[:, sl] = dv_pair

    assert max_dist in (BLK - 1, BLK) and w % wk == 0
    share = w // wk
    diag = nb > 1 and max_dist == BLK
    cur, prev, grid = _attn_specs(n, nb, True)
    in_specs = ([cur(w), cur(wk)] + ([prev(wk)] if nb > 1 else []) + [cur(wk)] + ([prev(wk)] if nb > 1 else [])
                + [cur(w), cur(LANES), cur(LANES)])
    args = [q, k] + ([k] if nb > 1 else []) + [v] + ([v] if nb > 1 else []) + [do, delta, lse]
    out_specs = [cur(w)] * 3
    out_shape = [jax.ShapeDtypeStruct((n, l, w), f32)] * 3
    if has_sink:
        in_specs = [pl.BlockSpec(memory_space=pltpu.SMEM)] + in_specs
        args = [sink] + args
        out_specs.append(pl.BlockSpec((NHEAD, LANES), lambda a, i: (0, 0)))
        out_shape.append(jax.ShapeDtypeStruct((NHEAD, LANES), f32))
    nh = w // HD
    scratch = [pltpu.VMEM((BLK, w), f32), pltpu.VMEM((BLK, w), f32)] if nb > 1 else []
    scratch += [pltpu.VMEM((nh, BLK, BLK), f32)] * 2 + [pltpu.VMEM((nh, BLK, 2 * BLK if nb > 1 else BLK), bf16)] * 2
    if diag:
        scratch += [pltpu.VMEM((nh, BLK, BLK), f32)] * 2
    return pl.pallas_call(
        body, name=name, grid=grid, in_specs=in_specs, out_specs=out_specs, out_shape=out_shape,
        scratch_shapes=scratch, compiler_params=_cp(("arbitrary", "arbitrary")),
    )(*args)


def _split2(x):
    hi = x.astype(bf16)
    return hi, (x - hi.astype(f32)).astype(bf16)


def _heads_to_lanes(xc, e):
    return sum(_dot(t, e) for t in _split2(xc))


def _lanes_to_heads(x, g):
    return sum(_dot(t, g) for t in _split2(x))


HEAD_EXPAND = (np.arange(LANES)[:, None] == np.arange(BW)[None, :] // HD).astype(np.float32)
HEAD_SUM = HEAD_EXPAND.T.copy()


def _branch_weights(l1_ref, l4_ref, l16_ref, scr):
    l4v = _perm_load(l4_ref, scr, 4)
    l16v = _perm_load(l16_ref, scr, 16)
    l1v = l1_ref[...]
    m = jnp.maximum(jnp.maximum(l1v, l4v), l16v)
    e1, e4, e16 = jnp.exp(l1v - m), jnp.exp(l4v - m), jnp.exp(l16v - m)
    z = e1 + e4 + e16
    return e1 / z, e4 / z, e16 / z


def _mix_out(oa, o1, l1, o4, l4, o16, l16, g_mix_a, g_mix_b, w_out, x, mod, g_post):
    def body(oa_ref, o1_ref, l1_ref, o4_ref, l4_ref, o16_ref, l16_ref, ga_ref, gb_ref, w_ref, x_ref, mod_ref, gp_ref, e_ref,
             x1_ref, y_ref, mixed_ref, ob_ref, scr):
        w1, w4, w16 = _branch_weights(l1_ref, l4_ref, l16_ref, scr)
        e = e_ref[...]
        x1w, x4w = _heads_to_lanes(w1, e), _heads_to_lanes(w4, e)
        ob = (x1w * o1_ref[...].astype(f32) + x4w * _perm_load(o4_ref, scr, 4)
              + (1.0 - x1w - x4w) * _perm_load(o16_ref, scr, 16))
        ob_ref[...] = ob
        oan, _ = _rms(oa_ref[...])
        obn, _ = _rms(ob)
        mixed = jnp.concatenate([oan * ga_ref[...], obn * gb_ref[...]], axis=1).astype(bf16)
        mixed_ref[...] = mixed
        y = _dot(mixed, w_ref[...])
        y_ref[...] = y
        yn, _ = _rms(y)
        x1_ref[...] = x_ref[...] + mod_ref[2:3, :] * (yn * gp_ref[...])

    nat = lambda w, dt: jax.ShapeDtypeStruct((BL, SEQ, w), dt)
    return pl.pallas_call(
        body, name="mix_out", grid=(BL, NJ),
        in_specs=[_tok(AQ), _tok(BW), _tok(LANES), _perm_spec(4, BW), _perm_spec(4, LANES), _perm_spec(16, BW),
                  _perm_spec(16, LANES), _full((1, AQ)), _full((1, BW)), _full((D, D)), _tok(D), MOD_SPEC, _full((1, D)),
                  _full((LANES, BW))],
        out_specs=[_tok(D), _tok(D), _tok(D), _tok(BW)],
        out_shape=[nat(D, f32), nat(D, f32), nat(D, bf16), nat(BW, f32)],
        scratch_shapes=[pltpu.VMEM((BW // LANES, TM, LANES), f32)],
        compiler_params=_cp(("arbitrary", "arbitrary")),
    )(oa, o1, l1, o4, l4, o16, l16, g_mix_a, g_mix_b, w_out, x, mod, g_post, jnp.asarray(HEAD_EXPAND, bf16))


def _mlp_up(x1, mod, g_pre, w_up):
    def body(x_ref, mod_ref, g_ref, w_ref, h_ref, u_ref, a_ref):
        xn, _ = _rms(x_ref[...])
        h = (xn * g_ref[...]) * (1.0 + mod_ref[4:5, :]) + mod_ref[3:4, :]
        hb = h.astype(bf16)
        h_ref[...] = hb
        for s in range(NCHIP):
            u = _dot(hb, w_ref[s])
            u_ref[:, D * s:D * (s + 1)] = u.astype(bf16)
            a_ref[:, D * s:D * (s + 1)] = jnp.square(jnp.maximum(u, 0.0)).astype(bf16)

    nat = lambda w: jax.ShapeDtypeStruct((BL, SEQ, w), bf16)
    return pl.pallas_call(
        body, name="mlp_up", grid=(BL, NJ),
        in_specs=[_tok(D), MOD_SPEC, _full((1, D)), _full((NCHIP, D, D))],
        out_specs=[_tok(D), _tok(DFF), _tok(DFF)], out_shape=[nat(D), nat(DFF), nat(DFF)],
        compiler_params=_cp(("arbitrary", "arbitrary")),
    )(x1, mod, g_pre, w_up)


def _mlp_down(a, w_down, x1, target, mod, g_post):
    def body(a_ref, w_ref, x_ref, t_ref, mod_ref, g_ref, gx_ref, dy_ref, accb_ref, accg_ref):
        _acc_init(accb_ref, accg_ref)
        y2 = _dot(a_ref[...], w_ref[...])
        yn, r = _rms(y2)
        g = g_ref[...]
        gt = mod_ref[5:6, :]
        n2 = yn * g
        err = x_ref[...] + gt * n2 - t_ref[...]
        gout = err * (1.0 / D)
        gx_ref[...] = gout
        dn2 = gout * gt
        dy_ref[...] = _rms_bwd(dn2 * g, yn, r).astype(bf16)
        accb_ref[0:1, :] += _colsum(gout * n2)
        accg_ref[0:1, :] += _colsum(dn2 * yn)
        accg_ref[1:2, :] += jnp.broadcast_to(jnp.sum(err * err, keepdims=True), (1, D))

    return pl.pallas_call(
        body, name="mlp_down", grid=(BL, NJ),
        in_specs=[_tok(DFF), _full((DFF, D)), _tok(D), _tok(D), MOD_SPEC, _full((1, D))],
        out_specs=[_tok(D), _tok(D), ACCB_SPEC, ACCG_SPEC],
        out_shape=[jax.ShapeDtypeStruct((BL, SEQ, D), f32), jax.ShapeDtypeStruct((BL, SEQ, D), bf16)] + ACC_SHAPES,
        compiler_params=_cp(("arbitrary", "arbitrary")),
    )(a, w_down, x1, target, mod, g_post)


def _mlp_bwd(dy2, u, w_down, w_up, x1, gx, mod, g_pre):
    def body(dy_ref, u_ref, wd_hbm, wu_hbm, x_ref, gx_ref, mod_ref, g_ref, du_ref, gx1_ref, accb_ref, accg_ref, wd, wu, sem):
        _acc_init(accb_ref, accg_ref)
        first = (pl.program_id(0) == 0) & (pl.program_id(1) == 0)
        c1 = pltpu.make_async_copy(wd_hbm, wd, sem.at[0])
        c2 = pltpu.make_async_copy(wu_hbm, wu, sem.at[1])

        @pl.when(first)
        def _():
            c1.start()
            c2.start()
            c1.wait()

        dy = dy_ref[...]
        for s in range(NCHIP):
            sl = slice(D * s, D * (s + 1))
            da = _dot_nt(dy, wd[sl, :])
            du_ref[:, sl] = (da * (2.0 * jnp.maximum(u_ref[:, sl].astype(f32), 0.0))).astype(bf16)

        @pl.when(first)
        def _():
            c2.wait()

        dh = jnp.zeros((TM, D), f32)
        for s in range(NCHIP):
            dh = dh + _dot_nt(du_ref[:, D * s:D * (s + 1)], wu[s])
        xn, r = _rms(x_ref[...])
        g = g_ref[...]
        n = xn * g
        dn = dh * (1.0 + mod_ref[4:5, :])
        gx1_ref[...] = gx_ref[...] + _rms_bwd(dn * g, xn, r)
        accb_ref[0:1, :] += _colsum(dh * n)
        accb_ref[1:2, :] += _colsum(dh)
        accg_ref[0:1, :] += _colsum(dn * xn)

    anyspec = pl.BlockSpec(memory_space=pl.ANY)
    return pl.pallas_call(
        body, name="mlp_bwd", grid=(BL, NJ),
        in_specs=[_tok(D), _tok(DFF), anyspec, anyspec, _tok(D), _tok(D), MOD_SPEC, _full((1, D))],
        out_specs=[_tok(DFF), _tok(D), ACCB_SPEC, ACCG_SPEC],
        out_shape=[jax.ShapeDtypeStruct((BL, SEQ, DFF), bf16), jax.ShapeDtypeStruct((BL, SEQ, D), f32)] + ACC_SHAPES,
        scratch_shapes=[pltpu.VMEM((DFF, D), bf16), pltpu.VMEM((NCHIP, D, D), bf16), pltpu.SemaphoreType.DMA((2,))],
        compiler_params=_cp(("arbitrary", "arbitrary")),
    )(dy2, u, w_down, w_up, x1, gx, mod, g_pre)


def _matmul_tn(a, b, *, tn, col_blocked, name, out_dtype=f32):
    t, m = a.shape
    n = b.shape[1]
    tmm = min(m, 1024)
    tk = 2048 if tn <= 1024 else 1024
    nk = t // tk

    def body(a_ref, b_ref, o_ref, acc):
        k = pl.program_id(2)

        @pl.when(k == 0)
        def _():
            acc[...] = jnp.zeros_like(acc)

        acc[...] += _dot_tn(a_ref[...], b_ref[...])

        @pl.when(k == nk - 1)
        def _():
            o_ref[...] = acc[...].astype(out_dtype)

    if col_blocked:
        out_spec = pl.BlockSpec((None, tmm, tn), lambda i, j, k: (j, i, 0))
        out_shape = jax.ShapeDtypeStruct((n // tn, m, tn), out_dtype)
    else:
        out_spec = pl.BlockSpec((tmm, tn), lambda i, j, k: (i, j))
        out_shape = jax.ShapeDtypeStruct((m, n), out_dtype)
    return pl.pallas_call(
        body, name=name, grid=(m // tmm, n // tn, nk),
        in_specs=[pl.BlockSpec((tk, tmm), lambda i, j, k: (k, i)), pl.BlockSpec((tk, tn), lambda i, j, k: (k, j))],
        out_specs=out_spec, out_shape=out_shape, scratch_shapes=[pltpu.VMEM((tmm, tn), f32)],
        compiler_params=_cp(("arbitrary", "arbitrary", "arbitrary")),
    )(a, b)


def _grad_w_in(h, dproj):
    t = h.shape[0]
    tk = 1024
    nk = t // tk
    sw = INW // NCHIP

    def body(a_ref, b_ref, o_ref, acc):
        k = pl.program_id(0)

        @pl.when(k == 0)
        def _():
            acc[...] = jnp.zeros_like(acc)

        acc[...] += _dot_tn(a_ref[...], b_ref[...])

        @pl.when(k == nk - 1)
        def _():
            for s in range(NCHIP):
                o_ref[s] = acc[:, sw * s:sw * (s + 1)].astype(bf16)

    return pl.pallas_call(
        body, name="grad_w_in", grid=(nk,),
        in_specs=[pl.BlockSpec((tk, D), lambda k: (k, 0)), pl.BlockSpec((tk, INW), lambda k: (k, 0))],
        out_specs=pl.BlockSpec((NCHIP, D, sw), lambda k: (0, 0, 0)), out_shape=jax.ShapeDtypeStruct((NCHIP, D, sw), bf16),
        scratch_shapes=[pltpu.VMEM((D, INW), f32)], compiler_params=_cp(("arbitrary",)),
    )(h, dproj)


def _attn_out_bwd(gx1, y, mod, g_post, w_out, oa, ob, g_mix_a, g_mix_b, l1, l4, l16):
    def body(gx_ref, y_ref, mod_ref, gp_ref, w_ref, oa_ref, ob_ref, ga_ref, gb_ref, l1_ref, l4_ref, l16_ref, e_ref, g_ref,
             dy_ref, doa_ref, do1_ref, do4_ref, do16_ref, da_ref, d1_ref, d4_ref, d16_ref, accb_ref, accg_ref, scr):
        _acc_init(accb_ref, accg_ref)
        w1, w4, w16 = _branch_weights(l1_ref, l4_ref, l16_ref, scr)
        e, hs = e_ref[...], g_ref[...]
        gx1v = gx_ref[...]
        yn, ry = _rms(y_ref[...])
        gp = gp_ref[...]
        gt = mod_ref[2:3, :]
        dn1 = gx1v * gt
        dy = _rms_bwd(dn1 * gp, yn, ry).astype(bf16)
        dy_ref[...] = dy
        dmixed = _dot_nt(dy, w_ref[...])
        dma, dmb = dmixed[:, :AQ], dmixed[:, AQ:]
        oa, ob = oa_ref[...], ob_ref[...]
        oan, ra = _rms(oa)
        obn, rb = _rms(ob)
        doa = _rms_bwd(dma * ga_ref[...], oan, ra)
        doa_ref[...] = doa.astype(bf16)
        da_ref[...] = _lanes_to_heads(doa * oa, hs)
        dob = _rms_bwd(dmb * gb_ref[...], obn, rb)
        dd = _lanes_to_heads(dob * ob, hs)
        x1w, x4w = _heads_to_lanes(w1, e), _heads_to_lanes(w4, e)
        do1_ref[...] = (x1w * dob).astype(bf16)
        d1_ref[...] = w1 * dd
        _perm_store(x4w * dob, scr, do4_ref, 4)
        _perm_store(w4 * dd, scr, d4_ref, 4)
        _perm_store((1.0 - x1w - x4w) * dob, scr, do16_ref, 16)
        _perm_store(w16 * dd, scr, d16_ref, 16)
        accb_ref[0:1, :] += _colsum(gx1v * (yn * gp))
        accg_ref[0:1, :] += _colsum(dn1 * yn)
        accg_ref[1:2, :] += jnp.concatenate([_colsum(dma * oan), _colsum(dmb * obn)], axis=1)

    nat = lambda w, dt: jax.ShapeDtypeStruct((BL, SEQ, w), dt)
    return pl.pallas_call(
        body, name="attn_out_bwd", grid=(BL, NJ),
        in_specs=[_tok(D), _tok(D), MOD_SPEC, _full((1, D)), _full((D, D)), _tok(AQ), _tok(BW), _full((1, AQ)), _full((1, BW)),
                  _tok(LANES), _perm_spec(4, LANES), _perm_spec(16, LANES), _full((LANES, BW)), _full((BW, LANES))],
        out_specs=[_tok(D), _tok(AQ), _tok(BW), _perm_spec(4, BW), _perm_spec(16, BW),
                   _tok(LANES), _tok(LANES), _perm_spec(4, LANES), _perm_spec(16, LANES), ACCB_SPEC, ACCG_SPEC],
        out_shape=[nat(D, bf16), nat(AQ, bf16), nat(BW, bf16), jax.ShapeDtypeStruct((BL, 4, SEQ // 4, BW), bf16),
                   jax.ShapeDtypeStruct((BL, 16, SEQ // 16, BW), bf16), nat(LANES, f32), nat(LANES, f32),
                   jax.ShapeDtypeStruct((BL, 4, SEQ // 4, LANES), f32), jax.ShapeDtypeStruct((BL, 16, SEQ // 16, LANES), f32)]
                  + ACC_SHAPES,
        scratch_shapes=[pltpu.VMEM((BW // LANES, TM, LANES), f32)],
        compiler_params=_cp(("arbitrary", "arbitrary")),
    )(gx1, y, mod, g_post, w_out, oa, ob, g_mix_a, g_mix_b, l1, l4, l16, jnp.asarray(HEAD_EXPAND, bf16),
      jnp.asarray(HEAD_SUM, bf16))


def _attn_in_bwd(dqa, dka, dva, d1, d4, d16, tc, ts1, ts2, w_in, x, gx1, mod, g_pre):
    def body(dqa_ref, dka_ref, dva_ref, dq1_ref, dk1_ref, dv1_ref, dq4_ref, dk4_ref, dv4_ref, dq16_ref, dk16_ref, dv16_ref,
             c_ref, s1_ref, s2_ref, w_ref, x_ref, gx_ref, mod_ref, g_ref, dproj_ref, dx_ref, accb_ref, accg_ref, scr):
        _acc_init(accb_ref, accg_ref)
        c, s1, s2 = c_ref[...], s1_ref[...], s2_ref[...]
        tot = lambda r1, r4, r16: r1[...] + _perm_load(r4, scr, 4) + _perm_load(r16, scr, 16)
        dqb = tot(dq1_ref, dq4_ref, dq16_ref)
        dkb = tot(dk1_ref, dk4_ref, dk16_ref)
        dvb = tot(dv1_ref, dv4_ref, dv16_ref)
        dproj = jnp.concatenate([
            _rope_t(dqa_ref[...], c, s1, s2) * QSCALE, _rope_t(_per_kv_head(dka_ref[...]), c, s1, s2),
            _per_kv_head(dva_ref[...]),
            _rope_t(dqb, c, s1, s2) * QSCALE, _rope_t(dkb, c, s1, s2), dvb], axis=1).astype(bf16)
        dproj_ref[...] = dproj
        dh = _dot_nt(dproj, w_ref[...])
        xn, r = _rms(x_ref[...])
        g = g_ref[...]
        dn = dh * (1.0 + mod_ref[1:2, :])
        dx_ref[...] = gx_ref[...] + _rms_bwd(dn * g, xn, r)
        accb_ref[0:1, :] += _colsum(dh * (xn * g))
        accb_ref[1:2, :] += _colsum(dh)
        accg_ref[0:1, :] += _colsum(dn * xn)

    return pl.pallas_call(
        body, name="attn_in_bwd", grid=(BL, NJ),
        in_specs=[_tok(AQ), _tok(AQ), _tok(AQ)] + [_tok(BW)] * 3 + [_perm_spec(4, BW)] * 3 + [_perm_spec(16, BW)] * 3
                 + [_tok(LANES)] * 3 + [_full((D, INW)), _tok(D), _tok(D), MOD_SPEC, _full((1, D))],
        out_specs=[_tok(INW), _tok(D), ACCB_SPEC, ACCG_SPEC],
        out_shape=[jax.ShapeDtypeStruct((BL, SEQ, INW), bf16), jax.ShapeDtypeStruct((BL, SEQ, D), f32)] + ACC_SHAPES,
        scratch_shapes=[pltpu.VMEM((BW // LANES, TM, LANES), f32)],
        compiler_params=_cp(("arbitrary", "arbitrary")),
    )(dqa, dka, dva, *d1, *d4, *d16, tc, ts1, ts2, w_in, x, gx1, mod, g_pre)


def _inv_lane():
    inv = np.float32(THETA) ** (-np.arange(0, ROT, 2, dtype=np.float32) / np.float32(ROT))
    lane = np.arange(LANES) % HD
    return jnp.asarray(np.where(lane < ROT, inv[lane % (ROT // 2)], 0.0).astype(np.float32)[None, :])


def _local_step(x, tabs, mod, target, w_in, later_weights, grad_ready, g_attn_pre,
                g_attn_post, sink_a, g_mix_a, g_mix_b, g_mlp_pre, g_mlp_post):
    tc, ts1, ts2 = [t.reshape(BL, SEQ, LANES) for t in tabs]

    (h, qa, ka, va, q1, k1, v1, q4, k4, v4, q16, k16, v16, w_in) = _attn_in(x, mod, g_attn_pre, w_in, tc, ts1, ts2)
    seqs = lambda t: t.reshape(t.shape[0] * t.shape[1], t.shape[2], t.shape[3])
    q4, k4, v4, q16, k16, v16 = [seqs(t) for t in (q4, k4, v4, q16, k16, v16)]
    oa, la = _attn_fwd(qa, ka, va, sink_a, max_dist=BLK - 1, o_dtype=f32, name="attn_a_fwd")
    o1, l1 = _attn_fwd(q1, k1, v1, None, max_dist=BLK, o_dtype=bf16, name="attn_b1_fwd")
    o4, l4 = _attn_fwd(q4, k4, v4, None, max_dist=BLK, o_dtype=bf16, name="attn_b4_fwd")
    out_weight, mlp_weights, mod = later_weights((oa, o1, o4), mod)
    q16, mod = lax.optimization_barrier((q16, mod))
    o16, l16 = _attn_fwd(q16, k16, v16, None, max_dist=BLK, o_dtype=bf16, name="attn_b16_fwd")
    w_out = out_weight((o16,))
    b4 = lambda t: t.reshape(BL, 4, SEQ // 4, t.shape[-1])
    b16 = lambda t: t.reshape(BL, 16, SEQ // 16, t.shape[-1])
    x1, y, mixed, ob = _mix_out(oa, o1, l1, b4(o4), b4(l4), b16(o16), b16(l16), g_mix_a, g_mix_b, w_out, x, mod, g_attn_post)
    w_up, w_down = mlp_weights((x1,))
    h2, u, a = _mlp_up(x1, mod, g_mlp_pre, w_up)
    gx, dy2, accb_d, accg_d = _mlp_down(a, w_down, x1, target, mod, g_mlp_post)

    flat = lambda t: t.reshape(BL * SEQ, t.shape[-1])
    mod = grad_ready("w_down", _matmul_tn(flat(a), flat(dy2), tn=D, col_blocked=False, name="grad_w_down", out_dtype=bf16), mod)
    du, gx1, accb_m, accg_m = _mlp_bwd(dy2, u, w_down, w_up, x1, gx, mod, g_mlp_pre)
    mod = grad_ready("w_up", _matmul_tn(flat(h2), flat(du), tn=D, col_blocked=True, name="grad_w_up", out_dtype=bf16), mod)

    dy, doa, do1, do4, do16, da, dl1, dl4, dl16, accb_o, accg_o = _attn_out_bwd(
        gx1, y, mod, g_attn_post, w_out, oa, ob, g_mix_a, g_mix_b, l1, b4(l4), b16(l16))
    sink_behind = grad_ready("w_out", _matmul_tn(flat(mixed), flat(dy), tn=D, col_blocked=False, name="grad_w_out",
                                                  out_dtype=bf16), sink_a)
    dqa, dka, dva, dsink = _attn_bwd(qa, ka, va, doa, da, la, sink_behind, max_dist=BLK - 1, name="attn_a_bwd")
    d1 = _attn_bwd(q1, k1, v1, do1, dl1, l1, None, max_dist=BLK, name="attn_b1_bwd")
    d4 = _attn_bwd(q4, k4, v4, seqs(do4), seqs(dl4), l4, None, max_dist=BLK, name="attn_b4_bwd")
    d16 = _attn_bwd(q16, k16, v16, seqs(do16), seqs(dl16), l16, None, max_dist=BLK, name="attn_b16_bwd")
    dproj, grad_x, accb_i, accg_i = _attn_in_bwd(dqa, dka, dva, d1, [b4(t) for t in d4], [b16(t) for t in d16],
                                                 tc, ts1, ts2, w_in, x, gx1, mod, g_attn_pre)
    gw_in = _grad_w_in(flat(h), flat(dproj))
    dsink = grad_ready("w_in", gw_in, dsink)

    return grad_x, (accb_i, accb_o, accb_m, accb_d, accg_i, accg_o, accg_m, accg_d, dsink)


ADAW = NMOD * D // NCHIP


def _pos():
    return lax.axis_index("x"), lax.axis_index("y"), lax.axis_index("c")


def _flip(v, bit):
    return 1 - v if bit else v


def _all_peers(x, y, c):
    return [(_flip(x, k >> 2 & 1), _flip(y, k >> 1 & 1), _flip(c, k & 1)) for k in range(1, NDEV)]


def _other_chips(x, y):
    return [(1 - x, y), (x, 1 - y), (1 - x, 1 - y)]


def _rcopy(src, dst, send, recv, k, dev, k_recv=None):
    return pltpu.make_async_remote_copy(src_ref=src, dst_ref=dst, send_sem=send.at[k],
                                        recv_sem=recv.at[k if k_recv is None else k_recv],
                                        device_id=dev, device_id_type=MESH)


def _small_copies(src, land, send, recv):
    x, y, c = _pos()
    me = 4 * x + 2 * y + c
    return [(_rcopy(src, land.at[me], send, recv, k, p), _rcopy(src, land.at[4 * p[0] + 2 * p[1] + p[2]], send, recv, k, p))
            for k, p in enumerate(_all_peers(x, y, c))]


def _ada_fwd(c_in, landed, w_ada, b_cols):
    def body(c_ref, land, w_hbm, b_ref, mod_ref, cond_ref, mbuf, w_ref, s2, r2, wsem):
        x, y, c = _pos()
        chip = 2 * x + y
        me = 4 * x + 2 * y + c
        wcopy = pltpu.make_async_copy(w_hbm, w_ref, wsem)
        wcopy.start()
        for i in range(NDEV):
            @pl.when(me == i)
            def _():
                cond_ref[BL * i:BL * (i + 1), :] = c_ref[...]

            @pl.when(me != i)
            def _():
                cond_ref[BL * i:BL * (i + 1), :] = land[i]
        call = cond_ref[...]
        cond = call / (1.0 + jnp.exp(-call))
        cond_ref[...] = cond
        wcopy.wait()
        mbuf[chip] = _dot(cond.astype(bf16), w_ref[...].astype(bf16)) + b_ref[...]
        chips = _other_chips(x, y)
        sends = [_rcopy(mbuf.at[chip], mbuf.at[chip], s2, r2, j, (px, py, c)) for j, (px, py) in enumerate(chips)]
        for cp in sends:
            cp.start()
        for j, (px, py) in enumerate(chips):
            _rcopy(mbuf.at[chip], mbuf.at[2 * px + py], s2, r2, j, (px, py, c)).wait_recv()
        for cp in sends:
            cp.wait_send()
        row = lax.broadcasted_iota(jnp.int32, (BL * NDEV, ADAW), 0)
        for s in range(NCHIP):
            slab = mbuf[s]
            for j in range(BL):
                mod_ref[j:j + 1, ADAW * s:ADAW * (s + 1)] = jnp.sum(jnp.where(row == BL * me + j, slab, 0.0), axis=0, keepdims=True)

    vm = pl.BlockSpec(memory_space=pltpu.VMEM)
    return pl.pallas_call(
        body, name="ada_fwd", in_specs=[vm, vm, pl.BlockSpec(memory_space=pl.ANY), vm], out_specs=[vm, vm],
        out_shape=[jax.ShapeDtypeStruct((BL, NMOD * D), f32), jax.ShapeDtypeStruct((BL * NDEV, D), f32)],
        scratch_shapes=[pltpu.VMEM((NCHIP, BL * NDEV, ADAW), f32), pltpu.VMEM((D, ADAW), f32),
                        pltpu.SemaphoreType.DMA((NCHIP - 1,)), pltpu.SemaphoreType.DMA((NCHIP - 1,)),
                        pltpu.SemaphoreType.DMA],
        compiler_params=pltpu.CompilerParams(vmem_limit_bytes=VMEM_LIMIT),
    )(c_in, landed, w_ada, b_cols)


PAY_ROWS = 4


def _small_pack(accs):
    def body(bi, bo, bm, bd, gi, go, gm, gd, dsink, pay):
        pay[...] = jnp.zeros_like(pay)
        for b in range(BL):
            for k, (ref, r) in enumerate(((bi, 1), (bi, 0), (bo, 0), (bm, 1), (bm, 0), (bd, 0))):
                pay[b:b + 1, D * k:D * (k + 1)] = ref[b, r:r + 1, :]
        for off, ref, r in ((OFF_G_ATTN_PRE, gi, 0), (OFF_G_ATTN_POST, go, 0), (OFF_G_MIX_A, go, 1), (OFF_G_MLP_PRE, gm, 0),
                            (OFF_G_MLP_POST, gd, 0)):
            pay[BL:BL + 1, off:off + D] = ref[r:r + 1, :]
        eye = lax.broadcasted_iota(jnp.int32, (NHEAD, LANES), 0) == lax.broadcasted_iota(jnp.int32, (NHEAD, LANES), 1)
        pay[BL:BL + 1, OFF_SINK:OFF_SINK + LANES] = jnp.sum(jnp.where(eye, dsink[...], 0.0), axis=0, keepdims=True)
        pay[BL:BL + 1, OFF_LOSS:OFF_LOSS + LANES] = gd[1:2, 0:LANES]

    vm = pl.BlockSpec(memory_space=pltpu.VMEM)
    return pl.pallas_call(body, name="small_pack", in_specs=[vm] * 9, out_specs=vm,
                          out_shape=jax.ShapeDtypeStruct((PAY_ROWS, PAYW), f32))(*accs)


def _small_sum(own, landed, cond_all):
    def body(pay, land, cond_ref, gw_ref, gb_ref, small_ref, pbuf, dall):
        x, y, c = _pos()
        chip = 2 * x + y
        me = 4 * x + 2 * y + c
        for i in range(NDEV):
            @pl.when(me == i)
            def _():
                pbuf[i] = pay[...]

            @pl.when(me != i)
            def _():
                pbuf[i] = land[i]
        small = pbuf[0, BL:BL + 1, :]
        for i in range(1, NDEV):
            small = small + pbuf[i, BL:BL + 1, :]
        small_ref[...] = small
        for i in range(NDEV):
            dall[BL * i:BL * (i + 1), :] = pbuf[i, 0:BL, :]
        gb_ref[...] = jnp.sum(dall[...], axis=0, keepdims=True)
        cols = jnp.zeros((BL * NDEV, ADAW), f32)
        for s in range(NCHIP):
            cols = cols + jnp.where(chip == s, dall[:, ADAW * s:ADAW * (s + 1)], 0.0)
        gw_ref[...] = _dot_tn(cond_ref[...].astype(bf16), cols.astype(bf16))

    vm = pl.BlockSpec(memory_space=pltpu.VMEM)
    return pl.pallas_call(
        body, name="small_sum", in_specs=[vm] * 3, out_specs=[vm] * 3,
        out_shape=[jax.ShapeDtypeStruct((D, ADAW), f32), jax.ShapeDtypeStruct((1, PAYW), f32), jax.ShapeDtypeStruct((1, PAYW), f32)],
        scratch_shapes=[pltpu.VMEM((NDEV, PAY_ROWS, PAYW), f32), pltpu.VMEM((BL * NDEV, PAYW), f32)],
        compiler_params=pltpu.CompilerParams(vmem_limit_bytes=VMEM_LIMIT),
    )(own, landed, cond_all)


def _half(ref, c):
    r2 = ref.shape[0] // 2
    return ref.at[pl.ds(c * r2 if isinstance(c, int) else pl.multiple_of(c * r2, 16), r2), :]


HBM_SPEC = pl.BlockSpec(memory_space=pltpu.HBM)
SEM_SPEC = pl.BlockSpec(memory_space=pltpu.SEMAPHORE)
EFFECT = pltpu.SideEffectType.DATAFLOW_SIDE_EFFECTING
NLINK = NCHIP - 1


def _in_hbm(a):
    return pltpu.with_memory_space_constraint(a, pltpu.HBM)


NSEM = 8


def _split_start(name, srcs, land_shapes, builds, carry, after=(), lands=None):
    n = len(srcs)
    na, nc = len(after), len(carry)

    def body(*refs):
        src, land = refs[:n], refs[n:2 * n]
        kept = refs[2 * n + na:2 * n + na + nc]
        outs = refs[2 * n + na + nc:]
        send, recv, passed = outs[:n], outs[n:2 * n], outs[4 * n:]
        for t in range(n):
            for out_cp, _ in builds[t](src[t], land[t], send[t], recv[t]):
                out_cp.start()
        for a, b in zip(kept, passed):
            b[...] = a[...]

    if lands is None:
        lands = [lax.empty(s.shape, s.dtype) for s in land_shapes]
    lands = [_in_hbm(a) for a in lands]
    sems = [pltpu.SemaphoreType.DMA((NSEM,))] * (2 * n)
    thru = [pltpu.HBM(a.shape, a.dtype) for a in list(srcs) + lands]
    vm = pl.BlockSpec(memory_space=pltpu.VMEM)
    res = pl.pallas_call(
        body, name=name, out_shape=sems + thru + [jax.ShapeDtypeStruct(a.shape, a.dtype) for a in carry],
        in_specs=[HBM_SPEC] * (2 * n) + [pl.BlockSpec(memory_space=pl.ANY)] * na + [vm] * nc,
        out_specs=[SEM_SPEC] * (2 * n) + [HBM_SPEC] * (2 * n) + [vm] * nc,
        input_output_aliases={i: 2 * n + i for i in range(2 * n)},
        compiler_params=pltpu.CompilerParams(has_side_effects=EFFECT),
    )(*[_in_hbm(a) for a in srcs], *lands, *after, *carry)
    flight = [(res[2 * n + t], res[3 * n + t], res[t], res[n + t]) for t in range(n)]
    return flight, list(res[4 * n:])


def _split_wait(name, flight, builds, after):
    m = len(flight)
    na = len(after)

    def body(*refs):
        src, land, send, recv = refs[:m], refs[m:2 * m], refs[2 * m:3 * m], refs[3 * m:4 * m]
        for t in range(m):
            for out_cp, in_cp in builds[t](src[t], land[t], send[t], recv[t]):
                out_cp.wait_send()
                in_cp.wait_recv()

    ops = [f[0] for f in flight] + [f[1] for f in flight] + [f[2] for f in flight] + [f[3] for f in flight]
    res = pl.pallas_call(
        body, name=name, out_shape=[pltpu.HBM(a.shape, a.dtype) for a in ops[:2 * m]],
        in_specs=[HBM_SPEC] * (2 * m) + [SEM_SPEC] * (2 * m) + [pl.BlockSpec(memory_space=pl.ANY)] * na,
        out_specs=[HBM_SPEC] * (2 * m), input_output_aliases={i: i for i in range(2 * m)},
        compiler_params=pltpu.CompilerParams(has_side_effects=EFFECT),
    )(*ops, *after)
    return res[:m], res[m:2 * m]


def _weight_copies(src, land, send, recv):
    x, y, c = _pos()
    chip = 2 * x + y
    return [(_rcopy(_half(src, c), _half(land.at[chip], c), send, recv, j, (px, py, c)),
             _rcopy(_half(src, c), _half(land.at[2 * px + py], c), send, recv, j, (px, py, c)))
            for j, (px, py) in enumerate(_other_chips(x, y))]


NDIRECT = NDEV - 1


def _direct_grad_copies(src, land, send, recv):
    x, y, c = _pos()
    out, arrive = [], []
    for j, (px, py) in enumerate(_other_chips(x, y)):
        for hc in range(2):
            out.append(_rcopy(_half(src.at[2 * px + py], hc), land.at[2 * j + c], send, recv, 2 * j + hc, (px, py, hc),
                              k_recv=2 * j + c))
            arrive.append(_rcopy(_half(src.at[2 * px + py], hc), land.at[2 * j + hc], send, recv, 2 * j + hc, (px, py, hc)))
    own = _rcopy(_half(src.at[2 * x + y], 1 - c), land.at[NDIRECT - 1], send, recv, NDIRECT - 1, (x, y, 1 - c))
    return list(zip(out, arrive)) + [(own, own)]


def _pair_weight_copies(src, land, send, recv):
    x, y, c = _pos()
    sib = (x, y, 1 - c)
    cps = []
    for j, (px, py) in enumerate(_other_chips(x, y)):
        mine, theirs = _half(land.at[2 * px + py], c), _half(land.at[2 * px + py], 1 - c)
        cps.append((_rcopy(mine, mine, send, recv, j, sib), _rcopy(theirs, theirs, send, recv, j, sib)))
    own = _rcopy(src, land.at[2 * x + y], send, recv, NLINK, sib)
    return cps + [(own, own)]


RS_ROWS = 256


def _chip_add(own, landed, pos_arr, name):
    nl, r2, cw = landed.shape
    rows = min(RS_ROWS, r2)
    nr = r2 // rows

    def body(s_ref, h_ref, q_ref, o_ref):
        acc = h_ref[...].astype(f32)
        for j in range(nl):
            acc = acc + q_ref[j].astype(f32)
        o_ref[...] = acc

    gs = pltpu.PrefetchScalarGridSpec(
        num_scalar_prefetch=1, grid=(nr,),
        in_specs=[pl.BlockSpec((None, rows, cw), lambda j, s: (s[0], s[1] * nr + j, 0)),
                  pl.BlockSpec((nl, rows, cw), lambda j, s: (0, j, 0))],
        out_specs=pl.BlockSpec((rows, cw), lambda j, s: (s[1] * nr + j, 0)))
    return pl.pallas_call(body, name=name, grid_spec=gs, out_shape=jax.ShapeDtypeStruct((2 * r2, cw), f32),
                          compiler_params=_cp(("arbitrary",)))(pos_arr, own, landed)


def _pair_gather_copies(src, land, send, recv):
    x, y, c = _pos()
    sib = (x, y, 1 - c)
    return [(_rcopy(_half(land, c), _half(land, c), send, recv, 0, sib),
             _rcopy(_half(land, 1 - c), _half(land, 1 - c), send, recv, 0, sib))]


def _adamw_math(w, g, m, v):
    m = B1 * m + (1.0 - B1) * g
    v = B2 * v + (1.0 - B2) * jnp.square(g)
    m_hat = m / (1.0 - B1 ** STEP)
    v_hat = v / (1.0 - B2 ** STEP)
    return -LR * (m_hat / (jnp.sqrt(v_hat) + AEPS) + WD * w), m, v


ADAM_BLOCK = 512 * 1024


def _adamw(w, g, m, v, name, after=(), landed=True):
    r, cw = w.shape
    na = len(after)

    def body(w_ref, g_ref, m_ref, v_ref, *rest):
        outs = rest[na:]
        g = g_ref[...]
        if landed:
            outs[0][...] = g
        outs[-3][...], outs[-2][...], outs[-1][...] = _adamw_math(w_ref[...], g, m_ref[...], v_ref[...])

    rows = max(k for k in range(SUBLANES, ADAM_BLOCK // cw + 1, SUBLANES) if r % k == 0)
    spec = pl.BlockSpec((rows, cw), lambda i: (i, 0))
    nout = 4 if landed else 3
    res = pl.pallas_call(body, name=name, grid=(r // rows,), in_specs=[spec] * 4 + [pl.BlockSpec(memory_space=pl.ANY)] * na,
                         out_specs=[spec] * nout, out_shape=[jax.ShapeDtypeStruct((r, cw), f32)] * nout,
                         compiler_params=_cp(("arbitrary",)))(w, g, m, v, *after)
    return list(res) if landed else [g, *res]


SMALL = (("b_ada", None, PAYW), ("g_attn_pre", OFF_G_ATTN_PRE, D), ("g_attn_post", OFF_G_ATTN_POST, D), ("sink_a", OFF_SINK, 8),
         ("g_mix_a", OFF_G_MIX_A, AQ), ("g_mix_b", OFF_G_MIX_B, BW), ("g_mlp_pre", OFF_G_MLP_PRE, D), ("g_mlp_post", OFF_G_MLP_POST, D))


def _adamw_small(small, gb, params):
    n = len(SMALL)

    def body(*refs):
        small_ref, gb_ref = refs[:2]
        wmv = refs[2:2 + 3 * n]
        loss_ref = refs[2 + 3 * n]
        outs = refs[3 + 3 * n:]
        loss_ref[...] = small_ref[:, OFF_LOSS:OFF_LOSS + 1] * (0.5 / D)
        for i, (_, off, width) in enumerate(SMALL):
            g = gb_ref[...] if off is None else small_ref[:, off:off + width]
            w_ref, m_ref, v_ref = wmv[3 * i:3 * i + 3]
            outs[4 * i][...] = g
            outs[4 * i + 1][...], outs[4 * i + 2][...], outs[4 * i + 3][...] = _adamw_math(w_ref[...], g, m_ref[...], v_ref[...])

    vm = pl.BlockSpec(memory_space=pltpu.VMEM)
    out_shape = [jax.ShapeDtypeStruct((1, 1), f32)]
    for _, _, width in SMALL:
        out_shape += [jax.ShapeDtypeStruct((1, width), f32)] * 4
    flat = [a for wmv in params for a in wmv]
    res = pl.pallas_call(body, name="adamw_small", in_specs=[vm] * (2 + 3 * n), out_specs=[vm] * len(out_shape),
                         out_shape=out_shape)(small, gb, *flat)
    return res[0], {name: res[1 + 4 * i:5 + 4 * i] for i, (name, _, _) in enumerate(SMALL)}


def kernel(x, c, positions, w_ada, b_ada, g_attn_pre, g_attn_post, w_in, sink_a, g_mix_a, g_mix_b, w_out, g_mlp_pre, g_mlp_post, w_up, w_down, loss_target, m_w_ada, m_b_ada, m_g_attn_pre, m_g_attn_post, m_w_in, m_sink_a, m_g_mix_a, m_g_mix_b, m_w_out, m_g_mlp_pre, m_g_mlp_post, m_w_up, m_w_down, v_w_ada, v_b_ada, v_g_attn_pre, v_g_attn_post, v_w_in, v_sink_a, v_g_mix_a, v_g_mix_b, v_w_out, v_g_mlp_pre, v_g_mlp_post, v_w_up, v_w_down):
    given = dict(w_ada=w_ada, b_ada=b_ada, g_attn_pre=g_attn_pre, g_attn_post=g_attn_post, w_in=w_in, sink_a=sink_a, g_mix_a=g_mix_a,
                 g_mix_b=g_mix_b, w_out=w_out, g_mlp_pre=g_mlp_pre, g_mlp_post=g_mlp_post, w_up=w_up, w_down=w_down)
    moms = dict(w_ada=(m_w_ada, v_w_ada), b_ada=(m_b_ada, v_b_ada), g_attn_pre=(m_g_attn_pre, v_g_attn_pre),
                g_attn_post=(m_g_attn_post, v_g_attn_post), w_in=(m_w_in, v_w_in), sink_a=(m_sink_a, v_sink_a),
                g_mix_a=(m_g_mix_a, v_g_mix_a), g_mix_b=(m_g_mix_b, v_g_mix_b), w_out=(m_w_out, v_w_out),
                g_mlp_pre=(m_g_mlp_pre, v_g_mlp_pre), g_mlp_post=(m_g_mlp_post, v_g_mlp_post), w_up=(m_w_up, v_w_up),
                w_down=(m_w_down, v_w_down))
    order = ["w_ada", "b_ada", "g_attn_pre", "g_attn_post", "w_in", "sink_a", "g_mix_a", "g_mix_b", "w_out", "g_mlp_pre",
             "g_mlp_post", "w_up", "w_down"]
    xi, yi, ci = _pos()
    chip = 2 * xi + yi

    pos_arr = jnp.stack([chip, ci]).astype(jnp.int32)
    big = ("w_in", "w_out", "w_up", "w_down")

    gathered = [jax.ShapeDtypeStruct((NCHIP,) + given[n].shape[1:], bf16) for n in big]
    (flight_c, *flight_in), (inv_lane,) = _split_start(
        "weights_start_first", [c, w_in[0].astype(bf16)], [jax.ShapeDtypeStruct((NDEV, BL, D), f32), gathered[0]],
        [_small_copies, _weight_copies], [_inv_lane()])
    inv_lane, rest = lax.optimization_barrier((inv_lane, [given[n][0] for n in big[1:]]))
    tabs = _rope_tables(positions.reshape(BL * SEQ, 1), inv_lane)
    rest = [w.astype(bf16) for w in rest]
    b_cols = lax.dynamic_slice(b_ada, (0, chip * ADAW), (1, ADAW))
    (c_own,), (c_all,) = _split_wait("cond_wait", [flight_c], [_small_copies], (*tabs, *rest))
    mod, cond_all = _ada_fwd(c_own, c_all, w_ada[0], b_cols)

    srcs, lands = _split_wait("weights_wait_first", flight_in, [_weight_copies], (mod,))
    cross, (mod,) = _split_start("weights_pair_start_first", srcs, None, [_pair_weight_copies], [mod], lands=lands)
    flight_rest, (mod,) = _split_start("weights_start_rest", rest, gathered[1:], [_weight_copies] * 3, [mod])
    _, (win_g,) = _split_wait("weights_pair_wait_first", cross, [_pair_weight_copies], (mod,))
    mod = mod.reshape(BL, NMOD, D)

    def later_weights(after, carry):
        srcs, lands = _split_wait("weights_wait_rest", flight_rest, [_weight_copies] * 3, after)
        fl, (carry,) = _split_start("weights_pair_start_rest", srcs, None, [_pair_weight_copies] * 3, [carry], lands=lands)
        def out_weight(after):
            _, (wout_g,) = _split_wait("weights_pair_wait_out", fl[:1], [_pair_weight_copies], after)
            return wout_g.reshape(D, D)

        def mlp_weights(after):
            _, (wup_g, wdn_g) = _split_wait("weights_pair_wait_mlp", fl[1:], [_pair_weight_copies] * 2, after)
            return wup_g, wdn_g.reshape(DFF, D)

        return out_weight, mlp_weights, carry

    waiting, pending = {}, {}

    def send_grads(carry):
        names = list(waiting)
        slabs = [waiting.pop(n) for n in names]
        lands = [jax.ShapeDtypeStruct((NDIRECT, s.shape[1] // 2, s.shape[2]), bf16) for s in slabs]
        fl, (carry,) = _split_start("grad_start_" + names[-1], slabs, lands, [_direct_grad_copies] * len(names), [carry])
        for n, f in zip(names, fl):
            pending[n] = [f]
        return carry

    def grad_ready(name, g, carry):
        waiting[name] = g if g.ndim == 3 else g.reshape(NCHIP, g.shape[0] // NCHIP, g.shape[1])
        return send_grads(carry) if name == "w_out" else carry

    grad_x, accs = _local_step(x, tabs, mod, loss_target, win_g, later_weights, grad_ready,
                               g_attn_pre, g_attn_post, sink_a, g_mix_a, g_mix_b, g_mlp_pre, g_mlp_post)

    grads, out = {}, {}

    def update(n, after=()):
        tr = (lambda a: a.T) if n == "w_in" else (lambda a: a)
        res = _adamw(tr(given[n][0]), tr(grads[n]), tr(moms[n][0][0]), tr(moms[n][1][0]), "adamw_" + n, after,
                     landed=n != "w_ada")
        out[n] = tuple(tr(a)[None] for a in res)
        return res[3]

    def reduce(names, after):
        fl = sum((pending[n] for n in names), [])
        halves, landed = _split_wait("grad_wait_" + names[0], fl, [_direct_grad_copies] * len(names), after)
        fulls = [_chip_add(h, q, pos_arr, "grad_chip_sum_" + n) for h, q, n in zip(halves, landed, names)]
        tokens = [jnp.full((SUBLANES, LANES), float(i), f32) for i in range(len(names))]
        return _split_start("grad_gather_start_" + names[0], tokens, None, [_pair_gather_copies] * len(names), [],
                            lands=fulls)[0]

    def gathered_update(n, flight, after):
        _, (grads[n],) = _split_wait("grad_gather_wait_" + n, [flight], [_pair_gather_copies], after)
        return update(n)

    slab = waiting.pop("w_in")
    (fl_small, fl_in), (cond_all,) = _split_start(
        "small_start", [_small_pack(accs), slab],
        [jax.ShapeDtypeStruct((NDEV, PAY_ROWS, PAYW), f32), jax.ShapeDtypeStruct((NDIRECT, slab.shape[1] // 2, slab.shape[2]), bf16)],
        [_small_copies, _direct_grad_copies], [cond_all])
    pending["w_in"] = [fl_in]
    fl_down, fl_up, fl_out = reduce(("w_down", "w_up", "w_out"), (cond_all,))
    (pay,), (landed,) = _split_wait("small_wait", [fl_small], [_small_copies], (fl_out[0],))
    grads["w_ada"], gb, small = _small_sum(pay, landed, cond_all)
    last = update("w_ada")
    last = gathered_update("w_down", fl_down, (last,))
    last = gathered_update("w_up", fl_up, (last,))
    (fl_in,) = reduce(("w_in",), (last,))
    last = gathered_update("w_out", fl_out, (fl_in[0],))
    gathered_update("w_in", fl_in, (last,))
    loss, res = _adamw_small(small, gb, [(given[n], moms[n][0], moms[n][1]) for n, _, _ in SMALL])
    for n, _, _ in SMALL:
        out[n] = tuple(res[n])
    return (loss.reshape(()), grad_x, *[out[n][0] for n in order], *[out[n][1] for n in order],
            *[out[n][2] for n in order], *[out[n][3] for n in order])
```

```python
import numpy as np
import jax
import jax.numpy as jnp
from jax import lax
from jax.experimental import pallas as pl
from jax.experimental.pallas import tpu as pltpu

f32 = jnp.float32
bf16 = jnp.bfloat16
MESH = pl.DeviceIdType.MESH

D = 1024
SEQ = 2048
BL = 2
HD = 64
AQ = 512
AKV = 128
BW = 512
INW = 2304
DFF = 4096
NMOD = 6
ROT = 16
THETA = 500000.0
EPS = 1e-6
NEG = -1e30
BLK = 128
TM = 512
NJ = SEQ // TM
LANES = 128
SUBLANES = 8
NHEAD = AQ // HD
QSCALE = HD ** -0.5
NCHIP = 4
NDEV = 8
VMEM_LIMIT = 56 << 20

LR, B1, B2, AEPS, WD, STEP = 0.001, 0.9, 0.999, 1e-08, 0.01, 10

OFF_G_ATTN_PRE, OFF_G_ATTN_POST, OFF_G_MIX_A, OFF_G_MIX_B = 0, 1024, 2048, 2560
OFF_G_MLP_PRE, OFF_G_MLP_POST, OFF_SINK, OFF_LOSS = 3072, 4096, 5120, 5248
PAYW = NMOD * D


def _cp(sem=None):
    return pltpu.CompilerParams(dimension_semantics=sem, vmem_limit_bytes=VMEM_LIMIT)


def _dot(a, b):
    return jnp.dot(a, b, preferred_element_type=f32)


def _dot_nt(a, b):
    return lax.dot_general(a, b, (((1,), (1,)), ((), ())), preferred_element_type=f32)


def _dot_tn(a, b):
    return lax.dot_general(a, b, (((0,), (0,)), ((), ())), preferred_element_type=f32)


def _rms(x):
    r = lax.rsqrt(jnp.mean(x * x, axis=-1, keepdims=True) + EPS)
    return x * r, r


def _rms_bwd(dy, y, r):
    return r * (dy - y * jnp.mean(dy * y, axis=-1, keepdims=True))


def _colsum(v):
    return jnp.sum(v, axis=0, keepdims=True)


def _rope(p, c, s1, s2):
    outs = []
    for c0 in range(0, p.shape[1], LANES):
        pc = p[:, c0:c0 + LANES]
        outs.append(pc * c + pltpu.roll(pc, LANES - ROT // 2, 1) * s1 + pltpu.roll(pc, ROT // 2, 1) * s2)
    return outs[0] if len(outs) == 1 else jnp.concatenate(outs, axis=1)


def _rope_t(g, c, s1, s2):
    outs = []
    for c0 in range(0, g.shape[1], LANES):
        gc = g[:, c0:c0 + LANES]
        outs.append(gc * c + pltpu.roll(gc * s1, ROT // 2, 1) + pltpu.roll(gc * s2, LANES - ROT // 2, 1))
    return outs[0] if len(outs) == 1 else jnp.concatenate(outs, axis=1)


def _perm_store(val, scr, out_ref, d):
    nc = val.shape[1] // LANES
    for c in range(nc):
        scr[c] = val[:, LANES * c:LANES * (c + 1)]
    for c in range(nc):
        for r in range(d):
            out_ref[r, :, LANES * c:LANES * (c + 1)] = scr[c, pl.ds(r, TM // d, stride=d), :].astype(out_ref.dtype)


def _perm_load(in_ref, scr, d):
    nc = in_ref.shape[-1] // LANES
    for c in range(nc):
        for r in range(d):
            scr[c, pl.ds(r, TM // d, stride=d), :] = in_ref[r, :, LANES * c:LANES * (c + 1)].astype(f32)
    return jnp.concatenate([scr[c] for c in range(nc)], axis=1)


def _per_query_head(kv):
    r = pltpu.roll(kv, HD, 1)
    lo = lax.broadcasted_iota(jnp.int32, kv.shape, 1) < HD
    return jnp.concatenate([jnp.where(lo, kv, r), jnp.where(lo, r, kv)], axis=1)


def _per_kv_head(g):
    g0, g1 = g[:, :LANES] + g[:, LANES:2 * LANES], g[:, 2 * LANES:3 * LANES] + g[:, 3 * LANES:]
    lo = lax.broadcasted_iota(jnp.int32, g0.shape, 1) < HD
    return jnp.where(lo, g0 + pltpu.roll(g0, HD, 1), g1 + pltpu.roll(g1, HD, 1))


def _tok(w):
    return pl.BlockSpec((None, TM, w), lambda b, j: (b, j, 0))


def _perm_spec(d, w):
    return pl.BlockSpec((None, d, TM // d, w), lambda b, j: (b, 0, j, 0))


def _full(shape):
    n = len(shape)
    return pl.BlockSpec(shape, lambda b, j: (0,) * n)


MOD_SPEC = pl.BlockSpec((None, NMOD, D), lambda b, j: (b, 0, 0))
ACCB_SPEC = pl.BlockSpec((None, SUBLANES, D), lambda b, j: (b, 0, 0))
ACCG_SPEC = pl.BlockSpec((SUBLANES, D), lambda b, j: (0, 0))
ACC_SHAPES = [jax.ShapeDtypeStruct((BL, SUBLANES, D), f32), jax.ShapeDtypeStruct((SUBLANES, D), f32)]


def _acc_init(accb_ref, accg_ref):
    b, j = pl.program_id(0), pl.program_id(1)

    @pl.when(j == 0)
    def _():
        accb_ref[...] = jnp.zeros_like(accb_ref)

    @pl.when((b == 0) & (j == 0))
    def _():
        accg_ref[...] = jnp.zeros_like(accg_ref)


def _rope_tables(pos_col, inv_lane):
    def body(p_ref, inv_ref, c_ref, s1_ref, s2_ref):
        ang = p_ref[...].astype(f32) * inv_ref[...]
        j = lax.broadcasted_iota(jnp.int32, (TM, LANES), 1) % HD
        cs, sn = jnp.cos(ang), jnp.sin(ang)
        c_ref[...] = jnp.where(j < ROT, cs, 1.0)
        s1_ref[...] = jnp.where(j < ROT // 2, -sn, 0.0)
        s2_ref[...] = jnp.where((j >= ROT // 2) & (j < ROT), sn, 0.0)

    n = BL * SEQ // TM
    return pl.pallas_call(
        body, name="rope_tables", grid=(n,),
        in_specs=[pl.BlockSpec((TM, 1), lambda i: (i, 0)), pl.BlockSpec((1, LANES), lambda i: (0, 0))],
        out_specs=[pl.BlockSpec((TM, LANES), lambda i: (i, 0))] * 3,
        out_shape=[jax.ShapeDtypeStruct((BL * SEQ, LANES), f32)] * 3,
    )(pos_col, inv_lane)


def _attn_in(x, mod, g_pre, w_in, tc, ts1, ts2):
    def body(x_ref, mod_ref, g_ref, wg_ref, c_ref, s1_ref, s2_ref,
             h_ref, qa_ref, ka_ref, va_ref, q1_ref, k1_ref, v1_ref, q4_ref, k4_ref, v4_ref, q16_ref, k16_ref, v16_ref,
             w_ref, scr):
        @pl.when((pl.program_id(0) == 0) & (pl.program_id(1) == 0))
        def _():
            w_ref[...] = jnp.concatenate([wg_ref[s] for s in range(NCHIP)], axis=1)

        xn, _ = _rms(x_ref[...])
        h = (xn * g_ref[...]) * (1.0 + mod_ref[1:2, :]) + mod_ref[0:1, :]
        hb = h.astype(bf16)
        h_ref[...] = hb
        proj = _dot(hb, w_ref[...])
        c, s1, s2 = c_ref[...], s1_ref[...], s2_ref[...]
        o1, o2, o3, o4, o5 = AQ, AQ + AKV, AQ + 2 * AKV, AQ + 2 * AKV + BW, AQ + 2 * AKV + 2 * BW
        qa_ref[...] = (_rope(proj[:, :o1], c, s1, s2) * QSCALE).astype(bf16)
        ka_ref[...] = _per_query_head(_rope(proj[:, o1:o2], c, s1, s2)).astype(bf16)
        va_ref[...] = _per_query_head(proj[:, o2:o3]).astype(bf16)
        qb = _rope(proj[:, o3:o4], c, s1, s2) * QSCALE
        kb = _rope(proj[:, o4:o5], c, s1, s2)
        vb = proj[:, o5:]
        for val, r1, r4, r16 in ((qb, q1_ref, q4_ref, q16_ref), (kb, k1_ref, k4_ref, k16_ref), (vb, v1_ref, v4_ref, v16_ref)):
            r1[...] = val.astype(bf16)
            _perm_store(val, scr, r4, 4)
            _perm_store(val, scr, r16, 16)

    nat = lambda w: jax.ShapeDtypeStruct((BL, SEQ, w), bf16)
    p4 = jax.ShapeDtypeStruct((BL, 4, SEQ // 4, BW), bf16)
    p16 = jax.ShapeDtypeStruct((BL, 16, SEQ // 16, BW), bf16)
    return pl.pallas_call(
        body, name="attn_in", grid=(BL, NJ),
        in_specs=[_tok(D), MOD_SPEC, _full((1, D)), _full((NCHIP, D, INW // NCHIP)), _tok(LANES), _tok(LANES), _tok(LANES)],
        out_specs=([_tok(D), _tok(AQ), _tok(2 * AKV), _tok(2 * AKV)] + [_tok(BW)] * 3 + [_perm_spec(4, BW)] * 3 + [_perm_spec(16, BW)] * 3
                   + [_full((D, INW))]),
        out_shape=[nat(D), nat(AQ), nat(2 * AKV), nat(2 * AKV)] + [nat(BW)] * 3 + [p4] * 3 + [p16] * 3
                  + [jax.ShapeDtypeStruct((D, INW), bf16)],
        scratch_shapes=[pltpu.VMEM((BW // LANES, TM, LANES), f32)],
        compiler_params=_cp(("arbitrary", "arbitrary")),
    )(x, mod, g_pre, w_in, tc, ts1, ts2)


def _kv_cat(cur_ref, prev_ref, p, cache):
    key = (id(cur_ref), p)
    if key not in cache:
        sl = slice(LANES * p, LANES * (p + 1))
        cache[key] = cur_ref[:, sl] if prev_ref is None else jnp.concatenate([prev_ref[:, sl], cur_ref[:, sl]], axis=0)
    return cache[key]


def _lane_half(a, hh):
    lo = lax.broadcasted_iota(jnp.int32, a.shape, 1) < HD
    return jnp.where(lo, a, jnp.zeros_like(a)) if hh == 0 else jnp.where(lo, jnp.zeros_like(a), a)


ATT_UNITS = 4


def _att_units(nb):
    return ATT_UNITS if nb == 1 else min(ATT_UNITS, nb)


def _attn_specs(n, nb, descending):
    u = _att_units(nb)
    if nb == 1:
        return (lambda ww: pl.BlockSpec((u, BLK, ww), lambda a, i: (a, 0, 0))), None, (n // u, 1)
    steps = nb // u
    at = (lambda i: steps - 1 - i) if descending else (lambda i: i)
    cur = lambda ww: pl.BlockSpec((None, u * BLK, ww), lambda a, i: (a, at(i), 0))
    prev = lambda ww: pl.BlockSpec((None, BLK, ww), lambda a, i: (a, jnp.maximum(u * at(i) - 1, 0), 0))
    return cur, prev, (n, steps)


def _attn_fwd(q, k, v, sink, *, max_dist, o_dtype, name):
    n, l, w = q.shape
    wk = k.shape[-1]
    nb = l // BLK
    has_sink = sink is not None

    def body(*refs):
        sink_ref = None
        if has_sink:
            sink_ref, refs = refs[0], refs[1:]
        if nb > 1:
            q_ref, kc_ref, kp_ref, vc_ref, vp_ref, o_ref, lse_ref = refs[:7]
            first = pl.program_id(1) == 0
            for u in range(_att_units(nb)):
                rows, before = pl.ds(BLK * u, BLK), pl.ds(BLK * (u - 1), BLK)
                unit(q_ref.at[rows, :], kc_ref.at[rows, :], kp_ref if u == 0 else kc_ref.at[before, :],
                     vc_ref.at[rows, :], vp_ref if u == 0 else vc_ref.at[before, :], o_ref.at[rows, :], lse_ref.at[rows, :],
                     jnp.logical_not(first) if u == 0 else True, sink_ref, *refs[7:])
        else:
            q_ref, kc_ref, vc_ref, o_ref, lse_ref = refs[:5]
            for u in range(_att_units(nb)):
                unit(q_ref.at[u], kc_ref.at[u], None, vc_ref.at[u], None, o_ref.at[u], lse_ref.at[u], None, sink_ref, *refs[5:])

    def unit(q_ref, kc_ref, kp_ref, vc_ref, vp_ref, o_ref, lse_ref, has_prev, sink_ref, sscr, pscr, dscr):
        qi = lax.broadcasted_iota(jnp.int32, (BLK, BLK), 0)
        kj = lax.broadcasted_iota(jnp.int32, (BLK, BLK), 1)
        tri = kj <= qi
        eye = kj == qi
        cache = {}
        for p in range(w // LANES):
            qpair = q_ref[:, LANES * p:LANES * (p + 1)]
            kcat = _kv_cat(kc_ref, kp_ref, p // share, cache)
            for hh in range(2):
                s = _dot_nt(_lane_half(qpair, hh), kcat)
                if nb > 1:
                    sp = s[:, :BLK] if has_prev is True else jnp.where(has_prev, s[:, :BLK], NEG)
                    sscr[2 * p + hh] = jnp.where(tri, s[:, BLK:], sp)
                    if diag:
                        dscr[2 * p + hh] = jnp.where(eye, sp, NEG)
                else:
                    sscr[2 * p + hh] = jnp.where(tri, s, NEG)
        lane = lax.broadcasted_iota(jnp.int32, (BLK, LANES), 1)
        lse_all = jnp.zeros((BLK, LANES), f32)
        for p in range(w // LANES):
            for hh in range(2):
                h = 2 * p + hh
                comb = sscr[h]
                if diag:
                    dtile = dscr[h]
                    m = jnp.max(jnp.maximum(comb, dtile), axis=-1, keepdims=True)
                else:
                    m = jnp.max(comb, axis=-1, keepdims=True)
                if has_sink:
                    sk = sink_ref[0, h]
                    m = jnp.maximum(m, sk)
                e = jnp.exp(comb - m)
                if diag:
                    ed = jnp.exp(dtile - m)
                    den = jnp.sum(e + ed, axis=-1, keepdims=True)
                else:
                    den = jnp.sum(e, axis=-1, keepdims=True)
                if has_sink:
                    den = den + jnp.exp(sk - m)
                inv = 1.0 / den
                if nb > 1:
                    pscr[h, :, :BLK] = (jnp.where(tri, ed if diag else 0.0, e) * inv).astype(bf16)
                    pscr[h, :, BLK:] = (jnp.where(tri, e, 0.0) * inv).astype(bf16)
                else:
                    pscr[h] = (e * inv).astype(bf16)
                lse_all = jnp.where(lane == h, jnp.broadcast_to(m + jnp.log(den), (BLK, LANES)), lse_all)
        lse_ref[...] = lse_all
        for p in range(w // LANES):
            vcat = _kv_cat(vc_ref, vp_ref, p // share, cache)
            o_ref[:, LANES * p:LANES * (p + 1)] = (_dot(pscr[2 * p], _lane_half(vcat, 0))
                                                   + _dot(pscr[2 * p + 1], _lane_half(vcat, 1))).astype(o_ref.dtype)

    assert max_dist in (BLK - 1, BLK) and w % wk == 0
    share = w // wk
    diag = nb > 1 and max_dist == BLK
    cur, prev, grid = _attn_specs(n, nb, False)
    in_specs = [cur(w), cur(wk)] + ([prev(wk)] if nb > 1 else []) + [cur(wk)] + ([prev(wk)] if nb > 1 else [])
    args = [q, k] + ([k] if nb > 1 else []) + [v] + ([v] if nb > 1 else [])
    if has_sink:
        in_specs = [pl.BlockSpec(memory_space=pltpu.SMEM)] + in_specs
        args = [sink] + args
    return pl.pallas_call(
        body, name=name, grid=grid, in_specs=in_specs,
        out_specs=[cur(w), cur(LANES)],
        out_shape=[jax.ShapeDtypeStruct((n, l, w), o_dtype), jax.ShapeDtypeStruct((n, l, LANES), f32)],
        scratch_shapes=[pltpu.VMEM((w // HD, BLK, BLK), f32), pltpu.VMEM((w // HD, BLK, 2 * BLK if nb > 1 else BLK), bf16),
                        pltpu.VMEM((w // HD if diag else 1, BLK, BLK), f32)],
        compiler_params=_cp(("arbitrary", "arbitrary")),
    )(*args)


def _attn_bwd(q, k, v, do, delta, lse, sink, *, max_dist, name):
    n, l, w = q.shape
    wk = k.shape[-1]
    nb = l // BLK
    has_sink = sink is not None

    def body(*refs):
        sink_ref = dsink_ref = ck = cv = None
        if has_sink:
            sink_ref, refs = refs[0], refs[1:]
        nin = 8 if nb > 1 else 6
        ins, rest = refs[:nin], refs[nin:]
        if has_sink:
            dq_ref, dk_ref, dv_ref, dsink_ref = rest[:4]
            rest = rest[4:]
        else:
            dq_ref, dk_ref, dv_ref = rest[:3]
            rest = rest[3:]
        step = pl.program_id(1)
        if has_sink:
            @pl.when((pl.program_id(0) == 0) & (step == 0))
            def _():
                dsink_ref[...] = jnp.zeros_like(dsink_ref)

        if nb > 1:
            q_ref, kc_ref, kp_ref, vc_ref, vp_ref, do_ref, delta_ref, lse_ref = ins
            ck, cv = rest[:2]

            @pl.when(step == 0)
            def _():
                ck[...] = jnp.zeros_like(ck)
                cv[...] = jnp.zeros_like(cv)

            last = step == nb // _att_units(nb) - 1
            for u in reversed(range(_att_units(nb))):
                rows, before = pl.ds(BLK * u, BLK), pl.ds(BLK * (u - 1), BLK)
                unit(q_ref.at[rows, :], kc_ref.at[rows, :], kp_ref if u == 0 else kc_ref.at[before, :],
                     vc_ref.at[rows, :], vp_ref if u == 0 else vc_ref.at[before, :], do_ref.at[rows, :],
                     delta_ref.at[rows, :], lse_ref.at[rows, :], dq_ref.at[rows, :], dk_ref.at[rows, :], dv_ref.at[rows, :],
                     jnp.logical_not(last) if u == 0 else True, sink_ref, dsink_ref, ck, cv, *rest[2:])
        else:
            q_ref, kc_ref, vc_ref, do_ref, delta_ref, lse_ref = ins
            for u in range(_att_units(nb)):
                unit(q_ref.at[u], kc_ref.at[u], None, vc_ref.at[u], None, do_ref.at[u], delta_ref.at[u], lse_ref.at[u],
                     dq_ref.at[u], dk_ref.at[u], dv_ref.at[u], None, sink_ref, dsink_ref, None, None, *rest)

    def unit(q_ref, kc_ref, kp_ref, vc_ref, vp_ref, do_ref, delta_ref, lse_ref, dq_ref, dk_ref, dv_ref, has_prev,
             sink_ref, dsink_ref, ck, cv, sscr, dpscr, pscr, dsscr, dscr=None, ddscr=None):
        lane = lax.broadcasted_iota(jnp.int32, (BLK, LANES), 1)
        qi = lax.broadcasted_iota(jnp.int32, (BLK, BLK), 0)
        kj = lax.broadcasted_iota(jnp.int32, (BLK, BLK), 1)
        tri = kj <= qi
        eye = kj == qi
        cache = {}
        kp, vp = kp_ref, vp_ref
        for p in range(w // LANES):
            sl = slice(LANES * p, LANES * (p + 1))
            qpair, dopair = q_ref[:, sl], do_ref[:, sl]
            kcat, vcat = _kv_cat(kc_ref, kp, p // share, cache), _kv_cat(vc_ref, vp, p // share, cache)
            for hh in range(2):
                h = 2 * p + hh
                s = _dot_nt(_lane_half(qpair, hh), kcat)
                dp = _dot_nt(_lane_half(dopair, hh), vcat)
                if nb > 1:
                    sp = s[:, :BLK] if has_prev is True else jnp.where(has_prev, s[:, :BLK], NEG)
                    sscr[h] = jnp.where(tri, s[:, BLK:], sp)
                    dpscr[h] = jnp.where(tri, dp[:, BLK:], dp[:, :BLK])
                    if diag:
                        dscr[h] = jnp.where(eye, sp, NEG)
                        ddscr[h] = dp[:, :BLK]
                else:
                    sscr[h] = jnp.where(tri, s, NEG)
                    dpscr[h] = dp
        for p in range(w // LANES):
            for hh in range(2):
                h = 2 * p + hh
                lse_b = jnp.broadcast_to(lse_ref[:, h:h + 1], (BLK, BLK))
                delta = jnp.broadcast_to(delta_ref[:, h:h + 1], (BLK, BLK))
                pr = jnp.exp(sscr[h] - lse_b)
                ds = pr * (dpscr[h] - delta)
                if nb > 1:
                    if diag:
                        prd = jnp.exp(dscr[h] - lse_b)
                        dsd = prd * (ddscr[h] - delta)
                    else:
                        prd = dsd = 0.0
                    pscr[h, :, :BLK] = jnp.where(tri, prd, pr).astype(bf16)
                    pscr[h, :, BLK:] = jnp.where(tri, pr, 0.0).astype(bf16)
                    dsscr[h, :, :BLK] = jnp.where(tri, dsd, ds).astype(bf16)
                    dsscr[h, :, BLK:] = jnp.where(tri, ds, 0.0).astype(bf16)
                else:
                    pscr[h] = pr.astype(bf16)
                    dsscr[h] = ds.astype(bf16)
                if has_sink:
                    dsk = -jnp.sum(jnp.where(lane == 0, jnp.exp(sink_ref[0, h] - lse_b) * delta, 0.0), keepdims=True)
                    dsink_ref[h:h + 1, :] += jnp.broadcast_to(dsk, (1, LANES))
        for p in range(w // LANES):
            sl = slice(LANES * p, LANES * (p + 1))
            qpair, dopair = q_ref[:, sl], do_ref[:, sl]
            kcat = _kv_cat(kc_ref, kp, p // share, cache)
            dq_ref[:, sl] = _dot(dsscr[2 * p], _lane_half(kcat, 0)) + _dot(dsscr[2 * p + 1], _lane_half(kcat, 1))
            dk_pair = _dot_tn(dsscr[2 * p], _lane_half(qpair, 0)) + _dot_tn(dsscr[2 * p + 1], _lane_half(qpair, 1))
            dv_pair = _dot_tn(pscr[2 * p], _lane_half(dopair, 0)) + _dot_tn(pscr[2 * p + 1], _lane_half(dopair, 1))
            if nb > 1:
                dk_ref[:, sl] = dk_pair[BLK:] + ck[:, sl]
                dv_ref[:, sl] = dv_pair[BLK:] + cv[:, sl]
                ck[:, sl] = dk_pair[:BLK]
                cv[:, sl] = dv_pair[:BLK]
            else:
                dk_ref[:, sl] = dk_pair
                dv_ref[:, sl] = dv_pair

    assert max_dist in (BLK - 1, BLK) and w % wk == 0
    share = w // wk
    diag = nb > 1 and max_dist == BLK
    cur, prev, grid = _attn_specs(n, nb, True)
    in_specs = ([cur(w), cur(wk)] + ([prev(wk)] if nb > 1 else []) + [cur(wk)] + ([prev(wk)] if nb > 1 else [])
                + [cur(w), cur(LANES), cur(LANES)])
    args = [q, k] + ([k] if nb > 1 else []) + [v] + ([v] if nb > 1 else []) + [do, delta, lse]
    out_specs = [cur(w)] * 3
    out_shape = [jax.ShapeDtypeStruct((n, l, w), f32)] * 3
    if has_sink:
        in_specs = [pl.BlockSpec(memory_space=pltpu.SMEM)] + in_specs
        args = [sink] + args
        out_specs.append(pl.BlockSpec((NHEAD, LANES), lambda a, i: (0, 0)))
        out_shape.append(jax.ShapeDtypeStruct((NHEAD, LANES), f32))
    nh = w // HD
    scratch = [pltpu.VMEM((BLK, w), f32), pltpu.VMEM((BLK, w), f32)] if nb > 1 else []
    scratch += [pltpu.VMEM((nh, BLK, BLK), f32)] * 2 + [pltpu.VMEM((nh, BLK, 2 * BLK if nb > 1 else BLK), bf16)] * 2
    if diag:
        scratch += [pltpu.VMEM((nh, BLK, BLK), f32)] * 2
    return pl.pallas_call(
        body, name=name, grid=grid, in_specs=in_specs, out_specs=out_specs, out_shape=out_shape,
        scratch_shapes=scratch, compiler_params=_cp(("arbitrary", "arbitrary")),
    )(*args)


def _split2(x):
    hi = x.astype(bf16)
    return hi, (x - hi.astype(f32)).astype(bf16)


def _heads_to_lanes(xc, e):
    return sum(_dot(t, e) for t in _split2(xc))


def _lanes_to_heads(x, g):
    return sum(_dot(t, g) for t in _split2(x))


HEAD_EXPAND = (np.arange(LANES)[:, None] == np.arange(BW)[None, :] // HD).astype(np.float32)
HEAD_SUM = HEAD_EXPAND.T.copy()


def _branch_weights(l1_ref, l4_ref, l16_ref, scr):
    l4v = _perm_load(l4_ref, scr, 4)
    l16v = _perm_load(l16_ref, scr, 16)
    l1v = l1_ref[...]
    m = jnp.maximum(jnp.maximum(l1v, l4v), l16v)
    e1, e4, e16 = jnp.exp(l1v - m), jnp.exp(l4v - m), jnp.exp(l16v - m)
    z = e1 + e4 + e16
    return e1 / z, e4 / z, e16 / z


def _mix_out(oa, o1, l1, o4, l4, o16, l16, g_mix_a, g_mix_b, w_out, x, mod, g_post):
    def body(oa_ref, o1_ref, l1_ref, o4_ref, l4_ref, o16_ref, l16_ref, ga_ref, gb_ref, w_ref, x_ref, mod_ref, gp_ref, e_ref,
             x1_ref, y_ref, mixed_ref, ob_ref, scr):
        w1, w4, w16 = _branch_weights(l1_ref, l4_ref, l16_ref, scr)
        e = e_ref[...]
        x1w, x4w = _heads_to_lanes(w1, e), _heads_to_lanes(w4, e)
        ob = (x1w * o1_ref[...].astype(f32) + x4w * _perm_load(o4_ref, scr, 4)
              + (1.0 - x1w - x4w) * _perm_load(o16_ref, scr, 16))
        ob_ref[...] = ob
        oan, _ = _rms(oa_ref[...])
        obn, _ = _rms(ob)
        mixed = jnp.concatenate([oan * ga_ref[...], obn * gb_ref[...]], axis=1).astype(bf16)
        mixed_ref[...] = mixed
        y = _dot(mixed, w_ref[...])
        y_ref[...] = y
        yn, _ = _rms(y)
        x1_ref[...] = x_ref[...] + mod_ref[2:3, :] * (yn * gp_ref[...])

    nat = lambda w, dt: jax.ShapeDtypeStruct((BL, SEQ, w), dt)
    return pl.pallas_call(
        body, name="mix_out", grid=(BL, NJ),
        in_specs=[_tok(AQ), _tok(BW), _tok(LANES), _perm_spec(4, BW), _perm_spec(4, LANES), _perm_spec(16, BW),
                  _perm_spec(16, LANES), _full((1, AQ)), _full((1, BW)), _full((D, D)), _tok(D), MOD_SPEC, _full((1, D)),
                  _full((LANES, BW))],
        out_specs=[_tok(D), _tok(D), _tok(D), _tok(BW)],
        out_shape=[nat(D, f32), nat(D, f32), nat(D, bf16), nat(BW, f32)],
        scratch_shapes=[pltpu.VMEM((BW // LANES, TM, LANES), f32)],
        compiler_params=_cp(("arbitrary", "arbitrary")),
    )(oa, o1, l1, o4, l4, o16, l16, g_mix_a, g_mix_b, w_out, x, mod, g_post, jnp.asarray(HEAD_EXPAND, bf16))


def _mlp_up(x1, mod, g_pre, w_up):
    def body(x_ref, mod_ref, g_ref, w_ref, h_ref, u_ref, a_ref):
        xn, _ = _rms(x_ref[...])
        h = (xn * g_ref[...]) * (1.0 + mod_ref[4:5, :]) + mod_ref[3:4, :]
        hb = h.astype(bf16)
        h_ref[...] = hb
        for s in range(NCHIP):
            u = _dot(hb, w_ref[s])
            u_ref[:, D * s:D * (s + 1)] = u.astype(bf16)
            a_ref[:, D * s:D * (s + 1)] = jnp.square(jnp.maximum(u, 0.0)).astype(bf16)

    nat = lambda w: jax.ShapeDtypeStruct((BL, SEQ, w), bf16)
    return pl.pallas_call(
        body, name="mlp_up", grid=(BL, NJ),
        in_specs=[_tok(D), MOD_SPEC, _full((1, D)), _full((NCHIP, D, D))],
        out_specs=[_tok(D), _tok(DFF), _tok(DFF)], out_shape=[nat(D), nat(DFF), nat(DFF)],
        compiler_params=_cp(("arbitrary", "arbitrary")),
    )(x1, mod, g_pre, w_up)


def _mlp_down(a, w_down, x1, target, mod, g_post):
    def body(a_ref, w_ref, x_ref, t_ref, mod_ref, g_ref, gx_ref, dy_ref, accb_ref, accg_ref):
        _acc_init(accb_ref, accg_ref)
        y2 = _dot(a_ref[...], w_ref[...])
        yn, r = _rms(y2)
        g = g_ref[...]
        gt = mod_ref[5:6, :]
        n2 = yn * g
        err = x_ref[...] + gt * n2 - t_ref[...]
        gout = err * (1.0 / D)
        gx_ref[...] = gout
        dn2 = gout * gt
        dy_ref[...] = _rms_bwd(dn2 * g, yn, r).astype(bf16)
        accb_ref[0:1, :] += _colsum(gout * n2)
        accg_ref[0:1, :] += _colsum(dn2 * yn)
        accg_ref[1:2, :] += jnp.broadcast_to(jnp.sum(err * err, keepdims=True), (1, D))

    return pl.pallas_call(
        body, name="mlp_down", grid=(BL, NJ),
        in_specs=[_tok(DFF), _full((DFF, D)), _tok(D), _tok(D), MOD_SPEC, _full((1, D))],
        out_specs=[_tok(D), _tok(D), ACCB_SPEC, ACCG_SPEC],
        out_shape=[jax.ShapeDtypeStruct((BL, SEQ, D), f32), jax.ShapeDtypeStruct((BL, SEQ, D), bf16)] + ACC_SHAPES,
        compiler_params=_cp(("arbitrary", "arbitrary")),
    )(a, w_down, x1, target, mod, g_post)


def _mlp_bwd(dy2, u, w_down, w_up, x1, gx, mod, g_pre):
    def body(dy_ref, u_ref, wd_hbm, wu_hbm, x_ref, gx_ref, mod_ref, g_ref, du_ref, gx1_ref, accb_ref, accg_ref, wd, wu, sem):
        _acc_init(accb_ref, accg_ref)
        first = (pl.program_id(0) == 0) & (pl.program_id(1) == 0)
        c1 = pltpu.make_async_copy(wd_hbm, wd, sem.at[0])
        c2 = pltpu.make_async_copy(wu_hbm, wu, sem.at[1])

        @pl.when(first)
        def _():
            c1.start()
            c2.start()
            c1.wait()

        dy = dy_ref[...]
        for s in range(NCHIP):
            sl = slice(D * s, D * (s + 1))
            da = _dot_nt(dy, wd[sl, :])
            du_ref[:, sl] = (da * (2.0 * jnp.maximum(u_ref[:, sl].astype(f32), 0.0))).astype(bf16)

        @pl.when(first)
        def _():
            c2.wait()

        dh = jnp.zeros((TM, D), f32)
        for s in range(NCHIP):
            dh = dh + _dot_nt(du_ref[:, D * s:D * (s + 1)], wu[s])
        xn, r = _rms(x_ref[...])
        g = g_ref[...]
        n = xn * g
        dn = dh * (1.0 + mod_ref[4:5, :])
        gx1_ref[...] = gx_ref[...] + _rms_bwd(dn * g, xn, r)
        accb_ref[0:1, :] += _colsum(dh * n)
        accb_ref[1:2, :] += _colsum(dh)
        accg_ref[0:1, :] += _colsum(dn * xn)

    anyspec = pl.BlockSpec(memory_space=pl.ANY)
    return pl.pallas_call(
        body, name="mlp_bwd", grid=(BL, NJ),
        in_specs=[_tok(D), _tok(DFF), anyspec, anyspec, _tok(D), _tok(D), MOD_SPEC, _full((1, D))],
        out_specs=[_tok(DFF), _tok(D), ACCB_SPEC, ACCG_SPEC],
        out_shape=[jax.ShapeDtypeStruct((BL, SEQ, DFF), bf16), jax.ShapeDtypeStruct((BL, SEQ, D), f32)] + ACC_SHAPES,
        scratch_shapes=[pltpu.VMEM((DFF, D), bf16), pltpu.VMEM((NCHIP, D, D), bf16), pltpu.SemaphoreType.DMA((2,))],
        compiler_params=_cp(("arbitrary", "arbitrary")),
    )(dy2, u, w_down, w_up, x1, gx, mod, g_pre)


def _matmul_tn(a, b, *, tn, col_blocked, name, out_dtype=f32):
    t, m = a.shape
    n = b.shape[1]
    tmm = min(m, 1024)
    tk = 2048 if tn <= 1024 else 1024
    nk = t // tk

    def body(a_ref, b_ref, o_ref, acc):
        k = pl.program_id(2)

        @pl.when(k == 0)
        def _():
            acc[...] = jnp.zeros_like(acc)

        acc[...] += _dot_tn(a_ref[...], b_ref[...])

        @pl.when(k == nk - 1)
        def _():
            o_ref[...] = acc[...].astype(out_dtype)

    if col_blocked:
        out_spec = pl.BlockSpec((None, tmm, tn), lambda i, j, k: (j, i, 0))
        out_shape = jax.ShapeDtypeStruct((n // tn, m, tn), out_dtype)
    else:
        out_spec = pl.BlockSpec((tmm, tn), lambda i, j, k: (i, j))
        out_shape = jax.ShapeDtypeStruct((m, n), out_dtype)
    return pl.pallas_call(
        body, name=name, grid=(m // tmm, n // tn, nk),
        in_specs=[pl.BlockSpec((tk, tmm), lambda i, j, k: (k, i)), pl.BlockSpec((tk, tn), lambda i, j, k: (k, j))],
        out_specs=out_spec, out_shape=out_shape, scratch_shapes=[pltpu.VMEM((tmm, tn), f32)],
        compiler_params=_cp(("arbitrary", "arbitrary", "arbitrary")),
    )(a, b)


def _grad_w_in(h, dproj):
    t = h.shape[0]
    tk = 1024
    nk = t // tk
    sw = INW // NCHIP

    def body(a_ref, b_ref, o_ref, acc):
        k = pl.program_id(0)

        @pl.when(k == 0)
        def _():
            acc[...] = jnp.zeros_like(acc)

        acc[...] += _dot_tn(a_ref[...], b_ref[...])

        @pl.when(k == nk - 1)
        def _():
            for s in range(NCHIP):
                o_ref[s] = acc[:, sw * s:sw * (s + 1)].astype(bf16)

    return pl.pallas_call(
        body, name="grad_w_in", grid=(nk,),
        in_specs=[pl.BlockSpec((tk, D), lambda k: (k, 0)), pl.BlockSpec((tk, INW), lambda k: (k, 0))],
        out_specs=pl.BlockSpec((NCHIP, D, sw), lambda k: (0, 0, 0)), out_shape=jax.ShapeDtypeStruct((NCHIP, D, sw), bf16),
        scratch_shapes=[pltpu.VMEM((D, INW), f32)], compiler_params=_cp(("arbitrary",)),
    )(h, dproj)


def _attn_out_bwd(gx1, y, mod, g_post, w_out, oa, ob, g_mix_a, g_mix_b, l1, l4, l16):
    def body(gx_ref, y_ref, mod_ref, gp_ref, w_ref, oa_ref, ob_ref, ga_ref, gb_ref, l1_ref, l4_ref, l16_ref, e_ref, g_ref,
             dy_ref, doa_ref, do1_ref, do4_ref, do16_ref, da_ref, d1_ref, d4_ref, d16_ref, accb_ref, accg_ref, scr):
        _acc_init(accb_ref, accg_ref)
        w1, w4, w16 = _branch_weights(l1_ref, l4_ref, l16_ref, scr)
        e, hs = e_ref[...], g_ref[...]
        gx1v = gx_ref[...]
        yn, ry = _rms(y_ref[...])
        gp = gp_ref[...]
        gt = mod_ref[2:3, :]
        dn1 = gx1v * gt
        dy = _rms_bwd(dn1 * gp, yn, ry).astype(bf16)
        dy_ref[...] = dy
        dmixed = _dot_nt(dy, w_ref[...])
        dma, dmb = dmixed[:, :AQ], dmixed[:, AQ:]
        oa, ob = oa_ref[...], ob_ref[...]
        oan, ra = _rms(oa)
        obn, rb = _rms(ob)
        doa = _rms_bwd(dma * ga_ref[...], oan, ra)
        doa_ref[...] = doa.astype(bf16)
        da_ref[...] = _lanes_to_heads(doa * oa, hs)
        dob = _rms_bwd(dmb * gb_ref[...], obn, rb)
        dd = _lanes_to_heads(dob * ob, hs)
        x1w, x4w = _heads_to_lanes(w1, e), _heads_to_lanes(w4, e)
        do1_ref[...] = (x1w * dob).astype(bf16)
        d1_ref[...] = w1 * dd
        _perm_store(x4w * dob, scr, do4_ref, 4)
        _perm_store(w4 * dd, scr, d4_ref, 4)
        _perm_store((1.0 - x1w - x4w) * dob, scr, do16_ref, 16)
        _perm_store(w16 * dd, scr, d16_ref, 16)
        accb_ref[0:1, :] += _colsum(gx1v * (yn * gp))
        accg_ref[0:1, :] += _colsum(dn1 * yn)
        accg_ref[1:2, :] += jnp.concatenate([_colsum(dma * oan), _colsum(dmb * obn)], axis=1)

    nat = lambda w, dt: jax.ShapeDtypeStruct((BL, SEQ, w), dt)
    return pl.pallas_call(
        body, name="attn_out_bwd", grid=(BL, NJ),
        in_specs=[_tok(D), _tok(D), MOD_SPEC, _full((1, D)), _full((D, D)), _tok(AQ), _tok(BW), _full((1, AQ)), _full((1, BW)),
                  _tok(LANES), _perm_spec(4, LANES), _perm_spec(16, LANES), _full((LANES, BW)), _full((BW, LANES))],
        out_specs=[_tok(D), _tok(AQ), _tok(BW), _perm_spec(4, BW), _perm_spec(16, BW),
                   _tok(LANES), _tok(LANES), _perm_spec(4, LANES), _perm_spec(16, LANES), ACCB_SPEC, ACCG_SPEC],
        out_shape=[nat(D, bf16), nat(AQ, bf16), nat(BW, bf16), jax.ShapeDtypeStruct((BL, 4, SEQ // 4, BW), bf16),
                   jax.ShapeDtypeStruct((BL, 16, SEQ // 16, BW), bf16), nat(LANES, f32), nat(LANES, f32),
                   jax.ShapeDtypeStruct((BL, 4, SEQ // 4, LANES), f32), jax.ShapeDtypeStruct((BL, 16, SEQ // 16, LANES), f32)]
                  + ACC_SHAPES,
        scratch_shapes=[pltpu.VMEM((BW // LANES, TM, LANES), f32)],
        compiler_params=_cp(("arbitrary", "arbitrary")),
    )(gx1, y, mod, g_post, w_out, oa, ob, g_mix_a, g_mix_b, l1, l4, l16, jnp.asarray(HEAD_EXPAND, bf16),
      jnp.asarray(HEAD_SUM, bf16))


def _attn_in_bwd(dqa, dka, dva, d1, d4, d16, tc, ts1, ts2, w_in, x, gx1, mod, g_pre):
    def body(dqa_ref, dka_ref, dva_ref, dq1_ref, dk1_ref, dv1_ref, dq4_ref, dk4_ref, dv4_ref, dq16_ref, dk16_ref, dv16_ref,
             c_ref, s1_ref, s2_ref, w_ref, x_ref, gx_ref, mod_ref, g_ref, dproj_ref, dx_ref, accb_ref, accg_ref, scr):
        _acc_init(accb_ref, accg_ref)
        c, s1, s2 = c_ref[...], s1_ref[...], s2_ref[...]
        tot = lambda r1, r4, r16: r1[...] + _perm_load(r4, scr, 4) + _perm_load(r16, scr, 16)
        dqb = tot(dq1_ref, dq4_ref, dq16_ref)
        dkb = tot(dk1_ref, dk4_ref, dk16_ref)
        dvb = tot(dv1_ref, dv4_ref, dv16_ref)
        dproj = jnp.concatenate([
            _rope_t(dqa_ref[...], c, s1, s2) * QSCALE, _rope_t(_per_kv_head(dka_ref[...]), c, s1, s2),
            _per_kv_head(dva_ref[...]),
            _rope_t(dqb, c, s1, s2) * QSCALE, _rope_t(dkb, c, s1, s2), dvb], axis=1).astype(bf16)
        dproj_ref[...] = dproj
        dh = _dot_nt(dproj, w_ref[...])
        xn, r = _rms(x_ref[...])
        g = g_ref[...]
        dn = dh * (1.0 + mod_ref[1:2, :])
        dx_ref[...] = gx_ref[...] + _rms_bwd(dn * g, xn, r)
        accb_ref[0:1, :] += _colsum(dh * (xn * g))
        accb_ref[1:2, :] += _colsum(dh)
        accg_ref[0:1, :] += _colsum(dn * xn)

    return pl.pallas_call(
        body, name="attn_in_bwd", grid=(BL, NJ),
        in_specs=[_tok(AQ), _tok(AQ), _tok(AQ)] + [_tok(BW)] * 3 + [_perm_spec(4, BW)] * 3 + [_perm_spec(16, BW)] * 3
                 + [_tok(LANES)] * 3 + [_full((D, INW)), _tok(D), _tok(D), MOD_SPEC, _full((1, D))],
        out_specs=[_tok(INW), _tok(D), ACCB_SPEC, ACCG_SPEC],
        out_shape=[jax.ShapeDtypeStruct((BL, SEQ, INW), bf16), jax.ShapeDtypeStruct((BL, SEQ, D), f32)] + ACC_SHAPES,
        scratch_shapes=[pltpu.VMEM((BW // LANES, TM, LANES), f32)],
        compiler_params=_cp(("arbitrary", "arbitrary")),
    )(dqa, dka, dva, *d1, *d4, *d16, tc, ts1, ts2, w_in, x, gx1, mod, g_pre)


def _inv_lane():
    inv = np.float32(THETA) ** (-np.arange(0, ROT, 2, dtype=np.float32) / np.float32(ROT))
    lane = np.arange(LANES) % HD
    return jnp.asarray(np.where(lane < ROT, inv[lane % (ROT // 2)], 0.0).astype(np.float32)[None, :])


def _local_step(x, tabs, mod, target, w_in, later_weights, grad_ready, g_attn_pre,
                g_attn_post, sink_a, g_mix_a, g_mix_b, g_mlp_pre, g_mlp_post):
    tc, ts1, ts2 = [t.reshape(BL, SEQ, LANES) for t in tabs]

    (h, qa, ka, va, q1, k1, v1, q4, k4, v4, q16, k16, v16, w_in) = _attn_in(x, mod, g_attn_pre, w_in, tc, ts1, ts2)
    seqs = lambda t: t.reshape(t.shape[0] * t.shape[1], t.shape[2], t.shape[3])
    q4, k4, v4, q16, k16, v16 = [seqs(t) for t in (q4, k4, v4, q16, k16, v16)]
    oa, la = _attn_fwd(qa, ka, va, sink_a, max_dist=BLK - 1, o_dtype=f32, name="attn_a_fwd")
    o1, l1 = _attn_fwd(q1, k1, v1, None, max_dist=BLK, o_dtype=bf16, name="attn_b1_fwd")
    o4, l4 = _attn_fwd(q4, k4, v4, None, max_dist=BLK, o_dtype=bf16, name="attn_b4_fwd")
    out_weight, mlp_weights, mod = later_weights((oa, o1, o4), mod)
    q16, mod = lax.optimization_barrier((q16, mod))
    o16, l16 = _attn_fwd(q16, k16, v16, None, max_dist=BLK, o_dtype=bf16, name="attn_b16_fwd")
    w_out = out_weight((o16,))
    b4 = lambda t: t.reshape(BL, 4, SEQ // 4, t.shape[-1])
    b16 = lambda t: t.reshape(BL, 16, SEQ // 16, t.shape[-1])
    x1, y, mixed, ob = _mix_out(oa, o1, l1, b4(o4), b4(l4), b16(o16), b16(l16), g_mix_a, g_mix_b, w_out, x, mod, g_attn_post)
    w_up, w_down = mlp_weights((x1,))
    h2, u, a = _mlp_up(x1, mod, g_mlp_pre, w_up)
    gx, dy2, accb_d, accg_d = _mlp_down(a, w_down, x1, target, mod, g_mlp_post)

    flat = lambda t: t.reshape(BL * SEQ, t.shape[-1])
    mod = grad_ready("w_down", _matmul_tn(flat(a), flat(dy2), tn=D, col_blocked=False, name="grad_w_down", out_dtype=bf16), mod)
    du, gx1, accb_m, accg_m = _mlp_bwd(dy2, u, w_down, w_up, x1, gx, mod, g_mlp_pre)
    mod = grad_ready("w_up", _matmul_tn(flat(h2), flat(du), tn=D, col_blocked=True, name="grad_w_up", out_dtype=bf16), mod)

    dy, doa, do1, do4, do16, da, dl1, dl4, dl16, accb_o, accg_o = _attn_out_bwd(
        gx1, y, mod, g_attn_post, w_out, oa, ob, g_mix_a, g_mix_b, l1, b4(l4), b16(l16))
    sink_behind = grad_ready("w_out", _matmul_tn(flat(mixed), flat(dy), tn=D, col_blocked=False, name="grad_w_out",
                                                  out_dtype=bf16), sink_a)
    dqa, dka, dva, dsink = _attn_bwd(qa, ka, va, doa, da, la, sink_behind, max_dist=BLK - 1, name="attn_a_bwd")
    d1 = _attn_bwd(q1, k1, v1, do1, dl1, l1, None, max_dist=BLK, name="attn_b1_bwd")
    d4 = _attn_bwd(q4, k4, v4, seqs(do4), seqs(dl4), l4, None, max_dist=BLK, name="attn_b4_bwd")
    d16 = _attn_bwd(q16, k16, v16, seqs(do16), seqs(dl16), l16, None, max_dist=BLK, name="attn_b16_bwd")
    dproj, grad_x, accb_i, accg_i = _attn_in_bwd(dqa, dka, dva, d1, [b4(t) for t in d4], [b16(t) for t in d16],
                                                 tc, ts1, ts2, w_in, x, gx1, mod, g_attn_pre)
    gw_in = _grad_w_in(flat(h), flat(dproj))
    dsink = grad_ready("w_in", gw_in, dsink)

    return grad_x, (accb_i, accb_o, accb_m, accb_d, accg_i, accg_o, accg_m, accg_d, dsink)


ADAW = NMOD * D // NCHIP


def _pos():
    return lax.axis_index("x"), lax.axis_index("y"), lax.axis_index("c")


def _flip(v, bit):
    return 1 - v if bit else v


def _all_peers(x, y, c):
    return [(_flip(x, k >> 2 & 1), _flip(y, k >> 1 & 1), _flip(c, k & 1)) for k in range(1, NDEV)]


def _other_chips(x, y):
    return [(1 - x, y), (x, 1 - y), (1 - x, 1 - y)]


def _rcopy(src, dst, send, recv, k, dev, k_recv=None):
    return pltpu.make_async_remote_copy(src_ref=src, dst_ref=dst, send_sem=send.at[k],
                                        recv_sem=recv.at[k if k_recv is None else k_recv],
                                        device_id=dev, device_id_type=MESH)


def _small_copies(src, land, send, recv):
    x, y, c = _pos()
    me = 4 * x + 2 * y + c
    return [(_rcopy(src, land.at[me], send, recv, k, p), _rcopy(src, land.at[4 * p[0] + 2 * p[1] + p[2]], send, recv, k, p))
            for k, p in enumerate(_all_peers(x, y, c))]


def _ada_fwd(c_in, landed, w_ada, b_cols):
    def body(c_ref, land, w_hbm, b_ref, mod_ref, cond_ref, mbuf, w_ref, s2, r2, wsem):
        x, y, c = _pos()
        chip = 2 * x + y
        me = 4 * x + 2 * y + c
        wcopy = pltpu.make_async_copy(w_hbm, w_ref, wsem)
        wcopy.start()
        for i in range(NDEV):
            @pl.when(me == i)
            def _():
                cond_ref[BL * i:BL * (i + 1), :] = c_ref[...]

            @pl.when(me != i)
            def _():
                cond_ref[BL * i:BL * (i + 1), :] = land[i]
        call = cond_ref[...]
        cond = call / (1.0 + jnp.exp(-call))
        cond_ref[...] = cond
        wcopy.wait()
        mbuf[chip] = _dot(cond.astype(bf16), w_ref[...].astype(bf16)) + b_ref[...]
        chips = _other_chips(x, y)
        sends = [_rcopy(mbuf.at[chip], mbuf.at[chip], s2, r2, j, (px, py, c)) for j, (px, py) in enumerate(chips)]
        for cp in sends:
            cp.start()
        for j, (px, py) in enumerate(chips):
            _rcopy(mbuf.at[chip], mbuf.at[2 * px + py], s2, r2, j, (px, py, c)).wait_recv()
        for cp in sends:
            cp.wait_send()
        row = lax.broadcasted_iota(jnp.int32, (BL * NDEV, ADAW), 0)
        for s in range(NCHIP):
            slab = mbuf[s]
            for j in range(BL):
                mod_ref[j:j + 1, ADAW * s:ADAW * (s + 1)] = jnp.sum(jnp.where(row == BL * me + j, slab, 0.0), axis=0, keepdims=True)

    vm = pl.BlockSpec(memory_space=pltpu.VMEM)
    return pl.pallas_call(
        body, name="ada_fwd", in_specs=[vm, vm, pl.BlockSpec(memory_space=pl.ANY), vm], out_specs=[vm, vm],
        out_shape=[jax.ShapeDtypeStruct((BL, NMOD * D), f32), jax.ShapeDtypeStruct((BL * NDEV, D), f32)],
        scratch_shapes=[pltpu.VMEM((NCHIP, BL * NDEV, ADAW), f32), pltpu.VMEM((D, ADAW), f32),
                        pltpu.SemaphoreType.DMA((NCHIP - 1,)), pltpu.SemaphoreType.DMA((NCHIP - 1,)),
                        pltpu.SemaphoreType.DMA],
        compiler_params=pltpu.CompilerParams(vmem_limit_bytes=VMEM_LIMIT),
    )(c_in, landed, w_ada, b_cols)


PAY_ROWS = 4


def _small_pack(accs):
    def body(bi, bo, bm, bd, gi, go, gm, gd, dsink, pay):
        pay[...] = jnp.zeros_like(pay)
        for b in range(BL):
            for k, (ref, r) in enumerate(((bi, 1), (bi, 0), (bo, 0), (bm, 1), (bm, 0), (bd, 0))):
                pay[b:b + 1, D * k:D * (k + 1)] = ref[b, r:r + 1, :]
        for off, ref, r in ((OFF_G_ATTN_PRE, gi, 0), (OFF_G_ATTN_POST, go, 0), (OFF_G_MIX_A, go, 1), (OFF_G_MLP_PRE, gm, 0),
                            (OFF_G_MLP_POST, gd, 0)):
            pay[BL:BL + 1, off:off + D] = ref[r:r + 1, :]
        eye = lax.broadcasted_iota(jnp.int32, (NHEAD, LANES), 0) == lax.broadcasted_iota(jnp.int32, (NHEAD, LANES), 1)
        pay[BL:BL + 1, OFF_SINK:OFF_SINK + LANES] = jnp.sum(jnp.where(eye, dsink[...], 0.0), axis=0, keepdims=True)
        pay[BL:BL + 1, OFF_LOSS:OFF_LOSS + LANES] = gd[1:2, 0:LANES]

    vm = pl.BlockSpec(memory_space=pltpu.VMEM)
    return pl.pallas_call(body, name="small_pack", in_specs=[vm] * 9, out_specs=vm,
                          out_shape=jax.ShapeDtypeStruct((PAY_ROWS, PAYW), f32))(*accs)


def _small_sum(own, landed, cond_all):
    def body(pay, land, cond_ref, gw_ref, gb_ref, small_ref, pbuf, dall):
        x, y, c = _pos()
        chip = 2 * x + y
        me = 4 * x + 2 * y + c
        for i in range(NDEV):
            @pl.when(me == i)
            def _():
                pbuf[i] = pay[...]

            @pl.when(me != i)
            def _():
                pbuf[i] = land[i]
        small = pbuf[0, BL:BL + 1, :]
        for i in range(1, NDEV):
            small = small + pbuf[i, BL:BL + 1, :]
        small_ref[...] = small
        for i in range(NDEV):
            dall[BL * i:BL * (i + 1), :] = pbuf[i, 0:BL, :]
        gb_ref[...] = jnp.sum(dall[...], axis=0, keepdims=True)
        cols = jnp.zeros((BL * NDEV, ADAW), f32)
        for s in range(NCHIP):
            cols = cols + jnp.where(chip == s, dall[:, ADAW * s:ADAW * (s + 1)], 0.0)
        gw_ref[...] = _dot_tn(cond_ref[...].astype(bf16), cols.astype(bf16))

    vm = pl.BlockSpec(memory_space=pltpu.VMEM)
    return pl.pallas_call(
        body, name="small_sum", in_specs=[vm] * 3, out_specs=[vm] * 3,
        out_shape=[jax.ShapeDtypeStruct((D, ADAW), f32), jax.ShapeDtypeStruct((1, PAYW), f32), jax.ShapeDtypeStruct((1, PAYW), f32)],
        scratch_shapes=[pltpu.VMEM((NDEV, PAY_ROWS, PAYW), f32), pltpu.VMEM((BL * NDEV, PAYW), f32)],
        compiler_params=pltpu.CompilerParams(vmem_limit_bytes=VMEM_LIMIT),
    )(own, landed, cond_all)


def _half(ref, c):
    r2 = ref.shape[0] // 2
    return ref.at[pl.ds(c * r2 if isinstance(c, int) else pl.multiple_of(c * r2, 16), r2), :]


HBM_SPEC = pl.BlockSpec(memory_space=pltpu.HBM)
SEM_SPEC = pl.BlockSpec(memory_space=pltpu.SEMAPHORE)
EFFECT = pltpu.SideEffectType.DATAFLOW_SIDE_EFFECTING
NLINK = NCHIP - 1


def _in_hbm(a):
    return pltpu.with_memory_space_constraint(a, pltpu.HBM)


NSEM = 8


def _split_start(name, srcs, land_shapes, builds, carry, after=(), lands=None):
    n = len(srcs)
    na, nc = len(after), len(carry)

    def body(*refs):
        src, land = refs[:n], refs[n:2 * n]
        kept = refs[2 * n + na:2 * n + na + nc]
        outs = refs[2 * n + na + nc:]
        send, recv, passed = outs[:n], outs[n:2 * n], outs[4 * n:]
        for t in range(n):
            for out_cp, _ in builds[t](src[t], land[t], send[t], recv[t]):
                out_cp.start()
        for a, b in zip(kept, passed):
            b[...] = a[...]

    if lands is None:
        lands = [lax.empty(s.shape, s.dtype) for s in land_shapes]
    lands = [_in_hbm(a) for a in lands]
    sems = [pltpu.SemaphoreType.DMA((NSEM,))] * (2 * n)
    thru = [pltpu.HBM(a.shape, a.dtype) for a in list(srcs) + lands]
    vm = pl.BlockSpec(memory_space=pltpu.VMEM)
    res = pl.pallas_call(
        body, name=name, out_shape=sems + thru + [jax.ShapeDtypeStruct(a.shape, a.dtype) for a in carry],
        in_specs=[HBM_SPEC] * (2 * n) + [pl.BlockSpec(memory_space=pl.ANY)] * na + [vm] * nc,
        out_specs=[SEM_SPEC] * (2 * n) + [HBM_SPEC] * (2 * n) + [vm] * nc,
        input_output_aliases={i: 2 * n + i for i in range(2 * n)},
        compiler_params=pltpu.CompilerParams(has_side_effects=EFFECT),
    )(*[_in_hbm(a) for a in srcs], *lands, *after, *carry)
    flight = [(res[2 * n + t], res[3 * n + t], res[t], res[n + t]) for t in range(n)]
    return flight, list(res[4 * n:])


def _split_wait(name, flight, builds, after):
    m = len(flight)
    na = len(after)

    def body(*refs):
        src, land, send, recv = refs[:m], refs[m:2 * m], refs[2 * m:3 * m], refs[3 * m:4 * m]
        for t in range(m):
            for out_cp, in_cp in builds[t](src[t], land[t], send[t], recv[t]):
                out_cp.wait_send()
                in_cp.wait_recv()

    ops = [f[0] for f in flight] + [f[1] for f in flight] + [f[2] for f in flight] + [f[3] for f in flight]
    res = pl.pallas_call(
        body, name=name, out_shape=[pltpu.HBM(a.shape, a.dtype) for a in ops[:2 * m]],
        in_specs=[HBM_SPEC] * (2 * m) + [SEM_SPEC] * (2 * m) + [pl.BlockSpec(memory_space=pl.ANY)] * na,
        out_specs=[HBM_SPEC] * (2 * m), input_output_aliases={i: i for i in range(2 * m)},
        compiler_params=pltpu.CompilerParams(has_side_effects=EFFECT),
    )(*ops, *after)
    return res[:m], res[m:2 * m]


def _weight_copies(src, land, send, recv):
    x, y, c = _pos()
    chip = 2 * x + y
    return [(_rcopy(_half(src, c), _half(land.at[chip], c), send, recv, j, (px, py, c)),
             _rcopy(_half(src, c), _half(land.at[2 * px + py], c), send, recv, j, (px, py, c)))
            for j, (px, py) in enumerate(_other_chips(x, y))]


NDIRECT = NDEV - 1


def _direct_grad_copies(src, land, send, recv):
    x, y, c = _pos()
    out, arrive = [], []
    for j, (px, py) in enumerate(_other_chips(x, y)):
        for hc in range(2):
            out.append(_rcopy(_half(src.at[2 * px + py], hc), land.at[2 * j + c], send, recv, 2 * j + hc, (px, py, hc),
                              k_recv=2 * j + c))
            arrive.append(_rcopy(_half(src.at[2 * px + py], hc), land.at[2 * j + hc], send, recv, 2 * j + hc, (px, py, hc)))
    own = _rcopy(_half(src.at[2 * x + y], 1 - c), land.at[NDIRECT - 1], send, recv, NDIRECT - 1, (x, y, 1 - c))
    return list(zip(out, arrive)) + [(own, own)]


def _pair_weight_copies(src, land, send, recv):
    x, y, c = _pos()
    sib = (x, y, 1 - c)
    cps = []
    for j, (px, py) in enumerate(_other_chips(x, y)):
        mine, theirs = _half(land.at[2 * px + py], c), _half(land.at[2 * px + py], 1 - c)
        cps.append((_rcopy(mine, mine, send, recv, j, sib), _rcopy(theirs, theirs, send, recv, j, sib)))
    own = _rcopy(src, land.at[2 * x + y], send, recv, NLINK, sib)
    return cps + [(own, own)]


RS_ROWS = 256


def _chip_add(own, landed, pos_arr, name):
    nl, r2, cw = landed.shape
    rows = min(RS_ROWS, r2)
    nr = r2 // rows

    def body(s_ref, h_ref, q_ref, o_ref):
        acc = h_ref[...].astype(f32)
        for j in range(nl):
            acc = acc + q_ref[j].astype(f32)
        o_ref[...] = acc

    gs = pltpu.PrefetchScalarGridSpec(
        num_scalar_prefetch=1, grid=(nr,),
        in_specs=[pl.BlockSpec((None, rows, cw), lambda j, s: (s[0], s[1] * nr + j, 0)),
                  pl.BlockSpec((nl, rows, cw), lambda j, s: (0, j, 0))],
        out_specs=pl.BlockSpec((rows, cw), lambda j, s: (s[1] * nr + j, 0)))
    return pl.pallas_call(body, name=name, grid_spec=gs, out_shape=jax.ShapeDtypeStruct((2 * r2, cw), f32),
                          compiler_params=_cp(("arbitrary",)))(pos_arr, own, landed)


def _pair_gather_copies(src, land, send, recv):
    x, y, c = _pos()
    sib = (x, y, 1 - c)
    return [(_rcopy(_half(land, c), _half(land, c), send, recv, 0, sib),
             _rcopy(_half(land, 1 - c), _half(land, 1 - c), send, recv, 0, sib))]


def _adamw_math(w, g, m, v):
    m = B1 * m + (1.0 - B1) * g
    v = B2 * v + (1.0 - B2) * jnp.square(g)
    m_hat = m / (1.0 - B1 ** STEP)
    v_hat = v / (1.0 - B2 ** STEP)
    return -LR * (m_hat / (jnp.sqrt(v_hat) + AEPS) + WD * w), m, v


ADAM_BLOCK = 512 * 1024


def _adamw(w, g, m, v, name, after=(), landed=True):
    r, cw = w.shape
    na = len(after)

    def body(w_ref, g_ref, m_ref, v_ref, *rest):
        outs = rest[na:]
        g = g_ref[...]
        if landed:
            outs[0][...] = g
        outs[-3][...], outs[-2][...], outs[-1][...] = _adamw_math(w_ref[...], g, m_ref[...], v_ref[...])

    rows = max(k for k in range(SUBLANES, ADAM_BLOCK // cw + 1, SUBLANES) if r % k == 0)
    spec = pl.BlockSpec((rows, cw), lambda i: (i, 0))
    nout = 4 if landed else 3
    res = pl.pallas_call(body, name=name, grid=(r // rows,), in_specs=[spec] * 4 + [pl.BlockSpec(memory_space=pl.ANY)] * na,
                         out_specs=[spec] * nout, out_shape=[jax.ShapeDtypeStruct((r, cw), f32)] * nout,
                         compiler_params=_cp(("arbitrary",)))(w, g, m, v, *after)
    return list(res) if landed else [g, *res]


SMALL = (("b_ada", None, PAYW), ("g_attn_pre", OFF_G_ATTN_PRE, D), ("g_attn_post", OFF_G_ATTN_POST, D), ("sink_a", OFF_SINK, 8),
         ("g_mix_a", OFF_G_MIX_A, AQ), ("g_mix_b", OFF_G_MIX_B, BW), ("g_mlp_pre", OFF_G_MLP_PRE, D), ("g_mlp_post", OFF_G_MLP_POST, D))


def _adamw_small(small, gb, params):
    n = len(SMALL)

    def body(*refs):
        small_ref, gb_ref = refs[:2]
        wmv = refs[2:2 + 3 * n]
        loss_ref = refs[2 + 3 * n]
        outs = refs[3 + 3 * n:]
        loss_ref[...] = small_ref[:, OFF_LOSS:OFF_LOSS + 1] * (0.5 / D)
        for i, (_, off, width) in enumerate(SMALL):
            g = gb_ref[...] if off is None else small_ref[:, off:off + width]
            w_ref, m_ref, v_ref = wmv[3 * i:3 * i + 3]
            outs[4 * i][...] = g
            outs[4 * i + 1][...], outs[4 * i + 2][...], outs[4 * i + 3][...] = _adamw_math(w_ref[...], g, m_ref[...], v_ref[...])

    vm = pl.BlockSpec(memory_space=pltpu.VMEM)
    out_shape = [jax.ShapeDtypeStruct((1, 1), f32)]
    for _, _, width in SMALL:
        out_shape += [jax.ShapeDtypeStruct((1, width), f32)] * 4
    flat = [a for wmv in params for a in wmv]
    res = pl.pallas_call(body, name="adamw_small", in_specs=[vm] * (2 + 3 * n), out_specs=[vm] * len(out_shape),
                         out_shape=out_shape)(small, gb, *flat)
    return res[0], {name: res[1 + 4 * i:5 + 4 * i] for i, (name, _, _) in enumerate(SMALL)}


def kernel(x, c, positions, w_ada, b_ada, g_attn_pre, g_attn_post, w_in, sink_a, g_mix_a, g_mix_b, w_out, g_mlp_pre, g_mlp_post, w_up, w_down, loss_target, m_w_ada, m_b_ada, m_g_attn_pre, m_g_attn_post, m_w_in, m_sink_a, m_g_mix_a, m_g_mix_b, m_w_out, m_g_mlp_pre, m_g_mlp_post, m_w_up, m_w_down, v_w_ada, v_b_ada, v_g_attn_pre, v_g_attn_post, v_w_in, v_sink_a, v_g_mix_a, v_g_mix_b, v_w_out, v_g_mlp_pre, v_g_mlp_post, v_w_up, v_w_down):
    given = dict(w_ada=w_ada, b_ada=b_ada, g_attn_pre=g_attn_pre, g_attn_post=g_attn_post, w_in=w_in, sink_a=sink_a, g_mix_a=g_mix_a,
                 g_mix_b=g_mix_b, w_out=w_out, g_mlp_pre=g_mlp_pre, g_mlp_post=g_mlp_post, w_up=w_up, w_down=w_down)
    moms = dict(w_ada=(m_w_ada, v_w_ada), b_ada=(m_b_ada, v_b_ada), g_attn_pre=(m_g_attn_pre, v_g_attn_pre),
                g_attn_post=(m_g_attn_post, v_g_attn_post), w_in=(m_w_in, v_w_in), sink_a=(m_sink_a, v_sink_a),
                g_mix_a=(m_g_mix_a, v_g_mix_a), g_mix_b=(m_g_mix_b, v_g_mix_b), w_out=(m_w_out, v_w_out),
                g_mlp_pre=(m_g_mlp_pre, v_g_mlp_pre), g_mlp_post=(m_g_mlp_post, v_g_mlp_post), w_up=(m_w_up, v_w_up),
                w_down=(m_w_down, v_w_down))
    order = ["w_ada", "b_ada", "g_attn_pre", "g_attn_post", "w_in", "sink_a", "g_mix_a", "g_mix_b", "w_out", "g_mlp_pre",
             "g_mlp_post", "w_up", "w_down"]
    xi, yi, ci = _pos()
    chip = 2 * xi + yi

    pos_arr = jnp.stack([chip, ci]).astype(jnp.int32)
    big = ("w_in", "w_out", "w_up", "w_down")

    gathered = [jax.ShapeDtypeStruct((NCHIP,) + given[n].shape[1:], bf16) for n in big]
    (flight_c, *flight_in), (inv_lane,) = _split_start(
        "weights_start_first", [c, w_in[0].astype(bf16)], [jax.ShapeDtypeStruct((NDEV, BL, D), f32), gathered[0]],
        [_small_copies, _weight_copies], [_inv_lane()])
    inv_lane, rest, positions, b_all = lax.optimization_barrier((inv_lane, [given[n][0] for n in big[1:]], positions, b_ada))
    tabs = _rope_tables(positions.reshape(BL * SEQ, 1), inv_lane)
    rest = [w.astype(bf16) for w in rest]
    b_cols = lax.dynamic_slice(b_all, (0, chip * ADAW), (1, ADAW))
    (c_own,), (c_all,) = _split_wait("cond_wait", [flight_c], [_small_copies], (*tabs, *rest))
    mod, cond_all = _ada_fwd(c_own, c_all, w_ada[0], b_cols)

    srcs, lands = _split_wait("weights_wait_first", flight_in, [_weight_copies], (mod,))
    cross, (mod,) = _split_start("weights_pair_start_first", srcs, None, [_pair_weight_copies], [mod], lands=lands)
    flight_rest, (mod,) = _split_start("weights_start_rest", rest, gathered[1:], [_weight_copies] * 3, [mod])
    _, (win_g,) = _split_wait("weights_pair_wait_first", cross, [_pair_weight_copies], (mod,))
    mod = mod.reshape(BL, NMOD, D)

    def later_weights(after, carry):
        srcs, lands = _split_wait("weights_wait_rest", flight_rest, [_weight_copies] * 3, after)
        fl, (carry,) = _split_start("weights_pair_start_rest", srcs, None, [_pair_weight_copies] * 3, [carry], lands=lands)
        def out_weight(after):
            _, (wout_g,) = _split_wait("weights_pair_wait_out", fl[:1], [_pair_weight_copies], after)
            return wout_g.reshape(D, D)

        def mlp_weights(after):
            _, (wup_g, wdn_g) = _split_wait("weights_pair_wait_mlp", fl[1:], [_pair_weight_copies] * 2, after)
            return wup_g, wdn_g.reshape(DFF, D)

        return out_weight, mlp_weights, carry

    waiting, pending = {}, {}

    def send_grads(carry):
        names = list(waiting)
        slabs = [waiting.pop(n) for n in names]
        lands = [jax.ShapeDtypeStruct((NDIRECT, s.shape[1] // 2, s.shape[2]), bf16) for s in slabs]
        fl, (carry,) = _split_start("grad_start_" + names[-1], slabs, lands, [_direct_grad_copies] * len(names), [carry])
        for n, f in zip(names, fl):
            pending[n] = [f]
        return carry

    def grad_ready(name, g, carry):
        waiting[name] = g if g.ndim == 3 else g.reshape(NCHIP, g.shape[0] // NCHIP, g.shape[1])
        return send_grads(carry) if name == "w_out" else carry

    grad_x, accs = _local_step(x, tabs, mod, loss_target, win_g, later_weights, grad_ready,
                               g_attn_pre, g_attn_post, sink_a, g_mix_a, g_mix_b, g_mlp_pre, g_mlp_post)

    grads, out = {}, {}

    def update(n, after=()):
        tr = (lambda a: a.T) if n == "w_in" else (lambda a: a)
        res = _adamw(tr(given[n][0]), tr(grads[n]), tr(moms[n][0][0]), tr(moms[n][1][0]), "adamw_" + n, after,
                     landed=n != "w_ada")
        out[n] = tuple(tr(a)[None] for a in res)
        return res[3]

    def reduce(names, after):
        fl = sum((pending[n] for n in names), [])
        halves, landed = _split_wait("grad_wait_" + names[0], fl, [_direct_grad_copies] * len(names), after)
        fulls = [_chip_add(h, q, pos_arr, "grad_chip_sum_" + n) for h, q, n in zip(halves, landed, names)]
        tokens = [jnp.full((SUBLANES, LANES), float(i), f32) for i in range(len(names))]
        return _split_start("grad_gather_start_" + names[0], tokens, None, [_pair_gather_copies] * len(names), [],
                            lands=fulls)[0]

    def gathered_update(n, flight, after):
        _, (grads[n],) = _split_wait("grad_gather_wait_" + n, [flight], [_pair_gather_copies], after)
        return update(n)

    slab = waiting.pop("w_in")
    (fl_small, fl_in), (cond_all,) = _split_start(
        "small_start", [_small_pack(accs), slab],
        [jax.ShapeDtypeStruct((NDEV, PAY_ROWS, PAYW), f32), jax.ShapeDtypeStruct((NDIRECT, slab.shape[1] // 2, slab.shape[2]), bf16)],
        [_small_copies, _direct_grad_copies], [cond_all])
    pending["w_in"] = [fl_in]
    fl_down, fl_up, fl_out = reduce(("w_down", "w_up", "w_out"), (cond_all,))
    (pay,), (landed,) = _split_wait("small_wait", [fl_small], [_small_copies], (fl_out[0],))
    grads["w_ada"], gb, small = _small_sum(pay, landed, cond_all)
    last = update("w_ada")
    last = gathered_update("w_down", fl_down, (last,))
    last = gathered_update("w_up", fl_up, (last,))
    (fl_in,) = reduce(("w_in",), (last,))
    last = gathered_update("w_out", fl_out, (fl_in[0],))
    gathered_update("w_in", fl_in, (last,))
    loss, res = _adamw_small(small, gb, [(given[n], moms[n][0], moms[n][1]) for n, _, _ in SMALL])
    for n, _, _ in SMALL:
        out[n] = tuple(res[n])
    return (loss.reshape(()), grad_x, *[out[n][0] for n in order], *[out[n][1] for n in order],
            *[out[n][2] for n in order], *[out[n][3] for n in order])
```

```python
import numpy as np
import jax
import jax.numpy as jnp
from jax import lax
from jax.experimental import pallas as pl
from jax.experimental.pallas import tpu as pltpu

f32 = jnp.float32
bf16 = jnp.bfloat16
MESH = pl.DeviceIdType.MESH

D = 1024
SEQ = 2048
BL = 2
HD = 64
AQ = 512
AKV = 128
BW = 512
INW = 2304
DFF = 4096
NMOD = 6
ROT = 16
THETA = 500000.0
EPS = 1e-6
NEG = -1e30
BLK = 128
TM = 512
NJ = SEQ // TM
LANES = 128
SUBLANES = 8
NHEAD = AQ // HD
QSCALE = HD ** -0.5
NCHIP = 4
NDEV = 8
VMEM_LIMIT = 56 << 20

LR, B1, B2, AEPS, WD, STEP = 0.001, 0.9, 0.999, 1e-08, 0.01, 10

OFF_G_ATTN_PRE, OFF_G_ATTN_POST, OFF_G_MIX_A, OFF_G_MIX_B = 0, 1024, 2048, 2560
OFF_G_MLP_PRE, OFF_G_MLP_POST, OFF_SINK, OFF_LOSS = 3072, 4096, 5120, 5248
PAYW = NMOD * D


def _cp(sem=None):
    return pltpu.CompilerParams(dimension_semantics=sem, vmem_limit_bytes=VMEM_LIMIT)


def _dot(a, b):
    return jnp.dot(a, b, preferred_element_type=f32)


def _dot_nt(a, b):
    return lax.dot_general(a, b, (((1,), (1,)), ((), ())), preferred_element_type=f32)


def _dot_tn(a, b):
    return lax.dot_general(a, b, (((0,), (0,)), ((), ())), preferred_element_type=f32)


def _rms(x):
    r = lax.rsqrt(jnp.mean(x * x, axis=-1, keepdims=True) + EPS)
    return x * r, r


def _rms_bwd(dy, y, r):
    return r * (dy - y * jnp.mean(dy * y, axis=-1, keepdims=True))


def _colsum(v):
    return jnp.sum(v, axis=0, keepdims=True)


def _rope(p, c, s1, s2):
    outs = []
    for c0 in range(0, p.shape[1], LANES):
        pc = p[:, c0:c0 + LANES]
        outs.append(pc * c + pltpu.roll(pc, LANES - ROT // 2, 1) * s1 + pltpu.roll(pc, ROT // 2, 1) * s2)
    return outs[0] if len(outs) == 1 else jnp.concatenate(outs, axis=1)


def _rope_t(g, c, s1, s2):
    outs = []
    for c0 in range(0, g.shape[1], LANES):
        gc = g[:, c0:c0 + LANES]
        outs.append(gc * c + pltpu.roll(gc * s1, ROT // 2, 1) + pltpu.roll(gc * s2, LANES - ROT // 2, 1))
    return outs[0] if len(outs) == 1 else jnp.concatenate(outs, axis=1)


def _perm_store(val, scr, out_ref, d):
    nc = val.shape[1] // LANES
    for c in range(nc):
        scr[c] = val[:, LANES * c:LANES * (c + 1)]
    for c in range(nc):
        for r in range(d):
            out_ref[r, :, LANES * c:LANES * (c + 1)] = scr[c, pl.ds(r, TM // d, stride=d), :].astype(out_ref.dtype)


def _perm_load(in_ref, scr, d):
    nc = in_ref.shape[-1] // LANES
    for c in range(nc):
        for r in range(d):
            scr[c, pl.ds(r, TM // d, stride=d), :] = in_ref[r, :, LANES * c:LANES * (c + 1)].astype(f32)
    return jnp.concatenate([scr[c] for c in range(nc)], axis=1)


def _per_query_head(kv):
    r = pltpu.roll(kv, HD, 1)
    lo = lax.broadcasted_iota(jnp.int32, kv.shape, 1) < HD
    return jnp.concatenate([jnp.where(lo, kv, r), jnp.where(lo, r, kv)], axis=1)


def _per_kv_head(g):
    g0, g1 = g[:, :LANES] + g[:, LANES:2 * LANES], g[:, 2 * LANES:3 * LANES] + g[:, 3 * LANES:]
    lo = lax.broadcasted_iota(jnp.int32, g0.shape, 1) < HD
    return jnp.where(lo, g0 + pltpu.roll(g0, HD, 1), g1 + pltpu.roll(g1, HD, 1))


def _tok(w):
    return pl.BlockSpec((None, TM, w), lambda b, j: (b, j, 0))


def _perm_spec(d, w):
    return pl.BlockSpec((None, d, TM // d, w), lambda b, j: (b, 0, j, 0))


def _full(shape):
    n = len(shape)
    return pl.BlockSpec(shape, lambda b, j: (0,) * n)


MOD_SPEC = pl.BlockSpec((None, NMOD, D), lambda b, j: (b, 0, 0))
ACCB_SPEC = pl.BlockSpec((None, SUBLANES, D), lambda b, j: (b, 0, 0))
ACCG_SPEC = pl.BlockSpec((SUBLANES, D), lambda b, j: (0, 0))
ACC_SHAPES = [jax.ShapeDtypeStruct((BL, SUBLANES, D), f32), jax.ShapeDtypeStruct((SUBLANES, D), f32)]


def _acc_init(accb_ref, accg_ref):
    b, j = pl.program_id(0), pl.program_id(1)

    @pl.when(j == 0)
    def _():
        accb_ref[...] = jnp.zeros_like(accb_ref)

    @pl.when((b == 0) & (j == 0))
    def _():
        accg_ref[...] = jnp.zeros_like(accg_ref)


def _rope_tables(pos_col, inv_lane):
    def body(p_ref, inv_ref, c_ref, s1_ref, s2_ref):
        ang = p_ref[...].astype(f32) * inv_ref[...]
        j = lax.broadcasted_iota(jnp.int32, (TM, LANES), 1) % HD
        cs, sn = jnp.cos(ang), jnp.sin(ang)
        c_ref[...] = jnp.where(j < ROT, cs, 1.0)
        s1_ref[...] = jnp.where(j < ROT // 2, -sn, 0.0)
        s2_ref[...] = jnp.where((j >= ROT // 2) & (j < ROT), sn, 0.0)

    n = BL * SEQ // TM
    return pl.pallas_call(
        body, name="rope_tables", grid=(n,),
        in_specs=[pl.BlockSpec((TM, 1), lambda i: (i, 0)), pl.BlockSpec((1, LANES), lambda i: (0, 0))],
        out_specs=[pl.BlockSpec((TM, LANES), lambda i: (i, 0))] * 3,
        out_shape=[jax.ShapeDtypeStruct((BL * SEQ, LANES), f32)] * 3,
    )(pos_col, inv_lane)


def _attn_in(x, mod, g_pre, w_in, tc, ts1, ts2):
    def body(x_ref, mod_ref, g_ref, wg_ref, c_ref, s1_ref, s2_ref,
             h_ref, qa_ref, ka_ref, va_ref, q1_ref, k1_ref, v1_ref, q4_ref, k4_ref, v4_ref, q16_ref, k16_ref, v16_ref,
             w_ref, scr):
        @pl.when((pl.program_id(0) == 0) & (pl.program_id(1) == 0))
        def _():
            w_ref[...] = jnp.concatenate([wg_ref[s] for s in range(NCHIP)], axis=1)

        xn, _ = _rms(x_ref[...])
        h = (xn * g_ref[...]) * (1.0 + mod_ref[1:2, :]) + mod_ref[0:1, :]
        hb = h.astype(bf16)
        h_ref[...] = hb
        proj = _dot(hb, w_ref[...])
        c, s1, s2 = c_ref[...], s1_ref[...], s2_ref[...]
        o1, o2, o3, o4, o5 = AQ, AQ + AKV, AQ + 2 * AKV, AQ + 2 * AKV + BW, AQ + 2 * AKV + 2 * BW
        qa_ref[...] = (_rope(proj[:, :o1], c, s1, s2) * QSCALE).astype(bf16)
        ka_ref[...] = _per_query_head(_rope(proj[:, o1:o2], c, s1, s2)).astype(bf16)
        va_ref[...] = _per_query_head(proj[:, o2:o3]).astype(bf16)
        qb = _rope(proj[:, o3:o4], c, s1, s2) * QSCALE
        kb = _rope(proj[:, o4:o5], c, s1, s2)
        vb = proj[:, o5:]
        for val, r1, r4, r16 in ((qb, q1_ref, q4_ref, q16_ref), (kb, k1_ref, k4_ref, k16_ref), (vb, v1_ref, v4_ref, v16_ref)):
            r1[...] = val.astype(bf16)
            _perm_store(val, scr, r4, 4)
            _perm_store(val, scr, r16, 16)

    nat = lambda w: jax.ShapeDtypeStruct((BL, SEQ, w), bf16)
    p4 = jax.ShapeDtypeStruct((BL, 4, SEQ // 4, BW), bf16)
    p16 = jax.ShapeDtypeStruct((BL, 16, SEQ // 16, BW), bf16)
    return pl.pallas_call(
        body, name="attn_in", grid=(BL, NJ),
        in_specs=[_tok(D), MOD_SPEC, _full((1, D)), _full((NCHIP, D, INW // NCHIP)), _tok(LANES), _tok(LANES), _tok(LANES)],
        out_specs=([_tok(D), _tok(AQ), _tok(2 * AKV), _tok(2 * AKV)] + [_tok(BW)] * 3 + [_perm_spec(4, BW)] * 3 + [_perm_spec(16, BW)] * 3
                   + [_full((D, INW))]),
        out_shape=[nat(D), nat(AQ), nat(2 * AKV), nat(2 * AKV)] + [nat(BW)] * 3 + [p4] * 3 + [p16] * 3
                  + [jax.ShapeDtypeStruct((D, INW), bf16)],
        scratch_shapes=[pltpu.VMEM((BW // LANES, TM, LANES), f32)],
        compiler_params=_cp(("arbitrary", "arbitrary")),
    )(x, mod, g_pre, w_in, tc, ts1, ts2)


def _kv_cat(cur_ref, prev_ref, p, cache):
    key = (id(cur_ref), p)
    if key not in cache:
        sl = slice(LANES * p, LANES * (p + 1))
        cache[key] = cur_ref[:, sl] if prev_ref is None else jnp.concatenate([prev_ref[:, sl], cur_ref[:, sl]], axis=0)
    return cache[key]


def _lane_half(a, hh):
    lo = lax.broadcasted_iota(jnp.int32, a.shape, 1) < HD
    return jnp.where(lo, a, jnp.zeros_like(a)) if hh == 0 else jnp.where(lo, jnp.zeros_like(a), a)


ATT_UNITS = 4


def _att_units(nb):
    return ATT_UNITS if nb == 1 else min(ATT_UNITS, nb)


def _attn_specs(n, nb, descending):
    u = _att_units(nb)
    if nb == 1:
        return (lambda ww: pl.BlockSpec((u, BLK, ww), lambda a, i: (a, 0, 0))), None, (n // u, 1)
    steps = nb // u
    at = (lambda i: steps - 1 - i) if descending else (lambda i: i)
    cur = lambda ww: pl.BlockSpec((None, u * BLK, ww), lambda a, i: (a, at(i), 0))
    prev = lambda ww: pl.BlockSpec((None, BLK, ww), lambda a, i: (a, jnp.maximum(u * at(i) - 1, 0), 0))
    return cur, prev, (n, steps)


def _attn_fwd(q, k, v, sink, *, max_dist, o_dtype, name):
    n, l, w = q.shape
    wk = k.shape[-1]
    nb = l // BLK
    has_sink = sink is not None

    def body(*refs):
        sink_ref = None
        if has_sink:
            sink_ref, refs = refs[0], refs[1:]
        if nb > 1:
            q_ref, kc_ref, kp_ref, vc_ref, vp_ref, o_ref, lse_ref = refs[:7]
            first = pl.program_id(1) == 0
            for u in range(_att_units(nb)):
                rows, before = pl.ds(BLK * u, BLK), pl.ds(BLK * (u - 1), BLK)
                unit(q_ref.at[rows, :], kc_ref.at[rows, :], kp_ref if u == 0 else kc_ref.at[before, :],
                     vc_ref.at[rows, :], vp_ref if u == 0 else vc_ref.at[before, :], o_ref.at[rows, :], lse_ref.at[rows, :],
                     jnp.logical_not(first) if u == 0 else True, sink_ref, *refs[7:])
        else:
            q_ref, kc_ref, vc_ref, o_ref, lse_ref = refs[:5]
            for u in range(_att_units(nb)):
                unit(q_ref.at[u], kc_ref.at[u], None, vc_ref.at[u], None, o_ref.at[u], lse_ref.at[u], None, sink_ref, *refs[5:])

    def unit(q_ref, kc_ref, kp_ref, vc_ref, vp_ref, o_ref, lse_ref, has_prev, sink_ref, sscr, pscr, dscr):
        qi = lax.broadcasted_iota(jnp.int32, (BLK, BLK), 0)
        kj = lax.broadcasted_iota(jnp.int32, (BLK, BLK), 1)
        tri = kj <= qi
        eye = kj == qi
        cache = {}
        for p in range(w // LANES):
            qpair = q_ref[:, LANES * p:LANES * (p + 1)]
            kcat = _kv_cat(kc_ref, kp_ref, p // share, cache)
            for hh in range(2):
                s = _dot_nt(_lane_half(qpair, hh), kcat)
                if nb > 1:
                    sp = s[:, :BLK] if has_prev is True else jnp.where(has_prev, s[:, :BLK], NEG)
                    sscr[2 * p + hh] = jnp.where(tri, s[:, BLK:], sp)
                    if diag:
                        dscr[2 * p + hh] = jnp.where(eye, sp, NEG)
                else:
                    sscr[2 * p + hh] = jnp.where(tri, s, NEG)
        lane = lax.broadcasted_iota(jnp.int32, (BLK, LANES), 1)
        lse_all = jnp.zeros((BLK, LANES), f32)
        for p in range(w // LANES):
            for hh in range(2):
                h = 2 * p + hh
                comb = sscr[h]
                if diag:
                    dtile = dscr[h]
                    m = jnp.max(jnp.maximum(comb, dtile), axis=-1, keepdims=True)
                else:
                    m = jnp.max(comb, axis=-1, keepdims=True)
                if has_sink:
                    sk = sink_ref[0, h]
                    m = jnp.maximum(m, sk)
                e = jnp.exp(comb - m)
                if diag:
                    ed = jnp.exp(dtile - m)
                    den = jnp.sum(e + ed, axis=-1, keepdims=True)
                else:
                    den = jnp.sum(e, axis=-1, keepdims=True)
                if has_sink:
                    den = den + jnp.exp(sk - m)
                inv = 1.0 / den
                if nb > 1:
                    pscr[h, :, :BLK] = (jnp.where(tri, ed if diag else 0.0, e) * inv).astype(bf16)
                    pscr[h, :, BLK:] = (jnp.where(tri, e, 0.0) * inv).astype(bf16)
                else:
                    pscr[h] = (e * inv).astype(bf16)
                lse_all = jnp.where(lane == h, jnp.broadcast_to(m + jnp.log(den), (BLK, LANES)), lse_all)
        lse_ref[...] = lse_all
        for p in range(w // LANES):
            vcat = _kv_cat(vc_ref, vp_ref, p // share, cache)
            o_ref[:, LANES * p:LANES * (p + 1)] = (_dot(pscr[2 * p], _lane_half(vcat, 0))
                                                   + _dot(pscr[2 * p + 1], _lane_half(vcat, 1))).astype(o_ref.dtype)

    assert max_dist in (BLK - 1, BLK) and w % wk == 0
    share = w // wk
    diag = nb > 1 and max_dist == BLK
    cur, prev, grid = _attn_specs(n, nb, False)
    in_specs = [cur(w), cur(wk)] + ([prev(wk)] if nb > 1 else []) + [cur(wk)] + ([prev(wk)] if nb > 1 else [])
    args = [q, k] + ([k] if nb > 1 else []) + [v] + ([v] if nb > 1 else [])
    if has_sink:
        in_specs = [pl.BlockSpec(memory_space=pltpu.SMEM)] + in_specs
        args = [sink] + args
    return pl.pallas_call(
        body, name=name, grid=grid, in_specs=in_specs,
        out_specs=[cur(w), cur(LANES)],
        out_shape=[jax.ShapeDtypeStruct((n, l, w), o_dtype), jax.ShapeDtypeStruct((n, l, LANES), f32)],
        scratch_shapes=[pltpu.VMEM((w // HD, BLK, BLK), f32), pltpu.VMEM((w // HD, BLK, 2 * BLK if nb > 1 else BLK), bf16),
                        pltpu.VMEM((w // HD if diag else 1, BLK, BLK), f32)],
        compiler_params=_cp(("arbitrary", "arbitrary")),
    )(*args)


def _attn_bwd(q, k, v, do, delta, lse, sink, *, max_dist, name):
    n, l, w = q.shape
    wk = k.shape[-1]
    nb = l // BLK
    has_sink = sink is not None

    def body(*refs):
        sink_ref = dsink_ref = ck = cv = None
        if has_sink:
            sink_ref, refs = refs[0], refs[1:]
        nin = 8 if nb > 1 else 6
        ins, rest = refs[:nin], refs[nin:]
        if has_sink:
            dq_ref, dk_ref, dv_ref, dsink_ref = rest[:4]
            rest = rest[4:]
        else:
            dq_ref, dk_ref, dv_ref = rest[:3]
            rest = rest[3:]
        step = pl.program_id(1)
        if has_sink:
            @pl.when((pl.program_id(0) == 0) & (step == 0))
            def _():
                dsink_ref[...] = jnp.zeros_like(dsink_ref)

        if nb > 1:
            q_ref, kc_ref, kp_ref, vc_ref, vp_ref, do_ref, delta_ref, lse_ref = ins
            ck, cv = rest[:2]

            @pl.when(step == 0)
            def _():
                ck[...] = jnp.zeros_like(ck)
                cv[...] = jnp.zeros_like(cv)

            last = step == nb // _att_units(nb) - 1
            for u in reversed(range(_att_units(nb))):
                rows, before = pl.ds(BLK * u, BLK), pl.ds(BLK * (u - 1), BLK)
                unit(q_ref.at[rows, :], kc_ref.at[rows, :], kp_ref if u == 0 else kc_ref.at[before, :],
                     vc_ref.at[rows, :], vp_ref if u == 0 else vc_ref.at[before, :], do_ref.at[rows, :],
                     delta_ref.at[rows, :], lse_ref.at[rows, :], dq_ref.at[rows, :], dk_ref.at[rows, :], dv_ref.at[rows, :],
                     jnp.logical_not(last) if u == 0 else True, sink_ref, dsink_ref, ck, cv, *rest[2:])
        else:
            q_ref, kc_ref, vc_ref, do_ref, delta_ref, lse_ref = ins
            for u in range(_att_units(nb)):
                unit(q_ref.at[u], kc_ref.at[u], None, vc_ref.at[u], None, do_ref.at[u], delta_ref.at[u], lse_ref.at[u],
                     dq_ref.at[u], dk_ref.at[u], dv_ref.at[u], None, sink_ref, dsink_ref, None, None, *rest)

    def unit(q_ref, kc_ref, kp_ref, vc_ref, vp_ref, do_ref, delta_ref, lse_ref, dq_ref, dk_ref, dv_ref, has_prev,
             sink_ref, dsink_ref, ck, cv, sscr, dpscr, pscr, dsscr, dscr=None, ddscr=None):
        lane = lax.broadcasted_iota(jnp.int32, (BLK, LANES), 1)
        qi = lax.broadcasted_iota(jnp.int32, (BLK, BLK), 0)
        kj = lax.broadcasted_iota(jnp.int32, (BLK, BLK), 1)
        tri = kj <= qi
        eye = kj == qi
        cache = {}
        kp, vp = kp_ref, vp_ref
        for p in range(w // LANES):
            sl = slice(LANES * p, LANES * (p + 1))
            qpair, dopair = q_ref[:, sl], do_ref[:, sl]
            kcat, vcat = _kv_cat(kc_ref, kp, p // share, cache), _kv_cat(vc_ref, vp, p // share, cache)
            for hh in range(2):
                h = 2 * p + hh
                s = _dot_nt(_lane_half(qpair, hh), kcat)
                dp = _dot_nt(_lane_half(dopair, hh), vcat)
                if nb > 1:
                    sp = s[:, :BLK] if has_prev is True else jnp.where(has_prev, s[:, :BLK], NEG)
                    sscr[h] = jnp.where(tri, s[:, BLK:], sp)
                    dpscr[h] = jnp.where(tri, dp[:, BLK:], dp[:, :BLK])
                    if diag:
                        dscr[h] = jnp.where(eye, sp, NEG)
                        ddscr[h] = dp[:, :BLK]
                else:
                    sscr[h] = jnp.where(tri, s, NEG)
                    dpscr[h] = dp
        for p in range(w // LANES):
            for hh in range(2):
                h = 2 * p + hh
                lse_b = jnp.broadcast_to(lse_ref[:, h:h + 1], (BLK, BLK))
                delta = jnp.broadcast_to(delta_ref[:, h:h + 1], (BLK, BLK))
                pr = jnp.exp(sscr[h] - lse_b)
                ds = pr * (dpscr[h] - delta)
                if nb > 1:
                    if diag:
                        prd = jnp.exp(dscr[h] - lse_b)
                        dsd = prd * (ddscr[h] - delta)
                    else:
                        prd = dsd = 0.0
                    pscr[h, :, :BLK] = jnp.where(tri, prd, pr).astype(bf16)
                    pscr[h, :, BLK:] = jnp.where(tri, pr, 0.0).astype(bf16)
                    dsscr[h, :, :BLK] = jnp.where(tri, dsd, ds).astype(bf16)
                    dsscr[h, :, BLK:] = jnp.where(tri, ds, 0.0).astype(bf16)
                else:
                    pscr[h] = pr.astype(bf16)
                    dsscr[h] = ds.astype(bf16)
                if has_sink:
                    dsk = -jnp.sum(jnp.where(lane == 0, jnp.exp(sink_ref[0, h] - lse_b) * delta, 0.0), keepdims=True)
                    dsink_ref[h:h + 1, :] += jnp.broadcast_to(dsk, (1, LANES))
        for p in range(w // LANES):
            sl = slice(LANES * p, LANES * (p + 1))
            qpair, dopair = q_ref[:, sl], do_ref[:, sl]
            kcat = _kv_cat(kc_ref, kp, p // share, cache)
            dq_ref[:, sl] = _dot(dsscr[2 * p], _lane_half(kcat, 0)) + _dot(dsscr[2 * p + 1], _lane_half(kcat, 1))
            dk_pair = _dot_tn(dsscr[2 * p], _lane_half(qpair, 0)) + _dot_tn(dsscr[2 * p + 1], _lane_half(qpair, 1))
            dv_pair = _dot_tn(pscr[2 * p], _lane_half(dopair, 0)) + _dot_tn(pscr[2 * p + 1], _lane_half(dopair, 1))
            if nb > 1:
                dk_ref[:, sl] = dk_pair[BLK:] + ck[:, sl]
                dv_ref[:, sl] = dv_pair[BLK:] + cv[:, sl]
                ck[:, sl] = dk_pair[:BLK]
                cv[:, sl] = dv_pair[:BLK]
            else:
                dk_ref[:, sl] = dk_pair
                dv_ref[:, sl] = dv_pair

    assert max_dist in (BLK - 1, BLK) and w % wk == 0
    share = w // wk
    diag = nb > 1 and max_dist == BLK
    cur, prev, grid = _attn_specs(n, nb, True)
    in_specs = ([cur(w), cur(wk)] + ([prev(wk)] if nb > 1 else []) + [cur(wk)] + ([prev(wk)] if nb > 1 else [])
                + [cur(w), cur(LANES), cur(LANES)])
    args = [q, k] + ([k] if nb > 1 else []) + [v] + ([v] if nb > 1 else []) + [do, delta, lse]
    out_specs = [cur(w)] * 3
    out_shape = [jax.ShapeDtypeStruct((n, l, w), f32)] * 3
    if has_sink:
        in_specs = [pl.BlockSpec(memory_space=pltpu.SMEM)] + in_specs
        args = [sink] + args
        out_specs.append(pl.BlockSpec((NHEAD, LANES), lambda a, i: (0, 0)))
        out_shape.append(jax.ShapeDtypeStruct((NHEAD, LANES), f32))
    nh = w // HD
    scratch = [pltpu.VMEM((BLK, w), f32), pltpu.VMEM((BLK, w), f32)] if nb > 1 else []
    scratch += [pltpu.VMEM((nh, BLK, BLK), f32)] * 2 + [pltpu.VMEM((nh, BLK, 2 * BLK if nb > 1 else BLK), bf16)] * 2
    if diag:
        scratch += [pltpu.VMEM((nh, BLK, BLK), f32)] * 2
    return pl.pallas_call(
        body, name=name, grid=grid, in_specs=in_specs, out_specs=out_specs, out_shape=out_shape,
        scratch_shapes=scratch, compiler_params=_cp(("arbitrary", "arbitrary")),
    )(*args)


def _split2(x):
    hi = x.astype(bf16)
    return hi, (x - hi.astype(f32)).astype(bf16)


def _heads_to_lanes(xc, e):
    return sum(_dot(t, e) for t in _split2(xc))


def _lanes_to_heads(x, g):
    return sum(_dot(t, g) for t in _split2(x))


HEAD_EXPAND = (np.arange(LANES)[:, None] == np.arange(BW)[None, :] // HD).astype(np.float32)
HEAD_SUM = HEAD_EXPAND.T.copy()


def _branch_weights(l1_ref, l4_ref, l16_ref, scr):
    l4v = _perm_load(l4_ref, scr, 4)
    l16v = _perm_load(l16_ref, scr, 16)
    l1v = l1_ref[...]
    m = jnp.maximum(jnp.maximum(l1v, l4v), l16v)
    e1, e4, e16 = jnp.exp(l1v - m), jnp.exp(l4v - m), jnp.exp(l16v - m)
    z = e1 + e4 + e16
    return e1 / z, e4 / z, e16 / z


def _mix_out(oa, o1, l1, o4, l4, o16, l16, g_mix_a, g_mix_b, w_out, x, mod, g_post):
    def body(oa_ref, o1_ref, l1_ref, o4_ref, l4_ref, o16_ref, l16_ref, ga_ref, gb_ref, w_ref, x_ref, mod_ref, gp_ref, e_ref,
             x1_ref, y_ref, mixed_ref, ob_ref, scr):
        w1, w4, w16 = _branch_weights(l1_ref, l4_ref, l16_ref, scr)
        e = e_ref[...]
        x1w, x4w = _heads_to_lanes(w1, e), _heads_to_lanes(w4, e)
        ob = (x1w * o1_ref[...].astype(f32) + x4w * _perm_load(o4_ref, scr, 4)
              + (1.0 - x1w - x4w) * _perm_load(o16_ref, scr, 16))
        ob_ref[...] = ob
        oan, _ = _rms(oa_ref[...])
        obn, _ = _rms(ob)
        mixed = jnp.concatenate([oan * ga_ref[...], obn * gb_ref[...]], axis=1).astype(bf16)
        mixed_ref[...] = mixed
        y = _dot(mixed, w_ref[...])
        y_ref[...] = y
        yn, _ = _rms(y)
        x1_ref[...] = x_ref[...] + mod_ref[2:3, :] * (yn * gp_ref[...])

    nat = lambda w, dt: jax.ShapeDtypeStruct((BL, SEQ, w), dt)
    return pl.pallas_call(
        body, name="mix_out", grid=(BL, NJ),
        in_specs=[_tok(AQ), _tok(BW), _tok(LANES), _perm_spec(4, BW), _perm_spec(4, LANES), _perm_spec(16, BW),
                  _perm_spec(16, LANES), _full((1, AQ)), _full((1, BW)), _full((D, D)), _tok(D), MOD_SPEC, _full((1, D)),
                  _full((LANES, BW))],
        out_specs=[_tok(D), _tok(D), _tok(D), _tok(BW)],
        out_shape=[nat(D, f32), nat(D, f32), nat(D, bf16), nat(BW, f32)],
        scratch_shapes=[pltpu.VMEM((BW // LANES, TM, LANES), f32)],
        compiler_params=_cp(("arbitrary", "arbitrary")),
    )(oa, o1, l1, o4, l4, o16, l16, g_mix_a, g_mix_b, w_out, x, mod, g_post, jnp.asarray(HEAD_EXPAND, bf16))


def _mlp_up(x1, mod, g_pre, w_up):
    def body(x_ref, mod_ref, g_ref, w_ref, h_ref, u_ref, a_ref):
        xn, _ = _rms(x_ref[...])
        h = (xn * g_ref[...]) * (1.0 + mod_ref[4:5, :]) + mod_ref[3:4, :]
        hb = h.astype(bf16)
        h_ref[...] = hb
        for s in range(NCHIP):
            u = _dot(hb, w_ref[s])
            u_ref[:, D * s:D * (s + 1)] = u.astype(bf16)
            a_ref[:, D * s:D * (s + 1)] = jnp.square(jnp.maximum(u, 0.0)).astype(bf16)

    nat = lambda w: jax.ShapeDtypeStruct((BL, SEQ, w), bf16)
    return pl.pallas_call(
        body, name="mlp_up", grid=(BL, NJ),
        in_specs=[_tok(D), MOD_SPEC, _full((1, D)), _full((NCHIP, D, D))],
        out_specs=[_tok(D), _tok(DFF), _tok(DFF)], out_shape=[nat(D), nat(DFF), nat(DFF)],
        compiler_params=_cp(("arbitrary", "arbitrary")),
    )(x1, mod, g_pre, w_up)


def _mlp_down(a, w_down, x1, target, mod, g_post):
    def body(a_ref, w_ref, x_ref, t_ref, mod_ref, g_ref, gx_ref, dy_ref, accb_ref, accg_ref):
        _acc_init(accb_ref, accg_ref)
        y2 = _dot(a_ref[...], w_ref[...])
        yn, r = _rms(y2)
        g = g_ref[...]
        gt = mod_ref[5:6, :]
        n2 = yn * g
        err = x_ref[...] + gt * n2 - t_ref[...]
        gout = err * (1.0 / D)
        gx_ref[...] = gout
        dn2 = gout * gt
        dy_ref[...] = _rms_bwd(dn2 * g, yn, r).astype(bf16)
        accb_ref[0:1, :] += _colsum(gout * n2)
        accg_ref[0:1, :] += _colsum(dn2 * yn)
        accg_ref[1:2, :] += jnp.broadcast_to(jnp.sum(err * err, keepdims=True), (1, D))

    return pl.pallas_call(
        body, name="mlp_down", grid=(BL, NJ),
        in_specs=[_tok(DFF), _full((DFF, D)), _tok(D), _tok(D), MOD_SPEC, _full((1, D))],
        out_specs=[_tok(D), _tok(D), ACCB_SPEC, ACCG_SPEC],
        out_shape=[jax.ShapeDtypeStruct((BL, SEQ, D), f32), jax.ShapeDtypeStruct((BL, SEQ, D), bf16)] + ACC_SHAPES,
        compiler_params=_cp(("arbitrary", "arbitrary")),
    )(a, w_down, x1, target, mod, g_post)


def _mlp_bwd(dy2, u, w_down, w_up, x1, gx, mod, g_pre):
    def body(dy_ref, u_ref, wd_hbm, wu_hbm, x_ref, gx_ref, mod_ref, g_ref, du_ref, gx1_ref, accb_ref, accg_ref, wd, wu, sem):
        _acc_init(accb_ref, accg_ref)
        first = (pl.program_id(0) == 0) & (pl.program_id(1) == 0)
        c1 = pltpu.make_async_copy(wd_hbm, wd, sem.at[0])
        c2 = pltpu.make_async_copy(wu_hbm, wu, sem.at[1])

        @pl.when(first)
        def _():
            c1.start()
            c2.start()
            c1.wait()

        dy = dy_ref[...]
        for s in range(NCHIP):
            sl = slice(D * s, D * (s + 1))
            da = _dot_nt(dy, wd[sl, :])
            du_ref[:, sl] = (da * (2.0 * jnp.maximum(u_ref[:, sl].astype(f32), 0.0))).astype(bf16)

        @pl.when(first)
        def _():
            c2.wait()

        dh = jnp.zeros((TM, D), f32)
        for s in range(NCHIP):
            dh = dh + _dot_nt(du_ref[:, D * s:D * (s + 1)], wu[s])
        xn, r = _rms(x_ref[...])
        g = g_ref[...]
        n = xn * g
        dn = dh * (1.0 + mod_ref[4:5, :])
        gx1_ref[...] = gx_ref[...] + _rms_bwd(dn * g, xn, r)
        accb_ref[0:1, :] += _colsum(dh * n)
        accb_ref[1:2, :] += _colsum(dh)
        accg_ref[0:1, :] += _colsum(dn * xn)

    anyspec = pl.BlockSpec(memory_space=pl.ANY)
    return pl.pallas_call(
        body, name="mlp_bwd", grid=(BL, NJ),
        in_specs=[_tok(D), _tok(DFF), anyspec, anyspec, _tok(D), _tok(D), MOD_SPEC, _full((1, D))],
        out_specs=[_tok(DFF), _tok(D), ACCB_SPEC, ACCG_SPEC],
        out_shape=[jax.ShapeDtypeStruct((BL, SEQ, DFF), bf16), jax.ShapeDtypeStruct((BL, SEQ, D), f32)] + ACC_SHAPES,
        scratch_shapes=[pltpu.VMEM((DFF, D), bf16), pltpu.VMEM((NCHIP, D, D), bf16), pltpu.SemaphoreType.DMA((2,))],
        compiler_params=_cp(("arbitrary", "arbitrary")),
    )(dy2, u, w_down, w_up, x1, gx, mod, g_pre)


def _matmul_tn(a, b, *, tn, col_blocked, name, out_dtype=f32):
    t, m = a.shape
    n = b.shape[1]
    tmm = min(m, 1024)
    tk = 2048 if tn <= 1024 else 1024
    nk = t // tk

    def body(a_ref, b_ref, o_ref, acc):
        k = pl.program_id(2)

        @pl.when(k == 0)
        def _():
            acc[...] = jnp.zeros_like(acc)

        acc[...] += _dot_tn(a_ref[...], b_ref[...])

        @pl.when(k == nk - 1)
        def _():
            o_ref[...] = acc[...].astype(out_dtype)

    if col_blocked:
        out_spec = pl.BlockSpec((None, tmm, tn), lambda i, j, k: (j, i, 0))
        out_shape = jax.ShapeDtypeStruct((n // tn, m, tn), out_dtype)
    else:
        out_spec = pl.BlockSpec((tmm, tn), lambda i, j, k: (i, j))
        out_shape = jax.ShapeDtypeStruct((m, n), out_dtype)
    return pl.pallas_call(
        body, name=name, grid=(m // tmm, n // tn, nk),
        in_specs=[pl.BlockSpec((tk, tmm), lambda i, j, k: (k, i)), pl.BlockSpec((tk, tn), lambda i, j, k: (k, j))],
        out_specs=out_spec, out_shape=out_shape, scratch_shapes=[pltpu.VMEM((tmm, tn), f32)],
        compiler_params=_cp(("arbitrary", "arbitrary", "arbitrary")),
    )(a, b)


def _grad_w_in(h, dproj):
    t = h.shape[0]
    tk = 1024
    nk = t // tk
    sw = INW // NCHIP

    def body(a_ref, b_ref, o_ref, acc):
        k = pl.program_id(0)

        @pl.when(k == 0)
        def _():
            acc[...] = jnp.zeros_like(acc)

        acc[...] += _dot_tn(a_ref[...], b_ref[...])

        @pl.when(k == nk - 1)
        def _():
            for s in range(NCHIP):
                o_ref[s] = acc[:, sw * s:sw * (s + 1)].astype(bf16)

    return pl.pallas_call(
        body, name="grad_w_in", grid=(nk,),
        in_specs=[pl.BlockSpec((tk, D), lambda k: (k, 0)), pl.BlockSpec((tk, INW), lambda k: (k, 0))],
        out_specs=pl.BlockSpec((NCHIP, D, sw), lambda k: (0, 0, 0)), out_shape=jax.ShapeDtypeStruct((NCHIP, D, sw), bf16),
        scratch_shapes=[pltpu.VMEM((D, INW), f32)], compiler_params=_cp(("arbitrary",)),
    )(h, dproj)


def _attn_out_bwd(gx1, y, mod, g_post, w_out, oa, ob, g_mix_a, g_mix_b, l1, l4, l16):
    def body(gx_ref, y_ref, mod_ref, gp_ref, w_ref, oa_ref, ob_ref, ga_ref, gb_ref, l1_ref, l4_ref, l16_ref, e_ref, g_ref,
             dy_ref, doa_ref, do1_ref, do4_ref, do16_ref, da_ref, d1_ref, d4_ref, d16_ref, accb_ref, accg_ref, scr):
        _acc_init(accb_ref, accg_ref)
        w1, w4, w16 = _branch_weights(l1_ref, l4_ref, l16_ref, scr)
        e, hs = e_ref[...], g_ref[...]
        gx1v = gx_ref[...]
        yn, ry = _rms(y_ref[...])
        gp = gp_ref[...]
        gt = mod_ref[2:3, :]
        dn1 = gx1v * gt
        dy = _rms_bwd(dn1 * gp, yn, ry).astype(bf16)
        dy_ref[...] = dy
        dmixed = _dot_nt(dy, w_ref[...])
        dma, dmb = dmixed[:, :AQ], dmixed[:, AQ:]
        oa, ob = oa_ref[...], ob_ref[...]
        oan, ra = _rms(oa)
        obn, rb = _rms(ob)
        doa = _rms_bwd(dma * ga_ref[...], oan, ra)
        doa_ref[...] = doa.astype(bf16)
        da_ref[...] = _lanes_to_heads(doa * oa, hs)
        dob = _rms_bwd(dmb * gb_ref[...], obn, rb)
        dd = _lanes_to_heads(dob * ob, hs)
        x1w, x4w = _heads_to_lanes(w1, e), _heads_to_lanes(w4, e)
        do1_ref[...] = (x1w * dob).astype(bf16)
        d1_ref[...] = w1 * dd
        _perm_store(x4w * dob, scr, do4_ref, 4)
        _perm_store(w4 * dd, scr, d4_ref, 4)
        _perm_store((1.0 - x1w - x4w) * dob, scr, do16_ref, 16)
        _perm_store(w16 * dd, scr, d16_ref, 16)
        accb_ref[0:1, :] += _colsum(gx1v * (yn * gp))
        accg_ref[0:1, :] += _colsum(dn1 * yn)
        accg_ref[1:2, :] += jnp.concatenate([_colsum(dma * oan), _colsum(dmb * obn)], axis=1)

    nat = lambda w, dt: jax.ShapeDtypeStruct((BL, SEQ, w), dt)
    return pl.pallas_call(
        body, name="attn_out_bwd", grid=(BL, NJ),
        in_specs=[_tok(D), _tok(D), MOD_SPEC, _full((1, D)), _full((D, D)), _tok(AQ), _tok(BW), _full((1, AQ)), _full((1, BW)),
                  _tok(LANES), _perm_spec(4, LANES), _perm_spec(16, LANES), _full((LANES, BW)), _full((BW, LANES))],
        out_specs=[_tok(D), _tok(AQ), _tok(BW), _perm_spec(4, BW), _perm_spec(16, BW),
                   _tok(LANES), _tok(LANES), _perm_spec(4, LANES), _perm_spec(16, LANES), ACCB_SPEC, ACCG_SPEC],
        out_shape=[nat(D, bf16), nat(AQ, bf16), nat(BW, bf16), jax.ShapeDtypeStruct((BL, 4, SEQ // 4, BW), bf16),
                   jax.ShapeDtypeStruct((BL, 16, SEQ // 16, BW), bf16), nat(LANES, f32), nat(LANES, f32),
                   jax.ShapeDtypeStruct((BL, 4, SEQ // 4, LANES), f32), jax.ShapeDtypeStruct((BL, 16, SEQ // 16, LANES), f32)]
                  + ACC_SHAPES,
        scratch_shapes=[pltpu.VMEM((BW // LANES, TM, LANES), f32)],
        compiler_params=_cp(("arbitrary", "arbitrary")),
    )(gx1, y, mod, g_post, w_out, oa, ob, g_mix_a, g_mix_b, l1, l4, l16, jnp.asarray(HEAD_EXPAND, bf16),
      jnp.asarray(HEAD_SUM, bf16))


def _attn_in_bwd(dqa, dka, dva, d1, d4, d16, tc, ts1, ts2, w_in, x, gx1, mod, g_pre):
    def body(dqa_ref, dka_ref, dva_ref, dq1_ref, dk1_ref, dv1_ref, dq4_ref, dk4_ref, dv4_ref, dq16_ref, dk16_ref, dv16_ref,
             c_ref, s1_ref, s2_ref, w_ref, x_ref, gx_ref, mod_ref, g_ref, dproj_ref, dx_ref, accb_ref, accg_ref, scr):
        _acc_init(accb_ref, accg_ref)
        c, s1, s2 = c_ref[...], s1_ref[...], s2_ref[...]
        tot = lambda r1, r4, r16: r1[...] + _perm_load(r4, scr, 4) + _perm_load(r16, scr, 16)
        dqb = tot(dq1_ref, dq4_ref, dq16_ref)
        dkb = tot(dk1_ref, dk4_ref, dk16_ref)
        dvb = tot(dv1_ref, dv4_ref, dv16_ref)
        dproj = jnp.concatenate([
            _rope_t(dqa_ref[...], c, s1, s2) * QSCALE, _rope_t(_per_kv_head(dka_ref[...]), c, s1, s2),
            _per_kv_head(dva_ref[...]),
            _rope_t(dqb, c, s1, s2) * QSCALE, _rope_t(dkb, c, s1, s2), dvb], axis=1).astype(bf16)
        dproj_ref[...] = dproj
        dh = _dot_nt(dproj, w_ref[...])
        xn, r = _rms(x_ref[...])
        g = g_ref[...]
        dn = dh * (1.0 + mod_ref[1:2, :])
        dx_ref[...] = gx_ref[...] + _rms_bwd(dn * g, xn, r)
        accb_ref[0:1, :] += _colsum(dh * (xn * g))
        accb_ref[1:2, :] += _colsum(dh)
        accg_ref[0:1, :] += _colsum(dn * xn)

    return pl.pallas_call(
        body, name="attn_in_bwd", grid=(BL, NJ),
        in_specs=[_tok(AQ), _tok(AQ), _tok(AQ)] + [_tok(BW)] * 3 + [_perm_spec(4, BW)] * 3 + [_perm_spec(16, BW)] * 3
                 + [_tok(LANES)] * 3 + [_full((D, INW)), _tok(D), _tok(D), MOD_SPEC, _full((1, D))],
        out_specs=[_tok(INW), _tok(D), ACCB_SPEC, ACCG_SPEC],
        out_shape=[jax.ShapeDtypeStruct((BL, SEQ, INW), bf16), jax.ShapeDtypeStruct((BL, SEQ, D), f32)] + ACC_SHAPES,
        scratch_shapes=[pltpu.VMEM((BW // LANES, TM, LANES), f32)],
        compiler_params=_cp(("arbitrary", "arbitrary")),
    )(dqa, dka, dva, *d1, *d4, *d16, tc, ts1, ts2, w_in, x, gx1, mod, g_pre)


def _inv_lane():
    inv = np.float32(THETA) ** (-np.arange(0, ROT, 2, dtype=np.float32) / np.float32(ROT))
    lane = np.arange(LANES) % HD
    return jnp.asarray(np.where(lane < ROT, inv[lane % (ROT // 2)], 0.0).astype(np.float32)[None, :])


def _local_step(x, tabs, mod, target, w_in, later_weights, grad_ready, g_attn_pre,
                g_attn_post, sink_a, g_mix_a, g_mix_b, g_mlp_pre, g_mlp_post):
    tc, ts1, ts2 = [t.reshape(BL, SEQ, LANES) for t in tabs]

    (h, qa, ka, va, q1, k1, v1, q4, k4, v4, q16, k16, v16, w_in) = _attn_in(x, mod, g_attn_pre, w_in, tc, ts1, ts2)
    seqs = lambda t: t.reshape(t.shape[0] * t.shape[1], t.shape[2], t.shape[3])
    q4, k4, v4, q16, k16, v16 = [seqs(t) for t in (q4, k4, v4, q16, k16, v16)]
    oa, la = _attn_fwd(qa, ka, va, sink_a, max_dist=BLK - 1, o_dtype=f32, name="attn_a_fwd")
    o1, l1 = _attn_fwd(q1, k1, v1, None, max_dist=BLK, o_dtype=bf16, name="attn_b1_fwd")
    o4, l4 = _attn_fwd(q4, k4, v4, None, max_dist=BLK, o_dtype=bf16, name="attn_b4_fwd")
    out_weight, mlp_weights, mod = later_weights((oa, o1, o4), mod)
    q16, mod = lax.optimization_barrier((q16, mod))
    o16, l16 = _attn_fwd(q16, k16, v16, None, max_dist=BLK, o_dtype=bf16, name="attn_b16_fwd")
    w_out = out_weight((o16,))
    b4 = lambda t: t.reshape(BL, 4, SEQ // 4, t.shape[-1])
    b16 = lambda t: t.reshape(BL, 16, SEQ // 16, t.shape[-1])
    x1, y, mixed, ob = _mix_out(oa, o1, l1, b4(o4), b4(l4), b16(o16), b16(l16), g_mix_a, g_mix_b, w_out, x, mod, g_attn_post)
    w_up, w_down = mlp_weights((x1,))
    h2, u, a = _mlp_up(x1, mod, g_mlp_pre, w_up)
    gx, dy2, accb_d, accg_d = _mlp_down(a, w_down, x1, target, mod, g_mlp_post)

    flat = lambda t: t.reshape(BL * SEQ, t.shape[-1])
    mod = grad_ready("w_down", _matmul_tn(flat(a), flat(dy2), tn=D, col_blocked=False, name="grad_w_down", out_dtype=bf16), mod)
    du, gx1, accb_m, accg_m = _mlp_bwd(dy2, u, w_down, w_up, x1, gx, mod, g_mlp_pre)
    mod = grad_ready("w_up", _matmul_tn(flat(h2), flat(du), tn=D, col_blocked=True, name="grad_w_up", out_dtype=bf16), mod)

    dy, doa, do1, do4, do16, da, dl1, dl4, dl16, accb_o, accg_o = _attn_out_bwd(
        gx1, y, mod, g_attn_post, w_out, oa, ob, g_mix_a, g_mix_b, l1, b4(l4), b16(l16))
    sink_behind = grad_ready("w_out", _matmul_tn(flat(mixed), flat(dy), tn=D, col_blocked=False, name="grad_w_out",
                                                  out_dtype=bf16), sink_a)
    dqa, dka, dva, dsink = _attn_bwd(qa, ka, va, doa, da, la, sink_behind, max_dist=BLK - 1, name="attn_a_bwd")
    d1 = _attn_bwd(q1, k1, v1, do1, dl1, l1, None, max_dist=BLK, name="attn_b1_bwd")
    d4 = _attn_bwd(q4, k4, v4, seqs(do4), seqs(dl4), l4, None, max_dist=BLK, name="attn_b4_bwd")
    d16 = _attn_bwd(q16, k16, v16, seqs(do16), seqs(dl16), l16, None, max_dist=BLK, name="attn_b16_bwd")
    dproj, grad_x, accb_i, accg_i = _attn_in_bwd(dqa, dka, dva, d1, [b4(t) for t in d4], [b16(t) for t in d16],
                                                 tc, ts1, ts2, w_in, x, gx1, mod, g_attn_pre)
    gw_in = _grad_w_in(flat(h), flat(dproj))
    dsink = grad_ready("w_in", gw_in, dsink)

    return grad_x, (accb_i, accb_o, accb_m, accb_d, accg_i, accg_o, accg_m, accg_d, dsink)


ADAW = NMOD * D // NCHIP


def _pos():
    return lax.axis_index("x"), lax.axis_index("y"), lax.axis_index("c")


def _flip(v, bit):
    return 1 - v if bit else v


def _all_peers(x, y, c):
    return [(_flip(x, k >> 2 & 1), _flip(y, k >> 1 & 1), _flip(c, k & 1)) for k in range(1, NDEV)]


def _other_chips(x, y):
    return [(1 - x, y), (x, 1 - y), (1 - x, 1 - y)]


def _rcopy(src, dst, send, recv, k, dev, k_recv=None):
    return pltpu.make_async_remote_copy(src_ref=src, dst_ref=dst, send_sem=send.at[k],
                                        recv_sem=recv.at[k if k_recv is None else k_recv],
                                        device_id=dev, device_id_type=MESH)


def _small_copies(src, land, send, recv):
    x, y, c = _pos()
    me = 4 * x + 2 * y + c
    return [(_rcopy(src, land.at[me], send, recv, k, p), _rcopy(src, land.at[4 * p[0] + 2 * p[1] + p[2]], send, recv, k, p))
            for k, p in enumerate(_all_peers(x, y, c))]


def _ada_fwd(c_in, landed, w_ada, b_cols):
    def body(c_ref, land, w_hbm, b_ref, mod_ref, cond_ref, mbuf, w_ref, s2, r2, wsem):
        x, y, c = _pos()
        chip = 2 * x + y
        me = 4 * x + 2 * y + c
        wcopy = pltpu.make_async_copy(w_hbm, w_ref, wsem)
        wcopy.start()
        for i in range(NDEV):
            @pl.when(me == i)
            def _():
                cond_ref[BL * i:BL * (i + 1), :] = c_ref[...]

            @pl.when(me != i)
            def _():
                cond_ref[BL * i:BL * (i + 1), :] = land[i]
        call = cond_ref[...]
        cond = call / (1.0 + jnp.exp(-call))
        cond_ref[...] = cond
        wcopy.wait()
        mbuf[chip] = _dot(cond.astype(bf16), w_ref[...].astype(bf16)) + b_ref[...]
        chips = _other_chips(x, y)
        sends = [_rcopy(mbuf.at[chip], mbuf.at[chip], s2, r2, j, (px, py, c)) for j, (px, py) in enumerate(chips)]
        for cp in sends:
            cp.start()
        for j, (px, py) in enumerate(chips):
            _rcopy(mbuf.at[chip], mbuf.at[2 * px + py], s2, r2, j, (px, py, c)).wait_recv()
        for cp in sends:
            cp.wait_send()
        row = lax.broadcasted_iota(jnp.int32, (BL * NDEV, ADAW), 0)
        for s in range(NCHIP):
            slab = mbuf[s]
            for j in range(BL):
                mod_ref[j:j + 1, ADAW * s:ADAW * (s + 1)] = jnp.sum(jnp.where(row == BL * me + j, slab, 0.0), axis=0, keepdims=True)

    vm = pl.BlockSpec(memory_space=pltpu.VMEM)
    return pl.pallas_call(
        body, name="ada_fwd", in_specs=[vm, vm, pl.BlockSpec(memory_space=pl.ANY), vm], out_specs=[vm, vm],
        out_shape=[jax.ShapeDtypeStruct((BL, NMOD * D), f32), jax.ShapeDtypeStruct((BL * NDEV, D), f32)],
        scratch_shapes=[pltpu.VMEM((NCHIP, BL * NDEV, ADAW), f32), pltpu.VMEM((D, ADAW), f32),
                        pltpu.SemaphoreType.DMA((NCHIP - 1,)), pltpu.SemaphoreType.DMA((NCHIP - 1,)),
                        pltpu.SemaphoreType.DMA],
        compiler_params=pltpu.CompilerParams(vmem_limit_bytes=VMEM_LIMIT),
    )(c_in, landed, w_ada, b_cols)


PAY_ROWS = 4


def _small_pack(accs):
    def body(bi, bo, bm, bd, gi, go, gm, gd, dsink, pay):
        pay[...] = jnp.zeros_like(pay)
        for b in range(BL):
            for k, (ref, r) in enumerate(((bi, 1), (bi, 0), (bo, 0), (bm, 1), (bm, 0), (bd, 0))):
                pay[b:b + 1, D * k:D * (k + 1)] = ref[b, r:r + 1, :]
        for off, ref, r in ((OFF_G_ATTN_PRE, gi, 0), (OFF_G_ATTN_POST, go, 0), (OFF_G_MIX_A, go, 1), (OFF_G_MLP_PRE, gm, 0),
                            (OFF_G_MLP_POST, gd, 0)):
            pay[BL:BL + 1, off:off + D] = ref[r:r + 1, :]
        eye = lax.broadcasted_iota(jnp.int32, (NHEAD, LANES), 0) == lax.broadcasted_iota(jnp.int32, (NHEAD, LANES), 1)
        pay[BL:BL + 1, OFF_SINK:OFF_SINK + LANES] = jnp.sum(jnp.where(eye, dsink[...], 0.0), axis=0, keepdims=True)
        pay[BL:BL + 1, OFF_LOSS:OFF_LOSS + LANES] = gd[1:2, 0:LANES]

    vm = pl.BlockSpec(memory_space=pltpu.VMEM)
    return pl.pallas_call(body, name="small_pack", in_specs=[vm] * 9, out_specs=vm,
                          out_shape=jax.ShapeDtypeStruct((PAY_ROWS, PAYW), f32))(*accs)


def _small_sum(own, landed, cond_all):
    def body(pay, land, cond_ref, gw_ref, gb_ref, small_ref, pbuf, dall):
        x, y, c = _pos()
        chip = 2 * x + y
        me = 4 * x + 2 * y + c
        for i in range(NDEV):
            @pl.when(me == i)
            def _():
                pbuf[i] = pay[...]

            @pl.when(me != i)
            def _():
                pbuf[i] = land[i]
        small = pbuf[0, BL:BL + 1, :]
        for i in range(1, NDEV):
            small = small + pbuf[i, BL:BL + 1, :]
        small_ref[...] = small
        for i in range(NDEV):
            dall[BL * i:BL * (i + 1), :] = pbuf[i, 0:BL, :]
        gb_ref[...] = jnp.sum(dall[...], axis=0, keepdims=True)
        cols = jnp.zeros((BL * NDEV, ADAW), f32)
        for s in range(NCHIP):
            cols = cols + jnp.where(chip == s, dall[:, ADAW * s:ADAW * (s + 1)], 0.0)
        gw_ref[...] = _dot_tn(cond_ref[...].astype(bf16), cols.astype(bf16))

    vm = pl.BlockSpec(memory_space=pltpu.VMEM)
    return pl.pallas_call(
        body, name="small_sum", in_specs=[vm] * 3, out_specs=[vm] * 3,
        out_shape=[jax.ShapeDtypeStruct((D, ADAW), f32), jax.ShapeDtypeStruct((1, PAYW), f32), jax.ShapeDtypeStruct((1, PAYW), f32)],
        scratch_shapes=[pltpu.VMEM((NDEV, PAY_ROWS, PAYW), f32), pltpu.VMEM((BL * NDEV, PAYW), f32)],
        compiler_params=pltpu.CompilerParams(vmem_limit_bytes=VMEM_LIMIT),
    )(own, landed, cond_all)


def _half(ref, c):
    r2 = ref.shape[0] // 2
    return ref.at[pl.ds(c * r2 if isinstance(c, int) else pl.multiple_of(c * r2, 16), r2), :]


HBM_SPEC = pl.BlockSpec(memory_space=pltpu.HBM)
SEM_SPEC = pl.BlockSpec(memory_space=pltpu.SEMAPHORE)
EFFECT = pltpu.SideEffectType.DATAFLOW_SIDE_EFFECTING
NLINK = NCHIP - 1


def _in_hbm(a):
    return pltpu.with_memory_space_constraint(a, pltpu.HBM)


NSEM = 8


def _split_start(name, srcs, land_shapes, builds, carry, after=(), lands=None):
    n = len(srcs)
    na, nc = len(after), len(carry)

    def body(*refs):
        src, land = refs[:n], refs[n:2 * n]
        kept = refs[2 * n + na:2 * n + na + nc]
        outs = refs[2 * n + na + nc:]
        send, recv, passed = outs[:n], outs[n:2 * n], outs[4 * n:]
        for t in range(n):
            for out_cp, _ in builds[t](src[t], land[t], send[t], recv[t]):
                out_cp.start()
        for a, b in zip(kept, passed):
            b[...] = a[...]

    if lands is None:
        lands = [lax.empty(s.shape, s.dtype) for s in land_shapes]
    lands = [_in_hbm(a) for a in lands]
    sems = [pltpu.SemaphoreType.DMA((NSEM,))] * (2 * n)
    thru = [pltpu.HBM(a.shape, a.dtype) for a in list(srcs) + lands]
    vm = pl.BlockSpec(memory_space=pltpu.VMEM)
    res = pl.pallas_call(
        body, name=name, out_shape=sems + thru + [jax.ShapeDtypeStruct(a.shape, a.dtype) for a in carry],
        in_specs=[HBM_SPEC] * (2 * n) + [pl.BlockSpec(memory_space=pl.ANY)] * na + [vm] * nc,
        out_specs=[SEM_SPEC] * (2 * n) + [HBM_SPEC] * (2 * n) + [vm] * nc,
        input_output_aliases={i: 2 * n + i for i in range(2 * n)},
        compiler_params=pltpu.CompilerParams(has_side_effects=EFFECT),
    )(*[_in_hbm(a) for a in srcs], *lands, *after, *carry)
    flight = [(res[2 * n + t], res[3 * n + t], res[t], res[n + t]) for t in range(n)]
    return flight, list(res[4 * n:])


def _split_wait(name, flight, builds, after):
    m = len(flight)
    na = len(after)

    def body(*refs):
        src, land, send, recv = refs[:m], refs[m:2 * m], refs[2 * m:3 * m], refs[3 * m:4 * m]
        for t in range(m):
            for out_cp, in_cp in builds[t](src[t], land[t], send[t], recv[t]):
                out_cp.wait_send()
                in_cp.wait_recv()

    ops = [f[0] for f in flight] + [f[1] for f in flight] + [f[2] for f in flight] + [f[3] for f in flight]
    res = pl.pallas_call(
        body, name=name, out_shape=[pltpu.HBM(a.shape, a.dtype) for a in ops[:2 * m]],
        in_specs=[HBM_SPEC] * (2 * m) + [SEM_SPEC] * (2 * m) + [pl.BlockSpec(memory_space=pl.ANY)] * na,
        out_specs=[HBM_SPEC] * (2 * m), input_output_aliases={i: i for i in range(2 * m)},
        compiler_params=pltpu.CompilerParams(has_side_effects=EFFECT),
    )(*ops, *after)
    return res[:m], res[m:2 * m]


def _weight_copies(src, land, send, recv):
    x, y, c = _pos()
    chip = 2 * x + y
    return [(_rcopy(_half(src, c), _half(land.at[chip], c), send, recv, j, (px, py, c)),
             _rcopy(_half(src, c), _half(land.at[2 * px + py], c), send, recv, j, (px, py, c)))
            for j, (px, py) in enumerate(_other_chips(x, y))]


NDIRECT = NDEV - 1


def _direct_grad_copies(src, land, send, recv):
    x, y, c = _pos()
    out, arrive = [], []
    for j, (px, py) in enumerate(_other_chips(x, y)):
        for hc in range(2):
            out.append(_rcopy(_half(src.at[2 * px + py], hc), land.at[2 * j + c], send, recv, 2 * j + hc, (px, py, hc),
                              k_recv=2 * j + c))
            arrive.append(_rcopy(_half(src.at[2 * px + py], hc), land.at[2 * j + hc], send, recv, 2 * j + hc, (px, py, hc)))
    own = _rcopy(_half(src.at[2 * x + y], 1 - c), land.at[NDIRECT - 1], send, recv, NDIRECT - 1, (x, y, 1 - c))
    return list(zip(out, arrive)) + [(own, own)]


def _pair_weight_copies(src, land, send, recv):
    x, y, c = _pos()
    sib = (x, y, 1 - c)
    cps = []
    for j, (px, py) in enumerate(_other_chips(x, y)):
        mine, theirs = _half(land.at[2 * px + py], c), _half(land.at[2 * px + py], 1 - c)
        cps.append((_rcopy(mine, mine, send, recv, j, sib), _rcopy(theirs, theirs, send, recv, j, sib)))
    own = _rcopy(src, land.at[2 * x + y], send, recv, NLINK, sib)
    return cps + [(own, own)]


RS_ROWS = 256


def _chip_add(own, landed, pos_arr, name):
    nl, r2, cw = landed.shape
    rows = min(RS_ROWS, r2)
    nr = r2 // rows

    def body(s_ref, h_ref, q_ref, o_ref):
        acc = h_ref[...].astype(f32)
        for j in range(nl):
            acc = acc + q_ref[j].astype(f32)
        o_ref[...] = acc

    gs = pltpu.PrefetchScalarGridSpec(
        num_scalar_prefetch=1, grid=(nr,),
        in_specs=[pl.BlockSpec((None, rows, cw), lambda j, s: (s[0], s[1] * nr + j, 0)),
                  pl.BlockSpec((nl, rows, cw), lambda j, s: (0, j, 0))],
        out_specs=pl.BlockSpec((rows, cw), lambda j, s: (s[1] * nr + j, 0)))
    return pl.pallas_call(body, name=name, grid_spec=gs, out_shape=jax.ShapeDtypeStruct((2 * r2, cw), f32),
                          compiler_params=_cp(("arbitrary",)))(pos_arr, own, landed)


def _pair_gather_copies(src, land, send, recv):
    x, y, c = _pos()
    sib = (x, y, 1 - c)
    return [(_rcopy(_half(land, c), _half(land, c), send, recv, 0, sib),
             _rcopy(_half(land, 1 - c), _half(land, 1 - c), send, recv, 0, sib))]


def _adamw_math(w, g, m, v):
    m = B1 * m + (1.0 - B1) * g
    v = B2 * v + (1.0 - B2) * jnp.square(g)
    m_hat = m / (1.0 - B1 ** STEP)
    v_hat = v / (1.0 - B2 ** STEP)
    return -LR * (m_hat / (jnp.sqrt(v_hat) + AEPS) + WD * w), m, v


ADAM_BLOCK = 256 * 1024
ADAM_READS_AHEAD = 3


def _adamw(w, g, m, v, name, after=(), landed=True):
    r, cw = w.shape
    na = len(after)

    rows = max(k for k in range(SUBLANES, ADAM_BLOCK // cw + 1, SUBLANES) if r % k == 0)
    steps = r // rows
    ahead = min(ADAM_READS_AHEAD, steps)

    def body(*refs):
        ins, outs = refs[:4], refs[4 + na:4 + na + nout]
        bufs, sem = refs[4 + na + nout:-1], refs[-1]
        i = pl.program_id(0)

        def reads(step):
            at = pl.ds(step * rows if isinstance(step, int) else pl.multiple_of(step * rows, SUBLANES), rows)
            return [pltpu.make_async_copy(hbm.at[at, :], buf.at[step % ahead], sem.at[t, step % ahead])
                    for t, (hbm, buf) in enumerate(zip(ins, bufs))]

        @pl.when(i == 0)
        def _():
            for s in range(ahead - 1):
                for cp in reads(s):
                    cp.start()

        @pl.when(i + (ahead - 1) < steps)
        def _():
            for cp in reads(i + (ahead - 1)):
                cp.start()

        for cp in reads(i):
            cp.wait()
        w_blk, g, m_blk, v_blk = [buf[i % ahead] for buf in bufs]
        if landed:
            outs[0][...] = g
        outs[-3][...], outs[-2][...], outs[-1][...] = _adamw_math(w_blk, g, m_blk, v_blk)

    spec = pl.BlockSpec((rows, cw), lambda i: (i, 0))
    nout = 4 if landed else 3
    res = pl.pallas_call(body, name=name, grid=(steps,), in_specs=[pl.BlockSpec(memory_space=pl.ANY)] * (4 + na),
                         out_specs=[spec] * nout, out_shape=[jax.ShapeDtypeStruct((r, cw), f32)] * nout,
                         scratch_shapes=[pltpu.VMEM((ahead, rows, cw), f32)] * 4 + [pltpu.SemaphoreType.DMA((4, ahead))],
                         compiler_params=_cp(("arbitrary",)))(w, g, m, v, *after)
    return list(res) if landed else [g, *res]


SMALL = (("b_ada", None, PAYW), ("g_attn_pre", OFF_G_ATTN_PRE, D), ("g_attn_post", OFF_G_ATTN_POST, D), ("sink_a", OFF_SINK, 8),
         ("g_mix_a", OFF_G_MIX_A, AQ), ("g_mix_b", OFF_G_MIX_B, BW), ("g_mlp_pre", OFF_G_MLP_PRE, D), ("g_mlp_post", OFF_G_MLP_POST, D))


def _adamw_small(small, gb, params):
    n = len(SMALL)

    def body(*refs):
        small_ref, gb_ref = refs[:2]
        wmv = refs[2:2 + 3 * n]
        loss_ref = refs[2 + 3 * n]
        outs = refs[3 + 3 * n:]
        loss_ref[...] = small_ref[:, OFF_LOSS:OFF_LOSS + 1] * (0.5 / D)
        for i, (_, off, width) in enumerate(SMALL):
            g = gb_ref[...] if off is None else small_ref[:, off:off + width]
            w_ref, m_ref, v_ref = wmv[3 * i:3 * i + 3]
            outs[4 * i][...] = g
            outs[4 * i + 1][...], outs[4 * i + 2][...], outs[4 * i + 3][...] = _adamw_math(w_ref[...], g, m_ref[...], v_ref[...])

    vm = pl.BlockSpec(memory_space=pltpu.VMEM)
    out_shape = [jax.ShapeDtypeStruct((1, 1), f32)]
    for _, _, width in SMALL:
        out_shape += [jax.ShapeDtypeStruct((1, width), f32)] * 4
    flat = [a for wmv in params for a in wmv]
    res = pl.pallas_call(body, name="adamw_small", in_specs=[vm] * (2 + 3 * n), out_specs=[vm] * len(out_shape),
                         out_shape=out_shape)(small, gb, *flat)
    return res[0], {name: res[1 + 4 * i:5 + 4 * i] for i, (name, _, _) in enumerate(SMALL)}


def kernel(x, c, positions, w_ada, b_ada, g_attn_pre, g_attn_post, w_in, sink_a, g_mix_a, g_mix_b, w_out, g_mlp_pre, g_mlp_post, w_up, w_down, loss_target, m_w_ada, m_b_ada, m_g_attn_pre, m_g_attn_post, m_w_in, m_sink_a, m_g_mix_a, m_g_mix_b, m_w_out, m_g_mlp_pre, m_g_mlp_post, m_w_up, m_w_down, v_w_ada, v_b_ada, v_g_attn_pre, v_g_attn_post, v_w_in, v_sink_a, v_g_mix_a, v_g_mix_b, v_w_out, v_g_mlp_pre, v_g_mlp_post, v_w_up, v_w_down):
    given = dict(w_ada=w_ada, b_ada=b_ada, g_attn_pre=g_attn_pre, g_attn_post=g_attn_post, w_in=w_in, sink_a=sink_a, g_mix_a=g_mix_a,
                 g_mix_b=g_mix_b, w_out=w_out, g_mlp_pre=g_mlp_pre, g_mlp_post=g_mlp_post, w_up=w_up, w_down=w_down)
    moms = dict(w_ada=(m_w_ada, v_w_ada), b_ada=(m_b_ada, v_b_ada), g_attn_pre=(m_g_attn_pre, v_g_attn_pre),
                g_attn_post=(m_g_attn_post, v_g_attn_post), w_in=(m_w_in, v_w_in), sink_a=(m_sink_a, v_sink_a),
                g_mix_a=(m_g_mix_a, v_g_mix_a), g_mix_b=(m_g_mix_b, v_g_mix_b), w_out=(m_w_out, v_w_out),
                g_mlp_pre=(m_g_mlp_pre, v_g_mlp_pre), g_mlp_post=(m_g_mlp_post, v_g_mlp_post), w_up=(m_w_up, v_w_up),
                w_down=(m_w_down, v_w_down))
    order = ["w_ada", "b_ada", "g_attn_pre", "g_attn_post", "w_in", "sink_a", "g_mix_a", "g_mix_b", "w_out", "g_mlp_pre",
             "g_mlp_post", "w_up", "w_down"]
    xi, yi, ci = _pos()
    chip = 2 * xi + yi

    pos_arr = jnp.stack([chip, ci]).astype(jnp.int32)
    big = ("w_in", "w_out", "w_up", "w_down")

    gathered = [jax.ShapeDtypeStruct((NCHIP,) + given[n].shape[1:], bf16) for n in big]
    (flight_c, *flight_in), (inv_lane,) = _split_start(
        "weights_start_first", [c, w_in[0].astype(bf16)], [jax.ShapeDtypeStruct((NDEV, BL, D), f32), gathered[0]],
        [_small_copies, _weight_copies], [_inv_lane()])
    inv_lane, rest, positions, b_all = lax.optimization_barrier((inv_lane, [given[n][0] for n in big[1:]], positions, b_ada))
    tabs = _rope_tables(positions.reshape(BL * SEQ, 1), inv_lane)
    rest = [w.astype(bf16) for w in rest]
    b_cols = lax.dynamic_slice(b_all, (0, chip * ADAW), (1, ADAW))
    (c_own,), (c_all,) = _split_wait("cond_wait", [flight_c], [_small_copies], (*tabs, *rest))
    mod, cond_all = _ada_fwd(c_own, c_all, w_ada[0], b_cols)

    srcs, lands = _split_wait("weights_wait_first", flight_in, [_weight_copies], (mod,))
    cross, (mod,) = _split_start("weights_pair_start_first", srcs, None, [_pair_weight_copies], [mod], lands=lands)
    flight_rest, (mod,) = _split_start("weights_start_rest", rest, gathered[1:], [_weight_copies] * 3, [mod])
    _, (win_g,) = _split_wait("weights_pair_wait_first", cross, [_pair_weight_copies], (mod,))
    mod = mod.reshape(BL, NMOD, D)

    def later_weights(after, carry):
        srcs, lands = _split_wait("weights_wait_rest", flight_rest, [_weight_copies] * 3, after)
        fl, (carry,) = _split_start("weights_pair_start_rest", srcs, None, [_pair_weight_copies] * 3, [carry], lands=lands)
        def out_weight(after):
            _, (wout_g,) = _split_wait("weights_pair_wait_out", fl[:1], [_pair_weight_copies], after)
            return wout_g.reshape(D, D)

        def mlp_weights(after):
            _, (wup_g, wdn_g) = _split_wait("weights_pair_wait_mlp", fl[1:], [_pair_weight_copies] * 2, after)
            return wup_g, wdn_g.reshape(DFF, D)

        return out_weight, mlp_weights, carry

    waiting, pending = {}, {}

    def send_grads(carry):
        names = list(waiting)
        slabs = [waiting.pop(n) for n in names]
        lands = [jax.ShapeDtypeStruct((NDIRECT, s.shape[1] // 2, s.shape[2]), bf16) for s in slabs]
        fl, (carry,) = _split_start("grad_start_" + names[-1], slabs, lands, [_direct_grad_copies] * len(names), [carry])
        for n, f in zip(names, fl):
            pending[n] = [f]
        return carry

    def grad_ready(name, g, carry):
        waiting[name] = g if g.ndim == 3 else g.reshape(NCHIP, g.shape[0] // NCHIP, g.shape[1])
        return send_grads(carry) if name == "w_out" else carry

    grad_x, accs = _local_step(x, tabs, mod, loss_target, win_g, later_weights, grad_ready,
                               g_attn_pre, g_attn_post, sink_a, g_mix_a, g_mix_b, g_mlp_pre, g_mlp_post)

    grads, out = {}, {}

    def update(n, after=()):
        tr = (lambda a: a.T) if n == "w_in" else (lambda a: a)
        res = _adamw(tr(given[n][0]), tr(grads[n]), tr(moms[n][0][0]), tr(moms[n][1][0]), "adamw_" + n, after,
                     landed=n != "w_ada")
        out[n] = tuple(tr(a)[None] for a in res)
        return res[3]

    def reduce(names, after):
        fl = sum((pending[n] for n in names), [])
        halves, landed = _split_wait("grad_wait_" + names[0], fl, [_direct_grad_copies] * len(names), after)
        fulls = [_chip_add(h, q, pos_arr, "grad_chip_sum_" + n) for h, q, n in zip(halves, landed, names)]
        tokens = [jnp.full((SUBLANES, LANES), float(i), f32) for i in range(len(names))]
        return _split_start("grad_gather_start_" + names[0], tokens, None, [_pair_gather_copies] * len(names), [],
                            lands=fulls)[0]

    def gathered_update(n, flight, after):
        _, (grads[n],) = _split_wait("grad_gather_wait_" + n, [flight], [_pair_gather_copies], after)
        return update(n)

    slab = waiting.pop("w_in")
    (fl_small, fl_in), (cond_all,) = _split_start(
        "small_start", [_small_pack(accs), slab],
        [jax.ShapeDtypeStruct((NDEV, PAY_ROWS, PAYW), f32), jax.ShapeDtypeStruct((NDIRECT, slab.shape[1] // 2, slab.shape[2]), bf16)],
        [_small_copies, _direct_grad_copies], [cond_all])
    pending["w_in"] = [fl_in]
    fl_down, fl_up, fl_out = reduce(("w_down", "w_up", "w_out"), (cond_all,))
    (pay,), (landed,) = _split_wait("small_wait", [fl_small], [_small_copies], (fl_out[0],))
    grads["w_ada"], gb, small = _small_sum(pay, landed, cond_all)
    last = update("w_ada")
    last = gathered_update("w_down", fl_down, (last,))
    last = gathered_update("w_up", fl_up, (last,))
    (fl_in,) = reduce(("w_in",), (last,))
    last = gathered_update("w_out", fl_out, (fl_in[0],))
    gathered_update("w_in", fl_in, (last,))
    loss, res = _adamw_small(small, gb, [(given[n], moms[n][0], moms[n][1]) for n, _, _ in SMALL])
    for n, _, _ in SMALL:
        out[n] = tuple(res[n])
    return (loss.reshape(()), grad_x, *[out[n][0] for n in order], *[out[n][1] for n in order],
            *[out[n][2] for n in order], *[out[n][3] for n in order])
```
